```python
import math
import jax, jax.numpy as jnp
from jax import lax
import numpy as np

D_MODEL = 1024
BATCH = 8
SEQ = 8192
DEPTH = 2

MEM_LEN = 256
D_MIX = D_MODEL
SSD_WIDTH = D_MIX // 2
SSD_HEAD_DIM = 64
SSD_HEADS = SSD_WIDTH // SSD_HEAD_DIM
SSD_GROUPS = 2
SSD_HEADS_PER_GROUP = SSD_HEADS // SSD_GROUPS
SSD_STATE = 128
SSD_CONV = 4
SSD_CHUNK = 128
SSD_XBC = SSD_WIDTH + 2 * SSD_GROUPS * SSD_STATE

S5_WIDTH = D_MIX // 4
S5_GROUP_CH = 16
S5_GROUPS = S5_WIDTH // S5_GROUP_CH
S5_STATE = 64

RG_WIDTH = D_MIX - SSD_WIDTH - S5_WIDTH
RG_BLOCKS = 4
RG_BLOCK_DIM = RG_WIDTH // RG_BLOCKS
RG_CONV = 4
RG_C = 8.0

XA_HEADS = 4
XA_HEAD_DIM = D_MODEL // XA_HEADS
D_FF = 4 * D_MODEL

ALPHA = (2.0 * DEPTH) ** 0.25
BETA = (8.0 * DEPTH) ** -0.25
LN_EPS = 1e-5

IN_COLS = (SSD_WIDTH, SSD_XBC, SSD_HEADS, S5_WIDTH, RG_WIDTH, RG_WIDTH)
D_IN = SSD_WIDTH + SSD_XBC + SSD_HEADS + S5_WIDTH + RG_WIDTH + RG_WIDTH

kernel_name = "hybrid_ssd_s5_rglru_deepnorm"


def layer_norm(x, g, b):
    x32 = x.astype(jnp.float32)
    mu = jnp.mean(x32, axis=-1, keepdims=True)
    var = jnp.mean(jnp.square(x32 - mu), axis=-1, keepdims=True)
    return (x32 - mu) * lax.rsqrt(var + LN_EPS) * g.astype(jnp.float32) + b.astype(jnp.float32)


def causal_dwconv(x, w, b):
    k, c = w.shape
    y = lax.conv_general_dilated(x, w[:, None, :].astype(x.dtype), window_strides=(1,),
                                 padding=[(k - 1, 0)], dimension_numbers=('NWC', 'WIO', 'NWC'),
                                 feature_group_count=c)
    return y + b.astype(x.dtype)


def _lin_combine(left, right):
    a1, b1 = left
    a2, b2 = right
    return a1 * a2, a2 * b1 + b2


def linear_scan(a, b):
    return lax.associative_scan(_lin_combine, (a, b), axis=1)[1]


def ssd_mixer(z, xbc, dt_raw, conv_w, conv_b, dt_bias, a_log, d_skip, norm_w):
    bsz, seq, _ = z.shape
    nc = seq // SSD_CHUNK
    xbc = jax.nn.silu(causal_dwconv(xbc, conv_w.astype(jnp.float32), conv_b.astype(jnp.float32)))
    xs = xbc[..., :SSD_WIDTH]
    bm = xbc[..., SSD_WIDTH:SSD_WIDTH + SSD_GROUPS * SSD_STATE]
    cm = xbc[..., SSD_WIDTH + SSD_GROUPS * SSD_STATE:]
    dt = jax.nn.softplus(dt_raw + dt_bias.astype(jnp.float32))
    a = -jnp.exp(a_log.astype(jnp.float32))
    g, hg, p, n, q = SSD_GROUPS, SSD_HEADS_PER_GROUP, SSD_HEAD_DIM, SSD_STATE, SSD_CHUNK
    xh = xs.reshape(bsz, seq, SSD_HEADS, p)
    xdt = (xh * dt[..., None]).reshape(bsz, nc, q, g, hg, p)
    adt = (dt * a).reshape(bsz, nc, q, g, hg).transpose(0, 1, 3, 4, 2)
    bc = bm.reshape(bsz, nc, q, g, n)
    cc = cm.reshape(bsz, nc, q, g, n)
    a_cs = jnp.cumsum(adt, axis=-1)
    mask = jnp.tril(jnp.ones((q, q), dtype=bool))
    seg = a_cs[..., :, None] - a_cs[..., None, :]
    lmat = jnp.exp(jnp.where(mask, seg, -jnp.inf))
    cb = jnp.einsum('bclgn,bcsgn->bcgls', cc, bc)
    y_diag = jnp.einsum('bcgls,bcghls,bcsghp->bclghp', cb, lmat, xdt)
    decay_states = jnp.exp(a_cs[..., -1:] - a_cs)
    states = jnp.einsum('bclgn,bcghl,bclghp->bcghpn', bc, decay_states, xdt)
    chunk_decay = jnp.exp(a_cs[..., -1])

    def step(s, inp):
        st, dec = inp
        return s * dec[..., None, None] + st, s

    init = jnp.zeros((bsz, g, hg, p, n), jnp.float32)
    _, prev = lax.scan(step, init, (jnp.moveaxis(states, 1, 0), jnp.moveaxis(chunk_decay, 1, 0)))
    prev = jnp.moveaxis(prev, 0, 1)
    y_off = jnp.einsum('bclgn,bcghpn,bcghl->bclghp', cc, prev, jnp.exp(a_cs))
    y = (y_diag + y_off).reshape(bsz, seq, SSD_HEADS, p) + xh * d_skip.astype(jnp.float32)[:, None]
    y = y.reshape(bsz, seq, SSD_WIDTH) * jax.nn.silu(z)
    y = y * lax.rsqrt(jnp.mean(jnp.square(y), axis=-1, keepdims=True) + LN_EPS)
    return y * norm_w.astype(jnp.float32)


def s5_mixer(u, lam_re, lam_im, log_step, b_re, b_im, c_re, c_im, d_skip, glu_w, glu_b):
    bsz, seq, _ = u.shape
    f32 = jnp.float32
    ug = u.reshape(bsz, seq, S5_GROUPS, S5_GROUP_CH).astype(jnp.complex64)
    lam = lax.complex(lam_re.astype(f32), lam_im.astype(f32))
    step = jnp.exp(log_step.astype(f32))[:, None]
    lam_bar = jnp.exp(lam * step)
    bmat = lax.complex(b_re.astype(f32), b_im.astype(f32))
    b_bar = ((lam_bar - 1.0) / lam)[..., None] * bmat
    bu = jnp.einsum('gpc,blgc->blgp', b_bar, ug)
    h = linear_scan(jnp.broadcast_to(lam_bar, bu.shape), bu)
    cmat = lax.complex(c_re.astype(f32), c_im.astype(f32))
    y = jnp.real(jnp.einsum('gcp,blgp->blgc', cmat, h)).reshape(bsz, seq, S5_WIDTH)
    y = jax.nn.gelu(y + d_skip.astype(f32) * u)
    return y * jax.nn.sigmoid(jnp.einsum('blc,ce->ble', y, glu_w.astype(f32)) + glu_b.astype(f32))


def rglru_mixer(xr, gate_in, conv_w, conv_b, wa, ba, wx, bx, lam):
    bsz, seq, _ = xr.shape
    f32 = jnp.float32
    xc = causal_dwconv(xr, conv_w.astype(f32), conv_b.astype(f32))
    xh = xc.reshape(bsz, seq, RG_BLOCKS, RG_BLOCK_DIM)
    r = jax.nn.sigmoid(jnp.einsum('blhi,hij->blhj', xh, wa.astype(f32)) + ba.astype(f32)).reshape(bsz, seq, RG_WIDTH)
    i = jax.nn.sigmoid(jnp.einsum('blhi,hij->blhj', xh, wx.astype(f32)) + bx.astype(f32)).reshape(bsz, seq, RG_WIDTH)
    log_a = -RG_C * r * jax.nn.softplus(-lam.astype(f32))
    a = jnp.exp(log_a)
    mult = jnp.sqrt(-jnp.expm1(2.0 * log_a))
    h = linear_scan(a, mult * (i * xc))
    return h * jax.nn.gelu(gate_in)


def cross_attention(x, mem, wq, wk, wv, wo):
    bsz, seq, _ = x.shape
    f32 = jnp.float32
    q = jnp.einsum('bld,de->ble', x, wq.astype(f32)).reshape(bsz, seq, XA_HEADS, XA_HEAD_DIM)
    k = jnp.einsum('bmd,de->bme', mem, wk.astype(f32)).reshape(bsz, -1, XA_HEADS, XA_HEAD_DIM)
    v = jnp.einsum('bmd,de->bme', mem, wv.astype(f32)).reshape(bsz, -1, XA_HEADS, XA_HEAD_DIM)
    s = jnp.einsum('blhd,bmhd->bhlm', q, k) * (1.0 / math.sqrt(XA_HEAD_DIM))
    pr = jax.nn.softmax(s, axis=-1)
    o = jnp.einsum('bhlm,bmhd->blhd', pr, v).reshape(bsz, seq, D_MODEL)
    return jnp.einsum('ble,ed->bld', o, wo.astype(f32))


def squared_relu_mlp(x, w1, w2):
    hdn = jnp.square(jax.nn.relu(jnp.einsum('bld,df->blf', x, w1.astype(jnp.float32))))
    return jnp.einsum('blf,fd->bld', hdn, w2.astype(jnp.float32))


def _fwd_setup_inputs(seed: int = 0) -> dict:
    key = jax.random.key(seed)
    ks = iter(jax.random.split(key, 64))
    f32 = jnp.float32

    def nrm(shape, scale):
        return jax.random.normal(next(ks), shape, f32) * scale

    def uni(shape, lo, hi):
        return jax.random.uniform(next(ks), shape, f32, lo, hi)

    L = DEPTH
    x = nrm((BATCH, SEQ, D_MODEL), 1.0)
    mem = nrm((BATCH, MEM_LEN, D_MODEL), 1.0)
    dt0 = jnp.exp(uni((L, SSD_HEADS), math.log(1e-3), math.log(1e-1)))
    a_rg = uni((L, RG_WIDTH), 0.9, 0.999) ** (1.0 / RG_C)
    n_idx = jnp.arange(S5_STATE, dtype=f32)
    return {
        "x": x,
        "mem": mem,
        "w_in": nrm((L, D_MODEL, D_IN), D_MODEL ** -0.5),
        "w_out": nrm((L, D_MIX, D_MODEL), BETA * D_MIX ** -0.5),
        "ssd_conv_w": nrm((L, SSD_CONV, SSD_XBC), SSD_CONV ** -0.5),
        "ssd_conv_b": nrm((L, SSD_XBC), 0.02),
        "ssd_dt_bias": dt0 + jnp.log(-jnp.expm1(-dt0)),
        "ssd_a_log": jnp.log(uni((L, SSD_HEADS), 1.0, 16.0)),
        "ssd_d": 1.0 + nrm((L, SSD_HEADS), 0.02),
        "ssd_norm_w": 1.0 + nrm((L, SSD_WIDTH), 0.02),
        "s5_lam_re": -0.5 + nrm((L, S5_GROUPS, S5_STATE), 0.01),
        "s5_lam_im": jnp.pi * n_idx + nrm((L, S5_GROUPS, S5_STATE), 0.01),
        "s5_log_step": uni((L, S5_GROUPS), math.log(1e-3), math.log(1e-1)),
        "s5_b_re": nrm((L, S5_GROUPS, S5_STATE, S5_GROUP_CH), (2.0 * S5_GROUP_CH) ** -0.5),
        "s5_b_im": nrm((L, S5_GROUPS, S5_STATE, S5_GROUP_CH), (2.0 * S5_GROUP_CH) ** -0.5),
        "s5_c_re": nrm((L, S5_GROUPS, S5_GROUP_CH, S5_STATE), (2.0 * S5_STATE) ** -0.5),
        "s5_c_im": nrm((L, S5_GROUPS, S5_GROUP_CH, S5_STATE), (2.0 * S5_STATE) ** -0.5),
        "s5_d": nrm((L, S5_WIDTH), 1.0),
        "s5_glu_w": nrm((L, S5_WIDTH, S5_WIDTH), S5_WIDTH ** -0.5),
        "s5_glu_b": nrm((L, S5_WIDTH), 0.02),
        "rg_conv_w": nrm((L, RG_CONV, RG_WIDTH), RG_CONV ** -0.5),
        "rg_conv_b": nrm((L, RG_WIDTH), 0.02),
        "rg_wa": nrm((L, RG_BLOCKS, RG_BLOCK_DIM, RG_BLOCK_DIM), RG_BLOCK_DIM ** -0.5),
        "rg_ba": nrm((L, RG_BLOCKS, RG_BLOCK_DIM), 0.02),
        "rg_wx": nrm((L, RG_BLOCKS, RG_BLOCK_DIM, RG_BLOCK_DIM), RG_BLOCK_DIM ** -0.5),
        "rg_bx": nrm((L, RG_BLOCKS, RG_BLOCK_DIM), 0.02),
        "rg_lambda": jnp.log(a_rg / (1.0 - a_rg)),
        "ln1_g": 1.0 + nrm((L, D_MODEL), 0.02),
        "ln1_b": nrm((L, D_MODEL), 0.02),
        "xa_wq": nrm((L, D_MODEL, D_MODEL), D_MODEL ** -0.5),
        "xa_wk": nrm((L, D_MODEL, D_MODEL), D_MODEL ** -0.5),
        "xa_wv": nrm((L, D_MODEL, D_MODEL), BETA * D_MODEL ** -0.5),
        "xa_wo": nrm((L, D_MODEL, D_MODEL), BETA * D_MODEL ** -0.5),
        "ln2_g": 1.0 + nrm((L, D_MODEL), 0.02),
        "ln2_b": nrm((L, D_MODEL), 0.02),
        "mlp_w1": nrm((L, D_MODEL, D_FF), BETA * D_MODEL ** -0.5),
        "mlp_w2": nrm((L, D_FF, D_MODEL), BETA * D_FF ** -0.5),
        "ln3_g": 1.0 + nrm((L, D_MODEL), 0.02),
        "ln3_b": nrm((L, D_MODEL), 0.02),
    }


def _fwd_reference(x, mem, w_in, w_out, ssd_conv_w, ssd_conv_b, ssd_dt_bias, ssd_a_log, ssd_d, ssd_norm_w,
              s5_lam_re, s5_lam_im, s5_log_step, s5_b_re, s5_b_im, s5_c_re, s5_c_im, s5_d, s5_glu_w, s5_glu_b,
              rg_conv_w, rg_conv_b, rg_wa, rg_ba, rg_wx, rg_bx, rg_lambda, ln1_g, ln1_b,
              xa_wq, xa_wk, xa_wv, xa_wo, ln2_g, ln2_b, mlp_w1, mlp_w2, ln3_g, ln3_b):
    f32 = jnp.float32
    out_dtype = x.dtype
    h = x.astype(f32)
    memf = mem.astype(f32)
    split_idx = []
    acc = 0
    for w in IN_COLS[:-1]:
        acc += w
        split_idx.append(acc)
    for l in range(DEPTH):
        proj = jnp.einsum('bld,dk->blk', h, w_in[l].astype(f32))
        z, xbc, dt_raw, u_s5, x_rg, g_rg = jnp.split(proj, split_idx, axis=-1)
        y_ssd = ssd_mixer(z, xbc, dt_raw, ssd_conv_w[l], ssd_conv_b[l], ssd_dt_bias[l], ssd_a_log[l],
                          ssd_d[l], ssd_norm_w[l])
        y_s5 = s5_mixer(u_s5, s5_lam_re[l], s5_lam_im[l], s5_log_step[l], s5_b_re[l], s5_b_im[l],
                        s5_c_re[l], s5_c_im[l], s5_d[l], s5_glu_w[l], s5_glu_b[l])
        y_rg = rglru_mixer(x_rg, g_rg, rg_conv_w[l], rg_conv_b[l], rg_wa[l], rg_ba[l], rg_wx[l], rg_bx[l],
                           rg_lambda[l])
        y = jnp.concatenate([y_ssd, y_s5, y_rg], axis=-1)
        y = jnp.einsum('ble,ed->bld', y, w_out[l].astype(f32))
        h = layer_norm(ALPHA * h + y, ln1_g[l], ln1_b[l])
        h = layer_norm(ALPHA * h + cross_attention(h, memf, xa_wq[l], xa_wk[l], xa_wv[l], xa_wo[l]),
                       ln2_g[l], ln2_b[l])
        h = layer_norm(ALPHA * h + squared_relu_mlp(h, mlp_w1[l], mlp_w2[l]), ln3_g[l], ln3_b[l])
    return h.astype(out_dtype)


import jax as _jax
import jax.numpy as _jnp

TWIN_FORMAT = 'train_step'
FWD_PARAMS = ['x', 'mem', 'w_in', 'w_out', 'ssd_conv_w', 'ssd_conv_b', 'ssd_dt_bias', 'ssd_a_log', 'ssd_d', 'ssd_norm_w', 's5_lam_re', 's5_lam_im', 's5_log_step', 's5_b_re', 's5_b_im', 's5_c_re', 's5_c_im', 's5_d', 's5_glu_w', 's5_glu_b', 'rg_conv_w', 'rg_conv_b', 'rg_wa', 'rg_ba', 'rg_wx', 'rg_bx', 'rg_lambda', 'ln1_g', 'ln1_b', 'xa_wq', 'xa_wk', 'xa_wv', 'xa_wo', 'ln2_g', 'ln2_b', 'mlp_w1', 'mlp_w2', 'ln3_g', 'ln3_b']
TWIN_WEIGHTS = ['w_in', 'w_out', 'ssd_conv_w', 'ssd_conv_b', 'ssd_dt_bias', 'ssd_a_log', 'ssd_d', 'ssd_norm_w', 's5_lam_re', 's5_lam_im', 's5_log_step', 's5_b_re', 's5_b_im', 's5_c_re', 's5_c_im', 's5_d', 's5_glu_w', 's5_glu_b', 'rg_conv_w', 'rg_conv_b', 'rg_wa', 'rg_ba', 'rg_wx', 'rg_bx', 'rg_lambda', 'ln1_g', 'ln1_b', 'xa_wq', 'xa_wk', 'xa_wv', 'xa_wo', 'ln2_g', 'ln2_b', 'mlp_w1', 'mlp_w2', 'ln3_g', 'ln3_b']
TWIN_DIFF_INPUT = 'x'
TWIN_INPUTS = ['x', 'mem', 'w_in', 'w_out', 'ssd_conv_w', 'ssd_conv_b', 'ssd_dt_bias', 'ssd_a_log', 'ssd_d', 'ssd_norm_w', 's5_lam_re', 's5_lam_im', 's5_log_step', 's5_b_re', 's5_b_im', 's5_c_re', 's5_c_im', 's5_d', 's5_glu_w', 's5_glu_b', 'rg_conv_w', 'rg_conv_b', 'rg_wa', 'rg_ba', 'rg_wx', 'rg_bx', 'rg_lambda', 'ln1_g', 'ln1_b', 'xa_wq', 'xa_wk', 'xa_wv', 'xa_wo', 'ln2_g', 'ln2_b', 'mlp_w1', 'mlp_w2', 'ln3_g', 'ln3_b', 'loss_target', 'm_w_in', 'm_w_out', 'm_ssd_conv_w', 'm_ssd_conv_b', 'm_ssd_dt_bias', 'm_ssd_a_log', 'm_ssd_d', 'm_ssd_norm_w', 'm_s5_lam_re', 'm_s5_lam_im', 'm_s5_log_step', 'm_s5_b_re', 'm_s5_b_im', 'm_s5_c_re', 'm_s5_c_im', 'm_s5_d', 'm_s5_glu_w', 'm_s5_glu_b', 'm_rg_conv_w', 'm_rg_conv_b', 'm_rg_wa', 'm_rg_ba', 'm_rg_wx', 'm_rg_bx', 'm_rg_lambda', 'm_ln1_g', 'm_ln1_b', 'm_xa_wq', 'm_xa_wk', 'm_xa_wv', 'm_xa_wo', 'm_ln2_g', 'm_ln2_b', 'm_mlp_w1', 'm_mlp_w2', 'm_ln3_g', 'm_ln3_b', 'v_w_in', 'v_w_out', 'v_ssd_conv_w', 'v_ssd_conv_b', 'v_ssd_dt_bias', 'v_ssd_a_log', 'v_ssd_d', 'v_ssd_norm_w', 'v_s5_lam_re', 'v_s5_lam_im', 'v_s5_log_step', 'v_s5_b_re', 'v_s5_b_im', 'v_s5_c_re', 'v_s5_c_im', 'v_s5_d', 'v_s5_glu_w', 'v_s5_glu_b', 'v_rg_conv_w', 'v_rg_conv_b', 'v_rg_wa', 'v_rg_ba', 'v_rg_wx', 'v_rg_bx', 'v_rg_lambda', 'v_ln1_g', 'v_ln1_b', 'v_xa_wq', 'v_xa_wk', 'v_xa_wv', 'v_xa_wo', 'v_ln2_g', 'v_ln2_b', 'v_mlp_w1', 'v_mlp_w2', 'v_ln3_g', 'v_ln3_b']
TWIN_OUTPUTS = ['loss', 'grad_x', 'grad_w_in', 'grad_w_out', 'grad_ssd_conv_w', 'grad_ssd_conv_b', 'grad_ssd_dt_bias', 'grad_ssd_a_log', 'grad_ssd_d', 'grad_ssd_norm_w', 'grad_s5_lam_re', 'grad_s5_lam_im', 'grad_s5_log_step', 'grad_s5_b_re', 'grad_s5_b_im', 'grad_s5_c_re', 'grad_s5_c_im', 'grad_s5_d', 'grad_s5_glu_w', 'grad_s5_glu_b', 'grad_rg_conv_w', 'grad_rg_conv_b', 'grad_rg_wa', 'grad_rg_ba', 'grad_rg_wx', 'grad_rg_bx', 'grad_rg_lambda', 'grad_ln1_g', 'grad_ln1_b', 'grad_xa_wq', 'grad_xa_wk', 'grad_xa_wv', 'grad_xa_wo', 'grad_ln2_g', 'grad_ln2_b', 'grad_mlp_w1', 'grad_mlp_w2', 'grad_ln3_g', 'grad_ln3_b', 'delta_w_in', 'delta_w_out', 'delta_ssd_conv_w', 'delta_ssd_conv_b', 'delta_ssd_dt_bias', 'delta_ssd_a_log', 'delta_ssd_d', 'delta_ssd_norm_w', 'delta_s5_lam_re', 'delta_s5_lam_im', 'delta_s5_log_step', 'delta_s5_b_re', 'delta_s5_b_im', 'delta_s5_c_re', 'delta_s5_c_im', 'delta_s5_d', 'delta_s5_glu_w', 'delta_s5_glu_b', 'delta_rg_conv_w', 'delta_rg_conv_b', 'delta_rg_wa', 'delta_rg_ba', 'delta_rg_wx', 'delta_rg_bx', 'delta_rg_lambda', 'delta_ln1_g', 'delta_ln1_b', 'delta_xa_wq', 'delta_xa_wk', 'delta_xa_wv', 'delta_xa_wo', 'delta_ln2_g', 'delta_ln2_b', 'delta_mlp_w1', 'delta_mlp_w2', 'delta_ln3_g', 'delta_ln3_b', 'new_m_w_in', 'new_m_w_out', 'new_m_ssd_conv_w', 'new_m_ssd_conv_b', 'new_m_ssd_dt_bias', 'new_m_ssd_a_log', 'new_m_ssd_d', 'new_m_ssd_norm_w', 'new_m_s5_lam_re', 'new_m_s5_lam_im', 'new_m_s5_log_step', 'new_m_s5_b_re', 'new_m_s5_b_im', 'new_m_s5_c_re', 'new_m_s5_c_im', 'new_m_s5_d', 'new_m_s5_glu_w', 'new_m_s5_glu_b', 'new_m_rg_conv_w', 'new_m_rg_conv_b', 'new_m_rg_wa', 'new_m_rg_ba', 'new_m_rg_wx', 'new_m_rg_bx', 'new_m_rg_lambda', 'new_m_ln1_g', 'new_m_ln1_b', 'new_m_xa_wq', 'new_m_xa_wk', 'new_m_xa_wv', 'new_m_xa_wo', 'new_m_ln2_g', 'new_m_ln2_b', 'new_m_mlp_w1', 'new_m_mlp_w2', 'new_m_ln3_g', 'new_m_ln3_b', 'new_v_w_in', 'new_v_w_out', 'new_v_ssd_conv_w', 'new_v_ssd_conv_b', 'new_v_ssd_dt_bias', 'new_v_ssd_a_log', 'new_v_ssd_d', 'new_v_ssd_norm_w', 'new_v_s5_lam_re', 'new_v_s5_lam_im', 'new_v_s5_log_step', 'new_v_s5_b_re', 'new_v_s5_b_im', 'new_v_s5_c_re', 'new_v_s5_c_im', 'new_v_s5_d', 'new_v_s5_glu_w', 'new_v_s5_glu_b', 'new_v_rg_conv_w', 'new_v_rg_conv_b', 'new_v_rg_wa', 'new_v_rg_ba', 'new_v_rg_wx', 'new_v_rg_bx', 'new_v_rg_lambda', 'new_v_ln1_g', 'new_v_ln1_b', 'new_v_xa_wq', 'new_v_xa_wk', 'new_v_xa_wv', 'new_v_xa_wo', 'new_v_ln2_g', 'new_v_ln2_b', 'new_v_mlp_w1', 'new_v_mlp_w2', 'new_v_ln3_g', 'new_v_ln3_b']
TWIN_LEAF_KINDS = {'loss': 'loss', 'grad_x': 'grad_x', 'grad_w_in': 'grad_w', 'grad_w_out': 'grad_w', 'grad_ssd_conv_w': 'grad_w', 'grad_ssd_conv_b': 'grad_w', 'grad_ssd_dt_bias': 'grad_w', 'grad_ssd_a_log': 'grad_w', 'grad_ssd_d': 'grad_w', 'grad_ssd_norm_w': 'grad_w', 'grad_s5_lam_re': 'grad_w', 'grad_s5_lam_im': 'grad_w', 'grad_s5_log_step': 'grad_w', 'grad_s5_b_re': 'grad_w', 'grad_s5_b_im': 'grad_w', 'grad_s5_c_re': 'grad_w', 'grad_s5_c_im': 'grad_w', 'grad_s5_d': 'grad_w', 'grad_s5_glu_w': 'grad_w', 'grad_s5_glu_b': 'grad_w', 'grad_rg_conv_w': 'grad_w', 'grad_rg_conv_b': 'grad_w', 'grad_rg_wa': 'grad_w', 'grad_rg_ba': 'grad_w', 'grad_rg_wx': 'grad_w', 'grad_rg_bx': 'grad_w', 'grad_rg_lambda': 'grad_w', 'grad_ln1_g': 'grad_w', 'grad_ln1_b': 'grad_w', 'grad_xa_wq': 'grad_w', 'grad_xa_wk': 'grad_w', 'grad_xa_wv': 'grad_w', 'grad_xa_wo': 'grad_w', 'grad_ln2_g': 'grad_w', 'grad_ln2_b': 'grad_w', 'grad_mlp_w1': 'grad_w', 'grad_mlp_w2': 'grad_w', 'grad_ln3_g': 'grad_w', 'grad_ln3_b': 'grad_w', 'delta_w_in': 'delta_w', 'delta_w_out': 'delta_w', 'delta_ssd_conv_w': 'delta_w', 'delta_ssd_conv_b': 'delta_w', 'delta_ssd_dt_bias': 'delta_w', 'delta_ssd_a_log': 'delta_w', 'delta_ssd_d': 'delta_w', 'delta_ssd_norm_w': 'delta_w', 'delta_s5_lam_re': 'delta_w', 'delta_s5_lam_im': 'delta_w', 'delta_s5_log_step': 'delta_w', 'delta_s5_b_re': 'delta_w', 'delta_s5_b_im': 'delta_w', 'delta_s5_c_re': 'delta_w', 'delta_s5_c_im': 'delta_w', 'delta_s5_d': 'delta_w', 'delta_s5_glu_w': 'delta_w', 'delta_s5_glu_b': 'delta_w', 'delta_rg_conv_w': 'delta_w', 'delta_rg_conv_b': 'delta_w', 'delta_rg_wa': 'delta_w', 'delta_rg_ba': 'delta_w', 'delta_rg_wx': 'delta_w', 'delta_rg_bx': 'delta_w', 'delta_rg_lambda': 'delta_w', 'delta_ln1_g': 'delta_w', 'delta_ln1_b': 'delta_w', 'delta_xa_wq': 'delta_w', 'delta_xa_wk': 'delta_w', 'delta_xa_wv': 'delta_w', 'delta_xa_wo': 'delta_w', 'delta_ln2_g': 'delta_w', 'delta_ln2_b': 'delta_w', 'delta_mlp_w1': 'delta_w', 'delta_mlp_w2': 'delta_w', 'delta_ln3_g': 'delta_w', 'delta_ln3_b': 'delta_w', 'new_m_w_in': 'new_m', 'new_m_w_out': 'new_m', 'new_m_ssd_conv_w': 'new_m', 'new_m_ssd_conv_b': 'new_m', 'new_m_ssd_dt_bias': 'new_m', 'new_m_ssd_a_log': 'new_m', 'new_m_ssd_d': 'new_m', 'new_m_ssd_norm_w': 'new_m', 'new_m_s5_lam_re': 'new_m', 'new_m_s5_lam_im': 'new_m', 'new_m_s5_log_step': 'new_m', 'new_m_s5_b_re': 'new_m', 'new_m_s5_b_im': 'new_m', 'new_m_s5_c_re': 'new_m', 'new_m_s5_c_im': 'new_m', 'new_m_s5_d': 'new_m', 'new_m_s5_glu_w': 'new_m', 'new_m_s5_glu_b': 'new_m', 'new_m_rg_conv_w': 'new_m', 'new_m_rg_conv_b': 'new_m', 'new_m_rg_wa': 'new_m', 'new_m_rg_ba': 'new_m', 'new_m_rg_wx': 'new_m', 'new_m_rg_bx': 'new_m', 'new_m_rg_lambda': 'new_m', 'new_m_ln1_g': 'new_m', 'new_m_ln1_b': 'new_m', 'new_m_xa_wq': 'new_m', 'new_m_xa_wk': 'new_m', 'new_m_xa_wv': 'new_m', 'new_m_xa_wo': 'new_m', 'new_m_ln2_g': 'new_m', 'new_m_ln2_b': 'new_m', 'new_m_mlp_w1': 'new_m', 'new_m_mlp_w2': 'new_m', 'new_m_ln3_g': 'new_m', 'new_m_ln3_b': 'new_m', 'new_v_w_in': 'new_v', 'new_v_w_out': 'new_v', 'new_v_ssd_conv_w': 'new_v', 'new_v_ssd_conv_b': 'new_v', 'new_v_ssd_dt_bias': 'new_v', 'new_v_ssd_a_log': 'new_v', 'new_v_ssd_d': 'new_v', 'new_v_ssd_norm_w': 'new_v', 'new_v_s5_lam_re': 'new_v', 'new_v_s5_lam_im': 'new_v', 'new_v_s5_log_step': 'new_v', 'new_v_s5_b_re': 'new_v', 'new_v_s5_b_im': 'new_v', 'new_v_s5_c_re': 'new_v', 'new_v_s5_c_im': 'new_v', 'new_v_s5_d': 'new_v', 'new_v_s5_glu_w': 'new_v', 'new_v_s5_glu_b': 'new_v', 'new_v_rg_conv_w': 'new_v', 'new_v_rg_conv_b': 'new_v', 'new_v_rg_wa': 'new_v', 'new_v_rg_ba': 'new_v', 'new_v_rg_wx': 'new_v', 'new_v_rg_bx': 'new_v', 'new_v_rg_lambda': 'new_v', 'new_v_ln1_g': 'new_v', 'new_v_ln1_b': 'new_v', 'new_v_xa_wq': 'new_v', 'new_v_xa_wk': 'new_v', 'new_v_xa_wv': 'new_v', 'new_v_xa_wo': 'new_v', 'new_v_ln2_g': 'new_v', 'new_v_ln2_b': 'new_v', 'new_v_mlp_w1': 'new_v', 'new_v_mlp_w2': 'new_v', 'new_v_ln3_g': 'new_v', 'new_v_ln3_b': 'new_v'}


def _forward(args):
    return _fwd_reference(*[args[k] for k in FWD_PARAMS])


def _output_shape():
    def fwd():
        inp = _fwd_setup_inputs(0)
        return _fwd_reference(*[inp[k] for k in FWD_PARAMS])
    out = _jax.eval_shape(fwd)
    return out.shape, out.dtype

N_MICROBATCH = 1
ADAM_LR = 0.001
ADAM_B1 = 0.9
ADAM_B2 = 0.999
ADAM_EPS = 1e-08
ADAM_WD = 0.01
ADAM_STEP = 10
PER_EXAMPLE_BATCH_AXIS = {'x': 0, 'mem': 0, 'loss_target': 0}
SHARED_INPUTS = []
_WEIGHT_DTYPES = {'w_in': _jnp.float32, 'w_out': _jnp.float32, 'ssd_conv_w': _jnp.float32, 'ssd_conv_b': _jnp.float32, 'ssd_dt_bias': _jnp.float32, 'ssd_a_log': _jnp.float32, 'ssd_d': _jnp.float32, 'ssd_norm_w': _jnp.float32, 's5_lam_re': _jnp.float32, 's5_lam_im': _jnp.float32, 's5_log_step': _jnp.float32, 's5_b_re': _jnp.float32, 's5_b_im': _jnp.float32, 's5_c_re': _jnp.float32, 's5_c_im': _jnp.float32, 's5_d': _jnp.float32, 's5_glu_w': _jnp.float32, 's5_glu_b': _jnp.float32, 'rg_conv_w': _jnp.float32, 'rg_conv_b': _jnp.float32, 'rg_wa': _jnp.float32, 'rg_ba': _jnp.float32, 'rg_wx': _jnp.float32, 'rg_bx': _jnp.float32, 'rg_lambda': _jnp.float32, 'ln1_g': _jnp.float32, 'ln1_b': _jnp.float32, 'xa_wq': _jnp.float32, 'xa_wk': _jnp.float32, 'xa_wv': _jnp.float32, 'xa_wo': _jnp.float32, 'ln2_g': _jnp.float32, 'ln2_b': _jnp.float32, 'mlp_w1': _jnp.float32, 'mlp_w2': _jnp.float32, 'ln3_g': _jnp.float32, 'ln3_b': _jnp.float32}
MOMENT_SCALE = {'w_in': 6.580967e-02, 'w_out': 1.626278e-01, 'ssd_conv_w': 7.090474e-02, 'ssd_conv_b': 1.270983e-01, 'ssd_dt_bias': 1.579113e-01, 'ssd_a_log': 6.682099e-01, 'ssd_d': 9.146161e-01, 'ssd_norm_w': 9.718052e-02, 's5_lam_re': 1.892135e-03, 's5_lam_im': 1.910305e-03, 's5_log_step': 1.603546e+00, 's5_b_re': 1.141886e-03, 's5_b_im': 1.110744e-03, 's5_c_re': 2.266236e-03, 's5_c_im': 2.298424e-03, 's5_d': 5.362556e-02, 's5_glu_w': 9.700518e-03, 's5_glu_b': 1.856435e-02, 'rg_conv_w': 6.727304e-02, 'rg_conv_b': 6.756143e-01, 'rg_wa': 2.347773e-02, 'rg_ba': 1.561340e-02, 'rg_wx': 4.131258e-02, 'rg_bx': 2.563003e-02, 'rg_lambda': 3.277156e-02, 'ln1_g': 2.123354e+00, 'ln1_b': 9.521678e-01, 'xa_wq': 4.736375e-03, 'xa_wk': 4.763020e-03, 'xa_wv': 1.133722e-02, 'xa_wo': 1.138357e-02, 'ln2_g': 2.127408e+00, 'ln2_b': 9.514541e-01, 'mlp_w1': 3.215546e-02, 'mlp_w2': 8.637131e-02, 'ln3_g': 4.539405e+01, 'ln3_b': 5.104448e+00}


def _to_microbatches(a, axis):
    t = _jnp.moveaxis(a, axis, 0)
    t = t.reshape((N_MICROBATCH, t.shape[0] // N_MICROBATCH) + t.shape[1:])
    return _jnp.moveaxis(t, 1, axis + 1)


def setup_inputs(seed: int = 0) -> dict:
    inp = _fwd_setup_inputs(seed)
    key = _jax.random.fold_in(_jax.random.key(seed), 7919)
    shape, _ = _output_shape()
    out = dict(inp)
    out["loss_target"] = _jax.random.normal(_jax.random.fold_in(key, 0), shape, _jnp.float32)
    for i, name in enumerate(TWIN_WEIGHTS):
        w = inp[name].astype(_jnp.float32)
        if MOMENT_SCALE is None:
            s = _jnp.sqrt(_jnp.mean(_jnp.square(w)) + 1e-30)
        else:
            s = MOMENT_SCALE[name]
        km, kv = _jax.random.split(_jax.random.fold_in(key, i + 1))
        out[name] = w
        out["m_" + name] = s * _jax.random.normal(km, w.shape, _jnp.float32)
        out["v_" + name] = (s * s) * _jax.random.uniform(kv, w.shape, _jnp.float32, 0.5, 1.5)
    if N_MICROBATCH > 1:
        for name, axis in PER_EXAMPLE_BATCH_AXIS.items():
            out[name] = _to_microbatches(out[name], axis)
    return {'x': out['x'], 'mem': out['mem'], 'w_in': out['w_in'], 'w_out': out['w_out'], 'ssd_conv_w': out['ssd_conv_w'], 'ssd_conv_b': out['ssd_conv_b'], 'ssd_dt_bias': out['ssd_dt_bias'], 'ssd_a_log': out['ssd_a_log'], 'ssd_d': out['ssd_d'], 'ssd_norm_w': out['ssd_norm_w'], 's5_lam_re': out['s5_lam_re'], 's5_lam_im': out['s5_lam_im'], 's5_log_step': out['s5_log_step'], 's5_b_re': out['s5_b_re'], 's5_b_im': out['s5_b_im'], 's5_c_re': out['s5_c_re'], 's5_c_im': out['s5_c_im'], 's5_d': out['s5_d'], 's5_glu_w': out['s5_glu_w'], 's5_glu_b': out['s5_glu_b'], 'rg_conv_w': out['rg_conv_w'], 'rg_conv_b': out['rg_conv_b'], 'rg_wa': out['rg_wa'], 'rg_ba': out['rg_ba'], 'rg_wx': out['rg_wx'], 'rg_bx': out['rg_bx'], 'rg_lambda': out['rg_lambda'], 'ln1_g': out['ln1_g'], 'ln1_b': out['ln1_b'], 'xa_wq': out['xa_wq'], 'xa_wk': out['xa_wk'], 'xa_wv': out['xa_wv'], 'xa_wo': out['xa_wo'], 'ln2_g': out['ln2_g'], 'ln2_b': out['ln2_b'], 'mlp_w1': out['mlp_w1'], 'mlp_w2': out['mlp_w2'], 'ln3_g': out['ln3_g'], 'ln3_b': out['ln3_b'], 'loss_target': out['loss_target'], 'm_w_in': out['m_w_in'], 'm_w_out': out['m_w_out'], 'm_ssd_conv_w': out['m_ssd_conv_w'], 'm_ssd_conv_b': out['m_ssd_conv_b'], 'm_ssd_dt_bias': out['m_ssd_dt_bias'], 'm_ssd_a_log': out['m_ssd_a_log'], 'm_ssd_d': out['m_ssd_d'], 'm_ssd_norm_w': out['m_ssd_norm_w'], 'm_s5_lam_re': out['m_s5_lam_re'], 'm_s5_lam_im': out['m_s5_lam_im'], 'm_s5_log_step': out['m_s5_log_step'], 'm_s5_b_re': out['m_s5_b_re'], 'm_s5_b_im': out['m_s5_b_im'], 'm_s5_c_re': out['m_s5_c_re'], 'm_s5_c_im': out['m_s5_c_im'], 'm_s5_d': out['m_s5_d'], 'm_s5_glu_w': out['m_s5_glu_w'], 'm_s5_glu_b': out['m_s5_glu_b'], 'm_rg_conv_w': out['m_rg_conv_w'], 'm_rg_conv_b': out['m_rg_conv_b'], 'm_rg_wa': out['m_rg_wa'], 'm_rg_ba': out['m_rg_ba'], 'm_rg_wx': out['m_rg_wx'], 'm_rg_bx': out['m_rg_bx'], 'm_rg_lambda': out['m_rg_lambda'], 'm_ln1_g': out['m_ln1_g'], 'm_ln1_b': out['m_ln1_b'], 'm_xa_wq': out['m_xa_wq'], 'm_xa_wk': out['m_xa_wk'], 'm_xa_wv': out['m_xa_wv'], 'm_xa_wo': out['m_xa_wo'], 'm_ln2_g': out['m_ln2_g'], 'm_ln2_b': out['m_ln2_b'], 'm_mlp_w1': out['m_mlp_w1'], 'm_mlp_w2': out['m_mlp_w2'], 'm_ln3_g': out['m_ln3_g'], 'm_ln3_b': out['m_ln3_b'], 'v_w_in': out['v_w_in'], 'v_w_out': out['v_w_out'], 'v_ssd_conv_w': out['v_ssd_conv_w'], 'v_ssd_conv_b': out['v_ssd_conv_b'], 'v_ssd_dt_bias': out['v_ssd_dt_bias'], 'v_ssd_a_log': out['v_ssd_a_log'], 'v_ssd_d': out['v_ssd_d'], 'v_ssd_norm_w': out['v_ssd_norm_w'], 'v_s5_lam_re': out['v_s5_lam_re'], 'v_s5_lam_im': out['v_s5_lam_im'], 'v_s5_log_step': out['v_s5_log_step'], 'v_s5_b_re': out['v_s5_b_re'], 'v_s5_b_im': out['v_s5_b_im'], 'v_s5_c_re': out['v_s5_c_re'], 'v_s5_c_im': out['v_s5_c_im'], 'v_s5_d': out['v_s5_d'], 'v_s5_glu_w': out['v_s5_glu_w'], 'v_s5_glu_b': out['v_s5_glu_b'], 'v_rg_conv_w': out['v_rg_conv_w'], 'v_rg_conv_b': out['v_rg_conv_b'], 'v_rg_wa': out['v_rg_wa'], 'v_rg_ba': out['v_rg_ba'], 'v_rg_wx': out['v_rg_wx'], 'v_rg_bx': out['v_rg_bx'], 'v_rg_lambda': out['v_rg_lambda'], 'v_ln1_g': out['v_ln1_g'], 'v_ln1_b': out['v_ln1_b'], 'v_xa_wq': out['v_xa_wq'], 'v_xa_wk': out['v_xa_wk'], 'v_xa_wv': out['v_xa_wv'], 'v_xa_wo': out['v_xa_wo'], 'v_ln2_g': out['v_ln2_g'], 'v_ln2_b': out['v_ln2_b'], 'v_mlp_w1': out['v_mlp_w1'], 'v_mlp_w2': out['v_mlp_w2'], 'v_ln3_g': out['v_ln3_g'], 'v_ln3_b': out['v_ln3_b']}


def _loss(weights, diff, rest, loss_target):
    with _jax.named_scope("forward"):
        args = {**rest, TWIN_DIFF_INPUT: diff, **{k: w.astype(_WEIGHT_DTYPES[k]) for k, w in weights.items()}}
        y = _forward(args)
    with _jax.named_scope("loss_head"):
        err = _jnp.square(y.astype(_jnp.float32) - loss_target)
        return 0.5 * _jnp.sum(_jnp.mean(err, axis=-1)) if err.ndim else 0.5 * err


def _adamw(w, g, m, v):
    m = ADAM_B1 * m + (1.0 - ADAM_B1) * g
    v = ADAM_B2 * v + (1.0 - ADAM_B2) * _jnp.square(g)
    m_hat = m / (1.0 - ADAM_B1 ** ADAM_STEP)
    v_hat = v / (1.0 - ADAM_B2 ** ADAM_STEP)
    delta = -ADAM_LR * (m_hat / (_jnp.sqrt(v_hat) + ADAM_EPS) + ADAM_WD * w)
    return delta, m, v


def reference(x, mem, w_in, w_out, ssd_conv_w, ssd_conv_b, ssd_dt_bias, ssd_a_log, ssd_d, ssd_norm_w, s5_lam_re, s5_lam_im, s5_log_step, s5_b_re, s5_b_im, s5_c_re, s5_c_im, s5_d, s5_glu_w, s5_glu_b, rg_conv_w, rg_conv_b, rg_wa, rg_ba, rg_wx, rg_bx, rg_lambda, ln1_g, ln1_b, xa_wq, xa_wk, xa_wv, xa_wo, ln2_g, ln2_b, mlp_w1, mlp_w2, ln3_g, ln3_b, loss_target, m_w_in, m_w_out, m_ssd_conv_w, m_ssd_conv_b, m_ssd_dt_bias, m_ssd_a_log, m_ssd_d, m_ssd_norm_w, m_s5_lam_re, m_s5_lam_im, m_s5_log_step, m_s5_b_re, m_s5_b_im, m_s5_c_re, m_s5_c_im, m_s5_d, m_s5_glu_w, m_s5_glu_b, m_rg_conv_w, m_rg_conv_b, m_rg_wa, m_rg_ba, m_rg_wx, m_rg_bx, m_rg_lambda, m_ln1_g, m_ln1_b, m_xa_wq, m_xa_wk, m_xa_wv, m_xa_wo, m_ln2_g, m_ln2_b, m_mlp_w1, m_mlp_w2, m_ln3_g, m_ln3_b, v_w_in, v_w_out, v_ssd_conv_w, v_ssd_conv_b, v_ssd_dt_bias, v_ssd_a_log, v_ssd_d, v_ssd_norm_w, v_s5_lam_re, v_s5_lam_im, v_s5_log_step, v_s5_b_re, v_s5_b_im, v_s5_c_re, v_s5_c_im, v_s5_d, v_s5_glu_w, v_s5_glu_b, v_rg_conv_w, v_rg_conv_b, v_rg_wa, v_rg_ba, v_rg_wx, v_rg_bx, v_rg_lambda, v_ln1_g, v_ln1_b, v_xa_wq, v_xa_wk, v_xa_wv, v_xa_wo, v_ln2_g, v_ln2_b, v_mlp_w1, v_mlp_w2, v_ln3_g, v_ln3_b):
    given = dict(x=x, mem=mem, w_in=w_in, w_out=w_out, ssd_conv_w=ssd_conv_w, ssd_conv_b=ssd_conv_b, ssd_dt_bias=ssd_dt_bias, ssd_a_log=ssd_a_log, ssd_d=ssd_d, ssd_norm_w=ssd_norm_w, s5_lam_re=s5_lam_re, s5_lam_im=s5_lam_im, s5_log_step=s5_log_step, s5_b_re=s5_b_re, s5_b_im=s5_b_im, s5_c_re=s5_c_re, s5_c_im=s5_c_im, s5_d=s5_d, s5_glu_w=s5_glu_w, s5_glu_b=s5_glu_b, rg_conv_w=rg_conv_w, rg_conv_b=rg_conv_b, rg_wa=rg_wa, rg_ba=rg_ba, rg_wx=rg_wx, rg_bx=rg_bx, rg_lambda=rg_lambda, ln1_g=ln1_g, ln1_b=ln1_b, xa_wq=xa_wq, xa_wk=xa_wk, xa_wv=xa_wv, xa_wo=xa_wo, ln2_g=ln2_g, ln2_b=ln2_b, mlp_w1=mlp_w1, mlp_w2=mlp_w2, ln3_g=ln3_g, ln3_b=ln3_b, loss_target=loss_target, m_w_in=m_w_in, m_w_out=m_w_out, m_ssd_conv_w=m_ssd_conv_w, m_ssd_conv_b=m_ssd_conv_b, m_ssd_dt_bias=m_ssd_dt_bias, m_ssd_a_log=m_ssd_a_log, m_ssd_d=m_ssd_d, m_ssd_norm_w=m_ssd_norm_w, m_s5_lam_re=m_s5_lam_re, m_s5_lam_im=m_s5_lam_im, m_s5_log_step=m_s5_log_step, m_s5_b_re=m_s5_b_re, m_s5_b_im=m_s5_b_im, m_s5_c_re=m_s5_c_re, m_s5_c_im=m_s5_c_im, m_s5_d=m_s5_d, m_s5_glu_w=m_s5_glu_w, m_s5_glu_b=m_s5_glu_b, m_rg_conv_w=m_rg_conv_w, m_rg_conv_b=m_rg_conv_b, m_rg_wa=m_rg_wa, m_rg_ba=m_rg_ba, m_rg_wx=m_rg_wx, m_rg_bx=m_rg_bx, m_rg_lambda=m_rg_lambda, m_ln1_g=m_ln1_g, m_ln1_b=m_ln1_b, m_xa_wq=m_xa_wq, m_xa_wk=m_xa_wk, m_xa_wv=m_xa_wv, m_xa_wo=m_xa_wo, m_ln2_g=m_ln2_g, m_ln2_b=m_ln2_b, m_mlp_w1=m_mlp_w1, m_mlp_w2=m_mlp_w2, m_ln3_g=m_ln3_g, m_ln3_b=m_ln3_b, v_w_in=v_w_in, v_w_out=v_w_out, v_ssd_conv_w=v_ssd_conv_w, v_ssd_conv_b=v_ssd_conv_b, v_ssd_dt_bias=v_ssd_dt_bias, v_ssd_a_log=v_ssd_a_log, v_ssd_d=v_ssd_d, v_ssd_norm_w=v_ssd_norm_w, v_s5_lam_re=v_s5_lam_re, v_s5_lam_im=v_s5_lam_im, v_s5_log_step=v_s5_log_step, v_s5_b_re=v_s5_b_re, v_s5_b_im=v_s5_b_im, v_s5_c_re=v_s5_c_re, v_s5_c_im=v_s5_c_im, v_s5_d=v_s5_d, v_s5_glu_w=v_s5_glu_w, v_s5_glu_b=v_s5_glu_b, v_rg_conv_w=v_rg_conv_w, v_rg_conv_b=v_rg_conv_b, v_rg_wa=v_rg_wa, v_rg_ba=v_rg_ba, v_rg_wx=v_rg_wx, v_rg_bx=v_rg_bx, v_rg_lambda=v_rg_lambda, v_ln1_g=v_ln1_g, v_ln1_b=v_ln1_b, v_xa_wq=v_xa_wq, v_xa_wk=v_xa_wk, v_xa_wv=v_xa_wv, v_xa_wo=v_xa_wo, v_ln2_g=v_ln2_g, v_ln2_b=v_ln2_b, v_mlp_w1=v_mlp_w1, v_mlp_w2=v_mlp_w2, v_ln3_g=v_ln3_g, v_ln3_b=v_ln3_b)
    weights = {n: given[n] for n in TWIN_WEIGHTS}
    shared = {n: given[n] for n in SHARED_INPUTS}
    per_example = {n: given[n] for n in ['x', 'mem']}
    grad_fn = _jax.value_and_grad(_loss, argnums=(0, 1))

    def one_microbatch(ex, loss_target):
        ex = dict(ex)
        diff = ex.pop(TWIN_DIFF_INPUT)
        return grad_fn(weights, diff, {**shared, **ex}, loss_target)

    if N_MICROBATCH == 1:
        loss, (grad_w, grad_x) = one_microbatch(per_example, given["loss_target"])
    else:
        def body(carry, xs):
            loss_sum, grad_sum = carry
            l_k, (gw_k, gx_k) = one_microbatch(xs[0], xs[1])
            with _jax.named_scope("update"):
                return (loss_sum + l_k, _jax.tree.map(_jnp.add, grad_sum, gw_k)), gx_k

        init = (_jnp.zeros((), _jnp.float32), _jax.tree.map(_jnp.zeros_like, weights))
        (loss, grad_w), grad_x = _jax.lax.scan(body, init, (per_example, given["loss_target"]))
    with _jax.named_scope("update"):
        delta_w, new_m, new_v = {}, {}, {}
        for n in TWIN_WEIGHTS:
            delta_w[n], new_m[n], new_v[n] = _adamw(weights[n], grad_w[n], given["m_" + n], given["v_" + n])
    return (loss, grad_x, *[grad_w[n] for n in TWIN_WEIGHTS], *[delta_w[n] for n in TWIN_WEIGHTS],
            *[new_m[n] for n in TWIN_WEIGHTS], *[new_v[n] for n in TWIN_WEIGHTS])
```

```python
import functools
import math

import jax
import jax.numpy as jnp
from jax import lax
from jax.experimental import pallas as pl
from jax.experimental.pallas import tpu as pltpu

F32 = jnp.float32
MXU_DTYPE = jnp.bfloat16

D_MODEL = 1024
DEPTH = 2
MEM_LEN = 256
SSD_WIDTH = 512
SSD_HEADS = 8
SSD_STATE = 128
SSD_CHUNK = 128
SSD_XBC = 1024
S5_WIDTH = 256
S5_GROUPS = 16
S5_GROUP_CH = 16
S5_STATE = 64
S5_NSTATE = S5_GROUPS * S5_STATE
RG_WIDTH = 256
RG_BLOCKS = 4
RG_BLOCK_DIM = 64
RG_C = 8.0
XA_HEADS = 4
XA_HEAD_DIM = 256
D_FF = 4096
D_IN = 2312
ALPHA = (2.0 * DEPTH) ** 0.25
LN_EPS = 1e-5
ADAM_LR = 0.001
ADAM_B1 = 0.9
ADAM_B2 = 0.999
ADAM_EPS = 1e-08
ADAM_WD = 0.01
ADAM_STEP = 10

P_XBC, P_Z, P_U, P_XRG, P_GRG, P_DT = 0, 1024, 1536, 1792, 2048, 2304
D_PACK = 2432
O_Z, O_XBC, O_DT, O_U, O_XRG, O_GRG = 0, 512, 1536, 1544, 1800, 2056

LANES = 128
SUBLANES = 8
VMEM_LIMIT = 52 * 1024 * 1024
TM = 512
SSD_TM = 256
SCAN_TM = 512
FLAT = 1024

MESH = pl.DeviceIdType.MESH


def _cparams(sem):
    return pltpu.CompilerParams(dimension_semantics=sem, vmem_limit_bytes=VMEM_LIMIT)


def _dot(a, b):
    return jnp.dot(a.astype(MXU_DTYPE), b.astype(MXU_DTYPE), preferred_element_type=F32)


def _dot_nt(a, b):
    return lax.dot_general(a.astype(MXU_DTYPE), b.astype(MXU_DTYPE), (((1,), (1,)), ((), ())),
                           preferred_element_type=F32)


def _dot_tn(a, b):
    return lax.dot_general(a.astype(MXU_DTYPE), b.astype(MXU_DTYPE), (((0,), (0,)), ((), ())),
                           preferred_element_type=F32)


def _dot_f32(a, b):
    return jnp.dot(a, b, precision=lax.Precision.HIGHEST, preferred_element_type=F32)


def _dot_f32_tn(a, b):
    return lax.dot_general(a, b, (((0,), (0,)), ((), ())), precision=lax.Precision.HIGHEST,
                           preferred_element_type=F32)


def _sigmoid(x):
    return 1.0 / (1.0 + jnp.exp(-x))


def _softplus(x):
    return jnp.maximum(x, 0.0) + jnp.log(1.0 + jnp.exp(-jnp.abs(x)))


_GELU_K = math.sqrt(2.0 / math.pi)


def _gelu(x):
    return 0.5 * x * (1.0 + jnp.tanh(_GELU_K * (x + 0.044715 * x * x * x)))


def _gelu_grad(x):
    t = jnp.tanh(_GELU_K * (x + 0.044715 * x * x * x))
    return 0.5 * (1.0 + t) + 0.5 * x * (1.0 - t * t) * _GELU_K * (1.0 + 3.0 * 0.044715 * x * x)


def _expm1(x):
    small = x * (1.0 + x * (0.5 + x * (1.0 / 6.0 + x * (1.0 / 24.0))))
    return jnp.where(jnp.abs(x) < 0.05, small, jnp.exp(x) - 1.0)


def _sum0(x):
    return jnp.sum(x, axis=0, keepdims=True)


def _ln_fwd(r, g, b):
    mu = jnp.mean(r, axis=-1, keepdims=True)
    xc = r - mu
    var = jnp.mean(xc * xc, axis=-1, keepdims=True)
    rstd = lax.rsqrt(var + LN_EPS)
    xhat = xc * rstd
    return xhat * g + b, xhat, rstd


def _ln_bwd(dout, xhat, rstd, g):
    dxh = dout * g
    m1 = jnp.mean(dxh, axis=-1, keepdims=True)
    m2 = jnp.mean(dxh * xhat, axis=-1, keepdims=True)
    return rstd * (dxh - m1 - xhat * m2)


def _rows(tm, n, col=0):
    return pl.BlockSpec((tm, n), lambda i: (i, col))


def _const(shape):
    nd = len(shape)
    return pl.BlockSpec(shape, lambda i: (0,) * nd)


def _mm(a, w, *, nt=False, add=None, out_dtype=F32, name):
    t, k = a.shape
    n = w.shape[0] if nt else w.shape[1]
    tm = min(TM, t)

    def body(*refs):
        if add is None:
            a_ref, w_ref, o_ref = refs
        else:
            a_ref, w_ref, add_ref, o_ref = refs
        r = _dot_nt(a_ref[...], w_ref[...]) if nt else _dot(a_ref[...], w_ref[...])
        if add is not None:
            r = r + add_ref[...]
        o_ref[...] = r.astype(out_dtype)

    in_specs = [_rows(tm, k), _const(w.shape)]
    args = [a, w]
    if add is not None:
        in_specs.append(_rows(tm, n))
        args.append(add)
    return pl.pallas_call(
        body, name=name, grid=(t // tm,), in_specs=in_specs, out_specs=_rows(tm, n),
        out_shape=jax.ShapeDtypeStruct((t, n), out_dtype), compiler_params=_cparams(("arbitrary",)),
    )(*args)


def _mm_tn(a, g, *, name):
    t, k = a.shape
    n = g.shape[1]
    tt = min(512, t)
    tk = min(1024, k)
    tn = 1024 if n % 1024 == 0 else n
    nsteps = t // tt

    def body(a_ref, g_ref, o_ref):
        s = pl.program_id(2)
        part = _dot_tn(a_ref[...], g_ref[...])

        @pl.when(s == 0)
        def _():
            o_ref[...] = part

        @pl.when(s > 0)
        def _():
            o_ref[...] += part

    return pl.pallas_call(
        body, name=name, grid=(k // tk, n // tn, nsteps),
        in_specs=[pl.BlockSpec((tt, tk), lambda i, j, s: (s, i)), pl.BlockSpec((tt, tn), lambda i, j, s: (s, j))],
        out_specs=pl.BlockSpec((tk, tn), lambda i, j, s: (i, j)),
        out_shape=jax.ShapeDtypeStruct((k, n), F32),
        compiler_params=_cparams(("arbitrary", "arbitrary", "arbitrary")),
    )(a, g)


def _outproj_ln_fwd(ycat, h, w, g, b):
    t = h.shape[0]

    def body(y_ref, h_ref, w_ref, g_ref, b_ref, hn_ref, xh_ref, rs_ref):
        r = ALPHA * h_ref[...] + _dot(y_ref[...], w_ref[...])
        out, xhat, rstd = _ln_fwd(r, g_ref[...], b_ref[...])
        hn_ref[...] = out
        xh_ref[...] = xhat
        rs_ref[...] = rstd

    return pl.pallas_call(
        body, name="outproj_ln_fwd", grid=(t // TM,),
        in_specs=[_rows(TM, D_MODEL), _rows(TM, D_MODEL), _const((D_MODEL, D_MODEL)), _const((1, D_MODEL)),
                  _const((1, D_MODEL))],
        out_specs=[_rows(TM, D_MODEL), _rows(TM, D_MODEL), _rows(TM, 1)],
        out_shape=[jax.ShapeDtypeStruct((t, D_MODEL), F32), jax.ShapeDtypeStruct((t, D_MODEL), F32),
                   jax.ShapeDtypeStruct((t, 1), F32)],
        compiler_params=_cparams(("arbitrary",)),
    )(ycat, h, w, g, b)


def _attn_probs(q, kb, hh):
    sl = slice(hh * XA_HEAD_DIM, (hh + 1) * XA_HEAD_DIM)
    s = _dot_nt(q[:, sl], kb[:, sl]) * (1.0 / math.sqrt(XA_HEAD_DIM))
    m = jnp.max(s, axis=-1, keepdims=True)
    e = jnp.exp(s - m)
    return e / jnp.sum(e, axis=-1, keepdims=True)


def _attn_ln_fwd(h1, wq, wo, kb, vb, g, b):
    t = h1.shape[0]

    def body(h_ref, wq_ref, wo_ref, k_ref, v_ref, g_ref, b_ref, hn_ref, xh_ref, rs_ref, o_ref):
        h = h_ref[...]
        q = _dot(h, wq_ref[...])
        kb_ = k_ref[...]
        vb_ = v_ref[...]
        for hh in range(XA_HEADS):
            sl = slice(hh * XA_HEAD_DIM, (hh + 1) * XA_HEAD_DIM)
            p = _attn_probs(q, kb_, hh)
            o_ref[:, sl] = _dot(p, vb_[:, sl]).astype(o_ref.dtype)
        r = ALPHA * h + _dot(o_ref[...], wo_ref[...])
        out, xhat, rstd = _ln_fwd(r, g_ref[...], b_ref[...])
        hn_ref[...] = out
        xh_ref[...] = xhat
        rs_ref[...] = rstd

    return pl.pallas_call(
        body, name="attn_ln_fwd", grid=(t // TM,),
        in_specs=[_rows(TM, D_MODEL), _const((D_MODEL, D_MODEL)), _const((D_MODEL, D_MODEL)),
                  _const((MEM_LEN, D_MODEL)), _const((MEM_LEN, D_MODEL)), _const((1, D_MODEL)), _const((1, D_MODEL))],
        out_specs=[_rows(TM, D_MODEL), _rows(TM, D_MODEL), _rows(TM, 1), _rows(TM, D_MODEL)],
        out_shape=[jax.ShapeDtypeStruct((t, D_MODEL), F32), jax.ShapeDtypeStruct((t, D_MODEL), F32),
                   jax.ShapeDtypeStruct((t, 1), F32), jax.ShapeDtypeStruct((t, D_MODEL), MXU_DTYPE)],
        compiler_params=_cparams(("arbitrary",)),
    )(h1, wq, wo, kb, vb, g, b)


def _attn_ln_bwd(dh2, xhat, rstd, g, h1, wq, wo, kb, vb):
    t = h1.shape[0]

    def body(dh_ref, xh_ref, rs_ref, g_ref, h_ref, wq_ref, wo_ref, k_ref, v_ref,
             dr_ref, dq_ref, dh1_ref, dk_ref, dv_ref, dg_ref, db_ref):
        i = pl.program_id(0)

        @pl.when(i == 0)
        def _():
            dk_ref[...] = jnp.zeros_like(dk_ref)
            dv_ref[...] = jnp.zeros_like(dv_ref)
            dg_ref[...] = jnp.zeros_like(dg_ref)
            db_ref[...] = jnp.zeros_like(db_ref)

        dout = dh_ref[...]
        xh = xh_ref[...]
        dg_ref[...] += _sum0(dout * xh)
        db_ref[...] += _sum0(dout)
        dr = _ln_bwd(dout, xh, rs_ref[...], g_ref[...])
        dr_ref[...] = dr.astype(dr_ref.dtype)
        do = _dot_nt(dr, wo_ref[...])
        h = h_ref[...]
        q = _dot(h, wq_ref[...])
        kb_ = k_ref[...]
        vb_ = v_ref[...]
        scale = 1.0 / math.sqrt(XA_HEAD_DIM)
        for hh in range(XA_HEADS):
            sl = slice(hh * XA_HEAD_DIM, (hh + 1) * XA_HEAD_DIM)
            p = _attn_probs(q, kb_, hh)
            do_h = do[:, sl]
            dp = _dot_nt(do_h, vb_[:, sl])
            ds = p * (dp - jnp.sum(dp * p, axis=-1, keepdims=True)) * scale
            dq_ref[:, sl] = _dot(ds, kb_[:, sl]).astype(dq_ref.dtype)
            dk_ref[:, sl] += _dot_tn(ds, q[:, sl])
            dv_ref[:, sl] += _dot_tn(p, do_h)
        dh1_ref[...] = ALPHA * dr + _dot_nt(dq_ref[...], wq_ref[...])

    return pl.pallas_call(
        body, name="attn_ln_bwd", grid=(t // TM,),
        in_specs=[_rows(TM, D_MODEL), _rows(TM, D_MODEL), _rows(TM, 1), _const((1, D_MODEL)), _rows(TM, D_MODEL),
                  _const((D_MODEL, D_MODEL)), _const((D_MODEL, D_MODEL)), _const((MEM_LEN, D_MODEL)),
                  _const((MEM_LEN, D_MODEL))],
        out_specs=[_rows(TM, D_MODEL), _rows(TM, D_MODEL), _rows(TM, D_MODEL), _const((MEM_LEN, D_MODEL)),
                   _const((MEM_LEN, D_MODEL)), _const((1, D_MODEL)), _const((1, D_MODEL))],
        out_shape=[jax.ShapeDtypeStruct((t, D_MODEL), MXU_DTYPE), jax.ShapeDtypeStruct((t, D_MODEL), MXU_DTYPE),
                   jax.ShapeDtypeStruct((t, D_MODEL), F32), jax.ShapeDtypeStruct((MEM_LEN, D_MODEL), F32),
                   jax.ShapeDtypeStruct((MEM_LEN, D_MODEL), F32), jax.ShapeDtypeStruct((1, D_MODEL), F32),
                   jax.ShapeDtypeStruct((1, D_MODEL), F32)],
        compiler_params=_cparams(("arbitrary",)),
    )(dh2, xhat, rstd, g, h1, wq, wo, kb, vb)


FF_CHUNK = 1024
N_FF = D_FF // FF_CHUNK


def _mlp_ln_fwd(h2, w1, w2, g, b):
    t = h2.shape[0]

    def body(h_ref, w1_ref, w2_ref, g_ref, b_ref, hn_ref, xh_ref, rs_ref, u_ref, acc_ref):
        j = pl.program_id(1)
        u = _dot(h_ref[...], w1_ref[...])
        u_ref[...] = u
        part = _dot(jnp.square(jnp.maximum(u, 0.0)), w2_ref[...])

        @pl.when(j == 0)
        def _():
            acc_ref[...] = part

        @pl.when(j > 0)
        def _():
            acc_ref[...] += part

        @pl.when(j == N_FF - 1)
        def _():
            r = ALPHA * h_ref[...] + acc_ref[...]
            out, xhat, rstd = _ln_fwd(r, g_ref[...], b_ref[...])
            hn_ref[...] = out
            xh_ref[...] = xhat
            rs_ref[...] = rstd

    row = lambda n: pl.BlockSpec((TM, n), lambda i, j: (i, 0))
    cst = pl.BlockSpec((1, D_MODEL), lambda i, j: (0, 0))
    return pl.pallas_call(
        body, name="mlp_ln_fwd", grid=(t // TM, N_FF),
        in_specs=[row(D_MODEL), pl.BlockSpec((D_MODEL, FF_CHUNK), lambda i, j: (0, j)),
                  pl.BlockSpec((FF_CHUNK, D_MODEL), lambda i, j: (j, 0)), cst, cst],
        out_specs=[row(D_MODEL), row(D_MODEL), row(1), pl.BlockSpec((TM, FF_CHUNK), lambda i, j: (i, j))],
        out_shape=[jax.ShapeDtypeStruct((t, D_MODEL), F32), jax.ShapeDtypeStruct((t, D_MODEL), F32),
                   jax.ShapeDtypeStruct((t, 1), F32), jax.ShapeDtypeStruct((t, D_FF), F32)],
        scratch_shapes=[pltpu.VMEM((TM, D_MODEL), F32)],
        compiler_params=_cparams(("arbitrary", "arbitrary")),
    )(h2, w1, w2, g, b)


def _mlp_ln_bwd(dh3, xhat, rstd, g, u, w1, w2):
    t = dh3.shape[0]

    def body(dh_ref, xh_ref, rs_ref, g_ref, u_ref, w1_ref, w2_ref,
             dr_ref, du_ref, hd_ref, dh2_ref, dg_ref, db_ref, acc_ref, drf_ref):
        i = pl.program_id(0)
        j = pl.program_id(1)

        @pl.when((i == 0) & (j == 0))
        def _():
            dg_ref[...] = jnp.zeros_like(dg_ref)
            db_ref[...] = jnp.zeros_like(db_ref)

        @pl.when(j == 0)
        def _():
            dout = dh_ref[...]
            xh = xh_ref[...]
            dg_ref[...] += _sum0(dout * xh)
            db_ref[...] += _sum0(dout)
            dr = _ln_bwd(dout, xh, rs_ref[...], g_ref[...])
            drf_ref[...] = dr
            dr_ref[...] = dr.astype(dr_ref.dtype)

        uu = u_ref[...]
        ru = jnp.maximum(uu, 0.0)
        hd_ref[...] = (ru * ru).astype(hd_ref.dtype)
        dhd = _dot_nt(dr_ref[...], w2_ref[...])
        du = (dhd * (2.0 * ru)).astype(du_ref.dtype)
        du_ref[...] = du
        part = _dot_nt(du, w1_ref[...])

        @pl.when(j == 0)
        def _():
            acc_ref[...] = ALPHA * drf_ref[...] + part

        @pl.when(j > 0)
        def _():
            acc_ref[...] += part

        @pl.when(j == N_FF - 1)
        def _():
            dh2_ref[...] = acc_ref[...]

    row = lambda n: pl.BlockSpec((TM, n), lambda i, j: (i, 0))
    cst = pl.BlockSpec((1, D_MODEL), lambda i, j: (0, 0))
    chunk = pl.BlockSpec((TM, FF_CHUNK), lambda i, j: (i, j))
    return pl.pallas_call(
        body, name="mlp_ln_bwd", grid=(t // TM, N_FF),
        in_specs=[row(D_MODEL), row(D_MODEL), row(1), cst, chunk,
                  pl.BlockSpec((D_MODEL, FF_CHUNK), lambda i, j: (0, j)),
                  pl.BlockSpec((FF_CHUNK, D_MODEL), lambda i, j: (j, 0))],
        out_specs=[row(D_MODEL), chunk, chunk, row(D_MODEL), cst, cst],
        out_shape=[jax.ShapeDtypeStruct((t, D_MODEL), MXU_DTYPE), jax.ShapeDtypeStruct((t, D_FF), MXU_DTYPE),
                   jax.ShapeDtypeStruct((t, D_FF), MXU_DTYPE), jax.ShapeDtypeStruct((t, D_MODEL), F32),
                   jax.ShapeDtypeStruct((1, D_MODEL), F32), jax.ShapeDtypeStruct((1, D_MODEL), F32)],
        scratch_shapes=[pltpu.VMEM((TM, D_MODEL), F32), pltpu.VMEM((TM, D_MODEL), F32)],
        compiler_params=_cparams(("arbitrary", "arbitrary")),
    )(dh3, xhat, rstd, g, u, w1, w2)


def _outproj_ln_bwd(dh1, xhat, rstd, g, w):
    t = dh1.shape[0]

    def body(dh_ref, xh_ref, rs_ref, g_ref, w_ref, dr_ref, res_ref, dy_ref, dg_ref, db_ref):
        i = pl.program_id(0)

        @pl.when(i == 0)
        def _():
            dg_ref[...] = jnp.zeros_like(dg_ref)
            db_ref[...] = jnp.zeros_like(db_ref)

        dout = dh_ref[...]
        xh = xh_ref[...]
        dg_ref[...] += _sum0(dout * xh)
        db_ref[...] += _sum0(dout)
        dr = _ln_bwd(dout, xh, rs_ref[...], g_ref[...])
        dr_ref[...] = dr.astype(dr_ref.dtype)
        res_ref[...] = ALPHA * dr
        dy_ref[...] = _dot_nt(dr, w_ref[...])

    return pl.pallas_call(
        body, name="outproj_ln_bwd", grid=(t // TM,),
        in_specs=[_rows(TM, D_MODEL), _rows(TM, D_MODEL), _rows(TM, 1), _const((1, D_MODEL)),
                  _const((D_MODEL, D_MODEL))],
        out_specs=[_rows(TM, D_MODEL), _rows(TM, D_MODEL), _rows(TM, D_MODEL), _const((1, D_MODEL)),
                   _const((1, D_MODEL))],
        out_shape=[jax.ShapeDtypeStruct((t, D_MODEL), MXU_DTYPE), jax.ShapeDtypeStruct((t, D_MODEL), F32),
                   jax.ShapeDtypeStruct((t, D_MODEL), F32), jax.ShapeDtypeStruct((1, D_MODEL), F32),
                   jax.ShapeDtypeStruct((1, D_MODEL), F32)],
        compiler_params=_cparams(("arbitrary",)),
    )(dh1, xhat, rstd, g, w)


def _loss_fwd_bwd(h, target):
    t = h.shape[0]

    def body(h_ref, t_ref, l_ref, dh_ref):
        i = pl.program_id(0)

        @pl.when(i == 0)
        def _():
            l_ref[...] = jnp.zeros_like(l_ref)

        e = h_ref[...] - t_ref[...]
        dh_ref[...] = e * (1.0 / D_MODEL)
        per_tok = jnp.mean(e * e, axis=-1, keepdims=True)
        l_ref[...] += 0.5 * jnp.sum(per_tok, axis=0, keepdims=True)

    return pl.pallas_call(
        body, name="loss_fwd_bwd", grid=(t // TM,),
        in_specs=[_rows(TM, D_MODEL), _rows(TM, D_MODEL)],
        out_specs=[_const((1, 1)), _rows(TM, D_MODEL)],
        out_shape=[jax.ShapeDtypeStruct((1, 1), F32), jax.ShapeDtypeStruct((t, D_MODEL), F32)],
        compiler_params=_cparams(("arbitrary",)),
    )(h, target)


def _pick_col(x, idx):
    lane = lax.broadcasted_iota(jnp.int32, x.shape, 1)
    return jnp.sum(jnp.where(lane == idx, x, 0.0), axis=1, keepdims=True)


def _pick_row(x, idx):
    sub = lax.broadcasted_iota(jnp.int32, x.shape, 0)
    return jnp.sum(jnp.where(sub == idx, x, 0.0), axis=0, keepdims=True)


def _conv_taps(pad_ref, w, tm, base):
    acc = w[0:1, :] * pad_ref[base:base + tm, :]
    for k in range(1, 4):
        acc = acc + w[k:k + 1, :] * pad_ref[base + k:base + k + tm, :]
    return acc


def _ssd_chunk_common(adt_c, tri):
    cs = _dot_f32(tri, adt_c)
    return cs, cs.T, jnp.exp(cs)


def _ssd_head_terms(cs, cst, ecs, dt_c, h, tri):
    cs_col = _pick_col(cs, h)
    cs_row = _pick_row(cst, h)
    dt_col = _pick_col(dt_c, h)
    cs_last = cs_col[SSD_CHUNK - 1:SSD_CHUNK, :]
    lmat = jnp.exp(jnp.where(tri > 0.0, cs_col - cs_row, -1e30))
    ecs_col = _pick_col(ecs, h)
    decay_col = jnp.exp(cs_last - cs_col)
    return cs_col, dt_col, cs_last, lmat, ecs_col, decay_col


def _ssd_fwd(proj, cw, cb, dtb, a_neg, d_lanes, nw):
    t = proj.shape[0]
    tm = SSD_TM
    nt = t // tm
    ncq = tm // SSD_CHUNK
    hb = tm // SUBLANES

    def body(xbc_ref, halo_ref, z_ref, dt_ref, cw_ref, cb_ref, dtb_ref, a_ref, d_ref, nw_ref,
             y_ref, yy_ref, st_ref, xpad, xact, state):
        i = pl.program_id(0)

        @pl.when(i == 0)
        def _():
            state[...] = jnp.zeros_like(state)

        xpad[0:SUBLANES, :] = jnp.where(i > 0, halo_ref[...], 0.0)
        xpad[SUBLANES:SUBLANES + tm, :] = xbc_ref[...]
        acc = cb_ref[...] + _conv_taps(xpad, cw_ref[...], tm, SUBLANES - 3)
        xact[...] = acc * _sigmoid(acc)
        dt = _softplus(dt_ref[...] + dtb_ref[...])
        adt = dt * a_ref[...]
        r_i = lax.broadcasted_iota(jnp.int32, (SSD_CHUNK, SSD_CHUNK), 0)
        c_i = lax.broadcasted_iota(jnp.int32, (SSD_CHUNK, SSD_CHUNK), 1)
        tri = (r_i >= c_i).astype(F32)
        lane1 = lax.broadcasted_iota(jnp.int32, (1, LANES), 1)
        for c in range(ncq):
            sl = slice(c * SSD_CHUNK, (c + 1) * SSD_CHUNK)
            dt_c = dt[sl]
            cs, cst, ecs = _ssd_chunk_common(adt[sl], tri)
            for g in range(2):
                bg = xact[sl, 512 + g * 128:512 + (g + 1) * 128]
                cg = xact[sl, 768 + g * 128:768 + (g + 1) * 128]
                cbm = _dot_nt(cg, bg)
                for pr in range(2):
                    pi = g * 2 + pr
                    psl = slice(pi * 128, (pi + 1) * 128)
                    xp = xact[sl, psl]
                    prev = state[pi]
                    st_ref[c, pi] = prev
                    yp = xp * d_ref[:, psl]
                    new_s = jnp.zeros((SSD_STATE, LANES), F32)
                    dec_lane = jnp.zeros((1, LANES), F32)
                    for hh in range(2):
                        h = g * 4 + pr * 2 + hh
                        lm = (lane1 >= 64) if hh else (lane1 < 64)
                        _, dt_col, cs_last, lmat, ecs_col, decay_col = _ssd_head_terms(cs, cst, ecs, dt_c, h, tri)
                        xdt = jnp.where(lm, xp, 0.0) * dt_col
                        yp = yp + _dot(cbm * lmat, xdt)
                        yp = yp + _dot(cg * ecs_col, jnp.where(lm, prev, 0.0))
                        new_s = new_s + _dot_tn(bg * decay_col, xdt)
                        dec_lane = dec_lane + jnp.where(lm, jnp.exp(cs_last), 0.0)
                    state[pi] = prev * dec_lane + new_s
                    yy_ref[sl, psl] = yp
        yy = yy_ref[...]
        z = z_ref[...]
        yg = yy * (z * _sigmoid(z))
        ms = jnp.mean(yg * yg, axis=-1, keepdims=True)
        y_ref[...] = yg * lax.rsqrt(ms + LN_EPS) * nw_ref[...]

    halo_map = lambda i: (jnp.maximum(i * hb - 1, 0), 0)
    return pl.pallas_call(
        body, name="ssd_fwd", grid=(nt,),
        in_specs=[pl.BlockSpec((tm, SSD_XBC), lambda i: (i, 0)), pl.BlockSpec((SUBLANES, SSD_XBC), halo_map),
                  pl.BlockSpec((tm, SSD_WIDTH), lambda i: (i, P_Z // SSD_WIDTH)),
                  pl.BlockSpec((tm, LANES), lambda i: (i, P_DT // LANES)),
                  _const((4, SSD_XBC)), _const((1, SSD_XBC)), _const((1, LANES)), _const((1, LANES)),
                  _const((1, SSD_WIDTH)), _const((1, SSD_WIDTH))],
        out_specs=[_rows(tm, SSD_WIDTH), _rows(tm, SSD_WIDTH),
                   pl.BlockSpec((ncq, 4, SSD_STATE, LANES), lambda i: (i, 0, 0, 0))],
        out_shape=[jax.ShapeDtypeStruct((t, SSD_WIDTH), F32), jax.ShapeDtypeStruct((t, SSD_WIDTH), F32),
                   jax.ShapeDtypeStruct((t // SSD_CHUNK, 4, SSD_STATE, LANES), F32)],
        scratch_shapes=[pltpu.VMEM((tm + SUBLANES, SSD_XBC), F32), pltpu.VMEM((tm, SSD_XBC), F32),
                        pltpu.VMEM((4, SSD_STATE, LANES), F32)],
        compiler_params=_cparams(("arbitrary",)),
    )(proj, proj, proj, proj, cw, cb, dtb, a_neg, d_lanes, nw)


def _ssd_bwd(dycat, proj, yy, states, cw, cb, dtb, a_neg, d_lanes, nw):
    t = proj.shape[0]
    tm = SSD_TM
    nt = t // tm
    ncq = tm // SSD_CHUNK
    hb = tm // SUBLANES

    def body(dy_ref, xbc_ref, halo_ref, z_ref, dt_ref, yy_ref, st_ref, cw_ref, cb_ref, dtb_ref, a_ref, d_ref, nw_ref,
             dxbc_ref, dz_ref, ddt_ref, dcw_ref, dcb_ref, ddtb_ref, da_ref, dd_ref, dnw_ref,
             xpad, xact, dxact, dpad, dstate, dnext):
        i = pl.program_id(0)

        @pl.when(i == 0)
        def _():
            for r in (dcw_ref, dcb_ref, ddtb_ref, da_ref, dd_ref, dnw_ref, dstate, dnext):
                r[...] = jnp.zeros_like(r)

        xpad[0:SUBLANES, :] = jnp.where(i < nt - 1, halo_ref[...], 0.0)
        xpad[SUBLANES:SUBLANES + tm, :] = xbc_ref[...]
        cw_v = cw_ref[...]
        acc = cb_ref[...] + _conv_taps(xpad, cw_v, tm, SUBLANES - 3)
        sig = _sigmoid(acc)
        xact[...] = acc * sig
        dt_raw = dt_ref[...] + dtb_ref[...]
        dt = _softplus(dt_raw)
        a_v = a_ref[...]
        adt = dt * a_v
        yy = yy_ref[...]
        z = z_ref[...]
        sz = _sigmoid(z)
        siluz = z * sz
        yg = yy * siluz
        ms = jnp.mean(yg * yg, axis=-1, keepdims=True)
        rinv = lax.rsqrt(ms + LN_EPS)
        dout = dy_ref[...]
        dnw_ref[...] += _sum0(dout * yg * rinv)
        dyn = dout * nw_ref[...]
        dyg = rinv * dyn - yg * (rinv * rinv * rinv) * jnp.mean(dyn * yg, axis=-1, keepdims=True)
        dyy = dyg * siluz
        dz_ref[...] = dyg * yy * (sz * (1.0 + z * (1.0 - sz)))
        dd_ref[...] += _sum0(dyy * xact[:, 0:SSD_WIDTH])

        r_i = lax.broadcasted_iota(jnp.int32, (SSD_CHUNK, SSD_CHUNK), 0)
        c_i = lax.broadcasted_iota(jnp.int32, (SSD_CHUNK, SSD_CHUNK), 1)
        tri = (r_i >= c_i).astype(F32)
        lane1 = lax.broadcasted_iota(jnp.int32, (1, LANES), 1)
        for c in reversed(range(ncq)):
            sl = slice(c * SSD_CHUNK, (c + 1) * SSD_CHUNK)
            dt_c = dt[sl]
            cs, cst, ecs = _ssd_chunk_common(adt[sl], tri)
            cacc = jnp.zeros((SSD_CHUNK, LANES), F32)
            racc = jnp.zeros((SSD_CHUNK, LANES), F32)
            ddtx = jnp.zeros((SSD_CHUNK, LANES), F32)
            for g in range(2):
                bg = xact[sl, 512 + g * 128:512 + (g + 1) * 128]
                cg = xact[sl, 768 + g * 128:768 + (g + 1) * 128]
                cbm = _dot_nt(cg, bg)
                dcb_m = jnp.zeros((SSD_CHUNK, SSD_CHUNK), F32)
                dbg = jnp.zeros((SSD_CHUNK, SSD_STATE), F32)
                dcg = jnp.zeros((SSD_CHUNK, SSD_STATE), F32)
                for pr in range(2):
                    pi = g * 2 + pr
                    psl = slice(pi * 128, (pi + 1) * 128)
                    xp = xact[sl, psl]
                    dyp = dyy[sl, psl]
                    prev = st_ref[c, pi]
                    ds_all = dstate[pi]
                    dxdt_p = jnp.zeros((SSD_CHUNK, LANES), F32)
                    dprev_new = jnp.zeros((SSD_STATE, LANES), F32)
                    dec_lane = jnp.zeros((1, LANES), F32)
                    dt_lanes = jnp.zeros((SSD_CHUNK, LANES), F32)
                    for hh in range(2):
                        h = g * 4 + pr * 2 + hh
                        lm = (lane1 >= 64) if hh else (lane1 < 64)
                        oh_l = (c_i == h).astype(F32)
                        oh_s = (r_i == h).astype(F32)
                        _, dt_col, cs_last, lmat, ecs_col, decay_col = _ssd_head_terms(cs, cst, ecs, dt_c, h, tri)
                        gm = cbm * lmat
                        xm = jnp.where(lm, xp, 0.0)
                        xdt = xm * dt_col
                        dym = jnp.where(lm, dyp, 0.0)
                        prevm = jnp.where(lm, prev, 0.0)
                        dsm = jnp.where(lm, ds_all, 0.0)
                        bdec = bg * decay_col
                        dxdt = _dot_tn(gm, dym) + _dot(bdec, dsm)
                        dxdt_p = dxdt_p + dxdt
                        ddtx = ddtx + oh_l * jnp.sum(dxdt * xm, axis=1, keepdims=True)
                        dt_lanes = dt_lanes + jnp.where(lm, dt_col, 0.0)
                        dgm = _dot_nt(dym, xdt)
                        dcb_m = dcb_m + dgm * lmat
                        w = dgm * gm
                        cacc = cacc + oh_l * jnp.sum(w, axis=1, keepdims=True)
                        racc = racc - oh_s * jnp.sum(w, axis=0, keepdims=True)
                        dce = _dot_nt(dym, prevm)
                        dcg = dcg + dce * ecs_col
                        cacc = cacc + oh_l * (jnp.sum(dce * cg, axis=1, keepdims=True) * ecs_col)
                        dprev_new = dprev_new + _dot_tn(cg * ecs_col, dym)
                        dbdec = _dot_nt(xdt, dsm)
                        dbg = dbg + dbdec * decay_col
                        dd = jnp.sum(dbdec * bg, axis=1, keepdims=True) * decay_col
                        cacc = cacc - oh_l * dd
                        cd = jnp.exp(cs_last)
                        dlast = jnp.sum(dd, axis=0, keepdims=True) + jnp.sum(
                            jnp.sum(dsm * prevm, axis=1, keepdims=True), axis=0, keepdims=True) * cd
                        cacc = cacc + jnp.where((r_i == SSD_CHUNK - 1) & (c_i == h), dlast, 0.0)
                        dec_lane = dec_lane + jnp.where(lm, cd, 0.0)
                    dstate[pi] = ds_all * dec_lane + dprev_new
                    dxact[sl, psl] = dxdt_p * dt_lanes + dyp * d_ref[:, psl]
                dcg = dcg + _dot(dcb_m, bg)
                dbg = dbg + _dot_tn(dcb_m, cg)
                dxact[sl, 512 + g * 128:512 + (g + 1) * 128] = dbg
                dxact[sl, 768 + g * 128:768 + (g + 1) * 128] = dcg
            dcs = cacc + racc.T
            dadt = _dot_f32((r_i <= c_i).astype(F32), dcs)
            ddt = dadt * a_v + ddtx
            da_ref[...] += _sum0(dadt * dt_c)
            ddt_raw = ddt * _sigmoid(dt_raw[sl])
            ddt_ref[sl, :] = ddt_raw
            ddtb_ref[...] += _sum0(ddt_raw)
        dacc = dxact[...] * (sig * (1.0 + acc * (1.0 - sig)))
        dcb_ref[...] += _sum0(dacc)
        for k in range(4):
            dcw_ref[k:k + 1, :] += _sum0(dacc * xpad[SUBLANES - 3 + k:SUBLANES - 3 + k + tm, :])
        dpad[0:tm, :] = dacc
        dpad[tm:tm + SUBLANES, :] = dnext[...]
        dx = cw_v[0:1, :] * dpad[3:3 + tm, :]
        for k in range(1, 4):
            dx = dx + cw_v[k:k + 1, :] * dpad[3 - k:3 - k + tm, :]
        dxbc_ref[...] = dx
        dnext[...] = dacc[0:SUBLANES, :]

    rev = lambda i: nt - 1 - i
    halo_map = lambda i: (jnp.maximum(rev(i) * hb - 1, 0), 0)
    rrow = lambda n, col=0: pl.BlockSpec((tm, n), lambda i: (rev(i), col))
    return pl.pallas_call(
        body, name="ssd_bwd", grid=(nt,),
        in_specs=[rrow(SSD_WIDTH), rrow(SSD_XBC), pl.BlockSpec((SUBLANES, SSD_XBC), halo_map),
                  rrow(SSD_WIDTH, P_Z // SSD_WIDTH), rrow(LANES, P_DT // LANES), rrow(SSD_WIDTH),
                  pl.BlockSpec((ncq, 4, SSD_STATE, LANES), lambda i: (rev(i), 0, 0, 0)),
                  _const((4, SSD_XBC)), _const((1, SSD_XBC)), _const((1, LANES)), _const((1, LANES)),
                  _const((1, SSD_WIDTH)), _const((1, SSD_WIDTH))],
        out_specs=[rrow(SSD_XBC), rrow(SSD_WIDTH), rrow(LANES), _const((SUBLANES, SSD_XBC)), _const((1, SSD_XBC)),
                   _const((1, LANES)), _const((1, LANES)), _const((1, SSD_WIDTH)), _const((1, SSD_WIDTH))],
        out_shape=[jax.ShapeDtypeStruct((t, SSD_XBC), F32), jax.ShapeDtypeStruct((t, SSD_WIDTH), F32),
                   jax.ShapeDtypeStruct((t, LANES), F32), jax.ShapeDtypeStruct((SUBLANES, SSD_XBC), F32),
                   jax.ShapeDtypeStruct((1, SSD_XBC), F32), jax.ShapeDtypeStruct((1, LANES), F32),
                   jax.ShapeDtypeStruct((1, LANES), F32), jax.ShapeDtypeStruct((1, SSD_WIDTH), F32),
                   jax.ShapeDtypeStruct((1, SSD_WIDTH), F32)],
        scratch_shapes=[pltpu.VMEM((tm + SUBLANES, SSD_XBC), F32), pltpu.VMEM((tm, SSD_XBC), F32),
                        pltpu.VMEM((tm, SSD_XBC), F32), pltpu.VMEM((tm + SUBLANES, SSD_XBC), F32),
                        pltpu.VMEM((4, SSD_STATE, LANES), F32), pltpu.VMEM((SUBLANES, SSD_XBC), F32)],
        compiler_params=_cparams(("arbitrary",)),
    )(dycat, proj, proj, proj, proj, yy, states, cw, cb, dtb, a_neg, d_lanes, nw)


def _cmul_add(ar, ai, br, bi, cr, ci):
    return ar + br * cr - bi * ci, ai + br * ci + bi * cr


def _s5_fwd(proj, bre, bim, cre, cim, d_skip, glu_w, glu_b, coef):
    t = proj.shape[0]
    tm = SCAN_TM
    ng = tm // SUBLANES

    def body(u_ref, bre_ref, bim_ref, cre_ref, cim_ref, d_ref, w_ref, b_ref, coef_ref,
             y_ref, y2_ref, hre_ref, him_ref, carry):
        i = pl.program_id(0)

        @pl.when(i == 0)
        def _():
            carry[...] = jnp.zeros_like(carry)

        u = u_ref[...]
        hre_ref[...] = _dot(u, bre_ref[...])
        him_ref[...] = _dot(u, bim_ref[...])

        def step(gi, car):
            cr_, ci_ = car
            rows = pl.ds(pl.multiple_of(gi * SUBLANES, SUBLANES), SUBLANES)
            r = hre_ref[rows, :]
            m = him_ref[rows, :]
            for k, sh in enumerate((1, 2, 4)):
                r, m = _cmul_add(r, m, coef_ref[k, 0], coef_ref[k, 1], pltpu.roll(r, sh, 0), pltpu.roll(m, sh, 0))
            r, m = _cmul_add(r, m, coef_ref[3, 0], coef_ref[3, 1], cr_, ci_)
            hre_ref[rows, :] = r
            him_ref[rows, :] = m
            return (jnp.broadcast_to(r[SUBLANES - 1:SUBLANES, :], r.shape),
                    jnp.broadcast_to(m[SUBLANES - 1:SUBLANES, :], m.shape))

        cr_, ci_ = lax.fori_loop(0, ng, step, (carry[0], carry[1]))
        carry[0] = cr_
        carry[1] = ci_
        y2 = _dot(hre_ref[...], cre_ref[...]) - _dot(him_ref[...], cim_ref[...]) + d_ref[...] * u
        y2_ref[...] = y2
        ya = _gelu(y2)
        y_ref[...] = ya * _sigmoid(_dot(ya, w_ref[...]) + b_ref[...])

    return pl.pallas_call(
        body, name="s5_fwd", grid=(t // tm,),
        in_specs=[pl.BlockSpec((tm, S5_WIDTH), lambda i: (i, P_U // S5_WIDTH)),
                  _const((S5_WIDTH, S5_NSTATE)), _const((S5_WIDTH, S5_NSTATE)), _const((S5_NSTATE, S5_WIDTH)),
                  _const((S5_NSTATE, S5_WIDTH)), _const((1, S5_WIDTH)), _const((S5_WIDTH, S5_WIDTH)),
                  _const((1, S5_WIDTH)), _const((5, 2, SUBLANES, S5_NSTATE))],
        out_specs=[_rows(tm, S5_WIDTH), _rows(tm, S5_WIDTH), _rows(tm, S5_NSTATE), _rows(tm, S5_NSTATE)],
        out_shape=[jax.ShapeDtypeStruct((t, S5_WIDTH), F32), jax.ShapeDtypeStruct((t, S5_WIDTH), F32),
                   jax.ShapeDtypeStruct((t, S5_NSTATE), F32), jax.ShapeDtypeStruct((t, S5_NSTATE), F32)],
        scratch_shapes=[pltpu.VMEM((2, SUBLANES, S5_NSTATE), F32)],
        compiler_params=_cparams(("arbitrary",)),
    )(proj, bre, bim, cre, cim, d_skip, glu_w, glu_b, coef)


def _s5_bwd(dycat, proj, y2, hre, him, bre, bim, cre, cim, d_skip, glu_w, glu_b, rcoef):
    t = proj.shape[0]
    tm = SCAN_TM
    nt = t // tm
    ng = tm // SUBLANES
    hb = tm // SUBLANES

    def body(dy_ref, u_ref, y2_ref, hre_ref, him_ref, hre_halo, him_halo, bre_ref, bim_ref, cre_ref, cim_ref, d_ref,
             w_ref, b_ref, coef_ref,
             du_ref, dbre_ref, dbim_ref, dcre_ref, dcim_ref, dlam_ref, dd_ref, dw_ref, dgb_ref,
             gre, gim, hpre, hpim, carry):
        i = pl.program_id(0)

        @pl.when(i == 0)
        def _():
            for r in (dbre_ref, dbim_ref, dcre_ref, dcim_ref, dlam_ref, dd_ref, dw_ref, dgb_ref, carry):
                r[...] = jnp.zeros_like(r)

        u = u_ref[...]
        y2 = y2_ref[...]
        dout = dy_ref[...]
        ya = _gelu(y2)
        sg = _sigmoid(_dot(ya, w_ref[...]) + b_ref[...])
        dv = dout * ya * sg * (1.0 - sg)
        dya = dout * sg + _dot_nt(dv, w_ref[...])
        dw_ref[...] += _dot_tn(ya, dv)
        dgb_ref[...] += _sum0(dv)
        dy2 = dya * _gelu_grad(y2)
        dd_ref[...] += _sum0(dy2 * u)
        hre_v = hre_ref[...]
        him_v = him_ref[...]
        dcre_ref[...] += _dot_tn(hre_v, dy2)
        dcim_ref[...] -= _dot_tn(him_v, dy2)
        gre[...] = _dot_nt(dy2, cre_ref[...])
        gim[...] = -_dot_nt(dy2, cim_ref[...])
        first = i == nt - 1
        hpre[0:SUBLANES, :] = jnp.where(first, 0.0, hre_halo[...])
        hpim[0:SUBLANES, :] = jnp.where(first, 0.0, him_halo[...])
        hpre[SUBLANES:SUBLANES + tm, :] = hre_v
        hpim[SUBLANES:SUBLANES + tm, :] = him_v
        row0 = lax.broadcasted_iota(jnp.int32, (SUBLANES, S5_NSTATE), 0) == 0

        def step(k, car):
            cr_, ci_, dlr, dli = car
            gi = ng - 1 - k
            rows = pl.ds(pl.multiple_of(gi * SUBLANES, SUBLANES), SUBLANES)
            nrows = pl.ds(pl.multiple_of(gi * SUBLANES + SUBLANES, SUBLANES), SUBLANES)
            r = gre[rows, :]
            m = gim[rows, :]
            for kk, sh in enumerate((1, 2, 4)):
                r, m = _cmul_add(r, m, coef_ref[kk, 0], coef_ref[kk, 1], pltpu.roll(r, SUBLANES - sh, 0),
                                 pltpu.roll(m, SUBLANES - sh, 0))
            r, m = _cmul_add(r, m, coef_ref[3, 0], coef_ref[3, 1], cr_, ci_)
            gre[rows, :] = r
            gim[rows, :] = m
            pr_ = hpre[rows, :]
            pm_ = hpim[rows, :]
            hr_ = jnp.where(row0, jnp.broadcast_to(pr_[SUBLANES - 1:SUBLANES, :], pr_.shape),
                            pltpu.roll(hpre[nrows, :], 1, 0))
            hm_ = jnp.where(row0, jnp.broadcast_to(pm_[SUBLANES - 1:SUBLANES, :], pm_.shape),
                            pltpu.roll(hpim[nrows, :], 1, 0))
            dlr = dlr + hr_ * r + hm_ * m
            dli = dli + hr_ * m - hm_ * r
            return (jnp.broadcast_to(r[0:1, :], r.shape), jnp.broadcast_to(m[0:1, :], m.shape), dlr, dli)

        z8 = jnp.zeros((SUBLANES, S5_NSTATE), F32)
        cr_, ci_, dlr, dli = lax.fori_loop(0, ng, step, (carry[0], carry[1], z8, z8))
        carry[0] = cr_
        carry[1] = ci_
        dlam_ref[0] += dlr
        dlam_ref[1] += dli
        g_re = gre[...]
        g_im = gim[...]
        du_ref[...] = dy2 * d_ref[...] + _dot_nt(g_re, bre_ref[...]) + _dot_nt(g_im, bim_ref[...])
        dbre_ref[...] += _dot_tn(u, g_re)
        dbim_ref[...] += _dot_tn(u, g_im)

    rev = lambda i: nt - 1 - i
    rrow = lambda n, col=0: pl.BlockSpec((tm, n), lambda i: (rev(i), col))
    halo = pl.BlockSpec((SUBLANES, S5_NSTATE), lambda i: (jnp.maximum(rev(i) * hb - 1, 0), 0))
    return pl.pallas_call(
        body, name="s5_bwd", grid=(nt,),
        in_specs=[rrow(S5_WIDTH, 512 // S5_WIDTH), rrow(S5_WIDTH, P_U // S5_WIDTH), rrow(S5_WIDTH),
                  rrow(S5_NSTATE), rrow(S5_NSTATE), halo, halo,
                  _const((S5_WIDTH, S5_NSTATE)), _const((S5_WIDTH, S5_NSTATE)), _const((S5_NSTATE, S5_WIDTH)),
                  _const((S5_NSTATE, S5_WIDTH)), _const((1, S5_WIDTH)), _const((S5_WIDTH, S5_WIDTH)),
                  _const((1, S5_WIDTH)), _const((5, 2, SUBLANES, S5_NSTATE))],
        out_specs=[rrow(S5_WIDTH), _const((S5_WIDTH, S5_NSTATE)), _const((S5_WIDTH, S5_NSTATE)),
                   _const((S5_NSTATE, S5_WIDTH)), _const((S5_NSTATE, S5_WIDTH)), _const((2, SUBLANES, S5_NSTATE)),
                   _const((1, S5_WIDTH)), _const((S5_WIDTH, S5_WIDTH)), _const((1, S5_WIDTH))],
        out_shape=[jax.ShapeDtypeStruct((t, S5_WIDTH), F32), jax.ShapeDtypeStruct((S5_WIDTH, S5_NSTATE), F32),
                   jax.ShapeDtypeStruct((S5_WIDTH, S5_NSTATE), F32), jax.ShapeDtypeStruct((S5_NSTATE, S5_WIDTH), F32),
                   jax.ShapeDtypeStruct((S5_NSTATE, S5_WIDTH), F32),
                   jax.ShapeDtypeStruct((2, SUBLANES, S5_NSTATE), F32), jax.ShapeDtypeStruct((1, S5_WIDTH), F32),
                   jax.ShapeDtypeStruct((S5_WIDTH, S5_WIDTH), F32), jax.ShapeDtypeStruct((1, S5_WIDTH), F32)],
        scratch_shapes=[pltpu.VMEM((tm, S5_NSTATE), F32), pltpu.VMEM((tm, S5_NSTATE), F32),
                        pltpu.VMEM((tm + SUBLANES, S5_NSTATE), F32), pltpu.VMEM((tm + SUBLANES, S5_NSTATE), F32),
                        pltpu.VMEM((2, SUBLANES, S5_NSTATE), F32)],
        compiler_params=_cparams(("arbitrary",)),
    )(dycat, proj, y2, hre, him, hre, him, bre, bim, cre, cim, d_skip, glu_w, glu_b, rcoef)


def _rg_gates(xc, wa, ba, wx, bx, nsp):
    r = _sigmoid(_dot(xc, wa) + ba)
    ig = _sigmoid(_dot(xc, wx) + bx)
    log_a = nsp * r
    a = jnp.exp(log_a)
    mult = jnp.sqrt(-_expm1(2.0 * log_a))
    return r, ig, a, mult


def _rg_fwd(proj, cw, cb, wa, ba, wx, bx, nsp):
    t = proj.shape[0]
    tm = SCAN_TM
    ng = tm // SUBLANES
    hb = tm // SUBLANES

    def body(x_ref, halo_ref, gt_ref, cw_ref, cb_ref, wa_ref, ba_ref, wx_ref, bx_ref, nsp_ref,
             y_ref, h_ref, xpad, abuf, carry):
        i = pl.program_id(0)

        @pl.when(i == 0)
        def _():
            carry[...] = jnp.zeros_like(carry)

        xpad[0:SUBLANES, :] = jnp.where(i > 0, halo_ref[...], 0.0)
        xpad[SUBLANES:SUBLANES + tm, :] = x_ref[...]
        xc = cb_ref[...] + _conv_taps(xpad, cw_ref[...], tm, SUBLANES - 3)
        _, ig, a, mult = _rg_gates(xc, wa_ref[...], ba_ref[...], wx_ref[...], bx_ref[...], nsp_ref[...])
        abuf[...] = a
        h_ref[...] = mult * (ig * xc)
        sub = lax.broadcasted_iota(jnp.int32, (SUBLANES, RG_WIDTH), 0)

        def step(gi, car):
            rows = pl.ds(pl.multiple_of(gi * SUBLANES, SUBLANES), SUBLANES)
            av = abuf[rows, :]
            bv = h_ref[rows, :]
            for sh in (1, 2, 4):
                m = sub >= sh
                bv = jnp.where(m, av * pltpu.roll(bv, sh, 0) + bv, bv)
                av = jnp.where(m, av * pltpu.roll(av, sh, 0), av)
            hv = bv + av * car
            h_ref[rows, :] = hv
            return jnp.broadcast_to(hv[SUBLANES - 1:SUBLANES, :], hv.shape)

        carry[...] = lax.fori_loop(0, ng, step, carry[...])
        y_ref[...] = h_ref[...] * _gelu(gt_ref[...])

    return pl.pallas_call(
        body, name="rg_fwd", grid=(t // tm,),
        in_specs=[pl.BlockSpec((tm, RG_WIDTH), lambda i: (i, P_XRG // RG_WIDTH)),
                  pl.BlockSpec((SUBLANES, RG_WIDTH), lambda i: (jnp.maximum(i * hb - 1, 0), P_XRG // RG_WIDTH)),
                  pl.BlockSpec((tm, RG_WIDTH), lambda i: (i, P_GRG // RG_WIDTH)),
                  _const((4, RG_WIDTH)), _const((1, RG_WIDTH)), _const((RG_WIDTH, RG_WIDTH)), _const((1, RG_WIDTH)),
                  _const((RG_WIDTH, RG_WIDTH)), _const((1, RG_WIDTH)), _const((1, RG_WIDTH))],
        out_specs=[_rows(tm, RG_WIDTH), _rows(tm, RG_WIDTH)],
        out_shape=[jax.ShapeDtypeStruct((t, RG_WIDTH), F32), jax.ShapeDtypeStruct((t, RG_WIDTH), F32)],
        scratch_shapes=[pltpu.VMEM((tm + SUBLANES, RG_WIDTH), F32), pltpu.VMEM((tm, RG_WIDTH), F32),
                        pltpu.VMEM((SUBLANES, RG_WIDTH), F32)],
        compiler_params=_cparams(("arbitrary",)),
    )(proj, proj, proj, cw, cb, wa, ba, wx, bx, nsp)


def _rg_bwd(dycat, proj, hs, cw, cb, wa, ba, wx, bx, nsp):
    t = proj.shape[0]
    tm = SCAN_TM
    nt = t // tm
    ng = tm // SUBLANES
    hb = tm // SUBLANES

    def body(dy_ref, x_ref, halo_ref, gt_ref, h_ref, h_halo, cw_ref, cb_ref, wa_ref, ba_ref, wx_ref, bx_ref, nsp_ref,
             dx_ref, dgt_ref, dcw_ref, dcb_ref, dwa_ref, dba_ref, dwx_ref, dbx_ref, dnsp_ref,
             xpad, abuf, gbuf, hpad, dabuf, dpad, carry, dnext):
        i = pl.program_id(0)

        @pl.when(i == 0)
        def _():
            for r in (dcw_ref, dcb_ref, dwa_ref, dba_ref, dwx_ref, dbx_ref, dnsp_ref, carry, dnext):
                r[...] = jnp.zeros_like(r)

        first = i == nt - 1
        xpad[0:SUBLANES, :] = jnp.where(first, 0.0, halo_ref[...])
        xpad[SUBLANES:SUBLANES + tm, :] = x_ref[...]
        cw_v = cw_ref[...]
        xc = cb_ref[...] + _conv_taps(xpad, cw_v, tm, SUBLANES - 3)
        nsp_v = nsp_ref[...]
        r, ig, a, mult = _rg_gates(xc, wa_ref[...], ba_ref[...], wx_ref[...], bx_ref[...], nsp_v)
        abuf[...] = a
        hv = h_ref[...]
        hpad[0:SUBLANES, :] = jnp.where(first, 0.0, h_halo[...])
        hpad[SUBLANES:SUBLANES + tm, :] = hv
        gt = gt_ref[...]
        dout = dy_ref[...]
        dgt_ref[...] = dout * hv * _gelu_grad(gt)
        gbuf[...] = dout * _gelu(gt)
        sub = lax.broadcasted_iota(jnp.int32, (SUBLANES, RG_WIDTH), 0)
        last_row = sub == SUBLANES - 1
        row0 = sub == 0

        def step(k, car):
            gi = ng - 1 - k
            rows = pl.ds(pl.multiple_of(gi * SUBLANES, SUBLANES), SUBLANES)
            nrows = pl.ds(pl.multiple_of(gi * SUBLANES + SUBLANES, SUBLANES), SUBLANES)
            av = abuf[rows, :]
            bv = gbuf[rows, :] + jnp.where(last_row, car, 0.0)
            ev = jnp.where(last_row, 0.0, pltpu.roll(av, SUBLANES - 1, 0))
            for sh in (1, 2, 4):
                m = sub < SUBLANES - sh
                bv = jnp.where(m, bv + ev * pltpu.roll(bv, SUBLANES - sh, 0), bv)
                ev = jnp.where(m, ev * pltpu.roll(ev, SUBLANES - sh, 0), 0.0)
            gbuf[rows, :] = bv
            pv = hpad[rows, :]
            hprev = jnp.where(row0, jnp.broadcast_to(pv[SUBLANES - 1:SUBLANES, :], pv.shape),
                              pltpu.roll(hpad[nrows, :], 1, 0))
            dabuf[rows, :] = bv * hprev
            return jnp.broadcast_to((av * bv)[0:1, :], bv.shape)

        carry[...] = lax.fori_loop(0, ng, step, carry[...])
        gv = gbuf[...]
        da = dabuf[...]
        ix = ig * xc
        dmult = gv * ix
        dig = gv * mult * xc
        dxc = gv * mult * ig
        dlog_a = da * a - dmult * (a * a) / mult
        dnsp_ref[...] += _sum0(dlog_a * r)
        dpr = dlog_a * nsp_v * r * (1.0 - r)
        dpi = dig * ig * (1.0 - ig)
        dxc = dxc + _dot_nt(dpr, wa_ref[...]) + _dot_nt(dpi, wx_ref[...])
        dwa_ref[...] += _dot_tn(xc, dpr)
        dwx_ref[...] += _dot_tn(xc, dpi)
        dba_ref[...] += _sum0(dpr)
        dbx_ref[...] += _sum0(dpi)
        dcb_ref[...] += _sum0(dxc)
        for k in range(4):
            dcw_ref[k:k + 1, :] += _sum0(dxc * xpad[SUBLANES - 3 + k:SUBLANES - 3 + k + tm, :])
        dpad[0:tm, :] = dxc
        dpad[tm:tm + SUBLANES, :] = dnext[...]
        dx = cw_v[0:1, :] * dpad[3:3 + tm, :]
        for k in range(1, 4):
            dx = dx + cw_v[k:k + 1, :] * dpad[3 - k:3 - k + tm, :]
        dx_ref[...] = dx
        dnext[...] = dxc[0:SUBLANES, :]

    rev = lambda i: nt - 1 - i
    rrow = lambda n, col=0: pl.BlockSpec((tm, n), lambda i: (rev(i), col))
    sq = _const((RG_WIDTH, RG_WIDTH))
    vec = _const((1, RG_WIDTH))
    return pl.pallas_call(
        body, name="rg_bwd", grid=(nt,),
        in_specs=[rrow(RG_WIDTH, 768 // RG_WIDTH), rrow(RG_WIDTH, P_XRG // RG_WIDTH),
                  pl.BlockSpec((SUBLANES, RG_WIDTH), lambda i: (jnp.maximum(rev(i) * hb - 1, 0), P_XRG // RG_WIDTH)),
                  rrow(RG_WIDTH, P_GRG // RG_WIDTH), rrow(RG_WIDTH),
                  pl.BlockSpec((SUBLANES, RG_WIDTH), lambda i: (jnp.maximum(rev(i) * hb - 1, 0), 0)),
                  _const((4, RG_WIDTH)), vec, sq, vec, sq, vec, vec],
        out_specs=[rrow(RG_WIDTH), rrow(RG_WIDTH), _const((SUBLANES, RG_WIDTH)), vec, sq, vec, sq, vec, vec],
        out_shape=[jax.ShapeDtypeStruct((t, RG_WIDTH), F32), jax.ShapeDtypeStruct((t, RG_WIDTH), F32),
                   jax.ShapeDtypeStruct((SUBLANES, RG_WIDTH), F32), jax.ShapeDtypeStruct((1, RG_WIDTH), F32),
                   jax.ShapeDtypeStruct((RG_WIDTH, RG_WIDTH), F32), jax.ShapeDtypeStruct((1, RG_WIDTH), F32),
                   jax.ShapeDtypeStruct((RG_WIDTH, RG_WIDTH), F32), jax.ShapeDtypeStruct((1, RG_WIDTH), F32),
                   jax.ShapeDtypeStruct((1, RG_WIDTH), F32)],
        scratch_shapes=[pltpu.VMEM((tm + SUBLANES, RG_WIDTH), F32), pltpu.VMEM((tm, RG_WIDTH), F32),
                        pltpu.VMEM((tm, RG_WIDTH), F32), pltpu.VMEM((tm + SUBLANES, RG_WIDTH), F32),
                        pltpu.VMEM((tm, RG_WIDTH), F32), pltpu.VMEM((tm + SUBLANES, RG_WIDTH), F32),
                        pltpu.VMEM((SUBLANES, RG_WIDTH), F32), pltpu.VMEM((SUBLANES, RG_WIDTH), F32)],
        compiler_params=_cparams(("arbitrary",)),
    )(dycat, proj, proj, proj, hs, hs, cw, cb, wa, ba, wx, bx, nsp)


def _block_diag(blocks):
    g, a, b = blocks.shape
    eye = jnp.eye(g, dtype=blocks.dtype)
    return (eye[:, None, :, None] * blocks[:, :, None, :]).reshape(g * a, g * b)


def _block_diag_extract(m, g):
    a, b = m.shape[0] // g, m.shape[1] // g
    m4 = m.reshape(g, a, g, b)
    idx = jnp.arange(g)
    return m4[idx, :, idx, :]


def _s5_prepare(lam_re, lam_im, log_step, b_re, b_im, c_re, c_im):
    step = jnp.exp(log_step)[:, None]
    mag = jnp.exp(lam_re * step)
    lbr = mag * jnp.cos(lam_im * step)
    lbi = mag * jnp.sin(lam_im * step)
    nr, ni = lbr - 1.0, lbi
    den = lam_re * lam_re + lam_im * lam_im
    cr = (nr * lam_re + ni * lam_im) / den
    ci = (ni * lam_re - nr * lam_im) / den
    bbr = cr[..., None] * b_re - ci[..., None] * b_im
    bbi = cr[..., None] * b_im + ci[..., None] * b_re
    bre = _block_diag(jnp.swapaxes(bbr, 1, 2))
    bim = _block_diag(jnp.swapaxes(bbi, 1, 2))
    cre = _block_diag(jnp.swapaxes(c_re, 1, 2))
    cim = _block_diag(jnp.swapaxes(c_im, 1, 2))
    return lbr.reshape(-1), lbi.reshape(-1), bre, bim, cre, cim


def _s5_scan_coef(lbr, lbi, reverse):
    if reverse:
        lbi = -lbi
    pr, pi = [lbr], [lbi]
    for _ in range(7):
        pr, pi = pr + [pr[-1] * lbr - pi[-1] * lbi], pi + [pr[-1] * lbi + pi[-1] * lbr]
    row = jnp.arange(SUBLANES)[:, None]
    tabs = []
    for sh in (1, 2, 4):
        keep = (row < SUBLANES - sh) if reverse else (row >= sh)
        tabs.append(jnp.stack([jnp.where(keep, pr[sh - 1][None, :], 0.0), jnp.where(keep, pi[sh - 1][None, :], 0.0)]))
    powr = jnp.stack(pr)
    powi = jnp.stack(pi)
    if reverse:
        powr, powi = powr[::-1], powi[::-1]
    tabs.append(jnp.stack([powr, powi]))
    tabs.append(jnp.zeros_like(tabs[-1]))
    return jnp.stack(tabs).astype(F32)


def _xy_peers():
    x, y, c = lax.axis_index("x"), lax.axis_index("y"), lax.axis_index("c")
    return x, y, c, [(1 - x, y), (x, 1 - y), (1 - x, 1 - y)]


def _hbm():
    return pl.BlockSpec(memory_space=pl.ANY)


def _xy_allgather(buf, *, name):
    n, w = buf.shape

    def body(x_ref, out_ref, send_sems, recv_sems, local_sem):
        x, y, c, peers = _xy_peers()
        me = 2 * x + y
        own = pltpu.make_async_copy(x_ref, out_ref.at[me], local_sem)
        own.start()
        sends = []
        for k, (px, py) in enumerate(peers):
            cp = pltpu.make_async_remote_copy(src_ref=x_ref, dst_ref=out_ref.at[me], send_sem=send_sems.at[k],
                                              recv_sem=recv_sems.at[k], device_id=(px, py, c), device_id_type=MESH)
            cp.start()
            sends.append(cp)
        for k, (px, py) in enumerate(peers):
            pltpu.make_async_remote_copy(src_ref=x_ref, dst_ref=out_ref.at[2 * px + py], send_sem=send_sems.at[k],
                                         recv_sem=recv_sems.at[k], device_id=(px, py, c),
                                         device_id_type=MESH).wait_recv()
        for cp in sends:
            cp.wait_send()
        own.wait()

    return pl.pallas_call(
        body, name=name, in_specs=[_hbm()], out_specs=_hbm(),
        out_shape=jax.ShapeDtypeStruct((4, n, w), buf.dtype),
        scratch_shapes=[pltpu.SemaphoreType.DMA((3,)), pltpu.SemaphoreType.DMA((3,)), pltpu.SemaphoreType.DMA],
    )(buf)


def _c_reduce_scatter(g):
    _, n, w = g.shape
    n2 = n // 2

    def body(g_ref, own_ref, got_ref, send_sem, recv_sem, local_sem):
        x, y, c = lax.axis_index("x"), lax.axis_index("y"), lax.axis_index("c")
        mine = g_ref.at[:, pl.ds(c * n2, n2), :]
        theirs = g_ref.at[:, pl.ds((1 - c) * n2, n2), :]
        own = pltpu.make_async_copy(mine, own_ref, local_sem)
        own.start()
        cp = pltpu.make_async_remote_copy(src_ref=theirs, dst_ref=got_ref, send_sem=send_sem, recv_sem=recv_sem,
                                          device_id=(x, y, 1 - c), device_id_type=MESH)
        cp.start()
        cp.wait_recv()
        cp.wait_send()
        own.wait()

    return pl.pallas_call(
        body, name="grad_c_reduce_scatter", in_specs=[_hbm()], out_specs=[_hbm(), _hbm()],
        out_shape=[jax.ShapeDtypeStruct((4, n2, w), g.dtype), jax.ShapeDtypeStruct((4, n2, w), g.dtype)],
        scratch_shapes=[pltpu.SemaphoreType.DMA, pltpu.SemaphoreType.DMA, pltpu.SemaphoreType.DMA],
    )(g)


def _xy_exchange(hs):
    _, n, w = hs.shape

    def body(h_ref, out_ref, send_sems, recv_sems, local_sem):
        x, y, c, peers = _xy_peers()
        me = 2 * x + y
        own = pltpu.make_async_copy(h_ref.at[me], out_ref.at[me], local_sem)
        own.start()
        sends = []
        for k, (px, py) in enumerate(peers):
            cp = pltpu.make_async_remote_copy(src_ref=h_ref.at[2 * px + py], dst_ref=out_ref.at[me],
                                              send_sem=send_sems.at[k], recv_sem=recv_sems.at[k],
                                              device_id=(px, py, c), device_id_type=MESH)
            cp.start()
            sends.append(cp)
        for k, (px, py) in enumerate(peers):
            pltpu.make_async_remote_copy(src_ref=h_ref.at[me], dst_ref=out_ref.at[2 * px + py],
                                         send_sem=send_sems.at[k], recv_sem=recv_sems.at[k], device_id=(px, py, c),
                                         device_id_type=MESH).wait_recv()
        for cp in sends:
            cp.wait_send()
        own.wait()

    return pl.pallas_call(
        body, name="grad_xy_exchange", in_specs=[_hbm()], out_specs=_hbm(),
        out_shape=jax.ShapeDtypeStruct((4, n, w), hs.dtype),
        scratch_shapes=[pltpu.SemaphoreType.DMA((3,)), pltpu.SemaphoreType.DMA((3,)), pltpu.SemaphoreType.DMA],
    )(hs)


def _c_allgather(f):
    n2, w = f.shape

    def body(f_ref, out_ref, send_sem, recv_sem, local_sem):
        x, y, c = lax.axis_index("x"), lax.axis_index("y"), lax.axis_index("c")
        mine = out_ref.at[pl.ds(c * n2, n2), :]
        own = pltpu.make_async_copy(f_ref, mine, local_sem)
        own.start()
        cp = pltpu.make_async_remote_copy(src_ref=f_ref, dst_ref=mine, send_sem=send_sem, recv_sem=recv_sem,
                                          device_id=(x, y, 1 - c), device_id_type=MESH)
        cp.start()
        pltpu.make_async_remote_copy(src_ref=f_ref, dst_ref=out_ref.at[pl.ds((1 - c) * n2, n2), :], send_sem=send_sem,
                                     recv_sem=recv_sem, device_id=(x, y, 1 - c), device_id_type=MESH).wait_recv()
        cp.wait_send()
        own.wait()

    return pl.pallas_call(
        body, name="grad_c_allgather", in_specs=[_hbm()], out_specs=_hbm(),
        out_shape=jax.ShapeDtypeStruct((2 * n2, w), f.dtype),
        scratch_shapes=[pltpu.SemaphoreType.DMA, pltpu.SemaphoreType.DMA, pltpu.SemaphoreType.DMA],
    )(f)


def _flat_rows(n):
    for tr in (512, 256, 128, 64, 32, 16, 8):
        if n % tr == 0:
            return tr
    return n


def _add2(a, b):
    _, n, w = a.shape
    tr = _flat_rows(n)

    def body(a_ref, b_ref, o_ref):
        o_ref[...] = a_ref[...] + b_ref[...]

    spec = pl.BlockSpec((1, tr, w), lambda s, i: (s, i, 0))
    return pl.pallas_call(
        body, name="grad_add2", grid=(4, n // tr), in_specs=[spec, spec], out_specs=spec,
        out_shape=jax.ShapeDtypeStruct(a.shape, a.dtype), compiler_params=_cparams(("arbitrary", "arbitrary")),
    )(a, b)


def _sum4(r):
    _, n, w = r.shape
    tr = _flat_rows(n)

    def body(r_ref, o_ref):
        o_ref[...] = ((r_ref[0] + r_ref[1]) + r_ref[2]) + r_ref[3]

    return pl.pallas_call(
        body, name="grad_sum4", grid=(n // tr,), in_specs=[pl.BlockSpec((4, tr, w), lambda i: (0, i, 0))],
        out_specs=pl.BlockSpec((tr, w), lambda i: (i, 0)), out_shape=jax.ShapeDtypeStruct((n, w), r.dtype),
        compiler_params=_cparams(("arbitrary",)),
    )(r)


def _adamw(w, g, m, v):
    n, width = w.shape
    tr = _flat_rows(n)
    c1 = 1.0 / (1.0 - ADAM_B1 ** ADAM_STEP)
    c2 = 1.0 / (1.0 - ADAM_B2 ** ADAM_STEP)

    def body(w_ref, g_ref, m_ref, v_ref, d_ref, nm_ref, nv_ref):
        gg = g_ref[...]
        nm = ADAM_B1 * m_ref[...] + (1.0 - ADAM_B1) * gg
        nv = ADAM_B2 * v_ref[...] + (1.0 - ADAM_B2) * (gg * gg)
        nm_ref[...] = nm
        nv_ref[...] = nv
        d_ref[...] = -ADAM_LR * ((nm * c1) / (jnp.sqrt(nv * c2) + ADAM_EPS) + ADAM_WD * w_ref[...])

    spec = pl.BlockSpec((tr, width), lambda i: (i, 0))
    sds = jax.ShapeDtypeStruct((n, width), F32)
    return pl.pallas_call(
        body, name="adamw", grid=(n // tr,), in_specs=[spec] * 4, out_specs=[spec] * 3, out_shape=[sds] * 3,
        compiler_params=_cparams(("arbitrary",)),
    )(w, g, m, v)


SHARDED = (
    ("w_in", (2, 1024, 578), 2), ("w_out", (2, 256, 1024), 1), ("xa_wq", (2, 256, 1024), 1),
    ("xa_wk", (2, 256, 1024), 1), ("xa_wv", (2, 256, 1024), 1), ("xa_wo", (2, 256, 1024), 1),
    ("mlp_w1", (2, 1024, 1024), 2), ("mlp_w2", (2, 1024, 1024), 1), ("s5_glu_w", (2, 64, 256), 1),
    ("ssd_conv_w", (2, 4, 256), 2), ("rg_conv_w", (2, 4, 64), 2),
)
N_MXU_SHARDED = 9
REPLICATED = (
    ("ssd_conv_b", (2, 1024)), ("ssd_dt_bias", (2, 8)), ("ssd_a_log", (2, 8)), ("ssd_d", (2, 8)),
    ("ssd_norm_w", (2, 512)), ("s5_lam_re", (2, 16, 64)), ("s5_lam_im", (2, 16, 64)), ("s5_log_step", (2, 16)),
    ("s5_b_re", (2, 16, 64, 16)), ("s5_b_im", (2, 16, 64, 16)), ("s5_c_re", (2, 16, 16, 64)),
    ("s5_c_im", (2, 16, 16, 64)), ("s5_d", (2, 256)), ("s5_glu_b", (2, 256)), ("rg_conv_b", (2, 256)),
    ("rg_wa", (2, 4, 64, 64)), ("rg_ba", (2, 4, 64)), ("rg_wx", (2, 4, 64, 64)), ("rg_bx", (2, 4, 64)),
    ("rg_lambda", (2, 256)), ("ln1_g", (2, 1024)), ("ln1_b", (2, 1024)), ("ln2_g", (2, 1024)), ("ln2_b", (2, 1024)),
    ("ln3_g", (2, 1024)), ("ln3_b", (2, 1024)),
)
WEIGHT_ORDER = (
    "w_in", "w_out", "ssd_conv_w", "ssd_conv_b", "ssd_dt_bias", "ssd_a_log", "ssd_d", "ssd_norm_w", "s5_lam_re",
    "s5_lam_im", "s5_log_step", "s5_b_re", "s5_b_im", "s5_c_re", "s5_c_im", "s5_d", "s5_glu_w", "s5_glu_b",
    "rg_conv_w", "rg_conv_b", "rg_wa", "rg_ba", "rg_wx", "rg_bx", "rg_lambda", "ln1_g", "ln1_b", "xa_wq", "xa_wk",
    "xa_wv", "xa_wo", "ln2_g", "ln2_b", "mlp_w1", "mlp_w2", "ln3_g", "ln3_b",
)


def _size(shape):
    return int(math.prod(shape))


def _pad_rows(flat, rows):
    return jnp.pad(flat, (0, rows * FLAT - flat.shape[0])).reshape(rows, FLAT)


def _round_up(a, b):
    return (a + b - 1) // b * b


SH_ELEMS = sum(_size(s) for _, s, _ in SHARDED)
REP_ELEMS = sum(_size(s) for _, s in REPLICATED)
REP_QROWS = _round_up(-(-REP_ELEMS // (4 * FLAT)), 8)
SH_ROWS = _round_up(-(-SH_ELEMS // FLAT) + REP_QROWS, 1024)
MXU_ELEMS = sum(_size(s) for _, s, _ in SHARDED[:N_MXU_SHARDED])
MXU_ROWS = _round_up(-(-MXU_ELEMS // FLAT), 16)
CONV_ROWS = 8


def _pack_shards(tensors, names_shapes):
    return jnp.concatenate([tensors[n].reshape(-1) for n, *_ in names_shapes])


def _unpack(flat, names_shapes):
    out, off = {}, 0
    for n, s, *_ in names_shapes:
        out[n] = flat[off:off + _size(s)].reshape(s)
        off += _size(s)
    return out


def _gather_full(gathered, names_shapes):
    flat = gathered.reshape(4, -1)
    out, off = {}, 0
    for n, s, ax in names_shapes:
        parts = flat[:, off:off + _size(s)].reshape((4,) + s)
        out[n] = jnp.concatenate([parts[k] for k in range(4)], axis=ax)
        off += _size(s)
    return out


def _split_shards(full, names_shapes):
    rows = []
    for k in range(4):
        parts = []
        for n, s, ax in names_shapes:
            w = s[ax]
            parts.append(lax.slice_in_dim(full[n], k * w, (k + 1) * w, axis=ax).reshape(-1))
        rows.append(jnp.concatenate(parts))
    return jnp.stack(rows)


def _pack_cols(w):
    pad = jnp.zeros((w.shape[0], LANES - SSD_HEADS), w.dtype)
    return jnp.concatenate([w[:, O_XBC:O_XBC + 1024], w[:, O_Z:O_Z + 512], w[:, O_U:O_U + 256],
                            w[:, O_XRG:O_XRG + 256], w[:, O_GRG:O_GRG + 256], w[:, O_DT:O_DT + 8], pad], axis=1)


def _unpack_cols(w):
    return jnp.concatenate([w[:, P_Z:P_Z + 512], w[:, P_XBC:P_XBC + 1024], w[:, P_DT:P_DT + 8],
                            w[:, P_U:P_U + 256], w[:, P_XRG:P_XRG + 256], w[:, P_GRG:P_GRG + 256]], axis=1)


def _lanes(v, width):
    return jnp.pad(v, (0, width - v.shape[0])).reshape(1, width)


def _layer_params(full, rep, l):
    p = {}
    p["w_in"] = _pack_cols(full["w_in"][l])
    for n in ("w_out", "xa_wq", "xa_wk", "xa_wv", "xa_wo", "mlp_w1", "mlp_w2", "s5_glu_w"):
        p[n] = full[n][l]
    p["ssd_cw"] = full["ssd_conv_w"][l]
    p["ssd_cb"] = rep["ssd_conv_b"][l].reshape(1, -1)
    p["ssd_dtb"] = _lanes(rep["ssd_dt_bias"][l], LANES)
    p["ssd_a"] = _lanes(-jnp.exp(rep["ssd_a_log"][l]), LANES)
    p["ssd_d"] = jnp.repeat(rep["ssd_d"][l], 64).reshape(1, -1)
    p["ssd_nw"] = rep["ssd_norm_w"][l].reshape(1, -1)
    s5_args = tuple(rep[n][l] for n in ("s5_lam_re", "s5_lam_im", "s5_log_step", "s5_b_re", "s5_b_im", "s5_c_re",
                                        "s5_c_im"))
    (lbr, lbi, bre, bim, cre, cim), p["s5_vjp"] = jax.vjp(_s5_prepare, *s5_args)
    p.update(s5_bre=bre, s5_bim=bim, s5_cre=cre, s5_cim=cim)
    p["s5_coef"] = _s5_scan_coef(lbr, lbi, False)
    p["s5_rcoef"] = _s5_scan_coef(lbr, lbi, True)
    p["s5_d"] = rep["s5_d"][l].reshape(1, -1)
    p["s5_gb"] = rep["s5_glu_b"][l].reshape(1, -1)
    p["rg_cw"] = full["rg_conv_w"][l]
    p["rg_cb"] = rep["rg_conv_b"][l].reshape(1, -1)
    p["rg_wa"] = _block_diag(rep["rg_wa"][l])
    p["rg_wx"] = _block_diag(rep["rg_wx"][l])
    p["rg_ba"] = rep["rg_ba"][l].reshape(1, -1)
    p["rg_bx"] = rep["rg_bx"][l].reshape(1, -1)
    p["rg_nsp"] = (-RG_C * jax.nn.softplus(-rep["rg_lambda"][l])).reshape(1, -1)
    p["rg_dnsp"] = RG_C * jax.nn.sigmoid(-rep["rg_lambda"][l])
    for n in ("ln1_g", "ln1_b", "ln2_g", "ln2_b", "ln3_g", "ln3_b"):
        p[n] = rep[n][l].reshape(1, -1)
    return p


def _layer_fwd(h, mem, p):
    s = {"h0": h}
    proj = _mm(h, p["w_in"], name="in_proj")
    s["proj"] = proj
    y_ssd, s["ssd_yy"], s["ssd_states"] = _ssd_fwd(proj, p["ssd_cw"], p["ssd_cb"], p["ssd_dtb"], p["ssd_a"],
                                                     p["ssd_d"], p["ssd_nw"])
    y_s5, s["s5_y2"], s["s5_hre"], s["s5_him"] = _s5_fwd(proj, p["s5_bre"], p["s5_bim"], p["s5_cre"], p["s5_cim"],
                                                         p["s5_d"], p["s5_glu_w"], p["s5_gb"], p["s5_coef"])
    y_rg, s["rg_h"] = _rg_fwd(proj, p["rg_cw"], p["rg_cb"], p["rg_wa"], p["rg_ba"], p["rg_wx"], p["rg_bx"],
                              p["rg_nsp"])
    ycat = jnp.concatenate([y_ssd, y_s5, y_rg], axis=1)
    s["ycat"] = ycat
    h1, s["xh1"], s["rs1"] = _outproj_ln_fwd(ycat, h, p["w_out"], p["ln1_g"], p["ln1_b"])
    s["h1"] = h1
    kb = _mm(mem, p["xa_wk"], name="mem_proj")
    vb = _mm(mem, p["xa_wv"], name="mem_proj")
    s["kb"], s["vb"] = kb, vb
    h2, s["xh2"], s["rs2"], s["attn_o"] = _attn_ln_fwd(h1, p["xa_wq"], p["xa_wo"], kb, vb, p["ln2_g"], p["ln2_b"])
    s["h2"] = h2
    h3, s["xh3"], s["rs3"], s["mlp_u"] = _mlp_ln_fwd(h2, p["mlp_w1"], p["mlp_w2"], p["ln3_g"], p["ln3_b"])
    return h3, s


def _layer_bwd(dh3, mem, p, s):
    g = {}
    dr3, du, hdn, dh2, g["ln3_g"], g["ln3_b"] = _mlp_ln_bwd(dh3, s["xh3"], s["rs3"], p["ln3_g"], s["mlp_u"],
                                                             p["mlp_w1"], p["mlp_w2"])
    g["mlp_w1"] = _mm_tn(s["h2"], du, name="wgrad")
    g["mlp_w2"] = _mm_tn(hdn, dr3, name="wgrad")
    dr2, dq, dh1, dkb, dvb, g["ln2_g"], g["ln2_b"] = _attn_ln_bwd(dh2, s["xh2"], s["rs2"], p["ln2_g"], s["h1"],
                                                                   p["xa_wq"], p["xa_wo"], s["kb"], s["vb"])
    g["xa_wo"] = _mm_tn(s["attn_o"], dr2, name="wgrad")
    g["xa_wq"] = _mm_tn(s["h1"], dq, name="wgrad")
    g["xa_wk"] = _mm_tn(mem, dkb, name="wgrad_mem")
    g["xa_wv"] = _mm_tn(mem, dvb, name="wgrad_mem")
    dr1, dres, dycat, g["ln1_g"], g["ln1_b"] = _outproj_ln_bwd(dh1, s["xh1"], s["rs1"], p["ln1_g"], p["w_out"])
    g["w_out"] = _mm_tn(s["ycat"], dr1, name="wgrad")
    proj = s["proj"]
    (dxbc, dz, ddt, dcw, dcb, ddtb, da_neg, dd_l, dnw) = _ssd_bwd(
        dycat, proj, s["ssd_yy"], s["ssd_states"], p["ssd_cw"], p["ssd_cb"], p["ssd_dtb"], p["ssd_a"], p["ssd_d"],
        p["ssd_nw"])
    g["ssd_conv_w"] = dcw[0:4]
    g["ssd_conv_b"] = dcb[0]
    g["ssd_dt_bias"] = ddtb[0, :SSD_HEADS]
    g["ssd_a_log"] = da_neg[0, :SSD_HEADS] * p["ssd_a"][0, :SSD_HEADS]
    g["ssd_d"] = dd_l.reshape(SSD_HEADS, 64).sum(axis=1)
    g["ssd_norm_w"] = dnw[0]
    (du_s5, dbre, dbim, dcre, dcim, dlam, dd5, dgw, dgb) = _s5_bwd(
        dycat, proj, s["s5_y2"], s["s5_hre"], s["s5_him"], p["s5_bre"], p["s5_bim"], p["s5_cre"], p["s5_cim"],
        p["s5_d"], p["s5_glu_w"], p["s5_gb"], p["s5_rcoef"])
    dl = dlam.sum(axis=1)
    s5g = p["s5_vjp"]((dl[0], dl[1], dbre, dbim, dcre, dcim))
    for n, v in zip(("s5_lam_re", "s5_lam_im", "s5_log_step", "s5_b_re", "s5_b_im", "s5_c_re", "s5_c_im"), s5g):
        g[n] = v
    g["s5_d"] = dd5[0]
    g["s5_glu_w"] = dgw
    g["s5_glu_b"] = dgb[0]
    (dxrg, dgrg, drcw, drcb, dwa, dba, dwx, dbx, dnsp) = _rg_bwd(
        dycat, proj, s["rg_h"], p["rg_cw"], p["rg_cb"], p["rg_wa"], p["rg_ba"], p["rg_wx"], p["rg_bx"], p["rg_nsp"])
    g["rg_conv_w"] = drcw[0:4]
    g["rg_conv_b"] = drcb[0]
    g["rg_wa"] = _block_diag_extract(dwa, RG_BLOCKS)
    g["rg_wx"] = _block_diag_extract(dwx, RG_BLOCKS)
    g["rg_ba"] = dba.reshape(RG_BLOCKS, RG_BLOCK_DIM)
    g["rg_bx"] = dbx.reshape(RG_BLOCKS, RG_BLOCK_DIM)
    g["rg_lambda"] = dnsp[0] * p["rg_dnsp"]
    dproj = jnp.concatenate([dxbc, dz, du_s5, dxrg, dgrg, ddt], axis=1)
    g["w_in"] = _unpack_cols(_mm_tn(s["h0"], dproj, name="wgrad_in"))
    dh0 = _mm(dproj, p["w_in"], nt=True, add=dres, name="in_proj_bwd")
    for n in ("ln1_g", "ln1_b", "ln2_g", "ln2_b", "ln3_g", "ln3_b"):
        g[n] = g[n][0]
    return dh0, g


def _local_step(h, memf, target, full, rep):
    params, saved = [], []
    for l in range(DEPTH):
        p = _layer_params(full, rep, l)
        params.append(p)
        h, s = _layer_fwd(h, memf, p)
        saved.append(s)
    loss11, dh = _loss_fwd_bwd(h, target)
    grads = [None] * DEPTH
    for l in reversed(range(DEPTH)):
        dh, grads[l] = _layer_bwd(dh, memf, params[l], saved[l])
    return loss11, dh, {n: jnp.stack([grads[l][n] for l in range(DEPTH)]) for n in WEIGHT_ORDER}


def kernel(x, mem, w_in, w_out, ssd_conv_w, ssd_conv_b, ssd_dt_bias, ssd_a_log, ssd_d, ssd_norm_w, s5_lam_re, s5_lam_im, s5_log_step, s5_b_re, s5_b_im, s5_c_re, s5_c_im, s5_d, s5_glu_w, s5_glu_b, rg_conv_w, rg_conv_b, rg_wa, rg_ba, rg_wx, rg_bx, rg_lambda, ln1_g, ln1_b, xa_wq, xa_wk, xa_wv, xa_wo, ln2_g, ln2_b, mlp_w1, mlp_w2, ln3_g, ln3_b, loss_target, m_w_in, m_w_out, m_ssd_conv_w, m_ssd_conv_b, m_ssd_dt_bias, m_ssd_a_log, m_ssd_d, m_ssd_norm_w, m_s5_lam_re, m_s5_lam_im, m_s5_log_step, m_s5_b_re, m_s5_b_im, m_s5_c_re, m_s5_c_im, m_s5_d, m_s5_glu_w, m_s5_glu_b, m_rg_conv_w, m_rg_conv_b, m_rg_wa, m_rg_ba, m_rg_wx, m_rg_bx, m_rg_lambda, m_ln1_g, m_ln1_b, m_xa_wq, m_xa_wk, m_xa_wv, m_xa_wo, m_ln2_g, m_ln2_b, m_mlp_w1, m_mlp_w2, m_ln3_g, m_ln3_b, v_w_in, v_w_out, v_ssd_conv_w, v_ssd_conv_b, v_ssd_dt_bias, v_ssd_a_log, v_ssd_d, v_ssd_norm_w, v_s5_lam_re, v_s5_lam_im, v_s5_log_step, v_s5_b_re, v_s5_b_im, v_s5_c_re, v_s5_c_im, v_s5_d, v_s5_glu_w, v_s5_glu_b, v_rg_conv_w, v_rg_conv_b, v_rg_wa, v_rg_ba, v_rg_wx, v_rg_bx, v_rg_lambda, v_ln1_g, v_ln1_b, v_xa_wq, v_xa_wk, v_xa_wv, v_xa_wo, v_ln2_g, v_ln2_b, v_mlp_w1, v_mlp_w2, v_ln3_g, v_ln3_b):
    args = dict(locals())
    weights = {n: args[n] for n in WEIGHT_ORDER}
    mom_m = {n: args["m_" + n] for n in WEIGHT_ORDER}
    mom_v = {n: args["v_" + n] for n in WEIGHT_ORDER}

    mxu_names = SHARDED[:N_MXU_SHARDED]
    conv_names = SHARDED[N_MXU_SHARDED:]
    mxu_flat = _pad_rows(_pack_shards(weights, mxu_names), MXU_ROWS).astype(MXU_DTYPE)
    conv_flat = _pad_rows(_pack_shards(weights, conv_names), CONV_ROWS)
    full = _gather_full(_xy_allgather(mxu_flat, name="weights_allgather"), mxu_names)
    full.update(_gather_full(_xy_allgather(conv_flat, name="conv_weights_allgather"), conv_names))
    rep = {n: weights[n] for n, _ in REPLICATED}

    loss11, dx, gfull = _local_step(x[0], mem[0], loss_target[0], full, rep)
    grad_x = dx[None]
    loss = lax.psum(loss11[0, 0], ("x", "y", "c"))

    sh_flat = _split_shards(gfull, SHARDED)
    rep_flat = jnp.pad(_pack_shards(gfull, REPLICATED), (0, 4 * REP_QROWS * FLAT - REP_ELEMS))
    rep_q = rep_flat.reshape(4, REP_QROWS * FLAT)
    body_rows = SH_ROWS - REP_QROWS
    sh_pad = jnp.pad(sh_flat, ((0, 0), (0, body_rows * FLAT - SH_ELEMS)))
    gbuf = jnp.concatenate([sh_pad, rep_q], axis=1).reshape(4, SH_ROWS, FLAT)
    own_half, sib_half = _c_reduce_scatter(gbuf)
    chip_sum = _add2(own_half, sib_half)
    reduced_half = _sum4(_xy_exchange(chip_sum))
    reduced = _c_allgather(reduced_half)
    g_sh = _unpack(reduced[:body_rows].reshape(-1), SHARDED)
    rep_all = _xy_allgather(reduced[body_rows:], name="small_grads_allgather")
    g_rep = _unpack(rep_all.reshape(-1), REPLICATED)
    g_red = {**g_sh, **g_rep}

    all_specs = tuple((n, s) for n, s, _ in SHARDED) + REPLICATED
    tot = SH_ELEMS + REP_ELEMS
    rows = _round_up(-(-tot // FLAT), 512)
    pack = lambda d: _pad_rows(_pack_shards(d, all_specs), rows)
    delta_f, newm_f, newv_f = _adamw(pack(weights), pack(g_red), pack(mom_m), pack(mom_v))
    delta = _unpack(delta_f.reshape(-1), all_specs)
    new_m = _unpack(newm_f.reshape(-1), all_specs)
    new_v = _unpack(newv_f.reshape(-1), all_specs)
    return (loss, grad_x, *[g_red[n] for n in WEIGHT_ORDER], *[delta[n] for n in WEIGHT_ORDER],
            *[new_m[n] for n in WEIGHT_ORDER], *[new_v[n] for n in WEIGHT_ORDER])
```

```python
import functools
import math

import jax
import jax.numpy as jnp
from jax import lax
from jax.experimental import pallas as pl
from jax.experimental.pallas import tpu as pltpu

F32 = jnp.float32
MXU_DTYPE = jnp.bfloat16

D_MODEL = 1024
DEPTH = 2
MEM_LEN = 256
SSD_WIDTH = 512
SSD_HEADS = 8
SSD_STATE = 128
SSD_CHUNK = 128
SSD_XBC = 1024
S5_WIDTH = 256
S5_GROUPS = 16
S5_GROUP_CH = 16
S5_STATE = 64
S5_NSTATE = S5_GROUPS * S5_STATE
RG_WIDTH = 256
RG_BLOCKS = 4
RG_BLOCK_DIM = 64
RG_C = 8.0
XA_HEADS = 4
XA_HEAD_DIM = 256
D_FF = 4096
D_IN = 2312
ALPHA = (2.0 * DEPTH) ** 0.25
LN_EPS = 1e-5
ADAM_LR = 0.001
ADAM_B1 = 0.9
ADAM_B2 = 0.999
ADAM_EPS = 1e-08
ADAM_WD = 0.01
ADAM_STEP = 10

P_XBC, P_Z, P_U, P_XRG, P_GRG, P_DT = 0, 1024, 1536, 1792, 2048, 2304
D_PACK = 2432
O_Z, O_XBC, O_DT, O_U, O_XRG, O_GRG = 0, 512, 1536, 1544, 1800, 2056

LANES = 128
SUBLANES = 8
VMEM_LIMIT = 52 * 1024 * 1024
TM = 512
SSD_TM = 256
SCAN_TM = 512
FLAT = 1024

MESH = pl.DeviceIdType.MESH


def _cparams(sem):
    return pltpu.CompilerParams(dimension_semantics=sem, vmem_limit_bytes=VMEM_LIMIT)


def _dot(a, b):
    return jnp.dot(a.astype(MXU_DTYPE), b.astype(MXU_DTYPE), preferred_element_type=F32)


def _dot_nt(a, b):
    return lax.dot_general(a.astype(MXU_DTYPE), b.astype(MXU_DTYPE), (((1,), (1,)), ((), ())),
                           preferred_element_type=F32)


def _dot_tn(a, b):
    return lax.dot_general(a.astype(MXU_DTYPE), b.astype(MXU_DTYPE), (((0,), (0,)), ((), ())),
                           preferred_element_type=F32)


def _dot_f32(a, b):
    return jnp.dot(a, b, precision=lax.Precision.HIGHEST, preferred_element_type=F32)


def _dot_f32_tn(a, b):
    return lax.dot_general(a, b, (((0,), (0,)), ((), ())), precision=lax.Precision.HIGHEST,
                           preferred_element_type=F32)


def _sigmoid(x):
    return 1.0 / (1.0 + jnp.exp(-x))


def _softplus(x):
    return jnp.maximum(x, 0.0) + jnp.log(1.0 + jnp.exp(-jnp.abs(x)))


_GELU_K = math.sqrt(2.0 / math.pi)


def _gelu(x):
    return 0.5 * x * (1.0 + jnp.tanh(_GELU_K * (x + 0.044715 * x * x * x)))


def _gelu_grad(x):
    t = jnp.tanh(_GELU_K * (x + 0.044715 * x * x * x))
    return 0.5 * (1.0 + t) + 0.5 * x * (1.0 - t * t) * _GELU_K * (1.0 + 3.0 * 0.044715 * x * x)


def _expm1(x):
    small = x * (1.0 + x * (0.5 + x * (1.0 / 6.0 + x * (1.0 / 24.0))))
    return jnp.where(jnp.abs(x) < 0.05, small, jnp.exp(x) - 1.0)


def _sum0(x):
    return jnp.sum(x, axis=0, keepdims=True)


def _ln_fwd(r, g, b):
    mu = jnp.mean(r, axis=-1, keepdims=True)
    xc = r - mu
    var = jnp.mean(xc * xc, axis=-1, keepdims=True)
    rstd = lax.rsqrt(var + LN_EPS)
    xhat = xc * rstd
    return xhat * g + b, xhat, rstd


def _ln_bwd(dout, xhat, rstd, g):
    dxh = dout * g
    m1 = jnp.mean(dxh, axis=-1, keepdims=True)
    m2 = jnp.mean(dxh * xhat, axis=-1, keepdims=True)
    return rstd * (dxh - m1 - xhat * m2)


def _rows(tm, n, col=0):
    return pl.BlockSpec((tm, n), lambda i: (i, col))


def _const(shape):
    nd = len(shape)
    return pl.BlockSpec(shape, lambda i: (0,) * nd)


def _mm(a, w, *, nt=False, add=None, out_dtype=F32, name):
    t, k = a.shape
    n = w.shape[0] if nt else w.shape[1]
    tm = min(TM, t)

    def body(*refs):
        if add is None:
            a_ref, w_ref, o_ref = refs
        else:
            a_ref, w_ref, add_ref, o_ref = refs
        r = _dot_nt(a_ref[...], w_ref[...]) if nt else _dot(a_ref[...], w_ref[...])
        if add is not None:
            r = r + add_ref[...]
        o_ref[...] = r.astype(out_dtype)

    in_specs = [_rows(tm, k), _const(w.shape)]
    args = [a, w]
    if add is not None:
        in_specs.append(_rows(tm, n))
        args.append(add)
    return pl.pallas_call(
        body, name=name, grid=(t // tm,), in_specs=in_specs, out_specs=_rows(tm, n),
        out_shape=jax.ShapeDtypeStruct((t, n), out_dtype), compiler_params=_cparams(("arbitrary",)),
    )(*args)


def _mm_tn(a, g, *, name):
    t, k = a.shape
    n = g.shape[1]
    tt = min(512, t)
    tk = min(1024, k)
    tn = 1024 if n % 1024 == 0 else n
    nsteps = t // tt

    def body(a_ref, g_ref, o_ref):
        s = pl.program_id(2)
        part = _dot_tn(a_ref[...], g_ref[...])

        @pl.when(s == 0)
        def _():
            o_ref[...] = part

        @pl.when(s > 0)
        def _():
            o_ref[...] += part

    return pl.pallas_call(
        body, name=name, grid=(k // tk, n // tn, nsteps),
        in_specs=[pl.BlockSpec((tt, tk), lambda i, j, s: (s, i)), pl.BlockSpec((tt, tn), lambda i, j, s: (s, j))],
        out_specs=pl.BlockSpec((tk, tn), lambda i, j, s: (i, j)),
        out_shape=jax.ShapeDtypeStruct((k, n), F32),
        compiler_params=_cparams(("arbitrary", "arbitrary", "arbitrary")),
    )(a, g)


G_ROWS = 8192
ROW_MLP_W1 = 0
ROW_MLP_W2 = 2048
ROW_W_IN = 4096
ROW_W_OUT = 5376
ROW_XA = {"xa_wq": 5888, "xa_wk": 6400, "xa_wv": 6912, "xa_wo": 7424}
ROW_MISC = 7936
MISC_ROWS = G_ROWS - ROW_MISC
MISC_REP_ROW = 40
W_IN_SHARD = 578
W_IN_PAD = 640


def _wgrad_flat(a, g, buf, *, mode, row_off, name):
    t = a.shape[0]
    tt = min(512, t)
    ns = t // tt
    blk = D_MODEL

    def accumulate(o_ref, part, s):
        @pl.when(s == 0)
        def _():
            if mode == "rows4":
                for q in range(4):
                    o_ref[q] = part[q * 256:(q + 1) * 256]
            else:
                o_ref[0] = part

        @pl.when(s > 0)
        def _():
            if mode == "rows4":
                for q in range(4):
                    o_ref[q] += part[q * 256:(q + 1) * 256]
            else:
                o_ref[0] += part

    if mode == "rows4":
        grid = (ns,)
        in_specs = [pl.BlockSpec((tt, blk), lambda s: (s, 0)), pl.BlockSpec((tt, blk), lambda s: (s, 0))]
        out_spec = pl.BlockSpec((4, 256, FLAT), lambda s: (0, row_off // 256, 0))
        sem = ("arbitrary",)

        def body(a_ref, g_ref, *rest):
            accumulate(rest[-1], _dot_tn(a_ref[...], g_ref[...]), pl.program_id(0))
    else:
        grid = (4, ns)
        if mode == "rowblk":
            in_specs = [pl.BlockSpec((tt, blk), lambda q, s: (s, q)), pl.BlockSpec((tt, blk), lambda q, s: (s, 0))]
        else:
            in_specs = [pl.BlockSpec((tt, blk), lambda q, s: (s, 0)), pl.BlockSpec((tt, blk), lambda q, s: (s, q))]
        out_spec = pl.BlockSpec((1, blk, FLAT), lambda q, s: (q, row_off // blk, 0))
        sem = ("arbitrary", "arbitrary")

        def body(a_ref, g_ref, *rest):
            accumulate(rest[-1], _dot_tn(a_ref[...], g_ref[...]), pl.program_id(1))

    args = [a, g]
    aliases = {}
    if buf is not None:
        in_specs.append(pl.BlockSpec(memory_space=pl.ANY))
        args.append(buf)
        aliases = {2: 0}
    return pl.pallas_call(
        body, name=name, grid=grid, in_specs=in_specs, out_specs=out_spec,
        out_shape=jax.ShapeDtypeStruct((4, G_ROWS, FLAT), F32), input_output_aliases=aliases,
        compiler_params=_cparams(sem),
    )(*args)


def _outproj_ln_fwd(ycat, h, w, g, b):
    t = h.shape[0]

    def body(y_ref, h_ref, w_ref, g_ref, b_ref, hn_ref, xh_ref, rs_ref):
        r = ALPHA * h_ref[...] + _dot(y_ref[...], w_ref[...])
        out, xhat, rstd = _ln_fwd(r, g_ref[...], b_ref[...])
        hn_ref[...] = out
        xh_ref[...] = xhat
        rs_ref[...] = rstd

    return pl.pallas_call(
        body, name="outproj_ln_fwd", grid=(t // TM,),
        in_specs=[_rows(TM, D_MODEL), _rows(TM, D_MODEL), _const((D_MODEL, D_MODEL)), _const((1, D_MODEL)),
                  _const((1, D_MODEL))],
        out_specs=[_rows(TM, D_MODEL), _rows(TM, D_MODEL), _rows(TM, 1)],
        out_shape=[jax.ShapeDtypeStruct((t, D_MODEL), F32), jax.ShapeDtypeStruct((t, D_MODEL), F32),
                   jax.ShapeDtypeStruct((t, 1), F32)],
        compiler_params=_cparams(("arbitrary",)),
    )(ycat, h, w, g, b)


def _attn_probs(q, kb, hh):
    sl = slice(hh * XA_HEAD_DIM, (hh + 1) * XA_HEAD_DIM)
    s = _dot_nt(q[:, sl], kb[:, sl]) * (1.0 / math.sqrt(XA_HEAD_DIM))
    m = jnp.max(s, axis=-1, keepdims=True)
    e = jnp.exp(s - m)
    return e / jnp.sum(e, axis=-1, keepdims=True)


def _attn_ln_fwd(h1, wq, wo, kb, vb, g, b):
    t = h1.shape[0]

    def body(h_ref, wq_ref, wo_ref, k_ref, v_ref, g_ref, b_ref, hn_ref, xh_ref, rs_ref, o_ref):
        h = h_ref[...]
        q = _dot(h, wq_ref[...])
        kb_ = k_ref[...]
        vb_ = v_ref[...]
        for hh in range(XA_HEADS):
            sl = slice(hh * XA_HEAD_DIM, (hh + 1) * XA_HEAD_DIM)
            p = _attn_probs(q, kb_, hh)
            o_ref[:, sl] = _dot(p, vb_[:, sl]).astype(o_ref.dtype)
        r = ALPHA * h + _dot(o_ref[...], wo_ref[...])
        out, xhat, rstd = _ln_fwd(r, g_ref[...], b_ref[...])
        hn_ref[...] = out
        xh_ref[...] = xhat
        rs_ref[...] = rstd

    return pl.pallas_call(
        body, name="attn_ln_fwd", grid=(t // TM,),
        in_specs=[_rows(TM, D_MODEL), _const((D_MODEL, D_MODEL)), _const((D_MODEL, D_MODEL)),
                  _const((MEM_LEN, D_MODEL)), _const((MEM_LEN, D_MODEL)), _const((1, D_MODEL)), _const((1, D_MODEL))],
        out_specs=[_rows(TM, D_MODEL), _rows(TM, D_MODEL), _rows(TM, 1), _rows(TM, D_MODEL)],
        out_shape=[jax.ShapeDtypeStruct((t, D_MODEL), F32), jax.ShapeDtypeStruct((t, D_MODEL), F32),
                   jax.ShapeDtypeStruct((t, 1), F32), jax.ShapeDtypeStruct((t, D_MODEL), MXU_DTYPE)],
        compiler_params=_cparams(("arbitrary",)),
    )(h1, wq, wo, kb, vb, g, b)


def _attn_ln_bwd(dh2, xhat, rstd, g, h1, wq, wo, kb, vb):
    t = h1.shape[0]

    def body(dh_ref, xh_ref, rs_ref, g_ref, h_ref, wq_ref, wo_ref, k_ref, v_ref,
             dr_ref, dq_ref, dh1_ref, dk_ref, dv_ref, dg_ref, db_ref):
        i = pl.program_id(0)

        @pl.when(i == 0)
        def _():
            dk_ref[...] = jnp.zeros_like(dk_ref)
            dv_ref[...] = jnp.zeros_like(dv_ref)
            dg_ref[...] = jnp.zeros_like(dg_ref)
            db_ref[...] = jnp.zeros_like(db_ref)

        dout = dh_ref[...]
        xh = xh_ref[...]
        dg_ref[...] += _sum0(dout * xh)
        db_ref[...] += _sum0(dout)
        dr = _ln_bwd(dout, xh, rs_ref[...], g_ref[...])
        dr_ref[...] = dr.astype(dr_ref.dtype)
        do = _dot_nt(dr, wo_ref[...])
        h = h_ref[...]
        q = _dot(h, wq_ref[...])
        kb_ = k_ref[...]
        vb_ = v_ref[...]
        scale = 1.0 / math.sqrt(XA_HEAD_DIM)
        for hh in range(XA_HEADS):
            sl = slice(hh * XA_HEAD_DIM, (hh + 1) * XA_HEAD_DIM)
            p = _attn_probs(q, kb_, hh)
            do_h = do[:, sl]
            dp = _dot_nt(do_h, vb_[:, sl])
            ds = p * (dp - jnp.sum(dp * p, axis=-1, keepdims=True)) * scale
            dq_ref[:, sl] = _dot(ds, kb_[:, sl]).astype(dq_ref.dtype)
            dk_ref[:, sl] += _dot_tn(ds, q[:, sl])
            dv_ref[:, sl] += _dot_tn(p, do_h)
        dh1_ref[...] = ALPHA * dr + _dot_nt(dq_ref[...], wq_ref[...])

    return pl.pallas_call(
        body, name="attn_ln_bwd", grid=(t // TM,),
        in_specs=[_rows(TM, D_MODEL), _rows(TM, D_MODEL), _rows(TM, 1), _const((1, D_MODEL)), _rows(TM, D_MODEL),
                  _const((D_MODEL, D_MODEL)), _const((D_MODEL, D_MODEL)), _const((MEM_LEN, D_MODEL)),
                  _const((MEM_LEN, D_MODEL))],
        out_specs=[_rows(TM, D_MODEL), _rows(TM, D_MODEL), _rows(TM, D_MODEL), _const((MEM_LEN, D_MODEL)),
                   _const((MEM_LEN, D_MODEL)), _const((1, D_MODEL)), _const((1, D_MODEL))],
        out_shape=[jax.ShapeDtypeStruct((t, D_MODEL), MXU_DTYPE), jax.ShapeDtypeStruct((t, D_MODEL), MXU_DTYPE),
                   jax.ShapeDtypeStruct((t, D_MODEL), F32), jax.ShapeDtypeStruct((MEM_LEN, D_MODEL), F32),
                   jax.ShapeDtypeStruct((MEM_LEN, D_MODEL), F32), jax.ShapeDtypeStruct((1, D_MODEL), F32),
                   jax.ShapeDtypeStruct((1, D_MODEL), F32)],
        compiler_params=_cparams(("arbitrary",)),
    )(dh2, xhat, rstd, g, h1, wq, wo, kb, vb)


FF_CHUNK = 1024
N_FF = D_FF // FF_CHUNK


def _mlp_ln_fwd(h2, w1, w2, g, b):
    t = h2.shape[0]

    def body(h_ref, w1_ref, w2_ref, g_ref, b_ref, hn_ref, xh_ref, rs_ref, u_ref, acc_ref):
        j = pl.program_id(1)
        u = _dot(h_ref[...], w1_ref[...])
        u_ref[...] = u
        part = _dot(jnp.square(jnp.maximum(u, 0.0)), w2_ref[...])

        @pl.when(j == 0)
        def _():
            acc_ref[...] = part

        @pl.when(j > 0)
        def _():
            acc_ref[...] += part

        @pl.when(j == N_FF - 1)
        def _():
            r = ALPHA * h_ref[...] + acc_ref[...]
            out, xhat, rstd = _ln_fwd(r, g_ref[...], b_ref[...])
            hn_ref[...] = out
            xh_ref[...] = xhat
            rs_ref[...] = rstd

    row = lambda n: pl.BlockSpec((TM, n), lambda i, j: (i, 0))
    cst = pl.BlockSpec((1, D_MODEL), lambda i, j: (0, 0))
    return pl.pallas_call(
        body, name="mlp_ln_fwd", grid=(t // TM, N_FF),
        in_specs=[row(D_MODEL), pl.BlockSpec((D_MODEL, FF_CHUNK), lambda i, j: (0, j)),
                  pl.BlockSpec((FF_CHUNK, D_MODEL), lambda i, j: (j, 0)), cst, cst],
        out_specs=[row(D_MODEL), row(D_MODEL), row(1), pl.BlockSpec((TM, FF_CHUNK), lambda i, j: (i, j))],
        out_shape=[jax.ShapeDtypeStruct((t, D_MODEL), F32), jax.ShapeDtypeStruct((t, D_MODEL), F32),
                   jax.ShapeDtypeStruct((t, 1), F32), jax.ShapeDtypeStruct((t, D_FF), F32)],
        scratch_shapes=[pltpu.VMEM((TM, D_MODEL), F32)],
        compiler_params=_cparams(("arbitrary", "arbitrary")),
    )(h2, w1, w2, g, b)


def _mlp_ln_bwd(dh3, xhat, rstd, g, u, w1, w2):
    t = dh3.shape[0]

    def body(dh_ref, xh_ref, rs_ref, g_ref, u_ref, w1_ref, w2_ref,
             dr_ref, du_ref, hd_ref, dh2_ref, dg_ref, db_ref, acc_ref, drf_ref):
        i = pl.program_id(0)
        j = pl.program_id(1)

        @pl.when((i == 0) & (j == 0))
        def _():
            dg_ref[...] = jnp.zeros_like(dg_ref)
            db_ref[...] = jnp.zeros_like(db_ref)

        @pl.when(j == 0)
        def _():
            dout = dh_ref[...]
            xh = xh_ref[...]
            dg_ref[...] += _sum0(dout * xh)
            db_ref[...] += _sum0(dout)
            dr = _ln_bwd(dout, xh, rs_ref[...], g_ref[...])
            drf_ref[...] = dr
            dr_ref[...] = dr.astype(dr_ref.dtype)

        uu = u_ref[...]
        ru = jnp.maximum(uu, 0.0)
        hd_ref[...] = (ru * ru).astype(hd_ref.dtype)
        dhd = _dot_nt(dr_ref[...], w2_ref[...])
        du = (dhd * (2.0 * ru)).astype(du_ref.dtype)
        du_ref[...] = du
        part = _dot_nt(du, w1_ref[...])

        @pl.when(j == 0)
        def _():
            acc_ref[...] = ALPHA * drf_ref[...] + part

        @pl.when(j > 0)
        def _():
            acc_ref[...] += part

        @pl.when(j == N_FF - 1)
        def _():
            dh2_ref[...] = acc_ref[...]

    row = lambda n: pl.BlockSpec((TM, n), lambda i, j: (i, 0))
    cst = pl.BlockSpec((1, D_MODEL), lambda i, j: (0, 0))
    chunk = pl.BlockSpec((TM, FF_CHUNK), lambda i, j: (i, j))
    return pl.pallas_call(
        body, name="mlp_ln_bwd", grid=(t // TM, N_FF),
        in_specs=[row(D_MODEL), row(D_MODEL), row(1), cst, chunk,
                  pl.BlockSpec((D_MODEL, FF_CHUNK), lambda i, j: (0, j)),
                  pl.BlockSpec((FF_CHUNK, D_MODEL), lambda i, j: (j, 0))],
        out_specs=[row(D_MODEL), chunk, chunk, row(D_MODEL), cst, cst],
        out_shape=[jax.ShapeDtypeStruct((t, D_MODEL), MXU_DTYPE), jax.ShapeDtypeStruct((t, D_FF), MXU_DTYPE),
                   jax.ShapeDtypeStruct((t, D_FF), MXU_DTYPE), jax.ShapeDtypeStruct((t, D_MODEL), F32),
                   jax.ShapeDtypeStruct((1, D_MODEL), F32), jax.ShapeDtypeStruct((1, D_MODEL), F32)],
        scratch_shapes=[pltpu.VMEM((TM, D_MODEL), F32), pltpu.VMEM((TM, D_MODEL), F32)],
        compiler_params=_cparams(("arbitrary", "arbitrary")),
    )(dh3, xhat, rstd, g, u, w1, w2)


def _outproj_ln_bwd(dh1, xhat, rstd, g, w):
    t = dh1.shape[0]

    def body(dh_ref, xh_ref, rs_ref, g_ref, w_ref, dr_ref, res_ref, dy_ref, dg_ref, db_ref):
        i = pl.program_id(0)

        @pl.when(i == 0)
        def _():
            dg_ref[...] = jnp.zeros_like(dg_ref)
            db_ref[...] = jnp.zeros_like(db_ref)

        dout = dh_ref[...]
        xh = xh_ref[...]
        dg_ref[...] += _sum0(dout * xh)
        db_ref[...] += _sum0(dout)
        dr = _ln_bwd(dout, xh, rs_ref[...], g_ref[...])
        dr_ref[...] = dr.astype(dr_ref.dtype)
        res_ref[...] = ALPHA * dr
        dy_ref[...] = _dot_nt(dr, w_ref[...])

    return pl.pallas_call(
        body, name="outproj_ln_bwd", grid=(t // TM,),
        in_specs=[_rows(TM, D_MODEL), _rows(TM, D_MODEL), _rows(TM, 1), _const((1, D_MODEL)),
                  _const((D_MODEL, D_MODEL))],
        out_specs=[_rows(TM, D_MODEL), _rows(TM, D_MODEL), _rows(TM, D_MODEL), _const((1, D_MODEL)),
                   _const((1, D_MODEL))],
        out_shape=[jax.ShapeDtypeStruct((t, D_MODEL), MXU_DTYPE), jax.ShapeDtypeStruct((t, D_MODEL), F32),
                   jax.ShapeDtypeStruct((t, D_MODEL), F32), jax.ShapeDtypeStruct((1, D_MODEL), F32),
                   jax.ShapeDtypeStruct((1, D_MODEL), F32)],
        compiler_params=_cparams(("arbitrary",)),
    )(dh1, xhat, rstd, g, w)


def _loss_fwd_bwd(h, target):
    t = h.shape[0]

    def body(h_ref, t_ref, l_ref, dh_ref):
        i = pl.program_id(0)

        @pl.when(i == 0)
        def _():
            l_ref[...] = jnp.zeros_like(l_ref)

        e = h_ref[...] - t_ref[...]
        dh_ref[...] = e * (1.0 / D_MODEL)
        per_tok = jnp.mean(e * e, axis=-1, keepdims=True)
        l_ref[...] += 0.5 * jnp.sum(per_tok, axis=0, keepdims=True)

    return pl.pallas_call(
        body, name="loss_fwd_bwd", grid=(t // TM,),
        in_specs=[_rows(TM, D_MODEL), _rows(TM, D_MODEL)],
        out_specs=[_const((1, 1)), _rows(TM, D_MODEL)],
        out_shape=[jax.ShapeDtypeStruct((1, 1), F32), jax.ShapeDtypeStruct((t, D_MODEL), F32)],
        compiler_params=_cparams(("arbitrary",)),
    )(h, target)


def _pick_col(x, idx):
    lane = lax.broadcasted_iota(jnp.int32, x.shape, 1)
    return jnp.sum(jnp.where(lane == idx, x, 0.0), axis=1, keepdims=True)


def _pick_row(x, idx):
    sub = lax.broadcasted_iota(jnp.int32, x.shape, 0)
    return jnp.sum(jnp.where(sub == idx, x, 0.0), axis=0, keepdims=True)


def _conv_taps(pad_ref, w, tm, base):
    acc = w[0:1, :] * pad_ref[base:base + tm, :]
    for k in range(1, 4):
        acc = acc + w[k:k + 1, :] * pad_ref[base + k:base + k + tm, :]
    return acc


def _ssd_chunk_common(adt_c, tri):
    cs = _dot_f32(tri, adt_c)
    return cs, cs.T, jnp.exp(cs)


def _ssd_head_terms(cs, cst, ecs, dt_c, h, tri):
    cs_col = _pick_col(cs, h)
    cs_row = _pick_row(cst, h)
    dt_col = _pick_col(dt_c, h)
    cs_last = cs_col[SSD_CHUNK - 1:SSD_CHUNK, :]
    lmat = jnp.exp(jnp.where(tri > 0.0, cs_col - cs_row, -1e30))
    ecs_col = _pick_col(ecs, h)
    decay_col = jnp.exp(cs_last - cs_col)
    return cs_col, dt_col, cs_last, lmat, ecs_col, decay_col


def _ssd_fwd(proj, cw, cb, dtb, a_neg, d_lanes, nw):
    t = proj.shape[0]
    tm = SSD_TM
    nt = t // tm
    ncq = tm // SSD_CHUNK
    hb = tm // SUBLANES

    def body(xbc_ref, halo_ref, z_ref, dt_ref, cw_ref, cb_ref, dtb_ref, a_ref, d_ref, nw_ref,
             y_ref, yy_ref, st_ref, xpad, xact, state):
        i = pl.program_id(0)

        @pl.when(i == 0)
        def _():
            state[...] = jnp.zeros_like(state)

        xpad[0:SUBLANES, :] = jnp.where(i > 0, halo_ref[...], 0.0)
        xpad[SUBLANES:SUBLANES + tm, :] = xbc_ref[...]
        acc = cb_ref[...] + _conv_taps(xpad, cw_ref[...], tm, SUBLANES - 3)
        xact[...] = acc * _sigmoid(acc)
        dt = _softplus(dt_ref[...] + dtb_ref[...])
        adt = dt * a_ref[...]
        r_i = lax.broadcasted_iota(jnp.int32, (SSD_CHUNK, SSD_CHUNK), 0)
        c_i = lax.broadcasted_iota(jnp.int32, (SSD_CHUNK, SSD_CHUNK), 1)
        tri = (r_i >= c_i).astype(F32)
        lane1 = lax.broadcasted_iota(jnp.int32, (1, LANES), 1)
        for c in range(ncq):
            sl = slice(c * SSD_CHUNK, (c + 1) * SSD_CHUNK)
            dt_c = dt[sl]
            cs, cst, ecs = _ssd_chunk_common(adt[sl], tri)
            for g in range(2):
                bg = xact[sl, 512 + g * 128:512 + (g + 1) * 128]
                cg = xact[sl, 768 + g * 128:768 + (g + 1) * 128]
                cbm = _dot_nt(cg, bg)
                for pr in range(2):
                    pi = g * 2 + pr
                    psl = slice(pi * 128, (pi + 1) * 128)
                    xp = xact[sl, psl]
                    prev = state[pi]
                    st_ref[c, pi] = prev
                    yp = xp * d_ref[:, psl]
                    new_s = jnp.zeros((SSD_STATE, LANES), F32)
                    dec_lane = jnp.zeros((1, LANES), F32)
                    for hh in range(2):
                        h = g * 4 + pr * 2 + hh
                        lm = (lane1 >= 64) if hh else (lane1 < 64)
                        _, dt_col, cs_last, lmat, ecs_col, decay_col = _ssd_head_terms(cs, cst, ecs, dt_c, h, tri)
                        xdt = jnp.where(lm, xp, 0.0) * dt_col
                        yp = yp + _dot(cbm * lmat, xdt)
                        yp = yp + _dot(cg * ecs_col, jnp.where(lm, prev, 0.0))
                        new_s = new_s + _dot_tn(bg * decay_col, xdt)
                        dec_lane = dec_lane + jnp.where(lm, jnp.exp(cs_last), 0.0)
                    state[pi] = prev * dec_lane + new_s
                    yy_ref[sl, psl] = yp
        yy = yy_ref[...]
        z = z_ref[...]
        yg = yy * (z * _sigmoid(z))
        ms = jnp.mean(yg * yg, axis=-1, keepdims=True)
        y_ref[...] = yg * lax.rsqrt(ms + LN_EPS) * nw_ref[...]

    halo_map = lambda i: (jnp.maximum(i * hb - 1, 0), 0)
    return pl.pallas_call(
        body, name="ssd_fwd", grid=(nt,),
        in_specs=[pl.BlockSpec((tm, SSD_XBC), lambda i: (i, 0)), pl.BlockSpec((SUBLANES, SSD_XBC), halo_map),
                  pl.BlockSpec((tm, SSD_WIDTH), lambda i: (i, P_Z // SSD_WIDTH)),
                  pl.BlockSpec((tm, LANES), lambda i: (i, P_DT // LANES)),
                  _const((4, SSD_XBC)), _const((1, SSD_XBC)), _const((1, LANES)), _const((1, LANES)),
                  _const((1, SSD_WIDTH)), _const((1, SSD_WIDTH))],
        out_specs=[_rows(tm, SSD_WIDTH), _rows(tm, SSD_WIDTH),
                   pl.BlockSpec((ncq, 4, SSD_STATE, LANES), lambda i: (i, 0, 0, 0))],
        out_shape=[jax.ShapeDtypeStruct((t, SSD_WIDTH), F32), jax.ShapeDtypeStruct((t, SSD_WIDTH), F32),
                   jax.ShapeDtypeStruct((t // SSD_CHUNK, 4, SSD_STATE, LANES), F32)],
        scratch_shapes=[pltpu.VMEM((tm + SUBLANES, SSD_XBC), F32), pltpu.VMEM((tm, SSD_XBC), F32),
                        pltpu.VMEM((4, SSD_STATE, LANES), F32)],
        compiler_params=_cparams(("arbitrary",)),
    )(proj, proj, proj, proj, cw, cb, dtb, a_neg, d_lanes, nw)


def _ssd_bwd(dycat, proj, yy, states, cw, cb, dtb, a_neg, d_lanes, nw):
    t = proj.shape[0]
    tm = SSD_TM
    nt = t // tm
    ncq = tm // SSD_CHUNK
    hb = tm // SUBLANES

    def body(dy_ref, xbc_ref, halo_ref, z_ref, dt_ref, yy_ref, st_ref, cw_ref, cb_ref, dtb_ref, a_ref, d_ref, nw_ref,
             dxbc_ref, dz_ref, ddt_ref, dcw_ref, dcb_ref, ddtb_ref, da_ref, dd_ref, dnw_ref,
             xpad, xact, dxact, dpad, dstate, dnext):
        i = pl.program_id(0)

        @pl.when(i == 0)
        def _():
            for r in (dcw_ref, dcb_ref, ddtb_ref, da_ref, dd_ref, dnw_ref, dstate, dnext):
                r[...] = jnp.zeros_like(r)

        xpad[0:SUBLANES, :] = jnp.where(i < nt - 1, halo_ref[...], 0.0)
        xpad[SUBLANES:SUBLANES + tm, :] = xbc_ref[...]
        cw_v = cw_ref[...]
        acc = cb_ref[...] + _conv_taps(xpad, cw_v, tm, SUBLANES - 3)
        sig = _sigmoid(acc)
        xact[...] = acc * sig
        dt_raw = dt_ref[...] + dtb_ref[...]
        dt = _softplus(dt_raw)
        a_v = a_ref[...]
        adt = dt * a_v
        yy = yy_ref[...]
        z = z_ref[...]
        sz = _sigmoid(z)
        siluz = z * sz
        yg = yy * siluz
        ms = jnp.mean(yg * yg, axis=-1, keepdims=True)
        rinv = lax.rsqrt(ms + LN_EPS)
        dout = dy_ref[...]
        dnw_ref[...] += _sum0(dout * yg * rinv)
        dyn = dout * nw_ref[...]
        dyg = rinv * dyn - yg * (rinv * rinv * rinv) * jnp.mean(dyn * yg, axis=-1, keepdims=True)
        dyy = dyg * siluz
        dz_ref[...] = dyg * yy * (sz * (1.0 + z * (1.0 - sz)))
        dd_ref[...] += _sum0(dyy * xact[:, 0:SSD_WIDTH])

        r_i = lax.broadcasted_iota(jnp.int32, (SSD_CHUNK, SSD_CHUNK), 0)
        c_i = lax.broadcasted_iota(jnp.int32, (SSD_CHUNK, SSD_CHUNK), 1)
        tri = (r_i >= c_i).astype(F32)
        lane1 = lax.broadcasted_iota(jnp.int32, (1, LANES), 1)
        for c in reversed(range(ncq)):
            sl = slice(c * SSD_CHUNK, (c + 1) * SSD_CHUNK)
            dt_c = dt[sl]
            cs, cst, ecs = _ssd_chunk_common(adt[sl], tri)
            cacc = jnp.zeros((SSD_CHUNK, LANES), F32)
            racc = jnp.zeros((SSD_CHUNK, LANES), F32)
            ddtx = jnp.zeros((SSD_CHUNK, LANES), F32)
            for g in range(2):
                bg = xact[sl, 512 + g * 128:512 + (g + 1) * 128]
                cg = xact[sl, 768 + g * 128:768 + (g + 1) * 128]
                cbm = _dot_nt(cg, bg)
                dcb_m = jnp.zeros((SSD_CHUNK, SSD_CHUNK), F32)
                dbg = jnp.zeros((SSD_CHUNK, SSD_STATE), F32)
                dcg = jnp.zeros((SSD_CHUNK, SSD_STATE), F32)
                for pr in range(2):
                    pi = g * 2 + pr
                    psl = slice(pi * 128, (pi + 1) * 128)
                    xp = xact[sl, psl]
                    dyp = dyy[sl, psl]
                    prev = st_ref[c, pi]
                    ds_all = dstate[pi]
                    dxdt_p = jnp.zeros((SSD_CHUNK, LANES), F32)
                    dprev_new = jnp.zeros((SSD_STATE, LANES), F32)
                    dec_lane = jnp.zeros((1, LANES), F32)
                    dt_lanes = jnp.zeros((SSD_CHUNK, LANES), F32)
                    for hh in range(2):
                        h = g * 4 + pr * 2 + hh
                        lm = (lane1 >= 64) if hh else (lane1 < 64)
                        oh_l = (c_i == h).astype(F32)
                        oh_s = (r_i == h).astype(F32)
                        _, dt_col, cs_last, lmat, ecs_col, decay_col = _ssd_head_terms(cs, cst, ecs, dt_c, h, tri)
                        gm = cbm * lmat
                        xm = jnp.where(lm, xp, 0.0)
                        xdt = xm * dt_col
                        dym = jnp.where(lm, dyp, 0.0)
                        prevm = jnp.where(lm, prev, 0.0)
                        dsm = jnp.where(lm, ds_all, 0.0)
                        bdec = bg * decay_col
                        dxdt = _dot_tn(gm, dym) + _dot(bdec, dsm)
                        dxdt_p = dxdt_p + dxdt
                        ddtx = ddtx + oh_l * jnp.sum(dxdt * xm, axis=1, keepdims=True)
                        dt_lanes = dt_lanes + jnp.where(lm, dt_col, 0.0)
                        dgm = _dot_nt(dym, xdt)
                        dcb_m = dcb_m + dgm * lmat
                        w = dgm * gm
                        cacc = cacc + oh_l * jnp.sum(w, axis=1, keepdims=True)
                        racc = racc - oh_s * jnp.sum(w, axis=0, keepdims=True)
                        dce = _dot_nt(dym, prevm)
                        dcg = dcg + dce * ecs_col
                        cacc = cacc + oh_l * (jnp.sum(dce * cg, axis=1, keepdims=True) * ecs_col)
                        dprev_new = dprev_new + _dot_tn(cg * ecs_col, dym)
                        dbdec = _dot_nt(xdt, dsm)
                        dbg = dbg + dbdec * decay_col
                        dd = jnp.sum(dbdec * bg, axis=1, keepdims=True) * decay_col
                        cacc = cacc - oh_l * dd
                        cd = jnp.exp(cs_last)
                        dlast = jnp.sum(dd, axis=0, keepdims=True) + jnp.sum(
                            jnp.sum(dsm * prevm, axis=1, keepdims=True), axis=0, keepdims=True) * cd
                        cacc = cacc + jnp.where((r_i == SSD_CHUNK - 1) & (c_i == h), dlast, 0.0)
                        dec_lane = dec_lane + jnp.where(lm, cd, 0.0)
                    dstate[pi] = ds_all * dec_lane + dprev_new
                    dxact[sl, psl] = dxdt_p * dt_lanes + dyp * d_ref[:, psl]
                dcg = dcg + _dot(dcb_m, bg)
                dbg = dbg + _dot_tn(dcb_m, cg)
                dxact[sl, 512 + g * 128:512 + (g + 1) * 128] = dbg
                dxact[sl, 768 + g * 128:768 + (g + 1) * 128] = dcg
            dcs = cacc + racc.T
            dadt = _dot_f32((r_i <= c_i).astype(F32), dcs)
            ddt = dadt * a_v + ddtx
            da_ref[...] += _sum0(dadt * dt_c)
            ddt_raw = ddt * _sigmoid(dt_raw[sl])
            ddt_ref[sl, :] = ddt_raw
            ddtb_ref[...] += _sum0(ddt_raw)
        dacc = dxact[...] * (sig * (1.0 + acc * (1.0 - sig)))
        dcb_ref[...] += _sum0(dacc)
        for k in range(4):
            dcw_ref[k:k + 1, :] += _sum0(dacc * xpad[SUBLANES - 3 + k:SUBLANES - 3 + k + tm, :])
        dpad[0:tm, :] = dacc
        dpad[tm:tm + SUBLANES, :] = dnext[...]
        dx = cw_v[0:1, :] * dpad[3:3 + tm, :]
        for k in range(1, 4):
            dx = dx + cw_v[k:k + 1, :] * dpad[3 - k:3 - k + tm, :]
        dxbc_ref[...] = dx
        dnext[...] = dacc[0:SUBLANES, :]

    rev = lambda i: nt - 1 - i
    halo_map = lambda i: (jnp.maximum(rev(i) * hb - 1, 0), 0)
    rrow = lambda n, col=0: pl.BlockSpec((tm, n), lambda i: (rev(i), col))
    return pl.pallas_call(
        body, name="ssd_bwd", grid=(nt,),
        in_specs=[rrow(SSD_WIDTH), rrow(SSD_XBC), pl.BlockSpec((SUBLANES, SSD_XBC), halo_map),
                  rrow(SSD_WIDTH, P_Z // SSD_WIDTH), rrow(LANES, P_DT // LANES), rrow(SSD_WIDTH),
                  pl.BlockSpec((ncq, 4, SSD_STATE, LANES), lambda i: (rev(i), 0, 0, 0)),
                  _const((4, SSD_XBC)), _const((1, SSD_XBC)), _const((1, LANES)), _const((1, LANES)),
                  _const((1, SSD_WIDTH)), _const((1, SSD_WIDTH))],
        out_specs=[rrow(SSD_XBC), rrow(SSD_WIDTH), rrow(LANES), _const((SUBLANES, SSD_XBC)), _const((1, SSD_XBC)),
                   _const((1, LANES)), _const((1, LANES)), _const((1, SSD_WIDTH)), _const((1, SSD_WIDTH))],
        out_shape=[jax.ShapeDtypeStruct((t, SSD_XBC), F32), jax.ShapeDtypeStruct((t, SSD_WIDTH), F32),
                   jax.ShapeDtypeStruct((t, LANES), F32), jax.ShapeDtypeStruct((SUBLANES, SSD_XBC), F32),
                   jax.ShapeDtypeStruct((1, SSD_XBC), F32), jax.ShapeDtypeStruct((1, LANES), F32),
                   jax.ShapeDtypeStruct((1, LANES), F32), jax.ShapeDtypeStruct((1, SSD_WIDTH), F32),
                   jax.ShapeDtypeStruct((1, SSD_WIDTH), F32)],
        scratch_shapes=[pltpu.VMEM((tm + SUBLANES, SSD_XBC), F32), pltpu.VMEM((tm, SSD_XBC), F32),
                        pltpu.VMEM((tm, SSD_XBC), F32), pltpu.VMEM((tm + SUBLANES, SSD_XBC), F32),
                        pltpu.VMEM((4, SSD_STATE, LANES), F32), pltpu.VMEM((SUBLANES, SSD_XBC), F32)],
        compiler_params=_cparams(("arbitrary",)),
    )(dycat, proj, proj, proj, proj, yy, states, cw, cb, dtb, a_neg, d_lanes, nw)


def _cmul_add(ar, ai, br, bi, cr, ci):
    return ar + br * cr - bi * ci, ai + br * ci + bi * cr


def _s5_fwd(proj, bre, bim, cre, cim, d_skip, glu_w, glu_b, coef):
    t = proj.shape[0]
    tm = SCAN_TM
    ng = tm // SUBLANES

    def body(u_ref, bre_ref, bim_ref, cre_ref, cim_ref, d_ref, w_ref, b_ref, coef_ref,
             y_ref, y2_ref, hre_ref, him_ref, carry):
        i = pl.program_id(0)

        @pl.when(i == 0)
        def _():
            carry[...] = jnp.zeros_like(carry)

        u = u_ref[...]
        hre_ref[...] = _dot(u, bre_ref[...])
        him_ref[...] = _dot(u, bim_ref[...])

        def step(gi, car):
            cr_, ci_ = car
            rows = pl.ds(pl.multiple_of(gi * SUBLANES, SUBLANES), SUBLANES)
            r = hre_ref[rows, :]
            m = him_ref[rows, :]
            for k, sh in enumerate((1, 2, 4)):
                r, m = _cmul_add(r, m, coef_ref[k, 0], coef_ref[k, 1], pltpu.roll(r, sh, 0), pltpu.roll(m, sh, 0))
            r, m = _cmul_add(r, m, coef_ref[3, 0], coef_ref[3, 1], cr_, ci_)
            hre_ref[rows, :] = r
            him_ref[rows, :] = m
            return (jnp.broadcast_to(r[SUBLANES - 1:SUBLANES, :], r.shape),
                    jnp.broadcast_to(m[SUBLANES - 1:SUBLANES, :], m.shape))

        cr_, ci_ = lax.fori_loop(0, ng, step, (carry[0], carry[1]))
        carry[0] = cr_
        carry[1] = ci_
        y2 = _dot(hre_ref[...], cre_ref[...]) - _dot(him_ref[...], cim_ref[...]) + d_ref[...] * u
        y2_ref[...] = y2
        ya = _gelu(y2)
        y_ref[...] = ya * _sigmoid(_dot(ya, w_ref[...]) + b_ref[...])

    return pl.pallas_call(
        body, name="s5_fwd", grid=(t // tm,),
        in_specs=[pl.BlockSpec((tm, S5_WIDTH), lambda i: (i, P_U // S5_WIDTH)),
                  _const((S5_WIDTH, S5_NSTATE)), _const((S5_WIDTH, S5_NSTATE)), _const((S5_NSTATE, S5_WIDTH)),
                  _const((S5_NSTATE, S5_WIDTH)), _const((1, S5_WIDTH)), _const((S5_WIDTH, S5_WIDTH)),
                  _const((1, S5_WIDTH)), _const((5, 2, SUBLANES, S5_NSTATE))],
        out_specs=[_rows(tm, S5_WIDTH), _rows(tm, S5_WIDTH), _rows(tm, S5_NSTATE), _rows(tm, S5_NSTATE)],
        out_shape=[jax.ShapeDtypeStruct((t, S5_WIDTH), F32), jax.ShapeDtypeStruct((t, S5_WIDTH), F32),
                   jax.ShapeDtypeStruct((t, S5_NSTATE), F32), jax.ShapeDtypeStruct((t, S5_NSTATE), F32)],
        scratch_shapes=[pltpu.VMEM((2, SUBLANES, S5_NSTATE), F32)],
        compiler_params=_cparams(("arbitrary",)),
    )(proj, bre, bim, cre, cim, d_skip, glu_w, glu_b, coef)


def _s5_bwd(dycat, proj, y2, hre, him, bre, bim, cre, cim, d_skip, glu_w, glu_b, rcoef):
    t = proj.shape[0]
    tm = SCAN_TM
    nt = t // tm
    ng = tm // SUBLANES
    hb = tm // SUBLANES

    def body(dy_ref, u_ref, y2_ref, hre_ref, him_ref, hre_halo, him_halo, bre_ref, bim_ref, cre_ref, cim_ref, d_ref,
             w_ref, b_ref, coef_ref,
             du_ref, dbre_ref, dbim_ref, dcre_ref, dcim_ref, dlam_ref, dd_ref, dw_ref, dgb_ref,
             gre, gim, hpre, hpim, carry):
        i = pl.program_id(0)

        @pl.when(i == 0)
        def _():
            for r in (dbre_ref, dbim_ref, dcre_ref, dcim_ref, dlam_ref, dd_ref, dw_ref, dgb_ref, carry):
                r[...] = jnp.zeros_like(r)

        u = u_ref[...]
        y2 = y2_ref[...]
        dout = dy_ref[...]
        ya = _gelu(y2)
        sg = _sigmoid(_dot(ya, w_ref[...]) + b_ref[...])
        dv = dout * ya * sg * (1.0 - sg)
        dya = dout * sg + _dot_nt(dv, w_ref[...])
        dw_ref[...] += _dot_tn(ya, dv)
        dgb_ref[...] += _sum0(dv)
        dy2 = dya * _gelu_grad(y2)
        dd_ref[...] += _sum0(dy2 * u)
        hre_v = hre_ref[...]
        him_v = him_ref[...]
        dcre_ref[...] += _dot_tn(hre_v, dy2)
        dcim_ref[...] -= _dot_tn(him_v, dy2)
        gre[...] = _dot_nt(dy2, cre_ref[...])
        gim[...] = -_dot_nt(dy2, cim_ref[...])
        first = i == nt - 1
        hpre[0:SUBLANES, :] = jnp.where(first, 0.0, hre_halo[...])
        hpim[0:SUBLANES, :] = jnp.where(first, 0.0, him_halo[...])
        hpre[SUBLANES:SUBLANES + tm, :] = hre_v
        hpim[SUBLANES:SUBLANES + tm, :] = him_v
        row0 = lax.broadcasted_iota(jnp.int32, (SUBLANES, S5_NSTATE), 0) == 0

        def step(k, car):
            cr_, ci_, dlr, dli = car
            gi = ng - 1 - k
            rows = pl.ds(pl.multiple_of(gi * SUBLANES, SUBLANES), SUBLANES)
            nrows = pl.ds(pl.multiple_of(gi * SUBLANES + SUBLANES, SUBLANES), SUBLANES)
            r = gre[rows, :]
            m = gim[rows, :]
            for kk, sh in enumerate((1, 2, 4)):
                r, m = _cmul_add(r, m, coef_ref[kk, 0], coef_ref[kk, 1], pltpu.roll(r, SUBLANES - sh, 0),
                                 pltpu.roll(m, SUBLANES - sh, 0))
            r, m = _cmul_add(r, m, coef_ref[3, 0], coef_ref[3, 1], cr_, ci_)
            gre[rows, :] = r
            gim[rows, :] = m
            pr_ = hpre[rows, :]
            pm_ = hpim[rows, :]
            hr_ = jnp.where(row0, jnp.broadcast_to(pr_[SUBLANES - 1:SUBLANES, :], pr_.shape),
                            pltpu.roll(hpre[nrows, :], 1, 0))
            hm_ = jnp.where(row0, jnp.broadcast_to(pm_[SUBLANES - 1:SUBLANES, :], pm_.shape),
                            pltpu.roll(hpim[nrows, :], 1, 0))
            dlr = dlr + hr_ * r + hm_ * m
            dli = dli + hr_ * m - hm_ * r
            return (jnp.broadcast_to(r[0:1, :], r.shape), jnp.broadcast_to(m[0:1, :], m.shape), dlr, dli)

        z8 = jnp.zeros((SUBLANES, S5_NSTATE), F32)
        cr_, ci_, dlr, dli = lax.fori_loop(0, ng, step, (carry[0], carry[1], z8, z8))
        carry[0] = cr_
        carry[1] = ci_
        dlam_ref[0] += dlr
        dlam_ref[1] += dli
        g_re = gre[...]
        g_im = gim[...]
        du_ref[...] = dy2 * d_ref[...] + _dot_nt(g_re, bre_ref[...]) + _dot_nt(g_im, bim_ref[...])
        dbre_ref[...] += _dot_tn(u, g_re)
        dbim_ref[...] += _dot_tn(u, g_im)

    rev = lambda i: nt - 1 - i
    rrow = lambda n, col=0: pl.BlockSpec((tm, n), lambda i: (rev(i), col))
    halo = pl.BlockSpec((SUBLANES, S5_NSTATE), lambda i: (jnp.maximum(rev(i) * hb - 1, 0), 0))
    return pl.pallas_call(
        body, name="s5_bwd", grid=(nt,),
        in_specs=[rrow(S5_WIDTH, 512 // S5_WIDTH), rrow(S5_WIDTH, P_U // S5_WIDTH), rrow(S5_WIDTH),
                  rrow(S5_NSTATE), rrow(S5_NSTATE), halo, halo,
                  _const((S5_WIDTH, S5_NSTATE)), _const((S5_WIDTH, S5_NSTATE)), _const((S5_NSTATE, S5_WIDTH)),
                  _const((S5_NSTATE, S5_WIDTH)), _const((1, S5_WIDTH)), _const((S5_WIDTH, S5_WIDTH)),
                  _const((1, S5_WIDTH)), _const((5, 2, SUBLANES, S5_NSTATE))],
        out_specs=[rrow(S5_WIDTH), _const((S5_WIDTH, S5_NSTATE)), _const((S5_WIDTH, S5_NSTATE)),
                   _const((S5_NSTATE, S5_WIDTH)), _const((S5_NSTATE, S5_WIDTH)), _const((2, SUBLANES, S5_NSTATE)),
                   _const((1, S5_WIDTH)), _const((S5_WIDTH, S5_WIDTH)), _const((1, S5_WIDTH))],
        out_shape=[jax.ShapeDtypeStruct((t, S5_WIDTH), F32), jax.ShapeDtypeStruct((S5_WIDTH, S5_NSTATE), F32),
                   jax.ShapeDtypeStruct((S5_WIDTH, S5_NSTATE), F32), jax.ShapeDtypeStruct((S5_NSTATE, S5_WIDTH), F32),
                   jax.ShapeDtypeStruct((S5_NSTATE, S5_WIDTH), F32),
                   jax.ShapeDtypeStruct((2, SUBLANES, S5_NSTATE), F32), jax.ShapeDtypeStruct((1, S5_WIDTH), F32),
                   jax.ShapeDtypeStruct((S5_WIDTH, S5_WIDTH), F32), jax.ShapeDtypeStruct((1, S5_WIDTH), F32)],
        scratch_shapes=[pltpu.VMEM((tm, S5_NSTATE), F32), pltpu.VMEM((tm, S5_NSTATE), F32),
                        pltpu.VMEM((tm + SUBLANES, S5_NSTATE), F32), pltpu.VMEM((tm + SUBLANES, S5_NSTATE), F32),
                        pltpu.VMEM((2, SUBLANES, S5_NSTATE), F32)],
        compiler_params=_cparams(("arbitrary",)),
    )(dycat, proj, y2, hre, him, hre, him, bre, bim, cre, cim, d_skip, glu_w, glu_b, rcoef)


def _rg_gates(xc, wa, ba, wx, bx, nsp):
    r = _sigmoid(_dot(xc, wa) + ba)
    ig = _sigmoid(_dot(xc, wx) + bx)
    log_a = nsp * r
    a = jnp.exp(log_a)
    mult = jnp.sqrt(-_expm1(2.0 * log_a))
    return r, ig, a, mult


def _rg_fwd(proj, cw, cb, wa, ba, wx, bx, nsp):
    t = proj.shape[0]
    tm = SCAN_TM
    ng = tm // SUBLANES
    hb = tm // SUBLANES

    def body(x_ref, halo_ref, gt_ref, cw_ref, cb_ref, wa_ref, ba_ref, wx_ref, bx_ref, nsp_ref,
             y_ref, h_ref, xpad, abuf, carry):
        i = pl.program_id(0)

        @pl.when(i == 0)
        def _():
            carry[...] = jnp.zeros_like(carry)

        xpad[0:SUBLANES, :] = jnp.where(i > 0, halo_ref[...], 0.0)
        xpad[SUBLANES:SUBLANES + tm, :] = x_ref[...]
        xc = cb_ref[...] + _conv_taps(xpad, cw_ref[...], tm, SUBLANES - 3)
        _, ig, a, mult = _rg_gates(xc, wa_ref[...], ba_ref[...], wx_ref[...], bx_ref[...], nsp_ref[...])
        abuf[...] = a
        h_ref[...] = mult * (ig * xc)
        sub = lax.broadcasted_iota(jnp.int32, (SUBLANES, RG_WIDTH), 0)

        def step(gi, car):
            rows = pl.ds(pl.multiple_of(gi * SUBLANES, SUBLANES), SUBLANES)
            av = abuf[rows, :]
            bv = h_ref[rows, :]
            for sh in (1, 2, 4):
                m = sub >= sh
                bv = jnp.where(m, av * pltpu.roll(bv, sh, 0) + bv, bv)
                av = jnp.where(m, av * pltpu.roll(av, sh, 0), av)
            hv = bv + av * car
            h_ref[rows, :] = hv
            return jnp.broadcast_to(hv[SUBLANES - 1:SUBLANES, :], hv.shape)

        carry[...] = lax.fori_loop(0, ng, step, carry[...])
        y_ref[...] = h_ref[...] * _gelu(gt_ref[...])

    return pl.pallas_call(
        body, name="rg_fwd", grid=(t // tm,),
        in_specs=[pl.BlockSpec((tm, RG_WIDTH), lambda i: (i, P_XRG // RG_WIDTH)),
                  pl.BlockSpec((SUBLANES, RG_WIDTH), lambda i: (jnp.maximum(i * hb - 1, 0), P_XRG // RG_WIDTH)),
                  pl.BlockSpec((tm, RG_WIDTH), lambda i: (i, P_GRG // RG_WIDTH)),
                  _const((4, RG_WIDTH)), _const((1, RG_WIDTH)), _const((RG_WIDTH, RG_WIDTH)), _const((1, RG_WIDTH)),
                  _const((RG_WIDTH, RG_WIDTH)), _const((1, RG_WIDTH)), _const((1, RG_WIDTH))],
        out_specs=[_rows(tm, RG_WIDTH), _rows(tm, RG_WIDTH)],
        out_shape=[jax.ShapeDtypeStruct((t, RG_WIDTH), F32), jax.ShapeDtypeStruct((t, RG_WIDTH), F32)],
        scratch_shapes=[pltpu.VMEM((tm + SUBLANES, RG_WIDTH), F32), pltpu.VMEM((tm, RG_WIDTH), F32),
                        pltpu.VMEM((SUBLANES, RG_WIDTH), F32)],
        compiler_params=_cparams(("arbitrary",)),
    )(proj, proj, proj, cw, cb, wa, ba, wx, bx, nsp)


def _rg_bwd(dycat, proj, hs, cw, cb, wa, ba, wx, bx, nsp):
    t = proj.shape[0]
    tm = SCAN_TM
    nt = t // tm
    ng = tm // SUBLANES
    hb = tm // SUBLANES

    def body(dy_ref, x_ref, halo_ref, gt_ref, h_ref, h_halo, cw_ref, cb_ref, wa_ref, ba_ref, wx_ref, bx_ref, nsp_ref,
             dx_ref, dgt_ref, dcw_ref, dcb_ref, dwa_ref, dba_ref, dwx_ref, dbx_ref, dnsp_ref,
             xpad, abuf, gbuf, hpad, dabuf, dpad, carry, dnext):
        i = pl.program_id(0)

        @pl.when(i == 0)
        def _():
            for r in (dcw_ref, dcb_ref, dwa_ref, dba_ref, dwx_ref, dbx_ref, dnsp_ref, carry, dnext):
                r[...] = jnp.zeros_like(r)

        first = i == nt - 1
        xpad[0:SUBLANES, :] = jnp.where(first, 0.0, halo_ref[...])
        xpad[SUBLANES:SUBLANES + tm, :] = x_ref[...]
        cw_v = cw_ref[...]
        xc = cb_ref[...] + _conv_taps(xpad, cw_v, tm, SUBLANES - 3)
        nsp_v = nsp_ref[...]
        r, ig, a, mult = _rg_gates(xc, wa_ref[...], ba_ref[...], wx_ref[...], bx_ref[...], nsp_v)
        abuf[...] = a
        hv = h_ref[...]
        hpad[0:SUBLANES, :] = jnp.where(first, 0.0, h_halo[...])
        hpad[SUBLANES:SUBLANES + tm, :] = hv
        gt = gt_ref[...]
        dout = dy_ref[...]
        dgt_ref[...] = dout * hv * _gelu_grad(gt)
        gbuf[...] = dout * _gelu(gt)
        sub = lax.broadcasted_iota(jnp.int32, (SUBLANES, RG_WIDTH), 0)
        last_row = sub == SUBLANES - 1
        row0 = sub == 0

        def step(k, car):
            gi = ng - 1 - k
            rows = pl.ds(pl.multiple_of(gi * SUBLANES, SUBLANES), SUBLANES)
            nrows = pl.ds(pl.multiple_of(gi * SUBLANES + SUBLANES, SUBLANES), SUBLANES)
            av = abuf[rows, :]
            bv = gbuf[rows, :] + jnp.where(last_row, car, 0.0)
            ev = jnp.where(last_row, 0.0, pltpu.roll(av, SUBLANES - 1, 0))
            for sh in (1, 2, 4):
                m = sub < SUBLANES - sh
                bv = jnp.where(m, bv + ev * pltpu.roll(bv, SUBLANES - sh, 0), bv)
                ev = jnp.where(m, ev * pltpu.roll(ev, SUBLANES - sh, 0), 0.0)
            gbuf[rows, :] = bv
            pv = hpad[rows, :]
            hprev = jnp.where(row0, jnp.broadcast_to(pv[SUBLANES - 1:SUBLANES, :], pv.shape),
                              pltpu.roll(hpad[nrows, :], 1, 0))
            dabuf[rows, :] = bv * hprev
            return jnp.broadcast_to((av * bv)[0:1, :], bv.shape)

        carry[...] = lax.fori_loop(0, ng, step, carry[...])
        gv = gbuf[...]
        da = dabuf[...]
        ix = ig * xc
        dmult = gv * ix
        dig = gv * mult * xc
        dxc = gv * mult * ig
        dlog_a = da * a - dmult * (a * a) / mult
        dnsp_ref[...] += _sum0(dlog_a * r)
        dpr = dlog_a * nsp_v * r * (1.0 - r)
        dpi = dig * ig * (1.0 - ig)
        dxc = dxc + _dot_nt(dpr, wa_ref[...]) + _dot_nt(dpi, wx_ref[...])
        dwa_ref[...] += _dot_tn(xc, dpr)
        dwx_ref[...] += _dot_tn(xc, dpi)
        dba_ref[...] += _sum0(dpr)
        dbx_ref[...] += _sum0(dpi)
        dcb_ref[...] += _sum0(dxc)
        for k in range(4):
            dcw_ref[k:k + 1, :] += _sum0(dxc * xpad[SUBLANES - 3 + k:SUBLANES - 3 + k + tm, :])
        dpad[0:tm, :] = dxc
        dpad[tm:tm + SUBLANES, :] = dnext[...]
        dx = cw_v[0:1, :] * dpad[3:3 + tm, :]
        for k in range(1, 4):
            dx = dx + cw_v[k:k + 1, :] * dpad[3 - k:3 - k + tm, :]
        dx_ref[...] = dx
        dnext[...] = dxc[0:SUBLANES, :]

    rev = lambda i: nt - 1 - i
    rrow = lambda n, col=0: pl.BlockSpec((tm, n), lambda i: (rev(i), col))
    sq = _const((RG_WIDTH, RG_WIDTH))
    vec = _const((1, RG_WIDTH))
    return pl.pallas_call(
        body, name="rg_bwd", grid=(nt,),
        in_specs=[rrow(RG_WIDTH, 768 // RG_WIDTH), rrow(RG_WIDTH, P_XRG // RG_WIDTH),
                  pl.BlockSpec((SUBLANES, RG_WIDTH), lambda i: (jnp.maximum(rev(i) * hb - 1, 0), P_XRG // RG_WIDTH)),
                  rrow(RG_WIDTH, P_GRG // RG_WIDTH), rrow(RG_WIDTH),
                  pl.BlockSpec((SUBLANES, RG_WIDTH), lambda i: (jnp.maximum(rev(i) * hb - 1, 0), 0)),
                  _const((4, RG_WIDTH)), vec, sq, vec, sq, vec, vec],
        out_specs=[rrow(RG_WIDTH), rrow(RG_WIDTH), _const((SUBLANES, RG_WIDTH)), vec, sq, vec, sq, vec, vec],
        out_shape=[jax.ShapeDtypeStruct((t, RG_WIDTH), F32), jax.ShapeDtypeStruct((t, RG_WIDTH), F32),
                   jax.ShapeDtypeStruct((SUBLANES, RG_WIDTH), F32), jax.ShapeDtypeStruct((1, RG_WIDTH), F32),
                   jax.ShapeDtypeStruct((RG_WIDTH, RG_WIDTH), F32), jax.ShapeDtypeStruct((1, RG_WIDTH), F32),
                   jax.ShapeDtypeStruct((RG_WIDTH, RG_WIDTH), F32), jax.ShapeDtypeStruct((1, RG_WIDTH), F32),
                   jax.ShapeDtypeStruct((1, RG_WIDTH), F32)],
        scratch_shapes=[pltpu.VMEM((tm + SUBLANES, RG_WIDTH), F32), pltpu.VMEM((tm, RG_WIDTH), F32),
                        pltpu.VMEM((tm, RG_WIDTH), F32), pltpu.VMEM((tm + SUBLANES, RG_WIDTH), F32),
                        pltpu.VMEM((tm, RG_WIDTH), F32), pltpu.VMEM((tm + SUBLANES, RG_WIDTH), F32),
                        pltpu.VMEM((SUBLANES, RG_WIDTH), F32), pltpu.VMEM((SUBLANES, RG_WIDTH), F32)],
        compiler_params=_cparams(("arbitrary",)),
    )(dycat, proj, proj, proj, hs, hs, cw, cb, wa, ba, wx, bx, nsp)


def _block_diag(blocks):
    g, a, b = blocks.shape
    eye = jnp.eye(g, dtype=blocks.dtype)
    return (eye[:, None, :, None] * blocks[:, :, None, :]).reshape(g * a, g * b)


def _block_diag_extract(m, g):
    a, b = m.shape[0] // g, m.shape[1] // g
    m4 = m.reshape(g, a, g, b)
    idx = jnp.arange(g)
    return m4[idx, :, idx, :]


def _s5_prepare(lam_re, lam_im, log_step, b_re, b_im, c_re, c_im):
    step = jnp.exp(log_step)[:, None]
    mag = jnp.exp(lam_re * step)
    lbr = mag * jnp.cos(lam_im * step)
    lbi = mag * jnp.sin(lam_im * step)
    nr, ni = lbr - 1.0, lbi
    den = lam_re * lam_re + lam_im * lam_im
    cr = (nr * lam_re + ni * lam_im) / den
    ci = (ni * lam_re - nr * lam_im) / den
    bbr = cr[..., None] * b_re - ci[..., None] * b_im
    bbi = cr[..., None] * b_im + ci[..., None] * b_re
    bre = _block_diag(jnp.swapaxes(bbr, 1, 2))
    bim = _block_diag(jnp.swapaxes(bbi, 1, 2))
    cre = _block_diag(jnp.swapaxes(c_re, 1, 2))
    cim = _block_diag(jnp.swapaxes(c_im, 1, 2))
    return lbr.reshape(-1), lbi.reshape(-1), bre, bim, cre, cim


def _s5_scan_coef(lbr, lbi, reverse):
    if reverse:
        lbi = -lbi
    pr, pi = [lbr], [lbi]
    for _ in range(7):
        pr, pi = pr + [pr[-1] * lbr - pi[-1] * lbi], pi + [pr[-1] * lbi + pi[-1] * lbr]
    row = jnp.arange(SUBLANES)[:, None]
    tabs = []
    for sh in (1, 2, 4):
        keep = (row < SUBLANES - sh) if reverse else (row >= sh)
        tabs.append(jnp.stack([jnp.where(keep, pr[sh - 1][None, :], 0.0), jnp.where(keep, pi[sh - 1][None, :], 0.0)]))
    powr = jnp.stack(pr)
    powi = jnp.stack(pi)
    if reverse:
        powr, powi = powr[::-1], powi[::-1]
    tabs.append(jnp.stack([powr, powi]))
    tabs.append(jnp.zeros_like(tabs[-1]))
    return jnp.stack(tabs).astype(F32)


def _xy_peers():
    x, y, c = lax.axis_index("x"), lax.axis_index("y"), lax.axis_index("c")
    return x, y, c, [(1 - x, y), (x, 1 - y), (1 - x, 1 - y)]


def _hbm():
    return pl.BlockSpec(memory_space=pl.ANY)


def _xy_allgather(buf, *, name):
    n, w = buf.shape

    def body(x_ref, out_ref, send_sems, recv_sems, local_sem):
        x, y, c, peers = _xy_peers()
        me = 2 * x + y
        own = pltpu.make_async_copy(x_ref, out_ref.at[me], local_sem)
        own.start()
        sends = []
        for k, (px, py) in enumerate(peers):
            cp = pltpu.make_async_remote_copy(src_ref=x_ref, dst_ref=out_ref.at[me], send_sem=send_sems.at[k],
                                              recv_sem=recv_sems.at[k], device_id=(px, py, c), device_id_type=MESH)
            cp.start()
            sends.append(cp)
        for k, (px, py) in enumerate(peers):
            pltpu.make_async_remote_copy(src_ref=x_ref, dst_ref=out_ref.at[2 * px + py], send_sem=send_sems.at[k],
                                         recv_sem=recv_sems.at[k], device_id=(px, py, c),
                                         device_id_type=MESH).wait_recv()
        for cp in sends:
            cp.wait_send()
        own.wait()

    return pl.pallas_call(
        body, name=name, in_specs=[_hbm()], out_specs=_hbm(),
        out_shape=jax.ShapeDtypeStruct((4, n, w), buf.dtype),
        scratch_shapes=[pltpu.SemaphoreType.DMA((3,)), pltpu.SemaphoreType.DMA((3,)), pltpu.SemaphoreType.DMA],
    )(buf)


def _remote(src, dst, send_sem, recv_sem, dev):
    return pltpu.make_async_remote_copy(src_ref=src, dst_ref=dst, send_sem=send_sem, recv_sem=recv_sem,
                                        device_id=dev, device_id_type=MESH)


GATHERED = (
    ("w_in", (2, 1024, W_IN_PAD), 2), ("w_out", (2, 256, 1024), 1), ("xa_wq", (2, 256, 1024), 1),
    ("xa_wk", (2, 256, 1024), 1), ("xa_wv", (2, 256, 1024), 1), ("xa_wo", (2, 256, 1024), 1),
    ("mlp_w1", (2, 1024, 1024), 2), ("mlp_w2", (2, 1024, 1024), 1), ("s5_glu_w", (2, 64, 256), 1),
)


def _weights_allgather(shards):
    n = len(shards)
    axes = [ax for _, _, ax in GATHERED]

    def body(*refs):
        srcs, dsts = refs[:n], refs[n:2 * n]
        send_sems, recv_sems, local_sems = refs[2 * n:]
        x, y, c, peers = _xy_peers()
        me = 2 * x + y

        def part(t, pos):
            w = shards[t].shape[axes[t]]
            idx = tuple(pl.ds(pos * w, w) if d == axes[t] else slice(None) for d in range(shards[t].ndim))
            return dsts[t].at[idx]

        own = []
        for t in range(n):
            cp = pltpu.make_async_copy(srcs[t], part(t, me), local_sems.at[t])
            cp.start()
            own.append(cp)
        sends = []
        for k, (px, py) in enumerate(peers):
            for t in range(n):
                cp = _remote(srcs[t], part(t, me), send_sems.at[k * n + t], recv_sems.at[k * n + t], (px, py, c))
                cp.start()
                sends.append(cp)
        for k, (px, py) in enumerate(peers):
            for t in range(n):
                _remote(srcs[t], part(t, 2 * px + py), send_sems.at[k * n + t], recv_sems.at[k * n + t],
                        (px, py, c)).wait_recv()
        for cp in sends:
            cp.wait_send()
        for cp in own:
            cp.wait()

    def full_shape(s, ax):
        return s[:ax] + (4 * s[ax],) + s[ax + 1:]

    return pl.pallas_call(
        body, name="weights_allgather", in_specs=[_hbm()] * n, out_specs=[_hbm()] * n,
        out_shape=[jax.ShapeDtypeStruct(full_shape(a.shape, ax), a.dtype) for a, ax in zip(shards, axes)],
        scratch_shapes=[pltpu.SemaphoreType.DMA((3 * n,)), pltpu.SemaphoreType.DMA((3 * n,)),
                        pltpu.SemaphoreType.DMA((n,))],
    )(*shards)


C_CHUNKS = 4
XY_CHUNKS = 4
EW_ROWS = 512


def _c_exchange(g):
    _, n, w = g.shape
    n2 = n // 2
    rq = n2 // C_CHUNKS

    def body(g_ref, got_ref, send_sems, recv_sems):
        x, y, c = lax.axis_index("x"), lax.axis_index("y"), lax.axis_index("c")
        cps = []
        for s in range(4):
            for q in range(C_CHUNKS):
                k = s * C_CHUNKS + q
                cp = _remote(g_ref.at[s, pl.ds((1 - c) * n2 + q * rq, rq), :], got_ref.at[s, pl.ds(q * rq, rq), :],
                             send_sems.at[k], recv_sems.at[k], (x, y, 1 - c))
                cp.start()
                cps.append(cp)
        for cp in cps:
            cp.wait_recv()
        for cp in cps:
            cp.wait_send()

    return pl.pallas_call(
        body, name="grad_c_exchange", in_specs=[_hbm()], out_specs=_hbm(),
        out_shape=jax.ShapeDtypeStruct((4, n2, w), g.dtype),
        scratch_shapes=[pltpu.SemaphoreType.DMA((4 * C_CHUNKS,)), pltpu.SemaphoreType.DMA((4 * C_CHUNKS,))],
    )(g)


def _add_own_half(g, got, c_arr):
    _, n, w = g.shape
    n2 = n // 2
    nb = n2 // EW_ROWS

    def body(c_ref, a_ref, b_ref, o_ref):
        o_ref[...] = a_ref[...] + b_ref[...]

    grid_spec = pltpu.PrefetchScalarGridSpec(
        num_scalar_prefetch=1, grid=(4, nb),
        in_specs=[pl.BlockSpec((1, EW_ROWS, w), lambda s, i, c: (s, c[0] * nb + i, 0)),
                  pl.BlockSpec((1, EW_ROWS, w), lambda s, i, c: (s, i, 0))],
        out_specs=pl.BlockSpec((1, EW_ROWS, w), lambda s, i, c: (s, i, 0)))
    return pl.pallas_call(
        body, name="grad_add_halves", grid_spec=grid_spec, out_shape=jax.ShapeDtypeStruct((4, n2, w), g.dtype),
        compiler_params=_cparams(("arbitrary", "arbitrary")),
    )(c_arr, g, got)


def _xy_exchange(hs):
    _, n, w = hs.shape
    rq = n // XY_CHUNKS

    def body(h_ref, out_ref, send_sems, recv_sems, local_sems):
        x, y, c, peers = _xy_peers()
        me = 2 * x + y
        rows = [pl.ds(q * rq, rq) for q in range(XY_CHUNKS)]
        own = []
        for q in range(XY_CHUNKS):
            cp = pltpu.make_async_copy(h_ref.at[me, rows[q], :], out_ref.at[me, rows[q], :], local_sems.at[q])
            cp.start()
            own.append(cp)
        sends = []
        for k, (px, py) in enumerate(peers):
            for q in range(XY_CHUNKS):
                j = k * XY_CHUNKS + q
                cp = _remote(h_ref.at[2 * px + py, rows[q], :], out_ref.at[me, rows[q], :], send_sems.at[j],
                             recv_sems.at[j], (px, py, c))
                cp.start()
                sends.append(cp)
        for k, (px, py) in enumerate(peers):
            for q in range(XY_CHUNKS):
                j = k * XY_CHUNKS + q
                _remote(h_ref.at[me, rows[q], :], out_ref.at[2 * px + py, rows[q], :], send_sems.at[j],
                        recv_sems.at[j], (px, py, c)).wait_recv()
        for cp in sends:
            cp.wait_send()
        for cp in own:
            cp.wait()

    return pl.pallas_call(
        body, name="grad_xy_exchange", in_specs=[_hbm()], out_specs=_hbm(),
        out_shape=jax.ShapeDtypeStruct((4, n, w), hs.dtype),
        scratch_shapes=[pltpu.SemaphoreType.DMA((3 * XY_CHUNKS,)), pltpu.SemaphoreType.DMA((3 * XY_CHUNKS,)),
                        pltpu.SemaphoreType.DMA((XY_CHUNKS,))],
    )(hs)


def _sum4_into_half(r, c_arr):
    _, n2, w = r.shape
    nb = n2 // EW_ROWS

    def body(c_ref, r_ref, o_ref):
        o_ref[...] = ((r_ref[0] + r_ref[1]) + r_ref[2]) + r_ref[3]

    grid_spec = pltpu.PrefetchScalarGridSpec(
        num_scalar_prefetch=1, grid=(nb,),
        in_specs=[pl.BlockSpec((4, EW_ROWS, w), lambda i, c: (0, i, 0))],
        out_specs=pl.BlockSpec((EW_ROWS, w), lambda i, c: (c[0] * nb + i, 0)))
    return pl.pallas_call(
        body, name="grad_sum4", grid_spec=grid_spec, out_shape=jax.ShapeDtypeStruct((2 * n2, w), r.dtype),
        compiler_params=_cparams(("arbitrary",)),
    )(c_arr, r)


C_GATHER_CHUNKS = 8


def _c_allgather_halves(f):
    n, w = f.shape
    n2 = n // 2
    rq = n2 // C_GATHER_CHUNKS

    def body(f_ref, out_ref, send_sems, recv_sems):
        x, y, c = lax.axis_index("x"), lax.axis_index("y"), lax.axis_index("c")
        sends = []
        for q in range(C_GATHER_CHUNKS):
            rows = pl.ds(c * n2 + q * rq, rq)
            cp = _remote(f_ref.at[rows, :], out_ref.at[rows, :], send_sems.at[q], recv_sems.at[q], (x, y, 1 - c))
            cp.start()
            sends.append(cp)
        for q in range(C_GATHER_CHUNKS):
            rows = pl.ds((1 - c) * n2 + q * rq, rq)
            _remote(f_ref.at[rows, :], out_ref.at[rows, :], send_sems.at[q], recv_sems.at[q],
                    (x, y, 1 - c)).wait_recv()
        for cp in sends:
            cp.wait_send()

    return pl.pallas_call(
        body, name="grad_c_allgather", in_specs=[_hbm()], out_specs=_hbm(), input_output_aliases={0: 0},
        out_shape=jax.ShapeDtypeStruct((n, w), f.dtype),
        scratch_shapes=[pltpu.SemaphoreType.DMA((C_GATHER_CHUNKS,)), pltpu.SemaphoreType.DMA((C_GATHER_CHUNKS,))],
    )(f)


def _adamw(w, m, v, g, g_row0=None):
    shape = w.shape
    cols = shape[-1]
    rows = int(math.prod(shape)) // cols
    tr = 256 if rows % 256 == 0 else rows
    from_flat = g_row0 is not None
    c1 = 1.0 / (1.0 - ADAM_B1 ** ADAM_STEP)
    c2 = 1.0 / (1.0 - ADAM_B2 ** ADAM_STEP)

    def body(w_ref, m_ref, v_ref, g_ref, *outs):
        gg = g_ref[...]
        nm = ADAM_B1 * m_ref[...] + (1.0 - ADAM_B1) * gg
        nv = ADAM_B2 * v_ref[...] + (1.0 - ADAM_B2) * (gg * gg)
        if from_flat:
            outs[0][...] = gg
        d_ref, nm_ref, nv_ref = outs[-3:]
        nm_ref[...] = nm
        nv_ref[...] = nv
        d_ref[...] = -ADAM_LR * ((nm * c1) / (jnp.sqrt(nv * c2) + ADAM_EPS) + ADAM_WD * w_ref[...])

    spec = pl.BlockSpec((tr, cols), lambda i: (i, 0))
    if from_flat:
        assert cols == FLAT and g_row0 % tr == 0
        g_spec = pl.BlockSpec((tr, cols), lambda i: (g_row0 // tr + i, 0))
        g_arg = g
    else:
        g_spec = spec
        g_arg = g.reshape(rows, cols)
    n_out = 4 if from_flat else 3
    sds = jax.ShapeDtypeStruct((rows, cols), F32)
    outs = pl.pallas_call(
        body, name="adamw", grid=(rows // tr,), in_specs=[spec, spec, spec, g_spec], out_specs=[spec] * n_out,
        out_shape=[sds] * n_out, compiler_params=_cparams(("arbitrary",)),
    )(w.reshape(rows, cols), m.reshape(rows, cols), v.reshape(rows, cols), g_arg)
    outs = [o.reshape(shape) for o in outs]
    return outs if from_flat else [g] + outs


SMALL_SHARDED = (("s5_glu_w", (2, 64, 256), 1), ("ssd_conv_w", (2, 4, 256), 2), ("rg_conv_w", (2, 4, 64), 2))
CONV_SHARDED = SMALL_SHARDED[1:]
REPLICATED = (
    ("ssd_conv_b", (2, 1024)), ("ssd_dt_bias", (2, 8)), ("ssd_a_log", (2, 8)), ("ssd_d", (2, 8)),
    ("ssd_norm_w", (2, 512)), ("s5_lam_re", (2, 16, 64)), ("s5_lam_im", (2, 16, 64)), ("s5_log_step", (2, 16)),
    ("s5_b_re", (2, 16, 64, 16)), ("s5_b_im", (2, 16, 64, 16)), ("s5_c_re", (2, 16, 16, 64)),
    ("s5_c_im", (2, 16, 16, 64)), ("s5_d", (2, 256)), ("s5_glu_b", (2, 256)), ("rg_conv_b", (2, 256)),
    ("rg_wa", (2, 4, 64, 64)), ("rg_ba", (2, 4, 64)), ("rg_wx", (2, 4, 64, 64)), ("rg_bx", (2, 4, 64)),
    ("rg_lambda", (2, 256)), ("ln1_g", (2, 1024)), ("ln1_b", (2, 1024)), ("ln2_g", (2, 1024)), ("ln2_b", (2, 1024)),
    ("ln3_g", (2, 1024)), ("ln3_b", (2, 1024)),
)
WEIGHT_ORDER = (
    "w_in", "w_out", "ssd_conv_w", "ssd_conv_b", "ssd_dt_bias", "ssd_a_log", "ssd_d", "ssd_norm_w", "s5_lam_re",
    "s5_lam_im", "s5_log_step", "s5_b_re", "s5_b_im", "s5_c_re", "s5_c_im", "s5_d", "s5_glu_w", "s5_glu_b",
    "rg_conv_w", "rg_conv_b", "rg_wa", "rg_ba", "rg_wx", "rg_bx", "rg_lambda", "ln1_g", "ln1_b", "xa_wq", "xa_wk",
    "xa_wv", "xa_wo", "ln2_g", "ln2_b", "mlp_w1", "mlp_w2", "ln3_g", "ln3_b",
)


def _size(shape):
    return int(math.prod(shape))


def _pad_rows(flat, rows):
    return jnp.pad(flat, (0, rows * FLAT - flat.shape[0])).reshape(rows, FLAT)


def _round_up(a, b):
    return (a + b - 1) // b * b


SMALL_ELEMS = sum(_size(s) for _, s, _ in SMALL_SHARDED)
REP_ELEMS = sum(_size(s) for _, s in REPLICATED)
REP_QROWS = _round_up(-(-REP_ELEMS // (4 * FLAT)), 8)
assert SMALL_ELEMS <= MISC_REP_ROW * FLAT and MISC_REP_ROW + REP_QROWS <= MISC_ROWS
CONV_ROWS = 8


def _pack_shards(tensors, names_shapes):
    return jnp.concatenate([tensors[n].reshape(-1) for n, *_ in names_shapes])


def _unpack(flat, names_shapes):
    out, off = {}, 0
    for n, s, *_ in names_shapes:
        out[n] = flat[off:off + _size(s)].reshape(s)
        off += _size(s)
    return out


def _gather_full(gathered, names_shapes):
    flat = gathered.reshape(4, -1)
    out, off = {}, 0
    for n, s, ax in names_shapes:
        parts = flat[:, off:off + _size(s)].reshape((4,) + s)
        out[n] = jnp.concatenate([parts[k] for k in range(4)], axis=ax)
        off += _size(s)
    return out


def _split_shards(full, names_shapes):
    rows = []
    for k in range(4):
        parts = []
        for n, s, ax in names_shapes:
            w = s[ax]
            parts.append(lax.slice_in_dim(full[n], k * w, (k + 1) * w, axis=ax).reshape(-1))
        rows.append(jnp.concatenate(parts))
    return jnp.stack(rows)


def _pack_cols(w):
    pad = jnp.zeros((w.shape[0], LANES - SSD_HEADS), w.dtype)
    return jnp.concatenate([w[:, O_XBC:O_XBC + 1024], w[:, O_Z:O_Z + 512], w[:, O_U:O_U + 256],
                            w[:, O_XRG:O_XRG + 256], w[:, O_GRG:O_GRG + 256], w[:, O_DT:O_DT + 8], pad], axis=1)


def _unpack_cols(w):
    return jnp.concatenate([w[:, P_Z:P_Z + 512], w[:, P_XBC:P_XBC + 1024], w[:, P_DT:P_DT + 8],
                            w[:, P_U:P_U + 256], w[:, P_XRG:P_XRG + 256], w[:, P_GRG:P_GRG + 256]], axis=1)


def _lanes(v, width):
    return jnp.pad(v, (0, width - v.shape[0])).reshape(1, width)


def _layer_params(full, rep, l):
    p = {}
    p["w_in"] = _pack_cols(full["w_in"][l])
    for n in ("w_out", "xa_wq", "xa_wk", "xa_wv", "xa_wo", "mlp_w1", "mlp_w2", "s5_glu_w"):
        p[n] = full[n][l]
    p["ssd_cw"] = full["ssd_conv_w"][l]
    p["ssd_cb"] = rep["ssd_conv_b"][l].reshape(1, -1)
    p["ssd_dtb"] = _lanes(rep["ssd_dt_bias"][l], LANES)
    p["ssd_a"] = _lanes(-jnp.exp(rep["ssd_a_log"][l]), LANES)
    p["ssd_d"] = jnp.repeat(rep["ssd_d"][l], 64).reshape(1, -1)
    p["ssd_nw"] = rep["ssd_norm_w"][l].reshape(1, -1)
    s5_args = tuple(rep[n][l] for n in ("s5_lam_re", "s5_lam_im", "s5_log_step", "s5_b_re", "s5_b_im", "s5_c_re",
                                        "s5_c_im"))
    (lbr, lbi, bre, bim, cre, cim), p["s5_vjp"] = jax.vjp(_s5_prepare, *s5_args)
    p.update(s5_bre=bre, s5_bim=bim, s5_cre=cre, s5_cim=cim)
    p["s5_coef"] = _s5_scan_coef(lbr, lbi, False)
    p["s5_rcoef"] = _s5_scan_coef(lbr, lbi, True)
    p["s5_d"] = rep["s5_d"][l].reshape(1, -1)
    p["s5_gb"] = rep["s5_glu_b"][l].reshape(1, -1)
    p["rg_cw"] = full["rg_conv_w"][l]
    p["rg_cb"] = rep["rg_conv_b"][l].reshape(1, -1)
    p["rg_wa"] = _block_diag(rep["rg_wa"][l])
    p["rg_wx"] = _block_diag(rep["rg_wx"][l])
    p["rg_ba"] = rep["rg_ba"][l].reshape(1, -1)
    p["rg_bx"] = rep["rg_bx"][l].reshape(1, -1)
    p["rg_nsp"] = (-RG_C * jax.nn.softplus(-rep["rg_lambda"][l])).reshape(1, -1)
    p["rg_dnsp"] = RG_C * jax.nn.sigmoid(-rep["rg_lambda"][l])
    for n in ("ln1_g", "ln1_b", "ln2_g", "ln2_b", "ln3_g", "ln3_b"):
        p[n] = rep[n][l].reshape(1, -1)
    return p


def _layer_fwd(h, mem, p):
    s = {"h0": h}
    proj = _mm(h, p["w_in"], name="in_proj")
    s["proj"] = proj
    y_ssd, s["ssd_yy"], s["ssd_states"] = _ssd_fwd(proj, p["ssd_cw"], p["ssd_cb"], p["ssd_dtb"], p["ssd_a"],
                                                     p["ssd_d"], p["ssd_nw"])
    y_s5, s["s5_y2"], s["s5_hre"], s["s5_him"] = _s5_fwd(proj, p["s5_bre"], p["s5_bim"], p["s5_cre"], p["s5_cim"],
                                                         p["s5_d"], p["s5_glu_w"], p["s5_gb"], p["s5_coef"])
    y_rg, s["rg_h"] = _rg_fwd(proj, p["rg_cw"], p["rg_cb"], p["rg_wa"], p["rg_ba"], p["rg_wx"], p["rg_bx"],
                              p["rg_nsp"])
    ycat = jnp.concatenate([y_ssd, y_s5, y_rg], axis=1)
    s["ycat"] = ycat
    h1, s["xh1"], s["rs1"] = _outproj_ln_fwd(ycat, h, p["w_out"], p["ln1_g"], p["ln1_b"])
    s["h1"] = h1
    kb = _mm(mem, p["xa_wk"], name="mem_proj")
    vb = _mm(mem, p["xa_wv"], name="mem_proj")
    s["kb"], s["vb"] = kb, vb
    h2, s["xh2"], s["rs2"], s["attn_o"] = _attn_ln_fwd(h1, p["xa_wq"], p["xa_wo"], kb, vb, p["ln2_g"], p["ln2_b"])
    s["h2"] = h2
    h3, s["xh3"], s["rs3"], s["mlp_u"] = _mlp_ln_fwd(h2, p["mlp_w1"], p["mlp_w2"], p["ln3_g"], p["ln3_b"])
    return h3, s


def _layer_bwd(dh3, mem, p, s, l, gbuf):
    g = {}
    dr3, du, hdn, dh2, g["ln3_g"], g["ln3_b"] = _mlp_ln_bwd(dh3, s["xh3"], s["rs3"], p["ln3_g"], s["mlp_u"],
                                                             p["mlp_w1"], p["mlp_w2"])
    gbuf = _wgrad_flat(s["h2"], du, gbuf, mode="colblk", row_off=ROW_MLP_W1 + 1024 * l, name="wgrad_mlp_w1")
    gbuf = _wgrad_flat(hdn, dr3, gbuf, mode="rowblk", row_off=ROW_MLP_W2 + 1024 * l, name="wgrad_mlp_w2")
    dr2, dq, dh1, dkb, dvb, g["ln2_g"], g["ln2_b"] = _attn_ln_bwd(dh2, s["xh2"], s["rs2"], p["ln2_g"], s["h1"],
                                                                   p["xa_wq"], p["xa_wo"], s["kb"], s["vb"])
    for n, a_op, g_op in (("xa_wo", s["attn_o"], dr2), ("xa_wq", s["h1"], dq), ("xa_wk", mem, dkb),
                          ("xa_wv", mem, dvb)):
        gbuf = _wgrad_flat(a_op, g_op, gbuf, mode="rows4", row_off=ROW_XA[n] + 256 * l, name="wgrad_" + n)
    dr1, dres, dycat, g["ln1_g"], g["ln1_b"] = _outproj_ln_bwd(dh1, s["xh1"], s["rs1"], p["ln1_g"], p["w_out"])
    gbuf = _wgrad_flat(s["ycat"], dr1, gbuf, mode="rows4", row_off=ROW_W_OUT + 256 * l, name="wgrad_w_out")
    proj = s["proj"]
    (dxbc, dz, ddt, dcw, dcb, ddtb, da_neg, dd_l, dnw) = _ssd_bwd(
        dycat, proj, s["ssd_yy"], s["ssd_states"], p["ssd_cw"], p["ssd_cb"], p["ssd_dtb"], p["ssd_a"], p["ssd_d"],
        p["ssd_nw"])
    g["ssd_conv_w"] = dcw[0:4]
    g["ssd_conv_b"] = dcb[0]
    g["ssd_dt_bias"] = ddtb[0, :SSD_HEADS]
    g["ssd_a_log"] = da_neg[0, :SSD_HEADS] * p["ssd_a"][0, :SSD_HEADS]
    g["ssd_d"] = dd_l.reshape(SSD_HEADS, 64).sum(axis=1)
    g["ssd_norm_w"] = dnw[0]
    (du_s5, dbre, dbim, dcre, dcim, dlam, dd5, dgw, dgb) = _s5_bwd(
        dycat, proj, s["s5_y2"], s["s5_hre"], s["s5_him"], p["s5_bre"], p["s5_bim"], p["s5_cre"], p["s5_cim"],
        p["s5_d"], p["s5_glu_w"], p["s5_gb"], p["s5_rcoef"])
    dl = dlam.sum(axis=1)
    s5g = p["s5_vjp"]((dl[0], dl[1], dbre, dbim, dcre, dcim))
    for n, v in zip(("s5_lam_re", "s5_lam_im", "s5_log_step", "s5_b_re", "s5_b_im", "s5_c_re", "s5_c_im"), s5g):
        g[n] = v
    g["s5_d"] = dd5[0]
    g["s5_glu_w"] = dgw
    g["s5_glu_b"] = dgb[0]
    (dxrg, dgrg, drcw, drcb, dwa, dba, dwx, dbx, dnsp) = _rg_bwd(
        dycat, proj, s["rg_h"], p["rg_cw"], p["rg_cb"], p["rg_wa"], p["rg_ba"], p["rg_wx"], p["rg_bx"], p["rg_nsp"])
    g["rg_conv_w"] = drcw[0:4]
    g["rg_conv_b"] = drcb[0]
    g["rg_wa"] = _block_diag_extract(dwa, RG_BLOCKS)
    g["rg_wx"] = _block_diag_extract(dwx, RG_BLOCKS)
    g["rg_ba"] = dba.reshape(RG_BLOCKS, RG_BLOCK_DIM)
    g["rg_bx"] = dbx.reshape(RG_BLOCKS, RG_BLOCK_DIM)
    g["rg_lambda"] = dnsp[0] * p["rg_dnsp"]
    dproj = jnp.concatenate([dxbc, dz, du_s5, dxrg, dgrg, ddt], axis=1)
    g["w_in"] = _unpack_cols(_mm_tn(s["h0"], dproj, name="wgrad_in"))
    dh0 = _mm(dproj, p["w_in"], nt=True, add=dres, name="in_proj_bwd")
    for n in ("ln1_g", "ln1_b", "ln2_g", "ln2_b", "ln3_g", "ln3_b"):
        g[n] = g[n][0]
    return dh0, g, gbuf


def _local_step(h, memf, target, full, rep):
    params, saved = [], []
    for l in range(DEPTH):
        p = _layer_params(full, rep, l)
        params.append(p)
        h, s = _layer_fwd(h, memf, p)
        saved.append(s)
    loss11, dh = _loss_fwd_bwd(h, target)
    grads = [None] * DEPTH
    gbuf = None
    for l in reversed(range(DEPTH)):
        dh, grads[l], gbuf = _layer_bwd(dh, memf, params[l], saved[l], l, gbuf)
    return loss11, dh, {n: jnp.stack([grads[l][n] for l in range(DEPTH)]) for n in grads[0]}, gbuf


def kernel(x, mem, w_in, w_out, ssd_conv_w, ssd_conv_b, ssd_dt_bias, ssd_a_log, ssd_d, ssd_norm_w, s5_lam_re, s5_lam_im, s5_log_step, s5_b_re, s5_b_im, s5_c_re, s5_c_im, s5_d, s5_glu_w, s5_glu_b, rg_conv_w, rg_conv_b, rg_wa, rg_ba, rg_wx, rg_bx, rg_lambda, ln1_g, ln1_b, xa_wq, xa_wk, xa_wv, xa_wo, ln2_g, ln2_b, mlp_w1, mlp_w2, ln3_g, ln3_b, loss_target, m_w_in, m_w_out, m_ssd_conv_w, m_ssd_conv_b, m_ssd_dt_bias, m_ssd_a_log, m_ssd_d, m_ssd_norm_w, m_s5_lam_re, m_s5_lam_im, m_s5_log_step, m_s5_b_re, m_s5_b_im, m_s5_c_re, m_s5_c_im, m_s5_d, m_s5_glu_w, m_s5_glu_b, m_rg_conv_w, m_rg_conv_b, m_rg_wa, m_rg_ba, m_rg_wx, m_rg_bx, m_rg_lambda, m_ln1_g, m_ln1_b, m_xa_wq, m_xa_wk, m_xa_wv, m_xa_wo, m_ln2_g, m_ln2_b, m_mlp_w1, m_mlp_w2, m_ln3_g, m_ln3_b, v_w_in, v_w_out, v_ssd_conv_w, v_ssd_conv_b, v_ssd_dt_bias, v_ssd_a_log, v_ssd_d, v_ssd_norm_w, v_s5_lam_re, v_s5_lam_im, v_s5_log_step, v_s5_b_re, v_s5_b_im, v_s5_c_re, v_s5_c_im, v_s5_d, v_s5_glu_w, v_s5_glu_b, v_rg_conv_w, v_rg_conv_b, v_rg_wa, v_rg_ba, v_rg_wx, v_rg_bx, v_rg_lambda, v_ln1_g, v_ln1_b, v_xa_wq, v_xa_wk, v_xa_wv, v_xa_wo, v_ln2_g, v_ln2_b, v_mlp_w1, v_mlp_w2, v_ln3_g, v_ln3_b):
    args = dict(locals())
    weights = {n: args[n] for n in WEIGHT_ORDER}
    mom_m = {n: args["m_" + n] for n in WEIGHT_ORDER}
    mom_v = {n: args["v_" + n] for n in WEIGHT_ORDER}

    w_in_pad = jnp.pad(w_in, ((0, 0), (0, 0), (0, W_IN_PAD - W_IN_SHARD)))
    shards = [w_in_pad.astype(MXU_DTYPE)] + [weights[n].astype(MXU_DTYPE) for n, _, _ in GATHERED[1:]]
    full = dict(zip([n for n, _, _ in GATHERED], _weights_allgather(shards)))
    full["w_in"] = jnp.concatenate(
        [full["w_in"][:, :, W_IN_PAD * k:W_IN_PAD * k + W_IN_SHARD] for k in range(4)], axis=2)
    conv_flat = _pad_rows(_pack_shards(weights, CONV_SHARDED), CONV_ROWS)
    full.update(_gather_full(_xy_allgather(conv_flat, name="conv_weights_allgather"), CONV_SHARDED))
    rep = {n: weights[n] for n, _ in REPLICATED}

    loss11, dx, gsmall, gbuf = _local_step(x[0], mem[0], loss_target[0], full, rep)
    grad_x = dx[None]
    loss = lax.psum(loss11[0, 0], ("x", "y", "c"))

    gw = gsmall["w_in"].reshape(DEPTH, D_MODEL, 4, W_IN_SHARD)
    gw = jnp.pad(gw, ((0, 0), (0, 0), (0, 0), (0, W_IN_PAD - W_IN_SHARD)))
    w_in_blk = jnp.transpose(gw, (2, 0, 1, 3)).reshape(4, DEPTH * W_IN_PAD, FLAT)
    small_q = _split_shards(gsmall, SMALL_SHARDED)
    rep_q = jnp.pad(_pack_shards(gsmall, REPLICATED), (0, 4 * REP_QROWS * FLAT - REP_ELEMS)).reshape(4, -1)
    misc = jnp.concatenate(
        [jnp.pad(small_q, ((0, 0), (0, MISC_REP_ROW * FLAT - SMALL_ELEMS))), rep_q,
         jnp.zeros((4, (MISC_ROWS - MISC_REP_ROW - REP_QROWS) * FLAT), F32)], axis=1).reshape(4, MISC_ROWS, FLAT)
    gbuf = lax.dynamic_update_slice(gbuf, w_in_blk, (0, ROW_W_IN, 0))
    gbuf = lax.dynamic_update_slice(gbuf, misc, (0, ROW_MISC, 0))
    c_arr = lax.axis_index("c").astype(jnp.int32).reshape(1)
    chip_sum = _add_own_half(gbuf, _c_exchange(gbuf), c_arr)
    reduced = _c_allgather_halves(_sum4_into_half(_xy_exchange(chip_sum), c_arr))
    misc_red = reduced[ROW_MISC:]
    rep_all = _xy_allgather(misc_red[MISC_REP_ROW:MISC_REP_ROW + REP_QROWS], name="small_grads_allgather")
    g_red = {**_unpack(misc_red[:MISC_REP_ROW].reshape(-1), SMALL_SHARDED),
             **_unpack(rep_all.reshape(-1), REPLICATED)}
    g_red["w_in"] = reduced[ROW_W_IN:ROW_W_IN + DEPTH * W_IN_PAD].reshape(DEPTH, D_MODEL, W_IN_PAD)[:, :, :W_IN_SHARD]

    flat_rows = {"mlp_w1": ROW_MLP_W1, "mlp_w2": ROW_MLP_W2, "w_out": ROW_W_OUT, **ROW_XA}
    res = {}
    for n in WEIGHT_ORDER:
        if n in flat_rows:
            res[n] = _adamw(weights[n], mom_m[n], mom_v[n], reduced, g_row0=flat_rows[n])
        else:
            res[n] = _adamw(weights[n], mom_m[n], mom_v[n], g_red[n])
    return (loss, grad_x, *[res[n][0] for n in WEIGHT_ORDER], *[res[n][1] for n in WEIGHT_ORDER],
            *[res[n][2] for n in WEIGHT_ORDER], *[res[n][3] for n in WEIGHT_ORDER])
```

```python
import functools
import math

import jax
import jax.numpy as jnp
from jax import lax
from jax.experimental import pallas as pl
from jax.experimental.pallas import tpu as pltpu

F32 = jnp.float32
MXU_DTYPE = jnp.bfloat16

D_MODEL = 1024
DEPTH = 2
MEM_LEN = 256
SSD_WIDTH = 512
SSD_HEADS = 8
SSD_STATE = 128
SSD_CHUNK = 128
SSD_XBC = 1024
S5_WIDTH = 256
S5_GROUPS = 16
S5_GROUP_CH = 16
S5_STATE = 64
S5_NSTATE = S5_GROUPS * S5_STATE
RG_WIDTH = 256
RG_BLOCKS = 4
RG_BLOCK_DIM = 64
RG_C = 8.0
XA_HEADS = 4
XA_HEAD_DIM = 256
D_FF = 4096
D_IN = 2312
ALPHA = (2.0 * DEPTH) ** 0.25
LN_EPS = 1e-5
ADAM_LR = 0.001
ADAM_B1 = 0.9
ADAM_B2 = 0.999
ADAM_EPS = 1e-08
ADAM_WD = 0.01
ADAM_STEP = 10

P_XBC, P_Z, P_U, P_XRG, P_GRG, P_DT = 0, 1024, 1536, 1792, 2048, 2304
D_PACK = 2432
O_Z, O_XBC, O_DT, O_U, O_XRG, O_GRG = 0, 512, 1536, 1544, 1800, 2056

LANES = 128
SUBLANES = 8
VMEM_LIMIT = 52 * 1024 * 1024
TM = 512
SSD_TM = 256
SCAN_TM = 512
FLAT = 1024

MESH = pl.DeviceIdType.MESH


def _cparams(sem):
    return pltpu.CompilerParams(dimension_semantics=sem, vmem_limit_bytes=VMEM_LIMIT)


def _dot(a, b):
    return jnp.dot(a.astype(MXU_DTYPE), b.astype(MXU_DTYPE), preferred_element_type=F32)


def _dot_nt(a, b):
    return lax.dot_general(a.astype(MXU_DTYPE), b.astype(MXU_DTYPE), (((1,), (1,)), ((), ())),
                           preferred_element_type=F32)


def _dot_tn(a, b):
    return lax.dot_general(a.astype(MXU_DTYPE), b.astype(MXU_DTYPE), (((0,), (0,)), ((), ())),
                           preferred_element_type=F32)


def _dot_f32(a, b):
    return jnp.dot(a, b, precision=lax.Precision.HIGHEST, preferred_element_type=F32)


def _dot_f32_tn(a, b):
    return lax.dot_general(a, b, (((0,), (0,)), ((), ())), precision=lax.Precision.HIGHEST,
                           preferred_element_type=F32)


def _sigmoid(x):
    return 1.0 / (1.0 + jnp.exp(-x))


def _softplus(x):
    return jnp.maximum(x, 0.0) + jnp.log(1.0 + jnp.exp(-jnp.abs(x)))


_GELU_K = math.sqrt(2.0 / math.pi)


def _gelu(x):
    return 0.5 * x * (1.0 + jnp.tanh(_GELU_K * (x + 0.044715 * x * x * x)))


def _gelu_grad(x):
    t = jnp.tanh(_GELU_K * (x + 0.044715 * x * x * x))
    return 0.5 * (1.0 + t) + 0.5 * x * (1.0 - t * t) * _GELU_K * (1.0 + 3.0 * 0.044715 * x * x)


def _expm1(x):
    small = x * (1.0 + x * (0.5 + x * (1.0 / 6.0 + x * (1.0 / 24.0))))
    return jnp.where(jnp.abs(x) < 0.05, small, jnp.exp(x) - 1.0)


def _sum0(x):
    return jnp.sum(x, axis=0, keepdims=True)


def _ln_fwd(r, g, b):
    mu = jnp.mean(r, axis=-1, keepdims=True)
    xc = r - mu
    var = jnp.mean(xc * xc, axis=-1, keepdims=True)
    rstd = lax.rsqrt(var + LN_EPS)
    xhat = xc * rstd
    return xhat * g + b, xhat, rstd


def _ln_bwd(dout, xhat, rstd, g):
    dxh = dout * g
    m1 = jnp.mean(dxh, axis=-1, keepdims=True)
    m2 = jnp.mean(dxh * xhat, axis=-1, keepdims=True)
    return rstd * (dxh - m1 - xhat * m2)


def _rows(tm, n, col=0):
    return pl.BlockSpec((tm, n), lambda i: (i, col))


def _const(shape):
    nd = len(shape)
    return pl.BlockSpec(shape, lambda i: (0,) * nd)


def _mm(a, w, *, nt=False, add=None, out_dtype=F32, name):
    t, k = a.shape
    n = w.shape[0] if nt else w.shape[1]
    tm = min(TM, t)

    def body(*refs):
        if add is None:
            a_ref, w_ref, o_ref = refs
        else:
            a_ref, w_ref, add_ref, o_ref = refs
        r = _dot_nt(a_ref[...], w_ref[...]) if nt else _dot(a_ref[...], w_ref[...])
        if add is not None:
            r = r + add_ref[...]
        o_ref[...] = r.astype(out_dtype)

    in_specs = [_rows(tm, k), _const(w.shape)]
    args = [a, w]
    if add is not None:
        in_specs.append(_rows(tm, n))
        args.append(add)
    return pl.pallas_call(
        body, name=name, grid=(t // tm,), in_specs=in_specs, out_specs=_rows(tm, n),
        out_shape=jax.ShapeDtypeStruct((t, n), out_dtype), compiler_params=_cparams(("arbitrary",)),
    )(*args)


def _mm_tn(a, g, *, name):
    t, k = a.shape
    n = g.shape[1]
    tt = min(512, t)
    tk = min(1024, k)
    tn = 1024 if n % 1024 == 0 else n
    nsteps = t // tt

    def body(a_ref, g_ref, o_ref):
        s = pl.program_id(2)
        part = _dot_tn(a_ref[...], g_ref[...])

        @pl.when(s == 0)
        def _():
            o_ref[...] = part

        @pl.when(s > 0)
        def _():
            o_ref[...] += part

    return pl.pallas_call(
        body, name=name, grid=(k // tk, n // tn, nsteps),
        in_specs=[pl.BlockSpec((tt, tk), lambda i, j, s: (s, i)), pl.BlockSpec((tt, tn), lambda i, j, s: (s, j))],
        out_specs=pl.BlockSpec((tk, tn), lambda i, j, s: (i, j)),
        out_shape=jax.ShapeDtypeStruct((k, n), F32),
        compiler_params=_cparams(("arbitrary", "arbitrary", "arbitrary")),
    )(a, g)


G_ROWS = 8192
ROW_MLP_W1 = 0
ROW_MLP_W2 = 2048
ROW_W_IN = 4096
ROW_W_OUT = 5376
ROW_XA = {"xa_wq": 5888, "xa_wk": 6400, "xa_wv": 6912, "xa_wo": 7424}
ROW_MISC = 7936
MISC_ROWS = G_ROWS - ROW_MISC
MISC_REP_ROW = 40
W_IN_SHARD = 578
W_IN_PAD = 640


def _wgrad_flat(a, g, buf, *, mode, row_off, name):
    t = a.shape[0]
    tt = min(1024, t)
    ns = t // tt
    blk = D_MODEL

    def accumulate(o_ref, part, s):
        @pl.when(s == 0)
        def _():
            if mode == "rows4":
                for q in range(4):
                    o_ref[q] = part[q * 256:(q + 1) * 256]
            else:
                o_ref[0] = part

        @pl.when(s > 0)
        def _():
            if mode == "rows4":
                for q in range(4):
                    o_ref[q] += part[q * 256:(q + 1) * 256]
            else:
                o_ref[0] += part

    if mode == "rows4":
        grid = (ns,)
        in_specs = [pl.BlockSpec((tt, blk), lambda s: (s, 0)), pl.BlockSpec((tt, blk), lambda s: (s, 0))]
        out_spec = pl.BlockSpec((4, 256, FLAT), lambda s: (0, row_off // 256, 0))
        sem = ("arbitrary",)

        def body(a_ref, g_ref, *rest):
            accumulate(rest[-1], _dot_tn(a_ref[...], g_ref[...]), pl.program_id(0))
    else:
        grid = (4, ns)
        if mode == "rowblk":
            in_specs = [pl.BlockSpec((tt, blk), lambda q, s: (s, q)), pl.BlockSpec((tt, blk), lambda q, s: (s, 0))]
        else:
            in_specs = [pl.BlockSpec((tt, blk), lambda q, s: (s, 0)), pl.BlockSpec((tt, blk), lambda q, s: (s, q))]
        out_spec = pl.BlockSpec((1, blk, FLAT), lambda q, s: (q, row_off // blk, 0))
        sem = ("arbitrary", "arbitrary")

        def body(a_ref, g_ref, *rest):
            accumulate(rest[-1], _dot_tn(a_ref[...], g_ref[...]), pl.program_id(1))

    args = [a, g]
    aliases = {}
    if buf is not None:
        in_specs.append(pl.BlockSpec(memory_space=pl.ANY))
        args.append(buf)
        aliases = {2: 0}
    return pl.pallas_call(
        body, name=name, grid=grid, in_specs=in_specs, out_specs=out_spec,
        out_shape=jax.ShapeDtypeStruct((4, G_ROWS, FLAT), F32), input_output_aliases=aliases,
        compiler_params=_cparams(sem),
    )(*args)


def _outproj_ln_fwd(ycat, h, w, g, b):
    t = h.shape[0]

    def body(y_ref, h_ref, w_ref, g_ref, b_ref, hn_ref, xh_ref, rs_ref):
        r = ALPHA * h_ref[...] + _dot(y_ref[...], w_ref[...])
        out, xhat, rstd = _ln_fwd(r, g_ref[...], b_ref[...])
        hn_ref[...] = out
        xh_ref[...] = xhat
        rs_ref[...] = rstd

    return pl.pallas_call(
        body, name="outproj_ln_fwd", grid=(t // TM,),
        in_specs=[_rows(TM, D_MODEL), _rows(TM, D_MODEL), _const((D_MODEL, D_MODEL)), _const((1, D_MODEL)),
                  _const((1, D_MODEL))],
        out_specs=[_rows(TM, D_MODEL), _rows(TM, D_MODEL), _rows(TM, 1)],
        out_shape=[jax.ShapeDtypeStruct((t, D_MODEL), F32), jax.ShapeDtypeStruct((t, D_MODEL), F32),
                   jax.ShapeDtypeStruct((t, 1), F32)],
        compiler_params=_cparams(("arbitrary",)),
    )(ycat, h, w, g, b)


def _attn_probs(q, kb, hh):
    sl = slice(hh * XA_HEAD_DIM, (hh + 1) * XA_HEAD_DIM)
    s = _dot_nt(q[:, sl], kb[:, sl]) * (1.0 / math.sqrt(XA_HEAD_DIM))
    m = jnp.max(s, axis=-1, keepdims=True)
    e = jnp.exp(s - m)
    return e / jnp.sum(e, axis=-1, keepdims=True)


def _attn_ln_fwd(h1, wq, wo, kb, vb, g, b):
    t = h1.shape[0]

    def body(h_ref, wq_ref, wo_ref, k_ref, v_ref, g_ref, b_ref, hn_ref, xh_ref, rs_ref, o_ref):
        h = h_ref[...]
        q = _dot(h, wq_ref[...])
        kb_ = k_ref[...]
        vb_ = v_ref[...]
        for hh in range(XA_HEADS):
            sl = slice(hh * XA_HEAD_DIM, (hh + 1) * XA_HEAD_DIM)
            p = _attn_probs(q, kb_, hh)
            o_ref[:, sl] = _dot(p, vb_[:, sl]).astype(o_ref.dtype)
        r = ALPHA * h + _dot(o_ref[...], wo_ref[...])
        out, xhat, rstd = _ln_fwd(r, g_ref[...], b_ref[...])
        hn_ref[...] = out
        xh_ref[...] = xhat
        rs_ref[...] = rstd

    return pl.pallas_call(
        body, name="attn_ln_fwd", grid=(t // TM,),
        in_specs=[_rows(TM, D_MODEL), _const((D_MODEL, D_MODEL)), _const((D_MODEL, D_MODEL)),
                  _const((MEM_LEN, D_MODEL)), _const((MEM_LEN, D_MODEL)), _const((1, D_MODEL)), _const((1, D_MODEL))],
        out_specs=[_rows(TM, D_MODEL), _rows(TM, D_MODEL), _rows(TM, 1), _rows(TM, D_MODEL)],
        out_shape=[jax.ShapeDtypeStruct((t, D_MODEL), F32), jax.ShapeDtypeStruct((t, D_MODEL), F32),
                   jax.ShapeDtypeStruct((t, 1), F32), jax.ShapeDtypeStruct((t, D_MODEL), MXU_DTYPE)],
        compiler_params=_cparams(("arbitrary",)),
    )(h1, wq, wo, kb, vb, g, b)


def _attn_ln_bwd(dh2, xhat, rstd, g, h1, wq, wo, kb, vb):
    t = h1.shape[0]

    def body(dh_ref, xh_ref, rs_ref, g_ref, h_ref, wq_ref, wo_ref, k_ref, v_ref,
             dr_ref, dq_ref, dh1_ref, dk_ref, dv_ref, dg_ref, db_ref):
        i = pl.program_id(0)

        @pl.when(i == 0)
        def _():
            dk_ref[...] = jnp.zeros_like(dk_ref)
            dv_ref[...] = jnp.zeros_like(dv_ref)
            dg_ref[...] = jnp.zeros_like(dg_ref)
            db_ref[...] = jnp.zeros_like(db_ref)

        dout = dh_ref[...]
        xh = xh_ref[...]
        dg_ref[...] += _sum0(dout * xh)
        db_ref[...] += _sum0(dout)
        dr = _ln_bwd(dout, xh, rs_ref[...], g_ref[...])
        dr_ref[...] = dr.astype(dr_ref.dtype)
        do = _dot_nt(dr, wo_ref[...])
        h = h_ref[...]
        q = _dot(h, wq_ref[...])
        kb_ = k_ref[...]
        vb_ = v_ref[...]
        scale = 1.0 / math.sqrt(XA_HEAD_DIM)
        for hh in range(XA_HEADS):
            sl = slice(hh * XA_HEAD_DIM, (hh + 1) * XA_HEAD_DIM)
            p = _attn_probs(q, kb_, hh)
            do_h = do[:, sl]
            dp = _dot_nt(do_h, vb_[:, sl])
            ds = p * (dp - jnp.sum(dp * p, axis=-1, keepdims=True)) * scale
            dq_ref[:, sl] = _dot(ds, kb_[:, sl]).astype(dq_ref.dtype)
            dk_ref[:, sl] += _dot_tn(ds, q[:, sl])
            dv_ref[:, sl] += _dot_tn(p, do_h)
        dh1_ref[...] = ALPHA * dr + _dot_nt(dq_ref[...], wq_ref[...])

    return pl.pallas_call(
        body, name="attn_ln_bwd", grid=(t // TM,),
        in_specs=[_rows(TM, D_MODEL), _rows(TM, D_MODEL), _rows(TM, 1), _const((1, D_MODEL)), _rows(TM, D_MODEL),
                  _const((D_MODEL, D_MODEL)), _const((D_MODEL, D_MODEL)), _const((MEM_LEN, D_MODEL)),
                  _const((MEM_LEN, D_MODEL))],
        out_specs=[_rows(TM, D_MODEL), _rows(TM, D_MODEL), _rows(TM, D_MODEL), _const((MEM_LEN, D_MODEL)),
                   _const((MEM_LEN, D_MODEL)), _const((1, D_MODEL)), _const((1, D_MODEL))],
        out_shape=[jax.ShapeDtypeStruct((t, D_MODEL), MXU_DTYPE), jax.ShapeDtypeStruct((t, D_MODEL), MXU_DTYPE),
                   jax.ShapeDtypeStruct((t, D_MODEL), F32), jax.ShapeDtypeStruct((MEM_LEN, D_MODEL), F32),
                   jax.ShapeDtypeStruct((MEM_LEN, D_MODEL), F32), jax.ShapeDtypeStruct((1, D_MODEL), F32),
                   jax.ShapeDtypeStruct((1, D_MODEL), F32)],
        compiler_params=_cparams(("arbitrary",)),
    )(dh2, xhat, rstd, g, h1, wq, wo, kb, vb)


FF_CHUNK = 1024
N_FF = D_FF // FF_CHUNK


def _load_resident(pairs, sems):
    copies = [pltpu.make_async_copy(src, dst, sems.at[k]) for k, (src, dst) in enumerate(pairs)]
    for cp in copies:
        cp.start()
    for cp in copies:
        cp.wait()


def _mlp_ln_fwd(h2, w1, w2, g, b):
    t = h2.shape[0]

    def body(h_ref, w1_hbm, w2_hbm, g_ref, b_ref, hn_ref, xh_ref, rs_ref, hd_ref, w1_v, w2_v, acc_ref, sems):
        @pl.when(pl.program_id(0) == 0)
        def _():
            _load_resident([(w1_hbm, w1_v), (w2_hbm, w2_v)], sems)

        h = h_ref[...]
        hb = h.astype(MXU_DTYPE)
        acc_ref[...] = ALPHA * h
        for j in range(N_FF):
            sl = slice(j * FF_CHUNK, (j + 1) * FF_CHUNK)
            u = _dot(hb, w1_v[:, sl])
            hd = jnp.square(jnp.maximum(u, 0.0)).astype(MXU_DTYPE)
            hd_ref[:, sl] = hd
            acc_ref[...] += _dot(hd, w2_v[sl, :])
        out, xhat, rstd = _ln_fwd(acc_ref[...], g_ref[...], b_ref[...])
        hn_ref[...] = out
        xh_ref[...] = xhat
        rs_ref[...] = rstd

    return pl.pallas_call(
        body, name="mlp_ln_fwd", grid=(t // TM,),
        in_specs=[_rows(TM, D_MODEL), _hbm(), _hbm(), _const((1, D_MODEL)), _const((1, D_MODEL))],
        out_specs=[_rows(TM, D_MODEL), _rows(TM, D_MODEL), _rows(TM, 1), _rows(TM, D_FF)],
        out_shape=[jax.ShapeDtypeStruct((t, D_MODEL), F32), jax.ShapeDtypeStruct((t, D_MODEL), F32),
                   jax.ShapeDtypeStruct((t, 1), F32), jax.ShapeDtypeStruct((t, D_FF), MXU_DTYPE)],
        scratch_shapes=[pltpu.VMEM((D_MODEL, D_FF), MXU_DTYPE), pltpu.VMEM((D_FF, D_MODEL), MXU_DTYPE),
                        pltpu.VMEM((TM, D_MODEL), F32), pltpu.SemaphoreType.DMA((2,))],
        compiler_params=_cparams(("arbitrary",)),
    )(h2, w1, w2, g, b)


def _mlp_ln_bwd(dh3, xhat, rstd, g, hdn, w1, w2):
    t = dh3.shape[0]

    def body(dh_ref, xh_ref, rs_ref, g_ref, hd_ref, w1_hbm, w2_hbm,
             dr_ref, du_ref, dh2_ref, dg_ref, db_ref, w1_v, w2_v, acc_ref, sems):
        @pl.when(pl.program_id(0) == 0)
        def _():
            _load_resident([(w1_hbm, w1_v), (w2_hbm, w2_v)], sems)
            dg_ref[...] = jnp.zeros_like(dg_ref)
            db_ref[...] = jnp.zeros_like(db_ref)

        dout = dh_ref[...]
        xh = xh_ref[...]
        dg_ref[...] += _sum0(dout * xh)
        db_ref[...] += _sum0(dout)
        dr = _ln_bwd(dout, xh, rs_ref[...], g_ref[...])
        drb = dr.astype(MXU_DTYPE)
        dr_ref[...] = drb
        acc_ref[...] = ALPHA * dr
        for j in range(N_FF):
            sl = slice(j * FF_CHUNK, (j + 1) * FF_CHUNK)
            dhd = _dot_nt(drb, w2_v[sl, :])
            du = (dhd * (2.0 * jnp.sqrt(hd_ref[:, sl].astype(F32)))).astype(MXU_DTYPE)
            du_ref[:, sl] = du
            acc_ref[...] += _dot_nt(du, w1_v[:, sl])
        dh2_ref[...] = acc_ref[...]

    tm = TM // 2
    return pl.pallas_call(
        body, name="mlp_ln_bwd", grid=(t // tm,),
        in_specs=[_rows(tm, D_MODEL), _rows(tm, D_MODEL), _rows(tm, 1), _const((1, D_MODEL)), _rows(tm, D_FF),
                  _hbm(), _hbm()],
        out_specs=[_rows(tm, D_MODEL), _rows(tm, D_FF), _rows(tm, D_MODEL), _const((1, D_MODEL)),
                   _const((1, D_MODEL))],
        out_shape=[jax.ShapeDtypeStruct((t, D_MODEL), MXU_DTYPE), jax.ShapeDtypeStruct((t, D_FF), MXU_DTYPE),
                   jax.ShapeDtypeStruct((t, D_MODEL), F32), jax.ShapeDtypeStruct((1, D_MODEL), F32),
                   jax.ShapeDtypeStruct((1, D_MODEL), F32)],
        scratch_shapes=[pltpu.VMEM((D_MODEL, D_FF), MXU_DTYPE), pltpu.VMEM((D_FF, D_MODEL), MXU_DTYPE),
                        pltpu.VMEM((tm, D_MODEL), F32), pltpu.SemaphoreType.DMA((2,))],
        compiler_params=_cparams(("arbitrary",)),
    )(dh3, xhat, rstd, g, hdn, w1, w2)


def _outproj_ln_bwd(dh1, xhat, rstd, g, w):
    t = dh1.shape[0]

    def body(dh_ref, xh_ref, rs_ref, g_ref, w_ref, dr_ref, res_ref, dy_ref, dg_ref, db_ref):
        i = pl.program_id(0)

        @pl.when(i == 0)
        def _():
            dg_ref[...] = jnp.zeros_like(dg_ref)
            db_ref[...] = jnp.zeros_like(db_ref)

        dout = dh_ref[...]
        xh = xh_ref[...]
        dg_ref[...] += _sum0(dout * xh)
        db_ref[...] += _sum0(dout)
        dr = _ln_bwd(dout, xh, rs_ref[...], g_ref[...])
        dr_ref[...] = dr.astype(dr_ref.dtype)
        res_ref[...] = ALPHA * dr
        dy_ref[...] = _dot_nt(dr, w_ref[...])

    return pl.pallas_call(
        body, name="outproj_ln_bwd", grid=(t // TM,),
        in_specs=[_rows(TM, D_MODEL), _rows(TM, D_MODEL), _rows(TM, 1), _const((1, D_MODEL)),
                  _const((D_MODEL, D_MODEL))],
        out_specs=[_rows(TM, D_MODEL), _rows(TM, D_MODEL), _rows(TM, D_MODEL), _const((1, D_MODEL)),
                   _const((1, D_MODEL))],
        out_shape=[jax.ShapeDtypeStruct((t, D_MODEL), MXU_DTYPE), jax.ShapeDtypeStruct((t, D_MODEL), F32),
                   jax.ShapeDtypeStruct((t, D_MODEL), F32), jax.ShapeDtypeStruct((1, D_MODEL), F32),
                   jax.ShapeDtypeStruct((1, D_MODEL), F32)],
        compiler_params=_cparams(("arbitrary",)),
    )(dh1, xhat, rstd, g, w)


def _loss_fwd_bwd(h, target):
    t = h.shape[0]

    def body(h_ref, t_ref, l_ref, dh_ref):
        i = pl.program_id(0)

        @pl.when(i == 0)
        def _():
            l_ref[...] = jnp.zeros_like(l_ref)

        e = h_ref[...] - t_ref[...]
        dh_ref[...] = e * (1.0 / D_MODEL)
        per_tok = jnp.mean(e * e, axis=-1, keepdims=True)
        l_ref[...] += 0.5 * jnp.sum(per_tok, axis=0, keepdims=True)

    return pl.pallas_call(
        body, name="loss_fwd_bwd", grid=(t // TM,),
        in_specs=[_rows(TM, D_MODEL), _rows(TM, D_MODEL)],
        out_specs=[_const((1, 1)), _rows(TM, D_MODEL)],
        out_shape=[jax.ShapeDtypeStruct((1, 1), F32), jax.ShapeDtypeStruct((t, D_MODEL), F32)],
        compiler_params=_cparams(("arbitrary",)),
    )(h, target)


def _pick_col(x, idx):
    lane = lax.broadcasted_iota(jnp.int32, x.shape, 1)
    return jnp.sum(jnp.where(lane == idx, x, 0.0), axis=1, keepdims=True)


def _pick_row(x, idx):
    sub = lax.broadcasted_iota(jnp.int32, x.shape, 0)
    return jnp.sum(jnp.where(sub == idx, x, 0.0), axis=0, keepdims=True)


def _conv_taps(pad_ref, w, tm, base):
    acc = w[0:1, :] * pad_ref[base:base + tm, :]
    for k in range(1, 4):
        acc = acc + w[k:k + 1, :] * pad_ref[base + k:base + k + tm, :]
    return acc


def _ssd_chunk_common(adt_c, tri):
    cs = _dot_f32(tri, adt_c)
    return cs, cs.T, jnp.exp(cs)


def _ssd_head_terms(cs, cst, ecs, dt_c, h, tri):
    cs_col = _pick_col(cs, h)
    cs_row = _pick_row(cst, h)
    dt_col = _pick_col(dt_c, h)
    cs_last = cs_col[SSD_CHUNK - 1:SSD_CHUNK, :]
    lmat = jnp.exp(jnp.where(tri > 0.0, cs_col - cs_row, -1e30))
    ecs_col = _pick_col(ecs, h)
    decay_col = jnp.exp(cs_last - cs_col)
    return cs_col, dt_col, cs_last, lmat, ecs_col, decay_col


def _ssd_fwd(proj, cw, cb, dtb, a_neg, d_lanes, nw):
    t = proj.shape[0]
    tm = SSD_TM
    nt = t // tm
    ncq = tm // SSD_CHUNK
    hb = tm // SUBLANES

    def body(xbc_ref, halo_ref, z_ref, dt_ref, cw_ref, cb_ref, dtb_ref, a_ref, d_ref, nw_ref,
             y_ref, yy_ref, st_ref, xpad, xact, state):
        i = pl.program_id(0)

        @pl.when(i == 0)
        def _():
            state[...] = jnp.zeros_like(state)

        xpad[0:SUBLANES, :] = jnp.where(i > 0, halo_ref[...], 0.0)
        xpad[SUBLANES:SUBLANES + tm, :] = xbc_ref[...]
        acc = cb_ref[...] + _conv_taps(xpad, cw_ref[...], tm, SUBLANES - 3)
        xact[...] = acc * _sigmoid(acc)
        dt = _softplus(dt_ref[...] + dtb_ref[...])
        adt = dt * a_ref[...]
        r_i = lax.broadcasted_iota(jnp.int32, (SSD_CHUNK, SSD_CHUNK), 0)
        c_i = lax.broadcasted_iota(jnp.int32, (SSD_CHUNK, SSD_CHUNK), 1)
        tri = (r_i >= c_i).astype(F32)
        lane1 = lax.broadcasted_iota(jnp.int32, (1, LANES), 1)
        for c in range(ncq):
            sl = slice(c * SSD_CHUNK, (c + 1) * SSD_CHUNK)
            dt_c = dt[sl]
            cs, cst, ecs = _ssd_chunk_common(adt[sl], tri)
            for g in range(2):
                bg = xact[sl, 512 + g * 128:512 + (g + 1) * 128]
                cg = xact[sl, 768 + g * 128:768 + (g + 1) * 128]
                cbm = _dot_nt(cg, bg)
                for pr in range(2):
                    pi = g * 2 + pr
                    psl = slice(pi * 128, (pi + 1) * 128)
                    xp = xact[sl, psl]
                    prev = state[pi]
                    st_ref[c, pi] = prev
                    yp = xp * d_ref[:, psl]
                    new_s = jnp.zeros((SSD_STATE, LANES), F32)
                    dec_lane = jnp.zeros((1, LANES), F32)
                    for hh in range(2):
                        h = g * 4 + pr * 2 + hh
                        lm = (lane1 >= 64) if hh else (lane1 < 64)
                        _, dt_col, cs_last, lmat, ecs_col, decay_col = _ssd_head_terms(cs, cst, ecs, dt_c, h, tri)
                        xdt = jnp.where(lm, xp, 0.0) * dt_col
                        yp = yp + _dot(cbm * lmat, xdt)
                        yp = yp + _dot(cg * ecs_col, jnp.where(lm, prev, 0.0))
                        new_s = new_s + _dot_tn(bg * decay_col, xdt)
                        dec_lane = dec_lane + jnp.where(lm, jnp.exp(cs_last), 0.0)
                    state[pi] = prev * dec_lane + new_s
                    yy_ref[sl, psl] = yp
        yy = yy_ref[...]
        z = z_ref[...]
        yg = yy * (z * _sigmoid(z))
        ms = jnp.mean(yg * yg, axis=-1, keepdims=True)
        y_ref[...] = yg * lax.rsqrt(ms + LN_EPS) * nw_ref[...]

    halo_map = lambda i: (jnp.maximum(i * hb - 1, 0), 0)
    return pl.pallas_call(
        body, name="ssd_fwd", grid=(nt,),
        in_specs=[pl.BlockSpec((tm, SSD_XBC), lambda i: (i, 0)), pl.BlockSpec((SUBLANES, SSD_XBC), halo_map),
                  pl.BlockSpec((tm, SSD_WIDTH), lambda i: (i, P_Z // SSD_WIDTH)),
                  pl.BlockSpec((tm, LANES), lambda i: (i, P_DT // LANES)),
                  _const((4, SSD_XBC)), _const((1, SSD_XBC)), _const((1, LANES)), _const((1, LANES)),
                  _const((1, SSD_WIDTH)), _const((1, SSD_WIDTH))],
        out_specs=[_rows(tm, SSD_WIDTH), _rows(tm, SSD_WIDTH),
                   pl.BlockSpec((ncq, 4, SSD_STATE, LANES), lambda i: (i, 0, 0, 0))],
        out_shape=[jax.ShapeDtypeStruct((t, SSD_WIDTH), F32), jax.ShapeDtypeStruct((t, SSD_WIDTH), F32),
                   jax.ShapeDtypeStruct((t // SSD_CHUNK, 4, SSD_STATE, LANES), F32)],
        scratch_shapes=[pltpu.VMEM((tm + SUBLANES, SSD_XBC), F32), pltpu.VMEM((tm, SSD_XBC), F32),
                        pltpu.VMEM((4, SSD_STATE, LANES), F32)],
        compiler_params=_cparams(("arbitrary",)),
    )(proj, proj, proj, proj, cw, cb, dtb, a_neg, d_lanes, nw)


def _ssd_bwd(dycat, proj, yy, states, cw, cb, dtb, a_neg, d_lanes, nw):
    t = proj.shape[0]
    tm = SSD_TM
    nt = t // tm
    ncq = tm // SSD_CHUNK
    hb = tm // SUBLANES

    def body(dy_ref, xbc_ref, halo_ref, z_ref, dt_ref, yy_ref, st_ref, cw_ref, cb_ref, dtb_ref, a_ref, d_ref, nw_ref,
             dxbc_ref, dz_ref, ddt_ref, dcw_ref, dcb_ref, ddtb_ref, da_ref, dd_ref, dnw_ref,
             xpad, xact, dxact, dpad, dstate, dnext):
        i = pl.program_id(0)

        @pl.when(i == 0)
        def _():
            for r in (dcw_ref, dcb_ref, ddtb_ref, da_ref, dd_ref, dnw_ref, dstate, dnext):
                r[...] = jnp.zeros_like(r)

        xpad[0:SUBLANES, :] = jnp.where(i < nt - 1, halo_ref[...], 0.0)
        xpad[SUBLANES:SUBLANES + tm, :] = xbc_ref[...]
        cw_v = cw_ref[...]
        acc = cb_ref[...] + _conv_taps(xpad, cw_v, tm, SUBLANES - 3)
        sig = _sigmoid(acc)
        xact[...] = acc * sig
        dt_raw = dt_ref[...] + dtb_ref[...]
        dt = _softplus(dt_raw)
        a_v = a_ref[...]
        adt = dt * a_v
        yy = yy_ref[...]
        z = z_ref[...]
        sz = _sigmoid(z)
        siluz = z * sz
        yg = yy * siluz
        ms = jnp.mean(yg * yg, axis=-1, keepdims=True)
        rinv = lax.rsqrt(ms + LN_EPS)
        dout = dy_ref[...]
        dnw_ref[...] += _sum0(dout * yg * rinv)
        dyn = dout * nw_ref[...]
        dyg = rinv * dyn - yg * (rinv * rinv * rinv) * jnp.mean(dyn * yg, axis=-1, keepdims=True)
        dyy = dyg * siluz
        dz_ref[...] = dyg * yy * (sz * (1.0 + z * (1.0 - sz)))
        dd_ref[...] += _sum0(dyy * xact[:, 0:SSD_WIDTH])

        r_i = lax.broadcasted_iota(jnp.int32, (SSD_CHUNK, SSD_CHUNK), 0)
        c_i = lax.broadcasted_iota(jnp.int32, (SSD_CHUNK, SSD_CHUNK), 1)
        tri = (r_i >= c_i).astype(F32)
        lane1 = lax.broadcasted_iota(jnp.int32, (1, LANES), 1)
        for c in reversed(range(ncq)):
            sl = slice(c * SSD_CHUNK, (c + 1) * SSD_CHUNK)
            dt_c = dt[sl]
            cs, cst, ecs = _ssd_chunk_common(adt[sl], tri)
            cacc = jnp.zeros((SSD_CHUNK, LANES), F32)
            racc = jnp.zeros((SSD_CHUNK, LANES), F32)
            ddtx = jnp.zeros((SSD_CHUNK, LANES), F32)
            for g in range(2):
                bg = xact[sl, 512 + g * 128:512 + (g + 1) * 128]
                cg = xact[sl, 768 + g * 128:768 + (g + 1) * 128]
                cbm = _dot_nt(cg, bg)
                dcb_m = jnp.zeros((SSD_CHUNK, SSD_CHUNK), F32)
                dbg = jnp.zeros((SSD_CHUNK, SSD_STATE), F32)
                dcg = jnp.zeros((SSD_CHUNK, SSD_STATE), F32)
                for pr in range(2):
                    pi = g * 2 + pr
                    psl = slice(pi * 128, (pi + 1) * 128)
                    xp = xact[sl, psl]
                    dyp = dyy[sl, psl]
                    prev = st_ref[c, pi]
                    ds_all = dstate[pi]
                    dxdt_p = jnp.zeros((SSD_CHUNK, LANES), F32)
                    dprev_new = jnp.zeros((SSD_STATE, LANES), F32)
                    dec_lane = jnp.zeros((1, LANES), F32)
                    dt_lanes = jnp.zeros((SSD_CHUNK, LANES), F32)
                    for hh in range(2):
                        h = g * 4 + pr * 2 + hh
                        lm = (lane1 >= 64) if hh else (lane1 < 64)
                        oh_l = (c_i == h).astype(F32)
                        oh_s = (r_i == h).astype(F32)
                        _, dt_col, cs_last, lmat, ecs_col, decay_col = _ssd_head_terms(cs, cst, ecs, dt_c, h, tri)
                        gm = cbm * lmat
                        xm = jnp.where(lm, xp, 0.0)
                        xdt = xm * dt_col
                        dym = jnp.where(lm, dyp, 0.0)
                        prevm = jnp.where(lm, prev, 0.0)
                        dsm = jnp.where(lm, ds_all, 0.0)
                        bdec = bg * decay_col
                        dxdt = _dot_tn(gm, dym) + _dot(bdec, dsm)
                        dxdt_p = dxdt_p + dxdt
                        ddtx = ddtx + oh_l * jnp.sum(dxdt * xm, axis=1, keepdims=True)
                        dt_lanes = dt_lanes + jnp.where(lm, dt_col, 0.0)
                        dgm = _dot_nt(dym, xdt)
                        dcb_m = dcb_m + dgm * lmat
                        w = dgm * gm
                        cacc = cacc + oh_l * jnp.sum(w, axis=1, keepdims=True)
                        racc = racc - oh_s * jnp.sum(w, axis=0, keepdims=True)
                        dce = _dot_nt(dym, prevm)
                        dcg = dcg + dce * ecs_col
                        cacc = cacc + oh_l * (jnp.sum(dce * cg, axis=1, keepdims=True) * ecs_col)
                        dprev_new = dprev_new + _dot_tn(cg * ecs_col, dym)
                        dbdec = _dot_nt(xdt, dsm)
                        dbg = dbg + dbdec * decay_col
                        dd = jnp.sum(dbdec * bg, axis=1, keepdims=True) * decay_col
                        cacc = cacc - oh_l * dd
                        cd = jnp.exp(cs_last)
                        dlast = jnp.sum(dd, axis=0, keepdims=True) + jnp.sum(
                            jnp.sum(dsm * prevm, axis=1, keepdims=True), axis=0, keepdims=True) * cd
                        cacc = cacc + jnp.where((r_i == SSD_CHUNK - 1) & (c_i == h), dlast, 0.0)
                        dec_lane = dec_lane + jnp.where(lm, cd, 0.0)
                    dstate[pi] = ds_all * dec_lane + dprev_new
                    dxact[sl, psl] = dxdt_p * dt_lanes + dyp * d_ref[:, psl]
                dcg = dcg + _dot(dcb_m, bg)
                dbg = dbg + _dot_tn(dcb_m, cg)
                dxact[sl, 512 + g * 128:512 + (g + 1) * 128] = dbg
                dxact[sl, 768 + g * 128:768 + (g + 1) * 128] = dcg
            dcs = cacc + racc.T
            dadt = _dot_f32((r_i <= c_i).astype(F32), dcs)
            ddt = dadt * a_v + ddtx
            da_ref[...] += _sum0(dadt * dt_c)
            ddt_raw = ddt * _sigmoid(dt_raw[sl])
            ddt_ref[sl, :] = ddt_raw
            ddtb_ref[...] += _sum0(ddt_raw)
        dacc = dxact[...] * (sig * (1.0 + acc * (1.0 - sig)))
        dcb_ref[...] += _sum0(dacc)
        for k in range(4):
            dcw_ref[k:k + 1, :] += _sum0(dacc * xpad[SUBLANES - 3 + k:SUBLANES - 3 + k + tm, :])
        dpad[0:tm, :] = dacc
        dpad[tm:tm + SUBLANES, :] = dnext[...]
        dx = cw_v[0:1, :] * dpad[3:3 + tm, :]
        for k in range(1, 4):
            dx = dx + cw_v[k:k + 1, :] * dpad[3 - k:3 - k + tm, :]
        dxbc_ref[...] = dx
        dnext[...] = dacc[0:SUBLANES, :]

    rev = lambda i: nt - 1 - i
    halo_map = lambda i: (jnp.maximum(rev(i) * hb - 1, 0), 0)
    rrow = lambda n, col=0: pl.BlockSpec((tm, n), lambda i: (rev(i), col))
    return pl.pallas_call(
        body, name="ssd_bwd", grid=(nt,),
        in_specs=[rrow(SSD_WIDTH), rrow(SSD_XBC), pl.BlockSpec((SUBLANES, SSD_XBC), halo_map),
                  rrow(SSD_WIDTH, P_Z // SSD_WIDTH), rrow(LANES, P_DT // LANES), rrow(SSD_WIDTH),
                  pl.BlockSpec((ncq, 4, SSD_STATE, LANES), lambda i: (rev(i), 0, 0, 0)),
                  _const((4, SSD_XBC)), _const((1, SSD_XBC)), _const((1, LANES)), _const((1, LANES)),
                  _const((1, SSD_WIDTH)), _const((1, SSD_WIDTH))],
        out_specs=[rrow(SSD_XBC), rrow(SSD_WIDTH), rrow(LANES), _const((SUBLANES, SSD_XBC)), _const((1, SSD_XBC)),
                   _const((1, LANES)), _const((1, LANES)), _const((1, SSD_WIDTH)), _const((1, SSD_WIDTH))],
        out_shape=[jax.ShapeDtypeStruct((t, SSD_XBC), F32), jax.ShapeDtypeStruct((t, SSD_WIDTH), F32),
                   jax.ShapeDtypeStruct((t, LANES), F32), jax.ShapeDtypeStruct((SUBLANES, SSD_XBC), F32),
                   jax.ShapeDtypeStruct((1, SSD_XBC), F32), jax.ShapeDtypeStruct((1, LANES), F32),
                   jax.ShapeDtypeStruct((1, LANES), F32), jax.ShapeDtypeStruct((1, SSD_WIDTH), F32),
                   jax.ShapeDtypeStruct((1, SSD_WIDTH), F32)],
        scratch_shapes=[pltpu.VMEM((tm + SUBLANES, SSD_XBC), F32), pltpu.VMEM((tm, SSD_XBC), F32),
                        pltpu.VMEM((tm, SSD_XBC), F32), pltpu.VMEM((tm + SUBLANES, SSD_XBC), F32),
                        pltpu.VMEM((4, SSD_STATE, LANES), F32), pltpu.VMEM((SUBLANES, SSD_XBC), F32)],
        compiler_params=_cparams(("arbitrary",)),
    )(dycat, proj, proj, proj, proj, yy, states, cw, cb, dtb, a_neg, d_lanes, nw)


def _cmul_add(ar, ai, br, bi, cr, ci):
    return ar + br * cr - bi * ci, ai + br * ci + bi * cr


def _s5_fwd(proj, bre, bim, cre, cim, d_skip, glu_w, glu_b, coef):
    t = proj.shape[0]
    tm = SCAN_TM
    ng = tm // SUBLANES

    def body(u_ref, bre_ref, bim_ref, cre_ref, cim_ref, d_ref, w_ref, b_ref, coef_ref,
             y_ref, y2_ref, hre_ref, him_ref, carry):
        i = pl.program_id(0)

        @pl.when(i == 0)
        def _():
            carry[...] = jnp.zeros_like(carry)

        u = u_ref[...]
        hre_ref[...] = _dot(u, bre_ref[...])
        him_ref[...] = _dot(u, bim_ref[...])

        def step(gi, car):
            cr_, ci_ = car
            rows = pl.ds(pl.multiple_of(gi * SUBLANES, SUBLANES), SUBLANES)
            r = hre_ref[rows, :]
            m = him_ref[rows, :]
            for k, sh in enumerate((1, 2, 4)):
                r, m = _cmul_add(r, m, coef_ref[k, 0], coef_ref[k, 1], pltpu.roll(r, sh, 0), pltpu.roll(m, sh, 0))
            r, m = _cmul_add(r, m, coef_ref[3, 0], coef_ref[3, 1], cr_, ci_)
            hre_ref[rows, :] = r
            him_ref[rows, :] = m
            return (jnp.broadcast_to(r[SUBLANES - 1:SUBLANES, :], r.shape),
                    jnp.broadcast_to(m[SUBLANES - 1:SUBLANES, :], m.shape))

        cr_, ci_ = lax.fori_loop(0, ng, step, (carry[0], carry[1]))
        carry[0] = cr_
        carry[1] = ci_
        y2 = _dot(hre_ref[...], cre_ref[...]) - _dot(him_ref[...], cim_ref[...]) + d_ref[...] * u
        y2_ref[...] = y2
        ya = _gelu(y2)
        y_ref[...] = ya * _sigmoid(_dot(ya, w_ref[...]) + b_ref[...])

    return pl.pallas_call(
        body, name="s5_fwd", grid=(t // tm,),
        in_specs=[pl.BlockSpec((tm, S5_WIDTH), lambda i: (i, P_U // S5_WIDTH)),
                  _const((S5_WIDTH, S5_NSTATE)), _const((S5_WIDTH, S5_NSTATE)), _const((S5_NSTATE, S5_WIDTH)),
                  _const((S5_NSTATE, S5_WIDTH)), _const((1, S5_WIDTH)), _const((S5_WIDTH, S5_WIDTH)),
                  _const((1, S5_WIDTH)), _const((5, 2, SUBLANES, S5_NSTATE))],
        out_specs=[_rows(tm, S5_WIDTH), _rows(tm, S5_WIDTH), _rows(tm, S5_NSTATE), _rows(tm, S5_NSTATE)],
        out_shape=[jax.ShapeDtypeStruct((t, S5_WIDTH), F32), jax.ShapeDtypeStruct((t, S5_WIDTH), F32),
                   jax.ShapeDtypeStruct((t, S5_NSTATE), F32), jax.ShapeDtypeStruct((t, S5_NSTATE), F32)],
        scratch_shapes=[pltpu.VMEM((2, SUBLANES, S5_NSTATE), F32)],
        compiler_params=_cparams(("arbitrary",)),
    )(proj, bre, bim, cre, cim, d_skip, glu_w, glu_b, coef)


def _s5_bwd(dycat, proj, y2, hre, him, bre, bim, cre, cim, d_skip, glu_w, glu_b, rcoef):
    t = proj.shape[0]
    tm = SCAN_TM
    nt = t // tm
    ng = tm // SUBLANES
    hb = tm // SUBLANES

    def body(dy_ref, u_ref, y2_ref, hre_ref, him_ref, hre_halo, him_halo, bre_ref, bim_ref, cre_ref, cim_ref, d_ref,
             w_ref, b_ref, coef_ref,
             du_ref, dbre_ref, dbim_ref, dcre_ref, dcim_ref, dlam_ref, dd_ref, dw_ref, dgb_ref,
             gre, gim, hpre, hpim, carry):
        i = pl.program_id(0)

        @pl.when(i == 0)
        def _():
            for r in (dbre_ref, dbim_ref, dcre_ref, dcim_ref, dlam_ref, dd_ref, dw_ref, dgb_ref, carry):
                r[...] = jnp.zeros_like(r)

        u = u_ref[...]
        y2 = y2_ref[...]
        dout = dy_ref[...]
        ya = _gelu(y2)
        sg = _sigmoid(_dot(ya, w_ref[...]) + b_ref[...])
        dv = dout * ya * sg * (1.0 - sg)
        dya = dout * sg + _dot_nt(dv, w_ref[...])
        dw_ref[...] += _dot_tn(ya, dv)
        dgb_ref[...] += _sum0(dv)
        dy2 = dya * _gelu_grad(y2)
        dd_ref[...] += _sum0(dy2 * u)
        hre_v = hre_ref[...]
        him_v = him_ref[...]
        dcre_ref[...] += _dot_tn(hre_v, dy2)
        dcim_ref[...] -= _dot_tn(him_v, dy2)
        gre[...] = _dot_nt(dy2, cre_ref[...])
        gim[...] = -_dot_nt(dy2, cim_ref[...])
        first = i == nt - 1
        hpre[0:SUBLANES, :] = jnp.where(first, 0.0, hre_halo[...])
        hpim[0:SUBLANES, :] = jnp.where(first, 0.0, him_halo[...])
        hpre[SUBLANES:SUBLANES + tm, :] = hre_v
        hpim[SUBLANES:SUBLANES + tm, :] = him_v
        row0 = lax.broadcasted_iota(jnp.int32, (SUBLANES, S5_NSTATE), 0) == 0

        def step(k, car):
            cr_, ci_, dlr, dli = car
            gi = ng - 1 - k
            rows = pl.ds(pl.multiple_of(gi * SUBLANES, SUBLANES), SUBLANES)
            nrows = pl.ds(pl.multiple_of(gi * SUBLANES + SUBLANES, SUBLANES), SUBLANES)
            r = gre[rows, :]
            m = gim[rows, :]
            for kk, sh in enumerate((1, 2, 4)):
                r, m = _cmul_add(r, m, coef_ref[kk, 0], coef_ref[kk, 1], pltpu.roll(r, SUBLANES - sh, 0),
                                 pltpu.roll(m, SUBLANES - sh, 0))
            r, m = _cmul_add(r, m, coef_ref[3, 0], coef_ref[3, 1], cr_, ci_)
            gre[rows, :] = r
            gim[rows, :] = m
            pr_ = hpre[rows, :]
            pm_ = hpim[rows, :]
            hr_ = jnp.where(row0, jnp.broadcast_to(pr_[SUBLANES - 1:SUBLANES, :], pr_.shape),
                            pltpu.roll(hpre[nrows, :], 1, 0))
            hm_ = jnp.where(row0, jnp.broadcast_to(pm_[SUBLANES - 1:SUBLANES, :], pm_.shape),
                            pltpu.roll(hpim[nrows, :], 1, 0))
            dlr = dlr + hr_ * r + hm_ * m
            dli = dli + hr_ * m - hm_ * r
            return (jnp.broadcast_to(r[0:1, :], r.shape), jnp.broadcast_to(m[0:1, :], m.shape), dlr, dli)

        z8 = jnp.zeros((SUBLANES, S5_NSTATE), F32)
        cr_, ci_, dlr, dli = lax.fori_loop(0, ng, step, (carry[0], carry[1], z8, z8))
        carry[0] = cr_
        carry[1] = ci_
        dlam_ref[0] += dlr
        dlam_ref[1] += dli
        g_re = gre[...]
        g_im = gim[...]
        du_ref[...] = dy2 * d_ref[...] + _dot_nt(g_re, bre_ref[...]) + _dot_nt(g_im, bim_ref[...])
        dbre_ref[...] += _dot_tn(u, g_re)
        dbim_ref[...] += _dot_tn(u, g_im)

    rev = lambda i: nt - 1 - i
    rrow = lambda n, col=0: pl.BlockSpec((tm, n), lambda i: (rev(i), col))
    halo = pl.BlockSpec((SUBLANES, S5_NSTATE), lambda i: (jnp.maximum(rev(i) * hb - 1, 0), 0))
    return pl.pallas_call(
        body, name="s5_bwd", grid=(nt,),
        in_specs=[rrow(S5_WIDTH, 512 // S5_WIDTH), rrow(S5_WIDTH, P_U // S5_WIDTH), rrow(S5_WIDTH),
                  rrow(S5_NSTATE), rrow(S5_NSTATE), halo, halo,
                  _const((S5_WIDTH, S5_NSTATE)), _const((S5_WIDTH, S5_NSTATE)), _const((S5_NSTATE, S5_WIDTH)),
                  _const((S5_NSTATE, S5_WIDTH)), _const((1, S5_WIDTH)), _const((S5_WIDTH, S5_WIDTH)),
                  _const((1, S5_WIDTH)), _const((5, 2, SUBLANES, S5_NSTATE))],
        out_specs=[rrow(S5_WIDTH), _const((S5_WIDTH, S5_NSTATE)), _const((S5_WIDTH, S5_NSTATE)),
                   _const((S5_NSTATE, S5_WIDTH)), _const((S5_NSTATE, S5_WIDTH)), _const((2, SUBLANES, S5_NSTATE)),
                   _const((1, S5_WIDTH)), _const((S5_WIDTH, S5_WIDTH)), _const((1, S5_WIDTH))],
        out_shape=[jax.ShapeDtypeStruct((t, S5_WIDTH), F32), jax.ShapeDtypeStruct((S5_WIDTH, S5_NSTATE), F32),
                   jax.ShapeDtypeStruct((S5_WIDTH, S5_NSTATE), F32), jax.ShapeDtypeStruct((S5_NSTATE, S5_WIDTH), F32),
                   jax.ShapeDtypeStruct((S5_NSTATE, S5_WIDTH), F32),
                   jax.ShapeDtypeStruct((2, SUBLANES, S5_NSTATE), F32), jax.ShapeDtypeStruct((1, S5_WIDTH), F32),
                   jax.ShapeDtypeStruct((S5_WIDTH, S5_WIDTH), F32), jax.ShapeDtypeStruct((1, S5_WIDTH), F32)],
        scratch_shapes=[pltpu.VMEM((tm, S5_NSTATE), F32), pltpu.VMEM((tm, S5_NSTATE), F32),
                        pltpu.VMEM((tm + SUBLANES, S5_NSTATE), F32), pltpu.VMEM((tm + SUBLANES, S5_NSTATE), F32),
                        pltpu.VMEM((2, SUBLANES, S5_NSTATE), F32)],
        compiler_params=_cparams(("arbitrary",)),
    )(dycat, proj, y2, hre, him, hre, him, bre, bim, cre, cim, d_skip, glu_w, glu_b, rcoef)


def _rg_gates(xc, wa, ba, wx, bx, nsp):
    r = _sigmoid(_dot(xc, wa) + ba)
    ig = _sigmoid(_dot(xc, wx) + bx)
    log_a = nsp * r
    a = jnp.exp(log_a)
    mult = jnp.sqrt(-_expm1(2.0 * log_a))
    return r, ig, a, mult


def _rg_fwd(proj, cw, cb, wa, ba, wx, bx, nsp):
    t = proj.shape[0]
    tm = SCAN_TM
    ng = tm // SUBLANES
    hb = tm // SUBLANES

    def body(x_ref, halo_ref, gt_ref, cw_ref, cb_ref, wa_ref, ba_ref, wx_ref, bx_ref, nsp_ref,
             y_ref, h_ref, xpad, abuf, carry):
        i = pl.program_id(0)

        @pl.when(i == 0)
        def _():
            carry[...] = jnp.zeros_like(carry)

        xpad[0:SUBLANES, :] = jnp.where(i > 0, halo_ref[...], 0.0)
        xpad[SUBLANES:SUBLANES + tm, :] = x_ref[...]
        xc = cb_ref[...] + _conv_taps(xpad, cw_ref[...], tm, SUBLANES - 3)
        _, ig, a, mult = _rg_gates(xc, wa_ref[...], ba_ref[...], wx_ref[...], bx_ref[...], nsp_ref[...])
        abuf[...] = a
        h_ref[...] = mult * (ig * xc)
        sub = lax.broadcasted_iota(jnp.int32, (SUBLANES, RG_WIDTH), 0)

        def step(gi, car):
            rows = pl.ds(pl.multiple_of(gi * SUBLANES, SUBLANES), SUBLANES)
            av = abuf[rows, :]
            bv = h_ref[rows, :]
            for sh in (1, 2, 4):
                m = sub >= sh
                bv = jnp.where(m, av * pltpu.roll(bv, sh, 0) + bv, bv)
                av = jnp.where(m, av * pltpu.roll(av, sh, 0), av)
            hv = bv + av * car
            h_ref[rows, :] = hv
            return jnp.broadcast_to(hv[SUBLANES - 1:SUBLANES, :], hv.shape)

        carry[...] = lax.fori_loop(0, ng, step, carry[...])
        y_ref[...] = h_ref[...] * _gelu(gt_ref[...])

    return pl.pallas_call(
        body, name="rg_fwd", grid=(t // tm,),
        in_specs=[pl.BlockSpec((tm, RG_WIDTH), lambda i: (i, P_XRG // RG_WIDTH)),
                  pl.BlockSpec((SUBLANES, RG_WIDTH), lambda i: (jnp.maximum(i * hb - 1, 0), P_XRG // RG_WIDTH)),
                  pl.BlockSpec((tm, RG_WIDTH), lambda i: (i, P_GRG // RG_WIDTH)),
                  _const((4, RG_WIDTH)), _const((1, RG_WIDTH)), _const((RG_WIDTH, RG_WIDTH)), _const((1, RG_WIDTH)),
                  _const((RG_WIDTH, RG_WIDTH)), _const((1, RG_WIDTH)), _const((1, RG_WIDTH))],
        out_specs=[_rows(tm, RG_WIDTH), _rows(tm, RG_WIDTH)],
        out_shape=[jax.ShapeDtypeStruct((t, RG_WIDTH), F32), jax.ShapeDtypeStruct((t, RG_WIDTH), F32)],
        scratch_shapes=[pltpu.VMEM((tm + SUBLANES, RG_WIDTH), F32), pltpu.VMEM((tm, RG_WIDTH), F32),
                        pltpu.VMEM((SUBLANES, RG_WIDTH), F32)],
        compiler_params=_cparams(("arbitrary",)),
    )(proj, proj, proj, cw, cb, wa, ba, wx, bx, nsp)


def _rg_bwd(dycat, proj, hs, cw, cb, wa, ba, wx, bx, nsp):
    t = proj.shape[0]
    tm = SCAN_TM
    nt = t // tm
    ng = tm // SUBLANES
    hb = tm // SUBLANES

    def body(dy_ref, x_ref, halo_ref, gt_ref, h_ref, h_halo, cw_ref, cb_ref, wa_ref, ba_ref, wx_ref, bx_ref, nsp_ref,
             dx_ref, dgt_ref, dcw_ref, dcb_ref, dwa_ref, dba_ref, dwx_ref, dbx_ref, dnsp_ref,
             xpad, abuf, gbuf, hpad, dabuf, dpad, carry, dnext):
        i = pl.program_id(0)

        @pl.when(i == 0)
        def _():
            for r in (dcw_ref, dcb_ref, dwa_ref, dba_ref, dwx_ref, dbx_ref, dnsp_ref, carry, dnext):
                r[...] = jnp.zeros_like(r)

        first = i == nt - 1
        xpad[0:SUBLANES, :] = jnp.where(first, 0.0, halo_ref[...])
        xpad[SUBLANES:SUBLANES + tm, :] = x_ref[...]
        cw_v = cw_ref[...]
        xc = cb_ref[...] + _conv_taps(xpad, cw_v, tm, SUBLANES - 3)
        nsp_v = nsp_ref[...]
        r, ig, a, mult = _rg_gates(xc, wa_ref[...], ba_ref[...], wx_ref[...], bx_ref[...], nsp_v)
        abuf[...] = a
        hv = h_ref[...]
        hpad[0:SUBLANES, :] = jnp.where(first, 0.0, h_halo[...])
        hpad[SUBLANES:SUBLANES + tm, :] = hv
        gt = gt_ref[...]
        dout = dy_ref[...]
        dgt_ref[...] = dout * hv * _gelu_grad(gt)
        gbuf[...] = dout * _gelu(gt)
        sub = lax.broadcasted_iota(jnp.int32, (SUBLANES, RG_WIDTH), 0)
        last_row = sub == SUBLANES - 1
        row0 = sub == 0

        def step(k, car):
            gi = ng - 1 - k
            rows = pl.ds(pl.multiple_of(gi * SUBLANES, SUBLANES), SUBLANES)
            nrows = pl.ds(pl.multiple_of(gi * SUBLANES + SUBLANES, SUBLANES), SUBLANES)
            av = abuf[rows, :]
            bv = gbuf[rows, :] + jnp.where(last_row, car, 0.0)
            ev = jnp.where(last_row, 0.0, pltpu.roll(av, SUBLANES - 1, 0))
            for sh in (1, 2, 4):
                m = sub < SUBLANES - sh
                bv = jnp.where(m, bv + ev * pltpu.roll(bv, SUBLANES - sh, 0), bv)
                ev = jnp.where(m, ev * pltpu.roll(ev, SUBLANES - sh, 0), 0.0)
            gbuf[rows, :] = bv
            pv = hpad[rows, :]
            hprev = jnp.where(row0, jnp.broadcast_to(pv[SUBLANES - 1:SUBLANES, :], pv.shape),
                              pltpu.roll(hpad[nrows, :], 1, 0))
            dabuf[rows, :] = bv * hprev
            return jnp.broadcast_to((av * bv)[0:1, :], bv.shape)

        carry[...] = lax.fori_loop(0, ng, step, carry[...])
        gv = gbuf[...]
        da = dabuf[...]
        ix = ig * xc
        dmult = gv * ix
        dig = gv * mult * xc
        dxc = gv * mult * ig
        dlog_a = da * a - dmult * (a * a) / mult
        dnsp_ref[...] += _sum0(dlog_a * r)
        dpr = dlog_a * nsp_v * r * (1.0 - r)
        dpi = dig * ig * (1.0 - ig)
        dxc = dxc + _dot_nt(dpr, wa_ref[...]) + _dot_nt(dpi, wx_ref[...])
        dwa_ref[...] += _dot_tn(xc, dpr)
        dwx_ref[...] += _dot_tn(xc, dpi)
        dba_ref[...] += _sum0(dpr)
        dbx_ref[...] += _sum0(dpi)
        dcb_ref[...] += _sum0(dxc)
        for k in range(4):
            dcw_ref[k:k + 1, :] += _sum0(dxc * xpad[SUBLANES - 3 + k:SUBLANES - 3 + k + tm, :])
        dpad[0:tm, :] = dxc
        dpad[tm:tm + SUBLANES, :] = dnext[...]
        dx = cw_v[0:1, :] * dpad[3:3 + tm, :]
        for k in range(1, 4):
            dx = dx + cw_v[k:k + 1, :] * dpad[3 - k:3 - k + tm, :]
        dx_ref[...] = dx
        dnext[...] = dxc[0:SUBLANES, :]

    rev = lambda i: nt - 1 - i
    rrow = lambda n, col=0: pl.BlockSpec((tm, n), lambda i: (rev(i), col))
    sq = _const((RG_WIDTH, RG_WIDTH))
    vec = _const((1, RG_WIDTH))
    return pl.pallas_call(
        body, name="rg_bwd", grid=(nt,),
        in_specs=[rrow(RG_WIDTH, 768 // RG_WIDTH), rrow(RG_WIDTH, P_XRG // RG_WIDTH),
                  pl.BlockSpec((SUBLANES, RG_WIDTH), lambda i: (jnp.maximum(rev(i) * hb - 1, 0), P_XRG // RG_WIDTH)),
                  rrow(RG_WIDTH, P_GRG // RG_WIDTH), rrow(RG_WIDTH),
                  pl.BlockSpec((SUBLANES, RG_WIDTH), lambda i: (jnp.maximum(rev(i) * hb - 1, 0), 0)),
                  _const((4, RG_WIDTH)), vec, sq, vec, sq, vec, vec],
        out_specs=[rrow(RG_WIDTH), rrow(RG_WIDTH), _const((SUBLANES, RG_WIDTH)), vec, sq, vec, sq, vec, vec],
        out_shape=[jax.ShapeDtypeStruct((t, RG_WIDTH), F32), jax.ShapeDtypeStruct((t, RG_WIDTH), F32),
                   jax.ShapeDtypeStruct((SUBLANES, RG_WIDTH), F32), jax.ShapeDtypeStruct((1, RG_WIDTH), F32),
                   jax.ShapeDtypeStruct((RG_WIDTH, RG_WIDTH), F32), jax.ShapeDtypeStruct((1, RG_WIDTH), F32),
                   jax.ShapeDtypeStruct((RG_WIDTH, RG_WIDTH), F32), jax.ShapeDtypeStruct((1, RG_WIDTH), F32),
                   jax.ShapeDtypeStruct((1, RG_WIDTH), F32)],
        scratch_shapes=[pltpu.VMEM((tm + SUBLANES, RG_WIDTH), F32), pltpu.VMEM((tm, RG_WIDTH), F32),
                        pltpu.VMEM((tm, RG_WIDTH), F32), pltpu.VMEM((tm + SUBLANES, RG_WIDTH), F32),
                        pltpu.VMEM((tm, RG_WIDTH), F32), pltpu.VMEM((tm + SUBLANES, RG_WIDTH), F32),
                        pltpu.VMEM((SUBLANES, RG_WIDTH), F32), pltpu.VMEM((SUBLANES, RG_WIDTH), F32)],
        compiler_params=_cparams(("arbitrary",)),
    )(dycat, proj, proj, proj, hs, hs, cw, cb, wa, ba, wx, bx, nsp)


def _block_diag(blocks):
    g, a, b = blocks.shape
    eye = jnp.eye(g, dtype=blocks.dtype)
    return (eye[:, None, :, None] * blocks[:, :, None, :]).reshape(g * a, g * b)


def _block_diag_extract(m, g):
    a, b = m.shape[0] // g, m.shape[1] // g
    m4 = m.reshape(g, a, g, b)
    idx = jnp.arange(g)
    return m4[idx, :, idx, :]


def _s5_prepare(lam_re, lam_im, log_step, b_re, b_im, c_re, c_im):
    step = jnp.exp(log_step)[:, None]
    mag = jnp.exp(lam_re * step)
    lbr = mag * jnp.cos(lam_im * step)
    lbi = mag * jnp.sin(lam_im * step)
    nr, ni = lbr - 1.0, lbi
    den = lam_re * lam_re + lam_im * lam_im
    cr = (nr * lam_re + ni * lam_im) / den
    ci = (ni * lam_re - nr * lam_im) / den
    bbr = cr[..., None] * b_re - ci[..., None] * b_im
    bbi = cr[..., None] * b_im + ci[..., None] * b_re
    bre = _block_diag(jnp.swapaxes(bbr, 1, 2))
    bim = _block_diag(jnp.swapaxes(bbi, 1, 2))
    cre = _block_diag(jnp.swapaxes(c_re, 1, 2))
    cim = _block_diag(jnp.swapaxes(c_im, 1, 2))
    return lbr.reshape(-1), lbi.reshape(-1), bre, bim, cre, cim


def _s5_scan_coef(lbr, lbi, reverse):
    if reverse:
        lbi = -lbi
    pr, pi = [lbr], [lbi]
    for _ in range(7):
        pr, pi = pr + [pr[-1] * lbr - pi[-1] * lbi], pi + [pr[-1] * lbi + pi[-1] * lbr]
    row = jnp.arange(SUBLANES)[:, None]
    tabs = []
    for sh in (1, 2, 4):
        keep = (row < SUBLANES - sh) if reverse else (row >= sh)
        tabs.append(jnp.stack([jnp.where(keep, pr[sh - 1][None, :], 0.0), jnp.where(keep, pi[sh - 1][None, :], 0.0)]))
    powr = jnp.stack(pr)
    powi = jnp.stack(pi)
    if reverse:
        powr, powi = powr[::-1], powi[::-1]
    tabs.append(jnp.stack([powr, powi]))
    tabs.append(jnp.zeros_like(tabs[-1]))
    return jnp.stack(tabs).astype(F32)


def _xy_peers():
    x, y, c = lax.axis_index("x"), lax.axis_index("y"), lax.axis_index("c")
    return x, y, c, [(1 - x, y), (x, 1 - y), (1 - x, 1 - y)]


def _hbm():
    return pl.BlockSpec(memory_space=pl.ANY)


def _xy_allgather(buf, *, name):
    n, w = buf.shape

    def body(x_ref, out_ref, send_sems, recv_sems, local_sem):
        x, y, c, peers = _xy_peers()
        me = 2 * x + y
        own = pltpu.make_async_copy(x_ref, out_ref.at[me], local_sem)
        own.start()
        sends = []
        for k, (px, py) in enumerate(peers):
            cp = pltpu.make_async_remote_copy(src_ref=x_ref, dst_ref=out_ref.at[me], send_sem=send_sems.at[k],
                                              recv_sem=recv_sems.at[k], device_id=(px, py, c), device_id_type=MESH)
            cp.start()
            sends.append(cp)
        for k, (px, py) in enumerate(peers):
            pltpu.make_async_remote_copy(src_ref=x_ref, dst_ref=out_ref.at[2 * px + py], send_sem=send_sems.at[k],
                                         recv_sem=recv_sems.at[k], device_id=(px, py, c),
                                         device_id_type=MESH).wait_recv()
        for cp in sends:
            cp.wait_send()
        own.wait()

    return pl.pallas_call(
        body, name=name, in_specs=[_hbm()], out_specs=_hbm(),
        out_shape=jax.ShapeDtypeStruct((4, n, w), buf.dtype),
        scratch_shapes=[pltpu.SemaphoreType.DMA((3,)), pltpu.SemaphoreType.DMA((3,)), pltpu.SemaphoreType.DMA],
    )(buf)


def _remote(src, dst, send_sem, recv_sem, dev):
    return pltpu.make_async_remote_copy(src_ref=src, dst_ref=dst, send_sem=send_sem, recv_sem=recv_sem,
                                        device_id=dev, device_id_type=MESH)


GATHERED = (
    ("w_in", (2, 1024, W_IN_PAD), 2), ("w_out", (2, 256, 1024), 1), ("xa_wq", (2, 256, 1024), 1),
    ("xa_wk", (2, 256, 1024), 1), ("xa_wv", (2, 256, 1024), 1), ("xa_wo", (2, 256, 1024), 1),
    ("mlp_w1", (2, 1024, 1024), 2), ("mlp_w2", (2, 1024, 1024), 1), ("s5_glu_w", (2, 64, 256), 1),
)


def _weights_allgather(shards):
    n = len(shards)
    axes = [ax for _, _, ax in GATHERED]

    def body(*refs):
        srcs, dsts = refs[:n], refs[n:2 * n]
        send_sems, recv_sems, local_sems = refs[2 * n:]
        x, y, c, peers = _xy_peers()
        me = 2 * x + y

        def part(t, pos):
            w = shards[t].shape[axes[t]]
            idx = tuple(pl.ds(pos * w, w) if d == axes[t] else slice(None) for d in range(shards[t].ndim))
            return dsts[t].at[idx]

        own = []
        for t in range(n):
            cp = pltpu.make_async_copy(srcs[t], part(t, me), local_sems.at[t])
            cp.start()
            own.append(cp)
        sends = []
        for k, (px, py) in enumerate(peers):
            for t in range(n):
                cp = _remote(srcs[t], part(t, me), send_sems.at[k * n + t], recv_sems.at[k * n + t], (px, py, c))
                cp.start()
                sends.append(cp)
        for k, (px, py) in enumerate(peers):
            for t in range(n):
                _remote(srcs[t], part(t, 2 * px + py), send_sems.at[k * n + t], recv_sems.at[k * n + t],
                        (px, py, c)).wait_recv()
        for cp in sends:
            cp.wait_send()
        for cp in own:
            cp.wait()

    def full_shape(s, ax):
        return s[:ax] + (4 * s[ax],) + s[ax + 1:]

    return pl.pallas_call(
        body, name="weights_allgather", in_specs=[_hbm()] * n, out_specs=[_hbm()] * n,
        out_shape=[jax.ShapeDtypeStruct(full_shape(a.shape, ax), a.dtype) for a, ax in zip(shards, axes)],
        scratch_shapes=[pltpu.SemaphoreType.DMA((3 * n,)), pltpu.SemaphoreType.DMA((3 * n,)),
                        pltpu.SemaphoreType.DMA((n,))],
    )(*shards)


C_CHUNKS = 4
XY_CHUNKS = 4
EW_ROWS = 512


def _c_exchange(g):
    _, n, w = g.shape
    n2 = n // 2
    rq = n2 // C_CHUNKS

    def body(g_ref, got_ref, send_sems, recv_sems):
        x, y, c = lax.axis_index("x"), lax.axis_index("y"), lax.axis_index("c")
        cps = []
        for s in range(4):
            for q in range(C_CHUNKS):
                k = s * C_CHUNKS + q
                cp = _remote(g_ref.at[s, pl.ds((1 - c) * n2 + q * rq, rq), :], got_ref.at[s, pl.ds(q * rq, rq), :],
                             send_sems.at[k], recv_sems.at[k], (x, y, 1 - c))
                cp.start()
                cps.append(cp)
        for cp in cps:
            cp.wait_recv()
        for cp in cps:
            cp.wait_send()

    return pl.pallas_call(
        body, name="grad_c_exchange", in_specs=[_hbm()], out_specs=_hbm(),
        out_shape=jax.ShapeDtypeStruct((4, n2, w), g.dtype),
        scratch_shapes=[pltpu.SemaphoreType.DMA((4 * C_CHUNKS,)), pltpu.SemaphoreType.DMA((4 * C_CHUNKS,))],
    )(g)


XFER_DTYPE = jnp.bfloat16


def _add_own_half(g, got, c_arr):
    _, n, w = g.shape
    n2 = n // 2
    nb = n2 // EW_ROWS

    def body(c_ref, a_ref, b_ref, o_ref, t_ref):
        sm = a_ref[...] + b_ref[...]
        o_ref[...] = sm.astype(o_ref.dtype)

        @pl.when(pl.program_id(1) == nb - 1)
        def _():
            t_ref[...] = sm[:, EW_ROWS - MISC_ROWS:, :]

    grid_spec = pltpu.PrefetchScalarGridSpec(
        num_scalar_prefetch=1, grid=(4, nb),
        in_specs=[pl.BlockSpec((1, EW_ROWS, w), lambda s, i, c: (s, c[0] * nb + i, 0)),
                  pl.BlockSpec((1, EW_ROWS, w), lambda s, i, c: (s, i, 0))],
        out_specs=[pl.BlockSpec((1, EW_ROWS, w), lambda s, i, c: (s, i, 0)),
                   pl.BlockSpec((1, MISC_ROWS, w), lambda s, i, c: (s, 0, 0))])
    return pl.pallas_call(
        body, name="grad_add_halves", grid_spec=grid_spec,
        out_shape=[jax.ShapeDtypeStruct((4, n2, w), XFER_DTYPE), jax.ShapeDtypeStruct((4, MISC_ROWS, w), g.dtype)],
        compiler_params=_cparams(("arbitrary", "arbitrary")),
    )(c_arr, g, got)


def _xy_exchange(arrs):
    na = len(arrs)
    pieces = []
    for a, arr in enumerate(arrs):
        nch = XY_CHUNKS if a == 0 else 1
        rq = arr.shape[1] // nch
        pieces += [(a, pl.ds(q * rq, rq)) for q in range(nch)]
    npc = len(pieces)

    def body(*refs):
        ins, outs = refs[:na], refs[na:2 * na]
        send_sems, recv_sems, local_sems = refs[2 * na:]
        x, y, c, peers = _xy_peers()
        me = 2 * x + y
        own = []
        for j, (a, rows) in enumerate(pieces):
            cp = pltpu.make_async_copy(ins[a].at[me, rows, :], outs[a].at[me, rows, :], local_sems.at[j])
            cp.start()
            own.append(cp)
        sends = []
        for k, (px, py) in enumerate(peers):
            for j, (a, rows) in enumerate(pieces):
                cp = _remote(ins[a].at[2 * px + py, rows, :], outs[a].at[me, rows, :], send_sems.at[k * npc + j],
                             recv_sems.at[k * npc + j], (px, py, c))
                cp.start()
                sends.append(cp)
        for k, (px, py) in enumerate(peers):
            for j, (a, rows) in enumerate(pieces):
                _remote(ins[a].at[me, rows, :], outs[a].at[2 * px + py, rows, :], send_sems.at[k * npc + j],
                        recv_sems.at[k * npc + j], (px, py, c)).wait_recv()
        for cp in sends:
            cp.wait_send()
        for cp in own:
            cp.wait()

    return pl.pallas_call(
        body, name="grad_xy_exchange", in_specs=[_hbm()] * na, out_specs=[_hbm()] * na,
        out_shape=[jax.ShapeDtypeStruct(a.shape, a.dtype) for a in arrs],
        scratch_shapes=[pltpu.SemaphoreType.DMA((3 * npc,)), pltpu.SemaphoreType.DMA((3 * npc,)),
                        pltpu.SemaphoreType.DMA((npc,))],
    )(*arrs)


def _sum4_into_half(r, rt, c_arr):
    _, n2, w = r.shape
    nb = n2 // EW_ROWS

    def body(c_ref, r_ref, t_ref, o_ref):
        o_ref[...] = ((r_ref[0].astype(F32) + r_ref[1].astype(F32)) + r_ref[2].astype(F32)) + r_ref[3].astype(F32)

        @pl.when(pl.program_id(0) == nb - 1)
        def _():
            o_ref[EW_ROWS - MISC_ROWS:, :] = ((t_ref[0] + t_ref[1]) + t_ref[2]) + t_ref[3]

    grid_spec = pltpu.PrefetchScalarGridSpec(
        num_scalar_prefetch=1, grid=(nb,),
        in_specs=[pl.BlockSpec((4, EW_ROWS, w), lambda i, c: (0, i, 0)),
                  pl.BlockSpec((4, MISC_ROWS, w), lambda i, c: (0, 0, 0))],
        out_specs=pl.BlockSpec((EW_ROWS, w), lambda i, c: (c[0] * nb + i, 0)))
    return pl.pallas_call(
        body, name="grad_sum4", grid_spec=grid_spec, out_shape=jax.ShapeDtypeStruct((2 * n2, w), F32),
        compiler_params=_cparams(("arbitrary",)),
    )(c_arr, r, rt)


C_GATHER_CHUNKS = 8


def _c_allgather_halves(f):
    n, w = f.shape
    n2 = n // 2
    rq = n2 // C_GATHER_CHUNKS

    def body(f_ref, out_ref, send_sems, recv_sems):
        x, y, c = lax.axis_index("x"), lax.axis_index("y"), lax.axis_index("c")
        sends = []
        for q in range(C_GATHER_CHUNKS):
            rows = pl.ds(c * n2 + q * rq, rq)
            cp = _remote(f_ref.at[rows, :], out_ref.at[rows, :], send_sems.at[q], recv_sems.at[q], (x, y, 1 - c))
            cp.start()
            sends.append(cp)
        for q in range(C_GATHER_CHUNKS):
            rows = pl.ds((1 - c) * n2 + q * rq, rq)
            _remote(f_ref.at[rows, :], out_ref.at[rows, :], send_sems.at[q], recv_sems.at[q],
                    (x, y, 1 - c)).wait_recv()
        for cp in sends:
            cp.wait_send()

    return pl.pallas_call(
        body, name="grad_c_allgather", in_specs=[_hbm()], out_specs=_hbm(), input_output_aliases={0: 0},
        out_shape=jax.ShapeDtypeStruct((n, w), f.dtype),
        scratch_shapes=[pltpu.SemaphoreType.DMA((C_GATHER_CHUNKS,)), pltpu.SemaphoreType.DMA((C_GATHER_CHUNKS,))],
    )(f)


def _adamw(w, m, v, g, g_row0=None):
    shape = w.shape
    cols = shape[-1]
    rows = int(math.prod(shape)) // cols
    tr = 256 if rows % 256 == 0 else rows
    from_flat = g_row0 is not None
    c1 = 1.0 / (1.0 - ADAM_B1 ** ADAM_STEP)
    c2 = 1.0 / (1.0 - ADAM_B2 ** ADAM_STEP)

    def body(w_ref, m_ref, v_ref, g_ref, *outs):
        gg = g_ref[...]
        nm = ADAM_B1 * m_ref[...] + (1.0 - ADAM_B1) * gg
        nv = ADAM_B2 * v_ref[...] + (1.0 - ADAM_B2) * (gg * gg)
        if from_flat:
            outs[0][...] = gg
        d_ref, nm_ref, nv_ref = outs[-3:]
        nm_ref[...] = nm
        nv_ref[...] = nv
        d_ref[...] = -ADAM_LR * ((nm * c1) / (jnp.sqrt(nv * c2) + ADAM_EPS) + ADAM_WD * w_ref[...])

    spec = pl.BlockSpec((tr, cols), lambda i: (i, 0))
    if from_flat:
        assert cols == FLAT and g_row0 % tr == 0
        g_spec = pl.BlockSpec((tr, cols), lambda i: (g_row0 // tr + i, 0))
        g_arg = g
    else:
        g_spec = spec
        g_arg = g.reshape(rows, cols)
    n_out = 4 if from_flat else 3
    sds = jax.ShapeDtypeStruct((rows, cols), F32)
    outs = pl.pallas_call(
        body, name="adamw", grid=(rows // tr,), in_specs=[spec, spec, spec, g_spec], out_specs=[spec] * n_out,
        out_shape=[sds] * n_out, compiler_params=_cparams(("arbitrary",)),
    )(w.reshape(rows, cols), m.reshape(rows, cols), v.reshape(rows, cols), g_arg)
    outs = [o.reshape(shape) for o in outs]
    return outs if from_flat else [g] + outs


SMALL_SHARDED = (("s5_glu_w", (2, 64, 256), 1), ("ssd_conv_w", (2, 4, 256), 2), ("rg_conv_w", (2, 4, 64), 2))
CONV_SHARDED = SMALL_SHARDED[1:]
REPLICATED = (
    ("ssd_conv_b", (2, 1024)), ("ssd_dt_bias", (2, 8)), ("ssd_a_log", (2, 8)), ("ssd_d", (2, 8)),
    ("ssd_norm_w", (2, 512)), ("s5_lam_re", (2, 16, 64)), ("s5_lam_im", (2, 16, 64)), ("s5_log_step", (2, 16)),
    ("s5_b_re", (2, 16, 64, 16)), ("s5_b_im", (2, 16, 64, 16)), ("s5_c_re", (2, 16, 16, 64)),
    ("s5_c_im", (2, 16, 16, 64)), ("s5_d", (2, 256)), ("s5_glu_b", (2, 256)), ("rg_conv_b", (2, 256)),
    ("rg_wa", (2, 4, 64, 64)), ("rg_ba", (2, 4, 64)), ("rg_wx", (2, 4, 64, 64)), ("rg_bx", (2, 4, 64)),
    ("rg_lambda", (2, 256)), ("ln1_g", (2, 1024)), ("ln1_b", (2, 1024)), ("ln2_g", (2, 1024)), ("ln2_b", (2, 1024)),
    ("ln3_g", (2, 1024)), ("ln3_b", (2, 1024)),
)
WEIGHT_ORDER = (
    "w_in", "w_out", "ssd_conv_w", "ssd_conv_b", "ssd_dt_bias", "ssd_a_log", "ssd_d", "ssd_norm_w", "s5_lam_re",
    "s5_lam_im", "s5_log_step", "s5_b_re", "s5_b_im", "s5_c_re", "s5_c_im", "s5_d", "s5_glu_w", "s5_glu_b",
    "rg_conv_w", "rg_conv_b", "rg_wa", "rg_ba", "rg_wx", "rg_bx", "rg_lambda", "ln1_g", "ln1_b", "xa_wq", "xa_wk",
    "xa_wv", "xa_wo", "ln2_g", "ln2_b", "mlp_w1", "mlp_w2", "ln3_g", "ln3_b",
)


def _size(shape):
    return int(math.prod(shape))


def _pad_rows(flat, rows):
    return jnp.pad(flat, (0, rows * FLAT - flat.shape[0])).reshape(rows, FLAT)


def _round_up(a, b):
    return (a + b - 1) // b * b


SMALL_ELEMS = sum(_size(s) for _, s, _ in SMALL_SHARDED)
REP_ELEMS = sum(_size(s) for _, s in REPLICATED)
REP_QROWS = _round_up(-(-REP_ELEMS // (4 * FLAT)), 8)
assert SMALL_ELEMS <= MISC_REP_ROW * FLAT and MISC_REP_ROW + REP_QROWS <= MISC_ROWS
CONV_ROWS = 8


def _pack_shards(tensors, names_shapes):
    return jnp.concatenate([tensors[n].reshape(-1) for n, *_ in names_shapes])


def _unpack(flat, names_shapes):
    out, off = {}, 0
    for n, s, *_ in names_shapes:
        out[n] = flat[off:off + _size(s)].reshape(s)
        off += _size(s)
    return out


def _gather_full(gathered, names_shapes):
    flat = gathered.reshape(4, -1)
    out, off = {}, 0
    for n, s, ax in names_shapes:
        parts = flat[:, off:off + _size(s)].reshape((4,) + s)
        out[n] = jnp.concatenate([parts[k] for k in range(4)], axis=ax)
        off += _size(s)
    return out


def _split_shards(full, names_shapes):
    rows = []
    for k in range(4):
        parts = []
        for n, s, ax in names_shapes:
            w = s[ax]
            parts.append(lax.slice_in_dim(full[n], k * w, (k + 1) * w, axis=ax).reshape(-1))
        rows.append(jnp.concatenate(parts))
    return jnp.stack(rows)


def _pack_cols(w):
    pad = jnp.zeros((w.shape[0], LANES - SSD_HEADS), w.dtype)
    return jnp.concatenate([w[:, O_XBC:O_XBC + 1024], w[:, O_Z:O_Z + 512], w[:, O_U:O_U + 256],
                            w[:, O_XRG:O_XRG + 256], w[:, O_GRG:O_GRG + 256], w[:, O_DT:O_DT + 8], pad], axis=1)


def _unpack_cols(w):
    return jnp.concatenate([w[:, P_Z:P_Z + 512], w[:, P_XBC:P_XBC + 1024], w[:, P_DT:P_DT + 8],
                            w[:, P_U:P_U + 256], w[:, P_XRG:P_XRG + 256], w[:, P_GRG:P_GRG + 256]], axis=1)


def _lanes(v, width):
    return jnp.pad(v, (0, width - v.shape[0])).reshape(1, width)


def _layer_params(full, rep, l):
    p = {}
    p["w_in"] = _pack_cols(full["w_in"][l])
    for n in ("w_out", "xa_wq", "xa_wk", "xa_wv", "xa_wo", "mlp_w1", "mlp_w2", "s5_glu_w"):
        p[n] = full[n][l]
    p["ssd_cw"] = full["ssd_conv_w"][l]
    p["ssd_cb"] = rep["ssd_conv_b"][l].reshape(1, -1)
    p["ssd_dtb"] = _lanes(rep["ssd_dt_bias"][l], LANES)
    p["ssd_a"] = _lanes(-jnp.exp(rep["ssd_a_log"][l]), LANES)
    p["ssd_d"] = jnp.repeat(rep["ssd_d"][l], 64).reshape(1, -1)
    p["ssd_nw"] = rep["ssd_norm_w"][l].reshape(1, -1)
    s5_args = tuple(rep[n][l] for n in ("s5_lam_re", "s5_lam_im", "s5_log_step", "s5_b_re", "s5_b_im", "s5_c_re",
                                        "s5_c_im"))
    (lbr, lbi, bre, bim, cre, cim), p["s5_vjp"] = jax.vjp(_s5_prepare, *s5_args)
    p.update(s5_bre=bre, s5_bim=bim, s5_cre=cre, s5_cim=cim)
    p["s5_coef"] = _s5_scan_coef(lbr, lbi, False)
    p["s5_rcoef"] = _s5_scan_coef(lbr, lbi, True)
    p["s5_d"] = rep["s5_d"][l].reshape(1, -1)
    p["s5_gb"] = rep["s5_glu_b"][l].reshape(1, -1)
    p["rg_cw"] = full["rg_conv_w"][l]
    p["rg_cb"] = rep["rg_conv_b"][l].reshape(1, -1)
    p["rg_wa"] = _block_diag(rep["rg_wa"][l])
    p["rg_wx"] = _block_diag(rep["rg_wx"][l])
    p["rg_ba"] = rep["rg_ba"][l].reshape(1, -1)
    p["rg_bx"] = rep["rg_bx"][l].reshape(1, -1)
    p["rg_nsp"] = (-RG_C * jax.nn.softplus(-rep["rg_lambda"][l])).reshape(1, -1)
    p["rg_dnsp"] = RG_C * jax.nn.sigmoid(-rep["rg_lambda"][l])
    for n in ("ln1_g", "ln1_b", "ln2_g", "ln2_b", "ln3_g", "ln3_b"):
        p[n] = rep[n][l].reshape(1, -1)
    return p


def _layer_fwd(h, mem, p):
    s = {"h0": h}
    proj = _mm(h, p["w_in"], name="in_proj")
    s["proj"] = proj
    y_ssd, s["ssd_yy"], s["ssd_states"] = _ssd_fwd(proj, p["ssd_cw"], p["ssd_cb"], p["ssd_dtb"], p["ssd_a"],
                                                     p["ssd_d"], p["ssd_nw"])
    y_s5, s["s5_y2"], s["s5_hre"], s["s5_him"] = _s5_fwd(proj, p["s5_bre"], p["s5_bim"], p["s5_cre"], p["s5_cim"],
                                                         p["s5_d"], p["s5_glu_w"], p["s5_gb"], p["s5_coef"])
    y_rg, s["rg_h"] = _rg_fwd(proj, p["rg_cw"], p["rg_cb"], p["rg_wa"], p["rg_ba"], p["rg_wx"], p["rg_bx"],
                              p["rg_nsp"])
    ycat = jnp.concatenate([y_ssd, y_s5, y_rg], axis=1)
    s["ycat"] = ycat
    h1, s["xh1"], s["rs1"] = _outproj_ln_fwd(ycat, h, p["w_out"], p["ln1_g"], p["ln1_b"])
    s["h1"] = h1
    kb = _mm(mem, p["xa_wk"], name="mem_proj")
    vb = _mm(mem, p["xa_wv"], name="mem_proj")
    s["kb"], s["vb"] = kb, vb
    h2, s["xh2"], s["rs2"], s["attn_o"] = _attn_ln_fwd(h1, p["xa_wq"], p["xa_wo"], kb, vb, p["ln2_g"], p["ln2_b"])
    s["h2"] = h2
    h3, s["xh3"], s["rs3"], s["mlp_hdn"] = _mlp_ln_fwd(h2, p["mlp_w1"], p["mlp_w2"], p["ln3_g"], p["ln3_b"])
    return h3, s


def _layer_bwd(dh3, mem, p, s, l, gbuf):
    g = {}
    dr3, du, dh2, g["ln3_g"], g["ln3_b"] = _mlp_ln_bwd(dh3, s["xh3"], s["rs3"], p["ln3_g"], s["mlp_hdn"],
                                                        p["mlp_w1"], p["mlp_w2"])
    gbuf = _wgrad_flat(s["h2"], du, gbuf, mode="colblk", row_off=ROW_MLP_W1 + 1024 * l, name="wgrad_mlp_w1")
    gbuf = _wgrad_flat(s["mlp_hdn"], dr3, gbuf, mode="rowblk", row_off=ROW_MLP_W2 + 1024 * l, name="wgrad_mlp_w2")
    dr2, dq, dh1, dkb, dvb, g["ln2_g"], g["ln2_b"] = _attn_ln_bwd(dh2, s["xh2"], s["rs2"], p["ln2_g"], s["h1"],
                                                                   p["xa_wq"], p["xa_wo"], s["kb"], s["vb"])
    for n, a_op, g_op in (("xa_wo", s["attn_o"], dr2), ("xa_wq", s["h1"], dq), ("xa_wk", mem, dkb),
                          ("xa_wv", mem, dvb)):
        gbuf = _wgrad_flat(a_op, g_op, gbuf, mode="rows4", row_off=ROW_XA[n] + 256 * l, name="wgrad_" + n)
    dr1, dres, dycat, g["ln1_g"], g["ln1_b"] = _outproj_ln_bwd(dh1, s["xh1"], s["rs1"], p["ln1_g"], p["w_out"])
    gbuf = _wgrad_flat(s["ycat"], dr1, gbuf, mode="rows4", row_off=ROW_W_OUT + 256 * l, name="wgrad_w_out")
    proj = s["proj"]
    (dxbc, dz, ddt, dcw, dcb, ddtb, da_neg, dd_l, dnw) = _ssd_bwd(
        dycat, proj, s["ssd_yy"], s["ssd_states"], p["ssd_cw"], p["ssd_cb"], p["ssd_dtb"], p["ssd_a"], p["ssd_d"],
        p["ssd_nw"])
    g["ssd_conv_w"] = dcw[0:4]
    g["ssd_conv_b"] = dcb[0]
    g["ssd_dt_bias"] = ddtb[0, :SSD_HEADS]
    g["ssd_a_log"] = da_neg[0, :SSD_HEADS] * p["ssd_a"][0, :SSD_HEADS]
    g["ssd_d"] = dd_l.reshape(SSD_HEADS, 64).sum(axis=1)
    g["ssd_norm_w"] = dnw[0]
    (du_s5, dbre, dbim, dcre, dcim, dlam, dd5, dgw, dgb) = _s5_bwd(
        dycat, proj, s["s5_y2"], s["s5_hre"], s["s5_him"], p["s5_bre"], p["s5_bim"], p["s5_cre"], p["s5_cim"],
        p["s5_d"], p["s5_glu_w"], p["s5_gb"], p["s5_rcoef"])
    dl = dlam.sum(axis=1)
    s5g = p["s5_vjp"]((dl[0], dl[1], dbre, dbim, dcre, dcim))
    for n, v in zip(("s5_lam_re", "s5_lam_im", "s5_log_step", "s5_b_re", "s5_b_im", "s5_c_re", "s5_c_im"), s5g):
        g[n] = v
    g["s5_d"] = dd5[0]
    g["s5_glu_w"] = dgw
    g["s5_glu_b"] = dgb[0]
    (dxrg, dgrg, drcw, drcb, dwa, dba, dwx, dbx, dnsp) = _rg_bwd(
        dycat, proj, s["rg_h"], p["rg_cw"], p["rg_cb"], p["rg_wa"], p["rg_ba"], p["rg_wx"], p["rg_bx"], p["rg_nsp"])
    g["rg_conv_w"] = drcw[0:4]
    g["rg_conv_b"] = drcb[0]
    g["rg_wa"] = _block_diag_extract(dwa, RG_BLOCKS)
    g["rg_wx"] = _block_diag_extract(dwx, RG_BLOCKS)
    g["rg_ba"] = dba.reshape(RG_BLOCKS, RG_BLOCK_DIM)
    g["rg_bx"] = dbx.reshape(RG_BLOCKS, RG_BLOCK_DIM)
    g["rg_lambda"] = dnsp[0] * p["rg_dnsp"]
    dproj = jnp.concatenate([dxbc, dz, du_s5, dxrg, dgrg, ddt], axis=1)
    g["w_in"] = _unpack_cols(_mm_tn(s["h0"], dproj, name="wgrad_in"))
    dh0 = _mm(dproj, p["w_in"], nt=True, add=dres, name="in_proj_bwd")
    for n in ("ln1_g", "ln1_b", "ln2_g", "ln2_b", "ln3_g", "ln3_b"):
        g[n] = g[n][0]
    return dh0, g, gbuf


def _local_step(h, memf, target, full, rep):
    params, saved = [], []
    for l in range(DEPTH):
        p = _layer_params(full, rep, l)
        params.append(p)
        h, s = _layer_fwd(h, memf, p)
        saved.append(s)
    loss11, dh = _loss_fwd_bwd(h, target)
    grads = [None] * DEPTH
    gbuf = None
    for l in reversed(range(DEPTH)):
        dh, grads[l], gbuf = _layer_bwd(dh, memf, params[l], saved[l], l, gbuf)
    return loss11, dh, {n: jnp.stack([grads[l][n] for l in range(DEPTH)]) for n in grads[0]}, gbuf


def kernel(x, mem, w_in, w_out, ssd_conv_w, ssd_conv_b, ssd_dt_bias, ssd_a_log, ssd_d, ssd_norm_w, s5_lam_re, s5_lam_im, s5_log_step, s5_b_re, s5_b_im, s5_c_re, s5_c_im, s5_d, s5_glu_w, s5_glu_b, rg_conv_w, rg_conv_b, rg_wa, rg_ba, rg_wx, rg_bx, rg_lambda, ln1_g, ln1_b, xa_wq, xa_wk, xa_wv, xa_wo, ln2_g, ln2_b, mlp_w1, mlp_w2, ln3_g, ln3_b, loss_target, m_w_in, m_w_out, m_ssd_conv_w, m_ssd_conv_b, m_ssd_dt_bias, m_ssd_a_log, m_ssd_d, m_ssd_norm_w, m_s5_lam_re, m_s5_lam_im, m_s5_log_step, m_s5_b_re, m_s5_b_im, m_s5_c_re, m_s5_c_im, m_s5_d, m_s5_glu_w, m_s5_glu_b, m_rg_conv_w, m_rg_conv_b, m_rg_wa, m_rg_ba, m_rg_wx, m_rg_bx, m_rg_lambda, m_ln1_g, m_ln1_b, m_xa_wq, m_xa_wk, m_xa_wv, m_xa_wo, m_ln2_g, m_ln2_b, m_mlp_w1, m_mlp_w2, m_ln3_g, m_ln3_b, v_w_in, v_w_out, v_ssd_conv_w, v_ssd_conv_b, v_ssd_dt_bias, v_ssd_a_log, v_ssd_d, v_ssd_norm_w, v_s5_lam_re, v_s5_lam_im, v_s5_log_step, v_s5_b_re, v_s5_b_im, v_s5_c_re, v_s5_c_im, v_s5_d, v_s5_glu_w, v_s5_glu_b, v_rg_conv_w, v_rg_conv_b, v_rg_wa, v_rg_ba, v_rg_wx, v_rg_bx, v_rg_lambda, v_ln1_g, v_ln1_b, v_xa_wq, v_xa_wk, v_xa_wv, v_xa_wo, v_ln2_g, v_ln2_b, v_mlp_w1, v_mlp_w2, v_ln3_g, v_ln3_b):
    args = dict(locals())
    weights = {n: args[n] for n in WEIGHT_ORDER}
    mom_m = {n: args["m_" + n] for n in WEIGHT_ORDER}
    mom_v = {n: args["v_" + n] for n in WEIGHT_ORDER}

    w_in_pad = jnp.pad(w_in, ((0, 0), (0, 0), (0, W_IN_PAD - W_IN_SHARD)))
    shards = [w_in_pad.astype(MXU_DTYPE)] + [weights[n].astype(MXU_DTYPE) for n, _, _ in GATHERED[1:]]
    full = dict(zip([n for n, _, _ in GATHERED], _weights_allgather(shards)))
    full["w_in"] = jnp.concatenate(
        [full["w_in"][:, :, W_IN_PAD * k:W_IN_PAD * k + W_IN_SHARD] for k in range(4)], axis=2)
    conv_flat = _pad_rows(_pack_shards(weights, CONV_SHARDED), CONV_ROWS)
    full.update(_gather_full(_xy_allgather(conv_flat, name="conv_weights_allgather"), CONV_SHARDED))
    rep = {n: weights[n] for n, _ in REPLICATED}

    loss11, dx, gsmall, gbuf = _local_step(x[0], mem[0], loss_target[0], full, rep)
    grad_x = dx[None]
    loss = lax.psum(loss11[0, 0], ("x", "y", "c"))

    gw = gsmall["w_in"].reshape(DEPTH, D_MODEL, 4, W_IN_SHARD)
    gw = jnp.pad(gw, ((0, 0), (0, 0), (0, 0), (0, W_IN_PAD - W_IN_SHARD)))
    w_in_blk = jnp.transpose(gw, (2, 0, 1, 3)).reshape(4, DEPTH * W_IN_PAD, FLAT)
    small_q = _split_shards(gsmall, SMALL_SHARDED)
    rep_q = jnp.pad(_pack_shards(gsmall, REPLICATED), (0, 4 * REP_QROWS * FLAT - REP_ELEMS)).reshape(4, -1)
    misc = jnp.concatenate(
        [jnp.pad(small_q, ((0, 0), (0, MISC_REP_ROW * FLAT - SMALL_ELEMS))), rep_q,
         jnp.zeros((4, (MISC_ROWS - MISC_REP_ROW - REP_QROWS) * FLAT), F32)], axis=1).reshape(4, MISC_ROWS, FLAT)
    gbuf = lax.dynamic_update_slice(gbuf, w_in_blk, (0, ROW_W_IN, 0))
    gbuf = lax.dynamic_update_slice(gbuf, misc, (0, ROW_MISC, 0))
    c_arr = lax.axis_index("c").astype(jnp.int32).reshape(1)
    chip_sum, chip_tail = _add_own_half(gbuf, _c_exchange(gbuf), c_arr)
    got_sum, got_tail = _xy_exchange([chip_sum, chip_tail])
    reduced = _c_allgather_halves(_sum4_into_half(got_sum, got_tail, c_arr))
    misc_red = reduced[ROW_MISC:]
    rep_all = _xy_allgather(misc_red[MISC_REP_ROW:MISC_REP_ROW + REP_QROWS], name="small_grads_allgather")
    g_red = {**_unpack(misc_red[:MISC_REP_ROW].reshape(-1), SMALL_SHARDED),
             **_unpack(rep_all.reshape(-1), REPLICATED)}
    g_red["w_in"] = reduced[ROW_W_IN:ROW_W_IN + DEPTH * W_IN_PAD].reshape(DEPTH, D_MODEL, W_IN_PAD)[:, :, :W_IN_SHARD]

    flat_rows = {"mlp_w1": ROW_MLP_W1, "mlp_w2": ROW_MLP_W2, "w_out": ROW_W_OUT, **ROW_XA}
    res = {}
    for n in WEIGHT_ORDER:
        if n in flat_rows:
            res[n] = _adamw(weights[n], mom_m[n], mom_v[n], reduced, g_row0=flat_rows[n])
        else:
            res[n] = _adamw(weights[n], mom_m[n], mom_v[n], g_red[n])
    return (loss, grad_x, *[res[n][0] for n in WEIGHT_ORDER], *[res[n][1] for n in WEIGHT_ORDER],
            *[res[n][2] for n in WEIGHT_ORDER], *[res[n][3] for n in WEIGHT_ORDER])
```

```python
import functools
import math

import jax
import jax.numpy as jnp
from jax import lax
from jax.experimental import pallas as pl
from jax.experimental.pallas import tpu as pltpu

F32 = jnp.float32
MXU_DTYPE = jnp.bfloat16

D_MODEL = 1024
DEPTH = 2
MEM_LEN = 256
SSD_WIDTH = 512
SSD_HEADS = 8
SSD_STATE = 128
SSD_CHUNK = 128
SSD_XBC = 1024
S5_WIDTH = 256
S5_GROUPS = 16
S5_GROUP_CH = 16
S5_STATE = 64
S5_NSTATE = S5_GROUPS * S5_STATE
RG_WIDTH = 256
RG_BLOCKS = 4
RG_BLOCK_DIM = 64
RG_C = 8.0
XA_HEADS = 4
XA_HEAD_DIM = 256
D_FF = 4096
D_IN = 2312
ALPHA = (2.0 * DEPTH) ** 0.25
LN_EPS = 1e-5
ADAM_LR = 0.001
ADAM_B1 = 0.9
ADAM_B2 = 0.999
ADAM_EPS = 1e-08
ADAM_WD = 0.01
ADAM_STEP = 10

P_XBC, P_Z, P_U, P_XRG, P_GRG, P_DT = 0, 1024, 1536, 1792, 2048, 2304
D_PACK = 2432
O_Z, O_XBC, O_DT, O_U, O_XRG, O_GRG = 0, 512, 1536, 1544, 1800, 2056

LANES = 128
SUBLANES = 8
VMEM_LIMIT = 52 * 1024 * 1024
TM = 512
SSD_TM = 256
SCAN_TM = 512
FLAT = 1024

MESH = pl.DeviceIdType.MESH


def _cparams(sem):
    return pltpu.CompilerParams(dimension_semantics=sem, vmem_limit_bytes=VMEM_LIMIT)


def _dot(a, b):
    return jnp.dot(a.astype(MXU_DTYPE), b.astype(MXU_DTYPE), preferred_element_type=F32)


def _dot_nt(a, b):
    return lax.dot_general(a.astype(MXU_DTYPE), b.astype(MXU_DTYPE), (((1,), (1,)), ((), ())),
                           preferred_element_type=F32)


def _dot_tn(a, b):
    return lax.dot_general(a.astype(MXU_DTYPE), b.astype(MXU_DTYPE), (((0,), (0,)), ((), ())),
                           preferred_element_type=F32)


def _dot_f32(a, b):
    return jnp.dot(a, b, precision=lax.Precision.HIGHEST, preferred_element_type=F32)


def _dot_f32_tn(a, b):
    return lax.dot_general(a, b, (((0,), (0,)), ((), ())), precision=lax.Precision.HIGHEST,
                           preferred_element_type=F32)


def _sigmoid(x):
    return 1.0 / (1.0 + jnp.exp(-x))


def _softplus(x):
    return jnp.maximum(x, 0.0) + jnp.log(1.0 + jnp.exp(-jnp.abs(x)))


_GELU_K = math.sqrt(2.0 / math.pi)


def _gelu(x):
    return 0.5 * x * (1.0 + jnp.tanh(_GELU_K * (x + 0.044715 * x * x * x)))


def _gelu_grad(x):
    t = jnp.tanh(_GELU_K * (x + 0.044715 * x * x * x))
    return 0.5 * (1.0 + t) + 0.5 * x * (1.0 - t * t) * _GELU_K * (1.0 + 3.0 * 0.044715 * x * x)


def _expm1(x):
    small = x * (1.0 + x * (0.5 + x * (1.0 / 6.0 + x * (1.0 / 24.0))))
    return jnp.where(jnp.abs(x) < 0.05, small, jnp.exp(x) - 1.0)


def _sum0(x):
    return jnp.sum(x, axis=0, keepdims=True)


def _ln_fwd(r, g, b):
    mu = jnp.mean(r, axis=-1, keepdims=True)
    xc = r - mu
    var = jnp.mean(xc * xc, axis=-1, keepdims=True)
    rstd = lax.rsqrt(var + LN_EPS)
    xhat = xc * rstd
    return xhat * g + b, xhat, rstd


def _ln_bwd(dout, xhat, rstd, g):
    dxh = dout * g
    m1 = jnp.mean(dxh, axis=-1, keepdims=True)
    m2 = jnp.mean(dxh * xhat, axis=-1, keepdims=True)
    return rstd * (dxh - m1 - xhat * m2)


def _rows(tm, n, col=0):
    return pl.BlockSpec((tm, n), lambda i: (i, col))


def _const(shape):
    nd = len(shape)
    return pl.BlockSpec(shape, lambda i: (0,) * nd)


def _mm(a, w, *, nt=False, add=None, out_dtype=F32, name):
    t, k = a.shape
    n = w.shape[0] if nt else w.shape[1]
    tm = min(TM, t)

    def body(*refs):
        if add is None:
            a_ref, w_ref, o_ref = refs
        else:
            a_ref, w_ref, add_ref, o_ref = refs
        r = _dot_nt(a_ref[...], w_ref[...]) if nt else _dot(a_ref[...], w_ref[...])
        if add is not None:
            r = r + add_ref[...]
        o_ref[...] = r.astype(out_dtype)

    in_specs = [_rows(tm, k), _const(w.shape)]
    args = [a, w]
    if add is not None:
        in_specs.append(_rows(tm, n))
        args.append(add)
    return pl.pallas_call(
        body, name=name, grid=(t // tm,), in_specs=in_specs, out_specs=_rows(tm, n),
        out_shape=jax.ShapeDtypeStruct((t, n), out_dtype), compiler_params=_cparams(("arbitrary",)),
    )(*args)


def _mm_tn(a, g, *, name):
    t, k = a.shape
    n = g.shape[1]
    tt = min(512, t)
    tk = min(1024, k)
    tn = 1024 if n % 1024 == 0 else n
    nsteps = t // tt

    def body(a_ref, g_ref, o_ref):
        s = pl.program_id(2)
        part = _dot_tn(a_ref[...], g_ref[...])

        @pl.when(s == 0)
        def _():
            o_ref[...] = part

        @pl.when(s > 0)
        def _():
            o_ref[...] += part

    return pl.pallas_call(
        body, name=name, grid=(k // tk, n // tn, nsteps),
        in_specs=[pl.BlockSpec((tt, tk), lambda i, j, s: (s, i)), pl.BlockSpec((tt, tn), lambda i, j, s: (s, j))],
        out_specs=pl.BlockSpec((tk, tn), lambda i, j, s: (i, j)),
        out_shape=jax.ShapeDtypeStruct((k, n), F32),
        compiler_params=_cparams(("arbitrary", "arbitrary", "arbitrary")),
    )(a, g)


G_ROWS = 8192
ROW_MLP_W1 = 0
ROW_MLP_W2 = 2048
ROW_W_IN = 4096
ROW_W_OUT = 5376
ROW_XA = {"xa_wq": 5888, "xa_wk": 6400, "xa_wv": 6912, "xa_wo": 7424}
ROW_MISC = 7936
MISC_ROWS = G_ROWS - ROW_MISC
MISC_REP_ROW = 40
W_IN_SHARD = 578
W_IN_PAD = 640


def _wgrad_flat(a, g, buf, *, mode, row_off, name):
    t = a.shape[0]
    tt = min(1024, t)
    ns = t // tt
    blk = D_MODEL

    def accumulate(o_ref, part, s):
        @pl.when(s == 0)
        def _():
            if mode == "rows4":
                for q in range(4):
                    o_ref[q] = part[q * 256:(q + 1) * 256]
            else:
                o_ref[0] = part

        @pl.when(s > 0)
        def _():
            if mode == "rows4":
                for q in range(4):
                    o_ref[q] += part[q * 256:(q + 1) * 256]
            else:
                o_ref[0] += part

    if mode == "rows4":
        grid = (ns,)
        in_specs = [pl.BlockSpec((tt, blk), lambda s: (s, 0)), pl.BlockSpec((tt, blk), lambda s: (s, 0))]
        out_spec = pl.BlockSpec((4, 256, FLAT), lambda s: (0, row_off // 256, 0))
        sem = ("arbitrary",)

        def body(a_ref, g_ref, *rest):
            accumulate(rest[-1], _dot_tn(a_ref[...], g_ref[...]), pl.program_id(0))
    else:
        grid = (4, ns)
        if mode == "rowblk":
            in_specs = [pl.BlockSpec((tt, blk), lambda q, s: (s, q)), pl.BlockSpec((tt, blk), lambda q, s: (s, 0))]
        else:
            in_specs = [pl.BlockSpec((tt, blk), lambda q, s: (s, 0)), pl.BlockSpec((tt, blk), lambda q, s: (s, q))]
        out_spec = pl.BlockSpec((1, blk, FLAT), lambda q, s: (q, row_off // blk, 0))
        sem = ("arbitrary", "arbitrary")

        def body(a_ref, g_ref, *rest):
            accumulate(rest[-1], _dot_tn(a_ref[...], g_ref[...]), pl.program_id(1))

    args = [a, g]
    aliases = {}
    if buf is not None:
        in_specs.append(pl.BlockSpec(memory_space=pl.ANY))
        args.append(buf)
        aliases = {2: 0}
    return pl.pallas_call(
        body, name=name, grid=grid, in_specs=in_specs, out_specs=out_spec,
        out_shape=jax.ShapeDtypeStruct((4, G_ROWS, FLAT), F32), input_output_aliases=aliases,
        compiler_params=_cparams(sem),
    )(*args)


def _outproj_ln_fwd(ycat, h, w, g, b):
    t = h.shape[0]

    def body(y_ref, h_ref, w_ref, g_ref, b_ref, hn_ref, xh_ref, rs_ref):
        r = ALPHA * h_ref[...] + _dot(y_ref[...], w_ref[...])
        out, xhat, rstd = _ln_fwd(r, g_ref[...], b_ref[...])
        hn_ref[...] = out
        xh_ref[...] = xhat
        rs_ref[...] = rstd

    return pl.pallas_call(
        body, name="outproj_ln_fwd", grid=(t // TM,),
        in_specs=[_rows(TM, D_MODEL), _rows(TM, D_MODEL), _const((D_MODEL, D_MODEL)), _const((1, D_MODEL)),
                  _const((1, D_MODEL))],
        out_specs=[_rows(TM, D_MODEL), _rows(TM, D_MODEL), _rows(TM, 1)],
        out_shape=[jax.ShapeDtypeStruct((t, D_MODEL), F32), jax.ShapeDtypeStruct((t, D_MODEL), F32),
                   jax.ShapeDtypeStruct((t, 1), F32)],
        compiler_params=_cparams(("arbitrary",)),
    )(ycat, h, w, g, b)


def _attn_probs(q, kb, hh):
    sl = slice(hh * XA_HEAD_DIM, (hh + 1) * XA_HEAD_DIM)
    s = _dot_nt(q[:, sl], kb[:, sl]) * (1.0 / math.sqrt(XA_HEAD_DIM))
    m = jnp.max(s, axis=-1, keepdims=True)
    e = jnp.exp(s - m)
    return e / jnp.sum(e, axis=-1, keepdims=True)


def _attn_ln_fwd(h1, wq, wo, kb, vb, g, b):
    t = h1.shape[0]

    def body(h_ref, wq_ref, wo_ref, k_ref, v_ref, g_ref, b_ref, hn_ref, xh_ref, rs_ref, o_ref):
        h = h_ref[...]
        q = _dot(h, wq_ref[...])
        kb_ = k_ref[...]
        vb_ = v_ref[...]
        for hh in range(XA_HEADS):
            sl = slice(hh * XA_HEAD_DIM, (hh + 1) * XA_HEAD_DIM)
            p = _attn_probs(q, kb_, hh)
            o_ref[:, sl] = _dot(p, vb_[:, sl]).astype(o_ref.dtype)
        r = ALPHA * h + _dot(o_ref[...], wo_ref[...])
        out, xhat, rstd = _ln_fwd(r, g_ref[...], b_ref[...])
        hn_ref[...] = out
        xh_ref[...] = xhat
        rs_ref[...] = rstd

    return pl.pallas_call(
        body, name="attn_ln_fwd", grid=(t // TM,),
        in_specs=[_rows(TM, D_MODEL), _const((D_MODEL, D_MODEL)), _const((D_MODEL, D_MODEL)),
                  _const((MEM_LEN, D_MODEL)), _const((MEM_LEN, D_MODEL)), _const((1, D_MODEL)), _const((1, D_MODEL))],
        out_specs=[_rows(TM, D_MODEL), _rows(TM, D_MODEL), _rows(TM, 1), _rows(TM, D_MODEL)],
        out_shape=[jax.ShapeDtypeStruct((t, D_MODEL), F32), jax.ShapeDtypeStruct((t, D_MODEL), F32),
                   jax.ShapeDtypeStruct((t, 1), F32), jax.ShapeDtypeStruct((t, D_MODEL), MXU_DTYPE)],
        compiler_params=_cparams(("arbitrary",)),
    )(h1, wq, wo, kb, vb, g, b)


def _attn_ln_bwd(dh2, xhat, rstd, g, h1, wq, wo, kb, vb):
    t = h1.shape[0]

    def body(dh_ref, xh_ref, rs_ref, g_ref, h_ref, wq_ref, wo_ref, k_ref, v_ref,
             dr_ref, dq_ref, dh1_ref, dk_ref, dv_ref, dg_ref, db_ref):
        i = pl.program_id(0)

        @pl.when(i == 0)
        def _():
            dk_ref[...] = jnp.zeros_like(dk_ref)
            dv_ref[...] = jnp.zeros_like(dv_ref)
            dg_ref[...] = jnp.zeros_like(dg_ref)
            db_ref[...] = jnp.zeros_like(db_ref)

        dout = dh_ref[...]
        xh = xh_ref[...]
        dg_ref[...] += _sum0(dout * xh)
        db_ref[...] += _sum0(dout)
        dr = _ln_bwd(dout, xh, rs_ref[...], g_ref[...])
        dr_ref[...] = dr.astype(dr_ref.dtype)
        do = _dot_nt(dr, wo_ref[...])
        h = h_ref[...]
        q = _dot(h, wq_ref[...])
        kb_ = k_ref[...]
        vb_ = v_ref[...]
        scale = 1.0 / math.sqrt(XA_HEAD_DIM)
        for hh in range(XA_HEADS):
            sl = slice(hh * XA_HEAD_DIM, (hh + 1) * XA_HEAD_DIM)
            p = _attn_probs(q, kb_, hh)
            do_h = do[:, sl]
            dp = _dot_nt(do_h, vb_[:, sl])
            ds = p * (dp - jnp.sum(dp * p, axis=-1, keepdims=True)) * scale
            dq_ref[:, sl] = _dot(ds, kb_[:, sl]).astype(dq_ref.dtype)
            dk_ref[:, sl] += _dot_tn(ds, q[:, sl])
            dv_ref[:, sl] += _dot_tn(p, do_h)
        dh1_ref[...] = ALPHA * dr + _dot_nt(dq_ref[...], wq_ref[...])

    return pl.pallas_call(
        body, name="attn_ln_bwd", grid=(t // TM,),
        in_specs=[_rows(TM, D_MODEL), _rows(TM, D_MODEL), _rows(TM, 1), _const((1, D_MODEL)), _rows(TM, D_MODEL),
                  _const((D_MODEL, D_MODEL)), _const((D_MODEL, D_MODEL)), _const((MEM_LEN, D_MODEL)),
                  _const((MEM_LEN, D_MODEL))],
        out_specs=[_rows(TM, D_MODEL), _rows(TM, D_MODEL), _rows(TM, D_MODEL), _const((MEM_LEN, D_MODEL)),
                   _const((MEM_LEN, D_MODEL)), _const((1, D_MODEL)), _const((1, D_MODEL))],
        out_shape=[jax.ShapeDtypeStruct((t, D_MODEL), MXU_DTYPE), jax.ShapeDtypeStruct((t, D_MODEL), MXU_DTYPE),
                   jax.ShapeDtypeStruct((t, D_MODEL), F32), jax.ShapeDtypeStruct((MEM_LEN, D_MODEL), F32),
                   jax.ShapeDtypeStruct((MEM_LEN, D_MODEL), F32), jax.ShapeDtypeStruct((1, D_MODEL), F32),
                   jax.ShapeDtypeStruct((1, D_MODEL), F32)],
        compiler_params=_cparams(("arbitrary",)),
    )(dh2, xhat, rstd, g, h1, wq, wo, kb, vb)


FF_CHUNK = 1024
N_FF = D_FF // FF_CHUNK


def _load_resident(pairs, sems):
    copies = [pltpu.make_async_copy(src, dst, sems.at[k]) for k, (src, dst) in enumerate(pairs)]
    for cp in copies:
        cp.start()
    for cp in copies:
        cp.wait()


def _mlp_ln_fwd(h2, w1, w2, g, b):
    t = h2.shape[0]

    def body(h_ref, w1_hbm, w2_hbm, g_ref, b_ref, hn_ref, xh_ref, rs_ref, hd_ref, w1_v, w2_v, acc_ref, sems):
        @pl.when(pl.program_id(0) == 0)
        def _():
            _load_resident([(w1_hbm, w1_v), (w2_hbm, w2_v)], sems)

        h = h_ref[...]
        hb = h.astype(MXU_DTYPE)
        acc_ref[...] = ALPHA * h
        for j in range(N_FF):
            sl = slice(j * FF_CHUNK, (j + 1) * FF_CHUNK)
            u = _dot(hb, w1_v[:, sl])
            hd = jnp.square(jnp.maximum(u, 0.0)).astype(MXU_DTYPE)
            hd_ref[:, sl] = hd
            acc_ref[...] += _dot(hd, w2_v[sl, :])
        out, xhat, rstd = _ln_fwd(acc_ref[...], g_ref[...], b_ref[...])
        hn_ref[...] = out
        xh_ref[...] = xhat
        rs_ref[...] = rstd

    return pl.pallas_call(
        body, name="mlp_ln_fwd", grid=(t // TM,),
        in_specs=[_rows(TM, D_MODEL), _hbm(), _hbm(), _const((1, D_MODEL)), _const((1, D_MODEL))],
        out_specs=[_rows(TM, D_MODEL), _rows(TM, D_MODEL), _rows(TM, 1), _rows(TM, D_FF)],
        out_shape=[jax.ShapeDtypeStruct((t, D_MODEL), F32), jax.ShapeDtypeStruct((t, D_MODEL), F32),
                   jax.ShapeDtypeStruct((t, 1), F32), jax.ShapeDtypeStruct((t, D_FF), MXU_DTYPE)],
        scratch_shapes=[pltpu.VMEM((D_MODEL, D_FF), MXU_DTYPE), pltpu.VMEM((D_FF, D_MODEL), MXU_DTYPE),
                        pltpu.VMEM((TM, D_MODEL), F32), pltpu.SemaphoreType.DMA((2,))],
        compiler_params=_cparams(("arbitrary",)),
    )(h2, w1, w2, g, b)


def _mlp_ln_bwd(dh3, xhat, rstd, g, hdn, w1, w2):
    t = dh3.shape[0]

    def body(dh_ref, xh_ref, rs_ref, g_ref, hd_ref, w1_hbm, w2_hbm,
             dr_ref, du_ref, dh2_ref, dg_ref, db_ref, w1_v, w2_v, acc_ref, sems):
        @pl.when(pl.program_id(0) == 0)
        def _():
            _load_resident([(w1_hbm, w1_v), (w2_hbm, w2_v)], sems)
            dg_ref[...] = jnp.zeros_like(dg_ref)
            db_ref[...] = jnp.zeros_like(db_ref)

        dout = dh_ref[...]
        xh = xh_ref[...]
        dg_ref[...] += _sum0(dout * xh)
        db_ref[...] += _sum0(dout)
        dr = _ln_bwd(dout, xh, rs_ref[...], g_ref[...])
        drb = dr.astype(MXU_DTYPE)
        dr_ref[...] = drb
        acc_ref[...] = ALPHA * dr
        for j in range(N_FF):
            sl = slice(j * FF_CHUNK, (j + 1) * FF_CHUNK)
            dhd = _dot_nt(drb, w2_v[sl, :])
            du = (dhd * (2.0 * jnp.sqrt(hd_ref[:, sl].astype(F32)))).astype(MXU_DTYPE)
            du_ref[:, sl] = du
            acc_ref[...] += _dot_nt(du, w1_v[:, sl])
        dh2_ref[...] = acc_ref[...]

    tm = TM // 2
    return pl.pallas_call(
        body, name="mlp_ln_bwd", grid=(t // tm,),
        in_specs=[_rows(tm, D_MODEL), _rows(tm, D_MODEL), _rows(tm, 1), _const((1, D_MODEL)), _rows(tm, D_FF),
                  _hbm(), _hbm()],
        out_specs=[_rows(tm, D_MODEL), _rows(tm, D_FF), _rows(tm, D_MODEL), _const((1, D_MODEL)),
                   _const((1, D_MODEL))],
        out_shape=[jax.ShapeDtypeStruct((t, D_MODEL), MXU_DTYPE), jax.ShapeDtypeStruct((t, D_FF), MXU_DTYPE),
                   jax.ShapeDtypeStruct((t, D_MODEL), F32), jax.ShapeDtypeStruct((1, D_MODEL), F32),
                   jax.ShapeDtypeStruct((1, D_MODEL), F32)],
        scratch_shapes=[pltpu.VMEM((D_MODEL, D_FF), MXU_DTYPE), pltpu.VMEM((D_FF, D_MODEL), MXU_DTYPE),
                        pltpu.VMEM((tm, D_MODEL), F32), pltpu.SemaphoreType.DMA((2,))],
        compiler_params=_cparams(("arbitrary",)),
    )(dh3, xhat, rstd, g, hdn, w1, w2)


def _outproj_ln_bwd(dh1, xhat, rstd, g, w):
    t = dh1.shape[0]

    def body(dh_ref, xh_ref, rs_ref, g_ref, w_ref, dr_ref, res_ref, dy_ref, dg_ref, db_ref):
        i = pl.program_id(0)

        @pl.when(i == 0)
        def _():
            dg_ref[...] = jnp.zeros_like(dg_ref)
            db_ref[...] = jnp.zeros_like(db_ref)

        dout = dh_ref[...]
        xh = xh_ref[...]
        dg_ref[...] += _sum0(dout * xh)
        db_ref[...] += _sum0(dout)
        dr = _ln_bwd(dout, xh, rs_ref[...], g_ref[...])
        dr_ref[...] = dr.astype(dr_ref.dtype)
        res_ref[...] = ALPHA * dr
        dy_ref[...] = _dot_nt(dr, w_ref[...])

    return pl.pallas_call(
        body, name="outproj_ln_bwd", grid=(t // TM,),
        in_specs=[_rows(TM, D_MODEL), _rows(TM, D_MODEL), _rows(TM, 1), _const((1, D_MODEL)),
                  _const((D_MODEL, D_MODEL))],
        out_specs=[_rows(TM, D_MODEL), _rows(TM, D_MODEL), _rows(TM, D_MODEL), _const((1, D_MODEL)),
                   _const((1, D_MODEL))],
        out_shape=[jax.ShapeDtypeStruct((t, D_MODEL), MXU_DTYPE), jax.ShapeDtypeStruct((t, D_MODEL), F32),
                   jax.ShapeDtypeStruct((t, D_MODEL), F32), jax.ShapeDtypeStruct((1, D_MODEL), F32),
                   jax.ShapeDtypeStruct((1, D_MODEL), F32)],
        compiler_params=_cparams(("arbitrary",)),
    )(dh1, xhat, rstd, g, w)


def _loss_fwd_bwd(h, target):
    t = h.shape[0]

    def body(h_ref, t_ref, l_ref, dh_ref):
        i = pl.program_id(0)

        @pl.when(i == 0)
        def _():
            l_ref[...] = jnp.zeros_like(l_ref)

        e = h_ref[...] - t_ref[...]
        dh_ref[...] = e * (1.0 / D_MODEL)
        per_tok = jnp.mean(e * e, axis=-1, keepdims=True)
        l_ref[...] += 0.5 * jnp.sum(per_tok, axis=0, keepdims=True)

    return pl.pallas_call(
        body, name="loss_fwd_bwd", grid=(t // TM,),
        in_specs=[_rows(TM, D_MODEL), _rows(TM, D_MODEL)],
        out_specs=[_const((1, 1)), _rows(TM, D_MODEL)],
        out_shape=[jax.ShapeDtypeStruct((1, 1), F32), jax.ShapeDtypeStruct((t, D_MODEL), F32)],
        compiler_params=_cparams(("arbitrary",)),
    )(h, target)


def _pick_col(x, idx):
    lane = lax.broadcasted_iota(jnp.int32, x.shape, 1)
    return jnp.sum(jnp.where(lane == idx, x, 0.0), axis=1, keepdims=True)


def _pick_row(x, idx):
    sub = lax.broadcasted_iota(jnp.int32, x.shape, 0)
    return jnp.sum(jnp.where(sub == idx, x, 0.0), axis=0, keepdims=True)


def _conv_taps(pad_ref, w, tm, base):
    acc = w[0:1, :] * pad_ref[base:base + tm, :]
    for k in range(1, 4):
        acc = acc + w[k:k + 1, :] * pad_ref[base + k:base + k + tm, :]
    return acc


def _ssd_chunk_common(adt_c, tri):
    cs = _dot_f32(tri, adt_c)
    return cs, cs.T, jnp.exp(cs)


def _ssd_head_terms(cs, cst, ecs, dt_c, h, tri):
    cs_col = _pick_col(cs, h)
    cs_row = _pick_row(cst, h)
    dt_col = _pick_col(dt_c, h)
    cs_last = cs_col[SSD_CHUNK - 1:SSD_CHUNK, :]
    lmat = jnp.exp(jnp.where(tri > 0.0, cs_col - cs_row, -1e30))
    ecs_col = _pick_col(ecs, h)
    decay_col = jnp.exp(cs_last - cs_col)
    return cs_col, dt_col, cs_last, lmat, ecs_col, decay_col


def _ssd_fwd(proj, cw, cb, dtb, a_neg, d_lanes, nw):
    t = proj.shape[0]
    tm = SSD_TM
    nt = t // tm
    ncq = tm // SSD_CHUNK
    hb = tm // SUBLANES

    def body(xbc_ref, halo_ref, z_ref, dt_ref, cw_ref, cb_ref, dtb_ref, a_ref, d_ref, nw_ref,
             y_ref, yy_ref, st_ref, xpad, xact, state):
        i = pl.program_id(0)

        @pl.when(i == 0)
        def _():
            state[...] = jnp.zeros_like(state)

        xpad[0:SUBLANES, :] = jnp.where(i > 0, halo_ref[...], 0.0)
        xpad[SUBLANES:SUBLANES + tm, :] = xbc_ref[...]
        acc = cb_ref[...] + _conv_taps(xpad, cw_ref[...], tm, SUBLANES - 3)
        xact[...] = acc * _sigmoid(acc)
        dt = _softplus(dt_ref[...] + dtb_ref[...])
        adt = dt * a_ref[...]
        r_i = lax.broadcasted_iota(jnp.int32, (SSD_CHUNK, SSD_CHUNK), 0)
        c_i = lax.broadcasted_iota(jnp.int32, (SSD_CHUNK, SSD_CHUNK), 1)
        tri = (r_i >= c_i).astype(F32)
        lane1 = lax.broadcasted_iota(jnp.int32, (1, LANES), 1)
        for c in range(ncq):
            sl = slice(c * SSD_CHUNK, (c + 1) * SSD_CHUNK)
            dt_c = dt[sl]
            cs, cst, ecs = _ssd_chunk_common(adt[sl], tri)
            for g in range(2):
                bg = xact[sl, 512 + g * 128:512 + (g + 1) * 128]
                cg = xact[sl, 768 + g * 128:768 + (g + 1) * 128]
                cbm = _dot_nt(cg, bg)
                for pr in range(2):
                    pi = g * 2 + pr
                    psl = slice(pi * 128, (pi + 1) * 128)
                    xp = xact[sl, psl]
                    prev = state[pi]
                    st_ref[c, pi] = prev
                    yp = xp * d_ref[:, psl]
                    new_s = jnp.zeros((SSD_STATE, LANES), F32)
                    dec_lane = jnp.zeros((1, LANES), F32)
                    for hh in range(2):
                        h = g * 4 + pr * 2 + hh
                        lm = (lane1 >= 64) if hh else (lane1 < 64)
                        _, dt_col, cs_last, lmat, ecs_col, decay_col = _ssd_head_terms(cs, cst, ecs, dt_c, h, tri)
                        xdt = jnp.where(lm, xp, 0.0) * dt_col
                        yp = yp + _dot(cbm * lmat, xdt)
                        yp = yp + _dot(cg * ecs_col, jnp.where(lm, prev, 0.0))
                        new_s = new_s + _dot_tn(bg * decay_col, xdt)
                        dec_lane = dec_lane + jnp.where(lm, jnp.exp(cs_last), 0.0)
                    state[pi] = prev * dec_lane + new_s
                    yy_ref[sl, psl] = yp
        yy = yy_ref[...]
        z = z_ref[...]
        yg = yy * (z * _sigmoid(z))
        ms = jnp.mean(yg * yg, axis=-1, keepdims=True)
        y_ref[...] = yg * lax.rsqrt(ms + LN_EPS) * nw_ref[...]

    halo_map = lambda i: (jnp.maximum(i * hb - 1, 0), 0)
    return pl.pallas_call(
        body, name="ssd_fwd", grid=(nt,),
        in_specs=[pl.BlockSpec((tm, SSD_XBC), lambda i: (i, 0)), pl.BlockSpec((SUBLANES, SSD_XBC), halo_map),
                  pl.BlockSpec((tm, SSD_WIDTH), lambda i: (i, P_Z // SSD_WIDTH)),
                  pl.BlockSpec((tm, LANES), lambda i: (i, P_DT // LANES)),
                  _const((4, SSD_XBC)), _const((1, SSD_XBC)), _const((1, LANES)), _const((1, LANES)),
                  _const((1, SSD_WIDTH)), _const((1, SSD_WIDTH))],
        out_specs=[_rows(tm, SSD_WIDTH), _rows(tm, SSD_WIDTH),
                   pl.BlockSpec((ncq, 4, SSD_STATE, LANES), lambda i: (i, 0, 0, 0))],
        out_shape=[jax.ShapeDtypeStruct((t, SSD_WIDTH), F32), jax.ShapeDtypeStruct((t, SSD_WIDTH), F32),
                   jax.ShapeDtypeStruct((t // SSD_CHUNK, 4, SSD_STATE, LANES), F32)],
        scratch_shapes=[pltpu.VMEM((tm + SUBLANES, SSD_XBC), F32), pltpu.VMEM((tm, SSD_XBC), F32),
                        pltpu.VMEM((4, SSD_STATE, LANES), F32)],
        compiler_params=_cparams(("arbitrary",)),
    )(proj, proj, proj, proj, cw, cb, dtb, a_neg, d_lanes, nw)


def _ssd_bwd(dycat, proj, yy, states, cw, cb, dtb, a_neg, d_lanes, nw):
    t = proj.shape[0]
    tm = SSD_TM
    nt = t // tm
    ncq = tm // SSD_CHUNK
    hb = tm // SUBLANES

    def body(dy_ref, xbc_ref, halo_ref, z_ref, dt_ref, yy_ref, st_ref, cw_ref, cb_ref, dtb_ref, a_ref, d_ref, nw_ref,
             dxbc_ref, dz_ref, ddt_ref, dcw_ref, dcb_ref, ddtb_ref, da_ref, dd_ref, dnw_ref,
             xpad, xact, dxact, dpad, dstate, dnext):
        i = pl.program_id(0)

        @pl.when(i == 0)
        def _():
            for r in (dcw_ref, dcb_ref, ddtb_ref, da_ref, dd_ref, dnw_ref, dstate, dnext):
                r[...] = jnp.zeros_like(r)

        xpad[0:SUBLANES, :] = jnp.where(i < nt - 1, halo_ref[...], 0.0)
        xpad[SUBLANES:SUBLANES + tm, :] = xbc_ref[...]
        cw_v = cw_ref[...]
        acc = cb_ref[...] + _conv_taps(xpad, cw_v, tm, SUBLANES - 3)
        sig = _sigmoid(acc)
        xact[...] = acc * sig
        dt_raw = dt_ref[...] + dtb_ref[...]
        dt = _softplus(dt_raw)
        a_v = a_ref[...]
        adt = dt * a_v
        yy = yy_ref[...]
        z = z_ref[...]
        sz = _sigmoid(z)
        siluz = z * sz
        yg = yy * siluz
        ms = jnp.mean(yg * yg, axis=-1, keepdims=True)
        rinv = lax.rsqrt(ms + LN_EPS)
        dout = dy_ref[...]
        dnw_ref[...] += _sum0(dout * yg * rinv)
        dyn = dout * nw_ref[...]
        dyg = rinv * dyn - yg * (rinv * rinv * rinv) * jnp.mean(dyn * yg, axis=-1, keepdims=True)
        dyy = dyg * siluz
        dz_ref[...] = dyg * yy * (sz * (1.0 + z * (1.0 - sz)))
        dd_ref[...] += _sum0(dyy * xact[:, 0:SSD_WIDTH])

        r_i = lax.broadcasted_iota(jnp.int32, (SSD_CHUNK, SSD_CHUNK), 0)
        c_i = lax.broadcasted_iota(jnp.int32, (SSD_CHUNK, SSD_CHUNK), 1)
        tri = (r_i >= c_i).astype(F32)
        lane1 = lax.broadcasted_iota(jnp.int32, (1, LANES), 1)
        for c in reversed(range(ncq)):
            sl = slice(c * SSD_CHUNK, (c + 1) * SSD_CHUNK)
            dt_c = dt[sl]
            cs, cst, ecs = _ssd_chunk_common(adt[sl], tri)
            cacc = jnp.zeros((SSD_CHUNK, LANES), F32)
            racc = jnp.zeros((SSD_CHUNK, LANES), F32)
            ddtx = jnp.zeros((SSD_CHUNK, LANES), F32)
            for g in range(2):
                bg = xact[sl, 512 + g * 128:512 + (g + 1) * 128]
                cg = xact[sl, 768 + g * 128:768 + (g + 1) * 128]
                cbm = _dot_nt(cg, bg)
                dcb_m = jnp.zeros((SSD_CHUNK, SSD_CHUNK), F32)
                dbg = jnp.zeros((SSD_CHUNK, SSD_STATE), F32)
                dcg = jnp.zeros((SSD_CHUNK, SSD_STATE), F32)
                for pr in range(2):
                    pi = g * 2 + pr
                    psl = slice(pi * 128, (pi + 1) * 128)
                    xp = xact[sl, psl]
                    dyp = dyy[sl, psl]
                    prev = st_ref[c, pi]
                    ds_all = dstate[pi]
                    dxdt_p = jnp.zeros((SSD_CHUNK, LANES), F32)
                    dprev_new = jnp.zeros((SSD_STATE, LANES), F32)
                    dec_lane = jnp.zeros((1, LANES), F32)
                    dt_lanes = jnp.zeros((SSD_CHUNK, LANES), F32)
                    for hh in range(2):
                        h = g * 4 + pr * 2 + hh
                        lm = (lane1 >= 64) if hh else (lane1 < 64)
                        oh_l = (c_i == h).astype(F32)
                        oh_s = (r_i == h).astype(F32)
                        _, dt_col, cs_last, lmat, ecs_col, decay_col = _ssd_head_terms(cs, cst, ecs, dt_c, h, tri)
                        gm = cbm * lmat
                        xm = jnp.where(lm, xp, 0.0)
                        xdt = xm * dt_col
                        dym = jnp.where(lm, dyp, 0.0)
                        prevm = jnp.where(lm, prev, 0.0)
                        dsm = jnp.where(lm, ds_all, 0.0)
                        bdec = bg * decay_col
                        dxdt = _dot_tn(gm, dym) + _dot(bdec, dsm)
                        dxdt_p = dxdt_p + dxdt
                        ddtx = ddtx + oh_l * jnp.sum(dxdt * xm, axis=1, keepdims=True)
                        dt_lanes = dt_lanes + jnp.where(lm, dt_col, 0.0)
                        dgm = _dot_nt(dym, xdt)
                        dcb_m = dcb_m + dgm * lmat
                        w = dgm * gm
                        cacc = cacc + oh_l * jnp.sum(w, axis=1, keepdims=True)
                        racc = racc - oh_s * jnp.sum(w, axis=0, keepdims=True)
                        dce = _dot_nt(dym, prevm)
                        dcg = dcg + dce * ecs_col
                        cacc = cacc + oh_l * (jnp.sum(dce * cg, axis=1, keepdims=True) * ecs_col)
                        dprev_new = dprev_new + _dot_tn(cg * ecs_col, dym)
                        dbdec = _dot_nt(xdt, dsm)
                        dbg = dbg + dbdec * decay_col
                        dd = jnp.sum(dbdec * bg, axis=1, keepdims=True) * decay_col
                        cacc = cacc - oh_l * dd
                        cd = jnp.exp(cs_last)
                        dlast = jnp.sum(dd, axis=0, keepdims=True) + jnp.sum(
                            jnp.sum(dsm * prevm, axis=1, keepdims=True), axis=0, keepdims=True) * cd
                        cacc = cacc + jnp.where((r_i == SSD_CHUNK - 1) & (c_i == h), dlast, 0.0)
                        dec_lane = dec_lane + jnp.where(lm, cd, 0.0)
                    dstate[pi] = ds_all * dec_lane + dprev_new
                    dxact[sl, psl] = dxdt_p * dt_lanes + dyp * d_ref[:, psl]
                dcg = dcg + _dot(dcb_m, bg)
                dbg = dbg + _dot_tn(dcb_m, cg)
                dxact[sl, 512 + g * 128:512 + (g + 1) * 128] = dbg
                dxact[sl, 768 + g * 128:768 + (g + 1) * 128] = dcg
            dcs = cacc + racc.T
            dadt = _dot_f32((r_i <= c_i).astype(F32), dcs)
            ddt = dadt * a_v + ddtx
            da_ref[...] += _sum0(dadt * dt_c)
            ddt_raw = ddt * _sigmoid(dt_raw[sl])
            ddt_ref[sl, :] = ddt_raw
            ddtb_ref[...] += _sum0(ddt_raw)
        dacc = dxact[...] * (sig * (1.0 + acc * (1.0 - sig)))
        dcb_ref[...] += _sum0(dacc)
        for k in range(4):
            dcw_ref[k:k + 1, :] += _sum0(dacc * xpad[SUBLANES - 3 + k:SUBLANES - 3 + k + tm, :])
        dpad[0:tm, :] = dacc
        dpad[tm:tm + SUBLANES, :] = dnext[...]
        dx = cw_v[0:1, :] * dpad[3:3 + tm, :]
        for k in range(1, 4):
            dx = dx + cw_v[k:k + 1, :] * dpad[3 - k:3 - k + tm, :]
        dxbc_ref[...] = dx
        dnext[...] = dacc[0:SUBLANES, :]

    rev = lambda i: nt - 1 - i
    halo_map = lambda i: (jnp.maximum(rev(i) * hb - 1, 0), 0)
    rrow = lambda n, col=0: pl.BlockSpec((tm, n), lambda i: (rev(i), col))
    return pl.pallas_call(
        body, name="ssd_bwd", grid=(nt,),
        in_specs=[rrow(SSD_WIDTH), rrow(SSD_XBC), pl.BlockSpec((SUBLANES, SSD_XBC), halo_map),
                  rrow(SSD_WIDTH, P_Z // SSD_WIDTH), rrow(LANES, P_DT // LANES), rrow(SSD_WIDTH),
                  pl.BlockSpec((ncq, 4, SSD_STATE, LANES), lambda i: (rev(i), 0, 0, 0)),
                  _const((4, SSD_XBC)), _const((1, SSD_XBC)), _const((1, LANES)), _const((1, LANES)),
                  _const((1, SSD_WIDTH)), _const((1, SSD_WIDTH))],
        out_specs=[rrow(SSD_XBC), rrow(SSD_WIDTH), rrow(LANES), _const((SUBLANES, SSD_XBC)), _const((1, SSD_XBC)),
                   _const((1, LANES)), _const((1, LANES)), _const((1, SSD_WIDTH)), _const((1, SSD_WIDTH))],
        out_shape=[jax.ShapeDtypeStruct((t, SSD_XBC), F32), jax.ShapeDtypeStruct((t, SSD_WIDTH), F32),
                   jax.ShapeDtypeStruct((t, LANES), F32), jax.ShapeDtypeStruct((SUBLANES, SSD_XBC), F32),
                   jax.ShapeDtypeStruct((1, SSD_XBC), F32), jax.ShapeDtypeStruct((1, LANES), F32),
                   jax.ShapeDtypeStruct((1, LANES), F32), jax.ShapeDtypeStruct((1, SSD_WIDTH), F32),
                   jax.ShapeDtypeStruct((1, SSD_WIDTH), F32)],
        scratch_shapes=[pltpu.VMEM((tm + SUBLANES, SSD_XBC), F32), pltpu.VMEM((tm, SSD_XBC), F32),
                        pltpu.VMEM((tm, SSD_XBC), F32), pltpu.VMEM((tm + SUBLANES, SSD_XBC), F32),
                        pltpu.VMEM((4, SSD_STATE, LANES), F32), pltpu.VMEM((SUBLANES, SSD_XBC), F32)],
        compiler_params=_cparams(("arbitrary",)),
    )(dycat, proj, proj, proj, proj, yy, states, cw, cb, dtb, a_neg, d_lanes, nw)


def _cmul_add(ar, ai, br, bi, cr, ci):
    return ar + br * cr - bi * ci, ai + br * ci + bi * cr


def _s5_fwd(proj, bre, bim, cre, cim, d_skip, glu_w, glu_b, coef):
    t = proj.shape[0]
    tm = SCAN_TM
    ng = tm // SUBLANES

    def body(u_ref, bre_ref, bim_ref, cre_ref, cim_ref, d_ref, w_ref, b_ref, coef_ref,
             y_ref, y2_ref, hre_ref, him_ref, carry):
        i = pl.program_id(0)

        @pl.when(i == 0)
        def _():
            carry[...] = jnp.zeros_like(carry)

        u = u_ref[...]
        hre_ref[...] = _dot(u, bre_ref[...])
        him_ref[...] = _dot(u, bim_ref[...])

        def step(gi, car):
            cr_, ci_ = car
            rows = pl.ds(pl.multiple_of(gi * SUBLANES, SUBLANES), SUBLANES)
            r = hre_ref[rows, :]
            m = him_ref[rows, :]
            for k, sh in enumerate((1, 2, 4)):
                r, m = _cmul_add(r, m, coef_ref[k, 0], coef_ref[k, 1], pltpu.roll(r, sh, 0), pltpu.roll(m, sh, 0))
            r, m = _cmul_add(r, m, coef_ref[3, 0], coef_ref[3, 1], cr_, ci_)
            hre_ref[rows, :] = r
            him_ref[rows, :] = m
            return (jnp.broadcast_to(r[SUBLANES - 1:SUBLANES, :], r.shape),
                    jnp.broadcast_to(m[SUBLANES - 1:SUBLANES, :], m.shape))

        cr_, ci_ = lax.fori_loop(0, ng, step, (carry[0], carry[1]))
        carry[0] = cr_
        carry[1] = ci_
        y2 = _dot(hre_ref[...], cre_ref[...]) - _dot(him_ref[...], cim_ref[...]) + d_ref[...] * u
        y2_ref[...] = y2
        ya = _gelu(y2)
        y_ref[...] = ya * _sigmoid(_dot(ya, w_ref[...]) + b_ref[...])

    return pl.pallas_call(
        body, name="s5_fwd", grid=(t // tm,),
        in_specs=[pl.BlockSpec((tm, S5_WIDTH), lambda i: (i, P_U // S5_WIDTH)),
                  _const((S5_WIDTH, S5_NSTATE)), _const((S5_WIDTH, S5_NSTATE)), _const((S5_NSTATE, S5_WIDTH)),
                  _const((S5_NSTATE, S5_WIDTH)), _const((1, S5_WIDTH)), _const((S5_WIDTH, S5_WIDTH)),
                  _const((1, S5_WIDTH)), _const((5, 2, SUBLANES, S5_NSTATE))],
        out_specs=[_rows(tm, S5_WIDTH), _rows(tm, S5_WIDTH), _rows(tm, S5_NSTATE), _rows(tm, S5_NSTATE)],
        out_shape=[jax.ShapeDtypeStruct((t, S5_WIDTH), F32), jax.ShapeDtypeStruct((t, S5_WIDTH), F32),
                   jax.ShapeDtypeStruct((t, S5_NSTATE), F32), jax.ShapeDtypeStruct((t, S5_NSTATE), F32)],
        scratch_shapes=[pltpu.VMEM((2, SUBLANES, S5_NSTATE), F32)],
        compiler_params=_cparams(("arbitrary",)),
    )(proj, bre, bim, cre, cim, d_skip, glu_w, glu_b, coef)


def _s5_bwd(dycat, proj, y2, hre, him, bre, bim, cre, cim, d_skip, glu_w, glu_b, rcoef):
    t = proj.shape[0]
    tm = SCAN_TM
    nt = t // tm
    ng = tm // SUBLANES
    hb = tm // SUBLANES

    def body(dy_ref, u_ref, y2_ref, hre_ref, him_ref, hre_halo, him_halo, bre_ref, bim_ref, cre_ref, cim_ref, d_ref,
             w_ref, b_ref, coef_ref,
             du_ref, dbre_ref, dbim_ref, dcre_ref, dcim_ref, dlam_ref, dd_ref, dw_ref, dgb_ref,
             gre, gim, hpre, hpim, carry):
        i = pl.program_id(0)

        @pl.when(i == 0)
        def _():
            for r in (dbre_ref, dbim_ref, dcre_ref, dcim_ref, dlam_ref, dd_ref, dw_ref, dgb_ref, carry):
                r[...] = jnp.zeros_like(r)

        u = u_ref[...]
        y2 = y2_ref[...]
        dout = dy_ref[...]
        ya = _gelu(y2)
        sg = _sigmoid(_dot(ya, w_ref[...]) + b_ref[...])
        dv = dout * ya * sg * (1.0 - sg)
        dya = dout * sg + _dot_nt(dv, w_ref[...])
        dw_ref[...] += _dot_tn(ya, dv)
        dgb_ref[...] += _sum0(dv)
        dy2 = dya * _gelu_grad(y2)
        dd_ref[...] += _sum0(dy2 * u)
        hre_v = hre_ref[...]
        him_v = him_ref[...]
        dcre_ref[...] += _dot_tn(hre_v, dy2)
        dcim_ref[...] -= _dot_tn(him_v, dy2)
        gre[...] = _dot_nt(dy2, cre_ref[...])
        gim[...] = -_dot_nt(dy2, cim_ref[...])
        first = i == nt - 1
        hpre[0:SUBLANES, :] = jnp.where(first, 0.0, hre_halo[...])
        hpim[0:SUBLANES, :] = jnp.where(first, 0.0, him_halo[...])
        hpre[SUBLANES:SUBLANES + tm, :] = hre_v
        hpim[SUBLANES:SUBLANES + tm, :] = him_v
        row0 = lax.broadcasted_iota(jnp.int32, (SUBLANES, S5_NSTATE), 0) == 0

        def step(k, car):
            cr_, ci_, dlr, dli = car
            gi = ng - 1 - k
            rows = pl.ds(pl.multiple_of(gi * SUBLANES, SUBLANES), SUBLANES)
            nrows = pl.ds(pl.multiple_of(gi * SUBLANES + SUBLANES, SUBLANES), SUBLANES)
            r = gre[rows, :]
            m = gim[rows, :]
            for kk, sh in enumerate((1, 2, 4)):
                r, m = _cmul_add(r, m, coef_ref[kk, 0], coef_ref[kk, 1], pltpu.roll(r, SUBLANES - sh, 0),
                                 pltpu.roll(m, SUBLANES - sh, 0))
            r, m = _cmul_add(r, m, coef_ref[3, 0], coef_ref[3, 1], cr_, ci_)
            gre[rows, :] = r
            gim[rows, :] = m
            pr_ = hpre[rows, :]
            pm_ = hpim[rows, :]
            hr_ = jnp.where(row0, jnp.broadcast_to(pr_[SUBLANES - 1:SUBLANES, :], pr_.shape),
                            pltpu.roll(hpre[nrows, :], 1, 0))
            hm_ = jnp.where(row0, jnp.broadcast_to(pm_[SUBLANES - 1:SUBLANES, :], pm_.shape),
                            pltpu.roll(hpim[nrows, :], 1, 0))
            dlr = dlr + hr_ * r + hm_ * m
            dli = dli + hr_ * m - hm_ * r
            return (jnp.broadcast_to(r[0:1, :], r.shape), jnp.broadcast_to(m[0:1, :], m.shape), dlr, dli)

        z8 = jnp.zeros((SUBLANES, S5_NSTATE), F32)
        cr_, ci_, dlr, dli = lax.fori_loop(0, ng, step, (carry[0], carry[1], z8, z8))
        carry[0] = cr_
        carry[1] = ci_
        dlam_ref[0] += dlr
        dlam_ref[1] += dli
        g_re = gre[...]
        g_im = gim[...]
        du_ref[...] = dy2 * d_ref[...] + _dot_nt(g_re, bre_ref[...]) + _dot_nt(g_im, bim_ref[...])
        dbre_ref[...] += _dot_tn(u, g_re)
        dbim_ref[...] += _dot_tn(u, g_im)

    rev = lambda i: nt - 1 - i
    rrow = lambda n, col=0: pl.BlockSpec((tm, n), lambda i: (rev(i), col))
    halo = pl.BlockSpec((SUBLANES, S5_NSTATE), lambda i: (jnp.maximum(rev(i) * hb - 1, 0), 0))
    return pl.pallas_call(
        body, name="s5_bwd", grid=(nt,),
        in_specs=[rrow(S5_WIDTH, 512 // S5_WIDTH), rrow(S5_WIDTH, P_U // S5_WIDTH), rrow(S5_WIDTH),
                  rrow(S5_NSTATE), rrow(S5_NSTATE), halo, halo,
                  _const((S5_WIDTH, S5_NSTATE)), _const((S5_WIDTH, S5_NSTATE)), _const((S5_NSTATE, S5_WIDTH)),
                  _const((S5_NSTATE, S5_WIDTH)), _const((1, S5_WIDTH)), _const((S5_WIDTH, S5_WIDTH)),
                  _const((1, S5_WIDTH)), _const((5, 2, SUBLANES, S5_NSTATE))],
        out_specs=[rrow(S5_WIDTH), _const((S5_WIDTH, S5_NSTATE)), _const((S5_WIDTH, S5_NSTATE)),
                   _const((S5_NSTATE, S5_WIDTH)), _const((S5_NSTATE, S5_WIDTH)), _const((2, SUBLANES, S5_NSTATE)),
                   _const((1, S5_WIDTH)), _const((S5_WIDTH, S5_WIDTH)), _const((1, S5_WIDTH))],
        out_shape=[jax.ShapeDtypeStruct((t, S5_WIDTH), F32), jax.ShapeDtypeStruct((S5_WIDTH, S5_NSTATE), F32),
                   jax.ShapeDtypeStruct((S5_WIDTH, S5_NSTATE), F32), jax.ShapeDtypeStruct((S5_NSTATE, S5_WIDTH), F32),
                   jax.ShapeDtypeStruct((S5_NSTATE, S5_WIDTH), F32),
                   jax.ShapeDtypeStruct((2, SUBLANES, S5_NSTATE), F32), jax.ShapeDtypeStruct((1, S5_WIDTH), F32),
                   jax.ShapeDtypeStruct((S5_WIDTH, S5_WIDTH), F32), jax.ShapeDtypeStruct((1, S5_WIDTH), F32)],
        scratch_shapes=[pltpu.VMEM((tm, S5_NSTATE), F32), pltpu.VMEM((tm, S5_NSTATE), F32),
                        pltpu.VMEM((tm + SUBLANES, S5_NSTATE), F32), pltpu.VMEM((tm + SUBLANES, S5_NSTATE), F32),
                        pltpu.VMEM((2, SUBLANES, S5_NSTATE), F32)],
        compiler_params=_cparams(("arbitrary",)),
    )(dycat, proj, y2, hre, him, hre, him, bre, bim, cre, cim, d_skip, glu_w, glu_b, rcoef)


def _rg_gates(xc, wa, ba, wx, bx, nsp):
    r = _sigmoid(_dot(xc, wa) + ba)
    ig = _sigmoid(_dot(xc, wx) + bx)
    log_a = nsp * r
    a = jnp.exp(log_a)
    mult = jnp.sqrt(-_expm1(2.0 * log_a))
    return r, ig, a, mult


def _rg_fwd(proj, cw, cb, wa, ba, wx, bx, nsp):
    t = proj.shape[0]
    tm = SCAN_TM
    ng = tm // SUBLANES
    hb = tm // SUBLANES

    def body(x_ref, halo_ref, gt_ref, cw_ref, cb_ref, wa_ref, ba_ref, wx_ref, bx_ref, nsp_ref,
             y_ref, h_ref, xpad, abuf, carry):
        i = pl.program_id(0)

        @pl.when(i == 0)
        def _():
            carry[...] = jnp.zeros_like(carry)

        xpad[0:SUBLANES, :] = jnp.where(i > 0, halo_ref[...], 0.0)
        xpad[SUBLANES:SUBLANES + tm, :] = x_ref[...]
        xc = cb_ref[...] + _conv_taps(xpad, cw_ref[...], tm, SUBLANES - 3)
        _, ig, a, mult = _rg_gates(xc, wa_ref[...], ba_ref[...], wx_ref[...], bx_ref[...], nsp_ref[...])
        abuf[...] = a
        h_ref[...] = mult * (ig * xc)
        sub = lax.broadcasted_iota(jnp.int32, (SUBLANES, RG_WIDTH), 0)

        def step(gi, car):
            rows = pl.ds(pl.multiple_of(gi * SUBLANES, SUBLANES), SUBLANES)
            av = abuf[rows, :]
            bv = h_ref[rows, :]
            for sh in (1, 2, 4):
                m = sub >= sh
                bv = jnp.where(m, av * pltpu.roll(bv, sh, 0) + bv, bv)
                av = jnp.where(m, av * pltpu.roll(av, sh, 0), av)
            hv = bv + av * car
            h_ref[rows, :] = hv
            return jnp.broadcast_to(hv[SUBLANES - 1:SUBLANES, :], hv.shape)

        carry[...] = lax.fori_loop(0, ng, step, carry[...])
        y_ref[...] = h_ref[...] * _gelu(gt_ref[...])

    return pl.pallas_call(
        body, name="rg_fwd", grid=(t // tm,),
        in_specs=[pl.BlockSpec((tm, RG_WIDTH), lambda i: (i, P_XRG // RG_WIDTH)),
                  pl.BlockSpec((SUBLANES, RG_WIDTH), lambda i: (jnp.maximum(i * hb - 1, 0), P_XRG // RG_WIDTH)),
                  pl.BlockSpec((tm, RG_WIDTH), lambda i: (i, P_GRG // RG_WIDTH)),
                  _const((4, RG_WIDTH)), _const((1, RG_WIDTH)), _const((RG_WIDTH, RG_WIDTH)), _const((1, RG_WIDTH)),
                  _const((RG_WIDTH, RG_WIDTH)), _const((1, RG_WIDTH)), _const((1, RG_WIDTH))],
        out_specs=[_rows(tm, RG_WIDTH), _rows(tm, RG_WIDTH)],
        out_shape=[jax.ShapeDtypeStruct((t, RG_WIDTH), F32), jax.ShapeDtypeStruct((t, RG_WIDTH), F32)],
        scratch_shapes=[pltpu.VMEM((tm + SUBLANES, RG_WIDTH), F32), pltpu.VMEM((tm, RG_WIDTH), F32),
                        pltpu.VMEM((SUBLANES, RG_WIDTH), F32)],
        compiler_params=_cparams(("arbitrary",)),
    )(proj, proj, proj, cw, cb, wa, ba, wx, bx, nsp)


def _rg_bwd(dycat, proj, hs, cw, cb, wa, ba, wx, bx, nsp):
    t = proj.shape[0]
    tm = SCAN_TM
    nt = t // tm
    ng = tm // SUBLANES
    hb = tm // SUBLANES

    def body(dy_ref, x_ref, halo_ref, gt_ref, h_ref, h_halo, cw_ref, cb_ref, wa_ref, ba_ref, wx_ref, bx_ref, nsp_ref,
             dx_ref, dgt_ref, dcw_ref, dcb_ref, dwa_ref, dba_ref, dwx_ref, dbx_ref, dnsp_ref,
             xpad, abuf, gbuf, hpad, dabuf, dpad, carry, dnext):
        i = pl.program_id(0)

        @pl.when(i == 0)
        def _():
            for r in (dcw_ref, dcb_ref, dwa_ref, dba_ref, dwx_ref, dbx_ref, dnsp_ref, carry, dnext):
                r[...] = jnp.zeros_like(r)

        first = i == nt - 1
        xpad[0:SUBLANES, :] = jnp.where(first, 0.0, halo_ref[...])
        xpad[SUBLANES:SUBLANES + tm, :] = x_ref[...]
        cw_v = cw_ref[...]
        xc = cb_ref[...] + _conv_taps(xpad, cw_v, tm, SUBLANES - 3)
        nsp_v = nsp_ref[...]
        r, ig, a, mult = _rg_gates(xc, wa_ref[...], ba_ref[...], wx_ref[...], bx_ref[...], nsp_v)
        abuf[...] = a
        hv = h_ref[...]
        hpad[0:SUBLANES, :] = jnp.where(first, 0.0, h_halo[...])
        hpad[SUBLANES:SUBLANES + tm, :] = hv
        gt = gt_ref[...]
        dout = dy_ref[...]
        dgt_ref[...] = dout * hv * _gelu_grad(gt)
        gbuf[...] = dout * _gelu(gt)
        sub = lax.broadcasted_iota(jnp.int32, (SUBLANES, RG_WIDTH), 0)
        last_row = sub == SUBLANES - 1
        row0 = sub == 0

        def step(k, car):
            gi = ng - 1 - k
            rows = pl.ds(pl.multiple_of(gi * SUBLANES, SUBLANES), SUBLANES)
            nrows = pl.ds(pl.multiple_of(gi * SUBLANES + SUBLANES, SUBLANES), SUBLANES)
            av = abuf[rows, :]
            bv = gbuf[rows, :] + jnp.where(last_row, car, 0.0)
            ev = jnp.where(last_row, 0.0, pltpu.roll(av, SUBLANES - 1, 0))
            for sh in (1, 2, 4):
                m = sub < SUBLANES - sh
                bv = jnp.where(m, bv + ev * pltpu.roll(bv, SUBLANES - sh, 0), bv)
                ev = jnp.where(m, ev * pltpu.roll(ev, SUBLANES - sh, 0), 0.0)
            gbuf[rows, :] = bv
            pv = hpad[rows, :]
            hprev = jnp.where(row0, jnp.broadcast_to(pv[SUBLANES - 1:SUBLANES, :], pv.shape),
                              pltpu.roll(hpad[nrows, :], 1, 0))
            dabuf[rows, :] = bv * hprev
            return jnp.broadcast_to((av * bv)[0:1, :], bv.shape)

        carry[...] = lax.fori_loop(0, ng, step, carry[...])
        gv = gbuf[...]
        da = dabuf[...]
        ix = ig * xc
        dmult = gv * ix
        dig = gv * mult * xc
        dxc = gv * mult * ig
        dlog_a = da * a - dmult * (a * a) / mult
        dnsp_ref[...] += _sum0(dlog_a * r)
        dpr = dlog_a * nsp_v * r * (1.0 - r)
        dpi = dig * ig * (1.0 - ig)
        dxc = dxc + _dot_nt(dpr, wa_ref[...]) + _dot_nt(dpi, wx_ref[...])
        dwa_ref[...] += _dot_tn(xc, dpr)
        dwx_ref[...] += _dot_tn(xc, dpi)
        dba_ref[...] += _sum0(dpr)
        dbx_ref[...] += _sum0(dpi)
        dcb_ref[...] += _sum0(dxc)
        for k in range(4):
            dcw_ref[k:k + 1, :] += _sum0(dxc * xpad[SUBLANES - 3 + k:SUBLANES - 3 + k + tm, :])
        dpad[0:tm, :] = dxc
        dpad[tm:tm + SUBLANES, :] = dnext[...]
        dx = cw_v[0:1, :] * dpad[3:3 + tm, :]
        for k in range(1, 4):
            dx = dx + cw_v[k:k + 1, :] * dpad[3 - k:3 - k + tm, :]
        dx_ref[...] = dx
        dnext[...] = dxc[0:SUBLANES, :]

    rev = lambda i: nt - 1 - i
    rrow = lambda n, col=0: pl.BlockSpec((tm, n), lambda i: (rev(i), col))
    sq = _const((RG_WIDTH, RG_WIDTH))
    vec = _const((1, RG_WIDTH))
    return pl.pallas_call(
        body, name="rg_bwd", grid=(nt,),
        in_specs=[rrow(RG_WIDTH, 768 // RG_WIDTH), rrow(RG_WIDTH, P_XRG // RG_WIDTH),
                  pl.BlockSpec((SUBLANES, RG_WIDTH), lambda i: (jnp.maximum(rev(i) * hb - 1, 0), P_XRG // RG_WIDTH)),
                  rrow(RG_WIDTH, P_GRG // RG_WIDTH), rrow(RG_WIDTH),
                  pl.BlockSpec((SUBLANES, RG_WIDTH), lambda i: (jnp.maximum(rev(i) * hb - 1, 0), 0)),
                  _const((4, RG_WIDTH)), vec, sq, vec, sq, vec, vec],
        out_specs=[rrow(RG_WIDTH), rrow(RG_WIDTH), _const((SUBLANES, RG_WIDTH)), vec, sq, vec, sq, vec, vec],
        out_shape=[jax.ShapeDtypeStruct((t, RG_WIDTH), F32), jax.ShapeDtypeStruct((t, RG_WIDTH), F32),
                   jax.ShapeDtypeStruct((SUBLANES, RG_WIDTH), F32), jax.ShapeDtypeStruct((1, RG_WIDTH), F32),
                   jax.ShapeDtypeStruct((RG_WIDTH, RG_WIDTH), F32), jax.ShapeDtypeStruct((1, RG_WIDTH), F32),
                   jax.ShapeDtypeStruct((RG_WIDTH, RG_WIDTH), F32), jax.ShapeDtypeStruct((1, RG_WIDTH), F32),
                   jax.ShapeDtypeStruct((1, RG_WIDTH), F32)],
        scratch_shapes=[pltpu.VMEM((tm + SUBLANES, RG_WIDTH), F32), pltpu.VMEM((tm, RG_WIDTH), F32),
                        pltpu.VMEM((tm, RG_WIDTH), F32), pltpu.VMEM((tm + SUBLANES, RG_WIDTH), F32),
                        pltpu.VMEM((tm, RG_WIDTH), F32), pltpu.VMEM((tm + SUBLANES, RG_WIDTH), F32),
                        pltpu.VMEM((SUBLANES, RG_WIDTH), F32), pltpu.VMEM((SUBLANES, RG_WIDTH), F32)],
        compiler_params=_cparams(("arbitrary",)),
    )(dycat, proj, proj, proj, hs, hs, cw, cb, wa, ba, wx, bx, nsp)


def _block_diag(blocks):
    g, a, b = blocks.shape
    eye = jnp.eye(g, dtype=blocks.dtype)
    return (eye[:, None, :, None] * blocks[:, :, None, :]).reshape(g * a, g * b)


def _block_diag_extract(m, g):
    a, b = m.shape[0] // g, m.shape[1] // g
    m4 = m.reshape(g, a, g, b)
    idx = jnp.arange(g)
    return m4[idx, :, idx, :]


def _s5_prepare(lam_re, lam_im, log_step, b_re, b_im, c_re, c_im):
    step = jnp.exp(log_step)[:, None]
    mag = jnp.exp(lam_re * step)
    lbr = mag * jnp.cos(lam_im * step)
    lbi = mag * jnp.sin(lam_im * step)
    nr, ni = lbr - 1.0, lbi
    den = lam_re * lam_re + lam_im * lam_im
    cr = (nr * lam_re + ni * lam_im) / den
    ci = (ni * lam_re - nr * lam_im) / den
    bbr = cr[..., None] * b_re - ci[..., None] * b_im
    bbi = cr[..., None] * b_im + ci[..., None] * b_re
    bre = _block_diag(jnp.swapaxes(bbr, 1, 2))
    bim = _block_diag(jnp.swapaxes(bbi, 1, 2))
    cre = _block_diag(jnp.swapaxes(c_re, 1, 2))
    cim = _block_diag(jnp.swapaxes(c_im, 1, 2))
    return lbr.reshape(-1), lbi.reshape(-1), bre, bim, cre, cim


def _s5_scan_coef(lbr, lbi, reverse):
    if reverse:
        lbi = -lbi
    pr, pi = [lbr], [lbi]
    for _ in range(7):
        pr, pi = pr + [pr[-1] * lbr - pi[-1] * lbi], pi + [pr[-1] * lbi + pi[-1] * lbr]
    row = jnp.arange(SUBLANES)[:, None]
    tabs = []
    for sh in (1, 2, 4):
        keep = (row < SUBLANES - sh) if reverse else (row >= sh)
        tabs.append(jnp.stack([jnp.where(keep, pr[sh - 1][None, :], 0.0), jnp.where(keep, pi[sh - 1][None, :], 0.0)]))
    powr = jnp.stack(pr)
    powi = jnp.stack(pi)
    if reverse:
        powr, powi = powr[::-1], powi[::-1]
    tabs.append(jnp.stack([powr, powi]))
    tabs.append(jnp.zeros_like(tabs[-1]))
    return jnp.stack(tabs).astype(F32)


def _xy_peers():
    x, y, c = lax.axis_index("x"), lax.axis_index("y"), lax.axis_index("c")
    return x, y, c, [(1 - x, y), (x, 1 - y), (1 - x, 1 - y)]


def _hbm():
    return pl.BlockSpec(memory_space=pl.ANY)


def _xy_allgather(buf, *, name):
    n, w = buf.shape

    def body(x_ref, out_ref, send_sems, recv_sems, local_sem):
        x, y, c, peers = _xy_peers()
        me = 2 * x + y
        own = pltpu.make_async_copy(x_ref, out_ref.at[me], local_sem)
        own.start()
        sends = []
        for k, (px, py) in enumerate(peers):
            cp = pltpu.make_async_remote_copy(src_ref=x_ref, dst_ref=out_ref.at[me], send_sem=send_sems.at[k],
                                              recv_sem=recv_sems.at[k], device_id=(px, py, c), device_id_type=MESH)
            cp.start()
            sends.append(cp)
        for k, (px, py) in enumerate(peers):
            pltpu.make_async_remote_copy(src_ref=x_ref, dst_ref=out_ref.at[2 * px + py], send_sem=send_sems.at[k],
                                         recv_sem=recv_sems.at[k], device_id=(px, py, c),
                                         device_id_type=MESH).wait_recv()
        for cp in sends:
            cp.wait_send()
        own.wait()

    return pl.pallas_call(
        body, name=name, in_specs=[_hbm()], out_specs=_hbm(),
        out_shape=jax.ShapeDtypeStruct((4, n, w), buf.dtype),
        scratch_shapes=[pltpu.SemaphoreType.DMA((3,)), pltpu.SemaphoreType.DMA((3,)), pltpu.SemaphoreType.DMA],
    )(buf)


def _remote(src, dst, send_sem, recv_sem, dev):
    return pltpu.make_async_remote_copy(src_ref=src, dst_ref=dst, send_sem=send_sem, recv_sem=recv_sem,
                                        device_id=dev, device_id_type=MESH)


LAYER_GATHERED = (
    ("w_in", (1024, W_IN_PAD), 1), ("s5_glu_w", (64, 256), 0), ("w_out", (256, 1024), 0), ("xa_wq", (256, 1024), 0),
    ("xa_wk", (256, 1024), 0), ("xa_wv", (256, 1024), 0), ("xa_wo", (256, 1024), 0), ("mlp_w1", (1024, 1024), 1),
    ("mlp_w2", (1024, 1024), 0),
)
N_GATHERED = len(LAYER_GATHERED)
WAIT_GROUPS = ((0, 1), (2,), (3, 4, 5, 6), (7, 8))
N_GATHER_COPIES = 3 * N_GATHERED * DEPTH


def _gather_part(ref, t, pos):
    _, shp, ax = LAYER_GATHERED[t % N_GATHERED]
    idx = tuple(pl.ds(pos * shp[ax], shp[ax]) if d == ax else slice(None) for d in range(len(shp)))
    return ref.at[idx]


def _gather_start(shards):
    n = len(shards)
    lands = []
    for t, s in enumerate(shards):
        _, shp, ax = LAYER_GATHERED[t % N_GATHERED]
        full = shp[:ax] + (4 * shp[ax],) + shp[ax + 1:]
        lands.append(pltpu.with_memory_space_constraint(lax.empty(full, s.dtype), pltpu.HBM))

    def body(*refs):
        srcs, lnds = refs[:n], refs[n:2 * n]
        send_sems, recv_sems = refs[2 * n], refs[2 * n + 1]
        token = refs[-1]
        x, y, c, peers = _xy_peers()
        me = 2 * x + y
        for t in range(n):
            for k, (px, py) in enumerate(peers):
                _remote(srcs[t], _gather_part(lnds[t], t, me), send_sems.at[k * n + t], recv_sems.at[k * n + t],
                        (px, py, c)).start()
        token[...] = jnp.zeros_like(token)

    hbm = pl.BlockSpec(memory_space=pltpu.HBM)
    sem = pl.BlockSpec(memory_space=pltpu.SEMAPHORE)
    outs = pl.pallas_call(
        body, name="weights_gather_start", in_specs=[hbm] * (2 * n),
        out_shape=(pltpu.SemaphoreType.DMA((3 * n,)), pltpu.SemaphoreType.DMA((3 * n,)),
                   *[pltpu.HBM(s.shape, s.dtype) for s in shards], *[pltpu.HBM(a.shape, a.dtype) for a in lands],
                   jax.ShapeDtypeStruct((SUBLANES, LANES), F32)),
        out_specs=(sem, sem, *[hbm] * (2 * n), pl.BlockSpec(memory_space=pltpu.VMEM)),
        input_output_aliases={i: 2 + i for i in range(2 * n)},
        compiler_params=pltpu.CompilerParams(has_side_effects=pltpu.SideEffectType.DATAFLOW_SIDE_EFFECTING),
    )(*[pltpu.with_memory_space_constraint(s, pltpu.HBM) for s in shards], *lands)
    return outs[0], outs[1], outs[2:2 + n], outs[2 + n:2 + 2 * n], outs[-1]


def _gather_wait(handle, ts, after, *, name):
    send_sems, recv_sems, src_thru, land_thru, _ = handle
    n = len(src_thru)
    m = len(ts)

    def body(*refs):
        srcs, lnds = refs[:m], refs[m:2 * m]
        ssem, rsem = refs[2 * m], refs[2 * m + 1]
        x, y, c, peers = _xy_peers()
        for i, t in enumerate(ts):
            for k, (px, py) in enumerate(peers):
                cp = _remote(srcs[i], _gather_part(lnds[i], t, 2 * px + py), ssem.at[k * n + t], rsem.at[k * n + t],
                             (px, py, c))
                cp.wait_send()
                cp.wait_recv()

    hbm = pl.BlockSpec(memory_space=pltpu.HBM)
    sem = pl.BlockSpec(memory_space=pltpu.SEMAPHORE)
    args = [src_thru[t] for t in ts] + [land_thru[t] for t in ts]
    outs = pl.pallas_call(
        body, name=name, in_specs=[hbm] * (2 * m) + [sem, sem, pl.BlockSpec(memory_space=pl.ANY)],
        out_shape=[pltpu.HBM(a.shape, a.dtype) for a in args], out_specs=[hbm] * (2 * m),
        input_output_aliases={i: i for i in range(2 * m)},
        compiler_params=pltpu.CompilerParams(has_side_effects=pltpu.SideEffectType.DATAFLOW_SIDE_EFFECTING),
    )(*args, send_sems, recv_sems, after)
    return outs[:m], outs[m:]


C_CHUNKS = 4
XY_CHUNKS = 4
EW_ROWS = 512


def _c_exchange(g):
    _, n, w = g.shape
    n2 = n // 2
    rq = n2 // C_CHUNKS

    def body(g_ref, got_ref, send_sems, recv_sems):
        x, y, c = lax.axis_index("x"), lax.axis_index("y"), lax.axis_index("c")
        cps = []
        for s in range(4):
            for q in range(C_CHUNKS):
                k = s * C_CHUNKS + q
                cp = _remote(g_ref.at[s, pl.ds((1 - c) * n2 + q * rq, rq), :], got_ref.at[s, pl.ds(q * rq, rq), :],
                             send_sems.at[k], recv_sems.at[k], (x, y, 1 - c))
                cp.start()
                cps.append(cp)
        for cp in cps:
            cp.wait_recv()
        for cp in cps:
            cp.wait_send()

    return pl.pallas_call(
        body, name="grad_c_exchange", in_specs=[_hbm()], out_specs=_hbm(),
        out_shape=jax.ShapeDtypeStruct((4, n2, w), g.dtype),
        scratch_shapes=[pltpu.SemaphoreType.DMA((4 * C_CHUNKS,)), pltpu.SemaphoreType.DMA((4 * C_CHUNKS,))],
    )(g)


XFER_DTYPE = jnp.bfloat16


def _add_own_half(g, got, c_arr):
    _, n, w = g.shape
    n2 = n // 2
    nb = n2 // EW_ROWS

    def body(c_ref, a_ref, b_ref, o_ref, t_ref):
        sm = a_ref[...] + b_ref[...]
        o_ref[...] = sm.astype(o_ref.dtype)

        @pl.when(pl.program_id(1) == nb - 1)
        def _():
            t_ref[...] = sm[:, EW_ROWS - MISC_ROWS:, :]

    grid_spec = pltpu.PrefetchScalarGridSpec(
        num_scalar_prefetch=1, grid=(4, nb),
        in_specs=[pl.BlockSpec((1, EW_ROWS, w), lambda s, i, c: (s, c[0] * nb + i, 0)),
                  pl.BlockSpec((1, EW_ROWS, w), lambda s, i, c: (s, i, 0))],
        out_specs=[pl.BlockSpec((1, EW_ROWS, w), lambda s, i, c: (s, i, 0)),
                   pl.BlockSpec((1, MISC_ROWS, w), lambda s, i, c: (s, 0, 0))])
    return pl.pallas_call(
        body, name="grad_add_halves", grid_spec=grid_spec,
        out_shape=[jax.ShapeDtypeStruct((4, n2, w), XFER_DTYPE), jax.ShapeDtypeStruct((4, MISC_ROWS, w), g.dtype)],
        compiler_params=_cparams(("arbitrary", "arbitrary")),
    )(c_arr, g, got)


def _xy_exchange(arrs):
    na = len(arrs)
    pieces = []
    for a, arr in enumerate(arrs):
        nch = XY_CHUNKS if a == 0 else 1
        rq = arr.shape[1] // nch
        pieces += [(a, pl.ds(q * rq, rq)) for q in range(nch)]
    npc = len(pieces)

    def body(*refs):
        ins, outs = refs[:na], refs[na:2 * na]
        send_sems, recv_sems, local_sems = refs[2 * na:]
        x, y, c, peers = _xy_peers()
        me = 2 * x + y
        own = []
        for j, (a, rows) in enumerate(pieces):
            cp = pltpu.make_async_copy(ins[a].at[me, rows, :], outs[a].at[me, rows, :], local_sems.at[j])
            cp.start()
            own.append(cp)
        sends = []
        for k, (px, py) in enumerate(peers):
            for j, (a, rows) in enumerate(pieces):
                cp = _remote(ins[a].at[2 * px + py, rows, :], outs[a].at[me, rows, :], send_sems.at[k * npc + j],
                             recv_sems.at[k * npc + j], (px, py, c))
                cp.start()
                sends.append(cp)
        for k, (px, py) in enumerate(peers):
            for j, (a, rows) in enumerate(pieces):
                _remote(ins[a].at[me, rows, :], outs[a].at[2 * px + py, rows, :], send_sems.at[k * npc + j],
                        recv_sems.at[k * npc + j], (px, py, c)).wait_recv()
        for cp in sends:
            cp.wait_send()
        for cp in own:
            cp.wait()

    return pl.pallas_call(
        body, name="grad_xy_exchange", in_specs=[_hbm()] * na, out_specs=[_hbm()] * na,
        out_shape=[jax.ShapeDtypeStruct(a.shape, a.dtype) for a in arrs],
        scratch_shapes=[pltpu.SemaphoreType.DMA((3 * npc,)), pltpu.SemaphoreType.DMA((3 * npc,)),
                        pltpu.SemaphoreType.DMA((npc,))],
    )(*arrs)


def _sum4_into_half(r, rt, c_arr):
    _, n2, w = r.shape
    nb = n2 // EW_ROWS

    def body(c_ref, r_ref, t_ref, o_ref):
        o_ref[...] = ((r_ref[0].astype(F32) + r_ref[1].astype(F32)) + r_ref[2].astype(F32)) + r_ref[3].astype(F32)

        @pl.when(pl.program_id(0) == nb - 1)
        def _():
            o_ref[EW_ROWS - MISC_ROWS:, :] = ((t_ref[0] + t_ref[1]) + t_ref[2]) + t_ref[3]

    grid_spec = pltpu.PrefetchScalarGridSpec(
        num_scalar_prefetch=1, grid=(nb,),
        in_specs=[pl.BlockSpec((4, EW_ROWS, w), lambda i, c: (0, i, 0)),
                  pl.BlockSpec((4, MISC_ROWS, w), lambda i, c: (0, 0, 0))],
        out_specs=pl.BlockSpec((EW_ROWS, w), lambda i, c: (c[0] * nb + i, 0)))
    return pl.pallas_call(
        body, name="grad_sum4", grid_spec=grid_spec, out_shape=jax.ShapeDtypeStruct((2 * n2, w), F32),
        compiler_params=_cparams(("arbitrary",)),
    )(c_arr, r, rt)


C_GATHER_CHUNKS = 8


def _c_allgather_halves(f):
    n, w = f.shape
    n2 = n // 2
    rq = n2 // C_GATHER_CHUNKS

    def body(f_ref, out_ref, send_sems, recv_sems):
        x, y, c = lax.axis_index("x"), lax.axis_index("y"), lax.axis_index("c")
        sends = []
        for q in range(C_GATHER_CHUNKS):
            rows = pl.ds(c * n2 + q * rq, rq)
            cp = _remote(f_ref.at[rows, :], out_ref.at[rows, :], send_sems.at[q], recv_sems.at[q], (x, y, 1 - c))
            cp.start()
            sends.append(cp)
        for q in range(C_GATHER_CHUNKS):
            rows = pl.ds((1 - c) * n2 + q * rq, rq)
            _remote(f_ref.at[rows, :], out_ref.at[rows, :], send_sems.at[q], recv_sems.at[q],
                    (x, y, 1 - c)).wait_recv()
        for cp in sends:
            cp.wait_send()

    return pl.pallas_call(
        body, name="grad_c_allgather", in_specs=[_hbm()], out_specs=_hbm(), input_output_aliases={0: 0},
        out_shape=jax.ShapeDtypeStruct((n, w), f.dtype),
        scratch_shapes=[pltpu.SemaphoreType.DMA((C_GATHER_CHUNKS,)), pltpu.SemaphoreType.DMA((C_GATHER_CHUNKS,))],
    )(f)


def _adamw(w, m, v, g, g_row0=None):
    shape = w.shape
    cols = shape[-1]
    rows = int(math.prod(shape)) // cols
    tr = 256 if rows % 256 == 0 else rows
    from_flat = g_row0 is not None
    c1 = 1.0 / (1.0 - ADAM_B1 ** ADAM_STEP)
    c2 = 1.0 / (1.0 - ADAM_B2 ** ADAM_STEP)

    def body(w_ref, m_ref, v_ref, g_ref, *outs):
        gg = g_ref[...]
        nm = ADAM_B1 * m_ref[...] + (1.0 - ADAM_B1) * gg
        nv = ADAM_B2 * v_ref[...] + (1.0 - ADAM_B2) * (gg * gg)
        if from_flat:
            outs[0][...] = gg
        d_ref, nm_ref, nv_ref = outs[-3:]
        nm_ref[...] = nm
        nv_ref[...] = nv
        d_ref[...] = -ADAM_LR * ((nm * c1) / (jnp.sqrt(nv * c2) + ADAM_EPS) + ADAM_WD * w_ref[...])

    spec = pl.BlockSpec((tr, cols), lambda i: (i, 0))
    if from_flat:
        assert cols == FLAT and g_row0 % tr == 0
        g_spec = pl.BlockSpec((tr, cols), lambda i: (g_row0 // tr + i, 0))
        g_arg = g
    else:
        g_spec = spec
        g_arg = g.reshape(rows, cols)
    n_out = 4 if from_flat else 3
    sds = jax.ShapeDtypeStruct((rows, cols), F32)
    outs = pl.pallas_call(
        body, name="adamw", grid=(rows // tr,), in_specs=[spec, spec, spec, g_spec], out_specs=[spec] * n_out,
        out_shape=[sds] * n_out, compiler_params=_cparams(("arbitrary",)),
    )(w.reshape(rows, cols), m.reshape(rows, cols), v.reshape(rows, cols), g_arg)
    outs = [o.reshape(shape) for o in outs]
    return outs if from_flat else [g] + outs


SMALL_SHARDED = (("s5_glu_w", (2, 64, 256), 1), ("ssd_conv_w", (2, 4, 256), 2), ("rg_conv_w", (2, 4, 64), 2))
CONV_SHARDED = SMALL_SHARDED[1:]
REPLICATED = (
    ("ssd_conv_b", (2, 1024)), ("ssd_dt_bias", (2, 8)), ("ssd_a_log", (2, 8)), ("ssd_d", (2, 8)),
    ("ssd_norm_w", (2, 512)), ("s5_lam_re", (2, 16, 64)), ("s5_lam_im", (2, 16, 64)), ("s5_log_step", (2, 16)),
    ("s5_b_re", (2, 16, 64, 16)), ("s5_b_im", (2, 16, 64, 16)), ("s5_c_re", (2, 16, 16, 64)),
    ("s5_c_im", (2, 16, 16, 64)), ("s5_d", (2, 256)), ("s5_glu_b", (2, 256)), ("rg_conv_b", (2, 256)),
    ("rg_wa", (2, 4, 64, 64)), ("rg_ba", (2, 4, 64)), ("rg_wx", (2, 4, 64, 64)), ("rg_bx", (2, 4, 64)),
    ("rg_lambda", (2, 256)), ("ln1_g", (2, 1024)), ("ln1_b", (2, 1024)), ("ln2_g", (2, 1024)), ("ln2_b", (2, 1024)),
    ("ln3_g", (2, 1024)), ("ln3_b", (2, 1024)),
)
WEIGHT_ORDER = (
    "w_in", "w_out", "ssd_conv_w", "ssd_conv_b", "ssd_dt_bias", "ssd_a_log", "ssd_d", "ssd_norm_w", "s5_lam_re",
    "s5_lam_im", "s5_log_step", "s5_b_re", "s5_b_im", "s5_c_re", "s5_c_im", "s5_d", "s5_glu_w", "s5_glu_b",
    "rg_conv_w", "rg_conv_b", "rg_wa", "rg_ba", "rg_wx", "rg_bx", "rg_lambda", "ln1_g", "ln1_b", "xa_wq", "xa_wk",
    "xa_wv", "xa_wo", "ln2_g", "ln2_b", "mlp_w1", "mlp_w2", "ln3_g", "ln3_b",
)


def _size(shape):
    return int(math.prod(shape))


def _pad_rows(flat, rows):
    return jnp.pad(flat, (0, rows * FLAT - flat.shape[0])).reshape(rows, FLAT)


def _round_up(a, b):
    return (a + b - 1) // b * b


SMALL_ELEMS = sum(_size(s) for _, s, _ in SMALL_SHARDED)
REP_ELEMS = sum(_size(s) for _, s in REPLICATED)
REP_QROWS = _round_up(-(-REP_ELEMS // (4 * FLAT)), 8)
assert SMALL_ELEMS <= MISC_REP_ROW * FLAT and MISC_REP_ROW + REP_QROWS <= MISC_ROWS
CONV_ROWS = 8


def _pack_shards(tensors, names_shapes):
    return jnp.concatenate([tensors[n].reshape(-1) for n, *_ in names_shapes])


def _unpack(flat, names_shapes):
    out, off = {}, 0
    for n, s, *_ in names_shapes:
        out[n] = flat[off:off + _size(s)].reshape(s)
        off += _size(s)
    return out


def _gather_full(gathered, names_shapes):
    flat = gathered.reshape(4, -1)
    out, off = {}, 0
    for n, s, ax in names_shapes:
        parts = flat[:, off:off + _size(s)].reshape((4,) + s)
        out[n] = jnp.concatenate([parts[k] for k in range(4)], axis=ax)
        off += _size(s)
    return out


def _split_shards(full, names_shapes):
    rows = []
    for k in range(4):
        parts = []
        for n, s, ax in names_shapes:
            w = s[ax]
            parts.append(lax.slice_in_dim(full[n], k * w, (k + 1) * w, axis=ax).reshape(-1))
        rows.append(jnp.concatenate(parts))
    return jnp.stack(rows)


def _pack_cols(w):
    pad = jnp.zeros((w.shape[0], LANES - SSD_HEADS), w.dtype)
    return jnp.concatenate([w[:, O_XBC:O_XBC + 1024], w[:, O_Z:O_Z + 512], w[:, O_U:O_U + 256],
                            w[:, O_XRG:O_XRG + 256], w[:, O_GRG:O_GRG + 256], w[:, O_DT:O_DT + 8], pad], axis=1)


def _unpack_cols(w):
    return jnp.concatenate([w[:, P_Z:P_Z + 512], w[:, P_XBC:P_XBC + 1024], w[:, P_DT:P_DT + 8],
                            w[:, P_U:P_U + 256], w[:, P_XRG:P_XRG + 256], w[:, P_GRG:P_GRG + 256]], axis=1)


def _lanes(v, width):
    return jnp.pad(v, (0, width - v.shape[0])).reshape(1, width)


def _layer_params(full, rep, l):
    p = {}
    p["ssd_cw"] = full["ssd_conv_w"][l]
    p["ssd_cb"] = rep["ssd_conv_b"][l].reshape(1, -1)
    p["ssd_dtb"] = _lanes(rep["ssd_dt_bias"][l], LANES)
    p["ssd_a"] = _lanes(-jnp.exp(rep["ssd_a_log"][l]), LANES)
    p["ssd_d"] = jnp.repeat(rep["ssd_d"][l], 64).reshape(1, -1)
    p["ssd_nw"] = rep["ssd_norm_w"][l].reshape(1, -1)
    s5_args = tuple(rep[n][l] for n in ("s5_lam_re", "s5_lam_im", "s5_log_step", "s5_b_re", "s5_b_im", "s5_c_re",
                                        "s5_c_im"))
    (lbr, lbi, bre, bim, cre, cim), p["s5_vjp"] = jax.vjp(_s5_prepare, *s5_args)
    p.update(s5_bre=bre, s5_bim=bim, s5_cre=cre, s5_cim=cim)
    p["s5_coef"] = _s5_scan_coef(lbr, lbi, False)
    p["s5_rcoef"] = _s5_scan_coef(lbr, lbi, True)
    p["s5_d"] = rep["s5_d"][l].reshape(1, -1)
    p["s5_gb"] = rep["s5_glu_b"][l].reshape(1, -1)
    p["rg_cw"] = full["rg_conv_w"][l]
    p["rg_cb"] = rep["rg_conv_b"][l].reshape(1, -1)
    p["rg_wa"] = _block_diag(rep["rg_wa"][l])
    p["rg_wx"] = _block_diag(rep["rg_wx"][l])
    p["rg_ba"] = rep["rg_ba"][l].reshape(1, -1)
    p["rg_bx"] = rep["rg_bx"][l].reshape(1, -1)
    p["rg_nsp"] = (-RG_C * jax.nn.softplus(-rep["rg_lambda"][l])).reshape(1, -1)
    p["rg_dnsp"] = RG_C * jax.nn.sigmoid(-rep["rg_lambda"][l])
    for n in ("ln1_g", "ln1_b", "ln2_g", "ln2_b", "ln3_g", "ln3_b"):
        p[n] = rep[n][l].reshape(1, -1)
    return p


def _layer_fwd(h, mem, p, fetch):
    s = {"h0": h}
    p.update(fetch(0, h))
    proj = _mm(h, p["w_in"], name="in_proj")
    s["proj"] = proj
    y_ssd, s["ssd_yy"], s["ssd_states"] = _ssd_fwd(proj, p["ssd_cw"], p["ssd_cb"], p["ssd_dtb"], p["ssd_a"],
                                                     p["ssd_d"], p["ssd_nw"])
    y_s5, s["s5_y2"], s["s5_hre"], s["s5_him"] = _s5_fwd(proj, p["s5_bre"], p["s5_bim"], p["s5_cre"], p["s5_cim"],
                                                         p["s5_d"], p["s5_glu_w"], p["s5_gb"], p["s5_coef"])
    y_rg, s["rg_h"] = _rg_fwd(proj, p["rg_cw"], p["rg_cb"], p["rg_wa"], p["rg_ba"], p["rg_wx"], p["rg_bx"],
                              p["rg_nsp"])
    ycat = jnp.concatenate([y_ssd, y_s5, y_rg], axis=1)
    s["ycat"] = ycat
    p.update(fetch(1, ycat))
    h1, s["xh1"], s["rs1"] = _outproj_ln_fwd(ycat, h, p["w_out"], p["ln1_g"], p["ln1_b"])
    s["h1"] = h1
    p.update(fetch(2, h1))
    kb = _mm(mem, p["xa_wk"], name="mem_proj")
    vb = _mm(mem, p["xa_wv"], name="mem_proj")
    s["kb"], s["vb"] = kb, vb
    h2, s["xh2"], s["rs2"], s["attn_o"] = _attn_ln_fwd(h1, p["xa_wq"], p["xa_wo"], kb, vb, p["ln2_g"], p["ln2_b"])
    s["h2"] = h2
    p.update(fetch(3, h2))
    h3, s["xh3"], s["rs3"], s["mlp_hdn"] = _mlp_ln_fwd(h2, p["mlp_w1"], p["mlp_w2"], p["ln3_g"], p["ln3_b"])
    return h3, s


def _layer_bwd(dh3, mem, p, s, l, gbuf):
    g = {}
    dr3, du, dh2, g["ln3_g"], g["ln3_b"] = _mlp_ln_bwd(dh3, s["xh3"], s["rs3"], p["ln3_g"], s["mlp_hdn"],
                                                        p["mlp_w1"], p["mlp_w2"])
    gbuf = _wgrad_flat(s["h2"], du, gbuf, mode="colblk", row_off=ROW_MLP_W1 + 1024 * l, name="wgrad_mlp_w1")
    gbuf = _wgrad_flat(s["mlp_hdn"], dr3, gbuf, mode="rowblk", row_off=ROW_MLP_W2 + 1024 * l, name="wgrad_mlp_w2")
    dr2, dq, dh1, dkb, dvb, g["ln2_g"], g["ln2_b"] = _attn_ln_bwd(dh2, s["xh2"], s["rs2"], p["ln2_g"], s["h1"],
                                                                   p["xa_wq"], p["xa_wo"], s["kb"], s["vb"])
    for n, a_op, g_op in (("xa_wo", s["attn_o"], dr2), ("xa_wq", s["h1"], dq), ("xa_wk", mem, dkb),
                          ("xa_wv", mem, dvb)):
        gbuf = _wgrad_flat(a_op, g_op, gbuf, mode="rows4", row_off=ROW_XA[n] + 256 * l, name="wgrad_" + n)
    dr1, dres, dycat, g["ln1_g"], g["ln1_b"] = _outproj_ln_bwd(dh1, s["xh1"], s["rs1"], p["ln1_g"], p["w_out"])
    gbuf = _wgrad_flat(s["ycat"], dr1, gbuf, mode="rows4", row_off=ROW_W_OUT + 256 * l, name="wgrad_w_out")
    proj = s["proj"]
    (dxbc, dz, ddt, dcw, dcb, ddtb, da_neg, dd_l, dnw) = _ssd_bwd(
        dycat, proj, s["ssd_yy"], s["ssd_states"], p["ssd_cw"], p["ssd_cb"], p["ssd_dtb"], p["ssd_a"], p["ssd_d"],
        p["ssd_nw"])
    g["ssd_conv_w"] = dcw[0:4]
    g["ssd_conv_b"] = dcb[0]
    g["ssd_dt_bias"] = ddtb[0, :SSD_HEADS]
    g["ssd_a_log"] = da_neg[0, :SSD_HEADS] * p["ssd_a"][0, :SSD_HEADS]
    g["ssd_d"] = dd_l.reshape(SSD_HEADS, 64).sum(axis=1)
    g["ssd_norm_w"] = dnw[0]
    (du_s5, dbre, dbim, dcre, dcim, dlam, dd5, dgw, dgb) = _s5_bwd(
        dycat, proj, s["s5_y2"], s["s5_hre"], s["s5_him"], p["s5_bre"], p["s5_bim"], p["s5_cre"], p["s5_cim"],
        p["s5_d"], p["s5_glu_w"], p["s5_gb"], p["s5_rcoef"])
    dl = dlam.sum(axis=1)
    s5g = p["s5_vjp"]((dl[0], dl[1], dbre, dbim, dcre, dcim))
    for n, v in zip(("s5_lam_re", "s5_lam_im", "s5_log_step", "s5_b_re", "s5_b_im", "s5_c_re", "s5_c_im"), s5g):
        g[n] = v
    g["s5_d"] = dd5[0]
    g["s5_glu_w"] = dgw
    g["s5_glu_b"] = dgb[0]
    (dxrg, dgrg, drcw, drcb, dwa, dba, dwx, dbx, dnsp) = _rg_bwd(
        dycat, proj, s["rg_h"], p["rg_cw"], p["rg_cb"], p["rg_wa"], p["rg_ba"], p["rg_wx"], p["rg_bx"], p["rg_nsp"])
    g["rg_conv_w"] = drcw[0:4]
    g["rg_conv_b"] = drcb[0]
    g["rg_wa"] = _block_diag_extract(dwa, RG_BLOCKS)
    g["rg_wx"] = _block_diag_extract(dwx, RG_BLOCKS)
    g["rg_ba"] = dba.reshape(RG_BLOCKS, RG_BLOCK_DIM)
    g["rg_bx"] = dbx.reshape(RG_BLOCKS, RG_BLOCK_DIM)
    g["rg_lambda"] = dnsp[0] * p["rg_dnsp"]
    dproj = jnp.concatenate([dxbc, dz, du_s5, dxrg, dgrg, ddt], axis=1)
    g["w_in"] = _unpack_cols(_mm_tn(s["h0"], dproj, name="wgrad_in"))
    dh0 = _mm(dproj, p["w_in"], nt=True, add=dres, name="in_proj_bwd")
    for n in ("ln1_g", "ln1_b", "ln2_g", "ln2_b", "ln3_g", "ln3_b"):
        g[n] = g[n][0]
    return dh0, g, gbuf


def _local_step(h, memf, target, full, rep, fetch):
    params, saved = [], []
    for l in range(DEPTH):
        p = _layer_params(full, rep, l)
        params.append(p)
        h, s = _layer_fwd(h, memf, p, functools.partial(fetch, l))
        saved.append(s)
    loss11, dh = _loss_fwd_bwd(h, target)
    grads = [None] * DEPTH
    gbuf = None
    for l in reversed(range(DEPTH)):
        dh, grads[l], gbuf = _layer_bwd(dh, memf, params[l], saved[l], l, gbuf)
    return loss11, dh, {n: jnp.stack([grads[l][n] for l in range(DEPTH)]) for n in grads[0]}, gbuf


def kernel(x, mem, w_in, w_out, ssd_conv_w, ssd_conv_b, ssd_dt_bias, ssd_a_log, ssd_d, ssd_norm_w, s5_lam_re, s5_lam_im, s5_log_step, s5_b_re, s5_b_im, s5_c_re, s5_c_im, s5_d, s5_glu_w, s5_glu_b, rg_conv_w, rg_conv_b, rg_wa, rg_ba, rg_wx, rg_bx, rg_lambda, ln1_g, ln1_b, xa_wq, xa_wk, xa_wv, xa_wo, ln2_g, ln2_b, mlp_w1, mlp_w2, ln3_g, ln3_b, loss_target, m_w_in, m_w_out, m_ssd_conv_w, m_ssd_conv_b, m_ssd_dt_bias, m_ssd_a_log, m_ssd_d, m_ssd_norm_w, m_s5_lam_re, m_s5_lam_im, m_s5_log_step, m_s5_b_re, m_s5_b_im, m_s5_c_re, m_s5_c_im, m_s5_d, m_s5_glu_w, m_s5_glu_b, m_rg_conv_w, m_rg_conv_b, m_rg_wa, m_rg_ba, m_rg_wx, m_rg_bx, m_rg_lambda, m_ln1_g, m_ln1_b, m_xa_wq, m_xa_wk, m_xa_wv, m_xa_wo, m_ln2_g, m_ln2_b, m_mlp_w1, m_mlp_w2, m_ln3_g, m_ln3_b, v_w_in, v_w_out, v_ssd_conv_w, v_ssd_conv_b, v_ssd_dt_bias, v_ssd_a_log, v_ssd_d, v_ssd_norm_w, v_s5_lam_re, v_s5_lam_im, v_s5_log_step, v_s5_b_re, v_s5_b_im, v_s5_c_re, v_s5_c_im, v_s5_d, v_s5_glu_w, v_s5_glu_b, v_rg_conv_w, v_rg_conv_b, v_rg_wa, v_rg_ba, v_rg_wx, v_rg_bx, v_rg_lambda, v_ln1_g, v_ln1_b, v_xa_wq, v_xa_wk, v_xa_wv, v_xa_wo, v_ln2_g, v_ln2_b, v_mlp_w1, v_mlp_w2, v_ln3_g, v_ln3_b):
    args = dict(locals())
    weights = {n: args[n] for n in WEIGHT_ORDER}
    mom_m = {n: args["m_" + n] for n in WEIGHT_ORDER}
    mom_v = {n: args["v_" + n] for n in WEIGHT_ORDER}

    shards = []
    for l in range(DEPTH):
        for n, _, _ in LAYER_GATHERED:
            w = weights[n][l]
            if n == "w_in":
                w = jnp.pad(w, ((0, 0), (0, W_IN_PAD - W_IN_SHARD)))
            shards.append(w.astype(MXU_DTYPE))
    handle = _gather_start(shards)
    me = 2 * lax.axis_index("x") + lax.axis_index("y")

    def fetch(l, grp, after):
        ts = [l * N_GATHERED + j for j in WAIT_GROUPS[grp]]
        srcs, landed = _gather_wait(handle, ts, after, name="weights_gather_wait_%d_%d" % (l, grp))
        out = {}
        for t, src, arr in zip(ts, srcs, landed):
            n, shp, ax = LAYER_GATHERED[t % N_GATHERED]
            arr = lax.dynamic_update_slice_in_dim(arr, src, me * shp[ax], axis=ax)
            if n == "w_in":
                arr = _pack_cols(jnp.concatenate(
                    [arr[:, W_IN_PAD * k:W_IN_PAD * k + W_IN_SHARD] for k in range(4)], axis=1))
            out[n] = arr
        return out

    conv_flat = _pad_rows(_pack_shards(weights, CONV_SHARDED), CONV_ROWS)
    full = _gather_full(_xy_allgather(conv_flat, name="conv_weights_allgather"), CONV_SHARDED)
    rep = {n: weights[n] for n, _ in REPLICATED}

    loss11, dx, gsmall, gbuf = _local_step(x[0], mem[0], loss_target[0], full, rep, fetch)
    grad_x = dx[None]
    loss = lax.psum(loss11[0, 0], ("x", "y", "c"))

    gw = gsmall["w_in"].reshape(DEPTH, D_MODEL, 4, W_IN_SHARD)
    gw = jnp.pad(gw, ((0, 0), (0, 0), (0, 0), (0, W_IN_PAD - W_IN_SHARD)))
    w_in_blk = jnp.transpose(gw, (2, 0, 1, 3)).reshape(4, DEPTH * W_IN_PAD, FLAT)
    small_q = _split_shards(gsmall, SMALL_SHARDED)
    rep_q = jnp.pad(_pack_shards(gsmall, REPLICATED), (0, 4 * REP_QROWS * FLAT - REP_ELEMS)).reshape(4, -1)
    misc = jnp.concatenate(
        [jnp.pad(small_q, ((0, 0), (0, MISC_REP_ROW * FLAT - SMALL_ELEMS))), rep_q,
         jnp.zeros((4, (MISC_ROWS - MISC_REP_ROW - REP_QROWS) * FLAT), F32)], axis=1).reshape(4, MISC_ROWS, FLAT)
    gbuf = lax.dynamic_update_slice(gbuf, w_in_blk, (0, ROW_W_IN, 0))
    gbuf = lax.dynamic_update_slice(gbuf, misc, (0, ROW_MISC, 0))
    c_arr = lax.axis_index("c").astype(jnp.int32).reshape(1)
    chip_sum, chip_tail = _add_own_half(gbuf, _c_exchange(gbuf), c_arr)
    got_sum, got_tail = _xy_exchange([chip_sum, chip_tail])
    reduced = _c_allgather_halves(_sum4_into_half(got_sum, got_tail, c_arr))
    misc_red = reduced[ROW_MISC:]
    rep_all = _xy_allgather(misc_red[MISC_REP_ROW:MISC_REP_ROW + REP_QROWS], name="small_grads_allgather")
    g_red = {**_unpack(misc_red[:MISC_REP_ROW].reshape(-1), SMALL_SHARDED),
             **_unpack(rep_all.reshape(-1), REPLICATED)}
    g_red["w_in"] = reduced[ROW_W_IN:ROW_W_IN + DEPTH * W_IN_PAD].reshape(DEPTH, D_MODEL, W_IN_PAD)[:, :, :W_IN_SHARD]

    flat_rows = {"mlp_w1": ROW_MLP_W1, "mlp_w2": ROW_MLP_W2, "w_out": ROW_W_OUT, **ROW_XA}
    res = {}
    for n in WEIGHT_ORDER:
        if n in flat_rows:
            res[n] = _adamw(weights[n], mom_m[n], mom_v[n], reduced, g_row0=flat_rows[n])
        else:
            res[n] = _adamw(weights[n], mom_m[n], mom_v[n], g_red[n])
    return (loss, grad_x, *[res[n][0] for n in WEIGHT_ORDER], *[res[n][1] for n in WEIGHT_ORDER],
            *[res[n][2] for n in WEIGHT_ORDER], *[res[n][3] for n in WEIGHT_ORDER])
```

```python
import functools
import math

import jax
import jax.numpy as jnp
from jax import lax
from jax.experimental import pallas as pl
from jax.experimental.pallas import tpu as pltpu

F32 = jnp.float32
MXU_DTYPE = jnp.bfloat16

D_MODEL = 1024
DEPTH = 2
MEM_LEN = 256
SSD_WIDTH = 512
SSD_HEADS = 8
SSD_STATE = 128
SSD_CHUNK = 128
SSD_XBC = 1024
S5_WIDTH = 256
S5_GROUPS = 16
S5_GROUP_CH = 16
S5_STATE = 64
S5_NSTATE = S5_GROUPS * S5_STATE
RG_WIDTH = 256
RG_BLOCKS = 4
RG_BLOCK_DIM = 64
RG_C = 8.0
XA_HEADS = 4
XA_HEAD_DIM = 256
D_FF = 4096
D_IN = 2312
ALPHA = (2.0 * DEPTH) ** 0.25
LN_EPS = 1e-5
ADAM_LR = 0.001
ADAM_B1 = 0.9
ADAM_B2 = 0.999
ADAM_EPS = 1e-08
ADAM_WD = 0.01
ADAM_STEP = 10

P_XBC, P_Z, P_U, P_XRG, P_GRG, P_DT = 0, 1024, 1536, 1792, 2048, 2304
D_PACK = 2432
O_Z, O_XBC, O_DT, O_U, O_XRG, O_GRG = 0, 512, 1536, 1544, 1800, 2056

LANES = 128
SUBLANES = 8
VMEM_LIMIT = 52 * 1024 * 1024
TM = 512
SSD_TM = 256
SCAN_TM = 512
FLAT = 1024

MESH = pl.DeviceIdType.MESH


def _cparams(sem):
    return pltpu.CompilerParams(dimension_semantics=sem, vmem_limit_bytes=VMEM_LIMIT)


def _dot(a, b):
    return jnp.dot(a.astype(MXU_DTYPE), b.astype(MXU_DTYPE), preferred_element_type=F32)


def _dot_nt(a, b):
    return lax.dot_general(a.astype(MXU_DTYPE), b.astype(MXU_DTYPE), (((1,), (1,)), ((), ())),
                           preferred_element_type=F32)


def _dot_tn(a, b):
    return lax.dot_general(a.astype(MXU_DTYPE), b.astype(MXU_DTYPE), (((0,), (0,)), ((), ())),
                           preferred_element_type=F32)


def _dot_f32(a, b):
    return jnp.dot(a, b, precision=lax.Precision.HIGHEST, preferred_element_type=F32)


def _dot_f32_tn(a, b):
    return lax.dot_general(a, b, (((0,), (0,)), ((), ())), precision=lax.Precision.HIGHEST,
                           preferred_element_type=F32)


def _sigmoid(x):
    return 1.0 / (1.0 + jnp.exp(-x))


def _softplus(x):
    return jnp.maximum(x, 0.0) + jnp.log(1.0 + jnp.exp(-jnp.abs(x)))


_GELU_K = math.sqrt(2.0 / math.pi)


def _gelu(x):
    return 0.5 * x * (1.0 + jnp.tanh(_GELU_K * (x + 0.044715 * x * x * x)))


def _gelu_grad(x):
    t = jnp.tanh(_GELU_K * (x + 0.044715 * x * x * x))
    return 0.5 * (1.0 + t) + 0.5 * x * (1.0 - t * t) * _GELU_K * (1.0 + 3.0 * 0.044715 * x * x)


def _expm1(x):
    small = x * (1.0 + x * (0.5 + x * (1.0 / 6.0 + x * (1.0 / 24.0))))
    return jnp.where(jnp.abs(x) < 0.05, small, jnp.exp(x) - 1.0)


def _sum0(x):
    return jnp.sum(x, axis=0, keepdims=True)


def _ln_fwd(r, g, b):
    mu = jnp.mean(r, axis=-1, keepdims=True)
    xc = r - mu
    var = jnp.mean(xc * xc, axis=-1, keepdims=True)
    rstd = lax.rsqrt(var + LN_EPS)
    xhat = xc * rstd
    return xhat * g + b, xhat, rstd


def _ln_bwd(dout, xhat, rstd, g):
    dxh = dout * g
    m1 = jnp.mean(dxh, axis=-1, keepdims=True)
    m2 = jnp.mean(dxh * xhat, axis=-1, keepdims=True)
    return rstd * (dxh - m1 - xhat * m2)


def _rows(tm, n, col=0):
    return pl.BlockSpec((tm, n), lambda i: (i, col))


def _const(shape):
    nd = len(shape)
    return pl.BlockSpec(shape, lambda i: (0,) * nd)


def _mm(a, w, *, nt=False, add=None, out_dtype=F32, name):
    t, k = a.shape
    n = w.shape[0] if nt else w.shape[1]
    tm = min(TM, t)

    def body(*refs):
        if add is None:
            a_ref, w_ref, o_ref = refs
        else:
            a_ref, w_ref, add_ref, o_ref = refs
        r = _dot_nt(a_ref[...], w_ref[...]) if nt else _dot(a_ref[...], w_ref[...])
        if add is not None:
            r = r + add_ref[...]
        o_ref[...] = r.astype(out_dtype)

    in_specs = [_rows(tm, k), _const(w.shape)]
    args = [a, w]
    if add is not None:
        in_specs.append(_rows(tm, n))
        args.append(add)
    return pl.pallas_call(
        body, name=name, grid=(t // tm,), in_specs=in_specs, out_specs=_rows(tm, n),
        out_shape=jax.ShapeDtypeStruct((t, n), out_dtype), compiler_params=_cparams(("arbitrary",)),
    )(*args)


def _mm_tn(a, g, *, name):
    t, k = a.shape
    n = g.shape[1]
    tt = min(512, t)
    tk = min(1024, k)
    tn = 1024 if n % 1024 == 0 else n
    nsteps = t // tt

    def body(a_ref, g_ref, o_ref):
        s = pl.program_id(2)
        part = _dot_tn(a_ref[...], g_ref[...])

        @pl.when(s == 0)
        def _():
            o_ref[...] = part

        @pl.when(s > 0)
        def _():
            o_ref[...] += part

    return pl.pallas_call(
        body, name=name, grid=(k // tk, n // tn, nsteps),
        in_specs=[pl.BlockSpec((tt, tk), lambda i, j, s: (s, i)), pl.BlockSpec((tt, tn), lambda i, j, s: (s, j))],
        out_specs=pl.BlockSpec((tk, tn), lambda i, j, s: (i, j)),
        out_shape=jax.ShapeDtypeStruct((k, n), F32),
        compiler_params=_cparams(("arbitrary", "arbitrary", "arbitrary")),
    )(a, g)


G_ROWS = 8192
ROW_MLP_W1 = 0
ROW_MLP_W2 = 2048
ROW_W_IN = 4096
ROW_W_OUT = 5376
ROW_XA = {"xa_wq": 5888, "xa_wk": 6400, "xa_wv": 6912, "xa_wo": 7424}
ROW_MISC = 7936
MISC_ROWS = G_ROWS - ROW_MISC
MISC_REP_ROW = 40
W_IN_SHARD = 578
W_IN_PAD = 640


def _wgrad_flat(a, g, buf, *, mode, row_off, name):
    t = a.shape[0]
    tt = min(1024, t)
    ns = t // tt
    blk = D_MODEL

    def accumulate(o_ref, part, s):
        @pl.when(s == 0)
        def _():
            if mode == "rows4":
                for q in range(4):
                    o_ref[q] = part[q * 256:(q + 1) * 256]
            else:
                o_ref[0] = part

        @pl.when(s > 0)
        def _():
            if mode == "rows4":
                for q in range(4):
                    o_ref[q] += part[q * 256:(q + 1) * 256]
            else:
                o_ref[0] += part

    if mode == "rows4":
        grid = (ns,)
        in_specs = [pl.BlockSpec((tt, blk), lambda s: (s, 0)), pl.BlockSpec((tt, blk), lambda s: (s, 0))]
        out_spec = pl.BlockSpec((4, 256, FLAT), lambda s: (0, row_off // 256, 0))
        sem = ("arbitrary",)

        def body(a_ref, g_ref, *rest):
            accumulate(rest[-1], _dot_tn(a_ref[...], g_ref[...]), pl.program_id(0))
    else:
        grid = (4, ns)
        if mode == "rowblk":
            in_specs = [pl.BlockSpec((tt, blk), lambda q, s: (s, q)), pl.BlockSpec((tt, blk), lambda q, s: (s, 0))]
        else:
            in_specs = [pl.BlockSpec((tt, blk), lambda q, s: (s, 0)), pl.BlockSpec((tt, blk), lambda q, s: (s, q))]
        out_spec = pl.BlockSpec((1, blk, FLAT), lambda q, s: (q, row_off // blk, 0))
        sem = ("arbitrary", "arbitrary")

        def body(a_ref, g_ref, *rest):
            accumulate(rest[-1], _dot_tn(a_ref[...], g_ref[...]), pl.program_id(1))

    args = [a, g]
    aliases = {}
    if buf is not None:
        in_specs.append(pl.BlockSpec(memory_space=pl.ANY))
        args.append(buf)
        aliases = {2: 0}
    return pl.pallas_call(
        body, name=name, grid=grid, in_specs=in_specs, out_specs=out_spec,
        out_shape=jax.ShapeDtypeStruct((4, G_ROWS, FLAT), F32), input_output_aliases=aliases,
        compiler_params=_cparams(sem),
    )(*args)


def _outproj_ln_fwd(ycat, h, w, g, b):
    t = h.shape[0]

    def body(y_ref, h_ref, w_ref, g_ref, b_ref, hn_ref, xh_ref, rs_ref):
        r = ALPHA * h_ref[...] + _dot(y_ref[...], w_ref[...])
        out, xhat, rstd = _ln_fwd(r, g_ref[...], b_ref[...])
        hn_ref[...] = out
        xh_ref[...] = xhat
        rs_ref[...] = rstd

    return pl.pallas_call(
        body, name="outproj_ln_fwd", grid=(t // TM,),
        in_specs=[_rows(TM, D_MODEL), _rows(TM, D_MODEL), _const((D_MODEL, D_MODEL)), _const((1, D_MODEL)),
                  _const((1, D_MODEL))],
        out_specs=[_rows(TM, D_MODEL), _rows(TM, D_MODEL), _rows(TM, 1)],
        out_shape=[jax.ShapeDtypeStruct((t, D_MODEL), F32), jax.ShapeDtypeStruct((t, D_MODEL), F32),
                   jax.ShapeDtypeStruct((t, 1), F32)],
        compiler_params=_cparams(("arbitrary",)),
    )(ycat, h, w, g, b)


def _attn_probs(q, kb, hh):
    sl = slice(hh * XA_HEAD_DIM, (hh + 1) * XA_HEAD_DIM)
    s = _dot_nt(q[:, sl], kb[:, sl]) * (1.0 / math.sqrt(XA_HEAD_DIM))
    m = jnp.max(s, axis=-1, keepdims=True)
    e = jnp.exp(s - m)
    return e / jnp.sum(e, axis=-1, keepdims=True)


def _attn_ln_fwd(h1, wq, wo, kb, vb, g, b):
    t = h1.shape[0]

    def body(h_ref, wq_ref, wo_ref, k_ref, v_ref, g_ref, b_ref, hn_ref, xh_ref, rs_ref, o_ref):
        h = h_ref[...]
        q = _dot(h, wq_ref[...])
        kb_ = k_ref[...]
        vb_ = v_ref[...]
        for hh in range(XA_HEADS):
            sl = slice(hh * XA_HEAD_DIM, (hh + 1) * XA_HEAD_DIM)
            p = _attn_probs(q, kb_, hh)
            o_ref[:, sl] = _dot(p, vb_[:, sl]).astype(o_ref.dtype)
        r = ALPHA * h + _dot(o_ref[...], wo_ref[...])
        out, xhat, rstd = _ln_fwd(r, g_ref[...], b_ref[...])
        hn_ref[...] = out
        xh_ref[...] = xhat
        rs_ref[...] = rstd

    return pl.pallas_call(
        body, name="attn_ln_fwd", grid=(t // TM,),
        in_specs=[_rows(TM, D_MODEL), _const((D_MODEL, D_MODEL)), _const((D_MODEL, D_MODEL)),
                  _const((MEM_LEN, D_MODEL)), _const((MEM_LEN, D_MODEL)), _const((1, D_MODEL)), _const((1, D_MODEL))],
        out_specs=[_rows(TM, D_MODEL), _rows(TM, D_MODEL), _rows(TM, 1), _rows(TM, D_MODEL)],
        out_shape=[jax.ShapeDtypeStruct((t, D_MODEL), F32), jax.ShapeDtypeStruct((t, D_MODEL), F32),
                   jax.ShapeDtypeStruct((t, 1), F32), jax.ShapeDtypeStruct((t, D_MODEL), MXU_DTYPE)],
        compiler_params=_cparams(("arbitrary",)),
    )(h1, wq, wo, kb, vb, g, b)


def _attn_ln_bwd(dh2, xhat, rstd, g, h1, wq, wo, kb, vb):
    t = h1.shape[0]

    def body(dh_ref, xh_ref, rs_ref, g_ref, h_ref, wq_ref, wo_ref, k_ref, v_ref,
             dr_ref, dq_ref, dh1_ref, dk_ref, dv_ref, dg_ref, db_ref):
        i = pl.program_id(0)

        @pl.when(i == 0)
        def _():
            dk_ref[...] = jnp.zeros_like(dk_ref)
            dv_ref[...] = jnp.zeros_like(dv_ref)
            dg_ref[...] = jnp.zeros_like(dg_ref)
            db_ref[...] = jnp.zeros_like(db_ref)

        dout = dh_ref[...]
        xh = xh_ref[...]
        dg_ref[...] += _sum0(dout * xh)
        db_ref[...] += _sum0(dout)
        dr = _ln_bwd(dout, xh, rs_ref[...], g_ref[...])
        dr_ref[...] = dr.astype(dr_ref.dtype)
        do = _dot_nt(dr, wo_ref[...])
        h = h_ref[...]
        q = _dot(h, wq_ref[...])
        kb_ = k_ref[...]
        vb_ = v_ref[...]
        scale = 1.0 / math.sqrt(XA_HEAD_DIM)
        for hh in range(XA_HEADS):
            sl = slice(hh * XA_HEAD_DIM, (hh + 1) * XA_HEAD_DIM)
            p = _attn_probs(q, kb_, hh)
            do_h = do[:, sl]
            dp = _dot_nt(do_h, vb_[:, sl])
            ds = p * (dp - jnp.sum(dp * p, axis=-1, keepdims=True)) * scale
            dq_ref[:, sl] = _dot(ds, kb_[:, sl]).astype(dq_ref.dtype)
            dk_ref[:, sl] += _dot_tn(ds, q[:, sl])
            dv_ref[:, sl] += _dot_tn(p, do_h)
        dh1_ref[...] = ALPHA * dr + _dot_nt(dq_ref[...], wq_ref[...])

    return pl.pallas_call(
        body, name="attn_ln_bwd", grid=(t // TM,),
        in_specs=[_rows(TM, D_MODEL), _rows(TM, D_MODEL), _rows(TM, 1), _const((1, D_MODEL)), _rows(TM, D_MODEL),
                  _const((D_MODEL, D_MODEL)), _const((D_MODEL, D_MODEL)), _const((MEM_LEN, D_MODEL)),
                  _const((MEM_LEN, D_MODEL))],
        out_specs=[_rows(TM, D_MODEL), _rows(TM, D_MODEL), _rows(TM, D_MODEL), _const((MEM_LEN, D_MODEL)),
                   _const((MEM_LEN, D_MODEL)), _const((1, D_MODEL)), _const((1, D_MODEL))],
        out_shape=[jax.ShapeDtypeStruct((t, D_MODEL), MXU_DTYPE), jax.ShapeDtypeStruct((t, D_MODEL), MXU_DTYPE),
                   jax.ShapeDtypeStruct((t, D_MODEL), F32), jax.ShapeDtypeStruct((MEM_LEN, D_MODEL), F32),
                   jax.ShapeDtypeStruct((MEM_LEN, D_MODEL), F32), jax.ShapeDtypeStruct((1, D_MODEL), F32),
                   jax.ShapeDtypeStruct((1, D_MODEL), F32)],
        compiler_params=_cparams(("arbitrary",)),
    )(dh2, xhat, rstd, g, h1, wq, wo, kb, vb)


FF_CHUNK = 1024
N_FF = D_FF // FF_CHUNK


def _load_resident(pairs, sems):
    copies = [pltpu.make_async_copy(src, dst, sems.at[k]) for k, (src, dst) in enumerate(pairs)]
    for cp in copies:
        cp.start()
    for cp in copies:
        cp.wait()


def _mlp_ln_fwd(h2, w1, w2, g, b):
    t = h2.shape[0]

    def body(h_ref, w1_hbm, w2_hbm, g_ref, b_ref, hn_ref, xh_ref, rs_ref, hd_ref, w1_v, w2_v, acc_ref, sems):
        @pl.when(pl.program_id(0) == 0)
        def _():
            _load_resident([(w1_hbm, w1_v), (w2_hbm, w2_v)], sems)

        h = h_ref[...]
        hb = h.astype(MXU_DTYPE)
        acc_ref[...] = ALPHA * h
        for j in range(N_FF):
            sl = slice(j * FF_CHUNK, (j + 1) * FF_CHUNK)
            u = _dot(hb, w1_v[:, sl])
            hd = jnp.square(jnp.maximum(u, 0.0)).astype(MXU_DTYPE)
            hd_ref[:, sl] = hd
            acc_ref[...] += _dot(hd, w2_v[sl, :])
        out, xhat, rstd = _ln_fwd(acc_ref[...], g_ref[...], b_ref[...])
        hn_ref[...] = out
        xh_ref[...] = xhat
        rs_ref[...] = rstd

    return pl.pallas_call(
        body, name="mlp_ln_fwd", grid=(t // TM,),
        in_specs=[_rows(TM, D_MODEL), _hbm(), _hbm(), _const((1, D_MODEL)), _const((1, D_MODEL))],
        out_specs=[_rows(TM, D_MODEL), _rows(TM, D_MODEL), _rows(TM, 1), _rows(TM, D_FF)],
        out_shape=[jax.ShapeDtypeStruct((t, D_MODEL), F32), jax.ShapeDtypeStruct((t, D_MODEL), F32),
                   jax.ShapeDtypeStruct((t, 1), F32), jax.ShapeDtypeStruct((t, D_FF), MXU_DTYPE)],
        scratch_shapes=[pltpu.VMEM((D_MODEL, D_FF), MXU_DTYPE), pltpu.VMEM((D_FF, D_MODEL), MXU_DTYPE),
                        pltpu.VMEM((TM, D_MODEL), F32), pltpu.SemaphoreType.DMA((2,))],
        compiler_params=_cparams(("arbitrary",)),
    )(h2, w1, w2, g, b)


def _mlp_ln_bwd(dh3, xhat, rstd, g, hdn, w1, w2):
    t = dh3.shape[0]

    def body(dh_ref, xh_ref, rs_ref, g_ref, hd_ref, w1_hbm, w2_hbm,
             dr_ref, du_ref, dh2_ref, dg_ref, db_ref, w1_v, w2_v, acc_ref, sems):
        @pl.when(pl.program_id(0) == 0)
        def _():
            _load_resident([(w1_hbm, w1_v), (w2_hbm, w2_v)], sems)
            dg_ref[...] = jnp.zeros_like(dg_ref)
            db_ref[...] = jnp.zeros_like(db_ref)

        dout = dh_ref[...]
        xh = xh_ref[...]
        dg_ref[...] += _sum0(dout * xh)
        db_ref[...] += _sum0(dout)
        dr = _ln_bwd(dout, xh, rs_ref[...], g_ref[...])
        drb = dr.astype(MXU_DTYPE)
        dr_ref[...] = drb
        acc_ref[...] = ALPHA * dr
        for j in range(N_FF):
            sl = slice(j * FF_CHUNK, (j + 1) * FF_CHUNK)
            dhd = _dot_nt(drb, w2_v[sl, :])
            du = (dhd * (2.0 * jnp.sqrt(hd_ref[:, sl].astype(F32)))).astype(MXU_DTYPE)
            du_ref[:, sl] = du
            acc_ref[...] += _dot_nt(du, w1_v[:, sl])
        dh2_ref[...] = acc_ref[...]

    tm = TM // 2
    return pl.pallas_call(
        body, name="mlp_ln_bwd", grid=(t // tm,),
        in_specs=[_rows(tm, D_MODEL), _rows(tm, D_MODEL), _rows(tm, 1), _const((1, D_MODEL)), _rows(tm, D_FF),
                  _hbm(), _hbm()],
        out_specs=[_rows(tm, D_MODEL), _rows(tm, D_FF), _rows(tm, D_MODEL), _const((1, D_MODEL)),
                   _const((1, D_MODEL))],
        out_shape=[jax.ShapeDtypeStruct((t, D_MODEL), MXU_DTYPE), jax.ShapeDtypeStruct((t, D_FF), MXU_DTYPE),
                   jax.ShapeDtypeStruct((t, D_MODEL), F32), jax.ShapeDtypeStruct((1, D_MODEL), F32),
                   jax.ShapeDtypeStruct((1, D_MODEL), F32)],
        scratch_shapes=[pltpu.VMEM((D_MODEL, D_FF), MXU_DTYPE), pltpu.VMEM((D_FF, D_MODEL), MXU_DTYPE),
                        pltpu.VMEM((tm, D_MODEL), F32), pltpu.SemaphoreType.DMA((2,))],
        compiler_params=_cparams(("arbitrary",)),
    )(dh3, xhat, rstd, g, hdn, w1, w2)


def _outproj_ln_bwd(dh1, xhat, rstd, g, w):
    t = dh1.shape[0]

    def body(dh_ref, xh_ref, rs_ref, g_ref, w_ref, dr_ref, res_ref, dy_ref, dg_ref, db_ref):
        i = pl.program_id(0)

        @pl.when(i == 0)
        def _():
            dg_ref[...] = jnp.zeros_like(dg_ref)
            db_ref[...] = jnp.zeros_like(db_ref)

        dout = dh_ref[...]
        xh = xh_ref[...]
        dg_ref[...] += _sum0(dout * xh)
        db_ref[...] += _sum0(dout)
        dr = _ln_bwd(dout, xh, rs_ref[...], g_ref[...])
        dr_ref[...] = dr.astype(dr_ref.dtype)
        res_ref[...] = ALPHA * dr
        dy_ref[...] = _dot_nt(dr, w_ref[...])

    return pl.pallas_call(
        body, name="outproj_ln_bwd", grid=(t // TM,),
        in_specs=[_rows(TM, D_MODEL), _rows(TM, D_MODEL), _rows(TM, 1), _const((1, D_MODEL)),
                  _const((D_MODEL, D_MODEL))],
        out_specs=[_rows(TM, D_MODEL), _rows(TM, D_MODEL), _rows(TM, D_MODEL), _const((1, D_MODEL)),
                   _const((1, D_MODEL))],
        out_shape=[jax.ShapeDtypeStruct((t, D_MODEL), MXU_DTYPE), jax.ShapeDtypeStruct((t, D_MODEL), F32),
                   jax.ShapeDtypeStruct((t, D_MODEL), F32), jax.ShapeDtypeStruct((1, D_MODEL), F32),
                   jax.ShapeDtypeStruct((1, D_MODEL), F32)],
        compiler_params=_cparams(("arbitrary",)),
    )(dh1, xhat, rstd, g, w)


def _loss_fwd_bwd(h, target):
    t = h.shape[0]

    def body(h_ref, t_ref, l_ref, dh_ref):
        i = pl.program_id(0)

        @pl.when(i == 0)
        def _():
            l_ref[...] = jnp.zeros_like(l_ref)

        e = h_ref[...] - t_ref[...]
        dh_ref[...] = e * (1.0 / D_MODEL)
        per_tok = jnp.mean(e * e, axis=-1, keepdims=True)
        l_ref[...] += 0.5 * jnp.sum(per_tok, axis=0, keepdims=True)

    return pl.pallas_call(
        body, name="loss_fwd_bwd", grid=(t // TM,),
        in_specs=[_rows(TM, D_MODEL), _rows(TM, D_MODEL)],
        out_specs=[_const((1, 1)), _rows(TM, D_MODEL)],
        out_shape=[jax.ShapeDtypeStruct((1, 1), F32), jax.ShapeDtypeStruct((t, D_MODEL), F32)],
        compiler_params=_cparams(("arbitrary",)),
    )(h, target)


def _pick_col(x, idx):
    lane = lax.broadcasted_iota(jnp.int32, x.shape, 1)
    return jnp.sum(jnp.where(lane == idx, x, 0.0), axis=1, keepdims=True)


def _pick_row(x, idx):
    sub = lax.broadcasted_iota(jnp.int32, x.shape, 0)
    return jnp.sum(jnp.where(sub == idx, x, 0.0), axis=0, keepdims=True)


def _conv_taps(pad_ref, w, tm, base):
    acc = w[0:1, :] * pad_ref[base:base + tm, :]
    for k in range(1, 4):
        acc = acc + w[k:k + 1, :] * pad_ref[base + k:base + k + tm, :]
    return acc


def _ssd_chunk_common(adt_c, tri):
    cs = _dot_f32(tri, adt_c)
    return cs, cs.T, jnp.exp(cs)


def _ssd_head_terms(cs, cst, ecs, dt_c, h, tri):
    cs_col = _pick_col(cs, h)
    cs_row = _pick_row(cst, h)
    dt_col = _pick_col(dt_c, h)
    cs_last = cs_col[SSD_CHUNK - 1:SSD_CHUNK, :]
    lmat = jnp.exp(jnp.where(tri > 0.0, cs_col - cs_row, -1e30))
    ecs_col = _pick_col(ecs, h)
    decay_col = jnp.exp(cs_last - cs_col)
    return cs_col, dt_col, cs_last, lmat, ecs_col, decay_col


def _ssd_fwd(proj, cw, cb, dtb, a_neg, d_lanes, nw):
    t = proj.shape[0]
    tm = SSD_TM
    nt = t // tm
    ncq = tm // SSD_CHUNK
    hb = tm // SUBLANES

    def body(xbc_ref, halo_ref, z_ref, dt_ref, cw_ref, cb_ref, dtb_ref, a_ref, d_ref, nw_ref,
             y_ref, yy_ref, st_ref, xpad, xact, state):
        i = pl.program_id(0)

        @pl.when(i == 0)
        def _():
            state[...] = jnp.zeros_like(state)

        xpad[0:SUBLANES, :] = jnp.where(i > 0, halo_ref[...], 0.0)
        xpad[SUBLANES:SUBLANES + tm, :] = xbc_ref[...]
        acc = cb_ref[...] + _conv_taps(xpad, cw_ref[...], tm, SUBLANES - 3)
        xact[...] = acc * _sigmoid(acc)
        dt = _softplus(dt_ref[...] + dtb_ref[...])
        adt = dt * a_ref[...]
        r_i = lax.broadcasted_iota(jnp.int32, (SSD_CHUNK, SSD_CHUNK), 0)
        c_i = lax.broadcasted_iota(jnp.int32, (SSD_CHUNK, SSD_CHUNK), 1)
        tri = (r_i >= c_i).astype(F32)
        lane1 = lax.broadcasted_iota(jnp.int32, (1, LANES), 1)
        for c in range(ncq):
            sl = slice(c * SSD_CHUNK, (c + 1) * SSD_CHUNK)
            dt_c = dt[sl]
            cs, cst, ecs = _ssd_chunk_common(adt[sl], tri)
            for g in range(2):
                bg = xact[sl, 512 + g * 128:512 + (g + 1) * 128]
                cg = xact[sl, 768 + g * 128:768 + (g + 1) * 128]
                cbm = _dot_nt(cg, bg)
                for pr in range(2):
                    pi = g * 2 + pr
                    psl = slice(pi * 128, (pi + 1) * 128)
                    xp = xact[sl, psl]
                    prev = state[pi]
                    st_ref[c, pi] = prev
                    yp = xp * d_ref[:, psl]
                    new_s = jnp.zeros((SSD_STATE, LANES), F32)
                    dec_lane = jnp.zeros((1, LANES), F32)
                    for hh in range(2):
                        h = g * 4 + pr * 2 + hh
                        lm = (lane1 >= 64) if hh else (lane1 < 64)
                        _, dt_col, cs_last, lmat, ecs_col, decay_col = _ssd_head_terms(cs, cst, ecs, dt_c, h, tri)
                        xdt = jnp.where(lm, xp, 0.0) * dt_col
                        yp = yp + _dot(cbm * lmat, xdt)
                        yp = yp + _dot(cg * ecs_col, jnp.where(lm, prev, 0.0))
                        new_s = new_s + _dot_tn(bg * decay_col, xdt)
                        dec_lane = dec_lane + jnp.where(lm, jnp.exp(cs_last), 0.0)
                    state[pi] = prev * dec_lane + new_s
                    yy_ref[sl, psl] = yp
        yy = yy_ref[...]
        z = z_ref[...]
        yg = yy * (z * _sigmoid(z))
        ms = jnp.mean(yg * yg, axis=-1, keepdims=True)
        y_ref[...] = yg * lax.rsqrt(ms + LN_EPS) * nw_ref[...]

    halo_map = lambda i: (jnp.maximum(i * hb - 1, 0), 0)
    return pl.pallas_call(
        body, name="ssd_fwd", grid=(nt,),
        in_specs=[pl.BlockSpec((tm, SSD_XBC), lambda i: (i, 0)), pl.BlockSpec((SUBLANES, SSD_XBC), halo_map),
                  pl.BlockSpec((tm, SSD_WIDTH), lambda i: (i, P_Z // SSD_WIDTH)),
                  pl.BlockSpec((tm, LANES), lambda i: (i, P_DT // LANES)),
                  _const((4, SSD_XBC)), _const((1, SSD_XBC)), _const((1, LANES)), _const((1, LANES)),
                  _const((1, SSD_WIDTH)), _const((1, SSD_WIDTH))],
        out_specs=[_rows(tm, SSD_WIDTH), _rows(tm, SSD_WIDTH),
                   pl.BlockSpec((ncq, 4, SSD_STATE, LANES), lambda i: (i, 0, 0, 0))],
        out_shape=[jax.ShapeDtypeStruct((t, SSD_WIDTH), F32), jax.ShapeDtypeStruct((t, SSD_WIDTH), F32),
                   jax.ShapeDtypeStruct((t // SSD_CHUNK, 4, SSD_STATE, LANES), F32)],
        scratch_shapes=[pltpu.VMEM((tm + SUBLANES, SSD_XBC), F32), pltpu.VMEM((tm, SSD_XBC), F32),
                        pltpu.VMEM((4, SSD_STATE, LANES), F32)],
        compiler_params=_cparams(("arbitrary",)),
    )(proj, proj, proj, proj, cw, cb, dtb, a_neg, d_lanes, nw)


def _ssd_bwd(dycat, proj, yy, states, cw, cb, dtb, a_neg, d_lanes, nw):
    t = proj.shape[0]
    tm = SSD_TM
    nt = t // tm
    ncq = tm // SSD_CHUNK
    hb = tm // SUBLANES

    def body(dy_ref, xbc_ref, halo_ref, z_ref, dt_ref, yy_ref, st_ref, cw_ref, cb_ref, dtb_ref, a_ref, d_ref, nw_ref,
             dxbc_ref, dz_ref, ddt_ref, dcw_ref, dcb_ref, ddtb_ref, da_ref, dd_ref, dnw_ref,
             xpad, xact, dxact, dpad, dstate, dnext):
        i = pl.program_id(0)

        @pl.when(i == 0)
        def _():
            for r in (dcw_ref, dcb_ref, ddtb_ref, da_ref, dd_ref, dnw_ref, dstate, dnext):
                r[...] = jnp.zeros_like(r)

        xpad[0:SUBLANES, :] = jnp.where(i < nt - 1, halo_ref[...], 0.0)
        xpad[SUBLANES:SUBLANES + tm, :] = xbc_ref[...]
        cw_v = cw_ref[...]
        acc = cb_ref[...] + _conv_taps(xpad, cw_v, tm, SUBLANES - 3)
        sig = _sigmoid(acc)
        xact[...] = acc * sig
        dt_raw = dt_ref[...] + dtb_ref[...]
        dt = _softplus(dt_raw)
        a_v = a_ref[...]
        adt = dt * a_v
        yy = yy_ref[...]
        z = z_ref[...]
        sz = _sigmoid(z)
        siluz = z * sz
        yg = yy * siluz
        ms = jnp.mean(yg * yg, axis=-1, keepdims=True)
        rinv = lax.rsqrt(ms + LN_EPS)
        dout = dy_ref[...]
        dnw_ref[...] += _sum0(dout * yg * rinv)
        dyn = dout * nw_ref[...]
        dyg = rinv * dyn - yg * (rinv * rinv * rinv) * jnp.mean(dyn * yg, axis=-1, keepdims=True)
        dyy = dyg * siluz
        dz_ref[...] = dyg * yy * (sz * (1.0 + z * (1.0 - sz)))
        dd_ref[...] += _sum0(dyy * xact[:, 0:SSD_WIDTH])

        r_i = lax.broadcasted_iota(jnp.int32, (SSD_CHUNK, SSD_CHUNK), 0)
        c_i = lax.broadcasted_iota(jnp.int32, (SSD_CHUNK, SSD_CHUNK), 1)
        tri = (r_i >= c_i).astype(F32)
        lane1 = lax.broadcasted_iota(jnp.int32, (1, LANES), 1)
        for c in reversed(range(ncq)):
            sl = slice(c * SSD_CHUNK, (c + 1) * SSD_CHUNK)
            dt_c = dt[sl]
            cs, cst, ecs = _ssd_chunk_common(adt[sl], tri)
            cacc = jnp.zeros((SSD_CHUNK, LANES), F32)
            racc = jnp.zeros((SSD_CHUNK, LANES), F32)
            ddtx = jnp.zeros((SSD_CHUNK, LANES), F32)
            for g in range(2):
                bg = xact[sl, 512 + g * 128:512 + (g + 1) * 128]
                cg = xact[sl, 768 + g * 128:768 + (g + 1) * 128]
                cbm = _dot_nt(cg, bg)
                dcb_m = jnp.zeros((SSD_CHUNK, SSD_CHUNK), F32)
                dbg = jnp.zeros((SSD_CHUNK, SSD_STATE), F32)
                dcg = jnp.zeros((SSD_CHUNK, SSD_STATE), F32)
                for pr in range(2):
                    pi = g * 2 + pr
                    psl = slice(pi * 128, (pi + 1) * 128)
                    xp = xact[sl, psl]
                    dyp = dyy[sl, psl]
                    prev = st_ref[c, pi]
                    ds_all = dstate[pi]
                    dxdt_p = jnp.zeros((SSD_CHUNK, LANES), F32)
                    dprev_new = jnp.zeros((SSD_STATE, LANES), F32)
                    dec_lane = jnp.zeros((1, LANES), F32)
                    dt_lanes = jnp.zeros((SSD_CHUNK, LANES), F32)
                    for hh in range(2):
                        h = g * 4 + pr * 2 + hh
                        lm = (lane1 >= 64) if hh else (lane1 < 64)
                        oh_l = (c_i == h).astype(F32)
                        oh_s = (r_i == h).astype(F32)
                        _, dt_col, cs_last, lmat, ecs_col, decay_col = _ssd_head_terms(cs, cst, ecs, dt_c, h, tri)
                        gm = cbm * lmat
                        xm = jnp.where(lm, xp, 0.0)
                        xdt = xm * dt_col
                        dym = jnp.where(lm, dyp, 0.0)
                        prevm = jnp.where(lm, prev, 0.0)
                        dsm = jnp.where(lm, ds_all, 0.0)
                        bdec = bg * decay_col
                        dxdt = _dot_tn(gm, dym) + _dot(bdec, dsm)
                        dxdt_p = dxdt_p + dxdt
                        ddtx = ddtx + oh_l * jnp.sum(dxdt * xm, axis=1, keepdims=True)
                        dt_lanes = dt_lanes + jnp.where(lm, dt_col, 0.0)
                        dgm = _dot_nt(dym, xdt)
                        dcb_m = dcb_m + dgm * lmat
                        w = dgm * gm
                        cacc = cacc + oh_l * jnp.sum(w, axis=1, keepdims=True)
                        racc = racc - oh_s * jnp.sum(w, axis=0, keepdims=True)
                        dce = _dot_nt(dym, prevm)
                        dcg = dcg + dce * ecs_col
                        cacc = cacc + oh_l * (jnp.sum(dce * cg, axis=1, keepdims=True) * ecs_col)
                        dprev_new = dprev_new + _dot_tn(cg * ecs_col, dym)
                        dbdec = _dot_nt(xdt, dsm)
                        dbg = dbg + dbdec * decay_col
                        dd = jnp.sum(dbdec * bg, axis=1, keepdims=True) * decay_col
                        cacc = cacc - oh_l * dd
                        cd = jnp.exp(cs_last)
                        dlast = jnp.sum(dd, axis=0, keepdims=True) + jnp.sum(
                            jnp.sum(dsm * prevm, axis=1, keepdims=True), axis=0, keepdims=True) * cd
                        cacc = cacc + jnp.where((r_i == SSD_CHUNK - 1) & (c_i == h), dlast, 0.0)
                        dec_lane = dec_lane + jnp.where(lm, cd, 0.0)
                    dstate[pi] = ds_all * dec_lane + dprev_new
                    dxact[sl, psl] = dxdt_p * dt_lanes + dyp * d_ref[:, psl]
                dcg = dcg + _dot(dcb_m, bg)
                dbg = dbg + _dot_tn(dcb_m, cg)
                dxact[sl, 512 + g * 128:512 + (g + 1) * 128] = dbg
                dxact[sl, 768 + g * 128:768 + (g + 1) * 128] = dcg
            dcs = cacc + racc.T
            dadt = _dot_f32((r_i <= c_i).astype(F32), dcs)
            ddt = dadt * a_v + ddtx
            da_ref[...] += _sum0(dadt * dt_c)
            ddt_raw = ddt * _sigmoid(dt_raw[sl])
            ddt_ref[sl, :] = ddt_raw
            ddtb_ref[...] += _sum0(ddt_raw)
        dacc = dxact[...] * (sig * (1.0 + acc * (1.0 - sig)))
        dcb_ref[...] += _sum0(dacc)
        for k in range(4):
            dcw_ref[k:k + 1, :] += _sum0(dacc * xpad[SUBLANES - 3 + k:SUBLANES - 3 + k + tm, :])
        dpad[0:tm, :] = dacc
        dpad[tm:tm + SUBLANES, :] = dnext[...]
        dx = cw_v[0:1, :] * dpad[3:3 + tm, :]
        for k in range(1, 4):
            dx = dx + cw_v[k:k + 1, :] * dpad[3 - k:3 - k + tm, :]
        dxbc_ref[...] = dx
        dnext[...] = dacc[0:SUBLANES, :]

    rev = lambda i: nt - 1 - i
    halo_map = lambda i: (jnp.maximum(rev(i) * hb - 1, 0), 0)
    rrow = lambda n, col=0: pl.BlockSpec((tm, n), lambda i: (rev(i), col))
    return pl.pallas_call(
        body, name="ssd_bwd", grid=(nt,),
        in_specs=[rrow(SSD_WIDTH), rrow(SSD_XBC), pl.BlockSpec((SUBLANES, SSD_XBC), halo_map),
                  rrow(SSD_WIDTH, P_Z // SSD_WIDTH), rrow(LANES, P_DT // LANES), rrow(SSD_WIDTH),
                  pl.BlockSpec((ncq, 4, SSD_STATE, LANES), lambda i: (rev(i), 0, 0, 0)),
                  _const((4, SSD_XBC)), _const((1, SSD_XBC)), _const((1, LANES)), _const((1, LANES)),
                  _const((1, SSD_WIDTH)), _const((1, SSD_WIDTH))],
        out_specs=[rrow(SSD_XBC), rrow(SSD_WIDTH), rrow(LANES), _const((SUBLANES, SSD_XBC)), _const((1, SSD_XBC)),
                   _const((1, LANES)), _const((1, LANES)), _const((1, SSD_WIDTH)), _const((1, SSD_WIDTH))],
        out_shape=[jax.ShapeDtypeStruct((t, SSD_XBC), F32), jax.ShapeDtypeStruct((t, SSD_WIDTH), F32),
                   jax.ShapeDtypeStruct((t, LANES), F32), jax.ShapeDtypeStruct((SUBLANES, SSD_XBC), F32),
                   jax.ShapeDtypeStruct((1, SSD_XBC), F32), jax.ShapeDtypeStruct((1, LANES), F32),
                   jax.ShapeDtypeStruct((1, LANES), F32), jax.ShapeDtypeStruct((1, SSD_WIDTH), F32),
                   jax.ShapeDtypeStruct((1, SSD_WIDTH), F32)],
        scratch_shapes=[pltpu.VMEM((tm + SUBLANES, SSD_XBC), F32), pltpu.VMEM((tm, SSD_XBC), F32),
                        pltpu.VMEM((tm, SSD_XBC), F32), pltpu.VMEM((tm + SUBLANES, SSD_XBC), F32),
                        pltpu.VMEM((4, SSD_STATE, LANES), F32), pltpu.VMEM((SUBLANES, SSD_XBC), F32)],
        compiler_params=_cparams(("arbitrary",)),
    )(dycat, proj, proj, proj, proj, yy, states, cw, cb, dtb, a_neg, d_lanes, nw)


def _cmul_add(ar, ai, br, bi, cr, ci):
    return ar + br * cr - bi * ci, ai + br * ci + bi * cr


def _s5_fwd(proj, bre, bim, cre, cim, d_skip, glu_w, glu_b, coef):
    t = proj.shape[0]
    tm = SCAN_TM
    ng = tm // SUBLANES

    def body(u_ref, bre_ref, bim_ref, cre_ref, cim_ref, d_ref, w_ref, b_ref, coef_ref,
             y_ref, y2_ref, hre_ref, him_ref, carry):
        i = pl.program_id(0)

        @pl.when(i == 0)
        def _():
            carry[...] = jnp.zeros_like(carry)

        u = u_ref[...]
        hre_ref[...] = _dot(u, bre_ref[...])
        him_ref[...] = _dot(u, bim_ref[...])

        def step(gi, car):
            cr_, ci_ = car
            rows = pl.ds(pl.multiple_of(gi * SUBLANES, SUBLANES), SUBLANES)
            r = hre_ref[rows, :]
            m = him_ref[rows, :]
            for k, sh in enumerate((1, 2, 4)):
                r, m = _cmul_add(r, m, coef_ref[k, 0], coef_ref[k, 1], pltpu.roll(r, sh, 0), pltpu.roll(m, sh, 0))
            r, m = _cmul_add(r, m, coef_ref[3, 0], coef_ref[3, 1], cr_, ci_)
            hre_ref[rows, :] = r
            him_ref[rows, :] = m
            return (jnp.broadcast_to(r[SUBLANES - 1:SUBLANES, :], r.shape),
                    jnp.broadcast_to(m[SUBLANES - 1:SUBLANES, :], m.shape))

        cr_, ci_ = lax.fori_loop(0, ng, step, (carry[0], carry[1]))
        carry[0] = cr_
        carry[1] = ci_
        y2 = _dot(hre_ref[...], cre_ref[...]) - _dot(him_ref[...], cim_ref[...]) + d_ref[...] * u
        y2_ref[...] = y2
        ya = _gelu(y2)
        y_ref[...] = ya * _sigmoid(_dot(ya, w_ref[...]) + b_ref[...])

    return pl.pallas_call(
        body, name="s5_fwd", grid=(t // tm,),
        in_specs=[pl.BlockSpec((tm, S5_WIDTH), lambda i: (i, P_U // S5_WIDTH)),
                  _const((S5_WIDTH, S5_NSTATE)), _const((S5_WIDTH, S5_NSTATE)), _const((S5_NSTATE, S5_WIDTH)),
                  _const((S5_NSTATE, S5_WIDTH)), _const((1, S5_WIDTH)), _const((S5_WIDTH, S5_WIDTH)),
                  _const((1, S5_WIDTH)), _const((5, 2, SUBLANES, S5_NSTATE))],
        out_specs=[_rows(tm, S5_WIDTH), _rows(tm, S5_WIDTH), _rows(tm, S5_NSTATE), _rows(tm, S5_NSTATE)],
        out_shape=[jax.ShapeDtypeStruct((t, S5_WIDTH), F32), jax.ShapeDtypeStruct((t, S5_WIDTH), F32),
                   jax.ShapeDtypeStruct((t, S5_NSTATE), F32), jax.ShapeDtypeStruct((t, S5_NSTATE), F32)],
        scratch_shapes=[pltpu.VMEM((2, SUBLANES, S5_NSTATE), F32)],
        compiler_params=_cparams(("arbitrary",)),
    )(proj, bre, bim, cre, cim, d_skip, glu_w, glu_b, coef)


def _s5_bwd(dycat, proj, y2, hre, him, bre, bim, cre, cim, d_skip, glu_w, glu_b, rcoef):
    t = proj.shape[0]
    tm = SCAN_TM
    nt = t // tm
    ng = tm // SUBLANES
    hb = tm // SUBLANES

    def body(dy_ref, u_ref, y2_ref, hre_ref, him_ref, hre_halo, him_halo, bre_ref, bim_ref, cre_ref, cim_ref, d_ref,
             w_ref, b_ref, coef_ref,
             du_ref, dbre_ref, dbim_ref, dcre_ref, dcim_ref, dlam_ref, dd_ref, dw_ref, dgb_ref,
             gre, gim, hpre, hpim, carry):
        i = pl.program_id(0)

        @pl.when(i == 0)
        def _():
            for r in (dbre_ref, dbim_ref, dcre_ref, dcim_ref, dlam_ref, dd_ref, dw_ref, dgb_ref, carry):
                r[...] = jnp.zeros_like(r)

        u = u_ref[...]
        y2 = y2_ref[...]
        dout = dy_ref[...]
        ya = _gelu(y2)
        sg = _sigmoid(_dot(ya, w_ref[...]) + b_ref[...])
        dv = dout * ya * sg * (1.0 - sg)
        dya = dout * sg + _dot_nt(dv, w_ref[...])
        dw_ref[...] += _dot_tn(ya, dv)
        dgb_ref[...] += _sum0(dv)
        dy2 = dya * _gelu_grad(y2)
        dd_ref[...] += _sum0(dy2 * u)
        hre_v = hre_ref[...]
        him_v = him_ref[...]
        dcre_ref[...] += _dot_tn(hre_v, dy2)
        dcim_ref[...] -= _dot_tn(him_v, dy2)
        gre[...] = _dot_nt(dy2, cre_ref[...])
        gim[...] = -_dot_nt(dy2, cim_ref[...])
        first = i == nt - 1
        hpre[0:SUBLANES, :] = jnp.where(first, 0.0, hre_halo[...])
        hpim[0:SUBLANES, :] = jnp.where(first, 0.0, him_halo[...])
        hpre[SUBLANES:SUBLANES + tm, :] = hre_v
        hpim[SUBLANES:SUBLANES + tm, :] = him_v
        row0 = lax.broadcasted_iota(jnp.int32, (SUBLANES, S5_NSTATE), 0) == 0

        def step(k, car):
            cr_, ci_, dlr, dli = car
            gi = ng - 1 - k
            rows = pl.ds(pl.multiple_of(gi * SUBLANES, SUBLANES), SUBLANES)
            nrows = pl.ds(pl.multiple_of(gi * SUBLANES + SUBLANES, SUBLANES), SUBLANES)
            r = gre[rows, :]
            m = gim[rows, :]
            for kk, sh in enumerate((1, 2, 4)):
                r, m = _cmul_add(r, m, coef_ref[kk, 0], coef_ref[kk, 1], pltpu.roll(r, SUBLANES - sh, 0),
                                 pltpu.roll(m, SUBLANES - sh, 0))
            r, m = _cmul_add(r, m, coef_ref[3, 0], coef_ref[3, 1], cr_, ci_)
            gre[rows, :] = r
            gim[rows, :] = m
            pr_ = hpre[rows, :]
            pm_ = hpim[rows, :]
            hr_ = jnp.where(row0, jnp.broadcast_to(pr_[SUBLANES - 1:SUBLANES, :], pr_.shape),
                            pltpu.roll(hpre[nrows, :], 1, 0))
            hm_ = jnp.where(row0, jnp.broadcast_to(pm_[SUBLANES - 1:SUBLANES, :], pm_.shape),
                            pltpu.roll(hpim[nrows, :], 1, 0))
            dlr = dlr + hr_ * r + hm_ * m
            dli = dli + hr_ * m - hm_ * r
            return (jnp.broadcast_to(r[0:1, :], r.shape), jnp.broadcast_to(m[0:1, :], m.shape), dlr, dli)

        z8 = jnp.zeros((SUBLANES, S5_NSTATE), F32)
        cr_, ci_, dlr, dli = lax.fori_loop(0, ng, step, (carry[0], carry[1], z8, z8))
        carry[0] = cr_
        carry[1] = ci_
        dlam_ref[0] += dlr
        dlam_ref[1] += dli
        g_re = gre[...]
        g_im = gim[...]
        du_ref[...] = dy2 * d_ref[...] + _dot_nt(g_re, bre_ref[...]) + _dot_nt(g_im, bim_ref[...])
        dbre_ref[...] += _dot_tn(u, g_re)
        dbim_ref[...] += _dot_tn(u, g_im)

    rev = lambda i: nt - 1 - i
    rrow = lambda n, col=0: pl.BlockSpec((tm, n), lambda i: (rev(i), col))
    halo = pl.BlockSpec((SUBLANES, S5_NSTATE), lambda i: (jnp.maximum(rev(i) * hb - 1, 0), 0))
    return pl.pallas_call(
        body, name="s5_bwd", grid=(nt,),
        in_specs=[rrow(S5_WIDTH, 512 // S5_WIDTH), rrow(S5_WIDTH, P_U // S5_WIDTH), rrow(S5_WIDTH),
                  rrow(S5_NSTATE), rrow(S5_NSTATE), halo, halo,
                  _const((S5_WIDTH, S5_NSTATE)), _const((S5_WIDTH, S5_NSTATE)), _const((S5_NSTATE, S5_WIDTH)),
                  _const((S5_NSTATE, S5_WIDTH)), _const((1, S5_WIDTH)), _const((S5_WIDTH, S5_WIDTH)),
                  _const((1, S5_WIDTH)), _const((5, 2, SUBLANES, S5_NSTATE))],
        out_specs=[rrow(S5_WIDTH), _const((S5_WIDTH, S5_NSTATE)), _const((S5_WIDTH, S5_NSTATE)),
                   _const((S5_NSTATE, S5_WIDTH)), _const((S5_NSTATE, S5_WIDTH)), _const((2, SUBLANES, S5_NSTATE)),
                   _const((1, S5_WIDTH)), _const((S5_WIDTH, S5_WIDTH)), _const((1, S5_WIDTH))],
        out_shape=[jax.ShapeDtypeStruct((t, S5_WIDTH), F32), jax.ShapeDtypeStruct((S5_WIDTH, S5_NSTATE), F32),
                   jax.ShapeDtypeStruct((S5_WIDTH, S5_NSTATE), F32), jax.ShapeDtypeStruct((S5_NSTATE, S5_WIDTH), F32),
                   jax.ShapeDtypeStruct((S5_NSTATE, S5_WIDTH), F32),
                   jax.ShapeDtypeStruct((2, SUBLANES, S5_NSTATE), F32), jax.ShapeDtypeStruct((1, S5_WIDTH), F32),
                   jax.ShapeDtypeStruct((S5_WIDTH, S5_WIDTH), F32), jax.ShapeDtypeStruct((1, S5_WIDTH), F32)],
        scratch_shapes=[pltpu.VMEM((tm, S5_NSTATE), F32), pltpu.VMEM((tm, S5_NSTATE), F32),
                        pltpu.VMEM((tm + SUBLANES, S5_NSTATE), F32), pltpu.VMEM((tm + SUBLANES, S5_NSTATE), F32),
                        pltpu.VMEM((2, SUBLANES, S5_NSTATE), F32)],
        compiler_params=_cparams(("arbitrary",)),
    )(dycat, proj, y2, hre, him, hre, him, bre, bim, cre, cim, d_skip, glu_w, glu_b, rcoef)


def _rg_gates(xc, wa, ba, wx, bx, nsp):
    r = _sigmoid(_dot(xc, wa) + ba)
    ig = _sigmoid(_dot(xc, wx) + bx)
    log_a = nsp * r
    a = jnp.exp(log_a)
    mult = jnp.sqrt(-_expm1(2.0 * log_a))
    return r, ig, a, mult


def _rg_fwd(proj, cw, cb, wa, ba, wx, bx, nsp):
    t = proj.shape[0]
    tm = SCAN_TM
    ng = tm // SUBLANES
    hb = tm // SUBLANES

    def body(x_ref, halo_ref, gt_ref, cw_ref, cb_ref, wa_ref, ba_ref, wx_ref, bx_ref, nsp_ref,
             y_ref, h_ref, xpad, abuf, carry):
        i = pl.program_id(0)

        @pl.when(i == 0)
        def _():
            carry[...] = jnp.zeros_like(carry)

        xpad[0:SUBLANES, :] = jnp.where(i > 0, halo_ref[...], 0.0)
        xpad[SUBLANES:SUBLANES + tm, :] = x_ref[...]
        xc = cb_ref[...] + _conv_taps(xpad, cw_ref[...], tm, SUBLANES - 3)
        _, ig, a, mult = _rg_gates(xc, wa_ref[...], ba_ref[...], wx_ref[...], bx_ref[...], nsp_ref[...])
        abuf[...] = a
        h_ref[...] = mult * (ig * xc)
        sub = lax.broadcasted_iota(jnp.int32, (SUBLANES, RG_WIDTH), 0)

        def step(gi, car):
            rows = pl.ds(pl.multiple_of(gi * SUBLANES, SUBLANES), SUBLANES)
            av = abuf[rows, :]
            bv = h_ref[rows, :]
            for sh in (1, 2, 4):
                m = sub >= sh
                bv = jnp.where(m, av * pltpu.roll(bv, sh, 0) + bv, bv)
                av = jnp.where(m, av * pltpu.roll(av, sh, 0), av)
            hv = bv + av * car
            h_ref[rows, :] = hv
            return jnp.broadcast_to(hv[SUBLANES - 1:SUBLANES, :], hv.shape)

        carry[...] = lax.fori_loop(0, ng, step, carry[...])
        y_ref[...] = h_ref[...] * _gelu(gt_ref[...])

    return pl.pallas_call(
        body, name="rg_fwd", grid=(t // tm,),
        in_specs=[pl.BlockSpec((tm, RG_WIDTH), lambda i: (i, P_XRG // RG_WIDTH)),
                  pl.BlockSpec((SUBLANES, RG_WIDTH), lambda i: (jnp.maximum(i * hb - 1, 0), P_XRG // RG_WIDTH)),
                  pl.BlockSpec((tm, RG_WIDTH), lambda i: (i, P_GRG // RG_WIDTH)),
                  _const((4, RG_WIDTH)), _const((1, RG_WIDTH)), _const((RG_WIDTH, RG_WIDTH)), _const((1, RG_WIDTH)),
                  _const((RG_WIDTH, RG_WIDTH)), _const((1, RG_WIDTH)), _const((1, RG_WIDTH))],
        out_specs=[_rows(tm, RG_WIDTH), _rows(tm, RG_WIDTH)],
        out_shape=[jax.ShapeDtypeStruct((t, RG_WIDTH), F32), jax.ShapeDtypeStruct((t, RG_WIDTH), F32)],
        scratch_shapes=[pltpu.VMEM((tm + SUBLANES, RG_WIDTH), F32), pltpu.VMEM((tm, RG_WIDTH), F32),
                        pltpu.VMEM((SUBLANES, RG_WIDTH), F32)],
        compiler_params=_cparams(("arbitrary",)),
    )(proj, proj, proj, cw, cb, wa, ba, wx, bx, nsp)


def _rg_bwd(dycat, proj, hs, cw, cb, wa, ba, wx, bx, nsp):
    t = proj.shape[0]
    tm = SCAN_TM
    nt = t // tm
    ng = tm // SUBLANES
    hb = tm // SUBLANES

    def body(dy_ref, x_ref, halo_ref, gt_ref, h_ref, h_halo, cw_ref, cb_ref, wa_ref, ba_ref, wx_ref, bx_ref, nsp_ref,
             dx_ref, dgt_ref, dcw_ref, dcb_ref, dwa_ref, dba_ref, dwx_ref, dbx_ref, dnsp_ref,
             xpad, abuf, gbuf, hpad, dabuf, dpad, carry, dnext):
        i = pl.program_id(0)

        @pl.when(i == 0)
        def _():
            for r in (dcw_ref, dcb_ref, dwa_ref, dba_ref, dwx_ref, dbx_ref, dnsp_ref, carry, dnext):
                r[...] = jnp.zeros_like(r)

        first = i == nt - 1
        xpad[0:SUBLANES, :] = jnp.where(first, 0.0, halo_ref[...])
        xpad[SUBLANES:SUBLANES + tm, :] = x_ref[...]
        cw_v = cw_ref[...]
        xc = cb_ref[...] + _conv_taps(xpad, cw_v, tm, SUBLANES - 3)
        nsp_v = nsp_ref[...]
        r, ig, a, mult = _rg_gates(xc, wa_ref[...], ba_ref[...], wx_ref[...], bx_ref[...], nsp_v)
        abuf[...] = a
        hv = h_ref[...]
        hpad[0:SUBLANES, :] = jnp.where(first, 0.0, h_halo[...])
        hpad[SUBLANES:SUBLANES + tm, :] = hv
        gt = gt_ref[...]
        dout = dy_ref[...]
        dgt_ref[...] = dout * hv * _gelu_grad(gt)
        gbuf[...] = dout * _gelu(gt)
        sub = lax.broadcasted_iota(jnp.int32, (SUBLANES, RG_WIDTH), 0)
        last_row = sub == SUBLANES - 1
        row0 = sub == 0

        def step(k, car):
            gi = ng - 1 - k
            rows = pl.ds(pl.multiple_of(gi * SUBLANES, SUBLANES), SUBLANES)
            nrows = pl.ds(pl.multiple_of(gi * SUBLANES + SUBLANES, SUBLANES), SUBLANES)
            av = abuf[rows, :]
            bv = gbuf[rows, :] + jnp.where(last_row, car, 0.0)
            ev = jnp.where(last_row, 0.0, pltpu.roll(av, SUBLANES - 1, 0))
            for sh in (1, 2, 4):
                m = sub < SUBLANES - sh
                bv = jnp.where(m, bv + ev * pltpu.roll(bv, SUBLANES - sh, 0), bv)
                ev = jnp.where(m, ev * pltpu.roll(ev, SUBLANES - sh, 0), 0.0)
            gbuf[rows, :] = bv
            pv = hpad[rows, :]
            hprev = jnp.where(row0, jnp.broadcast_to(pv[SUBLANES - 1:SUBLANES, :], pv.shape),
                              pltpu.roll(hpad[nrows, :], 1, 0))
            dabuf[rows, :] = bv * hprev
            return jnp.broadcast_to((av * bv)[0:1, :], bv.shape)

        carry[...] = lax.fori_loop(0, ng, step, carry[...])
        gv = gbuf[...]
        da = dabuf[...]
        ix = ig * xc
        dmult = gv * ix
        dig = gv * mult * xc
        dxc = gv * mult * ig
        dlog_a = da * a - dmult * (a * a) / mult
        dnsp_ref[...] += _sum0(dlog_a * r)
        dpr = dlog_a * nsp_v * r * (1.0 - r)
        dpi = dig * ig * (1.0 - ig)
        dxc = dxc + _dot_nt(dpr, wa_ref[...]) + _dot_nt(dpi, wx_ref[...])
        dwa_ref[...] += _dot_tn(xc, dpr)
        dwx_ref[...] += _dot_tn(xc, dpi)
        dba_ref[...] += _sum0(dpr)
        dbx_ref[...] += _sum0(dpi)
        dcb_ref[...] += _sum0(dxc)
        for k in range(4):
            dcw_ref[k:k + 1, :] += _sum0(dxc * xpad[SUBLANES - 3 + k:SUBLANES - 3 + k + tm, :])
        dpad[0:tm, :] = dxc
        dpad[tm:tm + SUBLANES, :] = dnext[...]
        dx = cw_v[0:1, :] * dpad[3:3 + tm, :]
        for k in range(1, 4):
            dx = dx + cw_v[k:k + 1, :] * dpad[3 - k:3 - k + tm, :]
        dx_ref[...] = dx
        dnext[...] = dxc[0:SUBLANES, :]

    rev = lambda i: nt - 1 - i
    rrow = lambda n, col=0: pl.BlockSpec((tm, n), lambda i: (rev(i), col))
    sq = _const((RG_WIDTH, RG_WIDTH))
    vec = _const((1, RG_WIDTH))
    return pl.pallas_call(
        body, name="rg_bwd", grid=(nt,),
        in_specs=[rrow(RG_WIDTH, 768 // RG_WIDTH), rrow(RG_WIDTH, P_XRG // RG_WIDTH),
                  pl.BlockSpec((SUBLANES, RG_WIDTH), lambda i: (jnp.maximum(rev(i) * hb - 1, 0), P_XRG // RG_WIDTH)),
                  rrow(RG_WIDTH, P_GRG // RG_WIDTH), rrow(RG_WIDTH),
                  pl.BlockSpec((SUBLANES, RG_WIDTH), lambda i: (jnp.maximum(rev(i) * hb - 1, 0), 0)),
                  _const((4, RG_WIDTH)), vec, sq, vec, sq, vec, vec],
        out_specs=[rrow(RG_WIDTH), rrow(RG_WIDTH), _const((SUBLANES, RG_WIDTH)), vec, sq, vec, sq, vec, vec],
        out_shape=[jax.ShapeDtypeStruct((t, RG_WIDTH), F32), jax.ShapeDtypeStruct((t, RG_WIDTH), F32),
                   jax.ShapeDtypeStruct((SUBLANES, RG_WIDTH), F32), jax.ShapeDtypeStruct((1, RG_WIDTH), F32),
                   jax.ShapeDtypeStruct((RG_WIDTH, RG_WIDTH), F32), jax.ShapeDtypeStruct((1, RG_WIDTH), F32),
                   jax.ShapeDtypeStruct((RG_WIDTH, RG_WIDTH), F32), jax.ShapeDtypeStruct((1, RG_WIDTH), F32),
                   jax.ShapeDtypeStruct((1, RG_WIDTH), F32)],
        scratch_shapes=[pltpu.VMEM((tm + SUBLANES, RG_WIDTH), F32), pltpu.VMEM((tm, RG_WIDTH), F32),
                        pltpu.VMEM((tm, RG_WIDTH), F32), pltpu.VMEM((tm + SUBLANES, RG_WIDTH), F32),
                        pltpu.VMEM((tm, RG_WIDTH), F32), pltpu.VMEM((tm + SUBLANES, RG_WIDTH), F32),
                        pltpu.VMEM((SUBLANES, RG_WIDTH), F32), pltpu.VMEM((SUBLANES, RG_WIDTH), F32)],
        compiler_params=_cparams(("arbitrary",)),
    )(dycat, proj, proj, proj, hs, hs, cw, cb, wa, ba, wx, bx, nsp)


def _block_diag(blocks):
    g, a, b = blocks.shape
    eye = jnp.eye(g, dtype=blocks.dtype)
    return (eye[:, None, :, None] * blocks[:, :, None, :]).reshape(g * a, g * b)


def _block_diag_extract(m, g):
    a, b = m.shape[0] // g, m.shape[1] // g
    m4 = m.reshape(g, a, g, b)
    idx = jnp.arange(g)
    return m4[idx, :, idx, :]


def _s5_prepare(lam_re, lam_im, log_step, b_re, b_im, c_re, c_im):
    step = jnp.exp(log_step)[:, None]
    mag = jnp.exp(lam_re * step)
    lbr = mag * jnp.cos(lam_im * step)
    lbi = mag * jnp.sin(lam_im * step)
    nr, ni = lbr - 1.0, lbi
    den = lam_re * lam_re + lam_im * lam_im
    cr = (nr * lam_re + ni * lam_im) / den
    ci = (ni * lam_re - nr * lam_im) / den
    bbr = cr[..., None] * b_re - ci[..., None] * b_im
    bbi = cr[..., None] * b_im + ci[..., None] * b_re
    bre = _block_diag(jnp.swapaxes(bbr, 1, 2))
    bim = _block_diag(jnp.swapaxes(bbi, 1, 2))
    cre = _block_diag(jnp.swapaxes(c_re, 1, 2))
    cim = _block_diag(jnp.swapaxes(c_im, 1, 2))
    return lbr.reshape(-1), lbi.reshape(-1), bre, bim, cre, cim


def _s5_scan_coef(lbr, lbi, reverse):
    if reverse:
        lbi = -lbi
    pr, pi = [lbr], [lbi]
    for _ in range(7):
        pr, pi = pr + [pr[-1] * lbr - pi[-1] * lbi], pi + [pr[-1] * lbi + pi[-1] * lbr]
    row = jnp.arange(SUBLANES)[:, None]
    tabs = []
    for sh in (1, 2, 4):
        keep = (row < SUBLANES - sh) if reverse else (row >= sh)
        tabs.append(jnp.stack([jnp.where(keep, pr[sh - 1][None, :], 0.0), jnp.where(keep, pi[sh - 1][None, :], 0.0)]))
    powr = jnp.stack(pr)
    powi = jnp.stack(pi)
    if reverse:
        powr, powi = powr[::-1], powi[::-1]
    tabs.append(jnp.stack([powr, powi]))
    tabs.append(jnp.zeros_like(tabs[-1]))
    return jnp.stack(tabs).astype(F32)


def _xy_peers():
    x, y, c = lax.axis_index("x"), lax.axis_index("y"), lax.axis_index("c")
    return x, y, c, [(1 - x, y), (x, 1 - y), (1 - x, 1 - y)]


def _hbm():
    return pl.BlockSpec(memory_space=pl.ANY)


def _xy_allgather(buf, *, name):
    n, w = buf.shape

    def body(x_ref, out_ref, send_sems, recv_sems, local_sem):
        x, y, c, peers = _xy_peers()
        me = 2 * x + y
        own = pltpu.make_async_copy(x_ref, out_ref.at[me], local_sem)
        own.start()
        sends = []
        for k, (px, py) in enumerate(peers):
            cp = pltpu.make_async_remote_copy(src_ref=x_ref, dst_ref=out_ref.at[me], send_sem=send_sems.at[k],
                                              recv_sem=recv_sems.at[k], device_id=(px, py, c), device_id_type=MESH)
            cp.start()
            sends.append(cp)
        for k, (px, py) in enumerate(peers):
            pltpu.make_async_remote_copy(src_ref=x_ref, dst_ref=out_ref.at[2 * px + py], send_sem=send_sems.at[k],
                                         recv_sem=recv_sems.at[k], device_id=(px, py, c),
                                         device_id_type=MESH).wait_recv()
        for cp in sends:
            cp.wait_send()
        own.wait()

    return pl.pallas_call(
        body, name=name, in_specs=[_hbm()], out_specs=_hbm(),
        out_shape=jax.ShapeDtypeStruct((4, n, w), buf.dtype),
        scratch_shapes=[pltpu.SemaphoreType.DMA((3,)), pltpu.SemaphoreType.DMA((3,)), pltpu.SemaphoreType.DMA],
    )(buf)


def _remote(src, dst, send_sem, recv_sem, dev):
    return pltpu.make_async_remote_copy(src_ref=src, dst_ref=dst, send_sem=send_sem, recv_sem=recv_sem,
                                        device_id=dev, device_id_type=MESH)


LAYER_GATHERED = (
    ("ssd_conv_w", (4, 256), 1), ("rg_conv_w", (4, LANES), 1),
    ("w_in", (1024, W_IN_PAD), 1), ("s5_glu_w", (64, 256), 0), ("w_out", (256, 1024), 0), ("xa_wq", (256, 1024), 0),
    ("xa_wk", (256, 1024), 0), ("xa_wv", (256, 1024), 0), ("xa_wo", (256, 1024), 0), ("mlp_w1", (1024, 1024), 1),
    ("mlp_w2", (1024, 1024), 0),
)
N_GATHERED = len(LAYER_GATHERED)
WAIT_GROUPS = ((0, 1, 2, 3), (4,), (5, 6, 7, 8), (9, 10))
RG_CONV_SHARD = RG_WIDTH // 4
N_GATHER_COPIES = 3 * N_GATHERED * DEPTH


def _gather_part(ref, t, pos):
    _, shp, ax = LAYER_GATHERED[t % N_GATHERED]
    idx = tuple(pl.ds(pos * shp[ax], shp[ax]) if d == ax else slice(None) for d in range(len(shp)))
    return ref.at[idx]


def _gather_start(shards, lands):
    n = len(shards)
    lands = [pltpu.with_memory_space_constraint(a, pltpu.HBM) for a in lands]

    def body(*refs):
        srcs, lnds = refs[:n], refs[n:2 * n]
        send_sems, recv_sems = refs[2 * n], refs[2 * n + 1]
        token = refs[-1]
        x, y, c, peers = _xy_peers()
        me = 2 * x + y
        for t in range(n):
            for k, (px, py) in enumerate(peers):
                _remote(srcs[t], _gather_part(lnds[t], t, me), send_sems.at[k * n + t], recv_sems.at[k * n + t],
                        (px, py, c)).start()
        token[...] = jnp.zeros_like(token)

    hbm = pl.BlockSpec(memory_space=pltpu.HBM)
    sem = pl.BlockSpec(memory_space=pltpu.SEMAPHORE)
    outs = pl.pallas_call(
        body, name="weights_gather_start", in_specs=[hbm] * (2 * n),
        out_shape=(pltpu.SemaphoreType.DMA((3 * n,)), pltpu.SemaphoreType.DMA((3 * n,)),
                   *[pltpu.HBM(s.shape, s.dtype) for s in shards], *[pltpu.HBM(a.shape, a.dtype) for a in lands],
                   jax.ShapeDtypeStruct((SUBLANES, LANES), F32)),
        out_specs=(sem, sem, *[hbm] * (2 * n), pl.BlockSpec(memory_space=pltpu.VMEM)),
        input_output_aliases={i: 2 + i for i in range(2 * n)},
        compiler_params=pltpu.CompilerParams(has_side_effects=pltpu.SideEffectType.DATAFLOW_SIDE_EFFECTING),
    )(*[pltpu.with_memory_space_constraint(s, pltpu.HBM) for s in shards], *lands)
    return outs[0], outs[1], outs[2:2 + n], outs[2 + n:2 + 2 * n], outs[-1]


def _gather_wait(handle, ts, after, *, name):
    send_sems, recv_sems, src_thru, land_thru, _ = handle
    n = len(src_thru)
    m = len(ts)

    def body(*refs):
        srcs, lnds = refs[:m], refs[m:2 * m]
        ssem, rsem = refs[2 * m], refs[2 * m + 1]
        x, y, c, peers = _xy_peers()
        for i, t in enumerate(ts):
            for k, (px, py) in enumerate(peers):
                cp = _remote(srcs[i], _gather_part(lnds[i], t, 2 * px + py), ssem.at[k * n + t], rsem.at[k * n + t],
                             (px, py, c))
                cp.wait_send()
                cp.wait_recv()

    hbm = pl.BlockSpec(memory_space=pltpu.HBM)
    sem = pl.BlockSpec(memory_space=pltpu.SEMAPHORE)
    args = [src_thru[t] for t in ts] + [land_thru[t] for t in ts]
    outs = pl.pallas_call(
        body, name=name, in_specs=[hbm] * (2 * m) + [sem, sem, pl.BlockSpec(memory_space=pl.ANY)],
        out_shape=[pltpu.HBM(a.shape, a.dtype) for a in args], out_specs=[hbm] * (2 * m),
        input_output_aliases={i: i for i in range(2 * m)},
        compiler_params=pltpu.CompilerParams(has_side_effects=pltpu.SideEffectType.DATAFLOW_SIDE_EFFECTING),
    )(*args, send_sems, recv_sems, after)
    return outs[:m], outs[m:]


C_CHUNKS = 4
XY_CHUNKS = 4
EW_ROWS = 512


def _c_exchange(g):
    _, n, w = g.shape
    n2 = n // 2
    rq = n2 // C_CHUNKS

    def body(g_ref, got_ref, send_sems, recv_sems):
        x, y, c = lax.axis_index("x"), lax.axis_index("y"), lax.axis_index("c")
        cps = []
        for s in range(4):
            for q in range(C_CHUNKS):
                k = s * C_CHUNKS + q
                cp = _remote(g_ref.at[s, pl.ds((1 - c) * n2 + q * rq, rq), :], got_ref.at[s, pl.ds(q * rq, rq), :],
                             send_sems.at[k], recv_sems.at[k], (x, y, 1 - c))
                cp.start()
                cps.append(cp)
        for cp in cps:
            cp.wait_recv()
        for cp in cps:
            cp.wait_send()

    return pl.pallas_call(
        body, name="grad_c_exchange", in_specs=[_hbm()], out_specs=_hbm(),
        out_shape=jax.ShapeDtypeStruct((4, n2, w), g.dtype),
        scratch_shapes=[pltpu.SemaphoreType.DMA((4 * C_CHUNKS,)), pltpu.SemaphoreType.DMA((4 * C_CHUNKS,))],
    )(g)


XFER_DTYPE = jnp.bfloat16


def _add_own_half(g, got, c_arr):
    _, n, w = g.shape
    n2 = n // 2
    nb = n2 // EW_ROWS

    def body(c_ref, a_ref, b_ref, o_ref, t_ref):
        sm = a_ref[...] + b_ref[...]
        o_ref[...] = sm.astype(o_ref.dtype)

        @pl.when(pl.program_id(1) == nb - 1)
        def _():
            t_ref[...] = sm[:, EW_ROWS - MISC_ROWS:, :]

    grid_spec = pltpu.PrefetchScalarGridSpec(
        num_scalar_prefetch=1, grid=(4, nb),
        in_specs=[pl.BlockSpec((1, EW_ROWS, w), lambda s, i, c: (s, c[0] * nb + i, 0)),
                  pl.BlockSpec((1, EW_ROWS, w), lambda s, i, c: (s, i, 0))],
        out_specs=[pl.BlockSpec((1, EW_ROWS, w), lambda s, i, c: (s, i, 0)),
                   pl.BlockSpec((1, MISC_ROWS, w), lambda s, i, c: (s, 0, 0))])
    return pl.pallas_call(
        body, name="grad_add_halves", grid_spec=grid_spec,
        out_shape=[jax.ShapeDtypeStruct((4, n2, w), XFER_DTYPE), jax.ShapeDtypeStruct((4, MISC_ROWS, w), g.dtype)],
        compiler_params=_cparams(("arbitrary", "arbitrary")),
    )(c_arr, g, got)


def _xy_exchange(arrs):
    na = len(arrs)
    pieces = []
    for a, arr in enumerate(arrs):
        nch = XY_CHUNKS if a == 0 else 1
        rq = arr.shape[1] // nch
        pieces += [(a, pl.ds(q * rq, rq)) for q in range(nch)]
    npc = len(pieces)

    def body(*refs):
        ins, outs = refs[:na], refs[na:2 * na]
        send_sems, recv_sems, local_sems = refs[2 * na:]
        x, y, c, peers = _xy_peers()
        me = 2 * x + y
        own = []
        for j, (a, rows) in enumerate(pieces):
            cp = pltpu.make_async_copy(ins[a].at[me, rows, :], outs[a].at[me, rows, :], local_sems.at[j])
            cp.start()
            own.append(cp)
        sends = []
        for k, (px, py) in enumerate(peers):
            for j, (a, rows) in enumerate(pieces):
                cp = _remote(ins[a].at[2 * px + py, rows, :], outs[a].at[me, rows, :], send_sems.at[k * npc + j],
                             recv_sems.at[k * npc + j], (px, py, c))
                cp.start()
                sends.append(cp)
        for k, (px, py) in enumerate(peers):
            for j, (a, rows) in enumerate(pieces):
                _remote(ins[a].at[me, rows, :], outs[a].at[2 * px + py, rows, :], send_sems.at[k * npc + j],
                        recv_sems.at[k * npc + j], (px, py, c)).wait_recv()
        for cp in sends:
            cp.wait_send()
        for cp in own:
            cp.wait()

    return pl.pallas_call(
        body, name="grad_xy_exchange", in_specs=[_hbm()] * na, out_specs=[_hbm()] * na,
        out_shape=[jax.ShapeDtypeStruct(a.shape, a.dtype) for a in arrs],
        scratch_shapes=[pltpu.SemaphoreType.DMA((3 * npc,)), pltpu.SemaphoreType.DMA((3 * npc,)),
                        pltpu.SemaphoreType.DMA((npc,))],
    )(*arrs)


def _sum4_into_half(r, rt, c_arr):
    _, n2, w = r.shape
    nb = n2 // EW_ROWS

    def body(c_ref, r_ref, t_ref, o_ref):
        o_ref[...] = ((r_ref[0].astype(F32) + r_ref[1].astype(F32)) + r_ref[2].astype(F32)) + r_ref[3].astype(F32)

        @pl.when(pl.program_id(0) == nb - 1)
        def _():
            o_ref[EW_ROWS - MISC_ROWS:, :] = ((t_ref[0] + t_ref[1]) + t_ref[2]) + t_ref[3]

    grid_spec = pltpu.PrefetchScalarGridSpec(
        num_scalar_prefetch=1, grid=(nb,),
        in_specs=[pl.BlockSpec((4, EW_ROWS, w), lambda i, c: (0, i, 0)),
                  pl.BlockSpec((4, MISC_ROWS, w), lambda i, c: (0, 0, 0))],
        out_specs=pl.BlockSpec((EW_ROWS, w), lambda i, c: (c[0] * nb + i, 0)))
    return pl.pallas_call(
        body, name="grad_sum4", grid_spec=grid_spec, out_shape=jax.ShapeDtypeStruct((2 * n2, w), F32),
        compiler_params=_cparams(("arbitrary",)),
    )(c_arr, r, rt)


C_GATHER_CHUNKS = 8


def _c_allgather_halves(f):
    n, w = f.shape
    n2 = n // 2
    rq = n2 // C_GATHER_CHUNKS

    def body(f_ref, out_ref, send_sems, recv_sems):
        x, y, c = lax.axis_index("x"), lax.axis_index("y"), lax.axis_index("c")
        sends = []
        for q in range(C_GATHER_CHUNKS):
            rows = pl.ds(c * n2 + q * rq, rq)
            cp = _remote(f_ref.at[rows, :], out_ref.at[rows, :], send_sems.at[q], recv_sems.at[q], (x, y, 1 - c))
            cp.start()
            sends.append(cp)
        for q in range(C_GATHER_CHUNKS):
            rows = pl.ds((1 - c) * n2 + q * rq, rq)
            _remote(f_ref.at[rows, :], out_ref.at[rows, :], send_sems.at[q], recv_sems.at[q],
                    (x, y, 1 - c)).wait_recv()
        for cp in sends:
            cp.wait_send()

    return pl.pallas_call(
        body, name="grad_c_allgather", in_specs=[_hbm()], out_specs=_hbm(), input_output_aliases={0: 0},
        out_shape=jax.ShapeDtypeStruct((n, w), f.dtype),
        scratch_shapes=[pltpu.SemaphoreType.DMA((C_GATHER_CHUNKS,)), pltpu.SemaphoreType.DMA((C_GATHER_CHUNKS,))],
    )(f)


def _adamw(w, m, v, g, g_row0=None):
    shape = w.shape
    cols = shape[-1]
    rows = int(math.prod(shape)) // cols
    tr = 256 if rows % 256 == 0 else rows
    from_flat = g_row0 is not None
    c1 = 1.0 / (1.0 - ADAM_B1 ** ADAM_STEP)
    c2 = 1.0 / (1.0 - ADAM_B2 ** ADAM_STEP)

    def body(w_ref, m_ref, v_ref, g_ref, *outs):
        gg = g_ref[...]
        nm = ADAM_B1 * m_ref[...] + (1.0 - ADAM_B1) * gg
        nv = ADAM_B2 * v_ref[...] + (1.0 - ADAM_B2) * (gg * gg)
        if from_flat:
            outs[0][...] = gg
        d_ref, nm_ref, nv_ref = outs[-3:]
        nm_ref[...] = nm
        nv_ref[...] = nv
        d_ref[...] = -ADAM_LR * ((nm * c1) / (jnp.sqrt(nv * c2) + ADAM_EPS) + ADAM_WD * w_ref[...])

    spec = pl.BlockSpec((tr, cols), lambda i: (i, 0))
    if from_flat:
        assert cols == FLAT and g_row0 % tr == 0
        g_spec = pl.BlockSpec((tr, cols), lambda i: (g_row0 // tr + i, 0))
        g_arg = g
    else:
        g_spec = spec
        g_arg = g.reshape(rows, cols)
    n_out = 4 if from_flat else 3
    sds = jax.ShapeDtypeStruct((rows, cols), F32)
    outs = pl.pallas_call(
        body, name="adamw", grid=(rows // tr,), in_specs=[spec, spec, spec, g_spec], out_specs=[spec] * n_out,
        out_shape=[sds] * n_out, compiler_params=_cparams(("arbitrary",)),
    )(w.reshape(rows, cols), m.reshape(rows, cols), v.reshape(rows, cols), g_arg)
    outs = [o.reshape(shape) for o in outs]
    return outs if from_flat else [g] + outs


SMALL_SHARDED = (("s5_glu_w", (2, 64, 256), 1), ("ssd_conv_w", (2, 4, 256), 2), ("rg_conv_w", (2, 4, 64), 2))
REPLICATED = (
    ("ssd_conv_b", (2, 1024)), ("ssd_dt_bias", (2, 8)), ("ssd_a_log", (2, 8)), ("ssd_d", (2, 8)),
    ("ssd_norm_w", (2, 512)), ("s5_lam_re", (2, 16, 64)), ("s5_lam_im", (2, 16, 64)), ("s5_log_step", (2, 16)),
    ("s5_b_re", (2, 16, 64, 16)), ("s5_b_im", (2, 16, 64, 16)), ("s5_c_re", (2, 16, 16, 64)),
    ("s5_c_im", (2, 16, 16, 64)), ("s5_d", (2, 256)), ("s5_glu_b", (2, 256)), ("rg_conv_b", (2, 256)),
    ("rg_wa", (2, 4, 64, 64)), ("rg_ba", (2, 4, 64)), ("rg_wx", (2, 4, 64, 64)), ("rg_bx", (2, 4, 64)),
    ("rg_lambda", (2, 256)), ("ln1_g", (2, 1024)), ("ln1_b", (2, 1024)), ("ln2_g", (2, 1024)), ("ln2_b", (2, 1024)),
    ("ln3_g", (2, 1024)), ("ln3_b", (2, 1024)),
)
WEIGHT_ORDER = (
    "w_in", "w_out", "ssd_conv_w", "ssd_conv_b", "ssd_dt_bias", "ssd_a_log", "ssd_d", "ssd_norm_w", "s5_lam_re",
    "s5_lam_im", "s5_log_step", "s5_b_re", "s5_b_im", "s5_c_re", "s5_c_im", "s5_d", "s5_glu_w", "s5_glu_b",
    "rg_conv_w", "rg_conv_b", "rg_wa", "rg_ba", "rg_wx", "rg_bx", "rg_lambda", "ln1_g", "ln1_b", "xa_wq", "xa_wk",
    "xa_wv", "xa_wo", "ln2_g", "ln2_b", "mlp_w1", "mlp_w2", "ln3_g", "ln3_b",
)


def _size(shape):
    return int(math.prod(shape))


def _round_up(a, b):
    return (a + b - 1) // b * b


SMALL_ELEMS = sum(_size(s) for _, s, _ in SMALL_SHARDED)
REP_ELEMS = sum(_size(s) for _, s in REPLICATED)
REP_QROWS = _round_up(-(-REP_ELEMS // (4 * FLAT)), 8)
assert SMALL_ELEMS <= MISC_REP_ROW * FLAT and MISC_REP_ROW + REP_QROWS <= MISC_ROWS


def _pack_shards(tensors, names_shapes):
    return jnp.concatenate([tensors[n].reshape(-1) for n, *_ in names_shapes])


def _unpack(flat, names_shapes):
    out, off = {}, 0
    for n, s, *_ in names_shapes:
        out[n] = flat[off:off + _size(s)].reshape(s)
        off += _size(s)
    return out


def _split_shards(full, names_shapes):
    rows = []
    for k in range(4):
        parts = []
        for n, s, ax in names_shapes:
            w = s[ax]
            parts.append(lax.slice_in_dim(full[n], k * w, (k + 1) * w, axis=ax).reshape(-1))
        rows.append(jnp.concatenate(parts))
    return jnp.stack(rows)


def _pack_cols(w):
    pad = jnp.zeros((w.shape[0], LANES - SSD_HEADS), w.dtype)
    return jnp.concatenate([w[:, O_XBC:O_XBC + 1024], w[:, O_Z:O_Z + 512], w[:, O_U:O_U + 256],
                            w[:, O_XRG:O_XRG + 256], w[:, O_GRG:O_GRG + 256], w[:, O_DT:O_DT + 8], pad], axis=1)


def _unpack_cols(w):
    return jnp.concatenate([w[:, P_Z:P_Z + 512], w[:, P_XBC:P_XBC + 1024], w[:, P_DT:P_DT + 8],
                            w[:, P_U:P_U + 256], w[:, P_XRG:P_XRG + 256], w[:, P_GRG:P_GRG + 256]], axis=1)


def _lanes(v, width):
    return jnp.pad(v, (0, width - v.shape[0])).reshape(1, width)


def _layer_params(rep, l):
    p = {}
    p["ssd_cb"] = rep["ssd_conv_b"][l].reshape(1, -1)
    p["ssd_dtb"] = _lanes(rep["ssd_dt_bias"][l], LANES)
    p["ssd_a"] = _lanes(-jnp.exp(rep["ssd_a_log"][l]), LANES)
    p["ssd_d"] = jnp.repeat(rep["ssd_d"][l], 64).reshape(1, -1)
    p["ssd_nw"] = rep["ssd_norm_w"][l].reshape(1, -1)
    s5_args = tuple(rep[n][l] for n in ("s5_lam_re", "s5_lam_im", "s5_log_step", "s5_b_re", "s5_b_im", "s5_c_re",
                                        "s5_c_im"))
    (lbr, lbi, bre, bim, cre, cim), p["s5_vjp"] = jax.vjp(_s5_prepare, *s5_args)
    p.update(s5_bre=bre, s5_bim=bim, s5_cre=cre, s5_cim=cim)
    p["s5_coef"] = _s5_scan_coef(lbr, lbi, False)
    p["s5_rcoef"] = _s5_scan_coef(lbr, lbi, True)
    p["s5_d"] = rep["s5_d"][l].reshape(1, -1)
    p["s5_gb"] = rep["s5_glu_b"][l].reshape(1, -1)
    p["rg_cb"] = rep["rg_conv_b"][l].reshape(1, -1)
    p["rg_wa"] = _block_diag(rep["rg_wa"][l])
    p["rg_wx"] = _block_diag(rep["rg_wx"][l])
    p["rg_ba"] = rep["rg_ba"][l].reshape(1, -1)
    p["rg_bx"] = rep["rg_bx"][l].reshape(1, -1)
    p["rg_nsp"] = (-RG_C * jax.nn.softplus(-rep["rg_lambda"][l])).reshape(1, -1)
    p["rg_dnsp"] = RG_C * jax.nn.sigmoid(-rep["rg_lambda"][l])
    for n in ("ln1_g", "ln1_b", "ln2_g", "ln2_b", "ln3_g", "ln3_b"):
        p[n] = rep[n][l].reshape(1, -1)
    return p


def _layer_fwd(h, mem, p, fetch):
    s = {"h0": h}
    p.update(fetch(0, h))
    proj = _mm(h, p["w_in"], name="in_proj")
    s["proj"] = proj
    y_ssd, s["ssd_yy"], s["ssd_states"] = _ssd_fwd(proj, p["ssd_cw"], p["ssd_cb"], p["ssd_dtb"], p["ssd_a"],
                                                     p["ssd_d"], p["ssd_nw"])
    y_s5, s["s5_y2"], s["s5_hre"], s["s5_him"] = _s5_fwd(proj, p["s5_bre"], p["s5_bim"], p["s5_cre"], p["s5_cim"],
                                                         p["s5_d"], p["s5_glu_w"], p["s5_gb"], p["s5_coef"])
    y_rg, s["rg_h"] = _rg_fwd(proj, p["rg_cw"], p["rg_cb"], p["rg_wa"], p["rg_ba"], p["rg_wx"], p["rg_bx"],
                              p["rg_nsp"])
    ycat = jnp.concatenate([y_ssd, y_s5, y_rg], axis=1)
    s["ycat"] = ycat
    p.update(fetch(1, ycat))
    h1, s["xh1"], s["rs1"] = _outproj_ln_fwd(ycat, h, p["w_out"], p["ln1_g"], p["ln1_b"])
    s["h1"] = h1
    p.update(fetch(2, h1))
    kb = _mm(mem, p["xa_wk"], name="mem_proj")
    vb = _mm(mem, p["xa_wv"], name="mem_proj")
    s["kb"], s["vb"] = kb, vb
    h2, s["xh2"], s["rs2"], s["attn_o"] = _attn_ln_fwd(h1, p["xa_wq"], p["xa_wo"], kb, vb, p["ln2_g"], p["ln2_b"])
    s["h2"] = h2
    p.update(fetch(3, h2))
    h3, s["xh3"], s["rs3"], s["mlp_hdn"] = _mlp_ln_fwd(h2, p["mlp_w1"], p["mlp_w2"], p["ln3_g"], p["ln3_b"])
    return h3, s


def _layer_bwd(dh3, mem, p, s, l, gbuf):
    g = {}
    dr3, du, dh2, g["ln3_g"], g["ln3_b"] = _mlp_ln_bwd(dh3, s["xh3"], s["rs3"], p["ln3_g"], s["mlp_hdn"],
                                                        p["mlp_w1"], p["mlp_w2"])
    gbuf = _wgrad_flat(s["h2"], du, gbuf, mode="colblk", row_off=ROW_MLP_W1 + 1024 * l, name="wgrad_mlp_w1")
    gbuf = _wgrad_flat(s["mlp_hdn"], dr3, gbuf, mode="rowblk", row_off=ROW_MLP_W2 + 1024 * l, name="wgrad_mlp_w2")
    dr2, dq, dh1, dkb, dvb, g["ln2_g"], g["ln2_b"] = _attn_ln_bwd(dh2, s["xh2"], s["rs2"], p["ln2_g"], s["h1"],
                                                                   p["xa_wq"], p["xa_wo"], s["kb"], s["vb"])
    for n, a_op, g_op in (("xa_wo", s["attn_o"], dr2), ("xa_wq", s["h1"], dq), ("xa_wk", mem, dkb),
                          ("xa_wv", mem, dvb)):
        gbuf = _wgrad_flat(a_op, g_op, gbuf, mode="rows4", row_off=ROW_XA[n] + 256 * l, name="wgrad_" + n)
    dr1, dres, dycat, g["ln1_g"], g["ln1_b"] = _outproj_ln_bwd(dh1, s["xh1"], s["rs1"], p["ln1_g"], p["w_out"])
    gbuf = _wgrad_flat(s["ycat"], dr1, gbuf, mode="rows4", row_off=ROW_W_OUT + 256 * l, name="wgrad_w_out")
    proj = s["proj"]
    (dxbc, dz, ddt, dcw, dcb, ddtb, da_neg, dd_l, dnw) = _ssd_bwd(
        dycat, proj, s["ssd_yy"], s["ssd_states"], p["ssd_cw"], p["ssd_cb"], p["ssd_dtb"], p["ssd_a"], p["ssd_d"],
        p["ssd_nw"])
    g["ssd_conv_w"] = dcw[0:4]
    g["ssd_conv_b"] = dcb[0]
    g["ssd_dt_bias"] = ddtb[0, :SSD_HEADS]
    g["ssd_a_log"] = da_neg[0, :SSD_HEADS] * p["ssd_a"][0, :SSD_HEADS]
    g["ssd_d"] = dd_l.reshape(SSD_HEADS, 64).sum(axis=1)
    g["ssd_norm_w"] = dnw[0]
    (du_s5, dbre, dbim, dcre, dcim, dlam, dd5, dgw, dgb) = _s5_bwd(
        dycat, proj, s["s5_y2"], s["s5_hre"], s["s5_him"], p["s5_bre"], p["s5_bim"], p["s5_cre"], p["s5_cim"],
        p["s5_d"], p["s5_glu_w"], p["s5_gb"], p["s5_rcoef"])
    dl = dlam.sum(axis=1)
    s5g = p["s5_vjp"]((dl[0], dl[1], dbre, dbim, dcre, dcim))
    for n, v in zip(("s5_lam_re", "s5_lam_im", "s5_log_step", "s5_b_re", "s5_b_im", "s5_c_re", "s5_c_im"), s5g):
        g[n] = v
    g["s5_d"] = dd5[0]
    g["s5_glu_w"] = dgw
    g["s5_glu_b"] = dgb[0]
    (dxrg, dgrg, drcw, drcb, dwa, dba, dwx, dbx, dnsp) = _rg_bwd(
        dycat, proj, s["rg_h"], p["rg_cw"], p["rg_cb"], p["rg_wa"], p["rg_ba"], p["rg_wx"], p["rg_bx"], p["rg_nsp"])
    g["rg_conv_w"] = drcw[0:4]
    g["rg_conv_b"] = drcb[0]
    g["rg_wa"] = _block_diag_extract(dwa, RG_BLOCKS)
    g["rg_wx"] = _block_diag_extract(dwx, RG_BLOCKS)
    g["rg_ba"] = dba.reshape(RG_BLOCKS, RG_BLOCK_DIM)
    g["rg_bx"] = dbx.reshape(RG_BLOCKS, RG_BLOCK_DIM)
    g["rg_lambda"] = dnsp[0] * p["rg_dnsp"]
    dproj = jnp.concatenate([dxbc, dz, du_s5, dxrg, dgrg, ddt], axis=1)
    g["w_in"] = _unpack_cols(_mm_tn(s["h0"], dproj, name="wgrad_in"))
    dh0 = _mm(dproj, p["w_in"], nt=True, add=dres, name="in_proj_bwd")
    for n in ("ln1_g", "ln1_b", "ln2_g", "ln2_b", "ln3_g", "ln3_b"):
        g[n] = g[n][0]
    return dh0, g, gbuf


def _local_step(h, memf, target, rep, fetch):
    params, saved = [], []
    for l in range(DEPTH):
        p = _layer_params(rep, l)
        params.append(p)
        h, s = _layer_fwd(h, memf, p, functools.partial(fetch, l))
        saved.append(s)
    loss11, dh = _loss_fwd_bwd(h, target)
    grads = [None] * DEPTH
    gbuf = None
    for l in reversed(range(DEPTH)):
        dh, grads[l], gbuf = _layer_bwd(dh, memf, params[l], saved[l], l, gbuf)
    return loss11, dh, {n: jnp.stack([grads[l][n] for l in range(DEPTH)]) for n in grads[0]}, gbuf


def kernel(x, mem, w_in, w_out, ssd_conv_w, ssd_conv_b, ssd_dt_bias, ssd_a_log, ssd_d, ssd_norm_w, s5_lam_re, s5_lam_im, s5_log_step, s5_b_re, s5_b_im, s5_c_re, s5_c_im, s5_d, s5_glu_w, s5_glu_b, rg_conv_w, rg_conv_b, rg_wa, rg_ba, rg_wx, rg_bx, rg_lambda, ln1_g, ln1_b, xa_wq, xa_wk, xa_wv, xa_wo, ln2_g, ln2_b, mlp_w1, mlp_w2, ln3_g, ln3_b, loss_target, m_w_in, m_w_out, m_ssd_conv_w, m_ssd_conv_b, m_ssd_dt_bias, m_ssd_a_log, m_ssd_d, m_ssd_norm_w, m_s5_lam_re, m_s5_lam_im, m_s5_log_step, m_s5_b_re, m_s5_b_im, m_s5_c_re, m_s5_c_im, m_s5_d, m_s5_glu_w, m_s5_glu_b, m_rg_conv_w, m_rg_conv_b, m_rg_wa, m_rg_ba, m_rg_wx, m_rg_bx, m_rg_lambda, m_ln1_g, m_ln1_b, m_xa_wq, m_xa_wk, m_xa_wv, m_xa_wo, m_ln2_g, m_ln2_b, m_mlp_w1, m_mlp_w2, m_ln3_g, m_ln3_b, v_w_in, v_w_out, v_ssd_conv_w, v_ssd_conv_b, v_ssd_dt_bias, v_ssd_a_log, v_ssd_d, v_ssd_norm_w, v_s5_lam_re, v_s5_lam_im, v_s5_log_step, v_s5_b_re, v_s5_b_im, v_s5_c_re, v_s5_c_im, v_s5_d, v_s5_glu_w, v_s5_glu_b, v_rg_conv_w, v_rg_conv_b, v_rg_wa, v_rg_ba, v_rg_wx, v_rg_bx, v_rg_lambda, v_ln1_g, v_ln1_b, v_xa_wq, v_xa_wk, v_xa_wv, v_xa_wo, v_ln2_g, v_ln2_b, v_mlp_w1, v_mlp_w2, v_ln3_g, v_ln3_b):
    args = dict(locals())
    weights = {n: args[n] for n in WEIGHT_ORDER}
    mom_m = {n: args["m_" + n] for n in WEIGHT_ORDER}
    mom_v = {n: args["v_" + n] for n in WEIGHT_ORDER}

    me = 2 * lax.axis_index("x") + lax.axis_index("y")
    shards, lands = [], []
    for l in range(DEPTH):
        for n, shp, ax in LAYER_GATHERED:
            w = weights[n][l]
            if w.shape[1] != shp[1]:
                w = jnp.pad(w, ((0, 0), (0, shp[1] - w.shape[1])))
            if n not in ("ssd_conv_w", "rg_conv_w"):
                w = w.astype(MXU_DTYPE)
            shards.append(w)
            full_shape = shp[:ax] + (4 * shp[ax],) + shp[ax + 1:]
            lands.append(lax.dynamic_update_slice_in_dim(lax.empty(full_shape, w.dtype), w, me * shp[ax], axis=ax))
    handle = _gather_start(shards, lands)

    def unpad(arr, padded, width):
        return jnp.concatenate([arr[:, padded * k:padded * k + width] for k in range(4)], axis=1)

    def fetch(l, grp, after):
        ts = [l * N_GATHERED + j for j in WAIT_GROUPS[grp]]
        _, landed = _gather_wait(handle, ts, after, name="weights_gather_wait_%d_%d" % (l, grp))
        out = {}
        for t, arr in zip(ts, landed):
            n = LAYER_GATHERED[t % N_GATHERED][0]
            if n == "w_in":
                arr = _pack_cols(unpad(arr, W_IN_PAD, W_IN_SHARD))
            elif n == "rg_conv_w":
                arr = unpad(arr, LANES, RG_CONV_SHARD)
            out[{"ssd_conv_w": "ssd_cw", "rg_conv_w": "rg_cw"}.get(n, n)] = arr
        return out

    rep = {n: weights[n] for n, _ in REPLICATED}

    loss11, dx, gsmall, gbuf = _local_step(x[0], mem[0], loss_target[0], rep, fetch)
    grad_x = dx[None]
    loss = lax.psum(loss11[0, 0], ("x", "y", "c"))

    gw = gsmall["w_in"].reshape(DEPTH, D_MODEL, 4, W_IN_SHARD)
    gw = jnp.pad(gw, ((0, 0), (0, 0), (0, 0), (0, W_IN_PAD - W_IN_SHARD)))
    w_in_blk = jnp.transpose(gw, (2, 0, 1, 3)).reshape(4, DEPTH * W_IN_PAD, FLAT)
    small_q = _split_shards(gsmall, SMALL_SHARDED)
    rep_q = jnp.pad(_pack_shards(gsmall, REPLICATED), (0, 4 * REP_QROWS * FLAT - REP_ELEMS)).reshape(4, -1)
    misc = jnp.concatenate(
        [jnp.pad(small_q, ((0, 0), (0, MISC_REP_ROW * FLAT - SMALL_ELEMS))), rep_q,
         jnp.zeros((4, (MISC_ROWS - MISC_REP_ROW - REP_QROWS) * FLAT), F32)], axis=1).reshape(4, MISC_ROWS, FLAT)
    gbuf = lax.dynamic_update_slice(gbuf, w_in_blk, (0, ROW_W_IN, 0))
    gbuf = lax.dynamic_update_slice(gbuf, misc, (0, ROW_MISC, 0))
    c_arr = lax.axis_index("c").astype(jnp.int32).reshape(1)
    chip_sum, chip_tail = _add_own_half(gbuf, _c_exchange(gbuf), c_arr)
    got_sum, got_tail = _xy_exchange([chip_sum, chip_tail])
    reduced = _c_allgather_halves(_sum4_into_half(got_sum, got_tail, c_arr))
    misc_red = reduced[ROW_MISC:]
    rep_all = _xy_allgather(misc_red[MISC_REP_ROW:MISC_REP_ROW + REP_QROWS], name="small_grads_allgather")
    g_red = {**_unpack(misc_red[:MISC_REP_ROW].reshape(-1), SMALL_SHARDED),
             **_unpack(rep_all.reshape(-1), REPLICATED)}
    g_red["w_in"] = reduced[ROW_W_IN:ROW_W_IN + DEPTH * W_IN_PAD].reshape(DEPTH, D_MODEL, W_IN_PAD)[:, :, :W_IN_SHARD]

    flat_rows = {"mlp_w1": ROW_MLP_W1, "mlp_w2": ROW_MLP_W2, "w_out": ROW_W_OUT, **ROW_XA}
    res = {}
    for n in WEIGHT_ORDER:
        if n in flat_rows:
            res[n] = _adamw(weights[n], mom_m[n], mom_v[n], reduced, g_row0=flat_rows[n])
        else:
            res[n] = _adamw(weights[n], mom_m[n], mom_v[n], g_red[n])
    return (loss, grad_x, *[res[n][0] for n in WEIGHT_ORDER], *[res[n][1] for n in WEIGHT_ORDER],
            *[res[n][2] for n in WEIGHT_ORDER], *[res[n][3] for n in WEIGHT_ORDER])
```

```python
import functools
import math

import jax
import jax.numpy as jnp
from jax import lax
from jax.experimental import pallas as pl
from jax.experimental.pallas import tpu as pltpu

F32 = jnp.float32
MXU_DTYPE = jnp.bfloat16

D_MODEL = 1024
DEPTH = 2
MEM_LEN = 256
SSD_WIDTH = 512
SSD_HEADS = 8
SSD_STATE = 128
SSD_CHUNK = 128
SSD_XBC = 1024
S5_WIDTH = 256
S5_GROUPS = 16
S5_GROUP_CH = 16
S5_STATE = 64
S5_NSTATE = S5_GROUPS * S5_STATE
RG_WIDTH = 256
RG_BLOCKS = 4
RG_BLOCK_DIM = 64
RG_C = 8.0
XA_HEADS = 4
XA_HEAD_DIM = 256
D_FF = 4096
D_IN = 2312
ALPHA = (2.0 * DEPTH) ** 0.25
LN_EPS = 1e-5
ADAM_LR = 0.001
ADAM_B1 = 0.9
ADAM_B2 = 0.999
ADAM_EPS = 1e-08
ADAM_WD = 0.01
ADAM_STEP = 10

P_XBC, P_Z, P_U, P_XRG, P_GRG, P_DT = 0, 1024, 1536, 1792, 2048, 2304
D_PACK = 2432
O_Z, O_XBC, O_DT, O_U, O_XRG, O_GRG = 0, 512, 1536, 1544, 1800, 2056

LANES = 128
SUBLANES = 8
VMEM_LIMIT = 52 * 1024 * 1024
TM = 512
SSD_TM = 256
SCAN_TM = 512
FLAT = 1024

MESH = pl.DeviceIdType.MESH


def _cparams(sem):
    return pltpu.CompilerParams(dimension_semantics=sem, vmem_limit_bytes=VMEM_LIMIT)


def _dot(a, b):
    return jnp.dot(a.astype(MXU_DTYPE), b.astype(MXU_DTYPE), preferred_element_type=F32)


def _dot_nt(a, b):
    return lax.dot_general(a.astype(MXU_DTYPE), b.astype(MXU_DTYPE), (((1,), (1,)), ((), ())),
                           preferred_element_type=F32)


def _dot_tn(a, b):
    return lax.dot_general(a.astype(MXU_DTYPE), b.astype(MXU_DTYPE), (((0,), (0,)), ((), ())),
                           preferred_element_type=F32)


def _dot_f32(a, b):
    return jnp.dot(a, b, precision=lax.Precision.HIGHEST, preferred_element_type=F32)


def _dot_f32_tn(a, b):
    return lax.dot_general(a, b, (((0,), (0,)), ((), ())), precision=lax.Precision.HIGHEST,
                           preferred_element_type=F32)


def _sigmoid(x):
    return 1.0 / (1.0 + jnp.exp(-x))


def _softplus(x):
    return jnp.maximum(x, 0.0) + jnp.log(1.0 + jnp.exp(-jnp.abs(x)))


_GELU_K = math.sqrt(2.0 / math.pi)


def _gelu(x):
    return 0.5 * x * (1.0 + jnp.tanh(_GELU_K * (x + 0.044715 * x * x * x)))


def _gelu_grad(x):
    t = jnp.tanh(_GELU_K * (x + 0.044715 * x * x * x))
    return 0.5 * (1.0 + t) + 0.5 * x * (1.0 - t * t) * _GELU_K * (1.0 + 3.0 * 0.044715 * x * x)


def _expm1(x):
    small = x * (1.0 + x * (0.5 + x * (1.0 / 6.0 + x * (1.0 / 24.0))))
    return jnp.where(jnp.abs(x) < 0.05, small, jnp.exp(x) - 1.0)


def _sum0(x):
    return jnp.sum(x, axis=0, keepdims=True)


def _ln_fwd(r, g, b):
    mu = jnp.mean(r, axis=-1, keepdims=True)
    xc = r - mu
    var = jnp.mean(xc * xc, axis=-1, keepdims=True)
    rstd = lax.rsqrt(var + LN_EPS)
    xhat = xc * rstd
    return xhat * g + b, xhat, rstd


def _ln_bwd(dout, xhat, rstd, g):
    dxh = dout * g
    m1 = jnp.mean(dxh, axis=-1, keepdims=True)
    m2 = jnp.mean(dxh * xhat, axis=-1, keepdims=True)
    return rstd * (dxh - m1 - xhat * m2)


def _rows(tm, n, col=0):
    return pl.BlockSpec((tm, n), lambda i: (i, col))


def _const(shape):
    nd = len(shape)
    return pl.BlockSpec(shape, lambda i: (0,) * nd)


def _mm(a, w, *, name):
    t, k = a.shape
    n = w.shape[1]
    tm = min(TM, t)

    def body(a_ref, w_ref, o_ref):
        o_ref[...] = _dot(a_ref[...], w_ref[...])

    return pl.pallas_call(
        body, name=name, grid=(t // tm,), in_specs=[_rows(tm, k), _const(w.shape)], out_specs=_rows(tm, n),
        out_shape=jax.ShapeDtypeStruct((t, n), F32), compiler_params=_cparams(("arbitrary",)),
    )(a, w)


DPROJ_PIECES = ((P_XBC, 1024), (P_Z, 512), (P_U, 256), (P_XRG, 256), (P_GRG, 256), (P_DT, LANES))


def _in_proj_bwd(pieces, w, dres):
    t = dres.shape[0]
    npc = len(pieces)

    def body(*refs):
        w_ref, r_ref, o_ref = refs[npc:]
        acc = r_ref[...]
        for p_ref, (off, k) in zip(refs[:npc], DPROJ_PIECES):
            acc = acc + _dot_nt(p_ref[...], w_ref[:, off:off + k])
        o_ref[...] = acc

    return pl.pallas_call(
        body, name="in_proj_bwd", grid=(t // TM,),
        in_specs=[_rows(TM, k) for _, k in DPROJ_PIECES] + [_const(w.shape), _rows(TM, D_MODEL)],
        out_specs=_rows(TM, D_MODEL), out_shape=jax.ShapeDtypeStruct((t, D_MODEL), F32),
        compiler_params=_cparams(("arbitrary",)),
    )(*pieces, w, dres)


def _wgrad_in(h0, pieces):
    t = h0.shape[0]
    npc = len(pieces)

    def body(*refs):
        h_ref, o_ref = refs[npc], refs[npc + 1]
        s = pl.program_id(0)
        hb = h_ref[...].astype(MXU_DTYPE)
        for p_ref, (off, k) in zip(refs[:npc], DPROJ_PIECES):
            part = _dot_tn(hb, p_ref[...])

            @pl.when(s == 0)
            def _():
                o_ref[:, off:off + k] = part

            @pl.when(s > 0)
            def _():
                o_ref[:, off:off + k] += part

    return pl.pallas_call(
        body, name="wgrad_in", grid=(t // TM,),
        in_specs=[_rows(TM, k) for _, k in DPROJ_PIECES] + [_rows(TM, D_MODEL)],
        out_specs=_const((D_MODEL, D_PACK)), out_shape=jax.ShapeDtypeStruct((D_MODEL, D_PACK), F32),
        compiler_params=_cparams(("arbitrary",)),
    )(*pieces, h0)


G_ROWS = 8192
ROW_MLP_W1 = 0
ROW_MLP_W2 = 2048
ROW_W_IN = 4096
ROW_W_OUT = 5376
ROW_XA = {"xa_wq": 5888, "xa_wk": 6400, "xa_wv": 6912, "xa_wo": 7424}
ROW_MISC = 7936
MISC_ROWS = G_ROWS - ROW_MISC
MISC_REP_ROW = 40
W_IN_SHARD = 578
W_IN_PAD = 640


def _wgrad_flat(a, g, buf, *, mode, row_off, name):
    pieces = list(a) if isinstance(a, (list, tuple)) else [a]
    t = g.shape[0]
    tt = min(1024, t)
    ns = t // tt
    blk = D_MODEL

    def accumulate(o_ref, parts, s):
        @pl.when(s == 0)
        def _():
            for q, v in parts:
                o_ref[q] = v

        @pl.when(s > 0)
        def _():
            for q, v in parts:
                o_ref[q] += v

    if mode == "rows4":
        grid = (ns,)
        in_specs = [pl.BlockSpec((tt, p.shape[1]), lambda s: (s, 0)) for p in pieces]
        in_specs.append(pl.BlockSpec((tt, blk), lambda s: (s, 0)))
        out_spec = pl.BlockSpec((4, 256, FLAT), lambda s: (0, row_off // 256, 0))
        sem = ("arbitrary",)
        npc = len(pieces)

        def body(*refs):
            g_v = refs[npc][...]
            parts, q0 = [], 0
            for p_ref in refs[:npc]:
                full = _dot_tn(p_ref[...], g_v)
                nq = full.shape[0] // 256
                parts += [(q0 + q, full[q * 256:(q + 1) * 256]) for q in range(nq)]
                q0 += nq
            accumulate(refs[-1], parts, pl.program_id(0))
    else:
        grid = (2, ns)
        if mode == "rowblk":
            in_specs = [pl.BlockSpec((tt, 2 * blk), lambda q, s: (s, q)), pl.BlockSpec((tt, blk), lambda q, s: (s, 0))]
        else:
            in_specs = [pl.BlockSpec((tt, blk), lambda q, s: (s, 0)), pl.BlockSpec((tt, 2 * blk), lambda q, s: (s, q))]
        out_spec = pl.BlockSpec((2, blk, FLAT), lambda q, s: (q, row_off // blk, 0))
        sem = ("arbitrary", "arbitrary")

        def body(a_ref, g_ref, *rest):
            full = _dot_tn(a_ref[...], g_ref[...])
            if mode == "rowblk":
                parts = [(0, full[:blk]), (1, full[blk:])]
            else:
                parts = [(0, full[:, :blk]), (1, full[:, blk:])]
            accumulate(rest[-1], parts, pl.program_id(1))

    args = pieces + [g]
    aliases = {}
    if buf is not None:
        in_specs.append(pl.BlockSpec(memory_space=pl.ANY))
        args.append(buf)
        aliases = {len(args) - 1: 0}
    return pl.pallas_call(
        body, name=name, grid=grid, in_specs=in_specs, out_specs=out_spec,
        out_shape=jax.ShapeDtypeStruct((4, G_ROWS, FLAT), F32), input_output_aliases=aliases,
        compiler_params=_cparams(sem),
    )(*args)


def _outproj_ln_fwd(ys, h, w, g, b):
    t = h.shape[0]
    npc = len(ys)

    def body(*refs):
        h_ref, w_ref, g_ref, b_ref, hn_ref, xh_ref, rs_ref = refs[npc:]
        r = ALPHA * h_ref[...]
        off = 0
        for y_ref in refs[:npc]:
            k = y_ref.shape[1]
            r = r + _dot(y_ref[...], w_ref[off:off + k, :])
            off += k
        out, xhat, rstd = _ln_fwd(r, g_ref[...], b_ref[...])
        hn_ref[...] = out
        xh_ref[...] = xhat
        rs_ref[...] = rstd

    return pl.pallas_call(
        body, name="outproj_ln_fwd", grid=(t // TM,),
        in_specs=[_rows(TM, y.shape[1]) for y in ys] + [_rows(TM, D_MODEL), _const((D_MODEL, D_MODEL)),
                                                        _const((1, D_MODEL)), _const((1, D_MODEL))],
        out_specs=[_rows(TM, D_MODEL), _rows(TM, D_MODEL), _rows(TM, 1)],
        out_shape=[jax.ShapeDtypeStruct((t, D_MODEL), F32), jax.ShapeDtypeStruct((t, D_MODEL), F32),
                   jax.ShapeDtypeStruct((t, 1), F32)],
        compiler_params=_cparams(("arbitrary",)),
    )(*ys, h, w, g, b)


def _attn_probs(q, kb, hh):
    sl = slice(hh * XA_HEAD_DIM, (hh + 1) * XA_HEAD_DIM)
    s = _dot_nt(q[:, sl], kb[:, sl]) * (1.0 / math.sqrt(XA_HEAD_DIM))
    m = jnp.max(s, axis=-1, keepdims=True)
    e = jnp.exp(s - m)
    return e / jnp.sum(e, axis=-1, keepdims=True)


def _attn_ln_fwd(h1, wq, wo, kb, vb, g, b):
    t = h1.shape[0]

    def body(h_ref, wq_ref, wo_ref, k_ref, v_ref, g_ref, b_ref, hn_ref, xh_ref, rs_ref, o_ref):
        h = h_ref[...]
        q = _dot(h, wq_ref[...])
        kb_ = k_ref[...]
        vb_ = v_ref[...]
        for hh in range(XA_HEADS):
            sl = slice(hh * XA_HEAD_DIM, (hh + 1) * XA_HEAD_DIM)
            p = _attn_probs(q, kb_, hh)
            o_ref[:, sl] = _dot(p, vb_[:, sl]).astype(o_ref.dtype)
        r = ALPHA * h + _dot(o_ref[...], wo_ref[...])
        out, xhat, rstd = _ln_fwd(r, g_ref[...], b_ref[...])
        hn_ref[...] = out
        xh_ref[...] = xhat
        rs_ref[...] = rstd

    return pl.pallas_call(
        body, name="attn_ln_fwd", grid=(t // TM,),
        in_specs=[_rows(TM, D_MODEL), _const((D_MODEL, D_MODEL)), _const((D_MODEL, D_MODEL)),
                  _const((MEM_LEN, D_MODEL)), _const((MEM_LEN, D_MODEL)), _const((1, D_MODEL)), _const((1, D_MODEL))],
        out_specs=[_rows(TM, D_MODEL), _rows(TM, D_MODEL), _rows(TM, 1), _rows(TM, D_MODEL)],
        out_shape=[jax.ShapeDtypeStruct((t, D_MODEL), F32), jax.ShapeDtypeStruct((t, D_MODEL), F32),
                   jax.ShapeDtypeStruct((t, 1), F32), jax.ShapeDtypeStruct((t, D_MODEL), MXU_DTYPE)],
        compiler_params=_cparams(("arbitrary",)),
    )(h1, wq, wo, kb, vb, g, b)


def _attn_ln_bwd(dh2, xhat, rstd, g, h1, wq, wo, kb, vb):
    t = h1.shape[0]

    def body(dh_ref, xh_ref, rs_ref, g_ref, h_ref, wq_ref, wo_ref, k_ref, v_ref,
             dr_ref, dq_ref, dh1_ref, dk_ref, dv_ref, dg_ref, db_ref):
        i = pl.program_id(0)

        @pl.when(i == 0)
        def _():
            dk_ref[...] = jnp.zeros_like(dk_ref)
            dv_ref[...] = jnp.zeros_like(dv_ref)
            dg_ref[...] = jnp.zeros_like(dg_ref)
            db_ref[...] = jnp.zeros_like(db_ref)

        dout = dh_ref[...]
        xh = xh_ref[...]
        dg_ref[...] += _sum0(dout * xh)
        db_ref[...] += _sum0(dout)
        dr = _ln_bwd(dout, xh, rs_ref[...], g_ref[...])
        dr_ref[...] = dr.astype(dr_ref.dtype)
        do = _dot_nt(dr, wo_ref[...])
        h = h_ref[...]
        q = _dot(h, wq_ref[...])
        kb_ = k_ref[...]
        vb_ = v_ref[...]
        scale = 1.0 / math.sqrt(XA_HEAD_DIM)
        for hh in range(XA_HEADS):
            sl = slice(hh * XA_HEAD_DIM, (hh + 1) * XA_HEAD_DIM)
            p = _attn_probs(q, kb_, hh)
            do_h = do[:, sl]
            dp = _dot_nt(do_h, vb_[:, sl])
            ds = p * (dp - jnp.sum(dp * p, axis=-1, keepdims=True)) * scale
            dq_ref[:, sl] = _dot(ds, kb_[:, sl]).astype(dq_ref.dtype)
            dk_ref[:, sl] += _dot_tn(ds, q[:, sl])
            dv_ref[:, sl] += _dot_tn(p, do_h)
        dh1_ref[...] = ALPHA * dr + _dot_nt(dq_ref[...], wq_ref[...])

    return pl.pallas_call(
        body, name="attn_ln_bwd", grid=(t // TM,),
        in_specs=[_rows(TM, D_MODEL), _rows(TM, D_MODEL), _rows(TM, 1), _const((1, D_MODEL)), _rows(TM, D_MODEL),
                  _const((D_MODEL, D_MODEL)), _const((D_MODEL, D_MODEL)), _const((MEM_LEN, D_MODEL)),
                  _const((MEM_LEN, D_MODEL))],
        out_specs=[_rows(TM, D_MODEL), _rows(TM, D_MODEL), _rows(TM, D_MODEL), _const((MEM_LEN, D_MODEL)),
                   _const((MEM_LEN, D_MODEL)), _const((1, D_MODEL)), _const((1, D_MODEL))],
        out_shape=[jax.ShapeDtypeStruct((t, D_MODEL), MXU_DTYPE), jax.ShapeDtypeStruct((t, D_MODEL), MXU_DTYPE),
                   jax.ShapeDtypeStruct((t, D_MODEL), F32), jax.ShapeDtypeStruct((MEM_LEN, D_MODEL), F32),
                   jax.ShapeDtypeStruct((MEM_LEN, D_MODEL), F32), jax.ShapeDtypeStruct((1, D_MODEL), F32),
                   jax.ShapeDtypeStruct((1, D_MODEL), F32)],
        compiler_params=_cparams(("arbitrary",)),
    )(dh2, xhat, rstd, g, h1, wq, wo, kb, vb)


FF_CHUNK = 1024
N_FF = D_FF // FF_CHUNK


def _load_resident(pairs, sems):
    copies = [pltpu.make_async_copy(src, dst, sems.at[k]) for k, (src, dst) in enumerate(pairs)]
    for cp in copies:
        cp.start()
    for cp in copies:
        cp.wait()


def _mlp_ln_fwd(h2, w1, w2, g, b):
    t = h2.shape[0]

    def body(h_ref, w1_hbm, w2_hbm, g_ref, b_ref, hn_ref, xh_ref, rs_ref, hd_ref, w1_v, w2_v, acc_ref, sems):
        @pl.when(pl.program_id(0) == 0)
        def _():
            _load_resident([(w1_hbm, w1_v), (w2_hbm, w2_v)], sems)

        h = h_ref[...]
        hb = h.astype(MXU_DTYPE)
        acc_ref[...] = ALPHA * h
        for j in range(N_FF):
            sl = slice(j * FF_CHUNK, (j + 1) * FF_CHUNK)
            u = _dot(hb, w1_v[:, sl])
            hd = jnp.square(jnp.maximum(u, 0.0)).astype(MXU_DTYPE)
            hd_ref[:, sl] = hd
            acc_ref[...] += _dot(hd, w2_v[sl, :])
        out, xhat, rstd = _ln_fwd(acc_ref[...], g_ref[...], b_ref[...])
        hn_ref[...] = out
        xh_ref[...] = xhat
        rs_ref[...] = rstd

    return pl.pallas_call(
        body, name="mlp_ln_fwd", grid=(t // TM,),
        in_specs=[_rows(TM, D_MODEL), _hbm(), _hbm(), _const((1, D_MODEL)), _const((1, D_MODEL))],
        out_specs=[_rows(TM, D_MODEL), _rows(TM, D_MODEL), _rows(TM, 1), _rows(TM, D_FF)],
        out_shape=[jax.ShapeDtypeStruct((t, D_MODEL), F32), jax.ShapeDtypeStruct((t, D_MODEL), F32),
                   jax.ShapeDtypeStruct((t, 1), F32), jax.ShapeDtypeStruct((t, D_FF), MXU_DTYPE)],
        scratch_shapes=[pltpu.VMEM((D_MODEL, D_FF), MXU_DTYPE), pltpu.VMEM((D_FF, D_MODEL), MXU_DTYPE),
                        pltpu.VMEM((TM, D_MODEL), F32), pltpu.SemaphoreType.DMA((2,))],
        compiler_params=_cparams(("arbitrary",)),
    )(h2, w1, w2, g, b)


def _mlp_ln_bwd(dh3, xhat, rstd, g, hdn, w1, w2):
    t = dh3.shape[0]

    def body(dh_ref, xh_ref, rs_ref, g_ref, hd_ref, w1_hbm, w2_hbm,
             dr_ref, du_ref, dh2_ref, dg_ref, db_ref, w1_v, w2_v, acc_ref, sems):
        @pl.when(pl.program_id(0) == 0)
        def _():
            _load_resident([(w1_hbm, w1_v), (w2_hbm, w2_v)], sems)
            dg_ref[...] = jnp.zeros_like(dg_ref)
            db_ref[...] = jnp.zeros_like(db_ref)

        dout = dh_ref[...]
        xh = xh_ref[...]
        dg_ref[...] += _sum0(dout * xh)
        db_ref[...] += _sum0(dout)
        dr = _ln_bwd(dout, xh, rs_ref[...], g_ref[...])
        drb = dr.astype(MXU_DTYPE)
        dr_ref[...] = drb
        acc_ref[...] = ALPHA * dr
        for j in range(N_FF):
            sl = slice(j * FF_CHUNK, (j + 1) * FF_CHUNK)
            dhd = _dot_nt(drb, w2_v[sl, :])
            du = (dhd * (2.0 * jnp.sqrt(hd_ref[:, sl].astype(F32)))).astype(MXU_DTYPE)
            du_ref[:, sl] = du
            acc_ref[...] += _dot_nt(du, w1_v[:, sl])
        dh2_ref[...] = acc_ref[...]

    tm = TM // 2
    return pl.pallas_call(
        body, name="mlp_ln_bwd", grid=(t // tm,),
        in_specs=[_rows(tm, D_MODEL), _rows(tm, D_MODEL), _rows(tm, 1), _const((1, D_MODEL)), _rows(tm, D_FF),
                  _hbm(), _hbm()],
        out_specs=[_rows(tm, D_MODEL), _rows(tm, D_FF), _rows(tm, D_MODEL), _const((1, D_MODEL)),
                   _const((1, D_MODEL))],
        out_shape=[jax.ShapeDtypeStruct((t, D_MODEL), MXU_DTYPE), jax.ShapeDtypeStruct((t, D_FF), MXU_DTYPE),
                   jax.ShapeDtypeStruct((t, D_MODEL), F32), jax.ShapeDtypeStruct((1, D_MODEL), F32),
                   jax.ShapeDtypeStruct((1, D_MODEL), F32)],
        scratch_shapes=[pltpu.VMEM((D_MODEL, D_FF), MXU_DTYPE), pltpu.VMEM((D_FF, D_MODEL), MXU_DTYPE),
                        pltpu.VMEM((tm, D_MODEL), F32), pltpu.SemaphoreType.DMA((2,))],
        compiler_params=_cparams(("arbitrary",)),
    )(dh3, xhat, rstd, g, hdn, w1, w2)


def _outproj_ln_bwd(dh1, xhat, rstd, g, w):
    t = dh1.shape[0]

    def body(dh_ref, xh_ref, rs_ref, g_ref, w_ref, dr_ref, res_ref, dy_ref, dg_ref, db_ref):
        i = pl.program_id(0)

        @pl.when(i == 0)
        def _():
            dg_ref[...] = jnp.zeros_like(dg_ref)
            db_ref[...] = jnp.zeros_like(db_ref)

        dout = dh_ref[...]
        xh = xh_ref[...]
        dg_ref[...] += _sum0(dout * xh)
        db_ref[...] += _sum0(dout)
        dr = _ln_bwd(dout, xh, rs_ref[...], g_ref[...])
        dr_ref[...] = dr.astype(dr_ref.dtype)
        res_ref[...] = ALPHA * dr
        dy_ref[...] = _dot_nt(dr, w_ref[...])

    return pl.pallas_call(
        body, name="outproj_ln_bwd", grid=(t // TM,),
        in_specs=[_rows(TM, D_MODEL), _rows(TM, D_MODEL), _rows(TM, 1), _const((1, D_MODEL)),
                  _const((D_MODEL, D_MODEL))],
        out_specs=[_rows(TM, D_MODEL), _rows(TM, D_MODEL), _rows(TM, D_MODEL), _const((1, D_MODEL)),
                   _const((1, D_MODEL))],
        out_shape=[jax.ShapeDtypeStruct((t, D_MODEL), MXU_DTYPE), jax.ShapeDtypeStruct((t, D_MODEL), F32),
                   jax.ShapeDtypeStruct((t, D_MODEL), F32), jax.ShapeDtypeStruct((1, D_MODEL), F32),
                   jax.ShapeDtypeStruct((1, D_MODEL), F32)],
        compiler_params=_cparams(("arbitrary",)),
    )(dh1, xhat, rstd, g, w)


def _loss_fwd_bwd(h, target):
    t = h.shape[0]

    def body(h_ref, t_ref, l_ref, dh_ref):
        i = pl.program_id(0)

        @pl.when(i == 0)
        def _():
            l_ref[...] = jnp.zeros_like(l_ref)

        e = h_ref[...] - t_ref[...]
        dh_ref[...] = e * (1.0 / D_MODEL)
        per_tok = jnp.mean(e * e, axis=-1, keepdims=True)
        l_ref[...] += 0.5 * jnp.sum(per_tok, axis=0, keepdims=True)

    return pl.pallas_call(
        body, name="loss_fwd_bwd", grid=(t // TM,),
        in_specs=[_rows(TM, D_MODEL), _rows(TM, D_MODEL)],
        out_specs=[_const((1, 1)), _rows(TM, D_MODEL)],
        out_shape=[jax.ShapeDtypeStruct((1, 1), F32), jax.ShapeDtypeStruct((t, D_MODEL), F32)],
        compiler_params=_cparams(("arbitrary",)),
    )(h, target)


def _pick_col(x, idx):
    lane = lax.broadcasted_iota(jnp.int32, x.shape, 1)
    return jnp.sum(jnp.where(lane == idx, x, 0.0), axis=1, keepdims=True)


def _pick_row(x, idx):
    sub = lax.broadcasted_iota(jnp.int32, x.shape, 0)
    return jnp.sum(jnp.where(sub == idx, x, 0.0), axis=0, keepdims=True)


def _conv_taps(pad_ref, w, tm, base):
    acc = w[0:1, :] * pad_ref[base:base + tm, :]
    for k in range(1, 4):
        acc = acc + w[k:k + 1, :] * pad_ref[base + k:base + k + tm, :]
    return acc


def _ssd_chunk_common(adt_c, tri):
    cs = _dot_f32(tri, adt_c)
    return cs, cs.T, jnp.exp(cs)


def _ssd_head_terms(cs, cst, ecs, dt_c, h, tri):
    cs_col = _pick_col(cs, h)
    cs_row = _pick_row(cst, h)
    dt_col = _pick_col(dt_c, h)
    cs_last = cs_col[SSD_CHUNK - 1:SSD_CHUNK, :]
    lmat = jnp.exp(jnp.where(tri > 0.0, cs_col - cs_row, -1e30))
    ecs_col = _pick_col(ecs, h)
    decay_col = jnp.exp(cs_last - cs_col)
    return cs_col, dt_col, cs_last, lmat, ecs_col, decay_col


def _ssd_fwd(proj, cw, cb, dtb, a_neg, d_lanes, nw):
    t = proj.shape[0]
    tm = SSD_TM
    nt = t // tm
    ncq = tm // SSD_CHUNK
    hb = tm // SUBLANES

    def body(xbc_ref, halo_ref, z_ref, dt_ref, cw_ref, cb_ref, dtb_ref, a_ref, d_ref, nw_ref,
             y_ref, yy_ref, st_ref, xpad, xact, state):
        i = pl.program_id(0)

        @pl.when(i == 0)
        def _():
            state[...] = jnp.zeros_like(state)

        xpad[0:SUBLANES, :] = jnp.where(i > 0, halo_ref[...], 0.0)
        xpad[SUBLANES:SUBLANES + tm, :] = xbc_ref[...]
        acc = cb_ref[...] + _conv_taps(xpad, cw_ref[...], tm, SUBLANES - 3)
        xact[...] = acc * _sigmoid(acc)
        dt = _softplus(dt_ref[...] + dtb_ref[...])
        adt = dt * a_ref[...]
        r_i = lax.broadcasted_iota(jnp.int32, (SSD_CHUNK, SSD_CHUNK), 0)
        c_i = lax.broadcasted_iota(jnp.int32, (SSD_CHUNK, SSD_CHUNK), 1)
        tri = (r_i >= c_i).astype(F32)
        lane1 = lax.broadcasted_iota(jnp.int32, (1, LANES), 1)
        for c in range(ncq):
            sl = slice(c * SSD_CHUNK, (c + 1) * SSD_CHUNK)
            dt_c = dt[sl]
            cs, cst, ecs = _ssd_chunk_common(adt[sl], tri)
            for g in range(2):
                bg = xact[sl, 512 + g * 128:512 + (g + 1) * 128]
                cg = xact[sl, 768 + g * 128:768 + (g + 1) * 128]
                cbm = _dot_nt(cg, bg)
                for pr in range(2):
                    pi = g * 2 + pr
                    psl = slice(pi * 128, (pi + 1) * 128)
                    xp = xact[sl, psl]
                    prev = state[pi]
                    st_ref[c, pi] = prev
                    yp = xp * d_ref[:, psl]
                    new_s = jnp.zeros((SSD_STATE, LANES), F32)
                    dec_lane = jnp.zeros((1, LANES), F32)
                    for hh in range(2):
                        h = g * 4 + pr * 2 + hh
                        lm = (lane1 >= 64) if hh else (lane1 < 64)
                        _, dt_col, cs_last, lmat, ecs_col, decay_col = _ssd_head_terms(cs, cst, ecs, dt_c, h, tri)
                        xdt = jnp.where(lm, xp, 0.0) * dt_col
                        yp = yp + _dot(cbm * lmat, xdt)
                        yp = yp + _dot(cg * ecs_col, jnp.where(lm, prev, 0.0))
                        new_s = new_s + _dot_tn(bg * decay_col, xdt)
                        dec_lane = dec_lane + jnp.where(lm, jnp.exp(cs_last), 0.0)
                    state[pi] = prev * dec_lane + new_s
                    yy_ref[sl, psl] = yp
        yy = yy_ref[...]
        z = z_ref[...]
        yg = yy * (z * _sigmoid(z))
        ms = jnp.mean(yg * yg, axis=-1, keepdims=True)
        y_ref[...] = yg * lax.rsqrt(ms + LN_EPS) * nw_ref[...]

    halo_map = lambda i: (jnp.maximum(i * hb - 1, 0), 0)
    return pl.pallas_call(
        body, name="ssd_fwd", grid=(nt,),
        in_specs=[pl.BlockSpec((tm, SSD_XBC), lambda i: (i, 0)), pl.BlockSpec((SUBLANES, SSD_XBC), halo_map),
                  pl.BlockSpec((tm, SSD_WIDTH), lambda i: (i, P_Z // SSD_WIDTH)),
                  pl.BlockSpec((tm, LANES), lambda i: (i, P_DT // LANES)),
                  _const((4, SSD_XBC)), _const((1, SSD_XBC)), _const((1, LANES)), _const((1, LANES)),
                  _const((1, SSD_WIDTH)), _const((1, SSD_WIDTH))],
        out_specs=[_rows(tm, SSD_WIDTH), _rows(tm, SSD_WIDTH),
                   pl.BlockSpec((ncq, 4, SSD_STATE, LANES), lambda i: (i, 0, 0, 0))],
        out_shape=[jax.ShapeDtypeStruct((t, SSD_WIDTH), F32), jax.ShapeDtypeStruct((t, SSD_WIDTH), F32),
                   jax.ShapeDtypeStruct((t // SSD_CHUNK, 4, SSD_STATE, LANES), F32)],
        scratch_shapes=[pltpu.VMEM((tm + SUBLANES, SSD_XBC), F32), pltpu.VMEM((tm, SSD_XBC), F32),
                        pltpu.VMEM((4, SSD_STATE, LANES), F32)],
        compiler_params=_cparams(("arbitrary",)),
    )(proj, proj, proj, proj, cw, cb, dtb, a_neg, d_lanes, nw)


def _ssd_bwd(dycat, proj, yy, states, cw, cb, dtb, a_neg, d_lanes, nw):
    t = proj.shape[0]
    tm = SSD_TM
    nt = t // tm
    ncq = tm // SSD_CHUNK
    hb = tm // SUBLANES

    def body(dy_ref, xbc_ref, halo_ref, z_ref, dt_ref, yy_ref, st_ref, cw_ref, cb_ref, dtb_ref, a_ref, d_ref, nw_ref,
             dxbc_ref, dz_ref, ddt_ref, dcw_ref, dcb_ref, ddtb_ref, da_ref, dd_ref, dnw_ref,
             xpad, xact, dxact, dpad, dstate, dnext):
        i = pl.program_id(0)

        @pl.when(i == 0)
        def _():
            for r in (dcw_ref, dcb_ref, ddtb_ref, da_ref, dd_ref, dnw_ref, dstate, dnext):
                r[...] = jnp.zeros_like(r)

        xpad[0:SUBLANES, :] = jnp.where(i < nt - 1, halo_ref[...], 0.0)
        xpad[SUBLANES:SUBLANES + tm, :] = xbc_ref[...]
        cw_v = cw_ref[...]
        acc = cb_ref[...] + _conv_taps(xpad, cw_v, tm, SUBLANES - 3)
        sig = _sigmoid(acc)
        xact[...] = acc * sig
        dt_raw = dt_ref[...] + dtb_ref[...]
        dt = _softplus(dt_raw)
        a_v = a_ref[...]
        adt = dt * a_v
        yy = yy_ref[...]
        z = z_ref[...]
        sz = _sigmoid(z)
        siluz = z * sz
        yg = yy * siluz
        ms = jnp.mean(yg * yg, axis=-1, keepdims=True)
        rinv = lax.rsqrt(ms + LN_EPS)
        dout = dy_ref[...]
        dnw_ref[...] += _sum0(dout * yg * rinv)
        dyn = dout * nw_ref[...]
        dyg = rinv * dyn - yg * (rinv * rinv * rinv) * jnp.mean(dyn * yg, axis=-1, keepdims=True)
        dyy = dyg * siluz
        dz_ref[...] = dyg * yy * (sz * (1.0 + z * (1.0 - sz)))
        dd_ref[...] += _sum0(dyy * xact[:, 0:SSD_WIDTH])

        r_i = lax.broadcasted_iota(jnp.int32, (SSD_CHUNK, SSD_CHUNK), 0)
        c_i = lax.broadcasted_iota(jnp.int32, (SSD_CHUNK, SSD_CHUNK), 1)
        tri = (r_i >= c_i).astype(F32)
        lane1 = lax.broadcasted_iota(jnp.int32, (1, LANES), 1)
        for c in reversed(range(ncq)):
            sl = slice(c * SSD_CHUNK, (c + 1) * SSD_CHUNK)
            dt_c = dt[sl]
            cs, cst, ecs = _ssd_chunk_common(adt[sl], tri)
            cacc = jnp.zeros((SSD_CHUNK, LANES), F32)
            racc = jnp.zeros((SSD_CHUNK, LANES), F32)
            ddtx = jnp.zeros((SSD_CHUNK, LANES), F32)
            for g in range(2):
                bg = xact[sl, 512 + g * 128:512 + (g + 1) * 128]
                cg = xact[sl, 768 + g * 128:768 + (g + 1) * 128]
                cbm = _dot_nt(cg, bg)
                dcb_m = jnp.zeros((SSD_CHUNK, SSD_CHUNK), F32)
                dbg = jnp.zeros((SSD_CHUNK, SSD_STATE), F32)
                dcg = jnp.zeros((SSD_CHUNK, SSD_STATE), F32)
                for pr in range(2):
                    pi = g * 2 + pr
                    psl = slice(pi * 128, (pi + 1) * 128)
                    xp = xact[sl, psl]
                    dyp = dyy[sl, psl]
                    prev = st_ref[c, pi]
                    ds_all = dstate[pi]
                    dxdt_p = jnp.zeros((SSD_CHUNK, LANES), F32)
                    dprev_new = jnp.zeros((SSD_STATE, LANES), F32)
                    dec_lane = jnp.zeros((1, LANES), F32)
                    dt_lanes = jnp.zeros((SSD_CHUNK, LANES), F32)
                    for hh in range(2):
                        h = g * 4 + pr * 2 + hh
                        lm = (lane1 >= 64) if hh else (lane1 < 64)
                        oh_l = (c_i == h).astype(F32)
                        oh_s = (r_i == h).astype(F32)
                        _, dt_col, cs_last, lmat, ecs_col, decay_col = _ssd_head_terms(cs, cst, ecs, dt_c, h, tri)
                        gm = cbm * lmat
                        xm = jnp.where(lm, xp, 0.0)
                        xdt = xm * dt_col
                        dym = jnp.where(lm, dyp, 0.0)
                        prevm = jnp.where(lm, prev, 0.0)
                        dsm = jnp.where(lm, ds_all, 0.0)
                        bdec = bg * decay_col
                        dxdt = _dot_tn(gm, dym) + _dot(bdec, dsm)
                        dxdt_p = dxdt_p + dxdt
                        ddtx = ddtx + oh_l * jnp.sum(dxdt * xm, axis=1, keepdims=True)
                        dt_lanes = dt_lanes + jnp.where(lm, dt_col, 0.0)
                        dgm = _dot_nt(dym, xdt)
                        dcb_m = dcb_m + dgm * lmat
                        w = dgm * gm
                        cacc = cacc + oh_l * jnp.sum(w, axis=1, keepdims=True)
                        racc = racc - oh_s * jnp.sum(w, axis=0, keepdims=True)
                        dce = _dot_nt(dym, prevm)
                        dcg = dcg + dce * ecs_col
                        cacc = cacc + oh_l * (jnp.sum(dce * cg, axis=1, keepdims=True) * ecs_col)
                        dprev_new = dprev_new + _dot_tn(cg * ecs_col, dym)
                        dbdec = _dot_nt(xdt, dsm)
                        dbg = dbg + dbdec * decay_col
                        dd = jnp.sum(dbdec * bg, axis=1, keepdims=True) * decay_col
                        cacc = cacc - oh_l * dd
                        cd = jnp.exp(cs_last)
                        dlast = jnp.sum(dd, axis=0, keepdims=True) + jnp.sum(
                            jnp.sum(dsm * prevm, axis=1, keepdims=True), axis=0, keepdims=True) * cd
                        cacc = cacc + jnp.where((r_i == SSD_CHUNK - 1) & (c_i == h), dlast, 0.0)
                        dec_lane = dec_lane + jnp.where(lm, cd, 0.0)
                    dstate[pi] = ds_all * dec_lane + dprev_new
                    dxact[sl, psl] = dxdt_p * dt_lanes + dyp * d_ref[:, psl]
                dcg = dcg + _dot(dcb_m, bg)
                dbg = dbg + _dot_tn(dcb_m, cg)
                dxact[sl, 512 + g * 128:512 + (g + 1) * 128] = dbg
                dxact[sl, 768 + g * 128:768 + (g + 1) * 128] = dcg
            dcs = cacc + racc.T
            dadt = _dot_f32((r_i <= c_i).astype(F32), dcs)
            ddt = dadt * a_v + ddtx
            da_ref[...] += _sum0(dadt * dt_c)
            ddt_raw = ddt * _sigmoid(dt_raw[sl])
            ddt_ref[sl, :] = ddt_raw
            ddtb_ref[...] += _sum0(ddt_raw)
        dacc = dxact[...] * (sig * (1.0 + acc * (1.0 - sig)))
        dcb_ref[...] += _sum0(dacc)
        for k in range(4):
            dcw_ref[k:k + 1, :] += _sum0(dacc * xpad[SUBLANES - 3 + k:SUBLANES - 3 + k + tm, :])
        dpad[0:tm, :] = dacc
        dpad[tm:tm + SUBLANES, :] = dnext[...]
        dx = cw_v[0:1, :] * dpad[3:3 + tm, :]
        for k in range(1, 4):
            dx = dx + cw_v[k:k + 1, :] * dpad[3 - k:3 - k + tm, :]
        dxbc_ref[...] = dx
        dnext[...] = dacc[0:SUBLANES, :]

    rev = lambda i: nt - 1 - i
    halo_map = lambda i: (jnp.maximum(rev(i) * hb - 1, 0), 0)
    rrow = lambda n, col=0: pl.BlockSpec((tm, n), lambda i: (rev(i), col))
    return pl.pallas_call(
        body, name="ssd_bwd", grid=(nt,),
        in_specs=[rrow(SSD_WIDTH), rrow(SSD_XBC), pl.BlockSpec((SUBLANES, SSD_XBC), halo_map),
                  rrow(SSD_WIDTH, P_Z // SSD_WIDTH), rrow(LANES, P_DT // LANES), rrow(SSD_WIDTH),
                  pl.BlockSpec((ncq, 4, SSD_STATE, LANES), lambda i: (rev(i), 0, 0, 0)),
                  _const((4, SSD_XBC)), _const((1, SSD_XBC)), _const((1, LANES)), _const((1, LANES)),
                  _const((1, SSD_WIDTH)), _const((1, SSD_WIDTH))],
        out_specs=[rrow(SSD_XBC), rrow(SSD_WIDTH), rrow(LANES), _const((SUBLANES, SSD_XBC)), _const((1, SSD_XBC)),
                   _const((1, LANES)), _const((1, LANES)), _const((1, SSD_WIDTH)), _const((1, SSD_WIDTH))],
        out_shape=[jax.ShapeDtypeStruct((t, SSD_XBC), F32), jax.ShapeDtypeStruct((t, SSD_WIDTH), F32),
                   jax.ShapeDtypeStruct((t, LANES), F32), jax.ShapeDtypeStruct((SUBLANES, SSD_XBC), F32),
                   jax.ShapeDtypeStruct((1, SSD_XBC), F32), jax.ShapeDtypeStruct((1, LANES), F32),
                   jax.ShapeDtypeStruct((1, LANES), F32), jax.ShapeDtypeStruct((1, SSD_WIDTH), F32),
                   jax.ShapeDtypeStruct((1, SSD_WIDTH), F32)],
        scratch_shapes=[pltpu.VMEM((tm + SUBLANES, SSD_XBC), F32), pltpu.VMEM((tm, SSD_XBC), F32),
                        pltpu.VMEM((tm, SSD_XBC), F32), pltpu.VMEM((tm + SUBLANES, SSD_XBC), F32),
                        pltpu.VMEM((4, SSD_STATE, LANES), F32), pltpu.VMEM((SUBLANES, SSD_XBC), F32)],
        compiler_params=_cparams(("arbitrary",)),
    )(dycat, proj, proj, proj, proj, yy, states, cw, cb, dtb, a_neg, d_lanes, nw)


def _cmul_add(ar, ai, br, bi, cr, ci):
    return ar + br * cr - bi * ci, ai + br * ci + bi * cr


def _s5_fwd(proj, bre, bim, cre, cim, d_skip, glu_w, glu_b, coef):
    t = proj.shape[0]
    tm = SCAN_TM
    ng = tm // SUBLANES

    def body(u_ref, bre_ref, bim_ref, cre_ref, cim_ref, d_ref, w_ref, b_ref, coef_ref,
             y_ref, y2_ref, hre_ref, him_ref, carry):
        i = pl.program_id(0)

        @pl.when(i == 0)
        def _():
            carry[...] = jnp.zeros_like(carry)

        u = u_ref[...]
        hre_ref[...] = _dot(u, bre_ref[...])
        him_ref[...] = _dot(u, bim_ref[...])

        def step(gi, car):
            cr_, ci_ = car
            rows = pl.ds(pl.multiple_of(gi * SUBLANES, SUBLANES), SUBLANES)
            r = hre_ref[rows, :]
            m = him_ref[rows, :]
            for k, sh in enumerate((1, 2, 4)):
                r, m = _cmul_add(r, m, coef_ref[k, 0], coef_ref[k, 1], pltpu.roll(r, sh, 0), pltpu.roll(m, sh, 0))
            r, m = _cmul_add(r, m, coef_ref[3, 0], coef_ref[3, 1], cr_, ci_)
            hre_ref[rows, :] = r
            him_ref[rows, :] = m
            return (jnp.broadcast_to(r[SUBLANES - 1:SUBLANES, :], r.shape),
                    jnp.broadcast_to(m[SUBLANES - 1:SUBLANES, :], m.shape))

        cr_, ci_ = lax.fori_loop(0, ng, step, (carry[0], carry[1]))
        carry[0] = cr_
        carry[1] = ci_
        y2 = _dot(hre_ref[...], cre_ref[...]) - _dot(him_ref[...], cim_ref[...]) + d_ref[...] * u
        y2_ref[...] = y2
        ya = _gelu(y2)
        y_ref[...] = ya * _sigmoid(_dot(ya, w_ref[...]) + b_ref[...])

    return pl.pallas_call(
        body, name="s5_fwd", grid=(t // tm,),
        in_specs=[pl.BlockSpec((tm, S5_WIDTH), lambda i: (i, P_U // S5_WIDTH)),
                  _const((S5_WIDTH, S5_NSTATE)), _const((S5_WIDTH, S5_NSTATE)), _const((S5_NSTATE, S5_WIDTH)),
                  _const((S5_NSTATE, S5_WIDTH)), _const((1, S5_WIDTH)), _const((S5_WIDTH, S5_WIDTH)),
                  _const((1, S5_WIDTH)), _const((5, 2, SUBLANES, S5_NSTATE))],
        out_specs=[_rows(tm, S5_WIDTH), _rows(tm, S5_WIDTH), _rows(tm, S5_NSTATE), _rows(tm, S5_NSTATE)],
        out_shape=[jax.ShapeDtypeStruct((t, S5_WIDTH), F32), jax.ShapeDtypeStruct((t, S5_WIDTH), F32),
                   jax.ShapeDtypeStruct((t, S5_NSTATE), F32), jax.ShapeDtypeStruct((t, S5_NSTATE), F32)],
        scratch_shapes=[pltpu.VMEM((2, SUBLANES, S5_NSTATE), F32)],
        compiler_params=_cparams(("arbitrary",)),
    )(proj, bre, bim, cre, cim, d_skip, glu_w, glu_b, coef)


def _s5_bwd(dycat, proj, y2, hre, him, bre, bim, cre, cim, d_skip, glu_w, glu_b, rcoef):
    t = proj.shape[0]
    tm = SCAN_TM
    nt = t // tm
    ng = tm // SUBLANES
    hb = tm // SUBLANES

    def body(dy_ref, u_ref, y2_ref, hre_ref, him_ref, hre_halo, him_halo, bre_ref, bim_ref, cre_ref, cim_ref, d_ref,
             w_ref, b_ref, coef_ref,
             du_ref, dbre_ref, dbim_ref, dcre_ref, dcim_ref, dlam_ref, dd_ref, dw_ref, dgb_ref,
             gre, gim, hpre, hpim, carry):
        i = pl.program_id(0)

        @pl.when(i == 0)
        def _():
            for r in (dbre_ref, dbim_ref, dcre_ref, dcim_ref, dlam_ref, dd_ref, dw_ref, dgb_ref, carry):
                r[...] = jnp.zeros_like(r)

        u = u_ref[...]
        y2 = y2_ref[...]
        dout = dy_ref[...]
        ya = _gelu(y2)
        sg = _sigmoid(_dot(ya, w_ref[...]) + b_ref[...])
        dv = dout * ya * sg * (1.0 - sg)
        dya = dout * sg + _dot_nt(dv, w_ref[...])
        dw_ref[...] += _dot_tn(ya, dv)
        dgb_ref[...] += _sum0(dv)
        dy2 = dya * _gelu_grad(y2)
        dd_ref[...] += _sum0(dy2 * u)
        hre_v = hre_ref[...]
        him_v = him_ref[...]
        dcre_ref[...] += _dot_tn(hre_v, dy2)
        dcim_ref[...] -= _dot_tn(him_v, dy2)
        gre[...] = _dot_nt(dy2, cre_ref[...])
        gim[...] = -_dot_nt(dy2, cim_ref[...])
        first = i == nt - 1
        hpre[0:SUBLANES, :] = jnp.where(first, 0.0, hre_halo[...])
        hpim[0:SUBLANES, :] = jnp.where(first, 0.0, him_halo[...])
        hpre[SUBLANES:SUBLANES + tm, :] = hre_v
        hpim[SUBLANES:SUBLANES + tm, :] = him_v
        row0 = lax.broadcasted_iota(jnp.int32, (SUBLANES, S5_NSTATE), 0) == 0

        def step(k, car):
            cr_, ci_, dlr, dli = car
            gi = ng - 1 - k
            rows = pl.ds(pl.multiple_of(gi * SUBLANES, SUBLANES), SUBLANES)
            nrows = pl.ds(pl.multiple_of(gi * SUBLANES + SUBLANES, SUBLANES), SUBLANES)
            r = gre[rows, :]
            m = gim[rows, :]
            for kk, sh in enumerate((1, 2, 4)):
                r, m = _cmul_add(r, m, coef_ref[kk, 0], coef_ref[kk, 1], pltpu.roll(r, SUBLANES - sh, 0),
                                 pltpu.roll(m, SUBLANES - sh, 0))
            r, m = _cmul_add(r, m, coef_ref[3, 0], coef_ref[3, 1], cr_, ci_)
            gre[rows, :] = r
            gim[rows, :] = m
            pr_ = hpre[rows, :]
            pm_ = hpim[rows, :]
            hr_ = jnp.where(row0, jnp.broadcast_to(pr_[SUBLANES - 1:SUBLANES, :], pr_.shape),
                            pltpu.roll(hpre[nrows, :], 1, 0))
            hm_ = jnp.where(row0, jnp.broadcast_to(pm_[SUBLANES - 1:SUBLANES, :], pm_.shape),
                            pltpu.roll(hpim[nrows, :], 1, 0))
            dlr = dlr + hr_ * r + hm_ * m
            dli = dli + hr_ * m - hm_ * r
            return (jnp.broadcast_to(r[0:1, :], r.shape), jnp.broadcast_to(m[0:1, :], m.shape), dlr, dli)

        z8 = jnp.zeros((SUBLANES, S5_NSTATE), F32)
        cr_, ci_, dlr, dli = lax.fori_loop(0, ng, step, (carry[0], carry[1], z8, z8))
        carry[0] = cr_
        carry[1] = ci_
        dlam_ref[0] += dlr
        dlam_ref[1] += dli
        g_re = gre[...]
        g_im = gim[...]
        du_ref[...] = dy2 * d_ref[...] + _dot_nt(g_re, bre_ref[...]) + _dot_nt(g_im, bim_ref[...])
        dbre_ref[...] += _dot_tn(u, g_re)
        dbim_ref[...] += _dot_tn(u, g_im)

    rev = lambda i: nt - 1 - i
    rrow = lambda n, col=0: pl.BlockSpec((tm, n), lambda i: (rev(i), col))
    halo = pl.BlockSpec((SUBLANES, S5_NSTATE), lambda i: (jnp.maximum(rev(i) * hb - 1, 0), 0))
    return pl.pallas_call(
        body, name="s5_bwd", grid=(nt,),
        in_specs=[rrow(S5_WIDTH, 512 // S5_WIDTH), rrow(S5_WIDTH, P_U // S5_WIDTH), rrow(S5_WIDTH),
                  rrow(S5_NSTATE), rrow(S5_NSTATE), halo, halo,
                  _const((S5_WIDTH, S5_NSTATE)), _const((S5_WIDTH, S5_NSTATE)), _const((S5_NSTATE, S5_WIDTH)),
                  _const((S5_NSTATE, S5_WIDTH)), _const((1, S5_WIDTH)), _const((S5_WIDTH, S5_WIDTH)),
                  _const((1, S5_WIDTH)), _const((5, 2, SUBLANES, S5_NSTATE))],
        out_specs=[rrow(S5_WIDTH), _const((S5_WIDTH, S5_NSTATE)), _const((S5_WIDTH, S5_NSTATE)),
                   _const((S5_NSTATE, S5_WIDTH)), _const((S5_NSTATE, S5_WIDTH)), _const((2, SUBLANES, S5_NSTATE)),
                   _const((1, S5_WIDTH)), _const((S5_WIDTH, S5_WIDTH)), _const((1, S5_WIDTH))],
        out_shape=[jax.ShapeDtypeStruct((t, S5_WIDTH), F32), jax.ShapeDtypeStruct((S5_WIDTH, S5_NSTATE), F32),
                   jax.ShapeDtypeStruct((S5_WIDTH, S5_NSTATE), F32), jax.ShapeDtypeStruct((S5_NSTATE, S5_WIDTH), F32),
                   jax.ShapeDtypeStruct((S5_NSTATE, S5_WIDTH), F32),
                   jax.ShapeDtypeStruct((2, SUBLANES, S5_NSTATE), F32), jax.ShapeDtypeStruct((1, S5_WIDTH), F32),
                   jax.ShapeDtypeStruct((S5_WIDTH, S5_WIDTH), F32), jax.ShapeDtypeStruct((1, S5_WIDTH), F32)],
        scratch_shapes=[pltpu.VMEM((tm, S5_NSTATE), F32), pltpu.VMEM((tm, S5_NSTATE), F32),
                        pltpu.VMEM((tm + SUBLANES, S5_NSTATE), F32), pltpu.VMEM((tm + SUBLANES, S5_NSTATE), F32),
                        pltpu.VMEM((2, SUBLANES, S5_NSTATE), F32)],
        compiler_params=_cparams(("arbitrary",)),
    )(dycat, proj, y2, hre, him, hre, him, bre, bim, cre, cim, d_skip, glu_w, glu_b, rcoef)


def _rg_gates(xc, wa, ba, wx, bx, nsp):
    r = _sigmoid(_dot(xc, wa) + ba)
    ig = _sigmoid(_dot(xc, wx) + bx)
    log_a = nsp * r
    a = jnp.exp(log_a)
    mult = jnp.sqrt(-_expm1(2.0 * log_a))
    return r, ig, a, mult


def _rg_fwd(proj, cw, cb, wa, ba, wx, bx, nsp):
    t = proj.shape[0]
    tm = SCAN_TM
    ng = tm // SUBLANES
    hb = tm // SUBLANES

    def body(x_ref, halo_ref, gt_ref, cw_ref, cb_ref, wa_ref, ba_ref, wx_ref, bx_ref, nsp_ref,
             y_ref, h_ref, xpad, abuf, carry):
        i = pl.program_id(0)

        @pl.when(i == 0)
        def _():
            carry[...] = jnp.zeros_like(carry)

        xpad[0:SUBLANES, :] = jnp.where(i > 0, halo_ref[...], 0.0)
        xpad[SUBLANES:SUBLANES + tm, :] = x_ref[...]
        xc = cb_ref[...] + _conv_taps(xpad, cw_ref[...], tm, SUBLANES - 3)
        _, ig, a, mult = _rg_gates(xc, wa_ref[...], ba_ref[...], wx_ref[...], bx_ref[...], nsp_ref[...])
        abuf[...] = a
        h_ref[...] = mult * (ig * xc)
        sub = lax.broadcasted_iota(jnp.int32, (SUBLANES, RG_WIDTH), 0)

        def step(gi, car):
            rows = pl.ds(pl.multiple_of(gi * SUBLANES, SUBLANES), SUBLANES)
            av = abuf[rows, :]
            bv = h_ref[rows, :]
            for sh in (1, 2, 4):
                m = sub >= sh
                bv = jnp.where(m, av * pltpu.roll(bv, sh, 0) + bv, bv)
                av = jnp.where(m, av * pltpu.roll(av, sh, 0), av)
            hv = bv + av * car
            h_ref[rows, :] = hv
            return jnp.broadcast_to(hv[SUBLANES - 1:SUBLANES, :], hv.shape)

        carry[...] = lax.fori_loop(0, ng, step, carry[...])
        y_ref[...] = h_ref[...] * _gelu(gt_ref[...])

    return pl.pallas_call(
        body, name="rg_fwd", grid=(t // tm,),
        in_specs=[pl.BlockSpec((tm, RG_WIDTH), lambda i: (i, P_XRG // RG_WIDTH)),
                  pl.BlockSpec((SUBLANES, RG_WIDTH), lambda i: (jnp.maximum(i * hb - 1, 0), P_XRG // RG_WIDTH)),
                  pl.BlockSpec((tm, RG_WIDTH), lambda i: (i, P_GRG // RG_WIDTH)),
                  _const((4, RG_WIDTH)), _const((1, RG_WIDTH)), _const((RG_WIDTH, RG_WIDTH)), _const((1, RG_WIDTH)),
                  _const((RG_WIDTH, RG_WIDTH)), _const((1, RG_WIDTH)), _const((1, RG_WIDTH))],
        out_specs=[_rows(tm, RG_WIDTH), _rows(tm, RG_WIDTH)],
        out_shape=[jax.ShapeDtypeStruct((t, RG_WIDTH), F32), jax.ShapeDtypeStruct((t, RG_WIDTH), F32)],
        scratch_shapes=[pltpu.VMEM((tm + SUBLANES, RG_WIDTH), F32), pltpu.VMEM((tm, RG_WIDTH), F32),
                        pltpu.VMEM((SUBLANES, RG_WIDTH), F32)],
        compiler_params=_cparams(("arbitrary",)),
    )(proj, proj, proj, cw, cb, wa, ba, wx, bx, nsp)


def _rg_bwd(dycat, proj, hs, cw, cb, wa, ba, wx, bx, nsp):
    t = proj.shape[0]
    tm = SCAN_TM
    nt = t // tm
    ng = tm // SUBLANES
    hb = tm // SUBLANES

    def body(dy_ref, x_ref, halo_ref, gt_ref, h_ref, h_halo, cw_ref, cb_ref, wa_ref, ba_ref, wx_ref, bx_ref, nsp_ref,
             dx_ref, dgt_ref, dcw_ref, dcb_ref, dwa_ref, dba_ref, dwx_ref, dbx_ref, dnsp_ref,
             xpad, abuf, gbuf, hpad, dabuf, dpad, carry, dnext):
        i = pl.program_id(0)

        @pl.when(i == 0)
        def _():
            for r in (dcw_ref, dcb_ref, dwa_ref, dba_ref, dwx_ref, dbx_ref, dnsp_ref, carry, dnext):
                r[...] = jnp.zeros_like(r)

        first = i == nt - 1
        xpad[0:SUBLANES, :] = jnp.where(first, 0.0, halo_ref[...])
        xpad[SUBLANES:SUBLANES + tm, :] = x_ref[...]
        cw_v = cw_ref[...]
        xc = cb_ref[...] + _conv_taps(xpad, cw_v, tm, SUBLANES - 3)
        nsp_v = nsp_ref[...]
        r, ig, a, mult = _rg_gates(xc, wa_ref[...], ba_ref[...], wx_ref[...], bx_ref[...], nsp_v)
        abuf[...] = a
        hv = h_ref[...]
        hpad[0:SUBLANES, :] = jnp.where(first, 0.0, h_halo[...])
        hpad[SUBLANES:SUBLANES + tm, :] = hv
        gt = gt_ref[...]
        dout = dy_ref[...]
        dgt_ref[...] = dout * hv * _gelu_grad(gt)
        gbuf[...] = dout * _gelu(gt)
        sub = lax.broadcasted_iota(jnp.int32, (SUBLANES, RG_WIDTH), 0)
        last_row = sub == SUBLANES - 1
        row0 = sub == 0

        def step(k, car):
            gi = ng - 1 - k
            rows = pl.ds(pl.multiple_of(gi * SUBLANES, SUBLANES), SUBLANES)
            nrows = pl.ds(pl.multiple_of(gi * SUBLANES + SUBLANES, SUBLANES), SUBLANES)
            av = abuf[rows, :]
            bv = gbuf[rows, :] + jnp.where(last_row, car, 0.0)
            ev = jnp.where(last_row, 0.0, pltpu.roll(av, SUBLANES - 1, 0))
            for sh in (1, 2, 4):
                m = sub < SUBLANES - sh
                bv = jnp.where(m, bv + ev * pltpu.roll(bv, SUBLANES - sh, 0), bv)
                ev = jnp.where(m, ev * pltpu.roll(ev, SUBLANES - sh, 0), 0.0)
            gbuf[rows, :] = bv
            pv = hpad[rows, :]
            hprev = jnp.where(row0, jnp.broadcast_to(pv[SUBLANES - 1:SUBLANES, :], pv.shape),
                              pltpu.roll(hpad[nrows, :], 1, 0))
            dabuf[rows, :] = bv * hprev
            return jnp.broadcast_to((av * bv)[0:1, :], bv.shape)

        carry[...] = lax.fori_loop(0, ng, step, carry[...])
        gv = gbuf[...]
        da = dabuf[...]
        ix = ig * xc
        dmult = gv * ix
        dig = gv * mult * xc
        dxc = gv * mult * ig
        dlog_a = da * a - dmult * (a * a) / mult
        dnsp_ref[...] += _sum0(dlog_a * r)
        dpr = dlog_a * nsp_v * r * (1.0 - r)
        dpi = dig * ig * (1.0 - ig)
        dxc = dxc + _dot_nt(dpr, wa_ref[...]) + _dot_nt(dpi, wx_ref[...])
        dwa_ref[...] += _dot_tn(xc, dpr)
        dwx_ref[...] += _dot_tn(xc, dpi)
        dba_ref[...] += _sum0(dpr)
        dbx_ref[...] += _sum0(dpi)
        dcb_ref[...] += _sum0(dxc)
        for k in range(4):
            dcw_ref[k:k + 1, :] += _sum0(dxc * xpad[SUBLANES - 3 + k:SUBLANES - 3 + k + tm, :])
        dpad[0:tm, :] = dxc
        dpad[tm:tm + SUBLANES, :] = dnext[...]
        dx = cw_v[0:1, :] * dpad[3:3 + tm, :]
        for k in range(1, 4):
            dx = dx + cw_v[k:k + 1, :] * dpad[3 - k:3 - k + tm, :]
        dx_ref[...] = dx
        dnext[...] = dxc[0:SUBLANES, :]

    rev = lambda i: nt - 1 - i
    rrow = lambda n, col=0: pl.BlockSpec((tm, n), lambda i: (rev(i), col))
    sq = _const((RG_WIDTH, RG_WIDTH))
    vec = _const((1, RG_WIDTH))
    return pl.pallas_call(
        body, name="rg_bwd", grid=(nt,),
        in_specs=[rrow(RG_WIDTH, 768 // RG_WIDTH), rrow(RG_WIDTH, P_XRG // RG_WIDTH),
                  pl.BlockSpec((SUBLANES, RG_WIDTH), lambda i: (jnp.maximum(rev(i) * hb - 1, 0), P_XRG // RG_WIDTH)),
                  rrow(RG_WIDTH, P_GRG // RG_WIDTH), rrow(RG_WIDTH),
                  pl.BlockSpec((SUBLANES, RG_WIDTH), lambda i: (jnp.maximum(rev(i) * hb - 1, 0), 0)),
                  _const((4, RG_WIDTH)), vec, sq, vec, sq, vec, vec],
        out_specs=[rrow(RG_WIDTH), rrow(RG_WIDTH), _const((SUBLANES, RG_WIDTH)), vec, sq, vec, sq, vec, vec],
        out_shape=[jax.ShapeDtypeStruct((t, RG_WIDTH), F32), jax.ShapeDtypeStruct((t, RG_WIDTH), F32),
                   jax.ShapeDtypeStruct((SUBLANES, RG_WIDTH), F32), jax.ShapeDtypeStruct((1, RG_WIDTH), F32),
                   jax.ShapeDtypeStruct((RG_WIDTH, RG_WIDTH), F32), jax.ShapeDtypeStruct((1, RG_WIDTH), F32),
                   jax.ShapeDtypeStruct((RG_WIDTH, RG_WIDTH), F32), jax.ShapeDtypeStruct((1, RG_WIDTH), F32),
                   jax.ShapeDtypeStruct((1, RG_WIDTH), F32)],
        scratch_shapes=[pltpu.VMEM((tm + SUBLANES, RG_WIDTH), F32), pltpu.VMEM((tm, RG_WIDTH), F32),
                        pltpu.VMEM((tm, RG_WIDTH), F32), pltpu.VMEM((tm + SUBLANES, RG_WIDTH), F32),
                        pltpu.VMEM((tm, RG_WIDTH), F32), pltpu.VMEM((tm + SUBLANES, RG_WIDTH), F32),
                        pltpu.VMEM((SUBLANES, RG_WIDTH), F32), pltpu.VMEM((SUBLANES, RG_WIDTH), F32)],
        compiler_params=_cparams(("arbitrary",)),
    )(dycat, proj, proj, proj, hs, hs, cw, cb, wa, ba, wx, bx, nsp)


def _block_diag(blocks):
    g, a, b = blocks.shape
    eye = jnp.eye(g, dtype=blocks.dtype)
    return (eye[:, None, :, None] * blocks[:, :, None, :]).reshape(g * a, g * b)


def _block_diag_extract(m, g):
    a, b = m.shape[0] // g, m.shape[1] // g
    m4 = m.reshape(g, a, g, b)
    idx = jnp.arange(g)
    return m4[idx, :, idx, :]


def _s5_prepare(lam_re, lam_im, log_step, b_re, b_im, c_re, c_im):
    step = jnp.exp(log_step)[:, None]
    mag = jnp.exp(lam_re * step)
    lbr = mag * jnp.cos(lam_im * step)
    lbi = mag * jnp.sin(lam_im * step)
    nr, ni = lbr - 1.0, lbi
    den = lam_re * lam_re + lam_im * lam_im
    cr = (nr * lam_re + ni * lam_im) / den
    ci = (ni * lam_re - nr * lam_im) / den
    bbr = cr[..., None] * b_re - ci[..., None] * b_im
    bbi = cr[..., None] * b_im + ci[..., None] * b_re
    bre = _block_diag(jnp.swapaxes(bbr, 1, 2))
    bim = _block_diag(jnp.swapaxes(bbi, 1, 2))
    cre = _block_diag(jnp.swapaxes(c_re, 1, 2))
    cim = _block_diag(jnp.swapaxes(c_im, 1, 2))
    return lbr.reshape(-1), lbi.reshape(-1), bre, bim, cre, cim


def _s5_scan_coef(lbr, lbi, reverse):
    if reverse:
        lbi = -lbi
    pr, pi = [lbr], [lbi]
    for _ in range(7):
        pr, pi = pr + [pr[-1] * lbr - pi[-1] * lbi], pi + [pr[-1] * lbi + pi[-1] * lbr]
    row = jnp.arange(SUBLANES)[:, None]
    tabs = []
    for sh in (1, 2, 4):
        keep = (row < SUBLANES - sh) if reverse else (row >= sh)
        tabs.append(jnp.stack([jnp.where(keep, pr[sh - 1][None, :], 0.0), jnp.where(keep, pi[sh - 1][None, :], 0.0)]))
    powr = jnp.stack(pr)
    powi = jnp.stack(pi)
    if reverse:
        powr, powi = powr[::-1], powi[::-1]
    tabs.append(jnp.stack([powr, powi]))
    tabs.append(jnp.zeros_like(tabs[-1]))
    return jnp.stack(tabs).astype(F32)


def _xy_peers():
    x, y, c = lax.axis_index("x"), lax.axis_index("y"), lax.axis_index("c")
    return x, y, c, [(1 - x, y), (x, 1 - y), (1 - x, 1 - y)]


def _hbm():
    return pl.BlockSpec(memory_space=pl.ANY)


def _xy_allgather(buf, *, name):
    n, w = buf.shape

    def body(x_ref, out_ref, send_sems, recv_sems, local_sem):
        x, y, c, peers = _xy_peers()
        me = 2 * x + y
        own = pltpu.make_async_copy(x_ref, out_ref.at[me], local_sem)
        own.start()
        sends = []
        for k, (px, py) in enumerate(peers):
            cp = pltpu.make_async_remote_copy(src_ref=x_ref, dst_ref=out_ref.at[me], send_sem=send_sems.at[k],
                                              recv_sem=recv_sems.at[k], device_id=(px, py, c), device_id_type=MESH)
            cp.start()
            sends.append(cp)
        for k, (px, py) in enumerate(peers):
            pltpu.make_async_remote_copy(src_ref=x_ref, dst_ref=out_ref.at[2 * px + py], send_sem=send_sems.at[k],
                                         recv_sem=recv_sems.at[k], device_id=(px, py, c),
                                         device_id_type=MESH).wait_recv()
        for cp in sends:
            cp.wait_send()
        own.wait()

    return pl.pallas_call(
        body, name=name, in_specs=[_hbm()], out_specs=_hbm(),
        out_shape=jax.ShapeDtypeStruct((4, n, w), buf.dtype),
        scratch_shapes=[pltpu.SemaphoreType.DMA((3,)), pltpu.SemaphoreType.DMA((3,)), pltpu.SemaphoreType.DMA],
    )(buf)


def _remote(src, dst, send_sem, recv_sem, dev):
    return pltpu.make_async_remote_copy(src_ref=src, dst_ref=dst, send_sem=send_sem, recv_sem=recv_sem,
                                        device_id=dev, device_id_type=MESH)


LAYER_GATHERED = (
    ("ssd_conv_w", (4, 256), 1), ("rg_conv_w", (4, LANES), 1),
    ("w_in", (1024, W_IN_PAD), 1), ("s5_glu_w", (64, 256), 0), ("w_out", (256, 1024), 0), ("xa_wq", (256, 1024), 0),
    ("xa_wk", (256, 1024), 0), ("xa_wv", (256, 1024), 0), ("xa_wo", (256, 1024), 0), ("mlp_w1", (1024, 1024), 1),
    ("mlp_w2", (1024, 1024), 0),
)
N_GATHERED = len(LAYER_GATHERED)
WAIT_GROUPS = ((0, 1, 2, 3), (4,), (5, 6, 7, 8), (9, 10))
RG_CONV_SHARD = RG_WIDTH // 4
N_GATHER_COPIES = 3 * N_GATHERED * DEPTH


def _gather_part(ref, t, pos):
    _, shp, ax = LAYER_GATHERED[t % N_GATHERED]
    idx = tuple(pl.ds(pos * shp[ax], shp[ax]) if d == ax else slice(None) for d in range(len(shp)))
    return ref.at[idx]


def _place_own(shards):
    n = len(shards)

    def body(*refs):
        srcs, dsts, sems = refs[:n], refs[n:2 * n], refs[2 * n]
        me = 2 * lax.axis_index("x") + lax.axis_index("y")
        cps = [pltpu.make_async_copy(srcs[t], _gather_part(dsts[t], t, me), sems.at[t]) for t in range(n)]
        for cp in cps:
            cp.start()
        for cp in cps:
            cp.wait()

    shapes = []
    for t, s in enumerate(shards):
        _, shp, ax = LAYER_GATHERED[t % N_GATHERED]
        shapes.append(jax.ShapeDtypeStruct(shp[:ax] + (4 * shp[ax],) + shp[ax + 1:], s.dtype))
    return pl.pallas_call(
        body, name="weights_place_own", in_specs=[_hbm()] * n, out_specs=[_hbm()] * n, out_shape=shapes,
        scratch_shapes=[pltpu.SemaphoreType.DMA((n,))],
    )(*shards)


def _gather_start(shards, lands):
    n = len(shards)
    lands = [pltpu.with_memory_space_constraint(a, pltpu.HBM) for a in lands]

    def body(*refs):
        srcs, lnds = refs[:n], refs[n:2 * n]
        send_sems, recv_sems = refs[2 * n], refs[2 * n + 1]
        token = refs[-1]
        x, y, c, peers = _xy_peers()
        me = 2 * x + y
        for t in range(n):
            for k, (px, py) in enumerate(peers):
                _remote(srcs[t], _gather_part(lnds[t], t, me), send_sems.at[k * n + t], recv_sems.at[k * n + t],
                        (px, py, c)).start()
        token[...] = jnp.zeros_like(token)

    hbm = pl.BlockSpec(memory_space=pltpu.HBM)
    sem = pl.BlockSpec(memory_space=pltpu.SEMAPHORE)
    outs = pl.pallas_call(
        body, name="weights_gather_start", in_specs=[hbm] * (2 * n),
        out_shape=(pltpu.SemaphoreType.DMA((3 * n,)), pltpu.SemaphoreType.DMA((3 * n,)),
                   *[pltpu.HBM(s.shape, s.dtype) for s in shards], *[pltpu.HBM(a.shape, a.dtype) for a in lands],
                   jax.ShapeDtypeStruct((SUBLANES, LANES), F32)),
        out_specs=(sem, sem, *[hbm] * (2 * n), pl.BlockSpec(memory_space=pltpu.VMEM)),
        input_output_aliases={i: 2 + i for i in range(2 * n)},
        compiler_params=pltpu.CompilerParams(has_side_effects=pltpu.SideEffectType.DATAFLOW_SIDE_EFFECTING),
    )(*[pltpu.with_memory_space_constraint(s, pltpu.HBM) for s in shards], *lands)
    return outs[0], outs[1], outs[2:2 + n], outs[2 + n:2 + 2 * n], outs[-1]


def _gather_wait(handle, ts, after, *, name):
    send_sems, recv_sems, src_thru, land_thru, _ = handle
    n = len(src_thru)
    m = len(ts)

    def body(*refs):
        srcs, lnds = refs[:m], refs[m:2 * m]
        ssem, rsem = refs[2 * m], refs[2 * m + 1]
        x, y, c, peers = _xy_peers()
        for i, t in enumerate(ts):
            for k, (px, py) in enumerate(peers):
                cp = _remote(srcs[i], _gather_part(lnds[i], t, 2 * px + py), ssem.at[k * n + t], rsem.at[k * n + t],
                             (px, py, c))
                cp.wait_send()
                cp.wait_recv()

    hbm = pl.BlockSpec(memory_space=pltpu.HBM)
    sem = pl.BlockSpec(memory_space=pltpu.SEMAPHORE)
    args = [src_thru[t] for t in ts] + [land_thru[t] for t in ts]
    outs = pl.pallas_call(
        body, name=name, in_specs=[hbm] * (2 * m) + [sem, sem, pl.BlockSpec(memory_space=pl.ANY)],
        out_shape=[pltpu.HBM(a.shape, a.dtype) for a in args], out_specs=[hbm] * (2 * m),
        input_output_aliases={i: i for i in range(2 * m)},
        compiler_params=pltpu.CompilerParams(has_side_effects=pltpu.SideEffectType.DATAFLOW_SIDE_EFFECTING),
    )(*args, send_sems, recv_sems, after)
    return outs[:m], outs[m:]


C_CHUNKS = 4
XY_CHUNKS = 4
EW_ROWS = 512


def _c_exchange(g):
    _, n, w = g.shape
    n2 = n // 2
    rq = n2 // C_CHUNKS

    def body(g_ref, got_ref, send_sems, recv_sems):
        x, y, c = lax.axis_index("x"), lax.axis_index("y"), lax.axis_index("c")
        cps = []
        for s in range(4):
            for q in range(C_CHUNKS):
                k = s * C_CHUNKS + q
                cp = _remote(g_ref.at[s, pl.ds((1 - c) * n2 + q * rq, rq), :], got_ref.at[s, pl.ds(q * rq, rq), :],
                             send_sems.at[k], recv_sems.at[k], (x, y, 1 - c))
                cp.start()
                cps.append(cp)
        for cp in cps:
            cp.wait_recv()
        for cp in cps:
            cp.wait_send()

    return pl.pallas_call(
        body, name="grad_c_exchange", in_specs=[_hbm()], out_specs=_hbm(),
        out_shape=jax.ShapeDtypeStruct((4, n2, w), g.dtype),
        scratch_shapes=[pltpu.SemaphoreType.DMA((4 * C_CHUNKS,)), pltpu.SemaphoreType.DMA((4 * C_CHUNKS,))],
    )(g)


XFER_DTYPE = jnp.bfloat16


def _add_own_half(g, got, c_arr):
    _, n, w = g.shape
    n2 = n // 2
    nb = n2 // EW_ROWS

    def body(c_ref, a_ref, b_ref, o_ref, t_ref):
        sm = a_ref[...] + b_ref[...]
        o_ref[...] = sm.astype(o_ref.dtype)

        @pl.when(pl.program_id(1) == nb - 1)
        def _():
            t_ref[...] = sm[:, EW_ROWS - MISC_ROWS:, :]

    grid_spec = pltpu.PrefetchScalarGridSpec(
        num_scalar_prefetch=1, grid=(4, nb),
        in_specs=[pl.BlockSpec((1, EW_ROWS, w), lambda s, i, c: (s, c[0] * nb + i, 0)),
                  pl.BlockSpec((1, EW_ROWS, w), lambda s, i, c: (s, i, 0))],
        out_specs=[pl.BlockSpec((1, EW_ROWS, w), lambda s, i, c: (s, i, 0)),
                   pl.BlockSpec((1, MISC_ROWS, w), lambda s, i, c: (s, 0, 0))])
    return pl.pallas_call(
        body, name="grad_add_halves", grid_spec=grid_spec,
        out_shape=[jax.ShapeDtypeStruct((4, n2, w), XFER_DTYPE), jax.ShapeDtypeStruct((4, MISC_ROWS, w), g.dtype)],
        compiler_params=_cparams(("arbitrary", "arbitrary")),
    )(c_arr, g, got)


def _xy_exchange(arrs):
    na = len(arrs)
    pieces = []
    for a, arr in enumerate(arrs):
        nch = XY_CHUNKS if a == 0 else 1
        rq = arr.shape[1] // nch
        pieces += [(a, pl.ds(q * rq, rq)) for q in range(nch)]
    npc = len(pieces)

    def body(*refs):
        ins, outs = refs[:na], refs[na:2 * na]
        send_sems, recv_sems, local_sems = refs[2 * na:]
        x, y, c, peers = _xy_peers()
        me = 2 * x + y
        own = []
        for j, (a, rows) in enumerate(pieces):
            cp = pltpu.make_async_copy(ins[a].at[me, rows, :], outs[a].at[me, rows, :], local_sems.at[j])
            cp.start()
            own.append(cp)
        sends = []
        for k, (px, py) in enumerate(peers):
            for j, (a, rows) in enumerate(pieces):
                cp = _remote(ins[a].at[2 * px + py, rows, :], outs[a].at[me, rows, :], send_sems.at[k * npc + j],
                             recv_sems.at[k * npc + j], (px, py, c))
                cp.start()
                sends.append(cp)
        for k, (px, py) in enumerate(peers):
            for j, (a, rows) in enumerate(pieces):
                _remote(ins[a].at[me, rows, :], outs[a].at[2 * px + py, rows, :], send_sems.at[k * npc + j],
                        recv_sems.at[k * npc + j], (px, py, c)).wait_recv()
        for cp in sends:
            cp.wait_send()
        for cp in own:
            cp.wait()

    return pl.pallas_call(
        body, name="grad_xy_exchange", in_specs=[_hbm()] * na, out_specs=[_hbm()] * na,
        out_shape=[jax.ShapeDtypeStruct(a.shape, a.dtype) for a in arrs],
        scratch_shapes=[pltpu.SemaphoreType.DMA((3 * npc,)), pltpu.SemaphoreType.DMA((3 * npc,)),
                        pltpu.SemaphoreType.DMA((npc,))],
    )(*arrs)


def _sum4_into_half(r, rt, c_arr):
    _, n2, w = r.shape
    nb = n2 // EW_ROWS

    def body(c_ref, r_ref, t_ref, o_ref):
        o_ref[...] = ((r_ref[0].astype(F32) + r_ref[1].astype(F32)) + r_ref[2].astype(F32)) + r_ref[3].astype(F32)

        @pl.when(pl.program_id(0) == nb - 1)
        def _():
            o_ref[EW_ROWS - MISC_ROWS:, :] = ((t_ref[0] + t_ref[1]) + t_ref[2]) + t_ref[3]

    grid_spec = pltpu.PrefetchScalarGridSpec(
        num_scalar_prefetch=1, grid=(nb,),
        in_specs=[pl.BlockSpec((4, EW_ROWS, w), lambda i, c: (0, i, 0)),
                  pl.BlockSpec((4, MISC_ROWS, w), lambda i, c: (0, 0, 0))],
        out_specs=pl.BlockSpec((EW_ROWS, w), lambda i, c: (c[0] * nb + i, 0)))
    return pl.pallas_call(
        body, name="grad_sum4", grid_spec=grid_spec, out_shape=jax.ShapeDtypeStruct((2 * n2, w), F32),
        compiler_params=_cparams(("arbitrary",)),
    )(c_arr, r, rt)


C_GATHER_CHUNKS = 8


def _c_allgather_halves(f):
    n, w = f.shape
    n2 = n // 2
    rq = n2 // C_GATHER_CHUNKS

    def body(f_ref, out_ref, send_sems, recv_sems):
        x, y, c = lax.axis_index("x"), lax.axis_index("y"), lax.axis_index("c")
        sends = []
        for q in range(C_GATHER_CHUNKS):
            rows = pl.ds(c * n2 + q * rq, rq)
            cp = _remote(f_ref.at[rows, :], out_ref.at[rows, :], send_sems.at[q], recv_sems.at[q], (x, y, 1 - c))
            cp.start()
            sends.append(cp)
        for q in range(C_GATHER_CHUNKS):
            rows = pl.ds((1 - c) * n2 + q * rq, rq)
            _remote(f_ref.at[rows, :], out_ref.at[rows, :], send_sems.at[q], recv_sems.at[q],
                    (x, y, 1 - c)).wait_recv()
        for cp in sends:
            cp.wait_send()

    return pl.pallas_call(
        body, name="grad_c_allgather", in_specs=[_hbm()], out_specs=_hbm(), input_output_aliases={0: 0},
        out_shape=jax.ShapeDtypeStruct((n, w), f.dtype),
        scratch_shapes=[pltpu.SemaphoreType.DMA((C_GATHER_CHUNKS,)), pltpu.SemaphoreType.DMA((C_GATHER_CHUNKS,))],
    )(f)


def _adamw(w, m, v, g, g_row0=None):
    shape = w.shape
    cols = shape[-1]
    rows = int(math.prod(shape)) // cols
    tr = 256 if rows % 256 == 0 else rows
    from_flat = g_row0 is not None
    c1 = 1.0 / (1.0 - ADAM_B1 ** ADAM_STEP)
    c2 = 1.0 / (1.0 - ADAM_B2 ** ADAM_STEP)

    def body(w_ref, m_ref, v_ref, g_ref, *outs):
        gg = g_ref[...]
        nm = ADAM_B1 * m_ref[...] + (1.0 - ADAM_B1) * gg
        nv = ADAM_B2 * v_ref[...] + (1.0 - ADAM_B2) * (gg * gg)
        if from_flat:
            outs[0][...] = gg
        d_ref, nm_ref, nv_ref = outs[-3:]
        nm_ref[...] = nm
        nv_ref[...] = nv
        d_ref[...] = -ADAM_LR * ((nm * c1) / (jnp.sqrt(nv * c2) + ADAM_EPS) + ADAM_WD * w_ref[...])

    spec = pl.BlockSpec((tr, cols), lambda i: (i, 0))
    if from_flat:
        assert cols == FLAT and g_row0 % tr == 0
        g_spec = pl.BlockSpec((tr, cols), lambda i: (g_row0 // tr + i, 0))
        g_arg = g
    else:
        g_spec = spec
        g_arg = g.reshape(rows, cols)
    n_out = 4 if from_flat else 3
    sds = jax.ShapeDtypeStruct((rows, cols), F32)
    outs = pl.pallas_call(
        body, name="adamw", grid=(rows // tr,), in_specs=[spec, spec, spec, g_spec], out_specs=[spec] * n_out,
        out_shape=[sds] * n_out, compiler_params=_cparams(("arbitrary",)),
    )(w.reshape(rows, cols), m.reshape(rows, cols), v.reshape(rows, cols), g_arg)
    outs = [o.reshape(shape) for o in outs]
    return outs if from_flat else [g] + outs


SMALL_SHARDED = (("s5_glu_w", (2, 64, 256), 1), ("ssd_conv_w", (2, 4, 256), 2), ("rg_conv_w", (2, 4, 64), 2))
REPLICATED = (
    ("ssd_conv_b", (2, 1024)), ("ssd_dt_bias", (2, 8)), ("ssd_a_log", (2, 8)), ("ssd_d", (2, 8)),
    ("ssd_norm_w", (2, 512)), ("s5_lam_re", (2, 16, 64)), ("s5_lam_im", (2, 16, 64)), ("s5_log_step", (2, 16)),
    ("s5_b_re", (2, 16, 64, 16)), ("s5_b_im", (2, 16, 64, 16)), ("s5_c_re", (2, 16, 16, 64)),
    ("s5_c_im", (2, 16, 16, 64)), ("s5_d", (2, 256)), ("s5_glu_b", (2, 256)), ("rg_conv_b", (2, 256)),
    ("rg_wa", (2, 4, 64, 64)), ("rg_ba", (2, 4, 64)), ("rg_wx", (2, 4, 64, 64)), ("rg_bx", (2, 4, 64)),
    ("rg_lambda", (2, 256)), ("ln1_g", (2, 1024)), ("ln1_b", (2, 1024)), ("ln2_g", (2, 1024)), ("ln2_b", (2, 1024)),
    ("ln3_g", (2, 1024)), ("ln3_b", (2, 1024)),
)
WEIGHT_ORDER = (
    "w_in", "w_out", "ssd_conv_w", "ssd_conv_b", "ssd_dt_bias", "ssd_a_log", "ssd_d", "ssd_norm_w", "s5_lam_re",
    "s5_lam_im", "s5_log_step", "s5_b_re", "s5_b_im", "s5_c_re", "s5_c_im", "s5_d", "s5_glu_w", "s5_glu_b",
    "rg_conv_w", "rg_conv_b", "rg_wa", "rg_ba", "rg_wx", "rg_bx", "rg_lambda", "ln1_g", "ln1_b", "xa_wq", "xa_wk",
    "xa_wv", "xa_wo", "ln2_g", "ln2_b", "mlp_w1", "mlp_w2", "ln3_g", "ln3_b",
)


def _size(shape):
    return int(math.prod(shape))


def _round_up(a, b):
    return (a + b - 1) // b * b


SMALL_ELEMS = sum(_size(s) for _, s, _ in SMALL_SHARDED)
REP_ELEMS = sum(_size(s) for _, s in REPLICATED)
REP_QROWS = _round_up(-(-REP_ELEMS // (4 * FLAT)), 8)
assert SMALL_ELEMS <= MISC_REP_ROW * FLAT and MISC_REP_ROW + REP_QROWS <= MISC_ROWS


def _pack_shards(tensors, names_shapes):
    return jnp.concatenate([tensors[n].reshape(-1) for n, *_ in names_shapes])


def _unpack(flat, names_shapes):
    out, off = {}, 0
    for n, s, *_ in names_shapes:
        out[n] = flat[off:off + _size(s)].reshape(s)
        off += _size(s)
    return out


def _split_shards(full, names_shapes):
    rows = []
    for k in range(4):
        parts = []
        for n, s, ax in names_shapes:
            w = s[ax]
            parts.append(lax.slice_in_dim(full[n], k * w, (k + 1) * w, axis=ax).reshape(-1))
        rows.append(jnp.concatenate(parts))
    return jnp.stack(rows)


def _pack_cols(w):
    pad = jnp.zeros((w.shape[0], LANES - SSD_HEADS), w.dtype)
    return jnp.concatenate([w[:, O_XBC:O_XBC + 1024], w[:, O_Z:O_Z + 512], w[:, O_U:O_U + 256],
                            w[:, O_XRG:O_XRG + 256], w[:, O_GRG:O_GRG + 256], w[:, O_DT:O_DT + 8], pad], axis=1)


def _unpack_cols(w):
    return jnp.concatenate([w[:, P_Z:P_Z + 512], w[:, P_XBC:P_XBC + 1024], w[:, P_DT:P_DT + 8],
                            w[:, P_U:P_U + 256], w[:, P_XRG:P_XRG + 256], w[:, P_GRG:P_GRG + 256]], axis=1)


def _lanes(v, width):
    return jnp.pad(v, (0, width - v.shape[0])).reshape(1, width)


def _layer_params(rep, l):
    p = {}
    p["ssd_cb"] = rep["ssd_conv_b"][l].reshape(1, -1)
    p["ssd_dtb"] = _lanes(rep["ssd_dt_bias"][l], LANES)
    p["ssd_a"] = _lanes(-jnp.exp(rep["ssd_a_log"][l]), LANES)
    p["ssd_d"] = jnp.repeat(rep["ssd_d"][l], 64).reshape(1, -1)
    p["ssd_nw"] = rep["ssd_norm_w"][l].reshape(1, -1)
    s5_args = tuple(rep[n][l] for n in ("s5_lam_re", "s5_lam_im", "s5_log_step", "s5_b_re", "s5_b_im", "s5_c_re",
                                        "s5_c_im"))
    (lbr, lbi, bre, bim, cre, cim), p["s5_vjp"] = jax.vjp(_s5_prepare, *s5_args)
    p.update(s5_bre=bre, s5_bim=bim, s5_cre=cre, s5_cim=cim)
    p["s5_coef"] = _s5_scan_coef(lbr, lbi, False)
    p["s5_rcoef"] = _s5_scan_coef(lbr, lbi, True)
    p["s5_d"] = rep["s5_d"][l].reshape(1, -1)
    p["s5_gb"] = rep["s5_glu_b"][l].reshape(1, -1)
    p["rg_cb"] = rep["rg_conv_b"][l].reshape(1, -1)
    p["rg_wa"] = _block_diag(rep["rg_wa"][l])
    p["rg_wx"] = _block_diag(rep["rg_wx"][l])
    p["rg_ba"] = rep["rg_ba"][l].reshape(1, -1)
    p["rg_bx"] = rep["rg_bx"][l].reshape(1, -1)
    p["rg_nsp"] = (-RG_C * jax.nn.softplus(-rep["rg_lambda"][l])).reshape(1, -1)
    p["rg_dnsp"] = RG_C * jax.nn.sigmoid(-rep["rg_lambda"][l])
    for n in ("ln1_g", "ln1_b", "ln2_g", "ln2_b", "ln3_g", "ln3_b"):
        p[n] = rep[n][l].reshape(1, -1)
    return p


def _layer_fwd(h, mem, p, fetch):
    s = {"h0": h}
    p.update(fetch(0, h))
    proj = _mm(h, p["w_in"], name="in_proj")
    s["proj"] = proj
    y_ssd, s["ssd_yy"], s["ssd_states"] = _ssd_fwd(proj, p["ssd_cw"], p["ssd_cb"], p["ssd_dtb"], p["ssd_a"],
                                                     p["ssd_d"], p["ssd_nw"])
    y_s5, s["s5_y2"], s["s5_hre"], s["s5_him"] = _s5_fwd(proj, p["s5_bre"], p["s5_bim"], p["s5_cre"], p["s5_cim"],
                                                         p["s5_d"], p["s5_glu_w"], p["s5_gb"], p["s5_coef"])
    y_rg, s["rg_h"] = _rg_fwd(proj, p["rg_cw"], p["rg_cb"], p["rg_wa"], p["rg_ba"], p["rg_wx"], p["rg_bx"],
                              p["rg_nsp"])
    s["ys"] = [y_ssd, y_s5, y_rg]
    p.update(fetch(1, y_rg))
    h1, s["xh1"], s["rs1"] = _outproj_ln_fwd(s["ys"], h, p["w_out"], p["ln1_g"], p["ln1_b"])
    s["h1"] = h1
    p.update(fetch(2, h1))
    kb = _mm(mem, p["xa_wk"], name="mem_proj")
    vb = _mm(mem, p["xa_wv"], name="mem_proj")
    s["kb"], s["vb"] = kb, vb
    h2, s["xh2"], s["rs2"], s["attn_o"] = _attn_ln_fwd(h1, p["xa_wq"], p["xa_wo"], kb, vb, p["ln2_g"], p["ln2_b"])
    s["h2"] = h2
    p.update(fetch(3, h2))
    h3, s["xh3"], s["rs3"], s["mlp_hdn"] = _mlp_ln_fwd(h2, p["mlp_w1"], p["mlp_w2"], p["ln3_g"], p["ln3_b"])
    return h3, s


def _layer_bwd(dh3, mem, p, s, l, gbuf):
    g = {}
    dr3, du, dh2, g["ln3_g"], g["ln3_b"] = _mlp_ln_bwd(dh3, s["xh3"], s["rs3"], p["ln3_g"], s["mlp_hdn"],
                                                        p["mlp_w1"], p["mlp_w2"])
    gbuf = _wgrad_flat(s["h2"], du, gbuf, mode="colblk", row_off=ROW_MLP_W1 + 1024 * l, name="wgrad_mlp_w1")
    gbuf = _wgrad_flat(s["mlp_hdn"], dr3, gbuf, mode="rowblk", row_off=ROW_MLP_W2 + 1024 * l, name="wgrad_mlp_w2")
    dr2, dq, dh1, dkb, dvb, g["ln2_g"], g["ln2_b"] = _attn_ln_bwd(dh2, s["xh2"], s["rs2"], p["ln2_g"], s["h1"],
                                                                   p["xa_wq"], p["xa_wo"], s["kb"], s["vb"])
    for n, a_op, g_op in (("xa_wo", s["attn_o"], dr2), ("xa_wq", s["h1"], dq), ("xa_wk", mem, dkb),
                          ("xa_wv", mem, dvb)):
        gbuf = _wgrad_flat(a_op, g_op, gbuf, mode="rows4", row_off=ROW_XA[n] + 256 * l, name="wgrad_" + n)
    dr1, dres, dycat, g["ln1_g"], g["ln1_b"] = _outproj_ln_bwd(dh1, s["xh1"], s["rs1"], p["ln1_g"], p["w_out"])
    gbuf = _wgrad_flat(s["ys"], dr1, gbuf, mode="rows4", row_off=ROW_W_OUT + 256 * l, name="wgrad_w_out")
    proj = s["proj"]
    (dxbc, dz, ddt, dcw, dcb, ddtb, da_neg, dd_l, dnw) = _ssd_bwd(
        dycat, proj, s["ssd_yy"], s["ssd_states"], p["ssd_cw"], p["ssd_cb"], p["ssd_dtb"], p["ssd_a"], p["ssd_d"],
        p["ssd_nw"])
    g["ssd_conv_w"] = dcw[0:4]
    g["ssd_conv_b"] = dcb[0]
    g["ssd_dt_bias"] = ddtb[0, :SSD_HEADS]
    g["ssd_a_log"] = da_neg[0, :SSD_HEADS] * p["ssd_a"][0, :SSD_HEADS]
    g["ssd_d"] = dd_l.reshape(SSD_HEADS, 64).sum(axis=1)
    g["ssd_norm_w"] = dnw[0]
    (du_s5, dbre, dbim, dcre, dcim, dlam, dd5, dgw, dgb) = _s5_bwd(
        dycat, proj, s["s5_y2"], s["s5_hre"], s["s5_him"], p["s5_bre"], p["s5_bim"], p["s5_cre"], p["s5_cim"],
        p["s5_d"], p["s5_glu_w"], p["s5_gb"], p["s5_rcoef"])
    dl = dlam.sum(axis=1)
    s5g = p["s5_vjp"]((dl[0], dl[1], dbre, dbim, dcre, dcim))
    for n, v in zip(("s5_lam_re", "s5_lam_im", "s5_log_step", "s5_b_re", "s5_b_im", "s5_c_re", "s5_c_im"), s5g):
        g[n] = v
    g["s5_d"] = dd5[0]
    g["s5_glu_w"] = dgw
    g["s5_glu_b"] = dgb[0]
    (dxrg, dgrg, drcw, drcb, dwa, dba, dwx, dbx, dnsp) = _rg_bwd(
        dycat, proj, s["rg_h"], p["rg_cw"], p["rg_cb"], p["rg_wa"], p["rg_ba"], p["rg_wx"], p["rg_bx"], p["rg_nsp"])
    g["rg_conv_w"] = drcw[0:4]
    g["rg_conv_b"] = drcb[0]
    g["rg_wa"] = _block_diag_extract(dwa, RG_BLOCKS)
    g["rg_wx"] = _block_diag_extract(dwx, RG_BLOCKS)
    g["rg_ba"] = dba.reshape(RG_BLOCKS, RG_BLOCK_DIM)
    g["rg_bx"] = dbx.reshape(RG_BLOCKS, RG_BLOCK_DIM)
    g["rg_lambda"] = dnsp[0] * p["rg_dnsp"]
    dproj = [dxbc, dz, du_s5, dxrg, dgrg, ddt]
    g["w_in"] = _unpack_cols(_wgrad_in(s["h0"], dproj))
    dh0 = _in_proj_bwd(dproj, p["w_in"], dres)
    for n in ("ln1_g", "ln1_b", "ln2_g", "ln2_b", "ln3_g", "ln3_b"):
        g[n] = g[n][0]
    return dh0, g, gbuf


def _local_step(h, memf, target, rep, fetch):
    params, saved = [], []
    for l in range(DEPTH):
        p = _layer_params(rep, l)
        params.append(p)
        h, s = _layer_fwd(h, memf, p, functools.partial(fetch, l))
        saved.append(s)
    loss11, dh = _loss_fwd_bwd(h, target)
    grads = [None] * DEPTH
    gbuf = None
    for l in reversed(range(DEPTH)):
        dh, grads[l], gbuf = _layer_bwd(dh, memf, params[l], saved[l], l, gbuf)
    return loss11, dh, {n: jnp.stack([grads[l][n] for l in range(DEPTH)]) for n in grads[0]}, gbuf


def kernel(x, mem, w_in, w_out, ssd_conv_w, ssd_conv_b, ssd_dt_bias, ssd_a_log, ssd_d, ssd_norm_w, s5_lam_re, s5_lam_im, s5_log_step, s5_b_re, s5_b_im, s5_c_re, s5_c_im, s5_d, s5_glu_w, s5_glu_b, rg_conv_w, rg_conv_b, rg_wa, rg_ba, rg_wx, rg_bx, rg_lambda, ln1_g, ln1_b, xa_wq, xa_wk, xa_wv, xa_wo, ln2_g, ln2_b, mlp_w1, mlp_w2, ln3_g, ln3_b, loss_target, m_w_in, m_w_out, m_ssd_conv_w, m_ssd_conv_b, m_ssd_dt_bias, m_ssd_a_log, m_ssd_d, m_ssd_norm_w, m_s5_lam_re, m_s5_lam_im, m_s5_log_step, m_s5_b_re, m_s5_b_im, m_s5_c_re, m_s5_c_im, m_s5_d, m_s5_glu_w, m_s5_glu_b, m_rg_conv_w, m_rg_conv_b, m_rg_wa, m_rg_ba, m_rg_wx, m_rg_bx, m_rg_lambda, m_ln1_g, m_ln1_b, m_xa_wq, m_xa_wk, m_xa_wv, m_xa_wo, m_ln2_g, m_ln2_b, m_mlp_w1, m_mlp_w2, m_ln3_g, m_ln3_b, v_w_in, v_w_out, v_ssd_conv_w, v_ssd_conv_b, v_ssd_dt_bias, v_ssd_a_log, v_ssd_d, v_ssd_norm_w, v_s5_lam_re, v_s5_lam_im, v_s5_log_step, v_s5_b_re, v_s5_b_im, v_s5_c_re, v_s5_c_im, v_s5_d, v_s5_glu_w, v_s5_glu_b, v_rg_conv_w, v_rg_conv_b, v_rg_wa, v_rg_ba, v_rg_wx, v_rg_bx, v_rg_lambda, v_ln1_g, v_ln1_b, v_xa_wq, v_xa_wk, v_xa_wv, v_xa_wo, v_ln2_g, v_ln2_b, v_mlp_w1, v_mlp_w2, v_ln3_g, v_ln3_b):
    args = dict(locals())
    weights = {n: args[n] for n in WEIGHT_ORDER}
    mom_m = {n: args["m_" + n] for n in WEIGHT_ORDER}
    mom_v = {n: args["v_" + n] for n in WEIGHT_ORDER}

    shards = []
    for l in range(DEPTH):
        for n, shp, ax in LAYER_GATHERED:
            w = weights[n][l]
            if w.shape[1] != shp[1]:
                w = jnp.pad(w, ((0, 0), (0, shp[1] - w.shape[1])))
            if n not in ("ssd_conv_w", "rg_conv_w"):
                w = w.astype(MXU_DTYPE)
            shards.append(w)
    handle = _gather_start(shards, _place_own(shards))

    def unpad(arr, padded, width):
        return jnp.concatenate([arr[:, padded * k:padded * k + width] for k in range(4)], axis=1)

    def fetch(l, grp, after):
        ts = [l * N_GATHERED + j for j in WAIT_GROUPS[grp]]
        _, landed = _gather_wait(handle, ts, after, name="weights_gather_wait_%d_%d" % (l, grp))
        out = {}
        for t, arr in zip(ts, landed):
            n = LAYER_GATHERED[t % N_GATHERED][0]
            if n == "w_in":
                arr = _pack_cols(unpad(arr, W_IN_PAD, W_IN_SHARD))
            elif n == "rg_conv_w":
                arr = unpad(arr, LANES, RG_CONV_SHARD)
            out[{"ssd_conv_w": "ssd_cw", "rg_conv_w": "rg_cw"}.get(n, n)] = arr
        return out

    rep = {n: weights[n] for n, _ in REPLICATED}

    loss11, dx, gsmall, gbuf = _local_step(x[0], mem[0], loss_target[0], rep, fetch)
    grad_x = dx[None]
    loss = lax.psum(loss11[0, 0], ("x", "y", "c"))

    gw = gsmall["w_in"].reshape(DEPTH, D_MODEL, 4, W_IN_SHARD)
    gw = jnp.pad(gw, ((0, 0), (0, 0), (0, 0), (0, W_IN_PAD - W_IN_SHARD)))
    w_in_blk = jnp.transpose(gw, (2, 0, 1, 3)).reshape(4, DEPTH * W_IN_PAD, FLAT)
    small_q = _split_shards(gsmall, SMALL_SHARDED)
    rep_q = jnp.pad(_pack_shards(gsmall, REPLICATED), (0, 4 * REP_QROWS * FLAT - REP_ELEMS)).reshape(4, -1)
    misc = jnp.concatenate(
        [jnp.pad(small_q, ((0, 0), (0, MISC_REP_ROW * FLAT - SMALL_ELEMS))), rep_q,
         jnp.zeros((4, (MISC_ROWS - MISC_REP_ROW - REP_QROWS) * FLAT), F32)], axis=1).reshape(4, MISC_ROWS, FLAT)
    gbuf = lax.dynamic_update_slice(gbuf, w_in_blk, (0, ROW_W_IN, 0))
    gbuf = lax.dynamic_update_slice(gbuf, misc, (0, ROW_MISC, 0))
    c_arr = lax.axis_index("c").astype(jnp.int32).reshape(1)
    chip_sum, chip_tail = _add_own_half(gbuf, _c_exchange(gbuf), c_arr)
    got_sum, got_tail = _xy_exchange([chip_sum, chip_tail])
    reduced = _c_allgather_halves(_sum4_into_half(got_sum, got_tail, c_arr))
    misc_red = reduced[ROW_MISC:]
    rep_all = _xy_allgather(misc_red[MISC_REP_ROW:MISC_REP_ROW + REP_QROWS], name="small_grads_allgather")
    g_red = {**_unpack(misc_red[:MISC_REP_ROW].reshape(-1), SMALL_SHARDED),
             **_unpack(rep_all.reshape(-1), REPLICATED)}
    g_red["w_in"] = reduced[ROW_W_IN:ROW_W_IN + DEPTH * W_IN_PAD].reshape(DEPTH, D_MODEL, W_IN_PAD)[:, :, :W_IN_SHARD]

    flat_rows = {"mlp_w1": ROW_MLP_W1, "mlp_w2": ROW_MLP_W2, "w_out": ROW_W_OUT, **ROW_XA}
    res = {}
    for n in WEIGHT_ORDER:
        if n in flat_rows:
            res[n] = _adamw(weights[n], mom_m[n], mom_v[n], reduced, g_row0=flat_rows[n])
        else:
            res[n] = _adamw(weights[n], mom_m[n], mom_v[n], g_red[n])
    return (loss, grad_x, *[res[n][0] for n in WEIGHT_ORDER], *[res[n][1] for n in WEIGHT_ORDER],
            *[res[n][2] for n in WEIGHT_ORDER], *[res[n][3] for n in WEIGHT_ORDER])
```

```python
import functools
import math

import jax
import jax.numpy as jnp
from jax import lax
from jax.experimental import pallas as pl
from jax.experimental.pallas import tpu as pltpu

F32 = jnp.float32
MXU_DTYPE = jnp.bfloat16

D_MODEL = 1024
DEPTH = 2
MEM_LEN = 256
SSD_WIDTH = 512
SSD_HEADS = 8
SSD_STATE = 128
SSD_CHUNK = 128
SSD_XBC = 1024
S5_WIDTH = 256
S5_GROUPS = 16
S5_GROUP_CH = 16
S5_STATE = 64
S5_NSTATE = S5_GROUPS * S5_STATE
RG_WIDTH = 256
RG_BLOCKS = 4
RG_BLOCK_DIM = 64
RG_C = 8.0
XA_HEADS = 4
XA_HEAD_DIM = 256
D_FF = 4096
D_IN = 2312
ALPHA = (2.0 * DEPTH) ** 0.25
LN_EPS = 1e-5
ADAM_LR = 0.001
ADAM_B1 = 0.9
ADAM_B2 = 0.999
ADAM_EPS = 1e-08
ADAM_WD = 0.01
ADAM_STEP = 10

P_XBC, P_Z, P_U, P_XRG, P_GRG, P_DT = 0, 1024, 1536, 1792, 2048, 2304
D_PACK = 2432
O_Z, O_XBC, O_DT, O_U, O_XRG, O_GRG = 0, 512, 1536, 1544, 1800, 2056

LANES = 128
SUBLANES = 8
VMEM_LIMIT = 52 * 1024 * 1024
TM = 512
SSD_TM = 256
SCAN_TM = 512
FLAT = 1024

MESH = pl.DeviceIdType.MESH


def _cparams(sem):
    return pltpu.CompilerParams(dimension_semantics=sem, vmem_limit_bytes=VMEM_LIMIT)


def _dot(a, b):
    return jnp.dot(a.astype(MXU_DTYPE), b.astype(MXU_DTYPE), preferred_element_type=F32)


def _dot_nt(a, b):
    return lax.dot_general(a.astype(MXU_DTYPE), b.astype(MXU_DTYPE), (((1,), (1,)), ((), ())),
                           preferred_element_type=F32)


def _dot_tn(a, b):
    return lax.dot_general(a.astype(MXU_DTYPE), b.astype(MXU_DTYPE), (((0,), (0,)), ((), ())),
                           preferred_element_type=F32)


def _dot_f32(a, b):
    return jnp.dot(a, b, precision=lax.Precision.HIGHEST, preferred_element_type=F32)


def _dot_f32_tn(a, b):
    return lax.dot_general(a, b, (((0,), (0,)), ((), ())), precision=lax.Precision.HIGHEST,
                           preferred_element_type=F32)


def _sigmoid(x):
    return 1.0 / (1.0 + jnp.exp(-x))


def _softplus(x):
    return jnp.maximum(x, 0.0) + jnp.log(1.0 + jnp.exp(-jnp.abs(x)))


_GELU_K = math.sqrt(2.0 / math.pi)


def _gelu(x):
    return 0.5 * x * (1.0 + jnp.tanh(_GELU_K * (x + 0.044715 * x * x * x)))


def _gelu_grad(x):
    t = jnp.tanh(_GELU_K * (x + 0.044715 * x * x * x))
    return 0.5 * (1.0 + t) + 0.5 * x * (1.0 - t * t) * _GELU_K * (1.0 + 3.0 * 0.044715 * x * x)


def _expm1(x):
    small = x * (1.0 + x * (0.5 + x * (1.0 / 6.0 + x * (1.0 / 24.0))))
    return jnp.where(jnp.abs(x) < 0.05, small, jnp.exp(x) - 1.0)


def _sum0(x):
    return jnp.sum(x, axis=0, keepdims=True)


def _ln_fwd(r, g, b):
    mu = jnp.mean(r, axis=-1, keepdims=True)
    xc = r - mu
    var = jnp.mean(xc * xc, axis=-1, keepdims=True)
    rstd = lax.rsqrt(var + LN_EPS)
    xhat = xc * rstd
    return xhat * g + b, xhat, rstd


def _ln_bwd(dout, xhat, rstd, g):
    dxh = dout * g
    m1 = jnp.mean(dxh, axis=-1, keepdims=True)
    m2 = jnp.mean(dxh * xhat, axis=-1, keepdims=True)
    return rstd * (dxh - m1 - xhat * m2)


def _rows(tm, n, col=0):
    return pl.BlockSpec((tm, n), lambda i: (i, col))


def _const(shape):
    nd = len(shape)
    return pl.BlockSpec(shape, lambda i: (0,) * nd)


def _mm(a, w, *, name):
    t, k = a.shape
    n = w.shape[1]
    tm = min(TM, t)

    def body(a_ref, w_ref, o_ref):
        o_ref[...] = _dot(a_ref[...], w_ref[...])

    return pl.pallas_call(
        body, name=name, grid=(t // tm,), in_specs=[_rows(tm, k), _const(w.shape)], out_specs=_rows(tm, n),
        out_shape=jax.ShapeDtypeStruct((t, n), F32), compiler_params=_cparams(("arbitrary",)),
    )(a, w)


DPROJ_PIECES = ((P_XBC, 1024), (P_Z, 512), (P_U, 256), (P_XRG, 256), (P_GRG, 256), (P_DT, LANES))


def _in_proj_bwd(pieces, w, dres):
    t = dres.shape[0]
    npc = len(pieces)

    def body(*refs):
        w_ref, r_ref, o_ref = refs[npc:]
        acc = r_ref[...]
        for p_ref, (off, k) in zip(refs[:npc], DPROJ_PIECES):
            acc = acc + _dot_nt(p_ref[...], w_ref[:, off:off + k])
        o_ref[...] = acc

    return pl.pallas_call(
        body, name="in_proj_bwd", grid=(t // TM,),
        in_specs=[_rows(TM, k) for _, k in DPROJ_PIECES] + [_const(w.shape), _rows(TM, D_MODEL)],
        out_specs=_rows(TM, D_MODEL), out_shape=jax.ShapeDtypeStruct((t, D_MODEL), F32),
        compiler_params=_cparams(("arbitrary",)),
    )(*pieces, w, dres)


def _wgrad_in(h0, pieces):
    t = h0.shape[0]
    npc = len(pieces)

    def body(*refs):
        h_ref, o_ref = refs[npc], refs[npc + 1]
        @pl.when(pl.program_id(0) == 0)
        def _():
            o_ref[...] = jnp.zeros_like(o_ref)

        hb = h_ref[...].astype(MXU_DTYPE)
        for p_ref, (off, k) in zip(refs[:npc], DPROJ_PIECES):
            o_ref[:, off:off + k] += _dot_tn(hb, p_ref[...])

    return pl.pallas_call(
        body, name="wgrad_in", grid=(t // TM,),
        in_specs=[_rows(TM, k) for _, k in DPROJ_PIECES] + [_rows(TM, D_MODEL)],
        out_specs=_const((D_MODEL, D_PACK)), out_shape=jax.ShapeDtypeStruct((D_MODEL, D_PACK), F32),
        compiler_params=_cparams(("arbitrary",)),
    )(*pieces, h0)


G_ROWS = 8192
ROW_MLP_W1 = 0
ROW_MLP_W2 = 2048
ROW_W_IN = 4096
ROW_W_OUT = 5376
ROW_XA = {"xa_wq": 5888, "xa_wk": 6400, "xa_wv": 6912, "xa_wo": 7424}
ROW_MISC = 7936
MISC_ROWS = G_ROWS - ROW_MISC
MISC_REP_ROW = 40
W_IN_SHARD = 578
W_IN_PAD = 640


def _wgrad_flat(a, g, buf, *, mode, row_off, name):
    pieces = list(a) if isinstance(a, (list, tuple)) else [a]
    t = g.shape[0]
    tt = min(1024, t)
    ns = t // tt
    blk = D_MODEL

    def accumulate(o_ref, parts, s):
        @pl.when(s == 0)
        def _():
            o_ref[...] = jnp.zeros_like(o_ref)

        for q, v in parts:
            o_ref[q] += v

    if mode == "rows4":
        grid = (ns,)
        in_specs = [pl.BlockSpec((tt, p.shape[1]), lambda s: (s, 0)) for p in pieces]
        in_specs.append(pl.BlockSpec((tt, blk), lambda s: (s, 0)))
        out_spec = pl.BlockSpec((4, 256, FLAT), lambda s: (0, row_off // 256, 0))
        sem = ("arbitrary",)
        npc = len(pieces)

        def body(*refs):
            g_v = refs[npc][...]
            parts, q0 = [], 0
            for p_ref in refs[:npc]:
                full = _dot_tn(p_ref[...], g_v)
                nq = full.shape[0] // 256
                parts += [(q0 + q, full[q * 256:(q + 1) * 256]) for q in range(nq)]
                q0 += nq
            accumulate(refs[-1], parts, pl.program_id(0))
    else:
        grid = (2, ns)
        if mode == "rowblk":
            in_specs = [pl.BlockSpec((tt, 2 * blk), lambda q, s: (s, q)), pl.BlockSpec((tt, blk), lambda q, s: (s, 0))]
        else:
            in_specs = [pl.BlockSpec((tt, blk), lambda q, s: (s, 0)), pl.BlockSpec((tt, 2 * blk), lambda q, s: (s, q))]
        out_spec = pl.BlockSpec((2, blk, FLAT), lambda q, s: (q, row_off // blk, 0))
        sem = ("arbitrary", "arbitrary")

        def body(a_ref, g_ref, *rest):
            full = _dot_tn(a_ref[...], g_ref[...])
            if mode == "rowblk":
                parts = [(0, full[:blk]), (1, full[blk:])]
            else:
                parts = [(0, full[:, :blk]), (1, full[:, blk:])]
            accumulate(rest[-1], parts, pl.program_id(1))

    args = pieces + [g]
    aliases = {}
    if buf is not None:
        in_specs.append(pl.BlockSpec(memory_space=pl.ANY))
        args.append(buf)
        aliases = {len(args) - 1: 0}
    return pl.pallas_call(
        body, name=name, grid=grid, in_specs=in_specs, out_specs=out_spec,
        out_shape=jax.ShapeDtypeStruct((4, G_ROWS, FLAT), F32), input_output_aliases=aliases,
        compiler_params=_cparams(sem),
    )(*args)


def _outproj_ln_fwd(ys, h, w, g, b):
    t = h.shape[0]
    npc = len(ys)

    def body(*refs):
        h_ref, w_ref, g_ref, b_ref, hn_ref, xh_ref, rs_ref = refs[npc:]
        r = ALPHA * h_ref[...]
        off = 0
        for y_ref in refs[:npc]:
            k = y_ref.shape[1]
            r = r + _dot(y_ref[...], w_ref[off:off + k, :])
            off += k
        out, xhat, rstd = _ln_fwd(r, g_ref[...], b_ref[...])
        hn_ref[...] = out
        xh_ref[...] = xhat
        rs_ref[...] = rstd

    return pl.pallas_call(
        body, name="outproj_ln_fwd", grid=(t // TM,),
        in_specs=[_rows(TM, y.shape[1]) for y in ys] + [_rows(TM, D_MODEL), _const((D_MODEL, D_MODEL)),
                                                        _const((1, D_MODEL)), _const((1, D_MODEL))],
        out_specs=[_rows(TM, D_MODEL), _rows(TM, D_MODEL), _rows(TM, 1)],
        out_shape=[jax.ShapeDtypeStruct((t, D_MODEL), F32), jax.ShapeDtypeStruct((t, D_MODEL), F32),
                   jax.ShapeDtypeStruct((t, 1), F32)],
        compiler_params=_cparams(("arbitrary",)),
    )(*ys, h, w, g, b)


def _attn_probs(q, kb, hh):
    sl = slice(hh * XA_HEAD_DIM, (hh + 1) * XA_HEAD_DIM)
    s = _dot_nt(q[:, sl], kb[:, sl]) * (1.0 / math.sqrt(XA_HEAD_DIM))
    m = jnp.max(s, axis=-1, keepdims=True)
    e = jnp.exp(s - m)
    return e / jnp.sum(e, axis=-1, keepdims=True)


def _attn_ln_fwd(h1, wq, wo, kb, vb, g, b):
    t = h1.shape[0]

    def body(h_ref, wq_ref, wo_ref, k_ref, v_ref, g_ref, b_ref, hn_ref, xh_ref, rs_ref, o_ref):
        h = h_ref[...]
        q = _dot(h, wq_ref[...])
        kb_ = k_ref[...]
        vb_ = v_ref[...]
        for hh in range(XA_HEADS):
            sl = slice(hh * XA_HEAD_DIM, (hh + 1) * XA_HEAD_DIM)
            p = _attn_probs(q, kb_, hh)
            o_ref[:, sl] = _dot(p, vb_[:, sl]).astype(o_ref.dtype)
        r = ALPHA * h + _dot(o_ref[...], wo_ref[...])
        out, xhat, rstd = _ln_fwd(r, g_ref[...], b_ref[...])
        hn_ref[...] = out
        xh_ref[...] = xhat
        rs_ref[...] = rstd

    return pl.pallas_call(
        body, name="attn_ln_fwd", grid=(t // TM,),
        in_specs=[_rows(TM, D_MODEL), _const((D_MODEL, D_MODEL)), _const((D_MODEL, D_MODEL)),
                  _const((MEM_LEN, D_MODEL)), _const((MEM_LEN, D_MODEL)), _const((1, D_MODEL)), _const((1, D_MODEL))],
        out_specs=[_rows(TM, D_MODEL), _rows(TM, D_MODEL), _rows(TM, 1), _rows(TM, D_MODEL)],
        out_shape=[jax.ShapeDtypeStruct((t, D_MODEL), F32), jax.ShapeDtypeStruct((t, D_MODEL), F32),
                   jax.ShapeDtypeStruct((t, 1), F32), jax.ShapeDtypeStruct((t, D_MODEL), MXU_DTYPE)],
        compiler_params=_cparams(("arbitrary",)),
    )(h1, wq, wo, kb, vb, g, b)


def _attn_ln_bwd(dh2, xhat, rstd, g, h1, wq, wo, kb, vb):
    t = h1.shape[0]

    def body(dh_ref, xh_ref, rs_ref, g_ref, h_ref, wq_ref, wo_ref, k_ref, v_ref,
             dr_ref, dq_ref, dh1_ref, dk_ref, dv_ref, dg_ref, db_ref):
        i = pl.program_id(0)

        @pl.when(i == 0)
        def _():
            dk_ref[...] = jnp.zeros_like(dk_ref)
            dv_ref[...] = jnp.zeros_like(dv_ref)
            dg_ref[...] = jnp.zeros_like(dg_ref)
            db_ref[...] = jnp.zeros_like(db_ref)

        dout = dh_ref[...]
        xh = xh_ref[...]
        dg_ref[...] += _sum0(dout * xh)
        db_ref[...] += _sum0(dout)
        dr = _ln_bwd(dout, xh, rs_ref[...], g_ref[...])
        dr_ref[...] = dr.astype(dr_ref.dtype)
        do = _dot_nt(dr, wo_ref[...])
        h = h_ref[...]
        q = _dot(h, wq_ref[...])
        kb_ = k_ref[...]
        vb_ = v_ref[...]
        scale = 1.0 / math.sqrt(XA_HEAD_DIM)
        for hh in range(XA_HEADS):
            sl = slice(hh * XA_HEAD_DIM, (hh + 1) * XA_HEAD_DIM)
            p = _attn_probs(q, kb_, hh)
            do_h = do[:, sl]
            dp = _dot_nt(do_h, vb_[:, sl])
            ds = p * (dp - jnp.sum(dp * p, axis=-1, keepdims=True)) * scale
            dq_ref[:, sl] = _dot(ds, kb_[:, sl]).astype(dq_ref.dtype)
            dk_ref[:, sl] += _dot_tn(ds, q[:, sl])
            dv_ref[:, sl] += _dot_tn(p, do_h)
        dh1_ref[...] = ALPHA * dr + _dot_nt(dq_ref[...], wq_ref[...])

    return pl.pallas_call(
        body, name="attn_ln_bwd", grid=(t // TM,),
        in_specs=[_rows(TM, D_MODEL), _rows(TM, D_MODEL), _rows(TM, 1), _const((1, D_MODEL)), _rows(TM, D_MODEL),
                  _const((D_MODEL, D_MODEL)), _const((D_MODEL, D_MODEL)), _const((MEM_LEN, D_MODEL)),
                  _const((MEM_LEN, D_MODEL))],
        out_specs=[_rows(TM, D_MODEL), _rows(TM, D_MODEL), _rows(TM, D_MODEL), _const((MEM_LEN, D_MODEL)),
                   _const((MEM_LEN, D_MODEL)), _const((1, D_MODEL)), _const((1, D_MODEL))],
        out_shape=[jax.ShapeDtypeStruct((t, D_MODEL), MXU_DTYPE), jax.ShapeDtypeStruct((t, D_MODEL), MXU_DTYPE),
                   jax.ShapeDtypeStruct((t, D_MODEL), F32), jax.ShapeDtypeStruct((MEM_LEN, D_MODEL), F32),
                   jax.ShapeDtypeStruct((MEM_LEN, D_MODEL), F32), jax.ShapeDtypeStruct((1, D_MODEL), F32),
                   jax.ShapeDtypeStruct((1, D_MODEL), F32)],
        compiler_params=_cparams(("arbitrary",)),
    )(dh2, xhat, rstd, g, h1, wq, wo, kb, vb)


FF_CHUNK = 1024
N_FF = D_FF // FF_CHUNK


def _load_resident(pairs, sems):
    copies = [pltpu.make_async_copy(src, dst, sems.at[k]) for k, (src, dst) in enumerate(pairs)]
    for cp in copies:
        cp.start()
    for cp in copies:
        cp.wait()


def _mlp_ln_fwd(h2, w1, w2, g, b):
    t = h2.shape[0]

    def body(h_ref, w1_hbm, w2_hbm, g_ref, b_ref, hn_ref, xh_ref, rs_ref, hd_ref, w1_v, w2_v, acc_ref, sems):
        @pl.when(pl.program_id(0) == 0)
        def _():
            _load_resident([(w1_hbm, w1_v), (w2_hbm, w2_v)], sems)

        h = h_ref[...]
        hb = h.astype(MXU_DTYPE)
        acc_ref[...] = ALPHA * h
        for j in range(N_FF):
            sl = slice(j * FF_CHUNK, (j + 1) * FF_CHUNK)
            u = _dot(hb, w1_v[:, sl])
            hd = jnp.square(jnp.maximum(u, 0.0)).astype(MXU_DTYPE)
            hd_ref[:, sl] = hd
            acc_ref[...] += _dot(hd, w2_v[sl, :])
        out, xhat, rstd = _ln_fwd(acc_ref[...], g_ref[...], b_ref[...])
        hn_ref[...] = out
        xh_ref[...] = xhat
        rs_ref[...] = rstd

    return pl.pallas_call(
        body, name="mlp_ln_fwd", grid=(t // TM,),
        in_specs=[_rows(TM, D_MODEL), _hbm(), _hbm(), _const((1, D_MODEL)), _const((1, D_MODEL))],
        out_specs=[_rows(TM, D_MODEL), _rows(TM, D_MODEL), _rows(TM, 1), _rows(TM, D_FF)],
        out_shape=[jax.ShapeDtypeStruct((t, D_MODEL), F32), jax.ShapeDtypeStruct((t, D_MODEL), F32),
                   jax.ShapeDtypeStruct((t, 1), F32), jax.ShapeDtypeStruct((t, D_FF), MXU_DTYPE)],
        scratch_shapes=[pltpu.VMEM((D_MODEL, D_FF), MXU_DTYPE), pltpu.VMEM((D_FF, D_MODEL), MXU_DTYPE),
                        pltpu.VMEM((TM, D_MODEL), F32), pltpu.SemaphoreType.DMA((2,))],
        compiler_params=_cparams(("arbitrary",)),
    )(h2, w1, w2, g, b)


def _mlp_ln_bwd(dh3, xhat, rstd, g, hdn, w1, w2):
    t = dh3.shape[0]

    def body(dh_ref, xh_ref, rs_ref, g_ref, hd_ref, w1_hbm, w2_hbm,
             dr_ref, du_ref, dh2_ref, dg_ref, db_ref, w1_v, w2_v, acc_ref, sems):
        @pl.when(pl.program_id(0) == 0)
        def _():
            _load_resident([(w1_hbm, w1_v), (w2_hbm, w2_v)], sems)
            dg_ref[...] = jnp.zeros_like(dg_ref)
            db_ref[...] = jnp.zeros_like(db_ref)

        dout = dh_ref[...]
        xh = xh_ref[...]
        dg_ref[...] += _sum0(dout * xh)
        db_ref[...] += _sum0(dout)
        dr = _ln_bwd(dout, xh, rs_ref[...], g_ref[...])
        drb = dr.astype(MXU_DTYPE)
        dr_ref[...] = drb
        acc_ref[...] = ALPHA * dr
        for j in range(N_FF):
            sl = slice(j * FF_CHUNK, (j + 1) * FF_CHUNK)
            dhd = _dot_nt(drb, w2_v[sl, :])
            du = (dhd * (2.0 * jnp.sqrt(hd_ref[:, sl].astype(F32)))).astype(MXU_DTYPE)
            du_ref[:, sl] = du
            acc_ref[...] += _dot_nt(du, w1_v[:, sl])
        dh2_ref[...] = acc_ref[...]

    tm = TM // 2
    return pl.pallas_call(
        body, name="mlp_ln_bwd", grid=(t // tm,),
        in_specs=[_rows(tm, D_MODEL), _rows(tm, D_MODEL), _rows(tm, 1), _const((1, D_MODEL)), _rows(tm, D_FF),
                  _hbm(), _hbm()],
        out_specs=[_rows(tm, D_MODEL), _rows(tm, D_FF), _rows(tm, D_MODEL), _const((1, D_MODEL)),
                   _const((1, D_MODEL))],
        out_shape=[jax.ShapeDtypeStruct((t, D_MODEL), MXU_DTYPE), jax.ShapeDtypeStruct((t, D_FF), MXU_DTYPE),
                   jax.ShapeDtypeStruct((t, D_MODEL), F32), jax.ShapeDtypeStruct((1, D_MODEL), F32),
                   jax.ShapeDtypeStruct((1, D_MODEL), F32)],
        scratch_shapes=[pltpu.VMEM((D_MODEL, D_FF), MXU_DTYPE), pltpu.VMEM((D_FF, D_MODEL), MXU_DTYPE),
                        pltpu.VMEM((tm, D_MODEL), F32), pltpu.SemaphoreType.DMA((2,))],
        compiler_params=_cparams(("arbitrary",)),
    )(dh3, xhat, rstd, g, hdn, w1, w2)


def _outproj_ln_bwd(dh1, xhat, rstd, g, w):
    t = dh1.shape[0]

    def body(dh_ref, xh_ref, rs_ref, g_ref, w_ref, dr_ref, res_ref, dy_ref, dg_ref, db_ref):
        i = pl.program_id(0)

        @pl.when(i == 0)
        def _():
            dg_ref[...] = jnp.zeros_like(dg_ref)
            db_ref[...] = jnp.zeros_like(db_ref)

        dout = dh_ref[...]
        xh = xh_ref[...]
        dg_ref[...] += _sum0(dout * xh)
        db_ref[...] += _sum0(dout)
        dr = _ln_bwd(dout, xh, rs_ref[...], g_ref[...])
        dr_ref[...] = dr.astype(dr_ref.dtype)
        res_ref[...] = ALPHA * dr
        dy_ref[...] = _dot_nt(dr, w_ref[...])

    return pl.pallas_call(
        body, name="outproj_ln_bwd", grid=(t // TM,),
        in_specs=[_rows(TM, D_MODEL), _rows(TM, D_MODEL), _rows(TM, 1), _const((1, D_MODEL)),
                  _const((D_MODEL, D_MODEL))],
        out_specs=[_rows(TM, D_MODEL), _rows(TM, D_MODEL), _rows(TM, D_MODEL), _const((1, D_MODEL)),
                   _const((1, D_MODEL))],
        out_shape=[jax.ShapeDtypeStruct((t, D_MODEL), MXU_DTYPE), jax.ShapeDtypeStruct((t, D_MODEL), F32),
                   jax.ShapeDtypeStruct((t, D_MODEL), F32), jax.ShapeDtypeStruct((1, D_MODEL), F32),
                   jax.ShapeDtypeStruct((1, D_MODEL), F32)],
        compiler_params=_cparams(("arbitrary",)),
    )(dh1, xhat, rstd, g, w)


def _loss_fwd_bwd(h, target):
    t = h.shape[0]

    def body(h_ref, t_ref, l_ref, dh_ref):
        i = pl.program_id(0)

        @pl.when(i == 0)
        def _():
            l_ref[...] = jnp.zeros_like(l_ref)

        e = h_ref[...] - t_ref[...]
        dh_ref[...] = e * (1.0 / D_MODEL)
        per_tok = jnp.mean(e * e, axis=-1, keepdims=True)
        l_ref[...] += 0.5 * jnp.sum(per_tok, axis=0, keepdims=True)

    return pl.pallas_call(
        body, name="loss_fwd_bwd", grid=(t // TM,),
        in_specs=[_rows(TM, D_MODEL), _rows(TM, D_MODEL)],
        out_specs=[_const((1, 1)), _rows(TM, D_MODEL)],
        out_shape=[jax.ShapeDtypeStruct((1, 1), F32), jax.ShapeDtypeStruct((t, D_MODEL), F32)],
        compiler_params=_cparams(("arbitrary",)),
    )(h, target)


def _pick_col(x, idx):
    lane = lax.broadcasted_iota(jnp.int32, x.shape, 1)
    return jnp.sum(jnp.where(lane == idx, x, 0.0), axis=1, keepdims=True)


def _pick_row(x, idx):
    sub = lax.broadcasted_iota(jnp.int32, x.shape, 0)
    return jnp.sum(jnp.where(sub == idx, x, 0.0), axis=0, keepdims=True)


def _conv_taps(pad_ref, w, tm, base):
    acc = w[0:1, :] * pad_ref[base:base + tm, :]
    for k in range(1, 4):
        acc = acc + w[k:k + 1, :] * pad_ref[base + k:base + k + tm, :]
    return acc


def _ssd_chunk_common(adt_c, tri):
    cs = _dot_f32(tri, adt_c)
    return cs, cs.T, jnp.exp(cs)


def _ssd_head_terms(cs, cst, ecs, dt_c, h, tri):
    cs_col = _pick_col(cs, h)
    cs_row = _pick_row(cst, h)
    dt_col = _pick_col(dt_c, h)
    cs_last = cs_col[SSD_CHUNK - 1:SSD_CHUNK, :]
    lmat = jnp.exp(jnp.where(tri > 0.0, cs_col - cs_row, -1e30))
    ecs_col = _pick_col(ecs, h)
    decay_col = jnp.exp(cs_last - cs_col)
    return cs_col, dt_col, cs_last, lmat, ecs_col, decay_col


def _ssd_fwd(proj, cw, cb, dtb, a_neg, d_lanes, nw):
    t = proj.shape[0]
    tm = SSD_TM
    nt = t // tm
    ncq = tm // SSD_CHUNK
    hb = tm // SUBLANES

    def body(xbc_ref, halo_ref, z_ref, dt_ref, cw_ref, cb_ref, dtb_ref, a_ref, d_ref, nw_ref,
             y_ref, yy_ref, st_ref, xpad, xact, state):
        i = pl.program_id(0)

        @pl.when(i == 0)
        def _():
            state[...] = jnp.zeros_like(state)

        xpad[0:SUBLANES, :] = jnp.where(i > 0, halo_ref[...], 0.0)
        xpad[SUBLANES:SUBLANES + tm, :] = xbc_ref[...]
        acc = cb_ref[...] + _conv_taps(xpad, cw_ref[...], tm, SUBLANES - 3)
        xact[...] = acc * _sigmoid(acc)
        dt = _softplus(dt_ref[...] + dtb_ref[...])
        adt = dt * a_ref[...]
        r_i = lax.broadcasted_iota(jnp.int32, (SSD_CHUNK, SSD_CHUNK), 0)
        c_i = lax.broadcasted_iota(jnp.int32, (SSD_CHUNK, SSD_CHUNK), 1)
        tri = (r_i >= c_i).astype(F32)
        lane1 = lax.broadcasted_iota(jnp.int32, (1, LANES), 1)
        for c in range(ncq):
            sl = slice(c * SSD_CHUNK, (c + 1) * SSD_CHUNK)
            dt_c = dt[sl]
            cs, cst, ecs = _ssd_chunk_common(adt[sl], tri)
            for g in range(2):
                bg = xact[sl, 512 + g * 128:512 + (g + 1) * 128]
                cg = xact[sl, 768 + g * 128:768 + (g + 1) * 128]
                cbm = _dot_nt(cg, bg)
                for pr in range(2):
                    pi = g * 2 + pr
                    psl = slice(pi * 128, (pi + 1) * 128)
                    xp = xact[sl, psl]
                    prev = state[pi]
                    st_ref[c, pi] = prev
                    yp = xp * d_ref[:, psl]
                    new_s = jnp.zeros((SSD_STATE, LANES), F32)
                    dec_lane = jnp.zeros((1, LANES), F32)
                    for hh in range(2):
                        h = g * 4 + pr * 2 + hh
                        lm = (lane1 >= 64) if hh else (lane1 < 64)
                        _, dt_col, cs_last, lmat, ecs_col, decay_col = _ssd_head_terms(cs, cst, ecs, dt_c, h, tri)
                        xdt = jnp.where(lm, xp, 0.0) * dt_col
                        yp = yp + _dot(cbm * lmat, xdt)
                        yp = yp + _dot(cg * ecs_col, jnp.where(lm, prev, 0.0))
                        new_s = new_s + _dot_tn(bg * decay_col, xdt)
                        dec_lane = dec_lane + jnp.where(lm, jnp.exp(cs_last), 0.0)
                    state[pi] = prev * dec_lane + new_s
                    yy_ref[sl, psl] = yp
        yy = yy_ref[...]
        z = z_ref[...]
        yg = yy * (z * _sigmoid(z))
        ms = jnp.mean(yg * yg, axis=-1, keepdims=True)
        y_ref[...] = yg * lax.rsqrt(ms + LN_EPS) * nw_ref[...]

    halo_map = lambda i: (jnp.maximum(i * hb - 1, 0), 0)
    return pl.pallas_call(
        body, name="ssd_fwd", grid=(nt,),
        in_specs=[pl.BlockSpec((tm, SSD_XBC), lambda i: (i, 0)), pl.BlockSpec((SUBLANES, SSD_XBC), halo_map),
                  pl.BlockSpec((tm, SSD_WIDTH), lambda i: (i, P_Z // SSD_WIDTH)),
                  pl.BlockSpec((tm, LANES), lambda i: (i, P_DT // LANES)),
                  _const((4, SSD_XBC)), _const((1, SSD_XBC)), _const((1, LANES)), _const((1, LANES)),
                  _const((1, SSD_WIDTH)), _const((1, SSD_WIDTH))],
        out_specs=[_rows(tm, SSD_WIDTH), _rows(tm, SSD_WIDTH),
                   pl.BlockSpec((ncq, 4, SSD_STATE, LANES), lambda i: (i, 0, 0, 0))],
        out_shape=[jax.ShapeDtypeStruct((t, SSD_WIDTH), F32), jax.ShapeDtypeStruct((t, SSD_WIDTH), F32),
                   jax.ShapeDtypeStruct((t // SSD_CHUNK, 4, SSD_STATE, LANES), F32)],
        scratch_shapes=[pltpu.VMEM((tm + SUBLANES, SSD_XBC), F32), pltpu.VMEM((tm, SSD_XBC), F32),
                        pltpu.VMEM((4, SSD_STATE, LANES), F32)],
        compiler_params=_cparams(("arbitrary",)),
    )(proj, proj, proj, proj, cw, cb, dtb, a_neg, d_lanes, nw)


def _ssd_bwd(dycat, proj, yy, states, cw, cb, dtb, a_neg, d_lanes, nw):
    t = proj.shape[0]
    tm = SSD_TM
    nt = t // tm
    ncq = tm // SSD_CHUNK
    hb = tm // SUBLANES

    def body(dy_ref, xbc_ref, halo_ref, z_ref, dt_ref, yy_ref, st_ref, cw_ref, cb_ref, dtb_ref, a_ref, d_ref, nw_ref,
             dxbc_ref, dz_ref, ddt_ref, dcw_ref, dcb_ref, ddtb_ref, da_ref, dd_ref, dnw_ref,
             xpad, xact, dxact, dpad, dstate, dnext):
        i = pl.program_id(0)

        @pl.when(i == 0)
        def _():
            for r in (dcw_ref, dcb_ref, ddtb_ref, da_ref, dd_ref, dnw_ref, dstate, dnext):
                r[...] = jnp.zeros_like(r)

        xpad[0:SUBLANES, :] = jnp.where(i < nt - 1, halo_ref[...], 0.0)
        xpad[SUBLANES:SUBLANES + tm, :] = xbc_ref[...]
        cw_v = cw_ref[...]
        acc = cb_ref[...] + _conv_taps(xpad, cw_v, tm, SUBLANES - 3)
        sig = _sigmoid(acc)
        xact[...] = acc * sig
        dt_raw = dt_ref[...] + dtb_ref[...]
        dt = _softplus(dt_raw)
        a_v = a_ref[...]
        adt = dt * a_v
        yy = yy_ref[...]
        z = z_ref[...]
        sz = _sigmoid(z)
        siluz = z * sz
        yg = yy * siluz
        ms = jnp.mean(yg * yg, axis=-1, keepdims=True)
        rinv = lax.rsqrt(ms + LN_EPS)
        dout = dy_ref[...]
        dnw_ref[...] += _sum0(dout * yg * rinv)
        dyn = dout * nw_ref[...]
        dyg = rinv * dyn - yg * (rinv * rinv * rinv) * jnp.mean(dyn * yg, axis=-1, keepdims=True)
        dyy = dyg * siluz
        dz_ref[...] = dyg * yy * (sz * (1.0 + z * (1.0 - sz)))
        dd_ref[...] += _sum0(dyy * xact[:, 0:SSD_WIDTH])

        r_i = lax.broadcasted_iota(jnp.int32, (SSD_CHUNK, SSD_CHUNK), 0)
        c_i = lax.broadcasted_iota(jnp.int32, (SSD_CHUNK, SSD_CHUNK), 1)
        tri = (r_i >= c_i).astype(F32)
        lane1 = lax.broadcasted_iota(jnp.int32, (1, LANES), 1)
        for c in reversed(range(ncq)):
            sl = slice(c * SSD_CHUNK, (c + 1) * SSD_CHUNK)
            dt_c = dt[sl]
            cs, cst, ecs = _ssd_chunk_common(adt[sl], tri)
            cacc = jnp.zeros((SSD_CHUNK, LANES), F32)
            racc = jnp.zeros((SSD_CHUNK, LANES), F32)
            ddtx = jnp.zeros((SSD_CHUNK, LANES), F32)
            for g in range(2):
                bg = xact[sl, 512 + g * 128:512 + (g + 1) * 128]
                cg = xact[sl, 768 + g * 128:768 + (g + 1) * 128]
                cbm = _dot_nt(cg, bg)
                dcb_m = jnp.zeros((SSD_CHUNK, SSD_CHUNK), F32)
                dbg = jnp.zeros((SSD_CHUNK, SSD_STATE), F32)
                dcg = jnp.zeros((SSD_CHUNK, SSD_STATE), F32)
                for pr in range(2):
                    pi = g * 2 + pr
                    psl = slice(pi * 128, (pi + 1) * 128)
                    xp = xact[sl, psl]
                    dyp = dyy[sl, psl]
                    prev = st_ref[c, pi]
                    ds_all = dstate[pi]
                    dxdt_p = jnp.zeros((SSD_CHUNK, LANES), F32)
                    dprev_new = jnp.zeros((SSD_STATE, LANES), F32)
                    dec_lane = jnp.zeros((1, LANES), F32)
                    dt_lanes = jnp.zeros((SSD_CHUNK, LANES), F32)
                    for hh in range(2):
                        h = g * 4 + pr * 2 + hh
                        lm = (lane1 >= 64) if hh else (lane1 < 64)
                        oh_l = (c_i == h).astype(F32)
                        oh_s = (r_i == h).astype(F32)
                        _, dt_col, cs_last, lmat, ecs_col, decay_col = _ssd_head_terms(cs, cst, ecs, dt_c, h, tri)
                        gm = cbm * lmat
                        xm = jnp.where(lm, xp, 0.0)
                        xdt = xm * dt_col
                        dym = jnp.where(lm, dyp, 0.0)
                        prevm = jnp.where(lm, prev, 0.0)
                        dsm = jnp.where(lm, ds_all, 0.0)
                        bdec = bg * decay_col
                        dxdt = _dot_tn(gm, dym) + _dot(bdec, dsm)
                        dxdt_p = dxdt_p + dxdt
                        ddtx = ddtx + oh_l * jnp.sum(dxdt * xm, axis=1, keepdims=True)
                        dt_lanes = dt_lanes + jnp.where(lm, dt_col, 0.0)
                        dgm = _dot_nt(dym, xdt)
                        dcb_m = dcb_m + dgm * lmat
                        w = dgm * gm
                        cacc = cacc + oh_l * jnp.sum(w, axis=1, keepdims=True)
                        racc = racc - oh_s * jnp.sum(w, axis=0, keepdims=True)
                        dce = _dot_nt(dym, prevm)
                        dcg = dcg + dce * ecs_col
                        cacc = cacc + oh_l * (jnp.sum(dce * cg, axis=1, keepdims=True) * ecs_col)
                        dprev_new = dprev_new + _dot_tn(cg * ecs_col, dym)
                        dbdec = _dot_nt(xdt, dsm)
                        dbg = dbg + dbdec * decay_col
                        dd = jnp.sum(dbdec * bg, axis=1, keepdims=True) * decay_col
                        cacc = cacc - oh_l * dd
                        cd = jnp.exp(cs_last)
                        dlast = jnp.sum(dd, axis=0, keepdims=True) + jnp.sum(
                            jnp.sum(dsm * prevm, axis=1, keepdims=True), axis=0, keepdims=True) * cd
                        cacc = cacc + jnp.where((r_i == SSD_CHUNK - 1) & (c_i == h), dlast, 0.0)
                        dec_lane = dec_lane + jnp.where(lm, cd, 0.0)
                    dstate[pi] = ds_all * dec_lane + dprev_new
                    dxact[sl, psl] = dxdt_p * dt_lanes + dyp * d_ref[:, psl]
                dcg = dcg + _dot(dcb_m, bg)
                dbg = dbg + _dot_tn(dcb_m, cg)
                dxact[sl, 512 + g * 128:512 + (g + 1) * 128] = dbg
                dxact[sl, 768 + g * 128:768 + (g + 1) * 128] = dcg
            dcs = cacc + racc.T
            dadt = _dot_f32((r_i <= c_i).astype(F32), dcs)
            ddt = dadt * a_v + ddtx
            da_ref[...] += _sum0(dadt * dt_c)
            ddt_raw = ddt * _sigmoid(dt_raw[sl])
            ddt_ref[sl, :] = ddt_raw
            ddtb_ref[...] += _sum0(ddt_raw)
        dacc = dxact[...] * (sig * (1.0 + acc * (1.0 - sig)))
        dcb_ref[...] += _sum0(dacc)
        for k in range(4):
            dcw_ref[k:k + 1, :] += _sum0(dacc * xpad[SUBLANES - 3 + k:SUBLANES - 3 + k + tm, :])
        dpad[0:tm, :] = dacc
        dpad[tm:tm + SUBLANES, :] = dnext[...]
        dx = cw_v[0:1, :] * dpad[3:3 + tm, :]
        for k in range(1, 4):
            dx = dx + cw_v[k:k + 1, :] * dpad[3 - k:3 - k + tm, :]
        dxbc_ref[...] = dx
        dnext[...] = dacc[0:SUBLANES, :]

    rev = lambda i: nt - 1 - i
    halo_map = lambda i: (jnp.maximum(rev(i) * hb - 1, 0), 0)
    rrow = lambda n, col=0: pl.BlockSpec((tm, n), lambda i: (rev(i), col))
    return pl.pallas_call(
        body, name="ssd_bwd", grid=(nt,),
        in_specs=[rrow(SSD_WIDTH), rrow(SSD_XBC), pl.BlockSpec((SUBLANES, SSD_XBC), halo_map),
                  rrow(SSD_WIDTH, P_Z // SSD_WIDTH), rrow(LANES, P_DT // LANES), rrow(SSD_WIDTH),
                  pl.BlockSpec((ncq, 4, SSD_STATE, LANES), lambda i: (rev(i), 0, 0, 0)),
                  _const((4, SSD_XBC)), _const((1, SSD_XBC)), _const((1, LANES)), _const((1, LANES)),
                  _const((1, SSD_WIDTH)), _const((1, SSD_WIDTH))],
        out_specs=[rrow(SSD_XBC), rrow(SSD_WIDTH), rrow(LANES), _const((SUBLANES, SSD_XBC)), _const((1, SSD_XBC)),
                   _const((1, LANES)), _const((1, LANES)), _const((1, SSD_WIDTH)), _const((1, SSD_WIDTH))],
        out_shape=[jax.ShapeDtypeStruct((t, SSD_XBC), F32), jax.ShapeDtypeStruct((t, SSD_WIDTH), F32),
                   jax.ShapeDtypeStruct((t, LANES), F32), jax.ShapeDtypeStruct((SUBLANES, SSD_XBC), F32),
                   jax.ShapeDtypeStruct((1, SSD_XBC), F32), jax.ShapeDtypeStruct((1, LANES), F32),
                   jax.ShapeDtypeStruct((1, LANES), F32), jax.ShapeDtypeStruct((1, SSD_WIDTH), F32),
                   jax.ShapeDtypeStruct((1, SSD_WIDTH), F32)],
        scratch_shapes=[pltpu.VMEM((tm + SUBLANES, SSD_XBC), F32), pltpu.VMEM((tm, SSD_XBC), F32),
                        pltpu.VMEM((tm, SSD_XBC), F32), pltpu.VMEM((tm + SUBLANES, SSD_XBC), F32),
                        pltpu.VMEM((4, SSD_STATE, LANES), F32), pltpu.VMEM((SUBLANES, SSD_XBC), F32)],
        compiler_params=_cparams(("arbitrary",)),
    )(dycat, proj, proj, proj, proj, yy, states, cw, cb, dtb, a_neg, d_lanes, nw)


def _cmul_add(ar, ai, br, bi, cr, ci):
    return ar + br * cr - bi * ci, ai + br * ci + bi * cr


def _s5_fwd(proj, bre, bim, cre, cim, d_skip, glu_w, glu_b, coef):
    t = proj.shape[0]
    tm = SCAN_TM
    ng = tm // SUBLANES

    def body(u_ref, bre_ref, bim_ref, cre_ref, cim_ref, d_ref, w_ref, b_ref, coef_ref,
             y_ref, y2_ref, hre_ref, him_ref, carry):
        i = pl.program_id(0)

        @pl.when(i == 0)
        def _():
            carry[...] = jnp.zeros_like(carry)

        u = u_ref[...]
        hre_ref[...] = _dot(u, bre_ref[...])
        him_ref[...] = _dot(u, bim_ref[...])

        def step(gi, car):
            cr_, ci_ = car
            rows = pl.ds(pl.multiple_of(gi * SUBLANES, SUBLANES), SUBLANES)
            r = hre_ref[rows, :]
            m = him_ref[rows, :]
            for k, sh in enumerate((1, 2, 4)):
                r, m = _cmul_add(r, m, coef_ref[k, 0], coef_ref[k, 1], pltpu.roll(r, sh, 0), pltpu.roll(m, sh, 0))
            r, m = _cmul_add(r, m, coef_ref[3, 0], coef_ref[3, 1], cr_, ci_)
            hre_ref[rows, :] = r
            him_ref[rows, :] = m
            return (jnp.broadcast_to(r[SUBLANES - 1:SUBLANES, :], r.shape),
                    jnp.broadcast_to(m[SUBLANES - 1:SUBLANES, :], m.shape))

        cr_, ci_ = lax.fori_loop(0, ng, step, (carry[0], carry[1]))
        carry[0] = cr_
        carry[1] = ci_
        y2 = _dot(hre_ref[...], cre_ref[...]) - _dot(him_ref[...], cim_ref[...]) + d_ref[...] * u
        y2_ref[...] = y2
        ya = _gelu(y2)
        y_ref[...] = ya * _sigmoid(_dot(ya, w_ref[...]) + b_ref[...])

    return pl.pallas_call(
        body, name="s5_fwd", grid=(t // tm,),
        in_specs=[pl.BlockSpec((tm, S5_WIDTH), lambda i: (i, P_U // S5_WIDTH)),
                  _const((S5_WIDTH, S5_NSTATE)), _const((S5_WIDTH, S5_NSTATE)), _const((S5_NSTATE, S5_WIDTH)),
                  _const((S5_NSTATE, S5_WIDTH)), _const((1, S5_WIDTH)), _const((S5_WIDTH, S5_WIDTH)),
                  _const((1, S5_WIDTH)), _const((5, 2, SUBLANES, S5_NSTATE))],
        out_specs=[_rows(tm, S5_WIDTH), _rows(tm, S5_WIDTH), _rows(tm, S5_NSTATE), _rows(tm, S5_NSTATE)],
        out_shape=[jax.ShapeDtypeStruct((t, S5_WIDTH), F32), jax.ShapeDtypeStruct((t, S5_WIDTH), F32),
                   jax.ShapeDtypeStruct((t, S5_NSTATE), F32), jax.ShapeDtypeStruct((t, S5_NSTATE), F32)],
        scratch_shapes=[pltpu.VMEM((2, SUBLANES, S5_NSTATE), F32)],
        compiler_params=_cparams(("arbitrary",)),
    )(proj, bre, bim, cre, cim, d_skip, glu_w, glu_b, coef)


def _s5_bwd(dycat, proj, y2, hre, him, bre, bim, cre, cim, d_skip, glu_w, glu_b, rcoef):
    t = proj.shape[0]
    tm = SCAN_TM
    nt = t // tm
    ng = tm // SUBLANES
    hb = tm // SUBLANES

    def body(dy_ref, u_ref, y2_ref, hre_ref, him_ref, hre_halo, him_halo, bre_ref, bim_ref, cre_ref, cim_ref, d_ref,
             w_ref, b_ref, coef_ref,
             du_ref, dbre_ref, dbim_ref, dcre_ref, dcim_ref, dlam_ref, dd_ref, dw_ref, dgb_ref,
             gre, gim, hpre, hpim, carry):
        i = pl.program_id(0)

        @pl.when(i == 0)
        def _():
            for r in (dbre_ref, dbim_ref, dcre_ref, dcim_ref, dlam_ref, dd_ref, dw_ref, dgb_ref, carry):
                r[...] = jnp.zeros_like(r)

        u = u_ref[...]
        y2 = y2_ref[...]
        dout = dy_ref[...]
        ya = _gelu(y2)
        sg = _sigmoid(_dot(ya, w_ref[...]) + b_ref[...])
        dv = dout * ya * sg * (1.0 - sg)
        dya = dout * sg + _dot_nt(dv, w_ref[...])
        dw_ref[...] += _dot_tn(ya, dv)
        dgb_ref[...] += _sum0(dv)
        dy2 = dya * _gelu_grad(y2)
        dd_ref[...] += _sum0(dy2 * u)
        hre_v = hre_ref[...]
        him_v = him_ref[...]
        dcre_ref[...] += _dot_tn(hre_v, dy2)
        dcim_ref[...] -= _dot_tn(him_v, dy2)
        gre[...] = _dot_nt(dy2, cre_ref[...])
        gim[...] = -_dot_nt(dy2, cim_ref[...])
        first = i == nt - 1
        hpre[0:SUBLANES, :] = jnp.where(first, 0.0, hre_halo[...])
        hpim[0:SUBLANES, :] = jnp.where(first, 0.0, him_halo[...])
        hpre[SUBLANES:SUBLANES + tm, :] = hre_v
        hpim[SUBLANES:SUBLANES + tm, :] = him_v
        row0 = lax.broadcasted_iota(jnp.int32, (SUBLANES, S5_NSTATE), 0) == 0

        def step(k, car):
            cr_, ci_, dlr, dli = car
            gi = ng - 1 - k
            rows = pl.ds(pl.multiple_of(gi * SUBLANES, SUBLANES), SUBLANES)
            nrows = pl.ds(pl.multiple_of(gi * SUBLANES + SUBLANES, SUBLANES), SUBLANES)
            r = gre[rows, :]
            m = gim[rows, :]
            for kk, sh in enumerate((1, 2, 4)):
                r, m = _cmul_add(r, m, coef_ref[kk, 0], coef_ref[kk, 1], pltpu.roll(r, SUBLANES - sh, 0),
                                 pltpu.roll(m, SUBLANES - sh, 0))
            r, m = _cmul_add(r, m, coef_ref[3, 0], coef_ref[3, 1], cr_, ci_)
            gre[rows, :] = r
            gim[rows, :] = m
            pr_ = hpre[rows, :]
            pm_ = hpim[rows, :]
            hr_ = jnp.where(row0, jnp.broadcast_to(pr_[SUBLANES - 1:SUBLANES, :], pr_.shape),
                            pltpu.roll(hpre[nrows, :], 1, 0))
            hm_ = jnp.where(row0, jnp.broadcast_to(pm_[SUBLANES - 1:SUBLANES, :], pm_.shape),
                            pltpu.roll(hpim[nrows, :], 1, 0))
            dlr = dlr + hr_ * r + hm_ * m
            dli = dli + hr_ * m - hm_ * r
            return (jnp.broadcast_to(r[0:1, :], r.shape), jnp.broadcast_to(m[0:1, :], m.shape), dlr, dli)

        z8 = jnp.zeros((SUBLANES, S5_NSTATE), F32)
        cr_, ci_, dlr, dli = lax.fori_loop(0, ng, step, (carry[0], carry[1], z8, z8))
        carry[0] = cr_
        carry[1] = ci_
        dlam_ref[0] += dlr
        dlam_ref[1] += dli
        g_re = gre[...]
        g_im = gim[...]
        du_ref[...] = dy2 * d_ref[...] + _dot_nt(g_re, bre_ref[...]) + _dot_nt(g_im, bim_ref[...])
        dbre_ref[...] += _dot_tn(u, g_re)
        dbim_ref[...] += _dot_tn(u, g_im)

    rev = lambda i: nt - 1 - i
    rrow = lambda n, col=0: pl.BlockSpec((tm, n), lambda i: (rev(i), col))
    halo = pl.BlockSpec((SUBLANES, S5_NSTATE), lambda i: (jnp.maximum(rev(i) * hb - 1, 0), 0))
    return pl.pallas_call(
        body, name="s5_bwd", grid=(nt,),
        in_specs=[rrow(S5_WIDTH, 512 // S5_WIDTH), rrow(S5_WIDTH, P_U // S5_WIDTH), rrow(S5_WIDTH),
                  rrow(S5_NSTATE), rrow(S5_NSTATE), halo, halo,
                  _const((S5_WIDTH, S5_NSTATE)), _const((S5_WIDTH, S5_NSTATE)), _const((S5_NSTATE, S5_WIDTH)),
                  _const((S5_NSTATE, S5_WIDTH)), _const((1, S5_WIDTH)), _const((S5_WIDTH, S5_WIDTH)),
                  _const((1, S5_WIDTH)), _const((5, 2, SUBLANES, S5_NSTATE))],
        out_specs=[rrow(S5_WIDTH), _const((S5_WIDTH, S5_NSTATE)), _const((S5_WIDTH, S5_NSTATE)),
                   _const((S5_NSTATE, S5_WIDTH)), _const((S5_NSTATE, S5_WIDTH)), _const((2, SUBLANES, S5_NSTATE)),
                   _const((1, S5_WIDTH)), _const((S5_WIDTH, S5_WIDTH)), _const((1, S5_WIDTH))],
        out_shape=[jax.ShapeDtypeStruct((t, S5_WIDTH), F32), jax.ShapeDtypeStruct((S5_WIDTH, S5_NSTATE), F32),
                   jax.ShapeDtypeStruct((S5_WIDTH, S5_NSTATE), F32), jax.ShapeDtypeStruct((S5_NSTATE, S5_WIDTH), F32),
                   jax.ShapeDtypeStruct((S5_NSTATE, S5_WIDTH), F32),
                   jax.ShapeDtypeStruct((2, SUBLANES, S5_NSTATE), F32), jax.ShapeDtypeStruct((1, S5_WIDTH), F32),
                   jax.ShapeDtypeStruct((S5_WIDTH, S5_WIDTH), F32), jax.ShapeDtypeStruct((1, S5_WIDTH), F32)],
        scratch_shapes=[pltpu.VMEM((tm, S5_NSTATE), F32), pltpu.VMEM((tm, S5_NSTATE), F32),
                        pltpu.VMEM((tm + SUBLANES, S5_NSTATE), F32), pltpu.VMEM((tm + SUBLANES, S5_NSTATE), F32),
                        pltpu.VMEM((2, SUBLANES, S5_NSTATE), F32)],
        compiler_params=_cparams(("arbitrary",)),
    )(dycat, proj, y2, hre, him, hre, him, bre, bim, cre, cim, d_skip, glu_w, glu_b, rcoef)


def _rg_gates(xc, wa, ba, wx, bx, nsp):
    r = _sigmoid(_dot(xc, wa) + ba)
    ig = _sigmoid(_dot(xc, wx) + bx)
    log_a = nsp * r
    a = jnp.exp(log_a)
    mult = jnp.sqrt(-_expm1(2.0 * log_a))
    return r, ig, a, mult


def _rg_fwd(proj, cw, cb, wa, ba, wx, bx, nsp):
    t = proj.shape[0]
    tm = SCAN_TM
    ng = tm // SUBLANES
    hb = tm // SUBLANES

    def body(x_ref, halo_ref, gt_ref, cw_ref, cb_ref, wa_ref, ba_ref, wx_ref, bx_ref, nsp_ref,
             y_ref, h_ref, xpad, abuf, carry):
        i = pl.program_id(0)

        @pl.when(i == 0)
        def _():
            carry[...] = jnp.zeros_like(carry)

        xpad[0:SUBLANES, :] = jnp.where(i > 0, halo_ref[...], 0.0)
        xpad[SUBLANES:SUBLANES + tm, :] = x_ref[...]
        xc = cb_ref[...] + _conv_taps(xpad, cw_ref[...], tm, SUBLANES - 3)
        _, ig, a, mult = _rg_gates(xc, wa_ref[...], ba_ref[...], wx_ref[...], bx_ref[...], nsp_ref[...])
        abuf[...] = a
        h_ref[...] = mult * (ig * xc)
        sub = lax.broadcasted_iota(jnp.int32, (SUBLANES, RG_WIDTH), 0)

        def step(gi, car):
            rows = pl.ds(pl.multiple_of(gi * SUBLANES, SUBLANES), SUBLANES)
            av = abuf[rows, :]
            bv = h_ref[rows, :]
            for sh in (1, 2, 4):
                m = sub >= sh
                bv = jnp.where(m, av * pltpu.roll(bv, sh, 0) + bv, bv)
                av = jnp.where(m, av * pltpu.roll(av, sh, 0), av)
            hv = bv + av * car
            h_ref[rows, :] = hv
            return jnp.broadcast_to(hv[SUBLANES - 1:SUBLANES, :], hv.shape)

        carry[...] = lax.fori_loop(0, ng, step, carry[...])
        y_ref[...] = h_ref[...] * _gelu(gt_ref[...])

    return pl.pallas_call(
        body, name="rg_fwd", grid=(t // tm,),
        in_specs=[pl.BlockSpec((tm, RG_WIDTH), lambda i: (i, P_XRG // RG_WIDTH)),
                  pl.BlockSpec((SUBLANES, RG_WIDTH), lambda i: (jnp.maximum(i * hb - 1, 0), P_XRG // RG_WIDTH)),
                  pl.BlockSpec((tm, RG_WIDTH), lambda i: (i, P_GRG // RG_WIDTH)),
                  _const((4, RG_WIDTH)), _const((1, RG_WIDTH)), _const((RG_WIDTH, RG_WIDTH)), _const((1, RG_WIDTH)),
                  _const((RG_WIDTH, RG_WIDTH)), _const((1, RG_WIDTH)), _const((1, RG_WIDTH))],
        out_specs=[_rows(tm, RG_WIDTH), _rows(tm, RG_WIDTH)],
        out_shape=[jax.ShapeDtypeStruct((t, RG_WIDTH), F32), jax.ShapeDtypeStruct((t, RG_WIDTH), F32)],
        scratch_shapes=[pltpu.VMEM((tm + SUBLANES, RG_WIDTH), F32), pltpu.VMEM((tm, RG_WIDTH), F32),
                        pltpu.VMEM((SUBLANES, RG_WIDTH), F32)],
        compiler_params=_cparams(("arbitrary",)),
    )(proj, proj, proj, cw, cb, wa, ba, wx, bx, nsp)


def _rg_bwd(dycat, proj, hs, cw, cb, wa, ba, wx, bx, nsp):
    t = proj.shape[0]
    tm = SCAN_TM
    nt = t // tm
    ng = tm // SUBLANES
    hb = tm // SUBLANES

    def body(dy_ref, x_ref, halo_ref, gt_ref, h_ref, h_halo, cw_ref, cb_ref, wa_ref, ba_ref, wx_ref, bx_ref, nsp_ref,
             dx_ref, dgt_ref, dcw_ref, dcb_ref, dwa_ref, dba_ref, dwx_ref, dbx_ref, dnsp_ref,
             xpad, abuf, gbuf, hpad, dabuf, dpad, carry, dnext):
        i = pl.program_id(0)

        @pl.when(i == 0)
        def _():
            for r in (dcw_ref, dcb_ref, dwa_ref, dba_ref, dwx_ref, dbx_ref, dnsp_ref, carry, dnext):
                r[...] = jnp.zeros_like(r)

        first = i == nt - 1
        xpad[0:SUBLANES, :] = jnp.where(first, 0.0, halo_ref[...])
        xpad[SUBLANES:SUBLANES + tm, :] = x_ref[...]
        cw_v = cw_ref[...]
        xc = cb_ref[...] + _conv_taps(xpad, cw_v, tm, SUBLANES - 3)
        nsp_v = nsp_ref[...]
        r, ig, a, mult = _rg_gates(xc, wa_ref[...], ba_ref[...], wx_ref[...], bx_ref[...], nsp_v)
        abuf[...] = a
        hv = h_ref[...]
        hpad[0:SUBLANES, :] = jnp.where(first, 0.0, h_halo[...])
        hpad[SUBLANES:SUBLANES + tm, :] = hv
        gt = gt_ref[...]
        dout = dy_ref[...]
        dgt_ref[...] = dout * hv * _gelu_grad(gt)
        gbuf[...] = dout * _gelu(gt)
        sub = lax.broadcasted_iota(jnp.int32, (SUBLANES, RG_WIDTH), 0)
        last_row = sub == SUBLANES - 1
        row0 = sub == 0

        def step(k, car):
            gi = ng - 1 - k
            rows = pl.ds(pl.multiple_of(gi * SUBLANES, SUBLANES), SUBLANES)
            nrows = pl.ds(pl.multiple_of(gi * SUBLANES + SUBLANES, SUBLANES), SUBLANES)
            av = abuf[rows, :]
            bv = gbuf[rows, :] + jnp.where(last_row, car, 0.0)
            ev = jnp.where(last_row, 0.0, pltpu.roll(av, SUBLANES - 1, 0))
            for sh in (1, 2, 4):
                m = sub < SUBLANES - sh
                bv = jnp.where(m, bv + ev * pltpu.roll(bv, SUBLANES - sh, 0), bv)
                ev = jnp.where(m, ev * pltpu.roll(ev, SUBLANES - sh, 0), 0.0)
            gbuf[rows, :] = bv
            pv = hpad[rows, :]
            hprev = jnp.where(row0, jnp.broadcast_to(pv[SUBLANES - 1:SUBLANES, :], pv.shape),
                              pltpu.roll(hpad[nrows, :], 1, 0))
            dabuf[rows, :] = bv * hprev
            return jnp.broadcast_to((av * bv)[0:1, :], bv.shape)

        carry[...] = lax.fori_loop(0, ng, step, carry[...])
        gv = gbuf[...]
        da = dabuf[...]
        ix = ig * xc
        dmult = gv * ix
        dig = gv * mult * xc
        dxc = gv * mult * ig
        dlog_a = da * a - dmult * (a * a) / mult
        dnsp_ref[...] += _sum0(dlog_a * r)
        dpr = dlog_a * nsp_v * r * (1.0 - r)
        dpi = dig * ig * (1.0 - ig)
        dxc = dxc + _dot_nt(dpr, wa_ref[...]) + _dot_nt(dpi, wx_ref[...])
        dwa_ref[...] += _dot_tn(xc, dpr)
        dwx_ref[...] += _dot_tn(xc, dpi)
        dba_ref[...] += _sum0(dpr)
        dbx_ref[...] += _sum0(dpi)
        dcb_ref[...] += _sum0(dxc)
        for k in range(4):
            dcw_ref[k:k + 1, :] += _sum0(dxc * xpad[SUBLANES - 3 + k:SUBLANES - 3 + k + tm, :])
        dpad[0:tm, :] = dxc
        dpad[tm:tm + SUBLANES, :] = dnext[...]
        dx = cw_v[0:1, :] * dpad[3:3 + tm, :]
        for k in range(1, 4):
            dx = dx + cw_v[k:k + 1, :] * dpad[3 - k:3 - k + tm, :]
        dx_ref[...] = dx
        dnext[...] = dxc[0:SUBLANES, :]

    rev = lambda i: nt - 1 - i
    rrow = lambda n, col=0: pl.BlockSpec((tm, n), lambda i: (rev(i), col))
    sq = _const((RG_WIDTH, RG_WIDTH))
    vec = _const((1, RG_WIDTH))
    return pl.pallas_call(
        body, name="rg_bwd", grid=(nt,),
        in_specs=[rrow(RG_WIDTH, 768 // RG_WIDTH), rrow(RG_WIDTH, P_XRG // RG_WIDTH),
                  pl.BlockSpec((SUBLANES, RG_WIDTH), lambda i: (jnp.maximum(rev(i) * hb - 1, 0), P_XRG // RG_WIDTH)),
                  rrow(RG_WIDTH, P_GRG // RG_WIDTH), rrow(RG_WIDTH),
                  pl.BlockSpec((SUBLANES, RG_WIDTH), lambda i: (jnp.maximum(rev(i) * hb - 1, 0), 0)),
                  _const((4, RG_WIDTH)), vec, sq, vec, sq, vec, vec],
        out_specs=[rrow(RG_WIDTH), rrow(RG_WIDTH), _const((SUBLANES, RG_WIDTH)), vec, sq, vec, sq, vec, vec],
        out_shape=[jax.ShapeDtypeStruct((t, RG_WIDTH), F32), jax.ShapeDtypeStruct((t, RG_WIDTH), F32),
                   jax.ShapeDtypeStruct((SUBLANES, RG_WIDTH), F32), jax.ShapeDtypeStruct((1, RG_WIDTH), F32),
                   jax.ShapeDtypeStruct((RG_WIDTH, RG_WIDTH), F32), jax.ShapeDtypeStruct((1, RG_WIDTH), F32),
                   jax.ShapeDtypeStruct((RG_WIDTH, RG_WIDTH), F32), jax.ShapeDtypeStruct((1, RG_WIDTH), F32),
                   jax.ShapeDtypeStruct((1, RG_WIDTH), F32)],
        scratch_shapes=[pltpu.VMEM((tm + SUBLANES, RG_WIDTH), F32), pltpu.VMEM((tm, RG_WIDTH), F32),
                        pltpu.VMEM((tm, RG_WIDTH), F32), pltpu.VMEM((tm + SUBLANES, RG_WIDTH), F32),
                        pltpu.VMEM((tm, RG_WIDTH), F32), pltpu.VMEM((tm + SUBLANES, RG_WIDTH), F32),
                        pltpu.VMEM((SUBLANES, RG_WIDTH), F32), pltpu.VMEM((SUBLANES, RG_WIDTH), F32)],
        compiler_params=_cparams(("arbitrary",)),
    )(dycat, proj, proj, proj, hs, hs, cw, cb, wa, ba, wx, bx, nsp)


def _block_diag(blocks):
    g, a, b = blocks.shape
    eye = jnp.eye(g, dtype=blocks.dtype)
    return (eye[:, None, :, None] * blocks[:, :, None, :]).reshape(g * a, g * b)


def _block_diag_extract(m, g):
    a, b = m.shape[0] // g, m.shape[1] // g
    m4 = m.reshape(g, a, g, b)
    idx = jnp.arange(g)
    return m4[idx, :, idx, :]


def _s5_prepare(lam_re, lam_im, log_step, b_re, b_im, c_re, c_im):
    step = jnp.exp(log_step)[:, None]
    mag = jnp.exp(lam_re * step)
    lbr = mag * jnp.cos(lam_im * step)
    lbi = mag * jnp.sin(lam_im * step)
    nr, ni = lbr - 1.0, lbi
    den = lam_re * lam_re + lam_im * lam_im
    cr = (nr * lam_re + ni * lam_im) / den
    ci = (ni * lam_re - nr * lam_im) / den
    bbr = cr[..., None] * b_re - ci[..., None] * b_im
    bbi = cr[..., None] * b_im + ci[..., None] * b_re
    bre = _block_diag(jnp.swapaxes(bbr, 1, 2))
    bim = _block_diag(jnp.swapaxes(bbi, 1, 2))
    cre = _block_diag(jnp.swapaxes(c_re, 1, 2))
    cim = _block_diag(jnp.swapaxes(c_im, 1, 2))
    return lbr.reshape(-1), lbi.reshape(-1), bre, bim, cre, cim


def _s5_scan_coef(lbr, lbi, reverse):
    if reverse:
        lbi = -lbi
    pr, pi = [lbr], [lbi]
    for _ in range(7):
        pr, pi = pr + [pr[-1] * lbr - pi[-1] * lbi], pi + [pr[-1] * lbi + pi[-1] * lbr]
    row = jnp.arange(SUBLANES)[:, None]
    tabs = []
    for sh in (1, 2, 4):
        keep = (row < SUBLANES - sh) if reverse else (row >= sh)
        tabs.append(jnp.stack([jnp.where(keep, pr[sh - 1][None, :], 0.0), jnp.where(keep, pi[sh - 1][None, :], 0.0)]))
    powr = jnp.stack(pr)
    powi = jnp.stack(pi)
    if reverse:
        powr, powi = powr[::-1], powi[::-1]
    tabs.append(jnp.stack([powr, powi]))
    tabs.append(jnp.zeros_like(tabs[-1]))
    return jnp.stack(tabs).astype(F32)


def _xy_peers():
    x, y, c = lax.axis_index("x"), lax.axis_index("y"), lax.axis_index("c")
    return x, y, c, [(1 - x, y), (x, 1 - y), (1 - x, 1 - y)]


def _hbm():
    return pl.BlockSpec(memory_space=pl.ANY)


def _xy_allgather(buf, *, name):
    n, w = buf.shape

    def body(x_ref, out_ref, send_sems, recv_sems, local_sem):
        x, y, c, peers = _xy_peers()
        me = 2 * x + y
        own = pltpu.make_async_copy(x_ref, out_ref.at[me], local_sem)
        own.start()
        sends = []
        for k, (px, py) in enumerate(peers):
            cp = pltpu.make_async_remote_copy(src_ref=x_ref, dst_ref=out_ref.at[me], send_sem=send_sems.at[k],
                                              recv_sem=recv_sems.at[k], device_id=(px, py, c), device_id_type=MESH)
            cp.start()
            sends.append(cp)
        for k, (px, py) in enumerate(peers):
            pltpu.make_async_remote_copy(src_ref=x_ref, dst_ref=out_ref.at[2 * px + py], send_sem=send_sems.at[k],
                                         recv_sem=recv_sems.at[k], device_id=(px, py, c),
                                         device_id_type=MESH).wait_recv()
        for cp in sends:
            cp.wait_send()
        own.wait()

    return pl.pallas_call(
        body, name=name, in_specs=[_hbm()], out_specs=_hbm(),
        out_shape=jax.ShapeDtypeStruct((4, n, w), buf.dtype),
        scratch_shapes=[pltpu.SemaphoreType.DMA((3,)), pltpu.SemaphoreType.DMA((3,)), pltpu.SemaphoreType.DMA],
    )(buf)


def _remote(src, dst, send_sem, recv_sem, dev):
    return pltpu.make_async_remote_copy(src_ref=src, dst_ref=dst, send_sem=send_sem, recv_sem=recv_sem,
                                        device_id=dev, device_id_type=MESH)


LAYER_GATHERED = (
    ("ssd_conv_w", (4, 256), 1), ("rg_conv_w", (4, LANES), 1),
    ("w_in", (1024, W_IN_PAD), 1), ("s5_glu_w", (64, 256), 0), ("w_out", (256, 1024), 0), ("xa_wq", (256, 1024), 0),
    ("xa_wk", (256, 1024), 0), ("xa_wv", (256, 1024), 0), ("xa_wo", (256, 1024), 0), ("mlp_w1", (1024, 1024), 1),
    ("mlp_w2", (1024, 1024), 0),
)
N_GATHERED = len(LAYER_GATHERED)
WAIT_GROUPS = ((0, 1, 2, 3), (4,), (5, 6, 7, 8), (9, 10))
RG_CONV_SHARD = RG_WIDTH // 4
N_GATHER_COPIES = 3 * N_GATHERED * DEPTH


def _gather_part(ref, t, pos):
    _, shp, ax = LAYER_GATHERED[t % N_GATHERED]
    idx = tuple(pl.ds(pos * shp[ax], shp[ax]) if d == ax else slice(None) for d in range(len(shp)))
    return ref.at[idx]


def _gather_start(shards):
    n = len(shards)
    lands = []
    for t, s in enumerate(shards):
        _, shp, ax = LAYER_GATHERED[t % N_GATHERED]
        full = shp[:ax] + (4 * shp[ax],) + shp[ax + 1:]
        lands.append(pltpu.with_memory_space_constraint(lax.empty(full, s.dtype), pltpu.HBM))

    def body(*refs):
        srcs, lnds = refs[:n], refs[n:2 * n]
        send_sems, recv_sems, local_sems = refs[2 * n:2 * n + 3]
        token = refs[-1]
        x, y, c, peers = _xy_peers()
        me = 2 * x + y
        for t in range(n):
            for k, (px, py) in enumerate(peers):
                _remote(srcs[t], _gather_part(lnds[t], t, me), send_sems.at[k * n + t], recv_sems.at[k * n + t],
                        (px, py, c)).start()
            pltpu.make_async_copy(srcs[t], _gather_part(lnds[t], t, me), local_sems.at[t]).start()
        token[...] = jnp.zeros_like(token)

    hbm = pl.BlockSpec(memory_space=pltpu.HBM)
    sem = pl.BlockSpec(memory_space=pltpu.SEMAPHORE)
    outs = pl.pallas_call(
        body, name="weights_gather_start", in_specs=[hbm] * (2 * n),
        out_shape=(pltpu.SemaphoreType.DMA((3 * n,)), pltpu.SemaphoreType.DMA((3 * n,)),
                   pltpu.SemaphoreType.DMA((n,)),
                   *[pltpu.HBM(s.shape, s.dtype) for s in shards], *[pltpu.HBM(a.shape, a.dtype) for a in lands],
                   jax.ShapeDtypeStruct((SUBLANES, LANES), F32)),
        out_specs=(sem, sem, sem, *[hbm] * (2 * n), pl.BlockSpec(memory_space=pltpu.VMEM)),
        input_output_aliases={i: 3 + i for i in range(2 * n)},
        compiler_params=pltpu.CompilerParams(has_side_effects=pltpu.SideEffectType.DATAFLOW_SIDE_EFFECTING),
    )(*[pltpu.with_memory_space_constraint(s, pltpu.HBM) for s in shards], *lands)
    return outs[0], outs[1], outs[2], outs[3:3 + n], outs[3 + n:3 + 2 * n], outs[-1]


def _gather_wait(handle, ts, after, *, name):
    send_sems, recv_sems, local_sems, src_thru, land_thru, _ = handle
    n = len(src_thru)
    m = len(ts)

    def body(*refs):
        srcs, lnds = refs[:m], refs[m:2 * m]
        ssem, rsem, lsem = refs[2 * m:2 * m + 3]
        x, y, c, peers = _xy_peers()
        me = 2 * x + y
        for i, t in enumerate(ts):
            for k, (px, py) in enumerate(peers):
                cp = _remote(srcs[i], _gather_part(lnds[i], t, 2 * px + py), ssem.at[k * n + t], rsem.at[k * n + t],
                             (px, py, c))
                cp.wait_send()
                cp.wait_recv()
            pltpu.make_async_copy(srcs[i], _gather_part(lnds[i], t, me), lsem.at[t]).wait()

    hbm = pl.BlockSpec(memory_space=pltpu.HBM)
    sem = pl.BlockSpec(memory_space=pltpu.SEMAPHORE)
    args = [src_thru[t] for t in ts] + [land_thru[t] for t in ts]
    outs = pl.pallas_call(
        body, name=name, in_specs=[hbm] * (2 * m) + [sem, sem, sem, pl.BlockSpec(memory_space=pl.ANY)],
        out_shape=[pltpu.HBM(a.shape, a.dtype) for a in args], out_specs=[hbm] * (2 * m),
        input_output_aliases={i: i for i in range(2 * m)},
        compiler_params=pltpu.CompilerParams(has_side_effects=pltpu.SideEffectType.DATAFLOW_SIDE_EFFECTING),
    )(*args, send_sems, recv_sems, local_sems, after)
    return outs[:m], outs[m:]


C_CHUNKS = 4
XY_CHUNKS = 4
EW_ROWS = 512


def _c_exchange(g):
    _, n, w = g.shape
    n2 = n // 2
    rq = n2 // C_CHUNKS

    def body(g_ref, got_ref, send_sems, recv_sems):
        x, y, c = lax.axis_index("x"), lax.axis_index("y"), lax.axis_index("c")
        cps = []
        for s in range(4):
            for q in range(C_CHUNKS):
                k = s * C_CHUNKS + q
                cp = _remote(g_ref.at[s, pl.ds((1 - c) * n2 + q * rq, rq), :], got_ref.at[s, pl.ds(q * rq, rq), :],
                             send_sems.at[k], recv_sems.at[k], (x, y, 1 - c))
                cp.start()
                cps.append(cp)
        for cp in cps:
            cp.wait_recv()
        for cp in cps:
            cp.wait_send()

    return pl.pallas_call(
        body, name="grad_c_exchange", in_specs=[_hbm()], out_specs=_hbm(),
        out_shape=jax.ShapeDtypeStruct((4, n2, w), g.dtype),
        scratch_shapes=[pltpu.SemaphoreType.DMA((4 * C_CHUNKS,)), pltpu.SemaphoreType.DMA((4 * C_CHUNKS,))],
    )(g)


XFER_DTYPE = jnp.bfloat16


def _add_own_half(g, got, c_arr):
    _, n, w = g.shape
    n2 = n // 2
    nb = n2 // EW_ROWS

    def body(c_ref, a_ref, b_ref, o_ref, t_ref):
        sm = a_ref[...] + b_ref[...]
        o_ref[...] = sm.astype(o_ref.dtype)

        @pl.when(pl.program_id(1) == nb - 1)
        def _():
            t_ref[...] = sm[:, EW_ROWS - MISC_ROWS:, :]

    grid_spec = pltpu.PrefetchScalarGridSpec(
        num_scalar_prefetch=1, grid=(4, nb),
        in_specs=[pl.BlockSpec((1, EW_ROWS, w), lambda s, i, c: (s, c[0] * nb + i, 0)),
                  pl.BlockSpec((1, EW_ROWS, w), lambda s, i, c: (s, i, 0))],
        out_specs=[pl.BlockSpec((1, EW_ROWS, w), lambda s, i, c: (s, i, 0)),
                   pl.BlockSpec((1, MISC_ROWS, w), lambda s, i, c: (s, 0, 0))])
    return pl.pallas_call(
        body, name="grad_add_halves", grid_spec=grid_spec,
        out_shape=[jax.ShapeDtypeStruct((4, n2, w), XFER_DTYPE), jax.ShapeDtypeStruct((4, MISC_ROWS, w), g.dtype)],
        compiler_params=_cparams(("arbitrary", "arbitrary")),
    )(c_arr, g, got)


def _xy_exchange(arrs):
    na = len(arrs)
    pieces = []
    for a, arr in enumerate(arrs):
        nch = XY_CHUNKS if a == 0 else 1
        rq = arr.shape[1] // nch
        pieces += [(a, pl.ds(q * rq, rq)) for q in range(nch)]
    npc = len(pieces)

    def body(*refs):
        ins, outs = refs[:na], refs[na:2 * na]
        send_sems, recv_sems, local_sems = refs[2 * na:]
        x, y, c, peers = _xy_peers()
        me = 2 * x + y
        own = []
        for j, (a, rows) in enumerate(pieces):
            cp = pltpu.make_async_copy(ins[a].at[me, rows, :], outs[a].at[me, rows, :], local_sems.at[j])
            cp.start()
            own.append(cp)
        sends = []
        for k, (px, py) in enumerate(peers):
            for j, (a, rows) in enumerate(pieces):
                cp = _remote(ins[a].at[2 * px + py, rows, :], outs[a].at[me, rows, :], send_sems.at[k * npc + j],
                             recv_sems.at[k * npc + j], (px, py, c))
                cp.start()
                sends.append(cp)
        for k, (px, py) in enumerate(peers):
            for j, (a, rows) in enumerate(pieces):
                _remote(ins[a].at[me, rows, :], outs[a].at[2 * px + py, rows, :], send_sems.at[k * npc + j],
                        recv_sems.at[k * npc + j], (px, py, c)).wait_recv()
        for cp in sends:
            cp.wait_send()
        for cp in own:
            cp.wait()

    return pl.pallas_call(
        body, name="grad_xy_exchange", in_specs=[_hbm()] * na, out_specs=[_hbm()] * na,
        out_shape=[jax.ShapeDtypeStruct(a.shape, a.dtype) for a in arrs],
        scratch_shapes=[pltpu.SemaphoreType.DMA((3 * npc,)), pltpu.SemaphoreType.DMA((3 * npc,)),
                        pltpu.SemaphoreType.DMA((npc,))],
    )(*arrs)


def _sum4_into_half(r, rt, c_arr):
    _, n2, w = r.shape
    nb = n2 // EW_ROWS

    def body(c_ref, r_ref, t_ref, o_ref):
        o_ref[...] = ((r_ref[0].astype(F32) + r_ref[1].astype(F32)) + r_ref[2].astype(F32)) + r_ref[3].astype(F32)

        @pl.when(pl.program_id(0) == nb - 1)
        def _():
            o_ref[EW_ROWS - MISC_ROWS:, :] = ((t_ref[0] + t_ref[1]) + t_ref[2]) + t_ref[3]

    grid_spec = pltpu.PrefetchScalarGridSpec(
        num_scalar_prefetch=1, grid=(nb,),
        in_specs=[pl.BlockSpec((4, EW_ROWS, w), lambda i, c: (0, i, 0)),
                  pl.BlockSpec((4, MISC_ROWS, w), lambda i, c: (0, 0, 0))],
        out_specs=pl.BlockSpec((EW_ROWS, w), lambda i, c: (c[0] * nb + i, 0)))
    return pl.pallas_call(
        body, name="grad_sum4", grid_spec=grid_spec, out_shape=jax.ShapeDtypeStruct((2 * n2, w), F32),
        compiler_params=_cparams(("arbitrary",)),
    )(c_arr, r, rt)


C_GATHER_CHUNKS = 8


def _c_allgather_halves(f):
    n, w = f.shape
    n2 = n // 2
    rq = n2 // C_GATHER_CHUNKS

    def body(f_ref, out_ref, send_sems, recv_sems):
        x, y, c = lax.axis_index("x"), lax.axis_index("y"), lax.axis_index("c")
        sends = []
        for q in range(C_GATHER_CHUNKS):
            rows = pl.ds(c * n2 + q * rq, rq)
            cp = _remote(f_ref.at[rows, :], out_ref.at[rows, :], send_sems.at[q], recv_sems.at[q], (x, y, 1 - c))
            cp.start()
            sends.append(cp)
        for q in range(C_GATHER_CHUNKS):
            rows = pl.ds((1 - c) * n2 + q * rq, rq)
            _remote(f_ref.at[rows, :], out_ref.at[rows, :], send_sems.at[q], recv_sems.at[q],
                    (x, y, 1 - c)).wait_recv()
        for cp in sends:
            cp.wait_send()

    return pl.pallas_call(
        body, name="grad_c_allgather", in_specs=[_hbm()], out_specs=_hbm(), input_output_aliases={0: 0},
        out_shape=jax.ShapeDtypeStruct((n, w), f.dtype),
        scratch_shapes=[pltpu.SemaphoreType.DMA((C_GATHER_CHUNKS,)), pltpu.SemaphoreType.DMA((C_GATHER_CHUNKS,))],
    )(f)


def _adamw(w, m, v, g, g_row0=None):
    shape = w.shape
    cols = shape[-1]
    rows = int(math.prod(shape)) // cols
    tr = 256 if rows % 256 == 0 else rows
    from_flat = g_row0 is not None
    c1 = 1.0 / (1.0 - ADAM_B1 ** ADAM_STEP)
    c2 = 1.0 / (1.0 - ADAM_B2 ** ADAM_STEP)

    def body(w_ref, m_ref, v_ref, g_ref, *outs):
        gg = g_ref[...]
        nm = ADAM_B1 * m_ref[...] + (1.0 - ADAM_B1) * gg
        nv = ADAM_B2 * v_ref[...] + (1.0 - ADAM_B2) * (gg * gg)
        if from_flat:
            outs[0][...] = gg
        d_ref, nm_ref, nv_ref = outs[-3:]
        nm_ref[...] = nm
        nv_ref[...] = nv
        d_ref[...] = -ADAM_LR * ((nm * c1) / (jnp.sqrt(nv * c2) + ADAM_EPS) + ADAM_WD * w_ref[...])

    spec = pl.BlockSpec((tr, cols), lambda i: (i, 0))
    if from_flat:
        assert cols == FLAT and g_row0 % tr == 0
        g_spec = pl.BlockSpec((tr, cols), lambda i: (g_row0 // tr + i, 0))
        g_arg = g
    else:
        g_spec = spec
        g_arg = g.reshape(rows, cols)
    n_out = 4 if from_flat else 3
    sds = jax.ShapeDtypeStruct((rows, cols), F32)
    outs = pl.pallas_call(
        body, name="adamw", grid=(rows // tr,), in_specs=[spec, spec, spec, g_spec], out_specs=[spec] * n_out,
        out_shape=[sds] * n_out, compiler_params=_cparams(("arbitrary",)),
    )(w.reshape(rows, cols), m.reshape(rows, cols), v.reshape(rows, cols), g_arg)
    outs = [o.reshape(shape) for o in outs]
    return outs if from_flat else [g] + outs


SMALL_SHARDED = (("s5_glu_w", (2, 64, 256), 1), ("ssd_conv_w", (2, 4, 256), 2), ("rg_conv_w", (2, 4, 64), 2))
REPLICATED = (
    ("ssd_conv_b", (2, 1024)), ("ssd_dt_bias", (2, 8)), ("ssd_a_log", (2, 8)), ("ssd_d", (2, 8)),
    ("ssd_norm_w", (2, 512)), ("s5_lam_re", (2, 16, 64)), ("s5_lam_im", (2, 16, 64)), ("s5_log_step", (2, 16)),
    ("s5_b_re", (2, 16, 64, 16)), ("s5_b_im", (2, 16, 64, 16)), ("s5_c_re", (2, 16, 16, 64)),
    ("s5_c_im", (2, 16, 16, 64)), ("s5_d", (2, 256)), ("s5_glu_b", (2, 256)), ("rg_conv_b", (2, 256)),
    ("rg_wa", (2, 4, 64, 64)), ("rg_ba", (2, 4, 64)), ("rg_wx", (2, 4, 64, 64)), ("rg_bx", (2, 4, 64)),
    ("rg_lambda", (2, 256)), ("ln1_g", (2, 1024)), ("ln1_b", (2, 1024)), ("ln2_g", (2, 1024)), ("ln2_b", (2, 1024)),
    ("ln3_g", (2, 1024)), ("ln3_b", (2, 1024)),
)
WEIGHT_ORDER = (
    "w_in", "w_out", "ssd_conv_w", "ssd_conv_b", "ssd_dt_bias", "ssd_a_log", "ssd_d", "ssd_norm_w", "s5_lam_re",
    "s5_lam_im", "s5_log_step", "s5_b_re", "s5_b_im", "s5_c_re", "s5_c_im", "s5_d", "s5_glu_w", "s5_glu_b",
    "rg_conv_w", "rg_conv_b", "rg_wa", "rg_ba", "rg_wx", "rg_bx", "rg_lambda", "ln1_g", "ln1_b", "xa_wq", "xa_wk",
    "xa_wv", "xa_wo", "ln2_g", "ln2_b", "mlp_w1", "mlp_w2", "ln3_g", "ln3_b",
)


def _size(shape):
    return int(math.prod(shape))


def _round_up(a, b):
    return (a + b - 1) // b * b


SMALL_ELEMS = sum(_size(s) for _, s, _ in SMALL_SHARDED)
REP_ELEMS = sum(_size(s) for _, s in REPLICATED)
REP_QROWS = _round_up(-(-REP_ELEMS // (4 * FLAT)), 8)
assert SMALL_ELEMS <= MISC_REP_ROW * FLAT and MISC_REP_ROW + REP_QROWS <= MISC_ROWS


def _pack_shards(tensors, names_shapes):
    return jnp.concatenate([tensors[n].reshape(-1) for n, *_ in names_shapes])


def _unpack(flat, names_shapes):
    out, off = {}, 0
    for n, s, *_ in names_shapes:
        out[n] = flat[off:off + _size(s)].reshape(s)
        off += _size(s)
    return out


def _split_shards(full, names_shapes):
    rows = []
    for k in range(4):
        parts = []
        for n, s, ax in names_shapes:
            w = s[ax]
            parts.append(lax.slice_in_dim(full[n], k * w, (k + 1) * w, axis=ax).reshape(-1))
        rows.append(jnp.concatenate(parts))
    return jnp.stack(rows)


def _pack_cols(w):
    pad = jnp.zeros((w.shape[0], LANES - SSD_HEADS), w.dtype)
    return jnp.concatenate([w[:, O_XBC:O_XBC + 1024], w[:, O_Z:O_Z + 512], w[:, O_U:O_U + 256],
                            w[:, O_XRG:O_XRG + 256], w[:, O_GRG:O_GRG + 256], w[:, O_DT:O_DT + 8], pad], axis=1)


def _unpack_cols(w):
    return jnp.concatenate([w[:, P_Z:P_Z + 512], w[:, P_XBC:P_XBC + 1024], w[:, P_DT:P_DT + 8],
                            w[:, P_U:P_U + 256], w[:, P_XRG:P_XRG + 256], w[:, P_GRG:P_GRG + 256]], axis=1)


def _lanes(v, width):
    return jnp.pad(v, (0, width - v.shape[0])).reshape(1, width)


def _layer_params(rep, l):
    p = {}
    p["ssd_cb"] = rep["ssd_conv_b"][l].reshape(1, -1)
    p["ssd_dtb"] = _lanes(rep["ssd_dt_bias"][l], LANES)
    p["ssd_a"] = _lanes(-jnp.exp(rep["ssd_a_log"][l]), LANES)
    p["ssd_d"] = jnp.repeat(rep["ssd_d"][l], 64).reshape(1, -1)
    p["ssd_nw"] = rep["ssd_norm_w"][l].reshape(1, -1)
    s5_args = tuple(rep[n][l] for n in ("s5_lam_re", "s5_lam_im", "s5_log_step", "s5_b_re", "s5_b_im", "s5_c_re",
                                        "s5_c_im"))
    (lbr, lbi, bre, bim, cre, cim), p["s5_vjp"] = jax.vjp(_s5_prepare, *s5_args)
    p.update(s5_bre=bre, s5_bim=bim, s5_cre=cre, s5_cim=cim)
    p["s5_coef"] = _s5_scan_coef(lbr, lbi, False)
    p["s5_rcoef"] = _s5_scan_coef(lbr, lbi, True)
    p["s5_d"] = rep["s5_d"][l].reshape(1, -1)
    p["s5_gb"] = rep["s5_glu_b"][l].reshape(1, -1)
    p["rg_cb"] = rep["rg_conv_b"][l].reshape(1, -1)
    p["rg_wa"] = _block_diag(rep["rg_wa"][l])
    p["rg_wx"] = _block_diag(rep["rg_wx"][l])
    p["rg_ba"] = rep["rg_ba"][l].reshape(1, -1)
    p["rg_bx"] = rep["rg_bx"][l].reshape(1, -1)
    p["rg_nsp"] = (-RG_C * jax.nn.softplus(-rep["rg_lambda"][l])).reshape(1, -1)
    p["rg_dnsp"] = RG_C * jax.nn.sigmoid(-rep["rg_lambda"][l])
    for n in ("ln1_g", "ln1_b", "ln2_g", "ln2_b", "ln3_g", "ln3_b"):
        p[n] = rep[n][l].reshape(1, -1)
    return p


def _layer_fwd(h, mem, p, fetch):
    s = {"h0": h}
    p.update(fetch(0, h))
    proj = _mm(h, p["w_in"], name="in_proj")
    s["proj"] = proj
    y_ssd, s["ssd_yy"], s["ssd_states"] = _ssd_fwd(proj, p["ssd_cw"], p["ssd_cb"], p["ssd_dtb"], p["ssd_a"],
                                                     p["ssd_d"], p["ssd_nw"])
    y_s5, s["s5_y2"], s["s5_hre"], s["s5_him"] = _s5_fwd(proj, p["s5_bre"], p["s5_bim"], p["s5_cre"], p["s5_cim"],
                                                         p["s5_d"], p["s5_glu_w"], p["s5_gb"], p["s5_coef"])
    y_rg, s["rg_h"] = _rg_fwd(proj, p["rg_cw"], p["rg_cb"], p["rg_wa"], p["rg_ba"], p["rg_wx"], p["rg_bx"],
                              p["rg_nsp"])
    s["ys"] = [y_ssd, y_s5, y_rg]
    p.update(fetch(1, y_rg))
    h1, s["xh1"], s["rs1"] = _outproj_ln_fwd(s["ys"], h, p["w_out"], p["ln1_g"], p["ln1_b"])
    s["h1"] = h1
    p.update(fetch(2, h1))
    kb = _mm(mem, p["xa_wk"], name="mem_proj")
    vb = _mm(mem, p["xa_wv"], name="mem_proj")
    s["kb"], s["vb"] = kb, vb
    h2, s["xh2"], s["rs2"], s["attn_o"] = _attn_ln_fwd(h1, p["xa_wq"], p["xa_wo"], kb, vb, p["ln2_g"], p["ln2_b"])
    s["h2"] = h2
    p.update(fetch(3, h2))
    h3, s["xh3"], s["rs3"], s["mlp_hdn"] = _mlp_ln_fwd(h2, p["mlp_w1"], p["mlp_w2"], p["ln3_g"], p["ln3_b"])
    return h3, s


def _layer_bwd(dh3, mem, p, s, l, gbuf):
    g = {}
    dr3, du, dh2, g["ln3_g"], g["ln3_b"] = _mlp_ln_bwd(dh3, s["xh3"], s["rs3"], p["ln3_g"], s["mlp_hdn"],
                                                        p["mlp_w1"], p["mlp_w2"])
    gbuf = _wgrad_flat(s["h2"], du, gbuf, mode="colblk", row_off=ROW_MLP_W1 + 1024 * l, name="wgrad_mlp_w1")
    gbuf = _wgrad_flat(s["mlp_hdn"], dr3, gbuf, mode="rowblk", row_off=ROW_MLP_W2 + 1024 * l, name="wgrad_mlp_w2")
    dr2, dq, dh1, dkb, dvb, g["ln2_g"], g["ln2_b"] = _attn_ln_bwd(dh2, s["xh2"], s["rs2"], p["ln2_g"], s["h1"],
                                                                   p["xa_wq"], p["xa_wo"], s["kb"], s["vb"])
    for n, a_op, g_op in (("xa_wo", s["attn_o"], dr2), ("xa_wq", s["h1"], dq), ("xa_wk", mem, dkb),
                          ("xa_wv", mem, dvb)):
        gbuf = _wgrad_flat(a_op, g_op, gbuf, mode="rows4", row_off=ROW_XA[n] + 256 * l, name="wgrad_" + n)
    dr1, dres, dycat, g["ln1_g"], g["ln1_b"] = _outproj_ln_bwd(dh1, s["xh1"], s["rs1"], p["ln1_g"], p["w_out"])
    gbuf = _wgrad_flat(s["ys"], dr1, gbuf, mode="rows4", row_off=ROW_W_OUT + 256 * l, name="wgrad_w_out")
    proj = s["proj"]
    (dxbc, dz, ddt, dcw, dcb, ddtb, da_neg, dd_l, dnw) = _ssd_bwd(
        dycat, proj, s["ssd_yy"], s["ssd_states"], p["ssd_cw"], p["ssd_cb"], p["ssd_dtb"], p["ssd_a"], p["ssd_d"],
        p["ssd_nw"])
    g["ssd_conv_w"] = dcw[0:4]
    g["ssd_conv_b"] = dcb[0]
    g["ssd_dt_bias"] = ddtb[0, :SSD_HEADS]
    g["ssd_a_log"] = da_neg[0, :SSD_HEADS] * p["ssd_a"][0, :SSD_HEADS]
    g["ssd_d"] = dd_l.reshape(SSD_HEADS, 64).sum(axis=1)
    g["ssd_norm_w"] = dnw[0]
    (du_s5, dbre, dbim, dcre, dcim, dlam, dd5, dgw, dgb) = _s5_bwd(
        dycat, proj, s["s5_y2"], s["s5_hre"], s["s5_him"], p["s5_bre"], p["s5_bim"], p["s5_cre"], p["s5_cim"],
        p["s5_d"], p["s5_glu_w"], p["s5_gb"], p["s5_rcoef"])
    dl = dlam.sum(axis=1)
    s5g = p["s5_vjp"]((dl[0], dl[1], dbre, dbim, dcre, dcim))
    for n, v in zip(("s5_lam_re", "s5_lam_im", "s5_log_step", "s5_b_re", "s5_b_im", "s5_c_re", "s5_c_im"), s5g):
        g[n] = v
    g["s5_d"] = dd5[0]
    g["s5_glu_w"] = dgw
    g["s5_glu_b"] = dgb[0]
    (dxrg, dgrg, drcw, drcb, dwa, dba, dwx, dbx, dnsp) = _rg_bwd(
        dycat, proj, s["rg_h"], p["rg_cw"], p["rg_cb"], p["rg_wa"], p["rg_ba"], p["rg_wx"], p["rg_bx"], p["rg_nsp"])
    g["rg_conv_w"] = drcw[0:4]
    g["rg_conv_b"] = drcb[0]
    g["rg_wa"] = _block_diag_extract(dwa, RG_BLOCKS)
    g["rg_wx"] = _block_diag_extract(dwx, RG_BLOCKS)
    g["rg_ba"] = dba.reshape(RG_BLOCKS, RG_BLOCK_DIM)
    g["rg_bx"] = dbx.reshape(RG_BLOCKS, RG_BLOCK_DIM)
    g["rg_lambda"] = dnsp[0] * p["rg_dnsp"]
    dproj = [dxbc, dz, du_s5, dxrg, dgrg, ddt]
    g["w_in"] = _unpack_cols(_wgrad_in(s["h0"], dproj))
    dh0 = _in_proj_bwd(dproj, p["w_in"], dres)
    for n in ("ln1_g", "ln1_b", "ln2_g", "ln2_b", "ln3_g", "ln3_b"):
        g[n] = g[n][0]
    return dh0, g, gbuf


def _local_step(h, memf, target, rep, fetch):
    params, saved = [], []
    for l in range(DEPTH):
        p = _layer_params(rep, l)
        params.append(p)
        h, s = _layer_fwd(h, memf, p, functools.partial(fetch, l))
        saved.append(s)
    loss11, dh = _loss_fwd_bwd(h, target)
    grads = [None] * DEPTH
    gbuf = None
    for l in reversed(range(DEPTH)):
        dh, grads[l], gbuf = _layer_bwd(dh, memf, params[l], saved[l], l, gbuf)
    return loss11, dh, {n: jnp.stack([grads[l][n] for l in range(DEPTH)]) for n in grads[0]}, gbuf


def kernel(x, mem, w_in, w_out, ssd_conv_w, ssd_conv_b, ssd_dt_bias, ssd_a_log, ssd_d, ssd_norm_w, s5_lam_re, s5_lam_im, s5_log_step, s5_b_re, s5_b_im, s5_c_re, s5_c_im, s5_d, s5_glu_w, s5_glu_b, rg_conv_w, rg_conv_b, rg_wa, rg_ba, rg_wx, rg_bx, rg_lambda, ln1_g, ln1_b, xa_wq, xa_wk, xa_wv, xa_wo, ln2_g, ln2_b, mlp_w1, mlp_w2, ln3_g, ln3_b, loss_target, m_w_in, m_w_out, m_ssd_conv_w, m_ssd_conv_b, m_ssd_dt_bias, m_ssd_a_log, m_ssd_d, m_ssd_norm_w, m_s5_lam_re, m_s5_lam_im, m_s5_log_step, m_s5_b_re, m_s5_b_im, m_s5_c_re, m_s5_c_im, m_s5_d, m_s5_glu_w, m_s5_glu_b, m_rg_conv_w, m_rg_conv_b, m_rg_wa, m_rg_ba, m_rg_wx, m_rg_bx, m_rg_lambda, m_ln1_g, m_ln1_b, m_xa_wq, m_xa_wk, m_xa_wv, m_xa_wo, m_ln2_g, m_ln2_b, m_mlp_w1, m_mlp_w2, m_ln3_g, m_ln3_b, v_w_in, v_w_out, v_ssd_conv_w, v_ssd_conv_b, v_ssd_dt_bias, v_ssd_a_log, v_ssd_d, v_ssd_norm_w, v_s5_lam_re, v_s5_lam_im, v_s5_log_step, v_s5_b_re, v_s5_b_im, v_s5_c_re, v_s5_c_im, v_s5_d, v_s5_glu_w, v_s5_glu_b, v_rg_conv_w, v_rg_conv_b, v_rg_wa, v_rg_ba, v_rg_wx, v_rg_bx, v_rg_lambda, v_ln1_g, v_ln1_b, v_xa_wq, v_xa_wk, v_xa_wv, v_xa_wo, v_ln2_g, v_ln2_b, v_mlp_w1, v_mlp_w2, v_ln3_g, v_ln3_b):
    args = dict(locals())
    weights = {n: args[n] for n in WEIGHT_ORDER}
    mom_m = {n: args["m_" + n] for n in WEIGHT_ORDER}
    mom_v = {n: args["v_" + n] for n in WEIGHT_ORDER}

    shards = []
    for l in range(DEPTH):
        for n, shp, ax in LAYER_GATHERED:
            w = weights[n][l]
            if w.shape[1] != shp[1]:
                w = jnp.pad(w, ((0, 0), (0, shp[1] - w.shape[1])))
            if n not in ("ssd_conv_w", "rg_conv_w"):
                w = w.astype(MXU_DTYPE)
            shards.append(w)
    handle = _gather_start(shards)

    def unpad(arr, padded, width):
        return jnp.concatenate([arr[:, padded * k:padded * k + width] for k in range(4)], axis=1)

    def fetch(l, grp, after):
        ts = [l * N_GATHERED + j for j in WAIT_GROUPS[grp]]
        _, landed = _gather_wait(handle, ts, after, name="weights_gather_wait_%d_%d" % (l, grp))
        out = {}
        for t, arr in zip(ts, landed):
            n = LAYER_GATHERED[t % N_GATHERED][0]
            if n == "w_in":
                arr = _pack_cols(unpad(arr, W_IN_PAD, W_IN_SHARD))
            elif n == "rg_conv_w":
                arr = unpad(arr, LANES, RG_CONV_SHARD)
            out[{"ssd_conv_w": "ssd_cw", "rg_conv_w": "rg_cw"}.get(n, n)] = arr
        return out

    rep = {n: weights[n] for n, _ in REPLICATED}

    loss11, dx, gsmall, gbuf = _local_step(x[0], mem[0], loss_target[0], rep, fetch)
    grad_x = dx[None]
    loss = lax.psum(loss11[0, 0], ("x", "y", "c"))

    gw = gsmall["w_in"].reshape(DEPTH, D_MODEL, 4, W_IN_SHARD)
    gw = jnp.pad(gw, ((0, 0), (0, 0), (0, 0), (0, W_IN_PAD - W_IN_SHARD)))
    w_in_blk = jnp.transpose(gw, (2, 0, 1, 3)).reshape(4, DEPTH * W_IN_PAD, FLAT)
    small_q = _split_shards(gsmall, SMALL_SHARDED)
    rep_q = jnp.pad(_pack_shards(gsmall, REPLICATED), (0, 4 * REP_QROWS * FLAT - REP_ELEMS)).reshape(4, -1)
    misc = jnp.concatenate(
        [jnp.pad(small_q, ((0, 0), (0, MISC_REP_ROW * FLAT - SMALL_ELEMS))), rep_q,
         jnp.zeros((4, (MISC_ROWS - MISC_REP_ROW - REP_QROWS) * FLAT), F32)], axis=1).reshape(4, MISC_ROWS, FLAT)
    gbuf = lax.dynamic_update_slice(gbuf, w_in_blk, (0, ROW_W_IN, 0))
    gbuf = lax.dynamic_update_slice(gbuf, misc, (0, ROW_MISC, 0))
    c_arr = lax.axis_index("c").astype(jnp.int32).reshape(1)
    chip_sum, chip_tail = _add_own_half(gbuf, _c_exchange(gbuf), c_arr)
    got_sum, got_tail = _xy_exchange([chip_sum, chip_tail])
    reduced = _c_allgather_halves(_sum4_into_half(got_sum, got_tail, c_arr))
    misc_red = reduced[ROW_MISC:]
    rep_all = _xy_allgather(misc_red[MISC_REP_ROW:MISC_REP_ROW + REP_QROWS], name="small_grads_allgather")
    g_red = {**_unpack(misc_red[:MISC_REP_ROW].reshape(-1), SMALL_SHARDED),
             **_unpack(rep_all.reshape(-1), REPLICATED)}
    g_red["w_in"] = reduced[ROW_W_IN:ROW_W_IN + DEPTH * W_IN_PAD].reshape(DEPTH, D_MODEL, W_IN_PAD)[:, :, :W_IN_SHARD]

    flat_rows = {"mlp_w1": ROW_MLP_W1, "mlp_w2": ROW_MLP_W2, "w_out": ROW_W_OUT, **ROW_XA}
    res = {}
    for n in WEIGHT_ORDER:
        if n in flat_rows:
            res[n] = _adamw(weights[n], mom_m[n], mom_v[n], reduced, g_row0=flat_rows[n])
        else:
            res[n] = _adamw(weights[n], mom_m[n], mom_v[n], g_red[n])
    return (loss, grad_x, *[res[n][0] for n in WEIGHT_ORDER], *[res[n][1] for n in WEIGHT_ORDER],
            *[res[n][2] for n in WEIGHT_ORDER], *[res[n][3] for n in WEIGHT_ORDER])
```

```python
import functools
import math

import jax
import jax.numpy as jnp
from jax import lax
from jax.experimental import pallas as pl
from jax.experimental.pallas import tpu as pltpu

F32 = jnp.float32
MXU_DTYPE = jnp.bfloat16

D_MODEL = 1024
DEPTH = 2
MEM_LEN = 256
SSD_WIDTH = 512
SSD_HEADS = 8
SSD_STATE = 128
SSD_CHUNK = 128
SSD_XBC = 1024
S5_WIDTH = 256
S5_GROUPS = 16
S5_GROUP_CH = 16
S5_STATE = 64
S5_NSTATE = S5_GROUPS * S5_STATE
RG_WIDTH = 256
RG_BLOCKS = 4
RG_BLOCK_DIM = 64
RG_C = 8.0
XA_HEADS = 4
XA_HEAD_DIM = 256
D_FF = 4096
D_IN = 2312
ALPHA = (2.0 * DEPTH) ** 0.25
LN_EPS = 1e-5
ADAM_LR = 0.001
ADAM_B1 = 0.9
ADAM_B2 = 0.999
ADAM_EPS = 1e-08
ADAM_WD = 0.01
ADAM_STEP = 10

P_XBC, P_Z, P_U, P_XRG, P_GRG, P_DT = 0, 1024, 1536, 1792, 2048, 2304
D_PACK = 2432
O_Z, O_XBC, O_DT, O_U, O_XRG, O_GRG = 0, 512, 1536, 1544, 1800, 2056

LANES = 128
SUBLANES = 8
VMEM_LIMIT = 52 * 1024 * 1024
TM = 512
SSD_TM = 256
SCAN_TM = 512
FLAT = 1024

MESH = pl.DeviceIdType.MESH


def _cparams(sem):
    return pltpu.CompilerParams(dimension_semantics=sem, vmem_limit_bytes=VMEM_LIMIT)


def _dot(a, b):
    return jnp.dot(a.astype(MXU_DTYPE), b.astype(MXU_DTYPE), preferred_element_type=F32)


def _dot_nt(a, b):
    return lax.dot_general(a.astype(MXU_DTYPE), b.astype(MXU_DTYPE), (((1,), (1,)), ((), ())),
                           preferred_element_type=F32)


def _dot_tn(a, b):
    return lax.dot_general(a.astype(MXU_DTYPE), b.astype(MXU_DTYPE), (((0,), (0,)), ((), ())),
                           preferred_element_type=F32)


def _dot_f32(a, b):
    return jnp.dot(a, b, precision=lax.Precision.HIGHEST, preferred_element_type=F32)


def _dot_f32_tn(a, b):
    return lax.dot_general(a, b, (((0,), (0,)), ((), ())), precision=lax.Precision.HIGHEST,
                           preferred_element_type=F32)


def _sigmoid(x):
    return 1.0 / (1.0 + jnp.exp(-x))


def _softplus(x):
    return jnp.maximum(x, 0.0) + jnp.log(1.0 + jnp.exp(-jnp.abs(x)))


_GELU_K = math.sqrt(2.0 / math.pi)


def _gelu(x):
    return 0.5 * x * (1.0 + jnp.tanh(_GELU_K * (x + 0.044715 * x * x * x)))


def _gelu_grad(x):
    t = jnp.tanh(_GELU_K * (x + 0.044715 * x * x * x))
    return 0.5 * (1.0 + t) + 0.5 * x * (1.0 - t * t) * _GELU_K * (1.0 + 3.0 * 0.044715 * x * x)


def _expm1(x):
    small = x * (1.0 + x * (0.5 + x * (1.0 / 6.0 + x * (1.0 / 24.0))))
    return jnp.where(jnp.abs(x) < 0.05, small, jnp.exp(x) - 1.0)


def _sum0(x):
    return jnp.sum(x, axis=0, keepdims=True)


def _ln_fwd(r, g, b):
    mu = jnp.mean(r, axis=-1, keepdims=True)
    xc = r - mu
    var = jnp.mean(xc * xc, axis=-1, keepdims=True)
    rstd = lax.rsqrt(var + LN_EPS)
    xhat = xc * rstd
    return xhat * g + b, xhat, rstd


def _ln_bwd(dout, xhat, rstd, g):
    dxh = dout * g
    m1 = jnp.mean(dxh, axis=-1, keepdims=True)
    m2 = jnp.mean(dxh * xhat, axis=-1, keepdims=True)
    return rstd * (dxh - m1 - xhat * m2)


def _rows(tm, n, col=0):
    return pl.BlockSpec((tm, n), lambda i: (i, col))


def _const(shape):
    nd = len(shape)
    return pl.BlockSpec(shape, lambda i: (0,) * nd)


def _mm(a, w, *, name):
    t, k = a.shape
    n = w.shape[1]
    tm = min(TM, t)

    def body(a_ref, w_ref, o_ref):
        o_ref[...] = _dot(a_ref[...], w_ref[...])

    return pl.pallas_call(
        body, name=name, grid=(t // tm,), in_specs=[_rows(tm, k), _const(w.shape)], out_specs=_rows(tm, n),
        out_shape=jax.ShapeDtypeStruct((t, n), F32), compiler_params=_cparams(("arbitrary",)),
    )(a, w)


DPROJ_PIECES = ((P_XBC, 1024), (P_Z, 512), (P_U, 256), (P_XRG, 256), (P_GRG, 256), (P_DT, LANES))


def _in_proj_bwd(pieces, w, dres):
    t = dres.shape[0]
    npc = len(pieces)

    def body(*refs):
        w_ref, r_ref, o_ref = refs[npc:]
        acc = r_ref[...]
        for p_ref, (off, k) in zip(refs[:npc], DPROJ_PIECES):
            acc = acc + _dot_nt(p_ref[...], w_ref[:, off:off + k])
        o_ref[...] = acc

    return pl.pallas_call(
        body, name="in_proj_bwd", grid=(t // TM,),
        in_specs=[_rows(TM, k) for _, k in DPROJ_PIECES] + [_const(w.shape), _rows(TM, D_MODEL)],
        out_specs=_rows(TM, D_MODEL), out_shape=jax.ShapeDtypeStruct((t, D_MODEL), F32),
        compiler_params=_cparams(("arbitrary",)),
    )(*pieces, w, dres)


def _wgrad_in(h0, pieces):
    t = h0.shape[0]
    npc = len(pieces)

    def body(*refs):
        h_ref, o_ref = refs[npc], refs[npc + 1]
        @pl.when(pl.program_id(0) == 0)
        def _():
            o_ref[...] = jnp.zeros_like(o_ref)

        hb = h_ref[...].astype(MXU_DTYPE)
        for p_ref, (off, k) in zip(refs[:npc], DPROJ_PIECES):
            o_ref[:, off:off + k] += _dot_tn(hb, p_ref[...])

    return pl.pallas_call(
        body, name="wgrad_in", grid=(t // TM,),
        in_specs=[_rows(TM, k) for _, k in DPROJ_PIECES] + [_rows(TM, D_MODEL)],
        out_specs=_const((D_MODEL, D_PACK)), out_shape=jax.ShapeDtypeStruct((D_MODEL, D_PACK), F32),
        compiler_params=_cparams(("arbitrary",)),
    )(*pieces, h0)


G_ROWS = 8192
G_QUARTER = G_ROWS // 4
W_IN_SHARD = 578
W_IN_PAD = 640
MISC_ROWS = 128
MISC_REP_ROW = 40
ROW_MISC = G_ROWS - MISC_ROWS


def _grad_row(name, l):
    c0 = (1 - l) * G_QUARTER
    c1 = 2 * G_QUARTER + c0
    return {"mlp_w1": c0, "mlp_w2": c0 + 1024, "w_out": c1, "xa_wq": c1 + 256, "xa_wk": c1 + 512, "xa_wv": c1 + 768,
            "xa_wo": c1 + 1024, "w_in": c1 + 1280}[name]


W_IN_BLOCK_ROWS = G_QUARTER - 1280


def _wgrad_flat(a, g, buf, *, mode, row_off, name):
    pieces = list(a) if isinstance(a, (list, tuple)) else [a]
    t = g.shape[0]
    tt = min(1024, t)
    ns = t // tt
    blk = D_MODEL

    def accumulate(o_ref, parts, s):
        @pl.when(s == 0)
        def _():
            o_ref[...] = jnp.zeros_like(o_ref)

        for q, v in parts:
            o_ref[q] += v

    if mode == "rows4":
        grid = (ns,)
        in_specs = [pl.BlockSpec((tt, p.shape[1]), lambda s: (s, 0)) for p in pieces]
        in_specs.append(pl.BlockSpec((tt, blk), lambda s: (s, 0)))
        out_spec = pl.BlockSpec((4, 256, FLAT), lambda s: (0, row_off // 256, 0))
        sem = ("arbitrary",)
        npc = len(pieces)

        def body(*refs):
            g_v = refs[npc][...]
            parts, q0 = [], 0
            for p_ref in refs[:npc]:
                full = _dot_tn(p_ref[...], g_v)
                nq = full.shape[0] // 256
                parts += [(q0 + q, full[q * 256:(q + 1) * 256]) for q in range(nq)]
                q0 += nq
            accumulate(refs[-1], parts, pl.program_id(0))
    else:
        grid = (2, ns)
        if mode == "rowblk":
            in_specs = [pl.BlockSpec((tt, 2 * blk), lambda q, s: (s, q)), pl.BlockSpec((tt, blk), lambda q, s: (s, 0))]
        else:
            in_specs = [pl.BlockSpec((tt, blk), lambda q, s: (s, 0)), pl.BlockSpec((tt, 2 * blk), lambda q, s: (s, q))]
        out_spec = pl.BlockSpec((2, blk, FLAT), lambda q, s: (q, row_off // blk, 0))
        sem = ("arbitrary", "arbitrary")

        def body(a_ref, g_ref, *rest):
            full = _dot_tn(a_ref[...], g_ref[...])
            if mode == "rowblk":
                parts = [(0, full[:blk]), (1, full[blk:])]
            else:
                parts = [(0, full[:, :blk]), (1, full[:, blk:])]
            accumulate(rest[-1], parts, pl.program_id(1))

    args = pieces + [g]
    aliases = {}
    if buf is not None:
        in_specs.append(pl.BlockSpec(memory_space=pl.ANY))
        args.append(buf)
        aliases = {len(args) - 1: 0}
    return pl.pallas_call(
        body, name=name, grid=grid, in_specs=in_specs, out_specs=out_spec,
        out_shape=jax.ShapeDtypeStruct((4, G_ROWS, FLAT), F32), input_output_aliases=aliases,
        compiler_params=_cparams(sem),
    )(*args)


def _outproj_ln_fwd(ys, h, w, g, b):
    t = h.shape[0]
    npc = len(ys)

    def body(*refs):
        h_ref, w_ref, g_ref, b_ref, hn_ref, xh_ref, rs_ref = refs[npc:]
        r = ALPHA * h_ref[...]
        off = 0
        for y_ref in refs[:npc]:
            k = y_ref.shape[1]
            r = r + _dot(y_ref[...], w_ref[off:off + k, :])
            off += k
        out, xhat, rstd = _ln_fwd(r, g_ref[...], b_ref[...])
        hn_ref[...] = out
        xh_ref[...] = xhat
        rs_ref[...] = rstd

    return pl.pallas_call(
        body, name="outproj_ln_fwd", grid=(t // TM,),
        in_specs=[_rows(TM, y.shape[1]) for y in ys] + [_rows(TM, D_MODEL), _const((D_MODEL, D_MODEL)),
                                                        _const((1, D_MODEL)), _const((1, D_MODEL))],
        out_specs=[_rows(TM, D_MODEL), _rows(TM, D_MODEL), _rows(TM, 1)],
        out_shape=[jax.ShapeDtypeStruct((t, D_MODEL), F32), jax.ShapeDtypeStruct((t, D_MODEL), F32),
                   jax.ShapeDtypeStruct((t, 1), F32)],
        compiler_params=_cparams(("arbitrary",)),
    )(*ys, h, w, g, b)


def _attn_probs(q, kb, hh):
    sl = slice(hh * XA_HEAD_DIM, (hh + 1) * XA_HEAD_DIM)
    s = _dot_nt(q[:, sl], kb[:, sl]) * (1.0 / math.sqrt(XA_HEAD_DIM))
    m = jnp.max(s, axis=-1, keepdims=True)
    e = jnp.exp(s - m)
    return e / jnp.sum(e, axis=-1, keepdims=True)


def _attn_ln_fwd(h1, wq, wo, kb, vb, g, b):
    t = h1.shape[0]

    def body(h_ref, wq_ref, wo_ref, k_ref, v_ref, g_ref, b_ref, hn_ref, xh_ref, rs_ref, o_ref):
        h = h_ref[...]
        q = _dot(h, wq_ref[...])
        kb_ = k_ref[...]
        vb_ = v_ref[...]
        for hh in range(XA_HEADS):
            sl = slice(hh * XA_HEAD_DIM, (hh + 1) * XA_HEAD_DIM)
            p = _attn_probs(q, kb_, hh)
            o_ref[:, sl] = _dot(p, vb_[:, sl]).astype(o_ref.dtype)
        r = ALPHA * h + _dot(o_ref[...], wo_ref[...])
        out, xhat, rstd = _ln_fwd(r, g_ref[...], b_ref[...])
        hn_ref[...] = out
        xh_ref[...] = xhat
        rs_ref[...] = rstd

    return pl.pallas_call(
        body, name="attn_ln_fwd", grid=(t // TM,),
        in_specs=[_rows(TM, D_MODEL), _const((D_MODEL, D_MODEL)), _const((D_MODEL, D_MODEL)),
                  _const((MEM_LEN, D_MODEL)), _const((MEM_LEN, D_MODEL)), _const((1, D_MODEL)), _const((1, D_MODEL))],
        out_specs=[_rows(TM, D_MODEL), _rows(TM, D_MODEL), _rows(TM, 1), _rows(TM, D_MODEL)],
        out_shape=[jax.ShapeDtypeStruct((t, D_MODEL), F32), jax.ShapeDtypeStruct((t, D_MODEL), F32),
                   jax.ShapeDtypeStruct((t, 1), F32), jax.ShapeDtypeStruct((t, D_MODEL), MXU_DTYPE)],
        compiler_params=_cparams(("arbitrary",)),
    )(h1, wq, wo, kb, vb, g, b)


def _attn_ln_bwd(dh2, xhat, rstd, g, h1, wq, wo, kb, vb):
    t = h1.shape[0]

    def body(dh_ref, xh_ref, rs_ref, g_ref, h_ref, wq_ref, wo_ref, k_ref, v_ref,
             dr_ref, dq_ref, dh1_ref, dk_ref, dv_ref, dg_ref, db_ref):
        i = pl.program_id(0)

        @pl.when(i == 0)
        def _():
            dk_ref[...] = jnp.zeros_like(dk_ref)
            dv_ref[...] = jnp.zeros_like(dv_ref)
            dg_ref[...] = jnp.zeros_like(dg_ref)
            db_ref[...] = jnp.zeros_like(db_ref)

        dout = dh_ref[...]
        xh = xh_ref[...]
        dg_ref[...] += _sum0(dout * xh)
        db_ref[...] += _sum0(dout)
        dr = _ln_bwd(dout, xh, rs_ref[...], g_ref[...])
        dr_ref[...] = dr.astype(dr_ref.dtype)
        do = _dot_nt(dr, wo_ref[...])
        h = h_ref[...]
        q = _dot(h, wq_ref[...])
        kb_ = k_ref[...]
        vb_ = v_ref[...]
        scale = 1.0 / math.sqrt(XA_HEAD_DIM)
        for hh in range(XA_HEADS):
            sl = slice(hh * XA_HEAD_DIM, (hh + 1) * XA_HEAD_DIM)
            p = _attn_probs(q, kb_, hh)
            do_h = do[:, sl]
            dp = _dot_nt(do_h, vb_[:, sl])
            ds = p * (dp - jnp.sum(dp * p, axis=-1, keepdims=True)) * scale
            dq_ref[:, sl] = _dot(ds, kb_[:, sl]).astype(dq_ref.dtype)
            dk_ref[:, sl] += _dot_tn(ds, q[:, sl])
            dv_ref[:, sl] += _dot_tn(p, do_h)
        dh1_ref[...] = ALPHA * dr + _dot_nt(dq_ref[...], wq_ref[...])

    return pl.pallas_call(
        body, name="attn_ln_bwd", grid=(t // TM,),
        in_specs=[_rows(TM, D_MODEL), _rows(TM, D_MODEL), _rows(TM, 1), _const((1, D_MODEL)), _rows(TM, D_MODEL),
                  _const((D_MODEL, D_MODEL)), _const((D_MODEL, D_MODEL)), _const((MEM_LEN, D_MODEL)),
                  _const((MEM_LEN, D_MODEL))],
        out_specs=[_rows(TM, D_MODEL), _rows(TM, D_MODEL), _rows(TM, D_MODEL), _const((MEM_LEN, D_MODEL)),
                   _const((MEM_LEN, D_MODEL)), _const((1, D_MODEL)), _const((1, D_MODEL))],
        out_shape=[jax.ShapeDtypeStruct((t, D_MODEL), MXU_DTYPE), jax.ShapeDtypeStruct((t, D_MODEL), MXU_DTYPE),
                   jax.ShapeDtypeStruct((t, D_MODEL), F32), jax.ShapeDtypeStruct((MEM_LEN, D_MODEL), F32),
                   jax.ShapeDtypeStruct((MEM_LEN, D_MODEL), F32), jax.ShapeDtypeStruct((1, D_MODEL), F32),
                   jax.ShapeDtypeStruct((1, D_MODEL), F32)],
        compiler_params=_cparams(("arbitrary",)),
    )(dh2, xhat, rstd, g, h1, wq, wo, kb, vb)


FF_CHUNK = 1024
N_FF = D_FF // FF_CHUNK


def _load_resident(pairs, sems):
    copies = [pltpu.make_async_copy(src, dst, sems.at[k]) for k, (src, dst) in enumerate(pairs)]
    for cp in copies:
        cp.start()
    for cp in copies:
        cp.wait()


def _mlp_ln_fwd(h2, w1, w2, g, b):
    t = h2.shape[0]

    def body(h_ref, w1_hbm, w2_hbm, g_ref, b_ref, hn_ref, xh_ref, rs_ref, hd_ref, w1_v, w2_v, acc_ref, sems):
        @pl.when(pl.program_id(0) == 0)
        def _():
            _load_resident([(w1_hbm, w1_v), (w2_hbm, w2_v)], sems)

        h = h_ref[...]
        hb = h.astype(MXU_DTYPE)
        acc_ref[...] = ALPHA * h
        for j in range(N_FF):
            sl = slice(j * FF_CHUNK, (j + 1) * FF_CHUNK)
            u = _dot(hb, w1_v[:, sl])
            hd = jnp.square(jnp.maximum(u, 0.0)).astype(MXU_DTYPE)
            hd_ref[:, sl] = hd
            acc_ref[...] += _dot(hd, w2_v[sl, :])
        out, xhat, rstd = _ln_fwd(acc_ref[...], g_ref[...], b_ref[...])
        hn_ref[...] = out
        xh_ref[...] = xhat
        rs_ref[...] = rstd

    return pl.pallas_call(
        body, name="mlp_ln_fwd", grid=(t // TM,),
        in_specs=[_rows(TM, D_MODEL), _hbm(), _hbm(), _const((1, D_MODEL)), _const((1, D_MODEL))],
        out_specs=[_rows(TM, D_MODEL), _rows(TM, D_MODEL), _rows(TM, 1), _rows(TM, D_FF)],
        out_shape=[jax.ShapeDtypeStruct((t, D_MODEL), F32), jax.ShapeDtypeStruct((t, D_MODEL), F32),
                   jax.ShapeDtypeStruct((t, 1), F32), jax.ShapeDtypeStruct((t, D_FF), MXU_DTYPE)],
        scratch_shapes=[pltpu.VMEM((D_MODEL, D_FF), MXU_DTYPE), pltpu.VMEM((D_FF, D_MODEL), MXU_DTYPE),
                        pltpu.VMEM((TM, D_MODEL), F32), pltpu.SemaphoreType.DMA((2,))],
        compiler_params=_cparams(("arbitrary",)),
    )(h2, w1, w2, g, b)


def _mlp_ln_bwd(dh3, xhat, rstd, g, hdn, w1, w2):
    t = dh3.shape[0]

    def body(dh_ref, xh_ref, rs_ref, g_ref, hd_ref, w1_hbm, w2_hbm,
             dr_ref, du_ref, dh2_ref, dg_ref, db_ref, w1_v, w2_v, acc_ref, sems):
        @pl.when(pl.program_id(0) == 0)
        def _():
            _load_resident([(w1_hbm, w1_v), (w2_hbm, w2_v)], sems)
            dg_ref[...] = jnp.zeros_like(dg_ref)
            db_ref[...] = jnp.zeros_like(db_ref)

        dout = dh_ref[...]
        xh = xh_ref[...]
        dg_ref[...] += _sum0(dout * xh)
        db_ref[...] += _sum0(dout)
        dr = _ln_bwd(dout, xh, rs_ref[...], g_ref[...])
        drb = dr.astype(MXU_DTYPE)
        dr_ref[...] = drb
        acc_ref[...] = ALPHA * dr
        for j in range(N_FF):
            sl = slice(j * FF_CHUNK, (j + 1) * FF_CHUNK)
            dhd = _dot_nt(drb, w2_v[sl, :])
            du = (dhd * (2.0 * jnp.sqrt(hd_ref[:, sl].astype(F32)))).astype(MXU_DTYPE)
            du_ref[:, sl] = du
            acc_ref[...] += _dot_nt(du, w1_v[:, sl])
        dh2_ref[...] = acc_ref[...]

    tm = TM // 2
    return pl.pallas_call(
        body, name="mlp_ln_bwd", grid=(t // tm,),
        in_specs=[_rows(tm, D_MODEL), _rows(tm, D_MODEL), _rows(tm, 1), _const((1, D_MODEL)), _rows(tm, D_FF),
                  _hbm(), _hbm()],
        out_specs=[_rows(tm, D_MODEL), _rows(tm, D_FF), _rows(tm, D_MODEL), _const((1, D_MODEL)),
                   _const((1, D_MODEL))],
        out_shape=[jax.ShapeDtypeStruct((t, D_MODEL), MXU_DTYPE), jax.ShapeDtypeStruct((t, D_FF), MXU_DTYPE),
                   jax.ShapeDtypeStruct((t, D_MODEL), F32), jax.ShapeDtypeStruct((1, D_MODEL), F32),
                   jax.ShapeDtypeStruct((1, D_MODEL), F32)],
        scratch_shapes=[pltpu.VMEM((D_MODEL, D_FF), MXU_DTYPE), pltpu.VMEM((D_FF, D_MODEL), MXU_DTYPE),
                        pltpu.VMEM((tm, D_MODEL), F32), pltpu.SemaphoreType.DMA((2,))],
        compiler_params=_cparams(("arbitrary",)),
    )(dh3, xhat, rstd, g, hdn, w1, w2)


def _outproj_ln_bwd(dh1, xhat, rstd, g, w):
    t = dh1.shape[0]

    def body(dh_ref, xh_ref, rs_ref, g_ref, w_ref, dr_ref, res_ref, dy_ref, dg_ref, db_ref):
        i = pl.program_id(0)

        @pl.when(i == 0)
        def _():
            dg_ref[...] = jnp.zeros_like(dg_ref)
            db_ref[...] = jnp.zeros_like(db_ref)

        dout = dh_ref[...]
        xh = xh_ref[...]
        dg_ref[...] += _sum0(dout * xh)
        db_ref[...] += _sum0(dout)
        dr = _ln_bwd(dout, xh, rs_ref[...], g_ref[...])
        dr_ref[...] = dr.astype(dr_ref.dtype)
        res_ref[...] = ALPHA * dr
        dy_ref[...] = _dot_nt(dr, w_ref[...])

    return pl.pallas_call(
        body, name="outproj_ln_bwd", grid=(t // TM,),
        in_specs=[_rows(TM, D_MODEL), _rows(TM, D_MODEL), _rows(TM, 1), _const((1, D_MODEL)),
                  _const((D_MODEL, D_MODEL))],
        out_specs=[_rows(TM, D_MODEL), _rows(TM, D_MODEL), _rows(TM, D_MODEL), _const((1, D_MODEL)),
                   _const((1, D_MODEL))],
        out_shape=[jax.ShapeDtypeStruct((t, D_MODEL), MXU_DTYPE), jax.ShapeDtypeStruct((t, D_MODEL), F32),
                   jax.ShapeDtypeStruct((t, D_MODEL), F32), jax.ShapeDtypeStruct((1, D_MODEL), F32),
                   jax.ShapeDtypeStruct((1, D_MODEL), F32)],
        compiler_params=_cparams(("arbitrary",)),
    )(dh1, xhat, rstd, g, w)


def _loss_fwd_bwd(h, target):
    t = h.shape[0]

    def body(h_ref, t_ref, l_ref, dh_ref):
        i = pl.program_id(0)

        @pl.when(i == 0)
        def _():
            l_ref[...] = jnp.zeros_like(l_ref)

        e = h_ref[...] - t_ref[...]
        dh_ref[...] = e * (1.0 / D_MODEL)
        per_tok = jnp.mean(e * e, axis=-1, keepdims=True)
        l_ref[...] += 0.5 * jnp.sum(per_tok, axis=0, keepdims=True)

    return pl.pallas_call(
        body, name="loss_fwd_bwd", grid=(t // TM,),
        in_specs=[_rows(TM, D_MODEL), _rows(TM, D_MODEL)],
        out_specs=[_const((1, 1)), _rows(TM, D_MODEL)],
        out_shape=[jax.ShapeDtypeStruct((1, 1), F32), jax.ShapeDtypeStruct((t, D_MODEL), F32)],
        compiler_params=_cparams(("arbitrary",)),
    )(h, target)


def _pick_col(x, idx):
    lane = lax.broadcasted_iota(jnp.int32, x.shape, 1)
    return jnp.sum(jnp.where(lane == idx, x, 0.0), axis=1, keepdims=True)


def _pick_row(x, idx):
    sub = lax.broadcasted_iota(jnp.int32, x.shape, 0)
    return jnp.sum(jnp.where(sub == idx, x, 0.0), axis=0, keepdims=True)


def _conv_taps(pad_ref, w, tm, base):
    acc = w[0:1, :] * pad_ref[base:base + tm, :]
    for k in range(1, 4):
        acc = acc + w[k:k + 1, :] * pad_ref[base + k:base + k + tm, :]
    return acc


def _ssd_chunk_common(adt_c, tri):
    cs = _dot_f32(tri, adt_c)
    return cs, cs.T, jnp.exp(cs)


def _ssd_head_terms(cs, cst, ecs, dt_c, h, tri):
    cs_col = _pick_col(cs, h)
    cs_row = _pick_row(cst, h)
    dt_col = _pick_col(dt_c, h)
    cs_last = cs_col[SSD_CHUNK - 1:SSD_CHUNK, :]
    lmat = jnp.exp(jnp.where(tri > 0.0, cs_col - cs_row, -1e30))
    ecs_col = _pick_col(ecs, h)
    decay_col = jnp.exp(cs_last - cs_col)
    return cs_col, dt_col, cs_last, lmat, ecs_col, decay_col


def _ssd_fwd(proj, cw, cb, dtb, a_neg, d_lanes, nw):
    t = proj.shape[0]
    tm = SSD_TM
    nt = t // tm
    ncq = tm // SSD_CHUNK
    hb = tm // SUBLANES

    def body(xbc_ref, halo_ref, z_ref, dt_ref, cw_ref, cb_ref, dtb_ref, a_ref, d_ref, nw_ref,
             y_ref, yy_ref, st_ref, xpad, xact, state):
        i = pl.program_id(0)

        @pl.when(i == 0)
        def _():
            state[...] = jnp.zeros_like(state)

        xpad[0:SUBLANES, :] = jnp.where(i > 0, halo_ref[...], 0.0)
        xpad[SUBLANES:SUBLANES + tm, :] = xbc_ref[...]
        acc = cb_ref[...] + _conv_taps(xpad, cw_ref[...], tm, SUBLANES - 3)
        xact[...] = acc * _sigmoid(acc)
        dt = _softplus(dt_ref[...] + dtb_ref[...])
        adt = dt * a_ref[...]
        r_i = lax.broadcasted_iota(jnp.int32, (SSD_CHUNK, SSD_CHUNK), 0)
        c_i = lax.broadcasted_iota(jnp.int32, (SSD_CHUNK, SSD_CHUNK), 1)
        tri = (r_i >= c_i).astype(F32)
        lane1 = lax.broadcasted_iota(jnp.int32, (1, LANES), 1)
        for c in range(ncq):
            sl = slice(c * SSD_CHUNK, (c + 1) * SSD_CHUNK)
            dt_c = dt[sl]
            cs, cst, ecs = _ssd_chunk_common(adt[sl], tri)
            for g in range(2):
                bg = xact[sl, 512 + g * 128:512 + (g + 1) * 128]
                cg = xact[sl, 768 + g * 128:768 + (g + 1) * 128]
                cbm = _dot_nt(cg, bg)
                for pr in range(2):
                    pi = g * 2 + pr
                    psl = slice(pi * 128, (pi + 1) * 128)
                    xp = xact[sl, psl]
                    prev = state[pi]
                    st_ref[c, pi] = prev
                    yp = xp * d_ref[:, psl]
                    new_s = jnp.zeros((SSD_STATE, LANES), F32)
                    dec_lane = jnp.zeros((1, LANES), F32)
                    for hh in range(2):
                        h = g * 4 + pr * 2 + hh
                        lm = (lane1 >= 64) if hh else (lane1 < 64)
                        _, dt_col, cs_last, lmat, ecs_col, decay_col = _ssd_head_terms(cs, cst, ecs, dt_c, h, tri)
                        xdt = jnp.where(lm, xp, 0.0) * dt_col
                        yp = yp + _dot(cbm * lmat, xdt)
                        yp = yp + _dot(cg * ecs_col, jnp.where(lm, prev, 0.0))
                        new_s = new_s + _dot_tn(bg * decay_col, xdt)
                        dec_lane = dec_lane + jnp.where(lm, jnp.exp(cs_last), 0.0)
                    state[pi] = prev * dec_lane + new_s
                    yy_ref[sl, psl] = yp
        yy = yy_ref[...]
        z = z_ref[...]
        yg = yy * (z * _sigmoid(z))
        ms = jnp.mean(yg * yg, axis=-1, keepdims=True)
        y_ref[...] = yg * lax.rsqrt(ms + LN_EPS) * nw_ref[...]

    halo_map = lambda i: (jnp.maximum(i * hb - 1, 0), 0)
    return pl.pallas_call(
        body, name="ssd_fwd", grid=(nt,),
        in_specs=[pl.BlockSpec((tm, SSD_XBC), lambda i: (i, 0)), pl.BlockSpec((SUBLANES, SSD_XBC), halo_map),
                  pl.BlockSpec((tm, SSD_WIDTH), lambda i: (i, P_Z // SSD_WIDTH)),
                  pl.BlockSpec((tm, LANES), lambda i: (i, P_DT // LANES)),
                  _const((4, SSD_XBC)), _const((1, SSD_XBC)), _const((1, LANES)), _const((1, LANES)),
                  _const((1, SSD_WIDTH)), _const((1, SSD_WIDTH))],
        out_specs=[_rows(tm, SSD_WIDTH), _rows(tm, SSD_WIDTH),
                   pl.BlockSpec((ncq, 4, SSD_STATE, LANES), lambda i: (i, 0, 0, 0))],
        out_shape=[jax.ShapeDtypeStruct((t, SSD_WIDTH), F32), jax.ShapeDtypeStruct((t, SSD_WIDTH), F32),
                   jax.ShapeDtypeStruct((t // SSD_CHUNK, 4, SSD_STATE, LANES), F32)],
        scratch_shapes=[pltpu.VMEM((tm + SUBLANES, SSD_XBC), F32), pltpu.VMEM((tm, SSD_XBC), F32),
                        pltpu.VMEM((4, SSD_STATE, LANES), F32)],
        compiler_params=_cparams(("arbitrary",)),
    )(proj, proj, proj, proj, cw, cb, dtb, a_neg, d_lanes, nw)


def _ssd_bwd(dycat, proj, yy, states, cw, cb, dtb, a_neg, d_lanes, nw):
    t = proj.shape[0]
    tm = SSD_TM
    nt = t // tm
    ncq = tm // SSD_CHUNK
    hb = tm // SUBLANES

    def body(dy_ref, xbc_ref, halo_ref, z_ref, dt_ref, yy_ref, st_ref, cw_ref, cb_ref, dtb_ref, a_ref, d_ref, nw_ref,
             dxbc_ref, dz_ref, ddt_ref, dcw_ref, dcb_ref, ddtb_ref, da_ref, dd_ref, dnw_ref,
             xpad, xact, dxact, dpad, dstate, dnext):
        i = pl.program_id(0)

        @pl.when(i == 0)
        def _():
            for r in (dcw_ref, dcb_ref, ddtb_ref, da_ref, dd_ref, dnw_ref, dstate, dnext):
                r[...] = jnp.zeros_like(r)

        xpad[0:SUBLANES, :] = jnp.where(i < nt - 1, halo_ref[...], 0.0)
        xpad[SUBLANES:SUBLANES + tm, :] = xbc_ref[...]
        cw_v = cw_ref[...]
        acc = cb_ref[...] + _conv_taps(xpad, cw_v, tm, SUBLANES - 3)
        sig = _sigmoid(acc)
        xact[...] = acc * sig
        dt_raw = dt_ref[...] + dtb_ref[...]
        dt = _softplus(dt_raw)
        a_v = a_ref[...]
        adt = dt * a_v
        yy = yy_ref[...]
        z = z_ref[...]
        sz = _sigmoid(z)
        siluz = z * sz
        yg = yy * siluz
        ms = jnp.mean(yg * yg, axis=-1, keepdims=True)
        rinv = lax.rsqrt(ms + LN_EPS)
        dout = dy_ref[...]
        dnw_ref[...] += _sum0(dout * yg * rinv)
        dyn = dout * nw_ref[...]
        dyg = rinv * dyn - yg * (rinv * rinv * rinv) * jnp.mean(dyn * yg, axis=-1, keepdims=True)
        dyy = dyg * siluz
        dz_ref[...] = dyg * yy * (sz * (1.0 + z * (1.0 - sz)))
        dd_ref[...] += _sum0(dyy * xact[:, 0:SSD_WIDTH])

        r_i = lax.broadcasted_iota(jnp.int32, (SSD_CHUNK, SSD_CHUNK), 0)
        c_i = lax.broadcasted_iota(jnp.int32, (SSD_CHUNK, SSD_CHUNK), 1)
        tri = (r_i >= c_i).astype(F32)
        lane1 = lax.broadcasted_iota(jnp.int32, (1, LANES), 1)
        for c in reversed(range(ncq)):
            sl = slice(c * SSD_CHUNK, (c + 1) * SSD_CHUNK)
            dt_c = dt[sl]
            cs, cst, ecs = _ssd_chunk_common(adt[sl], tri)
            cacc = jnp.zeros((SSD_CHUNK, LANES), F32)
            racc = jnp.zeros((SSD_CHUNK, LANES), F32)
            ddtx = jnp.zeros((SSD_CHUNK, LANES), F32)
            for g in range(2):
                bg = xact[sl, 512 + g * 128:512 + (g + 1) * 128]
                cg = xact[sl, 768 + g * 128:768 + (g + 1) * 128]
                cbm = _dot_nt(cg, bg)
                dcb_m = jnp.zeros((SSD_CHUNK, SSD_CHUNK), F32)
                dbg = jnp.zeros((SSD_CHUNK, SSD_STATE), F32)
                dcg = jnp.zeros((SSD_CHUNK, SSD_STATE), F32)
                for pr in range(2):
                    pi = g * 2 + pr
                    psl = slice(pi * 128, (pi + 1) * 128)
                    xp = xact[sl, psl]
                    dyp = dyy[sl, psl]
                    prev = st_ref[c, pi]
                    ds_all = dstate[pi]
                    dxdt_p = jnp.zeros((SSD_CHUNK, LANES), F32)
                    dprev_new = jnp.zeros((SSD_STATE, LANES), F32)
                    dec_lane = jnp.zeros((1, LANES), F32)
                    dt_lanes = jnp.zeros((SSD_CHUNK, LANES), F32)
                    for hh in range(2):
                        h = g * 4 + pr * 2 + hh
                        lm = (lane1 >= 64) if hh else (lane1 < 64)
                        oh_l = (c_i == h).astype(F32)
                        oh_s = (r_i == h).astype(F32)
                        _, dt_col, cs_last, lmat, ecs_col, decay_col = _ssd_head_terms(cs, cst, ecs, dt_c, h, tri)
                        gm = cbm * lmat
                        xm = jnp.where(lm, xp, 0.0)
                        xdt = xm * dt_col
                        dym = jnp.where(lm, dyp, 0.0)
                        prevm = jnp.where(lm, prev, 0.0)
                        dsm = jnp.where(lm, ds_all, 0.0)
                        bdec = bg * decay_col
                        dxdt = _dot_tn(gm, dym) + _dot(bdec, dsm)
                        dxdt_p = dxdt_p + dxdt
                        ddtx = ddtx + oh_l * jnp.sum(dxdt * xm, axis=1, keepdims=True)
                        dt_lanes = dt_lanes + jnp.where(lm, dt_col, 0.0)
                        dgm = _dot_nt(dym, xdt)
                        dcb_m = dcb_m + dgm * lmat
                        w = dgm * gm
                        cacc = cacc + oh_l * jnp.sum(w, axis=1, keepdims=True)
                        racc = racc - oh_s * jnp.sum(w, axis=0, keepdims=True)
                        dce = _dot_nt(dym, prevm)
                        dcg = dcg + dce * ecs_col
                        cacc = cacc + oh_l * (jnp.sum(dce * cg, axis=1, keepdims=True) * ecs_col)
                        dprev_new = dprev_new + _dot_tn(cg * ecs_col, dym)
                        dbdec = _dot_nt(xdt, dsm)
                        dbg = dbg + dbdec * decay_col
                        dd = jnp.sum(dbdec * bg, axis=1, keepdims=True) * decay_col
                        cacc = cacc - oh_l * dd
                        cd = jnp.exp(cs_last)
                        dlast = jnp.sum(dd, axis=0, keepdims=True) + jnp.sum(
                            jnp.sum(dsm * prevm, axis=1, keepdims=True), axis=0, keepdims=True) * cd
                        cacc = cacc + jnp.where((r_i == SSD_CHUNK - 1) & (c_i == h), dlast, 0.0)
                        dec_lane = dec_lane + jnp.where(lm, cd, 0.0)
                    dstate[pi] = ds_all * dec_lane + dprev_new
                    dxact[sl, psl] = dxdt_p * dt_lanes + dyp * d_ref[:, psl]
                dcg = dcg + _dot(dcb_m, bg)
                dbg = dbg + _dot_tn(dcb_m, cg)
                dxact[sl, 512 + g * 128:512 + (g + 1) * 128] = dbg
                dxact[sl, 768 + g * 128:768 + (g + 1) * 128] = dcg
            dcs = cacc + racc.T
            dadt = _dot_f32((r_i <= c_i).astype(F32), dcs)
            ddt = dadt * a_v + ddtx
            da_ref[...] += _sum0(dadt * dt_c)
            ddt_raw = ddt * _sigmoid(dt_raw[sl])
            ddt_ref[sl, :] = ddt_raw
            ddtb_ref[...] += _sum0(ddt_raw)
        dacc = dxact[...] * (sig * (1.0 + acc * (1.0 - sig)))
        dcb_ref[...] += _sum0(dacc)
        for k in range(4):
            dcw_ref[k:k + 1, :] += _sum0(dacc * xpad[SUBLANES - 3 + k:SUBLANES - 3 + k + tm, :])
        dpad[0:tm, :] = dacc
        dpad[tm:tm + SUBLANES, :] = dnext[...]
        dx = cw_v[0:1, :] * dpad[3:3 + tm, :]
        for k in range(1, 4):
            dx = dx + cw_v[k:k + 1, :] * dpad[3 - k:3 - k + tm, :]
        dxbc_ref[...] = dx
        dnext[...] = dacc[0:SUBLANES, :]

    rev = lambda i: nt - 1 - i
    halo_map = lambda i: (jnp.maximum(rev(i) * hb - 1, 0), 0)
    rrow = lambda n, col=0: pl.BlockSpec((tm, n), lambda i: (rev(i), col))
    return pl.pallas_call(
        body, name="ssd_bwd", grid=(nt,),
        in_specs=[rrow(SSD_WIDTH), rrow(SSD_XBC), pl.BlockSpec((SUBLANES, SSD_XBC), halo_map),
                  rrow(SSD_WIDTH, P_Z // SSD_WIDTH), rrow(LANES, P_DT // LANES), rrow(SSD_WIDTH),
                  pl.BlockSpec((ncq, 4, SSD_STATE, LANES), lambda i: (rev(i), 0, 0, 0)),
                  _const((4, SSD_XBC)), _const((1, SSD_XBC)), _const((1, LANES)), _const((1, LANES)),
                  _const((1, SSD_WIDTH)), _const((1, SSD_WIDTH))],
        out_specs=[rrow(SSD_XBC), rrow(SSD_WIDTH), rrow(LANES), _const((SUBLANES, SSD_XBC)), _const((1, SSD_XBC)),
                   _const((1, LANES)), _const((1, LANES)), _const((1, SSD_WIDTH)), _const((1, SSD_WIDTH))],
        out_shape=[jax.ShapeDtypeStruct((t, SSD_XBC), F32), jax.ShapeDtypeStruct((t, SSD_WIDTH), F32),
                   jax.ShapeDtypeStruct((t, LANES), F32), jax.ShapeDtypeStruct((SUBLANES, SSD_XBC), F32),
                   jax.ShapeDtypeStruct((1, SSD_XBC), F32), jax.ShapeDtypeStruct((1, LANES), F32),
                   jax.ShapeDtypeStruct((1, LANES), F32), jax.ShapeDtypeStruct((1, SSD_WIDTH), F32),
                   jax.ShapeDtypeStruct((1, SSD_WIDTH), F32)],
        scratch_shapes=[pltpu.VMEM((tm + SUBLANES, SSD_XBC), F32), pltpu.VMEM((tm, SSD_XBC), F32),
                        pltpu.VMEM((tm, SSD_XBC), F32), pltpu.VMEM((tm + SUBLANES, SSD_XBC), F32),
                        pltpu.VMEM((4, SSD_STATE, LANES), F32), pltpu.VMEM((SUBLANES, SSD_XBC), F32)],
        compiler_params=_cparams(("arbitrary",)),
    )(dycat, proj, proj, proj, proj, yy, states, cw, cb, dtb, a_neg, d_lanes, nw)


def _cmul_add(ar, ai, br, bi, cr, ci):
    return ar + br * cr - bi * ci, ai + br * ci + bi * cr


def _s5_fwd(proj, bre, bim, cre, cim, d_skip, glu_w, glu_b, coef):
    t = proj.shape[0]
    tm = SCAN_TM
    ng = tm // SUBLANES

    def body(u_ref, bre_ref, bim_ref, cre_ref, cim_ref, d_ref, w_ref, b_ref, coef_ref,
             y_ref, y2_ref, hre_ref, him_ref, carry):
        i = pl.program_id(0)

        @pl.when(i == 0)
        def _():
            carry[...] = jnp.zeros_like(carry)

        u = u_ref[...]
        hre_ref[...] = _dot(u, bre_ref[...])
        him_ref[...] = _dot(u, bim_ref[...])

        def step(gi, car):
            cr_, ci_ = car
            rows = pl.ds(pl.multiple_of(gi * SUBLANES, SUBLANES), SUBLANES)
            r = hre_ref[rows, :]
            m = him_ref[rows, :]
            for k, sh in enumerate((1, 2, 4)):
                r, m = _cmul_add(r, m, coef_ref[k, 0], coef_ref[k, 1], pltpu.roll(r, sh, 0), pltpu.roll(m, sh, 0))
            r, m = _cmul_add(r, m, coef_ref[3, 0], coef_ref[3, 1], cr_, ci_)
            hre_ref[rows, :] = r
            him_ref[rows, :] = m
            return (jnp.broadcast_to(r[SUBLANES - 1:SUBLANES, :], r.shape),
                    jnp.broadcast_to(m[SUBLANES - 1:SUBLANES, :], m.shape))

        cr_, ci_ = lax.fori_loop(0, ng, step, (carry[0], carry[1]))
        carry[0] = cr_
        carry[1] = ci_
        y2 = _dot(hre_ref[...], cre_ref[...]) - _dot(him_ref[...], cim_ref[...]) + d_ref[...] * u
        y2_ref[...] = y2
        ya = _gelu(y2)
        y_ref[...] = ya * _sigmoid(_dot(ya, w_ref[...]) + b_ref[...])

    return pl.pallas_call(
        body, name="s5_fwd", grid=(t // tm,),
        in_specs=[pl.BlockSpec((tm, S5_WIDTH), lambda i: (i, P_U // S5_WIDTH)),
                  _const((S5_WIDTH, S5_NSTATE)), _const((S5_WIDTH, S5_NSTATE)), _const((S5_NSTATE, S5_WIDTH)),
                  _const((S5_NSTATE, S5_WIDTH)), _const((1, S5_WIDTH)), _const((S5_WIDTH, S5_WIDTH)),
                  _const((1, S5_WIDTH)), _const((5, 2, SUBLANES, S5_NSTATE))],
        out_specs=[_rows(tm, S5_WIDTH), _rows(tm, S5_WIDTH), _rows(tm, S5_NSTATE), _rows(tm, S5_NSTATE)],
        out_shape=[jax.ShapeDtypeStruct((t, S5_WIDTH), F32), jax.ShapeDtypeStruct((t, S5_WIDTH), F32),
                   jax.ShapeDtypeStruct((t, S5_NSTATE), F32), jax.ShapeDtypeStruct((t, S5_NSTATE), F32)],
        scratch_shapes=[pltpu.VMEM((2, SUBLANES, S5_NSTATE), F32)],
        compiler_params=_cparams(("arbitrary",)),
    )(proj, bre, bim, cre, cim, d_skip, glu_w, glu_b, coef)


def _s5_bwd(dycat, proj, y2, hre, him, bre, bim, cre, cim, d_skip, glu_w, glu_b, rcoef):
    t = proj.shape[0]
    tm = SCAN_TM
    nt = t // tm
    ng = tm // SUBLANES
    hb = tm // SUBLANES

    def body(dy_ref, u_ref, y2_ref, hre_ref, him_ref, hre_halo, him_halo, bre_ref, bim_ref, cre_ref, cim_ref, d_ref,
             w_ref, b_ref, coef_ref,
             du_ref, dbre_ref, dbim_ref, dcre_ref, dcim_ref, dlam_ref, dd_ref, dw_ref, dgb_ref,
             gre, gim, hpre, hpim, carry):
        i = pl.program_id(0)

        @pl.when(i == 0)
        def _():
            for r in (dbre_ref, dbim_ref, dcre_ref, dcim_ref, dlam_ref, dd_ref, dw_ref, dgb_ref, carry):
                r[...] = jnp.zeros_like(r)

        u = u_ref[...]
        y2 = y2_ref[...]
        dout = dy_ref[...]
        ya = _gelu(y2)
        sg = _sigmoid(_dot(ya, w_ref[...]) + b_ref[...])
        dv = dout * ya * sg * (1.0 - sg)
        dya = dout * sg + _dot_nt(dv, w_ref[...])
        dw_ref[...] += _dot_tn(ya, dv)
        dgb_ref[...] += _sum0(dv)
        dy2 = dya * _gelu_grad(y2)
        dd_ref[...] += _sum0(dy2 * u)
        hre_v = hre_ref[...]
        him_v = him_ref[...]
        dcre_ref[...] += _dot_tn(hre_v, dy2)
        dcim_ref[...] -= _dot_tn(him_v, dy2)
        gre[...] = _dot_nt(dy2, cre_ref[...])
        gim[...] = -_dot_nt(dy2, cim_ref[...])
        first = i == nt - 1
        hpre[0:SUBLANES, :] = jnp.where(first, 0.0, hre_halo[...])
        hpim[0:SUBLANES, :] = jnp.where(first, 0.0, him_halo[...])
        hpre[SUBLANES:SUBLANES + tm, :] = hre_v
        hpim[SUBLANES:SUBLANES + tm, :] = him_v
        row0 = lax.broadcasted_iota(jnp.int32, (SUBLANES, S5_NSTATE), 0) == 0

        def step(k, car):
            cr_, ci_, dlr, dli = car
            gi = ng - 1 - k
            rows = pl.ds(pl.multiple_of(gi * SUBLANES, SUBLANES), SUBLANES)
            nrows = pl.ds(pl.multiple_of(gi * SUBLANES + SUBLANES, SUBLANES), SUBLANES)
            r = gre[rows, :]
            m = gim[rows, :]
            for kk, sh in enumerate((1, 2, 4)):
                r, m = _cmul_add(r, m, coef_ref[kk, 0], coef_ref[kk, 1], pltpu.roll(r, SUBLANES - sh, 0),
                                 pltpu.roll(m, SUBLANES - sh, 0))
            r, m = _cmul_add(r, m, coef_ref[3, 0], coef_ref[3, 1], cr_, ci_)
            gre[rows, :] = r
            gim[rows, :] = m
            pr_ = hpre[rows, :]
            pm_ = hpim[rows, :]
            hr_ = jnp.where(row0, jnp.broadcast_to(pr_[SUBLANES - 1:SUBLANES, :], pr_.shape),
                            pltpu.roll(hpre[nrows, :], 1, 0))
            hm_ = jnp.where(row0, jnp.broadcast_to(pm_[SUBLANES - 1:SUBLANES, :], pm_.shape),
                            pltpu.roll(hpim[nrows, :], 1, 0))
            dlr = dlr + hr_ * r + hm_ * m
            dli = dli + hr_ * m - hm_ * r
            return (jnp.broadcast_to(r[0:1, :], r.shape), jnp.broadcast_to(m[0:1, :], m.shape), dlr, dli)

        z8 = jnp.zeros((SUBLANES, S5_NSTATE), F32)
        cr_, ci_, dlr, dli = lax.fori_loop(0, ng, step, (carry[0], carry[1], z8, z8))
        carry[0] = cr_
        carry[1] = ci_
        dlam_ref[0] += dlr
        dlam_ref[1] += dli
        g_re = gre[...]
        g_im = gim[...]
        du_ref[...] = dy2 * d_ref[...] + _dot_nt(g_re, bre_ref[...]) + _dot_nt(g_im, bim_ref[...])
        dbre_ref[...] += _dot_tn(u, g_re)
        dbim_ref[...] += _dot_tn(u, g_im)

    rev = lambda i: nt - 1 - i
    rrow = lambda n, col=0: pl.BlockSpec((tm, n), lambda i: (rev(i), col))
    halo = pl.BlockSpec((SUBLANES, S5_NSTATE), lambda i: (jnp.maximum(rev(i) * hb - 1, 0), 0))
    return pl.pallas_call(
        body, name="s5_bwd", grid=(nt,),
        in_specs=[rrow(S5_WIDTH, 512 // S5_WIDTH), rrow(S5_WIDTH, P_U // S5_WIDTH), rrow(S5_WIDTH),
                  rrow(S5_NSTATE), rrow(S5_NSTATE), halo, halo,
                  _const((S5_WIDTH, S5_NSTATE)), _const((S5_WIDTH, S5_NSTATE)), _const((S5_NSTATE, S5_WIDTH)),
                  _const((S5_NSTATE, S5_WIDTH)), _const((1, S5_WIDTH)), _const((S5_WIDTH, S5_WIDTH)),
                  _const((1, S5_WIDTH)), _const((5, 2, SUBLANES, S5_NSTATE))],
        out_specs=[rrow(S5_WIDTH), _const((S5_WIDTH, S5_NSTATE)), _const((S5_WIDTH, S5_NSTATE)),
                   _const((S5_NSTATE, S5_WIDTH)), _const((S5_NSTATE, S5_WIDTH)), _const((2, SUBLANES, S5_NSTATE)),
                   _const((1, S5_WIDTH)), _const((S5_WIDTH, S5_WIDTH)), _const((1, S5_WIDTH))],
        out_shape=[jax.ShapeDtypeStruct((t, S5_WIDTH), F32), jax.ShapeDtypeStruct((S5_WIDTH, S5_NSTATE), F32),
                   jax.ShapeDtypeStruct((S5_WIDTH, S5_NSTATE), F32), jax.ShapeDtypeStruct((S5_NSTATE, S5_WIDTH), F32),
                   jax.ShapeDtypeStruct((S5_NSTATE, S5_WIDTH), F32),
                   jax.ShapeDtypeStruct((2, SUBLANES, S5_NSTATE), F32), jax.ShapeDtypeStruct((1, S5_WIDTH), F32),
                   jax.ShapeDtypeStruct((S5_WIDTH, S5_WIDTH), F32), jax.ShapeDtypeStruct((1, S5_WIDTH), F32)],
        scratch_shapes=[pltpu.VMEM((tm, S5_NSTATE), F32), pltpu.VMEM((tm, S5_NSTATE), F32),
                        pltpu.VMEM((tm + SUBLANES, S5_NSTATE), F32), pltpu.VMEM((tm + SUBLANES, S5_NSTATE), F32),
                        pltpu.VMEM((2, SUBLANES, S5_NSTATE), F32)],
        compiler_params=_cparams(("arbitrary",)),
    )(dycat, proj, y2, hre, him, hre, him, bre, bim, cre, cim, d_skip, glu_w, glu_b, rcoef)


def _rg_gates(xc, wa, ba, wx, bx, nsp):
    r = _sigmoid(_dot(xc, wa) + ba)
    ig = _sigmoid(_dot(xc, wx) + bx)
    log_a = nsp * r
    a = jnp.exp(log_a)
    mult = jnp.sqrt(-_expm1(2.0 * log_a))
    return r, ig, a, mult


def _rg_fwd(proj, cw, cb, wa, ba, wx, bx, nsp):
    t = proj.shape[0]
    tm = SCAN_TM
    ng = tm // SUBLANES
    hb = tm // SUBLANES

    def body(x_ref, halo_ref, gt_ref, cw_ref, cb_ref, wa_ref, ba_ref, wx_ref, bx_ref, nsp_ref,
             y_ref, h_ref, xpad, abuf, carry):
        i = pl.program_id(0)

        @pl.when(i == 0)
        def _():
            carry[...] = jnp.zeros_like(carry)

        xpad[0:SUBLANES, :] = jnp.where(i > 0, halo_ref[...], 0.0)
        xpad[SUBLANES:SUBLANES + tm, :] = x_ref[...]
        xc = cb_ref[...] + _conv_taps(xpad, cw_ref[...], tm, SUBLANES - 3)
        _, ig, a, mult = _rg_gates(xc, wa_ref[...], ba_ref[...], wx_ref[...], bx_ref[...], nsp_ref[...])
        abuf[...] = a
        h_ref[...] = mult * (ig * xc)
        sub = lax.broadcasted_iota(jnp.int32, (SUBLANES, RG_WIDTH), 0)

        def step(gi, car):
            rows = pl.ds(pl.multiple_of(gi * SUBLANES, SUBLANES), SUBLANES)
            av = abuf[rows, :]
            bv = h_ref[rows, :]
            for sh in (1, 2, 4):
                m = sub >= sh
                bv = jnp.where(m, av * pltpu.roll(bv, sh, 0) + bv, bv)
                av = jnp.where(m, av * pltpu.roll(av, sh, 0), av)
            hv = bv + av * car
            h_ref[rows, :] = hv
            return jnp.broadcast_to(hv[SUBLANES - 1:SUBLANES, :], hv.shape)

        carry[...] = lax.fori_loop(0, ng, step, carry[...])
        y_ref[...] = h_ref[...] * _gelu(gt_ref[...])

    return pl.pallas_call(
        body, name="rg_fwd", grid=(t // tm,),
        in_specs=[pl.BlockSpec((tm, RG_WIDTH), lambda i: (i, P_XRG // RG_WIDTH)),
                  pl.BlockSpec((SUBLANES, RG_WIDTH), lambda i: (jnp.maximum(i * hb - 1, 0), P_XRG // RG_WIDTH)),
                  pl.BlockSpec((tm, RG_WIDTH), lambda i: (i, P_GRG // RG_WIDTH)),
                  _const((4, RG_WIDTH)), _const((1, RG_WIDTH)), _const((RG_WIDTH, RG_WIDTH)), _const((1, RG_WIDTH)),
                  _const((RG_WIDTH, RG_WIDTH)), _const((1, RG_WIDTH)), _const((1, RG_WIDTH))],
        out_specs=[_rows(tm, RG_WIDTH), _rows(tm, RG_WIDTH)],
        out_shape=[jax.ShapeDtypeStruct((t, RG_WIDTH), F32), jax.ShapeDtypeStruct((t, RG_WIDTH), F32)],
        scratch_shapes=[pltpu.VMEM((tm + SUBLANES, RG_WIDTH), F32), pltpu.VMEM((tm, RG_WIDTH), F32),
                        pltpu.VMEM((SUBLANES, RG_WIDTH), F32)],
        compiler_params=_cparams(("arbitrary",)),
    )(proj, proj, proj, cw, cb, wa, ba, wx, bx, nsp)


def _rg_bwd(dycat, proj, hs, cw, cb, wa, ba, wx, bx, nsp):
    t = proj.shape[0]
    tm = SCAN_TM
    nt = t // tm
    ng = tm // SUBLANES
    hb = tm // SUBLANES

    def body(dy_ref, x_ref, halo_ref, gt_ref, h_ref, h_halo, cw_ref, cb_ref, wa_ref, ba_ref, wx_ref, bx_ref, nsp_ref,
             dx_ref, dgt_ref, dcw_ref, dcb_ref, dwa_ref, dba_ref, dwx_ref, dbx_ref, dnsp_ref,
             xpad, abuf, gbuf, hpad, dabuf, dpad, carry, dnext):
        i = pl.program_id(0)

        @pl.when(i == 0)
        def _():
            for r in (dcw_ref, dcb_ref, dwa_ref, dba_ref, dwx_ref, dbx_ref, dnsp_ref, carry, dnext):
                r[...] = jnp.zeros_like(r)

        first = i == nt - 1
        xpad[0:SUBLANES, :] = jnp.where(first, 0.0, halo_ref[...])
        xpad[SUBLANES:SUBLANES + tm, :] = x_ref[...]
        cw_v = cw_ref[...]
        xc = cb_ref[...] + _conv_taps(xpad, cw_v, tm, SUBLANES - 3)
        nsp_v = nsp_ref[...]
        r, ig, a, mult = _rg_gates(xc, wa_ref[...], ba_ref[...], wx_ref[...], bx_ref[...], nsp_v)
        abuf[...] = a
        hv = h_ref[...]
        hpad[0:SUBLANES, :] = jnp.where(first, 0.0, h_halo[...])
        hpad[SUBLANES:SUBLANES + tm, :] = hv
        gt = gt_ref[...]
        dout = dy_ref[...]
        dgt_ref[...] = dout * hv * _gelu_grad(gt)
        gbuf[...] = dout * _gelu(gt)
        sub = lax.broadcasted_iota(jnp.int32, (SUBLANES, RG_WIDTH), 0)
        last_row = sub == SUBLANES - 1
        row0 = sub == 0

        def step(k, car):
            gi = ng - 1 - k
            rows = pl.ds(pl.multiple_of(gi * SUBLANES, SUBLANES), SUBLANES)
            nrows = pl.ds(pl.multiple_of(gi * SUBLANES + SUBLANES, SUBLANES), SUBLANES)
            av = abuf[rows, :]
            bv = gbuf[rows, :] + jnp.where(last_row, car, 0.0)
            ev = jnp.where(last_row, 0.0, pltpu.roll(av, SUBLANES - 1, 0))
            for sh in (1, 2, 4):
                m = sub < SUBLANES - sh
                bv = jnp.where(m, bv + ev * pltpu.roll(bv, SUBLANES - sh, 0), bv)
                ev = jnp.where(m, ev * pltpu.roll(ev, SUBLANES - sh, 0), 0.0)
            gbuf[rows, :] = bv
            pv = hpad[rows, :]
            hprev = jnp.where(row0, jnp.broadcast_to(pv[SUBLANES - 1:SUBLANES, :], pv.shape),
                              pltpu.roll(hpad[nrows, :], 1, 0))
            dabuf[rows, :] = bv * hprev
            return jnp.broadcast_to((av * bv)[0:1, :], bv.shape)

        carry[...] = lax.fori_loop(0, ng, step, carry[...])
        gv = gbuf[...]
        da = dabuf[...]
        ix = ig * xc
        dmult = gv * ix
        dig = gv * mult * xc
        dxc = gv * mult * ig
        dlog_a = da * a - dmult * (a * a) / mult
        dnsp_ref[...] += _sum0(dlog_a * r)
        dpr = dlog_a * nsp_v * r * (1.0 - r)
        dpi = dig * ig * (1.0 - ig)
        dxc = dxc + _dot_nt(dpr, wa_ref[...]) + _dot_nt(dpi, wx_ref[...])
        dwa_ref[...] += _dot_tn(xc, dpr)
        dwx_ref[...] += _dot_tn(xc, dpi)
        dba_ref[...] += _sum0(dpr)
        dbx_ref[...] += _sum0(dpi)
        dcb_ref[...] += _sum0(dxc)
        for k in range(4):
            dcw_ref[k:k + 1, :] += _sum0(dxc * xpad[SUBLANES - 3 + k:SUBLANES - 3 + k + tm, :])
        dpad[0:tm, :] = dxc
        dpad[tm:tm + SUBLANES, :] = dnext[...]
        dx = cw_v[0:1, :] * dpad[3:3 + tm, :]
        for k in range(1, 4):
            dx = dx + cw_v[k:k + 1, :] * dpad[3 - k:3 - k + tm, :]
        dx_ref[...] = dx
        dnext[...] = dxc[0:SUBLANES, :]

    rev = lambda i: nt - 1 - i
    rrow = lambda n, col=0: pl.BlockSpec((tm, n), lambda i: (rev(i), col))
    sq = _const((RG_WIDTH, RG_WIDTH))
    vec = _const((1, RG_WIDTH))
    return pl.pallas_call(
        body, name="rg_bwd", grid=(nt,),
        in_specs=[rrow(RG_WIDTH, 768 // RG_WIDTH), rrow(RG_WIDTH, P_XRG // RG_WIDTH),
                  pl.BlockSpec((SUBLANES, RG_WIDTH), lambda i: (jnp.maximum(rev(i) * hb - 1, 0), P_XRG // RG_WIDTH)),
                  rrow(RG_WIDTH, P_GRG // RG_WIDTH), rrow(RG_WIDTH),
                  pl.BlockSpec((SUBLANES, RG_WIDTH), lambda i: (jnp.maximum(rev(i) * hb - 1, 0), 0)),
                  _const((4, RG_WIDTH)), vec, sq, vec, sq, vec, vec],
        out_specs=[rrow(RG_WIDTH), rrow(RG_WIDTH), _const((SUBLANES, RG_WIDTH)), vec, sq, vec, sq, vec, vec],
        out_shape=[jax.ShapeDtypeStruct((t, RG_WIDTH), F32), jax.ShapeDtypeStruct((t, RG_WIDTH), F32),
                   jax.ShapeDtypeStruct((SUBLANES, RG_WIDTH), F32), jax.ShapeDtypeStruct((1, RG_WIDTH), F32),
                   jax.ShapeDtypeStruct((RG_WIDTH, RG_WIDTH), F32), jax.ShapeDtypeStruct((1, RG_WIDTH), F32),
                   jax.ShapeDtypeStruct((RG_WIDTH, RG_WIDTH), F32), jax.ShapeDtypeStruct((1, RG_WIDTH), F32),
                   jax.ShapeDtypeStruct((1, RG_WIDTH), F32)],
        scratch_shapes=[pltpu.VMEM((tm + SUBLANES, RG_WIDTH), F32), pltpu.VMEM((tm, RG_WIDTH), F32),
                        pltpu.VMEM((tm, RG_WIDTH), F32), pltpu.VMEM((tm + SUBLANES, RG_WIDTH), F32),
                        pltpu.VMEM((tm, RG_WIDTH), F32), pltpu.VMEM((tm + SUBLANES, RG_WIDTH), F32),
                        pltpu.VMEM((SUBLANES, RG_WIDTH), F32), pltpu.VMEM((SUBLANES, RG_WIDTH), F32)],
        compiler_params=_cparams(("arbitrary",)),
    )(dycat, proj, proj, proj, hs, hs, cw, cb, wa, ba, wx, bx, nsp)


def _block_diag(blocks):
    g, a, b = blocks.shape
    eye = jnp.eye(g, dtype=blocks.dtype)
    return (eye[:, None, :, None] * blocks[:, :, None, :]).reshape(g * a, g * b)


def _block_diag_extract(m, g):
    a, b = m.shape[0] // g, m.shape[1] // g
    m4 = m.reshape(g, a, g, b)
    idx = jnp.arange(g)
    return m4[idx, :, idx, :]


def _s5_prepare(lam_re, lam_im, log_step, b_re, b_im, c_re, c_im):
    step = jnp.exp(log_step)[:, None]
    mag = jnp.exp(lam_re * step)
    lbr = mag * jnp.cos(lam_im * step)
    lbi = mag * jnp.sin(lam_im * step)
    nr, ni = lbr - 1.0, lbi
    den = lam_re * lam_re + lam_im * lam_im
    cr = (nr * lam_re + ni * lam_im) / den
    ci = (ni * lam_re - nr * lam_im) / den
    bbr = cr[..., None] * b_re - ci[..., None] * b_im
    bbi = cr[..., None] * b_im + ci[..., None] * b_re
    bre = _block_diag(jnp.swapaxes(bbr, 1, 2))
    bim = _block_diag(jnp.swapaxes(bbi, 1, 2))
    cre = _block_diag(jnp.swapaxes(c_re, 1, 2))
    cim = _block_diag(jnp.swapaxes(c_im, 1, 2))
    return lbr.reshape(-1), lbi.reshape(-1), bre, bim, cre, cim


def _s5_scan_coef(lbr, lbi, reverse):
    if reverse:
        lbi = -lbi
    pr, pi = [lbr], [lbi]
    for _ in range(7):
        pr, pi = pr + [pr[-1] * lbr - pi[-1] * lbi], pi + [pr[-1] * lbi + pi[-1] * lbr]
    row = jnp.arange(SUBLANES)[:, None]
    tabs = []
    for sh in (1, 2, 4):
        keep = (row < SUBLANES - sh) if reverse else (row >= sh)
        tabs.append(jnp.stack([jnp.where(keep, pr[sh - 1][None, :], 0.0), jnp.where(keep, pi[sh - 1][None, :], 0.0)]))
    powr = jnp.stack(pr)
    powi = jnp.stack(pi)
    if reverse:
        powr, powi = powr[::-1], powi[::-1]
    tabs.append(jnp.stack([powr, powi]))
    tabs.append(jnp.zeros_like(tabs[-1]))
    return jnp.stack(tabs).astype(F32)


def _xy_peers():
    x, y, c = lax.axis_index("x"), lax.axis_index("y"), lax.axis_index("c")
    return x, y, c, [(1 - x, y), (x, 1 - y), (1 - x, 1 - y)]


def _hbm():
    return pl.BlockSpec(memory_space=pl.ANY)


def _xy_allgather(buf, *, name):
    n, w = buf.shape

    def body(x_ref, out_ref, send_sems, recv_sems, local_sem):
        x, y, c, peers = _xy_peers()
        me = 2 * x + y
        own = pltpu.make_async_copy(x_ref, out_ref.at[me], local_sem)
        own.start()
        sends = []
        for k, (px, py) in enumerate(peers):
            cp = pltpu.make_async_remote_copy(src_ref=x_ref, dst_ref=out_ref.at[me], send_sem=send_sems.at[k],
                                              recv_sem=recv_sems.at[k], device_id=(px, py, c), device_id_type=MESH)
            cp.start()
            sends.append(cp)
        for k, (px, py) in enumerate(peers):
            pltpu.make_async_remote_copy(src_ref=x_ref, dst_ref=out_ref.at[2 * px + py], send_sem=send_sems.at[k],
                                         recv_sem=recv_sems.at[k], device_id=(px, py, c),
                                         device_id_type=MESH).wait_recv()
        for cp in sends:
            cp.wait_send()
        own.wait()

    return pl.pallas_call(
        body, name=name, in_specs=[_hbm()], out_specs=_hbm(),
        out_shape=jax.ShapeDtypeStruct((4, n, w), buf.dtype),
        scratch_shapes=[pltpu.SemaphoreType.DMA((3,)), pltpu.SemaphoreType.DMA((3,)), pltpu.SemaphoreType.DMA],
    )(buf)


def _remote(src, dst, send_sem, recv_sem, dev):
    return pltpu.make_async_remote_copy(src_ref=src, dst_ref=dst, send_sem=send_sem, recv_sem=recv_sem,
                                        device_id=dev, device_id_type=MESH)


LAYER_GATHERED = (
    ("ssd_conv_w", (4, 256), 1), ("rg_conv_w", (4, LANES), 1),
    ("w_in", (1024, W_IN_PAD), 1), ("s5_glu_w", (64, 256), 0), ("w_out", (256, 1024), 0), ("xa_wq", (256, 1024), 0),
    ("xa_wk", (256, 1024), 0), ("xa_wv", (256, 1024), 0), ("xa_wo", (256, 1024), 0), ("mlp_w1", (1024, 1024), 1),
    ("mlp_w2", (1024, 1024), 0),
)
N_GATHERED = len(LAYER_GATHERED)
WAIT_GROUPS = ((0, 1, 2, 3), (4,), (5, 6, 7, 8), (9, 10))
RG_CONV_SHARD = RG_WIDTH // 4
N_GATHER_COPIES = 3 * N_GATHERED * DEPTH


def _gather_part(ref, t, pos):
    _, shp, ax = LAYER_GATHERED[t % N_GATHERED]
    idx = tuple(pl.ds(pos * shp[ax], shp[ax]) if d == ax else slice(None) for d in range(len(shp)))
    return ref.at[idx]


def _gather_start(shards):
    n = len(shards)
    lands = []
    for t, s in enumerate(shards):
        _, shp, ax = LAYER_GATHERED[t % N_GATHERED]
        full = shp[:ax] + (4 * shp[ax],) + shp[ax + 1:]
        lands.append(pltpu.with_memory_space_constraint(lax.empty(full, s.dtype), pltpu.HBM))

    def body(*refs):
        srcs, lnds = refs[:n], refs[n:2 * n]
        send_sems, recv_sems, local_sems = refs[2 * n:2 * n + 3]
        token = refs[-1]
        x, y, c, peers = _xy_peers()
        me = 2 * x + y
        for t in range(n):
            for k, (px, py) in enumerate(peers):
                _remote(srcs[t], _gather_part(lnds[t], t, me), send_sems.at[k * n + t], recv_sems.at[k * n + t],
                        (px, py, c)).start()
            pltpu.make_async_copy(srcs[t], _gather_part(lnds[t], t, me), local_sems.at[t]).start()
        token[...] = jnp.zeros_like(token)

    hbm = pl.BlockSpec(memory_space=pltpu.HBM)
    sem = pl.BlockSpec(memory_space=pltpu.SEMAPHORE)
    outs = pl.pallas_call(
        body, name="weights_gather_start", in_specs=[hbm] * (2 * n),
        out_shape=(pltpu.SemaphoreType.DMA((3 * n,)), pltpu.SemaphoreType.DMA((3 * n,)),
                   pltpu.SemaphoreType.DMA((n,)),
                   *[pltpu.HBM(s.shape, s.dtype) for s in shards], *[pltpu.HBM(a.shape, a.dtype) for a in lands],
                   jax.ShapeDtypeStruct((SUBLANES, LANES), F32)),
        out_specs=(sem, sem, sem, *[hbm] * (2 * n), pl.BlockSpec(memory_space=pltpu.VMEM)),
        input_output_aliases={i: 3 + i for i in range(2 * n)},
        compiler_params=pltpu.CompilerParams(has_side_effects=pltpu.SideEffectType.DATAFLOW_SIDE_EFFECTING),
    )(*[pltpu.with_memory_space_constraint(s, pltpu.HBM) for s in shards], *lands)
    return outs[0], outs[1], outs[2], outs[3:3 + n], outs[3 + n:3 + 2 * n], outs[-1]


def _gather_wait(handle, ts, after, *, name):
    send_sems, recv_sems, local_sems, src_thru, land_thru, _ = handle
    n = len(src_thru)
    m = len(ts)

    def body(*refs):
        srcs, lnds = refs[:m], refs[m:2 * m]
        ssem, rsem, lsem = refs[2 * m:2 * m + 3]
        x, y, c, peers = _xy_peers()
        me = 2 * x + y
        for i, t in enumerate(ts):
            for k, (px, py) in enumerate(peers):
                cp = _remote(srcs[i], _gather_part(lnds[i], t, 2 * px + py), ssem.at[k * n + t], rsem.at[k * n + t],
                             (px, py, c))
                cp.wait_send()
                cp.wait_recv()
            pltpu.make_async_copy(srcs[i], _gather_part(lnds[i], t, me), lsem.at[t]).wait()

    hbm = pl.BlockSpec(memory_space=pltpu.HBM)
    sem = pl.BlockSpec(memory_space=pltpu.SEMAPHORE)
    args = [src_thru[t] for t in ts] + [land_thru[t] for t in ts]
    outs = pl.pallas_call(
        body, name=name, in_specs=[hbm] * (2 * m) + [sem, sem, sem, pl.BlockSpec(memory_space=pl.ANY)],
        out_shape=[pltpu.HBM(a.shape, a.dtype) for a in args], out_specs=[hbm] * (2 * m),
        input_output_aliases={i: i for i in range(2 * m)},
        compiler_params=pltpu.CompilerParams(has_side_effects=pltpu.SideEffectType.DATAFLOW_SIDE_EFFECTING),
    )(*args, send_sems, recv_sems, local_sems, after)
    return outs[:m], outs[m:]


C_CHUNKS = 4
XY_CHUNKS = 4
EW_ROWS = 512


def _c_exchange(g, part):
    _, n, w = g.shape
    n2 = n // 2
    n4 = n // 4
    rq = n4 // C_CHUNKS

    def body(g_ref, got_ref, send_sems, recv_sems):
        x, y, c = lax.axis_index("x"), lax.axis_index("y"), lax.axis_index("c")
        cps = []
        for s in range(4):
            for q in range(C_CHUNKS):
                k = s * C_CHUNKS + q
                cp = _remote(g_ref.at[s, pl.ds((1 - c) * n2 + part * n4 + q * rq, rq), :],
                             got_ref.at[s, pl.ds(q * rq, rq), :], send_sems.at[k], recv_sems.at[k], (x, y, 1 - c))
                cp.start()
                cps.append(cp)
        for cp in cps:
            cp.wait_recv()
        for cp in cps:
            cp.wait_send()

    return pl.pallas_call(
        body, name="grad_c_exchange_%d" % part, in_specs=[_hbm()], out_specs=_hbm(),
        out_shape=jax.ShapeDtypeStruct((4, n4, w), g.dtype),
        scratch_shapes=[pltpu.SemaphoreType.DMA((4 * C_CHUNKS,)), pltpu.SemaphoreType.DMA((4 * C_CHUNKS,))],
    )(g)


XFER_DTYPE = jnp.bfloat16


def _add_own_quarter(g, got, c_arr, part):
    _, n, w = g.shape
    n4 = n // 4
    nb = n4 // EW_ROWS

    def body(c_ref, a_ref, b_ref, o_ref, t_ref):
        sm = a_ref[...] + b_ref[...]
        o_ref[...] = sm.astype(o_ref.dtype)

        @pl.when(pl.program_id(1) == nb - 1)
        def _():
            t_ref[...] = sm[:, EW_ROWS - MISC_ROWS:, :]

    grid_spec = pltpu.PrefetchScalarGridSpec(
        num_scalar_prefetch=1, grid=(4, nb),
        in_specs=[pl.BlockSpec((1, EW_ROWS, w), lambda s, i, c: (s, (2 * c[0] + part) * nb + i, 0)),
                  pl.BlockSpec((1, EW_ROWS, w), lambda s, i, c: (s, i, 0))],
        out_specs=[pl.BlockSpec((1, EW_ROWS, w), lambda s, i, c: (s, i, 0)),
                   pl.BlockSpec((1, MISC_ROWS, w), lambda s, i, c: (s, 0, 0))])
    return pl.pallas_call(
        body, name="grad_add_quarters", grid_spec=grid_spec,
        out_shape=[jax.ShapeDtypeStruct((4, n4, w), XFER_DTYPE), jax.ShapeDtypeStruct((4, MISC_ROWS, w), g.dtype)],
        compiler_params=_cparams(("arbitrary", "arbitrary")),
    )(c_arr, g, got)


def _xy_exchange(arrs):
    na = len(arrs)
    pieces = []
    for a, arr in enumerate(arrs):
        nch = XY_CHUNKS if a == 0 else 1
        rq = arr.shape[1] // nch
        pieces += [(a, pl.ds(q * rq, rq)) for q in range(nch)]
    npc = len(pieces)

    def body(*refs):
        ins, outs = refs[:na], refs[na:2 * na]
        send_sems, recv_sems, local_sems = refs[2 * na:]
        x, y, c, peers = _xy_peers()
        me = 2 * x + y
        own = []
        for j, (a, rows) in enumerate(pieces):
            cp = pltpu.make_async_copy(ins[a].at[me, rows, :], outs[a].at[me, rows, :], local_sems.at[j])
            cp.start()
            own.append(cp)
        sends = []
        for k, (px, py) in enumerate(peers):
            for j, (a, rows) in enumerate(pieces):
                cp = _remote(ins[a].at[2 * px + py, rows, :], outs[a].at[me, rows, :], send_sems.at[k * npc + j],
                             recv_sems.at[k * npc + j], (px, py, c))
                cp.start()
                sends.append(cp)
        for k, (px, py) in enumerate(peers):
            for j, (a, rows) in enumerate(pieces):
                _remote(ins[a].at[me, rows, :], outs[a].at[2 * px + py, rows, :], send_sems.at[k * npc + j],
                        recv_sems.at[k * npc + j], (px, py, c)).wait_recv()
        for cp in sends:
            cp.wait_send()
        for cp in own:
            cp.wait()

    return pl.pallas_call(
        body, name="grad_xy_exchange", in_specs=[_hbm()] * na, out_specs=[_hbm()] * na,
        out_shape=[jax.ShapeDtypeStruct(a.shape, a.dtype) for a in arrs],
        scratch_shapes=[pltpu.SemaphoreType.DMA((3 * npc,)), pltpu.SemaphoreType.DMA((3 * npc,)),
                        pltpu.SemaphoreType.DMA((npc,))],
    )(*arrs)


def _xy_pieces(arrs):
    pieces = []
    for a, arr in enumerate(arrs):
        nch = XY_CHUNKS if a == 0 else 1
        rq = arr.shape[1] // nch
        pieces += [(a, pl.ds(q * rq, rq)) for q in range(nch)]
    return pieces


def _xy_start(arrs):
    na = len(arrs)
    pieces = _xy_pieces(arrs)
    npc = len(pieces)
    lands = [pltpu.with_memory_space_constraint(lax.empty(a.shape, a.dtype), pltpu.HBM) for a in arrs]

    def body(*refs):
        ins, outs = refs[:na], refs[na:2 * na]
        send_sems, recv_sems, local_sems = refs[2 * na:2 * na + 3]
        token = refs[-1]
        x, y, c, peers = _xy_peers()
        me = 2 * x + y
        for k, (px, py) in enumerate(peers):
            for j, (a, rows) in enumerate(pieces):
                _remote(ins[a].at[2 * px + py, rows, :], outs[a].at[me, rows, :], send_sems.at[k * npc + j],
                        recv_sems.at[k * npc + j], (px, py, c)).start()
        for j, (a, rows) in enumerate(pieces):
            pltpu.make_async_copy(ins[a].at[me, rows, :], outs[a].at[me, rows, :], local_sems.at[j]).start()
        token[...] = jnp.zeros_like(token)

    hbm = pl.BlockSpec(memory_space=pltpu.HBM)
    sem = pl.BlockSpec(memory_space=pltpu.SEMAPHORE)
    outs = pl.pallas_call(
        body, name="grad_xy_start", in_specs=[hbm] * (2 * na),
        out_shape=(pltpu.SemaphoreType.DMA((3 * npc,)), pltpu.SemaphoreType.DMA((3 * npc,)),
                   pltpu.SemaphoreType.DMA((npc,)),
                   *[pltpu.HBM(a.shape, a.dtype) for a in arrs], *[pltpu.HBM(a.shape, a.dtype) for a in arrs],
                   jax.ShapeDtypeStruct((SUBLANES, LANES), F32)),
        out_specs=(sem, sem, sem, *[hbm] * (2 * na), pl.BlockSpec(memory_space=pltpu.VMEM)),
        input_output_aliases={i: 3 + i for i in range(2 * na)},
        compiler_params=pltpu.CompilerParams(has_side_effects=pltpu.SideEffectType.DATAFLOW_SIDE_EFFECTING),
    )(*[pltpu.with_memory_space_constraint(a, pltpu.HBM) for a in arrs], *lands)
    return outs[0], outs[1], outs[2], outs[3:3 + na], outs[3 + na:3 + 2 * na]


def _xy_wait(handle, after):
    send_sems, recv_sems, local_sems, src_thru, land_thru = handle
    na = len(src_thru)
    pieces = _xy_pieces(src_thru)
    npc = len(pieces)

    def body(*refs):
        ins, outs = refs[:na], refs[na:2 * na]
        ssem, rsem, lsem = refs[2 * na:2 * na + 3]
        x, y, c, peers = _xy_peers()
        me = 2 * x + y
        for k, (px, py) in enumerate(peers):
            for j, (a, rows) in enumerate(pieces):
                cp = _remote(ins[a].at[me, rows, :], outs[a].at[2 * px + py, rows, :], ssem.at[k * npc + j],
                             rsem.at[k * npc + j], (px, py, c))
                cp.wait_send()
                cp.wait_recv()
        for j, (a, rows) in enumerate(pieces):
            pltpu.make_async_copy(ins[a].at[me, rows, :], outs[a].at[me, rows, :], lsem.at[j]).wait()

    hbm = pl.BlockSpec(memory_space=pltpu.HBM)
    sem = pl.BlockSpec(memory_space=pltpu.SEMAPHORE)
    args = list(src_thru) + list(land_thru)
    outs = pl.pallas_call(
        body, name="grad_xy_wait", in_specs=[hbm] * (2 * na) + [sem, sem, sem, pl.BlockSpec(memory_space=pl.ANY)],
        out_shape=[pltpu.HBM(a.shape, a.dtype) for a in args], out_specs=[hbm] * (2 * na),
        input_output_aliases={i: i for i in range(2 * na)},
        compiler_params=pltpu.CompilerParams(has_side_effects=pltpu.SideEffectType.DATAFLOW_SIDE_EFFECTING),
    )(*args, send_sems, recv_sems, local_sems, after)
    return outs[na:]


def _sum4_into_quarter(r, rt, c_arr, part, fbuf):
    _, n4, w = r.shape
    nb = n4 // EW_ROWS

    def body(c_ref, r_ref, t_ref, *rest):
        o_ref = rest[-1]
        o_ref[...] = ((r_ref[0].astype(F32) + r_ref[1].astype(F32)) + r_ref[2].astype(F32)) + r_ref[3].astype(F32)

        @pl.when(pl.program_id(0) == nb - 1)
        def _():
            o_ref[EW_ROWS - MISC_ROWS:, :] = ((t_ref[0] + t_ref[1]) + t_ref[2]) + t_ref[3]

    in_specs = [pl.BlockSpec((4, EW_ROWS, w), lambda i, c: (0, i, 0)),
                pl.BlockSpec((4, MISC_ROWS, w), lambda i, c: (0, 0, 0))]
    args = [c_arr, r, rt]
    aliases = {}
    if fbuf is not None:
        in_specs.append(pl.BlockSpec(memory_space=pl.ANY))
        args.append(fbuf)
        aliases = {3: 0}
    grid_spec = pltpu.PrefetchScalarGridSpec(
        num_scalar_prefetch=1, grid=(nb,), in_specs=in_specs,
        out_specs=pl.BlockSpec((EW_ROWS, w), lambda i, c: ((2 * c[0] + part) * nb + i, 0)))
    return pl.pallas_call(
        body, name="grad_sum4", grid_spec=grid_spec, out_shape=jax.ShapeDtypeStruct((4 * n4, w), F32),
        input_output_aliases=aliases, compiler_params=_cparams(("arbitrary",)),
    )(*args)


C_GATHER_CHUNKS = 8


def _c_allgather_halves(f):
    n, w = f.shape
    n2 = n // 2
    rq = n2 // C_GATHER_CHUNKS

    def body(f_ref, out_ref, send_sems, recv_sems):
        x, y, c = lax.axis_index("x"), lax.axis_index("y"), lax.axis_index("c")
        sends = []
        for q in range(C_GATHER_CHUNKS):
            rows = pl.ds(c * n2 + q * rq, rq)
            cp = _remote(f_ref.at[rows, :], out_ref.at[rows, :], send_sems.at[q], recv_sems.at[q], (x, y, 1 - c))
            cp.start()
            sends.append(cp)
        for q in range(C_GATHER_CHUNKS):
            rows = pl.ds((1 - c) * n2 + q * rq, rq)
            _remote(f_ref.at[rows, :], out_ref.at[rows, :], send_sems.at[q], recv_sems.at[q],
                    (x, y, 1 - c)).wait_recv()
        for cp in sends:
            cp.wait_send()

    return pl.pallas_call(
        body, name="grad_c_allgather", in_specs=[_hbm()], out_specs=_hbm(), input_output_aliases={0: 0},
        out_shape=jax.ShapeDtypeStruct((n, w), f.dtype),
        scratch_shapes=[pltpu.SemaphoreType.DMA((C_GATHER_CHUNKS,)), pltpu.SemaphoreType.DMA((C_GATHER_CHUNKS,))],
    )(f)


def _adamw(w, m, v, g, g_rows=None):
    shape = w.shape
    cols = shape[-1]
    rows = int(math.prod(shape)) // cols
    tr = 256 if rows % 256 == 0 else rows
    from_flat = g_rows is not None
    c1 = 1.0 / (1.0 - ADAM_B1 ** ADAM_STEP)
    c2 = 1.0 / (1.0 - ADAM_B2 ** ADAM_STEP)

    def body(w_ref, m_ref, v_ref, g_ref, *outs):
        gg = g_ref[...]
        nm = ADAM_B1 * m_ref[...] + (1.0 - ADAM_B1) * gg
        nv = ADAM_B2 * v_ref[...] + (1.0 - ADAM_B2) * (gg * gg)
        if from_flat:
            outs[0][...] = gg
        d_ref, nm_ref, nv_ref = outs[-3:]
        nm_ref[...] = nm
        nv_ref[...] = nv
        d_ref[...] = -ADAM_LR * ((nm * c1) / (jnp.sqrt(nv * c2) + ADAM_EPS) + ADAM_WD * w_ref[...])

    spec = pl.BlockSpec((tr, cols), lambda i: (i, 0))
    if from_flat:
        nbl = rows // DEPTH // tr
        assert cols == FLAT and all(r % tr == 0 for r in g_rows) and len(g_rows) == DEPTH == 2
        b0, b1 = g_rows[0] // tr, g_rows[1] // tr
        g_spec = pl.BlockSpec((tr, cols), lambda i: (jnp.where(i < nbl, b0 + i, b1 + i - nbl), 0))
        g_arg = g
    else:
        g_spec = spec
        g_arg = g.reshape(rows, cols)
    n_out = 4 if from_flat else 3
    sds = jax.ShapeDtypeStruct((rows, cols), F32)
    outs = pl.pallas_call(
        body, name="adamw", grid=(rows // tr,), in_specs=[spec, spec, spec, g_spec], out_specs=[spec] * n_out,
        out_shape=[sds] * n_out, compiler_params=_cparams(("arbitrary",)),
    )(w.reshape(rows, cols), m.reshape(rows, cols), v.reshape(rows, cols), g_arg)
    outs = [o.reshape(shape) for o in outs]
    return outs if from_flat else [g] + outs


SMALL_SHARDED = (("s5_glu_w", (2, 64, 256), 1), ("ssd_conv_w", (2, 4, 256), 2), ("rg_conv_w", (2, 4, 64), 2))
REPLICATED = (
    ("ssd_conv_b", (2, 1024)), ("ssd_dt_bias", (2, 8)), ("ssd_a_log", (2, 8)), ("ssd_d", (2, 8)),
    ("ssd_norm_w", (2, 512)), ("s5_lam_re", (2, 16, 64)), ("s5_lam_im", (2, 16, 64)), ("s5_log_step", (2, 16)),
    ("s5_b_re", (2, 16, 64, 16)), ("s5_b_im", (2, 16, 64, 16)), ("s5_c_re", (2, 16, 16, 64)),
    ("s5_c_im", (2, 16, 16, 64)), ("s5_d", (2, 256)), ("s5_glu_b", (2, 256)), ("rg_conv_b", (2, 256)),
    ("rg_wa", (2, 4, 64, 64)), ("rg_ba", (2, 4, 64)), ("rg_wx", (2, 4, 64, 64)), ("rg_bx", (2, 4, 64)),
    ("rg_lambda", (2, 256)), ("ln1_g", (2, 1024)), ("ln1_b", (2, 1024)), ("ln2_g", (2, 1024)), ("ln2_b", (2, 1024)),
    ("ln3_g", (2, 1024)), ("ln3_b", (2, 1024)),
)
WEIGHT_ORDER = (
    "w_in", "w_out", "ssd_conv_w", "ssd_conv_b", "ssd_dt_bias", "ssd_a_log", "ssd_d", "ssd_norm_w", "s5_lam_re",
    "s5_lam_im", "s5_log_step", "s5_b_re", "s5_b_im", "s5_c_re", "s5_c_im", "s5_d", "s5_glu_w", "s5_glu_b",
    "rg_conv_w", "rg_conv_b", "rg_wa", "rg_ba", "rg_wx", "rg_bx", "rg_lambda", "ln1_g", "ln1_b", "xa_wq", "xa_wk",
    "xa_wv", "xa_wo", "ln2_g", "ln2_b", "mlp_w1", "mlp_w2", "ln3_g", "ln3_b",
)


def _size(shape):
    return int(math.prod(shape))


def _round_up(a, b):
    return (a + b - 1) // b * b


SMALL_ELEMS = sum(_size(s) for _, s, _ in SMALL_SHARDED)
REP_ELEMS = sum(_size(s) for _, s in REPLICATED)
REP_QROWS = _round_up(-(-REP_ELEMS // (4 * FLAT)), 8)
assert SMALL_ELEMS <= MISC_REP_ROW * FLAT and MISC_REP_ROW + REP_QROWS <= MISC_ROWS


def _pack_shards(tensors, names_shapes):
    return jnp.concatenate([tensors[n].reshape(-1) for n, *_ in names_shapes])


def _unpack(flat, names_shapes):
    out, off = {}, 0
    for n, s, *_ in names_shapes:
        out[n] = flat[off:off + _size(s)].reshape(s)
        off += _size(s)
    return out


def _split_shards(full, names_shapes):
    rows = []
    for k in range(4):
        parts = []
        for n, s, ax in names_shapes:
            w = s[ax]
            parts.append(lax.slice_in_dim(full[n], k * w, (k + 1) * w, axis=ax).reshape(-1))
        rows.append(jnp.concatenate(parts))
    return jnp.stack(rows)


def _pack_cols(w):
    pad = jnp.zeros((w.shape[0], LANES - SSD_HEADS), w.dtype)
    return jnp.concatenate([w[:, O_XBC:O_XBC + 1024], w[:, O_Z:O_Z + 512], w[:, O_U:O_U + 256],
                            w[:, O_XRG:O_XRG + 256], w[:, O_GRG:O_GRG + 256], w[:, O_DT:O_DT + 8], pad], axis=1)


def _unpack_cols(w):
    return jnp.concatenate([w[:, P_Z:P_Z + 512], w[:, P_XBC:P_XBC + 1024], w[:, P_DT:P_DT + 8],
                            w[:, P_U:P_U + 256], w[:, P_XRG:P_XRG + 256], w[:, P_GRG:P_GRG + 256]], axis=1)


def _lanes(v, width):
    return jnp.pad(v, (0, width - v.shape[0])).reshape(1, width)


def _layer_params(rep, l):
    p = {}
    p["ssd_cb"] = rep["ssd_conv_b"][l].reshape(1, -1)
    p["ssd_dtb"] = _lanes(rep["ssd_dt_bias"][l], LANES)
    p["ssd_a"] = _lanes(-jnp.exp(rep["ssd_a_log"][l]), LANES)
    p["ssd_d"] = jnp.repeat(rep["ssd_d"][l], 64).reshape(1, -1)
    p["ssd_nw"] = rep["ssd_norm_w"][l].reshape(1, -1)
    s5_args = tuple(rep[n][l] for n in ("s5_lam_re", "s5_lam_im", "s5_log_step", "s5_b_re", "s5_b_im", "s5_c_re",
                                        "s5_c_im"))
    (lbr, lbi, bre, bim, cre, cim), p["s5_vjp"] = jax.vjp(_s5_prepare, *s5_args)
    p.update(s5_bre=bre, s5_bim=bim, s5_cre=cre, s5_cim=cim)
    p["s5_coef"] = _s5_scan_coef(lbr, lbi, False)
    p["s5_rcoef"] = _s5_scan_coef(lbr, lbi, True)
    p["s5_d"] = rep["s5_d"][l].reshape(1, -1)
    p["s5_gb"] = rep["s5_glu_b"][l].reshape(1, -1)
    p["rg_cb"] = rep["rg_conv_b"][l].reshape(1, -1)
    p["rg_wa"] = _block_diag(rep["rg_wa"][l])
    p["rg_wx"] = _block_diag(rep["rg_wx"][l])
    p["rg_ba"] = rep["rg_ba"][l].reshape(1, -1)
    p["rg_bx"] = rep["rg_bx"][l].reshape(1, -1)
    p["rg_nsp"] = (-RG_C * jax.nn.softplus(-rep["rg_lambda"][l])).reshape(1, -1)
    p["rg_dnsp"] = RG_C * jax.nn.sigmoid(-rep["rg_lambda"][l])
    for n in ("ln1_g", "ln1_b", "ln2_g", "ln2_b", "ln3_g", "ln3_b"):
        p[n] = rep[n][l].reshape(1, -1)
    return p


def _layer_fwd(h, mem, p, fetch):
    s = {"h0": h}
    p.update(fetch(0, h))
    proj = _mm(h, p["w_in"], name="in_proj")
    s["proj"] = proj
    y_ssd, s["ssd_yy"], s["ssd_states"] = _ssd_fwd(proj, p["ssd_cw"], p["ssd_cb"], p["ssd_dtb"], p["ssd_a"],
                                                     p["ssd_d"], p["ssd_nw"])
    y_s5, s["s5_y2"], s["s5_hre"], s["s5_him"] = _s5_fwd(proj, p["s5_bre"], p["s5_bim"], p["s5_cre"], p["s5_cim"],
                                                         p["s5_d"], p["s5_glu_w"], p["s5_gb"], p["s5_coef"])
    y_rg, s["rg_h"] = _rg_fwd(proj, p["rg_cw"], p["rg_cb"], p["rg_wa"], p["rg_ba"], p["rg_wx"], p["rg_bx"],
                              p["rg_nsp"])
    s["ys"] = [y_ssd, y_s5, y_rg]
    p.update(fetch(1, y_rg))
    h1, s["xh1"], s["rs1"] = _outproj_ln_fwd(s["ys"], h, p["w_out"], p["ln1_g"], p["ln1_b"])
    s["h1"] = h1
    p.update(fetch(2, h1))
    kb = _mm(mem, p["xa_wk"], name="mem_proj")
    vb = _mm(mem, p["xa_wv"], name="mem_proj")
    s["kb"], s["vb"] = kb, vb
    h2, s["xh2"], s["rs2"], s["attn_o"] = _attn_ln_fwd(h1, p["xa_wq"], p["xa_wo"], kb, vb, p["ln2_g"], p["ln2_b"])
    s["h2"] = h2
    p.update(fetch(3, h2))
    h3, s["xh3"], s["rs3"], s["mlp_hdn"] = _mlp_ln_fwd(h2, p["mlp_w1"], p["mlp_w2"], p["ln3_g"], p["ln3_b"])
    return h3, s


def _layer_bwd(dh3, mem, p, s, l, gbuf):
    g = {}
    dr3, du, dh2, g["ln3_g"], g["ln3_b"] = _mlp_ln_bwd(dh3, s["xh3"], s["rs3"], p["ln3_g"], s["mlp_hdn"],
                                                        p["mlp_w1"], p["mlp_w2"])
    gbuf = _wgrad_flat(s["h2"], du, gbuf, mode="colblk", row_off=_grad_row("mlp_w1", l), name="wgrad_mlp_w1")
    gbuf = _wgrad_flat(s["mlp_hdn"], dr3, gbuf, mode="rowblk", row_off=_grad_row("mlp_w2", l), name="wgrad_mlp_w2")
    dr2, dq, dh1, dkb, dvb, g["ln2_g"], g["ln2_b"] = _attn_ln_bwd(dh2, s["xh2"], s["rs2"], p["ln2_g"], s["h1"],
                                                                   p["xa_wq"], p["xa_wo"], s["kb"], s["vb"])
    for n, a_op, g_op in (("xa_wo", s["attn_o"], dr2), ("xa_wq", s["h1"], dq), ("xa_wk", mem, dkb),
                          ("xa_wv", mem, dvb)):
        gbuf = _wgrad_flat(a_op, g_op, gbuf, mode="rows4", row_off=_grad_row(n, l), name="wgrad_" + n)
    dr1, dres, dycat, g["ln1_g"], g["ln1_b"] = _outproj_ln_bwd(dh1, s["xh1"], s["rs1"], p["ln1_g"], p["w_out"])
    gbuf = _wgrad_flat(s["ys"], dr1, gbuf, mode="rows4", row_off=_grad_row("w_out", l), name="wgrad_w_out")
    proj = s["proj"]
    (dxbc, dz, ddt, dcw, dcb, ddtb, da_neg, dd_l, dnw) = _ssd_bwd(
        dycat, proj, s["ssd_yy"], s["ssd_states"], p["ssd_cw"], p["ssd_cb"], p["ssd_dtb"], p["ssd_a"], p["ssd_d"],
        p["ssd_nw"])
    g["ssd_conv_w"] = dcw[0:4]
    g["ssd_conv_b"] = dcb[0]
    g["ssd_dt_bias"] = ddtb[0, :SSD_HEADS]
    g["ssd_a_log"] = da_neg[0, :SSD_HEADS] * p["ssd_a"][0, :SSD_HEADS]
    g["ssd_d"] = dd_l.reshape(SSD_HEADS, 64).sum(axis=1)
    g["ssd_norm_w"] = dnw[0]
    (du_s5, dbre, dbim, dcre, dcim, dlam, dd5, dgw, dgb) = _s5_bwd(
        dycat, proj, s["s5_y2"], s["s5_hre"], s["s5_him"], p["s5_bre"], p["s5_bim"], p["s5_cre"], p["s5_cim"],
        p["s5_d"], p["s5_glu_w"], p["s5_gb"], p["s5_rcoef"])
    dl = dlam.sum(axis=1)
    s5g = p["s5_vjp"]((dl[0], dl[1], dbre, dbim, dcre, dcim))
    for n, v in zip(("s5_lam_re", "s5_lam_im", "s5_log_step", "s5_b_re", "s5_b_im", "s5_c_re", "s5_c_im"), s5g):
        g[n] = v
    g["s5_d"] = dd5[0]
    g["s5_glu_w"] = dgw
    g["s5_glu_b"] = dgb[0]
    (dxrg, dgrg, drcw, drcb, dwa, dba, dwx, dbx, dnsp) = _rg_bwd(
        dycat, proj, s["rg_h"], p["rg_cw"], p["rg_cb"], p["rg_wa"], p["rg_ba"], p["rg_wx"], p["rg_bx"], p["rg_nsp"])
    g["rg_conv_w"] = drcw[0:4]
    g["rg_conv_b"] = drcb[0]
    g["rg_wa"] = _block_diag_extract(dwa, RG_BLOCKS)
    g["rg_wx"] = _block_diag_extract(dwx, RG_BLOCKS)
    g["rg_ba"] = dba.reshape(RG_BLOCKS, RG_BLOCK_DIM)
    g["rg_bx"] = dbx.reshape(RG_BLOCKS, RG_BLOCK_DIM)
    g["rg_lambda"] = dnsp[0] * p["rg_dnsp"]
    dproj = [dxbc, dz, du_s5, dxrg, dgrg, ddt]
    g["w_in"] = _unpack_cols(_wgrad_in(s["h0"], dproj))
    dh0 = _in_proj_bwd(dproj, p["w_in"], dres)
    for n in ("ln1_g", "ln1_b", "ln2_g", "ln2_b", "ln3_g", "ln3_b"):
        g[n] = g[n][0]
    return dh0, g, gbuf


def _local_step(h, memf, target, rep, fetch):
    params, saved = [], []
    for l in range(DEPTH):
        p = _layer_params(rep, l)
        params.append(p)
        h, s = _layer_fwd(h, memf, p, functools.partial(fetch, l))
        saved.append(s)
    loss11, dh = _loss_fwd_bwd(h, target)
    grads = [None] * DEPTH
    gbuf = None
    c_arr = lax.axis_index("c").astype(jnp.int32).reshape(1)
    for l in reversed(range(DEPTH)):
        dh, grads[l], gbuf = _layer_bwd(dh, memf, params[l], saved[l], l, gbuf)
        if l == DEPTH - 1:
            gbuf = lax.dynamic_update_slice(
                gbuf, _w_in_block(grads[l]["w_in"], jnp.zeros((4, MISC_ROWS, FLAT), F32)),
                (0, _grad_row("w_in", l), 0))
            xy_handle = _xy_start(_chip_sums(gbuf, c_arr, 0))
    gsmall = {n: jnp.stack([grads[l][n] for l in range(DEPTH)]) for n in grads[0] if n != "w_in"}
    return loss11, dh, gsmall, grads[0]["w_in"], gbuf, xy_handle, c_arr


def _w_in_block(gw, tail):
    gw = jnp.pad(gw.reshape(D_MODEL, 4, W_IN_SHARD), ((0, 0), (0, 0), (0, W_IN_PAD - W_IN_SHARD)))
    return jnp.concatenate([jnp.transpose(gw, (1, 0, 2)).reshape(4, W_IN_PAD, FLAT), tail], axis=1)


def _chip_sums(gbuf, c_arr, part):
    return list(_add_own_quarter(gbuf, _c_exchange(gbuf, part), c_arr, part))


def kernel(x, mem, w_in, w_out, ssd_conv_w, ssd_conv_b, ssd_dt_bias, ssd_a_log, ssd_d, ssd_norm_w, s5_lam_re, s5_lam_im, s5_log_step, s5_b_re, s5_b_im, s5_c_re, s5_c_im, s5_d, s5_glu_w, s5_glu_b, rg_conv_w, rg_conv_b, rg_wa, rg_ba, rg_wx, rg_bx, rg_lambda, ln1_g, ln1_b, xa_wq, xa_wk, xa_wv, xa_wo, ln2_g, ln2_b, mlp_w1, mlp_w2, ln3_g, ln3_b, loss_target, m_w_in, m_w_out, m_ssd_conv_w, m_ssd_conv_b, m_ssd_dt_bias, m_ssd_a_log, m_ssd_d, m_ssd_norm_w, m_s5_lam_re, m_s5_lam_im, m_s5_log_step, m_s5_b_re, m_s5_b_im, m_s5_c_re, m_s5_c_im, m_s5_d, m_s5_glu_w, m_s5_glu_b, m_rg_conv_w, m_rg_conv_b, m_rg_wa, m_rg_ba, m_rg_wx, m_rg_bx, m_rg_lambda, m_ln1_g, m_ln1_b, m_xa_wq, m_xa_wk, m_xa_wv, m_xa_wo, m_ln2_g, m_ln2_b, m_mlp_w1, m_mlp_w2, m_ln3_g, m_ln3_b, v_w_in, v_w_out, v_ssd_conv_w, v_ssd_conv_b, v_ssd_dt_bias, v_ssd_a_log, v_ssd_d, v_ssd_norm_w, v_s5_lam_re, v_s5_lam_im, v_s5_log_step, v_s5_b_re, v_s5_b_im, v_s5_c_re, v_s5_c_im, v_s5_d, v_s5_glu_w, v_s5_glu_b, v_rg_conv_w, v_rg_conv_b, v_rg_wa, v_rg_ba, v_rg_wx, v_rg_bx, v_rg_lambda, v_ln1_g, v_ln1_b, v_xa_wq, v_xa_wk, v_xa_wv, v_xa_wo, v_ln2_g, v_ln2_b, v_mlp_w1, v_mlp_w2, v_ln3_g, v_ln3_b):
    args = dict(locals())
    weights = {n: args[n] for n in WEIGHT_ORDER}
    mom_m = {n: args["m_" + n] for n in WEIGHT_ORDER}
    mom_v = {n: args["v_" + n] for n in WEIGHT_ORDER}

    shards = []
    for l in range(DEPTH):
        for n, shp, ax in LAYER_GATHERED:
            w = weights[n][l]
            if w.shape[1] != shp[1]:
                w = jnp.pad(w, ((0, 0), (0, shp[1] - w.shape[1])))
            if n not in ("ssd_conv_w", "rg_conv_w"):
                w = w.astype(MXU_DTYPE)
            shards.append(w)
    handle = _gather_start(shards)

    def unpad(arr, padded, width):
        return jnp.concatenate([arr[:, padded * k:padded * k + width] for k in range(4)], axis=1)

    def fetch(l, grp, after):
        ts = [l * N_GATHERED + j for j in WAIT_GROUPS[grp]]
        _, landed = _gather_wait(handle, ts, after, name="weights_gather_wait_%d_%d" % (l, grp))
        out = {}
        for t, arr in zip(ts, landed):
            n = LAYER_GATHERED[t % N_GATHERED][0]
            if n == "w_in":
                arr = _pack_cols(unpad(arr, W_IN_PAD, W_IN_SHARD))
            elif n == "rg_conv_w":
                arr = unpad(arr, LANES, RG_CONV_SHARD)
            out[{"ssd_conv_w": "ssd_cw", "rg_conv_w": "rg_cw"}.get(n, n)] = arr
        return out

    rep = {n: weights[n] for n, _ in REPLICATED}

    loss11, dx, gsmall, gw_in0, gbuf, xy_handle, c_arr = _local_step(x[0], mem[0], loss_target[0], rep, fetch)
    grad_x = dx[None]
    loss = lax.psum(loss11[0, 0], ("x", "y", "c"))

    small_q = _split_shards(gsmall, SMALL_SHARDED)
    rep_q = jnp.pad(_pack_shards(gsmall, REPLICATED), (0, 4 * REP_QROWS * FLAT - REP_ELEMS)).reshape(4, -1)
    misc = jnp.concatenate(
        [jnp.pad(small_q, ((0, 0), (0, MISC_REP_ROW * FLAT - SMALL_ELEMS))), rep_q,
         jnp.zeros((4, (MISC_ROWS - MISC_REP_ROW - REP_QROWS) * FLAT), F32)], axis=1).reshape(4, MISC_ROWS, FLAT)
    gbuf = lax.dynamic_update_slice(gbuf, _w_in_block(gw_in0, misc), (0, _grad_row("w_in", 0), 0))
    got0 = _xy_exchange(_chip_sums(gbuf, c_arr, 1))
    got1 = _xy_wait(xy_handle, dx)
    fbuf = _sum4_into_quarter(got1[0], got1[1], c_arr, 0, None)
    reduced = _c_allgather_halves(_sum4_into_quarter(got0[0], got0[1], c_arr, 1, fbuf))
    misc_red = reduced[ROW_MISC:]
    rep_all = _xy_allgather(misc_red[MISC_REP_ROW:MISC_REP_ROW + REP_QROWS], name="small_grads_allgather")
    g_red = {**_unpack(misc_red[:MISC_REP_ROW].reshape(-1), SMALL_SHARDED),
             **_unpack(rep_all.reshape(-1), REPLICATED)}
    g_red["w_in"] = jnp.stack([
        reduced[_grad_row("w_in", l):_grad_row("w_in", l) + W_IN_PAD].reshape(D_MODEL, W_IN_PAD)[:, :W_IN_SHARD]
        for l in range(DEPTH)])

    res = {}
    for n in WEIGHT_ORDER:
        if n in ("mlp_w1", "mlp_w2", "w_out", "xa_wq", "xa_wk", "xa_wv", "xa_wo"):
            res[n] = _adamw(weights[n], mom_m[n], mom_v[n], reduced, g_rows=[_grad_row(n, l) for l in range(DEPTH)])
        else:
            res[n] = _adamw(weights[n], mom_m[n], mom_v[n], g_red[n])
    return (loss, grad_x, *[res[n][0] for n in WEIGHT_ORDER], *[res[n][1] for n in WEIGHT_ORDER],
            *[res[n][2] for n in WEIGHT_ORDER], *[res[n][3] for n in WEIGHT_ORDER])
```

```python
import functools
import math

import jax
import jax.numpy as jnp
from jax import lax
from jax.experimental import pallas as pl
from jax.experimental.pallas import tpu as pltpu

F32 = jnp.float32
MXU_DTYPE = jnp.bfloat16

D_MODEL = 1024
DEPTH = 2
MEM_LEN = 256
SSD_WIDTH = 512
SSD_HEADS = 8
SSD_STATE = 128
SSD_CHUNK = 128
SSD_XBC = 1024
S5_WIDTH = 256
S5_GROUPS = 16
S5_GROUP_CH = 16
S5_STATE = 64
S5_NSTATE = S5_GROUPS * S5_STATE
RG_WIDTH = 256
RG_BLOCKS = 4
RG_BLOCK_DIM = 64
RG_C = 8.0
XA_HEADS = 4
XA_HEAD_DIM = 256
D_FF = 4096
D_IN = 2312
ALPHA = (2.0 * DEPTH) ** 0.25
LN_EPS = 1e-5
ADAM_LR = 0.001
ADAM_B1 = 0.9
ADAM_B2 = 0.999
ADAM_EPS = 1e-08
ADAM_WD = 0.01
ADAM_STEP = 10

P_XBC, P_Z, P_U, P_XRG, P_GRG, P_DT = 0, 1024, 1536, 1792, 2048, 2304
D_PACK = 2432
O_Z, O_XBC, O_DT, O_U, O_XRG, O_GRG = 0, 512, 1536, 1544, 1800, 2056

LANES = 128
SUBLANES = 8
VMEM_LIMIT = 52 * 1024 * 1024
TM = 512
SSD_TM = 256
SCAN_TM = 512
FLAT = 1024

MESH = pl.DeviceIdType.MESH


def _cparams(sem):
    return pltpu.CompilerParams(dimension_semantics=sem, vmem_limit_bytes=VMEM_LIMIT)


def _dot(a, b):
    return jnp.dot(a.astype(MXU_DTYPE), b.astype(MXU_DTYPE), preferred_element_type=F32)


def _dot_nt(a, b):
    return lax.dot_general(a.astype(MXU_DTYPE), b.astype(MXU_DTYPE), (((1,), (1,)), ((), ())),
                           preferred_element_type=F32)


def _dot_tn(a, b):
    return lax.dot_general(a.astype(MXU_DTYPE), b.astype(MXU_DTYPE), (((0,), (0,)), ((), ())),
                           preferred_element_type=F32)


def _dot_f32(a, b):
    return jnp.dot(a, b, precision=lax.Precision.HIGHEST, preferred_element_type=F32)


def _dot_f32_tn(a, b):
    return lax.dot_general(a, b, (((0,), (0,)), ((), ())), precision=lax.Precision.HIGHEST,
                           preferred_element_type=F32)


def _sigmoid(x):
    return 1.0 / (1.0 + jnp.exp(-x))


def _softplus(x):
    return jnp.maximum(x, 0.0) + jnp.log(1.0 + jnp.exp(-jnp.abs(x)))


_GELU_K = math.sqrt(2.0 / math.pi)


def _gelu(x):
    return 0.5 * x * (1.0 + jnp.tanh(_GELU_K * (x + 0.044715 * x * x * x)))


def _gelu_grad(x):
    t = jnp.tanh(_GELU_K * (x + 0.044715 * x * x * x))
    return 0.5 * (1.0 + t) + 0.5 * x * (1.0 - t * t) * _GELU_K * (1.0 + 3.0 * 0.044715 * x * x)


def _expm1(x):
    small = x * (1.0 + x * (0.5 + x * (1.0 / 6.0 + x * (1.0 / 24.0))))
    return jnp.where(jnp.abs(x) < 0.05, small, jnp.exp(x) - 1.0)


def _sum0(x):
    return jnp.sum(x, axis=0, keepdims=True)


def _ln_fwd(r, g, b):
    mu = jnp.mean(r, axis=-1, keepdims=True)
    xc = r - mu
    var = jnp.mean(xc * xc, axis=-1, keepdims=True)
    rstd = lax.rsqrt(var + LN_EPS)
    xhat = xc * rstd
    return xhat * g + b, xhat, rstd


def _ln_bwd(dout, xhat, rstd, g):
    dxh = dout * g
    m1 = jnp.mean(dxh, axis=-1, keepdims=True)
    m2 = jnp.mean(dxh * xhat, axis=-1, keepdims=True)
    return rstd * (dxh - m1 - xhat * m2)


def _rows(tm, n, col=0):
    return pl.BlockSpec((tm, n), lambda i: (i, col))


def _const(shape):
    nd = len(shape)
    return pl.BlockSpec(shape, lambda i: (0,) * nd)


def _mm(a, w, *, name):
    t, k = a.shape
    n = w.shape[1]
    tm = min(TM, t)

    def body(a_ref, w_ref, o_ref):
        o_ref[...] = _dot(a_ref[...], w_ref[...])

    return pl.pallas_call(
        body, name=name, grid=(t // tm,), in_specs=[_rows(tm, k), _const(w.shape)], out_specs=_rows(tm, n),
        out_shape=jax.ShapeDtypeStruct((t, n), F32), compiler_params=_cparams(("arbitrary",)),
    )(a, w)


DPROJ_PIECES = ((P_XBC, 1024), (P_Z, 512), (P_U, 256), (P_XRG, 256), (P_GRG, 256), (P_DT, LANES))


def _in_proj_bwd(pieces, w, dres):
    t = dres.shape[0]
    npc = len(pieces)

    def body(*refs):
        w_ref, r_ref, o_ref = refs[npc:]
        acc = r_ref[...]
        for p_ref, (off, k) in zip(refs[:npc], DPROJ_PIECES):
            acc = acc + _dot_nt(p_ref[...], w_ref[:, off:off + k])
        o_ref[...] = acc

    return pl.pallas_call(
        body, name="in_proj_bwd", grid=(t // TM,),
        in_specs=[_rows(TM, k) for _, k in DPROJ_PIECES] + [_const(w.shape), _rows(TM, D_MODEL)],
        out_specs=_rows(TM, D_MODEL), out_shape=jax.ShapeDtypeStruct((t, D_MODEL), F32),
        compiler_params=_cparams(("arbitrary",)),
    )(*pieces, w, dres)


def _wgrad_in(h0, pieces):
    t = h0.shape[0]
    npc = len(pieces)

    def body(*refs):
        h_ref, o_ref = refs[npc], refs[npc + 1]
        @pl.when(pl.program_id(0) == 0)
        def _():
            o_ref[...] = jnp.zeros_like(o_ref)

        hb = h_ref[...].astype(MXU_DTYPE)
        for p_ref, (off, k) in zip(refs[:npc], DPROJ_PIECES):
            o_ref[:, off:off + k] += _dot_tn(hb, p_ref[...])

    return pl.pallas_call(
        body, name="wgrad_in", grid=(t // TM,),
        in_specs=[_rows(TM, k) for _, k in DPROJ_PIECES] + [_rows(TM, D_MODEL)],
        out_specs=_const((D_MODEL, D_PACK)), out_shape=jax.ShapeDtypeStruct((D_MODEL, D_PACK), F32),
        compiler_params=_cparams(("arbitrary",)),
    )(*pieces, h0)


G_ROWS = 8192
G_QUARTER = G_ROWS // 4
W_IN_SHARD = 578
W_IN_PAD = 640
MISC_ROWS = 128
MISC_REP_ROW = 40
ROW_MISC = G_ROWS - MISC_ROWS


def _grad_row(name, l):
    c0 = (1 - l) * G_QUARTER
    c1 = 2 * G_QUARTER + c0
    return {"mlp_w1": c0, "mlp_w2": c0 + 1024, "w_out": c1, "xa_wq": c1 + 256, "xa_wk": c1 + 512, "xa_wv": c1 + 768,
            "xa_wo": c1 + 1024, "w_in": c1 + 1280}[name]


W_IN_BLOCK_ROWS = G_QUARTER - 1280


def _wgrad_flat(a, g, buf, *, mode, row_off, name):
    pieces = list(a) if isinstance(a, (list, tuple)) else [a]
    t = g.shape[0]
    tt = min(1024, t)
    ns = t // tt
    blk = D_MODEL

    def accumulate(o_ref, parts, s):
        @pl.when(s == 0)
        def _():
            o_ref[...] = jnp.zeros_like(o_ref)

        for q, v in parts:
            o_ref[q] += v

    if mode == "rows4":
        grid = (ns,)
        in_specs = [pl.BlockSpec((tt, p.shape[1]), lambda s: (s, 0)) for p in pieces]
        in_specs.append(pl.BlockSpec((tt, blk), lambda s: (s, 0)))
        out_spec = pl.BlockSpec((4, 256, FLAT), lambda s: (0, row_off // 256, 0))
        sem = ("arbitrary",)
        npc = len(pieces)

        def body(*refs):
            g_v = refs[npc][...]
            parts, q0 = [], 0
            for p_ref in refs[:npc]:
                full = _dot_tn(p_ref[...], g_v)
                nq = full.shape[0] // 256
                parts += [(q0 + q, full[q * 256:(q + 1) * 256]) for q in range(nq)]
                q0 += nq
            accumulate(refs[-1], parts, pl.program_id(0))
    else:
        grid = (2, ns)
        if mode == "rowblk":
            in_specs = [pl.BlockSpec((tt, 2 * blk), lambda q, s: (s, q)), pl.BlockSpec((tt, blk), lambda q, s: (s, 0))]
        else:
            in_specs = [pl.BlockSpec((tt, blk), lambda q, s: (s, 0)), pl.BlockSpec((tt, 2 * blk), lambda q, s: (s, q))]
        out_spec = pl.BlockSpec((2, blk, FLAT), lambda q, s: (q, row_off // blk, 0))
        sem = ("arbitrary", "arbitrary")

        def body(a_ref, g_ref, *rest):
            full = _dot_tn(a_ref[...], g_ref[...])
            if mode == "rowblk":
                parts = [(0, full[:blk]), (1, full[blk:])]
            else:
                parts = [(0, full[:, :blk]), (1, full[:, blk:])]
            accumulate(rest[-1], parts, pl.program_id(1))

    args = pieces + [g]
    aliases = {}
    if buf is not None:
        in_specs.append(pl.BlockSpec(memory_space=pl.ANY))
        args.append(buf)
        aliases = {len(args) - 1: 0}
    return pl.pallas_call(
        body, name=name, grid=grid, in_specs=in_specs, out_specs=out_spec,
        out_shape=jax.ShapeDtypeStruct((4, G_ROWS, FLAT), F32), input_output_aliases=aliases,
        compiler_params=_cparams(sem),
    )(*args)


def _outproj_ln_fwd(ys, h, w, g, b):
    t = h.shape[0]
    npc = len(ys)

    def body(*refs):
        h_ref, w_ref, g_ref, b_ref, hn_ref, xh_ref, rs_ref = refs[npc:]
        r = ALPHA * h_ref[...]
        off = 0
        for y_ref in refs[:npc]:
            k = y_ref.shape[1]
            r = r + _dot(y_ref[...], w_ref[off:off + k, :])
            off += k
        out, xhat, rstd = _ln_fwd(r, g_ref[...], b_ref[...])
        hn_ref[...] = out
        xh_ref[...] = xhat
        rs_ref[...] = rstd

    return pl.pallas_call(
        body, name="outproj_ln_fwd", grid=(t // TM,),
        in_specs=[_rows(TM, y.shape[1]) for y in ys] + [_rows(TM, D_MODEL), _const((D_MODEL, D_MODEL)),
                                                        _const((1, D_MODEL)), _const((1, D_MODEL))],
        out_specs=[_rows(TM, D_MODEL), _rows(TM, D_MODEL), _rows(TM, 1)],
        out_shape=[jax.ShapeDtypeStruct((t, D_MODEL), F32), jax.ShapeDtypeStruct((t, D_MODEL), F32),
                   jax.ShapeDtypeStruct((t, 1), F32)],
        compiler_params=_cparams(("arbitrary",)),
    )(*ys, h, w, g, b)


def _attn_probs(q, kb, hh):
    sl = slice(hh * XA_HEAD_DIM, (hh + 1) * XA_HEAD_DIM)
    s = _dot_nt(q[:, sl], kb[:, sl]) * (1.0 / math.sqrt(XA_HEAD_DIM))
    m = jnp.max(s, axis=-1, keepdims=True)
    e = jnp.exp(s - m)
    return e / jnp.sum(e, axis=-1, keepdims=True)


def _attn_ln_fwd(h1, wq, wo, kb, vb, g, b):
    t = h1.shape[0]

    def body(h_ref, wq_ref, wo_ref, k_ref, v_ref, g_ref, b_ref, hn_ref, xh_ref, rs_ref, o_ref):
        h = h_ref[...]
        q = _dot(h, wq_ref[...])
        kb_ = k_ref[...]
        vb_ = v_ref[...]
        for hh in range(XA_HEADS):
            sl = slice(hh * XA_HEAD_DIM, (hh + 1) * XA_HEAD_DIM)
            p = _attn_probs(q, kb_, hh)
            o_ref[:, sl] = _dot(p, vb_[:, sl]).astype(o_ref.dtype)
        r = ALPHA * h + _dot(o_ref[...], wo_ref[...])
        out, xhat, rstd = _ln_fwd(r, g_ref[...], b_ref[...])
        hn_ref[...] = out
        xh_ref[...] = xhat
        rs_ref[...] = rstd

    return pl.pallas_call(
        body, name="attn_ln_fwd", grid=(t // TM,),
        in_specs=[_rows(TM, D_MODEL), _const((D_MODEL, D_MODEL)), _const((D_MODEL, D_MODEL)),
                  _const((MEM_LEN, D_MODEL)), _const((MEM_LEN, D_MODEL)), _const((1, D_MODEL)), _const((1, D_MODEL))],
        out_specs=[_rows(TM, D_MODEL), _rows(TM, D_MODEL), _rows(TM, 1), _rows(TM, D_MODEL)],
        out_shape=[jax.ShapeDtypeStruct((t, D_MODEL), F32), jax.ShapeDtypeStruct((t, D_MODEL), F32),
                   jax.ShapeDtypeStruct((t, 1), F32), jax.ShapeDtypeStruct((t, D_MODEL), MXU_DTYPE)],
        compiler_params=_cparams(("arbitrary",)),
    )(h1, wq, wo, kb, vb, g, b)


def _attn_ln_bwd(dh2, xhat, rstd, g, h1, wq, wo, kb, vb):
    t = h1.shape[0]

    def body(dh_ref, xh_ref, rs_ref, g_ref, h_ref, wq_ref, wo_ref, k_ref, v_ref,
             dr_ref, dq_ref, dh1_ref, dk_ref, dv_ref, dg_ref, db_ref):
        i = pl.program_id(0)

        @pl.when(i == 0)
        def _():
            dk_ref[...] = jnp.zeros_like(dk_ref)
            dv_ref[...] = jnp.zeros_like(dv_ref)
            dg_ref[...] = jnp.zeros_like(dg_ref)
            db_ref[...] = jnp.zeros_like(db_ref)

        dout = dh_ref[...]
        xh = xh_ref[...]
        dg_ref[...] += _sum0(dout * xh)
        db_ref[...] += _sum0(dout)
        dr = _ln_bwd(dout, xh, rs_ref[...], g_ref[...])
        dr_ref[...] = dr.astype(dr_ref.dtype)
        do = _dot_nt(dr, wo_ref[...])
        h = h_ref[...]
        q = _dot(h, wq_ref[...])
        kb_ = k_ref[...]
        vb_ = v_ref[...]
        scale = 1.0 / math.sqrt(XA_HEAD_DIM)
        for hh in range(XA_HEADS):
            sl = slice(hh * XA_HEAD_DIM, (hh + 1) * XA_HEAD_DIM)
            p = _attn_probs(q, kb_, hh)
            do_h = do[:, sl]
            dp = _dot_nt(do_h, vb_[:, sl])
            ds = p * (dp - jnp.sum(dp * p, axis=-1, keepdims=True)) * scale
            dq_ref[:, sl] = _dot(ds, kb_[:, sl]).astype(dq_ref.dtype)
            dk_ref[:, sl] += _dot_tn(ds, q[:, sl])
            dv_ref[:, sl] += _dot_tn(p, do_h)
        dh1_ref[...] = ALPHA * dr + _dot_nt(dq_ref[...], wq_ref[...])

    return pl.pallas_call(
        body, name="attn_ln_bwd", grid=(t // TM,),
        in_specs=[_rows(TM, D_MODEL), _rows(TM, D_MODEL), _rows(TM, 1), _const((1, D_MODEL)), _rows(TM, D_MODEL),
                  _const((D_MODEL, D_MODEL)), _const((D_MODEL, D_MODEL)), _const((MEM_LEN, D_MODEL)),
                  _const((MEM_LEN, D_MODEL))],
        out_specs=[_rows(TM, D_MODEL), _rows(TM, D_MODEL), _rows(TM, D_MODEL), _const((MEM_LEN, D_MODEL)),
                   _const((MEM_LEN, D_MODEL)), _const((1, D_MODEL)), _const((1, D_MODEL))],
        out_shape=[jax.ShapeDtypeStruct((t, D_MODEL), MXU_DTYPE), jax.ShapeDtypeStruct((t, D_MODEL), MXU_DTYPE),
                   jax.ShapeDtypeStruct((t, D_MODEL), F32), jax.ShapeDtypeStruct((MEM_LEN, D_MODEL), F32),
                   jax.ShapeDtypeStruct((MEM_LEN, D_MODEL), F32), jax.ShapeDtypeStruct((1, D_MODEL), F32),
                   jax.ShapeDtypeStruct((1, D_MODEL), F32)],
        compiler_params=_cparams(("arbitrary",)),
    )(dh2, xhat, rstd, g, h1, wq, wo, kb, vb)


FF_CHUNK = 1024
N_FF = D_FF // FF_CHUNK


def _load_resident(pairs, sems):
    copies = [pltpu.make_async_copy(src, dst, sems.at[k]) for k, (src, dst) in enumerate(pairs)]
    for cp in copies:
        cp.start()
    for cp in copies:
        cp.wait()


def _mlp_ln_fwd(h2, w1, w2, g, b):
    t = h2.shape[0]

    def body(h_ref, w1_hbm, w2_hbm, g_ref, b_ref, hn_ref, xh_ref, rs_ref, hd_ref, w1_v, w2_v, acc_ref, sems):
        @pl.when(pl.program_id(0) == 0)
        def _():
            _load_resident([(w1_hbm, w1_v), (w2_hbm, w2_v)], sems)

        h = h_ref[...]
        hb = h.astype(MXU_DTYPE)
        acc_ref[...] = ALPHA * h
        for j in range(N_FF):
            sl = slice(j * FF_CHUNK, (j + 1) * FF_CHUNK)
            u = _dot(hb, w1_v[:, sl])
            hd = jnp.square(jnp.maximum(u, 0.0)).astype(MXU_DTYPE)
            hd_ref[:, sl] = hd
            acc_ref[...] += _dot(hd, w2_v[sl, :])
        out, xhat, rstd = _ln_fwd(acc_ref[...], g_ref[...], b_ref[...])
        hn_ref[...] = out
        xh_ref[...] = xhat
        rs_ref[...] = rstd

    return pl.pallas_call(
        body, name="mlp_ln_fwd", grid=(t // TM,),
        in_specs=[_rows(TM, D_MODEL), _hbm(), _hbm(), _const((1, D_MODEL)), _const((1, D_MODEL))],
        out_specs=[_rows(TM, D_MODEL), _rows(TM, D_MODEL), _rows(TM, 1), _rows(TM, D_FF)],
        out_shape=[jax.ShapeDtypeStruct((t, D_MODEL), F32), jax.ShapeDtypeStruct((t, D_MODEL), F32),
                   jax.ShapeDtypeStruct((t, 1), F32), jax.ShapeDtypeStruct((t, D_FF), MXU_DTYPE)],
        scratch_shapes=[pltpu.VMEM((D_MODEL, D_FF), MXU_DTYPE), pltpu.VMEM((D_FF, D_MODEL), MXU_DTYPE),
                        pltpu.VMEM((TM, D_MODEL), F32), pltpu.SemaphoreType.DMA((2,))],
        compiler_params=_cparams(("arbitrary",)),
    )(h2, w1, w2, g, b)


def _mlp_ln_bwd(dh3, xhat, rstd, g, hdn, w1, w2):
    t = dh3.shape[0]

    def body(dh_ref, xh_ref, rs_ref, g_ref, hd_ref, w1_hbm, w2_hbm,
             dr_ref, du_ref, dh2_ref, dg_ref, db_ref, w1_v, w2_v, acc_ref, sems):
        @pl.when(pl.program_id(0) == 0)
        def _():
            _load_resident([(w1_hbm, w1_v), (w2_hbm, w2_v)], sems)
            dg_ref[...] = jnp.zeros_like(dg_ref)
            db_ref[...] = jnp.zeros_like(db_ref)

        dout = dh_ref[...]
        xh = xh_ref[...]
        dg_ref[...] += _sum0(dout * xh)
        db_ref[...] += _sum0(dout)
        dr = _ln_bwd(dout, xh, rs_ref[...], g_ref[...])
        drb = dr.astype(MXU_DTYPE)
        dr_ref[...] = drb
        acc_ref[...] = ALPHA * dr
        for j in range(N_FF):
            sl = slice(j * FF_CHUNK, (j + 1) * FF_CHUNK)
            dhd = _dot_nt(drb, w2_v[sl, :])
            du = (dhd * (2.0 * jnp.sqrt(hd_ref[:, sl].astype(F32)))).astype(MXU_DTYPE)
            du_ref[:, sl] = du
            acc_ref[...] += _dot_nt(du, w1_v[:, sl])
        dh2_ref[...] = acc_ref[...]

    tm = TM // 2
    return pl.pallas_call(
        body, name="mlp_ln_bwd", grid=(t // tm,),
        in_specs=[_rows(tm, D_MODEL), _rows(tm, D_MODEL), _rows(tm, 1), _const((1, D_MODEL)), _rows(tm, D_FF),
                  _hbm(), _hbm()],
        out_specs=[_rows(tm, D_MODEL), _rows(tm, D_FF), _rows(tm, D_MODEL), _const((1, D_MODEL)),
                   _const((1, D_MODEL))],
        out_shape=[jax.ShapeDtypeStruct((t, D_MODEL), MXU_DTYPE), jax.ShapeDtypeStruct((t, D_FF), MXU_DTYPE),
                   jax.ShapeDtypeStruct((t, D_MODEL), F32), jax.ShapeDtypeStruct((1, D_MODEL), F32),
                   jax.ShapeDtypeStruct((1, D_MODEL), F32)],
        scratch_shapes=[pltpu.VMEM((D_MODEL, D_FF), MXU_DTYPE), pltpu.VMEM((D_FF, D_MODEL), MXU_DTYPE),
                        pltpu.VMEM((tm, D_MODEL), F32), pltpu.SemaphoreType.DMA((2,))],
        compiler_params=_cparams(("arbitrary",)),
    )(dh3, xhat, rstd, g, hdn, w1, w2)


def _outproj_ln_bwd(dh1, xhat, rstd, g, w):
    t = dh1.shape[0]

    def body(dh_ref, xh_ref, rs_ref, g_ref, w_ref, dr_ref, res_ref, dy_ref, dg_ref, db_ref):
        i = pl.program_id(0)

        @pl.when(i == 0)
        def _():
            dg_ref[...] = jnp.zeros_like(dg_ref)
            db_ref[...] = jnp.zeros_like(db_ref)

        dout = dh_ref[...]
        xh = xh_ref[...]
        dg_ref[...] += _sum0(dout * xh)
        db_ref[...] += _sum0(dout)
        dr = _ln_bwd(dout, xh, rs_ref[...], g_ref[...])
        dr_ref[...] = dr.astype(dr_ref.dtype)
        res_ref[...] = ALPHA * dr
        dy_ref[...] = _dot_nt(dr, w_ref[...])

    return pl.pallas_call(
        body, name="outproj_ln_bwd", grid=(t // TM,),
        in_specs=[_rows(TM, D_MODEL), _rows(TM, D_MODEL), _rows(TM, 1), _const((1, D_MODEL)),
                  _const((D_MODEL, D_MODEL))],
        out_specs=[_rows(TM, D_MODEL), _rows(TM, D_MODEL), _rows(TM, D_MODEL), _const((1, D_MODEL)),
                   _const((1, D_MODEL))],
        out_shape=[jax.ShapeDtypeStruct((t, D_MODEL), MXU_DTYPE), jax.ShapeDtypeStruct((t, D_MODEL), F32),
                   jax.ShapeDtypeStruct((t, D_MODEL), F32), jax.ShapeDtypeStruct((1, D_MODEL), F32),
                   jax.ShapeDtypeStruct((1, D_MODEL), F32)],
        compiler_params=_cparams(("arbitrary",)),
    )(dh1, xhat, rstd, g, w)


def _loss_fwd_bwd(h, target):
    t = h.shape[0]

    def body(h_ref, t_ref, l_ref, dh_ref):
        i = pl.program_id(0)

        @pl.when(i == 0)
        def _():
            l_ref[...] = jnp.zeros_like(l_ref)

        e = h_ref[...] - t_ref[...]
        dh_ref[...] = e * (1.0 / D_MODEL)
        per_tok = jnp.mean(e * e, axis=-1, keepdims=True)
        l_ref[...] += 0.5 * jnp.sum(per_tok, axis=0, keepdims=True)

    return pl.pallas_call(
        body, name="loss_fwd_bwd", grid=(t // TM,),
        in_specs=[_rows(TM, D_MODEL), _rows(TM, D_MODEL)],
        out_specs=[_const((1, 1)), _rows(TM, D_MODEL)],
        out_shape=[jax.ShapeDtypeStruct((1, 1), F32), jax.ShapeDtypeStruct((t, D_MODEL), F32)],
        compiler_params=_cparams(("arbitrary",)),
    )(h, target)


def _pick_col(x, idx):
    lane = lax.broadcasted_iota(jnp.int32, x.shape, 1)
    return jnp.sum(jnp.where(lane == idx, x, 0.0), axis=1, keepdims=True)


def _pick_row(x, idx):
    sub = lax.broadcasted_iota(jnp.int32, x.shape, 0)
    return jnp.sum(jnp.where(sub == idx, x, 0.0), axis=0, keepdims=True)


def _conv_taps(pad_ref, w, tm, base):
    acc = w[0:1, :] * pad_ref[base:base + tm, :]
    for k in range(1, 4):
        acc = acc + w[k:k + 1, :] * pad_ref[base + k:base + k + tm, :]
    return acc


def _ssd_chunk_common(adt_c, tri):
    cs = _dot_f32(tri, adt_c)
    return cs, cs.T, jnp.exp(cs)


def _ssd_head_terms(cs, cst, ecs, dt_c, h, tri):
    cs_col = _pick_col(cs, h)
    cs_row = _pick_row(cst, h)
    dt_col = _pick_col(dt_c, h)
    cs_last = cs_col[SSD_CHUNK - 1:SSD_CHUNK, :]
    lmat = jnp.exp(jnp.where(tri > 0.0, cs_col - cs_row, -1e30))
    ecs_col = _pick_col(ecs, h)
    decay_col = jnp.exp(cs_last - cs_col)
    return cs_col, dt_col, cs_last, lmat, ecs_col, decay_col


def _ssd_fwd(proj, cw, cb, dtb, a_neg, d_lanes, nw):
    t = proj.shape[0]
    tm = SSD_TM
    nt = t // tm
    ncq = tm // SSD_CHUNK
    hb = tm // SUBLANES

    def body(xbc_ref, halo_ref, z_ref, dt_ref, cw_ref, cb_ref, dtb_ref, a_ref, d_ref, nw_ref,
             y_ref, yy_ref, st_ref, xpad, xact, state):
        i = pl.program_id(0)

        @pl.when(i == 0)
        def _():
            state[...] = jnp.zeros_like(state)

        xpad[0:SUBLANES, :] = jnp.where(i > 0, halo_ref[...], 0.0)
        xpad[SUBLANES:SUBLANES + tm, :] = xbc_ref[...]
        acc = cb_ref[...] + _conv_taps(xpad, cw_ref[...], tm, SUBLANES - 3)
        xact[...] = acc * _sigmoid(acc)
        dt = _softplus(dt_ref[...] + dtb_ref[...])
        adt = dt * a_ref[...]
        r_i = lax.broadcasted_iota(jnp.int32, (SSD_CHUNK, SSD_CHUNK), 0)
        c_i = lax.broadcasted_iota(jnp.int32, (SSD_CHUNK, SSD_CHUNK), 1)
        tri = (r_i >= c_i).astype(F32)
        lane1 = lax.broadcasted_iota(jnp.int32, (1, LANES), 1)
        for c in range(ncq):
            sl = slice(c * SSD_CHUNK, (c + 1) * SSD_CHUNK)
            dt_c = dt[sl]
            cs, cst, ecs = _ssd_chunk_common(adt[sl], tri)
            for g in range(2):
                bg = xact[sl, 512 + g * 128:512 + (g + 1) * 128]
                cg = xact[sl, 768 + g * 128:768 + (g + 1) * 128]
                cbm = _dot_nt(cg, bg)
                for pr in range(2):
                    pi = g * 2 + pr
                    psl = slice(pi * 128, (pi + 1) * 128)
                    xp = xact[sl, psl]
                    prev = state[pi]
                    st_ref[c, pi] = prev
                    yp = xp * d_ref[:, psl]
                    new_s = jnp.zeros((SSD_STATE, LANES), F32)
                    dec_lane = jnp.zeros((1, LANES), F32)
                    for hh in range(2):
                        h = g * 4 + pr * 2 + hh
                        lm = (lane1 >= 64) if hh else (lane1 < 64)
                        _, dt_col, cs_last, lmat, ecs_col, decay_col = _ssd_head_terms(cs, cst, ecs, dt_c, h, tri)
                        xdt = jnp.where(lm, xp, 0.0) * dt_col
                        yp = yp + _dot(cbm * lmat, xdt)
                        yp = yp + _dot(cg * ecs_col, jnp.where(lm, prev, 0.0))
                        new_s = new_s + _dot_tn(bg * decay_col, xdt)
                        dec_lane = dec_lane + jnp.where(lm, jnp.exp(cs_last), 0.0)
                    state[pi] = prev * dec_lane + new_s
                    yy_ref[sl, psl] = yp
        yy = yy_ref[...]
        z = z_ref[...]
        yg = yy * (z * _sigmoid(z))
        ms = jnp.mean(yg * yg, axis=-1, keepdims=True)
        y_ref[...] = yg * lax.rsqrt(ms + LN_EPS) * nw_ref[...]

    halo_map = lambda i: (jnp.maximum(i * hb - 1, 0), 0)
    return pl.pallas_call(
        body, name="ssd_fwd", grid=(nt,),
        in_specs=[pl.BlockSpec((tm, SSD_XBC), lambda i: (i, 0)), pl.BlockSpec((SUBLANES, SSD_XBC), halo_map),
                  pl.BlockSpec((tm, SSD_WIDTH), lambda i: (i, P_Z // SSD_WIDTH)),
                  pl.BlockSpec((tm, LANES), lambda i: (i, P_DT // LANES)),
                  _const((4, SSD_XBC)), _const((1, SSD_XBC)), _const((1, LANES)), _const((1, LANES)),
                  _const((1, SSD_WIDTH)), _const((1, SSD_WIDTH))],
        out_specs=[_rows(tm, SSD_WIDTH), _rows(tm, SSD_WIDTH),
                   pl.BlockSpec((ncq, 4, SSD_STATE, LANES), lambda i: (i, 0, 0, 0))],
        out_shape=[jax.ShapeDtypeStruct((t, SSD_WIDTH), F32), jax.ShapeDtypeStruct((t, SSD_WIDTH), F32),
                   jax.ShapeDtypeStruct((t // SSD_CHUNK, 4, SSD_STATE, LANES), F32)],
        scratch_shapes=[pltpu.VMEM((tm + SUBLANES, SSD_XBC), F32), pltpu.VMEM((tm, SSD_XBC), F32),
                        pltpu.VMEM((4, SSD_STATE, LANES), F32)],
        compiler_params=_cparams(("arbitrary",)),
    )(proj, proj, proj, proj, cw, cb, dtb, a_neg, d_lanes, nw)


def _ssd_bwd(dycat, proj, yy, states, cw, cb, dtb, a_neg, d_lanes, nw):
    t = proj.shape[0]
    tm = SSD_TM
    nt = t // tm
    ncq = tm // SSD_CHUNK
    hb = tm // SUBLANES

    def body(dy_ref, xbc_ref, halo_ref, z_ref, dt_ref, yy_ref, st_ref, cw_ref, cb_ref, dtb_ref, a_ref, d_ref, nw_ref,
             dxbc_ref, dz_ref, ddt_ref, dcw_ref, dcb_ref, ddtb_ref, da_ref, dd_ref, dnw_ref,
             xpad, xact, dxact, dpad, dstate, dnext):
        i = pl.program_id(0)

        @pl.when(i == 0)
        def _():
            for r in (dcw_ref, dcb_ref, ddtb_ref, da_ref, dd_ref, dnw_ref, dstate, dnext):
                r[...] = jnp.zeros_like(r)

        xpad[0:SUBLANES, :] = jnp.where(i < nt - 1, halo_ref[...], 0.0)
        xpad[SUBLANES:SUBLANES + tm, :] = xbc_ref[...]
        cw_v = cw_ref[...]
        acc = cb_ref[...] + _conv_taps(xpad, cw_v, tm, SUBLANES - 3)
        sig = _sigmoid(acc)
        xact[...] = acc * sig
        dt_raw = dt_ref[...] + dtb_ref[...]
        dt = _softplus(dt_raw)
        a_v = a_ref[...]
        adt = dt * a_v
        yy = yy_ref[...]
        z = z_ref[...]
        sz = _sigmoid(z)
        siluz = z * sz
        yg = yy * siluz
        ms = jnp.mean(yg * yg, axis=-1, keepdims=True)
        rinv = lax.rsqrt(ms + LN_EPS)
        dout = dy_ref[...]
        dnw_ref[...] += _sum0(dout * yg * rinv)
        dyn = dout * nw_ref[...]
        dyg = rinv * dyn - yg * (rinv * rinv * rinv) * jnp.mean(dyn * yg, axis=-1, keepdims=True)
        dyy = dyg * siluz
        dz_ref[...] = dyg * yy * (sz * (1.0 + z * (1.0 - sz)))
        dd_ref[...] += _sum0(dyy * xact[:, 0:SSD_WIDTH])

        r_i = lax.broadcasted_iota(jnp.int32, (SSD_CHUNK, SSD_CHUNK), 0)
        c_i = lax.broadcasted_iota(jnp.int32, (SSD_CHUNK, SSD_CHUNK), 1)
        tri = (r_i >= c_i).astype(F32)
        lane1 = lax.broadcasted_iota(jnp.int32, (1, LANES), 1)
        for c in reversed(range(ncq)):
            sl = slice(c * SSD_CHUNK, (c + 1) * SSD_CHUNK)
            dt_c = dt[sl]
            cs, cst, ecs = _ssd_chunk_common(adt[sl], tri)
            cacc = jnp.zeros((SSD_CHUNK, LANES), F32)
            racc = jnp.zeros((SSD_CHUNK, LANES), F32)
            ddtx = jnp.zeros((SSD_CHUNK, LANES), F32)
            for g in range(2):
                bg = xact[sl, 512 + g * 128:512 + (g + 1) * 128]
                cg = xact[sl, 768 + g * 128:768 + (g + 1) * 128]
                cbm = _dot_nt(cg, bg)
                dcb_m = jnp.zeros((SSD_CHUNK, SSD_CHUNK), F32)
                dbg = jnp.zeros((SSD_CHUNK, SSD_STATE), F32)
                dcg = jnp.zeros((SSD_CHUNK, SSD_STATE), F32)
                for pr in range(2):
                    pi = g * 2 + pr
                    psl = slice(pi * 128, (pi + 1) * 128)
                    xp = xact[sl, psl]
                    dyp = dyy[sl, psl]
                    prev = st_ref[c, pi]
                    ds_all = dstate[pi]
                    dxdt_p = jnp.zeros((SSD_CHUNK, LANES), F32)
                    dprev_new = jnp.zeros((SSD_STATE, LANES), F32)
                    dec_lane = jnp.zeros((1, LANES), F32)
                    dt_lanes = jnp.zeros((SSD_CHUNK, LANES), F32)
                    for hh in range(2):
                        h = g * 4 + pr * 2 + hh
                        lm = (lane1 >= 64) if hh else (lane1 < 64)
                        oh_l = (c_i == h).astype(F32)
                        oh_s = (r_i == h).astype(F32)
                        _, dt_col, cs_last, lmat, ecs_col, decay_col = _ssd_head_terms(cs, cst, ecs, dt_c, h, tri)
                        gm = cbm * lmat
                        xm = jnp.where(lm, xp, 0.0)
                        xdt = xm * dt_col
                        dym = jnp.where(lm, dyp, 0.0)
                        prevm = jnp.where(lm, prev, 0.0)
                        dsm = jnp.where(lm, ds_all, 0.0)
                        bdec = bg * decay_col
                        dxdt = _dot_tn(gm, dym) + _dot(bdec, dsm)
                        dxdt_p = dxdt_p + dxdt
                        ddtx = ddtx + oh_l * jnp.sum(dxdt * xm, axis=1, keepdims=True)
                        dt_lanes = dt_lanes + jnp.where(lm, dt_col, 0.0)
                        dgm = _dot_nt(dym, xdt)
                        dcb_m = dcb_m + dgm * lmat
                        w = dgm * gm
                        cacc = cacc + oh_l * jnp.sum(w, axis=1, keepdims=True)
                        racc = racc - oh_s * jnp.sum(w, axis=0, keepdims=True)
                        dce = _dot_nt(dym, prevm)
                        dcg = dcg + dce * ecs_col
                        cacc = cacc + oh_l * (jnp.sum(dce * cg, axis=1, keepdims=True) * ecs_col)
                        dprev_new = dprev_new + _dot_tn(cg * ecs_col, dym)
                        dbdec = _dot_nt(xdt, dsm)
                        dbg = dbg + dbdec * decay_col
                        dd = jnp.sum(dbdec * bg, axis=1, keepdims=True) * decay_col
                        cacc = cacc - oh_l * dd
                        cd = jnp.exp(cs_last)
                        dlast = jnp.sum(dd, axis=0, keepdims=True) + jnp.sum(
                            jnp.sum(dsm * prevm, axis=1, keepdims=True), axis=0, keepdims=True) * cd
                        cacc = cacc + jnp.where((r_i == SSD_CHUNK - 1) & (c_i == h), dlast, 0.0)
                        dec_lane = dec_lane + jnp.where(lm, cd, 0.0)
                    dstate[pi] = ds_all * dec_lane + dprev_new
                    dxact[sl, psl] = dxdt_p * dt_lanes + dyp * d_ref[:, psl]
                dcg = dcg + _dot(dcb_m, bg)
                dbg = dbg + _dot_tn(dcb_m, cg)
                dxact[sl, 512 + g * 128:512 + (g + 1) * 128] = dbg
                dxact[sl, 768 + g * 128:768 + (g + 1) * 128] = dcg
            dcs = cacc + racc.T
            dadt = _dot_f32((r_i <= c_i).astype(F32), dcs)
            ddt = dadt * a_v + ddtx
            da_ref[...] += _sum0(dadt * dt_c)
            ddt_raw = ddt * _sigmoid(dt_raw[sl])
            ddt_ref[sl, :] = ddt_raw
            ddtb_ref[...] += _sum0(ddt_raw)
        dacc = dxact[...] * (sig * (1.0 + acc * (1.0 - sig)))
        dcb_ref[...] += _sum0(dacc)
        for k in range(4):
            dcw_ref[k:k + 1, :] += _sum0(dacc * xpad[SUBLANES - 3 + k:SUBLANES - 3 + k + tm, :])
        dpad[0:tm, :] = dacc
        dpad[tm:tm + SUBLANES, :] = dnext[...]
        dx = cw_v[0:1, :] * dpad[3:3 + tm, :]
        for k in range(1, 4):
            dx = dx + cw_v[k:k + 1, :] * dpad[3 - k:3 - k + tm, :]
        dxbc_ref[...] = dx
        dnext[...] = dacc[0:SUBLANES, :]

    rev = lambda i: nt - 1 - i
    halo_map = lambda i: (jnp.maximum(rev(i) * hb - 1, 0), 0)
    rrow = lambda n, col=0: pl.BlockSpec((tm, n), lambda i: (rev(i), col))
    return pl.pallas_call(
        body, name="ssd_bwd", grid=(nt,),
        in_specs=[rrow(SSD_WIDTH), rrow(SSD_XBC), pl.BlockSpec((SUBLANES, SSD_XBC), halo_map),
                  rrow(SSD_WIDTH, P_Z // SSD_WIDTH), rrow(LANES, P_DT // LANES), rrow(SSD_WIDTH),
                  pl.BlockSpec((ncq, 4, SSD_STATE, LANES), lambda i: (rev(i), 0, 0, 0)),
                  _const((4, SSD_XBC)), _const((1, SSD_XBC)), _const((1, LANES)), _const((1, LANES)),
                  _const((1, SSD_WIDTH)), _const((1, SSD_WIDTH))],
        out_specs=[rrow(SSD_XBC), rrow(SSD_WIDTH), rrow(LANES), _const((SUBLANES, SSD_XBC)), _const((1, SSD_XBC)),
                   _const((1, LANES)), _const((1, LANES)), _const((1, SSD_WIDTH)), _const((1, SSD_WIDTH))],
        out_shape=[jax.ShapeDtypeStruct((t, SSD_XBC), F32), jax.ShapeDtypeStruct((t, SSD_WIDTH), F32),
                   jax.ShapeDtypeStruct((t, LANES), F32), jax.ShapeDtypeStruct((SUBLANES, SSD_XBC), F32),
                   jax.ShapeDtypeStruct((1, SSD_XBC), F32), jax.ShapeDtypeStruct((1, LANES), F32),
                   jax.ShapeDtypeStruct((1, LANES), F32), jax.ShapeDtypeStruct((1, SSD_WIDTH), F32),
                   jax.ShapeDtypeStruct((1, SSD_WIDTH), F32)],
        scratch_shapes=[pltpu.VMEM((tm + SUBLANES, SSD_XBC), F32), pltpu.VMEM((tm, SSD_XBC), F32),
                        pltpu.VMEM((tm, SSD_XBC), F32), pltpu.VMEM((tm + SUBLANES, SSD_XBC), F32),
                        pltpu.VMEM((4, SSD_STATE, LANES), F32), pltpu.VMEM((SUBLANES, SSD_XBC), F32)],
        compiler_params=_cparams(("arbitrary",)),
    )(dycat, proj, proj, proj, proj, yy, states, cw, cb, dtb, a_neg, d_lanes, nw)


def _cmul_add(ar, ai, br, bi, cr, ci):
    return ar + br * cr - bi * ci, ai + br * ci + bi * cr


def _s5_fwd(proj, bre, bim, cre, cim, d_skip, glu_w, glu_b, coef):
    t = proj.shape[0]
    tm = SCAN_TM
    ng = tm // SUBLANES

    def body(u_ref, bre_ref, bim_ref, cre_ref, cim_ref, d_ref, w_ref, b_ref, coef_ref,
             y_ref, y2_ref, hre_ref, him_ref, carry):
        i = pl.program_id(0)

        @pl.when(i == 0)
        def _():
            carry[...] = jnp.zeros_like(carry)

        u = u_ref[...]
        hre_ref[...] = _dot(u, bre_ref[...])
        him_ref[...] = _dot(u, bim_ref[...])

        def step(gi, car):
            cr_, ci_ = car
            rows = pl.ds(pl.multiple_of(gi * SUBLANES, SUBLANES), SUBLANES)
            r = hre_ref[rows, :]
            m = him_ref[rows, :]
            for k, sh in enumerate((1, 2, 4)):
                r, m = _cmul_add(r, m, coef_ref[k, 0], coef_ref[k, 1], pltpu.roll(r, sh, 0), pltpu.roll(m, sh, 0))
            r, m = _cmul_add(r, m, coef_ref[3, 0], coef_ref[3, 1], cr_, ci_)
            hre_ref[rows, :] = r
            him_ref[rows, :] = m
            return (jnp.broadcast_to(r[SUBLANES - 1:SUBLANES, :], r.shape),
                    jnp.broadcast_to(m[SUBLANES - 1:SUBLANES, :], m.shape))

        cr_, ci_ = lax.fori_loop(0, ng, step, (carry[0], carry[1]))
        carry[0] = cr_
        carry[1] = ci_
        y2 = _dot(hre_ref[...], cre_ref[...]) - _dot(him_ref[...], cim_ref[...]) + d_ref[...] * u
        y2_ref[...] = y2
        ya = _gelu(y2)
        y_ref[...] = ya * _sigmoid(_dot(ya, w_ref[...]) + b_ref[...])

    return pl.pallas_call(
        body, name="s5_fwd", grid=(t // tm,),
        in_specs=[pl.BlockSpec((tm, S5_WIDTH), lambda i: (i, P_U // S5_WIDTH)),
                  _const((S5_WIDTH, S5_NSTATE)), _const((S5_WIDTH, S5_NSTATE)), _const((S5_NSTATE, S5_WIDTH)),
                  _const((S5_NSTATE, S5_WIDTH)), _const((1, S5_WIDTH)), _const((S5_WIDTH, S5_WIDTH)),
                  _const((1, S5_WIDTH)), _const((5, 2, SUBLANES, S5_NSTATE))],
        out_specs=[_rows(tm, S5_WIDTH), _rows(tm, S5_WIDTH), _rows(tm, S5_NSTATE), _rows(tm, S5_NSTATE)],
        out_shape=[jax.ShapeDtypeStruct((t, S5_WIDTH), F32), jax.ShapeDtypeStruct((t, S5_WIDTH), F32),
                   jax.ShapeDtypeStruct((t, S5_NSTATE), F32), jax.ShapeDtypeStruct((t, S5_NSTATE), F32)],
        scratch_shapes=[pltpu.VMEM((2, SUBLANES, S5_NSTATE), F32)],
        compiler_params=_cparams(("arbitrary",)),
    )(proj, bre, bim, cre, cim, d_skip, glu_w, glu_b, coef)


def _s5_bwd(dycat, proj, y2, hre, him, bre, bim, cre, cim, d_skip, glu_w, glu_b, rcoef):
    t = proj.shape[0]
    tm = SCAN_TM
    nt = t // tm
    ng = tm // SUBLANES
    hb = tm // SUBLANES

    def body(dy_ref, u_ref, y2_ref, hre_ref, him_ref, hre_halo, him_halo, bre_ref, bim_ref, cre_ref, cim_ref, d_ref,
             w_ref, b_ref, coef_ref,
             du_ref, dbre_ref, dbim_ref, dcre_ref, dcim_ref, dlam_ref, dd_ref, dw_ref, dgb_ref,
             gre, gim, hpre, hpim, carry):
        i = pl.program_id(0)

        @pl.when(i == 0)
        def _():
            for r in (dbre_ref, dbim_ref, dcre_ref, dcim_ref, dlam_ref, dd_ref, dw_ref, dgb_ref, carry):
                r[...] = jnp.zeros_like(r)

        u = u_ref[...]
        y2 = y2_ref[...]
        dout = dy_ref[...]
        ya = _gelu(y2)
        sg = _sigmoid(_dot(ya, w_ref[...]) + b_ref[...])
        dv = dout * ya * sg * (1.0 - sg)
        dya = dout * sg + _dot_nt(dv, w_ref[...])
        dw_ref[...] += _dot_tn(ya, dv)
        dgb_ref[...] += _sum0(dv)
        dy2 = dya * _gelu_grad(y2)
        dd_ref[...] += _sum0(dy2 * u)
        hre_v = hre_ref[...]
        him_v = him_ref[...]
        dcre_ref[...] += _dot_tn(hre_v, dy2)
        dcim_ref[...] -= _dot_tn(him_v, dy2)
        gre[...] = _dot_nt(dy2, cre_ref[...])
        gim[...] = -_dot_nt(dy2, cim_ref[...])
        first = i == nt - 1
        hpre[0:SUBLANES, :] = jnp.where(first, 0.0, hre_halo[...])
        hpim[0:SUBLANES, :] = jnp.where(first, 0.0, him_halo[...])
        hpre[SUBLANES:SUBLANES + tm, :] = hre_v
        hpim[SUBLANES:SUBLANES + tm, :] = him_v
        row0 = lax.broadcasted_iota(jnp.int32, (SUBLANES, S5_NSTATE), 0) == 0

        def step(k, car):
            cr_, ci_, dlr, dli = car
            gi = ng - 1 - k
            rows = pl.ds(pl.multiple_of(gi * SUBLANES, SUBLANES), SUBLANES)
            nrows = pl.ds(pl.multiple_of(gi * SUBLANES + SUBLANES, SUBLANES), SUBLANES)
            r = gre[rows, :]
            m = gim[rows, :]
            for kk, sh in enumerate((1, 2, 4)):
                r, m = _cmul_add(r, m, coef_ref[kk, 0], coef_ref[kk, 1], pltpu.roll(r, SUBLANES - sh, 0),
                                 pltpu.roll(m, SUBLANES - sh, 0))
            r, m = _cmul_add(r, m, coef_ref[3, 0], coef_ref[3, 1], cr_, ci_)
            gre[rows, :] = r
            gim[rows, :] = m
            pr_ = hpre[rows, :]
            pm_ = hpim[rows, :]
            hr_ = jnp.where(row0, jnp.broadcast_to(pr_[SUBLANES - 1:SUBLANES, :], pr_.shape),
                            pltpu.roll(hpre[nrows, :], 1, 0))
            hm_ = jnp.where(row0, jnp.broadcast_to(pm_[SUBLANES - 1:SUBLANES, :], pm_.shape),
                            pltpu.roll(hpim[nrows, :], 1, 0))
            dlr = dlr + hr_ * r + hm_ * m
            dli = dli + hr_ * m - hm_ * r
            return (jnp.broadcast_to(r[0:1, :], r.shape), jnp.broadcast_to(m[0:1, :], m.shape), dlr, dli)

        z8 = jnp.zeros((SUBLANES, S5_NSTATE), F32)
        cr_, ci_, dlr, dli = lax.fori_loop(0, ng, step, (carry[0], carry[1], z8, z8))
        carry[0] = cr_
        carry[1] = ci_
        dlam_ref[0] += dlr
        dlam_ref[1] += dli
        g_re = gre[...]
        g_im = gim[...]
        du_ref[...] = dy2 * d_ref[...] + _dot_nt(g_re, bre_ref[...]) + _dot_nt(g_im, bim_ref[...])
        dbre_ref[...] += _dot_tn(u, g_re)
        dbim_ref[...] += _dot_tn(u, g_im)

    rev = lambda i: nt - 1 - i
    rrow = lambda n, col=0: pl.BlockSpec((tm, n), lambda i: (rev(i), col))
    halo = pl.BlockSpec((SUBLANES, S5_NSTATE), lambda i: (jnp.maximum(rev(i) * hb - 1, 0), 0))
    return pl.pallas_call(
        body, name="s5_bwd", grid=(nt,),
        in_specs=[rrow(S5_WIDTH, 512 // S5_WIDTH), rrow(S5_WIDTH, P_U // S5_WIDTH), rrow(S5_WIDTH),
                  rrow(S5_NSTATE), rrow(S5_NSTATE), halo, halo,
                  _const((S5_WIDTH, S5_NSTATE)), _const((S5_WIDTH, S5_NSTATE)), _const((S5_NSTATE, S5_WIDTH)),
                  _const((S5_NSTATE, S5_WIDTH)), _const((1, S5_WIDTH)), _const((S5_WIDTH, S5_WIDTH)),
                  _const((1, S5_WIDTH)), _const((5, 2, SUBLANES, S5_NSTATE))],
        out_specs=[rrow(S5_WIDTH), _const((S5_WIDTH, S5_NSTATE)), _const((S5_WIDTH, S5_NSTATE)),
                   _const((S5_NSTATE, S5_WIDTH)), _const((S5_NSTATE, S5_WIDTH)), _const((2, SUBLANES, S5_NSTATE)),
                   _const((1, S5_WIDTH)), _const((S5_WIDTH, S5_WIDTH)), _const((1, S5_WIDTH))],
        out_shape=[jax.ShapeDtypeStruct((t, S5_WIDTH), F32), jax.ShapeDtypeStruct((S5_WIDTH, S5_NSTATE), F32),
                   jax.ShapeDtypeStruct((S5_WIDTH, S5_NSTATE), F32), jax.ShapeDtypeStruct((S5_NSTATE, S5_WIDTH), F32),
                   jax.ShapeDtypeStruct((S5_NSTATE, S5_WIDTH), F32),
                   jax.ShapeDtypeStruct((2, SUBLANES, S5_NSTATE), F32), jax.ShapeDtypeStruct((1, S5_WIDTH), F32),
                   jax.ShapeDtypeStruct((S5_WIDTH, S5_WIDTH), F32), jax.ShapeDtypeStruct((1, S5_WIDTH), F32)],
        scratch_shapes=[pltpu.VMEM((tm, S5_NSTATE), F32), pltpu.VMEM((tm, S5_NSTATE), F32),
                        pltpu.VMEM((tm + SUBLANES, S5_NSTATE), F32), pltpu.VMEM((tm + SUBLANES, S5_NSTATE), F32),
                        pltpu.VMEM((2, SUBLANES, S5_NSTATE), F32)],
        compiler_params=_cparams(("arbitrary",)),
    )(dycat, proj, y2, hre, him, hre, him, bre, bim, cre, cim, d_skip, glu_w, glu_b, rcoef)


def _rg_gates(xc, wa, ba, wx, bx, nsp):
    r = _sigmoid(_dot(xc, wa) + ba)
    ig = _sigmoid(_dot(xc, wx) + bx)
    log_a = nsp * r
    a = jnp.exp(log_a)
    mult = jnp.sqrt(-_expm1(2.0 * log_a))
    return r, ig, a, mult


def _rg_fwd(proj, cw, cb, wa, ba, wx, bx, nsp):
    t = proj.shape[0]
    tm = SCAN_TM
    ng = tm // SUBLANES
    hb = tm // SUBLANES

    def body(x_ref, halo_ref, gt_ref, cw_ref, cb_ref, wa_ref, ba_ref, wx_ref, bx_ref, nsp_ref,
             y_ref, h_ref, xpad, abuf, carry):
        i = pl.program_id(0)

        @pl.when(i == 0)
        def _():
            carry[...] = jnp.zeros_like(carry)

        xpad[0:SUBLANES, :] = jnp.where(i > 0, halo_ref[...], 0.0)
        xpad[SUBLANES:SUBLANES + tm, :] = x_ref[...]
        xc = cb_ref[...] + _conv_taps(xpad, cw_ref[...], tm, SUBLANES - 3)
        _, ig, a, mult = _rg_gates(xc, wa_ref[...], ba_ref[...], wx_ref[...], bx_ref[...], nsp_ref[...])
        abuf[...] = a
        h_ref[...] = mult * (ig * xc)
        sub = lax.broadcasted_iota(jnp.int32, (SUBLANES, RG_WIDTH), 0)

        def step(gi, car):
            rows = pl.ds(pl.multiple_of(gi * SUBLANES, SUBLANES), SUBLANES)
            av = abuf[rows, :]
            bv = h_ref[rows, :]
            for sh in (1, 2, 4):
                m = sub >= sh
                bv = jnp.where(m, av * pltpu.roll(bv, sh, 0) + bv, bv)
                av = jnp.where(m, av * pltpu.roll(av, sh, 0), av)
            hv = bv + av * car
            h_ref[rows, :] = hv
            return jnp.broadcast_to(hv[SUBLANES - 1:SUBLANES, :], hv.shape)

        carry[...] = lax.fori_loop(0, ng, step, carry[...])
        y_ref[...] = h_ref[...] * _gelu(gt_ref[...])

    return pl.pallas_call(
        body, name="rg_fwd", grid=(t // tm,),
        in_specs=[pl.BlockSpec((tm, RG_WIDTH), lambda i: (i, P_XRG // RG_WIDTH)),
                  pl.BlockSpec((SUBLANES, RG_WIDTH), lambda i: (jnp.maximum(i * hb - 1, 0), P_XRG // RG_WIDTH)),
                  pl.BlockSpec((tm, RG_WIDTH), lambda i: (i, P_GRG // RG_WIDTH)),
                  _const((4, RG_WIDTH)), _const((1, RG_WIDTH)), _const((RG_WIDTH, RG_WIDTH)), _const((1, RG_WIDTH)),
                  _const((RG_WIDTH, RG_WIDTH)), _const((1, RG_WIDTH)), _const((1, RG_WIDTH))],
        out_specs=[_rows(tm, RG_WIDTH), _rows(tm, RG_WIDTH)],
        out_shape=[jax.ShapeDtypeStruct((t, RG_WIDTH), F32), jax.ShapeDtypeStruct((t, RG_WIDTH), F32)],
        scratch_shapes=[pltpu.VMEM((tm + SUBLANES, RG_WIDTH), F32), pltpu.VMEM((tm, RG_WIDTH), F32),
                        pltpu.VMEM((SUBLANES, RG_WIDTH), F32)],
        compiler_params=_cparams(("arbitrary",)),
    )(proj, proj, proj, cw, cb, wa, ba, wx, bx, nsp)


def _rg_bwd(dycat, proj, hs, cw, cb, wa, ba, wx, bx, nsp):
    t = proj.shape[0]
    tm = SCAN_TM
    nt = t // tm
    ng = tm // SUBLANES
    hb = tm // SUBLANES

    def body(dy_ref, x_ref, halo_ref, gt_ref, h_ref, h_halo, cw_ref, cb_ref, wa_ref, ba_ref, wx_ref, bx_ref, nsp_ref,
             dx_ref, dgt_ref, dcw_ref, dcb_ref, dwa_ref, dba_ref, dwx_ref, dbx_ref, dnsp_ref,
             xpad, abuf, gbuf, hpad, dabuf, dpad, carry, dnext):
        i = pl.program_id(0)

        @pl.when(i == 0)
        def _():
            for r in (dcw_ref, dcb_ref, dwa_ref, dba_ref, dwx_ref, dbx_ref, dnsp_ref, carry, dnext):
                r[...] = jnp.zeros_like(r)

        first = i == nt - 1
        xpad[0:SUBLANES, :] = jnp.where(first, 0.0, halo_ref[...])
        xpad[SUBLANES:SUBLANES + tm, :] = x_ref[...]
        cw_v = cw_ref[...]
        xc = cb_ref[...] + _conv_taps(xpad, cw_v, tm, SUBLANES - 3)
        nsp_v = nsp_ref[...]
        r, ig, a, mult = _rg_gates(xc, wa_ref[...], ba_ref[...], wx_ref[...], bx_ref[...], nsp_v)
        abuf[...] = a
        hv = h_ref[...]
        hpad[0:SUBLANES, :] = jnp.where(first, 0.0, h_halo[...])
        hpad[SUBLANES:SUBLANES + tm, :] = hv
        gt = gt_ref[...]
        dout = dy_ref[...]
        dgt_ref[...] = dout * hv * _gelu_grad(gt)
        gbuf[...] = dout * _gelu(gt)
        sub = lax.broadcasted_iota(jnp.int32, (SUBLANES, RG_WIDTH), 0)
        last_row = sub == SUBLANES - 1
        row0 = sub == 0

        def step(k, car):
            gi = ng - 1 - k
            rows = pl.ds(pl.multiple_of(gi * SUBLANES, SUBLANES), SUBLANES)
            nrows = pl.ds(pl.multiple_of(gi * SUBLANES + SUBLANES, SUBLANES), SUBLANES)
            av = abuf[rows, :]
            bv = gbuf[rows, :] + jnp.where(last_row, car, 0.0)
            ev = jnp.where(last_row, 0.0, pltpu.roll(av, SUBLANES - 1, 0))
            for sh in (1, 2, 4):
                m = sub < SUBLANES - sh
                bv = jnp.where(m, bv + ev * pltpu.roll(bv, SUBLANES - sh, 0), bv)
                ev = jnp.where(m, ev * pltpu.roll(ev, SUBLANES - sh, 0), 0.0)
            gbuf[rows, :] = bv
            pv = hpad[rows, :]
            hprev = jnp.where(row0, jnp.broadcast_to(pv[SUBLANES - 1:SUBLANES, :], pv.shape),
                              pltpu.roll(hpad[nrows, :], 1, 0))
            dabuf[rows, :] = bv * hprev
            return jnp.broadcast_to((av * bv)[0:1, :], bv.shape)

        carry[...] = lax.fori_loop(0, ng, step, carry[...])
        gv = gbuf[...]
        da = dabuf[...]
        ix = ig * xc
        dmult = gv * ix
        dig = gv * mult * xc
        dxc = gv * mult * ig
        dlog_a = da * a - dmult * (a * a) / mult
        dnsp_ref[...] += _sum0(dlog_a * r)
        dpr = dlog_a * nsp_v * r * (1.0 - r)
        dpi = dig * ig * (1.0 - ig)
        dxc = dxc + _dot_nt(dpr, wa_ref[...]) + _dot_nt(dpi, wx_ref[...])
        dwa_ref[...] += _dot_tn(xc, dpr)
        dwx_ref[...] += _dot_tn(xc, dpi)
        dba_ref[...] += _sum0(dpr)
        dbx_ref[...] += _sum0(dpi)
        dcb_ref[...] += _sum0(dxc)
        for k in range(4):
            dcw_ref[k:k + 1, :] += _sum0(dxc * xpad[SUBLANES - 3 + k:SUBLANES - 3 + k + tm, :])
        dpad[0:tm, :] = dxc
        dpad[tm:tm + SUBLANES, :] = dnext[...]
        dx = cw_v[0:1, :] * dpad[3:3 + tm, :]
        for k in range(1, 4):
            dx = dx + cw_v[k:k + 1, :] * dpad[3 - k:3 - k + tm, :]
        dx_ref[...] = dx
        dnext[...] = dxc[0:SUBLANES, :]

    rev = lambda i: nt - 1 - i
    rrow = lambda n, col=0: pl.BlockSpec((tm, n), lambda i: (rev(i), col))
    sq = _const((RG_WIDTH, RG_WIDTH))
    vec = _const((1, RG_WIDTH))
    return pl.pallas_call(
        body, name="rg_bwd", grid=(nt,),
        in_specs=[rrow(RG_WIDTH, 768 // RG_WIDTH), rrow(RG_WIDTH, P_XRG // RG_WIDTH),
                  pl.BlockSpec((SUBLANES, RG_WIDTH), lambda i: (jnp.maximum(rev(i) * hb - 1, 0), P_XRG // RG_WIDTH)),
                  rrow(RG_WIDTH, P_GRG // RG_WIDTH), rrow(RG_WIDTH),
                  pl.BlockSpec((SUBLANES, RG_WIDTH), lambda i: (jnp.maximum(rev(i) * hb - 1, 0), 0)),
                  _const((4, RG_WIDTH)), vec, sq, vec, sq, vec, vec],
        out_specs=[rrow(RG_WIDTH), rrow(RG_WIDTH), _const((SUBLANES, RG_WIDTH)), vec, sq, vec, sq, vec, vec],
        out_shape=[jax.ShapeDtypeStruct((t, RG_WIDTH), F32), jax.ShapeDtypeStruct((t, RG_WIDTH), F32),
                   jax.ShapeDtypeStruct((SUBLANES, RG_WIDTH), F32), jax.ShapeDtypeStruct((1, RG_WIDTH), F32),
                   jax.ShapeDtypeStruct((RG_WIDTH, RG_WIDTH), F32), jax.ShapeDtypeStruct((1, RG_WIDTH), F32),
                   jax.ShapeDtypeStruct((RG_WIDTH, RG_WIDTH), F32), jax.ShapeDtypeStruct((1, RG_WIDTH), F32),
                   jax.ShapeDtypeStruct((1, RG_WIDTH), F32)],
        scratch_shapes=[pltpu.VMEM((tm + SUBLANES, RG_WIDTH), F32), pltpu.VMEM((tm, RG_WIDTH), F32),
                        pltpu.VMEM((tm, RG_WIDTH), F32), pltpu.VMEM((tm + SUBLANES, RG_WIDTH), F32),
                        pltpu.VMEM((tm, RG_WIDTH), F32), pltpu.VMEM((tm + SUBLANES, RG_WIDTH), F32),
                        pltpu.VMEM((SUBLANES, RG_WIDTH), F32), pltpu.VMEM((SUBLANES, RG_WIDTH), F32)],
        compiler_params=_cparams(("arbitrary",)),
    )(dycat, proj, proj, proj, hs, hs, cw, cb, wa, ba, wx, bx, nsp)


def _block_diag(blocks):
    g, a, b = blocks.shape
    eye = jnp.eye(g, dtype=blocks.dtype)
    return (eye[:, None, :, None] * blocks[:, :, None, :]).reshape(g * a, g * b)


def _block_diag_extract(m, g):
    a, b = m.shape[0] // g, m.shape[1] // g
    m4 = m.reshape(g, a, g, b)
    idx = jnp.arange(g)
    return m4[idx, :, idx, :]


def _s5_prepare(lam_re, lam_im, log_step, b_re, b_im, c_re, c_im):
    step = jnp.exp(log_step)[:, None]
    mag = jnp.exp(lam_re * step)
    lbr = mag * jnp.cos(lam_im * step)
    lbi = mag * jnp.sin(lam_im * step)
    nr, ni = lbr - 1.0, lbi
    den = lam_re * lam_re + lam_im * lam_im
    cr = (nr * lam_re + ni * lam_im) / den
    ci = (ni * lam_re - nr * lam_im) / den
    bbr = cr[..., None] * b_re - ci[..., None] * b_im
    bbi = cr[..., None] * b_im + ci[..., None] * b_re
    bre = _block_diag(jnp.swapaxes(bbr, 1, 2))
    bim = _block_diag(jnp.swapaxes(bbi, 1, 2))
    cre = _block_diag(jnp.swapaxes(c_re, 1, 2))
    cim = _block_diag(jnp.swapaxes(c_im, 1, 2))
    return lbr.reshape(-1), lbi.reshape(-1), bre, bim, cre, cim


def _s5_scan_coef(lbr, lbi, reverse):
    if reverse:
        lbi = -lbi
    pr, pi = [lbr], [lbi]
    for _ in range(7):
        pr, pi = pr + [pr[-1] * lbr - pi[-1] * lbi], pi + [pr[-1] * lbi + pi[-1] * lbr]
    row = jnp.arange(SUBLANES)[:, None]
    tabs = []
    for sh in (1, 2, 4):
        keep = (row < SUBLANES - sh) if reverse else (row >= sh)
        tabs.append(jnp.stack([jnp.where(keep, pr[sh - 1][None, :], 0.0), jnp.where(keep, pi[sh - 1][None, :], 0.0)]))
    powr = jnp.stack(pr)
    powi = jnp.stack(pi)
    if reverse:
        powr, powi = powr[::-1], powi[::-1]
    tabs.append(jnp.stack([powr, powi]))
    tabs.append(jnp.zeros_like(tabs[-1]))
    return jnp.stack(tabs).astype(F32)


def _xy_peers():
    x, y, c = lax.axis_index("x"), lax.axis_index("y"), lax.axis_index("c")
    return x, y, c, [(1 - x, y), (x, 1 - y), (1 - x, 1 - y)]


def _hbm():
    return pl.BlockSpec(memory_space=pl.ANY)


def _xy_allgather(buf, *, name):
    n, w = buf.shape

    def body(x_ref, out_ref, send_sems, recv_sems, local_sem):
        x, y, c, peers = _xy_peers()
        me = 2 * x + y
        own = pltpu.make_async_copy(x_ref, out_ref.at[me], local_sem)
        own.start()
        sends = []
        for k, (px, py) in enumerate(peers):
            cp = pltpu.make_async_remote_copy(src_ref=x_ref, dst_ref=out_ref.at[me], send_sem=send_sems.at[k],
                                              recv_sem=recv_sems.at[k], device_id=(px, py, c), device_id_type=MESH)
            cp.start()
            sends.append(cp)
        for k, (px, py) in enumerate(peers):
            pltpu.make_async_remote_copy(src_ref=x_ref, dst_ref=out_ref.at[2 * px + py], send_sem=send_sems.at[k],
                                         recv_sem=recv_sems.at[k], device_id=(px, py, c),
                                         device_id_type=MESH).wait_recv()
        for cp in sends:
            cp.wait_send()
        own.wait()

    return pl.pallas_call(
        body, name=name, in_specs=[_hbm()], out_specs=_hbm(),
        out_shape=jax.ShapeDtypeStruct((4, n, w), buf.dtype),
        scratch_shapes=[pltpu.SemaphoreType.DMA((3,)), pltpu.SemaphoreType.DMA((3,)), pltpu.SemaphoreType.DMA],
    )(buf)


def _remote(src, dst, send_sem, recv_sem, dev):
    return pltpu.make_async_remote_copy(src_ref=src, dst_ref=dst, send_sem=send_sem, recv_sem=recv_sem,
                                        device_id=dev, device_id_type=MESH)


LAYER_GATHERED = (
    ("ssd_conv_w", (4, 256), 1), ("rg_conv_w", (4, LANES), 1),
    ("w_in", (1024, W_IN_PAD), 1), ("s5_glu_w", (64, 256), 0), ("w_out", (256, 1024), 0), ("xa_wq", (256, 1024), 0),
    ("xa_wk", (256, 1024), 0), ("xa_wv", (256, 1024), 0), ("xa_wo", (256, 1024), 0), ("mlp_w1", (1024, 1024), 1),
    ("mlp_w2", (1024, 1024), 0),
)
N_GATHERED = len(LAYER_GATHERED)
WAIT_GROUPS = ((0, 1, 2, 3), (4,), (5, 6, 7, 8), (9, 10))
RG_CONV_SHARD = RG_WIDTH // 4
N_GATHER_COPIES = 3 * N_GATHERED * DEPTH


def _gather_part(ref, t, pos):
    _, shp, ax = LAYER_GATHERED[t % N_GATHERED]
    idx = tuple(pl.ds(pos * shp[ax], shp[ax]) if d == ax else slice(None) for d in range(len(shp)))
    return ref.at[idx]


def _gather_start(shards):
    n = len(shards)
    lands = []
    for t, s in enumerate(shards):
        _, shp, ax = LAYER_GATHERED[t % N_GATHERED]
        full = shp[:ax] + (4 * shp[ax],) + shp[ax + 1:]
        lands.append(pltpu.with_memory_space_constraint(lax.empty(full, s.dtype), pltpu.HBM))

    def body(*refs):
        srcs, lnds = refs[:n], refs[n:2 * n]
        send_sems, recv_sems, local_sems = refs[2 * n:2 * n + 3]
        token = refs[-1]
        x, y, c, peers = _xy_peers()
        me = 2 * x + y
        for t in range(n):
            for k, (px, py) in enumerate(peers):
                _remote(srcs[t], _gather_part(lnds[t], t, me), send_sems.at[k * n + t], recv_sems.at[k * n + t],
                        (px, py, c)).start()
            pltpu.make_async_copy(srcs[t], _gather_part(lnds[t], t, me), local_sems.at[t]).start()
        token[...] = jnp.zeros_like(token)

    hbm = pl.BlockSpec(memory_space=pltpu.HBM)
    sem = pl.BlockSpec(memory_space=pltpu.SEMAPHORE)
    outs = pl.pallas_call(
        body, name="weights_gather_start", in_specs=[hbm] * (2 * n),
        out_shape=(pltpu.SemaphoreType.DMA((3 * n,)), pltpu.SemaphoreType.DMA((3 * n,)),
                   pltpu.SemaphoreType.DMA((n,)),
                   *[pltpu.HBM(s.shape, s.dtype) for s in shards], *[pltpu.HBM(a.shape, a.dtype) for a in lands],
                   jax.ShapeDtypeStruct((SUBLANES, LANES), F32)),
        out_specs=(sem, sem, sem, *[hbm] * (2 * n), pl.BlockSpec(memory_space=pltpu.VMEM)),
        input_output_aliases={i: 3 + i for i in range(2 * n)},
        compiler_params=pltpu.CompilerParams(has_side_effects=pltpu.SideEffectType.DATAFLOW_SIDE_EFFECTING),
    )(*[pltpu.with_memory_space_constraint(s, pltpu.HBM) for s in shards], *lands)
    return outs[0], outs[1], outs[2], outs[3:3 + n], outs[3 + n:3 + 2 * n], outs[-1]


def _gather_wait(handle, ts, after, *, name):
    send_sems, recv_sems, local_sems, src_thru, land_thru, _ = handle
    n = len(src_thru)
    m = len(ts)

    def body(*refs):
        srcs, lnds = refs[:m], refs[m:2 * m]
        ssem, rsem, lsem = refs[2 * m:2 * m + 3]
        x, y, c, peers = _xy_peers()
        me = 2 * x + y
        for i, t in enumerate(ts):
            for k, (px, py) in enumerate(peers):
                cp = _remote(srcs[i], _gather_part(lnds[i], t, 2 * px + py), ssem.at[k * n + t], rsem.at[k * n + t],
                             (px, py, c))
                cp.wait_send()
                cp.wait_recv()
            pltpu.make_async_copy(srcs[i], _gather_part(lnds[i], t, me), lsem.at[t]).wait()

    hbm = pl.BlockSpec(memory_space=pltpu.HBM)
    sem = pl.BlockSpec(memory_space=pltpu.SEMAPHORE)
    args = [src_thru[t] for t in ts] + [land_thru[t] for t in ts]
    outs = pl.pallas_call(
        body, name=name, in_specs=[hbm] * (2 * m) + [sem, sem, sem, pl.BlockSpec(memory_space=pl.ANY)],
        out_shape=[pltpu.HBM(a.shape, a.dtype) for a in args], out_specs=[hbm] * (2 * m),
        input_output_aliases={i: i for i in range(2 * m)},
        compiler_params=pltpu.CompilerParams(has_side_effects=pltpu.SideEffectType.DATAFLOW_SIDE_EFFECTING),
    )(*args, send_sems, recv_sems, local_sems, after)
    return outs[:m], outs[m:]


C_CHUNKS = 4
XY_CHUNKS = 4
EW_ROWS = 512


def _c_exchange(g, part):
    _, n, w = g.shape
    n2 = n // 2
    n4 = n // 4
    rq = n4 // C_CHUNKS

    def body(g_ref, got_ref, send_sems, recv_sems):
        x, y, c = lax.axis_index("x"), lax.axis_index("y"), lax.axis_index("c")
        cps = []
        for s in range(4):
            for q in range(C_CHUNKS):
                k = s * C_CHUNKS + q
                cp = _remote(g_ref.at[s, pl.ds((1 - c) * n2 + part * n4 + q * rq, rq), :],
                             got_ref.at[s, pl.ds(q * rq, rq), :], send_sems.at[k], recv_sems.at[k], (x, y, 1 - c))
                cp.start()
                cps.append(cp)
        for cp in cps:
            cp.wait_recv()
        for cp in cps:
            cp.wait_send()

    return pl.pallas_call(
        body, name="grad_c_exchange_%d" % part, in_specs=[_hbm()], out_specs=_hbm(),
        out_shape=jax.ShapeDtypeStruct((4, n4, w), g.dtype),
        scratch_shapes=[pltpu.SemaphoreType.DMA((4 * C_CHUNKS,)), pltpu.SemaphoreType.DMA((4 * C_CHUNKS,))],
    )(g)


XFER_DTYPE = jnp.bfloat16


def _add_own_quarter(g, got, c_arr, part):
    _, n, w = g.shape
    n4 = n // 4
    nb = n4 // EW_ROWS

    def body(c_ref, a_ref, b_ref, o_ref, t_ref):
        sm = a_ref[...] + b_ref[...]
        o_ref[...] = sm.astype(o_ref.dtype)

        @pl.when(pl.program_id(1) == nb - 1)
        def _():
            t_ref[...] = sm[:, EW_ROWS - MISC_ROWS:, :]

    grid_spec = pltpu.PrefetchScalarGridSpec(
        num_scalar_prefetch=1, grid=(4, nb),
        in_specs=[pl.BlockSpec((1, EW_ROWS, w), lambda s, i, c: (s, (2 * c[0] + part) * nb + i, 0)),
                  pl.BlockSpec((1, EW_ROWS, w), lambda s, i, c: (s, i, 0))],
        out_specs=[pl.BlockSpec((1, EW_ROWS, w), lambda s, i, c: (s, i, 0)),
                   pl.BlockSpec((1, MISC_ROWS, w), lambda s, i, c: (s, 0, 0))])
    return pl.pallas_call(
        body, name="grad_add_quarters", grid_spec=grid_spec,
        out_shape=[jax.ShapeDtypeStruct((4, n4, w), XFER_DTYPE), jax.ShapeDtypeStruct((4, MISC_ROWS, w), g.dtype)],
        compiler_params=_cparams(("arbitrary", "arbitrary")),
    )(c_arr, g, got)


def _xy_exchange(arrs):
    na = len(arrs)
    pieces = []
    for a, arr in enumerate(arrs):
        nch = XY_CHUNKS if a == 0 else 1
        rq = arr.shape[1] // nch
        pieces += [(a, pl.ds(q * rq, rq)) for q in range(nch)]
    npc = len(pieces)

    def body(*refs):
        ins, outs = refs[:na], refs[na:2 * na]
        send_sems, recv_sems, local_sems = refs[2 * na:]
        x, y, c, peers = _xy_peers()
        me = 2 * x + y
        own = []
        for j, (a, rows) in enumerate(pieces):
            cp = pltpu.make_async_copy(ins[a].at[me, rows, :], outs[a].at[me, rows, :], local_sems.at[j])
            cp.start()
            own.append(cp)
        sends = []
        for k, (px, py) in enumerate(peers):
            for j, (a, rows) in enumerate(pieces):
                cp = _remote(ins[a].at[2 * px + py, rows, :], outs[a].at[me, rows, :], send_sems.at[k * npc + j],
                             recv_sems.at[k * npc + j], (px, py, c))
                cp.start()
                sends.append(cp)
        for k, (px, py) in enumerate(peers):
            for j, (a, rows) in enumerate(pieces):
                _remote(ins[a].at[me, rows, :], outs[a].at[2 * px + py, rows, :], send_sems.at[k * npc + j],
                        recv_sems.at[k * npc + j], (px, py, c)).wait_recv()
        for cp in sends:
            cp.wait_send()
        for cp in own:
            cp.wait()

    return pl.pallas_call(
        body, name="grad_xy_exchange", in_specs=[_hbm()] * na, out_specs=[_hbm()] * na,
        out_shape=[jax.ShapeDtypeStruct(a.shape, a.dtype) for a in arrs],
        scratch_shapes=[pltpu.SemaphoreType.DMA((3 * npc,)), pltpu.SemaphoreType.DMA((3 * npc,)),
                        pltpu.SemaphoreType.DMA((npc,))],
    )(*arrs)


def _xy_pieces(arrs):
    pieces = []
    for a, arr in enumerate(arrs):
        nch = XY_CHUNKS if a == 0 else 1
        rq = arr.shape[1] // nch
        pieces += [(a, pl.ds(q * rq, rq)) for q in range(nch)]
    return pieces


def _xy_start(arrs):
    na = len(arrs)
    pieces = _xy_pieces(arrs)
    npc = len(pieces)
    lands = [pltpu.with_memory_space_constraint(lax.empty(a.shape, a.dtype), pltpu.HBM) for a in arrs]

    def body(*refs):
        ins, outs = refs[:na], refs[na:2 * na]
        send_sems, recv_sems, local_sems = refs[2 * na:2 * na + 3]
        token = refs[-1]
        x, y, c, peers = _xy_peers()
        me = 2 * x + y
        for k, (px, py) in enumerate(peers):
            for j, (a, rows) in enumerate(pieces):
                _remote(ins[a].at[2 * px + py, rows, :], outs[a].at[me, rows, :], send_sems.at[k * npc + j],
                        recv_sems.at[k * npc + j], (px, py, c)).start()
        for j, (a, rows) in enumerate(pieces):
            pltpu.make_async_copy(ins[a].at[me, rows, :], outs[a].at[me, rows, :], local_sems.at[j]).start()
        token[...] = jnp.zeros_like(token)

    hbm = pl.BlockSpec(memory_space=pltpu.HBM)
    sem = pl.BlockSpec(memory_space=pltpu.SEMAPHORE)
    outs = pl.pallas_call(
        body, name="grad_xy_start", in_specs=[hbm] * (2 * na),
        out_shape=(pltpu.SemaphoreType.DMA((3 * npc,)), pltpu.SemaphoreType.DMA((3 * npc,)),
                   pltpu.SemaphoreType.DMA((npc,)),
                   *[pltpu.HBM(a.shape, a.dtype) for a in arrs], *[pltpu.HBM(a.shape, a.dtype) for a in arrs],
                   jax.ShapeDtypeStruct((SUBLANES, LANES), F32)),
        out_specs=(sem, sem, sem, *[hbm] * (2 * na), pl.BlockSpec(memory_space=pltpu.VMEM)),
        input_output_aliases={i: 3 + i for i in range(2 * na)},
        compiler_params=pltpu.CompilerParams(has_side_effects=pltpu.SideEffectType.DATAFLOW_SIDE_EFFECTING),
    )(*[pltpu.with_memory_space_constraint(a, pltpu.HBM) for a in arrs], *lands)
    return (outs[0], outs[1], outs[2], outs[3:3 + na], outs[3 + na:3 + 2 * na]), outs[-1]


def _xy_wait(handle, after):
    send_sems, recv_sems, local_sems, src_thru, land_thru = handle
    na = len(src_thru)
    pieces = _xy_pieces(src_thru)
    npc = len(pieces)

    def body(*refs):
        ins, outs = refs[:na], refs[na:2 * na]
        ssem, rsem, lsem = refs[2 * na:2 * na + 3]
        x, y, c, peers = _xy_peers()
        me = 2 * x + y
        for k, (px, py) in enumerate(peers):
            for j, (a, rows) in enumerate(pieces):
                cp = _remote(ins[a].at[me, rows, :], outs[a].at[2 * px + py, rows, :], ssem.at[k * npc + j],
                             rsem.at[k * npc + j], (px, py, c))
                cp.wait_send()
                cp.wait_recv()
        for j, (a, rows) in enumerate(pieces):
            pltpu.make_async_copy(ins[a].at[me, rows, :], outs[a].at[me, rows, :], lsem.at[j]).wait()

    hbm = pl.BlockSpec(memory_space=pltpu.HBM)
    sem = pl.BlockSpec(memory_space=pltpu.SEMAPHORE)
    args = list(src_thru) + list(land_thru)
    outs = pl.pallas_call(
        body, name="grad_xy_wait", in_specs=[hbm] * (2 * na) + [sem, sem, sem, pl.BlockSpec(memory_space=pl.ANY)],
        out_shape=[pltpu.HBM(a.shape, a.dtype) for a in args], out_specs=[hbm] * (2 * na),
        input_output_aliases={i: i for i in range(2 * na)},
        compiler_params=pltpu.CompilerParams(has_side_effects=pltpu.SideEffectType.DATAFLOW_SIDE_EFFECTING),
    )(*args, send_sems, recv_sems, local_sems, after)
    return outs[na:]


def _sum4_into_quarter(r, rt, c_arr, part, fbuf):
    _, n4, w = r.shape
    nb = n4 // EW_ROWS

    def body(c_ref, r_ref, t_ref, *rest):
        o_ref = rest[-1]
        o_ref[...] = ((r_ref[0].astype(F32) + r_ref[1].astype(F32)) + r_ref[2].astype(F32)) + r_ref[3].astype(F32)

        @pl.when(pl.program_id(0) == nb - 1)
        def _():
            o_ref[EW_ROWS - MISC_ROWS:, :] = ((t_ref[0] + t_ref[1]) + t_ref[2]) + t_ref[3]

    in_specs = [pl.BlockSpec((4, EW_ROWS, w), lambda i, c: (0, i, 0)),
                pl.BlockSpec((4, MISC_ROWS, w), lambda i, c: (0, 0, 0))]
    args = [c_arr, r, rt]
    aliases = {}
    if fbuf is not None:
        in_specs.append(pl.BlockSpec(memory_space=pl.ANY))
        args.append(fbuf)
        aliases = {3: 0}
    grid_spec = pltpu.PrefetchScalarGridSpec(
        num_scalar_prefetch=1, grid=(nb,), in_specs=in_specs,
        out_specs=pl.BlockSpec((EW_ROWS, w), lambda i, c: ((2 * c[0] + part) * nb + i, 0)))
    return pl.pallas_call(
        body, name="grad_sum4", grid_spec=grid_spec, out_shape=jax.ShapeDtypeStruct((4 * n4, w), F32),
        input_output_aliases=aliases, compiler_params=_cparams(("arbitrary",)),
    )(*args)


C_GATHER_CHUNKS = 8


def _c_allgather_halves(f):
    n, w = f.shape
    n2 = n // 2
    rq = n2 // C_GATHER_CHUNKS

    def body(f_ref, out_ref, send_sems, recv_sems):
        x, y, c = lax.axis_index("x"), lax.axis_index("y"), lax.axis_index("c")
        sends = []
        for q in range(C_GATHER_CHUNKS):
            rows = pl.ds(c * n2 + q * rq, rq)
            cp = _remote(f_ref.at[rows, :], out_ref.at[rows, :], send_sems.at[q], recv_sems.at[q], (x, y, 1 - c))
            cp.start()
            sends.append(cp)
        for q in range(C_GATHER_CHUNKS):
            rows = pl.ds((1 - c) * n2 + q * rq, rq)
            _remote(f_ref.at[rows, :], out_ref.at[rows, :], send_sems.at[q], recv_sems.at[q],
                    (x, y, 1 - c)).wait_recv()
        for cp in sends:
            cp.wait_send()

    return pl.pallas_call(
        body, name="grad_c_allgather", in_specs=[_hbm()], out_specs=_hbm(), input_output_aliases={0: 0},
        out_shape=jax.ShapeDtypeStruct((n, w), f.dtype),
        scratch_shapes=[pltpu.SemaphoreType.DMA((C_GATHER_CHUNKS,)), pltpu.SemaphoreType.DMA((C_GATHER_CHUNKS,))],
    )(f)


def _adamw(w, m, v, g, g_rows=None):
    shape = w.shape
    cols = shape[-1]
    rows = int(math.prod(shape)) // cols
    tr = 256 if rows % 256 == 0 else rows
    from_flat = g_rows is not None
    c1 = 1.0 / (1.0 - ADAM_B1 ** ADAM_STEP)
    c2 = 1.0 / (1.0 - ADAM_B2 ** ADAM_STEP)

    def body(w_ref, m_ref, v_ref, g_ref, *outs):
        gg = g_ref[...]
        nm = ADAM_B1 * m_ref[...] + (1.0 - ADAM_B1) * gg
        nv = ADAM_B2 * v_ref[...] + (1.0 - ADAM_B2) * (gg * gg)
        if from_flat:
            outs[0][...] = gg
        d_ref, nm_ref, nv_ref = outs[-3:]
        nm_ref[...] = nm
        nv_ref[...] = nv
        d_ref[...] = -ADAM_LR * ((nm * c1) / (jnp.sqrt(nv * c2) + ADAM_EPS) + ADAM_WD * w_ref[...])

    spec = pl.BlockSpec((tr, cols), lambda i: (i, 0))
    if from_flat:
        nbl = rows // DEPTH // tr
        assert cols == FLAT and all(r % tr == 0 for r in g_rows) and len(g_rows) == DEPTH == 2
        b0, b1 = g_rows[0] // tr, g_rows[1] // tr
        g_spec = pl.BlockSpec((tr, cols), lambda i: (jnp.where(i < nbl, b0 + i, b1 + i - nbl), 0))
        g_arg = g
    else:
        g_spec = spec
        g_arg = g.reshape(rows, cols)
    n_out = 4 if from_flat else 3
    sds = jax.ShapeDtypeStruct((rows, cols), F32)
    outs = pl.pallas_call(
        body, name="adamw", grid=(rows // tr,), in_specs=[spec, spec, spec, g_spec], out_specs=[spec] * n_out,
        out_shape=[sds] * n_out, compiler_params=_cparams(("arbitrary",)),
    )(w.reshape(rows, cols), m.reshape(rows, cols), v.reshape(rows, cols), g_arg)
    outs = [o.reshape(shape) for o in outs]
    return outs if from_flat else [g] + outs


SMALL_SHARDED = (("s5_glu_w", (2, 64, 256), 1), ("ssd_conv_w", (2, 4, 256), 2), ("rg_conv_w", (2, 4, 64), 2))
REPLICATED = (
    ("ssd_conv_b", (2, 1024)), ("ssd_dt_bias", (2, 8)), ("ssd_a_log", (2, 8)), ("ssd_d", (2, 8)),
    ("ssd_norm_w", (2, 512)), ("s5_lam_re", (2, 16, 64)), ("s5_lam_im", (2, 16, 64)), ("s5_log_step", (2, 16)),
    ("s5_b_re", (2, 16, 64, 16)), ("s5_b_im", (2, 16, 64, 16)), ("s5_c_re", (2, 16, 16, 64)),
    ("s5_c_im", (2, 16, 16, 64)), ("s5_d", (2, 256)), ("s5_glu_b", (2, 256)), ("rg_conv_b", (2, 256)),
    ("rg_wa", (2, 4, 64, 64)), ("rg_ba", (2, 4, 64)), ("rg_wx", (2, 4, 64, 64)), ("rg_bx", (2, 4, 64)),
    ("rg_lambda", (2, 256)), ("ln1_g", (2, 1024)), ("ln1_b", (2, 1024)), ("ln2_g", (2, 1024)), ("ln2_b", (2, 1024)),
    ("ln3_g", (2, 1024)), ("ln3_b", (2, 1024)),
)
WEIGHT_ORDER = (
    "w_in", "w_out", "ssd_conv_w", "ssd_conv_b", "ssd_dt_bias", "ssd_a_log", "ssd_d", "ssd_norm_w", "s5_lam_re",
    "s5_lam_im", "s5_log_step", "s5_b_re", "s5_b_im", "s5_c_re", "s5_c_im", "s5_d", "s5_glu_w", "s5_glu_b",
    "rg_conv_w", "rg_conv_b", "rg_wa", "rg_ba", "rg_wx", "rg_bx", "rg_lambda", "ln1_g", "ln1_b", "xa_wq", "xa_wk",
    "xa_wv", "xa_wo", "ln2_g", "ln2_b", "mlp_w1", "mlp_w2", "ln3_g", "ln3_b",
)


def _size(shape):
    return int(math.prod(shape))


def _round_up(a, b):
    return (a + b - 1) // b * b


SMALL_ELEMS = sum(_size(s) for _, s, _ in SMALL_SHARDED)
REP_ELEMS = sum(_size(s) for _, s in REPLICATED)
REP_QROWS = _round_up(-(-REP_ELEMS // (4 * FLAT)), 8)
assert SMALL_ELEMS <= MISC_REP_ROW * FLAT and MISC_REP_ROW + REP_QROWS <= MISC_ROWS


def _pack_shards(tensors, names_shapes):
    return jnp.concatenate([tensors[n].reshape(-1) for n, *_ in names_shapes])


def _unpack(flat, names_shapes):
    out, off = {}, 0
    for n, s, *_ in names_shapes:
        out[n] = flat[off:off + _size(s)].reshape(s)
        off += _size(s)
    return out


def _split_shards(full, names_shapes):
    rows = []
    for k in range(4):
        parts = []
        for n, s, ax in names_shapes:
            w = s[ax]
            parts.append(lax.slice_in_dim(full[n], k * w, (k + 1) * w, axis=ax).reshape(-1))
        rows.append(jnp.concatenate(parts))
    return jnp.stack(rows)


def _pack_cols(w):
    pad = jnp.zeros((w.shape[0], LANES - SSD_HEADS), w.dtype)
    return jnp.concatenate([w[:, O_XBC:O_XBC + 1024], w[:, O_Z:O_Z + 512], w[:, O_U:O_U + 256],
                            w[:, O_XRG:O_XRG + 256], w[:, O_GRG:O_GRG + 256], w[:, O_DT:O_DT + 8], pad], axis=1)


def _unpack_cols(w):
    return jnp.concatenate([w[:, P_Z:P_Z + 512], w[:, P_XBC:P_XBC + 1024], w[:, P_DT:P_DT + 8],
                            w[:, P_U:P_U + 256], w[:, P_XRG:P_XRG + 256], w[:, P_GRG:P_GRG + 256]], axis=1)


def _lanes(v, width):
    return jnp.pad(v, (0, width - v.shape[0])).reshape(1, width)


def _layer_params(rep, l):
    p = {}
    p["ssd_cb"] = rep["ssd_conv_b"][l].reshape(1, -1)
    p["ssd_dtb"] = _lanes(rep["ssd_dt_bias"][l], LANES)
    p["ssd_a"] = _lanes(-jnp.exp(rep["ssd_a_log"][l]), LANES)
    p["ssd_d"] = jnp.repeat(rep["ssd_d"][l], 64).reshape(1, -1)
    p["ssd_nw"] = rep["ssd_norm_w"][l].reshape(1, -1)
    s5_args = tuple(rep[n][l] for n in ("s5_lam_re", "s5_lam_im", "s5_log_step", "s5_b_re", "s5_b_im", "s5_c_re",
                                        "s5_c_im"))
    (lbr, lbi, bre, bim, cre, cim), p["s5_vjp"] = jax.vjp(_s5_prepare, *s5_args)
    p.update(s5_bre=bre, s5_bim=bim, s5_cre=cre, s5_cim=cim)
    p["s5_coef"] = _s5_scan_coef(lbr, lbi, False)
    p["s5_rcoef"] = _s5_scan_coef(lbr, lbi, True)
    p["s5_d"] = rep["s5_d"][l].reshape(1, -1)
    p["s5_gb"] = rep["s5_glu_b"][l].reshape(1, -1)
    p["rg_cb"] = rep["rg_conv_b"][l].reshape(1, -1)
    p["rg_wa"] = _block_diag(rep["rg_wa"][l])
    p["rg_wx"] = _block_diag(rep["rg_wx"][l])
    p["rg_ba"] = rep["rg_ba"][l].reshape(1, -1)
    p["rg_bx"] = rep["rg_bx"][l].reshape(1, -1)
    p["rg_nsp"] = (-RG_C * jax.nn.softplus(-rep["rg_lambda"][l])).reshape(1, -1)
    p["rg_dnsp"] = RG_C * jax.nn.sigmoid(-rep["rg_lambda"][l])
    for n in ("ln1_g", "ln1_b", "ln2_g", "ln2_b", "ln3_g", "ln3_b"):
        p[n] = rep[n][l].reshape(1, -1)
    return p


def _layer_fwd(h, mem, p, fetch):
    s = {"h0": h}
    p.update(fetch(0, h))
    proj = _mm(h, p["w_in"], name="in_proj")
    s["proj"] = proj
    y_ssd, s["ssd_yy"], s["ssd_states"] = _ssd_fwd(proj, p["ssd_cw"], p["ssd_cb"], p["ssd_dtb"], p["ssd_a"],
                                                     p["ssd_d"], p["ssd_nw"])
    y_s5, s["s5_y2"], s["s5_hre"], s["s5_him"] = _s5_fwd(proj, p["s5_bre"], p["s5_bim"], p["s5_cre"], p["s5_cim"],
                                                         p["s5_d"], p["s5_glu_w"], p["s5_gb"], p["s5_coef"])
    y_rg, s["rg_h"] = _rg_fwd(proj, p["rg_cw"], p["rg_cb"], p["rg_wa"], p["rg_ba"], p["rg_wx"], p["rg_bx"],
                              p["rg_nsp"])
    s["ys"] = [y_ssd, y_s5, y_rg]
    p.update(fetch(1, y_rg))
    h1, s["xh1"], s["rs1"] = _outproj_ln_fwd(s["ys"], h, p["w_out"], p["ln1_g"], p["ln1_b"])
    s["h1"] = h1
    p.update(fetch(2, h1))
    kb = _mm(mem, p["xa_wk"], name="mem_proj")
    vb = _mm(mem, p["xa_wv"], name="mem_proj")
    s["kb"], s["vb"] = kb, vb
    h2, s["xh2"], s["rs2"], s["attn_o"] = _attn_ln_fwd(h1, p["xa_wq"], p["xa_wo"], kb, vb, p["ln2_g"], p["ln2_b"])
    s["h2"] = h2
    p.update(fetch(3, h2))
    h3, s["xh3"], s["rs3"], s["mlp_hdn"] = _mlp_ln_fwd(h2, p["mlp_w1"], p["mlp_w2"], p["ln3_g"], p["ln3_b"])
    return h3, s


def _layer_bwd(dh3, mem, p, s, l, gbuf):
    g = {}
    dr3, du, dh2, g["ln3_g"], g["ln3_b"] = _mlp_ln_bwd(dh3, s["xh3"], s["rs3"], p["ln3_g"], s["mlp_hdn"],
                                                        p["mlp_w1"], p["mlp_w2"])
    gbuf = _wgrad_flat(s["h2"], du, gbuf, mode="colblk", row_off=_grad_row("mlp_w1", l), name="wgrad_mlp_w1")
    gbuf = _wgrad_flat(s["mlp_hdn"], dr3, gbuf, mode="rowblk", row_off=_grad_row("mlp_w2", l), name="wgrad_mlp_w2")
    dr2, dq, dh1, dkb, dvb, g["ln2_g"], g["ln2_b"] = _attn_ln_bwd(dh2, s["xh2"], s["rs2"], p["ln2_g"], s["h1"],
                                                                   p["xa_wq"], p["xa_wo"], s["kb"], s["vb"])
    for n, a_op, g_op in (("xa_wo", s["attn_o"], dr2), ("xa_wq", s["h1"], dq), ("xa_wk", mem, dkb),
                          ("xa_wv", mem, dvb)):
        gbuf = _wgrad_flat(a_op, g_op, gbuf, mode="rows4", row_off=_grad_row(n, l), name="wgrad_" + n)
    dr1, dres, dycat, g["ln1_g"], g["ln1_b"] = _outproj_ln_bwd(dh1, s["xh1"], s["rs1"], p["ln1_g"], p["w_out"])
    gbuf = _wgrad_flat(s["ys"], dr1, gbuf, mode="rows4", row_off=_grad_row("w_out", l), name="wgrad_w_out")
    proj = s["proj"]
    (dxbc, dz, ddt, dcw, dcb, ddtb, da_neg, dd_l, dnw) = _ssd_bwd(
        dycat, proj, s["ssd_yy"], s["ssd_states"], p["ssd_cw"], p["ssd_cb"], p["ssd_dtb"], p["ssd_a"], p["ssd_d"],
        p["ssd_nw"])
    g["ssd_conv_w"] = dcw[0:4]
    g["ssd_conv_b"] = dcb[0]
    g["ssd_dt_bias"] = ddtb[0, :SSD_HEADS]
    g["ssd_a_log"] = da_neg[0, :SSD_HEADS] * p["ssd_a"][0, :SSD_HEADS]
    g["ssd_d"] = dd_l.reshape(SSD_HEADS, 64).sum(axis=1)
    g["ssd_norm_w"] = dnw[0]
    (du_s5, dbre, dbim, dcre, dcim, dlam, dd5, dgw, dgb) = _s5_bwd(
        dycat, proj, s["s5_y2"], s["s5_hre"], s["s5_him"], p["s5_bre"], p["s5_bim"], p["s5_cre"], p["s5_cim"],
        p["s5_d"], p["s5_glu_w"], p["s5_gb"], p["s5_rcoef"])
    dl = dlam.sum(axis=1)
    s5g = p["s5_vjp"]((dl[0], dl[1], dbre, dbim, dcre, dcim))
    for n, v in zip(("s5_lam_re", "s5_lam_im", "s5_log_step", "s5_b_re", "s5_b_im", "s5_c_re", "s5_c_im"), s5g):
        g[n] = v
    g["s5_d"] = dd5[0]
    g["s5_glu_w"] = dgw
    g["s5_glu_b"] = dgb[0]
    (dxrg, dgrg, drcw, drcb, dwa, dba, dwx, dbx, dnsp) = _rg_bwd(
        dycat, proj, s["rg_h"], p["rg_cw"], p["rg_cb"], p["rg_wa"], p["rg_ba"], p["rg_wx"], p["rg_bx"], p["rg_nsp"])
    g["rg_conv_w"] = drcw[0:4]
    g["rg_conv_b"] = drcb[0]
    g["rg_wa"] = _block_diag_extract(dwa, RG_BLOCKS)
    g["rg_wx"] = _block_diag_extract(dwx, RG_BLOCKS)
    g["rg_ba"] = dba.reshape(RG_BLOCKS, RG_BLOCK_DIM)
    g["rg_bx"] = dbx.reshape(RG_BLOCKS, RG_BLOCK_DIM)
    g["rg_lambda"] = dnsp[0] * p["rg_dnsp"]
    dproj = [dxbc, dz, du_s5, dxrg, dgrg, ddt]
    g["w_in"] = _unpack_cols(_wgrad_in(s["h0"], dproj))
    dh0 = _in_proj_bwd(dproj, p["w_in"], dres)
    for n in ("ln1_g", "ln1_b", "ln2_g", "ln2_b", "ln3_g", "ln3_b"):
        g[n] = g[n][0]
    return dh0, g, gbuf


def _local_step(h, memf, target, rep, fetch):
    params, saved = [], []
    for l in range(DEPTH):
        p = _layer_params(rep, l)
        params.append(p)
        h, s = _layer_fwd(h, memf, p, functools.partial(fetch, l))
        saved.append(s)
    loss11, dh = _loss_fwd_bwd(h, target)
    grads = [None] * DEPTH
    gbuf = None
    c_arr = lax.axis_index("c").astype(jnp.int32).reshape(1)
    for l in reversed(range(DEPTH)):
        dh, grads[l], gbuf = _layer_bwd(dh, memf, params[l], saved[l], l, gbuf)
        if l == DEPTH - 1:
            gbuf = lax.dynamic_update_slice(
                gbuf, _w_in_block(grads[l]["w_in"], jnp.zeros((4, MISC_ROWS, FLAT), F32)),
                (0, _grad_row("w_in", l), 0))
            xy_handle, token = _xy_start(_chip_sums(gbuf, c_arr, 0))
            params[0]["ln3_g"] = params[0]["ln3_g"] + token[0:1, 0:1]
    gsmall = {n: jnp.stack([grads[l][n] for l in range(DEPTH)]) for n in grads[0] if n != "w_in"}
    return loss11, dh, gsmall, grads[0]["w_in"], gbuf, xy_handle, c_arr


def _w_in_block(gw, tail):
    gw = jnp.pad(gw.reshape(D_MODEL, 4, W_IN_SHARD), ((0, 0), (0, 0), (0, W_IN_PAD - W_IN_SHARD)))
    return jnp.concatenate([jnp.transpose(gw, (1, 0, 2)).reshape(4, W_IN_PAD, FLAT), tail], axis=1)


def _chip_sums(gbuf, c_arr, part):
    return list(_add_own_quarter(gbuf, _c_exchange(gbuf, part), c_arr, part))


def kernel(x, mem, w_in, w_out, ssd_conv_w, ssd_conv_b, ssd_dt_bias, ssd_a_log, ssd_d, ssd_norm_w, s5_lam_re, s5_lam_im, s5_log_step, s5_b_re, s5_b_im, s5_c_re, s5_c_im, s5_d, s5_glu_w, s5_glu_b, rg_conv_w, rg_conv_b, rg_wa, rg_ba, rg_wx, rg_bx, rg_lambda, ln1_g, ln1_b, xa_wq, xa_wk, xa_wv, xa_wo, ln2_g, ln2_b, mlp_w1, mlp_w2, ln3_g, ln3_b, loss_target, m_w_in, m_w_out, m_ssd_conv_w, m_ssd_conv_b, m_ssd_dt_bias, m_ssd_a_log, m_ssd_d, m_ssd_norm_w, m_s5_lam_re, m_s5_lam_im, m_s5_log_step, m_s5_b_re, m_s5_b_im, m_s5_c_re, m_s5_c_im, m_s5_d, m_s5_glu_w, m_s5_glu_b, m_rg_conv_w, m_rg_conv_b, m_rg_wa, m_rg_ba, m_rg_wx, m_rg_bx, m_rg_lambda, m_ln1_g, m_ln1_b, m_xa_wq, m_xa_wk, m_xa_wv, m_xa_wo, m_ln2_g, m_ln2_b, m_mlp_w1, m_mlp_w2, m_ln3_g, m_ln3_b, v_w_in, v_w_out, v_ssd_conv_w, v_ssd_conv_b, v_ssd_dt_bias, v_ssd_a_log, v_ssd_d, v_ssd_norm_w, v_s5_lam_re, v_s5_lam_im, v_s5_log_step, v_s5_b_re, v_s5_b_im, v_s5_c_re, v_s5_c_im, v_s5_d, v_s5_glu_w, v_s5_glu_b, v_rg_conv_w, v_rg_conv_b, v_rg_wa, v_rg_ba, v_rg_wx, v_rg_bx, v_rg_lambda, v_ln1_g, v_ln1_b, v_xa_wq, v_xa_wk, v_xa_wv, v_xa_wo, v_ln2_g, v_ln2_b, v_mlp_w1, v_mlp_w2, v_ln3_g, v_ln3_b):
    args = dict(locals())
    weights = {n: args[n] for n in WEIGHT_ORDER}
    mom_m = {n: args["m_" + n] for n in WEIGHT_ORDER}
    mom_v = {n: args["v_" + n] for n in WEIGHT_ORDER}

    shards = []
    for l in range(DEPTH):
        for n, shp, ax in LAYER_GATHERED:
            w = weights[n][l]
            if w.shape[1] != shp[1]:
                w = jnp.pad(w, ((0, 0), (0, shp[1] - w.shape[1])))
            if n not in ("ssd_conv_w", "rg_conv_w"):
                w = w.astype(MXU_DTYPE)
            shards.append(w)
    handle = _gather_start(shards)

    def unpad(arr, padded, width):
        return jnp.concatenate([arr[:, padded * k:padded * k + width] for k in range(4)], axis=1)

    def fetch(l, grp, after):
        ts = [l * N_GATHERED + j for j in WAIT_GROUPS[grp]]
        _, landed = _gather_wait(handle, ts, after, name="weights_gather_wait_%d_%d" % (l, grp))
        out = {}
        for t, arr in zip(ts, landed):
            n = LAYER_GATHERED[t % N_GATHERED][0]
            if n == "w_in":
                arr = _pack_cols(unpad(arr, W_IN_PAD, W_IN_SHARD))
            elif n == "rg_conv_w":
                arr = unpad(arr, LANES, RG_CONV_SHARD)
            out[{"ssd_conv_w": "ssd_cw", "rg_conv_w": "rg_cw"}.get(n, n)] = arr
        return out

    rep = {n: weights[n] for n, _ in REPLICATED}

    loss11, dx, gsmall, gw_in0, gbuf, xy_handle, c_arr = _local_step(x[0], mem[0], loss_target[0], rep, fetch)
    grad_x = dx[None]
    loss = lax.psum(loss11[0, 0], ("x", "y", "c"))

    small_q = _split_shards(gsmall, SMALL_SHARDED)
    rep_q = jnp.pad(_pack_shards(gsmall, REPLICATED), (0, 4 * REP_QROWS * FLAT - REP_ELEMS)).reshape(4, -1)
    misc = jnp.concatenate(
        [jnp.pad(small_q, ((0, 0), (0, MISC_REP_ROW * FLAT - SMALL_ELEMS))), rep_q,
         jnp.zeros((4, (MISC_ROWS - MISC_REP_ROW - REP_QROWS) * FLAT), F32)], axis=1).reshape(4, MISC_ROWS, FLAT)
    gbuf = lax.dynamic_update_slice(gbuf, _w_in_block(gw_in0, misc), (0, _grad_row("w_in", 0), 0))
    got0 = _xy_exchange(_chip_sums(gbuf, c_arr, 1))
    got1 = _xy_wait(xy_handle, dx)
    fbuf = _sum4_into_quarter(got1[0], got1[1], c_arr, 0, None)
    reduced = _c_allgather_halves(_sum4_into_quarter(got0[0], got0[1], c_arr, 1, fbuf))
    misc_red = reduced[ROW_MISC:]
    rep_all = _xy_allgather(misc_red[MISC_REP_ROW:MISC_REP_ROW + REP_QROWS], name="small_grads_allgather")
    g_red = {**_unpack(misc_red[:MISC_REP_ROW].reshape(-1), SMALL_SHARDED),
             **_unpack(rep_all.reshape(-1), REPLICATED)}
    g_red["w_in"] = jnp.stack([
        reduced[_grad_row("w_in", l):_grad_row("w_in", l) + W_IN_PAD].reshape(D_MODEL, W_IN_PAD)[:, :W_IN_SHARD]
        for l in range(DEPTH)])

    res = {}
    for n in WEIGHT_ORDER:
        if n in ("mlp_w1", "mlp_w2", "w_out", "xa_wq", "xa_wk", "xa_wv", "xa_wo"):
            res[n] = _adamw(weights[n], mom_m[n], mom_v[n], reduced, g_rows=[_grad_row(n, l) for l in range(DEPTH)])
        else:
            res[n] = _adamw(weights[n], mom_m[n], mom_v[n], g_red[n])
    return (loss, grad_x, *[res[n][0] for n in WEIGHT_ORDER], *[res[n][1] for n in WEIGHT_ORDER],
            *[res[n][2] for n in WEIGHT_ORDER], *[res[n][3] for n in WEIGHT_ORDER])
```

```python
import functools
import math

import jax
import jax.numpy as jnp
from jax import lax
from jax.experimental import pallas as pl
from jax.experimental.pallas import tpu as pltpu

F32 = jnp.float32
MXU_DTYPE = jnp.bfloat16

D_MODEL = 1024
DEPTH = 2
MEM_LEN = 256
SSD_WIDTH = 512
SSD_HEADS = 8
SSD_STATE = 128
SSD_CHUNK = 128
SSD_XBC = 1024
S5_WIDTH = 256
S5_GROUPS = 16
S5_GROUP_CH = 16
S5_STATE = 64
S5_NSTATE = S5_GROUPS * S5_STATE
RG_WIDTH = 256
RG_BLOCKS = 4
RG_BLOCK_DIM = 64
RG_C = 8.0
XA_HEADS = 4
XA_HEAD_DIM = 256
D_FF = 4096
D_IN = 2312
ALPHA = (2.0 * DEPTH) ** 0.25
LN_EPS = 1e-5
ADAM_LR = 0.001
ADAM_B1 = 0.9
ADAM_B2 = 0.999
ADAM_EPS = 1e-08
ADAM_WD = 0.01
ADAM_STEP = 10

P_XBC, P_Z, P_U, P_XRG, P_GRG, P_DT = 0, 1024, 1536, 1792, 2048, 2304
D_PACK = 2432
O_Z, O_XBC, O_DT, O_U, O_XRG, O_GRG = 0, 512, 1536, 1544, 1800, 2056

LANES = 128
SUBLANES = 8
VMEM_LIMIT = 52 * 1024 * 1024
TM = 512
SSD_TM = 256
SCAN_TM = 512
FLAT = 1024

MESH = pl.DeviceIdType.MESH


def _cparams(sem):
    return pltpu.CompilerParams(dimension_semantics=sem, vmem_limit_bytes=VMEM_LIMIT)


def _dot(a, b):
    return jnp.dot(a.astype(MXU_DTYPE), b.astype(MXU_DTYPE), preferred_element_type=F32)


def _dot_nt(a, b):
    return lax.dot_general(a.astype(MXU_DTYPE), b.astype(MXU_DTYPE), (((1,), (1,)), ((), ())),
                           preferred_element_type=F32)


def _dot_tn(a, b):
    return lax.dot_general(a.astype(MXU_DTYPE), b.astype(MXU_DTYPE), (((0,), (0,)), ((), ())),
                           preferred_element_type=F32)


def _dot_f32(a, b):
    return jnp.dot(a, b, precision=lax.Precision.HIGHEST, preferred_element_type=F32)


def _dot_f32_tn(a, b):
    return lax.dot_general(a, b, (((0,), (0,)), ((), ())), precision=lax.Precision.HIGHEST,
                           preferred_element_type=F32)


def _sigmoid(x):
    return 1.0 / (1.0 + jnp.exp(-x))


def _softplus(x):
    return jnp.maximum(x, 0.0) + jnp.log(1.0 + jnp.exp(-jnp.abs(x)))


_GELU_K = math.sqrt(2.0 / math.pi)


def _gelu(x):
    return 0.5 * x * (1.0 + jnp.tanh(_GELU_K * (x + 0.044715 * x * x * x)))


def _gelu_grad(x):
    t = jnp.tanh(_GELU_K * (x + 0.044715 * x * x * x))
    return 0.5 * (1.0 + t) + 0.5 * x * (1.0 - t * t) * _GELU_K * (1.0 + 3.0 * 0.044715 * x * x)


def _expm1(x):
    small = x * (1.0 + x * (0.5 + x * (1.0 / 6.0 + x * (1.0 / 24.0))))
    return jnp.where(jnp.abs(x) < 0.05, small, jnp.exp(x) - 1.0)


def _sum0(x):
    return jnp.sum(x, axis=0, keepdims=True)


def _ln_fwd(r, g, b):
    mu = jnp.mean(r, axis=-1, keepdims=True)
    xc = r - mu
    var = jnp.mean(xc * xc, axis=-1, keepdims=True)
    rstd = lax.rsqrt(var + LN_EPS)
    xhat = xc * rstd
    return xhat * g + b, xhat, rstd


def _ln_bwd(dout, xhat, rstd, g):
    dxh = dout * g
    m1 = jnp.mean(dxh, axis=-1, keepdims=True)
    m2 = jnp.mean(dxh * xhat, axis=-1, keepdims=True)
    return rstd * (dxh - m1 - xhat * m2)


def _rows(tm, n, col=0):
    return pl.BlockSpec((tm, n), lambda i: (i, col))


def _const(shape):
    nd = len(shape)
    return pl.BlockSpec(shape, lambda i: (0,) * nd)


def _mm(a, w, *, name):
    t, k = a.shape
    n = w.shape[1]
    tm = min(TM, t)

    def body(a_ref, w_ref, o_ref):
        o_ref[...] = _dot(a_ref[...], w_ref[...])

    return pl.pallas_call(
        body, name=name, grid=(t // tm,), in_specs=[_rows(tm, k), _const(w.shape)], out_specs=_rows(tm, n),
        out_shape=jax.ShapeDtypeStruct((t, n), F32), compiler_params=_cparams(("arbitrary",)),
    )(a, w)


DPROJ_PIECES = ((P_XBC, 1024), (P_Z, 512), (P_U, 256), (P_XRG, 256), (P_GRG, 256), (P_DT, LANES))


def _in_proj_bwd(pieces, w, dres):
    t = dres.shape[0]
    npc = len(pieces)

    def body(*refs):
        w_ref, r_ref, o_ref = refs[npc:]
        acc = r_ref[...]
        for p_ref, (off, k) in zip(refs[:npc], DPROJ_PIECES):
            acc = acc + _dot_nt(p_ref[...], w_ref[:, off:off + k])
        o_ref[...] = acc

    return pl.pallas_call(
        body, name="in_proj_bwd", grid=(t // TM,),
        in_specs=[_rows(TM, k) for _, k in DPROJ_PIECES] + [_const(w.shape), _rows(TM, D_MODEL)],
        out_specs=_rows(TM, D_MODEL), out_shape=jax.ShapeDtypeStruct((t, D_MODEL), F32),
        compiler_params=_cparams(("arbitrary",)),
    )(*pieces, w, dres)


def _wgrad_in(h0, pieces):
    t = h0.shape[0]
    npc = len(pieces)

    def body(*refs):
        h_ref, o_ref = refs[npc], refs[npc + 1]
        @pl.when(pl.program_id(0) == 0)
        def _():
            o_ref[...] = jnp.zeros_like(o_ref)

        hb = h_ref[...].astype(MXU_DTYPE)
        for p_ref, (off, k) in zip(refs[:npc], DPROJ_PIECES):
            o_ref[:, off:off + k] += _dot_tn(hb, p_ref[...])

    return pl.pallas_call(
        body, name="wgrad_in", grid=(t // TM,),
        in_specs=[_rows(TM, k) for _, k in DPROJ_PIECES] + [_rows(TM, D_MODEL)],
        out_specs=_const((D_MODEL, D_PACK)), out_shape=jax.ShapeDtypeStruct((D_MODEL, D_PACK), F32),
        compiler_params=_cparams(("arbitrary",)),
    )(*pieces, h0)


G_ROWS = 8192
G_PARTS = ((0, 4096), (4096, 2048), (6144, 2048))
W_IN_SHARD = 578
W_IN_PAD = 640
MISC_ROWS = 128
MISC_REP_ROW = 40
ROW_MISC = G_ROWS - MISC_ROWS
W_IN_BLOCK_ROWS = W_IN_PAD + MISC_ROWS


def _grad_row(name, l):
    base = 0 if l == 1 else 4096
    mid = base + 2048 if l == 1 else 6144
    return {"mlp_w1": base, "mlp_w2": base + 1024, "w_out": mid, "xa_wq": mid + 256, "xa_wk": mid + 512,
            "xa_wv": mid + 768, "xa_wo": mid + 1024, "w_in": mid + 1280}[name]


def _wgrad_flat(a, g, buf, *, mode, row_off, name):
    pieces = list(a) if isinstance(a, (list, tuple)) else [a]
    t = g.shape[0]
    tt = min(1024, t)
    ns = t // tt
    blk = D_MODEL

    def accumulate(o_ref, parts, s):
        @pl.when(s == 0)
        def _():
            o_ref[...] = jnp.zeros_like(o_ref)

        for q, v in parts:
            o_ref[q] += v

    if mode == "rows4":
        grid = (ns,)
        in_specs = [pl.BlockSpec((tt, p.shape[1]), lambda s: (s, 0)) for p in pieces]
        in_specs.append(pl.BlockSpec((tt, blk), lambda s: (s, 0)))
        out_spec = pl.BlockSpec((4, 256, FLAT), lambda s: (0, row_off // 256, 0))
        sem = ("arbitrary",)
        npc = len(pieces)

        def body(*refs):
            g_v = refs[npc][...]
            parts, q0 = [], 0
            for p_ref in refs[:npc]:
                full = _dot_tn(p_ref[...], g_v)
                nq = full.shape[0] // 256
                parts += [(q0 + q, full[q * 256:(q + 1) * 256]) for q in range(nq)]
                q0 += nq
            accumulate(refs[-1], parts, pl.program_id(0))
    else:
        grid = (2, ns)
        if mode == "rowblk":
            in_specs = [pl.BlockSpec((tt, 2 * blk), lambda q, s: (s, q)), pl.BlockSpec((tt, blk), lambda q, s: (s, 0))]
        else:
            in_specs = [pl.BlockSpec((tt, blk), lambda q, s: (s, 0)), pl.BlockSpec((tt, 2 * blk), lambda q, s: (s, q))]
        out_spec = pl.BlockSpec((2, blk, FLAT), lambda q, s: (q, row_off // blk, 0))
        sem = ("arbitrary", "arbitrary")

        def body(a_ref, g_ref, *rest):
            full = _dot_tn(a_ref[...], g_ref[...])
            if mode == "rowblk":
                parts = [(0, full[:blk]), (1, full[blk:])]
            else:
                parts = [(0, full[:, :blk]), (1, full[:, blk:])]
            accumulate(rest[-1], parts, pl.program_id(1))

    args = pieces + [g]
    aliases = {}
    if buf is not None:
        in_specs.append(pl.BlockSpec(memory_space=pl.ANY))
        args.append(buf)
        aliases = {len(args) - 1: 0}
    return pl.pallas_call(
        body, name=name, grid=grid, in_specs=in_specs, out_specs=out_spec,
        out_shape=jax.ShapeDtypeStruct((4, G_ROWS, FLAT), F32), input_output_aliases=aliases,
        compiler_params=_cparams(sem),
    )(*args)


def _outproj_ln_fwd(ys, h, w, g, b):
    t = h.shape[0]
    npc = len(ys)

    def body(*refs):
        h_ref, w_ref, g_ref, b_ref, hn_ref, xh_ref, rs_ref = refs[npc:]
        r = ALPHA * h_ref[...]
        off = 0
        for y_ref in refs[:npc]:
            k = y_ref.shape[1]
            r = r + _dot(y_ref[...], w_ref[off:off + k, :])
            off += k
        out, xhat, rstd = _ln_fwd(r, g_ref[...], b_ref[...])
        hn_ref[...] = out
        xh_ref[...] = xhat
        rs_ref[...] = rstd

    return pl.pallas_call(
        body, name="outproj_ln_fwd", grid=(t // TM,),
        in_specs=[_rows(TM, y.shape[1]) for y in ys] + [_rows(TM, D_MODEL), _const((D_MODEL, D_MODEL)),
                                                        _const((1, D_MODEL)), _const((1, D_MODEL))],
        out_specs=[_rows(TM, D_MODEL), _rows(TM, D_MODEL), _rows(TM, 1)],
        out_shape=[jax.ShapeDtypeStruct((t, D_MODEL), F32), jax.ShapeDtypeStruct((t, D_MODEL), F32),
                   jax.ShapeDtypeStruct((t, 1), F32)],
        compiler_params=_cparams(("arbitrary",)),
    )(*ys, h, w, g, b)


def _attn_probs(q, kb, hh):
    sl = slice(hh * XA_HEAD_DIM, (hh + 1) * XA_HEAD_DIM)
    s = _dot_nt(q[:, sl], kb[:, sl]) * (1.0 / math.sqrt(XA_HEAD_DIM))
    m = jnp.max(s, axis=-1, keepdims=True)
    e = jnp.exp(s - m)
    return e / jnp.sum(e, axis=-1, keepdims=True)


def _attn_ln_fwd(h1, wq, wo, kb, vb, g, b):
    t = h1.shape[0]

    def body(h_ref, wq_ref, wo_ref, k_ref, v_ref, g_ref, b_ref, hn_ref, xh_ref, rs_ref, o_ref):
        h = h_ref[...]
        q = _dot(h, wq_ref[...])
        kb_ = k_ref[...]
        vb_ = v_ref[...]
        for hh in range(XA_HEADS):
            sl = slice(hh * XA_HEAD_DIM, (hh + 1) * XA_HEAD_DIM)
            p = _attn_probs(q, kb_, hh)
            o_ref[:, sl] = _dot(p, vb_[:, sl]).astype(o_ref.dtype)
        r = ALPHA * h + _dot(o_ref[...], wo_ref[...])
        out, xhat, rstd = _ln_fwd(r, g_ref[...], b_ref[...])
        hn_ref[...] = out
        xh_ref[...] = xhat
        rs_ref[...] = rstd

    return pl.pallas_call(
        body, name="attn_ln_fwd", grid=(t // TM,),
        in_specs=[_rows(TM, D_MODEL), _const((D_MODEL, D_MODEL)), _const((D_MODEL, D_MODEL)),
                  _const((MEM_LEN, D_MODEL)), _const((MEM_LEN, D_MODEL)), _const((1, D_MODEL)), _const((1, D_MODEL))],
        out_specs=[_rows(TM, D_MODEL), _rows(TM, D_MODEL), _rows(TM, 1), _rows(TM, D_MODEL)],
        out_shape=[jax.ShapeDtypeStruct((t, D_MODEL), F32), jax.ShapeDtypeStruct((t, D_MODEL), F32),
                   jax.ShapeDtypeStruct((t, 1), F32), jax.ShapeDtypeStruct((t, D_MODEL), MXU_DTYPE)],
        compiler_params=_cparams(("arbitrary",)),
    )(h1, wq, wo, kb, vb, g, b)


def _attn_ln_bwd(dh2, xhat, rstd, g, h1, wq, wo, kb, vb):
    t = h1.shape[0]

    def body(dh_ref, xh_ref, rs_ref, g_ref, h_ref, wq_ref, wo_ref, k_ref, v_ref,
             dr_ref, dq_ref, dh1_ref, dk_ref, dv_ref, dg_ref, db_ref):
        i = pl.program_id(0)

        @pl.when(i == 0)
        def _():
            dk_ref[...] = jnp.zeros_like(dk_ref)
            dv_ref[...] = jnp.zeros_like(dv_ref)
            dg_ref[...] = jnp.zeros_like(dg_ref)
            db_ref[...] = jnp.zeros_like(db_ref)

        dout = dh_ref[...]
        xh = xh_ref[...]
        dg_ref[...] += _sum0(dout * xh)
        db_ref[...] += _sum0(dout)
        dr = _ln_bwd(dout, xh, rs_ref[...], g_ref[...])
        dr_ref[...] = dr.astype(dr_ref.dtype)
        do = _dot_nt(dr, wo_ref[...])
        h = h_ref[...]
        q = _dot(h, wq_ref[...])
        kb_ = k_ref[...]
        vb_ = v_ref[...]
        scale = 1.0 / math.sqrt(XA_HEAD_DIM)
        for hh in range(XA_HEADS):
            sl = slice(hh * XA_HEAD_DIM, (hh + 1) * XA_HEAD_DIM)
            p = _attn_probs(q, kb_, hh)
            do_h = do[:, sl]
            dp = _dot_nt(do_h, vb_[:, sl])
            ds = p * (dp - jnp.sum(dp * p, axis=-1, keepdims=True)) * scale
            dq_ref[:, sl] = _dot(ds, kb_[:, sl]).astype(dq_ref.dtype)
            dk_ref[:, sl] += _dot_tn(ds, q[:, sl])
            dv_ref[:, sl] += _dot_tn(p, do_h)
        dh1_ref[...] = ALPHA * dr + _dot_nt(dq_ref[...], wq_ref[...])

    return pl.pallas_call(
        body, name="attn_ln_bwd", grid=(t // TM,),
        in_specs=[_rows(TM, D_MODEL), _rows(TM, D_MODEL), _rows(TM, 1), _const((1, D_MODEL)), _rows(TM, D_MODEL),
                  _const((D_MODEL, D_MODEL)), _const((D_MODEL, D_MODEL)), _const((MEM_LEN, D_MODEL)),
                  _const((MEM_LEN, D_MODEL))],
        out_specs=[_rows(TM, D_MODEL), _rows(TM, D_MODEL), _rows(TM, D_MODEL), _const((MEM_LEN, D_MODEL)),
                   _const((MEM_LEN, D_MODEL)), _const((1, D_MODEL)), _const((1, D_MODEL))],
        out_shape=[jax.ShapeDtypeStruct((t, D_MODEL), MXU_DTYPE), jax.ShapeDtypeStruct((t, D_MODEL), MXU_DTYPE),
                   jax.ShapeDtypeStruct((t, D_MODEL), F32), jax.ShapeDtypeStruct((MEM_LEN, D_MODEL), F32),
                   jax.ShapeDtypeStruct((MEM_LEN, D_MODEL), F32), jax.ShapeDtypeStruct((1, D_MODEL), F32),
                   jax.ShapeDtypeStruct((1, D_MODEL), F32)],
        compiler_params=_cparams(("arbitrary",)),
    )(dh2, xhat, rstd, g, h1, wq, wo, kb, vb)


FF_CHUNK = 1024
N_FF = D_FF // FF_CHUNK


def _load_resident(pairs, sems):
    copies = [pltpu.make_async_copy(src, dst, sems.at[k]) for k, (src, dst) in enumerate(pairs)]
    for cp in copies:
        cp.start()
    for cp in copies:
        cp.wait()


def _mlp_ln_fwd(h2, w1, w2, g, b):
    t = h2.shape[0]

    def body(h_ref, w1_hbm, w2_hbm, g_ref, b_ref, hn_ref, xh_ref, rs_ref, hd_ref, w1_v, w2_v, acc_ref, sems):
        @pl.when(pl.program_id(0) == 0)
        def _():
            _load_resident([(w1_hbm, w1_v), (w2_hbm, w2_v)], sems)

        h = h_ref[...]
        hb = h.astype(MXU_DTYPE)
        acc_ref[...] = ALPHA * h
        for j in range(N_FF):
            sl = slice(j * FF_CHUNK, (j + 1) * FF_CHUNK)
            u = _dot(hb, w1_v[:, sl])
            hd = jnp.square(jnp.maximum(u, 0.0)).astype(MXU_DTYPE)
            hd_ref[:, sl] = hd
            acc_ref[...] += _dot(hd, w2_v[sl, :])
        out, xhat, rstd = _ln_fwd(acc_ref[...], g_ref[...], b_ref[...])
        hn_ref[...] = out
        xh_ref[...] = xhat
        rs_ref[...] = rstd

    return pl.pallas_call(
        body, name="mlp_ln_fwd", grid=(t // TM,),
        in_specs=[_rows(TM, D_MODEL), _hbm(), _hbm(), _const((1, D_MODEL)), _const((1, D_MODEL))],
        out_specs=[_rows(TM, D_MODEL), _rows(TM, D_MODEL), _rows(TM, 1), _rows(TM, D_FF)],
        out_shape=[jax.ShapeDtypeStruct((t, D_MODEL), F32), jax.ShapeDtypeStruct((t, D_MODEL), F32),
                   jax.ShapeDtypeStruct((t, 1), F32), jax.ShapeDtypeStruct((t, D_FF), MXU_DTYPE)],
        scratch_shapes=[pltpu.VMEM((D_MODEL, D_FF), MXU_DTYPE), pltpu.VMEM((D_FF, D_MODEL), MXU_DTYPE),
                        pltpu.VMEM((TM, D_MODEL), F32), pltpu.SemaphoreType.DMA((2,))],
        compiler_params=_cparams(("arbitrary",)),
    )(h2, w1, w2, g, b)


def _mlp_ln_bwd(dh3, xhat, rstd, g, hdn, w1, w2):
    t = dh3.shape[0]

    def body(dh_ref, xh_ref, rs_ref, g_ref, hd_ref, w1_hbm, w2_hbm,
             dr_ref, du_ref, dh2_ref, dg_ref, db_ref, w1_v, w2_v, acc_ref, sems):
        @pl.when(pl.program_id(0) == 0)
        def _():
            _load_resident([(w1_hbm, w1_v), (w2_hbm, w2_v)], sems)
            dg_ref[...] = jnp.zeros_like(dg_ref)
            db_ref[...] = jnp.zeros_like(db_ref)

        dout = dh_ref[...]
        xh = xh_ref[...]
        dg_ref[...] += _sum0(dout * xh)
        db_ref[...] += _sum0(dout)
        dr = _ln_bwd(dout, xh, rs_ref[...], g_ref[...])
        drb = dr.astype(MXU_DTYPE)
        dr_ref[...] = drb
        acc_ref[...] = ALPHA * dr
        for j in range(N_FF):
            sl = slice(j * FF_CHUNK, (j + 1) * FF_CHUNK)
            dhd = _dot_nt(drb, w2_v[sl, :])
            du = (dhd * (2.0 * jnp.sqrt(hd_ref[:, sl].astype(F32)))).astype(MXU_DTYPE)
            du_ref[:, sl] = du
            acc_ref[...] += _dot_nt(du, w1_v[:, sl])
        dh2_ref[...] = acc_ref[...]

    tm = TM // 2
    return pl.pallas_call(
        body, name="mlp_ln_bwd", grid=(t // tm,),
        in_specs=[_rows(tm, D_MODEL), _rows(tm, D_MODEL), _rows(tm, 1), _const((1, D_MODEL)), _rows(tm, D_FF),
                  _hbm(), _hbm()],
        out_specs=[_rows(tm, D_MODEL), _rows(tm, D_FF), _rows(tm, D_MODEL), _const((1, D_MODEL)),
                   _const((1, D_MODEL))],
        out_shape=[jax.ShapeDtypeStruct((t, D_MODEL), MXU_DTYPE), jax.ShapeDtypeStruct((t, D_FF), MXU_DTYPE),
                   jax.ShapeDtypeStruct((t, D_MODEL), F32), jax.ShapeDtypeStruct((1, D_MODEL), F32),
                   jax.ShapeDtypeStruct((1, D_MODEL), F32)],
        scratch_shapes=[pltpu.VMEM((D_MODEL, D_FF), MXU_DTYPE), pltpu.VMEM((D_FF, D_MODEL), MXU_DTYPE),
                        pltpu.VMEM((tm, D_MODEL), F32), pltpu.SemaphoreType.DMA((2,))],
        compiler_params=_cparams(("arbitrary",)),
    )(dh3, xhat, rstd, g, hdn, w1, w2)


def _outproj_ln_bwd(dh1, xhat, rstd, g, w):
    t = dh1.shape[0]

    def body(dh_ref, xh_ref, rs_ref, g_ref, w_ref, dr_ref, res_ref, dy_ref, dg_ref, db_ref):
        i = pl.program_id(0)

        @pl.when(i == 0)
        def _():
            dg_ref[...] = jnp.zeros_like(dg_ref)
            db_ref[...] = jnp.zeros_like(db_ref)

        dout = dh_ref[...]
        xh = xh_ref[...]
        dg_ref[...] += _sum0(dout * xh)
        db_ref[...] += _sum0(dout)
        dr = _ln_bwd(dout, xh, rs_ref[...], g_ref[...])
        dr_ref[...] = dr.astype(dr_ref.dtype)
        res_ref[...] = ALPHA * dr
        dy_ref[...] = _dot_nt(dr, w_ref[...])

    return pl.pallas_call(
        body, name="outproj_ln_bwd", grid=(t // TM,),
        in_specs=[_rows(TM, D_MODEL), _rows(TM, D_MODEL), _rows(TM, 1), _const((1, D_MODEL)),
                  _const((D_MODEL, D_MODEL))],
        out_specs=[_rows(TM, D_MODEL), _rows(TM, D_MODEL), _rows(TM, D_MODEL), _const((1, D_MODEL)),
                   _const((1, D_MODEL))],
        out_shape=[jax.ShapeDtypeStruct((t, D_MODEL), MXU_DTYPE), jax.ShapeDtypeStruct((t, D_MODEL), F32),
                   jax.ShapeDtypeStruct((t, D_MODEL), F32), jax.ShapeDtypeStruct((1, D_MODEL), F32),
                   jax.ShapeDtypeStruct((1, D_MODEL), F32)],
        compiler_params=_cparams(("arbitrary",)),
    )(dh1, xhat, rstd, g, w)


def _loss_fwd_bwd(h, target):
    t = h.shape[0]

    def body(h_ref, t_ref, l_ref, dh_ref):
        i = pl.program_id(0)

        @pl.when(i == 0)
        def _():
            l_ref[...] = jnp.zeros_like(l_ref)

        e = h_ref[...] - t_ref[...]
        dh_ref[...] = e * (1.0 / D_MODEL)
        per_tok = jnp.mean(e * e, axis=-1, keepdims=True)
        l_ref[...] += 0.5 * jnp.sum(per_tok, axis=0, keepdims=True)

    return pl.pallas_call(
        body, name="loss_fwd_bwd", grid=(t // TM,),
        in_specs=[_rows(TM, D_MODEL), _rows(TM, D_MODEL)],
        out_specs=[_const((1, 1)), _rows(TM, D_MODEL)],
        out_shape=[jax.ShapeDtypeStruct((1, 1), F32), jax.ShapeDtypeStruct((t, D_MODEL), F32)],
        compiler_params=_cparams(("arbitrary",)),
    )(h, target)


def _pick_col(x, idx):
    lane = lax.broadcasted_iota(jnp.int32, x.shape, 1)
    return jnp.sum(jnp.where(lane == idx, x, 0.0), axis=1, keepdims=True)


def _pick_row(x, idx):
    sub = lax.broadcasted_iota(jnp.int32, x.shape, 0)
    return jnp.sum(jnp.where(sub == idx, x, 0.0), axis=0, keepdims=True)


def _conv_taps(pad_ref, w, tm, base):
    acc = w[0:1, :] * pad_ref[base:base + tm, :]
    for k in range(1, 4):
        acc = acc + w[k:k + 1, :] * pad_ref[base + k:base + k + tm, :]
    return acc


def _ssd_chunk_common(adt_c, tri):
    cs = _dot_f32(tri, adt_c)
    return cs, cs.T, jnp.exp(cs)


def _ssd_head_terms(cs, cst, ecs, dt_c, h, tri):
    cs_col = _pick_col(cs, h)
    cs_row = _pick_row(cst, h)
    dt_col = _pick_col(dt_c, h)
    cs_last = cs_col[SSD_CHUNK - 1:SSD_CHUNK, :]
    lmat = jnp.exp(jnp.where(tri > 0.0, cs_col - cs_row, -1e30))
    ecs_col = _pick_col(ecs, h)
    decay_col = jnp.exp(cs_last - cs_col)
    return cs_col, dt_col, cs_last, lmat, ecs_col, decay_col


def _ssd_fwd(proj, cw, cb, dtb, a_neg, d_lanes, nw):
    t = proj.shape[0]
    tm = SSD_TM
    nt = t // tm
    ncq = tm // SSD_CHUNK
    hb = tm // SUBLANES

    def body(xbc_ref, halo_ref, z_ref, dt_ref, cw_ref, cb_ref, dtb_ref, a_ref, d_ref, nw_ref,
             y_ref, yy_ref, st_ref, xpad, xact, state):
        i = pl.program_id(0)

        @pl.when(i == 0)
        def _():
            state[...] = jnp.zeros_like(state)

        xpad[0:SUBLANES, :] = jnp.where(i > 0, halo_ref[...], 0.0)
        xpad[SUBLANES:SUBLANES + tm, :] = xbc_ref[...]
        acc = cb_ref[...] + _conv_taps(xpad, cw_ref[...], tm, SUBLANES - 3)
        xact[...] = acc * _sigmoid(acc)
        dt = _softplus(dt_ref[...] + dtb_ref[...])
        adt = dt * a_ref[...]
        r_i = lax.broadcasted_iota(jnp.int32, (SSD_CHUNK, SSD_CHUNK), 0)
        c_i = lax.broadcasted_iota(jnp.int32, (SSD_CHUNK, SSD_CHUNK), 1)
        tri = (r_i >= c_i).astype(F32)
        lane1 = lax.broadcasted_iota(jnp.int32, (1, LANES), 1)
        for c in range(ncq):
            sl = slice(c * SSD_CHUNK, (c + 1) * SSD_CHUNK)
            dt_c = dt[sl]
            cs, cst, ecs = _ssd_chunk_common(adt[sl], tri)
            for g in range(2):
                bg = xact[sl, 512 + g * 128:512 + (g + 1) * 128]
                cg = xact[sl, 768 + g * 128:768 + (g + 1) * 128]
                cbm = _dot_nt(cg, bg)
                for pr in range(2):
                    pi = g * 2 + pr
                    psl = slice(pi * 128, (pi + 1) * 128)
                    xp = xact[sl, psl]
                    prev = state[pi]
                    st_ref[c, pi] = prev
                    yp = xp * d_ref[:, psl]
                    new_s = jnp.zeros((SSD_STATE, LANES), F32)
                    dec_lane = jnp.zeros((1, LANES), F32)
                    for hh in range(2):
                        h = g * 4 + pr * 2 + hh
                        lm = (lane1 >= 64) if hh else (lane1 < 64)
                        _, dt_col, cs_last, lmat, ecs_col, decay_col = _ssd_head_terms(cs, cst, ecs, dt_c, h, tri)
                        xdt = jnp.where(lm, xp, 0.0) * dt_col
                        yp = yp + _dot(cbm * lmat, xdt)
                        yp = yp + _dot(cg * ecs_col, jnp.where(lm, prev, 0.0))
                        new_s = new_s + _dot_tn(bg * decay_col, xdt)
                        dec_lane = dec_lane + jnp.where(lm, jnp.exp(cs_last), 0.0)
                    state[pi] = prev * dec_lane + new_s
                    yy_ref[sl, psl] = yp
        yy = yy_ref[...]
        z = z_ref[...]
        yg = yy * (z * _sigmoid(z))
        ms = jnp.mean(yg * yg, axis=-1, keepdims=True)
        y_ref[...] = yg * lax.rsqrt(ms + LN_EPS) * nw_ref[...]

    halo_map = lambda i: (jnp.maximum(i * hb - 1, 0), 0)
    return pl.pallas_call(
        body, name="ssd_fwd", grid=(nt,),
        in_specs=[pl.BlockSpec((tm, SSD_XBC), lambda i: (i, 0)), pl.BlockSpec((SUBLANES, SSD_XBC), halo_map),
                  pl.BlockSpec((tm, SSD_WIDTH), lambda i: (i, P_Z // SSD_WIDTH)),
                  pl.BlockSpec((tm, LANES), lambda i: (i, P_DT // LANES)),
                  _const((4, SSD_XBC)), _const((1, SSD_XBC)), _const((1, LANES)), _const((1, LANES)),
                  _const((1, SSD_WIDTH)), _const((1, SSD_WIDTH))],
        out_specs=[_rows(tm, SSD_WIDTH), _rows(tm, SSD_WIDTH),
                   pl.BlockSpec((ncq, 4, SSD_STATE, LANES), lambda i: (i, 0, 0, 0))],
        out_shape=[jax.ShapeDtypeStruct((t, SSD_WIDTH), F32), jax.ShapeDtypeStruct((t, SSD_WIDTH), F32),
                   jax.ShapeDtypeStruct((t // SSD_CHUNK, 4, SSD_STATE, LANES), F32)],
        scratch_shapes=[pltpu.VMEM((tm + SUBLANES, SSD_XBC), F32), pltpu.VMEM((tm, SSD_XBC), F32),
                        pltpu.VMEM((4, SSD_STATE, LANES), F32)],
        compiler_params=_cparams(("arbitrary",)),
    )(proj, proj, proj, proj, cw, cb, dtb, a_neg, d_lanes, nw)


def _ssd_bwd(dycat, proj, yy, states, cw, cb, dtb, a_neg, d_lanes, nw):
    t = proj.shape[0]
    tm = SSD_TM
    nt = t // tm
    ncq = tm // SSD_CHUNK
    hb = tm // SUBLANES

    def body(dy_ref, xbc_ref, halo_ref, z_ref, dt_ref, yy_ref, st_ref, cw_ref, cb_ref, dtb_ref, a_ref, d_ref, nw_ref,
             dxbc_ref, dz_ref, ddt_ref, dcw_ref, dcb_ref, ddtb_ref, da_ref, dd_ref, dnw_ref,
             xpad, xact, dxact, dpad, dstate, dnext):
        i = pl.program_id(0)

        @pl.when(i == 0)
        def _():
            for r in (dcw_ref, dcb_ref, ddtb_ref, da_ref, dd_ref, dnw_ref, dstate, dnext):
                r[...] = jnp.zeros_like(r)

        xpad[0:SUBLANES, :] = jnp.where(i < nt - 1, halo_ref[...], 0.0)
        xpad[SUBLANES:SUBLANES + tm, :] = xbc_ref[...]
        cw_v = cw_ref[...]
        acc = cb_ref[...] + _conv_taps(xpad, cw_v, tm, SUBLANES - 3)
        sig = _sigmoid(acc)
        xact[...] = acc * sig
        dt_raw = dt_ref[...] + dtb_ref[...]
        dt = _softplus(dt_raw)
        a_v = a_ref[...]
        adt = dt * a_v
        yy = yy_ref[...]
        z = z_ref[...]
        sz = _sigmoid(z)
        siluz = z * sz
        yg = yy * siluz
        ms = jnp.mean(yg * yg, axis=-1, keepdims=True)
        rinv = lax.rsqrt(ms + LN_EPS)
        dout = dy_ref[...]
        dnw_ref[...] += _sum0(dout * yg * rinv)
        dyn = dout * nw_ref[...]
        dyg = rinv * dyn - yg * (rinv * rinv * rinv) * jnp.mean(dyn * yg, axis=-1, keepdims=True)
        dyy = dyg * siluz
        dz_ref[...] = dyg * yy * (sz * (1.0 + z * (1.0 - sz)))
        dd_ref[...] += _sum0(dyy * xact[:, 0:SSD_WIDTH])

        r_i = lax.broadcasted_iota(jnp.int32, (SSD_CHUNK, SSD_CHUNK), 0)
        c_i = lax.broadcasted_iota(jnp.int32, (SSD_CHUNK, SSD_CHUNK), 1)
        tri = (r_i >= c_i).astype(F32)
        lane1 = lax.broadcasted_iota(jnp.int32, (1, LANES), 1)
        for c in reversed(range(ncq)):
            sl = slice(c * SSD_CHUNK, (c + 1) * SSD_CHUNK)
            dt_c = dt[sl]
            cs, cst, ecs = _ssd_chunk_common(adt[sl], tri)
            cacc = jnp.zeros((SSD_CHUNK, LANES), F32)
            racc = jnp.zeros((SSD_CHUNK, LANES), F32)
            ddtx = jnp.zeros((SSD_CHUNK, LANES), F32)
            for g in range(2):
                bg = xact[sl, 512 + g * 128:512 + (g + 1) * 128]
                cg = xact[sl, 768 + g * 128:768 + (g + 1) * 128]
                cbm = _dot_nt(cg, bg)
                dcb_m = jnp.zeros((SSD_CHUNK, SSD_CHUNK), F32)
                dbg = jnp.zeros((SSD_CHUNK, SSD_STATE), F32)
                dcg = jnp.zeros((SSD_CHUNK, SSD_STATE), F32)
                for pr in range(2):
                    pi = g * 2 + pr
                    psl = slice(pi * 128, (pi + 1) * 128)
                    xp = xact[sl, psl]
                    dyp = dyy[sl, psl]
                    prev = st_ref[c, pi]
                    ds_all = dstate[pi]
                    dxdt_p = jnp.zeros((SSD_CHUNK, LANES), F32)
                    dprev_new = jnp.zeros((SSD_STATE, LANES), F32)
                    dec_lane = jnp.zeros((1, LANES), F32)
                    dt_lanes = jnp.zeros((SSD_CHUNK, LANES), F32)
                    for hh in range(2):
                        h = g * 4 + pr * 2 + hh
                        lm = (lane1 >= 64) if hh else (lane1 < 64)
                        oh_l = (c_i == h).astype(F32)
                        oh_s = (r_i == h).astype(F32)
                        _, dt_col, cs_last, lmat, ecs_col, decay_col = _ssd_head_terms(cs, cst, ecs, dt_c, h, tri)
                        gm = cbm * lmat
                        xm = jnp.where(lm, xp, 0.0)
                        xdt = xm * dt_col
                        dym = jnp.where(lm, dyp, 0.0)
                        prevm = jnp.where(lm, prev, 0.0)
                        dsm = jnp.where(lm, ds_all, 0.0)
                        bdec = bg * decay_col
                        dxdt = _dot_tn(gm, dym) + _dot(bdec, dsm)
                        dxdt_p = dxdt_p + dxdt
                        ddtx = ddtx + oh_l * jnp.sum(dxdt * xm, axis=1, keepdims=True)
                        dt_lanes = dt_lanes + jnp.where(lm, dt_col, 0.0)
                        dgm = _dot_nt(dym, xdt)
                        dcb_m = dcb_m + dgm * lmat
                        w = dgm * gm
                        cacc = cacc + oh_l * jnp.sum(w, axis=1, keepdims=True)
                        racc = racc - oh_s * jnp.sum(w, axis=0, keepdims=True)
                        dce = _dot_nt(dym, prevm)
                        dcg = dcg + dce * ecs_col
                        cacc = cacc + oh_l * (jnp.sum(dce * cg, axis=1, keepdims=True) * ecs_col)
                        dprev_new = dprev_new + _dot_tn(cg * ecs_col, dym)
                        dbdec = _dot_nt(xdt, dsm)
                        dbg = dbg + dbdec * decay_col
                        dd = jnp.sum(dbdec * bg, axis=1, keepdims=True) * decay_col
                        cacc = cacc - oh_l * dd
                        cd = jnp.exp(cs_last)
                        dlast = jnp.sum(dd, axis=0, keepdims=True) + jnp.sum(
                            jnp.sum(dsm * prevm, axis=1, keepdims=True), axis=0, keepdims=True) * cd
                        cacc = cacc + jnp.where((r_i == SSD_CHUNK - 1) & (c_i == h), dlast, 0.0)
                        dec_lane = dec_lane + jnp.where(lm, cd, 0.0)
                    dstate[pi] = ds_all * dec_lane + dprev_new
                    dxact[sl, psl] = dxdt_p * dt_lanes + dyp * d_ref[:, psl]
                dcg = dcg + _dot(dcb_m, bg)
                dbg = dbg + _dot_tn(dcb_m, cg)
                dxact[sl, 512 + g * 128:512 + (g + 1) * 128] = dbg
                dxact[sl, 768 + g * 128:768 + (g + 1) * 128] = dcg
            dcs = cacc + racc.T
            dadt = _dot_f32((r_i <= c_i).astype(F32), dcs)
            ddt = dadt * a_v + ddtx
            da_ref[...] += _sum0(dadt * dt_c)
            ddt_raw = ddt * _sigmoid(dt_raw[sl])
            ddt_ref[sl, :] = ddt_raw
            ddtb_ref[...] += _sum0(ddt_raw)
        dacc = dxact[...] * (sig * (1.0 + acc * (1.0 - sig)))
        dcb_ref[...] += _sum0(dacc)
        for k in range(4):
            dcw_ref[k:k + 1, :] += _sum0(dacc * xpad[SUBLANES - 3 + k:SUBLANES - 3 + k + tm, :])
        dpad[0:tm, :] = dacc
        dpad[tm:tm + SUBLANES, :] = dnext[...]
        dx = cw_v[0:1, :] * dpad[3:3 + tm, :]
        for k in range(1, 4):
            dx = dx + cw_v[k:k + 1, :] * dpad[3 - k:3 - k + tm, :]
        dxbc_ref[...] = dx
        dnext[...] = dacc[0:SUBLANES, :]

    rev = lambda i: nt - 1 - i
    halo_map = lambda i: (jnp.maximum(rev(i) * hb - 1, 0), 0)
    rrow = lambda n, col=0: pl.BlockSpec((tm, n), lambda i: (rev(i), col))
    return pl.pallas_call(
        body, name="ssd_bwd", grid=(nt,),
        in_specs=[rrow(SSD_WIDTH), rrow(SSD_XBC), pl.BlockSpec((SUBLANES, SSD_XBC), halo_map),
                  rrow(SSD_WIDTH, P_Z // SSD_WIDTH), rrow(LANES, P_DT // LANES), rrow(SSD_WIDTH),
                  pl.BlockSpec((ncq, 4, SSD_STATE, LANES), lambda i: (rev(i), 0, 0, 0)),
                  _const((4, SSD_XBC)), _const((1, SSD_XBC)), _const((1, LANES)), _const((1, LANES)),
                  _const((1, SSD_WIDTH)), _const((1, SSD_WIDTH))],
        out_specs=[rrow(SSD_XBC), rrow(SSD_WIDTH), rrow(LANES), _const((SUBLANES, SSD_XBC)), _const((1, SSD_XBC)),
                   _const((1, LANES)), _const((1, LANES)), _const((1, SSD_WIDTH)), _const((1, SSD_WIDTH))],
        out_shape=[jax.ShapeDtypeStruct((t, SSD_XBC), F32), jax.ShapeDtypeStruct((t, SSD_WIDTH), F32),
                   jax.ShapeDtypeStruct((t, LANES), F32), jax.ShapeDtypeStruct((SUBLANES, SSD_XBC), F32),
                   jax.ShapeDtypeStruct((1, SSD_XBC), F32), jax.ShapeDtypeStruct((1, LANES), F32),
                   jax.ShapeDtypeStruct((1, LANES), F32), jax.ShapeDtypeStruct((1, SSD_WIDTH), F32),
                   jax.ShapeDtypeStruct((1, SSD_WIDTH), F32)],
        scratch_shapes=[pltpu.VMEM((tm + SUBLANES, SSD_XBC), F32), pltpu.VMEM((tm, SSD_XBC), F32),
                        pltpu.VMEM((tm, SSD_XBC), F32), pltpu.VMEM((tm + SUBLANES, SSD_XBC), F32),
                        pltpu.VMEM((4, SSD_STATE, LANES), F32), pltpu.VMEM((SUBLANES, SSD_XBC), F32)],
        compiler_params=_cparams(("arbitrary",)),
    )(dycat, proj, proj, proj, proj, yy, states, cw, cb, dtb, a_neg, d_lanes, nw)


def _cmul_add(ar, ai, br, bi, cr, ci):
    return ar + br * cr - bi * ci, ai + br * ci + bi * cr


def _s5_fwd(proj, bre, bim, cre, cim, d_skip, glu_w, glu_b, coef):
    t = proj.shape[0]
    tm = SCAN_TM
    ng = tm // SUBLANES

    def body(u_ref, bre_ref, bim_ref, cre_ref, cim_ref, d_ref, w_ref, b_ref, coef_ref,
             y_ref, y2_ref, hre_ref, him_ref, carry):
        i = pl.program_id(0)

        @pl.when(i == 0)
        def _():
            carry[...] = jnp.zeros_like(carry)

        u = u_ref[...]
        hre_ref[...] = _dot(u, bre_ref[...])
        him_ref[...] = _dot(u, bim_ref[...])

        def step(gi, car):
            cr_, ci_ = car
            rows = pl.ds(pl.multiple_of(gi * SUBLANES, SUBLANES), SUBLANES)
            r = hre_ref[rows, :]
            m = him_ref[rows, :]
            for k, sh in enumerate((1, 2, 4)):
                r, m = _cmul_add(r, m, coef_ref[k, 0], coef_ref[k, 1], pltpu.roll(r, sh, 0), pltpu.roll(m, sh, 0))
            r, m = _cmul_add(r, m, coef_ref[3, 0], coef_ref[3, 1], cr_, ci_)
            hre_ref[rows, :] = r
            him_ref[rows, :] = m
            return (jnp.broadcast_to(r[SUBLANES - 1:SUBLANES, :], r.shape),
                    jnp.broadcast_to(m[SUBLANES - 1:SUBLANES, :], m.shape))

        cr_, ci_ = lax.fori_loop(0, ng, step, (carry[0], carry[1]))
        carry[0] = cr_
        carry[1] = ci_
        y2 = _dot(hre_ref[...], cre_ref[...]) - _dot(him_ref[...], cim_ref[...]) + d_ref[...] * u
        y2_ref[...] = y2
        ya = _gelu(y2)
        y_ref[...] = ya * _sigmoid(_dot(ya, w_ref[...]) + b_ref[...])

    return pl.pallas_call(
        body, name="s5_fwd", grid=(t // tm,),
        in_specs=[pl.BlockSpec((tm, S5_WIDTH), lambda i: (i, P_U // S5_WIDTH)),
                  _const((S5_WIDTH, S5_NSTATE)), _const((S5_WIDTH, S5_NSTATE)), _const((S5_NSTATE, S5_WIDTH)),
                  _const((S5_NSTATE, S5_WIDTH)), _const((1, S5_WIDTH)), _const((S5_WIDTH, S5_WIDTH)),
                  _const((1, S5_WIDTH)), _const((5, 2, SUBLANES, S5_NSTATE))],
        out_specs=[_rows(tm, S5_WIDTH), _rows(tm, S5_WIDTH), _rows(tm, S5_NSTATE), _rows(tm, S5_NSTATE)],
        out_shape=[jax.ShapeDtypeStruct((t, S5_WIDTH), F32), jax.ShapeDtypeStruct((t, S5_WIDTH), F32),
                   jax.ShapeDtypeStruct((t, S5_NSTATE), F32), jax.ShapeDtypeStruct((t, S5_NSTATE), F32)],
        scratch_shapes=[pltpu.VMEM((2, SUBLANES, S5_NSTATE), F32)],
        compiler_params=_cparams(("arbitrary",)),
    )(proj, bre, bim, cre, cim, d_skip, glu_w, glu_b, coef)


def _s5_bwd(dycat, proj, y2, hre, him, bre, bim, cre, cim, d_skip, glu_w, glu_b, rcoef):
    t = proj.shape[0]
    tm = SCAN_TM
    nt = t // tm
    ng = tm // SUBLANES
    hb = tm // SUBLANES

    def body(dy_ref, u_ref, y2_ref, hre_ref, him_ref, hre_halo, him_halo, bre_ref, bim_ref, cre_ref, cim_ref, d_ref,
             w_ref, b_ref, coef_ref,
             du_ref, dbre_ref, dbim_ref, dcre_ref, dcim_ref, dlam_ref, dd_ref, dw_ref, dgb_ref,
             gre, gim, hpre, hpim, carry):
        i = pl.program_id(0)

        @pl.when(i == 0)
        def _():
            for r in (dbre_ref, dbim_ref, dcre_ref, dcim_ref, dlam_ref, dd_ref, dw_ref, dgb_ref, carry):
                r[...] = jnp.zeros_like(r)

        u = u_ref[...]
        y2 = y2_ref[...]
        dout = dy_ref[...]
        ya = _gelu(y2)
        sg = _sigmoid(_dot(ya, w_ref[...]) + b_ref[...])
        dv = dout * ya * sg * (1.0 - sg)
        dya = dout * sg + _dot_nt(dv, w_ref[...])
        dw_ref[...] += _dot_tn(ya, dv)
        dgb_ref[...] += _sum0(dv)
        dy2 = dya * _gelu_grad(y2)
        dd_ref[...] += _sum0(dy2 * u)
        hre_v = hre_ref[...]
        him_v = him_ref[...]
        dcre_ref[...] += _dot_tn(hre_v, dy2)
        dcim_ref[...] -= _dot_tn(him_v, dy2)
        gre[...] = _dot_nt(dy2, cre_ref[...])
        gim[...] = -_dot_nt(dy2, cim_ref[...])
        first = i == nt - 1
        hpre[0:SUBLANES, :] = jnp.where(first, 0.0, hre_halo[...])
        hpim[0:SUBLANES, :] = jnp.where(first, 0.0, him_halo[...])
        hpre[SUBLANES:SUBLANES + tm, :] = hre_v
        hpim[SUBLANES:SUBLANES + tm, :] = him_v
        row0 = lax.broadcasted_iota(jnp.int32, (SUBLANES, S5_NSTATE), 0) == 0

        def step(k, car):
            cr_, ci_, dlr, dli = car
            gi = ng - 1 - k
            rows = pl.ds(pl.multiple_of(gi * SUBLANES, SUBLANES), SUBLANES)
            nrows = pl.ds(pl.multiple_of(gi * SUBLANES + SUBLANES, SUBLANES), SUBLANES)
            r = gre[rows, :]
            m = gim[rows, :]
            for kk, sh in enumerate((1, 2, 4)):
                r, m = _cmul_add(r, m, coef_ref[kk, 0], coef_ref[kk, 1], pltpu.roll(r, SUBLANES - sh, 0),
                                 pltpu.roll(m, SUBLANES - sh, 0))
            r, m = _cmul_add(r, m, coef_ref[3, 0], coef_ref[3, 1], cr_, ci_)
            gre[rows, :] = r
            gim[rows, :] = m
            pr_ = hpre[rows, :]
            pm_ = hpim[rows, :]
            hr_ = jnp.where(row0, jnp.broadcast_to(pr_[SUBLANES - 1:SUBLANES, :], pr_.shape),
                            pltpu.roll(hpre[nrows, :], 1, 0))
            hm_ = jnp.where(row0, jnp.broadcast_to(pm_[SUBLANES - 1:SUBLANES, :], pm_.shape),
                            pltpu.roll(hpim[nrows, :], 1, 0))
            dlr = dlr + hr_ * r + hm_ * m
            dli = dli + hr_ * m - hm_ * r
            return (jnp.broadcast_to(r[0:1, :], r.shape), jnp.broadcast_to(m[0:1, :], m.shape), dlr, dli)

        z8 = jnp.zeros((SUBLANES, S5_NSTATE), F32)
        cr_, ci_, dlr, dli = lax.fori_loop(0, ng, step, (carry[0], carry[1], z8, z8))
        carry[0] = cr_
        carry[1] = ci_
        dlam_ref[0] += dlr
        dlam_ref[1] += dli
        g_re = gre[...]
        g_im = gim[...]
        du_ref[...] = dy2 * d_ref[...] + _dot_nt(g_re, bre_ref[...]) + _dot_nt(g_im, bim_ref[...])
        dbre_ref[...] += _dot_tn(u, g_re)
        dbim_ref[...] += _dot_tn(u, g_im)

    rev = lambda i: nt - 1 - i
    rrow = lambda n, col=0: pl.BlockSpec((tm, n), lambda i: (rev(i), col))
    halo = pl.BlockSpec((SUBLANES, S5_NSTATE), lambda i: (jnp.maximum(rev(i) * hb - 1, 0), 0))
    return pl.pallas_call(
        body, name="s5_bwd", grid=(nt,),
        in_specs=[rrow(S5_WIDTH, 512 // S5_WIDTH), rrow(S5_WIDTH, P_U // S5_WIDTH), rrow(S5_WIDTH),
                  rrow(S5_NSTATE), rrow(S5_NSTATE), halo, halo,
                  _const((S5_WIDTH, S5_NSTATE)), _const((S5_WIDTH, S5_NSTATE)), _const((S5_NSTATE, S5_WIDTH)),
                  _const((S5_NSTATE, S5_WIDTH)), _const((1, S5_WIDTH)), _const((S5_WIDTH, S5_WIDTH)),
                  _const((1, S5_WIDTH)), _const((5, 2, SUBLANES, S5_NSTATE))],
        out_specs=[rrow(S5_WIDTH), _const((S5_WIDTH, S5_NSTATE)), _const((S5_WIDTH, S5_NSTATE)),
                   _const((S5_NSTATE, S5_WIDTH)), _const((S5_NSTATE, S5_WIDTH)), _const((2, SUBLANES, S5_NSTATE)),
                   _const((1, S5_WIDTH)), _const((S5_WIDTH, S5_WIDTH)), _const((1, S5_WIDTH))],
        out_shape=[jax.ShapeDtypeStruct((t, S5_WIDTH), F32), jax.ShapeDtypeStruct((S5_WIDTH, S5_NSTATE), F32),
                   jax.ShapeDtypeStruct((S5_WIDTH, S5_NSTATE), F32), jax.ShapeDtypeStruct((S5_NSTATE, S5_WIDTH), F32),
                   jax.ShapeDtypeStruct((S5_NSTATE, S5_WIDTH), F32),
                   jax.ShapeDtypeStruct((2, SUBLANES, S5_NSTATE), F32), jax.ShapeDtypeStruct((1, S5_WIDTH), F32),
                   jax.ShapeDtypeStruct((S5_WIDTH, S5_WIDTH), F32), jax.ShapeDtypeStruct((1, S5_WIDTH), F32)],
        scratch_shapes=[pltpu.VMEM((tm, S5_NSTATE), F32), pltpu.VMEM((tm, S5_NSTATE), F32),
                        pltpu.VMEM((tm + SUBLANES, S5_NSTATE), F32), pltpu.VMEM((tm + SUBLANES, S5_NSTATE), F32),
                        pltpu.VMEM((2, SUBLANES, S5_NSTATE), F32)],
        compiler_params=_cparams(("arbitrary",)),
    )(dycat, proj, y2, hre, him, hre, him, bre, bim, cre, cim, d_skip, glu_w, glu_b, rcoef)


def _rg_gates(xc, wa, ba, wx, bx, nsp):
    r = _sigmoid(_dot(xc, wa) + ba)
    ig = _sigmoid(_dot(xc, wx) + bx)
    log_a = nsp * r
    a = jnp.exp(log_a)
    mult = jnp.sqrt(-_expm1(2.0 * log_a))
    return r, ig, a, mult


def _rg_fwd(proj, cw, cb, wa, ba, wx, bx, nsp):
    t = proj.shape[0]
    tm = SCAN_TM
    ng = tm // SUBLANES
    hb = tm // SUBLANES

    def body(x_ref, halo_ref, gt_ref, cw_ref, cb_ref, wa_ref, ba_ref, wx_ref, bx_ref, nsp_ref,
             y_ref, h_ref, xpad, abuf, carry):
        i = pl.program_id(0)

        @pl.when(i == 0)
        def _():
            carry[...] = jnp.zeros_like(carry)

        xpad[0:SUBLANES, :] = jnp.where(i > 0, halo_ref[...], 0.0)
        xpad[SUBLANES:SUBLANES + tm, :] = x_ref[...]
        xc = cb_ref[...] + _conv_taps(xpad, cw_ref[...], tm, SUBLANES - 3)
        _, ig, a, mult = _rg_gates(xc, wa_ref[...], ba_ref[...], wx_ref[...], bx_ref[...], nsp_ref[...])
        abuf[...] = a
        h_ref[...] = mult * (ig * xc)
        sub = lax.broadcasted_iota(jnp.int32, (SUBLANES, RG_WIDTH), 0)

        def step(gi, car):
            rows = pl.ds(pl.multiple_of(gi * SUBLANES, SUBLANES), SUBLANES)
            av = abuf[rows, :]
            bv = h_ref[rows, :]
            for sh in (1, 2, 4):
                m = sub >= sh
                bv = jnp.where(m, av * pltpu.roll(bv, sh, 0) + bv, bv)
                av = jnp.where(m, av * pltpu.roll(av, sh, 0), av)
            hv = bv + av * car
            h_ref[rows, :] = hv
            return jnp.broadcast_to(hv[SUBLANES - 1:SUBLANES, :], hv.shape)

        carry[...] = lax.fori_loop(0, ng, step, carry[...])
        y_ref[...] = h_ref[...] * _gelu(gt_ref[...])

    return pl.pallas_call(
        body, name="rg_fwd", grid=(t // tm,),
        in_specs=[pl.BlockSpec((tm, RG_WIDTH), lambda i: (i, P_XRG // RG_WIDTH)),
                  pl.BlockSpec((SUBLANES, RG_WIDTH), lambda i: (jnp.maximum(i * hb - 1, 0), P_XRG // RG_WIDTH)),
                  pl.BlockSpec((tm, RG_WIDTH), lambda i: (i, P_GRG // RG_WIDTH)),
                  _const((4, RG_WIDTH)), _const((1, RG_WIDTH)), _const((RG_WIDTH, RG_WIDTH)), _const((1, RG_WIDTH)),
                  _const((RG_WIDTH, RG_WIDTH)), _const((1, RG_WIDTH)), _const((1, RG_WIDTH))],
        out_specs=[_rows(tm, RG_WIDTH), _rows(tm, RG_WIDTH)],
        out_shape=[jax.ShapeDtypeStruct((t, RG_WIDTH), F32), jax.ShapeDtypeStruct((t, RG_WIDTH), F32)],
        scratch_shapes=[pltpu.VMEM((tm + SUBLANES, RG_WIDTH), F32), pltpu.VMEM((tm, RG_WIDTH), F32),
                        pltpu.VMEM((SUBLANES, RG_WIDTH), F32)],
        compiler_params=_cparams(("arbitrary",)),
    )(proj, proj, proj, cw, cb, wa, ba, wx, bx, nsp)


def _rg_bwd(dycat, proj, hs, cw, cb, wa, ba, wx, bx, nsp):
    t = proj.shape[0]
    tm = SCAN_TM
    nt = t // tm
    ng = tm // SUBLANES
    hb = tm // SUBLANES

    def body(dy_ref, x_ref, halo_ref, gt_ref, h_ref, h_halo, cw_ref, cb_ref, wa_ref, ba_ref, wx_ref, bx_ref, nsp_ref,
             dx_ref, dgt_ref, dcw_ref, dcb_ref, dwa_ref, dba_ref, dwx_ref, dbx_ref, dnsp_ref,
             xpad, abuf, gbuf, hpad, dabuf, dpad, carry, dnext):
        i = pl.program_id(0)

        @pl.when(i == 0)
        def _():
            for r in (dcw_ref, dcb_ref, dwa_ref, dba_ref, dwx_ref, dbx_ref, dnsp_ref, carry, dnext):
                r[...] = jnp.zeros_like(r)

        first = i == nt - 1
        xpad[0:SUBLANES, :] = jnp.where(first, 0.0, halo_ref[...])
        xpad[SUBLANES:SUBLANES + tm, :] = x_ref[...]
        cw_v = cw_ref[...]
        xc = cb_ref[...] + _conv_taps(xpad, cw_v, tm, SUBLANES - 3)
        nsp_v = nsp_ref[...]
        r, ig, a, mult = _rg_gates(xc, wa_ref[...], ba_ref[...], wx_ref[...], bx_ref[...], nsp_v)
        abuf[...] = a
        hv = h_ref[...]
        hpad[0:SUBLANES, :] = jnp.where(first, 0.0, h_halo[...])
        hpad[SUBLANES:SUBLANES + tm, :] = hv
        gt = gt_ref[...]
        dout = dy_ref[...]
        dgt_ref[...] = dout * hv * _gelu_grad(gt)
        gbuf[...] = dout * _gelu(gt)
        sub = lax.broadcasted_iota(jnp.int32, (SUBLANES, RG_WIDTH), 0)
        last_row = sub == SUBLANES - 1
        row0 = sub == 0

        def step(k, car):
            gi = ng - 1 - k
            rows = pl.ds(pl.multiple_of(gi * SUBLANES, SUBLANES), SUBLANES)
            nrows = pl.ds(pl.multiple_of(gi * SUBLANES + SUBLANES, SUBLANES), SUBLANES)
            av = abuf[rows, :]
            bv = gbuf[rows, :] + jnp.where(last_row, car, 0.0)
            ev = jnp.where(last_row, 0.0, pltpu.roll(av, SUBLANES - 1, 0))
            for sh in (1, 2, 4):
                m = sub < SUBLANES - sh
                bv = jnp.where(m, bv + ev * pltpu.roll(bv, SUBLANES - sh, 0), bv)
                ev = jnp.where(m, ev * pltpu.roll(ev, SUBLANES - sh, 0), 0.0)
            gbuf[rows, :] = bv
            pv = hpad[rows, :]
            hprev = jnp.where(row0, jnp.broadcast_to(pv[SUBLANES - 1:SUBLANES, :], pv.shape),
                              pltpu.roll(hpad[nrows, :], 1, 0))
            dabuf[rows, :] = bv * hprev
            return jnp.broadcast_to((av * bv)[0:1, :], bv.shape)

        carry[...] = lax.fori_loop(0, ng, step, carry[...])
        gv = gbuf[...]
        da = dabuf[...]
        ix = ig * xc
        dmult = gv * ix
        dig = gv * mult * xc
        dxc = gv * mult * ig
        dlog_a = da * a - dmult * (a * a) / mult
        dnsp_ref[...] += _sum0(dlog_a * r)
        dpr = dlog_a * nsp_v * r * (1.0 - r)
        dpi = dig * ig * (1.0 - ig)
        dxc = dxc + _dot_nt(dpr, wa_ref[...]) + _dot_nt(dpi, wx_ref[...])
        dwa_ref[...] += _dot_tn(xc, dpr)
        dwx_ref[...] += _dot_tn(xc, dpi)
        dba_ref[...] += _sum0(dpr)
        dbx_ref[...] += _sum0(dpi)
        dcb_ref[...] += _sum0(dxc)
        for k in range(4):
            dcw_ref[k:k + 1, :] += _sum0(dxc * xpad[SUBLANES - 3 + k:SUBLANES - 3 + k + tm, :])
        dpad[0:tm, :] = dxc
        dpad[tm:tm + SUBLANES, :] = dnext[...]
        dx = cw_v[0:1, :] * dpad[3:3 + tm, :]
        for k in range(1, 4):
            dx = dx + cw_v[k:k + 1, :] * dpad[3 - k:3 - k + tm, :]
        dx_ref[...] = dx
        dnext[...] = dxc[0:SUBLANES, :]

    rev = lambda i: nt - 1 - i
    rrow = lambda n, col=0: pl.BlockSpec((tm, n), lambda i: (rev(i), col))
    sq = _const((RG_WIDTH, RG_WIDTH))
    vec = _const((1, RG_WIDTH))
    return pl.pallas_call(
        body, name="rg_bwd", grid=(nt,),
        in_specs=[rrow(RG_WIDTH, 768 // RG_WIDTH), rrow(RG_WIDTH, P_XRG // RG_WIDTH),
                  pl.BlockSpec((SUBLANES, RG_WIDTH), lambda i: (jnp.maximum(rev(i) * hb - 1, 0), P_XRG // RG_WIDTH)),
                  rrow(RG_WIDTH, P_GRG // RG_WIDTH), rrow(RG_WIDTH),
                  pl.BlockSpec((SUBLANES, RG_WIDTH), lambda i: (jnp.maximum(rev(i) * hb - 1, 0), 0)),
                  _const((4, RG_WIDTH)), vec, sq, vec, sq, vec, vec],
        out_specs=[rrow(RG_WIDTH), rrow(RG_WIDTH), _const((SUBLANES, RG_WIDTH)), vec, sq, vec, sq, vec, vec],
        out_shape=[jax.ShapeDtypeStruct((t, RG_WIDTH), F32), jax.ShapeDtypeStruct((t, RG_WIDTH), F32),
                   jax.ShapeDtypeStruct((SUBLANES, RG_WIDTH), F32), jax.ShapeDtypeStruct((1, RG_WIDTH), F32),
                   jax.ShapeDtypeStruct((RG_WIDTH, RG_WIDTH), F32), jax.ShapeDtypeStruct((1, RG_WIDTH), F32),
                   jax.ShapeDtypeStruct((RG_WIDTH, RG_WIDTH), F32), jax.ShapeDtypeStruct((1, RG_WIDTH), F32),
                   jax.ShapeDtypeStruct((1, RG_WIDTH), F32)],
        scratch_shapes=[pltpu.VMEM((tm + SUBLANES, RG_WIDTH), F32), pltpu.VMEM((tm, RG_WIDTH), F32),
                        pltpu.VMEM((tm, RG_WIDTH), F32), pltpu.VMEM((tm + SUBLANES, RG_WIDTH), F32),
                        pltpu.VMEM((tm, RG_WIDTH), F32), pltpu.VMEM((tm + SUBLANES, RG_WIDTH), F32),
                        pltpu.VMEM((SUBLANES, RG_WIDTH), F32), pltpu.VMEM((SUBLANES, RG_WIDTH), F32)],
        compiler_params=_cparams(("arbitrary",)),
    )(dycat, proj, proj, proj, hs, hs, cw, cb, wa, ba, wx, bx, nsp)


def _block_diag(blocks):
    g, a, b = blocks.shape
    eye = jnp.eye(g, dtype=blocks.dtype)
    return (eye[:, None, :, None] * blocks[:, :, None, :]).reshape(g * a, g * b)


def _block_diag_extract(m, g):
    a, b = m.shape[0] // g, m.shape[1] // g
    m4 = m.reshape(g, a, g, b)
    idx = jnp.arange(g)
    return m4[idx, :, idx, :]


def _s5_prepare(lam_re, lam_im, log_step, b_re, b_im, c_re, c_im):
    step = jnp.exp(log_step)[:, None]
    mag = jnp.exp(lam_re * step)
    lbr = mag * jnp.cos(lam_im * step)
    lbi = mag * jnp.sin(lam_im * step)
    nr, ni = lbr - 1.0, lbi
    den = lam_re * lam_re + lam_im * lam_im
    cr = (nr * lam_re + ni * lam_im) / den
    ci = (ni * lam_re - nr * lam_im) / den
    bbr = cr[..., None] * b_re - ci[..., None] * b_im
    bbi = cr[..., None] * b_im + ci[..., None] * b_re
    bre = _block_diag(jnp.swapaxes(bbr, 1, 2))
    bim = _block_diag(jnp.swapaxes(bbi, 1, 2))
    cre = _block_diag(jnp.swapaxes(c_re, 1, 2))
    cim = _block_diag(jnp.swapaxes(c_im, 1, 2))
    return lbr.reshape(-1), lbi.reshape(-1), bre, bim, cre, cim


def _s5_scan_coef(lbr, lbi, reverse):
    if reverse:
        lbi = -lbi
    pr, pi = [lbr], [lbi]
    for _ in range(7):
        pr, pi = pr + [pr[-1] * lbr - pi[-1] * lbi], pi + [pr[-1] * lbi + pi[-1] * lbr]
    row = jnp.arange(SUBLANES)[:, None]
    tabs = []
    for sh in (1, 2, 4):
        keep = (row < SUBLANES - sh) if reverse else (row >= sh)
        tabs.append(jnp.stack([jnp.where(keep, pr[sh - 1][None, :], 0.0), jnp.where(keep, pi[sh - 1][None, :], 0.0)]))
    powr = jnp.stack(pr)
    powi = jnp.stack(pi)
    if reverse:
        powr, powi = powr[::-1], powi[::-1]
    tabs.append(jnp.stack([powr, powi]))
    tabs.append(jnp.zeros_like(tabs[-1]))
    return jnp.stack(tabs).astype(F32)


def _xy_peers():
    x, y, c = lax.axis_index("x"), lax.axis_index("y"), lax.axis_index("c")
    return x, y, c, [(1 - x, y), (x, 1 - y), (1 - x, 1 - y)]


def _hbm():
    return pl.BlockSpec(memory_space=pl.ANY)


def _xy_allgather(buf, *, name):
    n, w = buf.shape

    def body(x_ref, out_ref, send_sems, recv_sems, local_sem):
        x, y, c, peers = _xy_peers()
        me = 2 * x + y
        own = pltpu.make_async_copy(x_ref, out_ref.at[me], local_sem)
        own.start()
        sends = []
        for k, (px, py) in enumerate(peers):
            cp = pltpu.make_async_remote_copy(src_ref=x_ref, dst_ref=out_ref.at[me], send_sem=send_sems.at[k],
                                              recv_sem=recv_sems.at[k], device_id=(px, py, c), device_id_type=MESH)
            cp.start()
            sends.append(cp)
        for k, (px, py) in enumerate(peers):
            pltpu.make_async_remote_copy(src_ref=x_ref, dst_ref=out_ref.at[2 * px + py], send_sem=send_sems.at[k],
                                         recv_sem=recv_sems.at[k], device_id=(px, py, c),
                                         device_id_type=MESH).wait_recv()
        for cp in sends:
            cp.wait_send()
        own.wait()

    return pl.pallas_call(
        body, name=name, in_specs=[_hbm()], out_specs=_hbm(),
        out_shape=jax.ShapeDtypeStruct((4, n, w), buf.dtype),
        scratch_shapes=[pltpu.SemaphoreType.DMA((3,)), pltpu.SemaphoreType.DMA((3,)), pltpu.SemaphoreType.DMA],
    )(buf)


def _remote(src, dst, send_sem, recv_sem, dev):
    return pltpu.make_async_remote_copy(src_ref=src, dst_ref=dst, send_sem=send_sem, recv_sem=recv_sem,
                                        device_id=dev, device_id_type=MESH)


LAYER_GATHERED = (
    ("ssd_conv_w", (4, 256), 1), ("rg_conv_w", (4, LANES), 1),
    ("w_in", (1024, W_IN_PAD), 1), ("s5_glu_w", (64, 256), 0), ("w_out", (256, 1024), 0), ("xa_wq", (256, 1024), 0),
    ("xa_wk", (256, 1024), 0), ("xa_wv", (256, 1024), 0), ("xa_wo", (256, 1024), 0), ("mlp_w1", (1024, 1024), 1),
    ("mlp_w2", (1024, 1024), 0),
)
N_GATHERED = len(LAYER_GATHERED)
WAIT_GROUPS = ((0, 1, 2, 3), (4,), (5, 6, 7, 8), (9, 10))
RG_CONV_SHARD = RG_WIDTH // 4
N_GATHER_COPIES = 3 * N_GATHERED * DEPTH


def _gather_part(ref, t, pos):
    _, shp, ax = LAYER_GATHERED[t % N_GATHERED]
    idx = tuple(pl.ds(pos * shp[ax], shp[ax]) if d == ax else slice(None) for d in range(len(shp)))
    return ref.at[idx]


def _gather_start(shards):
    n = len(shards)
    lands = []
    for t, s in enumerate(shards):
        _, shp, ax = LAYER_GATHERED[t % N_GATHERED]
        full = shp[:ax] + (4 * shp[ax],) + shp[ax + 1:]
        lands.append(pltpu.with_memory_space_constraint(lax.empty(full, s.dtype), pltpu.HBM))

    def body(*refs):
        srcs, lnds = refs[:n], refs[n:2 * n]
        send_sems, recv_sems, local_sems = refs[2 * n:2 * n + 3]
        token = refs[-1]
        x, y, c, peers = _xy_peers()
        me = 2 * x + y
        for t in range(n):
            for k, (px, py) in enumerate(peers):
                _remote(srcs[t], _gather_part(lnds[t], t, me), send_sems.at[k * n + t], recv_sems.at[k * n + t],
                        (px, py, c)).start()
            pltpu.make_async_copy(srcs[t], _gather_part(lnds[t], t, me), local_sems.at[t]).start()
        token[...] = jnp.zeros_like(token)

    hbm = pl.BlockSpec(memory_space=pltpu.HBM)
    sem = pl.BlockSpec(memory_space=pltpu.SEMAPHORE)
    outs = pl.pallas_call(
        body, name="weights_gather_start", in_specs=[hbm] * (2 * n),
        out_shape=(pltpu.SemaphoreType.DMA((3 * n,)), pltpu.SemaphoreType.DMA((3 * n,)),
                   pltpu.SemaphoreType.DMA((n,)),
                   *[pltpu.HBM(s.shape, s.dtype) for s in shards], *[pltpu.HBM(a.shape, a.dtype) for a in lands],
                   jax.ShapeDtypeStruct((SUBLANES, LANES), F32)),
        out_specs=(sem, sem, sem, *[hbm] * (2 * n), pl.BlockSpec(memory_space=pltpu.VMEM)),
        input_output_aliases={i: 3 + i for i in range(2 * n)},
        compiler_params=pltpu.CompilerParams(has_side_effects=pltpu.SideEffectType.DATAFLOW_SIDE_EFFECTING),
    )(*[pltpu.with_memory_space_constraint(s, pltpu.HBM) for s in shards], *lands)
    return outs[0], outs[1], outs[2], outs[3:3 + n], outs[3 + n:3 + 2 * n], outs[-1]


def _gather_wait(handle, ts, after, *, name):
    send_sems, recv_sems, local_sems, src_thru, land_thru, _ = handle
    n = len(src_thru)
    m = len(ts)

    def body(*refs):
        srcs, lnds = refs[:m], refs[m:2 * m]
        ssem, rsem, lsem = refs[2 * m:2 * m + 3]
        x, y, c, peers = _xy_peers()
        me = 2 * x + y
        for i, t in enumerate(ts):
            for k, (px, py) in enumerate(peers):
                cp = _remote(srcs[i], _gather_part(lnds[i], t, 2 * px + py), ssem.at[k * n + t], rsem.at[k * n + t],
                             (px, py, c))
                cp.wait_send()
                cp.wait_recv()
            pltpu.make_async_copy(srcs[i], _gather_part(lnds[i], t, me), lsem.at[t]).wait()

    hbm = pl.BlockSpec(memory_space=pltpu.HBM)
    sem = pl.BlockSpec(memory_space=pltpu.SEMAPHORE)
    args = [src_thru[t] for t in ts] + [land_thru[t] for t in ts]
    outs = pl.pallas_call(
        body, name=name, in_specs=[hbm] * (2 * m) + [sem, sem, sem, pl.BlockSpec(memory_space=pl.ANY)],
        out_shape=[pltpu.HBM(a.shape, a.dtype) for a in args], out_specs=[hbm] * (2 * m),
        input_output_aliases={i: i for i in range(2 * m)},
        compiler_params=pltpu.CompilerParams(has_side_effects=pltpu.SideEffectType.DATAFLOW_SIDE_EFFECTING),
    )(*args, send_sems, recv_sems, local_sems, after)
    return outs[:m], outs[m:]


C_CHUNKS = 8
XY_CHUNKS = 4
EW_ROWS = 512


def _c_exchange(g, part):
    w = g.shape[2]
    row0, nrows = G_PARTS[part]
    half = nrows // 2
    rq = half // C_CHUNKS

    def body(g_ref, got_ref, send_sems, recv_sems):
        x, y, c = lax.axis_index("x"), lax.axis_index("y"), lax.axis_index("c")
        cps = []
        for s in range(4):
            for q in range(C_CHUNKS):
                k = s * C_CHUNKS + q
                cp = _remote(g_ref.at[s, pl.ds(row0 + (1 - c) * half + q * rq, rq), :],
                             got_ref.at[s, pl.ds(q * rq, rq), :], send_sems.at[k], recv_sems.at[k], (x, y, 1 - c))
                cp.start()
                cps.append(cp)
        for cp in cps:
            cp.wait_recv()
        for cp in cps:
            cp.wait_send()

    return pl.pallas_call(
        body, name="grad_c_exchange_%d" % part, in_specs=[_hbm()], out_specs=_hbm(),
        out_shape=jax.ShapeDtypeStruct((4, half, w), g.dtype),
        scratch_shapes=[pltpu.SemaphoreType.DMA((4 * C_CHUNKS,)), pltpu.SemaphoreType.DMA((4 * C_CHUNKS,))],
    )(g)


XFER_DTYPE = jnp.bfloat16


def _add_own_half(g, got, c_arr, part):
    w = g.shape[2]
    row0, nrows = G_PARTS[part]
    half = nrows // 2
    nb = half // EW_ROWS
    b0 = row0 // EW_ROWS

    def body(c_ref, a_ref, b_ref, o_ref, t_ref):
        sm = a_ref[...] + b_ref[...]
        o_ref[...] = sm.astype(o_ref.dtype)

        @pl.when(pl.program_id(1) == nb - 1)
        def _():
            t_ref[...] = sm[:, EW_ROWS - MISC_ROWS:, :]

    grid_spec = pltpu.PrefetchScalarGridSpec(
        num_scalar_prefetch=1, grid=(4, nb),
        in_specs=[pl.BlockSpec((1, EW_ROWS, w), lambda s, i, c: (s, b0 + c[0] * nb + i, 0)),
                  pl.BlockSpec((1, EW_ROWS, w), lambda s, i, c: (s, i, 0))],
        out_specs=[pl.BlockSpec((1, EW_ROWS, w), lambda s, i, c: (s, i, 0)),
                   pl.BlockSpec((1, MISC_ROWS, w), lambda s, i, c: (s, 0, 0))])
    return pl.pallas_call(
        body, name="grad_add_halves", grid_spec=grid_spec,
        out_shape=[jax.ShapeDtypeStruct((4, half, w), XFER_DTYPE), jax.ShapeDtypeStruct((4, MISC_ROWS, w), g.dtype)],
        compiler_params=_cparams(("arbitrary", "arbitrary")),
    )(c_arr, g, got)


def _xy_exchange(arrs):
    na = len(arrs)
    pieces = []
    for a, arr in enumerate(arrs):
        nch = XY_CHUNKS if a == 0 else 1
        rq = arr.shape[1] // nch
        pieces += [(a, pl.ds(q * rq, rq)) for q in range(nch)]
    npc = len(pieces)

    def body(*refs):
        ins, outs = refs[:na], refs[na:2 * na]
        send_sems, recv_sems, local_sems = refs[2 * na:]
        x, y, c, peers = _xy_peers()
        me = 2 * x + y
        own = []
        for j, (a, rows) in enumerate(pieces):
            cp = pltpu.make_async_copy(ins[a].at[me, rows, :], outs[a].at[me, rows, :], local_sems.at[j])
            cp.start()
            own.append(cp)
        sends = []
        for k, (px, py) in enumerate(peers):
            for j, (a, rows) in enumerate(pieces):
                cp = _remote(ins[a].at[2 * px + py, rows, :], outs[a].at[me, rows, :], send_sems.at[k * npc + j],
                             recv_sems.at[k * npc + j], (px, py, c))
                cp.start()
                sends.append(cp)
        for k, (px, py) in enumerate(peers):
            for j, (a, rows) in enumerate(pieces):
                _remote(ins[a].at[me, rows, :], outs[a].at[2 * px + py, rows, :], send_sems.at[k * npc + j],
                        recv_sems.at[k * npc + j], (px, py, c)).wait_recv()
        for cp in sends:
            cp.wait_send()
        for cp in own:
            cp.wait()

    return pl.pallas_call(
        body, name="grad_xy_exchange", in_specs=[_hbm()] * na, out_specs=[_hbm()] * na,
        out_shape=[jax.ShapeDtypeStruct(a.shape, a.dtype) for a in arrs],
        scratch_shapes=[pltpu.SemaphoreType.DMA((3 * npc,)), pltpu.SemaphoreType.DMA((3 * npc,)),
                        pltpu.SemaphoreType.DMA((npc,))],
    )(*arrs)


def _xy_pieces(arrs):
    pieces = []
    for a, arr in enumerate(arrs):
        nch = XY_CHUNKS if a == 0 else 1
        rq = arr.shape[1] // nch
        pieces += [(a, pl.ds(q * rq, rq)) for q in range(nch)]
    return pieces


def _xy_start(arrs, *, name):
    na = len(arrs)
    pieces = _xy_pieces(arrs)
    npc = len(pieces)
    lands = [pltpu.with_memory_space_constraint(lax.empty(a.shape, a.dtype), pltpu.HBM) for a in arrs]

    def body(*refs):
        ins, outs = refs[:na], refs[na:2 * na]
        send_sems, recv_sems, local_sems = refs[2 * na:2 * na + 3]
        token = refs[-1]
        x, y, c, peers = _xy_peers()
        me = 2 * x + y
        for k, (px, py) in enumerate(peers):
            for j, (a, rows) in enumerate(pieces):
                _remote(ins[a].at[2 * px + py, rows, :], outs[a].at[me, rows, :], send_sems.at[k * npc + j],
                        recv_sems.at[k * npc + j], (px, py, c)).start()
        for j, (a, rows) in enumerate(pieces):
            pltpu.make_async_copy(ins[a].at[me, rows, :], outs[a].at[me, rows, :], local_sems.at[j]).start()
        token[...] = jnp.zeros_like(token)

    hbm = pl.BlockSpec(memory_space=pltpu.HBM)
    sem = pl.BlockSpec(memory_space=pltpu.SEMAPHORE)
    outs = pl.pallas_call(
        body, name=name, in_specs=[hbm] * (2 * na),
        out_shape=(pltpu.SemaphoreType.DMA((3 * npc,)), pltpu.SemaphoreType.DMA((3 * npc,)),
                   pltpu.SemaphoreType.DMA((npc,)),
                   *[pltpu.HBM(a.shape, a.dtype) for a in arrs], *[pltpu.HBM(a.shape, a.dtype) for a in arrs],
                   jax.ShapeDtypeStruct((SUBLANES, LANES), F32)),
        out_specs=(sem, sem, sem, *[hbm] * (2 * na), pl.BlockSpec(memory_space=pltpu.VMEM)),
        input_output_aliases={i: 3 + i for i in range(2 * na)},
        compiler_params=pltpu.CompilerParams(has_side_effects=pltpu.SideEffectType.DATAFLOW_SIDE_EFFECTING),
    )(*[pltpu.with_memory_space_constraint(a, pltpu.HBM) for a in arrs], *lands)
    return (outs[0], outs[1], outs[2], outs[3:3 + na], outs[3 + na:3 + 2 * na]), outs[-1]


def _xy_wait(handle, after, *, name):
    send_sems, recv_sems, local_sems, src_thru, land_thru = handle
    na = len(src_thru)
    pieces = _xy_pieces(src_thru)
    npc = len(pieces)

    def body(*refs):
        ins, outs = refs[:na], refs[na:2 * na]
        ssem, rsem, lsem = refs[2 * na:2 * na + 3]
        x, y, c, peers = _xy_peers()
        me = 2 * x + y
        for k, (px, py) in enumerate(peers):
            for j, (a, rows) in enumerate(pieces):
                cp = _remote(ins[a].at[me, rows, :], outs[a].at[2 * px + py, rows, :], ssem.at[k * npc + j],
                             rsem.at[k * npc + j], (px, py, c))
                cp.wait_send()
                cp.wait_recv()
        for j, (a, rows) in enumerate(pieces):
            pltpu.make_async_copy(ins[a].at[me, rows, :], outs[a].at[me, rows, :], lsem.at[j]).wait()

    hbm = pl.BlockSpec(memory_space=pltpu.HBM)
    sem = pl.BlockSpec(memory_space=pltpu.SEMAPHORE)
    args = list(src_thru) + list(land_thru)
    outs = pl.pallas_call(
        body, name=name, in_specs=[hbm] * (2 * na) + [sem, sem, sem, pl.BlockSpec(memory_space=pl.ANY)],
        out_shape=[pltpu.HBM(a.shape, a.dtype) for a in args], out_specs=[hbm] * (2 * na),
        input_output_aliases={i: i for i in range(2 * na)},
        compiler_params=pltpu.CompilerParams(has_side_effects=pltpu.SideEffectType.DATAFLOW_SIDE_EFFECTING),
    )(*args, send_sems, recv_sems, local_sems, after)
    return outs[na:]


def _sum4_into_half(r, rt, c_arr, part, fbuf):
    _, half, w = r.shape
    nb = half // EW_ROWS
    b0 = G_PARTS[part][0] // EW_ROWS

    def body(c_ref, r_ref, t_ref, *rest):
        o_ref = rest[-1]
        o_ref[...] = ((r_ref[0].astype(F32) + r_ref[1].astype(F32)) + r_ref[2].astype(F32)) + r_ref[3].astype(F32)

        @pl.when(pl.program_id(0) == nb - 1)
        def _():
            o_ref[EW_ROWS - MISC_ROWS:, :] = ((t_ref[0] + t_ref[1]) + t_ref[2]) + t_ref[3]

    in_specs = [pl.BlockSpec((4, EW_ROWS, w), lambda i, c: (0, i, 0)),
                pl.BlockSpec((4, MISC_ROWS, w), lambda i, c: (0, 0, 0))]
    args = [c_arr, r, rt]
    aliases = {}
    if fbuf is not None:
        in_specs.append(pl.BlockSpec(memory_space=pl.ANY))
        args.append(fbuf)
        aliases = {3: 0}
    grid_spec = pltpu.PrefetchScalarGridSpec(
        num_scalar_prefetch=1, grid=(nb,), in_specs=in_specs,
        out_specs=pl.BlockSpec((EW_ROWS, w), lambda i, c: (b0 + c[0] * nb + i, 0)))
    return pl.pallas_call(
        body, name="grad_sum4", grid_spec=grid_spec, out_shape=jax.ShapeDtypeStruct((G_ROWS, w), F32),
        input_output_aliases=aliases, compiler_params=_cparams(("arbitrary",)),
    )(*args)


C_GATHER_ROWS = 512


def _c_allgather_halves(f):
    w = f.shape[1]
    chunks = []
    for part, (row0, nrows) in enumerate(G_PARTS):
        chunks += [(part, r) for r in range(0, nrows // 2, C_GATHER_ROWS)]
    nch = len(chunks)

    def body(f_ref, out_ref, send_sems, recv_sems):
        x, y, c = lax.axis_index("x"), lax.axis_index("y"), lax.axis_index("c")

        def rows(q, owner):
            part, r = chunks[q]
            row0, nrows = G_PARTS[part]
            return pl.ds(row0 + owner * (nrows // 2) + r, C_GATHER_ROWS)

        sends = []
        for q in range(nch):
            cp = _remote(f_ref.at[rows(q, c), :], out_ref.at[rows(q, c), :], send_sems.at[q], recv_sems.at[q],
                         (x, y, 1 - c))
            cp.start()
            sends.append(cp)
        for q in range(nch):
            _remote(f_ref.at[rows(q, 1 - c), :], out_ref.at[rows(q, 1 - c), :], send_sems.at[q], recv_sems.at[q],
                    (x, y, 1 - c)).wait_recv()
        for cp in sends:
            cp.wait_send()

    return pl.pallas_call(
        body, name="grad_c_allgather", in_specs=[_hbm()], out_specs=_hbm(), input_output_aliases={0: 0},
        out_shape=jax.ShapeDtypeStruct((G_ROWS, w), f.dtype),
        scratch_shapes=[pltpu.SemaphoreType.DMA((nch,)), pltpu.SemaphoreType.DMA((nch,))],
    )(f)


def _adamw(w, m, v, g, g_rows=None):
    shape = w.shape
    cols = shape[-1]
    rows = int(math.prod(shape)) // cols
    tr = 256 if rows % 256 == 0 else rows
    from_flat = g_rows is not None
    c1 = 1.0 / (1.0 - ADAM_B1 ** ADAM_STEP)
    c2 = 1.0 / (1.0 - ADAM_B2 ** ADAM_STEP)

    def body(w_ref, m_ref, v_ref, g_ref, *outs):
        gg = g_ref[...]
        nm = ADAM_B1 * m_ref[...] + (1.0 - ADAM_B1) * gg
        nv = ADAM_B2 * v_ref[...] + (1.0 - ADAM_B2) * (gg * gg)
        if from_flat:
            outs[0][...] = gg
        d_ref, nm_ref, nv_ref = outs[-3:]
        nm_ref[...] = nm
        nv_ref[...] = nv
        d_ref[...] = -ADAM_LR * ((nm * c1) / (jnp.sqrt(nv * c2) + ADAM_EPS) + ADAM_WD * w_ref[...])

    spec = pl.BlockSpec((tr, cols), lambda i: (i, 0))
    if from_flat:
        nbl = rows // DEPTH // tr
        assert cols == FLAT and all(r % tr == 0 for r in g_rows) and len(g_rows) == DEPTH == 2
        b0, b1 = g_rows[0] // tr, g_rows[1] // tr
        g_spec = pl.BlockSpec((tr, cols), lambda i: (jnp.where(i < nbl, b0 + i, b1 + i - nbl), 0))
        g_arg = g
    else:
        g_spec = spec
        g_arg = g.reshape(rows, cols)
    n_out = 4 if from_flat else 3
    sds = jax.ShapeDtypeStruct((rows, cols), F32)
    outs = pl.pallas_call(
        body, name="adamw", grid=(rows // tr,), in_specs=[spec, spec, spec, g_spec], out_specs=[spec] * n_out,
        out_shape=[sds] * n_out, compiler_params=_cparams(("arbitrary",)),
    )(w.reshape(rows, cols), m.reshape(rows, cols), v.reshape(rows, cols), g_arg)
    outs = [o.reshape(shape) for o in outs]
    return outs if from_flat else [g] + outs


SMALL_SHARDED = (("s5_glu_w", (2, 64, 256), 1), ("ssd_conv_w", (2, 4, 256), 2), ("rg_conv_w", (2, 4, 64), 2))
REPLICATED = (
    ("ssd_conv_b", (2, 1024)), ("ssd_dt_bias", (2, 8)), ("ssd_a_log", (2, 8)), ("ssd_d", (2, 8)),
    ("ssd_norm_w", (2, 512)), ("s5_lam_re", (2, 16, 64)), ("s5_lam_im", (2, 16, 64)), ("s5_log_step", (2, 16)),
    ("s5_b_re", (2, 16, 64, 16)), ("s5_b_im", (2, 16, 64, 16)), ("s5_c_re", (2, 16, 16, 64)),
    ("s5_c_im", (2, 16, 16, 64)), ("s5_d", (2, 256)), ("s5_glu_b", (2, 256)), ("rg_conv_b", (2, 256)),
    ("rg_wa", (2, 4, 64, 64)), ("rg_ba", (2, 4, 64)), ("rg_wx", (2, 4, 64, 64)), ("rg_bx", (2, 4, 64)),
    ("rg_lambda", (2, 256)), ("ln1_g", (2, 1024)), ("ln1_b", (2, 1024)), ("ln2_g", (2, 1024)), ("ln2_b", (2, 1024)),
    ("ln3_g", (2, 1024)), ("ln3_b", (2, 1024)),
)
WEIGHT_ORDER = (
    "w_in", "w_out", "ssd_conv_w", "ssd_conv_b", "ssd_dt_bias", "ssd_a_log", "ssd_d", "ssd_norm_w", "s5_lam_re",
    "s5_lam_im", "s5_log_step", "s5_b_re", "s5_b_im", "s5_c_re", "s5_c_im", "s5_d", "s5_glu_w", "s5_glu_b",
    "rg_conv_w", "rg_conv_b", "rg_wa", "rg_ba", "rg_wx", "rg_bx", "rg_lambda", "ln1_g", "ln1_b", "xa_wq", "xa_wk",
    "xa_wv", "xa_wo", "ln2_g", "ln2_b", "mlp_w1", "mlp_w2", "ln3_g", "ln3_b",
)


def _size(shape):
    return int(math.prod(shape))


def _round_up(a, b):
    return (a + b - 1) // b * b


SMALL_ELEMS = sum(_size(s) for _, s, _ in SMALL_SHARDED)
REP_ELEMS = sum(_size(s) for _, s in REPLICATED)
REP_QROWS = _round_up(-(-REP_ELEMS // (4 * FLAT)), 8)
assert SMALL_ELEMS <= MISC_REP_ROW * FLAT and MISC_REP_ROW + REP_QROWS <= MISC_ROWS


def _pack_shards(tensors, names_shapes):
    return jnp.concatenate([tensors[n].reshape(-1) for n, *_ in names_shapes])


def _unpack(flat, names_shapes):
    out, off = {}, 0
    for n, s, *_ in names_shapes:
        out[n] = flat[off:off + _size(s)].reshape(s)
        off += _size(s)
    return out


def _split_shards(full, names_shapes):
    rows = []
    for k in range(4):
        parts = []
        for n, s, ax in names_shapes:
            w = s[ax]
            parts.append(lax.slice_in_dim(full[n], k * w, (k + 1) * w, axis=ax).reshape(-1))
        rows.append(jnp.concatenate(parts))
    return jnp.stack(rows)


def _pack_cols(w):
    pad = jnp.zeros((w.shape[0], LANES - SSD_HEADS), w.dtype)
    return jnp.concatenate([w[:, O_XBC:O_XBC + 1024], w[:, O_Z:O_Z + 512], w[:, O_U:O_U + 256],
                            w[:, O_XRG:O_XRG + 256], w[:, O_GRG:O_GRG + 256], w[:, O_DT:O_DT + 8], pad], axis=1)


def _unpack_cols(w):
    return jnp.concatenate([w[:, P_Z:P_Z + 512], w[:, P_XBC:P_XBC + 1024], w[:, P_DT:P_DT + 8],
                            w[:, P_U:P_U + 256], w[:, P_XRG:P_XRG + 256], w[:, P_GRG:P_GRG + 256]], axis=1)


def _lanes(v, width):
    return jnp.pad(v, (0, width - v.shape[0])).reshape(1, width)


def _layer_params(rep, l):
    p = {}
    p["ssd_cb"] = rep["ssd_conv_b"][l].reshape(1, -1)
    p["ssd_dtb"] = _lanes(rep["ssd_dt_bias"][l], LANES)
    p["ssd_a"] = _lanes(-jnp.exp(rep["ssd_a_log"][l]), LANES)
    p["ssd_d"] = jnp.repeat(rep["ssd_d"][l], 64).reshape(1, -1)
    p["ssd_nw"] = rep["ssd_norm_w"][l].reshape(1, -1)
    s5_args = tuple(rep[n][l] for n in ("s5_lam_re", "s5_lam_im", "s5_log_step", "s5_b_re", "s5_b_im", "s5_c_re",
                                        "s5_c_im"))
    (lbr, lbi, bre, bim, cre, cim), p["s5_vjp"] = jax.vjp(_s5_prepare, *s5_args)
    p.update(s5_bre=bre, s5_bim=bim, s5_cre=cre, s5_cim=cim)
    p["s5_coef"] = _s5_scan_coef(lbr, lbi, False)
    p["s5_rcoef"] = _s5_scan_coef(lbr, lbi, True)
    p["s5_d"] = rep["s5_d"][l].reshape(1, -1)
    p["s5_gb"] = rep["s5_glu_b"][l].reshape(1, -1)
    p["rg_cb"] = rep["rg_conv_b"][l].reshape(1, -1)
    p["rg_wa"] = _block_diag(rep["rg_wa"][l])
    p["rg_wx"] = _block_diag(rep["rg_wx"][l])
    p["rg_ba"] = rep["rg_ba"][l].reshape(1, -1)
    p["rg_bx"] = rep["rg_bx"][l].reshape(1, -1)
    p["rg_nsp"] = (-RG_C * jax.nn.softplus(-rep["rg_lambda"][l])).reshape(1, -1)
    p["rg_dnsp"] = RG_C * jax.nn.sigmoid(-rep["rg_lambda"][l])
    for n in ("ln1_g", "ln1_b", "ln2_g", "ln2_b", "ln3_g", "ln3_b"):
        p[n] = rep[n][l].reshape(1, -1)
    return p


def _layer_fwd(h, mem, p, fetch):
    s = {"h0": h}
    p.update(fetch(0, h))
    proj = _mm(h, p["w_in"], name="in_proj")
    s["proj"] = proj
    y_ssd, s["ssd_yy"], s["ssd_states"] = _ssd_fwd(proj, p["ssd_cw"], p["ssd_cb"], p["ssd_dtb"], p["ssd_a"],
                                                     p["ssd_d"], p["ssd_nw"])
    y_s5, s["s5_y2"], s["s5_hre"], s["s5_him"] = _s5_fwd(proj, p["s5_bre"], p["s5_bim"], p["s5_cre"], p["s5_cim"],
                                                         p["s5_d"], p["s5_glu_w"], p["s5_gb"], p["s5_coef"])
    y_rg, s["rg_h"] = _rg_fwd(proj, p["rg_cw"], p["rg_cb"], p["rg_wa"], p["rg_ba"], p["rg_wx"], p["rg_bx"],
                              p["rg_nsp"])
    s["ys"] = [y_ssd, y_s5, y_rg]
    p.update(fetch(1, y_rg))
    h1, s["xh1"], s["rs1"] = _outproj_ln_fwd(s["ys"], h, p["w_out"], p["ln1_g"], p["ln1_b"])
    s["h1"] = h1
    p.update(fetch(2, h1))
    kb = _mm(mem, p["xa_wk"], name="mem_proj")
    vb = _mm(mem, p["xa_wv"], name="mem_proj")
    s["kb"], s["vb"] = kb, vb
    h2, s["xh2"], s["rs2"], s["attn_o"] = _attn_ln_fwd(h1, p["xa_wq"], p["xa_wo"], kb, vb, p["ln2_g"], p["ln2_b"])
    s["h2"] = h2
    p.update(fetch(3, h2))
    h3, s["xh3"], s["rs3"], s["mlp_hdn"] = _mlp_ln_fwd(h2, p["mlp_w1"], p["mlp_w2"], p["ln3_g"], p["ln3_b"])
    return h3, s


def _layer_bwd(dh3, mem, p, s, l, gbuf, after_mlp=None):
    g = {}
    dr3, du, dh2, g["ln3_g"], g["ln3_b"] = _mlp_ln_bwd(dh3, s["xh3"], s["rs3"], p["ln3_g"], s["mlp_hdn"],
                                                        p["mlp_w1"], p["mlp_w2"])
    gbuf = _wgrad_flat(s["h2"], du, gbuf, mode="colblk", row_off=_grad_row("mlp_w1", l), name="wgrad_mlp_w1")
    gbuf = _wgrad_flat(s["mlp_hdn"], dr3, gbuf, mode="rowblk", row_off=_grad_row("mlp_w2", l), name="wgrad_mlp_w2")
    ln2_g = p["ln2_g"] if after_mlp is None else p["ln2_g"] + after_mlp(gbuf)[0:1, 0:1]
    dr2, dq, dh1, dkb, dvb, g["ln2_g"], g["ln2_b"] = _attn_ln_bwd(dh2, s["xh2"], s["rs2"], ln2_g, s["h1"],
                                                                   p["xa_wq"], p["xa_wo"], s["kb"], s["vb"])
    for n, a_op, g_op in (("xa_wo", s["attn_o"], dr2), ("xa_wq", s["h1"], dq), ("xa_wk", mem, dkb),
                          ("xa_wv", mem, dvb)):
        gbuf = _wgrad_flat(a_op, g_op, gbuf, mode="rows4", row_off=_grad_row(n, l), name="wgrad_" + n)
    dr1, dres, dycat, g["ln1_g"], g["ln1_b"] = _outproj_ln_bwd(dh1, s["xh1"], s["rs1"], p["ln1_g"], p["w_out"])
    gbuf = _wgrad_flat(s["ys"], dr1, gbuf, mode="rows4", row_off=_grad_row("w_out", l), name="wgrad_w_out")
    proj = s["proj"]
    (dxbc, dz, ddt, dcw, dcb, ddtb, da_neg, dd_l, dnw) = _ssd_bwd(
        dycat, proj, s["ssd_yy"], s["ssd_states"], p["ssd_cw"], p["ssd_cb"], p["ssd_dtb"], p["ssd_a"], p["ssd_d"],
        p["ssd_nw"])
    g["ssd_conv_w"] = dcw[0:4]
    g["ssd_conv_b"] = dcb[0]
    g["ssd_dt_bias"] = ddtb[0, :SSD_HEADS]
    g["ssd_a_log"] = da_neg[0, :SSD_HEADS] * p["ssd_a"][0, :SSD_HEADS]
    g["ssd_d"] = dd_l.reshape(SSD_HEADS, 64).sum(axis=1)
    g["ssd_norm_w"] = dnw[0]
    (du_s5, dbre, dbim, dcre, dcim, dlam, dd5, dgw, dgb) = _s5_bwd(
        dycat, proj, s["s5_y2"], s["s5_hre"], s["s5_him"], p["s5_bre"], p["s5_bim"], p["s5_cre"], p["s5_cim"],
        p["s5_d"], p["s5_glu_w"], p["s5_gb"], p["s5_rcoef"])
    dl = dlam.sum(axis=1)
    s5g = p["s5_vjp"]((dl[0], dl[1], dbre, dbim, dcre, dcim))
    for n, v in zip(("s5_lam_re", "s5_lam_im", "s5_log_step", "s5_b_re", "s5_b_im", "s5_c_re", "s5_c_im"), s5g):
        g[n] = v
    g["s5_d"] = dd5[0]
    g["s5_glu_w"] = dgw
    g["s5_glu_b"] = dgb[0]
    (dxrg, dgrg, drcw, drcb, dwa, dba, dwx, dbx, dnsp) = _rg_bwd(
        dycat, proj, s["rg_h"], p["rg_cw"], p["rg_cb"], p["rg_wa"], p["rg_ba"], p["rg_wx"], p["rg_bx"], p["rg_nsp"])
    g["rg_conv_w"] = drcw[0:4]
    g["rg_conv_b"] = drcb[0]
    g["rg_wa"] = _block_diag_extract(dwa, RG_BLOCKS)
    g["rg_wx"] = _block_diag_extract(dwx, RG_BLOCKS)
    g["rg_ba"] = dba.reshape(RG_BLOCKS, RG_BLOCK_DIM)
    g["rg_bx"] = dbx.reshape(RG_BLOCKS, RG_BLOCK_DIM)
    g["rg_lambda"] = dnsp[0] * p["rg_dnsp"]
    dproj = [dxbc, dz, du_s5, dxrg, dgrg, ddt]
    g["w_in"] = _unpack_cols(_wgrad_in(s["h0"], dproj))
    dh0 = _in_proj_bwd(dproj, p["w_in"], dres)
    for n in ("ln1_g", "ln1_b", "ln2_g", "ln2_b", "ln3_g", "ln3_b"):
        g[n] = g[n][0]
    return dh0, g, gbuf


def _local_step(h, memf, target, rep, fetch):
    params, saved = [], []
    for l in range(DEPTH):
        p = _layer_params(rep, l)
        params.append(p)
        h, s = _layer_fwd(h, memf, p, functools.partial(fetch, l))
        saved.append(s)
    loss11, dh = _loss_fwd_bwd(h, target)
    grads = [None] * DEPTH
    gbuf = None
    c_arr = lax.axis_index("c").astype(jnp.int32).reshape(1)
    handles = {}

    def start_part(buf, part):
        handles[part], token = _xy_start(_chip_sums(buf, c_arr, part), name="grad_xy_start_%d" % part)
        return token

    for l in reversed(range(DEPTH)):
        hook = functools.partial(start_part, part=1) if l == 0 else None
        dh, grads[l], gbuf = _layer_bwd(dh, memf, params[l], saved[l], l, gbuf, hook)
        if l == DEPTH - 1:
            gbuf = lax.dynamic_update_slice(
                gbuf, _w_in_block(grads[l]["w_in"], jnp.zeros((4, MISC_ROWS, FLAT), F32)),
                (0, _grad_row("w_in", l), 0))
            params[0]["ln3_g"] = params[0]["ln3_g"] + start_part(gbuf, 0)[0:1, 0:1]
    gsmall = {n: jnp.stack([grads[l][n] for l in range(DEPTH)]) for n in grads[0] if n != "w_in"}
    return loss11, dh, gsmall, grads[0]["w_in"], gbuf, handles, c_arr


def _w_in_block(gw, tail):
    gw = jnp.pad(gw.reshape(D_MODEL, 4, W_IN_SHARD), ((0, 0), (0, 0), (0, W_IN_PAD - W_IN_SHARD)))
    return jnp.concatenate([jnp.transpose(gw, (1, 0, 2)).reshape(4, W_IN_PAD, FLAT), tail], axis=1)


def _chip_sums(gbuf, c_arr, part):
    return list(_add_own_half(gbuf, _c_exchange(gbuf, part), c_arr, part))


def kernel(x, mem, w_in, w_out, ssd_conv_w, ssd_conv_b, ssd_dt_bias, ssd_a_log, ssd_d, ssd_norm_w, s5_lam_re, s5_lam_im, s5_log_step, s5_b_re, s5_b_im, s5_c_re, s5_c_im, s5_d, s5_glu_w, s5_glu_b, rg_conv_w, rg_conv_b, rg_wa, rg_ba, rg_wx, rg_bx, rg_lambda, ln1_g, ln1_b, xa_wq, xa_wk, xa_wv, xa_wo, ln2_g, ln2_b, mlp_w1, mlp_w2, ln3_g, ln3_b, loss_target, m_w_in, m_w_out, m_ssd_conv_w, m_ssd_conv_b, m_ssd_dt_bias, m_ssd_a_log, m_ssd_d, m_ssd_norm_w, m_s5_lam_re, m_s5_lam_im, m_s5_log_step, m_s5_b_re, m_s5_b_im, m_s5_c_re, m_s5_c_im, m_s5_d, m_s5_glu_w, m_s5_glu_b, m_rg_conv_w, m_rg_conv_b, m_rg_wa, m_rg_ba, m_rg_wx, m_rg_bx, m_rg_lambda, m_ln1_g, m_ln1_b, m_xa_wq, m_xa_wk, m_xa_wv, m_xa_wo, m_ln2_g, m_ln2_b, m_mlp_w1, m_mlp_w2, m_ln3_g, m_ln3_b, v_w_in, v_w_out, v_ssd_conv_w, v_ssd_conv_b, v_ssd_dt_bias, v_ssd_a_log, v_ssd_d, v_ssd_norm_w, v_s5_lam_re, v_s5_lam_im, v_s5_log_step, v_s5_b_re, v_s5_b_im, v_s5_c_re, v_s5_c_im, v_s5_d, v_s5_glu_w, v_s5_glu_b, v_rg_conv_w, v_rg_conv_b, v_rg_wa, v_rg_ba, v_rg_wx, v_rg_bx, v_rg_lambda, v_ln1_g, v_ln1_b, v_xa_wq, v_xa_wk, v_xa_wv, v_xa_wo, v_ln2_g, v_ln2_b, v_mlp_w1, v_mlp_w2, v_ln3_g, v_ln3_b):
    args = dict(locals())
    weights = {n: args[n] for n in WEIGHT_ORDER}
    mom_m = {n: args["m_" + n] for n in WEIGHT_ORDER}
    mom_v = {n: args["v_" + n] for n in WEIGHT_ORDER}

    shards = []
    for l in range(DEPTH):
        for n, shp, ax in LAYER_GATHERED:
            w = weights[n][l]
            if w.shape[1] != shp[1]:
                w = jnp.pad(w, ((0, 0), (0, shp[1] - w.shape[1])))
            if n not in ("ssd_conv_w", "rg_conv_w"):
                w = w.astype(MXU_DTYPE)
            shards.append(w)
    handle = _gather_start(shards)

    def unpad(arr, padded, width):
        return jnp.concatenate([arr[:, padded * k:padded * k + width] for k in range(4)], axis=1)

    def fetch(l, grp, after):
        ts = [l * N_GATHERED + j for j in WAIT_GROUPS[grp]]
        _, landed = _gather_wait(handle, ts, after, name="weights_gather_wait_%d_%d" % (l, grp))
        out = {}
        for t, arr in zip(ts, landed):
            n = LAYER_GATHERED[t % N_GATHERED][0]
            if n == "w_in":
                arr = _pack_cols(unpad(arr, W_IN_PAD, W_IN_SHARD))
            elif n == "rg_conv_w":
                arr = unpad(arr, LANES, RG_CONV_SHARD)
            out[{"ssd_conv_w": "ssd_cw", "rg_conv_w": "rg_cw"}.get(n, n)] = arr
        return out

    rep = {n: weights[n] for n, _ in REPLICATED}

    loss11, dx, gsmall, gw_in0, gbuf, handles, c_arr = _local_step(x[0], mem[0], loss_target[0], rep, fetch)
    grad_x = dx[None]
    loss = lax.psum(loss11[0, 0], ("x", "y", "c"))

    small_q = _split_shards(gsmall, SMALL_SHARDED)
    rep_q = jnp.pad(_pack_shards(gsmall, REPLICATED), (0, 4 * REP_QROWS * FLAT - REP_ELEMS)).reshape(4, -1)
    misc = jnp.concatenate(
        [jnp.pad(small_q, ((0, 0), (0, MISC_REP_ROW * FLAT - SMALL_ELEMS))), rep_q,
         jnp.zeros((4, (MISC_ROWS - MISC_REP_ROW - REP_QROWS) * FLAT), F32)], axis=1).reshape(4, MISC_ROWS, FLAT)
    gbuf = lax.dynamic_update_slice(gbuf, _w_in_block(gw_in0, misc), (0, _grad_row("w_in", 0), 0))
    got = {2: _xy_exchange(_chip_sums(gbuf, c_arr, 2))}
    fbuf = None
    for part in range(len(G_PARTS)):
        if part in handles:
            got[part] = _xy_wait(handles[part], dx, name="grad_xy_wait_%d" % part)
        fbuf = _sum4_into_half(got[part][0], got[part][1], c_arr, part, fbuf)
    reduced = _c_allgather_halves(fbuf)
    misc_red = reduced[ROW_MISC:]
    rep_all = _xy_allgather(misc_red[MISC_REP_ROW:MISC_REP_ROW + REP_QROWS], name="small_grads_allgather")
    g_red = {**_unpack(misc_red[:MISC_REP_ROW].reshape(-1), SMALL_SHARDED),
             **_unpack(rep_all.reshape(-1), REPLICATED)}
    g_red["w_in"] = jnp.stack([
        reduced[_grad_row("w_in", l):_grad_row("w_in", l) + W_IN_PAD].reshape(D_MODEL, W_IN_PAD)[:, :W_IN_SHARD]
        for l in range(DEPTH)])

    res = {}
    for n in WEIGHT_ORDER:
        if n in ("mlp_w1", "mlp_w2", "w_out", "xa_wq", "xa_wk", "xa_wv", "xa_wo"):
            res[n] = _adamw(weights[n], mom_m[n], mom_v[n], reduced, g_rows=[_grad_row(n, l) for l in range(DEPTH)])
        else:
            res[n] = _adamw(weights[n], mom_m[n], mom_v[n], g_red[n])
    return (loss, grad_x, *[res[n][0] for n in WEIGHT_ORDER], *[res[n][1] for n in WEIGHT_ORDER],
            *[res[n][2] for n in WEIGHT_ORDER], *[res[n][3] for n in WEIGHT_ORDER])
```

```python
import functools
import math

import jax
import jax.numpy as jnp
from jax import lax
from jax.experimental import pallas as pl
from jax.experimental.pallas import tpu as pltpu

F32 = jnp.float32
MXU_DTYPE = jnp.bfloat16

D_MODEL = 1024
DEPTH = 2
MEM_LEN = 256
SSD_WIDTH = 512
SSD_HEADS = 8
SSD_STATE = 128
SSD_CHUNK = 128
SSD_XBC = 1024
S5_WIDTH = 256
S5_GROUPS = 16
S5_GROUP_CH = 16
S5_STATE = 64
S5_NSTATE = S5_GROUPS * S5_STATE
RG_WIDTH = 256
RG_BLOCKS = 4
RG_BLOCK_DIM = 64
RG_C = 8.0
XA_HEADS = 4
XA_HEAD_DIM = 256
D_FF = 4096
D_IN = 2312
ALPHA = (2.0 * DEPTH) ** 0.25
LN_EPS = 1e-5
ADAM_LR = 0.001
ADAM_B1 = 0.9
ADAM_B2 = 0.999
ADAM_EPS = 1e-08
ADAM_WD = 0.01
ADAM_STEP = 10

P_XBC, P_Z, P_U, P_XRG, P_GRG, P_DT = 0, 1024, 1536, 1792, 2048, 2304
D_PACK = 2432
O_Z, O_XBC, O_DT, O_U, O_XRG, O_GRG = 0, 512, 1536, 1544, 1800, 2056

LANES = 128
SUBLANES = 8
VMEM_LIMIT = 52 * 1024 * 1024
TM = 512
SSD_TM = 256
SCAN_TM = 512
FLAT = 1024

MESH = pl.DeviceIdType.MESH


def _cparams(sem):
    return pltpu.CompilerParams(dimension_semantics=sem, vmem_limit_bytes=VMEM_LIMIT)


def _dot(a, b):
    return jnp.dot(a.astype(MXU_DTYPE), b.astype(MXU_DTYPE), preferred_element_type=F32)


def _dot_nt(a, b):
    return lax.dot_general(a.astype(MXU_DTYPE), b.astype(MXU_DTYPE), (((1,), (1,)), ((), ())),
                           preferred_element_type=F32)


def _dot_tn(a, b):
    return lax.dot_general(a.astype(MXU_DTYPE), b.astype(MXU_DTYPE), (((0,), (0,)), ((), ())),
                           preferred_element_type=F32)


def _dot_f32(a, b):
    return jnp.dot(a, b, precision=lax.Precision.HIGHEST, preferred_element_type=F32)


def _dot_f32_tn(a, b):
    return lax.dot_general(a, b, (((0,), (0,)), ((), ())), precision=lax.Precision.HIGHEST,
                           preferred_element_type=F32)


def _sigmoid(x):
    return 1.0 / (1.0 + jnp.exp(-x))


def _softplus(x):
    return jnp.maximum(x, 0.0) + jnp.log(1.0 + jnp.exp(-jnp.abs(x)))


_GELU_K = math.sqrt(2.0 / math.pi)


def _gelu(x):
    return 0.5 * x * (1.0 + jnp.tanh(_GELU_K * (x + 0.044715 * x * x * x)))


def _gelu_grad(x):
    t = jnp.tanh(_GELU_K * (x + 0.044715 * x * x * x))
    return 0.5 * (1.0 + t) + 0.5 * x * (1.0 - t * t) * _GELU_K * (1.0 + 3.0 * 0.044715 * x * x)


def _expm1(x):
    small = x * (1.0 + x * (0.5 + x * (1.0 / 6.0 + x * (1.0 / 24.0))))
    return jnp.where(jnp.abs(x) < 0.05, small, jnp.exp(x) - 1.0)


def _sum0(x):
    return jnp.sum(x, axis=0, keepdims=True)


def _ln_fwd(r, g, b):
    mu = jnp.mean(r, axis=-1, keepdims=True)
    xc = r - mu
    var = jnp.mean(xc * xc, axis=-1, keepdims=True)
    rstd = lax.rsqrt(var + LN_EPS)
    xhat = xc * rstd
    return xhat * g + b, xhat, rstd


def _ln_bwd(dout, xhat, rstd, g):
    dxh = dout * g
    m1 = jnp.mean(dxh, axis=-1, keepdims=True)
    m2 = jnp.mean(dxh * xhat, axis=-1, keepdims=True)
    return rstd * (dxh - m1 - xhat * m2)


def _rows(tm, n, col=0):
    return pl.BlockSpec((tm, n), lambda i: (i, col))


def _const(shape):
    nd = len(shape)
    return pl.BlockSpec(shape, lambda i: (0,) * nd)


def _mm(a, w, *, name):
    t, k = a.shape
    n = w.shape[1]
    tm = min(TM, t)

    def body(a_ref, w_ref, o_ref):
        o_ref[...] = _dot(a_ref[...], w_ref[...])

    return pl.pallas_call(
        body, name=name, grid=(t // tm,), in_specs=[_rows(tm, k), _const(w.shape)], out_specs=_rows(tm, n),
        out_shape=jax.ShapeDtypeStruct((t, n), F32), compiler_params=_cparams(("arbitrary",)),
    )(a, w)


DPROJ_PIECES = ((P_XBC, 1024), (P_Z, 512), (P_U, 256), (P_XRG, 256), (P_GRG, 256), (P_DT, LANES))


def _in_proj_bwd(pieces, w, dres):
    t = dres.shape[0]
    npc = len(pieces)

    def body(*refs):
        w_ref, r_ref, o_ref = refs[npc:]
        acc = r_ref[...]
        for p_ref, (off, k) in zip(refs[:npc], DPROJ_PIECES):
            acc = acc + _dot_nt(p_ref[...], w_ref[:, off:off + k])
        o_ref[...] = acc

    return pl.pallas_call(
        body, name="in_proj_bwd", grid=(t // TM,),
        in_specs=[_rows(TM, k) for _, k in DPROJ_PIECES] + [_const(w.shape), _rows(TM, D_MODEL)],
        out_specs=_rows(TM, D_MODEL), out_shape=jax.ShapeDtypeStruct((t, D_MODEL), F32),
        compiler_params=_cparams(("arbitrary",)),
    )(*pieces, w, dres)


def _wgrad_in(h0, pieces):
    t = h0.shape[0]
    npc = len(pieces)

    def body(*refs):
        h_ref, o_ref = refs[npc], refs[npc + 1]
        @pl.when(pl.program_id(0) == 0)
        def _():
            o_ref[...] = jnp.zeros_like(o_ref)

        hb = h_ref[...].astype(MXU_DTYPE)
        for p_ref, (off, k) in zip(refs[:npc], DPROJ_PIECES):
            o_ref[:, off:off + k] += _dot_tn(hb, p_ref[...])

    return pl.pallas_call(
        body, name="wgrad_in", grid=(t // TM,),
        in_specs=[_rows(TM, k) for _, k in DPROJ_PIECES] + [_rows(TM, D_MODEL)],
        out_specs=_const((D_MODEL, D_PACK)), out_shape=jax.ShapeDtypeStruct((D_MODEL, D_PACK), F32),
        compiler_params=_cparams(("arbitrary",)),
    )(*pieces, h0)


G_ROWS = 8192
G_PARTS = ((0, 4096), (4096, 2048), (6144, 2048))
W_IN_SHARD = 578
W_IN_PAD = 640
MISC_ROWS = 128
MISC_REP_ROW = 40
ROW_MISC = G_ROWS - MISC_ROWS
W_IN_BLOCK_ROWS = W_IN_PAD + MISC_ROWS


def _grad_row(name, l):
    base = 0 if l == 1 else 4096
    mid = base + 2048 if l == 1 else 6144
    return {"mlp_w1": base, "mlp_w2": base + 1024, "w_out": mid, "xa_wq": mid + 256, "xa_wk": mid + 512,
            "xa_wv": mid + 768, "xa_wo": mid + 1024, "w_in": mid + 1280}[name]


def _wgrad_flat(a, g, buf, *, mode, row_off, name):
    pieces = list(a) if isinstance(a, (list, tuple)) else [a]
    t = g.shape[0]
    tt = min(1024, t)
    ns = t // tt
    blk = D_MODEL

    def accumulate(o_ref, parts, s):
        @pl.when(s == 0)
        def _():
            o_ref[...] = jnp.zeros_like(o_ref)

        for q, v in parts:
            o_ref[q] += v

    if mode == "rows4":
        grid = (ns,)
        in_specs = [pl.BlockSpec((tt, p.shape[1]), lambda s: (s, 0)) for p in pieces]
        in_specs.append(pl.BlockSpec((tt, blk), lambda s: (s, 0)))
        out_spec = pl.BlockSpec((4, 256, FLAT), lambda s: (0, row_off // 256, 0))
        sem = ("arbitrary",)
        npc = len(pieces)

        def body(*refs):
            g_v = refs[npc][...]
            parts, q0 = [], 0
            for p_ref in refs[:npc]:
                full = _dot_tn(p_ref[...], g_v)
                nq = full.shape[0] // 256
                parts += [(q0 + q, full[q * 256:(q + 1) * 256]) for q in range(nq)]
                q0 += nq
            accumulate(refs[-1], parts, pl.program_id(0))
    else:
        grid = (2, ns)
        if mode == "rowblk":
            in_specs = [pl.BlockSpec((tt, 2 * blk), lambda q, s: (s, q)), pl.BlockSpec((tt, blk), lambda q, s: (s, 0))]
        else:
            in_specs = [pl.BlockSpec((tt, blk), lambda q, s: (s, 0)), pl.BlockSpec((tt, 2 * blk), lambda q, s: (s, q))]
        out_spec = pl.BlockSpec((2, blk, FLAT), lambda q, s: (q, row_off // blk, 0))
        sem = ("arbitrary", "arbitrary")

        def body(a_ref, g_ref, *rest):
            full = _dot_tn(a_ref[...], g_ref[...])
            if mode == "rowblk":
                parts = [(0, full[:blk]), (1, full[blk:])]
            else:
                parts = [(0, full[:, :blk]), (1, full[:, blk:])]
            accumulate(rest[-1], parts, pl.program_id(1))

    args = pieces + [g]
    aliases = {}
    if buf is not None:
        in_specs.append(pl.BlockSpec(memory_space=pl.ANY))
        args.append(buf)
        aliases = {len(args) - 1: 0}
    return pl.pallas_call(
        body, name=name, grid=grid, in_specs=in_specs, out_specs=out_spec,
        out_shape=jax.ShapeDtypeStruct((4, G_ROWS, FLAT), F32), input_output_aliases=aliases,
        compiler_params=_cparams(sem),
    )(*args)


def _outproj_ln_fwd(ys, h, w, g, b):
    t = h.shape[0]
    npc = len(ys)

    def body(*refs):
        h_ref, w_ref, g_ref, b_ref, hn_ref, xh_ref, rs_ref = refs[npc:]
        r = ALPHA * h_ref[...]
        off = 0
        for y_ref in refs[:npc]:
            k = y_ref.shape[1]
            r = r + _dot(y_ref[...], w_ref[off:off + k, :])
            off += k
        out, xhat, rstd = _ln_fwd(r, g_ref[...], b_ref[...])
        hn_ref[...] = out
        xh_ref[...] = xhat
        rs_ref[...] = rstd

    return pl.pallas_call(
        body, name="outproj_ln_fwd", grid=(t // TM,),
        in_specs=[_rows(TM, y.shape[1]) for y in ys] + [_rows(TM, D_MODEL), _const((D_MODEL, D_MODEL)),
                                                        _const((1, D_MODEL)), _const((1, D_MODEL))],
        out_specs=[_rows(TM, D_MODEL), _rows(TM, D_MODEL), _rows(TM, 1)],
        out_shape=[jax.ShapeDtypeStruct((t, D_MODEL), F32), jax.ShapeDtypeStruct((t, D_MODEL), F32),
                   jax.ShapeDtypeStruct((t, 1), F32)],
        compiler_params=_cparams(("arbitrary",)),
    )(*ys, h, w, g, b)


def _attn_probs(q, kb, hh):
    sl = slice(hh * XA_HEAD_DIM, (hh + 1) * XA_HEAD_DIM)
    s = _dot_nt(q[:, sl], kb[:, sl]) * (1.0 / math.sqrt(XA_HEAD_DIM))
    m = jnp.max(s, axis=-1, keepdims=True)
    e = jnp.exp(s - m)
    return e / jnp.sum(e, axis=-1, keepdims=True)


def _attn_ln_fwd(h1, wq, wo, kb, vb, g, b):
    t = h1.shape[0]

    def body(h_ref, wq_ref, wo_ref, k_ref, v_ref, g_ref, b_ref, hn_ref, xh_ref, rs_ref, o_ref):
        h = h_ref[...]
        q = _dot(h, wq_ref[...])
        kb_ = k_ref[...]
        vb_ = v_ref[...]
        for hh in range(XA_HEADS):
            sl = slice(hh * XA_HEAD_DIM, (hh + 1) * XA_HEAD_DIM)
            p = _attn_probs(q, kb_, hh)
            o_ref[:, sl] = _dot(p, vb_[:, sl]).astype(o_ref.dtype)
        r = ALPHA * h + _dot(o_ref[...], wo_ref[...])
        out, xhat, rstd = _ln_fwd(r, g_ref[...], b_ref[...])
        hn_ref[...] = out
        xh_ref[...] = xhat
        rs_ref[...] = rstd

    return pl.pallas_call(
        body, name="attn_ln_fwd", grid=(t // TM,),
        in_specs=[_rows(TM, D_MODEL), _const((D_MODEL, D_MODEL)), _const((D_MODEL, D_MODEL)),
                  _const((MEM_LEN, D_MODEL)), _const((MEM_LEN, D_MODEL)), _const((1, D_MODEL)), _const((1, D_MODEL))],
        out_specs=[_rows(TM, D_MODEL), _rows(TM, D_MODEL), _rows(TM, 1), _rows(TM, D_MODEL)],
        out_shape=[jax.ShapeDtypeStruct((t, D_MODEL), F32), jax.ShapeDtypeStruct((t, D_MODEL), F32),
                   jax.ShapeDtypeStruct((t, 1), F32), jax.ShapeDtypeStruct((t, D_MODEL), MXU_DTYPE)],
        compiler_params=_cparams(("arbitrary",)),
    )(h1, wq, wo, kb, vb, g, b)


def _attn_ln_bwd(dh2, xhat, rstd, g, h1, wq, wo, kb, vb):
    t = h1.shape[0]

    def body(dh_ref, xh_ref, rs_ref, g_ref, h_ref, wq_ref, wo_ref, k_ref, v_ref,
             dr_ref, dq_ref, dh1_ref, dk_ref, dv_ref, dg_ref, db_ref):
        i = pl.program_id(0)

        @pl.when(i == 0)
        def _():
            dk_ref[...] = jnp.zeros_like(dk_ref)
            dv_ref[...] = jnp.zeros_like(dv_ref)
            dg_ref[...] = jnp.zeros_like(dg_ref)
            db_ref[...] = jnp.zeros_like(db_ref)

        dout = dh_ref[...]
        xh = xh_ref[...]
        dg_ref[...] += _sum0(dout * xh)
        db_ref[...] += _sum0(dout)
        dr = _ln_bwd(dout, xh, rs_ref[...], g_ref[...])
        dr_ref[...] = dr.astype(dr_ref.dtype)
        do = _dot_nt(dr, wo_ref[...])
        h = h_ref[...]
        q = _dot(h, wq_ref[...])
        kb_ = k_ref[...]
        vb_ = v_ref[...]
        scale = 1.0 / math.sqrt(XA_HEAD_DIM)
        for hh in range(XA_HEADS):
            sl = slice(hh * XA_HEAD_DIM, (hh + 1) * XA_HEAD_DIM)
            p = _attn_probs(q, kb_, hh)
            do_h = do[:, sl]
            dp = _dot_nt(do_h, vb_[:, sl])
            ds = p * (dp - jnp.sum(dp * p, axis=-1, keepdims=True)) * scale
            dq_ref[:, sl] = _dot(ds, kb_[:, sl]).astype(dq_ref.dtype)
            dk_ref[:, sl] += _dot_tn(ds, q[:, sl])
            dv_ref[:, sl] += _dot_tn(p, do_h)
        dh1_ref[...] = ALPHA * dr + _dot_nt(dq_ref[...], wq_ref[...])

    return pl.pallas_call(
        body, name="attn_ln_bwd", grid=(t // TM,),
        in_specs=[_rows(TM, D_MODEL), _rows(TM, D_MODEL), _rows(TM, 1), _const((1, D_MODEL)), _rows(TM, D_MODEL),
                  _const((D_MODEL, D_MODEL)), _const((D_MODEL, D_MODEL)), _const((MEM_LEN, D_MODEL)),
                  _const((MEM_LEN, D_MODEL))],
        out_specs=[_rows(TM, D_MODEL), _rows(TM, D_MODEL), _rows(TM, D_MODEL), _const((MEM_LEN, D_MODEL)),
                   _const((MEM_LEN, D_MODEL)), _const((1, D_MODEL)), _const((1, D_MODEL))],
        out_shape=[jax.ShapeDtypeStruct((t, D_MODEL), MXU_DTYPE), jax.ShapeDtypeStruct((t, D_MODEL), MXU_DTYPE),
                   jax.ShapeDtypeStruct((t, D_MODEL), F32), jax.ShapeDtypeStruct((MEM_LEN, D_MODEL), F32),
                   jax.ShapeDtypeStruct((MEM_LEN, D_MODEL), F32), jax.ShapeDtypeStruct((1, D_MODEL), F32),
                   jax.ShapeDtypeStruct((1, D_MODEL), F32)],
        compiler_params=_cparams(("arbitrary",)),
    )(dh2, xhat, rstd, g, h1, wq, wo, kb, vb)


FF_CHUNK = 1024
N_FF = D_FF // FF_CHUNK


def _load_resident(pairs, sems):
    copies = [pltpu.make_async_copy(src, dst, sems.at[k]) for k, (src, dst) in enumerate(pairs)]
    for cp in copies:
        cp.start()
    for cp in copies:
        cp.wait()


def _mlp_ln_fwd(h2, w1, w2, g, b):
    t = h2.shape[0]

    def body(h_ref, w1_hbm, w2_hbm, g_ref, b_ref, hn_ref, xh_ref, rs_ref, hd_ref, w1_v, w2_v, acc_ref, sems):
        @pl.when(pl.program_id(0) == 0)
        def _():
            _load_resident([(w1_hbm, w1_v), (w2_hbm, w2_v)], sems)

        h = h_ref[...]
        hb = h.astype(MXU_DTYPE)
        acc_ref[...] = ALPHA * h
        for j in range(N_FF):
            sl = slice(j * FF_CHUNK, (j + 1) * FF_CHUNK)
            u = _dot(hb, w1_v[:, sl])
            hd = jnp.square(jnp.maximum(u, 0.0)).astype(MXU_DTYPE)
            hd_ref[:, sl] = hd
            acc_ref[...] += _dot(hd, w2_v[sl, :])
        out, xhat, rstd = _ln_fwd(acc_ref[...], g_ref[...], b_ref[...])
        hn_ref[...] = out
        xh_ref[...] = xhat
        rs_ref[...] = rstd

    return pl.pallas_call(
        body, name="mlp_ln_fwd", grid=(t // TM,),
        in_specs=[_rows(TM, D_MODEL), _hbm(), _hbm(), _const((1, D_MODEL)), _const((1, D_MODEL))],
        out_specs=[_rows(TM, D_MODEL), _rows(TM, D_MODEL), _rows(TM, 1), _rows(TM, D_FF)],
        out_shape=[jax.ShapeDtypeStruct((t, D_MODEL), F32), jax.ShapeDtypeStruct((t, D_MODEL), F32),
                   jax.ShapeDtypeStruct((t, 1), F32), jax.ShapeDtypeStruct((t, D_FF), MXU_DTYPE)],
        scratch_shapes=[pltpu.VMEM((D_MODEL, D_FF), MXU_DTYPE), pltpu.VMEM((D_FF, D_MODEL), MXU_DTYPE),
                        pltpu.VMEM((TM, D_MODEL), F32), pltpu.SemaphoreType.DMA((2,))],
        compiler_params=_cparams(("arbitrary",)),
    )(h2, w1, w2, g, b)


def _mlp_ln_bwd(dh3, xhat, rstd, g, hdn, w1, w2):
    t = dh3.shape[0]

    def body(dh_ref, xh_ref, rs_ref, g_ref, hd_ref, w1_hbm, w2_hbm,
             dr_ref, du_ref, dh2_ref, dg_ref, db_ref, w1_v, w2_v, acc_ref, sems):
        @pl.when(pl.program_id(0) == 0)
        def _():
            _load_resident([(w1_hbm, w1_v), (w2_hbm, w2_v)], sems)
            dg_ref[...] = jnp.zeros_like(dg_ref)
            db_ref[...] = jnp.zeros_like(db_ref)

        dout = dh_ref[...]
        xh = xh_ref[...]
        dg_ref[...] += _sum0(dout * xh)
        db_ref[...] += _sum0(dout)
        dr = _ln_bwd(dout, xh, rs_ref[...], g_ref[...])
        drb = dr.astype(MXU_DTYPE)
        dr_ref[...] = drb
        acc_ref[...] = ALPHA * dr
        for j in range(N_FF):
            sl = slice(j * FF_CHUNK, (j + 1) * FF_CHUNK)
            dhd = _dot_nt(drb, w2_v[sl, :])
            du = (dhd * (2.0 * jnp.sqrt(hd_ref[:, sl].astype(F32)))).astype(MXU_DTYPE)
            du_ref[:, sl] = du
            acc_ref[...] += _dot_nt(du, w1_v[:, sl])
        dh2_ref[...] = acc_ref[...]

    tm = TM // 2
    return pl.pallas_call(
        body, name="mlp_ln_bwd", grid=(t // tm,),
        in_specs=[_rows(tm, D_MODEL), _rows(tm, D_MODEL), _rows(tm, 1), _const((1, D_MODEL)), _rows(tm, D_FF),
                  _hbm(), _hbm()],
        out_specs=[_rows(tm, D_MODEL), _rows(tm, D_FF), _rows(tm, D_MODEL), _const((1, D_MODEL)),
                   _const((1, D_MODEL))],
        out_shape=[jax.ShapeDtypeStruct((t, D_MODEL), MXU_DTYPE), jax.ShapeDtypeStruct((t, D_FF), MXU_DTYPE),
                   jax.ShapeDtypeStruct((t, D_MODEL), F32), jax.ShapeDtypeStruct((1, D_MODEL), F32),
                   jax.ShapeDtypeStruct((1, D_MODEL), F32)],
        scratch_shapes=[pltpu.VMEM((D_MODEL, D_FF), MXU_DTYPE), pltpu.VMEM((D_FF, D_MODEL), MXU_DTYPE),
                        pltpu.VMEM((tm, D_MODEL), F32), pltpu.SemaphoreType.DMA((2,))],
        compiler_params=_cparams(("arbitrary",)),
    )(dh3, xhat, rstd, g, hdn, w1, w2)


def _outproj_ln_bwd(dh1, xhat, rstd, g, w):
    t = dh1.shape[0]

    def body(dh_ref, xh_ref, rs_ref, g_ref, w_ref, dr_ref, res_ref, dy_ref, dg_ref, db_ref):
        i = pl.program_id(0)

        @pl.when(i == 0)
        def _():
            dg_ref[...] = jnp.zeros_like(dg_ref)
            db_ref[...] = jnp.zeros_like(db_ref)

        dout = dh_ref[...]
        xh = xh_ref[...]
        dg_ref[...] += _sum0(dout * xh)
        db_ref[...] += _sum0(dout)
        dr = _ln_bwd(dout, xh, rs_ref[...], g_ref[...])
        dr_ref[...] = dr.astype(dr_ref.dtype)
        res_ref[...] = ALPHA * dr
        dy_ref[...] = _dot_nt(dr, w_ref[...])

    return pl.pallas_call(
        body, name="outproj_ln_bwd", grid=(t // TM,),
        in_specs=[_rows(TM, D_MODEL), _rows(TM, D_MODEL), _rows(TM, 1), _const((1, D_MODEL)),
                  _const((D_MODEL, D_MODEL))],
        out_specs=[_rows(TM, D_MODEL), _rows(TM, D_MODEL), _rows(TM, D_MODEL), _const((1, D_MODEL)),
                   _const((1, D_MODEL))],
        out_shape=[jax.ShapeDtypeStruct((t, D_MODEL), MXU_DTYPE), jax.ShapeDtypeStruct((t, D_MODEL), F32),
                   jax.ShapeDtypeStruct((t, D_MODEL), F32), jax.ShapeDtypeStruct((1, D_MODEL), F32),
                   jax.ShapeDtypeStruct((1, D_MODEL), F32)],
        compiler_params=_cparams(("arbitrary",)),
    )(dh1, xhat, rstd, g, w)


def _loss_fwd_bwd(h, target):
    t = h.shape[0]

    def body(h_ref, t_ref, l_ref, dh_ref):
        i = pl.program_id(0)

        @pl.when(i == 0)
        def _():
            l_ref[...] = jnp.zeros_like(l_ref)

        e = h_ref[...] - t_ref[...]
        dh_ref[...] = e * (1.0 / D_MODEL)
        per_tok = jnp.mean(e * e, axis=-1, keepdims=True)
        l_ref[...] += 0.5 * jnp.sum(per_tok, axis=0, keepdims=True)

    return pl.pallas_call(
        body, name="loss_fwd_bwd", grid=(t // TM,),
        in_specs=[_rows(TM, D_MODEL), _rows(TM, D_MODEL)],
        out_specs=[_const((1, 1)), _rows(TM, D_MODEL)],
        out_shape=[jax.ShapeDtypeStruct((1, 1), F32), jax.ShapeDtypeStruct((t, D_MODEL), F32)],
        compiler_params=_cparams(("arbitrary",)),
    )(h, target)


def _pick_col(x, idx):
    lane = lax.broadcasted_iota(jnp.int32, x.shape, 1)
    return jnp.sum(jnp.where(lane == idx, x, 0.0), axis=1, keepdims=True)


def _pick_row(x, idx):
    sub = lax.broadcasted_iota(jnp.int32, x.shape, 0)
    return jnp.sum(jnp.where(sub == idx, x, 0.0), axis=0, keepdims=True)


def _conv_taps(pad_ref, w, tm, base):
    acc = w[0:1, :] * pad_ref[base:base + tm, :]
    for k in range(1, 4):
        acc = acc + w[k:k + 1, :] * pad_ref[base + k:base + k + tm, :]
    return acc


def _ssd_chunk_common(adt_c, tri):
    cs = _dot_f32(tri, adt_c)
    return cs, cs.T, jnp.exp(cs)


def _ssd_head_terms(cs, cst, ecs, dt_c, h, tri):
    cs_col = _pick_col(cs, h)
    cs_row = _pick_row(cst, h)
    dt_col = _pick_col(dt_c, h)
    cs_last = cs_col[SSD_CHUNK - 1:SSD_CHUNK, :]
    lmat = jnp.exp(jnp.where(tri > 0.0, cs_col - cs_row, -1e30))
    ecs_col = _pick_col(ecs, h)
    decay_col = jnp.exp(cs_last - cs_col)
    return cs_col, dt_col, cs_last, lmat, ecs_col, decay_col


def _ssd_fwd(proj, cw, cb, dtb, a_neg, d_lanes, nw):
    t = proj.shape[0]
    tm = SSD_TM
    nt = t // tm
    ncq = tm // SSD_CHUNK
    hb = tm // SUBLANES

    def body(xbc_ref, halo_ref, z_ref, dt_ref, cw_ref, cb_ref, dtb_ref, a_ref, d_ref, nw_ref,
             y_ref, yy_ref, st_ref, xpad, xact, state):
        i = pl.program_id(0)

        @pl.when(i == 0)
        def _():
            state[...] = jnp.zeros_like(state)

        xpad[0:SUBLANES, :] = jnp.where(i > 0, halo_ref[...], 0.0)
        xpad[SUBLANES:SUBLANES + tm, :] = xbc_ref[...]
        acc = cb_ref[...] + _conv_taps(xpad, cw_ref[...], tm, SUBLANES - 3)
        xact[...] = acc * _sigmoid(acc)
        dt = _softplus(dt_ref[...] + dtb_ref[...])
        adt = dt * a_ref[...]
        r_i = lax.broadcasted_iota(jnp.int32, (SSD_CHUNK, SSD_CHUNK), 0)
        c_i = lax.broadcasted_iota(jnp.int32, (SSD_CHUNK, SSD_CHUNK), 1)
        tri = (r_i >= c_i).astype(F32)
        lane1 = lax.broadcasted_iota(jnp.int32, (1, LANES), 1)
        for c in range(ncq):
            sl = slice(c * SSD_CHUNK, (c + 1) * SSD_CHUNK)
            dt_c = dt[sl]
            cs, cst, ecs = _ssd_chunk_common(adt[sl], tri)
            for g in range(2):
                bg = xact[sl, 512 + g * 128:512 + (g + 1) * 128]
                cg = xact[sl, 768 + g * 128:768 + (g + 1) * 128]
                cbm = _dot_nt(cg, bg)
                for pr in range(2):
                    pi = g * 2 + pr
                    psl = slice(pi * 128, (pi + 1) * 128)
                    xp = xact[sl, psl]
                    prev = state[pi]
                    st_ref[c, pi] = prev
                    yp = xp * d_ref[:, psl]
                    new_s = jnp.zeros((SSD_STATE, LANES), F32)
                    dec_lane = jnp.zeros((1, LANES), F32)
                    for hh in range(2):
                        h = g * 4 + pr * 2 + hh
                        lm = (lane1 >= 64) if hh else (lane1 < 64)
                        _, dt_col, cs_last, lmat, ecs_col, decay_col = _ssd_head_terms(cs, cst, ecs, dt_c, h, tri)
                        xdt = jnp.where(lm, xp, 0.0) * dt_col
                        yp = yp + _dot(cbm * lmat, xdt)
                        yp = yp + _dot(cg * ecs_col, jnp.where(lm, prev, 0.0))
                        new_s = new_s + _dot_tn(bg * decay_col, xdt)
                        dec_lane = dec_lane + jnp.where(lm, jnp.exp(cs_last), 0.0)
                    state[pi] = prev * dec_lane + new_s
                    yy_ref[sl, psl] = yp
        yy = yy_ref[...]
        z = z_ref[...]
        yg = yy * (z * _sigmoid(z))
        ms = jnp.mean(yg * yg, axis=-1, keepdims=True)
        y_ref[...] = (yg * lax.rsqrt(ms + LN_EPS) * nw_ref[...]).astype(y_ref.dtype)

    halo_map = lambda i: (jnp.maximum(i * hb - 1, 0), 0)
    return pl.pallas_call(
        body, name="ssd_fwd", grid=(nt,),
        in_specs=[pl.BlockSpec((tm, SSD_XBC), lambda i: (i, 0)), pl.BlockSpec((SUBLANES, SSD_XBC), halo_map),
                  pl.BlockSpec((tm, SSD_WIDTH), lambda i: (i, P_Z // SSD_WIDTH)),
                  pl.BlockSpec((tm, LANES), lambda i: (i, P_DT // LANES)),
                  _const((4, SSD_XBC)), _const((1, SSD_XBC)), _const((1, LANES)), _const((1, LANES)),
                  _const((1, SSD_WIDTH)), _const((1, SSD_WIDTH))],
        out_specs=[_rows(tm, SSD_WIDTH), _rows(tm, SSD_WIDTH),
                   pl.BlockSpec((ncq, 4, SSD_STATE, LANES), lambda i: (i, 0, 0, 0))],
        out_shape=[jax.ShapeDtypeStruct((t, SSD_WIDTH), MXU_DTYPE), jax.ShapeDtypeStruct((t, SSD_WIDTH), F32),
                   jax.ShapeDtypeStruct((t // SSD_CHUNK, 4, SSD_STATE, LANES), F32)],
        scratch_shapes=[pltpu.VMEM((tm + SUBLANES, SSD_XBC), F32), pltpu.VMEM((tm, SSD_XBC), F32),
                        pltpu.VMEM((4, SSD_STATE, LANES), F32)],
        compiler_params=_cparams(("arbitrary",)),
    )(proj, proj, proj, proj, cw, cb, dtb, a_neg, d_lanes, nw)


def _ssd_bwd(dycat, proj, yy, states, cw, cb, dtb, a_neg, d_lanes, nw):
    t = proj.shape[0]
    tm = SSD_TM
    nt = t // tm
    ncq = tm // SSD_CHUNK
    hb = tm // SUBLANES

    def body(dy_ref, xbc_ref, halo_ref, z_ref, dt_ref, yy_ref, st_ref, cw_ref, cb_ref, dtb_ref, a_ref, d_ref, nw_ref,
             dxbc_ref, dz_ref, ddt_ref, dcw_ref, dcb_ref, ddtb_ref, da_ref, dd_ref, dnw_ref,
             xpad, xact, dxact, dpad, dstate, dnext):
        i = pl.program_id(0)

        @pl.when(i == 0)
        def _():
            for r in (dcw_ref, dcb_ref, ddtb_ref, da_ref, dd_ref, dnw_ref, dstate, dnext):
                r[...] = jnp.zeros_like(r)

        xpad[0:SUBLANES, :] = jnp.where(i < nt - 1, halo_ref[...], 0.0)
        xpad[SUBLANES:SUBLANES + tm, :] = xbc_ref[...]
        cw_v = cw_ref[...]
        acc = cb_ref[...] + _conv_taps(xpad, cw_v, tm, SUBLANES - 3)
        sig = _sigmoid(acc)
        xact[...] = acc * sig
        dt_raw = dt_ref[...] + dtb_ref[...]
        dt = _softplus(dt_raw)
        a_v = a_ref[...]
        adt = dt * a_v
        yy = yy_ref[...]
        z = z_ref[...]
        sz = _sigmoid(z)
        siluz = z * sz
        yg = yy * siluz
        ms = jnp.mean(yg * yg, axis=-1, keepdims=True)
        rinv = lax.rsqrt(ms + LN_EPS)
        dout = dy_ref[...]
        dnw_ref[...] += _sum0(dout * yg * rinv)
        dyn = dout * nw_ref[...]
        dyg = rinv * dyn - yg * (rinv * rinv * rinv) * jnp.mean(dyn * yg, axis=-1, keepdims=True)
        dyy = dyg * siluz
        dz_ref[...] = (dyg * yy * (sz * (1.0 + z * (1.0 - sz)))).astype(dz_ref.dtype)
        dd_ref[...] += _sum0(dyy * xact[:, 0:SSD_WIDTH])

        r_i = lax.broadcasted_iota(jnp.int32, (SSD_CHUNK, SSD_CHUNK), 0)
        c_i = lax.broadcasted_iota(jnp.int32, (SSD_CHUNK, SSD_CHUNK), 1)
        tri = (r_i >= c_i).astype(F32)
        lane1 = lax.broadcasted_iota(jnp.int32, (1, LANES), 1)
        for c in reversed(range(ncq)):
            sl = slice(c * SSD_CHUNK, (c + 1) * SSD_CHUNK)
            dt_c = dt[sl]
            cs, cst, ecs = _ssd_chunk_common(adt[sl], tri)
            cacc = jnp.zeros((SSD_CHUNK, LANES), F32)
            racc = jnp.zeros((SSD_CHUNK, LANES), F32)
            ddtx = jnp.zeros((SSD_CHUNK, LANES), F32)
            for g in range(2):
                bg = xact[sl, 512 + g * 128:512 + (g + 1) * 128]
                cg = xact[sl, 768 + g * 128:768 + (g + 1) * 128]
                cbm = _dot_nt(cg, bg)
                dcb_m = jnp.zeros((SSD_CHUNK, SSD_CHUNK), F32)
                dbg = jnp.zeros((SSD_CHUNK, SSD_STATE), F32)
                dcg = jnp.zeros((SSD_CHUNK, SSD_STATE), F32)
                for pr in range(2):
                    pi = g * 2 + pr
                    psl = slice(pi * 128, (pi + 1) * 128)
                    xp = xact[sl, psl]
                    dyp = dyy[sl, psl]
                    prev = st_ref[c, pi]
                    ds_all = dstate[pi]
                    dxdt_p = jnp.zeros((SSD_CHUNK, LANES), F32)
                    dprev_new = jnp.zeros((SSD_STATE, LANES), F32)
                    dec_lane = jnp.zeros((1, LANES), F32)
                    dt_lanes = jnp.zeros((SSD_CHUNK, LANES), F32)
                    for hh in range(2):
                        h = g * 4 + pr * 2 + hh
                        lm = (lane1 >= 64) if hh else (lane1 < 64)
                        oh_l = (c_i == h).astype(F32)
                        oh_s = (r_i == h).astype(F32)
                        _, dt_col, cs_last, lmat, ecs_col, decay_col = _ssd_head_terms(cs, cst, ecs, dt_c, h, tri)
                        gm = cbm * lmat
                        xm = jnp.where(lm, xp, 0.0)
                        xdt = xm * dt_col
                        dym = jnp.where(lm, dyp, 0.0)
                        prevm = jnp.where(lm, prev, 0.0)
                        dsm = jnp.where(lm, ds_all, 0.0)
                        bdec = bg * decay_col
                        dxdt = _dot_tn(gm, dym) + _dot(bdec, dsm)
                        dxdt_p = dxdt_p + dxdt
                        ddtx = ddtx + oh_l * jnp.sum(dxdt * xm, axis=1, keepdims=True)
                        dt_lanes = dt_lanes + jnp.where(lm, dt_col, 0.0)
                        dgm = _dot_nt(dym, xdt)
                        dcb_m = dcb_m + dgm * lmat
                        w = dgm * gm
                        cacc = cacc + oh_l * jnp.sum(w, axis=1, keepdims=True)
                        racc = racc - oh_s * jnp.sum(w, axis=0, keepdims=True)
                        dce = _dot_nt(dym, prevm)
                        dcg = dcg + dce * ecs_col
                        cacc = cacc + oh_l * (jnp.sum(dce * cg, axis=1, keepdims=True) * ecs_col)
                        dprev_new = dprev_new + _dot_tn(cg * ecs_col, dym)
                        dbdec = _dot_nt(xdt, dsm)
                        dbg = dbg + dbdec * decay_col
                        dd = jnp.sum(dbdec * bg, axis=1, keepdims=True) * decay_col
                        cacc = cacc - oh_l * dd
                        cd = jnp.exp(cs_last)
                        dlast = jnp.sum(dd, axis=0, keepdims=True) + jnp.sum(
                            jnp.sum(dsm * prevm, axis=1, keepdims=True), axis=0, keepdims=True) * cd
                        cacc = cacc + jnp.where((r_i == SSD_CHUNK - 1) & (c_i == h), dlast, 0.0)
                        dec_lane = dec_lane + jnp.where(lm, cd, 0.0)
                    dstate[pi] = ds_all * dec_lane + dprev_new
                    dxact[sl, psl] = dxdt_p * dt_lanes + dyp * d_ref[:, psl]
                dcg = dcg + _dot(dcb_m, bg)
                dbg = dbg + _dot_tn(dcb_m, cg)
                dxact[sl, 512 + g * 128:512 + (g + 1) * 128] = dbg
                dxact[sl, 768 + g * 128:768 + (g + 1) * 128] = dcg
            dcs = cacc + racc.T
            dadt = _dot_f32((r_i <= c_i).astype(F32), dcs)
            ddt = dadt * a_v + ddtx
            da_ref[...] += _sum0(dadt * dt_c)
            ddt_raw = ddt * _sigmoid(dt_raw[sl])
            ddt_ref[sl, :] = ddt_raw.astype(ddt_ref.dtype)
            ddtb_ref[...] += _sum0(ddt_raw)
        dacc = dxact[...] * (sig * (1.0 + acc * (1.0 - sig)))
        dcb_ref[...] += _sum0(dacc)
        for k in range(4):
            dcw_ref[k:k + 1, :] += _sum0(dacc * xpad[SUBLANES - 3 + k:SUBLANES - 3 + k + tm, :])
        dpad[0:tm, :] = dacc
        dpad[tm:tm + SUBLANES, :] = dnext[...]
        dx = cw_v[0:1, :] * dpad[3:3 + tm, :]
        for k in range(1, 4):
            dx = dx + cw_v[k:k + 1, :] * dpad[3 - k:3 - k + tm, :]
        dxbc_ref[...] = dx.astype(dxbc_ref.dtype)
        dnext[...] = dacc[0:SUBLANES, :]

    rev = lambda i: nt - 1 - i
    halo_map = lambda i: (jnp.maximum(rev(i) * hb - 1, 0), 0)
    rrow = lambda n, col=0: pl.BlockSpec((tm, n), lambda i: (rev(i), col))
    return pl.pallas_call(
        body, name="ssd_bwd", grid=(nt,),
        in_specs=[rrow(SSD_WIDTH), rrow(SSD_XBC), pl.BlockSpec((SUBLANES, SSD_XBC), halo_map),
                  rrow(SSD_WIDTH, P_Z // SSD_WIDTH), rrow(LANES, P_DT // LANES), rrow(SSD_WIDTH),
                  pl.BlockSpec((ncq, 4, SSD_STATE, LANES), lambda i: (rev(i), 0, 0, 0)),
                  _const((4, SSD_XBC)), _const((1, SSD_XBC)), _const((1, LANES)), _const((1, LANES)),
                  _const((1, SSD_WIDTH)), _const((1, SSD_WIDTH))],
        out_specs=[rrow(SSD_XBC), rrow(SSD_WIDTH), rrow(LANES), _const((SUBLANES, SSD_XBC)), _const((1, SSD_XBC)),
                   _const((1, LANES)), _const((1, LANES)), _const((1, SSD_WIDTH)), _const((1, SSD_WIDTH))],
        out_shape=[jax.ShapeDtypeStruct((t, SSD_XBC), MXU_DTYPE), jax.ShapeDtypeStruct((t, SSD_WIDTH), MXU_DTYPE),
                   jax.ShapeDtypeStruct((t, LANES), MXU_DTYPE), jax.ShapeDtypeStruct((SUBLANES, SSD_XBC), F32),
                   jax.ShapeDtypeStruct((1, SSD_XBC), F32), jax.ShapeDtypeStruct((1, LANES), F32),
                   jax.ShapeDtypeStruct((1, LANES), F32), jax.ShapeDtypeStruct((1, SSD_WIDTH), F32),
                   jax.ShapeDtypeStruct((1, SSD_WIDTH), F32)],
        scratch_shapes=[pltpu.VMEM((tm + SUBLANES, SSD_XBC), F32), pltpu.VMEM((tm, SSD_XBC), F32),
                        pltpu.VMEM((tm, SSD_XBC), F32), pltpu.VMEM((tm + SUBLANES, SSD_XBC), F32),
                        pltpu.VMEM((4, SSD_STATE, LANES), F32), pltpu.VMEM((SUBLANES, SSD_XBC), F32)],
        compiler_params=_cparams(("arbitrary",)),
    )(dycat, proj, proj, proj, proj, yy, states, cw, cb, dtb, a_neg, d_lanes, nw)


def _cmul_add(ar, ai, br, bi, cr, ci):
    return ar + br * cr - bi * ci, ai + br * ci + bi * cr


def _s5_fwd(proj, bre, bim, cre, cim, d_skip, glu_w, glu_b, coef):
    t = proj.shape[0]
    tm = SCAN_TM
    ng = tm // SUBLANES

    def body(u_ref, bre_ref, bim_ref, cre_ref, cim_ref, d_ref, w_ref, b_ref, coef_ref,
             y_ref, y2_ref, hre_ref, him_ref, carry):
        i = pl.program_id(0)

        @pl.when(i == 0)
        def _():
            carry[...] = jnp.zeros_like(carry)

        u = u_ref[...]
        hre_ref[...] = _dot(u, bre_ref[...])
        him_ref[...] = _dot(u, bim_ref[...])

        def step(gi, car):
            cr_, ci_ = car
            rows = pl.ds(pl.multiple_of(gi * SUBLANES, SUBLANES), SUBLANES)
            r = hre_ref[rows, :]
            m = him_ref[rows, :]
            for k, sh in enumerate((1, 2, 4)):
                r, m = _cmul_add(r, m, coef_ref[k, 0], coef_ref[k, 1], pltpu.roll(r, sh, 0), pltpu.roll(m, sh, 0))
            r, m = _cmul_add(r, m, coef_ref[3, 0], coef_ref[3, 1], cr_, ci_)
            hre_ref[rows, :] = r
            him_ref[rows, :] = m
            return (jnp.broadcast_to(r[SUBLANES - 1:SUBLANES, :], r.shape),
                    jnp.broadcast_to(m[SUBLANES - 1:SUBLANES, :], m.shape))

        cr_, ci_ = lax.fori_loop(0, ng, step, (carry[0], carry[1]))
        carry[0] = cr_
        carry[1] = ci_
        y2 = _dot(hre_ref[...], cre_ref[...]) - _dot(him_ref[...], cim_ref[...]) + d_ref[...] * u
        y2_ref[...] = y2
        ya = _gelu(y2)
        y_ref[...] = (ya * _sigmoid(_dot(ya, w_ref[...]) + b_ref[...])).astype(y_ref.dtype)

    return pl.pallas_call(
        body, name="s5_fwd", grid=(t // tm,),
        in_specs=[pl.BlockSpec((tm, S5_WIDTH), lambda i: (i, P_U // S5_WIDTH)),
                  _const((S5_WIDTH, S5_NSTATE)), _const((S5_WIDTH, S5_NSTATE)), _const((S5_NSTATE, S5_WIDTH)),
                  _const((S5_NSTATE, S5_WIDTH)), _const((1, S5_WIDTH)), _const((S5_WIDTH, S5_WIDTH)),
                  _const((1, S5_WIDTH)), _const((5, 2, SUBLANES, S5_NSTATE))],
        out_specs=[_rows(tm, S5_WIDTH), _rows(tm, S5_WIDTH), _rows(tm, S5_NSTATE), _rows(tm, S5_NSTATE)],
        out_shape=[jax.ShapeDtypeStruct((t, S5_WIDTH), MXU_DTYPE), jax.ShapeDtypeStruct((t, S5_WIDTH), F32),
                   jax.ShapeDtypeStruct((t, S5_NSTATE), F32), jax.ShapeDtypeStruct((t, S5_NSTATE), F32)],
        scratch_shapes=[pltpu.VMEM((2, SUBLANES, S5_NSTATE), F32)],
        compiler_params=_cparams(("arbitrary",)),
    )(proj, bre, bim, cre, cim, d_skip, glu_w, glu_b, coef)


def _s5_bwd(dycat, proj, y2, hre, him, bre, bim, cre, cim, d_skip, glu_w, glu_b, rcoef):
    t = proj.shape[0]
    tm = SCAN_TM
    nt = t // tm
    ng = tm // SUBLANES
    hb = tm // SUBLANES

    def body(dy_ref, u_ref, y2_ref, hre_ref, him_ref, hre_halo, him_halo, bre_ref, bim_ref, cre_ref, cim_ref, d_ref,
             w_ref, b_ref, coef_ref,
             du_ref, dbre_ref, dbim_ref, dcre_ref, dcim_ref, dlam_ref, dd_ref, dw_ref, dgb_ref,
             gre, gim, hpre, hpim, carry):
        i = pl.program_id(0)

        @pl.when(i == 0)
        def _():
            for r in (dbre_ref, dbim_ref, dcre_ref, dcim_ref, dlam_ref, dd_ref, dw_ref, dgb_ref, carry):
                r[...] = jnp.zeros_like(r)

        u = u_ref[...]
        y2 = y2_ref[...]
        dout = dy_ref[...]
        ya = _gelu(y2)
        sg = _sigmoid(_dot(ya, w_ref[...]) + b_ref[...])
        dv = dout * ya * sg * (1.0 - sg)
        dya = dout * sg + _dot_nt(dv, w_ref[...])
        dw_ref[...] += _dot_tn(ya, dv)
        dgb_ref[...] += _sum0(dv)
        dy2 = dya * _gelu_grad(y2)
        dd_ref[...] += _sum0(dy2 * u)
        hre_v = hre_ref[...]
        him_v = him_ref[...]
        dcre_ref[...] += _dot_tn(hre_v, dy2)
        dcim_ref[...] -= _dot_tn(him_v, dy2)
        gre[...] = _dot_nt(dy2, cre_ref[...])
        gim[...] = -_dot_nt(dy2, cim_ref[...])
        first = i == nt - 1
        hpre[0:SUBLANES, :] = jnp.where(first, 0.0, hre_halo[...])
        hpim[0:SUBLANES, :] = jnp.where(first, 0.0, him_halo[...])
        hpre[SUBLANES:SUBLANES + tm, :] = hre_v
        hpim[SUBLANES:SUBLANES + tm, :] = him_v
        row0 = lax.broadcasted_iota(jnp.int32, (SUBLANES, S5_NSTATE), 0) == 0

        def step(k, car):
            cr_, ci_, dlr, dli = car
            gi = ng - 1 - k
            rows = pl.ds(pl.multiple_of(gi * SUBLANES, SUBLANES), SUBLANES)
            nrows = pl.ds(pl.multiple_of(gi * SUBLANES + SUBLANES, SUBLANES), SUBLANES)
            r = gre[rows, :]
            m = gim[rows, :]
            for kk, sh in enumerate((1, 2, 4)):
                r, m = _cmul_add(r, m, coef_ref[kk, 0], coef_ref[kk, 1], pltpu.roll(r, SUBLANES - sh, 0),
                                 pltpu.roll(m, SUBLANES - sh, 0))
            r, m = _cmul_add(r, m, coef_ref[3, 0], coef_ref[3, 1], cr_, ci_)
            gre[rows, :] = r
            gim[rows, :] = m
            pr_ = hpre[rows, :]
            pm_ = hpim[rows, :]
            hr_ = jnp.where(row0, jnp.broadcast_to(pr_[SUBLANES - 1:SUBLANES, :], pr_.shape),
                            pltpu.roll(hpre[nrows, :], 1, 0))
            hm_ = jnp.where(row0, jnp.broadcast_to(pm_[SUBLANES - 1:SUBLANES, :], pm_.shape),
                            pltpu.roll(hpim[nrows, :], 1, 0))
            dlr = dlr + hr_ * r + hm_ * m
            dli = dli + hr_ * m - hm_ * r
            return (jnp.broadcast_to(r[0:1, :], r.shape), jnp.broadcast_to(m[0:1, :], m.shape), dlr, dli)

        z8 = jnp.zeros((SUBLANES, S5_NSTATE), F32)
        cr_, ci_, dlr, dli = lax.fori_loop(0, ng, step, (carry[0], carry[1], z8, z8))
        carry[0] = cr_
        carry[1] = ci_
        dlam_ref[0] += dlr
        dlam_ref[1] += dli
        g_re = gre[...]
        g_im = gim[...]
        du_ref[...] = (dy2 * d_ref[...] + _dot_nt(g_re, bre_ref[...]) + _dot_nt(g_im, bim_ref[...])
                       ).astype(du_ref.dtype)
        dbre_ref[...] += _dot_tn(u, g_re)
        dbim_ref[...] += _dot_tn(u, g_im)

    rev = lambda i: nt - 1 - i
    rrow = lambda n, col=0: pl.BlockSpec((tm, n), lambda i: (rev(i), col))
    halo = pl.BlockSpec((SUBLANES, S5_NSTATE), lambda i: (jnp.maximum(rev(i) * hb - 1, 0), 0))
    return pl.pallas_call(
        body, name="s5_bwd", grid=(nt,),
        in_specs=[rrow(S5_WIDTH, 512 // S5_WIDTH), rrow(S5_WIDTH, P_U // S5_WIDTH), rrow(S5_WIDTH),
                  rrow(S5_NSTATE), rrow(S5_NSTATE), halo, halo,
                  _const((S5_WIDTH, S5_NSTATE)), _const((S5_WIDTH, S5_NSTATE)), _const((S5_NSTATE, S5_WIDTH)),
                  _const((S5_NSTATE, S5_WIDTH)), _const((1, S5_WIDTH)), _const((S5_WIDTH, S5_WIDTH)),
                  _const((1, S5_WIDTH)), _const((5, 2, SUBLANES, S5_NSTATE))],
        out_specs=[rrow(S5_WIDTH), _const((S5_WIDTH, S5_NSTATE)), _const((S5_WIDTH, S5_NSTATE)),
                   _const((S5_NSTATE, S5_WIDTH)), _const((S5_NSTATE, S5_WIDTH)), _const((2, SUBLANES, S5_NSTATE)),
                   _const((1, S5_WIDTH)), _const((S5_WIDTH, S5_WIDTH)), _const((1, S5_WIDTH))],
        out_shape=[jax.ShapeDtypeStruct((t, S5_WIDTH), MXU_DTYPE), jax.ShapeDtypeStruct((S5_WIDTH, S5_NSTATE), F32),
                   jax.ShapeDtypeStruct((S5_WIDTH, S5_NSTATE), F32), jax.ShapeDtypeStruct((S5_NSTATE, S5_WIDTH), F32),
                   jax.ShapeDtypeStruct((S5_NSTATE, S5_WIDTH), F32),
                   jax.ShapeDtypeStruct((2, SUBLANES, S5_NSTATE), F32), jax.ShapeDtypeStruct((1, S5_WIDTH), F32),
                   jax.ShapeDtypeStruct((S5_WIDTH, S5_WIDTH), F32), jax.ShapeDtypeStruct((1, S5_WIDTH), F32)],
        scratch_shapes=[pltpu.VMEM((tm, S5_NSTATE), F32), pltpu.VMEM((tm, S5_NSTATE), F32),
                        pltpu.VMEM((tm + SUBLANES, S5_NSTATE), F32), pltpu.VMEM((tm + SUBLANES, S5_NSTATE), F32),
                        pltpu.VMEM((2, SUBLANES, S5_NSTATE), F32)],
        compiler_params=_cparams(("arbitrary",)),
    )(dycat, proj, y2, hre, him, hre, him, bre, bim, cre, cim, d_skip, glu_w, glu_b, rcoef)


def _rg_gates(xc, wa, ba, wx, bx, nsp):
    r = _sigmoid(_dot(xc, wa) + ba)
    ig = _sigmoid(_dot(xc, wx) + bx)
    log_a = nsp * r
    a = jnp.exp(log_a)
    mult = jnp.sqrt(-_expm1(2.0 * log_a))
    return r, ig, a, mult


def _rg_fwd(proj, cw, cb, wa, ba, wx, bx, nsp):
    t = proj.shape[0]
    tm = SCAN_TM
    ng = tm // SUBLANES
    hb = tm // SUBLANES

    def body(x_ref, halo_ref, gt_ref, cw_ref, cb_ref, wa_ref, ba_ref, wx_ref, bx_ref, nsp_ref,
             y_ref, h_ref, xpad, abuf, carry):
        i = pl.program_id(0)

        @pl.when(i == 0)
        def _():
            carry[...] = jnp.zeros_like(carry)

        xpad[0:SUBLANES, :] = jnp.where(i > 0, halo_ref[...], 0.0)
        xpad[SUBLANES:SUBLANES + tm, :] = x_ref[...]
        xc = cb_ref[...] + _conv_taps(xpad, cw_ref[...], tm, SUBLANES - 3)
        _, ig, a, mult = _rg_gates(xc, wa_ref[...], ba_ref[...], wx_ref[...], bx_ref[...], nsp_ref[...])
        abuf[...] = a
        h_ref[...] = mult * (ig * xc)
        sub = lax.broadcasted_iota(jnp.int32, (SUBLANES, RG_WIDTH), 0)

        def step(gi, car):
            rows = pl.ds(pl.multiple_of(gi * SUBLANES, SUBLANES), SUBLANES)
            av = abuf[rows, :]
            bv = h_ref[rows, :]
            for sh in (1, 2, 4):
                m = sub >= sh
                bv = jnp.where(m, av * pltpu.roll(bv, sh, 0) + bv, bv)
                av = jnp.where(m, av * pltpu.roll(av, sh, 0), av)
            hv = bv + av * car
            h_ref[rows, :] = hv
            return jnp.broadcast_to(hv[SUBLANES - 1:SUBLANES, :], hv.shape)

        carry[...] = lax.fori_loop(0, ng, step, carry[...])
        y_ref[...] = (h_ref[...] * _gelu(gt_ref[...])).astype(y_ref.dtype)

    return pl.pallas_call(
        body, name="rg_fwd", grid=(t // tm,),
        in_specs=[pl.BlockSpec((tm, RG_WIDTH), lambda i: (i, P_XRG // RG_WIDTH)),
                  pl.BlockSpec((SUBLANES, RG_WIDTH), lambda i: (jnp.maximum(i * hb - 1, 0), P_XRG // RG_WIDTH)),
                  pl.BlockSpec((tm, RG_WIDTH), lambda i: (i, P_GRG // RG_WIDTH)),
                  _const((4, RG_WIDTH)), _const((1, RG_WIDTH)), _const((RG_WIDTH, RG_WIDTH)), _const((1, RG_WIDTH)),
                  _const((RG_WIDTH, RG_WIDTH)), _const((1, RG_WIDTH)), _const((1, RG_WIDTH))],
        out_specs=[_rows(tm, RG_WIDTH), _rows(tm, RG_WIDTH)],
        out_shape=[jax.ShapeDtypeStruct((t, RG_WIDTH), MXU_DTYPE), jax.ShapeDtypeStruct((t, RG_WIDTH), F32)],
        scratch_shapes=[pltpu.VMEM((tm + SUBLANES, RG_WIDTH), F32), pltpu.VMEM((tm, RG_WIDTH), F32),
                        pltpu.VMEM((SUBLANES, RG_WIDTH), F32)],
        compiler_params=_cparams(("arbitrary",)),
    )(proj, proj, proj, cw, cb, wa, ba, wx, bx, nsp)


def _rg_bwd(dycat, proj, hs, cw, cb, wa, ba, wx, bx, nsp):
    t = proj.shape[0]
    tm = SCAN_TM
    nt = t // tm
    ng = tm // SUBLANES
    hb = tm // SUBLANES

    def body(dy_ref, x_ref, halo_ref, gt_ref, h_ref, h_halo, cw_ref, cb_ref, wa_ref, ba_ref, wx_ref, bx_ref, nsp_ref,
             dx_ref, dgt_ref, dcw_ref, dcb_ref, dwa_ref, dba_ref, dwx_ref, dbx_ref, dnsp_ref,
             xpad, abuf, gbuf, hpad, dabuf, dpad, carry, dnext):
        i = pl.program_id(0)

        @pl.when(i == 0)
        def _():
            for r in (dcw_ref, dcb_ref, dwa_ref, dba_ref, dwx_ref, dbx_ref, dnsp_ref, carry, dnext):
                r[...] = jnp.zeros_like(r)

        first = i == nt - 1
        xpad[0:SUBLANES, :] = jnp.where(first, 0.0, halo_ref[...])
        xpad[SUBLANES:SUBLANES + tm, :] = x_ref[...]
        cw_v = cw_ref[...]
        xc = cb_ref[...] + _conv_taps(xpad, cw_v, tm, SUBLANES - 3)
        nsp_v = nsp_ref[...]
        r, ig, a, mult = _rg_gates(xc, wa_ref[...], ba_ref[...], wx_ref[...], bx_ref[...], nsp_v)
        abuf[...] = a
        hv = h_ref[...]
        hpad[0:SUBLANES, :] = jnp.where(first, 0.0, h_halo[...])
        hpad[SUBLANES:SUBLANES + tm, :] = hv
        gt = gt_ref[...]
        dout = dy_ref[...]
        dgt_ref[...] = (dout * hv * _gelu_grad(gt)).astype(dgt_ref.dtype)
        gbuf[...] = dout * _gelu(gt)
        sub = lax.broadcasted_iota(jnp.int32, (SUBLANES, RG_WIDTH), 0)
        last_row = sub == SUBLANES - 1
        row0 = sub == 0

        def step(k, car):
            gi = ng - 1 - k
            rows = pl.ds(pl.multiple_of(gi * SUBLANES, SUBLANES), SUBLANES)
            nrows = pl.ds(pl.multiple_of(gi * SUBLANES + SUBLANES, SUBLANES), SUBLANES)
            av = abuf[rows, :]
            bv = gbuf[rows, :] + jnp.where(last_row, car, 0.0)
            ev = jnp.where(last_row, 0.0, pltpu.roll(av, SUBLANES - 1, 0))
            for sh in (1, 2, 4):
                m = sub < SUBLANES - sh
                bv = jnp.where(m, bv + ev * pltpu.roll(bv, SUBLANES - sh, 0), bv)
                ev = jnp.where(m, ev * pltpu.roll(ev, SUBLANES - sh, 0), 0.0)
            gbuf[rows, :] = bv
            pv = hpad[rows, :]
            hprev = jnp.where(row0, jnp.broadcast_to(pv[SUBLANES - 1:SUBLANES, :], pv.shape),
                              pltpu.roll(hpad[nrows, :], 1, 0))
            dabuf[rows, :] = bv * hprev
            return jnp.broadcast_to((av * bv)[0:1, :], bv.shape)

        carry[...] = lax.fori_loop(0, ng, step, carry[...])
        gv = gbuf[...]
        da = dabuf[...]
        ix = ig * xc
        dmult = gv * ix
        dig = gv * mult * xc
        dxc = gv * mult * ig
        dlog_a = da * a - dmult * (a * a) / mult
        dnsp_ref[...] += _sum0(dlog_a * r)
        dpr = dlog_a * nsp_v * r * (1.0 - r)
        dpi = dig * ig * (1.0 - ig)
        dxc = dxc + _dot_nt(dpr, wa_ref[...]) + _dot_nt(dpi, wx_ref[...])
        dwa_ref[...] += _dot_tn(xc, dpr)
        dwx_ref[...] += _dot_tn(xc, dpi)
        dba_ref[...] += _sum0(dpr)
        dbx_ref[...] += _sum0(dpi)
        dcb_ref[...] += _sum0(dxc)
        for k in range(4):
            dcw_ref[k:k + 1, :] += _sum0(dxc * xpad[SUBLANES - 3 + k:SUBLANES - 3 + k + tm, :])
        dpad[0:tm, :] = dxc
        dpad[tm:tm + SUBLANES, :] = dnext[...]
        dx = cw_v[0:1, :] * dpad[3:3 + tm, :]
        for k in range(1, 4):
            dx = dx + cw_v[k:k + 1, :] * dpad[3 - k:3 - k + tm, :]
        dx_ref[...] = dx.astype(dx_ref.dtype)
        dnext[...] = dxc[0:SUBLANES, :]

    rev = lambda i: nt - 1 - i
    rrow = lambda n, col=0: pl.BlockSpec((tm, n), lambda i: (rev(i), col))
    sq = _const((RG_WIDTH, RG_WIDTH))
    vec = _const((1, RG_WIDTH))
    return pl.pallas_call(
        body, name="rg_bwd", grid=(nt,),
        in_specs=[rrow(RG_WIDTH, 768 // RG_WIDTH), rrow(RG_WIDTH, P_XRG // RG_WIDTH),
                  pl.BlockSpec((SUBLANES, RG_WIDTH), lambda i: (jnp.maximum(rev(i) * hb - 1, 0), P_XRG // RG_WIDTH)),
                  rrow(RG_WIDTH, P_GRG // RG_WIDTH), rrow(RG_WIDTH),
                  pl.BlockSpec((SUBLANES, RG_WIDTH), lambda i: (jnp.maximum(rev(i) * hb - 1, 0), 0)),
                  _const((4, RG_WIDTH)), vec, sq, vec, sq, vec, vec],
        out_specs=[rrow(RG_WIDTH), rrow(RG_WIDTH), _const((SUBLANES, RG_WIDTH)), vec, sq, vec, sq, vec, vec],
        out_shape=[jax.ShapeDtypeStruct((t, RG_WIDTH), MXU_DTYPE), jax.ShapeDtypeStruct((t, RG_WIDTH), MXU_DTYPE),
                   jax.ShapeDtypeStruct((SUBLANES, RG_WIDTH), F32), jax.ShapeDtypeStruct((1, RG_WIDTH), F32),
                   jax.ShapeDtypeStruct((RG_WIDTH, RG_WIDTH), F32), jax.ShapeDtypeStruct((1, RG_WIDTH), F32),
                   jax.ShapeDtypeStruct((RG_WIDTH, RG_WIDTH), F32), jax.ShapeDtypeStruct((1, RG_WIDTH), F32),
                   jax.ShapeDtypeStruct((1, RG_WIDTH), F32)],
        scratch_shapes=[pltpu.VMEM((tm + SUBLANES, RG_WIDTH), F32), pltpu.VMEM((tm, RG_WIDTH), F32),
                        pltpu.VMEM((tm, RG_WIDTH), F32), pltpu.VMEM((tm + SUBLANES, RG_WIDTH), F32),
                        pltpu.VMEM((tm, RG_WIDTH), F32), pltpu.VMEM((tm + SUBLANES, RG_WIDTH), F32),
                        pltpu.VMEM((SUBLANES, RG_WIDTH), F32), pltpu.VMEM((SUBLANES, RG_WIDTH), F32)],
        compiler_params=_cparams(("arbitrary",)),
    )(dycat, proj, proj, proj, hs, hs, cw, cb, wa, ba, wx, bx, nsp)


def _block_diag(blocks):
    g, a, b = blocks.shape
    eye = jnp.eye(g, dtype=blocks.dtype)
    return (eye[:, None, :, None] * blocks[:, :, None, :]).reshape(g * a, g * b)


def _block_diag_extract(m, g):
    a, b = m.shape[0] // g, m.shape[1] // g
    m4 = m.reshape(g, a, g, b)
    idx = jnp.arange(g)
    return m4[idx, :, idx, :]


def _s5_prepare(lam_re, lam_im, log_step, b_re, b_im, c_re, c_im):
    step = jnp.exp(log_step)[:, None]
    mag = jnp.exp(lam_re * step)
    lbr = mag * jnp.cos(lam_im * step)
    lbi = mag * jnp.sin(lam_im * step)
    nr, ni = lbr - 1.0, lbi
    den = lam_re * lam_re + lam_im * lam_im
    cr = (nr * lam_re + ni * lam_im) / den
    ci = (ni * lam_re - nr * lam_im) / den
    bbr = cr[..., None] * b_re - ci[..., None] * b_im
    bbi = cr[..., None] * b_im + ci[..., None] * b_re
    bre = _block_diag(jnp.swapaxes(bbr, 1, 2))
    bim = _block_diag(jnp.swapaxes(bbi, 1, 2))
    cre = _block_diag(jnp.swapaxes(c_re, 1, 2))
    cim = _block_diag(jnp.swapaxes(c_im, 1, 2))
    return lbr.reshape(-1), lbi.reshape(-1), bre, bim, cre, cim


def _s5_scan_coef(lbr, lbi, reverse):
    if reverse:
        lbi = -lbi
    pr, pi = [lbr], [lbi]
    for _ in range(7):
        pr, pi = pr + [pr[-1] * lbr - pi[-1] * lbi], pi + [pr[-1] * lbi + pi[-1] * lbr]
    row = jnp.arange(SUBLANES)[:, None]
    tabs = []
    for sh in (1, 2, 4):
        keep = (row < SUBLANES - sh) if reverse else (row >= sh)
        tabs.append(jnp.stack([jnp.where(keep, pr[sh - 1][None, :], 0.0), jnp.where(keep, pi[sh - 1][None, :], 0.0)]))
    powr = jnp.stack(pr)
    powi = jnp.stack(pi)
    if reverse:
        powr, powi = powr[::-1], powi[::-1]
    tabs.append(jnp.stack([powr, powi]))
    tabs.append(jnp.zeros_like(tabs[-1]))
    return jnp.stack(tabs).astype(F32)


def _xy_peers():
    x, y, c = lax.axis_index("x"), lax.axis_index("y"), lax.axis_index("c")
    return x, y, c, [(1 - x, y), (x, 1 - y), (1 - x, 1 - y)]


def _hbm():
    return pl.BlockSpec(memory_space=pl.ANY)


def _xy_allgather(buf, *, name):
    n, w = buf.shape

    def body(x_ref, out_ref, send_sems, recv_sems, local_sem):
        x, y, c, peers = _xy_peers()
        me = 2 * x + y
        own = pltpu.make_async_copy(x_ref, out_ref.at[me], local_sem)
        own.start()
        sends = []
        for k, (px, py) in enumerate(peers):
            cp = pltpu.make_async_remote_copy(src_ref=x_ref, dst_ref=out_ref.at[me], send_sem=send_sems.at[k],
                                              recv_sem=recv_sems.at[k], device_id=(px, py, c), device_id_type=MESH)
            cp.start()
            sends.append(cp)
        for k, (px, py) in enumerate(peers):
            pltpu.make_async_remote_copy(src_ref=x_ref, dst_ref=out_ref.at[2 * px + py], send_sem=send_sems.at[k],
                                         recv_sem=recv_sems.at[k], device_id=(px, py, c),
                                         device_id_type=MESH).wait_recv()
        for cp in sends:
            cp.wait_send()
        own.wait()

    return pl.pallas_call(
        body, name=name, in_specs=[_hbm()], out_specs=_hbm(),
        out_shape=jax.ShapeDtypeStruct((4, n, w), buf.dtype),
        scratch_shapes=[pltpu.SemaphoreType.DMA((3,)), pltpu.SemaphoreType.DMA((3,)), pltpu.SemaphoreType.DMA],
    )(buf)


def _remote(src, dst, send_sem, recv_sem, dev):
    return pltpu.make_async_remote_copy(src_ref=src, dst_ref=dst, send_sem=send_sem, recv_sem=recv_sem,
                                        device_id=dev, device_id_type=MESH)


LAYER_GATHERED = (
    ("ssd_conv_w", (4, 256), 1), ("rg_conv_w", (4, LANES), 1),
    ("w_in", (1024, W_IN_PAD), 1), ("s5_glu_w", (64, 256), 0), ("w_out", (256, 1024), 0), ("xa_wq", (256, 1024), 0),
    ("xa_wk", (256, 1024), 0), ("xa_wv", (256, 1024), 0), ("xa_wo", (256, 1024), 0), ("mlp_w1", (1024, 1024), 1),
    ("mlp_w2", (1024, 1024), 0),
)
N_GATHERED = len(LAYER_GATHERED)
WAIT_GROUPS = ((0, 1, 2, 3), (4,), (5, 6, 7, 8), (9, 10))
RG_CONV_SHARD = RG_WIDTH // 4
N_GATHER_COPIES = 3 * N_GATHERED * DEPTH


def _gather_part(ref, t, pos):
    _, shp, ax = LAYER_GATHERED[t % N_GATHERED]
    idx = tuple(pl.ds(pos * shp[ax], shp[ax]) if d == ax else slice(None) for d in range(len(shp)))
    return ref.at[idx]


def _gather_start(shards):
    n = len(shards)
    lands = []
    for t, s in enumerate(shards):
        _, shp, ax = LAYER_GATHERED[t % N_GATHERED]
        full = shp[:ax] + (4 * shp[ax],) + shp[ax + 1:]
        lands.append(pltpu.with_memory_space_constraint(lax.empty(full, s.dtype), pltpu.HBM))

    def body(*refs):
        srcs, lnds = refs[:n], refs[n:2 * n]
        send_sems, recv_sems, local_sems = refs[2 * n:2 * n + 3]
        token = refs[-1]
        x, y, c, peers = _xy_peers()
        me = 2 * x + y
        for t in range(n):
            for k, (px, py) in enumerate(peers):
                _remote(srcs[t], _gather_part(lnds[t], t, me), send_sems.at[k * n + t], recv_sems.at[k * n + t],
                        (px, py, c)).start()
            pltpu.make_async_copy(srcs[t], _gather_part(lnds[t], t, me), local_sems.at[t]).start()
        token[...] = jnp.zeros_like(token)

    hbm = pl.BlockSpec(memory_space=pltpu.HBM)
    sem = pl.BlockSpec(memory_space=pltpu.SEMAPHORE)
    outs = pl.pallas_call(
        body, name="weights_gather_start", in_specs=[hbm] * (2 * n),
        out_shape=(pltpu.SemaphoreType.DMA((3 * n,)), pltpu.SemaphoreType.DMA((3 * n,)),
                   pltpu.SemaphoreType.DMA((n,)),
                   *[pltpu.HBM(s.shape, s.dtype) for s in shards], *[pltpu.HBM(a.shape, a.dtype) for a in lands],
                   jax.ShapeDtypeStruct((SUBLANES, LANES), F32)),
        out_specs=(sem, sem, sem, *[hbm] * (2 * n), pl.BlockSpec(memory_space=pltpu.VMEM)),
        input_output_aliases={i: 3 + i for i in range(2 * n)},
        compiler_params=pltpu.CompilerParams(has_side_effects=pltpu.SideEffectType.DATAFLOW_SIDE_EFFECTING),
    )(*[pltpu.with_memory_space_constraint(s, pltpu.HBM) for s in shards], *lands)
    return outs[0], outs[1], outs[2], outs[3:3 + n], outs[3 + n:3 + 2 * n], outs[-1]


def _gather_wait(handle, ts, after, *, name):
    send_sems, recv_sems, local_sems, src_thru, land_thru, _ = handle
    n = len(src_thru)
    m = len(ts)

    def body(*refs):
        srcs, lnds = refs[:m], refs[m:2 * m]
        ssem, rsem, lsem = refs[2 * m:2 * m + 3]
        x, y, c, peers = _xy_peers()
        me = 2 * x + y
        for i, t in enumerate(ts):
            for k, (px, py) in enumerate(peers):
                cp = _remote(srcs[i], _gather_part(lnds[i], t, 2 * px + py), ssem.at[k * n + t], rsem.at[k * n + t],
                             (px, py, c))
                cp.wait_send()
                cp.wait_recv()
            pltpu.make_async_copy(srcs[i], _gather_part(lnds[i], t, me), lsem.at[t]).wait()

    hbm = pl.BlockSpec(memory_space=pltpu.HBM)
    sem = pl.BlockSpec(memory_space=pltpu.SEMAPHORE)
    args = [src_thru[t] for t in ts] + [land_thru[t] for t in ts]
    outs = pl.pallas_call(
        body, name=name, in_specs=[hbm] * (2 * m) + [sem, sem, sem, pl.BlockSpec(memory_space=pl.ANY)],
        out_shape=[pltpu.HBM(a.shape, a.dtype) for a in args], out_specs=[hbm] * (2 * m),
        input_output_aliases={i: i for i in range(2 * m)},
        compiler_params=pltpu.CompilerParams(has_side_effects=pltpu.SideEffectType.DATAFLOW_SIDE_EFFECTING),
    )(*args, send_sems, recv_sems, local_sems, after)
    return outs[:m], outs[m:]


C_CHUNKS = 8
XY_CHUNKS = 4
EW_ROWS = 512


def _c_exchange(g, part):
    w = g.shape[2]
    row0, nrows = G_PARTS[part]
    half = nrows // 2
    rq = half // C_CHUNKS

    def body(g_ref, got_ref, send_sems, recv_sems):
        x, y, c = lax.axis_index("x"), lax.axis_index("y"), lax.axis_index("c")
        cps = []
        for s in range(4):
            for q in range(C_CHUNKS):
                k = s * C_CHUNKS + q
                cp = _remote(g_ref.at[s, pl.ds(row0 + (1 - c) * half + q * rq, rq), :],
                             got_ref.at[s, pl.ds(q * rq, rq), :], send_sems.at[k], recv_sems.at[k], (x, y, 1 - c))
                cp.start()
                cps.append(cp)
        for cp in cps:
            cp.wait_recv()
        for cp in cps:
            cp.wait_send()

    return pl.pallas_call(
        body, name="grad_c_exchange_%d" % part, in_specs=[_hbm()], out_specs=_hbm(),
        out_shape=jax.ShapeDtypeStruct((4, half, w), g.dtype),
        scratch_shapes=[pltpu.SemaphoreType.DMA((4 * C_CHUNKS,)), pltpu.SemaphoreType.DMA((4 * C_CHUNKS,))],
    )(g)


XFER_DTYPE = jnp.bfloat16


def _add_own_half(g, got, c_arr, part):
    w = g.shape[2]
    row0, nrows = G_PARTS[part]
    half = nrows // 2
    nb = half // EW_ROWS
    b0 = row0 // EW_ROWS

    def body(c_ref, a_ref, b_ref, o_ref, t_ref):
        sm = a_ref[...] + b_ref[...]
        o_ref[...] = sm.astype(o_ref.dtype)

        @pl.when(pl.program_id(1) == nb - 1)
        def _():
            t_ref[...] = sm[:, EW_ROWS - MISC_ROWS:, :]

    grid_spec = pltpu.PrefetchScalarGridSpec(
        num_scalar_prefetch=1, grid=(4, nb),
        in_specs=[pl.BlockSpec((1, EW_ROWS, w), lambda s, i, c: (s, b0 + c[0] * nb + i, 0)),
                  pl.BlockSpec((1, EW_ROWS, w), lambda s, i, c: (s, i, 0))],
        out_specs=[pl.BlockSpec((1, EW_ROWS, w), lambda s, i, c: (s, i, 0)),
                   pl.BlockSpec((1, MISC_ROWS, w), lambda s, i, c: (s, 0, 0))])
    return pl.pallas_call(
        body, name="grad_add_halves", grid_spec=grid_spec,
        out_shape=[jax.ShapeDtypeStruct((4, half, w), XFER_DTYPE), jax.ShapeDtypeStruct((4, MISC_ROWS, w), g.dtype)],
        compiler_params=_cparams(("arbitrary", "arbitrary")),
    )(c_arr, g, got)


def _xy_exchange(arrs):
    na = len(arrs)
    pieces = []
    for a, arr in enumerate(arrs):
        nch = XY_CHUNKS if a == 0 else 1
        rq = arr.shape[1] // nch
        pieces += [(a, pl.ds(q * rq, rq)) for q in range(nch)]
    npc = len(pieces)

    def body(*refs):
        ins, outs = refs[:na], refs[na:2 * na]
        send_sems, recv_sems, local_sems = refs[2 * na:]
        x, y, c, peers = _xy_peers()
        me = 2 * x + y
        own = []
        for j, (a, rows) in enumerate(pieces):
            cp = pltpu.make_async_copy(ins[a].at[me, rows, :], outs[a].at[me, rows, :], local_sems.at[j])
            cp.start()
            own.append(cp)
        sends = []
        for k, (px, py) in enumerate(peers):
            for j, (a, rows) in enumerate(pieces):
                cp = _remote(ins[a].at[2 * px + py, rows, :], outs[a].at[me, rows, :], send_sems.at[k * npc + j],
                             recv_sems.at[k * npc + j], (px, py, c))
                cp.start()
                sends.append(cp)
        for k, (px, py) in enumerate(peers):
            for j, (a, rows) in enumerate(pieces):
                _remote(ins[a].at[me, rows, :], outs[a].at[2 * px + py, rows, :], send_sems.at[k * npc + j],
                        recv_sems.at[k * npc + j], (px, py, c)).wait_recv()
        for cp in sends:
            cp.wait_send()
        for cp in own:
            cp.wait()

    return pl.pallas_call(
        body, name="grad_xy_exchange", in_specs=[_hbm()] * na, out_specs=[_hbm()] * na,
        out_shape=[jax.ShapeDtypeStruct(a.shape, a.dtype) for a in arrs],
        scratch_shapes=[pltpu.SemaphoreType.DMA((3 * npc,)), pltpu.SemaphoreType.DMA((3 * npc,)),
                        pltpu.SemaphoreType.DMA((npc,))],
    )(*arrs)


def _xy_pieces(arrs):
    pieces = []
    for a, arr in enumerate(arrs):
        nch = XY_CHUNKS if a == 0 else 1
        rq = arr.shape[1] // nch
        pieces += [(a, pl.ds(q * rq, rq)) for q in range(nch)]
    return pieces


def _xy_start(arrs, *, name):
    na = len(arrs)
    pieces = _xy_pieces(arrs)
    npc = len(pieces)
    lands = [pltpu.with_memory_space_constraint(lax.empty(a.shape, a.dtype), pltpu.HBM) for a in arrs]

    def body(*refs):
        ins, outs = refs[:na], refs[na:2 * na]
        send_sems, recv_sems, local_sems = refs[2 * na:2 * na + 3]
        token = refs[-1]
        x, y, c, peers = _xy_peers()
        me = 2 * x + y
        for k, (px, py) in enumerate(peers):
            for j, (a, rows) in enumerate(pieces):
                _remote(ins[a].at[2 * px + py, rows, :], outs[a].at[me, rows, :], send_sems.at[k * npc + j],
                        recv_sems.at[k * npc + j], (px, py, c)).start()
        for j, (a, rows) in enumerate(pieces):
            pltpu.make_async_copy(ins[a].at[me, rows, :], outs[a].at[me, rows, :], local_sems.at[j]).start()
        token[...] = jnp.zeros_like(token)

    hbm = pl.BlockSpec(memory_space=pltpu.HBM)
    sem = pl.BlockSpec(memory_space=pltpu.SEMAPHORE)
    outs = pl.pallas_call(
        body, name=name, in_specs=[hbm] * (2 * na),
        out_shape=(pltpu.SemaphoreType.DMA((3 * npc,)), pltpu.SemaphoreType.DMA((3 * npc,)),
                   pltpu.SemaphoreType.DMA((npc,)),
                   *[pltpu.HBM(a.shape, a.dtype) for a in arrs], *[pltpu.HBM(a.shape, a.dtype) for a in arrs],
                   jax.ShapeDtypeStruct((SUBLANES, LANES), F32)),
        out_specs=(sem, sem, sem, *[hbm] * (2 * na), pl.BlockSpec(memory_space=pltpu.VMEM)),
        input_output_aliases={i: 3 + i for i in range(2 * na)},
        compiler_params=pltpu.CompilerParams(has_side_effects=pltpu.SideEffectType.DATAFLOW_SIDE_EFFECTING),
    )(*[pltpu.with_memory_space_constraint(a, pltpu.HBM) for a in arrs], *lands)
    return (outs[0], outs[1], outs[2], outs[3:3 + na], outs[3 + na:3 + 2 * na]), outs[-1]


def _xy_wait(handle, after, *, name):
    send_sems, recv_sems, local_sems, src_thru, land_thru = handle
    na = len(src_thru)
    pieces = _xy_pieces(src_thru)
    npc = len(pieces)

    def body(*refs):
        ins, outs = refs[:na], refs[na:2 * na]
        ssem, rsem, lsem = refs[2 * na:2 * na + 3]
        x, y, c, peers = _xy_peers()
        me = 2 * x + y
        for k, (px, py) in enumerate(peers):
            for j, (a, rows) in enumerate(pieces):
                cp = _remote(ins[a].at[me, rows, :], outs[a].at[2 * px + py, rows, :], ssem.at[k * npc + j],
                             rsem.at[k * npc + j], (px, py, c))
                cp.wait_send()
                cp.wait_recv()
        for j, (a, rows) in enumerate(pieces):
            pltpu.make_async_copy(ins[a].at[me, rows, :], outs[a].at[me, rows, :], lsem.at[j]).wait()

    hbm = pl.BlockSpec(memory_space=pltpu.HBM)
    sem = pl.BlockSpec(memory_space=pltpu.SEMAPHORE)
    args = list(src_thru) + list(land_thru)
    outs = pl.pallas_call(
        body, name=name, in_specs=[hbm] * (2 * na) + [sem, sem, sem, pl.BlockSpec(memory_space=pl.ANY)],
        out_shape=[pltpu.HBM(a.shape, a.dtype) for a in args], out_specs=[hbm] * (2 * na),
        input_output_aliases={i: i for i in range(2 * na)},
        compiler_params=pltpu.CompilerParams(has_side_effects=pltpu.SideEffectType.DATAFLOW_SIDE_EFFECTING),
    )(*args, send_sems, recv_sems, local_sems, after)
    return outs[na:]


def _sum4_into_half(r, rt, c_arr, part, fbuf):
    _, half, w = r.shape
    nb = half // EW_ROWS
    b0 = G_PARTS[part][0] // EW_ROWS

    def body(c_ref, r_ref, t_ref, *rest):
        o_ref = rest[-1]
        o_ref[...] = ((r_ref[0].astype(F32) + r_ref[1].astype(F32)) + r_ref[2].astype(F32)) + r_ref[3].astype(F32)

        @pl.when(pl.program_id(0) == nb - 1)
        def _():
            o_ref[EW_ROWS - MISC_ROWS:, :] = ((t_ref[0] + t_ref[1]) + t_ref[2]) + t_ref[3]

    in_specs = [pl.BlockSpec((4, EW_ROWS, w), lambda i, c: (0, i, 0)),
                pl.BlockSpec((4, MISC_ROWS, w), lambda i, c: (0, 0, 0))]
    args = [c_arr, r, rt]
    aliases = {}
    if fbuf is not None:
        in_specs.append(pl.BlockSpec(memory_space=pl.ANY))
        args.append(fbuf)
        aliases = {3: 0}
    grid_spec = pltpu.PrefetchScalarGridSpec(
        num_scalar_prefetch=1, grid=(nb,), in_specs=in_specs,
        out_specs=pl.BlockSpec((EW_ROWS, w), lambda i, c: (b0 + c[0] * nb + i, 0)))
    return pl.pallas_call(
        body, name="grad_sum4", grid_spec=grid_spec, out_shape=jax.ShapeDtypeStruct((G_ROWS, w), F32),
        input_output_aliases=aliases, compiler_params=_cparams(("arbitrary",)),
    )(*args)


C_GATHER_ROWS = 512


def _c_allgather_halves(f):
    w = f.shape[1]
    chunks = []
    for part, (row0, nrows) in enumerate(G_PARTS):
        chunks += [(part, r) for r in range(0, nrows // 2, C_GATHER_ROWS)]
    nch = len(chunks)

    def body(f_ref, out_ref, send_sems, recv_sems):
        x, y, c = lax.axis_index("x"), lax.axis_index("y"), lax.axis_index("c")

        def rows(q, owner):
            part, r = chunks[q]
            row0, nrows = G_PARTS[part]
            return pl.ds(row0 + owner * (nrows // 2) + r, C_GATHER_ROWS)

        sends = []
        for q in range(nch):
            cp = _remote(f_ref.at[rows(q, c), :], out_ref.at[rows(q, c), :], send_sems.at[q], recv_sems.at[q],
                         (x, y, 1 - c))
            cp.start()
            sends.append(cp)
        for q in range(nch):
            _remote(f_ref.at[rows(q, 1 - c), :], out_ref.at[rows(q, 1 - c), :], send_sems.at[q], recv_sems.at[q],
                    (x, y, 1 - c)).wait_recv()
        for cp in sends:
            cp.wait_send()

    return pl.pallas_call(
        body, name="grad_c_allgather", in_specs=[_hbm()], out_specs=_hbm(), input_output_aliases={0: 0},
        out_shape=jax.ShapeDtypeStruct((G_ROWS, w), f.dtype),
        scratch_shapes=[pltpu.SemaphoreType.DMA((nch,)), pltpu.SemaphoreType.DMA((nch,))],
    )(f)


def _adamw(w, m, v, g, g_rows=None):
    shape = w.shape
    cols = shape[-1]
    rows = int(math.prod(shape)) // cols
    tr = 256 if rows % 256 == 0 else rows
    from_flat = g_rows is not None
    c1 = 1.0 / (1.0 - ADAM_B1 ** ADAM_STEP)
    c2 = 1.0 / (1.0 - ADAM_B2 ** ADAM_STEP)

    def body(w_ref, m_ref, v_ref, g_ref, *outs):
        gg = g_ref[...]
        nm = ADAM_B1 * m_ref[...] + (1.0 - ADAM_B1) * gg
        nv = ADAM_B2 * v_ref[...] + (1.0 - ADAM_B2) * (gg * gg)
        if from_flat:
            outs[0][...] = gg
        d_ref, nm_ref, nv_ref = outs[-3:]
        nm_ref[...] = nm
        nv_ref[...] = nv
        d_ref[...] = -ADAM_LR * ((nm * c1) / (jnp.sqrt(nv * c2) + ADAM_EPS) + ADAM_WD * w_ref[...])

    spec = pl.BlockSpec((tr, cols), lambda i: (i, 0))
    if from_flat:
        nbl = rows // DEPTH // tr
        assert cols == FLAT and all(r % tr == 0 for r in g_rows) and len(g_rows) == DEPTH == 2
        b0, b1 = g_rows[0] // tr, g_rows[1] // tr
        g_spec = pl.BlockSpec((tr, cols), lambda i: (jnp.where(i < nbl, b0 + i, b1 + i - nbl), 0))
        g_arg = g
    else:
        g_spec = spec
        g_arg = g.reshape(rows, cols)
    n_out = 4 if from_flat else 3
    sds = jax.ShapeDtypeStruct((rows, cols), F32)
    outs = pl.pallas_call(
        body, name="adamw", grid=(rows // tr,), in_specs=[spec, spec, spec, g_spec], out_specs=[spec] * n_out,
        out_shape=[sds] * n_out, compiler_params=_cparams(("arbitrary",)),
    )(w.reshape(rows, cols), m.reshape(rows, cols), v.reshape(rows, cols), g_arg)
    outs = [o.reshape(shape) for o in outs]
    return outs if from_flat else [g] + outs


SMALL_SHARDED = (("s5_glu_w", (2, 64, 256), 1), ("ssd_conv_w", (2, 4, 256), 2), ("rg_conv_w", (2, 4, 64), 2))
REPLICATED = (
    ("ssd_conv_b", (2, 1024)), ("ssd_dt_bias", (2, 8)), ("ssd_a_log", (2, 8)), ("ssd_d", (2, 8)),
    ("ssd_norm_w", (2, 512)), ("s5_lam_re", (2, 16, 64)), ("s5_lam_im", (2, 16, 64)), ("s5_log_step", (2, 16)),
    ("s5_b_re", (2, 16, 64, 16)), ("s5_b_im", (2, 16, 64, 16)), ("s5_c_re", (2, 16, 16, 64)),
    ("s5_c_im", (2, 16, 16, 64)), ("s5_d", (2, 256)), ("s5_glu_b", (2, 256)), ("rg_conv_b", (2, 256)),
    ("rg_wa", (2, 4, 64, 64)), ("rg_ba", (2, 4, 64)), ("rg_wx", (2, 4, 64, 64)), ("rg_bx", (2, 4, 64)),
    ("rg_lambda", (2, 256)), ("ln1_g", (2, 1024)), ("ln1_b", (2, 1024)), ("ln2_g", (2, 1024)), ("ln2_b", (2, 1024)),
    ("ln3_g", (2, 1024)), ("ln3_b", (2, 1024)),
)
WEIGHT_ORDER = (
    "w_in", "w_out", "ssd_conv_w", "ssd_conv_b", "ssd_dt_bias", "ssd_a_log", "ssd_d", "ssd_norm_w", "s5_lam_re",
    "s5_lam_im", "s5_log_step", "s5_b_re", "s5_b_im", "s5_c_re", "s5_c_im", "s5_d", "s5_glu_w", "s5_glu_b",
    "rg_conv_w", "rg_conv_b", "rg_wa", "rg_ba", "rg_wx", "rg_bx", "rg_lambda", "ln1_g", "ln1_b", "xa_wq", "xa_wk",
    "xa_wv", "xa_wo", "ln2_g", "ln2_b", "mlp_w1", "mlp_w2", "ln3_g", "ln3_b",
)


def _size(shape):
    return int(math.prod(shape))


def _round_up(a, b):
    return (a + b - 1) // b * b


SMALL_ELEMS = sum(_size(s) for _, s, _ in SMALL_SHARDED)
REP_ELEMS = sum(_size(s) for _, s in REPLICATED)
REP_QROWS = _round_up(-(-REP_ELEMS // (4 * FLAT)), 8)
assert SMALL_ELEMS <= MISC_REP_ROW * FLAT and MISC_REP_ROW + REP_QROWS <= MISC_ROWS


def _pack_shards(tensors, names_shapes):
    return jnp.concatenate([tensors[n].reshape(-1) for n, *_ in names_shapes])


def _unpack(flat, names_shapes):
    out, off = {}, 0
    for n, s, *_ in names_shapes:
        out[n] = flat[off:off + _size(s)].reshape(s)
        off += _size(s)
    return out


def _split_shards(full, names_shapes):
    rows = []
    for k in range(4):
        parts = []
        for n, s, ax in names_shapes:
            w = s[ax]
            parts.append(lax.slice_in_dim(full[n], k * w, (k + 1) * w, axis=ax).reshape(-1))
        rows.append(jnp.concatenate(parts))
    return jnp.stack(rows)


def _pack_cols(w):
    pad = jnp.zeros((w.shape[0], LANES - SSD_HEADS), w.dtype)
    return jnp.concatenate([w[:, O_XBC:O_XBC + 1024], w[:, O_Z:O_Z + 512], w[:, O_U:O_U + 256],
                            w[:, O_XRG:O_XRG + 256], w[:, O_GRG:O_GRG + 256], w[:, O_DT:O_DT + 8], pad], axis=1)


def _unpack_cols(w):
    return jnp.concatenate([w[:, P_Z:P_Z + 512], w[:, P_XBC:P_XBC + 1024], w[:, P_DT:P_DT + 8],
                            w[:, P_U:P_U + 256], w[:, P_XRG:P_XRG + 256], w[:, P_GRG:P_GRG + 256]], axis=1)


def _lanes(v, width):
    return jnp.pad(v, (0, width - v.shape[0])).reshape(1, width)


def _layer_params(rep, l):
    p = {}
    p["ssd_cb"] = rep["ssd_conv_b"][l].reshape(1, -1)
    p["ssd_dtb"] = _lanes(rep["ssd_dt_bias"][l], LANES)
    p["ssd_a"] = _lanes(-jnp.exp(rep["ssd_a_log"][l]), LANES)
    p["ssd_d"] = jnp.repeat(rep["ssd_d"][l], 64).reshape(1, -1)
    p["ssd_nw"] = rep["ssd_norm_w"][l].reshape(1, -1)
    s5_args = tuple(rep[n][l] for n in ("s5_lam_re", "s5_lam_im", "s5_log_step", "s5_b_re", "s5_b_im", "s5_c_re",
                                        "s5_c_im"))
    (lbr, lbi, bre, bim, cre, cim), p["s5_vjp"] = jax.vjp(_s5_prepare, *s5_args)
    p.update(s5_bre=bre, s5_bim=bim, s5_cre=cre, s5_cim=cim)
    p["s5_coef"] = _s5_scan_coef(lbr, lbi, False)
    p["s5_rcoef"] = _s5_scan_coef(lbr, lbi, True)
    p["s5_d"] = rep["s5_d"][l].reshape(1, -1)
    p["s5_gb"] = rep["s5_glu_b"][l].reshape(1, -1)
    p["rg_cb"] = rep["rg_conv_b"][l].reshape(1, -1)
    p["rg_wa"] = _block_diag(rep["rg_wa"][l])
    p["rg_wx"] = _block_diag(rep["rg_wx"][l])
    p["rg_ba"] = rep["rg_ba"][l].reshape(1, -1)
    p["rg_bx"] = rep["rg_bx"][l].reshape(1, -1)
    p["rg_nsp"] = (-RG_C * jax.nn.softplus(-rep["rg_lambda"][l])).reshape(1, -1)
    p["rg_dnsp"] = RG_C * jax.nn.sigmoid(-rep["rg_lambda"][l])
    for n in ("ln1_g", "ln1_b", "ln2_g", "ln2_b", "ln3_g", "ln3_b"):
        p[n] = rep[n][l].reshape(1, -1)
    return p


def _layer_fwd(h, mem, p, fetch):
    s = {"h0": h}
    p.update(fetch(0, h))
    proj = _mm(h, p["w_in"], name="in_proj")
    s["proj"] = proj
    y_ssd, s["ssd_yy"], s["ssd_states"] = _ssd_fwd(proj, p["ssd_cw"], p["ssd_cb"], p["ssd_dtb"], p["ssd_a"],
                                                     p["ssd_d"], p["ssd_nw"])
    y_s5, s["s5_y2"], s["s5_hre"], s["s5_him"] = _s5_fwd(proj, p["s5_bre"], p["s5_bim"], p["s5_cre"], p["s5_cim"],
                                                         p["s5_d"], p["s5_glu_w"], p["s5_gb"], p["s5_coef"])
    y_rg, s["rg_h"] = _rg_fwd(proj, p["rg_cw"], p["rg_cb"], p["rg_wa"], p["rg_ba"], p["rg_wx"], p["rg_bx"],
                              p["rg_nsp"])
    s["ys"] = [y_ssd, y_s5, y_rg]
    p.update(fetch(1, y_rg))
    h1, s["xh1"], s["rs1"] = _outproj_ln_fwd(s["ys"], h, p["w_out"], p["ln1_g"], p["ln1_b"])
    s["h1"] = h1
    p.update(fetch(2, h1))
    kb = _mm(mem, p["xa_wk"], name="mem_proj")
    vb = _mm(mem, p["xa_wv"], name="mem_proj")
    s["kb"], s["vb"] = kb, vb
    h2, s["xh2"], s["rs2"], s["attn_o"] = _attn_ln_fwd(h1, p["xa_wq"], p["xa_wo"], kb, vb, p["ln2_g"], p["ln2_b"])
    s["h2"] = h2
    p.update(fetch(3, h2))
    h3, s["xh3"], s["rs3"], s["mlp_hdn"] = _mlp_ln_fwd(h2, p["mlp_w1"], p["mlp_w2"], p["ln3_g"], p["ln3_b"])
    return h3, s


def _layer_bwd(dh3, mem, p, s, l, gbuf, after_mlp=None):
    g = {}
    dr3, du, dh2, g["ln3_g"], g["ln3_b"] = _mlp_ln_bwd(dh3, s["xh3"], s["rs3"], p["ln3_g"], s["mlp_hdn"],
                                                        p["mlp_w1"], p["mlp_w2"])
    gbuf = _wgrad_flat(s["h2"], du, gbuf, mode="colblk", row_off=_grad_row("mlp_w1", l), name="wgrad_mlp_w1")
    gbuf = _wgrad_flat(s["mlp_hdn"], dr3, gbuf, mode="rowblk", row_off=_grad_row("mlp_w2", l), name="wgrad_mlp_w2")
    ln2_g = p["ln2_g"] if after_mlp is None else p["ln2_g"] + after_mlp(gbuf)[0:1, 0:1]
    dr2, dq, dh1, dkb, dvb, g["ln2_g"], g["ln2_b"] = _attn_ln_bwd(dh2, s["xh2"], s["rs2"], ln2_g, s["h1"],
                                                                   p["xa_wq"], p["xa_wo"], s["kb"], s["vb"])
    for n, a_op, g_op in (("xa_wo", s["attn_o"], dr2), ("xa_wq", s["h1"], dq), ("xa_wk", mem, dkb),
                          ("xa_wv", mem, dvb)):
        gbuf = _wgrad_flat(a_op, g_op, gbuf, mode="rows4", row_off=_grad_row(n, l), name="wgrad_" + n)
    dr1, dres, dycat, g["ln1_g"], g["ln1_b"] = _outproj_ln_bwd(dh1, s["xh1"], s["rs1"], p["ln1_g"], p["w_out"])
    gbuf = _wgrad_flat(s["ys"], dr1, gbuf, mode="rows4", row_off=_grad_row("w_out", l), name="wgrad_w_out")
    proj = s["proj"]
    (dxbc, dz, ddt, dcw, dcb, ddtb, da_neg, dd_l, dnw) = _ssd_bwd(
        dycat, proj, s["ssd_yy"], s["ssd_states"], p["ssd_cw"], p["ssd_cb"], p["ssd_dtb"], p["ssd_a"], p["ssd_d"],
        p["ssd_nw"])
    g["ssd_conv_w"] = dcw[0:4]
    g["ssd_conv_b"] = dcb[0]
    g["ssd_dt_bias"] = ddtb[0, :SSD_HEADS]
    g["ssd_a_log"] = da_neg[0, :SSD_HEADS] * p["ssd_a"][0, :SSD_HEADS]
    g["ssd_d"] = dd_l.reshape(SSD_HEADS, 64).sum(axis=1)
    g["ssd_norm_w"] = dnw[0]
    (du_s5, dbre, dbim, dcre, dcim, dlam, dd5, dgw, dgb) = _s5_bwd(
        dycat, proj, s["s5_y2"], s["s5_hre"], s["s5_him"], p["s5_bre"], p["s5_bim"], p["s5_cre"], p["s5_cim"],
        p["s5_d"], p["s5_glu_w"], p["s5_gb"], p["s5_rcoef"])
    dl = dlam.sum(axis=1)
    s5g = p["s5_vjp"]((dl[0], dl[1], dbre, dbim, dcre, dcim))
    for n, v in zip(("s5_lam_re", "s5_lam_im", "s5_log_step", "s5_b_re", "s5_b_im", "s5_c_re", "s5_c_im"), s5g):
        g[n] = v
    g["s5_d"] = dd5[0]
    g["s5_glu_w"] = dgw
    g["s5_glu_b"] = dgb[0]
    (dxrg, dgrg, drcw, drcb, dwa, dba, dwx, dbx, dnsp) = _rg_bwd(
        dycat, proj, s["rg_h"], p["rg_cw"], p["rg_cb"], p["rg_wa"], p["rg_ba"], p["rg_wx"], p["rg_bx"], p["rg_nsp"])
    g["rg_conv_w"] = drcw[0:4]
    g["rg_conv_b"] = drcb[0]
    g["rg_wa"] = _block_diag_extract(dwa, RG_BLOCKS)
    g["rg_wx"] = _block_diag_extract(dwx, RG_BLOCKS)
    g["rg_ba"] = dba.reshape(RG_BLOCKS, RG_BLOCK_DIM)
    g["rg_bx"] = dbx.reshape(RG_BLOCKS, RG_BLOCK_DIM)
    g["rg_lambda"] = dnsp[0] * p["rg_dnsp"]
    dproj = [dxbc, dz, du_s5, dxrg, dgrg, ddt]
    g["w_in"] = _unpack_cols(_wgrad_in(s["h0"], dproj))
    dh0 = _in_proj_bwd(dproj, p["w_in"], dres)
    for n in ("ln1_g", "ln1_b", "ln2_g", "ln2_b", "ln3_g", "ln3_b"):
        g[n] = g[n][0]
    return dh0, g, gbuf


def _local_step(h, memf, target, rep, fetch):
    params, saved = [], []
    for l in range(DEPTH):
        p = _layer_params(rep, l)
        params.append(p)
        h, s = _layer_fwd(h, memf, p, functools.partial(fetch, l))
        saved.append(s)
    loss11, dh = _loss_fwd_bwd(h, target)
    grads = [None] * DEPTH
    gbuf = None
    c_arr = lax.axis_index("c").astype(jnp.int32).reshape(1)
    handles = {}

    def start_part(buf, part):
        handles[part], token = _xy_start(_chip_sums(buf, c_arr, part), name="grad_xy_start_%d" % part)
        return token

    for l in reversed(range(DEPTH)):
        hook = functools.partial(start_part, part=1) if l == 0 else None
        dh, grads[l], gbuf = _layer_bwd(dh, memf, params[l], saved[l], l, gbuf, hook)
        if l == DEPTH - 1:
            gbuf = lax.dynamic_update_slice(
                gbuf, _w_in_block(grads[l]["w_in"], jnp.zeros((4, MISC_ROWS, FLAT), F32)),
                (0, _grad_row("w_in", l), 0))
            params[0]["ln3_g"] = params[0]["ln3_g"] + start_part(gbuf, 0)[0:1, 0:1]
    gsmall = {n: jnp.stack([grads[l][n] for l in range(DEPTH)]) for n in grads[0] if n != "w_in"}
    return loss11, dh, gsmall, grads[0]["w_in"], gbuf, handles, c_arr


def _w_in_block(gw, tail):
    gw = jnp.pad(gw.reshape(D_MODEL, 4, W_IN_SHARD), ((0, 0), (0, 0), (0, W_IN_PAD - W_IN_SHARD)))
    return jnp.concatenate([jnp.transpose(gw, (1, 0, 2)).reshape(4, W_IN_PAD, FLAT), tail], axis=1)


def _chip_sums(gbuf, c_arr, part):
    return list(_add_own_half(gbuf, _c_exchange(gbuf, part), c_arr, part))


def kernel(x, mem, w_in, w_out, ssd_conv_w, ssd_conv_b, ssd_dt_bias, ssd_a_log, ssd_d, ssd_norm_w, s5_lam_re, s5_lam_im, s5_log_step, s5_b_re, s5_b_im, s5_c_re, s5_c_im, s5_d, s5_glu_w, s5_glu_b, rg_conv_w, rg_conv_b, rg_wa, rg_ba, rg_wx, rg_bx, rg_lambda, ln1_g, ln1_b, xa_wq, xa_wk, xa_wv, xa_wo, ln2_g, ln2_b, mlp_w1, mlp_w2, ln3_g, ln3_b, loss_target, m_w_in, m_w_out, m_ssd_conv_w, m_ssd_conv_b, m_ssd_dt_bias, m_ssd_a_log, m_ssd_d, m_ssd_norm_w, m_s5_lam_re, m_s5_lam_im, m_s5_log_step, m_s5_b_re, m_s5_b_im, m_s5_c_re, m_s5_c_im, m_s5_d, m_s5_glu_w, m_s5_glu_b, m_rg_conv_w, m_rg_conv_b, m_rg_wa, m_rg_ba, m_rg_wx, m_rg_bx, m_rg_lambda, m_ln1_g, m_ln1_b, m_xa_wq, m_xa_wk, m_xa_wv, m_xa_wo, m_ln2_g, m_ln2_b, m_mlp_w1, m_mlp_w2, m_ln3_g, m_ln3_b, v_w_in, v_w_out, v_ssd_conv_w, v_ssd_conv_b, v_ssd_dt_bias, v_ssd_a_log, v_ssd_d, v_ssd_norm_w, v_s5_lam_re, v_s5_lam_im, v_s5_log_step, v_s5_b_re, v_s5_b_im, v_s5_c_re, v_s5_c_im, v_s5_d, v_s5_glu_w, v_s5_glu_b, v_rg_conv_w, v_rg_conv_b, v_rg_wa, v_rg_ba, v_rg_wx, v_rg_bx, v_rg_lambda, v_ln1_g, v_ln1_b, v_xa_wq, v_xa_wk, v_xa_wv, v_xa_wo, v_ln2_g, v_ln2_b, v_mlp_w1, v_mlp_w2, v_ln3_g, v_ln3_b):
    args = dict(locals())
    weights = {n: args[n] for n in WEIGHT_ORDER}
    mom_m = {n: args["m_" + n] for n in WEIGHT_ORDER}
    mom_v = {n: args["v_" + n] for n in WEIGHT_ORDER}

    shards = []
    for l in range(DEPTH):
        for n, shp, ax in LAYER_GATHERED:
            w = weights[n][l]
            if w.shape[1] != shp[1]:
                w = jnp.pad(w, ((0, 0), (0, shp[1] - w.shape[1])))
            if n not in ("ssd_conv_w", "rg_conv_w"):
                w = w.astype(MXU_DTYPE)
            shards.append(w)
    handle = _gather_start(shards)

    def unpad(arr, padded, width):
        return jnp.concatenate([arr[:, padded * k:padded * k + width] for k in range(4)], axis=1)

    def fetch(l, grp, after):
        ts = [l * N_GATHERED + j for j in WAIT_GROUPS[grp]]
        _, landed = _gather_wait(handle, ts, after, name="weights_gather_wait_%d_%d" % (l, grp))
        out = {}
        for t, arr in zip(ts, landed):
            n = LAYER_GATHERED[t % N_GATHERED][0]
            if n == "w_in":
                arr = _pack_cols(unpad(arr, W_IN_PAD, W_IN_SHARD))
            elif n == "rg_conv_w":
                arr = unpad(arr, LANES, RG_CONV_SHARD)
            out[{"ssd_conv_w": "ssd_cw", "rg_conv_w": "rg_cw"}.get(n, n)] = arr
        return out

    rep = {n: weights[n] for n, _ in REPLICATED}

    loss11, dx, gsmall, gw_in0, gbuf, handles, c_arr = _local_step(x[0], mem[0], loss_target[0], rep, fetch)
    grad_x = dx[None]
    loss = lax.psum(loss11[0, 0], ("x", "y", "c"))

    small_q = _split_shards(gsmall, SMALL_SHARDED)
    rep_q = jnp.pad(_pack_shards(gsmall, REPLICATED), (0, 4 * REP_QROWS * FLAT - REP_ELEMS)).reshape(4, -1)
    misc = jnp.concatenate(
        [jnp.pad(small_q, ((0, 0), (0, MISC_REP_ROW * FLAT - SMALL_ELEMS))), rep_q,
         jnp.zeros((4, (MISC_ROWS - MISC_REP_ROW - REP_QROWS) * FLAT), F32)], axis=1).reshape(4, MISC_ROWS, FLAT)
    gbuf = lax.dynamic_update_slice(gbuf, _w_in_block(gw_in0, misc), (0, _grad_row("w_in", 0), 0))
    got = {2: _xy_exchange(_chip_sums(gbuf, c_arr, 2))}
    fbuf = None
    for part in range(len(G_PARTS)):
        if part in handles:
            got[part] = _xy_wait(handles[part], dx, name="grad_xy_wait_%d" % part)
        fbuf = _sum4_into_half(got[part][0], got[part][1], c_arr, part, fbuf)
    reduced = _c_allgather_halves(fbuf)
    misc_red = reduced[ROW_MISC:]
    rep_all = _xy_allgather(misc_red[MISC_REP_ROW:MISC_REP_ROW + REP_QROWS], name="small_grads_allgather")
    g_red = {**_unpack(misc_red[:MISC_REP_ROW].reshape(-1), SMALL_SHARDED),
             **_unpack(rep_all.reshape(-1), REPLICATED)}
    g_red["w_in"] = jnp.stack([
        reduced[_grad_row("w_in", l):_grad_row("w_in", l) + W_IN_PAD].reshape(D_MODEL, W_IN_PAD)[:, :W_IN_SHARD]
        for l in range(DEPTH)])

    res = {}
    for n in WEIGHT_ORDER:
        if n in ("mlp_w1", "mlp_w2", "w_out", "xa_wq", "xa_wk", "xa_wv", "xa_wo"):
            res[n] = _adamw(weights[n], mom_m[n], mom_v[n], reduced, g_rows=[_grad_row(n, l) for l in range(DEPTH)])
        else:
            res[n] = _adamw(weights[n], mom_m[n], mom_v[n], g_red[n])
    return (loss, grad_x, *[res[n][0] for n in WEIGHT_ORDER], *[res[n][1] for n in WEIGHT_ORDER],
            *[res[n][2] for n in WEIGHT_ORDER], *[res[n][3] for n in WEIGHT_ORDER])
```

```python
import functools
import math

import jax
import jax.numpy as jnp
from jax import lax
from jax.experimental import pallas as pl
from jax.experimental.pallas import tpu as pltpu

F32 = jnp.float32
MXU_DTYPE = jnp.bfloat16

D_MODEL = 1024
DEPTH = 2
MEM_LEN = 256
SSD_WIDTH = 512
SSD_HEADS = 8
SSD_STATE = 128
SSD_CHUNK = 128
SSD_XBC = 1024
S5_WIDTH = 256
S5_GROUPS = 16
S5_GROUP_CH = 16
S5_STATE = 64
S5_NSTATE = S5_GROUPS * S5_STATE
RG_WIDTH = 256
RG_BLOCKS = 4
RG_BLOCK_DIM = 64
RG_C = 8.0
XA_HEADS = 4
XA_HEAD_DIM = 256
D_FF = 4096
D_IN = 2312
ALPHA = (2.0 * DEPTH) ** 0.25
LN_EPS = 1e-5
ADAM_LR = 0.001
ADAM_B1 = 0.9
ADAM_B2 = 0.999
ADAM_EPS = 1e-08
ADAM_WD = 0.01
ADAM_STEP = 10

P_XBC, P_Z, P_U, P_XRG, P_GRG, P_DT = 0, 1024, 1536, 1792, 2048, 2304
D_PACK = 2432
O_Z, O_XBC, O_DT, O_U, O_XRG, O_GRG = 0, 512, 1536, 1544, 1800, 2056

LANES = 128
SUBLANES = 8
VMEM_LIMIT = 52 * 1024 * 1024
TM = 512
SSD_TM = 256
SCAN_TM = 512
FLAT = 1024

MESH = pl.DeviceIdType.MESH


def _cparams(sem):
    return pltpu.CompilerParams(dimension_semantics=sem, vmem_limit_bytes=VMEM_LIMIT)


def _dot(a, b):
    return jnp.dot(a.astype(MXU_DTYPE), b.astype(MXU_DTYPE), preferred_element_type=F32)


def _dot_nt(a, b):
    return lax.dot_general(a.astype(MXU_DTYPE), b.astype(MXU_DTYPE), (((1,), (1,)), ((), ())),
                           preferred_element_type=F32)


def _dot_tn(a, b):
    return lax.dot_general(a.astype(MXU_DTYPE), b.astype(MXU_DTYPE), (((0,), (0,)), ((), ())),
                           preferred_element_type=F32)


def _dot_f32(a, b):
    return jnp.dot(a, b, precision=lax.Precision.HIGHEST, preferred_element_type=F32)


def _dot_f32_tn(a, b):
    return lax.dot_general(a, b, (((0,), (0,)), ((), ())), precision=lax.Precision.HIGHEST,
                           preferred_element_type=F32)


def _sigmoid(x):
    return 1.0 / (1.0 + jnp.exp(-x))


def _softplus(x):
    return jnp.maximum(x, 0.0) + jnp.log(1.0 + jnp.exp(-jnp.abs(x)))


_GELU_K = math.sqrt(2.0 / math.pi)


def _gelu(x):
    return 0.5 * x * (1.0 + jnp.tanh(_GELU_K * (x + 0.044715 * x * x * x)))


def _gelu_grad(x):
    t = jnp.tanh(_GELU_K * (x + 0.044715 * x * x * x))
    return 0.5 * (1.0 + t) + 0.5 * x * (1.0 - t * t) * _GELU_K * (1.0 + 3.0 * 0.044715 * x * x)


def _expm1(x):
    small = x * (1.0 + x * (0.5 + x * (1.0 / 6.0 + x * (1.0 / 24.0))))
    return jnp.where(jnp.abs(x) < 0.05, small, jnp.exp(x) - 1.0)


def _sum0(x):
    return jnp.sum(x, axis=0, keepdims=True)


def _ln_fwd(r, g, b):
    mu = jnp.mean(r, axis=-1, keepdims=True)
    xc = r - mu
    var = jnp.mean(xc * xc, axis=-1, keepdims=True)
    rstd = lax.rsqrt(var + LN_EPS)
    xhat = xc * rstd
    return xhat * g + b, xhat, rstd


def _ln_bwd(dout, xhat, rstd, g):
    dxh = dout * g
    m1 = jnp.mean(dxh, axis=-1, keepdims=True)
    m2 = jnp.mean(dxh * xhat, axis=-1, keepdims=True)
    return rstd * (dxh - m1 - xhat * m2)


def _rows(tm, n, col=0):
    return pl.BlockSpec((tm, n), lambda i: (i, col))


def _const(shape):
    nd = len(shape)
    return pl.BlockSpec(shape, lambda i: (0,) * nd)


def _mm(a, w, *, name):
    t, k = a.shape
    n = w.shape[1]
    tm = min(TM, t)

    def body(a_ref, w_ref, o_ref):
        o_ref[...] = _dot(a_ref[...], w_ref[...])

    return pl.pallas_call(
        body, name=name, grid=(t // tm,), in_specs=[_rows(tm, k), _const(w.shape)], out_specs=_rows(tm, n),
        out_shape=jax.ShapeDtypeStruct((t, n), F32), compiler_params=_cparams(("arbitrary",)),
    )(a, w)


DPROJ_PIECES = ((P_XBC, 1024), (P_Z, 512), (P_U, 256), (P_XRG, 256), (P_GRG, 256), (P_DT, LANES))


def _in_proj_bwd(pieces, w, dres):
    t = dres.shape[0]
    npc = len(pieces)

    def body(*refs):
        w_ref, r_ref, o_ref = refs[npc:]
        acc = r_ref[...]
        for p_ref, (off, k) in zip(refs[:npc], DPROJ_PIECES):
            acc = acc + _dot_nt(p_ref[...], w_ref[:, off:off + k])
        o_ref[...] = acc

    return pl.pallas_call(
        body, name="in_proj_bwd", grid=(t // TM,),
        in_specs=[_rows(TM, k) for _, k in DPROJ_PIECES] + [_const(w.shape), _rows(TM, D_MODEL)],
        out_specs=_rows(TM, D_MODEL), out_shape=jax.ShapeDtypeStruct((t, D_MODEL), F32),
        compiler_params=_cparams(("arbitrary",)),
    )(*pieces, w, dres)


def _wgrad_in(h0, pieces):
    t = h0.shape[0]
    npc = len(pieces)

    def body(*refs):
        h_ref, o_ref = refs[npc], refs[npc + 1]
        @pl.when(pl.program_id(0) == 0)
        def _():
            o_ref[...] = jnp.zeros_like(o_ref)

        hb = h_ref[...].astype(MXU_DTYPE)
        for p_ref, (off, k) in zip(refs[:npc], DPROJ_PIECES):
            o_ref[:, off:off + k] += _dot_tn(hb, p_ref[...])

    return pl.pallas_call(
        body, name="wgrad_in", grid=(t // TM,),
        in_specs=[_rows(TM, k) for _, k in DPROJ_PIECES] + [_rows(TM, D_MODEL)],
        out_specs=_const((D_MODEL, D_PACK)), out_shape=jax.ShapeDtypeStruct((D_MODEL, D_PACK), F32),
        compiler_params=_cparams(("arbitrary",)),
    )(*pieces, h0)


G_ROWS = 8192
G_PARTS = ((0, 4096), (4096, 2048), (6144, 2048))
W_IN_SHARD = 578
W_IN_PAD = 640
MISC_ROWS = 128
MISC_REP_ROW = 40
ROW_MISC = G_ROWS - MISC_ROWS
W_IN_BLOCK_ROWS = W_IN_PAD + MISC_ROWS


def _grad_row(name, l):
    base = 0 if l == 1 else 4096
    mid = base + 2048 if l == 1 else 6144
    return {"mlp_w1": base, "mlp_w2": base + 1024, "w_out": mid, "xa_wq": mid + 256, "xa_wk": mid + 512,
            "xa_wv": mid + 768, "xa_wo": mid + 1024, "w_in": mid + 1280}[name]


def _wgrad_flat(a, g, buf, *, mode, row_off, name):
    pieces = list(a) if isinstance(a, (list, tuple)) else [a]
    t = g.shape[0]
    tt = min(1024, t)
    ns = t // tt
    blk = D_MODEL

    def accumulate(o_ref, parts, s):
        @pl.when(s == 0)
        def _():
            o_ref[...] = jnp.zeros_like(o_ref)

        for q, v in parts:
            o_ref[q] += v

    if mode == "rows4":
        grid = (ns,)
        in_specs = [pl.BlockSpec((tt, p.shape[1]), lambda s: (s, 0)) for p in pieces]
        in_specs.append(pl.BlockSpec((tt, blk), lambda s: (s, 0)))
        out_spec = pl.BlockSpec((4, 256, FLAT), lambda s: (0, row_off // 256, 0))
        sem = ("arbitrary",)
        npc = len(pieces)

        def body(*refs):
            g_v = refs[npc][...]
            parts, q0 = [], 0
            for p_ref in refs[:npc]:
                full = _dot_tn(p_ref[...], g_v)
                nq = full.shape[0] // 256
                parts += [(q0 + q, full[q * 256:(q + 1) * 256]) for q in range(nq)]
                q0 += nq
            accumulate(refs[-1], parts, pl.program_id(0))
    else:
        grid = (2, ns)
        if mode == "rowblk":
            in_specs = [pl.BlockSpec((tt, 2 * blk), lambda q, s: (s, q)), pl.BlockSpec((tt, blk), lambda q, s: (s, 0))]
        else:
            in_specs = [pl.BlockSpec((tt, blk), lambda q, s: (s, 0)), pl.BlockSpec((tt, 2 * blk), lambda q, s: (s, q))]
        out_spec = pl.BlockSpec((2, blk, FLAT), lambda q, s: (q, row_off // blk, 0))
        sem = ("arbitrary", "arbitrary")

        def body(a_ref, g_ref, *rest):
            full = _dot_tn(a_ref[...], g_ref[...])
            if mode == "rowblk":
                parts = [(0, full[:blk]), (1, full[blk:])]
            else:
                parts = [(0, full[:, :blk]), (1, full[:, blk:])]
            accumulate(rest[-1], parts, pl.program_id(1))

    args = pieces + [g]
    aliases = {}
    if buf is not None:
        in_specs.append(pl.BlockSpec(memory_space=pl.ANY))
        args.append(buf)
        aliases = {len(args) - 1: 0}
    return pl.pallas_call(
        body, name=name, grid=grid, in_specs=in_specs, out_specs=out_spec,
        out_shape=jax.ShapeDtypeStruct((4, G_ROWS, FLAT), F32), input_output_aliases=aliases,
        compiler_params=_cparams(sem),
    )(*args)


def _outproj_ln_fwd(ys, h, w, g, b):
    t = h.shape[0]
    npc = len(ys)

    def body(*refs):
        h_ref, w_ref, g_ref, b_ref, hn_ref, xh_ref, rs_ref = refs[npc:]
        r = ALPHA * h_ref[...]
        off = 0
        for y_ref in refs[:npc]:
            k = y_ref.shape[1]
            r = r + _dot(y_ref[...], w_ref[off:off + k, :])
            off += k
        out, xhat, rstd = _ln_fwd(r, g_ref[...], b_ref[...])
        hn_ref[...] = out
        xh_ref[...] = xhat
        rs_ref[...] = rstd

    return pl.pallas_call(
        body, name="outproj_ln_fwd", grid=(t // TM,),
        in_specs=[_rows(TM, y.shape[1]) for y in ys] + [_rows(TM, D_MODEL), _const((D_MODEL, D_MODEL)),
                                                        _const((1, D_MODEL)), _const((1, D_MODEL))],
        out_specs=[_rows(TM, D_MODEL), _rows(TM, D_MODEL), _rows(TM, 1)],
        out_shape=[jax.ShapeDtypeStruct((t, D_MODEL), F32), jax.ShapeDtypeStruct((t, D_MODEL), F32),
                   jax.ShapeDtypeStruct((t, 1), F32)],
        compiler_params=_cparams(("arbitrary",)),
    )(*ys, h, w, g, b)


def _attn_probs(q, kb, hh):
    sl = slice(hh * XA_HEAD_DIM, (hh + 1) * XA_HEAD_DIM)
    s = _dot_nt(q[:, sl], kb[:, sl]) * (1.0 / math.sqrt(XA_HEAD_DIM))
    m = jnp.max(s, axis=-1, keepdims=True)
    e = jnp.exp(s - m)
    return e / jnp.sum(e, axis=-1, keepdims=True)


def _attn_ln_fwd(h1, wq, wo, kb, vb, g, b):
    t = h1.shape[0]

    def body(h_ref, wq_ref, wo_ref, k_ref, v_ref, g_ref, b_ref, hn_ref, xh_ref, rs_ref, o_ref):
        h = h_ref[...]
        q = _dot(h, wq_ref[...])
        kb_ = k_ref[...]
        vb_ = v_ref[...]
        for hh in range(XA_HEADS):
            sl = slice(hh * XA_HEAD_DIM, (hh + 1) * XA_HEAD_DIM)
            p = _attn_probs(q, kb_, hh)
            o_ref[:, sl] = _dot(p, vb_[:, sl]).astype(o_ref.dtype)
        r = ALPHA * h + _dot(o_ref[...], wo_ref[...])
        out, xhat, rstd = _ln_fwd(r, g_ref[...], b_ref[...])
        hn_ref[...] = out
        xh_ref[...] = xhat
        rs_ref[...] = rstd

    return pl.pallas_call(
        body, name="attn_ln_fwd", grid=(t // TM,),
        in_specs=[_rows(TM, D_MODEL), _const((D_MODEL, D_MODEL)), _const((D_MODEL, D_MODEL)),
                  _const((MEM_LEN, D_MODEL)), _const((MEM_LEN, D_MODEL)), _const((1, D_MODEL)), _const((1, D_MODEL))],
        out_specs=[_rows(TM, D_MODEL), _rows(TM, D_MODEL), _rows(TM, 1), _rows(TM, D_MODEL)],
        out_shape=[jax.ShapeDtypeStruct((t, D_MODEL), F32), jax.ShapeDtypeStruct((t, D_MODEL), F32),
                   jax.ShapeDtypeStruct((t, 1), F32), jax.ShapeDtypeStruct((t, D_MODEL), MXU_DTYPE)],
        compiler_params=_cparams(("arbitrary",)),
    )(h1, wq, wo, kb, vb, g, b)


def _attn_ln_bwd(dh2, xhat, rstd, g, h1, wq, wo, kb, vb):
    t = h1.shape[0]

    def body(dh_ref, xh_ref, rs_ref, g_ref, h_ref, wq_ref, wo_ref, k_ref, v_ref,
             dr_ref, dq_ref, dh1_ref, dk_ref, dv_ref, dg_ref, db_ref):
        i = pl.program_id(0)

        @pl.when(i == 0)
        def _():
            dk_ref[...] = jnp.zeros_like(dk_ref)
            dv_ref[...] = jnp.zeros_like(dv_ref)
            dg_ref[...] = jnp.zeros_like(dg_ref)
            db_ref[...] = jnp.zeros_like(db_ref)

        dout = dh_ref[...]
        xh = xh_ref[...]
        dg_ref[...] += _sum0(dout * xh)
        db_ref[...] += _sum0(dout)
        dr = _ln_bwd(dout, xh, rs_ref[...], g_ref[...])
        dr_ref[...] = dr.astype(dr_ref.dtype)
        do = _dot_nt(dr, wo_ref[...])
        h = h_ref[...]
        q = _dot(h, wq_ref[...])
        kb_ = k_ref[...]
        vb_ = v_ref[...]
        scale = 1.0 / math.sqrt(XA_HEAD_DIM)
        for hh in range(XA_HEADS):
            sl = slice(hh * XA_HEAD_DIM, (hh + 1) * XA_HEAD_DIM)
            p = _attn_probs(q, kb_, hh)
            do_h = do[:, sl]
            dp = _dot_nt(do_h, vb_[:, sl])
            ds = p * (dp - jnp.sum(dp * p, axis=-1, keepdims=True)) * scale
            dq_ref[:, sl] = _dot(ds, kb_[:, sl]).astype(dq_ref.dtype)
            dk_ref[:, sl] += _dot_tn(ds, q[:, sl])
            dv_ref[:, sl] += _dot_tn(p, do_h)
        dh1_ref[...] = ALPHA * dr + _dot_nt(dq_ref[...], wq_ref[...])

    return pl.pallas_call(
        body, name="attn_ln_bwd", grid=(t // TM,),
        in_specs=[_rows(TM, D_MODEL), _rows(TM, D_MODEL), _rows(TM, 1), _const((1, D_MODEL)), _rows(TM, D_MODEL),
                  _const((D_MODEL, D_MODEL)), _const((D_MODEL, D_MODEL)), _const((MEM_LEN, D_MODEL)),
                  _const((MEM_LEN, D_MODEL))],
        out_specs=[_rows(TM, D_MODEL), _rows(TM, D_MODEL), _rows(TM, D_MODEL), _const((MEM_LEN, D_MODEL)),
                   _const((MEM_LEN, D_MODEL)), _const((1, D_MODEL)), _const((1, D_MODEL))],
        out_shape=[jax.ShapeDtypeStruct((t, D_MODEL), MXU_DTYPE), jax.ShapeDtypeStruct((t, D_MODEL), MXU_DTYPE),
                   jax.ShapeDtypeStruct((t, D_MODEL), F32), jax.ShapeDtypeStruct((MEM_LEN, D_MODEL), F32),
                   jax.ShapeDtypeStruct((MEM_LEN, D_MODEL), F32), jax.ShapeDtypeStruct((1, D_MODEL), F32),
                   jax.ShapeDtypeStruct((1, D_MODEL), F32)],
        compiler_params=_cparams(("arbitrary",)),
    )(dh2, xhat, rstd, g, h1, wq, wo, kb, vb)


FF_CHUNK = 1024
N_FF = D_FF // FF_CHUNK


def _load_resident(pairs, sems):
    copies = [pltpu.make_async_copy(src, dst, sems.at[k]) for k, (src, dst) in enumerate(pairs)]
    for cp in copies:
        cp.start()
    for cp in copies:
        cp.wait()


def _mlp_ln_fwd(h2, w1, w2, g, b):
    t = h2.shape[0]

    def body(h_ref, w1_hbm, w2_hbm, g_ref, b_ref, hn_ref, xh_ref, rs_ref, hd_ref, w1_v, w2_v, acc_ref, sems):
        @pl.when(pl.program_id(0) == 0)
        def _():
            _load_resident([(w1_hbm, w1_v), (w2_hbm, w2_v)], sems)

        h = h_ref[...]
        hb = h.astype(MXU_DTYPE)
        acc_ref[...] = ALPHA * h
        for j in range(N_FF):
            sl = slice(j * FF_CHUNK, (j + 1) * FF_CHUNK)
            u = _dot(hb, w1_v[:, sl])
            hd = jnp.square(jnp.maximum(u, 0.0)).astype(MXU_DTYPE)
            hd_ref[:, sl] = hd
            acc_ref[...] += _dot(hd, w2_v[sl, :])
        out, xhat, rstd = _ln_fwd(acc_ref[...], g_ref[...], b_ref[...])
        hn_ref[...] = out
        xh_ref[...] = xhat
        rs_ref[...] = rstd

    return pl.pallas_call(
        body, name="mlp_ln_fwd", grid=(t // TM,),
        in_specs=[_rows(TM, D_MODEL), _hbm(), _hbm(), _const((1, D_MODEL)), _const((1, D_MODEL))],
        out_specs=[_rows(TM, D_MODEL), _rows(TM, D_MODEL), _rows(TM, 1), _rows(TM, D_FF)],
        out_shape=[jax.ShapeDtypeStruct((t, D_MODEL), F32), jax.ShapeDtypeStruct((t, D_MODEL), F32),
                   jax.ShapeDtypeStruct((t, 1), F32), jax.ShapeDtypeStruct((t, D_FF), MXU_DTYPE)],
        scratch_shapes=[pltpu.VMEM((D_MODEL, D_FF), MXU_DTYPE), pltpu.VMEM((D_FF, D_MODEL), MXU_DTYPE),
                        pltpu.VMEM((TM, D_MODEL), F32), pltpu.SemaphoreType.DMA((2,))],
        compiler_params=_cparams(("arbitrary",)),
    )(h2, w1, w2, g, b)


def _mlp_ln_bwd(dh3, xhat, rstd, g, hdn, w1, w2):
    t = dh3.shape[0]

    def body(dh_ref, xh_ref, rs_ref, g_ref, hd_ref, w1_hbm, w2_hbm,
             dr_ref, du_ref, dh2_ref, dg_ref, db_ref, w1_v, w2_v, acc_ref, sems):
        @pl.when(pl.program_id(0) == 0)
        def _():
            _load_resident([(w1_hbm, w1_v), (w2_hbm, w2_v)], sems)
            dg_ref[...] = jnp.zeros_like(dg_ref)
            db_ref[...] = jnp.zeros_like(db_ref)

        dout = dh_ref[...]
        xh = xh_ref[...]
        dg_ref[...] += _sum0(dout * xh)
        db_ref[...] += _sum0(dout)
        dr = _ln_bwd(dout, xh, rs_ref[...], g_ref[...])
        drb = dr.astype(MXU_DTYPE)
        dr_ref[...] = drb
        acc_ref[...] = ALPHA * dr
        for j in range(N_FF):
            sl = slice(j * FF_CHUNK, (j + 1) * FF_CHUNK)
            dhd = _dot_nt(drb, w2_v[sl, :])
            du = (dhd * (2.0 * jnp.sqrt(hd_ref[:, sl].astype(F32)))).astype(MXU_DTYPE)
            du_ref[:, sl] = du
            acc_ref[...] += _dot_nt(du, w1_v[:, sl])
        dh2_ref[...] = acc_ref[...]

    tm = TM // 2
    return pl.pallas_call(
        body, name="mlp_ln_bwd", grid=(t // tm,),
        in_specs=[_rows(tm, D_MODEL), _rows(tm, D_MODEL), _rows(tm, 1), _const((1, D_MODEL)), _rows(tm, D_FF),
                  _hbm(), _hbm()],
        out_specs=[_rows(tm, D_MODEL), _rows(tm, D_FF), _rows(tm, D_MODEL), _const((1, D_MODEL)),
                   _const((1, D_MODEL))],
        out_shape=[jax.ShapeDtypeStruct((t, D_MODEL), MXU_DTYPE), jax.ShapeDtypeStruct((t, D_FF), MXU_DTYPE),
                   jax.ShapeDtypeStruct((t, D_MODEL), F32), jax.ShapeDtypeStruct((1, D_MODEL), F32),
                   jax.ShapeDtypeStruct((1, D_MODEL), F32)],
        scratch_shapes=[pltpu.VMEM((D_MODEL, D_FF), MXU_DTYPE), pltpu.VMEM((D_FF, D_MODEL), MXU_DTYPE),
                        pltpu.VMEM((tm, D_MODEL), F32), pltpu.SemaphoreType.DMA((2,))],
        compiler_params=_cparams(("arbitrary",)),
    )(dh3, xhat, rstd, g, hdn, w1, w2)


def _outproj_ln_bwd(dh1, xhat, rstd, g, w):
    t = dh1.shape[0]

    def body(dh_ref, xh_ref, rs_ref, g_ref, w_ref, dr_ref, res_ref, dy_ref, dg_ref, db_ref):
        i = pl.program_id(0)

        @pl.when(i == 0)
        def _():
            dg_ref[...] = jnp.zeros_like(dg_ref)
            db_ref[...] = jnp.zeros_like(db_ref)

        dout = dh_ref[...]
        xh = xh_ref[...]
        dg_ref[...] += _sum0(dout * xh)
        db_ref[...] += _sum0(dout)
        dr = _ln_bwd(dout, xh, rs_ref[...], g_ref[...])
        dr_ref[...] = dr.astype(dr_ref.dtype)
        res_ref[...] = ALPHA * dr
        dy_ref[...] = _dot_nt(dr, w_ref[...])

    return pl.pallas_call(
        body, name="outproj_ln_bwd", grid=(t // TM,),
        in_specs=[_rows(TM, D_MODEL), _rows(TM, D_MODEL), _rows(TM, 1), _const((1, D_MODEL)),
                  _const((D_MODEL, D_MODEL))],
        out_specs=[_rows(TM, D_MODEL), _rows(TM, D_MODEL), _rows(TM, D_MODEL), _const((1, D_MODEL)),
                   _const((1, D_MODEL))],
        out_shape=[jax.ShapeDtypeStruct((t, D_MODEL), MXU_DTYPE), jax.ShapeDtypeStruct((t, D_MODEL), F32),
                   jax.ShapeDtypeStruct((t, D_MODEL), F32), jax.ShapeDtypeStruct((1, D_MODEL), F32),
                   jax.ShapeDtypeStruct((1, D_MODEL), F32)],
        compiler_params=_cparams(("arbitrary",)),
    )(dh1, xhat, rstd, g, w)


def _loss_fwd_bwd(h, target):
    t = h.shape[0]

    def body(h_ref, t_ref, l_ref, dh_ref):
        i = pl.program_id(0)

        @pl.when(i == 0)
        def _():
            l_ref[...] = jnp.zeros_like(l_ref)

        e = h_ref[...] - t_ref[...]
        dh_ref[...] = e * (1.0 / D_MODEL)
        per_tok = jnp.mean(e * e, axis=-1, keepdims=True)
        l_ref[...] += 0.5 * jnp.sum(per_tok, axis=0, keepdims=True)

    return pl.pallas_call(
        body, name="loss_fwd_bwd", grid=(t // TM,),
        in_specs=[_rows(TM, D_MODEL), _rows(TM, D_MODEL)],
        out_specs=[_const((1, 1)), _rows(TM, D_MODEL)],
        out_shape=[jax.ShapeDtypeStruct((1, 1), F32), jax.ShapeDtypeStruct((t, D_MODEL), F32)],
        compiler_params=_cparams(("arbitrary",)),
    )(h, target)


def _pick_col(x, idx):
    lane = lax.broadcasted_iota(jnp.int32, x.shape, 1)
    return jnp.sum(jnp.where(lane == idx, x, 0.0), axis=1, keepdims=True)


def _pick_row(x, idx):
    sub = lax.broadcasted_iota(jnp.int32, x.shape, 0)
    return jnp.sum(jnp.where(sub == idx, x, 0.0), axis=0, keepdims=True)


def _conv_taps(pad_ref, w, tm, base):
    acc = w[0:1, :] * pad_ref[base:base + tm, :]
    for k in range(1, 4):
        acc = acc + w[k:k + 1, :] * pad_ref[base + k:base + k + tm, :]
    return acc


def _ssd_chunk_common(adt_c, tri):
    cs = _dot_f32(tri, adt_c)
    return cs, cs.T, jnp.exp(cs)


def _ssd_head_terms(cs, cst, ecs, dt_c, h, tri):
    cs_col = _pick_col(cs, h)
    cs_row = _pick_row(cst, h)
    dt_col = _pick_col(dt_c, h)
    cs_last = cs_col[SSD_CHUNK - 1:SSD_CHUNK, :]
    lmat = jnp.exp(jnp.where(tri > 0.0, cs_col - cs_row, -1e30))
    ecs_col = _pick_col(ecs, h)
    decay_col = jnp.exp(cs_last - cs_col)
    return cs_col, dt_col, cs_last, lmat, ecs_col, decay_col


def _ssd_fwd(proj, cw, cb, dtb, a_neg, d_lanes, nw):
    t = proj.shape[0]
    tm = SSD_TM
    nt = t // tm
    ncq = tm // SSD_CHUNK
    hb = tm // SUBLANES

    def body(xbc_ref, halo_ref, z_ref, dt_ref, cw_ref, cb_ref, dtb_ref, a_ref, d_ref, nw_ref,
             y_ref, yy_ref, st_ref, xpad, xact, state):
        i = pl.program_id(0)

        @pl.when(i == 0)
        def _():
            state[...] = jnp.zeros_like(state)

        xpad[0:SUBLANES, :] = jnp.where(i > 0, halo_ref[...], 0.0)
        xpad[SUBLANES:SUBLANES + tm, :] = xbc_ref[...]
        acc = cb_ref[...] + _conv_taps(xpad, cw_ref[...], tm, SUBLANES - 3)
        xact[...] = acc * _sigmoid(acc)
        dt = _softplus(dt_ref[...] + dtb_ref[...])
        adt = dt * a_ref[...]
        r_i = lax.broadcasted_iota(jnp.int32, (SSD_CHUNK, SSD_CHUNK), 0)
        c_i = lax.broadcasted_iota(jnp.int32, (SSD_CHUNK, SSD_CHUNK), 1)
        tri = (r_i >= c_i).astype(F32)
        lane1 = lax.broadcasted_iota(jnp.int32, (1, LANES), 1)
        for c in range(ncq):
            sl = slice(c * SSD_CHUNK, (c + 1) * SSD_CHUNK)
            dt_c = dt[sl]
            cs, cst, ecs = _ssd_chunk_common(adt[sl], tri)
            for g in range(2):
                bg = xact[sl, 512 + g * 128:512 + (g + 1) * 128]
                cg = xact[sl, 768 + g * 128:768 + (g + 1) * 128]
                cbm = _dot_nt(cg, bg)
                for pr in range(2):
                    pi = g * 2 + pr
                    psl = slice(pi * 128, (pi + 1) * 128)
                    xp = xact[sl, psl]
                    prev = state[pi]
                    st_ref[c, pi] = prev
                    yp = xp * d_ref[:, psl]
                    new_s = jnp.zeros((SSD_STATE, LANES), F32)
                    dec_lane = jnp.zeros((1, LANES), F32)
                    for hh in range(2):
                        h = g * 4 + pr * 2 + hh
                        lm = (lane1 >= 64) if hh else (lane1 < 64)
                        _, dt_col, cs_last, lmat, ecs_col, decay_col = _ssd_head_terms(cs, cst, ecs, dt_c, h, tri)
                        xdt = jnp.where(lm, xp, 0.0) * dt_col
                        yp = yp + _dot(cbm * lmat, xdt)
                        yp = yp + _dot(cg * ecs_col, jnp.where(lm, prev, 0.0))
                        new_s = new_s + _dot_tn(bg * decay_col, xdt)
                        dec_lane = dec_lane + jnp.where(lm, jnp.exp(cs_last), 0.0)
                    state[pi] = prev * dec_lane + new_s
                    yy_ref[sl, psl] = yp
        yy = yy_ref[...]
        z = z_ref[...]
        yg = yy * (z * _sigmoid(z))
        ms = jnp.mean(yg * yg, axis=-1, keepdims=True)
        y_ref[...] = (yg * lax.rsqrt(ms + LN_EPS) * nw_ref[...]).astype(y_ref.dtype)

    halo_map = lambda i: (jnp.maximum(i * hb - 1, 0), 0)
    return pl.pallas_call(
        body, name="ssd_fwd", grid=(nt,),
        in_specs=[pl.BlockSpec((tm, SSD_XBC), lambda i: (i, 0)), pl.BlockSpec((SUBLANES, SSD_XBC), halo_map),
                  pl.BlockSpec((tm, SSD_WIDTH), lambda i: (i, P_Z // SSD_WIDTH)),
                  pl.BlockSpec((tm, LANES), lambda i: (i, P_DT // LANES)),
                  _const((4, SSD_XBC)), _const((1, SSD_XBC)), _const((1, LANES)), _const((1, LANES)),
                  _const((1, SSD_WIDTH)), _const((1, SSD_WIDTH))],
        out_specs=[_rows(tm, SSD_WIDTH), _rows(tm, SSD_WIDTH),
                   pl.BlockSpec((ncq, 4, SSD_STATE, LANES), lambda i: (i, 0, 0, 0))],
        out_shape=[jax.ShapeDtypeStruct((t, SSD_WIDTH), MXU_DTYPE), jax.ShapeDtypeStruct((t, SSD_WIDTH), F32),
                   jax.ShapeDtypeStruct((t // SSD_CHUNK, 4, SSD_STATE, LANES), F32)],
        scratch_shapes=[pltpu.VMEM((tm + SUBLANES, SSD_XBC), F32), pltpu.VMEM((tm, SSD_XBC), F32),
                        pltpu.VMEM((4, SSD_STATE, LANES), F32)],
        compiler_params=_cparams(("arbitrary",)),
    )(proj, proj, proj, proj, cw, cb, dtb, a_neg, d_lanes, nw)


def _ssd_bwd(dycat, proj, yy, states, cw, cb, dtb, a_neg, d_lanes, nw):
    t = proj.shape[0]
    tm = SSD_TM
    nt = t // tm
    ncq = tm // SSD_CHUNK
    hb = tm // SUBLANES

    def body(dy_ref, xbc_ref, halo_ref, z_ref, dt_ref, yy_ref, st_ref, cw_ref, cb_ref, dtb_ref, a_ref, d_ref, nw_ref,
             dxbc_ref, dz_ref, ddt_ref, dcw_ref, dcb_ref, ddtb_ref, da_ref, dd_ref, dnw_ref,
             xpad, xact, dxact, dpad, dstate, dnext):
        i = pl.program_id(0)

        @pl.when(i == 0)
        def _():
            for r in (dcw_ref, dcb_ref, ddtb_ref, da_ref, dd_ref, dnw_ref, dstate, dnext):
                r[...] = jnp.zeros_like(r)

        xpad[0:SUBLANES, :] = jnp.where(i < nt - 1, halo_ref[...], 0.0)
        xpad[SUBLANES:SUBLANES + tm, :] = xbc_ref[...]
        cw_v = cw_ref[...]
        acc = cb_ref[...] + _conv_taps(xpad, cw_v, tm, SUBLANES - 3)
        sig = _sigmoid(acc)
        xact[...] = acc * sig
        dt_raw = dt_ref[...] + dtb_ref[...]
        dt = _softplus(dt_raw)
        a_v = a_ref[...]
        adt = dt * a_v
        yy = yy_ref[...]
        z = z_ref[...]
        sz = _sigmoid(z)
        siluz = z * sz
        yg = yy * siluz
        ms = jnp.mean(yg * yg, axis=-1, keepdims=True)
        rinv = lax.rsqrt(ms + LN_EPS)
        dout = dy_ref[...]
        dnw_ref[...] += _sum0(dout * yg * rinv)
        dyn = dout * nw_ref[...]
        dyg = rinv * dyn - yg * (rinv * rinv * rinv) * jnp.mean(dyn * yg, axis=-1, keepdims=True)
        dyy = dyg * siluz
        dz_ref[...] = (dyg * yy * (sz * (1.0 + z * (1.0 - sz)))).astype(dz_ref.dtype)
        dd_ref[...] += _sum0(dyy * xact[:, 0:SSD_WIDTH])

        r_i = lax.broadcasted_iota(jnp.int32, (SSD_CHUNK, SSD_CHUNK), 0)
        c_i = lax.broadcasted_iota(jnp.int32, (SSD_CHUNK, SSD_CHUNK), 1)
        tri = (r_i >= c_i).astype(F32)
        lane1 = lax.broadcasted_iota(jnp.int32, (1, LANES), 1)
        for c in reversed(range(ncq)):
            sl = slice(c * SSD_CHUNK, (c + 1) * SSD_CHUNK)
            dt_c = dt[sl]
            cs, cst, ecs = _ssd_chunk_common(adt[sl], tri)
            cacc = jnp.zeros((SSD_CHUNK, LANES), F32)
            racc = jnp.zeros((SSD_CHUNK, LANES), F32)
            ddtx = jnp.zeros((SSD_CHUNK, LANES), F32)
            for g in range(2):
                bg = xact[sl, 512 + g * 128:512 + (g + 1) * 128]
                cg = xact[sl, 768 + g * 128:768 + (g + 1) * 128]
                cbm = _dot_nt(cg, bg)
                dcb_m = jnp.zeros((SSD_CHUNK, SSD_CHUNK), F32)
                dbg = jnp.zeros((SSD_CHUNK, SSD_STATE), F32)
                dcg = jnp.zeros((SSD_CHUNK, SSD_STATE), F32)
                for pr in range(2):
                    pi = g * 2 + pr
                    psl = slice(pi * 128, (pi + 1) * 128)
                    xp = xact[sl, psl]
                    dyp = dyy[sl, psl]
                    prev = st_ref[c, pi]
                    ds_all = dstate[pi]
                    dxdt_p = jnp.zeros((SSD_CHUNK, LANES), F32)
                    dprev_new = jnp.zeros((SSD_STATE, LANES), F32)
                    dec_lane = jnp.zeros((1, LANES), F32)
                    dt_lanes = jnp.zeros((SSD_CHUNK, LANES), F32)
                    for hh in range(2):
                        h = g * 4 + pr * 2 + hh
                        lm = (lane1 >= 64) if hh else (lane1 < 64)
                        oh_l = (c_i == h).astype(F32)
                        oh_s = (r_i == h).astype(F32)
                        _, dt_col, cs_last, lmat, ecs_col, decay_col = _ssd_head_terms(cs, cst, ecs, dt_c, h, tri)
                        gm = cbm * lmat
                        xm = jnp.where(lm, xp, 0.0)
                        xdt = xm * dt_col
                        dym = jnp.where(lm, dyp, 0.0)
                        prevm = jnp.where(lm, prev, 0.0)
                        dsm = jnp.where(lm, ds_all, 0.0)
                        bdec = bg * decay_col
                        dxdt = _dot_tn(gm, dym) + _dot(bdec, dsm)
                        dxdt_p = dxdt_p + dxdt
                        ddtx = ddtx + oh_l * jnp.sum(dxdt * xm, axis=1, keepdims=True)
                        dt_lanes = dt_lanes + jnp.where(lm, dt_col, 0.0)
                        dgm = _dot_nt(dym, xdt)
                        dcb_m = dcb_m + dgm * lmat
                        w = dgm * gm
                        cacc = cacc + oh_l * jnp.sum(w, axis=1, keepdims=True)
                        racc = racc - oh_s * jnp.sum(w, axis=0, keepdims=True)
                        dce = _dot_nt(dym, prevm)
                        dcg = dcg + dce * ecs_col
                        cacc = cacc + oh_l * (jnp.sum(dce * cg, axis=1, keepdims=True) * ecs_col)
                        dprev_new = dprev_new + _dot_tn(cg * ecs_col, dym)
                        dbdec = _dot_nt(xdt, dsm)
                        dbg = dbg + dbdec * decay_col
                        dd = jnp.sum(dbdec * bg, axis=1, keepdims=True) * decay_col
                        cacc = cacc - oh_l * dd
                        cd = jnp.exp(cs_last)
                        dlast = jnp.sum(dd, axis=0, keepdims=True) + jnp.sum(
                            jnp.sum(dsm * prevm, axis=1, keepdims=True), axis=0, keepdims=True) * cd
                        cacc = cacc + jnp.where((r_i == SSD_CHUNK - 1) & (c_i == h), dlast, 0.0)
                        dec_lane = dec_lane + jnp.where(lm, cd, 0.0)
                    dstate[pi] = ds_all * dec_lane + dprev_new
                    dxact[sl, psl] = dxdt_p * dt_lanes + dyp * d_ref[:, psl]
                dcg = dcg + _dot(dcb_m, bg)
                dbg = dbg + _dot_tn(dcb_m, cg)
                dxact[sl, 512 + g * 128:512 + (g + 1) * 128] = dbg
                dxact[sl, 768 + g * 128:768 + (g + 1) * 128] = dcg
            dcs = cacc + racc.T
            dadt = _dot_f32((r_i <= c_i).astype(F32), dcs)
            ddt = dadt * a_v + ddtx
            da_ref[...] += _sum0(dadt * dt_c)
            ddt_raw = ddt * _sigmoid(dt_raw[sl])
            ddt_ref[sl, :] = ddt_raw.astype(ddt_ref.dtype)
            ddtb_ref[...] += _sum0(ddt_raw)
        dacc = dxact[...] * (sig * (1.0 + acc * (1.0 - sig)))
        dcb_ref[...] += _sum0(dacc)
        for k in range(4):
            dcw_ref[k:k + 1, :] += _sum0(dacc * xpad[SUBLANES - 3 + k:SUBLANES - 3 + k + tm, :])
        dpad[0:tm, :] = dacc
        dpad[tm:tm + SUBLANES, :] = dnext[...]
        dx = cw_v[0:1, :] * dpad[3:3 + tm, :]
        for k in range(1, 4):
            dx = dx + cw_v[k:k + 1, :] * dpad[3 - k:3 - k + tm, :]
        dxbc_ref[...] = dx.astype(dxbc_ref.dtype)
        dnext[...] = dacc[0:SUBLANES, :]

    rev = lambda i: nt - 1 - i
    halo_map = lambda i: (jnp.maximum(rev(i) * hb - 1, 0), 0)
    rrow = lambda n, col=0: pl.BlockSpec((tm, n), lambda i: (rev(i), col))
    return pl.pallas_call(
        body, name="ssd_bwd", grid=(nt,),
        in_specs=[rrow(SSD_WIDTH), rrow(SSD_XBC), pl.BlockSpec((SUBLANES, SSD_XBC), halo_map),
                  rrow(SSD_WIDTH, P_Z // SSD_WIDTH), rrow(LANES, P_DT // LANES), rrow(SSD_WIDTH),
                  pl.BlockSpec((ncq, 4, SSD_STATE, LANES), lambda i: (rev(i), 0, 0, 0)),
                  _const((4, SSD_XBC)), _const((1, SSD_XBC)), _const((1, LANES)), _const((1, LANES)),
                  _const((1, SSD_WIDTH)), _const((1, SSD_WIDTH))],
        out_specs=[rrow(SSD_XBC), rrow(SSD_WIDTH), rrow(LANES), _const((SUBLANES, SSD_XBC)), _const((1, SSD_XBC)),
                   _const((1, LANES)), _const((1, LANES)), _const((1, SSD_WIDTH)), _const((1, SSD_WIDTH))],
        out_shape=[jax.ShapeDtypeStruct((t, SSD_XBC), MXU_DTYPE), jax.ShapeDtypeStruct((t, SSD_WIDTH), MXU_DTYPE),
                   jax.ShapeDtypeStruct((t, LANES), MXU_DTYPE), jax.ShapeDtypeStruct((SUBLANES, SSD_XBC), F32),
                   jax.ShapeDtypeStruct((1, SSD_XBC), F32), jax.ShapeDtypeStruct((1, LANES), F32),
                   jax.ShapeDtypeStruct((1, LANES), F32), jax.ShapeDtypeStruct((1, SSD_WIDTH), F32),
                   jax.ShapeDtypeStruct((1, SSD_WIDTH), F32)],
        scratch_shapes=[pltpu.VMEM((tm + SUBLANES, SSD_XBC), F32), pltpu.VMEM((tm, SSD_XBC), F32),
                        pltpu.VMEM((tm, SSD_XBC), F32), pltpu.VMEM((tm + SUBLANES, SSD_XBC), F32),
                        pltpu.VMEM((4, SSD_STATE, LANES), F32), pltpu.VMEM((SUBLANES, SSD_XBC), F32)],
        compiler_params=_cparams(("arbitrary",)),
    )(dycat, proj, proj, proj, proj, yy, states, cw, cb, dtb, a_neg, d_lanes, nw)


def _cmul_add(ar, ai, br, bi, cr, ci):
    return ar + br * cr - bi * ci, ai + br * ci + bi * cr


def _s5_fwd(proj, bre, bim, cre, cim, d_skip, glu_w, glu_b, coef):
    t = proj.shape[0]
    tm = SCAN_TM
    ng = tm // SUBLANES

    def body(u_ref, bre_ref, bim_ref, cre_ref, cim_ref, d_ref, w_ref, b_ref, coef_ref,
             y_ref, y2_ref, hre_ref, him_ref, carry):
        i = pl.program_id(0)

        @pl.when(i == 0)
        def _():
            carry[...] = jnp.zeros_like(carry)

        u = u_ref[...]
        hre_ref[...] = _dot(u, bre_ref[...])
        him_ref[...] = _dot(u, bim_ref[...])

        def step(gi, car):
            cr_, ci_ = car
            rows = pl.ds(pl.multiple_of(gi * SUBLANES, SUBLANES), SUBLANES)
            r = hre_ref[rows, :]
            m = him_ref[rows, :]
            for k, sh in enumerate((1, 2, 4)):
                r, m = _cmul_add(r, m, coef_ref[k, 0], coef_ref[k, 1], pltpu.roll(r, sh, 0), pltpu.roll(m, sh, 0))
            r, m = _cmul_add(r, m, coef_ref[3, 0], coef_ref[3, 1], cr_, ci_)
            hre_ref[rows, :] = r
            him_ref[rows, :] = m
            return (jnp.broadcast_to(r[SUBLANES - 1:SUBLANES, :], r.shape),
                    jnp.broadcast_to(m[SUBLANES - 1:SUBLANES, :], m.shape))

        cr_, ci_ = lax.fori_loop(0, ng, step, (carry[0], carry[1]))
        carry[0] = cr_
        carry[1] = ci_
        y2 = _dot(hre_ref[...], cre_ref[...]) - _dot(him_ref[...], cim_ref[...]) + d_ref[...] * u
        y2_ref[...] = y2
        ya = _gelu(y2)
        y_ref[...] = (ya * _sigmoid(_dot(ya, w_ref[...]) + b_ref[...])).astype(y_ref.dtype)

    return pl.pallas_call(
        body, name="s5_fwd", grid=(t // tm,),
        in_specs=[pl.BlockSpec((tm, S5_WIDTH), lambda i: (i, P_U // S5_WIDTH)),
                  _const((S5_WIDTH, S5_NSTATE)), _const((S5_WIDTH, S5_NSTATE)), _const((S5_NSTATE, S5_WIDTH)),
                  _const((S5_NSTATE, S5_WIDTH)), _const((1, S5_WIDTH)), _const((S5_WIDTH, S5_WIDTH)),
                  _const((1, S5_WIDTH)), _const((5, 2, SUBLANES, S5_NSTATE))],
        out_specs=[_rows(tm, S5_WIDTH), _rows(tm, S5_WIDTH), _rows(tm, S5_NSTATE), _rows(tm, S5_NSTATE)],
        out_shape=[jax.ShapeDtypeStruct((t, S5_WIDTH), MXU_DTYPE), jax.ShapeDtypeStruct((t, S5_WIDTH), F32),
                   jax.ShapeDtypeStruct((t, S5_NSTATE), F32), jax.ShapeDtypeStruct((t, S5_NSTATE), F32)],
        scratch_shapes=[pltpu.VMEM((2, SUBLANES, S5_NSTATE), F32)],
        compiler_params=_cparams(("arbitrary",)),
    )(proj, bre, bim, cre, cim, d_skip, glu_w, glu_b, coef)


def _s5_bwd(dycat, proj, y2, hre, him, bre, bim, cre, cim, d_skip, glu_w, glu_b, rcoef):
    t = proj.shape[0]
    tm = SCAN_TM
    nt = t // tm
    ng = tm // SUBLANES
    hb = tm // SUBLANES

    def body(dy_ref, u_ref, y2_ref, hre_ref, him_ref, hre_halo, him_halo, bre_ref, bim_ref, cre_ref, cim_ref, d_ref,
             w_ref, b_ref, coef_ref,
             du_ref, dbre_ref, dbim_ref, dcre_ref, dcim_ref, dlam_ref, dd_ref, dw_ref, dgb_ref,
             gre, gim, hpre, hpim, carry):
        i = pl.program_id(0)

        @pl.when(i == 0)
        def _():
            for r in (dbre_ref, dbim_ref, dcre_ref, dcim_ref, dlam_ref, dd_ref, dw_ref, dgb_ref, carry):
                r[...] = jnp.zeros_like(r)

        u = u_ref[...]
        y2 = y2_ref[...]
        dout = dy_ref[...]
        ya = _gelu(y2)
        sg = _sigmoid(_dot(ya, w_ref[...]) + b_ref[...])
        dv = dout * ya * sg * (1.0 - sg)
        dya = dout * sg + _dot_nt(dv, w_ref[...])
        dw_ref[...] += _dot_tn(ya, dv)
        dgb_ref[...] += _sum0(dv)
        dy2 = dya * _gelu_grad(y2)
        dd_ref[...] += _sum0(dy2 * u)
        hre_v = hre_ref[...]
        him_v = him_ref[...]
        dcre_ref[...] += _dot_tn(hre_v, dy2)
        dcim_ref[...] -= _dot_tn(him_v, dy2)
        gre[...] = _dot_nt(dy2, cre_ref[...])
        gim[...] = -_dot_nt(dy2, cim_ref[...])
        first = i == nt - 1
        hpre[0:SUBLANES, :] = jnp.where(first, 0.0, hre_halo[...])
        hpim[0:SUBLANES, :] = jnp.where(first, 0.0, him_halo[...])
        hpre[SUBLANES:SUBLANES + tm, :] = hre_v
        hpim[SUBLANES:SUBLANES + tm, :] = him_v
        row0 = lax.broadcasted_iota(jnp.int32, (SUBLANES, S5_NSTATE), 0) == 0

        def step(k, car):
            cr_, ci_, dlr, dli = car
            gi = ng - 1 - k
            rows = pl.ds(pl.multiple_of(gi * SUBLANES, SUBLANES), SUBLANES)
            nrows = pl.ds(pl.multiple_of(gi * SUBLANES + SUBLANES, SUBLANES), SUBLANES)
            r = gre[rows, :]
            m = gim[rows, :]
            for kk, sh in enumerate((1, 2, 4)):
                r, m = _cmul_add(r, m, coef_ref[kk, 0], coef_ref[kk, 1], pltpu.roll(r, SUBLANES - sh, 0),
                                 pltpu.roll(m, SUBLANES - sh, 0))
            r, m = _cmul_add(r, m, coef_ref[3, 0], coef_ref[3, 1], cr_, ci_)
            gre[rows, :] = r
            gim[rows, :] = m
            pr_ = hpre[rows, :]
            pm_ = hpim[rows, :]
            hr_ = jnp.where(row0, jnp.broadcast_to(pr_[SUBLANES - 1:SUBLANES, :], pr_.shape),
                            pltpu.roll(hpre[nrows, :], 1, 0))
            hm_ = jnp.where(row0, jnp.broadcast_to(pm_[SUBLANES - 1:SUBLANES, :], pm_.shape),
                            pltpu.roll(hpim[nrows, :], 1, 0))
            dlr = dlr + hr_ * r + hm_ * m
            dli = dli + hr_ * m - hm_ * r
            return (jnp.broadcast_to(r[0:1, :], r.shape), jnp.broadcast_to(m[0:1, :], m.shape), dlr, dli)

        z8 = jnp.zeros((SUBLANES, S5_NSTATE), F32)
        cr_, ci_, dlr, dli = lax.fori_loop(0, ng, step, (carry[0], carry[1], z8, z8))
        carry[0] = cr_
        carry[1] = ci_
        dlam_ref[0] += dlr
        dlam_ref[1] += dli
        g_re = gre[...]
        g_im = gim[...]
        du_ref[...] = (dy2 * d_ref[...] + _dot_nt(g_re, bre_ref[...]) + _dot_nt(g_im, bim_ref[...])
                       ).astype(du_ref.dtype)
        dbre_ref[...] += _dot_tn(u, g_re)
        dbim_ref[...] += _dot_tn(u, g_im)

    rev = lambda i: nt - 1 - i
    rrow = lambda n, col=0: pl.BlockSpec((tm, n), lambda i: (rev(i), col))
    halo = pl.BlockSpec((SUBLANES, S5_NSTATE), lambda i: (jnp.maximum(rev(i) * hb - 1, 0), 0))
    return pl.pallas_call(
        body, name="s5_bwd", grid=(nt,),
        in_specs=[rrow(S5_WIDTH, 512 // S5_WIDTH), rrow(S5_WIDTH, P_U // S5_WIDTH), rrow(S5_WIDTH),
                  rrow(S5_NSTATE), rrow(S5_NSTATE), halo, halo,
                  _const((S5_WIDTH, S5_NSTATE)), _const((S5_WIDTH, S5_NSTATE)), _const((S5_NSTATE, S5_WIDTH)),
                  _const((S5_NSTATE, S5_WIDTH)), _const((1, S5_WIDTH)), _const((S5_WIDTH, S5_WIDTH)),
                  _const((1, S5_WIDTH)), _const((5, 2, SUBLANES, S5_NSTATE))],
        out_specs=[rrow(S5_WIDTH), _const((S5_WIDTH, S5_NSTATE)), _const((S5_WIDTH, S5_NSTATE)),
                   _const((S5_NSTATE, S5_WIDTH)), _const((S5_NSTATE, S5_WIDTH)), _const((2, SUBLANES, S5_NSTATE)),
                   _const((1, S5_WIDTH)), _const((S5_WIDTH, S5_WIDTH)), _const((1, S5_WIDTH))],
        out_shape=[jax.ShapeDtypeStruct((t, S5_WIDTH), MXU_DTYPE), jax.ShapeDtypeStruct((S5_WIDTH, S5_NSTATE), F32),
                   jax.ShapeDtypeStruct((S5_WIDTH, S5_NSTATE), F32), jax.ShapeDtypeStruct((S5_NSTATE, S5_WIDTH), F32),
                   jax.ShapeDtypeStruct((S5_NSTATE, S5_WIDTH), F32),
                   jax.ShapeDtypeStruct((2, SUBLANES, S5_NSTATE), F32), jax.ShapeDtypeStruct((1, S5_WIDTH), F32),
                   jax.ShapeDtypeStruct((S5_WIDTH, S5_WIDTH), F32), jax.ShapeDtypeStruct((1, S5_WIDTH), F32)],
        scratch_shapes=[pltpu.VMEM((tm, S5_NSTATE), F32), pltpu.VMEM((tm, S5_NSTATE), F32),
                        pltpu.VMEM((tm + SUBLANES, S5_NSTATE), F32), pltpu.VMEM((tm + SUBLANES, S5_NSTATE), F32),
                        pltpu.VMEM((2, SUBLANES, S5_NSTATE), F32)],
        compiler_params=_cparams(("arbitrary",)),
    )(dycat, proj, y2, hre, him, hre, him, bre, bim, cre, cim, d_skip, glu_w, glu_b, rcoef)


def _rg_gates(xc, wa, ba, wx, bx, nsp):
    r = _sigmoid(_dot(xc, wa) + ba)
    ig = _sigmoid(_dot(xc, wx) + bx)
    log_a = nsp * r
    a = jnp.exp(log_a)
    mult = jnp.sqrt(-_expm1(2.0 * log_a))
    return r, ig, a, mult


def _rg_fwd(proj, cw, cb, wa, ba, wx, bx, nsp):
    t = proj.shape[0]
    tm = SCAN_TM
    ng = tm // SUBLANES
    hb = tm // SUBLANES

    def body(x_ref, halo_ref, gt_ref, cw_ref, cb_ref, wa_ref, ba_ref, wx_ref, bx_ref, nsp_ref,
             y_ref, h_ref, xpad, abuf, carry):
        i = pl.program_id(0)

        @pl.when(i == 0)
        def _():
            carry[...] = jnp.zeros_like(carry)

        xpad[0:SUBLANES, :] = jnp.where(i > 0, halo_ref[...], 0.0)
        xpad[SUBLANES:SUBLANES + tm, :] = x_ref[...]
        xc = cb_ref[...] + _conv_taps(xpad, cw_ref[...], tm, SUBLANES - 3)
        _, ig, a, mult = _rg_gates(xc, wa_ref[...], ba_ref[...], wx_ref[...], bx_ref[...], nsp_ref[...])
        abuf[...] = a
        h_ref[...] = mult * (ig * xc)
        sub = lax.broadcasted_iota(jnp.int32, (SUBLANES, RG_WIDTH), 0)

        def step(gi, car):
            rows = pl.ds(pl.multiple_of(gi * SUBLANES, SUBLANES), SUBLANES)
            av = abuf[rows, :]
            bv = h_ref[rows, :]
            for sh in (1, 2, 4):
                m = sub >= sh
                bv = jnp.where(m, av * pltpu.roll(bv, sh, 0) + bv, bv)
                av = jnp.where(m, av * pltpu.roll(av, sh, 0), av)
            hv = bv + av * car
            h_ref[rows, :] = hv
            return jnp.broadcast_to(hv[SUBLANES - 1:SUBLANES, :], hv.shape)

        carry[...] = lax.fori_loop(0, ng, step, carry[...])
        y_ref[...] = (h_ref[...] * _gelu(gt_ref[...])).astype(y_ref.dtype)

    return pl.pallas_call(
        body, name="rg_fwd", grid=(t // tm,),
        in_specs=[pl.BlockSpec((tm, RG_WIDTH), lambda i: (i, P_XRG // RG_WIDTH)),
                  pl.BlockSpec((SUBLANES, RG_WIDTH), lambda i: (jnp.maximum(i * hb - 1, 0), P_XRG // RG_WIDTH)),
                  pl.BlockSpec((tm, RG_WIDTH), lambda i: (i, P_GRG // RG_WIDTH)),
                  _const((4, RG_WIDTH)), _const((1, RG_WIDTH)), _const((RG_WIDTH, RG_WIDTH)), _const((1, RG_WIDTH)),
                  _const((RG_WIDTH, RG_WIDTH)), _const((1, RG_WIDTH)), _const((1, RG_WIDTH))],
        out_specs=[_rows(tm, RG_WIDTH), _rows(tm, RG_WIDTH)],
        out_shape=[jax.ShapeDtypeStruct((t, RG_WIDTH), MXU_DTYPE), jax.ShapeDtypeStruct((t, RG_WIDTH), F32)],
        scratch_shapes=[pltpu.VMEM((tm + SUBLANES, RG_WIDTH), F32), pltpu.VMEM((tm, RG_WIDTH), F32),
                        pltpu.VMEM((SUBLANES, RG_WIDTH), F32)],
        compiler_params=_cparams(("arbitrary",)),
    )(proj, proj, proj, cw, cb, wa, ba, wx, bx, nsp)


def _rg_bwd(dycat, proj, hs, cw, cb, wa, ba, wx, bx, nsp):
    t = proj.shape[0]
    tm = SCAN_TM
    nt = t // tm
    ng = tm // SUBLANES
    hb = tm // SUBLANES

    def body(dy_ref, x_ref, halo_ref, gt_ref, h_ref, h_halo, cw_ref, cb_ref, wa_ref, ba_ref, wx_ref, bx_ref, nsp_ref,
             dx_ref, dgt_ref, dcw_ref, dcb_ref, dwa_ref, dba_ref, dwx_ref, dbx_ref, dnsp_ref,
             xpad, abuf, gbuf, hpad, dabuf, dpad, carry, dnext):
        i = pl.program_id(0)

        @pl.when(i == 0)
        def _():
            for r in (dcw_ref, dcb_ref, dwa_ref, dba_ref, dwx_ref, dbx_ref, dnsp_ref, carry, dnext):
                r[...] = jnp.zeros_like(r)

        first = i == nt - 1
        xpad[0:SUBLANES, :] = jnp.where(first, 0.0, halo_ref[...])
        xpad[SUBLANES:SUBLANES + tm, :] = x_ref[...]
        cw_v = cw_ref[...]
        xc = cb_ref[...] + _conv_taps(xpad, cw_v, tm, SUBLANES - 3)
        nsp_v = nsp_ref[...]
        r, ig, a, mult = _rg_gates(xc, wa_ref[...], ba_ref[...], wx_ref[...], bx_ref[...], nsp_v)
        abuf[...] = a
        hv = h_ref[...]
        hpad[0:SUBLANES, :] = jnp.where(first, 0.0, h_halo[...])
        hpad[SUBLANES:SUBLANES + tm, :] = hv
        gt = gt_ref[...]
        dout = dy_ref[...]
        dgt_ref[...] = (dout * hv * _gelu_grad(gt)).astype(dgt_ref.dtype)
        gbuf[...] = dout * _gelu(gt)
        sub = lax.broadcasted_iota(jnp.int32, (SUBLANES, RG_WIDTH), 0)
        last_row = sub == SUBLANES - 1
        row0 = sub == 0

        def step(k, car):
            gi = ng - 1 - k
            rows = pl.ds(pl.multiple_of(gi * SUBLANES, SUBLANES), SUBLANES)
            nrows = pl.ds(pl.multiple_of(gi * SUBLANES + SUBLANES, SUBLANES), SUBLANES)
            av = abuf[rows, :]
            bv = gbuf[rows, :] + jnp.where(last_row, car, 0.0)
            ev = jnp.where(last_row, 0.0, pltpu.roll(av, SUBLANES - 1, 0))
            for sh in (1, 2, 4):
                m = sub < SUBLANES - sh
                bv = jnp.where(m, bv + ev * pltpu.roll(bv, SUBLANES - sh, 0), bv)
                ev = jnp.where(m, ev * pltpu.roll(ev, SUBLANES - sh, 0), 0.0)
            gbuf[rows, :] = bv
            pv = hpad[rows, :]
            hprev = jnp.where(row0, jnp.broadcast_to(pv[SUBLANES - 1:SUBLANES, :], pv.shape),
                              pltpu.roll(hpad[nrows, :], 1, 0))
            dabuf[rows, :] = bv * hprev
            return jnp.broadcast_to((av * bv)[0:1, :], bv.shape)

        carry[...] = lax.fori_loop(0, ng, step, carry[...])
        gv = gbuf[...]
        da = dabuf[...]
        ix = ig * xc
        dmult = gv * ix
        dig = gv * mult * xc
        dxc = gv * mult * ig
        dlog_a = da * a - dmult * (a * a) / mult
        dnsp_ref[...] += _sum0(dlog_a * r)
        dpr = dlog_a * nsp_v * r * (1.0 - r)
        dpi = dig * ig * (1.0 - ig)
        dxc = dxc + _dot_nt(dpr, wa_ref[...]) + _dot_nt(dpi, wx_ref[...])
        dwa_ref[...] += _dot_tn(xc, dpr)
        dwx_ref[...] += _dot_tn(xc, dpi)
        dba_ref[...] += _sum0(dpr)
        dbx_ref[...] += _sum0(dpi)
        dcb_ref[...] += _sum0(dxc)
        for k in range(4):
            dcw_ref[k:k + 1, :] += _sum0(dxc * xpad[SUBLANES - 3 + k:SUBLANES - 3 + k + tm, :])
        dpad[0:tm, :] = dxc
        dpad[tm:tm + SUBLANES, :] = dnext[...]
        dx = cw_v[0:1, :] * dpad[3:3 + tm, :]
        for k in range(1, 4):
            dx = dx + cw_v[k:k + 1, :] * dpad[3 - k:3 - k + tm, :]
        dx_ref[...] = dx.astype(dx_ref.dtype)
        dnext[...] = dxc[0:SUBLANES, :]

    rev = lambda i: nt - 1 - i
    rrow = lambda n, col=0: pl.BlockSpec((tm, n), lambda i: (rev(i), col))
    sq = _const((RG_WIDTH, RG_WIDTH))
    vec = _const((1, RG_WIDTH))
    return pl.pallas_call(
        body, name="rg_bwd", grid=(nt,),
        in_specs=[rrow(RG_WIDTH, 768 // RG_WIDTH), rrow(RG_WIDTH, P_XRG // RG_WIDTH),
                  pl.BlockSpec((SUBLANES, RG_WIDTH), lambda i: (jnp.maximum(rev(i) * hb - 1, 0), P_XRG // RG_WIDTH)),
                  rrow(RG_WIDTH, P_GRG // RG_WIDTH), rrow(RG_WIDTH),
                  pl.BlockSpec((SUBLANES, RG_WIDTH), lambda i: (jnp.maximum(rev(i) * hb - 1, 0), 0)),
                  _const((4, RG_WIDTH)), vec, sq, vec, sq, vec, vec],
        out_specs=[rrow(RG_WIDTH), rrow(RG_WIDTH), _const((SUBLANES, RG_WIDTH)), vec, sq, vec, sq, vec, vec],
        out_shape=[jax.ShapeDtypeStruct((t, RG_WIDTH), MXU_DTYPE), jax.ShapeDtypeStruct((t, RG_WIDTH), MXU_DTYPE),
                   jax.ShapeDtypeStruct((SUBLANES, RG_WIDTH), F32), jax.ShapeDtypeStruct((1, RG_WIDTH), F32),
                   jax.ShapeDtypeStruct((RG_WIDTH, RG_WIDTH), F32), jax.ShapeDtypeStruct((1, RG_WIDTH), F32),
                   jax.ShapeDtypeStruct((RG_WIDTH, RG_WIDTH), F32), jax.ShapeDtypeStruct((1, RG_WIDTH), F32),
                   jax.ShapeDtypeStruct((1, RG_WIDTH), F32)],
        scratch_shapes=[pltpu.VMEM((tm + SUBLANES, RG_WIDTH), F32), pltpu.VMEM((tm, RG_WIDTH), F32),
                        pltpu.VMEM((tm, RG_WIDTH), F32), pltpu.VMEM((tm + SUBLANES, RG_WIDTH), F32),
                        pltpu.VMEM((tm, RG_WIDTH), F32), pltpu.VMEM((tm + SUBLANES, RG_WIDTH), F32),
                        pltpu.VMEM((SUBLANES, RG_WIDTH), F32), pltpu.VMEM((SUBLANES, RG_WIDTH), F32)],
        compiler_params=_cparams(("arbitrary",)),
    )(dycat, proj, proj, proj, hs, hs, cw, cb, wa, ba, wx, bx, nsp)


def _block_diag(blocks):
    g, a, b = blocks.shape
    eye = jnp.eye(g, dtype=blocks.dtype)
    return (eye[:, None, :, None] * blocks[:, :, None, :]).reshape(g * a, g * b)


def _block_diag_extract(m, g):
    a, b = m.shape[0] // g, m.shape[1] // g
    m4 = m.reshape(g, a, g, b)
    idx = jnp.arange(g)
    return m4[idx, :, idx, :]


def _s5_prepare(lam_re, lam_im, log_step, b_re, b_im, c_re, c_im):
    step = jnp.exp(log_step)[:, None]
    mag = jnp.exp(lam_re * step)
    lbr = mag * jnp.cos(lam_im * step)
    lbi = mag * jnp.sin(lam_im * step)
    nr, ni = lbr - 1.0, lbi
    den = lam_re * lam_re + lam_im * lam_im
    cr = (nr * lam_re + ni * lam_im) / den
    ci = (ni * lam_re - nr * lam_im) / den
    bbr = cr[..., None] * b_re - ci[..., None] * b_im
    bbi = cr[..., None] * b_im + ci[..., None] * b_re
    bre = _block_diag(jnp.swapaxes(bbr, 1, 2))
    bim = _block_diag(jnp.swapaxes(bbi, 1, 2))
    cre = _block_diag(jnp.swapaxes(c_re, 1, 2))
    cim = _block_diag(jnp.swapaxes(c_im, 1, 2))
    return lbr.reshape(-1), lbi.reshape(-1), bre, bim, cre, cim


def _s5_scan_coef(lbr, lbi, reverse):
    if reverse:
        lbi = -lbi
    pr, pi = [lbr], [lbi]
    for _ in range(7):
        pr, pi = pr + [pr[-1] * lbr - pi[-1] * lbi], pi + [pr[-1] * lbi + pi[-1] * lbr]
    row = jnp.arange(SUBLANES)[:, None]
    tabs = []
    for sh in (1, 2, 4):
        keep = (row < SUBLANES - sh) if reverse else (row >= sh)
        tabs.append(jnp.stack([jnp.where(keep, pr[sh - 1][None, :], 0.0), jnp.where(keep, pi[sh - 1][None, :], 0.0)]))
    powr = jnp.stack(pr)
    powi = jnp.stack(pi)
    if reverse:
        powr, powi = powr[::-1], powi[::-1]
    tabs.append(jnp.stack([powr, powi]))
    tabs.append(jnp.zeros_like(tabs[-1]))
    return jnp.stack(tabs).astype(F32)


def _xy_peers():
    x, y, c = lax.axis_index("x"), lax.axis_index("y"), lax.axis_index("c")
    return x, y, c, [(1 - x, y), (x, 1 - y), (1 - x, 1 - y)]


def _hbm():
    return pl.BlockSpec(memory_space=pl.ANY)


def _xy_allgather(buf, *, name):
    n, w = buf.shape

    def body(x_ref, out_ref, send_sems, recv_sems, local_sem):
        x, y, c, peers = _xy_peers()
        me = 2 * x + y
        own = pltpu.make_async_copy(x_ref, out_ref.at[me], local_sem)
        own.start()
        sends = []
        for k, (px, py) in enumerate(peers):
            cp = pltpu.make_async_remote_copy(src_ref=x_ref, dst_ref=out_ref.at[me], send_sem=send_sems.at[k],
                                              recv_sem=recv_sems.at[k], device_id=(px, py, c), device_id_type=MESH)
            cp.start()
            sends.append(cp)
        for k, (px, py) in enumerate(peers):
            pltpu.make_async_remote_copy(src_ref=x_ref, dst_ref=out_ref.at[2 * px + py], send_sem=send_sems.at[k],
                                         recv_sem=recv_sems.at[k], device_id=(px, py, c),
                                         device_id_type=MESH).wait_recv()
        for cp in sends:
            cp.wait_send()
        own.wait()

    return pl.pallas_call(
        body, name=name, in_specs=[_hbm()], out_specs=_hbm(),
        out_shape=jax.ShapeDtypeStruct((4, n, w), buf.dtype),
        scratch_shapes=[pltpu.SemaphoreType.DMA((3,)), pltpu.SemaphoreType.DMA((3,)), pltpu.SemaphoreType.DMA],
    )(buf)


def _remote(src, dst, send_sem, recv_sem, dev):
    return pltpu.make_async_remote_copy(src_ref=src, dst_ref=dst, send_sem=send_sem, recv_sem=recv_sem,
                                        device_id=dev, device_id_type=MESH)


LAYER_GATHERED = (
    ("ssd_conv_w", (4, 256), 1), ("rg_conv_w", (4, LANES), 1),
    ("w_in", (1024, W_IN_PAD), 1), ("s5_glu_w", (64, 256), 0), ("w_out", (256, 1024), 0), ("xa_wq", (256, 1024), 0),
    ("xa_wk", (256, 1024), 0), ("xa_wv", (256, 1024), 0), ("xa_wo", (256, 1024), 0), ("mlp_w1", (1024, 1024), 1),
    ("mlp_w2", (1024, 1024), 0),
)
N_GATHERED = len(LAYER_GATHERED)
WAIT_GROUPS = ((0, 1, 2, 3), (4,), (5, 6, 7, 8), (9, 10))
RG_CONV_SHARD = RG_WIDTH // 4
N_GATHER_COPIES = 3 * N_GATHERED * DEPTH


def _gather_part(ref, t, pos):
    _, shp, ax = LAYER_GATHERED[t % N_GATHERED]
    idx = tuple(pl.ds(pos * shp[ax], shp[ax]) if d == ax else slice(None) for d in range(len(shp)))
    return ref.at[idx]


def _gather_start(shards):
    n = len(shards)
    lands = []
    for t, s in enumerate(shards):
        _, shp, ax = LAYER_GATHERED[t % N_GATHERED]
        full = shp[:ax] + (4 * shp[ax],) + shp[ax + 1:]
        lands.append(pltpu.with_memory_space_constraint(lax.empty(full, s.dtype), pltpu.HBM))

    def body(*refs):
        srcs, lnds = refs[:n], refs[n:2 * n]
        send_sems, recv_sems, local_sems = refs[2 * n:2 * n + 3]
        token = refs[-1]
        x, y, c, peers = _xy_peers()
        me = 2 * x + y
        for t in range(n):
            for k, (px, py) in enumerate(peers):
                _remote(srcs[t], _gather_part(lnds[t], t, me), send_sems.at[k * n + t], recv_sems.at[k * n + t],
                        (px, py, c)).start()
            pltpu.make_async_copy(srcs[t], _gather_part(lnds[t], t, me), local_sems.at[t]).start()
        token[...] = jnp.zeros_like(token)

    hbm = pl.BlockSpec(memory_space=pltpu.HBM)
    sem = pl.BlockSpec(memory_space=pltpu.SEMAPHORE)
    outs = pl.pallas_call(
        body, name="weights_gather_start", in_specs=[hbm] * (2 * n),
        out_shape=(pltpu.SemaphoreType.DMA((3 * n,)), pltpu.SemaphoreType.DMA((3 * n,)),
                   pltpu.SemaphoreType.DMA((n,)),
                   *[pltpu.HBM(s.shape, s.dtype) for s in shards], *[pltpu.HBM(a.shape, a.dtype) for a in lands],
                   jax.ShapeDtypeStruct((SUBLANES, LANES), F32)),
        out_specs=(sem, sem, sem, *[hbm] * (2 * n), pl.BlockSpec(memory_space=pltpu.VMEM)),
        input_output_aliases={i: 3 + i for i in range(2 * n)},
        compiler_params=pltpu.CompilerParams(has_side_effects=pltpu.SideEffectType.DATAFLOW_SIDE_EFFECTING),
    )(*[pltpu.with_memory_space_constraint(s, pltpu.HBM) for s in shards], *lands)
    return outs[0], outs[1], outs[2], outs[3:3 + n], outs[3 + n:3 + 2 * n], outs[-1]


def _gather_wait(handle, ts, after, *, name):
    send_sems, recv_sems, local_sems, src_thru, land_thru, _ = handle
    n = len(src_thru)
    m = len(ts)

    def body(*refs):
        srcs, lnds = refs[:m], refs[m:2 * m]
        ssem, rsem, lsem = refs[2 * m:2 * m + 3]
        x, y, c, peers = _xy_peers()
        me = 2 * x + y
        for i, t in enumerate(ts):
            for k, (px, py) in enumerate(peers):
                cp = _remote(srcs[i], _gather_part(lnds[i], t, 2 * px + py), ssem.at[k * n + t], rsem.at[k * n + t],
                             (px, py, c))
                cp.wait_send()
                cp.wait_recv()
            pltpu.make_async_copy(srcs[i], _gather_part(lnds[i], t, me), lsem.at[t]).wait()

    hbm = pl.BlockSpec(memory_space=pltpu.HBM)
    sem = pl.BlockSpec(memory_space=pltpu.SEMAPHORE)
    args = [src_thru[t] for t in ts] + [land_thru[t] for t in ts]
    outs = pl.pallas_call(
        body, name=name, in_specs=[hbm] * (2 * m) + [sem, sem, sem, pl.BlockSpec(memory_space=pl.ANY)],
        out_shape=[pltpu.HBM(a.shape, a.dtype) for a in args], out_specs=[hbm] * (2 * m),
        input_output_aliases={i: i for i in range(2 * m)},
        compiler_params=pltpu.CompilerParams(has_side_effects=pltpu.SideEffectType.DATAFLOW_SIDE_EFFECTING),
    )(*args, send_sems, recv_sems, local_sems, after)
    return outs[:m], outs[m:]


C_CHUNKS = 8
XY_CHUNKS = 4
EW_ROWS = 512


def _c_exchange(g, part):
    w = g.shape[2]
    row0, nrows = G_PARTS[part]
    half = nrows // 2
    rq = half // C_CHUNKS

    def body(g_ref, got_ref, send_sems, recv_sems):
        x, y, c = lax.axis_index("x"), lax.axis_index("y"), lax.axis_index("c")
        cps = []
        for s in range(4):
            for q in range(C_CHUNKS):
                k = s * C_CHUNKS + q
                cp = _remote(g_ref.at[s, pl.ds(row0 + (1 - c) * half + q * rq, rq), :],
                             got_ref.at[s, pl.ds(q * rq, rq), :], send_sems.at[k], recv_sems.at[k], (x, y, 1 - c))
                cp.start()
                cps.append(cp)
        for cp in cps:
            cp.wait_recv()
        for cp in cps:
            cp.wait_send()

    return pl.pallas_call(
        body, name="grad_c_exchange_%d" % part, in_specs=[_hbm()], out_specs=_hbm(),
        out_shape=jax.ShapeDtypeStruct((4, half, w), g.dtype),
        scratch_shapes=[pltpu.SemaphoreType.DMA((4 * C_CHUNKS,)), pltpu.SemaphoreType.DMA((4 * C_CHUNKS,))],
    )(g)


XFER_DTYPE = jnp.bfloat16


def _add_own_half(g, got, c_arr, part):
    w = g.shape[2]
    row0, nrows = G_PARTS[part]
    half = nrows // 2
    nb = half // EW_ROWS
    b0 = row0 // EW_ROWS

    def body(c_ref, a_ref, b_ref, o_ref, t_ref):
        sm = a_ref[...] + b_ref[...]
        o_ref[...] = sm.astype(o_ref.dtype)

        @pl.when(pl.program_id(1) == nb - 1)
        def _():
            t_ref[...] = sm[:, EW_ROWS - MISC_ROWS:, :]

    grid_spec = pltpu.PrefetchScalarGridSpec(
        num_scalar_prefetch=1, grid=(4, nb),
        in_specs=[pl.BlockSpec((1, EW_ROWS, w), lambda s, i, c: (s, b0 + c[0] * nb + i, 0)),
                  pl.BlockSpec((1, EW_ROWS, w), lambda s, i, c: (s, i, 0))],
        out_specs=[pl.BlockSpec((1, EW_ROWS, w), lambda s, i, c: (s, i, 0)),
                   pl.BlockSpec((1, MISC_ROWS, w), lambda s, i, c: (s, 0, 0))])
    return pl.pallas_call(
        body, name="grad_add_halves", grid_spec=grid_spec,
        out_shape=[jax.ShapeDtypeStruct((4, half, w), XFER_DTYPE), jax.ShapeDtypeStruct((4, MISC_ROWS, w), g.dtype)],
        compiler_params=_cparams(("arbitrary", "arbitrary")),
    )(c_arr, g, got)


def _xy_pieces(arrs):
    pieces = []
    for a, arr in enumerate(arrs):
        nch = XY_CHUNKS if a == 0 else 1
        rq = arr.shape[1] // nch
        pieces += [(a, pl.ds(q * rq, rq)) for q in range(nch)]
    return pieces


def _xy_start(arrs, *, name):
    na = len(arrs)
    pieces = _xy_pieces(arrs)
    npc = len(pieces)
    lands = [pltpu.with_memory_space_constraint(lax.empty(a.shape, a.dtype), pltpu.HBM) for a in arrs]

    def body(*refs):
        ins, outs = refs[:na], refs[na:2 * na]
        send_sems, recv_sems, local_sems = refs[2 * na:2 * na + 3]
        token = refs[-1]
        x, y, c, peers = _xy_peers()
        me = 2 * x + y
        for k, (px, py) in enumerate(peers):
            for j, (a, rows) in enumerate(pieces):
                _remote(ins[a].at[2 * px + py, rows, :], outs[a].at[me, rows, :], send_sems.at[k * npc + j],
                        recv_sems.at[k * npc + j], (px, py, c)).start()
        for j, (a, rows) in enumerate(pieces):
            pltpu.make_async_copy(ins[a].at[me, rows, :], outs[a].at[me, rows, :], local_sems.at[j]).start()
        token[...] = jnp.zeros_like(token)

    hbm = pl.BlockSpec(memory_space=pltpu.HBM)
    sem = pl.BlockSpec(memory_space=pltpu.SEMAPHORE)
    outs = pl.pallas_call(
        body, name=name, in_specs=[hbm] * (2 * na),
        out_shape=(pltpu.SemaphoreType.DMA((3 * npc,)), pltpu.SemaphoreType.DMA((3 * npc,)),
                   pltpu.SemaphoreType.DMA((npc,)),
                   *[pltpu.HBM(a.shape, a.dtype) for a in arrs], *[pltpu.HBM(a.shape, a.dtype) for a in arrs],
                   jax.ShapeDtypeStruct((SUBLANES, LANES), F32)),
        out_specs=(sem, sem, sem, *[hbm] * (2 * na), pl.BlockSpec(memory_space=pltpu.VMEM)),
        input_output_aliases={i: 3 + i for i in range(2 * na)},
        compiler_params=pltpu.CompilerParams(has_side_effects=pltpu.SideEffectType.DATAFLOW_SIDE_EFFECTING),
    )(*[pltpu.with_memory_space_constraint(a, pltpu.HBM) for a in arrs], *lands)
    return (outs[0], outs[1], outs[2], outs[3:3 + na], outs[3 + na:3 + 2 * na]), outs[-1]


def _xy_wait(handle, after, *, name):
    send_sems, recv_sems, local_sems, src_thru, land_thru = handle
    na = len(src_thru)
    pieces = _xy_pieces(src_thru)
    npc = len(pieces)

    def body(*refs):
        ins, outs = refs[:na], refs[na:2 * na]
        ssem, rsem, lsem = refs[2 * na:2 * na + 3]
        x, y, c, peers = _xy_peers()
        me = 2 * x + y
        for k, (px, py) in enumerate(peers):
            for j, (a, rows) in enumerate(pieces):
                cp = _remote(ins[a].at[me, rows, :], outs[a].at[2 * px + py, rows, :], ssem.at[k * npc + j],
                             rsem.at[k * npc + j], (px, py, c))
                cp.wait_send()
                cp.wait_recv()
        for j, (a, rows) in enumerate(pieces):
            pltpu.make_async_copy(ins[a].at[me, rows, :], outs[a].at[me, rows, :], lsem.at[j]).wait()

    hbm = pl.BlockSpec(memory_space=pltpu.HBM)
    sem = pl.BlockSpec(memory_space=pltpu.SEMAPHORE)
    args = list(src_thru) + list(land_thru)
    outs = pl.pallas_call(
        body, name=name, in_specs=[hbm] * (2 * na) + [sem, sem, sem, pl.BlockSpec(memory_space=pl.ANY)],
        out_shape=[pltpu.HBM(a.shape, a.dtype) for a in args], out_specs=[hbm] * (2 * na),
        input_output_aliases={i: i for i in range(2 * na)},
        compiler_params=pltpu.CompilerParams(has_side_effects=pltpu.SideEffectType.DATAFLOW_SIDE_EFFECTING),
    )(*args, send_sems, recv_sems, local_sems, after)
    return outs[na:]


def _sum4_into_half(r, rt, c_arr, part, fbuf):
    _, half, w = r.shape
    nb = half // EW_ROWS
    b0 = G_PARTS[part][0] // EW_ROWS

    def body(c_ref, r_ref, t_ref, *rest):
        o_ref = rest[-1]
        o_ref[...] = ((r_ref[0].astype(F32) + r_ref[1].astype(F32)) + r_ref[2].astype(F32)) + r_ref[3].astype(F32)

        @pl.when(pl.program_id(0) == nb - 1)
        def _():
            o_ref[EW_ROWS - MISC_ROWS:, :] = ((t_ref[0] + t_ref[1]) + t_ref[2]) + t_ref[3]

    in_specs = [pl.BlockSpec((4, EW_ROWS, w), lambda i, c: (0, i, 0)),
                pl.BlockSpec((4, MISC_ROWS, w), lambda i, c: (0, 0, 0))]
    args = [c_arr, r, rt]
    aliases = {}
    if fbuf is not None:
        in_specs.append(pl.BlockSpec(memory_space=pl.ANY))
        args.append(fbuf)
        aliases = {3: 0}
    grid_spec = pltpu.PrefetchScalarGridSpec(
        num_scalar_prefetch=1, grid=(nb,), in_specs=in_specs,
        out_specs=pl.BlockSpec((EW_ROWS, w), lambda i, c: (b0 + c[0] * nb + i, 0)))
    return pl.pallas_call(
        body, name="grad_sum4", grid_spec=grid_spec, out_shape=jax.ShapeDtypeStruct((G_ROWS, w), F32),
        input_output_aliases=aliases, compiler_params=_cparams(("arbitrary",)),
    )(*args)


C_GATHER_ROWS = 512


def _c_allgather_halves(f, parts):
    w = f.shape[1]
    chunks = []
    for part in parts:
        chunks += [(part, r) for r in range(0, G_PARTS[part][1] // 2, C_GATHER_ROWS)]
    nch = len(chunks)

    def body(f_ref, out_ref, send_sems, recv_sems):
        x, y, c = lax.axis_index("x"), lax.axis_index("y"), lax.axis_index("c")

        def rows(q, owner):
            part, r = chunks[q]
            row0, nrows = G_PARTS[part]
            return pl.ds(row0 + owner * (nrows // 2) + r, C_GATHER_ROWS)

        sends = []
        for q in range(nch):
            cp = _remote(f_ref.at[rows(q, c), :], out_ref.at[rows(q, c), :], send_sems.at[q], recv_sems.at[q],
                         (x, y, 1 - c))
            cp.start()
            sends.append(cp)
        for q in range(nch):
            _remote(f_ref.at[rows(q, 1 - c), :], out_ref.at[rows(q, 1 - c), :], send_sems.at[q], recv_sems.at[q],
                    (x, y, 1 - c)).wait_recv()
        for cp in sends:
            cp.wait_send()

    return pl.pallas_call(
        body, name="grad_c_allgather_" + "".join(str(p) for p in parts), in_specs=[_hbm()], out_specs=_hbm(),
        input_output_aliases={0: 0},
        out_shape=jax.ShapeDtypeStruct((G_ROWS, w), f.dtype),
        scratch_shapes=[pltpu.SemaphoreType.DMA((nch,)), pltpu.SemaphoreType.DMA((nch,))],
    )(f)


def _adamw(w, m, v, g, g_rows=None):
    shape = w.shape
    cols = shape[-1]
    rows = int(math.prod(shape)) // cols
    tr = 256 if rows % 256 == 0 else rows
    from_flat = g_rows is not None
    c1 = 1.0 / (1.0 - ADAM_B1 ** ADAM_STEP)
    c2 = 1.0 / (1.0 - ADAM_B2 ** ADAM_STEP)

    def body(w_ref, m_ref, v_ref, g_ref, *outs):
        gg = g_ref[...]
        nm = ADAM_B1 * m_ref[...] + (1.0 - ADAM_B1) * gg
        nv = ADAM_B2 * v_ref[...] + (1.0 - ADAM_B2) * (gg * gg)
        if from_flat:
            outs[0][...] = gg
        d_ref, nm_ref, nv_ref = outs[-3:]
        nm_ref[...] = nm
        nv_ref[...] = nv
        d_ref[...] = -ADAM_LR * ((nm * c1) / (jnp.sqrt(nv * c2) + ADAM_EPS) + ADAM_WD * w_ref[...])

    spec = pl.BlockSpec((tr, cols), lambda i: (i, 0))
    if from_flat:
        nbl = rows // DEPTH // tr
        assert cols == FLAT and all(r % tr == 0 for r in g_rows) and len(g_rows) == DEPTH == 2
        b0, b1 = g_rows[0] // tr, g_rows[1] // tr
        g_spec = pl.BlockSpec((tr, cols), lambda i: (jnp.where(i < nbl, b0 + i, b1 + i - nbl), 0))
        g_arg = g
    else:
        g_spec = spec
        g_arg = g.reshape(rows, cols)
    n_out = 4 if from_flat else 3
    sds = jax.ShapeDtypeStruct((rows, cols), F32)
    outs = pl.pallas_call(
        body, name="adamw", grid=(rows // tr,), in_specs=[spec, spec, spec, g_spec], out_specs=[spec] * n_out,
        out_shape=[sds] * n_out, compiler_params=_cparams(("arbitrary",)),
    )(w.reshape(rows, cols), m.reshape(rows, cols), v.reshape(rows, cols), g_arg)
    outs = [o.reshape(shape) for o in outs]
    return outs if from_flat else [g] + outs


SMALL_SHARDED = (("s5_glu_w", (2, 64, 256), 1), ("ssd_conv_w", (2, 4, 256), 2), ("rg_conv_w", (2, 4, 64), 2))
REPLICATED = (
    ("ssd_conv_b", (2, 1024)), ("ssd_dt_bias", (2, 8)), ("ssd_a_log", (2, 8)), ("ssd_d", (2, 8)),
    ("ssd_norm_w", (2, 512)), ("s5_lam_re", (2, 16, 64)), ("s5_lam_im", (2, 16, 64)), ("s5_log_step", (2, 16)),
    ("s5_b_re", (2, 16, 64, 16)), ("s5_b_im", (2, 16, 64, 16)), ("s5_c_re", (2, 16, 16, 64)),
    ("s5_c_im", (2, 16, 16, 64)), ("s5_d", (2, 256)), ("s5_glu_b", (2, 256)), ("rg_conv_b", (2, 256)),
    ("rg_wa", (2, 4, 64, 64)), ("rg_ba", (2, 4, 64)), ("rg_wx", (2, 4, 64, 64)), ("rg_bx", (2, 4, 64)),
    ("rg_lambda", (2, 256)), ("ln1_g", (2, 1024)), ("ln1_b", (2, 1024)), ("ln2_g", (2, 1024)), ("ln2_b", (2, 1024)),
    ("ln3_g", (2, 1024)), ("ln3_b", (2, 1024)),
)
WEIGHT_ORDER = (
    "w_in", "w_out", "ssd_conv_w", "ssd_conv_b", "ssd_dt_bias", "ssd_a_log", "ssd_d", "ssd_norm_w", "s5_lam_re",
    "s5_lam_im", "s5_log_step", "s5_b_re", "s5_b_im", "s5_c_re", "s5_c_im", "s5_d", "s5_glu_w", "s5_glu_b",
    "rg_conv_w", "rg_conv_b", "rg_wa", "rg_ba", "rg_wx", "rg_bx", "rg_lambda", "ln1_g", "ln1_b", "xa_wq", "xa_wk",
    "xa_wv", "xa_wo", "ln2_g", "ln2_b", "mlp_w1", "mlp_w2", "ln3_g", "ln3_b",
)


def _size(shape):
    return int(math.prod(shape))


def _round_up(a, b):
    return (a + b - 1) // b * b


SMALL_ELEMS = sum(_size(s) for _, s, _ in SMALL_SHARDED)
REP_ELEMS = sum(_size(s) for _, s in REPLICATED)
REP_QROWS = _round_up(-(-REP_ELEMS // (4 * FLAT)), 8)
assert SMALL_ELEMS <= MISC_REP_ROW * FLAT and MISC_REP_ROW + REP_QROWS <= MISC_ROWS


def _pack_shards(tensors, names_shapes):
    return jnp.concatenate([tensors[n].reshape(-1) for n, *_ in names_shapes])


def _unpack(flat, names_shapes):
    out, off = {}, 0
    for n, s, *_ in names_shapes:
        out[n] = flat[off:off + _size(s)].reshape(s)
        off += _size(s)
    return out


def _split_shards(full, names_shapes):
    rows = []
    for k in range(4):
        parts = []
        for n, s, ax in names_shapes:
            w = s[ax]
            parts.append(lax.slice_in_dim(full[n], k * w, (k + 1) * w, axis=ax).reshape(-1))
        rows.append(jnp.concatenate(parts))
    return jnp.stack(rows)


def _pack_cols(w):
    pad = jnp.zeros((w.shape[0], LANES - SSD_HEADS), w.dtype)
    return jnp.concatenate([w[:, O_XBC:O_XBC + 1024], w[:, O_Z:O_Z + 512], w[:, O_U:O_U + 256],
                            w[:, O_XRG:O_XRG + 256], w[:, O_GRG:O_GRG + 256], w[:, O_DT:O_DT + 8], pad], axis=1)


def _unpack_cols(w):
    return jnp.concatenate([w[:, P_Z:P_Z + 512], w[:, P_XBC:P_XBC + 1024], w[:, P_DT:P_DT + 8],
                            w[:, P_U:P_U + 256], w[:, P_XRG:P_XRG + 256], w[:, P_GRG:P_GRG + 256]], axis=1)


def _lanes(v, width):
    return jnp.pad(v, (0, width - v.shape[0])).reshape(1, width)


def _layer_params(rep, l):
    p = {}
    p["ssd_cb"] = rep["ssd_conv_b"][l].reshape(1, -1)
    p["ssd_dtb"] = _lanes(rep["ssd_dt_bias"][l], LANES)
    p["ssd_a"] = _lanes(-jnp.exp(rep["ssd_a_log"][l]), LANES)
    p["ssd_d"] = jnp.repeat(rep["ssd_d"][l], 64).reshape(1, -1)
    p["ssd_nw"] = rep["ssd_norm_w"][l].reshape(1, -1)
    s5_args = tuple(rep[n][l] for n in ("s5_lam_re", "s5_lam_im", "s5_log_step", "s5_b_re", "s5_b_im", "s5_c_re",
                                        "s5_c_im"))
    (lbr, lbi, bre, bim, cre, cim), p["s5_vjp"] = jax.vjp(_s5_prepare, *s5_args)
    p.update(s5_bre=bre, s5_bim=bim, s5_cre=cre, s5_cim=cim)
    p["s5_coef"] = _s5_scan_coef(lbr, lbi, False)
    p["s5_rcoef"] = _s5_scan_coef(lbr, lbi, True)
    p["s5_d"] = rep["s5_d"][l].reshape(1, -1)
    p["s5_gb"] = rep["s5_glu_b"][l].reshape(1, -1)
    p["rg_cb"] = rep["rg_conv_b"][l].reshape(1, -1)
    p["rg_wa"] = _block_diag(rep["rg_wa"][l])
    p["rg_wx"] = _block_diag(rep["rg_wx"][l])
    p["rg_ba"] = rep["rg_ba"][l].reshape(1, -1)
    p["rg_bx"] = rep["rg_bx"][l].reshape(1, -1)
    p["rg_nsp"] = (-RG_C * jax.nn.softplus(-rep["rg_lambda"][l])).reshape(1, -1)
    p["rg_dnsp"] = RG_C * jax.nn.sigmoid(-rep["rg_lambda"][l])
    for n in ("ln1_g", "ln1_b", "ln2_g", "ln2_b", "ln3_g", "ln3_b"):
        p[n] = rep[n][l].reshape(1, -1)
    return p


def _layer_fwd(h, mem, p, fetch):
    s = {"h0": h}
    p.update(fetch(0, h))
    proj = _mm(h, p["w_in"], name="in_proj")
    s["proj"] = proj
    y_ssd, s["ssd_yy"], s["ssd_states"] = _ssd_fwd(proj, p["ssd_cw"], p["ssd_cb"], p["ssd_dtb"], p["ssd_a"],
                                                     p["ssd_d"], p["ssd_nw"])
    y_s5, s["s5_y2"], s["s5_hre"], s["s5_him"] = _s5_fwd(proj, p["s5_bre"], p["s5_bim"], p["s5_cre"], p["s5_cim"],
                                                         p["s5_d"], p["s5_glu_w"], p["s5_gb"], p["s5_coef"])
    y_rg, s["rg_h"] = _rg_fwd(proj, p["rg_cw"], p["rg_cb"], p["rg_wa"], p["rg_ba"], p["rg_wx"], p["rg_bx"],
                              p["rg_nsp"])
    s["ys"] = [y_ssd, y_s5, y_rg]
    p.update(fetch(1, y_rg))
    h1, s["xh1"], s["rs1"] = _outproj_ln_fwd(s["ys"], h, p["w_out"], p["ln1_g"], p["ln1_b"])
    s["h1"] = h1
    p.update(fetch(2, h1))
    kb = _mm(mem, p["xa_wk"], name="mem_proj")
    vb = _mm(mem, p["xa_wv"], name="mem_proj")
    s["kb"], s["vb"] = kb, vb
    h2, s["xh2"], s["rs2"], s["attn_o"] = _attn_ln_fwd(h1, p["xa_wq"], p["xa_wo"], kb, vb, p["ln2_g"], p["ln2_b"])
    s["h2"] = h2
    p.update(fetch(3, h2))
    h3, s["xh3"], s["rs3"], s["mlp_hdn"] = _mlp_ln_fwd(h2, p["mlp_w1"], p["mlp_w2"], p["ln3_g"], p["ln3_b"])
    return h3, s


def _layer_bwd(dh3, mem, p, s, l, gbuf, after_mlp=None):
    g = {}
    dr3, du, dh2, g["ln3_g"], g["ln3_b"] = _mlp_ln_bwd(dh3, s["xh3"], s["rs3"], p["ln3_g"], s["mlp_hdn"],
                                                        p["mlp_w1"], p["mlp_w2"])
    gbuf = _wgrad_flat(s["h2"], du, gbuf, mode="colblk", row_off=_grad_row("mlp_w1", l), name="wgrad_mlp_w1")
    gbuf = _wgrad_flat(s["mlp_hdn"], dr3, gbuf, mode="rowblk", row_off=_grad_row("mlp_w2", l), name="wgrad_mlp_w2")
    ln2_g = p["ln2_g"] if after_mlp is None else p["ln2_g"] + after_mlp(gbuf)[0:1, 0:1]
    dr2, dq, dh1, dkb, dvb, g["ln2_g"], g["ln2_b"] = _attn_ln_bwd(dh2, s["xh2"], s["rs2"], ln2_g, s["h1"],
                                                                   p["xa_wq"], p["xa_wo"], s["kb"], s["vb"])
    for n, a_op, g_op in (("xa_wo", s["attn_o"], dr2), ("xa_wq", s["h1"], dq), ("xa_wk", mem, dkb),
                          ("xa_wv", mem, dvb)):
        gbuf = _wgrad_flat(a_op, g_op, gbuf, mode="rows4", row_off=_grad_row(n, l), name="wgrad_" + n)
    dr1, dres, dycat, g["ln1_g"], g["ln1_b"] = _outproj_ln_bwd(dh1, s["xh1"], s["rs1"], p["ln1_g"], p["w_out"])
    gbuf = _wgrad_flat(s["ys"], dr1, gbuf, mode="rows4", row_off=_grad_row("w_out", l), name="wgrad_w_out")
    proj = s["proj"]
    (dxbc, dz, ddt, dcw, dcb, ddtb, da_neg, dd_l, dnw) = _ssd_bwd(
        dycat, proj, s["ssd_yy"], s["ssd_states"], p["ssd_cw"], p["ssd_cb"], p["ssd_dtb"], p["ssd_a"], p["ssd_d"],
        p["ssd_nw"])
    g["ssd_conv_w"] = dcw[0:4]
    g["ssd_conv_b"] = dcb[0]
    g["ssd_dt_bias"] = ddtb[0, :SSD_HEADS]
    g["ssd_a_log"] = da_neg[0, :SSD_HEADS] * p["ssd_a"][0, :SSD_HEADS]
    g["ssd_d"] = dd_l.reshape(SSD_HEADS, 64).sum(axis=1)
    g["ssd_norm_w"] = dnw[0]
    (du_s5, dbre, dbim, dcre, dcim, dlam, dd5, dgw, dgb) = _s5_bwd(
        dycat, proj, s["s5_y2"], s["s5_hre"], s["s5_him"], p["s5_bre"], p["s5_bim"], p["s5_cre"], p["s5_cim"],
        p["s5_d"], p["s5_glu_w"], p["s5_gb"], p["s5_rcoef"])
    dl = dlam.sum(axis=1)
    s5g = p["s5_vjp"]((dl[0], dl[1], dbre, dbim, dcre, dcim))
    for n, v in zip(("s5_lam_re", "s5_lam_im", "s5_log_step", "s5_b_re", "s5_b_im", "s5_c_re", "s5_c_im"), s5g):
        g[n] = v
    g["s5_d"] = dd5[0]
    g["s5_glu_w"] = dgw
    g["s5_glu_b"] = dgb[0]
    (dxrg, dgrg, drcw, drcb, dwa, dba, dwx, dbx, dnsp) = _rg_bwd(
        dycat, proj, s["rg_h"], p["rg_cw"], p["rg_cb"], p["rg_wa"], p["rg_ba"], p["rg_wx"], p["rg_bx"], p["rg_nsp"])
    g["rg_conv_w"] = drcw[0:4]
    g["rg_conv_b"] = drcb[0]
    g["rg_wa"] = _block_diag_extract(dwa, RG_BLOCKS)
    g["rg_wx"] = _block_diag_extract(dwx, RG_BLOCKS)
    g["rg_ba"] = dba.reshape(RG_BLOCKS, RG_BLOCK_DIM)
    g["rg_bx"] = dbx.reshape(RG_BLOCKS, RG_BLOCK_DIM)
    g["rg_lambda"] = dnsp[0] * p["rg_dnsp"]
    dproj = [dxbc, dz, du_s5, dxrg, dgrg, ddt]
    g["w_in"] = _unpack_cols(_wgrad_in(s["h0"], dproj))
    dh0 = _in_proj_bwd(dproj, p["w_in"], dres)
    for n in ("ln1_g", "ln1_b", "ln2_g", "ln2_b", "ln3_g", "ln3_b"):
        g[n] = g[n][0]
    return dh0, g, gbuf


def _local_step(h, memf, target, rep, fetch):
    params, saved = [], []
    for l in range(DEPTH):
        p = _layer_params(rep, l)
        params.append(p)
        h, s = _layer_fwd(h, memf, p, functools.partial(fetch, l))
        saved.append(s)
    loss11, dh = _loss_fwd_bwd(h, target)
    grads = [None] * DEPTH
    gbuf = None
    c_arr = lax.axis_index("c").astype(jnp.int32).reshape(1)
    handles = {}

    def start_part(buf, part):
        handles[part], token = _xy_start(_chip_sums(buf, c_arr, part), name="grad_xy_start_%d" % part)
        return token

    for l in reversed(range(DEPTH)):
        hook = functools.partial(start_part, part=1) if l == 0 else None
        dh, grads[l], gbuf = _layer_bwd(dh, memf, params[l], saved[l], l, gbuf, hook)
        if l == DEPTH - 1:
            gbuf = lax.dynamic_update_slice(
                gbuf, _w_in_block(grads[l]["w_in"], jnp.zeros((4, MISC_ROWS, FLAT), F32)),
                (0, _grad_row("w_in", l), 0))
            params[0]["ln3_g"] = params[0]["ln3_g"] + start_part(gbuf, 0)[0:1, 0:1]
    gsmall = {n: jnp.stack([grads[l][n] for l in range(DEPTH)]) for n in grads[0] if n != "w_in"}
    return loss11, dh, gsmall, grads[0]["w_in"], gbuf, handles, c_arr


def _w_in_block(gw, tail):
    gw = jnp.pad(gw.reshape(D_MODEL, 4, W_IN_SHARD), ((0, 0), (0, 0), (0, W_IN_PAD - W_IN_SHARD)))
    return jnp.concatenate([jnp.transpose(gw, (1, 0, 2)).reshape(4, W_IN_PAD, FLAT), tail], axis=1)


def _chip_sums(gbuf, c_arr, part):
    return list(_add_own_half(gbuf, _c_exchange(gbuf, part), c_arr, part))


def kernel(x, mem, w_in, w_out, ssd_conv_w, ssd_conv_b, ssd_dt_bias, ssd_a_log, ssd_d, ssd_norm_w, s5_lam_re, s5_lam_im, s5_log_step, s5_b_re, s5_b_im, s5_c_re, s5_c_im, s5_d, s5_glu_w, s5_glu_b, rg_conv_w, rg_conv_b, rg_wa, rg_ba, rg_wx, rg_bx, rg_lambda, ln1_g, ln1_b, xa_wq, xa_wk, xa_wv, xa_wo, ln2_g, ln2_b, mlp_w1, mlp_w2, ln3_g, ln3_b, loss_target, m_w_in, m_w_out, m_ssd_conv_w, m_ssd_conv_b, m_ssd_dt_bias, m_ssd_a_log, m_ssd_d, m_ssd_norm_w, m_s5_lam_re, m_s5_lam_im, m_s5_log_step, m_s5_b_re, m_s5_b_im, m_s5_c_re, m_s5_c_im, m_s5_d, m_s5_glu_w, m_s5_glu_b, m_rg_conv_w, m_rg_conv_b, m_rg_wa, m_rg_ba, m_rg_wx, m_rg_bx, m_rg_lambda, m_ln1_g, m_ln1_b, m_xa_wq, m_xa_wk, m_xa_wv, m_xa_wo, m_ln2_g, m_ln2_b, m_mlp_w1, m_mlp_w2, m_ln3_g, m_ln3_b, v_w_in, v_w_out, v_ssd_conv_w, v_ssd_conv_b, v_ssd_dt_bias, v_ssd_a_log, v_ssd_d, v_ssd_norm_w, v_s5_lam_re, v_s5_lam_im, v_s5_log_step, v_s5_b_re, v_s5_b_im, v_s5_c_re, v_s5_c_im, v_s5_d, v_s5_glu_w, v_s5_glu_b, v_rg_conv_w, v_rg_conv_b, v_rg_wa, v_rg_ba, v_rg_wx, v_rg_bx, v_rg_lambda, v_ln1_g, v_ln1_b, v_xa_wq, v_xa_wk, v_xa_wv, v_xa_wo, v_ln2_g, v_ln2_b, v_mlp_w1, v_mlp_w2, v_ln3_g, v_ln3_b):
    args = dict(locals())
    weights = {n: args[n] for n in WEIGHT_ORDER}
    mom_m = {n: args["m_" + n] for n in WEIGHT_ORDER}
    mom_v = {n: args["v_" + n] for n in WEIGHT_ORDER}

    shards = []
    for l in range(DEPTH):
        for n, shp, ax in LAYER_GATHERED:
            w = weights[n][l]
            if w.shape[1] != shp[1]:
                w = jnp.pad(w, ((0, 0), (0, shp[1] - w.shape[1])))
            if n not in ("ssd_conv_w", "rg_conv_w"):
                w = w.astype(MXU_DTYPE)
            shards.append(w)
    handle = _gather_start(shards)

    def unpad(arr, padded, width):
        return jnp.concatenate([arr[:, padded * k:padded * k + width] for k in range(4)], axis=1)

    def fetch(l, grp, after):
        ts = [l * N_GATHERED + j for j in WAIT_GROUPS[grp]]
        _, landed = _gather_wait(handle, ts, after, name="weights_gather_wait_%d_%d" % (l, grp))
        out = {}
        for t, arr in zip(ts, landed):
            n = LAYER_GATHERED[t % N_GATHERED][0]
            if n == "w_in":
                arr = _pack_cols(unpad(arr, W_IN_PAD, W_IN_SHARD))
            elif n == "rg_conv_w":
                arr = unpad(arr, LANES, RG_CONV_SHARD)
            out[{"ssd_conv_w": "ssd_cw", "rg_conv_w": "rg_cw"}.get(n, n)] = arr
        return out

    rep = {n: weights[n] for n, _ in REPLICATED}

    loss11, dx, gsmall, gw_in0, gbuf, handles, c_arr = _local_step(x[0], mem[0], loss_target[0], rep, fetch)
    grad_x = dx[None]
    loss = lax.psum(loss11[0, 0], ("x", "y", "c"))

    small_q = _split_shards(gsmall, SMALL_SHARDED)
    rep_q = jnp.pad(_pack_shards(gsmall, REPLICATED), (0, 4 * REP_QROWS * FLAT - REP_ELEMS)).reshape(4, -1)
    misc = jnp.concatenate(
        [jnp.pad(small_q, ((0, 0), (0, MISC_REP_ROW * FLAT - SMALL_ELEMS))), rep_q,
         jnp.zeros((4, (MISC_ROWS - MISC_REP_ROW - REP_QROWS) * FLAT), F32)], axis=1).reshape(4, MISC_ROWS, FLAT)
    gbuf = lax.dynamic_update_slice(gbuf, _w_in_block(gw_in0, misc), (0, _grad_row("w_in", 0), 0))
    handles[2], token = _xy_start(_chip_sums(gbuf, c_arr, 2), name="grad_xy_start_2")
    fbuf = None
    for part in (0, 1):
        got = _xy_wait(handles[part], dx, name="grad_xy_wait_%d" % part)
        fbuf = _sum4_into_half(got[0], got[1] + token[0:1, 0:1], c_arr, part, fbuf)
    fbuf = _c_allgather_halves(fbuf, (0, 1))
    res = {n: _adamw(weights[n], mom_m[n], mom_v[n], fbuf, g_rows=[_grad_row(n, l) for l in range(DEPTH)])
           for n in ("mlp_w1", "mlp_w2")}
    got = _xy_wait(handles[2], res["mlp_w2"][1], name="grad_xy_wait_2")
    reduced = _c_allgather_halves(_sum4_into_half(got[0], got[1], c_arr, 2, fbuf), (2,))
    misc_red = reduced[ROW_MISC:]
    rep_all = _xy_allgather(misc_red[MISC_REP_ROW:MISC_REP_ROW + REP_QROWS], name="small_grads_allgather")
    g_red = {**_unpack(misc_red[:MISC_REP_ROW].reshape(-1), SMALL_SHARDED),
             **_unpack(rep_all.reshape(-1), REPLICATED)}
    g_red["w_in"] = jnp.stack([
        reduced[_grad_row("w_in", l):_grad_row("w_in", l) + W_IN_PAD].reshape(D_MODEL, W_IN_PAD)[:, :W_IN_SHARD]
        for l in range(DEPTH)])

    for n in WEIGHT_ORDER:
        if n in ("w_out", "xa_wq", "xa_wk", "xa_wv", "xa_wo"):
            res[n] = _adamw(weights[n], mom_m[n], mom_v[n], reduced, g_rows=[_grad_row(n, l) for l in range(DEPTH)])
        elif n not in res:
            res[n] = _adamw(weights[n], mom_m[n], mom_v[n], g_red[n])
    return (loss, grad_x, *[res[n][0] for n in WEIGHT_ORDER], *[res[n][1] for n in WEIGHT_ORDER],
            *[res[n][2] for n in WEIGHT_ORDER], *[res[n][3] for n in WEIGHT_ORDER])
```

```python
import functools
import math

import jax
import jax.numpy as jnp
from jax import lax
from jax.experimental import pallas as pl
from jax.experimental.pallas import tpu as pltpu

F32 = jnp.float32
MXU_DTYPE = jnp.bfloat16

D_MODEL = 1024
DEPTH = 2
MEM_LEN = 256
SSD_WIDTH = 512
SSD_HEADS = 8
SSD_STATE = 128
SSD_CHUNK = 128
SSD_XBC = 1024
S5_WIDTH = 256
S5_GROUPS = 16
S5_GROUP_CH = 16
S5_STATE = 64
S5_NSTATE = S5_GROUPS * S5_STATE
RG_WIDTH = 256
RG_BLOCKS = 4
RG_BLOCK_DIM = 64
RG_C = 8.0
XA_HEADS = 4
XA_HEAD_DIM = 256
D_FF = 4096
D_IN = 2312
ALPHA = (2.0 * DEPTH) ** 0.25
LN_EPS = 1e-5
ADAM_LR = 0.001
ADAM_B1 = 0.9
ADAM_B2 = 0.999
ADAM_EPS = 1e-08
ADAM_WD = 0.01
ADAM_STEP = 10

P_XBC, P_Z, P_U, P_XRG, P_GRG, P_DT = 0, 1024, 1536, 1792, 2048, 2304
D_PACK = 2432
O_Z, O_XBC, O_DT, O_U, O_XRG, O_GRG = 0, 512, 1536, 1544, 1800, 2056

LANES = 128
SUBLANES = 8
VMEM_LIMIT = 52 * 1024 * 1024
TM = 512
SSD_TM = 256
SCAN_TM = 512
FLAT = 1024

MESH = pl.DeviceIdType.MESH


def _cparams(sem):
    return pltpu.CompilerParams(dimension_semantics=sem, vmem_limit_bytes=VMEM_LIMIT)


def _dot(a, b):
    return jnp.dot(a.astype(MXU_DTYPE), b.astype(MXU_DTYPE), preferred_element_type=F32)


def _dot_nt(a, b):
    return lax.dot_general(a.astype(MXU_DTYPE), b.astype(MXU_DTYPE), (((1,), (1,)), ((), ())),
                           preferred_element_type=F32)


def _dot_tn(a, b):
    return lax.dot_general(a.astype(MXU_DTYPE), b.astype(MXU_DTYPE), (((0,), (0,)), ((), ())),
                           preferred_element_type=F32)


def _dot_f32(a, b):
    return jnp.dot(a, b, precision=lax.Precision.HIGHEST, preferred_element_type=F32)


def _dot_f32_tn(a, b):
    return lax.dot_general(a, b, (((0,), (0,)), ((), ())), precision=lax.Precision.HIGHEST,
                           preferred_element_type=F32)


def _sigmoid(x):
    return 1.0 / (1.0 + jnp.exp(-x))


def _softplus(x):
    return jnp.maximum(x, 0.0) + jnp.log(1.0 + jnp.exp(-jnp.abs(x)))


_GELU_K = math.sqrt(2.0 / math.pi)


def _gelu(x):
    return 0.5 * x * (1.0 + jnp.tanh(_GELU_K * (x + 0.044715 * x * x * x)))


def _gelu_grad(x):
    t = jnp.tanh(_GELU_K * (x + 0.044715 * x * x * x))
    return 0.5 * (1.0 + t) + 0.5 * x * (1.0 - t * t) * _GELU_K * (1.0 + 3.0 * 0.044715 * x * x)


def _expm1(x):
    small = x * (1.0 + x * (0.5 + x * (1.0 / 6.0 + x * (1.0 / 24.0))))
    return jnp.where(jnp.abs(x) < 0.05, small, jnp.exp(x) - 1.0)


def _sum0(x):
    return jnp.sum(x, axis=0, keepdims=True)


def _ln_fwd(r, g, b):
    mu = jnp.mean(r, axis=-1, keepdims=True)
    xc = r - mu
    var = jnp.mean(xc * xc, axis=-1, keepdims=True)
    rstd = lax.rsqrt(var + LN_EPS)
    xhat = xc * rstd
    return xhat * g + b, xhat, rstd


def _ln_bwd(dout, xhat, rstd, g):
    dxh = dout * g
    m1 = jnp.mean(dxh, axis=-1, keepdims=True)
    m2 = jnp.mean(dxh * xhat, axis=-1, keepdims=True)
    return rstd * (dxh - m1 - xhat * m2)


def _rows(tm, n, col=0):
    return pl.BlockSpec((tm, n), lambda i: (i, col))


def _const(shape):
    nd = len(shape)
    return pl.BlockSpec(shape, lambda i: (0,) * nd)


def _mm(a, w, *, name):
    t, k = a.shape
    n = w.shape[1]
    tm = min(TM, t)

    def body(a_ref, w_ref, o_ref):
        o_ref[...] = _dot(a_ref[...], w_ref[...])

    return pl.pallas_call(
        body, name=name, grid=(t // tm,), in_specs=[_rows(tm, k), _const(w.shape)], out_specs=_rows(tm, n),
        out_shape=jax.ShapeDtypeStruct((t, n), F32), compiler_params=_cparams(("arbitrary",)),
    )(a, w)


DPROJ_PIECES = ((P_XBC, 1024), (P_Z, 512), (P_U, 256), (P_XRG, 256), (P_GRG, 256), (P_DT, LANES))


def _in_proj_bwd(pieces, w, dres):
    t = dres.shape[0]
    npc = len(pieces)

    def body(*refs):
        w_ref, r_ref, o_ref = refs[npc:]
        acc = r_ref[...]
        for p_ref, (off, k) in zip(refs[:npc], DPROJ_PIECES):
            acc = acc + _dot_nt(p_ref[...], w_ref[:, off:off + k])
        o_ref[...] = acc

    return pl.pallas_call(
        body, name="in_proj_bwd", grid=(t // TM,),
        in_specs=[_rows(TM, k) for _, k in DPROJ_PIECES] + [_const(w.shape), _rows(TM, D_MODEL)],
        out_specs=_rows(TM, D_MODEL), out_shape=jax.ShapeDtypeStruct((t, D_MODEL), F32),
        compiler_params=_cparams(("arbitrary",)),
    )(*pieces, w, dres)


def _wgrad_in(h0, pieces):
    t = h0.shape[0]
    npc = len(pieces)

    def body(*refs):
        h_ref, o_ref = refs[npc], refs[npc + 1]
        @pl.when(pl.program_id(0) == 0)
        def _():
            o_ref[...] = jnp.zeros_like(o_ref)

        hb = h_ref[...].astype(MXU_DTYPE)
        for p_ref, (off, k) in zip(refs[:npc], DPROJ_PIECES):
            o_ref[:, off:off + k] += _dot_tn(hb, p_ref[...])

    return pl.pallas_call(
        body, name="wgrad_in", grid=(t // TM,),
        in_specs=[_rows(TM, k) for _, k in DPROJ_PIECES] + [_rows(TM, D_MODEL)],
        out_specs=_const((D_MODEL, D_PACK)), out_shape=jax.ShapeDtypeStruct((D_MODEL, D_PACK), F32),
        compiler_params=_cparams(("arbitrary",)),
    )(*pieces, h0)


G_ROWS = 8192
G_PARTS = ((0, 4096), (4096, 2048), (6144, 2048))
W_IN_SHARD = 578
W_IN_PAD = 640
MISC_ROWS = 128
MISC_REP_ROW = 40
ROW_MISC = G_ROWS - MISC_ROWS
W_IN_BLOCK_ROWS = W_IN_PAD + MISC_ROWS


def _grad_row(name, l):
    base = 0 if l == 1 else 4096
    mid = base + 2048 if l == 1 else 6144
    return {"mlp_w1": base, "mlp_w2": base + 1024, "w_out": mid, "xa_wq": mid + 256, "xa_wk": mid + 512,
            "xa_wv": mid + 768, "xa_wo": mid + 1024, "w_in": mid + 1280}[name]


def _wgrad_flat(a, g, buf, *, mode, row_off, name):
    pieces = list(a) if isinstance(a, (list, tuple)) else [a]
    t = g.shape[0]
    tt = min(1024, t)
    ns = t // tt
    blk = D_MODEL

    def accumulate(o_ref, parts, s):
        @pl.when(s == 0)
        def _():
            o_ref[...] = jnp.zeros_like(o_ref)

        for q, v in parts:
            o_ref[q] += v

    if mode == "rows4":
        grid = (ns,)
        in_specs = [pl.BlockSpec((tt, p.shape[1]), lambda s: (s, 0)) for p in pieces]
        in_specs.append(pl.BlockSpec((tt, blk), lambda s: (s, 0)))
        out_spec = pl.BlockSpec((4, 256, FLAT), lambda s: (0, row_off // 256, 0))
        sem = ("arbitrary",)
        npc = len(pieces)

        def body(*refs):
            g_v = refs[npc][...]
            parts, q0 = [], 0
            for p_ref in refs[:npc]:
                full = _dot_tn(p_ref[...], g_v)
                nq = full.shape[0] // 256
                parts += [(q0 + q, full[q * 256:(q + 1) * 256]) for q in range(nq)]
                q0 += nq
            accumulate(refs[-1], parts, pl.program_id(0))
    else:
        grid = (2, ns)
        if mode == "rowblk":
            in_specs = [pl.BlockSpec((tt, 2 * blk), lambda q, s: (s, q)), pl.BlockSpec((tt, blk), lambda q, s: (s, 0))]
        else:
            in_specs = [pl.BlockSpec((tt, blk), lambda q, s: (s, 0)), pl.BlockSpec((tt, 2 * blk), lambda q, s: (s, q))]
        out_spec = pl.BlockSpec((2, blk, FLAT), lambda q, s: (q, row_off // blk, 0))
        sem = ("arbitrary", "arbitrary")

        def body(a_ref, g_ref, *rest):
            full = _dot_tn(a_ref[...], g_ref[...])
            if mode == "rowblk":
                parts = [(0, full[:blk]), (1, full[blk:])]
            else:
                parts = [(0, full[:, :blk]), (1, full[:, blk:])]
            accumulate(rest[-1], parts, pl.program_id(1))

    args = pieces + [g]
    aliases = {}
    if buf is not None:
        in_specs.append(pl.BlockSpec(memory_space=pl.ANY))
        args.append(buf)
        aliases = {len(args) - 1: 0}
    return pl.pallas_call(
        body, name=name, grid=grid, in_specs=in_specs, out_specs=out_spec,
        out_shape=jax.ShapeDtypeStruct((4, G_ROWS, FLAT), F32), input_output_aliases=aliases,
        compiler_params=_cparams(sem),
    )(*args)


def _outproj_ln_fwd(ys, h, w, g, b):
    t = h.shape[0]
    npc = len(ys)

    def body(*refs):
        h_ref, w_ref, g_ref, b_ref, hn_ref, xh_ref, rs_ref = refs[npc:]
        r = ALPHA * h_ref[...]
        off = 0
        for y_ref in refs[:npc]:
            k = y_ref.shape[1]
            r = r + _dot(y_ref[...], w_ref[off:off + k, :])
            off += k
        out, xhat, rstd = _ln_fwd(r, g_ref[...], b_ref[...])
        hn_ref[...] = out
        xh_ref[...] = xhat
        rs_ref[...] = rstd

    return pl.pallas_call(
        body, name="outproj_ln_fwd", grid=(t // TM,),
        in_specs=[_rows(TM, y.shape[1]) for y in ys] + [_rows(TM, D_MODEL), _const((D_MODEL, D_MODEL)),
                                                        _const((1, D_MODEL)), _const((1, D_MODEL))],
        out_specs=[_rows(TM, D_MODEL), _rows(TM, D_MODEL), _rows(TM, 1)],
        out_shape=[jax.ShapeDtypeStruct((t, D_MODEL), F32), jax.ShapeDtypeStruct((t, D_MODEL), F32),
                   jax.ShapeDtypeStruct((t, 1), F32)],
        compiler_params=_cparams(("arbitrary",)),
    )(*ys, h, w, g, b)


def _attn_probs(q, kb, hh):
    sl = slice(hh * XA_HEAD_DIM, (hh + 1) * XA_HEAD_DIM)
    s = _dot_nt(q[:, sl], kb[:, sl]) * (1.0 / math.sqrt(XA_HEAD_DIM))
    m = jnp.max(s, axis=-1, keepdims=True)
    e = jnp.exp(s - m)
    return e / jnp.sum(e, axis=-1, keepdims=True)


def _attn_ln_fwd(h1, wq, wo, kb, vb, g, b):
    t = h1.shape[0]

    def body(h_ref, wq_ref, wo_ref, k_ref, v_ref, g_ref, b_ref, hn_ref, xh_ref, rs_ref, o_ref):
        h = h_ref[...]
        q = _dot(h, wq_ref[...])
        kb_ = k_ref[...]
        vb_ = v_ref[...]
        for hh in range(XA_HEADS):
            sl = slice(hh * XA_HEAD_DIM, (hh + 1) * XA_HEAD_DIM)
            p = _attn_probs(q, kb_, hh)
            o_ref[:, sl] = _dot(p, vb_[:, sl]).astype(o_ref.dtype)
        r = ALPHA * h + _dot(o_ref[...], wo_ref[...])
        out, xhat, rstd = _ln_fwd(r, g_ref[...], b_ref[...])
        hn_ref[...] = out
        xh_ref[...] = xhat
        rs_ref[...] = rstd

    return pl.pallas_call(
        body, name="attn_ln_fwd", grid=(t // TM,),
        in_specs=[_rows(TM, D_MODEL), _const((D_MODEL, D_MODEL)), _const((D_MODEL, D_MODEL)),
                  _const((MEM_LEN, D_MODEL)), _const((MEM_LEN, D_MODEL)), _const((1, D_MODEL)), _const((1, D_MODEL))],
        out_specs=[_rows(TM, D_MODEL), _rows(TM, D_MODEL), _rows(TM, 1), _rows(TM, D_MODEL)],
        out_shape=[jax.ShapeDtypeStruct((t, D_MODEL), F32), jax.ShapeDtypeStruct((t, D_MODEL), F32),
                   jax.ShapeDtypeStruct((t, 1), F32), jax.ShapeDtypeStruct((t, D_MODEL), MXU_DTYPE)],
        compiler_params=_cparams(("arbitrary",)),
    )(h1, wq, wo, kb, vb, g, b)


def _attn_ln_bwd(dh2, xhat, rstd, g, h1, wq, wo, kb, vb):
    t = h1.shape[0]

    def body(dh_ref, xh_ref, rs_ref, g_ref, h_ref, wq_ref, wo_ref, k_ref, v_ref,
             dr_ref, dq_ref, dh1_ref, dk_ref, dv_ref, dg_ref, db_ref):
        i = pl.program_id(0)

        @pl.when(i == 0)
        def _():
            dk_ref[...] = jnp.zeros_like(dk_ref)
            dv_ref[...] = jnp.zeros_like(dv_ref)
            dg_ref[...] = jnp.zeros_like(dg_ref)
            db_ref[...] = jnp.zeros_like(db_ref)

        dout = dh_ref[...]
        xh = xh_ref[...]
        dg_ref[...] += _sum0(dout * xh)
        db_ref[...] += _sum0(dout)
        dr = _ln_bwd(dout, xh, rs_ref[...], g_ref[...])
        dr_ref[...] = dr.astype(dr_ref.dtype)
        do = _dot_nt(dr, wo_ref[...])
        h = h_ref[...]
        q = _dot(h, wq_ref[...])
        kb_ = k_ref[...]
        vb_ = v_ref[...]
        scale = 1.0 / math.sqrt(XA_HEAD_DIM)
        for hh in range(XA_HEADS):
            sl = slice(hh * XA_HEAD_DIM, (hh + 1) * XA_HEAD_DIM)
            p = _attn_probs(q, kb_, hh)
            do_h = do[:, sl]
            dp = _dot_nt(do_h, vb_[:, sl])
            ds = p * (dp - jnp.sum(dp * p, axis=-1, keepdims=True)) * scale
            dq_ref[:, sl] = _dot(ds, kb_[:, sl]).astype(dq_ref.dtype)
            dk_ref[:, sl] += _dot_tn(ds, q[:, sl])
            dv_ref[:, sl] += _dot_tn(p, do_h)
        dh1_ref[...] = ALPHA * dr + _dot_nt(dq_ref[...], wq_ref[...])

    return pl.pallas_call(
        body, name="attn_ln_bwd", grid=(t // TM,),
        in_specs=[_rows(TM, D_MODEL), _rows(TM, D_MODEL), _rows(TM, 1), _const((1, D_MODEL)), _rows(TM, D_MODEL),
                  _const((D_MODEL, D_MODEL)), _const((D_MODEL, D_MODEL)), _const((MEM_LEN, D_MODEL)),
                  _const((MEM_LEN, D_MODEL))],
        out_specs=[_rows(TM, D_MODEL), _rows(TM, D_MODEL), _rows(TM, D_MODEL), _const((MEM_LEN, D_MODEL)),
                   _const((MEM_LEN, D_MODEL)), _const((1, D_MODEL)), _const((1, D_MODEL))],
        out_shape=[jax.ShapeDtypeStruct((t, D_MODEL), MXU_DTYPE), jax.ShapeDtypeStruct((t, D_MODEL), MXU_DTYPE),
                   jax.ShapeDtypeStruct((t, D_MODEL), F32), jax.ShapeDtypeStruct((MEM_LEN, D_MODEL), F32),
                   jax.ShapeDtypeStruct((MEM_LEN, D_MODEL), F32), jax.ShapeDtypeStruct((1, D_MODEL), F32),
                   jax.ShapeDtypeStruct((1, D_MODEL), F32)],
        compiler_params=_cparams(("arbitrary",)),
    )(dh2, xhat, rstd, g, h1, wq, wo, kb, vb)


FF_CHUNK = 1024
N_FF = D_FF // FF_CHUNK


def _load_resident(pairs, sems):
    copies = [pltpu.make_async_copy(src, dst, sems.at[k]) for k, (src, dst) in enumerate(pairs)]
    for cp in copies:
        cp.start()
    for cp in copies:
        cp.wait()


def _mlp_ln_fwd(h2, w1, w2, g, b):
    t = h2.shape[0]

    def body(h_ref, w1_hbm, w2_hbm, g_ref, b_ref, hn_ref, xh_ref, rs_ref, hd_ref, w1_v, w2_v, acc_ref, sems):
        @pl.when(pl.program_id(0) == 0)
        def _():
            _load_resident([(w1_hbm, w1_v), (w2_hbm, w2_v)], sems)

        h = h_ref[...]
        hb = h.astype(MXU_DTYPE)
        acc_ref[...] = ALPHA * h
        for j in range(N_FF):
            sl = slice(j * FF_CHUNK, (j + 1) * FF_CHUNK)
            u = _dot(hb, w1_v[:, sl])
            hd = jnp.square(jnp.maximum(u, 0.0)).astype(MXU_DTYPE)
            hd_ref[:, sl] = hd
            acc_ref[...] += _dot(hd, w2_v[sl, :])
        out, xhat, rstd = _ln_fwd(acc_ref[...], g_ref[...], b_ref[...])
        hn_ref[...] = out
        xh_ref[...] = xhat
        rs_ref[...] = rstd

    return pl.pallas_call(
        body, name="mlp_ln_fwd", grid=(t // TM,),
        in_specs=[_rows(TM, D_MODEL), _hbm(), _hbm(), _const((1, D_MODEL)), _const((1, D_MODEL))],
        out_specs=[_rows(TM, D_MODEL), _rows(TM, D_MODEL), _rows(TM, 1), _rows(TM, D_FF)],
        out_shape=[jax.ShapeDtypeStruct((t, D_MODEL), F32), jax.ShapeDtypeStruct((t, D_MODEL), F32),
                   jax.ShapeDtypeStruct((t, 1), F32), jax.ShapeDtypeStruct((t, D_FF), MXU_DTYPE)],
        scratch_shapes=[pltpu.VMEM((D_MODEL, D_FF), MXU_DTYPE), pltpu.VMEM((D_FF, D_MODEL), MXU_DTYPE),
                        pltpu.VMEM((TM, D_MODEL), F32), pltpu.SemaphoreType.DMA((2,))],
        compiler_params=_cparams(("arbitrary",)),
    )(h2, w1, w2, g, b)


def _mlp_ln_bwd(dh3, xhat, rstd, g, hdn, w1, w2):
    t = dh3.shape[0]

    def body(dh_ref, xh_ref, rs_ref, g_ref, hd_ref, w1_hbm, w2_hbm,
             dr_ref, du_ref, dh2_ref, dg_ref, db_ref, w1_v, w2_v, acc_ref, sems):
        @pl.when(pl.program_id(0) == 0)
        def _():
            _load_resident([(w1_hbm, w1_v), (w2_hbm, w2_v)], sems)
            dg_ref[...] = jnp.zeros_like(dg_ref)
            db_ref[...] = jnp.zeros_like(db_ref)

        dout = dh_ref[...]
        xh = xh_ref[...]
        dg_ref[...] += _sum0(dout * xh)
        db_ref[...] += _sum0(dout)
        dr = _ln_bwd(dout, xh, rs_ref[...], g_ref[...])
        drb = dr.astype(MXU_DTYPE)
        dr_ref[...] = drb
        acc_ref[...] = ALPHA * dr
        for j in range(N_FF):
            sl = slice(j * FF_CHUNK, (j + 1) * FF_CHUNK)
            dhd = _dot_nt(drb, w2_v[sl, :])
            du = (dhd * (2.0 * jnp.sqrt(hd_ref[:, sl].astype(F32)))).astype(MXU_DTYPE)
            du_ref[:, sl] = du
            acc_ref[...] += _dot_nt(du, w1_v[:, sl])
        dh2_ref[...] = acc_ref[...]

    tm = TM // 2
    return pl.pallas_call(
        body, name="mlp_ln_bwd", grid=(t // tm,),
        in_specs=[_rows(tm, D_MODEL), _rows(tm, D_MODEL), _rows(tm, 1), _const((1, D_MODEL)), _rows(tm, D_FF),
                  _hbm(), _hbm()],
        out_specs=[_rows(tm, D_MODEL), _rows(tm, D_FF), _rows(tm, D_MODEL), _const((1, D_MODEL)),
                   _const((1, D_MODEL))],
        out_shape=[jax.ShapeDtypeStruct((t, D_MODEL), MXU_DTYPE), jax.ShapeDtypeStruct((t, D_FF), MXU_DTYPE),
                   jax.ShapeDtypeStruct((t, D_MODEL), F32), jax.ShapeDtypeStruct((1, D_MODEL), F32),
                   jax.ShapeDtypeStruct((1, D_MODEL), F32)],
        scratch_shapes=[pltpu.VMEM((D_MODEL, D_FF), MXU_DTYPE), pltpu.VMEM((D_FF, D_MODEL), MXU_DTYPE),
                        pltpu.VMEM((tm, D_MODEL), F32), pltpu.SemaphoreType.DMA((2,))],
        compiler_params=_cparams(("arbitrary",)),
    )(dh3, xhat, rstd, g, hdn, w1, w2)


def _outproj_ln_bwd(dh1, xhat, rstd, g, w):
    t = dh1.shape[0]

    def body(dh_ref, xh_ref, rs_ref, g_ref, w_ref, dr_ref, res_ref, dy_ref, dg_ref, db_ref):
        i = pl.program_id(0)

        @pl.when(i == 0)
        def _():
            dg_ref[...] = jnp.zeros_like(dg_ref)
            db_ref[...] = jnp.zeros_like(db_ref)

        dout = dh_ref[...]
        xh = xh_ref[...]
        dg_ref[...] += _sum0(dout * xh)
        db_ref[...] += _sum0(dout)
        dr = _ln_bwd(dout, xh, rs_ref[...], g_ref[...])
        dr_ref[...] = dr.astype(dr_ref.dtype)
        res_ref[...] = ALPHA * dr
        dy_ref[...] = _dot_nt(dr, w_ref[...])

    return pl.pallas_call(
        body, name="outproj_ln_bwd", grid=(t // TM,),
        in_specs=[_rows(TM, D_MODEL), _rows(TM, D_MODEL), _rows(TM, 1), _const((1, D_MODEL)),
                  _const((D_MODEL, D_MODEL))],
        out_specs=[_rows(TM, D_MODEL), _rows(TM, D_MODEL), _rows(TM, D_MODEL), _const((1, D_MODEL)),
                   _const((1, D_MODEL))],
        out_shape=[jax.ShapeDtypeStruct((t, D_MODEL), MXU_DTYPE), jax.ShapeDtypeStruct((t, D_MODEL), F32),
                   jax.ShapeDtypeStruct((t, D_MODEL), F32), jax.ShapeDtypeStruct((1, D_MODEL), F32),
                   jax.ShapeDtypeStruct((1, D_MODEL), F32)],
        compiler_params=_cparams(("arbitrary",)),
    )(dh1, xhat, rstd, g, w)


def _loss_fwd_bwd(h, target):
    t = h.shape[0]

    def body(h_ref, t_ref, l_ref, dh_ref):
        i = pl.program_id(0)

        @pl.when(i == 0)
        def _():
            l_ref[...] = jnp.zeros_like(l_ref)

        e = h_ref[...] - t_ref[...]
        dh_ref[...] = e * (1.0 / D_MODEL)
        per_tok = jnp.mean(e * e, axis=-1, keepdims=True)
        l_ref[...] += 0.5 * jnp.sum(per_tok, axis=0, keepdims=True)

    return pl.pallas_call(
        body, name="loss_fwd_bwd", grid=(t // TM,),
        in_specs=[_rows(TM, D_MODEL), _rows(TM, D_MODEL)],
        out_specs=[_const((1, 1)), _rows(TM, D_MODEL)],
        out_shape=[jax.ShapeDtypeStruct((1, 1), F32), jax.ShapeDtypeStruct((t, D_MODEL), F32)],
        compiler_params=_cparams(("arbitrary",)),
    )(h, target)


def _pick_col(x, idx):
    lane = lax.broadcasted_iota(jnp.int32, x.shape, 1)
    return jnp.sum(jnp.where(lane == idx, x, 0.0), axis=1, keepdims=True)


def _pick_row(x, idx):
    sub = lax.broadcasted_iota(jnp.int32, x.shape, 0)
    return jnp.sum(jnp.where(sub == idx, x, 0.0), axis=0, keepdims=True)


def _conv_taps(pad_ref, w, tm, base):
    acc = w[0:1, :] * pad_ref[base:base + tm, :]
    for k in range(1, 4):
        acc = acc + w[k:k + 1, :] * pad_ref[base + k:base + k + tm, :]
    return acc


def _ssd_chunk_common(adt_c, tri):
    cs = _dot_f32(tri, adt_c)
    return cs, cs.T, jnp.exp(cs)


def _ssd_head_terms(cs, cst, ecs, dt_c, h, tri):
    cs_col = _pick_col(cs, h)
    cs_row = _pick_row(cst, h)
    dt_col = _pick_col(dt_c, h)
    cs_last = cs_col[SSD_CHUNK - 1:SSD_CHUNK, :]
    lmat = jnp.exp(jnp.where(tri > 0.0, cs_col - cs_row, -1e30))
    ecs_col = _pick_col(ecs, h)
    decay_col = jnp.exp(cs_last - cs_col)
    return cs_col, dt_col, cs_last, lmat, ecs_col, decay_col


def _ssd_fwd(proj, cw, cb, dtb, a_neg, d_lanes, nw):
    t = proj.shape[0]
    tm = SSD_TM
    nt = t // tm
    ncq = tm // SSD_CHUNK
    hb = tm // SUBLANES

    def body(xbc_ref, halo_ref, z_ref, dt_ref, cw_ref, cb_ref, dtb_ref, a_ref, d_ref, nw_ref,
             y_ref, yy_ref, st_ref, xpad, xact, state):
        i = pl.program_id(0)

        @pl.when(i == 0)
        def _():
            state[...] = jnp.zeros_like(state)

        xpad[0:SUBLANES, :] = jnp.where(i > 0, halo_ref[...], 0.0)
        xpad[SUBLANES:SUBLANES + tm, :] = xbc_ref[...]
        acc = cb_ref[...] + _conv_taps(xpad, cw_ref[...], tm, SUBLANES - 3)
        xact[...] = acc * _sigmoid(acc)
        dt = _softplus(dt_ref[...] + dtb_ref[...])
        adt = dt * a_ref[...]
        r_i = lax.broadcasted_iota(jnp.int32, (SSD_CHUNK, SSD_CHUNK), 0)
        c_i = lax.broadcasted_iota(jnp.int32, (SSD_CHUNK, SSD_CHUNK), 1)
        tri = (r_i >= c_i).astype(F32)
        lane1 = lax.broadcasted_iota(jnp.int32, (1, LANES), 1)
        for c in range(ncq):
            sl = slice(c * SSD_CHUNK, (c + 1) * SSD_CHUNK)
            dt_c = dt[sl]
            cs, cst, ecs = _ssd_chunk_common(adt[sl], tri)
            for g in range(2):
                bg = xact[sl, 512 + g * 128:512 + (g + 1) * 128]
                cg = xact[sl, 768 + g * 128:768 + (g + 1) * 128]
                cbm = _dot_nt(cg, bg)
                for pr in range(2):
                    pi = g * 2 + pr
                    psl = slice(pi * 128, (pi + 1) * 128)
                    xp = xact[sl, psl]
                    prev = state[pi]
                    st_ref[c, pi] = prev
                    yp = xp * d_ref[:, psl]
                    new_s = jnp.zeros((SSD_STATE, LANES), F32)
                    dec_lane = jnp.zeros((1, LANES), F32)
                    for hh in range(2):
                        h = g * 4 + pr * 2 + hh
                        lm = (lane1 >= 64) if hh else (lane1 < 64)
                        _, dt_col, cs_last, lmat, ecs_col, decay_col = _ssd_head_terms(cs, cst, ecs, dt_c, h, tri)
                        xdt = jnp.where(lm, xp, 0.0) * dt_col
                        yp = yp + _dot(cbm * lmat, xdt)
                        yp = yp + _dot(cg * ecs_col, jnp.where(lm, prev, 0.0))
                        new_s = new_s + _dot_tn(bg * decay_col, xdt)
                        dec_lane = dec_lane + jnp.where(lm, jnp.exp(cs_last), 0.0)
                    state[pi] = prev * dec_lane + new_s
                    yy_ref[sl, psl] = yp
        yy = yy_ref[...]
        z = z_ref[...]
        yg = yy * (z * _sigmoid(z))
        ms = jnp.mean(yg * yg, axis=-1, keepdims=True)
        y_ref[...] = (yg * lax.rsqrt(ms + LN_EPS) * nw_ref[...]).astype(y_ref.dtype)

    halo_map = lambda i: (jnp.maximum(i * hb - 1, 0), 0)
    return pl.pallas_call(
        body, name="ssd_fwd", grid=(nt,),
        in_specs=[pl.BlockSpec((tm, SSD_XBC), lambda i: (i, 0)), pl.BlockSpec((SUBLANES, SSD_XBC), halo_map),
                  pl.BlockSpec((tm, SSD_WIDTH), lambda i: (i, P_Z // SSD_WIDTH)),
                  pl.BlockSpec((tm, LANES), lambda i: (i, P_DT // LANES)),
                  _const((4, SSD_XBC)), _const((1, SSD_XBC)), _const((1, LANES)), _const((1, LANES)),
                  _const((1, SSD_WIDTH)), _const((1, SSD_WIDTH))],
        out_specs=[_rows(tm, SSD_WIDTH), _rows(tm, SSD_WIDTH),
                   pl.BlockSpec((ncq, 4, SSD_STATE, LANES), lambda i: (i, 0, 0, 0))],
        out_shape=[jax.ShapeDtypeStruct((t, SSD_WIDTH), MXU_DTYPE), jax.ShapeDtypeStruct((t, SSD_WIDTH), F32),
                   jax.ShapeDtypeStruct((t // SSD_CHUNK, 4, SSD_STATE, LANES), F32)],
        scratch_shapes=[pltpu.VMEM((tm + SUBLANES, SSD_XBC), F32), pltpu.VMEM((tm, SSD_XBC), F32),
                        pltpu.VMEM((4, SSD_STATE, LANES), F32)],
        compiler_params=_cparams(("arbitrary",)),
    )(proj, proj, proj, proj, cw, cb, dtb, a_neg, d_lanes, nw)


def _ssd_bwd(dycat, proj, yy, states, cw, cb, dtb, a_neg, d_lanes, nw):
    t = proj.shape[0]
    tm = SSD_TM
    nt = t // tm
    ncq = tm // SSD_CHUNK
    hb = tm // SUBLANES

    def body(dy_ref, xbc_ref, halo_ref, z_ref, dt_ref, yy_ref, st_ref, cw_ref, cb_ref, dtb_ref, a_ref, d_ref, nw_ref,
             dxbc_ref, dz_ref, ddt_ref, dcw_ref, dcb_ref, ddtb_ref, da_ref, dd_ref, dnw_ref,
             xpad, xact, dxact, dpad, dstate, dnext):
        i = pl.program_id(0)

        @pl.when(i == 0)
        def _():
            for r in (dcw_ref, dcb_ref, ddtb_ref, da_ref, dd_ref, dnw_ref, dstate, dnext):
                r[...] = jnp.zeros_like(r)

        xpad[0:SUBLANES, :] = jnp.where(i < nt - 1, halo_ref[...], 0.0)
        xpad[SUBLANES:SUBLANES + tm, :] = xbc_ref[...]
        cw_v = cw_ref[...]
        acc = cb_ref[...] + _conv_taps(xpad, cw_v, tm, SUBLANES - 3)
        sig = _sigmoid(acc)
        xact[...] = acc * sig
        dt_raw = dt_ref[...] + dtb_ref[...]
        dt = _softplus(dt_raw)
        a_v = a_ref[...]
        adt = dt * a_v
        yy = yy_ref[...]
        z = z_ref[...]
        sz = _sigmoid(z)
        siluz = z * sz
        yg = yy * siluz
        ms = jnp.mean(yg * yg, axis=-1, keepdims=True)
        rinv = lax.rsqrt(ms + LN_EPS)
        dout = dy_ref[...]
        dnw_ref[...] += _sum0(dout * yg * rinv)
        dyn = dout * nw_ref[...]
        dyg = rinv * dyn - yg * (rinv * rinv * rinv) * jnp.mean(dyn * yg, axis=-1, keepdims=True)
        dyy = dyg * siluz
        dz_ref[...] = (dyg * yy * (sz * (1.0 + z * (1.0 - sz)))).astype(dz_ref.dtype)
        dd_ref[...] += _sum0(dyy * xact[:, 0:SSD_WIDTH])

        r_i = lax.broadcasted_iota(jnp.int32, (SSD_CHUNK, SSD_CHUNK), 0)
        c_i = lax.broadcasted_iota(jnp.int32, (SSD_CHUNK, SSD_CHUNK), 1)
        tri = (r_i >= c_i).astype(F32)
        lane1 = lax.broadcasted_iota(jnp.int32, (1, LANES), 1)
        for c in reversed(range(ncq)):
            sl = slice(c * SSD_CHUNK, (c + 1) * SSD_CHUNK)
            dt_c = dt[sl]
            cs, cst, ecs = _ssd_chunk_common(adt[sl], tri)
            cacc = jnp.zeros((SSD_CHUNK, LANES), F32)
            racc = jnp.zeros((SSD_CHUNK, LANES), F32)
            ddtx = jnp.zeros((SSD_CHUNK, LANES), F32)
            for g in range(2):
                bg = xact[sl, 512 + g * 128:512 + (g + 1) * 128]
                cg = xact[sl, 768 + g * 128:768 + (g + 1) * 128]
                cbm = _dot_nt(cg, bg)
                dcb_m = jnp.zeros((SSD_CHUNK, SSD_CHUNK), F32)
                dbg = jnp.zeros((SSD_CHUNK, SSD_STATE), F32)
                dcg = jnp.zeros((SSD_CHUNK, SSD_STATE), F32)
                for pr in range(2):
                    pi = g * 2 + pr
                    psl = slice(pi * 128, (pi + 1) * 128)
                    xp = xact[sl, psl]
                    dyp = dyy[sl, psl]
                    prev = st_ref[c, pi]
                    ds_all = dstate[pi]
                    dxdt_p = jnp.zeros((SSD_CHUNK, LANES), F32)
                    dprev_new = jnp.zeros((SSD_STATE, LANES), F32)
                    dec_lane = jnp.zeros((1, LANES), F32)
                    dt_lanes = jnp.zeros((SSD_CHUNK, LANES), F32)
                    for hh in range(2):
                        h = g * 4 + pr * 2 + hh
                        lm = (lane1 >= 64) if hh else (lane1 < 64)
                        oh_l = (c_i == h).astype(F32)
                        oh_s = (r_i == h).astype(F32)
                        _, dt_col, cs_last, lmat, ecs_col, decay_col = _ssd_head_terms(cs, cst, ecs, dt_c, h, tri)
                        gm = cbm * lmat
                        xm = jnp.where(lm, xp, 0.0)
                        xdt = xm * dt_col
                        dym = jnp.where(lm, dyp, 0.0)
                        prevm = jnp.where(lm, prev, 0.0)
                        dsm = jnp.where(lm, ds_all, 0.0)
                        bdec = bg * decay_col
                        dxdt = _dot_tn(gm, dym) + _dot(bdec, dsm)
                        dxdt_p = dxdt_p + dxdt
                        ddtx = ddtx + oh_l * jnp.sum(dxdt * xm, axis=1, keepdims=True)
                        dt_lanes = dt_lanes + jnp.where(lm, dt_col, 0.0)
                        dgm = _dot_nt(dym, xdt)
                        dcb_m = dcb_m + dgm * lmat
                        w = dgm * gm
                        cacc = cacc + oh_l * jnp.sum(w, axis=1, keepdims=True)
                        racc = racc - oh_s * jnp.sum(w, axis=0, keepdims=True)
                        dce = _dot_nt(dym, prevm)
                        dcg = dcg + dce * ecs_col
                        cacc = cacc + oh_l * (jnp.sum(dce * cg, axis=1, keepdims=True) * ecs_col)
                        dprev_new = dprev_new + _dot_tn(cg * ecs_col, dym)
                        dbdec = _dot_nt(xdt, dsm)
                        dbg = dbg + dbdec * decay_col
                        dd = jnp.sum(dbdec * bg, axis=1, keepdims=True) * decay_col
                        cacc = cacc - oh_l * dd
                        cd = jnp.exp(cs_last)
                        dlast = jnp.sum(dd, axis=0, keepdims=True) + jnp.sum(
                            jnp.sum(dsm * prevm, axis=1, keepdims=True), axis=0, keepdims=True) * cd
                        cacc = cacc + jnp.where((r_i == SSD_CHUNK - 1) & (c_i == h), dlast, 0.0)
                        dec_lane = dec_lane + jnp.where(lm, cd, 0.0)
                    dstate[pi] = ds_all * dec_lane + dprev_new
                    dxact[sl, psl] = dxdt_p * dt_lanes + dyp * d_ref[:, psl]
                dcg = dcg + _dot(dcb_m, bg)
                dbg = dbg + _dot_tn(dcb_m, cg)
                dxact[sl, 512 + g * 128:512 + (g + 1) * 128] = dbg
                dxact[sl, 768 + g * 128:768 + (g + 1) * 128] = dcg
            dcs = cacc + racc.T
            dadt = _dot_f32((r_i <= c_i).astype(F32), dcs)
            ddt = dadt * a_v + ddtx
            da_ref[...] += _sum0(dadt * dt_c)
            ddt_raw = ddt * _sigmoid(dt_raw[sl])
            ddt_ref[sl, :] = ddt_raw.astype(ddt_ref.dtype)
            ddtb_ref[...] += _sum0(ddt_raw)
        dacc = dxact[...] * (sig * (1.0 + acc * (1.0 - sig)))
        dcb_ref[...] += _sum0(dacc)
        for k in range(4):
            dcw_ref[k:k + 1, :] += _sum0(dacc * xpad[SUBLANES - 3 + k:SUBLANES - 3 + k + tm, :])
        dpad[0:tm, :] = dacc
        dpad[tm:tm + SUBLANES, :] = dnext[...]
        dx = cw_v[0:1, :] * dpad[3:3 + tm, :]
        for k in range(1, 4):
            dx = dx + cw_v[k:k + 1, :] * dpad[3 - k:3 - k + tm, :]
        dxbc_ref[...] = dx.astype(dxbc_ref.dtype)
        dnext[...] = dacc[0:SUBLANES, :]

    rev = lambda i: nt - 1 - i
    halo_map = lambda i: (jnp.maximum(rev(i) * hb - 1, 0), 0)
    rrow = lambda n, col=0: pl.BlockSpec((tm, n), lambda i: (rev(i), col))
    return pl.pallas_call(
        body, name="ssd_bwd", grid=(nt,),
        in_specs=[rrow(SSD_WIDTH), rrow(SSD_XBC), pl.BlockSpec((SUBLANES, SSD_XBC), halo_map),
                  rrow(SSD_WIDTH, P_Z // SSD_WIDTH), rrow(LANES, P_DT // LANES), rrow(SSD_WIDTH),
                  pl.BlockSpec((ncq, 4, SSD_STATE, LANES), lambda i: (rev(i), 0, 0, 0)),
                  _const((4, SSD_XBC)), _const((1, SSD_XBC)), _const((1, LANES)), _const((1, LANES)),
                  _const((1, SSD_WIDTH)), _const((1, SSD_WIDTH))],
        out_specs=[rrow(SSD_XBC), rrow(SSD_WIDTH), rrow(LANES), _const((SUBLANES, SSD_XBC)), _const((1, SSD_XBC)),
                   _const((1, LANES)), _const((1, LANES)), _const((1, SSD_WIDTH)), _const((1, SSD_WIDTH))],
        out_shape=[jax.ShapeDtypeStruct((t, SSD_XBC), MXU_DTYPE), jax.ShapeDtypeStruct((t, SSD_WIDTH), MXU_DTYPE),
                   jax.ShapeDtypeStruct((t, LANES), MXU_DTYPE), jax.ShapeDtypeStruct((SUBLANES, SSD_XBC), F32),
                   jax.ShapeDtypeStruct((1, SSD_XBC), F32), jax.ShapeDtypeStruct((1, LANES), F32),
                   jax.ShapeDtypeStruct((1, LANES), F32), jax.ShapeDtypeStruct((1, SSD_WIDTH), F32),
                   jax.ShapeDtypeStruct((1, SSD_WIDTH), F32)],
        scratch_shapes=[pltpu.VMEM((tm + SUBLANES, SSD_XBC), F32), pltpu.VMEM((tm, SSD_XBC), F32),
                        pltpu.VMEM((tm, SSD_XBC), F32), pltpu.VMEM((tm + SUBLANES, SSD_XBC), F32),
                        pltpu.VMEM((4, SSD_STATE, LANES), F32), pltpu.VMEM((SUBLANES, SSD_XBC), F32)],
        compiler_params=_cparams(("arbitrary",)),
    )(dycat, proj, proj, proj, proj, yy, states, cw, cb, dtb, a_neg, d_lanes, nw)


def _cmul_add(ar, ai, br, bi, cr, ci):
    return ar + br * cr - bi * ci, ai + br * ci + bi * cr


def _s5_fwd(proj, bre, bim, cre, cim, d_skip, glu_w, glu_b, coef):
    t = proj.shape[0]
    tm = SCAN_TM
    ng = tm // SUBLANES

    def body(u_ref, bre_ref, bim_ref, cre_ref, cim_ref, d_ref, w_ref, b_ref, coef_ref,
             y_ref, y2_ref, hre_ref, him_ref, carry):
        i = pl.program_id(0)

        @pl.when(i == 0)
        def _():
            carry[...] = jnp.zeros_like(carry)

        u = u_ref[...]
        hre_ref[...] = _dot(u, bre_ref[...])
        him_ref[...] = _dot(u, bim_ref[...])

        def step(gi, car):
            cr_, ci_ = car
            rows = pl.ds(pl.multiple_of(gi * SUBLANES, SUBLANES), SUBLANES)
            r = hre_ref[rows, :]
            m = him_ref[rows, :]
            for k, sh in enumerate((1, 2, 4)):
                r, m = _cmul_add(r, m, coef_ref[k, 0], coef_ref[k, 1], pltpu.roll(r, sh, 0), pltpu.roll(m, sh, 0))
            r, m = _cmul_add(r, m, coef_ref[3, 0], coef_ref[3, 1], cr_, ci_)
            hre_ref[rows, :] = r
            him_ref[rows, :] = m
            return (jnp.broadcast_to(r[SUBLANES - 1:SUBLANES, :], r.shape),
                    jnp.broadcast_to(m[SUBLANES - 1:SUBLANES, :], m.shape))

        cr_, ci_ = lax.fori_loop(0, ng, step, (carry[0], carry[1]))
        carry[0] = cr_
        carry[1] = ci_
        y2 = _dot(hre_ref[...], cre_ref[...]) - _dot(him_ref[...], cim_ref[...]) + d_ref[...] * u
        y2_ref[...] = y2
        ya = _gelu(y2)
        y_ref[...] = (ya * _sigmoid(_dot(ya, w_ref[...]) + b_ref[...])).astype(y_ref.dtype)

    return pl.pallas_call(
        body, name="s5_fwd", grid=(t // tm,),
        in_specs=[pl.BlockSpec((tm, S5_WIDTH), lambda i: (i, P_U // S5_WIDTH)),
                  _const((S5_WIDTH, S5_NSTATE)), _const((S5_WIDTH, S5_NSTATE)), _const((S5_NSTATE, S5_WIDTH)),
                  _const((S5_NSTATE, S5_WIDTH)), _const((1, S5_WIDTH)), _const((S5_WIDTH, S5_WIDTH)),
                  _const((1, S5_WIDTH)), _const((5, 2, SUBLANES, S5_NSTATE))],
        out_specs=[_rows(tm, S5_WIDTH), _rows(tm, S5_WIDTH), _rows(tm, S5_NSTATE), _rows(tm, S5_NSTATE)],
        out_shape=[jax.ShapeDtypeStruct((t, S5_WIDTH), MXU_DTYPE), jax.ShapeDtypeStruct((t, S5_WIDTH), F32),
                   jax.ShapeDtypeStruct((t, S5_NSTATE), F32), jax.ShapeDtypeStruct((t, S5_NSTATE), F32)],
        scratch_shapes=[pltpu.VMEM((2, SUBLANES, S5_NSTATE), F32)],
        compiler_params=_cparams(("arbitrary",)),
    )(proj, bre, bim, cre, cim, d_skip, glu_w, glu_b, coef)


def _s5_bwd(dycat, proj, y2, hre, him, bre, bim, cre, cim, d_skip, glu_w, glu_b, rcoef):
    t = proj.shape[0]
    tm = SCAN_TM
    nt = t // tm
    ng = tm // SUBLANES
    hb = tm // SUBLANES

    def body(dy_ref, u_ref, y2_ref, hre_ref, him_ref, hre_halo, him_halo, bre_ref, bim_ref, cre_ref, cim_ref, d_ref,
             w_ref, b_ref, coef_ref,
             du_ref, dbre_ref, dbim_ref, dcre_ref, dcim_ref, dlam_ref, dd_ref, dw_ref, dgb_ref,
             gre, gim, hpre, hpim, carry):
        i = pl.program_id(0)

        @pl.when(i == 0)
        def _():
            for r in (dbre_ref, dbim_ref, dcre_ref, dcim_ref, dlam_ref, dd_ref, dw_ref, dgb_ref, carry):
                r[...] = jnp.zeros_like(r)

        u = u_ref[...]
        y2 = y2_ref[...]
        dout = dy_ref[...]
        ya = _gelu(y2)
        sg = _sigmoid(_dot(ya, w_ref[...]) + b_ref[...])
        dv = dout * ya * sg * (1.0 - sg)
        dya = dout * sg + _dot_nt(dv, w_ref[...])
        dw_ref[...] += _dot_tn(ya, dv)
        dgb_ref[...] += _sum0(dv)
        dy2 = dya * _gelu_grad(y2)
        dd_ref[...] += _sum0(dy2 * u)
        hre_v = hre_ref[...]
        him_v = him_ref[...]
        dcre_ref[...] += _dot_tn(hre_v, dy2)
        dcim_ref[...] -= _dot_tn(him_v, dy2)
        gre[...] = _dot_nt(dy2, cre_ref[...])
        gim[...] = -_dot_nt(dy2, cim_ref[...])
        first = i == nt - 1
        hpre[0:SUBLANES, :] = jnp.where(first, 0.0, hre_halo[...])
        hpim[0:SUBLANES, :] = jnp.where(first, 0.0, him_halo[...])
        hpre[SUBLANES:SUBLANES + tm, :] = hre_v
        hpim[SUBLANES:SUBLANES + tm, :] = him_v
        row0 = lax.broadcasted_iota(jnp.int32, (SUBLANES, S5_NSTATE), 0) == 0

        def step(k, car):
            cr_, ci_, dlr, dli = car
            gi = ng - 1 - k
            rows = pl.ds(pl.multiple_of(gi * SUBLANES, SUBLANES), SUBLANES)
            nrows = pl.ds(pl.multiple_of(gi * SUBLANES + SUBLANES, SUBLANES), SUBLANES)
            r = gre[rows, :]
            m = gim[rows, :]
            for kk, sh in enumerate((1, 2, 4)):
                r, m = _cmul_add(r, m, coef_ref[kk, 0], coef_ref[kk, 1], pltpu.roll(r, SUBLANES - sh, 0),
                                 pltpu.roll(m, SUBLANES - sh, 0))
            r, m = _cmul_add(r, m, coef_ref[3, 0], coef_ref[3, 1], cr_, ci_)
            gre[rows, :] = r
            gim[rows, :] = m
            pr_ = hpre[rows, :]
            pm_ = hpim[rows, :]
            hr_ = jnp.where(row0, jnp.broadcast_to(pr_[SUBLANES - 1:SUBLANES, :], pr_.shape),
                            pltpu.roll(hpre[nrows, :], 1, 0))
            hm_ = jnp.where(row0, jnp.broadcast_to(pm_[SUBLANES - 1:SUBLANES, :], pm_.shape),
                            pltpu.roll(hpim[nrows, :], 1, 0))
            dlr = dlr + hr_ * r + hm_ * m
            dli = dli + hr_ * m - hm_ * r
            return (jnp.broadcast_to(r[0:1, :], r.shape), jnp.broadcast_to(m[0:1, :], m.shape), dlr, dli)

        z8 = jnp.zeros((SUBLANES, S5_NSTATE), F32)
        cr_, ci_, dlr, dli = lax.fori_loop(0, ng, step, (carry[0], carry[1], z8, z8))
        carry[0] = cr_
        carry[1] = ci_
        dlam_ref[0] += dlr
        dlam_ref[1] += dli
        g_re = gre[...]
        g_im = gim[...]
        du_ref[...] = (dy2 * d_ref[...] + _dot_nt(g_re, bre_ref[...]) + _dot_nt(g_im, bim_ref[...])
                       ).astype(du_ref.dtype)
        dbre_ref[...] += _dot_tn(u, g_re)
        dbim_ref[...] += _dot_tn(u, g_im)

    rev = lambda i: nt - 1 - i
    rrow = lambda n, col=0: pl.BlockSpec((tm, n), lambda i: (rev(i), col))
    halo = pl.BlockSpec((SUBLANES, S5_NSTATE), lambda i: (jnp.maximum(rev(i) * hb - 1, 0), 0))
    return pl.pallas_call(
        body, name="s5_bwd", grid=(nt,),
        in_specs=[rrow(S5_WIDTH, 512 // S5_WIDTH), rrow(S5_WIDTH, P_U // S5_WIDTH), rrow(S5_WIDTH),
                  rrow(S5_NSTATE), rrow(S5_NSTATE), halo, halo,
                  _const((S5_WIDTH, S5_NSTATE)), _const((S5_WIDTH, S5_NSTATE)), _const((S5_NSTATE, S5_WIDTH)),
                  _const((S5_NSTATE, S5_WIDTH)), _const((1, S5_WIDTH)), _const((S5_WIDTH, S5_WIDTH)),
                  _const((1, S5_WIDTH)), _const((5, 2, SUBLANES, S5_NSTATE))],
        out_specs=[rrow(S5_WIDTH), _const((S5_WIDTH, S5_NSTATE)), _const((S5_WIDTH, S5_NSTATE)),
                   _const((S5_NSTATE, S5_WIDTH)), _const((S5_NSTATE, S5_WIDTH)), _const((2, SUBLANES, S5_NSTATE)),
                   _const((1, S5_WIDTH)), _const((S5_WIDTH, S5_WIDTH)), _const((1, S5_WIDTH))],
        out_shape=[jax.ShapeDtypeStruct((t, S5_WIDTH), MXU_DTYPE), jax.ShapeDtypeStruct((S5_WIDTH, S5_NSTATE), F32),
                   jax.ShapeDtypeStruct((S5_WIDTH, S5_NSTATE), F32), jax.ShapeDtypeStruct((S5_NSTATE, S5_WIDTH), F32),
                   jax.ShapeDtypeStruct((S5_NSTATE, S5_WIDTH), F32),
                   jax.ShapeDtypeStruct((2, SUBLANES, S5_NSTATE), F32), jax.ShapeDtypeStruct((1, S5_WIDTH), F32),
                   jax.ShapeDtypeStruct((S5_WIDTH, S5_WIDTH), F32), jax.ShapeDtypeStruct((1, S5_WIDTH), F32)],
        scratch_shapes=[pltpu.VMEM((tm, S5_NSTATE), F32), pltpu.VMEM((tm, S5_NSTATE), F32),
                        pltpu.VMEM((tm + SUBLANES, S5_NSTATE), F32), pltpu.VMEM((tm + SUBLANES, S5_NSTATE), F32),
                        pltpu.VMEM((2, SUBLANES, S5_NSTATE), F32)],
        compiler_params=_cparams(("arbitrary",)),
    )(dycat, proj, y2, hre, him, hre, him, bre, bim, cre, cim, d_skip, glu_w, glu_b, rcoef)


def _rg_gates(xc, wa, ba, wx, bx, nsp):
    r = _sigmoid(_dot(xc, wa) + ba)
    ig = _sigmoid(_dot(xc, wx) + bx)
    log_a = nsp * r
    a = jnp.exp(log_a)
    mult = jnp.sqrt(-_expm1(2.0 * log_a))
    return r, ig, a, mult


def _rg_fwd(proj, cw, cb, wa, ba, wx, bx, nsp):
    t = proj.shape[0]
    tm = SCAN_TM
    ng = tm // SUBLANES
    hb = tm // SUBLANES

    def body(x_ref, halo_ref, gt_ref, cw_ref, cb_ref, wa_ref, ba_ref, wx_ref, bx_ref, nsp_ref,
             y_ref, h_ref, xpad, abuf, carry):
        i = pl.program_id(0)

        @pl.when(i == 0)
        def _():
            carry[...] = jnp.zeros_like(carry)

        xpad[0:SUBLANES, :] = jnp.where(i > 0, halo_ref[...], 0.0)
        xpad[SUBLANES:SUBLANES + tm, :] = x_ref[...]
        xc = cb_ref[...] + _conv_taps(xpad, cw_ref[...], tm, SUBLANES - 3)
        _, ig, a, mult = _rg_gates(xc, wa_ref[...], ba_ref[...], wx_ref[...], bx_ref[...], nsp_ref[...])
        abuf[...] = a
        h_ref[...] = mult * (ig * xc)
        sub = lax.broadcasted_iota(jnp.int32, (SUBLANES, RG_WIDTH), 0)

        def step(gi, car):
            rows = pl.ds(pl.multiple_of(gi * SUBLANES, SUBLANES), SUBLANES)
            av = abuf[rows, :]
            bv = h_ref[rows, :]
            for sh in (1, 2, 4):
                m = sub >= sh
                bv = jnp.where(m, av * pltpu.roll(bv, sh, 0) + bv, bv)
                av = jnp.where(m, av * pltpu.roll(av, sh, 0), av)
            hv = bv + av * car
            h_ref[rows, :] = hv
            return jnp.broadcast_to(hv[SUBLANES - 1:SUBLANES, :], hv.shape)

        carry[...] = lax.fori_loop(0, ng, step, carry[...])
        y_ref[...] = (h_ref[...] * _gelu(gt_ref[...])).astype(y_ref.dtype)

    return pl.pallas_call(
        body, name="rg_fwd", grid=(t // tm,),
        in_specs=[pl.BlockSpec((tm, RG_WIDTH), lambda i: (i, P_XRG // RG_WIDTH)),
                  pl.BlockSpec((SUBLANES, RG_WIDTH), lambda i: (jnp.maximum(i * hb - 1, 0), P_XRG // RG_WIDTH)),
                  pl.BlockSpec((tm, RG_WIDTH), lambda i: (i, P_GRG // RG_WIDTH)),
                  _const((4, RG_WIDTH)), _const((1, RG_WIDTH)), _const((RG_WIDTH, RG_WIDTH)), _const((1, RG_WIDTH)),
                  _const((RG_WIDTH, RG_WIDTH)), _const((1, RG_WIDTH)), _const((1, RG_WIDTH))],
        out_specs=[_rows(tm, RG_WIDTH), _rows(tm, RG_WIDTH)],
        out_shape=[jax.ShapeDtypeStruct((t, RG_WIDTH), MXU_DTYPE), jax.ShapeDtypeStruct((t, RG_WIDTH), F32)],
        scratch_shapes=[pltpu.VMEM((tm + SUBLANES, RG_WIDTH), F32), pltpu.VMEM((tm, RG_WIDTH), F32),
                        pltpu.VMEM((SUBLANES, RG_WIDTH), F32)],
        compiler_params=_cparams(("arbitrary",)),
    )(proj, proj, proj, cw, cb, wa, ba, wx, bx, nsp)


def _rg_bwd(dycat, proj, hs, cw, cb, wa, ba, wx, bx, nsp):
    t = proj.shape[0]
    tm = SCAN_TM
    nt = t // tm
    ng = tm // SUBLANES
    hb = tm // SUBLANES

    def body(dy_ref, x_ref, halo_ref, gt_ref, h_ref, h_halo, cw_ref, cb_ref, wa_ref, ba_ref, wx_ref, bx_ref, nsp_ref,
             dx_ref, dgt_ref, dcw_ref, dcb_ref, dwa_ref, dba_ref, dwx_ref, dbx_ref, dnsp_ref,
             xpad, abuf, gbuf, hpad, dabuf, dpad, carry, dnext):
        i = pl.program_id(0)

        @pl.when(i == 0)
        def _():
            for r in (dcw_ref, dcb_ref, dwa_ref, dba_ref, dwx_ref, dbx_ref, dnsp_ref, carry, dnext):
                r[...] = jnp.zeros_like(r)

        first = i == nt - 1
        xpad[0:SUBLANES, :] = jnp.where(first, 0.0, halo_ref[...])
        xpad[SUBLANES:SUBLANES + tm, :] = x_ref[...]
        cw_v = cw_ref[...]
        xc = cb_ref[...] + _conv_taps(xpad, cw_v, tm, SUBLANES - 3)
        nsp_v = nsp_ref[...]
        r, ig, a, mult = _rg_gates(xc, wa_ref[...], ba_ref[...], wx_ref[...], bx_ref[...], nsp_v)
        abuf[...] = a
        hv = h_ref[...]
        hpad[0:SUBLANES, :] = jnp.where(first, 0.0, h_halo[...])
        hpad[SUBLANES:SUBLANES + tm, :] = hv
        gt = gt_ref[...]
        dout = dy_ref[...]
        dgt_ref[...] = (dout * hv * _gelu_grad(gt)).astype(dgt_ref.dtype)
        gbuf[...] = dout * _gelu(gt)
        sub = lax.broadcasted_iota(jnp.int32, (SUBLANES, RG_WIDTH), 0)
        last_row = sub == SUBLANES - 1
        row0 = sub == 0

        def step(k, car):
            gi = ng - 1 - k
            rows = pl.ds(pl.multiple_of(gi * SUBLANES, SUBLANES), SUBLANES)
            nrows = pl.ds(pl.multiple_of(gi * SUBLANES + SUBLANES, SUBLANES), SUBLANES)
            av = abuf[rows, :]
            bv = gbuf[rows, :] + jnp.where(last_row, car, 0.0)
            ev = jnp.where(last_row, 0.0, pltpu.roll(av, SUBLANES - 1, 0))
            for sh in (1, 2, 4):
                m = sub < SUBLANES - sh
                bv = jnp.where(m, bv + ev * pltpu.roll(bv, SUBLANES - sh, 0), bv)
                ev = jnp.where(m, ev * pltpu.roll(ev, SUBLANES - sh, 0), 0.0)
            gbuf[rows, :] = bv
            pv = hpad[rows, :]
            hprev = jnp.where(row0, jnp.broadcast_to(pv[SUBLANES - 1:SUBLANES, :], pv.shape),
                              pltpu.roll(hpad[nrows, :], 1, 0))
            dabuf[rows, :] = bv * hprev
            return jnp.broadcast_to((av * bv)[0:1, :], bv.shape)

        carry[...] = lax.fori_loop(0, ng, step, carry[...])
        gv = gbuf[...]
        da = dabuf[...]
        ix = ig * xc
        dmult = gv * ix
        dig = gv * mult * xc
        dxc = gv * mult * ig
        dlog_a = da * a - dmult * (a * a) / mult
        dnsp_ref[...] += _sum0(dlog_a * r)
        dpr = dlog_a * nsp_v * r * (1.0 - r)
        dpi = dig * ig * (1.0 - ig)
        dxc = dxc + _dot_nt(dpr, wa_ref[...]) + _dot_nt(dpi, wx_ref[...])
        dwa_ref[...] += _dot_tn(xc, dpr)
        dwx_ref[...] += _dot_tn(xc, dpi)
        dba_ref[...] += _sum0(dpr)
        dbx_ref[...] += _sum0(dpi)
        dcb_ref[...] += _sum0(dxc)
        for k in range(4):
            dcw_ref[k:k + 1, :] += _sum0(dxc * xpad[SUBLANES - 3 + k:SUBLANES - 3 + k + tm, :])
        dpad[0:tm, :] = dxc
        dpad[tm:tm + SUBLANES, :] = dnext[...]
        dx = cw_v[0:1, :] * dpad[3:3 + tm, :]
        for k in range(1, 4):
            dx = dx + cw_v[k:k + 1, :] * dpad[3 - k:3 - k + tm, :]
        dx_ref[...] = dx.astype(dx_ref.dtype)
        dnext[...] = dxc[0:SUBLANES, :]

    rev = lambda i: nt - 1 - i
    rrow = lambda n, col=0: pl.BlockSpec((tm, n), lambda i: (rev(i), col))
    sq = _const((RG_WIDTH, RG_WIDTH))
    vec = _const((1, RG_WIDTH))
    return pl.pallas_call(
        body, name="rg_bwd", grid=(nt,),
        in_specs=[rrow(RG_WIDTH, 768 // RG_WIDTH), rrow(RG_WIDTH, P_XRG // RG_WIDTH),
                  pl.BlockSpec((SUBLANES, RG_WIDTH), lambda i: (jnp.maximum(rev(i) * hb - 1, 0), P_XRG // RG_WIDTH)),
                  rrow(RG_WIDTH, P_GRG // RG_WIDTH), rrow(RG_WIDTH),
                  pl.BlockSpec((SUBLANES, RG_WIDTH), lambda i: (jnp.maximum(rev(i) * hb - 1, 0), 0)),
                  _const((4, RG_WIDTH)), vec, sq, vec, sq, vec, vec],
        out_specs=[rrow(RG_WIDTH), rrow(RG_WIDTH), _const((SUBLANES, RG_WIDTH)), vec, sq, vec, sq, vec, vec],
        out_shape=[jax.ShapeDtypeStruct((t, RG_WIDTH), MXU_DTYPE), jax.ShapeDtypeStruct((t, RG_WIDTH), MXU_DTYPE),
                   jax.ShapeDtypeStruct((SUBLANES, RG_WIDTH), F32), jax.ShapeDtypeStruct((1, RG_WIDTH), F32),
                   jax.ShapeDtypeStruct((RG_WIDTH, RG_WIDTH), F32), jax.ShapeDtypeStruct((1, RG_WIDTH), F32),
                   jax.ShapeDtypeStruct((RG_WIDTH, RG_WIDTH), F32), jax.ShapeDtypeStruct((1, RG_WIDTH), F32),
                   jax.ShapeDtypeStruct((1, RG_WIDTH), F32)],
        scratch_shapes=[pltpu.VMEM((tm + SUBLANES, RG_WIDTH), F32), pltpu.VMEM((tm, RG_WIDTH), F32),
                        pltpu.VMEM((tm, RG_WIDTH), F32), pltpu.VMEM((tm + SUBLANES, RG_WIDTH), F32),
                        pltpu.VMEM((tm, RG_WIDTH), F32), pltpu.VMEM((tm + SUBLANES, RG_WIDTH), F32),
                        pltpu.VMEM((SUBLANES, RG_WIDTH), F32), pltpu.VMEM((SUBLANES, RG_WIDTH), F32)],
        compiler_params=_cparams(("arbitrary",)),
    )(dycat, proj, proj, proj, hs, hs, cw, cb, wa, ba, wx, bx, nsp)


def _block_diag(blocks):
    g, a, b = blocks.shape
    eye = jnp.eye(g, dtype=blocks.dtype)
    return (eye[:, None, :, None] * blocks[:, :, None, :]).reshape(g * a, g * b)


def _block_diag_extract(m, g):
    a, b = m.shape[0] // g, m.shape[1] // g
    m4 = m.reshape(g, a, g, b)
    idx = jnp.arange(g)
    return m4[idx, :, idx, :]


def _s5_prepare(lam_re, lam_im, log_step, b_re, b_im, c_re, c_im):
    step = jnp.exp(log_step)[:, None]
    mag = jnp.exp(lam_re * step)
    lbr = mag * jnp.cos(lam_im * step)
    lbi = mag * jnp.sin(lam_im * step)
    nr, ni = lbr - 1.0, lbi
    den = lam_re * lam_re + lam_im * lam_im
    cr = (nr * lam_re + ni * lam_im) / den
    ci = (ni * lam_re - nr * lam_im) / den
    bbr = cr[..., None] * b_re - ci[..., None] * b_im
    bbi = cr[..., None] * b_im + ci[..., None] * b_re
    bre = _block_diag(jnp.swapaxes(bbr, 1, 2))
    bim = _block_diag(jnp.swapaxes(bbi, 1, 2))
    cre = _block_diag(jnp.swapaxes(c_re, 1, 2))
    cim = _block_diag(jnp.swapaxes(c_im, 1, 2))
    return lbr.reshape(-1), lbi.reshape(-1), bre, bim, cre, cim


def _s5_scan_coef(lbr, lbi, reverse):
    if reverse:
        lbi = -lbi
    pr, pi = [lbr], [lbi]
    for _ in range(7):
        pr, pi = pr + [pr[-1] * lbr - pi[-1] * lbi], pi + [pr[-1] * lbi + pi[-1] * lbr]
    row = jnp.arange(SUBLANES)[:, None]
    tabs = []
    for sh in (1, 2, 4):
        keep = (row < SUBLANES - sh) if reverse else (row >= sh)
        tabs.append(jnp.stack([jnp.where(keep, pr[sh - 1][None, :], 0.0), jnp.where(keep, pi[sh - 1][None, :], 0.0)]))
    powr = jnp.stack(pr)
    powi = jnp.stack(pi)
    if reverse:
        powr, powi = powr[::-1], powi[::-1]
    tabs.append(jnp.stack([powr, powi]))
    tabs.append(jnp.zeros_like(tabs[-1]))
    return jnp.stack(tabs).astype(F32)


def _xy_peers():
    x, y, c = lax.axis_index("x"), lax.axis_index("y"), lax.axis_index("c")
    return x, y, c, [(1 - x, y), (x, 1 - y), (1 - x, 1 - y)]


def _hbm():
    return pl.BlockSpec(memory_space=pl.ANY)


def _xy_allgather(buf, *, name):
    n, w = buf.shape

    def body(x_ref, out_ref, send_sems, recv_sems, local_sem):
        x, y, c, peers = _xy_peers()
        me = 2 * x + y
        own = pltpu.make_async_copy(x_ref, out_ref.at[me], local_sem)
        own.start()
        sends = []
        for k, (px, py) in enumerate(peers):
            cp = pltpu.make_async_remote_copy(src_ref=x_ref, dst_ref=out_ref.at[me], send_sem=send_sems.at[k],
                                              recv_sem=recv_sems.at[k], device_id=(px, py, c), device_id_type=MESH)
            cp.start()
            sends.append(cp)
        for k, (px, py) in enumerate(peers):
            pltpu.make_async_remote_copy(src_ref=x_ref, dst_ref=out_ref.at[2 * px + py], send_sem=send_sems.at[k],
                                         recv_sem=recv_sems.at[k], device_id=(px, py, c),
                                         device_id_type=MESH).wait_recv()
        for cp in sends:
            cp.wait_send()
        own.wait()

    return pl.pallas_call(
        body, name=name, in_specs=[_hbm()], out_specs=_hbm(),
        out_shape=jax.ShapeDtypeStruct((4, n, w), buf.dtype),
        scratch_shapes=[pltpu.SemaphoreType.DMA((3,)), pltpu.SemaphoreType.DMA((3,)), pltpu.SemaphoreType.DMA],
    )(buf)


def _remote(src, dst, send_sem, recv_sem, dev):
    return pltpu.make_async_remote_copy(src_ref=src, dst_ref=dst, send_sem=send_sem, recv_sem=recv_sem,
                                        device_id=dev, device_id_type=MESH)


LAYER_GATHERED = (
    ("ssd_conv_w", (4, 256), 1), ("rg_conv_w", (4, LANES), 1),
    ("w_in", (1024, W_IN_PAD), 1), ("s5_glu_w", (64, 256), 0), ("w_out", (256, 1024), 0), ("xa_wq", (256, 1024), 0),
    ("xa_wk", (256, 1024), 0), ("xa_wv", (256, 1024), 0), ("xa_wo", (256, 1024), 0), ("mlp_w1", (1024, 1024), 1),
    ("mlp_w2", (1024, 1024), 0),
)
N_GATHERED = len(LAYER_GATHERED)
WAIT_GROUPS = ((0, 1, 2, 3), (4,), (5, 6, 7, 8), (9, 10))
RG_CONV_SHARD = RG_WIDTH // 4
N_GATHER_COPIES = 3 * N_GATHERED * DEPTH


def _gather_part(ref, t, pos):
    _, shp, ax = LAYER_GATHERED[t % N_GATHERED]
    idx = tuple(pl.ds(pos * shp[ax], shp[ax]) if d == ax else slice(None) for d in range(len(shp)))
    return ref.at[idx]


def _gather_start(shards):
    n = len(shards)
    lands = []
    for t, s in enumerate(shards):
        _, shp, ax = LAYER_GATHERED[t % N_GATHERED]
        full = shp[:ax] + (4 * shp[ax],) + shp[ax + 1:]
        lands.append(pltpu.with_memory_space_constraint(lax.empty(full, s.dtype), pltpu.HBM))

    def body(*refs):
        srcs, lnds = refs[:n], refs[n:2 * n]
        send_sems, recv_sems, local_sems = refs[2 * n:2 * n + 3]
        token = refs[-1]
        x, y, c, peers = _xy_peers()
        me = 2 * x + y
        for t in range(n):
            for k, (px, py) in enumerate(peers):
                _remote(srcs[t], _gather_part(lnds[t], t, me), send_sems.at[k * n + t], recv_sems.at[k * n + t],
                        (px, py, c)).start()
            pltpu.make_async_copy(srcs[t], _gather_part(lnds[t], t, me), local_sems.at[t]).start()
        token[...] = jnp.zeros_like(token)

    hbm = pl.BlockSpec(memory_space=pltpu.HBM)
    sem = pl.BlockSpec(memory_space=pltpu.SEMAPHORE)
    outs = pl.pallas_call(
        body, name="weights_gather_start", in_specs=[hbm] * (2 * n),
        out_shape=(pltpu.SemaphoreType.DMA((3 * n,)), pltpu.SemaphoreType.DMA((3 * n,)),
                   pltpu.SemaphoreType.DMA((n,)),
                   *[pltpu.HBM(s.shape, s.dtype) for s in shards], *[pltpu.HBM(a.shape, a.dtype) for a in lands],
                   jax.ShapeDtypeStruct((SUBLANES, LANES), F32)),
        out_specs=(sem, sem, sem, *[hbm] * (2 * n), pl.BlockSpec(memory_space=pltpu.VMEM)),
        input_output_aliases={i: 3 + i for i in range(2 * n)},
        compiler_params=pltpu.CompilerParams(has_side_effects=pltpu.SideEffectType.DATAFLOW_SIDE_EFFECTING),
    )(*[pltpu.with_memory_space_constraint(s, pltpu.HBM) for s in shards], *lands)
    return outs[0], outs[1], outs[2], outs[3:3 + n], outs[3 + n:3 + 2 * n], outs[-1]


def _gather_wait(handle, ts, after, *, name):
    send_sems, recv_sems, local_sems, src_thru, land_thru, _ = handle
    n = len(src_thru)
    m = len(ts)

    def body(*refs):
        srcs, lnds = refs[:m], refs[m:2 * m]
        ssem, rsem, lsem = refs[2 * m:2 * m + 3]
        x, y, c, peers = _xy_peers()
        me = 2 * x + y
        for i, t in enumerate(ts):
            for k, (px, py) in enumerate(peers):
                cp = _remote(srcs[i], _gather_part(lnds[i], t, 2 * px + py), ssem.at[k * n + t], rsem.at[k * n + t],
                             (px, py, c))
                cp.wait_send()
                cp.wait_recv()
            pltpu.make_async_copy(srcs[i], _gather_part(lnds[i], t, me), lsem.at[t]).wait()

    hbm = pl.BlockSpec(memory_space=pltpu.HBM)
    sem = pl.BlockSpec(memory_space=pltpu.SEMAPHORE)
    args = [src_thru[t] for t in ts] + [land_thru[t] for t in ts]
    outs = pl.pallas_call(
        body, name=name, in_specs=[hbm] * (2 * m) + [sem, sem, sem, pl.BlockSpec(memory_space=pl.ANY)],
        out_shape=[pltpu.HBM(a.shape, a.dtype) for a in args], out_specs=[hbm] * (2 * m),
        input_output_aliases={i: i for i in range(2 * m)},
        compiler_params=pltpu.CompilerParams(has_side_effects=pltpu.SideEffectType.DATAFLOW_SIDE_EFFECTING),
    )(*args, send_sems, recv_sems, local_sems, after)
    return outs[:m], outs[m:]


C_CHUNKS = 8
XY_CHUNKS = 8
EW_ROWS = 512


def _c_exchange(g, part):
    w = g.shape[2]
    row0, nrows = G_PARTS[part]
    half = nrows // 2
    rq = half // C_CHUNKS

    def body(g_ref, got_ref, send_sems, recv_sems):
        x, y, c = lax.axis_index("x"), lax.axis_index("y"), lax.axis_index("c")
        cps = []
        for s in range(4):
            for q in range(C_CHUNKS):
                k = s * C_CHUNKS + q
                cp = _remote(g_ref.at[s, pl.ds(row0 + (1 - c) * half + q * rq, rq), :],
                             got_ref.at[s, pl.ds(q * rq, rq), :], send_sems.at[k], recv_sems.at[k], (x, y, 1 - c))
                cp.start()
                cps.append(cp)
        for cp in cps:
            cp.wait_recv()
        for cp in cps:
            cp.wait_send()

    return pl.pallas_call(
        body, name="grad_c_exchange_%d" % part, in_specs=[_hbm()], out_specs=_hbm(),
        out_shape=jax.ShapeDtypeStruct((4, half, w), g.dtype),
        scratch_shapes=[pltpu.SemaphoreType.DMA((4 * C_CHUNKS,)), pltpu.SemaphoreType.DMA((4 * C_CHUNKS,))],
    )(g)


XFER_DTYPE = jnp.bfloat16


def _add_own_half(g, got, c_arr, part):
    w = g.shape[2]
    row0, nrows = G_PARTS[part]
    half = nrows // 2
    nb = half // EW_ROWS
    b0 = row0 // EW_ROWS

    def body(c_ref, a_ref, b_ref, o_ref, t_ref):
        sm = a_ref[...] + b_ref[...]
        o_ref[...] = sm.astype(o_ref.dtype)

        @pl.when(pl.program_id(1) == nb - 1)
        def _():
            t_ref[...] = sm[:, EW_ROWS - MISC_ROWS:, :]

    grid_spec = pltpu.PrefetchScalarGridSpec(
        num_scalar_prefetch=1, grid=(4, nb),
        in_specs=[pl.BlockSpec((1, EW_ROWS, w), lambda s, i, c: (s, b0 + c[0] * nb + i, 0)),
                  pl.BlockSpec((1, EW_ROWS, w), lambda s, i, c: (s, i, 0))],
        out_specs=[pl.BlockSpec((1, EW_ROWS, w), lambda s, i, c: (s, i, 0)),
                   pl.BlockSpec((1, MISC_ROWS, w), lambda s, i, c: (s, 0, 0))])
    return pl.pallas_call(
        body, name="grad_add_halves", grid_spec=grid_spec,
        out_shape=[jax.ShapeDtypeStruct((4, half, w), XFER_DTYPE), jax.ShapeDtypeStruct((4, MISC_ROWS, w), g.dtype)],
        compiler_params=_cparams(("arbitrary", "arbitrary")),
    )(c_arr, g, got)


def _xy_pieces(arrs):
    pieces = []
    for a, arr in enumerate(arrs):
        nch = XY_CHUNKS if a == 0 else 1
        rq = arr.shape[1] // nch
        pieces += [(a, pl.ds(q * rq, rq)) for q in range(nch)]
    return pieces


def _xy_start(arrs, *, name):
    na = len(arrs)
    pieces = _xy_pieces(arrs)
    npc = len(pieces)
    lands = [pltpu.with_memory_space_constraint(lax.empty(a.shape, a.dtype), pltpu.HBM) for a in arrs]

    def body(*refs):
        ins, outs = refs[:na], refs[na:2 * na]
        send_sems, recv_sems, local_sems = refs[2 * na:2 * na + 3]
        token = refs[-1]
        x, y, c, peers = _xy_peers()
        me = 2 * x + y
        for k, (px, py) in enumerate(peers):
            for j, (a, rows) in enumerate(pieces):
                _remote(ins[a].at[2 * px + py, rows, :], outs[a].at[me, rows, :], send_sems.at[k * npc + j],
                        recv_sems.at[k * npc + j], (px, py, c)).start()
        for j, (a, rows) in enumerate(pieces):
            pltpu.make_async_copy(ins[a].at[me, rows, :], outs[a].at[me, rows, :], local_sems.at[j]).start()
        token[...] = jnp.zeros_like(token)

    hbm = pl.BlockSpec(memory_space=pltpu.HBM)
    sem = pl.BlockSpec(memory_space=pltpu.SEMAPHORE)
    outs = pl.pallas_call(
        body, name=name, in_specs=[hbm] * (2 * na),
        out_shape=(pltpu.SemaphoreType.DMA((3 * npc,)), pltpu.SemaphoreType.DMA((3 * npc,)),
                   pltpu.SemaphoreType.DMA((npc,)),
                   *[pltpu.HBM(a.shape, a.dtype) for a in arrs], *[pltpu.HBM(a.shape, a.dtype) for a in arrs],
                   jax.ShapeDtypeStruct((SUBLANES, LANES), F32)),
        out_specs=(sem, sem, sem, *[hbm] * (2 * na), pl.BlockSpec(memory_space=pltpu.VMEM)),
        input_output_aliases={i: 3 + i for i in range(2 * na)},
        compiler_params=pltpu.CompilerParams(has_side_effects=pltpu.SideEffectType.DATAFLOW_SIDE_EFFECTING),
    )(*[pltpu.with_memory_space_constraint(a, pltpu.HBM) for a in arrs], *lands)
    return (outs[0], outs[1], outs[2], outs[3:3 + na], outs[3 + na:3 + 2 * na]), outs[-1]


def _xy_wait(handle, after, *, name):
    send_sems, recv_sems, local_sems, src_thru, land_thru = handle
    na = len(src_thru)
    pieces = _xy_pieces(src_thru)
    npc = len(pieces)

    def body(*refs):
        ins, outs = refs[:na], refs[na:2 * na]
        ssem, rsem, lsem = refs[2 * na:2 * na + 3]
        x, y, c, peers = _xy_peers()
        me = 2 * x + y
        for k, (px, py) in enumerate(peers):
            for j, (a, rows) in enumerate(pieces):
                cp = _remote(ins[a].at[me, rows, :], outs[a].at[2 * px + py, rows, :], ssem.at[k * npc + j],
                             rsem.at[k * npc + j], (px, py, c))
                cp.wait_send()
                cp.wait_recv()
        for j, (a, rows) in enumerate(pieces):
            pltpu.make_async_copy(ins[a].at[me, rows, :], outs[a].at[me, rows, :], lsem.at[j]).wait()

    hbm = pl.BlockSpec(memory_space=pltpu.HBM)
    sem = pl.BlockSpec(memory_space=pltpu.SEMAPHORE)
    args = list(src_thru) + list(land_thru)
    outs = pl.pallas_call(
        body, name=name, in_specs=[hbm] * (2 * na) + [sem, sem, sem, pl.BlockSpec(memory_space=pl.ANY)],
        out_shape=[pltpu.HBM(a.shape, a.dtype) for a in args], out_specs=[hbm] * (2 * na),
        input_output_aliases={i: i for i in range(2 * na)},
        compiler_params=pltpu.CompilerParams(has_side_effects=pltpu.SideEffectType.DATAFLOW_SIDE_EFFECTING),
    )(*args, send_sems, recv_sems, local_sems, after)
    return outs[na:]


def _sum4_into_half(r, rt, c_arr, part, fbuf):
    _, half, w = r.shape
    nb = half // EW_ROWS
    b0 = G_PARTS[part][0] // EW_ROWS

    def body(c_ref, r_ref, t_ref, *rest):
        o_ref = rest[-1]
        o_ref[...] = ((r_ref[0].astype(F32) + r_ref[1].astype(F32)) + r_ref[2].astype(F32)) + r_ref[3].astype(F32)

        @pl.when(pl.program_id(0) == nb - 1)
        def _():
            o_ref[EW_ROWS - MISC_ROWS:, :] = ((t_ref[0] + t_ref[1]) + t_ref[2]) + t_ref[3]

    in_specs = [pl.BlockSpec((4, EW_ROWS, w), lambda i, c: (0, i, 0)),
                pl.BlockSpec((4, MISC_ROWS, w), lambda i, c: (0, 0, 0))]
    args = [c_arr, r, rt]
    aliases = {}
    if fbuf is not None:
        in_specs.append(pl.BlockSpec(memory_space=pl.ANY))
        args.append(fbuf)
        aliases = {3: 0}
    grid_spec = pltpu.PrefetchScalarGridSpec(
        num_scalar_prefetch=1, grid=(nb,), in_specs=in_specs,
        out_specs=pl.BlockSpec((EW_ROWS, w), lambda i, c: (b0 + c[0] * nb + i, 0)))
    return pl.pallas_call(
        body, name="grad_sum4", grid_spec=grid_spec, out_shape=jax.ShapeDtypeStruct((G_ROWS, w), F32),
        input_output_aliases=aliases, compiler_params=_cparams(("arbitrary",)),
    )(*args)


C_GATHER_ROWS = 512


def _c_allgather_halves(f, parts):
    w = f.shape[1]
    chunks = []
    for part in parts:
        chunks += [(part, r) for r in range(0, G_PARTS[part][1] // 2, C_GATHER_ROWS)]
    nch = len(chunks)

    def body(f_ref, out_ref, send_sems, recv_sems):
        x, y, c = lax.axis_index("x"), lax.axis_index("y"), lax.axis_index("c")

        def rows(q, owner):
            part, r = chunks[q]
            row0, nrows = G_PARTS[part]
            return pl.ds(row0 + owner * (nrows // 2) + r, C_GATHER_ROWS)

        sends = []
        for q in range(nch):
            cp = _remote(f_ref.at[rows(q, c), :], out_ref.at[rows(q, c), :], send_sems.at[q], recv_sems.at[q],
                         (x, y, 1 - c))
            cp.start()
            sends.append(cp)
        for q in range(nch):
            _remote(f_ref.at[rows(q, 1 - c), :], out_ref.at[rows(q, 1 - c), :], send_sems.at[q], recv_sems.at[q],
                    (x, y, 1 - c)).wait_recv()
        for cp in sends:
            cp.wait_send()

    return pl.pallas_call(
        body, name="grad_c_allgather_" + "".join(str(p) for p in parts), in_specs=[_hbm()], out_specs=_hbm(),
        input_output_aliases={0: 0},
        out_shape=jax.ShapeDtypeStruct((G_ROWS, w), f.dtype),
        scratch_shapes=[pltpu.SemaphoreType.DMA((nch,)), pltpu.SemaphoreType.DMA((nch,))],
    )(f)


def _adamw(w, m, v, g, g_rows=None):
    shape = w.shape
    cols = shape[-1]
    rows = int(math.prod(shape)) // cols
    tr = 256 if rows % 256 == 0 else rows
    from_flat = g_rows is not None
    c1 = 1.0 / (1.0 - ADAM_B1 ** ADAM_STEP)
    c2 = 1.0 / (1.0 - ADAM_B2 ** ADAM_STEP)

    def body(w_ref, m_ref, v_ref, g_ref, *outs):
        gg = g_ref[...]
        nm = ADAM_B1 * m_ref[...] + (1.0 - ADAM_B1) * gg
        nv = ADAM_B2 * v_ref[...] + (1.0 - ADAM_B2) * (gg * gg)
        if from_flat:
            outs[0][...] = gg
        d_ref, nm_ref, nv_ref = outs[-3:]
        nm_ref[...] = nm
        nv_ref[...] = nv
        d_ref[...] = -ADAM_LR * ((nm * c1) / (jnp.sqrt(nv * c2) + ADAM_EPS) + ADAM_WD * w_ref[...])

    spec = pl.BlockSpec((tr, cols), lambda i: (i, 0))
    if from_flat:
        nbl = rows // DEPTH // tr
        assert cols == FLAT and all(r % tr == 0 for r in g_rows) and len(g_rows) == DEPTH == 2
        b0, b1 = g_rows[0] // tr, g_rows[1] // tr
        g_spec = pl.BlockSpec((tr, cols), lambda i: (jnp.where(i < nbl, b0 + i, b1 + i - nbl), 0))
        g_arg = g
    else:
        g_spec = spec
        g_arg = g.reshape(rows, cols)
    n_out = 4 if from_flat else 3
    sds = jax.ShapeDtypeStruct((rows, cols), F32)
    outs = pl.pallas_call(
        body, name="adamw", grid=(rows // tr,), in_specs=[spec, spec, spec, g_spec], out_specs=[spec] * n_out,
        out_shape=[sds] * n_out, compiler_params=_cparams(("arbitrary",)),
    )(w.reshape(rows, cols), m.reshape(rows, cols), v.reshape(rows, cols), g_arg)
    outs = [o.reshape(shape) for o in outs]
    return outs if from_flat else [g] + outs


SMALL_SHARDED = (("s5_glu_w", (2, 64, 256), 1), ("ssd_conv_w", (2, 4, 256), 2), ("rg_conv_w", (2, 4, 64), 2))
REPLICATED = (
    ("ssd_conv_b", (2, 1024)), ("ssd_dt_bias", (2, 8)), ("ssd_a_log", (2, 8)), ("ssd_d", (2, 8)),
    ("ssd_norm_w", (2, 512)), ("s5_lam_re", (2, 16, 64)), ("s5_lam_im", (2, 16, 64)), ("s5_log_step", (2, 16)),
    ("s5_b_re", (2, 16, 64, 16)), ("s5_b_im", (2, 16, 64, 16)), ("s5_c_re", (2, 16, 16, 64)),
    ("s5_c_im", (2, 16, 16, 64)), ("s5_d", (2, 256)), ("s5_glu_b", (2, 256)), ("rg_conv_b", (2, 256)),
    ("rg_wa", (2, 4, 64, 64)), ("rg_ba", (2, 4, 64)), ("rg_wx", (2, 4, 64, 64)), ("rg_bx", (2, 4, 64)),
    ("rg_lambda", (2, 256)), ("ln1_g", (2, 1024)), ("ln1_b", (2, 1024)), ("ln2_g", (2, 1024)), ("ln2_b", (2, 1024)),
    ("ln3_g", (2, 1024)), ("ln3_b", (2, 1024)),
)
WEIGHT_ORDER = (
    "w_in", "w_out", "ssd_conv_w", "ssd_conv_b", "ssd_dt_bias", "ssd_a_log", "ssd_d", "ssd_norm_w", "s5_lam_re",
    "s5_lam_im", "s5_log_step", "s5_b_re", "s5_b_im", "s5_c_re", "s5_c_im", "s5_d", "s5_glu_w", "s5_glu_b",
    "rg_conv_w", "rg_conv_b", "rg_wa", "rg_ba", "rg_wx", "rg_bx", "rg_lambda", "ln1_g", "ln1_b", "xa_wq", "xa_wk",
    "xa_wv", "xa_wo", "ln2_g", "ln2_b", "mlp_w1", "mlp_w2", "ln3_g", "ln3_b",
)


def _size(shape):
    return int(math.prod(shape))


def _round_up(a, b):
    return (a + b - 1) // b * b


SMALL_ELEMS = sum(_size(s) for _, s, _ in SMALL_SHARDED)
REP_ELEMS = sum(_size(s) for _, s in REPLICATED)
REP_QROWS = _round_up(-(-REP_ELEMS // (4 * FLAT)), 8)
assert SMALL_ELEMS <= MISC_REP_ROW * FLAT and MISC_REP_ROW + REP_QROWS <= MISC_ROWS


def _pack_shards(tensors, names_shapes):
    return jnp.concatenate([tensors[n].reshape(-1) for n, *_ in names_shapes])


def _unpack(flat, names_shapes):
    out, off = {}, 0
    for n, s, *_ in names_shapes:
        out[n] = flat[off:off + _size(s)].reshape(s)
        off += _size(s)
    return out


def _split_shards(full, names_shapes):
    rows = []
    for k in range(4):
        parts = []
        for n, s, ax in names_shapes:
            w = s[ax]
            parts.append(lax.slice_in_dim(full[n], k * w, (k + 1) * w, axis=ax).reshape(-1))
        rows.append(jnp.concatenate(parts))
    return jnp.stack(rows)


def _pack_cols(w):
    pad = jnp.zeros((w.shape[0], LANES - SSD_HEADS), w.dtype)
    return jnp.concatenate([w[:, O_XBC:O_XBC + 1024], w[:, O_Z:O_Z + 512], w[:, O_U:O_U + 256],
                            w[:, O_XRG:O_XRG + 256], w[:, O_GRG:O_GRG + 256], w[:, O_DT:O_DT + 8], pad], axis=1)


def _unpack_cols(w):
    return jnp.concatenate([w[:, P_Z:P_Z + 512], w[:, P_XBC:P_XBC + 1024], w[:, P_DT:P_DT + 8],
                            w[:, P_U:P_U + 256], w[:, P_XRG:P_XRG + 256], w[:, P_GRG:P_GRG + 256]], axis=1)


def _lanes(v, width):
    return jnp.pad(v, (0, width - v.shape[0])).reshape(1, width)


def _layer_params(rep, l):
    p = {}
    p["ssd_cb"] = rep["ssd_conv_b"][l].reshape(1, -1)
    p["ssd_dtb"] = _lanes(rep["ssd_dt_bias"][l], LANES)
    p["ssd_a"] = _lanes(-jnp.exp(rep["ssd_a_log"][l]), LANES)
    p["ssd_d"] = jnp.repeat(rep["ssd_d"][l], 64).reshape(1, -1)
    p["ssd_nw"] = rep["ssd_norm_w"][l].reshape(1, -1)
    s5_args = tuple(rep[n][l] for n in ("s5_lam_re", "s5_lam_im", "s5_log_step", "s5_b_re", "s5_b_im", "s5_c_re",
                                        "s5_c_im"))
    (lbr, lbi, bre, bim, cre, cim), p["s5_vjp"] = jax.vjp(_s5_prepare, *s5_args)
    p.update(s5_bre=bre, s5_bim=bim, s5_cre=cre, s5_cim=cim)
    p["s5_coef"] = _s5_scan_coef(lbr, lbi, False)
    p["s5_rcoef"] = _s5_scan_coef(lbr, lbi, True)
    p["s5_d"] = rep["s5_d"][l].reshape(1, -1)
    p["s5_gb"] = rep["s5_glu_b"][l].reshape(1, -1)
    p["rg_cb"] = rep["rg_conv_b"][l].reshape(1, -1)
    p["rg_wa"] = _block_diag(rep["rg_wa"][l])
    p["rg_wx"] = _block_diag(rep["rg_wx"][l])
    p["rg_ba"] = rep["rg_ba"][l].reshape(1, -1)
    p["rg_bx"] = rep["rg_bx"][l].reshape(1, -1)
    p["rg_nsp"] = (-RG_C * jax.nn.softplus(-rep["rg_lambda"][l])).reshape(1, -1)
    p["rg_dnsp"] = RG_C * jax.nn.sigmoid(-rep["rg_lambda"][l])
    for n in ("ln1_g", "ln1_b", "ln2_g", "ln2_b", "ln3_g", "ln3_b"):
        p[n] = rep[n][l].reshape(1, -1)
    return p


def _layer_fwd(h, mem, p, fetch, first_after=None):
    s = {"h0": h}
    p.update(fetch(0, h if first_after is None else first_after))
    proj = _mm(h, p["w_in"], name="in_proj")
    s["proj"] = proj
    y_ssd, s["ssd_yy"], s["ssd_states"] = _ssd_fwd(proj, p["ssd_cw"], p["ssd_cb"], p["ssd_dtb"], p["ssd_a"],
                                                     p["ssd_d"], p["ssd_nw"])
    y_s5, s["s5_y2"], s["s5_hre"], s["s5_him"] = _s5_fwd(proj, p["s5_bre"], p["s5_bim"], p["s5_cre"], p["s5_cim"],
                                                         p["s5_d"], p["s5_glu_w"], p["s5_gb"], p["s5_coef"])
    y_rg, s["rg_h"] = _rg_fwd(proj, p["rg_cw"], p["rg_cb"], p["rg_wa"], p["rg_ba"], p["rg_wx"], p["rg_bx"],
                              p["rg_nsp"])
    s["ys"] = [y_ssd, y_s5, y_rg]
    p.update(fetch(1, y_rg))
    h1, s["xh1"], s["rs1"] = _outproj_ln_fwd(s["ys"], h, p["w_out"], p["ln1_g"], p["ln1_b"])
    s["h1"] = h1
    p.update(fetch(2, h1))
    kb = _mm(mem, p["xa_wk"], name="mem_proj")
    vb = _mm(mem, p["xa_wv"], name="mem_proj")
    s["kb"], s["vb"] = kb, vb
    h2, s["xh2"], s["rs2"], s["attn_o"] = _attn_ln_fwd(h1, p["xa_wq"], p["xa_wo"], kb, vb, p["ln2_g"], p["ln2_b"])
    s["h2"] = h2
    p.update(fetch(3, h2))
    h3, s["xh3"], s["rs3"], s["mlp_hdn"] = _mlp_ln_fwd(h2, p["mlp_w1"], p["mlp_w2"], p["ln3_g"], p["ln3_b"])
    return h3, s


def _layer_bwd(dh3, mem, p, s, l, gbuf, after_mlp=None):
    g = {}
    dr3, du, dh2, g["ln3_g"], g["ln3_b"] = _mlp_ln_bwd(dh3, s["xh3"], s["rs3"], p["ln3_g"], s["mlp_hdn"],
                                                        p["mlp_w1"], p["mlp_w2"])
    gbuf = _wgrad_flat(s["h2"], du, gbuf, mode="colblk", row_off=_grad_row("mlp_w1", l), name="wgrad_mlp_w1")
    gbuf = _wgrad_flat(s["mlp_hdn"], dr3, gbuf, mode="rowblk", row_off=_grad_row("mlp_w2", l), name="wgrad_mlp_w2")
    ln2_g = p["ln2_g"] if after_mlp is None else p["ln2_g"] + after_mlp(gbuf)[0:1, 0:1]
    dr2, dq, dh1, dkb, dvb, g["ln2_g"], g["ln2_b"] = _attn_ln_bwd(dh2, s["xh2"], s["rs2"], ln2_g, s["h1"],
                                                                   p["xa_wq"], p["xa_wo"], s["kb"], s["vb"])
    for n, a_op, g_op in (("xa_wo", s["attn_o"], dr2), ("xa_wq", s["h1"], dq), ("xa_wk", mem, dkb),
                          ("xa_wv", mem, dvb)):
        gbuf = _wgrad_flat(a_op, g_op, gbuf, mode="rows4", row_off=_grad_row(n, l), name="wgrad_" + n)
    dr1, dres, dycat, g["ln1_g"], g["ln1_b"] = _outproj_ln_bwd(dh1, s["xh1"], s["rs1"], p["ln1_g"], p["w_out"])
    gbuf = _wgrad_flat(s["ys"], dr1, gbuf, mode="rows4", row_off=_grad_row("w_out", l), name="wgrad_w_out")
    proj = s["proj"]
    (dxbc, dz, ddt, dcw, dcb, ddtb, da_neg, dd_l, dnw) = _ssd_bwd(
        dycat, proj, s["ssd_yy"], s["ssd_states"], p["ssd_cw"], p["ssd_cb"], p["ssd_dtb"], p["ssd_a"], p["ssd_d"],
        p["ssd_nw"])
    g["ssd_conv_w"] = dcw[0:4]
    g["ssd_conv_b"] = dcb[0]
    g["ssd_dt_bias"] = ddtb[0, :SSD_HEADS]
    g["ssd_a_log"] = da_neg[0, :SSD_HEADS] * p["ssd_a"][0, :SSD_HEADS]
    g["ssd_d"] = dd_l.reshape(SSD_HEADS, 64).sum(axis=1)
    g["ssd_norm_w"] = dnw[0]
    (du_s5, dbre, dbim, dcre, dcim, dlam, dd5, dgw, dgb) = _s5_bwd(
        dycat, proj, s["s5_y2"], s["s5_hre"], s["s5_him"], p["s5_bre"], p["s5_bim"], p["s5_cre"], p["s5_cim"],
        p["s5_d"], p["s5_glu_w"], p["s5_gb"], p["s5_rcoef"])
    dl = dlam.sum(axis=1)
    s5g = p["s5_vjp"]((dl[0], dl[1], dbre, dbim, dcre, dcim))
    for n, v in zip(("s5_lam_re", "s5_lam_im", "s5_log_step", "s5_b_re", "s5_b_im", "s5_c_re", "s5_c_im"), s5g):
        g[n] = v
    g["s5_d"] = dd5[0]
    g["s5_glu_w"] = dgw
    g["s5_glu_b"] = dgb[0]
    (dxrg, dgrg, drcw, drcb, dwa, dba, dwx, dbx, dnsp) = _rg_bwd(
        dycat, proj, s["rg_h"], p["rg_cw"], p["rg_cb"], p["rg_wa"], p["rg_ba"], p["rg_wx"], p["rg_bx"], p["rg_nsp"])
    g["rg_conv_w"] = drcw[0:4]
    g["rg_conv_b"] = drcb[0]
    g["rg_wa"] = _block_diag_extract(dwa, RG_BLOCKS)
    g["rg_wx"] = _block_diag_extract(dwx, RG_BLOCKS)
    g["rg_ba"] = dba.reshape(RG_BLOCKS, RG_BLOCK_DIM)
    g["rg_bx"] = dbx.reshape(RG_BLOCKS, RG_BLOCK_DIM)
    g["rg_lambda"] = dnsp[0] * p["rg_dnsp"]
    dproj = [dxbc, dz, du_s5, dxrg, dgrg, ddt]
    g["w_in"] = _unpack_cols(_wgrad_in(s["h0"], dproj))
    dh0 = _in_proj_bwd(dproj, p["w_in"], dres)
    for n in ("ln1_g", "ln1_b", "ln2_g", "ln2_b", "ln3_g", "ln3_b"):
        g[n] = g[n][0]
    return dh0, g, gbuf


def _local_step(h, memf, target, rep, fetch):
    params = [_layer_params(rep, l) for l in range(DEPTH)]
    prep_done = sum(v.reshape(-1)[0:1] for p in params for v in p.values() if isinstance(v, jax.Array))
    saved = []
    for l in range(DEPTH):
        h, s = _layer_fwd(h, memf, params[l], functools.partial(fetch, l), prep_done if l == 0 else None)
        saved.append(s)
    loss11, dh = _loss_fwd_bwd(h, target)
    grads = [None] * DEPTH
    gbuf = None
    c_arr = lax.axis_index("c").astype(jnp.int32).reshape(1)
    handles = {}

    def start_part(buf, part):
        handles[part], token = _xy_start(_chip_sums(buf, c_arr, part), name="grad_xy_start_%d" % part)
        return token

    for l in reversed(range(DEPTH)):
        hook = functools.partial(start_part, part=1) if l == 0 else None
        dh, grads[l], gbuf = _layer_bwd(dh, memf, params[l], saved[l], l, gbuf, hook)
        if l == DEPTH - 1:
            gbuf = lax.dynamic_update_slice(
                gbuf, _w_in_block(grads[l]["w_in"], jnp.zeros((4, MISC_ROWS, FLAT), F32)),
                (0, _grad_row("w_in", l), 0))
            params[0]["ln3_g"] = params[0]["ln3_g"] + start_part(gbuf, 0)[0:1, 0:1]
    gsmall = {n: jnp.stack([grads[l][n] for l in range(DEPTH)]) for n in grads[0] if n != "w_in"}
    return loss11, dh, gsmall, grads[0]["w_in"], gbuf, handles, c_arr


def _w_in_block(gw, tail):
    gw = jnp.pad(gw.reshape(D_MODEL, 4, W_IN_SHARD), ((0, 0), (0, 0), (0, W_IN_PAD - W_IN_SHARD)))
    return jnp.concatenate([jnp.transpose(gw, (1, 0, 2)).reshape(4, W_IN_PAD, FLAT), tail], axis=1)


def _chip_sums(gbuf, c_arr, part):
    return list(_add_own_half(gbuf, _c_exchange(gbuf, part), c_arr, part))


def kernel(x, mem, w_in, w_out, ssd_conv_w, ssd_conv_b, ssd_dt_bias, ssd_a_log, ssd_d, ssd_norm_w, s5_lam_re, s5_lam_im, s5_log_step, s5_b_re, s5_b_im, s5_c_re, s5_c_im, s5_d, s5_glu_w, s5_glu_b, rg_conv_w, rg_conv_b, rg_wa, rg_ba, rg_wx, rg_bx, rg_lambda, ln1_g, ln1_b, xa_wq, xa_wk, xa_wv, xa_wo, ln2_g, ln2_b, mlp_w1, mlp_w2, ln3_g, ln3_b, loss_target, m_w_in, m_w_out, m_ssd_conv_w, m_ssd_conv_b, m_ssd_dt_bias, m_ssd_a_log, m_ssd_d, m_ssd_norm_w, m_s5_lam_re, m_s5_lam_im, m_s5_log_step, m_s5_b_re, m_s5_b_im, m_s5_c_re, m_s5_c_im, m_s5_d, m_s5_glu_w, m_s5_glu_b, m_rg_conv_w, m_rg_conv_b, m_rg_wa, m_rg_ba, m_rg_wx, m_rg_bx, m_rg_lambda, m_ln1_g, m_ln1_b, m_xa_wq, m_xa_wk, m_xa_wv, m_xa_wo, m_ln2_g, m_ln2_b, m_mlp_w1, m_mlp_w2, m_ln3_g, m_ln3_b, v_w_in, v_w_out, v_ssd_conv_w, v_ssd_conv_b, v_ssd_dt_bias, v_ssd_a_log, v_ssd_d, v_ssd_norm_w, v_s5_lam_re, v_s5_lam_im, v_s5_log_step, v_s5_b_re, v_s5_b_im, v_s5_c_re, v_s5_c_im, v_s5_d, v_s5_glu_w, v_s5_glu_b, v_rg_conv_w, v_rg_conv_b, v_rg_wa, v_rg_ba, v_rg_wx, v_rg_bx, v_rg_lambda, v_ln1_g, v_ln1_b, v_xa_wq, v_xa_wk, v_xa_wv, v_xa_wo, v_ln2_g, v_ln2_b, v_mlp_w1, v_mlp_w2, v_ln3_g, v_ln3_b):
    args = dict(locals())
    weights = {n: args[n] for n in WEIGHT_ORDER}
    mom_m = {n: args["m_" + n] for n in WEIGHT_ORDER}
    mom_v = {n: args["v_" + n] for n in WEIGHT_ORDER}

    shards = []
    for l in range(DEPTH):
        for n, shp, ax in LAYER_GATHERED:
            w = weights[n][l]
            if w.shape[1] != shp[1]:
                w = jnp.pad(w, ((0, 0), (0, shp[1] - w.shape[1])))
            if n not in ("ssd_conv_w", "rg_conv_w"):
                w = w.astype(MXU_DTYPE)
            shards.append(w)
    handle = _gather_start(shards)

    def unpad(arr, padded, width):
        return jnp.concatenate([arr[:, padded * k:padded * k + width] for k in range(4)], axis=1)

    def fetch(l, grp, after):
        ts = [l * N_GATHERED + j for j in WAIT_GROUPS[grp]]
        _, landed = _gather_wait(handle, ts, after, name="weights_gather_wait_%d_%d" % (l, grp))
        out = {}
        for t, arr in zip(ts, landed):
            n = LAYER_GATHERED[t % N_GATHERED][0]
            if n == "w_in":
                arr = _pack_cols(unpad(arr, W_IN_PAD, W_IN_SHARD))
            elif n == "rg_conv_w":
                arr = unpad(arr, LANES, RG_CONV_SHARD)
            out[{"ssd_conv_w": "ssd_cw", "rg_conv_w": "rg_cw"}.get(n, n)] = arr
        return out

    rep = {n: weights[n] for n, _ in REPLICATED}

    loss11, dx, gsmall, gw_in0, gbuf, handles, c_arr = _local_step(x[0], mem[0], loss_target[0], rep, fetch)
    grad_x = dx[None]
    loss = lax.psum(loss11[0, 0], ("x", "y", "c"))

    small_q = _split_shards(gsmall, SMALL_SHARDED)
    rep_q = jnp.pad(_pack_shards(gsmall, REPLICATED), (0, 4 * REP_QROWS * FLAT - REP_ELEMS)).reshape(4, -1)
    misc = jnp.concatenate(
        [jnp.pad(small_q, ((0, 0), (0, MISC_REP_ROW * FLAT - SMALL_ELEMS))), rep_q,
         jnp.zeros((4, (MISC_ROWS - MISC_REP_ROW - REP_QROWS) * FLAT), F32)], axis=1).reshape(4, MISC_ROWS, FLAT)
    gbuf = lax.dynamic_update_slice(gbuf, _w_in_block(gw_in0, misc), (0, _grad_row("w_in", 0), 0))
    handles[2], token = _xy_start(_chip_sums(gbuf, c_arr, 2), name="grad_xy_start_2")
    fbuf = None
    for part in (0, 1):
        got = _xy_wait(handles[part], dx, name="grad_xy_wait_%d" % part)
        fbuf = _sum4_into_half(got[0], got[1] + token[0:1, 0:1], c_arr, part, fbuf)
    fbuf = _c_allgather_halves(fbuf, (0, 1))
    res = {n: _adamw(weights[n], mom_m[n], mom_v[n], fbuf, g_rows=[_grad_row(n, l) for l in range(DEPTH)])
           for n in ("mlp_w1", "mlp_w2")}
    got = _xy_wait(handles[2], res["mlp_w2"][1], name="grad_xy_wait_2")
    reduced = _c_allgather_halves(_sum4_into_half(got[0], got[1], c_arr, 2, fbuf), (2,))
    misc_red = reduced[ROW_MISC:]
    rep_all = _xy_allgather(misc_red[MISC_REP_ROW:MISC_REP_ROW + REP_QROWS], name="small_grads_allgather")
    g_red = {**_unpack(misc_red[:MISC_REP_ROW].reshape(-1), SMALL_SHARDED),
             **_unpack(rep_all.reshape(-1), REPLICATED)}
    g_red["w_in"] = jnp.stack([
        reduced[_grad_row("w_in", l):_grad_row("w_in", l) + W_IN_PAD].reshape(D_MODEL, W_IN_PAD)[:, :W_IN_SHARD]
        for l in range(DEPTH)])

    for n in WEIGHT_ORDER:
        if n in ("w_out", "xa_wq", "xa_wk", "xa_wv", "xa_wo"):
            res[n] = _adamw(weights[n], mom_m[n], mom_v[n], reduced, g_rows=[_grad_row(n, l) for l in range(DEPTH)])
        elif n not in res:
            res[n] = _adamw(weights[n], mom_m[n], mom_v[n], g_red[n])
    return (loss, grad_x, *[res[n][0] for n in WEIGHT_ORDER], *[res[n][1] for n in WEIGHT_ORDER],
            *[res[n][2] for n in WEIGHT_ORDER], *[res[n][3] for n in WEIGHT_ORDER])
```

```python
import functools
import math

import jax
import jax.numpy as jnp
from jax import lax
from jax.experimental import pallas as pl
from jax.experimental.pallas import tpu as pltpu

F32 = jnp.float32
MXU_DTYPE = jnp.bfloat16

D_MODEL = 1024
DEPTH = 2
MEM_LEN = 256
SSD_WIDTH = 512
SSD_HEADS = 8
SSD_STATE = 128
SSD_CHUNK = 128
SSD_XBC = 1024
S5_WIDTH = 256
S5_GROUPS = 16
S5_GROUP_CH = 16
S5_STATE = 64
S5_NSTATE = S5_GROUPS * S5_STATE
RG_WIDTH = 256
RG_BLOCKS = 4
RG_BLOCK_DIM = 64
RG_C = 8.0
XA_HEADS = 4
XA_HEAD_DIM = 256
D_FF = 4096
D_IN = 2312
ALPHA = (2.0 * DEPTH) ** 0.25
LN_EPS = 1e-5
ADAM_LR = 0.001
ADAM_B1 = 0.9
ADAM_B2 = 0.999
ADAM_EPS = 1e-08
ADAM_WD = 0.01
ADAM_STEP = 10

P_XBC, P_Z, P_U, P_XRG, P_GRG, P_DT = 0, 1024, 1536, 1792, 2048, 2304
D_PACK = 2432
O_Z, O_XBC, O_DT, O_U, O_XRG, O_GRG = 0, 512, 1536, 1544, 1800, 2056

LANES = 128
SUBLANES = 8
VMEM_LIMIT = 52 * 1024 * 1024
TM = 512
SSD_TM = 512
SCAN_TM = 512
FLAT = 1024

MESH = pl.DeviceIdType.MESH


def _cparams(sem):
    return pltpu.CompilerParams(dimension_semantics=sem, vmem_limit_bytes=VMEM_LIMIT)


def _dot(a, b):
    return jnp.dot(a.astype(MXU_DTYPE), b.astype(MXU_DTYPE), preferred_element_type=F32)


def _dot_nt(a, b):
    return lax.dot_general(a.astype(MXU_DTYPE), b.astype(MXU_DTYPE), (((1,), (1,)), ((), ())),
                           preferred_element_type=F32)


def _dot_tn(a, b):
    return lax.dot_general(a.astype(MXU_DTYPE), b.astype(MXU_DTYPE), (((0,), (0,)), ((), ())),
                           preferred_element_type=F32)


def _dot_f32(a, b):
    return jnp.dot(a, b, precision=lax.Precision.HIGHEST, preferred_element_type=F32)


def _dot_f32_tn(a, b):
    return lax.dot_general(a, b, (((0,), (0,)), ((), ())), precision=lax.Precision.HIGHEST,
                           preferred_element_type=F32)


def _sigmoid(x):
    return 1.0 / (1.0 + jnp.exp(-x))


def _softplus(x):
    return jnp.maximum(x, 0.0) + jnp.log(1.0 + jnp.exp(-jnp.abs(x)))


_GELU_K = math.sqrt(2.0 / math.pi)


def _gelu(x):
    return 0.5 * x * (1.0 + jnp.tanh(_GELU_K * (x + 0.044715 * x * x * x)))


def _gelu_grad(x):
    t = jnp.tanh(_GELU_K * (x + 0.044715 * x * x * x))
    return 0.5 * (1.0 + t) + 0.5 * x * (1.0 - t * t) * _GELU_K * (1.0 + 3.0 * 0.044715 * x * x)


def _expm1(x):
    small = x * (1.0 + x * (0.5 + x * (1.0 / 6.0 + x * (1.0 / 24.0))))
    return jnp.where(jnp.abs(x) < 0.05, small, jnp.exp(x) - 1.0)


def _sum0(x):
    return jnp.sum(x, axis=0, keepdims=True)


def _ln_fwd(r, g, b):
    mu = jnp.mean(r, axis=-1, keepdims=True)
    xc = r - mu
    var = jnp.mean(xc * xc, axis=-1, keepdims=True)
    rstd = lax.rsqrt(var + LN_EPS)
    xhat = xc * rstd
    return xhat * g + b, xhat, rstd


def _ln_bwd(dout, xhat, rstd, g):
    dxh = dout * g
    m1 = jnp.mean(dxh, axis=-1, keepdims=True)
    m2 = jnp.mean(dxh * xhat, axis=-1, keepdims=True)
    return rstd * (dxh - m1 - xhat * m2)


def _rows(tm, n, col=0):
    return pl.BlockSpec((tm, n), lambda i: (i, col))


def _const(shape):
    nd = len(shape)
    return pl.BlockSpec(shape, lambda i: (0,) * nd)


def _mm(a, w, *, name):
    t, k = a.shape
    n = w.shape[1]
    tm = min(TM, t)

    def body(a_ref, w_ref, o_ref):
        o_ref[...] = _dot(a_ref[...], w_ref[...])

    return pl.pallas_call(
        body, name=name, grid=(t // tm,), in_specs=[_rows(tm, k), _const(w.shape)], out_specs=_rows(tm, n),
        out_shape=jax.ShapeDtypeStruct((t, n), F32), compiler_params=_cparams(("arbitrary",)),
    )(a, w)


DPROJ_PIECES = ((P_XBC, 1024), (P_Z, 512), (P_U, 256), (P_XRG, 256), (P_GRG, 256), (P_DT, LANES))


def _in_proj_bwd(pieces, w, dres):
    t = dres.shape[0]
    npc = len(pieces)

    def body(*refs):
        w_ref, r_ref, o_ref = refs[npc:]
        acc = r_ref[...]
        for p_ref, (off, k) in zip(refs[:npc], DPROJ_PIECES):
            acc = acc + _dot_nt(p_ref[...], w_ref[:, off:off + k])
        o_ref[...] = acc

    return pl.pallas_call(
        body, name="in_proj_bwd", grid=(t // TM,),
        in_specs=[_rows(TM, k) for _, k in DPROJ_PIECES] + [_const(w.shape), _rows(TM, D_MODEL)],
        out_specs=_rows(TM, D_MODEL), out_shape=jax.ShapeDtypeStruct((t, D_MODEL), F32),
        compiler_params=_cparams(("arbitrary",)),
    )(*pieces, w, dres)


def _wgrad_in(h0, pieces):
    t = h0.shape[0]
    npc = len(pieces)

    def body(*refs):
        h_ref, o_ref = refs[npc], refs[npc + 1]
        @pl.when(pl.program_id(0) == 0)
        def _():
            o_ref[...] = jnp.zeros_like(o_ref)

        hb = h_ref[...].astype(MXU_DTYPE)
        for p_ref, (off, k) in zip(refs[:npc], DPROJ_PIECES):
            o_ref[:, off:off + k] += _dot_tn(hb, p_ref[...])

    return pl.pallas_call(
        body, name="wgrad_in", grid=(t // TM,),
        in_specs=[_rows(TM, k) for _, k in DPROJ_PIECES] + [_rows(TM, D_MODEL)],
        out_specs=_const((D_MODEL, D_PACK)), out_shape=jax.ShapeDtypeStruct((D_MODEL, D_PACK), F32),
        compiler_params=_cparams(("arbitrary",)),
    )(*pieces, h0)


G_ROWS = 8192
G_PARTS = ((0, 4096), (4096, 2048), (6144, 2048))
W_IN_SHARD = 578
W_IN_PAD = 640
MISC_ROWS = 128
MISC_REP_ROW = 40
ROW_MISC = G_ROWS - MISC_ROWS
W_IN_BLOCK_ROWS = W_IN_PAD + MISC_ROWS


def _grad_row(name, l):
    base = 0 if l == 1 else 4096
    mid = base + 2048 if l == 1 else 6144
    return {"mlp_w1": base, "mlp_w2": base + 1024, "w_out": mid, "xa_wq": mid + 256, "xa_wk": mid + 512,
            "xa_wv": mid + 768, "xa_wo": mid + 1024, "w_in": mid + 1280}[name]


def _wgrad_flat(a, g, buf, *, mode, row_off, name):
    pieces = list(a) if isinstance(a, (list, tuple)) else [a]
    t = g.shape[0]
    tt = min(1024, t)
    ns = t // tt
    blk = D_MODEL

    def accumulate(o_ref, parts, s):
        @pl.when(s == 0)
        def _():
            o_ref[...] = jnp.zeros_like(o_ref)

        for q, v in parts:
            o_ref[q] += v

    if mode == "rows4":
        grid = (ns,)
        in_specs = [pl.BlockSpec((tt, p.shape[1]), lambda s: (s, 0)) for p in pieces]
        in_specs.append(pl.BlockSpec((tt, blk), lambda s: (s, 0)))
        out_spec = pl.BlockSpec((4, 256, FLAT), lambda s: (0, row_off // 256, 0))
        sem = ("arbitrary",)
        npc = len(pieces)

        def body(*refs):
            g_v = refs[npc][...]
            parts, q0 = [], 0
            for p_ref in refs[:npc]:
                full = _dot_tn(p_ref[...], g_v)
                nq = full.shape[0] // 256
                parts += [(q0 + q, full[q * 256:(q + 1) * 256]) for q in range(nq)]
                q0 += nq
            accumulate(refs[-1], parts, pl.program_id(0))
    else:
        grid = (2, ns)
        if mode == "rowblk":
            in_specs = [pl.BlockSpec((tt, 2 * blk), lambda q, s: (s, q)), pl.BlockSpec((tt, blk), lambda q, s: (s, 0))]
        else:
            in_specs = [pl.BlockSpec((tt, blk), lambda q, s: (s, 0)), pl.BlockSpec((tt, 2 * blk), lambda q, s: (s, q))]
        out_spec = pl.BlockSpec((2, blk, FLAT), lambda q, s: (q, row_off // blk, 0))
        sem = ("arbitrary", "arbitrary")

        def body(a_ref, g_ref, *rest):
            full = _dot_tn(a_ref[...], g_ref[...])
            if mode == "rowblk":
                parts = [(0, full[:blk]), (1, full[blk:])]
            else:
                parts = [(0, full[:, :blk]), (1, full[:, blk:])]
            accumulate(rest[-1], parts, pl.program_id(1))

    args = pieces + [g]
    aliases = {}
    if buf is not None:
        in_specs.append(pl.BlockSpec(memory_space=pl.ANY))
        args.append(buf)
        aliases = {len(args) - 1: 0}
    return pl.pallas_call(
        body, name=name, grid=grid, in_specs=in_specs, out_specs=out_spec,
        out_shape=jax.ShapeDtypeStruct((4, G_ROWS, FLAT), F32), input_output_aliases=aliases,
        compiler_params=_cparams(sem),
    )(*args)


def _outproj_ln_fwd(ys, h, w, g, b):
    t = h.shape[0]
    npc = len(ys)

    def body(*refs):
        h_ref, w_ref, g_ref, b_ref, hn_ref, xh_ref, rs_ref = refs[npc:]
        r = ALPHA * h_ref[...]
        off = 0
        for y_ref in refs[:npc]:
            k = y_ref.shape[1]
            r = r + _dot(y_ref[...], w_ref[off:off + k, :])
            off += k
        out, xhat, rstd = _ln_fwd(r, g_ref[...], b_ref[...])
        hn_ref[...] = out
        xh_ref[...] = xhat
        rs_ref[...] = rstd

    return pl.pallas_call(
        body, name="outproj_ln_fwd", grid=(t // TM,),
        in_specs=[_rows(TM, y.shape[1]) for y in ys] + [_rows(TM, D_MODEL), _const((D_MODEL, D_MODEL)),
                                                        _const((1, D_MODEL)), _const((1, D_MODEL))],
        out_specs=[_rows(TM, D_MODEL), _rows(TM, D_MODEL), _rows(TM, 1)],
        out_shape=[jax.ShapeDtypeStruct((t, D_MODEL), F32), jax.ShapeDtypeStruct((t, D_MODEL), F32),
                   jax.ShapeDtypeStruct((t, 1), F32)],
        compiler_params=_cparams(("arbitrary",)),
    )(*ys, h, w, g, b)


def _attn_probs(q, kb, hh):
    sl = slice(hh * XA_HEAD_DIM, (hh + 1) * XA_HEAD_DIM)
    s = _dot_nt(q[:, sl], kb[:, sl]) * (1.0 / math.sqrt(XA_HEAD_DIM))
    m = jnp.max(s, axis=-1, keepdims=True)
    e = jnp.exp(s - m)
    return e / jnp.sum(e, axis=-1, keepdims=True)


def _attn_ln_fwd(h1, wq, wo, kb, vb, g, b):
    t = h1.shape[0]

    def body(h_ref, wq_ref, wo_ref, k_ref, v_ref, g_ref, b_ref, hn_ref, xh_ref, rs_ref, o_ref):
        h = h_ref[...]
        q = _dot(h, wq_ref[...])
        kb_ = k_ref[...]
        vb_ = v_ref[...]
        for hh in range(XA_HEADS):
            sl = slice(hh * XA_HEAD_DIM, (hh + 1) * XA_HEAD_DIM)
            p = _attn_probs(q, kb_, hh)
            o_ref[:, sl] = _dot(p, vb_[:, sl]).astype(o_ref.dtype)
        r = ALPHA * h + _dot(o_ref[...], wo_ref[...])
        out, xhat, rstd = _ln_fwd(r, g_ref[...], b_ref[...])
        hn_ref[...] = out
        xh_ref[...] = xhat
        rs_ref[...] = rstd

    return pl.pallas_call(
        body, name="attn_ln_fwd", grid=(t // TM,),
        in_specs=[_rows(TM, D_MODEL), _const((D_MODEL, D_MODEL)), _const((D_MODEL, D_MODEL)),
                  _const((MEM_LEN, D_MODEL)), _const((MEM_LEN, D_MODEL)), _const((1, D_MODEL)), _const((1, D_MODEL))],
        out_specs=[_rows(TM, D_MODEL), _rows(TM, D_MODEL), _rows(TM, 1), _rows(TM, D_MODEL)],
        out_shape=[jax.ShapeDtypeStruct((t, D_MODEL), F32), jax.ShapeDtypeStruct((t, D_MODEL), F32),
                   jax.ShapeDtypeStruct((t, 1), F32), jax.ShapeDtypeStruct((t, D_MODEL), MXU_DTYPE)],
        compiler_params=_cparams(("arbitrary",)),
    )(h1, wq, wo, kb, vb, g, b)


def _attn_ln_bwd(dh2, xhat, rstd, g, h1, wq, wo, kb, vb):
    t = h1.shape[0]

    def body(dh_ref, xh_ref, rs_ref, g_ref, h_ref, wq_ref, wo_ref, k_ref, v_ref,
             dr_ref, dq_ref, dh1_ref, dk_ref, dv_ref, dg_ref, db_ref):
        i = pl.program_id(0)

        @pl.when(i == 0)
        def _():
            dk_ref[...] = jnp.zeros_like(dk_ref)
            dv_ref[...] = jnp.zeros_like(dv_ref)
            dg_ref[...] = jnp.zeros_like(dg_ref)
            db_ref[...] = jnp.zeros_like(db_ref)

        dout = dh_ref[...]
        xh = xh_ref[...]
        dg_ref[...] += _sum0(dout * xh)
        db_ref[...] += _sum0(dout)
        dr = _ln_bwd(dout, xh, rs_ref[...], g_ref[...])
        dr_ref[...] = dr.astype(dr_ref.dtype)
        do = _dot_nt(dr, wo_ref[...])
        h = h_ref[...]
        q = _dot(h, wq_ref[...])
        kb_ = k_ref[...]
        vb_ = v_ref[...]
        scale = 1.0 / math.sqrt(XA_HEAD_DIM)
        for hh in range(XA_HEADS):
            sl = slice(hh * XA_HEAD_DIM, (hh + 1) * XA_HEAD_DIM)
            p = _attn_probs(q, kb_, hh)
            do_h = do[:, sl]
            dp = _dot_nt(do_h, vb_[:, sl])
            ds = p * (dp - jnp.sum(dp * p, axis=-1, keepdims=True)) * scale
            dq_ref[:, sl] = _dot(ds, kb_[:, sl]).astype(dq_ref.dtype)
            dk_ref[:, sl] += _dot_tn(ds, q[:, sl])
            dv_ref[:, sl] += _dot_tn(p, do_h)
        dh1_ref[...] = ALPHA * dr + _dot_nt(dq_ref[...], wq_ref[...])

    return pl.pallas_call(
        body, name="attn_ln_bwd", grid=(t // TM,),
        in_specs=[_rows(TM, D_MODEL), _rows(TM, D_MODEL), _rows(TM, 1), _const((1, D_MODEL)), _rows(TM, D_MODEL),
                  _const((D_MODEL, D_MODEL)), _const((D_MODEL, D_MODEL)), _const((MEM_LEN, D_MODEL)),
                  _const((MEM_LEN, D_MODEL))],
        out_specs=[_rows(TM, D_MODEL), _rows(TM, D_MODEL), _rows(TM, D_MODEL), _const((MEM_LEN, D_MODEL)),
                   _const((MEM_LEN, D_MODEL)), _const((1, D_MODEL)), _const((1, D_MODEL))],
        out_shape=[jax.ShapeDtypeStruct((t, D_MODEL), MXU_DTYPE), jax.ShapeDtypeStruct((t, D_MODEL), MXU_DTYPE),
                   jax.ShapeDtypeStruct((t, D_MODEL), F32), jax.ShapeDtypeStruct((MEM_LEN, D_MODEL), F32),
                   jax.ShapeDtypeStruct((MEM_LEN, D_MODEL), F32), jax.ShapeDtypeStruct((1, D_MODEL), F32),
                   jax.ShapeDtypeStruct((1, D_MODEL), F32)],
        compiler_params=_cparams(("arbitrary",)),
    )(dh2, xhat, rstd, g, h1, wq, wo, kb, vb)


FF_CHUNK = 1024
N_FF = D_FF // FF_CHUNK


def _load_resident(pairs, sems):
    copies = [pltpu.make_async_copy(src, dst, sems.at[k]) for k, (src, dst) in enumerate(pairs)]
    for cp in copies:
        cp.start()
    for cp in copies:
        cp.wait()


def _mlp_ln_fwd(h2, w1, w2, g, b):
    t = h2.shape[0]

    def body(h_ref, w1_hbm, w2_hbm, g_ref, b_ref, hn_ref, xh_ref, rs_ref, hd_ref, w1_v, w2_v, acc_ref, sems):
        @pl.when(pl.program_id(0) == 0)
        def _():
            _load_resident([(w1_hbm, w1_v), (w2_hbm, w2_v)], sems)

        h = h_ref[...]
        hb = h.astype(MXU_DTYPE)
        acc_ref[...] = ALPHA * h
        for j in range(N_FF):
            sl = slice(j * FF_CHUNK, (j + 1) * FF_CHUNK)
            u = _dot(hb, w1_v[:, sl])
            hd = jnp.square(jnp.maximum(u, 0.0)).astype(MXU_DTYPE)
            hd_ref[:, sl] = hd
            acc_ref[...] += _dot(hd, w2_v[sl, :])
        out, xhat, rstd = _ln_fwd(acc_ref[...], g_ref[...], b_ref[...])
        hn_ref[...] = out
        xh_ref[...] = xhat
        rs_ref[...] = rstd

    return pl.pallas_call(
        body, name="mlp_ln_fwd", grid=(t // TM,),
        in_specs=[_rows(TM, D_MODEL), _hbm(), _hbm(), _const((1, D_MODEL)), _const((1, D_MODEL))],
        out_specs=[_rows(TM, D_MODEL), _rows(TM, D_MODEL), _rows(TM, 1), _rows(TM, D_FF)],
        out_shape=[jax.ShapeDtypeStruct((t, D_MODEL), F32), jax.ShapeDtypeStruct((t, D_MODEL), F32),
                   jax.ShapeDtypeStruct((t, 1), F32), jax.ShapeDtypeStruct((t, D_FF), MXU_DTYPE)],
        scratch_shapes=[pltpu.VMEM((D_MODEL, D_FF), MXU_DTYPE), pltpu.VMEM((D_FF, D_MODEL), MXU_DTYPE),
                        pltpu.VMEM((TM, D_MODEL), F32), pltpu.SemaphoreType.DMA((2,))],
        compiler_params=_cparams(("arbitrary",)),
    )(h2, w1, w2, g, b)


def _mlp_ln_bwd(dh3, xhat, rstd, g, hdn, w1, w2):
    t = dh3.shape[0]

    def body(dh_ref, xh_ref, rs_ref, g_ref, hd_ref, w1_hbm, w2_hbm,
             dr_ref, du_ref, dh2_ref, dg_ref, db_ref, w1_v, w2_v, acc_ref, sems):
        @pl.when(pl.program_id(0) == 0)
        def _():
            _load_resident([(w1_hbm, w1_v), (w2_hbm, w2_v)], sems)
            dg_ref[...] = jnp.zeros_like(dg_ref)
            db_ref[...] = jnp.zeros_like(db_ref)

        dout = dh_ref[...]
        xh = xh_ref[...]
        dg_ref[...] += _sum0(dout * xh)
        db_ref[...] += _sum0(dout)
        dr = _ln_bwd(dout, xh, rs_ref[...], g_ref[...])
        drb = dr.astype(MXU_DTYPE)
        dr_ref[...] = drb
        acc_ref[...] = ALPHA * dr
        for j in range(N_FF):
            sl = slice(j * FF_CHUNK, (j + 1) * FF_CHUNK)
            dhd = _dot_nt(drb, w2_v[sl, :])
            du = (dhd * (2.0 * jnp.sqrt(hd_ref[:, sl].astype(F32)))).astype(MXU_DTYPE)
            du_ref[:, sl] = du
            acc_ref[...] += _dot_nt(du, w1_v[:, sl])
        dh2_ref[...] = acc_ref[...]

    tm = TM // 2
    return pl.pallas_call(
        body, name="mlp_ln_bwd", grid=(t // tm,),
        in_specs=[_rows(tm, D_MODEL), _rows(tm, D_MODEL), _rows(tm, 1), _const((1, D_MODEL)), _rows(tm, D_FF),
                  _hbm(), _hbm()],
        out_specs=[_rows(tm, D_MODEL), _rows(tm, D_FF), _rows(tm, D_MODEL), _const((1, D_MODEL)),
                   _const((1, D_MODEL))],
        out_shape=[jax.ShapeDtypeStruct((t, D_MODEL), MXU_DTYPE), jax.ShapeDtypeStruct((t, D_FF), MXU_DTYPE),
                   jax.ShapeDtypeStruct((t, D_MODEL), F32), jax.ShapeDtypeStruct((1, D_MODEL), F32),
                   jax.ShapeDtypeStruct((1, D_MODEL), F32)],
        scratch_shapes=[pltpu.VMEM((D_MODEL, D_FF), MXU_DTYPE), pltpu.VMEM((D_FF, D_MODEL), MXU_DTYPE),
                        pltpu.VMEM((tm, D_MODEL), F32), pltpu.SemaphoreType.DMA((2,))],
        compiler_params=_cparams(("arbitrary",)),
    )(dh3, xhat, rstd, g, hdn, w1, w2)


def _outproj_ln_bwd(dh1, xhat, rstd, g, w):
    t = dh1.shape[0]

    def body(dh_ref, xh_ref, rs_ref, g_ref, w_ref, dr_ref, res_ref, dy_ref, dg_ref, db_ref):
        i = pl.program_id(0)

        @pl.when(i == 0)
        def _():
            dg_ref[...] = jnp.zeros_like(dg_ref)
            db_ref[...] = jnp.zeros_like(db_ref)

        dout = dh_ref[...]
        xh = xh_ref[...]
        dg_ref[...] += _sum0(dout * xh)
        db_ref[...] += _sum0(dout)
        dr = _ln_bwd(dout, xh, rs_ref[...], g_ref[...])
        dr_ref[...] = dr.astype(dr_ref.dtype)
        res_ref[...] = ALPHA * dr
        dy_ref[...] = _dot_nt(dr, w_ref[...])

    return pl.pallas_call(
        body, name="outproj_ln_bwd", grid=(t // TM,),
        in_specs=[_rows(TM, D_MODEL), _rows(TM, D_MODEL), _rows(TM, 1), _const((1, D_MODEL)),
                  _const((D_MODEL, D_MODEL))],
        out_specs=[_rows(TM, D_MODEL), _rows(TM, D_MODEL), _rows(TM, D_MODEL), _const((1, D_MODEL)),
                   _const((1, D_MODEL))],
        out_shape=[jax.ShapeDtypeStruct((t, D_MODEL), MXU_DTYPE), jax.ShapeDtypeStruct((t, D_MODEL), F32),
                   jax.ShapeDtypeStruct((t, D_MODEL), F32), jax.ShapeDtypeStruct((1, D_MODEL), F32),
                   jax.ShapeDtypeStruct((1, D_MODEL), F32)],
        compiler_params=_cparams(("arbitrary",)),
    )(dh1, xhat, rstd, g, w)


def _loss_fwd_bwd(h, target):
    t = h.shape[0]

    def body(h_ref, t_ref, l_ref, dh_ref):
        i = pl.program_id(0)

        @pl.when(i == 0)
        def _():
            l_ref[...] = jnp.zeros_like(l_ref)

        e = h_ref[...] - t_ref[...]
        dh_ref[...] = e * (1.0 / D_MODEL)
        per_tok = jnp.mean(e * e, axis=-1, keepdims=True)
        l_ref[...] += 0.5 * jnp.sum(per_tok, axis=0, keepdims=True)

    return pl.pallas_call(
        body, name="loss_fwd_bwd", grid=(t // TM,),
        in_specs=[_rows(TM, D_MODEL), _rows(TM, D_MODEL)],
        out_specs=[_const((1, 1)), _rows(TM, D_MODEL)],
        out_shape=[jax.ShapeDtypeStruct((1, 1), F32), jax.ShapeDtypeStruct((t, D_MODEL), F32)],
        compiler_params=_cparams(("arbitrary",)),
    )(h, target)


def _pick_col(x, idx):
    lane = lax.broadcasted_iota(jnp.int32, x.shape, 1)
    return jnp.sum(jnp.where(lane == idx, x, 0.0), axis=1, keepdims=True)


def _pick_row(x, idx):
    sub = lax.broadcasted_iota(jnp.int32, x.shape, 0)
    return jnp.sum(jnp.where(sub == idx, x, 0.0), axis=0, keepdims=True)


def _conv_taps(pad_ref, w, tm, base):
    acc = w[0:1, :] * pad_ref[base:base + tm, :]
    for k in range(1, 4):
        acc = acc + w[k:k + 1, :] * pad_ref[base + k:base + k + tm, :]
    return acc


def _ssd_chunk_common(adt_c, tri):
    cs = _dot_f32(tri, adt_c)
    return cs, cs.T, jnp.exp(cs)


def _ssd_head_terms(cs, cst, ecs, dt_c, h, tri):
    cs_col = _pick_col(cs, h)
    cs_row = _pick_row(cst, h)
    dt_col = _pick_col(dt_c, h)
    cs_last = cs_col[SSD_CHUNK - 1:SSD_CHUNK, :]
    lmat = jnp.exp(jnp.where(tri > 0.0, cs_col - cs_row, -1e30))
    ecs_col = _pick_col(ecs, h)
    decay_col = jnp.exp(cs_last - cs_col)
    return cs_col, dt_col, cs_last, lmat, ecs_col, decay_col


def _ssd_fwd(proj, cw, cb, dtb, a_neg, d_lanes, nw):
    t = proj.shape[0]
    tm = SSD_TM
    nt = t // tm
    ncq = tm // SSD_CHUNK
    hb = tm // SUBLANES

    def body(xbc_ref, halo_ref, z_ref, dt_ref, cw_ref, cb_ref, dtb_ref, a_ref, d_ref, nw_ref,
             y_ref, yy_ref, st_ref, xpad, xact, state):
        i = pl.program_id(0)

        @pl.when(i == 0)
        def _():
            state[...] = jnp.zeros_like(state)

        xpad[0:SUBLANES, :] = jnp.where(i > 0, halo_ref[...], 0.0)
        xpad[SUBLANES:SUBLANES + tm, :] = xbc_ref[...]
        acc = cb_ref[...] + _conv_taps(xpad, cw_ref[...], tm, SUBLANES - 3)
        xact[...] = acc * _sigmoid(acc)
        dt = _softplus(dt_ref[...] + dtb_ref[...])
        adt = dt * a_ref[...]
        r_i = lax.broadcasted_iota(jnp.int32, (SSD_CHUNK, SSD_CHUNK), 0)
        c_i = lax.broadcasted_iota(jnp.int32, (SSD_CHUNK, SSD_CHUNK), 1)
        tri = (r_i >= c_i).astype(F32)
        lane1 = lax.broadcasted_iota(jnp.int32, (1, LANES), 1)
        for c in range(ncq):
            sl = slice(c * SSD_CHUNK, (c + 1) * SSD_CHUNK)
            dt_c = dt[sl]
            cs, cst, ecs = _ssd_chunk_common(adt[sl], tri)
            for g in range(2):
                bg = xact[sl, 512 + g * 128:512 + (g + 1) * 128]
                cg = xact[sl, 768 + g * 128:768 + (g + 1) * 128]
                cbm = _dot_nt(cg, bg)
                for pr in range(2):
                    pi = g * 2 + pr
                    psl = slice(pi * 128, (pi + 1) * 128)
                    xp = xact[sl, psl]
                    prev = state[pi]
                    st_ref[c, pi] = prev
                    yp = xp * d_ref[:, psl]
                    new_s = jnp.zeros((SSD_STATE, LANES), F32)
                    dec_lane = jnp.zeros((1, LANES), F32)
                    for hh in range(2):
                        h = g * 4 + pr * 2 + hh
                        lm = (lane1 >= 64) if hh else (lane1 < 64)
                        _, dt_col, cs_last, lmat, ecs_col, decay_col = _ssd_head_terms(cs, cst, ecs, dt_c, h, tri)
                        xdt = jnp.where(lm, xp, 0.0) * dt_col
                        yp = yp + _dot(cbm * lmat, xdt)
                        yp = yp + _dot(cg * ecs_col, jnp.where(lm, prev, 0.0))
                        new_s = new_s + _dot_tn(bg * decay_col, xdt)
                        dec_lane = dec_lane + jnp.where(lm, jnp.exp(cs_last), 0.0)
                    state[pi] = prev * dec_lane + new_s
                    yy_ref[sl, psl] = yp
        yy = yy_ref[...]
        z = z_ref[...]
        yg = yy * (z * _sigmoid(z))
        ms = jnp.mean(yg * yg, axis=-1, keepdims=True)
        y_ref[...] = (yg * lax.rsqrt(ms + LN_EPS) * nw_ref[...]).astype(y_ref.dtype)

    halo_map = lambda i: (jnp.maximum(i * hb - 1, 0), 0)
    return pl.pallas_call(
        body, name="ssd_fwd", grid=(nt,),
        in_specs=[pl.BlockSpec((tm, SSD_XBC), lambda i: (i, 0)), pl.BlockSpec((SUBLANES, SSD_XBC), halo_map),
                  pl.BlockSpec((tm, SSD_WIDTH), lambda i: (i, P_Z // SSD_WIDTH)),
                  pl.BlockSpec((tm, LANES), lambda i: (i, P_DT // LANES)),
                  _const((4, SSD_XBC)), _const((1, SSD_XBC)), _const((1, LANES)), _const((1, LANES)),
                  _const((1, SSD_WIDTH)), _const((1, SSD_WIDTH))],
        out_specs=[_rows(tm, SSD_WIDTH), _rows(tm, SSD_WIDTH),
                   pl.BlockSpec((ncq, 4, SSD_STATE, LANES), lambda i: (i, 0, 0, 0))],
        out_shape=[jax.ShapeDtypeStruct((t, SSD_WIDTH), MXU_DTYPE), jax.ShapeDtypeStruct((t, SSD_WIDTH), F32),
                   jax.ShapeDtypeStruct((t // SSD_CHUNK, 4, SSD_STATE, LANES), F32)],
        scratch_shapes=[pltpu.VMEM((tm + SUBLANES, SSD_XBC), F32), pltpu.VMEM((tm, SSD_XBC), F32),
                        pltpu.VMEM((4, SSD_STATE, LANES), F32)],
        compiler_params=_cparams(("arbitrary",)),
    )(proj, proj, proj, proj, cw, cb, dtb, a_neg, d_lanes, nw)


def _ssd_bwd(dycat, proj, yy, states, cw, cb, dtb, a_neg, d_lanes, nw):
    t = proj.shape[0]
    tm = SSD_TM
    nt = t // tm
    ncq = tm // SSD_CHUNK
    hb = tm // SUBLANES

    def body(dy_ref, xbc_ref, halo_ref, z_ref, dt_ref, yy_ref, st_ref, cw_ref, cb_ref, dtb_ref, a_ref, d_ref, nw_ref,
             dxbc_ref, dz_ref, ddt_ref, dcw_ref, dcb_ref, ddtb_ref, da_ref, dd_ref, dnw_ref,
             xpad, xact, dxact, dpad, dstate, dnext):
        i = pl.program_id(0)

        @pl.when(i == 0)
        def _():
            for r in (dcw_ref, dcb_ref, ddtb_ref, da_ref, dd_ref, dnw_ref, dstate, dnext):
                r[...] = jnp.zeros_like(r)

        xpad[0:SUBLANES, :] = jnp.where(i < nt - 1, halo_ref[...], 0.0)
        xpad[SUBLANES:SUBLANES + tm, :] = xbc_ref[...]
        cw_v = cw_ref[...]
        acc = cb_ref[...] + _conv_taps(xpad, cw_v, tm, SUBLANES - 3)
        sig = _sigmoid(acc)
        xact[...] = acc * sig
        dt_raw = dt_ref[...] + dtb_ref[...]
        dt = _softplus(dt_raw)
        a_v = a_ref[...]
        adt = dt * a_v
        yy = yy_ref[...]
        z = z_ref[...]
        sz = _sigmoid(z)
        siluz = z * sz
        yg = yy * siluz
        ms = jnp.mean(yg * yg, axis=-1, keepdims=True)
        rinv = lax.rsqrt(ms + LN_EPS)
        dout = dy_ref[...]
        dnw_ref[...] += _sum0(dout * yg * rinv)
        dyn = dout * nw_ref[...]
        dyg = rinv * dyn - yg * (rinv * rinv * rinv) * jnp.mean(dyn * yg, axis=-1, keepdims=True)
        dyy = dyg * siluz
        dz_ref[...] = (dyg * yy * (sz * (1.0 + z * (1.0 - sz)))).astype(dz_ref.dtype)
        dd_ref[...] += _sum0(dyy * xact[:, 0:SSD_WIDTH])

        r_i = lax.broadcasted_iota(jnp.int32, (SSD_CHUNK, SSD_CHUNK), 0)
        c_i = lax.broadcasted_iota(jnp.int32, (SSD_CHUNK, SSD_CHUNK), 1)
        tri = (r_i >= c_i).astype(F32)
        lane1 = lax.broadcasted_iota(jnp.int32, (1, LANES), 1)
        for c in reversed(range(ncq)):
            sl = slice(c * SSD_CHUNK, (c + 1) * SSD_CHUNK)
            dt_c = dt[sl]
            cs, cst, ecs = _ssd_chunk_common(adt[sl], tri)
            cacc = jnp.zeros((SSD_CHUNK, LANES), F32)
            racc = jnp.zeros((SSD_CHUNK, LANES), F32)
            ddtx = jnp.zeros((SSD_CHUNK, LANES), F32)
            for g in range(2):
                bg = xact[sl, 512 + g * 128:512 + (g + 1) * 128]
                cg = xact[sl, 768 + g * 128:768 + (g + 1) * 128]
                cbm = _dot_nt(cg, bg)
                dcb_m = jnp.zeros((SSD_CHUNK, SSD_CHUNK), F32)
                dbg = jnp.zeros((SSD_CHUNK, SSD_STATE), F32)
                dcg = jnp.zeros((SSD_CHUNK, SSD_STATE), F32)
                for pr in range(2):
                    pi = g * 2 + pr
                    psl = slice(pi * 128, (pi + 1) * 128)
                    xp = xact[sl, psl]
                    dyp = dyy[sl, psl]
                    prev = st_ref[c, pi]
                    ds_all = dstate[pi]
                    dxdt_p = jnp.zeros((SSD_CHUNK, LANES), F32)
                    dprev_new = jnp.zeros((SSD_STATE, LANES), F32)
                    dec_lane = jnp.zeros((1, LANES), F32)
                    dt_lanes = jnp.zeros((SSD_CHUNK, LANES), F32)
                    for hh in range(2):
                        h = g * 4 + pr * 2 + hh
                        lm = (lane1 >= 64) if hh else (lane1 < 64)
                        oh_l = (c_i == h).astype(F32)
                        oh_s = (r_i == h).astype(F32)
                        _, dt_col, cs_last, lmat, ecs_col, decay_col = _ssd_head_terms(cs, cst, ecs, dt_c, h, tri)
                        gm = cbm * lmat
                        xm = jnp.where(lm, xp, 0.0)
                        xdt = xm * dt_col
                        dym = jnp.where(lm, dyp, 0.0)
                        prevm = jnp.where(lm, prev, 0.0)
                        dsm = jnp.where(lm, ds_all, 0.0)
                        bdec = bg * decay_col
                        dxdt = _dot_tn(gm, dym) + _dot(bdec, dsm)
                        dxdt_p = dxdt_p + dxdt
                        ddtx = ddtx + oh_l * jnp.sum(dxdt * xm, axis=1, keepdims=True)
                        dt_lanes = dt_lanes + jnp.where(lm, dt_col, 0.0)
                        dgm = _dot_nt(dym, xdt)
                        dcb_m = dcb_m + dgm * lmat
                        w = dgm * gm
                        cacc = cacc + oh_l * jnp.sum(w, axis=1, keepdims=True)
                        racc = racc - oh_s * jnp.sum(w, axis=0, keepdims=True)
                        dce = _dot_nt(dym, prevm)
                        dcg = dcg + dce * ecs_col
                        cacc = cacc + oh_l * (jnp.sum(dce * cg, axis=1, keepdims=True) * ecs_col)
                        dprev_new = dprev_new + _dot_tn(cg * ecs_col, dym)
                        dbdec = _dot_nt(xdt, dsm)
                        dbg = dbg + dbdec * decay_col
                        dd = jnp.sum(dbdec * bg, axis=1, keepdims=True) * decay_col
                        cacc = cacc - oh_l * dd
                        cd = jnp.exp(cs_last)
                        dlast = jnp.sum(dd, axis=0, keepdims=True) + jnp.sum(
                            jnp.sum(dsm * prevm, axis=1, keepdims=True), axis=0, keepdims=True) * cd
                        cacc = cacc + jnp.where((r_i == SSD_CHUNK - 1) & (c_i == h), dlast, 0.0)
                        dec_lane = dec_lane + jnp.where(lm, cd, 0.0)
                    dstate[pi] = ds_all * dec_lane + dprev_new
                    dxact[sl, psl] = dxdt_p * dt_lanes + dyp * d_ref[:, psl]
                dcg = dcg + _dot(dcb_m, bg)
                dbg = dbg + _dot_tn(dcb_m, cg)
                dxact[sl, 512 + g * 128:512 + (g + 1) * 128] = dbg
                dxact[sl, 768 + g * 128:768 + (g + 1) * 128] = dcg
            dcs = cacc + racc.T
            dadt = _dot_f32((r_i <= c_i).astype(F32), dcs)
            ddt = dadt * a_v + ddtx
            da_ref[...] += _sum0(dadt * dt_c)
            ddt_raw = ddt * _sigmoid(dt_raw[sl])
            ddt_ref[sl, :] = ddt_raw.astype(ddt_ref.dtype)
            ddtb_ref[...] += _sum0(ddt_raw)
        dacc = dxact[...] * (sig * (1.0 + acc * (1.0 - sig)))
        dcb_ref[...] += _sum0(dacc)
        for k in range(4):
            dcw_ref[k:k + 1, :] += _sum0(dacc * xpad[SUBLANES - 3 + k:SUBLANES - 3 + k + tm, :])
        dpad[0:tm, :] = dacc
        dpad[tm:tm + SUBLANES, :] = dnext[...]
        dx = cw_v[0:1, :] * dpad[3:3 + tm, :]
        for k in range(1, 4):
            dx = dx + cw_v[k:k + 1, :] * dpad[3 - k:3 - k + tm, :]
        dxbc_ref[...] = dx.astype(dxbc_ref.dtype)
        dnext[...] = dacc[0:SUBLANES, :]

    rev = lambda i: nt - 1 - i
    halo_map = lambda i: (jnp.maximum(rev(i) * hb - 1, 0), 0)
    rrow = lambda n, col=0: pl.BlockSpec((tm, n), lambda i: (rev(i), col))
    return pl.pallas_call(
        body, name="ssd_bwd", grid=(nt,),
        in_specs=[rrow(SSD_WIDTH), rrow(SSD_XBC), pl.BlockSpec((SUBLANES, SSD_XBC), halo_map),
                  rrow(SSD_WIDTH, P_Z // SSD_WIDTH), rrow(LANES, P_DT // LANES), rrow(SSD_WIDTH),
                  pl.BlockSpec((ncq, 4, SSD_STATE, LANES), lambda i: (rev(i), 0, 0, 0)),
                  _const((4, SSD_XBC)), _const((1, SSD_XBC)), _const((1, LANES)), _const((1, LANES)),
                  _const((1, SSD_WIDTH)), _const((1, SSD_WIDTH))],
        out_specs=[rrow(SSD_XBC), rrow(SSD_WIDTH), rrow(LANES), _const((SUBLANES, SSD_XBC)), _const((1, SSD_XBC)),
                   _const((1, LANES)), _const((1, LANES)), _const((1, SSD_WIDTH)), _const((1, SSD_WIDTH))],
        out_shape=[jax.ShapeDtypeStruct((t, SSD_XBC), MXU_DTYPE), jax.ShapeDtypeStruct((t, SSD_WIDTH), MXU_DTYPE),
                   jax.ShapeDtypeStruct((t, LANES), MXU_DTYPE), jax.ShapeDtypeStruct((SUBLANES, SSD_XBC), F32),
                   jax.ShapeDtypeStruct((1, SSD_XBC), F32), jax.ShapeDtypeStruct((1, LANES), F32),
                   jax.ShapeDtypeStruct((1, LANES), F32), jax.ShapeDtypeStruct((1, SSD_WIDTH), F32),
                   jax.ShapeDtypeStruct((1, SSD_WIDTH), F32)],
        scratch_shapes=[pltpu.VMEM((tm + SUBLANES, SSD_XBC), F32), pltpu.VMEM((tm, SSD_XBC), F32),
                        pltpu.VMEM((tm, SSD_XBC), F32), pltpu.VMEM((tm + SUBLANES, SSD_XBC), F32),
                        pltpu.VMEM((4, SSD_STATE, LANES), F32), pltpu.VMEM((SUBLANES, SSD_XBC), F32)],
        compiler_params=_cparams(("arbitrary",)),
    )(dycat, proj, proj, proj, proj, yy, states, cw, cb, dtb, a_neg, d_lanes, nw)


def _cmul_add(ar, ai, br, bi, cr, ci):
    return ar + br * cr - bi * ci, ai + br * ci + bi * cr


def _s5_fwd(proj, bre, bim, cre, cim, d_skip, glu_w, glu_b, coef):
    t = proj.shape[0]
    tm = SCAN_TM
    ng = tm // SUBLANES

    def body(u_ref, bre_ref, bim_ref, cre_ref, cim_ref, d_ref, w_ref, b_ref, coef_ref,
             y_ref, y2_ref, hre_ref, him_ref, carry):
        i = pl.program_id(0)

        @pl.when(i == 0)
        def _():
            carry[...] = jnp.zeros_like(carry)

        u = u_ref[...]
        hre_ref[...] = _dot(u, bre_ref[...])
        him_ref[...] = _dot(u, bim_ref[...])

        def step(gi, car):
            cr_, ci_ = car
            rows = pl.ds(pl.multiple_of(gi * SUBLANES, SUBLANES), SUBLANES)
            r = hre_ref[rows, :]
            m = him_ref[rows, :]
            for k, sh in enumerate((1, 2, 4)):
                r, m = _cmul_add(r, m, coef_ref[k, 0], coef_ref[k, 1], pltpu.roll(r, sh, 0), pltpu.roll(m, sh, 0))
            r, m = _cmul_add(r, m, coef_ref[3, 0], coef_ref[3, 1], cr_, ci_)
            hre_ref[rows, :] = r
            him_ref[rows, :] = m
            return (jnp.broadcast_to(r[SUBLANES - 1:SUBLANES, :], r.shape),
                    jnp.broadcast_to(m[SUBLANES - 1:SUBLANES, :], m.shape))

        cr_, ci_ = lax.fori_loop(0, ng, step, (carry[0], carry[1]))
        carry[0] = cr_
        carry[1] = ci_
        y2 = _dot(hre_ref[...], cre_ref[...]) - _dot(him_ref[...], cim_ref[...]) + d_ref[...] * u
        y2_ref[...] = y2
        ya = _gelu(y2)
        y_ref[...] = (ya * _sigmoid(_dot(ya, w_ref[...]) + b_ref[...])).astype(y_ref.dtype)

    return pl.pallas_call(
        body, name="s5_fwd", grid=(t // tm,),
        in_specs=[pl.BlockSpec((tm, S5_WIDTH), lambda i: (i, P_U // S5_WIDTH)),
                  _const((S5_WIDTH, S5_NSTATE)), _const((S5_WIDTH, S5_NSTATE)), _const((S5_NSTATE, S5_WIDTH)),
                  _const((S5_NSTATE, S5_WIDTH)), _const((1, S5_WIDTH)), _const((S5_WIDTH, S5_WIDTH)),
                  _const((1, S5_WIDTH)), _const((5, 2, SUBLANES, S5_NSTATE))],
        out_specs=[_rows(tm, S5_WIDTH), _rows(tm, S5_WIDTH), _rows(tm, S5_NSTATE), _rows(tm, S5_NSTATE)],
        out_shape=[jax.ShapeDtypeStruct((t, S5_WIDTH), MXU_DTYPE), jax.ShapeDtypeStruct((t, S5_WIDTH), F32),
                   jax.ShapeDtypeStruct((t, S5_NSTATE), F32), jax.ShapeDtypeStruct((t, S5_NSTATE), F32)],
        scratch_shapes=[pltpu.VMEM((2, SUBLANES, S5_NSTATE), F32)],
        compiler_params=_cparams(("arbitrary",)),
    )(proj, bre, bim, cre, cim, d_skip, glu_w, glu_b, coef)


def _s5_bwd(dycat, proj, y2, hre, him, bre, bim, cre, cim, d_skip, glu_w, glu_b, rcoef):
    t = proj.shape[0]
    tm = SCAN_TM
    nt = t // tm
    ng = tm // SUBLANES
    hb = tm // SUBLANES

    def body(dy_ref, u_ref, y2_ref, hre_ref, him_ref, hre_halo, him_halo, bre_ref, bim_ref, cre_ref, cim_ref, d_ref,
             w_ref, b_ref, coef_ref,
             du_ref, dbre_ref, dbim_ref, dcre_ref, dcim_ref, dlam_ref, dd_ref, dw_ref, dgb_ref,
             gre, gim, hpre, hpim, carry):
        i = pl.program_id(0)

        @pl.when(i == 0)
        def _():
            for r in (dbre_ref, dbim_ref, dcre_ref, dcim_ref, dlam_ref, dd_ref, dw_ref, dgb_ref, carry):
                r[...] = jnp.zeros_like(r)

        u = u_ref[...]
        y2 = y2_ref[...]
        dout = dy_ref[...]
        ya = _gelu(y2)
        sg = _sigmoid(_dot(ya, w_ref[...]) + b_ref[...])
        dv = dout * ya * sg * (1.0 - sg)
        dya = dout * sg + _dot_nt(dv, w_ref[...])
        dw_ref[...] += _dot_tn(ya, dv)
        dgb_ref[...] += _sum0(dv)
        dy2 = dya * _gelu_grad(y2)
        dd_ref[...] += _sum0(dy2 * u)
        hre_v = hre_ref[...]
        him_v = him_ref[...]
        dcre_ref[...] += _dot_tn(hre_v, dy2)
        dcim_ref[...] -= _dot_tn(him_v, dy2)
        gre[...] = _dot_nt(dy2, cre_ref[...])
        gim[...] = -_dot_nt(dy2, cim_ref[...])
        first = i == nt - 1
        hpre[0:SUBLANES, :] = jnp.where(first, 0.0, hre_halo[...])
        hpim[0:SUBLANES, :] = jnp.where(first, 0.0, him_halo[...])
        hpre[SUBLANES:SUBLANES + tm, :] = hre_v
        hpim[SUBLANES:SUBLANES + tm, :] = him_v
        row0 = lax.broadcasted_iota(jnp.int32, (SUBLANES, S5_NSTATE), 0) == 0

        def step(k, car):
            cr_, ci_, dlr, dli = car
            gi = ng - 1 - k
            rows = pl.ds(pl.multiple_of(gi * SUBLANES, SUBLANES), SUBLANES)
            nrows = pl.ds(pl.multiple_of(gi * SUBLANES + SUBLANES, SUBLANES), SUBLANES)
            r = gre[rows, :]
            m = gim[rows, :]
            for kk, sh in enumerate((1, 2, 4)):
                r, m = _cmul_add(r, m, coef_ref[kk, 0], coef_ref[kk, 1], pltpu.roll(r, SUBLANES - sh, 0),
                                 pltpu.roll(m, SUBLANES - sh, 0))
            r, m = _cmul_add(r, m, coef_ref[3, 0], coef_ref[3, 1], cr_, ci_)
            gre[rows, :] = r
            gim[rows, :] = m
            pr_ = hpre[rows, :]
            pm_ = hpim[rows, :]
            hr_ = jnp.where(row0, jnp.broadcast_to(pr_[SUBLANES - 1:SUBLANES, :], pr_.shape),
                            pltpu.roll(hpre[nrows, :], 1, 0))
            hm_ = jnp.where(row0, jnp.broadcast_to(pm_[SUBLANES - 1:SUBLANES, :], pm_.shape),
                            pltpu.roll(hpim[nrows, :], 1, 0))
            dlr = dlr + hr_ * r + hm_ * m
            dli = dli + hr_ * m - hm_ * r
            return (jnp.broadcast_to(r[0:1, :], r.shape), jnp.broadcast_to(m[0:1, :], m.shape), dlr, dli)

        z8 = jnp.zeros((SUBLANES, S5_NSTATE), F32)
        cr_, ci_, dlr, dli = lax.fori_loop(0, ng, step, (carry[0], carry[1], z8, z8))
        carry[0] = cr_
        carry[1] = ci_
        dlam_ref[0] += dlr
        dlam_ref[1] += dli
        g_re = gre[...]
        g_im = gim[...]
        du_ref[...] = (dy2 * d_ref[...] + _dot_nt(g_re, bre_ref[...]) + _dot_nt(g_im, bim_ref[...])
                       ).astype(du_ref.dtype)
        dbre_ref[...] += _dot_tn(u, g_re)
        dbim_ref[...] += _dot_tn(u, g_im)

    rev = lambda i: nt - 1 - i
    rrow = lambda n, col=0: pl.BlockSpec((tm, n), lambda i: (rev(i), col))
    halo = pl.BlockSpec((SUBLANES, S5_NSTATE), lambda i: (jnp.maximum(rev(i) * hb - 1, 0), 0))
    return pl.pallas_call(
        body, name="s5_bwd", grid=(nt,),
        in_specs=[rrow(S5_WIDTH, 512 // S5_WIDTH), rrow(S5_WIDTH, P_U // S5_WIDTH), rrow(S5_WIDTH),
                  rrow(S5_NSTATE), rrow(S5_NSTATE), halo, halo,
                  _const((S5_WIDTH, S5_NSTATE)), _const((S5_WIDTH, S5_NSTATE)), _const((S5_NSTATE, S5_WIDTH)),
                  _const((S5_NSTATE, S5_WIDTH)), _const((1, S5_WIDTH)), _const((S5_WIDTH, S5_WIDTH)),
                  _const((1, S5_WIDTH)), _const((5, 2, SUBLANES, S5_NSTATE))],
        out_specs=[rrow(S5_WIDTH), _const((S5_WIDTH, S5_NSTATE)), _const((S5_WIDTH, S5_NSTATE)),
                   _const((S5_NSTATE, S5_WIDTH)), _const((S5_NSTATE, S5_WIDTH)), _const((2, SUBLANES, S5_NSTATE)),
                   _const((1, S5_WIDTH)), _const((S5_WIDTH, S5_WIDTH)), _const((1, S5_WIDTH))],
        out_shape=[jax.ShapeDtypeStruct((t, S5_WIDTH), MXU_DTYPE), jax.ShapeDtypeStruct((S5_WIDTH, S5_NSTATE), F32),
                   jax.ShapeDtypeStruct((S5_WIDTH, S5_NSTATE), F32), jax.ShapeDtypeStruct((S5_NSTATE, S5_WIDTH), F32),
                   jax.ShapeDtypeStruct((S5_NSTATE, S5_WIDTH), F32),
                   jax.ShapeDtypeStruct((2, SUBLANES, S5_NSTATE), F32), jax.ShapeDtypeStruct((1, S5_WIDTH), F32),
                   jax.ShapeDtypeStruct((S5_WIDTH, S5_WIDTH), F32), jax.ShapeDtypeStruct((1, S5_WIDTH), F32)],
        scratch_shapes=[pltpu.VMEM((tm, S5_NSTATE), F32), pltpu.VMEM((tm, S5_NSTATE), F32),
                        pltpu.VMEM((tm + SUBLANES, S5_NSTATE), F32), pltpu.VMEM((tm + SUBLANES, S5_NSTATE), F32),
                        pltpu.VMEM((2, SUBLANES, S5_NSTATE), F32)],
        compiler_params=_cparams(("arbitrary",)),
    )(dycat, proj, y2, hre, him, hre, him, bre, bim, cre, cim, d_skip, glu_w, glu_b, rcoef)


def _rg_gates(xc, wa, ba, wx, bx, nsp):
    r = _sigmoid(_dot(xc, wa) + ba)
    ig = _sigmoid(_dot(xc, wx) + bx)
    log_a = nsp * r
    a = jnp.exp(log_a)
    mult = jnp.sqrt(-_expm1(2.0 * log_a))
    return r, ig, a, mult


def _rg_fwd(proj, cw, cb, wa, ba, wx, bx, nsp):
    t = proj.shape[0]
    tm = SCAN_TM
    ng = tm // SUBLANES
    hb = tm // SUBLANES

    def body(x_ref, halo_ref, gt_ref, cw_ref, cb_ref, wa_ref, ba_ref, wx_ref, bx_ref, nsp_ref,
             y_ref, h_ref, xpad, abuf, carry):
        i = pl.program_id(0)

        @pl.when(i == 0)
        def _():
            carry[...] = jnp.zeros_like(carry)

        xpad[0:SUBLANES, :] = jnp.where(i > 0, halo_ref[...], 0.0)
        xpad[SUBLANES:SUBLANES + tm, :] = x_ref[...]
        xc = cb_ref[...] + _conv_taps(xpad, cw_ref[...], tm, SUBLANES - 3)
        _, ig, a, mult = _rg_gates(xc, wa_ref[...], ba_ref[...], wx_ref[...], bx_ref[...], nsp_ref[...])
        abuf[...] = a
        h_ref[...] = mult * (ig * xc)
        sub = lax.broadcasted_iota(jnp.int32, (SUBLANES, RG_WIDTH), 0)

        def step(gi, car):
            rows = pl.ds(pl.multiple_of(gi * SUBLANES, SUBLANES), SUBLANES)
            av = abuf[rows, :]
            bv = h_ref[rows, :]
            for sh in (1, 2, 4):
                m = sub >= sh
                bv = jnp.where(m, av * pltpu.roll(bv, sh, 0) + bv, bv)
                av = jnp.where(m, av * pltpu.roll(av, sh, 0), av)
            hv = bv + av * car
            h_ref[rows, :] = hv
            return jnp.broadcast_to(hv[SUBLANES - 1:SUBLANES, :], hv.shape)

        carry[...] = lax.fori_loop(0, ng, step, carry[...])
        y_ref[...] = (h_ref[...] * _gelu(gt_ref[...])).astype(y_ref.dtype)

    return pl.pallas_call(
        body, name="rg_fwd", grid=(t // tm,),
        in_specs=[pl.BlockSpec((tm, RG_WIDTH), lambda i: (i, P_XRG // RG_WIDTH)),
                  pl.BlockSpec((SUBLANES, RG_WIDTH), lambda i: (jnp.maximum(i * hb - 1, 0), P_XRG // RG_WIDTH)),
                  pl.BlockSpec((tm, RG_WIDTH), lambda i: (i, P_GRG // RG_WIDTH)),
                  _const((4, RG_WIDTH)), _const((1, RG_WIDTH)), _const((RG_WIDTH, RG_WIDTH)), _const((1, RG_WIDTH)),
                  _const((RG_WIDTH, RG_WIDTH)), _const((1, RG_WIDTH)), _const((1, RG_WIDTH))],
        out_specs=[_rows(tm, RG_WIDTH), _rows(tm, RG_WIDTH)],
        out_shape=[jax.ShapeDtypeStruct((t, RG_WIDTH), MXU_DTYPE), jax.ShapeDtypeStruct((t, RG_WIDTH), F32)],
        scratch_shapes=[pltpu.VMEM((tm + SUBLANES, RG_WIDTH), F32), pltpu.VMEM((tm, RG_WIDTH), F32),
                        pltpu.VMEM((SUBLANES, RG_WIDTH), F32)],
        compiler_params=_cparams(("arbitrary",)),
    )(proj, proj, proj, cw, cb, wa, ba, wx, bx, nsp)


def _rg_bwd(dycat, proj, hs, cw, cb, wa, ba, wx, bx, nsp):
    t = proj.shape[0]
    tm = SCAN_TM
    nt = t // tm
    ng = tm // SUBLANES
    hb = tm // SUBLANES

    def body(dy_ref, x_ref, halo_ref, gt_ref, h_ref, h_halo, cw_ref, cb_ref, wa_ref, ba_ref, wx_ref, bx_ref, nsp_ref,
             dx_ref, dgt_ref, dcw_ref, dcb_ref, dwa_ref, dba_ref, dwx_ref, dbx_ref, dnsp_ref,
             xpad, abuf, gbuf, hpad, dabuf, dpad, carry, dnext):
        i = pl.program_id(0)

        @pl.when(i == 0)
        def _():
            for r in (dcw_ref, dcb_ref, dwa_ref, dba_ref, dwx_ref, dbx_ref, dnsp_ref, carry, dnext):
                r[...] = jnp.zeros_like(r)

        first = i == nt - 1
        xpad[0:SUBLANES, :] = jnp.where(first, 0.0, halo_ref[...])
        xpad[SUBLANES:SUBLANES + tm, :] = x_ref[...]
        cw_v = cw_ref[...]
        xc = cb_ref[...] + _conv_taps(xpad, cw_v, tm, SUBLANES - 3)
        nsp_v = nsp_ref[...]
        r, ig, a, mult = _rg_gates(xc, wa_ref[...], ba_ref[...], wx_ref[...], bx_ref[...], nsp_v)
        abuf[...] = a
        hv = h_ref[...]
        hpad[0:SUBLANES, :] = jnp.where(first, 0.0, h_halo[...])
        hpad[SUBLANES:SUBLANES + tm, :] = hv
        gt = gt_ref[...]
        dout = dy_ref[...]
        dgt_ref[...] = (dout * hv * _gelu_grad(gt)).astype(dgt_ref.dtype)
        gbuf[...] = dout * _gelu(gt)
        sub = lax.broadcasted_iota(jnp.int32, (SUBLANES, RG_WIDTH), 0)
        last_row = sub == SUBLANES - 1
        row0 = sub == 0

        def step(k, car):
            gi = ng - 1 - k
            rows = pl.ds(pl.multiple_of(gi * SUBLANES, SUBLANES), SUBLANES)
            nrows = pl.ds(pl.multiple_of(gi * SUBLANES + SUBLANES, SUBLANES), SUBLANES)
            av = abuf[rows, :]
            bv = gbuf[rows, :] + jnp.where(last_row, car, 0.0)
            ev = jnp.where(last_row, 0.0, pltpu.roll(av, SUBLANES - 1, 0))
            for sh in (1, 2, 4):
                m = sub < SUBLANES - sh
                bv = jnp.where(m, bv + ev * pltpu.roll(bv, SUBLANES - sh, 0), bv)
                ev = jnp.where(m, ev * pltpu.roll(ev, SUBLANES - sh, 0), 0.0)
            gbuf[rows, :] = bv
            pv = hpad[rows, :]
            hprev = jnp.where(row0, jnp.broadcast_to(pv[SUBLANES - 1:SUBLANES, :], pv.shape),
                              pltpu.roll(hpad[nrows, :], 1, 0))
            dabuf[rows, :] = bv * hprev
            return jnp.broadcast_to((av * bv)[0:1, :], bv.shape)

        carry[...] = lax.fori_loop(0, ng, step, carry[...])
        gv = gbuf[...]
        da = dabuf[...]
        ix = ig * xc
        dmult = gv * ix
        dig = gv * mult * xc
        dxc = gv * mult * ig
        dlog_a = da * a - dmult * (a * a) / mult
        dnsp_ref[...] += _sum0(dlog_a * r)
        dpr = dlog_a * nsp_v * r * (1.0 - r)
        dpi = dig * ig * (1.0 - ig)
        dxc = dxc + _dot_nt(dpr, wa_ref[...]) + _dot_nt(dpi, wx_ref[...])
        dwa_ref[...] += _dot_tn(xc, dpr)
        dwx_ref[...] += _dot_tn(xc, dpi)
        dba_ref[...] += _sum0(dpr)
        dbx_ref[...] += _sum0(dpi)
        dcb_ref[...] += _sum0(dxc)
        for k in range(4):
            dcw_ref[k:k + 1, :] += _sum0(dxc * xpad[SUBLANES - 3 + k:SUBLANES - 3 + k + tm, :])
        dpad[0:tm, :] = dxc
        dpad[tm:tm + SUBLANES, :] = dnext[...]
        dx = cw_v[0:1, :] * dpad[3:3 + tm, :]
        for k in range(1, 4):
            dx = dx + cw_v[k:k + 1, :] * dpad[3 - k:3 - k + tm, :]
        dx_ref[...] = dx.astype(dx_ref.dtype)
        dnext[...] = dxc[0:SUBLANES, :]

    rev = lambda i: nt - 1 - i
    rrow = lambda n, col=0: pl.BlockSpec((tm, n), lambda i: (rev(i), col))
    sq = _const((RG_WIDTH, RG_WIDTH))
    vec = _const((1, RG_WIDTH))
    return pl.pallas_call(
        body, name="rg_bwd", grid=(nt,),
        in_specs=[rrow(RG_WIDTH, 768 // RG_WIDTH), rrow(RG_WIDTH, P_XRG // RG_WIDTH),
                  pl.BlockSpec((SUBLANES, RG_WIDTH), lambda i: (jnp.maximum(rev(i) * hb - 1, 0), P_XRG // RG_WIDTH)),
                  rrow(RG_WIDTH, P_GRG // RG_WIDTH), rrow(RG_WIDTH),
                  pl.BlockSpec((SUBLANES, RG_WIDTH), lambda i: (jnp.maximum(rev(i) * hb - 1, 0), 0)),
                  _const((4, RG_WIDTH)), vec, sq, vec, sq, vec, vec],
        out_specs=[rrow(RG_WIDTH), rrow(RG_WIDTH), _const((SUBLANES, RG_WIDTH)), vec, sq, vec, sq, vec, vec],
        out_shape=[jax.ShapeDtypeStruct((t, RG_WIDTH), MXU_DTYPE), jax.ShapeDtypeStruct((t, RG_WIDTH), MXU_DTYPE),
                   jax.ShapeDtypeStruct((SUBLANES, RG_WIDTH), F32), jax.ShapeDtypeStruct((1, RG_WIDTH), F32),
                   jax.ShapeDtypeStruct((RG_WIDTH, RG_WIDTH), F32), jax.ShapeDtypeStruct((1, RG_WIDTH), F32),
                   jax.ShapeDtypeStruct((RG_WIDTH, RG_WIDTH), F32), jax.ShapeDtypeStruct((1, RG_WIDTH), F32),
                   jax.ShapeDtypeStruct((1, RG_WIDTH), F32)],
        scratch_shapes=[pltpu.VMEM((tm + SUBLANES, RG_WIDTH), F32), pltpu.VMEM((tm, RG_WIDTH), F32),
                        pltpu.VMEM((tm, RG_WIDTH), F32), pltpu.VMEM((tm + SUBLANES, RG_WIDTH), F32),
                        pltpu.VMEM((tm, RG_WIDTH), F32), pltpu.VMEM((tm + SUBLANES, RG_WIDTH), F32),
                        pltpu.VMEM((SUBLANES, RG_WIDTH), F32), pltpu.VMEM((SUBLANES, RG_WIDTH), F32)],
        compiler_params=_cparams(("arbitrary",)),
    )(dycat, proj, proj, proj, hs, hs, cw, cb, wa, ba, wx, bx, nsp)


def _block_diag(blocks):
    g, a, b = blocks.shape
    eye = jnp.eye(g, dtype=blocks.dtype)
    return (eye[:, None, :, None] * blocks[:, :, None, :]).reshape(g * a, g * b)


def _block_diag_extract(m, g):
    a, b = m.shape[0] // g, m.shape[1] // g
    m4 = m.reshape(g, a, g, b)
    idx = jnp.arange(g)
    return m4[idx, :, idx, :]


def _s5_prepare(lam_re, lam_im, log_step, b_re, b_im, c_re, c_im):
    step = jnp.exp(log_step)[:, None]
    mag = jnp.exp(lam_re * step)
    lbr = mag * jnp.cos(lam_im * step)
    lbi = mag * jnp.sin(lam_im * step)
    nr, ni = lbr - 1.0, lbi
    den = lam_re * lam_re + lam_im * lam_im
    cr = (nr * lam_re + ni * lam_im) / den
    ci = (ni * lam_re - nr * lam_im) / den
    bbr = cr[..., None] * b_re - ci[..., None] * b_im
    bbi = cr[..., None] * b_im + ci[..., None] * b_re
    bre = _block_diag(jnp.swapaxes(bbr, 1, 2))
    bim = _block_diag(jnp.swapaxes(bbi, 1, 2))
    cre = _block_diag(jnp.swapaxes(c_re, 1, 2))
    cim = _block_diag(jnp.swapaxes(c_im, 1, 2))
    return lbr.reshape(-1), lbi.reshape(-1), bre, bim, cre, cim


def _s5_scan_coef(lbr, lbi, reverse):
    if reverse:
        lbi = -lbi
    pr, pi = [lbr], [lbi]
    for _ in range(7):
        pr, pi = pr + [pr[-1] * lbr - pi[-1] * lbi], pi + [pr[-1] * lbi + pi[-1] * lbr]
    row = jnp.arange(SUBLANES)[:, None]
    tabs = []
    for sh in (1, 2, 4):
        keep = (row < SUBLANES - sh) if reverse else (row >= sh)
        tabs.append(jnp.stack([jnp.where(keep, pr[sh - 1][None, :], 0.0), jnp.where(keep, pi[sh - 1][None, :], 0.0)]))
    powr = jnp.stack(pr)
    powi = jnp.stack(pi)
    if reverse:
        powr, powi = powr[::-1], powi[::-1]
    tabs.append(jnp.stack([powr, powi]))
    tabs.append(jnp.zeros_like(tabs[-1]))
    return jnp.stack(tabs).astype(F32)


def _xy_peers():
    x, y, c = lax.axis_index("x"), lax.axis_index("y"), lax.axis_index("c")
    return x, y, c, [(1 - x, y), (x, 1 - y), (1 - x, 1 - y)]


def _hbm():
    return pl.BlockSpec(memory_space=pl.ANY)


def _xy_allgather(buf, *, name):
    n, w = buf.shape

    def body(x_ref, out_ref, send_sems, recv_sems, local_sem):
        x, y, c, peers = _xy_peers()
        me = 2 * x + y
        own = pltpu.make_async_copy(x_ref, out_ref.at[me], local_sem)
        own.start()
        sends = []
        for k, (px, py) in enumerate(peers):
            cp = pltpu.make_async_remote_copy(src_ref=x_ref, dst_ref=out_ref.at[me], send_sem=send_sems.at[k],
                                              recv_sem=recv_sems.at[k], device_id=(px, py, c), device_id_type=MESH)
            cp.start()
            sends.append(cp)
        for k, (px, py) in enumerate(peers):
            pltpu.make_async_remote_copy(src_ref=x_ref, dst_ref=out_ref.at[2 * px + py], send_sem=send_sems.at[k],
                                         recv_sem=recv_sems.at[k], device_id=(px, py, c),
                                         device_id_type=MESH).wait_recv()
        for cp in sends:
            cp.wait_send()
        own.wait()

    return pl.pallas_call(
        body, name=name, in_specs=[_hbm()], out_specs=_hbm(),
        out_shape=jax.ShapeDtypeStruct((4, n, w), buf.dtype),
        scratch_shapes=[pltpu.SemaphoreType.DMA((3,)), pltpu.SemaphoreType.DMA((3,)), pltpu.SemaphoreType.DMA],
    )(buf)


def _remote(src, dst, send_sem, recv_sem, dev):
    return pltpu.make_async_remote_copy(src_ref=src, dst_ref=dst, send_sem=send_sem, recv_sem=recv_sem,
                                        device_id=dev, device_id_type=MESH)


LAYER_GATHERED = (
    ("ssd_conv_w", (4, 256), 1), ("rg_conv_w", (4, LANES), 1),
    ("w_in", (1024, W_IN_PAD), 1), ("s5_glu_w", (64, 256), 0), ("w_out", (256, 1024), 0), ("xa_wq", (256, 1024), 0),
    ("xa_wk", (256, 1024), 0), ("xa_wv", (256, 1024), 0), ("xa_wo", (256, 1024), 0), ("mlp_w1", (1024, 1024), 1),
    ("mlp_w2", (1024, 1024), 0),
)
N_GATHERED = len(LAYER_GATHERED)
WAIT_GROUPS = ((0, 1, 2, 3), (4,), (5, 6, 7, 8), (9, 10))
RG_CONV_SHARD = RG_WIDTH // 4
N_GATHER_COPIES = 3 * N_GATHERED * DEPTH


def _gather_part(ref, t, pos):
    _, shp, ax = LAYER_GATHERED[t % N_GATHERED]
    idx = tuple(pl.ds(pos * shp[ax], shp[ax]) if d == ax else slice(None) for d in range(len(shp)))
    return ref.at[idx]


def _gather_start(shards):
    n = len(shards)
    lands = []
    for t, s in enumerate(shards):
        _, shp, ax = LAYER_GATHERED[t % N_GATHERED]
        full = shp[:ax] + (4 * shp[ax],) + shp[ax + 1:]
        lands.append(pltpu.with_memory_space_constraint(lax.empty(full, s.dtype), pltpu.HBM))

    def body(*refs):
        srcs, lnds = refs[:n], refs[n:2 * n]
        send_sems, recv_sems, local_sems = refs[2 * n:2 * n + 3]
        token = refs[-1]
        x, y, c, peers = _xy_peers()
        me = 2 * x + y
        for t in range(n):
            for k, (px, py) in enumerate(peers):
                _remote(srcs[t], _gather_part(lnds[t], t, me), send_sems.at[k * n + t], recv_sems.at[k * n + t],
                        (px, py, c)).start()
            pltpu.make_async_copy(srcs[t], _gather_part(lnds[t], t, me), local_sems.at[t]).start()
        token[...] = jnp.zeros_like(token)

    hbm = pl.BlockSpec(memory_space=pltpu.HBM)
    sem = pl.BlockSpec(memory_space=pltpu.SEMAPHORE)
    outs = pl.pallas_call(
        body, name="weights_gather_start", in_specs=[hbm] * (2 * n),
        out_shape=(pltpu.SemaphoreType.DMA((3 * n,)), pltpu.SemaphoreType.DMA((3 * n,)),
                   pltpu.SemaphoreType.DMA((n,)),
                   *[pltpu.HBM(s.shape, s.dtype) for s in shards], *[pltpu.HBM(a.shape, a.dtype) for a in lands],
                   jax.ShapeDtypeStruct((SUBLANES, LANES), F32)),
        out_specs=(sem, sem, sem, *[hbm] * (2 * n), pl.BlockSpec(memory_space=pltpu.VMEM)),
        input_output_aliases={i: 3 + i for i in range(2 * n)},
        compiler_params=pltpu.CompilerParams(has_side_effects=pltpu.SideEffectType.DATAFLOW_SIDE_EFFECTING),
    )(*[pltpu.with_memory_space_constraint(s, pltpu.HBM) for s in shards], *lands)
    return outs[0], outs[1], outs[2], outs[3:3 + n], outs[3 + n:3 + 2 * n], outs[-1]


def _gather_wait(handle, ts, after, *, name):
    send_sems, recv_sems, local_sems, src_thru, land_thru, _ = handle
    n = len(src_thru)
    m = len(ts)

    def body(*refs):
        srcs, lnds = refs[:m], refs[m:2 * m]
        ssem, rsem, lsem = refs[2 * m:2 * m + 3]
        x, y, c, peers = _xy_peers()
        me = 2 * x + y
        for i, t in enumerate(ts):
            for k, (px, py) in enumerate(peers):
                cp = _remote(srcs[i], _gather_part(lnds[i], t, 2 * px + py), ssem.at[k * n + t], rsem.at[k * n + t],
                             (px, py, c))
                cp.wait_send()
                cp.wait_recv()
            pltpu.make_async_copy(srcs[i], _gather_part(lnds[i], t, me), lsem.at[t]).wait()

    hbm = pl.BlockSpec(memory_space=pltpu.HBM)
    sem = pl.BlockSpec(memory_space=pltpu.SEMAPHORE)
    args = [src_thru[t] for t in ts] + [land_thru[t] for t in ts]
    outs = pl.pallas_call(
        body, name=name, in_specs=[hbm] * (2 * m) + [sem, sem, sem, pl.BlockSpec(memory_space=pl.ANY)],
        out_shape=[pltpu.HBM(a.shape, a.dtype) for a in args], out_specs=[hbm] * (2 * m),
        input_output_aliases={i: i for i in range(2 * m)},
        compiler_params=pltpu.CompilerParams(has_side_effects=pltpu.SideEffectType.DATAFLOW_SIDE_EFFECTING),
    )(*args, send_sems, recv_sems, local_sems, after)
    return outs[:m], outs[m:]


C_CHUNKS = 8
XY_CHUNKS = 8
EW_ROWS = 512


def _c_exchange(g, part):
    w = g.shape[2]
    row0, nrows = G_PARTS[part]
    half = nrows // 2
    rq = half // C_CHUNKS

    def body(g_ref, got_ref, send_sems, recv_sems):
        x, y, c = lax.axis_index("x"), lax.axis_index("y"), lax.axis_index("c")
        cps = []
        for s in range(4):
            for q in range(C_CHUNKS):
                k = s * C_CHUNKS + q
                cp = _remote(g_ref.at[s, pl.ds(row0 + (1 - c) * half + q * rq, rq), :],
                             got_ref.at[s, pl.ds(q * rq, rq), :], send_sems.at[k], recv_sems.at[k], (x, y, 1 - c))
                cp.start()
                cps.append(cp)
        for cp in cps:
            cp.wait_recv()
        for cp in cps:
            cp.wait_send()

    return pl.pallas_call(
        body, name="grad_c_exchange_%d" % part, in_specs=[_hbm()], out_specs=_hbm(),
        out_shape=jax.ShapeDtypeStruct((4, half, w), g.dtype),
        scratch_shapes=[pltpu.SemaphoreType.DMA((4 * C_CHUNKS,)), pltpu.SemaphoreType.DMA((4 * C_CHUNKS,))],
    )(g)


XFER_DTYPE = jnp.bfloat16


def _add_own_half(g, got, c_arr, part):
    w = g.shape[2]
    row0, nrows = G_PARTS[part]
    half = nrows // 2
    nb = half // EW_ROWS
    b0 = row0 // EW_ROWS

    def body(c_ref, a_ref, b_ref, o_ref, t_ref):
        sm = a_ref[...] + b_ref[...]
        o_ref[...] = sm.astype(o_ref.dtype)

        @pl.when(pl.program_id(1) == nb - 1)
        def _():
            t_ref[...] = sm[:, EW_ROWS - MISC_ROWS:, :]

    grid_spec = pltpu.PrefetchScalarGridSpec(
        num_scalar_prefetch=1, grid=(4, nb),
        in_specs=[pl.BlockSpec((1, EW_ROWS, w), lambda s, i, c: (s, b0 + c[0] * nb + i, 0)),
                  pl.BlockSpec((1, EW_ROWS, w), lambda s, i, c: (s, i, 0))],
        out_specs=[pl.BlockSpec((1, EW_ROWS, w), lambda s, i, c: (s, i, 0)),
                   pl.BlockSpec((1, MISC_ROWS, w), lambda s, i, c: (s, 0, 0))])
    return pl.pallas_call(
        body, name="grad_add_halves", grid_spec=grid_spec,
        out_shape=[jax.ShapeDtypeStruct((4, half, w), XFER_DTYPE), jax.ShapeDtypeStruct((4, MISC_ROWS, w), g.dtype)],
        compiler_params=_cparams(("arbitrary", "arbitrary")),
    )(c_arr, g, got)


def _xy_pieces(arrs):
    pieces = []
    for a, arr in enumerate(arrs):
        nch = XY_CHUNKS if a == 0 else 1
        rq = arr.shape[1] // nch
        pieces += [(a, pl.ds(q * rq, rq)) for q in range(nch)]
    return pieces


def _xy_start(arrs, *, name):
    na = len(arrs)
    pieces = _xy_pieces(arrs)
    npc = len(pieces)
    lands = [pltpu.with_memory_space_constraint(lax.empty(a.shape, a.dtype), pltpu.HBM) for a in arrs]

    def body(*refs):
        ins, outs = refs[:na], refs[na:2 * na]
        send_sems, recv_sems, local_sems = refs[2 * na:2 * na + 3]
        token = refs[-1]
        x, y, c, peers = _xy_peers()
        me = 2 * x + y
        for k, (px, py) in enumerate(peers):
            for j, (a, rows) in enumerate(pieces):
                _remote(ins[a].at[2 * px + py, rows, :], outs[a].at[me, rows, :], send_sems.at[k * npc + j],
                        recv_sems.at[k * npc + j], (px, py, c)).start()
        for j, (a, rows) in enumerate(pieces):
            pltpu.make_async_copy(ins[a].at[me, rows, :], outs[a].at[me, rows, :], local_sems.at[j]).start()
        token[...] = jnp.zeros_like(token)

    hbm = pl.BlockSpec(memory_space=pltpu.HBM)
    sem = pl.BlockSpec(memory_space=pltpu.SEMAPHORE)
    outs = pl.pallas_call(
        body, name=name, in_specs=[hbm] * (2 * na),
        out_shape=(pltpu.SemaphoreType.DMA((3 * npc,)), pltpu.SemaphoreType.DMA((3 * npc,)),
                   pltpu.SemaphoreType.DMA((npc,)),
                   *[pltpu.HBM(a.shape, a.dtype) for a in arrs], *[pltpu.HBM(a.shape, a.dtype) for a in arrs],
                   jax.ShapeDtypeStruct((SUBLANES, LANES), F32)),
        out_specs=(sem, sem, sem, *[hbm] * (2 * na), pl.BlockSpec(memory_space=pltpu.VMEM)),
        input_output_aliases={i: 3 + i for i in range(2 * na)},
        compiler_params=pltpu.CompilerParams(has_side_effects=pltpu.SideEffectType.DATAFLOW_SIDE_EFFECTING),
    )(*[pltpu.with_memory_space_constraint(a, pltpu.HBM) for a in arrs], *lands)
    return (outs[0], outs[1], outs[2], outs[3:3 + na], outs[3 + na:3 + 2 * na]), outs[-1]


def _xy_wait(handle, after, *, name):
    send_sems, recv_sems, local_sems, src_thru, land_thru = handle
    na = len(src_thru)
    pieces = _xy_pieces(src_thru)
    npc = len(pieces)

    def body(*refs):
        ins, outs = refs[:na], refs[na:2 * na]
        ssem, rsem, lsem = refs[2 * na:2 * na + 3]
        x, y, c, peers = _xy_peers()
        me = 2 * x + y
        for k, (px, py) in enumerate(peers):
            for j, (a, rows) in enumerate(pieces):
                cp = _remote(ins[a].at[me, rows, :], outs[a].at[2 * px + py, rows, :], ssem.at[k * npc + j],
                             rsem.at[k * npc + j], (px, py, c))
                cp.wait_send()
                cp.wait_recv()
        for j, (a, rows) in enumerate(pieces):
            pltpu.make_async_copy(ins[a].at[me, rows, :], outs[a].at[me, rows, :], lsem.at[j]).wait()

    hbm = pl.BlockSpec(memory_space=pltpu.HBM)
    sem = pl.BlockSpec(memory_space=pltpu.SEMAPHORE)
    args = list(src_thru) + list(land_thru)
    outs = pl.pallas_call(
        body, name=name, in_specs=[hbm] * (2 * na) + [sem, sem, sem, pl.BlockSpec(memory_space=pl.ANY)],
        out_shape=[pltpu.HBM(a.shape, a.dtype) for a in args], out_specs=[hbm] * (2 * na),
        input_output_aliases={i: i for i in range(2 * na)},
        compiler_params=pltpu.CompilerParams(has_side_effects=pltpu.SideEffectType.DATAFLOW_SIDE_EFFECTING),
    )(*args, send_sems, recv_sems, local_sems, after)
    return outs[na:]


def _sum4_into_half(r, rt, c_arr, part, fbuf):
    _, half, w = r.shape
    nb = half // EW_ROWS
    b0 = G_PARTS[part][0] // EW_ROWS

    def body(c_ref, r_ref, t_ref, *rest):
        o_ref = rest[-1]
        o_ref[...] = ((r_ref[0].astype(F32) + r_ref[1].astype(F32)) + r_ref[2].astype(F32)) + r_ref[3].astype(F32)

        @pl.when(pl.program_id(0) == nb - 1)
        def _():
            o_ref[EW_ROWS - MISC_ROWS:, :] = ((t_ref[0] + t_ref[1]) + t_ref[2]) + t_ref[3]

    in_specs = [pl.BlockSpec((4, EW_ROWS, w), lambda i, c: (0, i, 0)),
                pl.BlockSpec((4, MISC_ROWS, w), lambda i, c: (0, 0, 0))]
    args = [c_arr, r, rt]
    aliases = {}
    if fbuf is not None:
        in_specs.append(pl.BlockSpec(memory_space=pl.ANY))
        args.append(fbuf)
        aliases = {3: 0}
    grid_spec = pltpu.PrefetchScalarGridSpec(
        num_scalar_prefetch=1, grid=(nb,), in_specs=in_specs,
        out_specs=pl.BlockSpec((EW_ROWS, w), lambda i, c: (b0 + c[0] * nb + i, 0)))
    return pl.pallas_call(
        body, name="grad_sum4", grid_spec=grid_spec, out_shape=jax.ShapeDtypeStruct((G_ROWS, w), F32),
        input_output_aliases=aliases, compiler_params=_cparams(("arbitrary",)),
    )(*args)


C_GATHER_ROWS = 512


def _c_allgather_halves(f, parts):
    w = f.shape[1]
    chunks = []
    for part in parts:
        chunks += [(part, r) for r in range(0, G_PARTS[part][1] // 2, C_GATHER_ROWS)]
    nch = len(chunks)

    def body(f_ref, out_ref, send_sems, recv_sems):
        x, y, c = lax.axis_index("x"), lax.axis_index("y"), lax.axis_index("c")

        def rows(q, owner):
            part, r = chunks[q]
            row0, nrows = G_PARTS[part]
            return pl.ds(row0 + owner * (nrows // 2) + r, C_GATHER_ROWS)

        sends = []
        for q in range(nch):
            cp = _remote(f_ref.at[rows(q, c), :], out_ref.at[rows(q, c), :], send_sems.at[q], recv_sems.at[q],
                         (x, y, 1 - c))
            cp.start()
            sends.append(cp)
        for q in range(nch):
            _remote(f_ref.at[rows(q, 1 - c), :], out_ref.at[rows(q, 1 - c), :], send_sems.at[q], recv_sems.at[q],
                    (x, y, 1 - c)).wait_recv()
        for cp in sends:
            cp.wait_send()

    return pl.pallas_call(
        body, name="grad_c_allgather_" + "".join(str(p) for p in parts), in_specs=[_hbm()], out_specs=_hbm(),
        input_output_aliases={0: 0},
        out_shape=jax.ShapeDtypeStruct((G_ROWS, w), f.dtype),
        scratch_shapes=[pltpu.SemaphoreType.DMA((nch,)), pltpu.SemaphoreType.DMA((nch,))],
    )(f)


def _adamw(w, m, v, g, g_rows=None):
    shape = w.shape
    cols = shape[-1]
    rows = int(math.prod(shape)) // cols
    tr = 256 if rows % 256 == 0 else rows
    from_flat = g_rows is not None
    c1 = 1.0 / (1.0 - ADAM_B1 ** ADAM_STEP)
    c2 = 1.0 / (1.0 - ADAM_B2 ** ADAM_STEP)

    def body(w_ref, m_ref, v_ref, g_ref, *outs):
        gg = g_ref[...]
        nm = ADAM_B1 * m_ref[...] + (1.0 - ADAM_B1) * gg
        nv = ADAM_B2 * v_ref[...] + (1.0 - ADAM_B2) * (gg * gg)
        if from_flat:
            outs[0][...] = gg
        d_ref, nm_ref, nv_ref = outs[-3:]
        nm_ref[...] = nm
        nv_ref[...] = nv
        d_ref[...] = -ADAM_LR * ((nm * c1) / (jnp.sqrt(nv * c2) + ADAM_EPS) + ADAM_WD * w_ref[...])

    spec = pl.BlockSpec((tr, cols), lambda i: (i, 0))
    if from_flat:
        nbl = rows // DEPTH // tr
        assert cols == FLAT and all(r % tr == 0 for r in g_rows) and len(g_rows) == DEPTH == 2
        b0, b1 = g_rows[0] // tr, g_rows[1] // tr
        g_spec = pl.BlockSpec((tr, cols), lambda i: (jnp.where(i < nbl, b0 + i, b1 + i - nbl), 0))
        g_arg = g
    else:
        g_spec = spec
        g_arg = g.reshape(rows, cols)
    n_out = 4 if from_flat else 3
    sds = jax.ShapeDtypeStruct((rows, cols), F32)
    outs = pl.pallas_call(
        body, name="adamw", grid=(rows // tr,), in_specs=[spec, spec, spec, g_spec], out_specs=[spec] * n_out,
        out_shape=[sds] * n_out, compiler_params=_cparams(("arbitrary",)),
    )(w.reshape(rows, cols), m.reshape(rows, cols), v.reshape(rows, cols), g_arg)
    outs = [o.reshape(shape) for o in outs]
    return outs if from_flat else [g] + outs


SMALL_SHARDED = (("s5_glu_w", (2, 64, 256), 1), ("ssd_conv_w", (2, 4, 256), 2), ("rg_conv_w", (2, 4, 64), 2))
REPLICATED = (
    ("ssd_conv_b", (2, 1024)), ("ssd_dt_bias", (2, 8)), ("ssd_a_log", (2, 8)), ("ssd_d", (2, 8)),
    ("ssd_norm_w", (2, 512)), ("s5_lam_re", (2, 16, 64)), ("s5_lam_im", (2, 16, 64)), ("s5_log_step", (2, 16)),
    ("s5_b_re", (2, 16, 64, 16)), ("s5_b_im", (2, 16, 64, 16)), ("s5_c_re", (2, 16, 16, 64)),
    ("s5_c_im", (2, 16, 16, 64)), ("s5_d", (2, 256)), ("s5_glu_b", (2, 256)), ("rg_conv_b", (2, 256)),
    ("rg_wa", (2, 4, 64, 64)), ("rg_ba", (2, 4, 64)), ("rg_wx", (2, 4, 64, 64)), ("rg_bx", (2, 4, 64)),
    ("rg_lambda", (2, 256)), ("ln1_g", (2, 1024)), ("ln1_b", (2, 1024)), ("ln2_g", (2, 1024)), ("ln2_b", (2, 1024)),
    ("ln3_g", (2, 1024)), ("ln3_b", (2, 1024)),
)
WEIGHT_ORDER = (
    "w_in", "w_out", "ssd_conv_w", "ssd_conv_b", "ssd_dt_bias", "ssd_a_log", "ssd_d", "ssd_norm_w", "s5_lam_re",
    "s5_lam_im", "s5_log_step", "s5_b_re", "s5_b_im", "s5_c_re", "s5_c_im", "s5_d", "s5_glu_w", "s5_glu_b",
    "rg_conv_w", "rg_conv_b", "rg_wa", "rg_ba", "rg_wx", "rg_bx", "rg_lambda", "ln1_g", "ln1_b", "xa_wq", "xa_wk",
    "xa_wv", "xa_wo", "ln2_g", "ln2_b", "mlp_w1", "mlp_w2", "ln3_g", "ln3_b",
)


def _size(shape):
    return int(math.prod(shape))


def _round_up(a, b):
    return (a + b - 1) // b * b


SMALL_ELEMS = sum(_size(s) for _, s, _ in SMALL_SHARDED)
REP_ELEMS = sum(_size(s) for _, s in REPLICATED)
REP_QROWS = _round_up(-(-REP_ELEMS // (4 * FLAT)), 8)
assert SMALL_ELEMS <= MISC_REP_ROW * FLAT and MISC_REP_ROW + REP_QROWS <= MISC_ROWS


def _pack_shards(tensors, names_shapes):
    return jnp.concatenate([tensors[n].reshape(-1) for n, *_ in names_shapes])


def _unpack(flat, names_shapes):
    out, off = {}, 0
    for n, s, *_ in names_shapes:
        out[n] = flat[off:off + _size(s)].reshape(s)
        off += _size(s)
    return out


def _split_shards(full, names_shapes):
    rows = []
    for k in range(4):
        parts = []
        for n, s, ax in names_shapes:
            w = s[ax]
            parts.append(lax.slice_in_dim(full[n], k * w, (k + 1) * w, axis=ax).reshape(-1))
        rows.append(jnp.concatenate(parts))
    return jnp.stack(rows)


def _pack_cols(w):
    pad = jnp.zeros((w.shape[0], LANES - SSD_HEADS), w.dtype)
    return jnp.concatenate([w[:, O_XBC:O_XBC + 1024], w[:, O_Z:O_Z + 512], w[:, O_U:O_U + 256],
                            w[:, O_XRG:O_XRG + 256], w[:, O_GRG:O_GRG + 256], w[:, O_DT:O_DT + 8], pad], axis=1)


def _unpack_cols(w):
    return jnp.concatenate([w[:, P_Z:P_Z + 512], w[:, P_XBC:P_XBC + 1024], w[:, P_DT:P_DT + 8],
                            w[:, P_U:P_U + 256], w[:, P_XRG:P_XRG + 256], w[:, P_GRG:P_GRG + 256]], axis=1)


def _lanes(v, width):
    return jnp.pad(v, (0, width - v.shape[0])).reshape(1, width)


def _layer_params(rep, l):
    p = {}
    p["ssd_cb"] = rep["ssd_conv_b"][l].reshape(1, -1)
    p["ssd_dtb"] = _lanes(rep["ssd_dt_bias"][l], LANES)
    p["ssd_a"] = _lanes(-jnp.exp(rep["ssd_a_log"][l]), LANES)
    p["ssd_d"] = jnp.repeat(rep["ssd_d"][l], 64).reshape(1, -1)
    p["ssd_nw"] = rep["ssd_norm_w"][l].reshape(1, -1)
    s5_args = tuple(rep[n][l] for n in ("s5_lam_re", "s5_lam_im", "s5_log_step", "s5_b_re", "s5_b_im", "s5_c_re",
                                        "s5_c_im"))
    (lbr, lbi, bre, bim, cre, cim), p["s5_vjp"] = jax.vjp(_s5_prepare, *s5_args)
    p.update(s5_bre=bre, s5_bim=bim, s5_cre=cre, s5_cim=cim)
    p["s5_coef"] = _s5_scan_coef(lbr, lbi, False)
    p["s5_rcoef"] = _s5_scan_coef(lbr, lbi, True)
    p["s5_d"] = rep["s5_d"][l].reshape(1, -1)
    p["s5_gb"] = rep["s5_glu_b"][l].reshape(1, -1)
    p["rg_cb"] = rep["rg_conv_b"][l].reshape(1, -1)
    p["rg_wa"] = _block_diag(rep["rg_wa"][l])
    p["rg_wx"] = _block_diag(rep["rg_wx"][l])
    p["rg_ba"] = rep["rg_ba"][l].reshape(1, -1)
    p["rg_bx"] = rep["rg_bx"][l].reshape(1, -1)
    p["rg_nsp"] = (-RG_C * jax.nn.softplus(-rep["rg_lambda"][l])).reshape(1, -1)
    p["rg_dnsp"] = RG_C * jax.nn.sigmoid(-rep["rg_lambda"][l])
    for n in ("ln1_g", "ln1_b", "ln2_g", "ln2_b", "ln3_g", "ln3_b"):
        p[n] = rep[n][l].reshape(1, -1)
    return p


def _layer_fwd(h, mem, p, fetch):
    s = {"h0": h}
    p.update(fetch(0, h))
    proj = _mm(h, p["w_in"], name="in_proj")
    s["proj"] = proj
    y_ssd, s["ssd_yy"], s["ssd_states"] = _ssd_fwd(proj, p["ssd_cw"], p["ssd_cb"], p["ssd_dtb"], p["ssd_a"],
                                                     p["ssd_d"], p["ssd_nw"])
    y_s5, s["s5_y2"], s["s5_hre"], s["s5_him"] = _s5_fwd(proj, p["s5_bre"], p["s5_bim"], p["s5_cre"], p["s5_cim"],
                                                         p["s5_d"], p["s5_glu_w"], p["s5_gb"], p["s5_coef"])
    y_rg, s["rg_h"] = _rg_fwd(proj, p["rg_cw"], p["rg_cb"], p["rg_wa"], p["rg_ba"], p["rg_wx"], p["rg_bx"],
                              p["rg_nsp"])
    s["ys"] = [y_ssd, y_s5, y_rg]
    p.update(fetch(1, y_rg))
    h1, s["xh1"], s["rs1"] = _outproj_ln_fwd(s["ys"], h, p["w_out"], p["ln1_g"], p["ln1_b"])
    s["h1"] = h1
    p.update(fetch(2, h1))
    kb = _mm(mem, p["xa_wk"], name="mem_proj")
    vb = _mm(mem, p["xa_wv"], name="mem_proj")
    s["kb"], s["vb"] = kb, vb
    h2, s["xh2"], s["rs2"], s["attn_o"] = _attn_ln_fwd(h1, p["xa_wq"], p["xa_wo"], kb, vb, p["ln2_g"], p["ln2_b"])
    s["h2"] = h2
    p.update(fetch(3, h2))
    h3, s["xh3"], s["rs3"], s["mlp_hdn"] = _mlp_ln_fwd(h2, p["mlp_w1"], p["mlp_w2"], p["ln3_g"], p["ln3_b"])
    return h3, s


def _layer_bwd(dh3, mem, p, s, l, gbuf, after_mlp=None):
    g = {}
    dr3, du, dh2, g["ln3_g"], g["ln3_b"] = _mlp_ln_bwd(dh3, s["xh3"], s["rs3"], p["ln3_g"], s["mlp_hdn"],
                                                        p["mlp_w1"], p["mlp_w2"])
    gbuf = _wgrad_flat(s["h2"], du, gbuf, mode="colblk", row_off=_grad_row("mlp_w1", l), name="wgrad_mlp_w1")
    gbuf = _wgrad_flat(s["mlp_hdn"], dr3, gbuf, mode="rowblk", row_off=_grad_row("mlp_w2", l), name="wgrad_mlp_w2")
    ln2_g = p["ln2_g"] if after_mlp is None else p["ln2_g"] + after_mlp(gbuf)[0:1, 0:1]
    dr2, dq, dh1, dkb, dvb, g["ln2_g"], g["ln2_b"] = _attn_ln_bwd(dh2, s["xh2"], s["rs2"], ln2_g, s["h1"],
                                                                   p["xa_wq"], p["xa_wo"], s["kb"], s["vb"])
    for n, a_op, g_op in (("xa_wo", s["attn_o"], dr2), ("xa_wq", s["h1"], dq), ("xa_wk", mem, dkb),
                          ("xa_wv", mem, dvb)):
        gbuf = _wgrad_flat(a_op, g_op, gbuf, mode="rows4", row_off=_grad_row(n, l), name="wgrad_" + n)
    dr1, dres, dycat, g["ln1_g"], g["ln1_b"] = _outproj_ln_bwd(dh1, s["xh1"], s["rs1"], p["ln1_g"], p["w_out"])
    gbuf = _wgrad_flat(s["ys"], dr1, gbuf, mode="rows4", row_off=_grad_row("w_out", l), name="wgrad_w_out")
    proj = s["proj"]
    (dxbc, dz, ddt, dcw, dcb, ddtb, da_neg, dd_l, dnw) = _ssd_bwd(
        dycat, proj, s["ssd_yy"], s["ssd_states"], p["ssd_cw"], p["ssd_cb"], p["ssd_dtb"], p["ssd_a"], p["ssd_d"],
        p["ssd_nw"])
    g["ssd_conv_w"] = dcw[0:4]
    g["ssd_conv_b"] = dcb[0]
    g["ssd_dt_bias"] = ddtb[0, :SSD_HEADS]
    g["ssd_a_log"] = da_neg[0, :SSD_HEADS] * p["ssd_a"][0, :SSD_HEADS]
    g["ssd_d"] = dd_l.reshape(SSD_HEADS, 64).sum(axis=1)
    g["ssd_norm_w"] = dnw[0]
    (du_s5, dbre, dbim, dcre, dcim, dlam, dd5, dgw, dgb) = _s5_bwd(
        dycat, proj, s["s5_y2"], s["s5_hre"], s["s5_him"], p["s5_bre"], p["s5_bim"], p["s5_cre"], p["s5_cim"],
        p["s5_d"], p["s5_glu_w"], p["s5_gb"], p["s5_rcoef"])
    dl = dlam.sum(axis=1)
    s5g = p["s5_vjp"]((dl[0], dl[1], dbre, dbim, dcre, dcim))
    for n, v in zip(("s5_lam_re", "s5_lam_im", "s5_log_step", "s5_b_re", "s5_b_im", "s5_c_re", "s5_c_im"), s5g):
        g[n] = v
    g["s5_d"] = dd5[0]
    g["s5_glu_w"] = dgw
    g["s5_glu_b"] = dgb[0]
    (dxrg, dgrg, drcw, drcb, dwa, dba, dwx, dbx, dnsp) = _rg_bwd(
        dycat, proj, s["rg_h"], p["rg_cw"], p["rg_cb"], p["rg_wa"], p["rg_ba"], p["rg_wx"], p["rg_bx"], p["rg_nsp"])
    g["rg_conv_w"] = drcw[0:4]
    g["rg_conv_b"] = drcb[0]
    g["rg_wa"] = _block_diag_extract(dwa, RG_BLOCKS)
    g["rg_wx"] = _block_diag_extract(dwx, RG_BLOCKS)
    g["rg_ba"] = dba.reshape(RG_BLOCKS, RG_BLOCK_DIM)
    g["rg_bx"] = dbx.reshape(RG_BLOCKS, RG_BLOCK_DIM)
    g["rg_lambda"] = dnsp[0] * p["rg_dnsp"]
    dproj = [dxbc, dz, du_s5, dxrg, dgrg, ddt]
    g["w_in"] = _unpack_cols(_wgrad_in(s["h0"], dproj))
    dh0 = _in_proj_bwd(dproj, p["w_in"], dres)
    for n in ("ln1_g", "ln1_b", "ln2_g", "ln2_b", "ln3_g", "ln3_b"):
        g[n] = g[n][0]
    return dh0, g, gbuf


def _local_step(h, memf, target, rep, fetch):
    params, saved = [], []
    for l in range(DEPTH):
        p = _layer_params(rep, l)
        params.append(p)
        h, s = _layer_fwd(h, memf, p, functools.partial(fetch, l))
        saved.append(s)
    loss11, dh = _loss_fwd_bwd(h, target)
    grads = [None] * DEPTH
    gbuf = None
    c_arr = lax.axis_index("c").astype(jnp.int32).reshape(1)
    handles = {}

    def start_part(buf, part):
        handles[part], token = _xy_start(_chip_sums(buf, c_arr, part), name="grad_xy_start_%d" % part)
        return token

    for l in reversed(range(DEPTH)):
        hook = functools.partial(start_part, part=1) if l == 0 else None
        dh, grads[l], gbuf = _layer_bwd(dh, memf, params[l], saved[l], l, gbuf, hook)
        if l == DEPTH - 1:
            gbuf = lax.dynamic_update_slice(
                gbuf, _w_in_block(grads[l]["w_in"], jnp.zeros((4, MISC_ROWS, FLAT), F32)),
                (0, _grad_row("w_in", l), 0))
            params[0]["ln3_g"] = params[0]["ln3_g"] + start_part(gbuf, 0)[0:1, 0:1]
    gsmall = {n: jnp.stack([grads[l][n] for l in range(DEPTH)]) for n in grads[0] if n != "w_in"}
    return loss11, dh, gsmall, grads[0]["w_in"], gbuf, handles, c_arr


def _w_in_block(gw, tail):
    gw = jnp.pad(gw.reshape(D_MODEL, 4, W_IN_SHARD), ((0, 0), (0, 0), (0, W_IN_PAD - W_IN_SHARD)))
    return jnp.concatenate([jnp.transpose(gw, (1, 0, 2)).reshape(4, W_IN_PAD, FLAT), tail], axis=1)


def _chip_sums(gbuf, c_arr, part):
    return list(_add_own_half(gbuf, _c_exchange(gbuf, part), c_arr, part))


def kernel(x, mem, w_in, w_out, ssd_conv_w, ssd_conv_b, ssd_dt_bias, ssd_a_log, ssd_d, ssd_norm_w, s5_lam_re, s5_lam_im, s5_log_step, s5_b_re, s5_b_im, s5_c_re, s5_c_im, s5_d, s5_glu_w, s5_glu_b, rg_conv_w, rg_conv_b, rg_wa, rg_ba, rg_wx, rg_bx, rg_lambda, ln1_g, ln1_b, xa_wq, xa_wk, xa_wv, xa_wo, ln2_g, ln2_b, mlp_w1, mlp_w2, ln3_g, ln3_b, loss_target, m_w_in, m_w_out, m_ssd_conv_w, m_ssd_conv_b, m_ssd_dt_bias, m_ssd_a_log, m_ssd_d, m_ssd_norm_w, m_s5_lam_re, m_s5_lam_im, m_s5_log_step, m_s5_b_re, m_s5_b_im, m_s5_c_re, m_s5_c_im, m_s5_d, m_s5_glu_w, m_s5_glu_b, m_rg_conv_w, m_rg_conv_b, m_rg_wa, m_rg_ba, m_rg_wx, m_rg_bx, m_rg_lambda, m_ln1_g, m_ln1_b, m_xa_wq, m_xa_wk, m_xa_wv, m_xa_wo, m_ln2_g, m_ln2_b, m_mlp_w1, m_mlp_w2, m_ln3_g, m_ln3_b, v_w_in, v_w_out, v_ssd_conv_w, v_ssd_conv_b, v_ssd_dt_bias, v_ssd_a_log, v_ssd_d, v_ssd_norm_w, v_s5_lam_re, v_s5_lam_im, v_s5_log_step, v_s5_b_re, v_s5_b_im, v_s5_c_re, v_s5_c_im, v_s5_d, v_s5_glu_w, v_s5_glu_b, v_rg_conv_w, v_rg_conv_b, v_rg_wa, v_rg_ba, v_rg_wx, v_rg_bx, v_rg_lambda, v_ln1_g, v_ln1_b, v_xa_wq, v_xa_wk, v_xa_wv, v_xa_wo, v_ln2_g, v_ln2_b, v_mlp_w1, v_mlp_w2, v_ln3_g, v_ln3_b):
    args = dict(locals())
    weights = {n: args[n] for n in WEIGHT_ORDER}
    mom_m = {n: args["m_" + n] for n in WEIGHT_ORDER}
    mom_v = {n: args["v_" + n] for n in WEIGHT_ORDER}

    shards = []
    for l in range(DEPTH):
        for n, shp, ax in LAYER_GATHERED:
            w = weights[n][l]
            if w.shape[1] != shp[1]:
                w = jnp.pad(w, ((0, 0), (0, shp[1] - w.shape[1])))
            if n not in ("ssd_conv_w", "rg_conv_w"):
                w = w.astype(MXU_DTYPE)
            shards.append(w)
    handle = _gather_start(shards)

    def unpad(arr, padded, width):
        return jnp.concatenate([arr[:, padded * k:padded * k + width] for k in range(4)], axis=1)

    def fetch(l, grp, after):
        ts = [l * N_GATHERED + j for j in WAIT_GROUPS[grp]]
        _, landed = _gather_wait(handle, ts, after, name="weights_gather_wait_%d_%d" % (l, grp))
        out = {}
        for t, arr in zip(ts, landed):
            n = LAYER_GATHERED[t % N_GATHERED][0]
            if n == "w_in":
                arr = _pack_cols(unpad(arr, W_IN_PAD, W_IN_SHARD))
            elif n == "rg_conv_w":
                arr = unpad(arr, LANES, RG_CONV_SHARD)
            out[{"ssd_conv_w": "ssd_cw", "rg_conv_w": "rg_cw"}.get(n, n)] = arr
        return out

    rep = {n: weights[n] for n, _ in REPLICATED}

    loss11, dx, gsmall, gw_in0, gbuf, handles, c_arr = _local_step(x[0], mem[0], loss_target[0], rep, fetch)
    grad_x = dx[None]
    loss = lax.psum(loss11[0, 0], ("x", "y", "c"))

    small_q = _split_shards(gsmall, SMALL_SHARDED)
    rep_q = jnp.pad(_pack_shards(gsmall, REPLICATED), (0, 4 * REP_QROWS * FLAT - REP_ELEMS)).reshape(4, -1)
    misc = jnp.concatenate(
        [jnp.pad(small_q, ((0, 0), (0, MISC_REP_ROW * FLAT - SMALL_ELEMS))), rep_q,
         jnp.zeros((4, (MISC_ROWS - MISC_REP_ROW - REP_QROWS) * FLAT), F32)], axis=1).reshape(4, MISC_ROWS, FLAT)
    gbuf = lax.dynamic_update_slice(gbuf, _w_in_block(gw_in0, misc), (0, _grad_row("w_in", 0), 0))
    handles[2], token = _xy_start(_chip_sums(gbuf, c_arr, 2), name="grad_xy_start_2")
    fbuf = None
    for part in (0, 1):
        got = _xy_wait(handles[part], dx, name="grad_xy_wait_%d" % part)
        fbuf = _sum4_into_half(got[0], got[1] + token[0:1, 0:1], c_arr, part, fbuf)
    fbuf = _c_allgather_halves(fbuf, (0, 1))
    res = {n: _adamw(weights[n], mom_m[n], mom_v[n], fbuf, g_rows=[_grad_row(n, l) for l in range(DEPTH)])
           for n in ("mlp_w1", "mlp_w2")}
    got = _xy_wait(handles[2], res["mlp_w2"][1], name="grad_xy_wait_2")
    reduced = _c_allgather_halves(_sum4_into_half(got[0], got[1], c_arr, 2, fbuf), (2,))
    misc_red = reduced[ROW_MISC:]
    rep_all = _xy_allgather(misc_red[MISC_REP_ROW:MISC_REP_ROW + REP_QROWS], name="small_grads_allgather")
    g_red = {**_unpack(misc_red[:MISC_REP_ROW].reshape(-1), SMALL_SHARDED),
             **_unpack(rep_all.reshape(-1), REPLICATED)}
    g_red["w_in"] = jnp.stack([
        reduced[_grad_row("w_in", l):_grad_row("w_in", l) + W_IN_PAD].reshape(D_MODEL, W_IN_PAD)[:, :W_IN_SHARD]
        for l in range(DEPTH)])

    for n in WEIGHT_ORDER:
        if n in ("w_out", "xa_wq", "xa_wk", "xa_wv", "xa_wo"):
            res[n] = _adamw(weights[n], mom_m[n], mom_v[n], reduced, g_rows=[_grad_row(n, l) for l in range(DEPTH)])
        elif n not in res:
            res[n] = _adamw(weights[n], mom_m[n], mom_v[n], g_red[n])
    return (loss, grad_x, *[res[n][0] for n in WEIGHT_ORDER], *[res[n][1] for n in WEIGHT_ORDER],
            *[res[n][2] for n in WEIGHT_ORDER], *[res[n][3] for n in WEIGHT_ORDER])
```

```python
import functools
import math

import jax
import jax.numpy as jnp
from jax import lax
from jax.experimental import pallas as pl
from jax.experimental.pallas import tpu as pltpu

F32 = jnp.float32
MXU_DTYPE = jnp.bfloat16

D_MODEL = 1024
DEPTH = 2
MEM_LEN = 256
SSD_WIDTH = 512
SSD_HEADS = 8
SSD_STATE = 128
SSD_CHUNK = 128
SSD_XBC = 1024
S5_WIDTH = 256
S5_GROUPS = 16
S5_GROUP_CH = 16
S5_STATE = 64
S5_NSTATE = S5_GROUPS * S5_STATE
RG_WIDTH = 256
RG_BLOCKS = 4
RG_BLOCK_DIM = 64
RG_C = 8.0
XA_HEADS = 4
XA_HEAD_DIM = 256
D_FF = 4096
D_IN = 2312
ALPHA = (2.0 * DEPTH) ** 0.25
LN_EPS = 1e-5
ADAM_LR = 0.001
ADAM_B1 = 0.9
ADAM_B2 = 0.999
ADAM_EPS = 1e-08
ADAM_WD = 0.01
ADAM_STEP = 10

P_XBC, P_Z, P_U, P_XRG, P_GRG, P_DT = 0, 1024, 1536, 1792, 2048, 2304
D_PACK = 2432
O_Z, O_XBC, O_DT, O_U, O_XRG, O_GRG = 0, 512, 1536, 1544, 1800, 2056

LANES = 128
SUBLANES = 8
VMEM_LIMIT = 52 * 1024 * 1024
TM = 512
SSD_TM = 128
SCAN_TM = 512
FLAT = 1024

MESH = pl.DeviceIdType.MESH


def _cparams(sem):
    return pltpu.CompilerParams(dimension_semantics=sem, vmem_limit_bytes=VMEM_LIMIT)


def _dot(a, b):
    return jnp.dot(a.astype(MXU_DTYPE), b.astype(MXU_DTYPE), preferred_element_type=F32)


def _dot_nt(a, b):
    return lax.dot_general(a.astype(MXU_DTYPE), b.astype(MXU_DTYPE), (((1,), (1,)), ((), ())),
                           preferred_element_type=F32)


def _dot_tn(a, b):
    return lax.dot_general(a.astype(MXU_DTYPE), b.astype(MXU_DTYPE), (((0,), (0,)), ((), ())),
                           preferred_element_type=F32)


def _dot_f32(a, b):
    return jnp.dot(a, b, precision=lax.Precision.HIGHEST, preferred_element_type=F32)


def _dot_f32_tn(a, b):
    return lax.dot_general(a, b, (((0,), (0,)), ((), ())), precision=lax.Precision.HIGHEST,
                           preferred_element_type=F32)


def _sigmoid(x):
    return 1.0 / (1.0 + jnp.exp(-x))


def _softplus(x):
    return jnp.maximum(x, 0.0) + jnp.log(1.0 + jnp.exp(-jnp.abs(x)))


_GELU_K = math.sqrt(2.0 / math.pi)


def _gelu(x):
    return 0.5 * x * (1.0 + jnp.tanh(_GELU_K * (x + 0.044715 * x * x * x)))


def _gelu_grad(x):
    t = jnp.tanh(_GELU_K * (x + 0.044715 * x * x * x))
    return 0.5 * (1.0 + t) + 0.5 * x * (1.0 - t * t) * _GELU_K * (1.0 + 3.0 * 0.044715 * x * x)


def _expm1(x):
    small = x * (1.0 + x * (0.5 + x * (1.0 / 6.0 + x * (1.0 / 24.0))))
    return jnp.where(jnp.abs(x) < 0.05, small, jnp.exp(x) - 1.0)


def _sum0(x):
    return jnp.sum(x, axis=0, keepdims=True)


def _ln_fwd(r, g, b):
    mu = jnp.mean(r, axis=-1, keepdims=True)
    xc = r - mu
    var = jnp.mean(xc * xc, axis=-1, keepdims=True)
    rstd = lax.rsqrt(var + LN_EPS)
    xhat = xc * rstd
    return xhat * g + b, xhat, rstd


def _ln_bwd(dout, xhat, rstd, g):
    dxh = dout * g
    m1 = jnp.mean(dxh, axis=-1, keepdims=True)
    m2 = jnp.mean(dxh * xhat, axis=-1, keepdims=True)
    return rstd * (dxh - m1 - xhat * m2)


def _rows(tm, n, col=0):
    return pl.BlockSpec((tm, n), lambda i: (i, col))


def _const(shape):
    nd = len(shape)
    return pl.BlockSpec(shape, lambda i: (0,) * nd)


def _mm(a, w, *, name):
    t, k = a.shape
    n = w.shape[1]
    tm = min(TM, t)

    def body(a_ref, w_ref, o_ref):
        o_ref[...] = _dot(a_ref[...], w_ref[...])

    return pl.pallas_call(
        body, name=name, grid=(t // tm,), in_specs=[_rows(tm, k), _const(w.shape)], out_specs=_rows(tm, n),
        out_shape=jax.ShapeDtypeStruct((t, n), F32), compiler_params=_cparams(("arbitrary",)),
    )(a, w)


DPROJ_PIECES = ((P_XBC, 1024), (P_Z, 512), (P_U, 256), (P_XRG, 256), (P_GRG, 256), (P_DT, LANES))


def _in_proj_bwd(pieces, w, dres):
    t = dres.shape[0]
    npc = len(pieces)

    def body(*refs):
        w_ref, r_ref, o_ref = refs[npc:]
        acc = r_ref[...]
        for p_ref, (off, k) in zip(refs[:npc], DPROJ_PIECES):
            acc = acc + _dot_nt(p_ref[...], w_ref[:, off:off + k])
        o_ref[...] = acc

    return pl.pallas_call(
        body, name="in_proj_bwd", grid=(t // TM,),
        in_specs=[_rows(TM, k) for _, k in DPROJ_PIECES] + [_const(w.shape), _rows(TM, D_MODEL)],
        out_specs=_rows(TM, D_MODEL), out_shape=jax.ShapeDtypeStruct((t, D_MODEL), F32),
        compiler_params=_cparams(("arbitrary",)),
    )(*pieces, w, dres)


def _wgrad_in(h0, pieces):
    t = h0.shape[0]
    npc = len(pieces)

    def body(*refs):
        h_ref, o_ref = refs[npc], refs[npc + 1]
        @pl.when(pl.program_id(0) == 0)
        def _():
            o_ref[...] = jnp.zeros_like(o_ref)

        hb = h_ref[...].astype(MXU_DTYPE)
        for p_ref, (off, k) in zip(refs[:npc], DPROJ_PIECES):
            o_ref[:, off:off + k] += _dot_tn(hb, p_ref[...])

    return pl.pallas_call(
        body, name="wgrad_in", grid=(t // TM,),
        in_specs=[_rows(TM, k) for _, k in DPROJ_PIECES] + [_rows(TM, D_MODEL)],
        out_specs=_const((D_MODEL, D_PACK)), out_shape=jax.ShapeDtypeStruct((D_MODEL, D_PACK), F32),
        compiler_params=_cparams(("arbitrary",)),
    )(*pieces, h0)


G_ROWS = 8192
G_PARTS = ((0, 4096), (4096, 2048), (6144, 2048))
W_IN_SHARD = 578
W_IN_PAD = 640
MISC_ROWS = 128
MISC_REP_ROW = 40
ROW_MISC = G_ROWS - MISC_ROWS
W_IN_BLOCK_ROWS = W_IN_PAD + MISC_ROWS


def _grad_row(name, l):
    base = 0 if l == 1 else 4096
    mid = base + 2048 if l == 1 else 6144
    return {"mlp_w1": base, "mlp_w2": base + 1024, "w_out": mid, "xa_wq": mid + 256, "xa_wk": mid + 512,
            "xa_wv": mid + 768, "xa_wo": mid + 1024, "w_in": mid + 1280}[name]


def _wgrad_flat(a, g, buf, *, mode, row_off, name):
    pieces = list(a) if isinstance(a, (list, tuple)) else [a]
    t = g.shape[0]
    tt = min(1024, t)
    ns = t // tt
    blk = D_MODEL

    def accumulate(o_ref, parts, s):
        @pl.when(s == 0)
        def _():
            o_ref[...] = jnp.zeros_like(o_ref)

        for q, v in parts:
            o_ref[q] += v

    if mode == "rows4":
        grid = (ns,)
        in_specs = [pl.BlockSpec((tt, p.shape[1]), lambda s: (s, 0)) for p in pieces]
        in_specs.append(pl.BlockSpec((tt, blk), lambda s: (s, 0)))
        out_spec = pl.BlockSpec((4, 256, FLAT), lambda s: (0, row_off // 256, 0))
        sem = ("arbitrary",)
        npc = len(pieces)

        def body(*refs):
            g_v = refs[npc][...]
            parts, q0 = [], 0
            for p_ref in refs[:npc]:
                full = _dot_tn(p_ref[...], g_v)
                nq = full.shape[0] // 256
                parts += [(q0 + q, full[q * 256:(q + 1) * 256]) for q in range(nq)]
                q0 += nq
            accumulate(refs[-1], parts, pl.program_id(0))
    else:
        grid = (2, ns)
        if mode == "rowblk":
            in_specs = [pl.BlockSpec((tt, 2 * blk), lambda q, s: (s, q)), pl.BlockSpec((tt, blk), lambda q, s: (s, 0))]
        else:
            in_specs = [pl.BlockSpec((tt, blk), lambda q, s: (s, 0)), pl.BlockSpec((tt, 2 * blk), lambda q, s: (s, q))]
        out_spec = pl.BlockSpec((2, blk, FLAT), lambda q, s: (q, row_off // blk, 0))
        sem = ("arbitrary", "arbitrary")

        def body(a_ref, g_ref, *rest):
            full = _dot_tn(a_ref[...], g_ref[...])
            if mode == "rowblk":
                parts = [(0, full[:blk]), (1, full[blk:])]
            else:
                parts = [(0, full[:, :blk]), (1, full[:, blk:])]
            accumulate(rest[-1], parts, pl.program_id(1))

    args = pieces + [g]
    aliases = {}
    if buf is not None:
        in_specs.append(pl.BlockSpec(memory_space=pl.ANY))
        args.append(buf)
        aliases = {len(args) - 1: 0}
    return pl.pallas_call(
        body, name=name, grid=grid, in_specs=in_specs, out_specs=out_spec,
        out_shape=jax.ShapeDtypeStruct((4, G_ROWS, FLAT), F32), input_output_aliases=aliases,
        compiler_params=_cparams(sem),
    )(*args)


def _outproj_ln_fwd(ys, h, w, g, b):
    t = h.shape[0]
    npc = len(ys)

    def body(*refs):
        h_ref, w_ref, g_ref, b_ref, hn_ref, xh_ref, rs_ref = refs[npc:]
        r = ALPHA * h_ref[...]
        off = 0
        for y_ref in refs[:npc]:
            k = y_ref.shape[1]
            r = r + _dot(y_ref[...], w_ref[off:off + k, :])
            off += k
        out, xhat, rstd = _ln_fwd(r, g_ref[...], b_ref[...])
        hn_ref[...] = out
        xh_ref[...] = xhat
        rs_ref[...] = rstd

    return pl.pallas_call(
        body, name="outproj_ln_fwd", grid=(t // TM,),
        in_specs=[_rows(TM, y.shape[1]) for y in ys] + [_rows(TM, D_MODEL), _const((D_MODEL, D_MODEL)),
                                                        _const((1, D_MODEL)), _const((1, D_MODEL))],
        out_specs=[_rows(TM, D_MODEL), _rows(TM, D_MODEL), _rows(TM, 1)],
        out_shape=[jax.ShapeDtypeStruct((t, D_MODEL), F32), jax.ShapeDtypeStruct((t, D_MODEL), F32),
                   jax.ShapeDtypeStruct((t, 1), F32)],
        compiler_params=_cparams(("arbitrary",)),
    )(*ys, h, w, g, b)


def _attn_probs(q, kb, hh):
    sl = slice(hh * XA_HEAD_DIM, (hh + 1) * XA_HEAD_DIM)
    s = _dot_nt(q[:, sl], kb[:, sl]) * (1.0 / math.sqrt(XA_HEAD_DIM))
    m = jnp.max(s, axis=-1, keepdims=True)
    e = jnp.exp(s - m)
    return e / jnp.sum(e, axis=-1, keepdims=True)


def _attn_ln_fwd(h1, wq, wo, kb, vb, g, b):
    t = h1.shape[0]

    def body(h_ref, wq_ref, wo_ref, k_ref, v_ref, g_ref, b_ref, hn_ref, xh_ref, rs_ref, o_ref):
        h = h_ref[...]
        q = _dot(h, wq_ref[...])
        kb_ = k_ref[...]
        vb_ = v_ref[...]
        for hh in range(XA_HEADS):
            sl = slice(hh * XA_HEAD_DIM, (hh + 1) * XA_HEAD_DIM)
            p = _attn_probs(q, kb_, hh)
            o_ref[:, sl] = _dot(p, vb_[:, sl]).astype(o_ref.dtype)
        r = ALPHA * h + _dot(o_ref[...], wo_ref[...])
        out, xhat, rstd = _ln_fwd(r, g_ref[...], b_ref[...])
        hn_ref[...] = out
        xh_ref[...] = xhat
        rs_ref[...] = rstd

    return pl.pallas_call(
        body, name="attn_ln_fwd", grid=(t // TM,),
        in_specs=[_rows(TM, D_MODEL), _const((D_MODEL, D_MODEL)), _const((D_MODEL, D_MODEL)),
                  _const((MEM_LEN, D_MODEL)), _const((MEM_LEN, D_MODEL)), _const((1, D_MODEL)), _const((1, D_MODEL))],
        out_specs=[_rows(TM, D_MODEL), _rows(TM, D_MODEL), _rows(TM, 1), _rows(TM, D_MODEL)],
        out_shape=[jax.ShapeDtypeStruct((t, D_MODEL), F32), jax.ShapeDtypeStruct((t, D_MODEL), F32),
                   jax.ShapeDtypeStruct((t, 1), F32), jax.ShapeDtypeStruct((t, D_MODEL), MXU_DTYPE)],
        compiler_params=_cparams(("arbitrary",)),
    )(h1, wq, wo, kb, vb, g, b)


def _attn_ln_bwd(dh2, xhat, rstd, g, h1, wq, wo, kb, vb):
    t = h1.shape[0]

    def body(dh_ref, xh_ref, rs_ref, g_ref, h_ref, wq_ref, wo_ref, k_ref, v_ref,
             dr_ref, dq_ref, dh1_ref, dk_ref, dv_ref, dg_ref, db_ref):
        i = pl.program_id(0)

        @pl.when(i == 0)
        def _():
            dk_ref[...] = jnp.zeros_like(dk_ref)
            dv_ref[...] = jnp.zeros_like(dv_ref)
            dg_ref[...] = jnp.zeros_like(dg_ref)
            db_ref[...] = jnp.zeros_like(db_ref)

        dout = dh_ref[...]
        xh = xh_ref[...]
        dg_ref[...] += _sum0(dout * xh)
        db_ref[...] += _sum0(dout)
        dr = _ln_bwd(dout, xh, rs_ref[...], g_ref[...])
        dr_ref[...] = dr.astype(dr_ref.dtype)
        do = _dot_nt(dr, wo_ref[...])
        h = h_ref[...]
        q = _dot(h, wq_ref[...])
        kb_ = k_ref[...]
        vb_ = v_ref[...]
        scale = 1.0 / math.sqrt(XA_HEAD_DIM)
        for hh in range(XA_HEADS):
            sl = slice(hh * XA_HEAD_DIM, (hh + 1) * XA_HEAD_DIM)
            p = _attn_probs(q, kb_, hh)
            do_h = do[:, sl]
            dp = _dot_nt(do_h, vb_[:, sl])
            ds = p * (dp - jnp.sum(dp * p, axis=-1, keepdims=True)) * scale
            dq_ref[:, sl] = _dot(ds, kb_[:, sl]).astype(dq_ref.dtype)
            dk_ref[:, sl] += _dot_tn(ds, q[:, sl])
            dv_ref[:, sl] += _dot_tn(p, do_h)
        dh1_ref[...] = ALPHA * dr + _dot_nt(dq_ref[...], wq_ref[...])

    return pl.pallas_call(
        body, name="attn_ln_bwd", grid=(t // TM,),
        in_specs=[_rows(TM, D_MODEL), _rows(TM, D_MODEL), _rows(TM, 1), _const((1, D_MODEL)), _rows(TM, D_MODEL),
                  _const((D_MODEL, D_MODEL)), _const((D_MODEL, D_MODEL)), _const((MEM_LEN, D_MODEL)),
                  _const((MEM_LEN, D_MODEL))],
        out_specs=[_rows(TM, D_MODEL), _rows(TM, D_MODEL), _rows(TM, D_MODEL), _const((MEM_LEN, D_MODEL)),
                   _const((MEM_LEN, D_MODEL)), _const((1, D_MODEL)), _const((1, D_MODEL))],
        out_shape=[jax.ShapeDtypeStruct((t, D_MODEL), MXU_DTYPE), jax.ShapeDtypeStruct((t, D_MODEL), MXU_DTYPE),
                   jax.ShapeDtypeStruct((t, D_MODEL), F32), jax.ShapeDtypeStruct((MEM_LEN, D_MODEL), F32),
                   jax.ShapeDtypeStruct((MEM_LEN, D_MODEL), F32), jax.ShapeDtypeStruct((1, D_MODEL), F32),
                   jax.ShapeDtypeStruct((1, D_MODEL), F32)],
        compiler_params=_cparams(("arbitrary",)),
    )(dh2, xhat, rstd, g, h1, wq, wo, kb, vb)


FF_CHUNK = 1024
N_FF = D_FF // FF_CHUNK


def _load_resident(pairs, sems):
    copies = [pltpu.make_async_copy(src, dst, sems.at[k]) for k, (src, dst) in enumerate(pairs)]
    for cp in copies:
        cp.start()
    for cp in copies:
        cp.wait()


def _mlp_ln_fwd(h2, w1, w2, g, b):
    t = h2.shape[0]

    def body(h_ref, w1_hbm, w2_hbm, g_ref, b_ref, hn_ref, xh_ref, rs_ref, hd_ref, w1_v, w2_v, acc_ref, sems):
        @pl.when(pl.program_id(0) == 0)
        def _():
            _load_resident([(w1_hbm, w1_v), (w2_hbm, w2_v)], sems)

        h = h_ref[...]
        hb = h.astype(MXU_DTYPE)
        acc_ref[...] = ALPHA * h
        for j in range(N_FF):
            sl = slice(j * FF_CHUNK, (j + 1) * FF_CHUNK)
            u = _dot(hb, w1_v[:, sl])
            hd = jnp.square(jnp.maximum(u, 0.0)).astype(MXU_DTYPE)
            hd_ref[:, sl] = hd
            acc_ref[...] += _dot(hd, w2_v[sl, :])
        out, xhat, rstd = _ln_fwd(acc_ref[...], g_ref[...], b_ref[...])
        hn_ref[...] = out
        xh_ref[...] = xhat
        rs_ref[...] = rstd

    return pl.pallas_call(
        body, name="mlp_ln_fwd", grid=(t // TM,),
        in_specs=[_rows(TM, D_MODEL), _hbm(), _hbm(), _const((1, D_MODEL)), _const((1, D_MODEL))],
        out_specs=[_rows(TM, D_MODEL), _rows(TM, D_MODEL), _rows(TM, 1), _rows(TM, D_FF)],
        out_shape=[jax.ShapeDtypeStruct((t, D_MODEL), F32), jax.ShapeDtypeStruct((t, D_MODEL), F32),
                   jax.ShapeDtypeStruct((t, 1), F32), jax.ShapeDtypeStruct((t, D_FF), MXU_DTYPE)],
        scratch_shapes=[pltpu.VMEM((D_MODEL, D_FF), MXU_DTYPE), pltpu.VMEM((D_FF, D_MODEL), MXU_DTYPE),
                        pltpu.VMEM((TM, D_MODEL), F32), pltpu.SemaphoreType.DMA((2,))],
        compiler_params=_cparams(("arbitrary",)),
    )(h2, w1, w2, g, b)


def _mlp_ln_bwd(dh3, xhat, rstd, g, hdn, w1, w2):
    t = dh3.shape[0]

    def body(dh_ref, xh_ref, rs_ref, g_ref, hd_ref, w1_hbm, w2_hbm,
             dr_ref, du_ref, dh2_ref, dg_ref, db_ref, w1_v, w2_v, acc_ref, sems):
        @pl.when(pl.program_id(0) == 0)
        def _():
            _load_resident([(w1_hbm, w1_v), (w2_hbm, w2_v)], sems)
            dg_ref[...] = jnp.zeros_like(dg_ref)
            db_ref[...] = jnp.zeros_like(db_ref)

        dout = dh_ref[...]
        xh = xh_ref[...]
        dg_ref[...] += _sum0(dout * xh)
        db_ref[...] += _sum0(dout)
        dr = _ln_bwd(dout, xh, rs_ref[...], g_ref[...])
        drb = dr.astype(MXU_DTYPE)
        dr_ref[...] = drb
        acc_ref[...] = ALPHA * dr
        for j in range(N_FF):
            sl = slice(j * FF_CHUNK, (j + 1) * FF_CHUNK)
            dhd = _dot_nt(drb, w2_v[sl, :])
            du = (dhd * (2.0 * jnp.sqrt(hd_ref[:, sl].astype(F32)))).astype(MXU_DTYPE)
            du_ref[:, sl] = du
            acc_ref[...] += _dot_nt(du, w1_v[:, sl])
        dh2_ref[...] = acc_ref[...]

    tm = TM // 2
    return pl.pallas_call(
        body, name="mlp_ln_bwd", grid=(t // tm,),
        in_specs=[_rows(tm, D_MODEL), _rows(tm, D_MODEL), _rows(tm, 1), _const((1, D_MODEL)), _rows(tm, D_FF),
                  _hbm(), _hbm()],
        out_specs=[_rows(tm, D_MODEL), _rows(tm, D_FF), _rows(tm, D_MODEL), _const((1, D_MODEL)),
                   _const((1, D_MODEL))],
        out_shape=[jax.ShapeDtypeStruct((t, D_MODEL), MXU_DTYPE), jax.ShapeDtypeStruct((t, D_FF), MXU_DTYPE),
                   jax.ShapeDtypeStruct((t, D_MODEL), F32), jax.ShapeDtypeStruct((1, D_MODEL), F32),
                   jax.ShapeDtypeStruct((1, D_MODEL), F32)],
        scratch_shapes=[pltpu.VMEM((D_MODEL, D_FF), MXU_DTYPE), pltpu.VMEM((D_FF, D_MODEL), MXU_DTYPE),
                        pltpu.VMEM((tm, D_MODEL), F32), pltpu.SemaphoreType.DMA((2,))],
        compiler_params=_cparams(("arbitrary",)),
    )(dh3, xhat, rstd, g, hdn, w1, w2)


def _outproj_ln_bwd(dh1, xhat, rstd, g, w):
    t = dh1.shape[0]

    def body(dh_ref, xh_ref, rs_ref, g_ref, w_ref, dr_ref, res_ref, dy_ref, dg_ref, db_ref):
        i = pl.program_id(0)

        @pl.when(i == 0)
        def _():
            dg_ref[...] = jnp.zeros_like(dg_ref)
            db_ref[...] = jnp.zeros_like(db_ref)

        dout = dh_ref[...]
        xh = xh_ref[...]
        dg_ref[...] += _sum0(dout * xh)
        db_ref[...] += _sum0(dout)
        dr = _ln_bwd(dout, xh, rs_ref[...], g_ref[...])
        dr_ref[...] = dr.astype(dr_ref.dtype)
        res_ref[...] = ALPHA * dr
        dy_ref[...] = _dot_nt(dr, w_ref[...])

    return pl.pallas_call(
        body, name="outproj_ln_bwd", grid=(t // TM,),
        in_specs=[_rows(TM, D_MODEL), _rows(TM, D_MODEL), _rows(TM, 1), _const((1, D_MODEL)),
                  _const((D_MODEL, D_MODEL))],
        out_specs=[_rows(TM, D_MODEL), _rows(TM, D_MODEL), _rows(TM, D_MODEL), _const((1, D_MODEL)),
                   _const((1, D_MODEL))],
        out_shape=[jax.ShapeDtypeStruct((t, D_MODEL), MXU_DTYPE), jax.ShapeDtypeStruct((t, D_MODEL), F32),
                   jax.ShapeDtypeStruct((t, D_MODEL), F32), jax.ShapeDtypeStruct((1, D_MODEL), F32),
                   jax.ShapeDtypeStruct((1, D_MODEL), F32)],
        compiler_params=_cparams(("arbitrary",)),
    )(dh1, xhat, rstd, g, w)


def _loss_fwd_bwd(h, target):
    t = h.shape[0]

    def body(h_ref, t_ref, l_ref, dh_ref):
        i = pl.program_id(0)

        @pl.when(i == 0)
        def _():
            l_ref[...] = jnp.zeros_like(l_ref)

        e = h_ref[...] - t_ref[...]
        dh_ref[...] = e * (1.0 / D_MODEL)
        per_tok = jnp.mean(e * e, axis=-1, keepdims=True)
        l_ref[...] += 0.5 * jnp.sum(per_tok, axis=0, keepdims=True)

    return pl.pallas_call(
        body, name="loss_fwd_bwd", grid=(t // TM,),
        in_specs=[_rows(TM, D_MODEL), _rows(TM, D_MODEL)],
        out_specs=[_const((1, 1)), _rows(TM, D_MODEL)],
        out_shape=[jax.ShapeDtypeStruct((1, 1), F32), jax.ShapeDtypeStruct((t, D_MODEL), F32)],
        compiler_params=_cparams(("arbitrary",)),
    )(h, target)


def _pick_col(x, idx):
    lane = lax.broadcasted_iota(jnp.int32, x.shape, 1)
    return jnp.sum(jnp.where(lane == idx, x, 0.0), axis=1, keepdims=True)


def _pick_row(x, idx):
    sub = lax.broadcasted_iota(jnp.int32, x.shape, 0)
    return jnp.sum(jnp.where(sub == idx, x, 0.0), axis=0, keepdims=True)


def _conv_taps(pad_ref, w, tm, base):
    acc = w[0:1, :] * pad_ref[base:base + tm, :]
    for k in range(1, 4):
        acc = acc + w[k:k + 1, :] * pad_ref[base + k:base + k + tm, :]
    return acc


def _ssd_chunk_common(adt_c, tri):
    cs = _dot_f32(tri, adt_c)
    return cs, cs.T, jnp.exp(cs)


def _ssd_head_terms(cs, cst, ecs, dt_c, h, tri):
    cs_col = _pick_col(cs, h)
    cs_row = _pick_row(cst, h)
    dt_col = _pick_col(dt_c, h)
    cs_last = cs_col[SSD_CHUNK - 1:SSD_CHUNK, :]
    lmat = jnp.exp(jnp.where(tri > 0.0, cs_col - cs_row, -1e30))
    ecs_col = _pick_col(ecs, h)
    decay_col = jnp.exp(cs_last - cs_col)
    return cs_col, dt_col, cs_last, lmat, ecs_col, decay_col


def _ssd_fwd(proj, cw, cb, dtb, a_neg, d_lanes, nw):
    t = proj.shape[0]
    tm = SSD_TM
    nt = t // tm
    ncq = tm // SSD_CHUNK
    hb = tm // SUBLANES

    def body(xbc_ref, halo_ref, z_ref, dt_ref, cw_ref, cb_ref, dtb_ref, a_ref, d_ref, nw_ref,
             y_ref, yy_ref, st_ref, xpad, xact, state):
        i = pl.program_id(0)

        @pl.when(i == 0)
        def _():
            state[...] = jnp.zeros_like(state)

        xpad[0:SUBLANES, :] = jnp.where(i > 0, halo_ref[...], 0.0)
        xpad[SUBLANES:SUBLANES + tm, :] = xbc_ref[...]
        acc = cb_ref[...] + _conv_taps(xpad, cw_ref[...], tm, SUBLANES - 3)
        xact[...] = acc * _sigmoid(acc)
        dt = _softplus(dt_ref[...] + dtb_ref[...])
        adt = dt * a_ref[...]
        r_i = lax.broadcasted_iota(jnp.int32, (SSD_CHUNK, SSD_CHUNK), 0)
        c_i = lax.broadcasted_iota(jnp.int32, (SSD_CHUNK, SSD_CHUNK), 1)
        tri = (r_i >= c_i).astype(F32)
        lane1 = lax.broadcasted_iota(jnp.int32, (1, LANES), 1)
        for c in range(ncq):
            sl = slice(c * SSD_CHUNK, (c + 1) * SSD_CHUNK)
            dt_c = dt[sl]
            cs, cst, ecs = _ssd_chunk_common(adt[sl], tri)
            for g in range(2):
                bg = xact[sl, 512 + g * 128:512 + (g + 1) * 128]
                cg = xact[sl, 768 + g * 128:768 + (g + 1) * 128]
                cbm = _dot_nt(cg, bg)
                for pr in range(2):
                    pi = g * 2 + pr
                    psl = slice(pi * 128, (pi + 1) * 128)
                    xp = xact[sl, psl]
                    prev = state[pi]
                    st_ref[c, pi] = prev
                    yp = xp * d_ref[:, psl]
                    new_s = jnp.zeros((SSD_STATE, LANES), F32)
                    dec_lane = jnp.zeros((1, LANES), F32)
                    for hh in range(2):
                        h = g * 4 + pr * 2 + hh
                        lm = (lane1 >= 64) if hh else (lane1 < 64)
                        _, dt_col, cs_last, lmat, ecs_col, decay_col = _ssd_head_terms(cs, cst, ecs, dt_c, h, tri)
                        xdt = jnp.where(lm, xp, 0.0) * dt_col
                        yp = yp + _dot(cbm * lmat, xdt)
                        yp = yp + _dot(cg * ecs_col, jnp.where(lm, prev, 0.0))
                        new_s = new_s + _dot_tn(bg * decay_col, xdt)
                        dec_lane = dec_lane + jnp.where(lm, jnp.exp(cs_last), 0.0)
                    state[pi] = prev * dec_lane + new_s
                    yy_ref[sl, psl] = yp
        yy = yy_ref[...]
        z = z_ref[...]
        yg = yy * (z * _sigmoid(z))
        ms = jnp.mean(yg * yg, axis=-1, keepdims=True)
        y_ref[...] = (yg * lax.rsqrt(ms + LN_EPS) * nw_ref[...]).astype(y_ref.dtype)

    halo_map = lambda i: (jnp.maximum(i * hb - 1, 0), 0)
    return pl.pallas_call(
        body, name="ssd_fwd", grid=(nt,),
        in_specs=[pl.BlockSpec((tm, SSD_XBC), lambda i: (i, 0)), pl.BlockSpec((SUBLANES, SSD_XBC), halo_map),
                  pl.BlockSpec((tm, SSD_WIDTH), lambda i: (i, P_Z // SSD_WIDTH)),
                  pl.BlockSpec((tm, LANES), lambda i: (i, P_DT // LANES)),
                  _const((4, SSD_XBC)), _const((1, SSD_XBC)), _const((1, LANES)), _const((1, LANES)),
                  _const((1, SSD_WIDTH)), _const((1, SSD_WIDTH))],
        out_specs=[_rows(tm, SSD_WIDTH), _rows(tm, SSD_WIDTH),
                   pl.BlockSpec((ncq, 4, SSD_STATE, LANES), lambda i: (i, 0, 0, 0))],
        out_shape=[jax.ShapeDtypeStruct((t, SSD_WIDTH), MXU_DTYPE), jax.ShapeDtypeStruct((t, SSD_WIDTH), F32),
                   jax.ShapeDtypeStruct((t // SSD_CHUNK, 4, SSD_STATE, LANES), F32)],
        scratch_shapes=[pltpu.VMEM((tm + SUBLANES, SSD_XBC), F32), pltpu.VMEM((tm, SSD_XBC), F32),
                        pltpu.VMEM((4, SSD_STATE, LANES), F32)],
        compiler_params=_cparams(("arbitrary",)),
    )(proj, proj, proj, proj, cw, cb, dtb, a_neg, d_lanes, nw)


def _ssd_bwd(dycat, proj, yy, states, cw, cb, dtb, a_neg, d_lanes, nw):
    t = proj.shape[0]
    tm = SSD_TM
    nt = t // tm
    ncq = tm // SSD_CHUNK
    hb = tm // SUBLANES

    def body(dy_ref, xbc_ref, halo_ref, z_ref, dt_ref, yy_ref, st_ref, cw_ref, cb_ref, dtb_ref, a_ref, d_ref, nw_ref,
             dxbc_ref, dz_ref, ddt_ref, dcw_ref, dcb_ref, ddtb_ref, da_ref, dd_ref, dnw_ref,
             xpad, xact, dxact, dpad, dstate, dnext):
        i = pl.program_id(0)

        @pl.when(i == 0)
        def _():
            for r in (dcw_ref, dcb_ref, ddtb_ref, da_ref, dd_ref, dnw_ref, dstate, dnext):
                r[...] = jnp.zeros_like(r)

        xpad[0:SUBLANES, :] = jnp.where(i < nt - 1, halo_ref[...], 0.0)
        xpad[SUBLANES:SUBLANES + tm, :] = xbc_ref[...]
        cw_v = cw_ref[...]
        acc = cb_ref[...] + _conv_taps(xpad, cw_v, tm, SUBLANES - 3)
        sig = _sigmoid(acc)
        xact[...] = acc * sig
        dt_raw = dt_ref[...] + dtb_ref[...]
        dt = _softplus(dt_raw)
        a_v = a_ref[...]
        adt = dt * a_v
        yy = yy_ref[...]
        z = z_ref[...]
        sz = _sigmoid(z)
        siluz = z * sz
        yg = yy * siluz
        ms = jnp.mean(yg * yg, axis=-1, keepdims=True)
        rinv = lax.rsqrt(ms + LN_EPS)
        dout = dy_ref[...]
        dnw_ref[...] += _sum0(dout * yg * rinv)
        dyn = dout * nw_ref[...]
        dyg = rinv * dyn - yg * (rinv * rinv * rinv) * jnp.mean(dyn * yg, axis=-1, keepdims=True)
        dyy = dyg * siluz
        dz_ref[...] = (dyg * yy * (sz * (1.0 + z * (1.0 - sz)))).astype(dz_ref.dtype)
        dd_ref[...] += _sum0(dyy * xact[:, 0:SSD_WIDTH])

        r_i = lax.broadcasted_iota(jnp.int32, (SSD_CHUNK, SSD_CHUNK), 0)
        c_i = lax.broadcasted_iota(jnp.int32, (SSD_CHUNK, SSD_CHUNK), 1)
        tri = (r_i >= c_i).astype(F32)
        lane1 = lax.broadcasted_iota(jnp.int32, (1, LANES), 1)
        for c in reversed(range(ncq)):
            sl = slice(c * SSD_CHUNK, (c + 1) * SSD_CHUNK)
            dt_c = dt[sl]
            cs, cst, ecs = _ssd_chunk_common(adt[sl], tri)
            cacc = jnp.zeros((SSD_CHUNK, LANES), F32)
            racc = jnp.zeros((SSD_CHUNK, LANES), F32)
            ddtx = jnp.zeros((SSD_CHUNK, LANES), F32)
            for g in range(2):
                bg = xact[sl, 512 + g * 128:512 + (g + 1) * 128]
                cg = xact[sl, 768 + g * 128:768 + (g + 1) * 128]
                cbm = _dot_nt(cg, bg)
                dcb_m = jnp.zeros((SSD_CHUNK, SSD_CHUNK), F32)
                dbg = jnp.zeros((SSD_CHUNK, SSD_STATE), F32)
                dcg = jnp.zeros((SSD_CHUNK, SSD_STATE), F32)
                for pr in range(2):
                    pi = g * 2 + pr
                    psl = slice(pi * 128, (pi + 1) * 128)
                    xp = xact[sl, psl]
                    dyp = dyy[sl, psl]
                    prev = st_ref[c, pi]
                    ds_all = dstate[pi]
                    dxdt_p = jnp.zeros((SSD_CHUNK, LANES), F32)
                    dprev_new = jnp.zeros((SSD_STATE, LANES), F32)
                    dec_lane = jnp.zeros((1, LANES), F32)
                    dt_lanes = jnp.zeros((SSD_CHUNK, LANES), F32)
                    for hh in range(2):
                        h = g * 4 + pr * 2 + hh
                        lm = (lane1 >= 64) if hh else (lane1 < 64)
                        oh_l = (c_i == h).astype(F32)
                        oh_s = (r_i == h).astype(F32)
                        _, dt_col, cs_last, lmat, ecs_col, decay_col = _ssd_head_terms(cs, cst, ecs, dt_c, h, tri)
                        gm = cbm * lmat
                        xm = jnp.where(lm, xp, 0.0)
                        xdt = xm * dt_col
                        dym = jnp.where(lm, dyp, 0.0)
                        prevm = jnp.where(lm, prev, 0.0)
                        dsm = jnp.where(lm, ds_all, 0.0)
                        bdec = bg * decay_col
                        dxdt = _dot_tn(gm, dym) + _dot(bdec, dsm)
                        dxdt_p = dxdt_p + dxdt
                        ddtx = ddtx + oh_l * jnp.sum(dxdt * xm, axis=1, keepdims=True)
                        dt_lanes = dt_lanes + jnp.where(lm, dt_col, 0.0)
                        dgm = _dot_nt(dym, xdt)
                        dcb_m = dcb_m + dgm * lmat
                        w = dgm * gm
                        cacc = cacc + oh_l * jnp.sum(w, axis=1, keepdims=True)
                        racc = racc - oh_s * jnp.sum(w, axis=0, keepdims=True)
                        dce = _dot_nt(dym, prevm)
                        dcg = dcg + dce * ecs_col
                        cacc = cacc + oh_l * (jnp.sum(dce * cg, axis=1, keepdims=True) * ecs_col)
                        dprev_new = dprev_new + _dot_tn(cg * ecs_col, dym)
                        dbdec = _dot_nt(xdt, dsm)
                        dbg = dbg + dbdec * decay_col
                        dd = jnp.sum(dbdec * bg, axis=1, keepdims=True) * decay_col
                        cacc = cacc - oh_l * dd
                        cd = jnp.exp(cs_last)
                        dlast = jnp.sum(dd, axis=0, keepdims=True) + jnp.sum(
                            jnp.sum(dsm * prevm, axis=1, keepdims=True), axis=0, keepdims=True) * cd
                        cacc = cacc + jnp.where((r_i == SSD_CHUNK - 1) & (c_i == h), dlast, 0.0)
                        dec_lane = dec_lane + jnp.where(lm, cd, 0.0)
                    dstate[pi] = ds_all * dec_lane + dprev_new
                    dxact[sl, psl] = dxdt_p * dt_lanes + dyp * d_ref[:, psl]
                dcg = dcg + _dot(dcb_m, bg)
                dbg = dbg + _dot_tn(dcb_m, cg)
                dxact[sl, 512 + g * 128:512 + (g + 1) * 128] = dbg
                dxact[sl, 768 + g * 128:768 + (g + 1) * 128] = dcg
            dcs = cacc + racc.T
            dadt = _dot_f32((r_i <= c_i).astype(F32), dcs)
            ddt = dadt * a_v + ddtx
            da_ref[...] += _sum0(dadt * dt_c)
            ddt_raw = ddt * _sigmoid(dt_raw[sl])
            ddt_ref[sl, :] = ddt_raw.astype(ddt_ref.dtype)
            ddtb_ref[...] += _sum0(ddt_raw)
        dacc = dxact[...] * (sig * (1.0 + acc * (1.0 - sig)))
        dcb_ref[...] += _sum0(dacc)
        for k in range(4):
            dcw_ref[k:k + 1, :] += _sum0(dacc * xpad[SUBLANES - 3 + k:SUBLANES - 3 + k + tm, :])
        dpad[0:tm, :] = dacc
        dpad[tm:tm + SUBLANES, :] = dnext[...]
        dx = cw_v[0:1, :] * dpad[3:3 + tm, :]
        for k in range(1, 4):
            dx = dx + cw_v[k:k + 1, :] * dpad[3 - k:3 - k + tm, :]
        dxbc_ref[...] = dx.astype(dxbc_ref.dtype)
        dnext[...] = dacc[0:SUBLANES, :]

    rev = lambda i: nt - 1 - i
    halo_map = lambda i: (jnp.maximum(rev(i) * hb - 1, 0), 0)
    rrow = lambda n, col=0: pl.BlockSpec((tm, n), lambda i: (rev(i), col))
    return pl.pallas_call(
        body, name="ssd_bwd", grid=(nt,),
        in_specs=[rrow(SSD_WIDTH), rrow(SSD_XBC), pl.BlockSpec((SUBLANES, SSD_XBC), halo_map),
                  rrow(SSD_WIDTH, P_Z // SSD_WIDTH), rrow(LANES, P_DT // LANES), rrow(SSD_WIDTH),
                  pl.BlockSpec((ncq, 4, SSD_STATE, LANES), lambda i: (rev(i), 0, 0, 0)),
                  _const((4, SSD_XBC)), _const((1, SSD_XBC)), _const((1, LANES)), _const((1, LANES)),
                  _const((1, SSD_WIDTH)), _const((1, SSD_WIDTH))],
        out_specs=[rrow(SSD_XBC), rrow(SSD_WIDTH), rrow(LANES), _const((SUBLANES, SSD_XBC)), _const((1, SSD_XBC)),
                   _const((1, LANES)), _const((1, LANES)), _const((1, SSD_WIDTH)), _const((1, SSD_WIDTH))],
        out_shape=[jax.ShapeDtypeStruct((t, SSD_XBC), MXU_DTYPE), jax.ShapeDtypeStruct((t, SSD_WIDTH), MXU_DTYPE),
                   jax.ShapeDtypeStruct((t, LANES), MXU_DTYPE), jax.ShapeDtypeStruct((SUBLANES, SSD_XBC), F32),
                   jax.ShapeDtypeStruct((1, SSD_XBC), F32), jax.ShapeDtypeStruct((1, LANES), F32),
                   jax.ShapeDtypeStruct((1, LANES), F32), jax.ShapeDtypeStruct((1, SSD_WIDTH), F32),
                   jax.ShapeDtypeStruct((1, SSD_WIDTH), F32)],
        scratch_shapes=[pltpu.VMEM((tm + SUBLANES, SSD_XBC), F32), pltpu.VMEM((tm, SSD_XBC), F32),
                        pltpu.VMEM((tm, SSD_XBC), F32), pltpu.VMEM((tm + SUBLANES, SSD_XBC), F32),
                        pltpu.VMEM((4, SSD_STATE, LANES), F32), pltpu.VMEM((SUBLANES, SSD_XBC), F32)],
        compiler_params=_cparams(("arbitrary",)),
    )(dycat, proj, proj, proj, proj, yy, states, cw, cb, dtb, a_neg, d_lanes, nw)


def _cmul_add(ar, ai, br, bi, cr, ci):
    return ar + br * cr - bi * ci, ai + br * ci + bi * cr


def _s5_fwd(proj, bre, bim, cre, cim, d_skip, glu_w, glu_b, coef):
    t = proj.shape[0]
    tm = SCAN_TM
    ng = tm // SUBLANES

    def body(u_ref, bre_ref, bim_ref, cre_ref, cim_ref, d_ref, w_ref, b_ref, coef_ref,
             y_ref, y2_ref, hre_ref, him_ref, carry):
        i = pl.program_id(0)

        @pl.when(i == 0)
        def _():
            carry[...] = jnp.zeros_like(carry)

        u = u_ref[...]
        hre_ref[...] = _dot(u, bre_ref[...])
        him_ref[...] = _dot(u, bim_ref[...])

        def step(gi, car):
            cr_, ci_ = car
            rows = pl.ds(pl.multiple_of(gi * SUBLANES, SUBLANES), SUBLANES)
            r = hre_ref[rows, :]
            m = him_ref[rows, :]
            for k, sh in enumerate((1, 2, 4)):
                r, m = _cmul_add(r, m, coef_ref[k, 0], coef_ref[k, 1], pltpu.roll(r, sh, 0), pltpu.roll(m, sh, 0))
            r, m = _cmul_add(r, m, coef_ref[3, 0], coef_ref[3, 1], cr_, ci_)
            hre_ref[rows, :] = r
            him_ref[rows, :] = m
            return (jnp.broadcast_to(r[SUBLANES - 1:SUBLANES, :], r.shape),
                    jnp.broadcast_to(m[SUBLANES - 1:SUBLANES, :], m.shape))

        cr_, ci_ = lax.fori_loop(0, ng, step, (carry[0], carry[1]))
        carry[0] = cr_
        carry[1] = ci_
        y2 = _dot(hre_ref[...], cre_ref[...]) - _dot(him_ref[...], cim_ref[...]) + d_ref[...] * u
        y2_ref[...] = y2
        ya = _gelu(y2)
        y_ref[...] = (ya * _sigmoid(_dot(ya, w_ref[...]) + b_ref[...])).astype(y_ref.dtype)

    return pl.pallas_call(
        body, name="s5_fwd", grid=(t // tm,),
        in_specs=[pl.BlockSpec((tm, S5_WIDTH), lambda i: (i, P_U // S5_WIDTH)),
                  _const((S5_WIDTH, S5_NSTATE)), _const((S5_WIDTH, S5_NSTATE)), _const((S5_NSTATE, S5_WIDTH)),
                  _const((S5_NSTATE, S5_WIDTH)), _const((1, S5_WIDTH)), _const((S5_WIDTH, S5_WIDTH)),
                  _const((1, S5_WIDTH)), _const((5, 2, SUBLANES, S5_NSTATE))],
        out_specs=[_rows(tm, S5_WIDTH), _rows(tm, S5_WIDTH), _rows(tm, S5_NSTATE), _rows(tm, S5_NSTATE)],
        out_shape=[jax.ShapeDtypeStruct((t, S5_WIDTH), MXU_DTYPE), jax.ShapeDtypeStruct((t, S5_WIDTH), F32),
                   jax.ShapeDtypeStruct((t, S5_NSTATE), F32), jax.ShapeDtypeStruct((t, S5_NSTATE), F32)],
        scratch_shapes=[pltpu.VMEM((2, SUBLANES, S5_NSTATE), F32)],
        compiler_params=_cparams(("arbitrary",)),
    )(proj, bre, bim, cre, cim, d_skip, glu_w, glu_b, coef)


def _s5_bwd(dycat, proj, y2, hre, him, bre, bim, cre, cim, d_skip, glu_w, glu_b, rcoef):
    t = proj.shape[0]
    tm = SCAN_TM
    nt = t // tm
    ng = tm // SUBLANES
    hb = tm // SUBLANES

    def body(dy_ref, u_ref, y2_ref, hre_ref, him_ref, hre_halo, him_halo, bre_ref, bim_ref, cre_ref, cim_ref, d_ref,
             w_ref, b_ref, coef_ref,
             du_ref, dbre_ref, dbim_ref, dcre_ref, dcim_ref, dlam_ref, dd_ref, dw_ref, dgb_ref,
             gre, gim, hpre, hpim, carry):
        i = pl.program_id(0)

        @pl.when(i == 0)
        def _():
            for r in (dbre_ref, dbim_ref, dcre_ref, dcim_ref, dlam_ref, dd_ref, dw_ref, dgb_ref, carry):
                r[...] = jnp.zeros_like(r)

        u = u_ref[...]
        y2 = y2_ref[...]
        dout = dy_ref[...]
        ya = _gelu(y2)
        sg = _sigmoid(_dot(ya, w_ref[...]) + b_ref[...])
        dv = dout * ya * sg * (1.0 - sg)
        dya = dout * sg + _dot_nt(dv, w_ref[...])
        dw_ref[...] += _dot_tn(ya, dv)
        dgb_ref[...] += _sum0(dv)
        dy2 = dya * _gelu_grad(y2)
        dd_ref[...] += _sum0(dy2 * u)
        hre_v = hre_ref[...]
        him_v = him_ref[...]
        dcre_ref[...] += _dot_tn(hre_v, dy2)
        dcim_ref[...] -= _dot_tn(him_v, dy2)
        gre[...] = _dot_nt(dy2, cre_ref[...])
        gim[...] = -_dot_nt(dy2, cim_ref[...])
        first = i == nt - 1
        hpre[0:SUBLANES, :] = jnp.where(first, 0.0, hre_halo[...])
        hpim[0:SUBLANES, :] = jnp.where(first, 0.0, him_halo[...])
        hpre[SUBLANES:SUBLANES + tm, :] = hre_v
        hpim[SUBLANES:SUBLANES + tm, :] = him_v
        row0 = lax.broadcasted_iota(jnp.int32, (SUBLANES, S5_NSTATE), 0) == 0

        def step(k, car):
            cr_, ci_, dlr, dli = car
            gi = ng - 1 - k
            rows = pl.ds(pl.multiple_of(gi * SUBLANES, SUBLANES), SUBLANES)
            nrows = pl.ds(pl.multiple_of(gi * SUBLANES + SUBLANES, SUBLANES), SUBLANES)
            r = gre[rows, :]
            m = gim[rows, :]
            for kk, sh in enumerate((1, 2, 4)):
                r, m = _cmul_add(r, m, coef_ref[kk, 0], coef_ref[kk, 1], pltpu.roll(r, SUBLANES - sh, 0),
                                 pltpu.roll(m, SUBLANES - sh, 0))
            r, m = _cmul_add(r, m, coef_ref[3, 0], coef_ref[3, 1], cr_, ci_)
            gre[rows, :] = r
            gim[rows, :] = m
            pr_ = hpre[rows, :]
            pm_ = hpim[rows, :]
            hr_ = jnp.where(row0, jnp.broadcast_to(pr_[SUBLANES - 1:SUBLANES, :], pr_.shape),
                            pltpu.roll(hpre[nrows, :], 1, 0))
            hm_ = jnp.where(row0, jnp.broadcast_to(pm_[SUBLANES - 1:SUBLANES, :], pm_.shape),
                            pltpu.roll(hpim[nrows, :], 1, 0))
            dlr = dlr + hr_ * r + hm_ * m
            dli = dli + hr_ * m - hm_ * r
            return (jnp.broadcast_to(r[0:1, :], r.shape), jnp.broadcast_to(m[0:1, :], m.shape), dlr, dli)

        z8 = jnp.zeros((SUBLANES, S5_NSTATE), F32)
        cr_, ci_, dlr, dli = lax.fori_loop(0, ng, step, (carry[0], carry[1], z8, z8))
        carry[0] = cr_
        carry[1] = ci_
        dlam_ref[0] += dlr
        dlam_ref[1] += dli
        g_re = gre[...]
        g_im = gim[...]
        du_ref[...] = (dy2 * d_ref[...] + _dot_nt(g_re, bre_ref[...]) + _dot_nt(g_im, bim_ref[...])
                       ).astype(du_ref.dtype)
        dbre_ref[...] += _dot_tn(u, g_re)
        dbim_ref[...] += _dot_tn(u, g_im)

    rev = lambda i: nt - 1 - i
    rrow = lambda n, col=0: pl.BlockSpec((tm, n), lambda i: (rev(i), col))
    halo = pl.BlockSpec((SUBLANES, S5_NSTATE), lambda i: (jnp.maximum(rev(i) * hb - 1, 0), 0))
    return pl.pallas_call(
        body, name="s5_bwd", grid=(nt,),
        in_specs=[rrow(S5_WIDTH, 512 // S5_WIDTH), rrow(S5_WIDTH, P_U // S5_WIDTH), rrow(S5_WIDTH),
                  rrow(S5_NSTATE), rrow(S5_NSTATE), halo, halo,
                  _const((S5_WIDTH, S5_NSTATE)), _const((S5_WIDTH, S5_NSTATE)), _const((S5_NSTATE, S5_WIDTH)),
                  _const((S5_NSTATE, S5_WIDTH)), _const((1, S5_WIDTH)), _const((S5_WIDTH, S5_WIDTH)),
                  _const((1, S5_WIDTH)), _const((5, 2, SUBLANES, S5_NSTATE))],
        out_specs=[rrow(S5_WIDTH), _const((S5_WIDTH, S5_NSTATE)), _const((S5_WIDTH, S5_NSTATE)),
                   _const((S5_NSTATE, S5_WIDTH)), _const((S5_NSTATE, S5_WIDTH)), _const((2, SUBLANES, S5_NSTATE)),
                   _const((1, S5_WIDTH)), _const((S5_WIDTH, S5_WIDTH)), _const((1, S5_WIDTH))],
        out_shape=[jax.ShapeDtypeStruct((t, S5_WIDTH), MXU_DTYPE), jax.ShapeDtypeStruct((S5_WIDTH, S5_NSTATE), F32),
                   jax.ShapeDtypeStruct((S5_WIDTH, S5_NSTATE), F32), jax.ShapeDtypeStruct((S5_NSTATE, S5_WIDTH), F32),
                   jax.ShapeDtypeStruct((S5_NSTATE, S5_WIDTH), F32),
                   jax.ShapeDtypeStruct((2, SUBLANES, S5_NSTATE), F32), jax.ShapeDtypeStruct((1, S5_WIDTH), F32),
                   jax.ShapeDtypeStruct((S5_WIDTH, S5_WIDTH), F32), jax.ShapeDtypeStruct((1, S5_WIDTH), F32)],
        scratch_shapes=[pltpu.VMEM((tm, S5_NSTATE), F32), pltpu.VMEM((tm, S5_NSTATE), F32),
                        pltpu.VMEM((tm + SUBLANES, S5_NSTATE), F32), pltpu.VMEM((tm + SUBLANES, S5_NSTATE), F32),
                        pltpu.VMEM((2, SUBLANES, S5_NSTATE), F32)],
        compiler_params=_cparams(("arbitrary",)),
    )(dycat, proj, y2, hre, him, hre, him, bre, bim, cre, cim, d_skip, glu_w, glu_b, rcoef)


def _rg_gates(xc, wa, ba, wx, bx, nsp):
    r = _sigmoid(_dot(xc, wa) + ba)
    ig = _sigmoid(_dot(xc, wx) + bx)
    log_a = nsp * r
    a = jnp.exp(log_a)
    mult = jnp.sqrt(-_expm1(2.0 * log_a))
    return r, ig, a, mult


def _rg_fwd(proj, cw, cb, wa, ba, wx, bx, nsp):
    t = proj.shape[0]
    tm = SCAN_TM
    ng = tm // SUBLANES
    hb = tm // SUBLANES

    def body(x_ref, halo_ref, gt_ref, cw_ref, cb_ref, wa_ref, ba_ref, wx_ref, bx_ref, nsp_ref,
             y_ref, h_ref, xpad, abuf, carry):
        i = pl.program_id(0)

        @pl.when(i == 0)
        def _():
            carry[...] = jnp.zeros_like(carry)

        xpad[0:SUBLANES, :] = jnp.where(i > 0, halo_ref[...], 0.0)
        xpad[SUBLANES:SUBLANES + tm, :] = x_ref[...]
        xc = cb_ref[...] + _conv_taps(xpad, cw_ref[...], tm, SUBLANES - 3)
        _, ig, a, mult = _rg_gates(xc, wa_ref[...], ba_ref[...], wx_ref[...], bx_ref[...], nsp_ref[...])
        abuf[...] = a
        h_ref[...] = mult * (ig * xc)
        sub = lax.broadcasted_iota(jnp.int32, (SUBLANES, RG_WIDTH), 0)

        def step(gi, car):
            rows = pl.ds(pl.multiple_of(gi * SUBLANES, SUBLANES), SUBLANES)
            av = abuf[rows, :]
            bv = h_ref[rows, :]
            for sh in (1, 2, 4):
                m = sub >= sh
                bv = jnp.where(m, av * pltpu.roll(bv, sh, 0) + bv, bv)
                av = jnp.where(m, av * pltpu.roll(av, sh, 0), av)
            hv = bv + av * car
            h_ref[rows, :] = hv
            return jnp.broadcast_to(hv[SUBLANES - 1:SUBLANES, :], hv.shape)

        carry[...] = lax.fori_loop(0, ng, step, carry[...])
        y_ref[...] = (h_ref[...] * _gelu(gt_ref[...])).astype(y_ref.dtype)

    return pl.pallas_call(
        body, name="rg_fwd", grid=(t // tm,),
        in_specs=[pl.BlockSpec((tm, RG_WIDTH), lambda i: (i, P_XRG // RG_WIDTH)),
                  pl.BlockSpec((SUBLANES, RG_WIDTH), lambda i: (jnp.maximum(i * hb - 1, 0), P_XRG // RG_WIDTH)),
                  pl.BlockSpec((tm, RG_WIDTH), lambda i: (i, P_GRG // RG_WIDTH)),
                  _const((4, RG_WIDTH)), _const((1, RG_WIDTH)), _const((RG_WIDTH, RG_WIDTH)), _const((1, RG_WIDTH)),
                  _const((RG_WIDTH, RG_WIDTH)), _const((1, RG_WIDTH)), _const((1, RG_WIDTH))],
        out_specs=[_rows(tm, RG_WIDTH), _rows(tm, RG_WIDTH)],
        out_shape=[jax.ShapeDtypeStruct((t, RG_WIDTH), MXU_DTYPE), jax.ShapeDtypeStruct((t, RG_WIDTH), F32)],
        scratch_shapes=[pltpu.VMEM((tm + SUBLANES, RG_WIDTH), F32), pltpu.VMEM((tm, RG_WIDTH), F32),
                        pltpu.VMEM((SUBLANES, RG_WIDTH), F32)],
        compiler_params=_cparams(("arbitrary",)),
    )(proj, proj, proj, cw, cb, wa, ba, wx, bx, nsp)


def _rg_bwd(dycat, proj, hs, cw, cb, wa, ba, wx, bx, nsp):
    t = proj.shape[0]
    tm = SCAN_TM
    nt = t // tm
    ng = tm // SUBLANES
    hb = tm // SUBLANES

    def body(dy_ref, x_ref, halo_ref, gt_ref, h_ref, h_halo, cw_ref, cb_ref, wa_ref, ba_ref, wx_ref, bx_ref, nsp_ref,
             dx_ref, dgt_ref, dcw_ref, dcb_ref, dwa_ref, dba_ref, dwx_ref, dbx_ref, dnsp_ref,
             xpad, abuf, gbuf, hpad, dabuf, dpad, carry, dnext):
        i = pl.program_id(0)

        @pl.when(i == 0)
        def _():
            for r in (dcw_ref, dcb_ref, dwa_ref, dba_ref, dwx_ref, dbx_ref, dnsp_ref, carry, dnext):
                r[...] = jnp.zeros_like(r)

        first = i == nt - 1
        xpad[0:SUBLANES, :] = jnp.where(first, 0.0, halo_ref[...])
        xpad[SUBLANES:SUBLANES + tm, :] = x_ref[...]
        cw_v = cw_ref[...]
        xc = cb_ref[...] + _conv_taps(xpad, cw_v, tm, SUBLANES - 3)
        nsp_v = nsp_ref[...]
        r, ig, a, mult = _rg_gates(xc, wa_ref[...], ba_ref[...], wx_ref[...], bx_ref[...], nsp_v)
        abuf[...] = a
        hv = h_ref[...]
        hpad[0:SUBLANES, :] = jnp.where(first, 0.0, h_halo[...])
        hpad[SUBLANES:SUBLANES + tm, :] = hv
        gt = gt_ref[...]
        dout = dy_ref[...]
        dgt_ref[...] = (dout * hv * _gelu_grad(gt)).astype(dgt_ref.dtype)
        gbuf[...] = dout * _gelu(gt)
        sub = lax.broadcasted_iota(jnp.int32, (SUBLANES, RG_WIDTH), 0)
        last_row = sub == SUBLANES - 1
        row0 = sub == 0

        def step(k, car):
            gi = ng - 1 - k
            rows = pl.ds(pl.multiple_of(gi * SUBLANES, SUBLANES), SUBLANES)
            nrows = pl.ds(pl.multiple_of(gi * SUBLANES + SUBLANES, SUBLANES), SUBLANES)
            av = abuf[rows, :]
            bv = gbuf[rows, :] + jnp.where(last_row, car, 0.0)
            ev = jnp.where(last_row, 0.0, pltpu.roll(av, SUBLANES - 1, 0))
            for sh in (1, 2, 4):
                m = sub < SUBLANES - sh
                bv = jnp.where(m, bv + ev * pltpu.roll(bv, SUBLANES - sh, 0), bv)
                ev = jnp.where(m, ev * pltpu.roll(ev, SUBLANES - sh, 0), 0.0)
            gbuf[rows, :] = bv
            pv = hpad[rows, :]
            hprev = jnp.where(row0, jnp.broadcast_to(pv[SUBLANES - 1:SUBLANES, :], pv.shape),
                              pltpu.roll(hpad[nrows, :], 1, 0))
            dabuf[rows, :] = bv * hprev
            return jnp.broadcast_to((av * bv)[0:1, :], bv.shape)

        carry[...] = lax.fori_loop(0, ng, step, carry[...])
        gv = gbuf[...]
        da = dabuf[...]
        ix = ig * xc
        dmult = gv * ix
        dig = gv * mult * xc
        dxc = gv * mult * ig
        dlog_a = da * a - dmult * (a * a) / mult
        dnsp_ref[...] += _sum0(dlog_a * r)
        dpr = dlog_a * nsp_v * r * (1.0 - r)
        dpi = dig * ig * (1.0 - ig)
        dxc = dxc + _dot_nt(dpr, wa_ref[...]) + _dot_nt(dpi, wx_ref[...])
        dwa_ref[...] += _dot_tn(xc, dpr)
        dwx_ref[...] += _dot_tn(xc, dpi)
        dba_ref[...] += _sum0(dpr)
        dbx_ref[...] += _sum0(dpi)
        dcb_ref[...] += _sum0(dxc)
        for k in range(4):
            dcw_ref[k:k + 1, :] += _sum0(dxc * xpad[SUBLANES - 3 + k:SUBLANES - 3 + k + tm, :])
        dpad[0:tm, :] = dxc
        dpad[tm:tm + SUBLANES, :] = dnext[...]
        dx = cw_v[0:1, :] * dpad[3:3 + tm, :]
        for k in range(1, 4):
            dx = dx + cw_v[k:k + 1, :] * dpad[3 - k:3 - k + tm, :]
        dx_ref[...] = dx.astype(dx_ref.dtype)
        dnext[...] = dxc[0:SUBLANES, :]

    rev = lambda i: nt - 1 - i
    rrow = lambda n, col=0: pl.BlockSpec((tm, n), lambda i: (rev(i), col))
    sq = _const((RG_WIDTH, RG_WIDTH))
    vec = _const((1, RG_WIDTH))
    return pl.pallas_call(
        body, name="rg_bwd", grid=(nt,),
        in_specs=[rrow(RG_WIDTH, 768 // RG_WIDTH), rrow(RG_WIDTH, P_XRG // RG_WIDTH),
                  pl.BlockSpec((SUBLANES, RG_WIDTH), lambda i: (jnp.maximum(rev(i) * hb - 1, 0), P_XRG // RG_WIDTH)),
                  rrow(RG_WIDTH, P_GRG // RG_WIDTH), rrow(RG_WIDTH),
                  pl.BlockSpec((SUBLANES, RG_WIDTH), lambda i: (jnp.maximum(rev(i) * hb - 1, 0), 0)),
                  _const((4, RG_WIDTH)), vec, sq, vec, sq, vec, vec],
        out_specs=[rrow(RG_WIDTH), rrow(RG_WIDTH), _const((SUBLANES, RG_WIDTH)), vec, sq, vec, sq, vec, vec],
        out_shape=[jax.ShapeDtypeStruct((t, RG_WIDTH), MXU_DTYPE), jax.ShapeDtypeStruct((t, RG_WIDTH), MXU_DTYPE),
                   jax.ShapeDtypeStruct((SUBLANES, RG_WIDTH), F32), jax.ShapeDtypeStruct((1, RG_WIDTH), F32),
                   jax.ShapeDtypeStruct((RG_WIDTH, RG_WIDTH), F32), jax.ShapeDtypeStruct((1, RG_WIDTH), F32),
                   jax.ShapeDtypeStruct((RG_WIDTH, RG_WIDTH), F32), jax.ShapeDtypeStruct((1, RG_WIDTH), F32),
                   jax.ShapeDtypeStruct((1, RG_WIDTH), F32)],
        scratch_shapes=[pltpu.VMEM((tm + SUBLANES, RG_WIDTH), F32), pltpu.VMEM((tm, RG_WIDTH), F32),
                        pltpu.VMEM((tm, RG_WIDTH), F32), pltpu.VMEM((tm + SUBLANES, RG_WIDTH), F32),
                        pltpu.VMEM((tm, RG_WIDTH), F32), pltpu.VMEM((tm + SUBLANES, RG_WIDTH), F32),
                        pltpu.VMEM((SUBLANES, RG_WIDTH), F32), pltpu.VMEM((SUBLANES, RG_WIDTH), F32)],
        compiler_params=_cparams(("arbitrary",)),
    )(dycat, proj, proj, proj, hs, hs, cw, cb, wa, ba, wx, bx, nsp)


def _block_diag(blocks):
    g, a, b = blocks.shape
    eye = jnp.eye(g, dtype=blocks.dtype)
    return (eye[:, None, :, None] * blocks[:, :, None, :]).reshape(g * a, g * b)


def _block_diag_extract(m, g):
    a, b = m.shape[0] // g, m.shape[1] // g
    m4 = m.reshape(g, a, g, b)
    idx = jnp.arange(g)
    return m4[idx, :, idx, :]


def _s5_prepare(lam_re, lam_im, log_step, b_re, b_im, c_re, c_im):
    step = jnp.exp(log_step)[:, None]
    mag = jnp.exp(lam_re * step)
    lbr = mag * jnp.cos(lam_im * step)
    lbi = mag * jnp.sin(lam_im * step)
    nr, ni = lbr - 1.0, lbi
    den = lam_re * lam_re + lam_im * lam_im
    cr = (nr * lam_re + ni * lam_im) / den
    ci = (ni * lam_re - nr * lam_im) / den
    bbr = cr[..., None] * b_re - ci[..., None] * b_im
    bbi = cr[..., None] * b_im + ci[..., None] * b_re
    bre = _block_diag(jnp.swapaxes(bbr, 1, 2))
    bim = _block_diag(jnp.swapaxes(bbi, 1, 2))
    cre = _block_diag(jnp.swapaxes(c_re, 1, 2))
    cim = _block_diag(jnp.swapaxes(c_im, 1, 2))
    return lbr.reshape(-1), lbi.reshape(-1), bre, bim, cre, cim


def _s5_scan_coef(lbr, lbi, reverse):
    if reverse:
        lbi = -lbi
    pr, pi = [lbr], [lbi]
    for _ in range(7):
        pr, pi = pr + [pr[-1] * lbr - pi[-1] * lbi], pi + [pr[-1] * lbi + pi[-1] * lbr]
    row = jnp.arange(SUBLANES)[:, None]
    tabs = []
    for sh in (1, 2, 4):
        keep = (row < SUBLANES - sh) if reverse else (row >= sh)
        tabs.append(jnp.stack([jnp.where(keep, pr[sh - 1][None, :], 0.0), jnp.where(keep, pi[sh - 1][None, :], 0.0)]))
    powr = jnp.stack(pr)
    powi = jnp.stack(pi)
    if reverse:
        powr, powi = powr[::-1], powi[::-1]
    tabs.append(jnp.stack([powr, powi]))
    tabs.append(jnp.zeros_like(tabs[-1]))
    return jnp.stack(tabs).astype(F32)


def _xy_peers():
    x, y, c = lax.axis_index("x"), lax.axis_index("y"), lax.axis_index("c")
    return x, y, c, [(1 - x, y), (x, 1 - y), (1 - x, 1 - y)]


def _hbm():
    return pl.BlockSpec(memory_space=pl.ANY)


def _xy_allgather(buf, *, name):
    n, w = buf.shape

    def body(x_ref, out_ref, send_sems, recv_sems, local_sem):
        x, y, c, peers = _xy_peers()
        me = 2 * x + y
        own = pltpu.make_async_copy(x_ref, out_ref.at[me], local_sem)
        own.start()
        sends = []
        for k, (px, py) in enumerate(peers):
            cp = pltpu.make_async_remote_copy(src_ref=x_ref, dst_ref=out_ref.at[me], send_sem=send_sems.at[k],
                                              recv_sem=recv_sems.at[k], device_id=(px, py, c), device_id_type=MESH)
            cp.start()
            sends.append(cp)
        for k, (px, py) in enumerate(peers):
            pltpu.make_async_remote_copy(src_ref=x_ref, dst_ref=out_ref.at[2 * px + py], send_sem=send_sems.at[k],
                                         recv_sem=recv_sems.at[k], device_id=(px, py, c),
                                         device_id_type=MESH).wait_recv()
        for cp in sends:
            cp.wait_send()
        own.wait()

    return pl.pallas_call(
        body, name=name, in_specs=[_hbm()], out_specs=_hbm(),
        out_shape=jax.ShapeDtypeStruct((4, n, w), buf.dtype),
        scratch_shapes=[pltpu.SemaphoreType.DMA((3,)), pltpu.SemaphoreType.DMA((3,)), pltpu.SemaphoreType.DMA],
    )(buf)


def _remote(src, dst, send_sem, recv_sem, dev):
    return pltpu.make_async_remote_copy(src_ref=src, dst_ref=dst, send_sem=send_sem, recv_sem=recv_sem,
                                        device_id=dev, device_id_type=MESH)


LAYER_GATHERED = (
    ("ssd_conv_w", (4, 256), 1), ("rg_conv_w", (4, LANES), 1),
    ("w_in", (1024, W_IN_PAD), 1), ("s5_glu_w", (64, 256), 0), ("w_out", (256, 1024), 0), ("xa_wq", (256, 1024), 0),
    ("xa_wk", (256, 1024), 0), ("xa_wv", (256, 1024), 0), ("xa_wo", (256, 1024), 0), ("mlp_w1", (1024, 1024), 1),
    ("mlp_w2", (1024, 1024), 0),
)
N_GATHERED = len(LAYER_GATHERED)
WAIT_GROUPS = ((0, 1, 2, 3), (4,), (5, 6, 7, 8), (9, 10))
RG_CONV_SHARD = RG_WIDTH // 4
N_GATHER_COPIES = 3 * N_GATHERED * DEPTH


def _gather_part(ref, t, pos):
    _, shp, ax = LAYER_GATHERED[t % N_GATHERED]
    idx = tuple(pl.ds(pos * shp[ax], shp[ax]) if d == ax else slice(None) for d in range(len(shp)))
    return ref.at[idx]


def _gather_start(shards):
    n = len(shards)
    lands = []
    for t, s in enumerate(shards):
        _, shp, ax = LAYER_GATHERED[t % N_GATHERED]
        full = shp[:ax] + (4 * shp[ax],) + shp[ax + 1:]
        lands.append(pltpu.with_memory_space_constraint(lax.empty(full, s.dtype), pltpu.HBM))

    def body(*refs):
        srcs, lnds = refs[:n], refs[n:2 * n]
        send_sems, recv_sems, local_sems = refs[2 * n:2 * n + 3]
        token = refs[-1]
        x, y, c, peers = _xy_peers()
        me = 2 * x + y
        for t in range(n):
            for k, (px, py) in enumerate(peers):
                _remote(srcs[t], _gather_part(lnds[t], t, me), send_sems.at[k * n + t], recv_sems.at[k * n + t],
                        (px, py, c)).start()
            pltpu.make_async_copy(srcs[t], _gather_part(lnds[t], t, me), local_sems.at[t]).start()
        token[...] = jnp.zeros_like(token)

    hbm = pl.BlockSpec(memory_space=pltpu.HBM)
    sem = pl.BlockSpec(memory_space=pltpu.SEMAPHORE)
    outs = pl.pallas_call(
        body, name="weights_gather_start", in_specs=[hbm] * (2 * n),
        out_shape=(pltpu.SemaphoreType.DMA((3 * n,)), pltpu.SemaphoreType.DMA((3 * n,)),
                   pltpu.SemaphoreType.DMA((n,)),
                   *[pltpu.HBM(s.shape, s.dtype) for s in shards], *[pltpu.HBM(a.shape, a.dtype) for a in lands],
                   jax.ShapeDtypeStruct((SUBLANES, LANES), F32)),
        out_specs=(sem, sem, sem, *[hbm] * (2 * n), pl.BlockSpec(memory_space=pltpu.VMEM)),
        input_output_aliases={i: 3 + i for i in range(2 * n)},
        compiler_params=pltpu.CompilerParams(has_side_effects=pltpu.SideEffectType.DATAFLOW_SIDE_EFFECTING),
    )(*[pltpu.with_memory_space_constraint(s, pltpu.HBM) for s in shards], *lands)
    return outs[0], outs[1], outs[2], outs[3:3 + n], outs[3 + n:3 + 2 * n], outs[-1]


def _gather_wait(handle, ts, after, *, name):
    send_sems, recv_sems, local_sems, src_thru, land_thru, _ = handle
    n = len(src_thru)
    m = len(ts)

    def body(*refs):
        srcs, lnds = refs[:m], refs[m:2 * m]
        ssem, rsem, lsem = refs[2 * m:2 * m + 3]
        x, y, c, peers = _xy_peers()
        me = 2 * x + y
        for i, t in enumerate(ts):
            for k, (px, py) in enumerate(peers):
                cp = _remote(srcs[i], _gather_part(lnds[i], t, 2 * px + py), ssem.at[k * n + t], rsem.at[k * n + t],
                             (px, py, c))
                cp.wait_send()
                cp.wait_recv()
            pltpu.make_async_copy(srcs[i], _gather_part(lnds[i], t, me), lsem.at[t]).wait()

    hbm = pl.BlockSpec(memory_space=pltpu.HBM)
    sem = pl.BlockSpec(memory_space=pltpu.SEMAPHORE)
    args = [src_thru[t] for t in ts] + [land_thru[t] for t in ts]
    outs = pl.pallas_call(
        body, name=name, in_specs=[hbm] * (2 * m) + [sem, sem, sem, pl.BlockSpec(memory_space=pl.ANY)],
        out_shape=[pltpu.HBM(a.shape, a.dtype) for a in args], out_specs=[hbm] * (2 * m),
        input_output_aliases={i: i for i in range(2 * m)},
        compiler_params=pltpu.CompilerParams(has_side_effects=pltpu.SideEffectType.DATAFLOW_SIDE_EFFECTING),
    )(*args, send_sems, recv_sems, local_sems, after)
    return outs[:m], outs[m:]


C_CHUNKS = 8
XY_CHUNKS = 8
EW_ROWS = 512


def _c_exchange(g, part):
    w = g.shape[2]
    row0, nrows = G_PARTS[part]
    half = nrows // 2
    rq = half // C_CHUNKS

    def body(g_ref, got_ref, send_sems, recv_sems):
        x, y, c = lax.axis_index("x"), lax.axis_index("y"), lax.axis_index("c")
        cps = []
        for s in range(4):
            for q in range(C_CHUNKS):
                k = s * C_CHUNKS + q
                cp = _remote(g_ref.at[s, pl.ds(row0 + (1 - c) * half + q * rq, rq), :],
                             got_ref.at[s, pl.ds(q * rq, rq), :], send_sems.at[k], recv_sems.at[k], (x, y, 1 - c))
                cp.start()
                cps.append(cp)
        for cp in cps:
            cp.wait_recv()
        for cp in cps:
            cp.wait_send()

    return pl.pallas_call(
        body, name="grad_c_exchange_%d" % part, in_specs=[_hbm()], out_specs=_hbm(),
        out_shape=jax.ShapeDtypeStruct((4, half, w), g.dtype),
        scratch_shapes=[pltpu.SemaphoreType.DMA((4 * C_CHUNKS,)), pltpu.SemaphoreType.DMA((4 * C_CHUNKS,))],
    )(g)


XFER_DTYPE = jnp.bfloat16


def _add_own_half(g, got, c_arr, part):
    w = g.shape[2]
    row0, nrows = G_PARTS[part]
    half = nrows // 2
    nb = half // EW_ROWS
    b0 = row0 // EW_ROWS

    def body(c_ref, a_ref, b_ref, o_ref, t_ref):
        sm = a_ref[...] + b_ref[...]
        o_ref[...] = sm.astype(o_ref.dtype)

        @pl.when(pl.program_id(1) == nb - 1)
        def _():
            t_ref[...] = sm[:, EW_ROWS - MISC_ROWS:, :]

    grid_spec = pltpu.PrefetchScalarGridSpec(
        num_scalar_prefetch=1, grid=(4, nb),
        in_specs=[pl.BlockSpec((1, EW_ROWS, w), lambda s, i, c: (s, b0 + c[0] * nb + i, 0)),
                  pl.BlockSpec((1, EW_ROWS, w), lambda s, i, c: (s, i, 0))],
        out_specs=[pl.BlockSpec((1, EW_ROWS, w), lambda s, i, c: (s, i, 0)),
                   pl.BlockSpec((1, MISC_ROWS, w), lambda s, i, c: (s, 0, 0))])
    return pl.pallas_call(
        body, name="grad_add_halves", grid_spec=grid_spec,
        out_shape=[jax.ShapeDtypeStruct((4, half, w), XFER_DTYPE), jax.ShapeDtypeStruct((4, MISC_ROWS, w), g.dtype)],
        compiler_params=_cparams(("arbitrary", "arbitrary")),
    )(c_arr, g, got)


def _xy_pieces(arrs):
    pieces = []
    for a, arr in enumerate(arrs):
        nch = XY_CHUNKS if a == 0 else 1
        rq = arr.shape[1] // nch
        pieces += [(a, pl.ds(q * rq, rq)) for q in range(nch)]
    return pieces


def _xy_start(arrs, *, name):
    na = len(arrs)
    pieces = _xy_pieces(arrs)
    npc = len(pieces)
    lands = [pltpu.with_memory_space_constraint(lax.empty(a.shape, a.dtype), pltpu.HBM) for a in arrs]

    def body(*refs):
        ins, outs = refs[:na], refs[na:2 * na]
        send_sems, recv_sems, local_sems = refs[2 * na:2 * na + 3]
        token = refs[-1]
        x, y, c, peers = _xy_peers()
        me = 2 * x + y
        for k, (px, py) in enumerate(peers):
            for j, (a, rows) in enumerate(pieces):
                _remote(ins[a].at[2 * px + py, rows, :], outs[a].at[me, rows, :], send_sems.at[k * npc + j],
                        recv_sems.at[k * npc + j], (px, py, c)).start()
        for j, (a, rows) in enumerate(pieces):
            pltpu.make_async_copy(ins[a].at[me, rows, :], outs[a].at[me, rows, :], local_sems.at[j]).start()
        token[...] = jnp.zeros_like(token)

    hbm = pl.BlockSpec(memory_space=pltpu.HBM)
    sem = pl.BlockSpec(memory_space=pltpu.SEMAPHORE)
    outs = pl.pallas_call(
        body, name=name, in_specs=[hbm] * (2 * na),
        out_shape=(pltpu.SemaphoreType.DMA((3 * npc,)), pltpu.SemaphoreType.DMA((3 * npc,)),
                   pltpu.SemaphoreType.DMA((npc,)),
                   *[pltpu.HBM(a.shape, a.dtype) for a in arrs], *[pltpu.HBM(a.shape, a.dtype) for a in arrs],
                   jax.ShapeDtypeStruct((SUBLANES, LANES), F32)),
        out_specs=(sem, sem, sem, *[hbm] * (2 * na), pl.BlockSpec(memory_space=pltpu.VMEM)),
        input_output_aliases={i: 3 + i for i in range(2 * na)},
        compiler_params=pltpu.CompilerParams(has_side_effects=pltpu.SideEffectType.DATAFLOW_SIDE_EFFECTING),
    )(*[pltpu.with_memory_space_constraint(a, pltpu.HBM) for a in arrs], *lands)
    return (outs[0], outs[1], outs[2], outs[3:3 + na], outs[3 + na:3 + 2 * na]), outs[-1]


def _xy_wait(handle, after, *, name):
    send_sems, recv_sems, local_sems, src_thru, land_thru = handle
    na = len(src_thru)
    pieces = _xy_pieces(src_thru)
    npc = len(pieces)

    def body(*refs):
        ins, outs = refs[:na], refs[na:2 * na]
        ssem, rsem, lsem = refs[2 * na:2 * na + 3]
        x, y, c, peers = _xy_peers()
        me = 2 * x + y
        for k, (px, py) in enumerate(peers):
            for j, (a, rows) in enumerate(pieces):
                cp = _remote(ins[a].at[me, rows, :], outs[a].at[2 * px + py, rows, :], ssem.at[k * npc + j],
                             rsem.at[k * npc + j], (px, py, c))
                cp.wait_send()
                cp.wait_recv()
        for j, (a, rows) in enumerate(pieces):
            pltpu.make_async_copy(ins[a].at[me, rows, :], outs[a].at[me, rows, :], lsem.at[j]).wait()

    hbm = pl.BlockSpec(memory_space=pltpu.HBM)
    sem = pl.BlockSpec(memory_space=pltpu.SEMAPHORE)
    args = list(src_thru) + list(land_thru)
    outs = pl.pallas_call(
        body, name=name, in_specs=[hbm] * (2 * na) + [sem, sem, sem, pl.BlockSpec(memory_space=pl.ANY)],
        out_shape=[pltpu.HBM(a.shape, a.dtype) for a in args], out_specs=[hbm] * (2 * na),
        input_output_aliases={i: i for i in range(2 * na)},
        compiler_params=pltpu.CompilerParams(has_side_effects=pltpu.SideEffectType.DATAFLOW_SIDE_EFFECTING),
    )(*args, send_sems, recv_sems, local_sems, after)
    return outs[na:]


def _sum4_into_half(r, rt, c_arr, part, fbuf):
    _, half, w = r.shape
    nb = half // EW_ROWS
    b0 = G_PARTS[part][0] // EW_ROWS

    def body(c_ref, r_ref, t_ref, *rest):
        o_ref = rest[-1]
        o_ref[...] = ((r_ref[0].astype(F32) + r_ref[1].astype(F32)) + r_ref[2].astype(F32)) + r_ref[3].astype(F32)

        @pl.when(pl.program_id(0) == nb - 1)
        def _():
            o_ref[EW_ROWS - MISC_ROWS:, :] = ((t_ref[0] + t_ref[1]) + t_ref[2]) + t_ref[3]

    in_specs = [pl.BlockSpec((4, EW_ROWS, w), lambda i, c: (0, i, 0)),
                pl.BlockSpec((4, MISC_ROWS, w), lambda i, c: (0, 0, 0))]
    args = [c_arr, r, rt]
    aliases = {}
    if fbuf is not None:
        in_specs.append(pl.BlockSpec(memory_space=pl.ANY))
        args.append(fbuf)
        aliases = {3: 0}
    grid_spec = pltpu.PrefetchScalarGridSpec(
        num_scalar_prefetch=1, grid=(nb,), in_specs=in_specs,
        out_specs=pl.BlockSpec((EW_ROWS, w), lambda i, c: (b0 + c[0] * nb + i, 0)))
    return pl.pallas_call(
        body, name="grad_sum4", grid_spec=grid_spec, out_shape=jax.ShapeDtypeStruct((G_ROWS, w), F32),
        input_output_aliases=aliases, compiler_params=_cparams(("arbitrary",)),
    )(*args)


C_GATHER_ROWS = 512


def _c_allgather_halves(f, parts):
    w = f.shape[1]
    chunks = []
    for part in parts:
        chunks += [(part, r) for r in range(0, G_PARTS[part][1] // 2, C_GATHER_ROWS)]
    nch = len(chunks)

    def body(f_ref, out_ref, send_sems, recv_sems):
        x, y, c = lax.axis_index("x"), lax.axis_index("y"), lax.axis_index("c")

        def rows(q, owner):
            part, r = chunks[q]
            row0, nrows = G_PARTS[part]
            return pl.ds(row0 + owner * (nrows // 2) + r, C_GATHER_ROWS)

        sends = []
        for q in range(nch):
            cp = _remote(f_ref.at[rows(q, c), :], out_ref.at[rows(q, c), :], send_sems.at[q], recv_sems.at[q],
                         (x, y, 1 - c))
            cp.start()
            sends.append(cp)
        for q in range(nch):
            _remote(f_ref.at[rows(q, 1 - c), :], out_ref.at[rows(q, 1 - c), :], send_sems.at[q], recv_sems.at[q],
                    (x, y, 1 - c)).wait_recv()
        for cp in sends:
            cp.wait_send()

    return pl.pallas_call(
        body, name="grad_c_allgather_" + "".join(str(p) for p in parts), in_specs=[_hbm()], out_specs=_hbm(),
        input_output_aliases={0: 0},
        out_shape=jax.ShapeDtypeStruct((G_ROWS, w), f.dtype),
        scratch_shapes=[pltpu.SemaphoreType.DMA((nch,)), pltpu.SemaphoreType.DMA((nch,))],
    )(f)


def _adamw(w, m, v, g, g_rows=None):
    shape = w.shape
    cols = shape[-1]
    rows = int(math.prod(shape)) // cols
    tr = 256 if rows % 256 == 0 else rows
    from_flat = g_rows is not None
    c1 = 1.0 / (1.0 - ADAM_B1 ** ADAM_STEP)
    c2 = 1.0 / (1.0 - ADAM_B2 ** ADAM_STEP)

    def body(w_ref, m_ref, v_ref, g_ref, *outs):
        gg = g_ref[...]
        nm = ADAM_B1 * m_ref[...] + (1.0 - ADAM_B1) * gg
        nv = ADAM_B2 * v_ref[...] + (1.0 - ADAM_B2) * (gg * gg)
        if from_flat:
            outs[0][...] = gg
        d_ref, nm_ref, nv_ref = outs[-3:]
        nm_ref[...] = nm
        nv_ref[...] = nv
        d_ref[...] = -ADAM_LR * ((nm * c1) / (jnp.sqrt(nv * c2) + ADAM_EPS) + ADAM_WD * w_ref[...])

    spec = pl.BlockSpec((tr, cols), lambda i: (i, 0))
    if from_flat:
        nbl = rows // DEPTH // tr
        assert cols == FLAT and all(r % tr == 0 for r in g_rows) and len(g_rows) == DEPTH == 2
        b0, b1 = g_rows[0] // tr, g_rows[1] // tr
        g_spec = pl.BlockSpec((tr, cols), lambda i: (jnp.where(i < nbl, b0 + i, b1 + i - nbl), 0))
        g_arg = g
    else:
        g_spec = spec
        g_arg = g.reshape(rows, cols)
    n_out = 4 if from_flat else 3
    sds = jax.ShapeDtypeStruct((rows, cols), F32)
    outs = pl.pallas_call(
        body, name="adamw", grid=(rows // tr,), in_specs=[spec, spec, spec, g_spec], out_specs=[spec] * n_out,
        out_shape=[sds] * n_out, compiler_params=_cparams(("arbitrary",)),
    )(w.reshape(rows, cols), m.reshape(rows, cols), v.reshape(rows, cols), g_arg)
    outs = [o.reshape(shape) for o in outs]
    return outs if from_flat else [g] + outs


SMALL_SHARDED = (("s5_glu_w", (2, 64, 256), 1), ("ssd_conv_w", (2, 4, 256), 2), ("rg_conv_w", (2, 4, 64), 2))
REPLICATED = (
    ("ssd_conv_b", (2, 1024)), ("ssd_dt_bias", (2, 8)), ("ssd_a_log", (2, 8)), ("ssd_d", (2, 8)),
    ("ssd_norm_w", (2, 512)), ("s5_lam_re", (2, 16, 64)), ("s5_lam_im", (2, 16, 64)), ("s5_log_step", (2, 16)),
    ("s5_b_re", (2, 16, 64, 16)), ("s5_b_im", (2, 16, 64, 16)), ("s5_c_re", (2, 16, 16, 64)),
    ("s5_c_im", (2, 16, 16, 64)), ("s5_d", (2, 256)), ("s5_glu_b", (2, 256)), ("rg_conv_b", (2, 256)),
    ("rg_wa", (2, 4, 64, 64)), ("rg_ba", (2, 4, 64)), ("rg_wx", (2, 4, 64, 64)), ("rg_bx", (2, 4, 64)),
    ("rg_lambda", (2, 256)), ("ln1_g", (2, 1024)), ("ln1_b", (2, 1024)), ("ln2_g", (2, 1024)), ("ln2_b", (2, 1024)),
    ("ln3_g", (2, 1024)), ("ln3_b", (2, 1024)),
)
WEIGHT_ORDER = (
    "w_in", "w_out", "ssd_conv_w", "ssd_conv_b", "ssd_dt_bias", "ssd_a_log", "ssd_d", "ssd_norm_w", "s5_lam_re",
    "s5_lam_im", "s5_log_step", "s5_b_re", "s5_b_im", "s5_c_re", "s5_c_im", "s5_d", "s5_glu_w", "s5_glu_b",
    "rg_conv_w", "rg_conv_b", "rg_wa", "rg_ba", "rg_wx", "rg_bx", "rg_lambda", "ln1_g", "ln1_b", "xa_wq", "xa_wk",
    "xa_wv", "xa_wo", "ln2_g", "ln2_b", "mlp_w1", "mlp_w2", "ln3_g", "ln3_b",
)


def _size(shape):
    return int(math.prod(shape))


def _round_up(a, b):
    return (a + b - 1) // b * b


SMALL_ELEMS = sum(_size(s) for _, s, _ in SMALL_SHARDED)
REP_ELEMS = sum(_size(s) for _, s in REPLICATED)
REP_QROWS = _round_up(-(-REP_ELEMS // (4 * FLAT)), 8)
assert SMALL_ELEMS <= MISC_REP_ROW * FLAT and MISC_REP_ROW + REP_QROWS <= MISC_ROWS


def _pack_shards(tensors, names_shapes):
    return jnp.concatenate([tensors[n].reshape(-1) for n, *_ in names_shapes])


def _unpack(flat, names_shapes):
    out, off = {}, 0
    for n, s, *_ in names_shapes:
        out[n] = flat[off:off + _size(s)].reshape(s)
        off += _size(s)
    return out


def _split_shards(full, names_shapes):
    rows = []
    for k in range(4):
        parts = []
        for n, s, ax in names_shapes:
            w = s[ax]
            parts.append(lax.slice_in_dim(full[n], k * w, (k + 1) * w, axis=ax).reshape(-1))
        rows.append(jnp.concatenate(parts))
    return jnp.stack(rows)


def _pack_cols(w):
    pad = jnp.zeros((w.shape[0], LANES - SSD_HEADS), w.dtype)
    return jnp.concatenate([w[:, O_XBC:O_XBC + 1024], w[:, O_Z:O_Z + 512], w[:, O_U:O_U + 256],
                            w[:, O_XRG:O_XRG + 256], w[:, O_GRG:O_GRG + 256], w[:, O_DT:O_DT + 8], pad], axis=1)


def _unpack_cols(w):
    return jnp.concatenate([w[:, P_Z:P_Z + 512], w[:, P_XBC:P_XBC + 1024], w[:, P_DT:P_DT + 8],
                            w[:, P_U:P_U + 256], w[:, P_XRG:P_XRG + 256], w[:, P_GRG:P_GRG + 256]], axis=1)


def _lanes(v, width):
    return jnp.pad(v, (0, width - v.shape[0])).reshape(1, width)


def _layer_params(rep, l):
    p = {}
    p["ssd_cb"] = rep["ssd_conv_b"][l].reshape(1, -1)
    p["ssd_dtb"] = _lanes(rep["ssd_dt_bias"][l], LANES)
    p["ssd_a"] = _lanes(-jnp.exp(rep["ssd_a_log"][l]), LANES)
    p["ssd_d"] = jnp.repeat(rep["ssd_d"][l], 64).reshape(1, -1)
    p["ssd_nw"] = rep["ssd_norm_w"][l].reshape(1, -1)
    s5_args = tuple(rep[n][l] for n in ("s5_lam_re", "s5_lam_im", "s5_log_step", "s5_b_re", "s5_b_im", "s5_c_re",
                                        "s5_c_im"))
    (lbr, lbi, bre, bim, cre, cim), p["s5_vjp"] = jax.vjp(_s5_prepare, *s5_args)
    p.update(s5_bre=bre, s5_bim=bim, s5_cre=cre, s5_cim=cim)
    p["s5_coef"] = _s5_scan_coef(lbr, lbi, False)
    p["s5_rcoef"] = _s5_scan_coef(lbr, lbi, True)
    p["s5_d"] = rep["s5_d"][l].reshape(1, -1)
    p["s5_gb"] = rep["s5_glu_b"][l].reshape(1, -1)
    p["rg_cb"] = rep["rg_conv_b"][l].reshape(1, -1)
    p["rg_wa"] = _block_diag(rep["rg_wa"][l])
    p["rg_wx"] = _block_diag(rep["rg_wx"][l])
    p["rg_ba"] = rep["rg_ba"][l].reshape(1, -1)
    p["rg_bx"] = rep["rg_bx"][l].reshape(1, -1)
    p["rg_nsp"] = (-RG_C * jax.nn.softplus(-rep["rg_lambda"][l])).reshape(1, -1)
    p["rg_dnsp"] = RG_C * jax.nn.sigmoid(-rep["rg_lambda"][l])
    for n in ("ln1_g", "ln1_b", "ln2_g", "ln2_b", "ln3_g", "ln3_b"):
        p[n] = rep[n][l].reshape(1, -1)
    return p


def _layer_fwd(h, mem, p, fetch):
    s = {"h0": h}
    p.update(fetch(0, h))
    proj = _mm(h, p["w_in"], name="in_proj")
    s["proj"] = proj
    y_ssd, s["ssd_yy"], s["ssd_states"] = _ssd_fwd(proj, p["ssd_cw"], p["ssd_cb"], p["ssd_dtb"], p["ssd_a"],
                                                     p["ssd_d"], p["ssd_nw"])
    y_s5, s["s5_y2"], s["s5_hre"], s["s5_him"] = _s5_fwd(proj, p["s5_bre"], p["s5_bim"], p["s5_cre"], p["s5_cim"],
                                                         p["s5_d"], p["s5_glu_w"], p["s5_gb"], p["s5_coef"])
    y_rg, s["rg_h"] = _rg_fwd(proj, p["rg_cw"], p["rg_cb"], p["rg_wa"], p["rg_ba"], p["rg_wx"], p["rg_bx"],
                              p["rg_nsp"])
    s["ys"] = [y_ssd, y_s5, y_rg]
    p.update(fetch(1, y_rg))
    h1, s["xh1"], s["rs1"] = _outproj_ln_fwd(s["ys"], h, p["w_out"], p["ln1_g"], p["ln1_b"])
    s["h1"] = h1
    p.update(fetch(2, h1))
    kb = _mm(mem, p["xa_wk"], name="mem_proj")
    vb = _mm(mem, p["xa_wv"], name="mem_proj")
    s["kb"], s["vb"] = kb, vb
    h2, s["xh2"], s["rs2"], s["attn_o"] = _attn_ln_fwd(h1, p["xa_wq"], p["xa_wo"], kb, vb, p["ln2_g"], p["ln2_b"])
    s["h2"] = h2
    p.update(fetch(3, h2))
    h3, s["xh3"], s["rs3"], s["mlp_hdn"] = _mlp_ln_fwd(h2, p["mlp_w1"], p["mlp_w2"], p["ln3_g"], p["ln3_b"])
    return h3, s


def _layer_bwd(dh3, mem, p, s, l, gbuf, after_mlp=None):
    g = {}
    dr3, du, dh2, g["ln3_g"], g["ln3_b"] = _mlp_ln_bwd(dh3, s["xh3"], s["rs3"], p["ln3_g"], s["mlp_hdn"],
                                                        p["mlp_w1"], p["mlp_w2"])
    gbuf = _wgrad_flat(s["h2"], du, gbuf, mode="colblk", row_off=_grad_row("mlp_w1", l), name="wgrad_mlp_w1")
    gbuf = _wgrad_flat(s["mlp_hdn"], dr3, gbuf, mode="rowblk", row_off=_grad_row("mlp_w2", l), name="wgrad_mlp_w2")
    ln2_g = p["ln2_g"] if after_mlp is None else p["ln2_g"] + after_mlp(gbuf)[0:1, 0:1]
    dr2, dq, dh1, dkb, dvb, g["ln2_g"], g["ln2_b"] = _attn_ln_bwd(dh2, s["xh2"], s["rs2"], ln2_g, s["h1"],
                                                                   p["xa_wq"], p["xa_wo"], s["kb"], s["vb"])
    for n, a_op, g_op in (("xa_wo", s["attn_o"], dr2), ("xa_wq", s["h1"], dq), ("xa_wk", mem, dkb),
                          ("xa_wv", mem, dvb)):
        gbuf = _wgrad_flat(a_op, g_op, gbuf, mode="rows4", row_off=_grad_row(n, l), name="wgrad_" + n)
    dr1, dres, dycat, g["ln1_g"], g["ln1_b"] = _outproj_ln_bwd(dh1, s["xh1"], s["rs1"], p["ln1_g"], p["w_out"])
    gbuf = _wgrad_flat(s["ys"], dr1, gbuf, mode="rows4", row_off=_grad_row("w_out", l), name="wgrad_w_out")
    proj = s["proj"]
    (dxbc, dz, ddt, dcw, dcb, ddtb, da_neg, dd_l, dnw) = _ssd_bwd(
        dycat, proj, s["ssd_yy"], s["ssd_states"], p["ssd_cw"], p["ssd_cb"], p["ssd_dtb"], p["ssd_a"], p["ssd_d"],
        p["ssd_nw"])
    g["ssd_conv_w"] = dcw[0:4]
    g["ssd_conv_b"] = dcb[0]
    g["ssd_dt_bias"] = ddtb[0, :SSD_HEADS]
    g["ssd_a_log"] = da_neg[0, :SSD_HEADS] * p["ssd_a"][0, :SSD_HEADS]
    g["ssd_d"] = dd_l.reshape(SSD_HEADS, 64).sum(axis=1)
    g["ssd_norm_w"] = dnw[0]
    (du_s5, dbre, dbim, dcre, dcim, dlam, dd5, dgw, dgb) = _s5_bwd(
        dycat, proj, s["s5_y2"], s["s5_hre"], s["s5_him"], p["s5_bre"], p["s5_bim"], p["s5_cre"], p["s5_cim"],
        p["s5_d"], p["s5_glu_w"], p["s5_gb"], p["s5_rcoef"])
    dl = dlam.sum(axis=1)
    s5g = p["s5_vjp"]((dl[0], dl[1], dbre, dbim, dcre, dcim))
    for n, v in zip(("s5_lam_re", "s5_lam_im", "s5_log_step", "s5_b_re", "s5_b_im", "s5_c_re", "s5_c_im"), s5g):
        g[n] = v
    g["s5_d"] = dd5[0]
    g["s5_glu_w"] = dgw
    g["s5_glu_b"] = dgb[0]
    (dxrg, dgrg, drcw, drcb, dwa, dba, dwx, dbx, dnsp) = _rg_bwd(
        dycat, proj, s["rg_h"], p["rg_cw"], p["rg_cb"], p["rg_wa"], p["rg_ba"], p["rg_wx"], p["rg_bx"], p["rg_nsp"])
    g["rg_conv_w"] = drcw[0:4]
    g["rg_conv_b"] = drcb[0]
    g["rg_wa"] = _block_diag_extract(dwa, RG_BLOCKS)
    g["rg_wx"] = _block_diag_extract(dwx, RG_BLOCKS)
    g["rg_ba"] = dba.reshape(RG_BLOCKS, RG_BLOCK_DIM)
    g["rg_bx"] = dbx.reshape(RG_BLOCKS, RG_BLOCK_DIM)
    g["rg_lambda"] = dnsp[0] * p["rg_dnsp"]
    dproj = [dxbc, dz, du_s5, dxrg, dgrg, ddt]
    g["w_in"] = _unpack_cols(_wgrad_in(s["h0"], dproj))
    dh0 = _in_proj_bwd(dproj, p["w_in"], dres)
    for n in ("ln1_g", "ln1_b", "ln2_g", "ln2_b", "ln3_g", "ln3_b"):
        g[n] = g[n][0]
    return dh0, g, gbuf


def _local_step(h, memf, target, rep, fetch):
    params, saved = [], []
    for l in range(DEPTH):
        p = _layer_params(rep, l)
        params.append(p)
        h, s = _layer_fwd(h, memf, p, functools.partial(fetch, l))
        saved.append(s)
    loss11, dh = _loss_fwd_bwd(h, target)
    grads = [None] * DEPTH
    gbuf = None
    c_arr = lax.axis_index("c").astype(jnp.int32).reshape(1)
    handles = {}

    def start_part(buf, part):
        handles[part], token = _xy_start(_chip_sums(buf, c_arr, part), name="grad_xy_start_%d" % part)
        return token

    for l in reversed(range(DEPTH)):
        hook = functools.partial(start_part, part=1) if l == 0 else None
        dh, grads[l], gbuf = _layer_bwd(dh, memf, params[l], saved[l], l, gbuf, hook)
        if l == DEPTH - 1:
            gbuf = lax.dynamic_update_slice(
                gbuf, _w_in_block(grads[l]["w_in"], jnp.zeros((4, MISC_ROWS, FLAT), F32)),
                (0, _grad_row("w_in", l), 0))
            params[0]["ln3_g"] = params[0]["ln3_g"] + start_part(gbuf, 0)[0:1, 0:1]
    gsmall = {n: jnp.stack([grads[l][n] for l in range(DEPTH)]) for n in grads[0] if n != "w_in"}
    return loss11, dh, gsmall, grads[0]["w_in"], gbuf, handles, c_arr


def _w_in_block(gw, tail):
    gw = jnp.pad(gw.reshape(D_MODEL, 4, W_IN_SHARD), ((0, 0), (0, 0), (0, W_IN_PAD - W_IN_SHARD)))
    return jnp.concatenate([jnp.transpose(gw, (1, 0, 2)).reshape(4, W_IN_PAD, FLAT), tail], axis=1)


def _chip_sums(gbuf, c_arr, part):
    return list(_add_own_half(gbuf, _c_exchange(gbuf, part), c_arr, part))


def kernel(x, mem, w_in, w_out, ssd_conv_w, ssd_conv_b, ssd_dt_bias, ssd_a_log, ssd_d, ssd_norm_w, s5_lam_re, s5_lam_im, s5_log_step, s5_b_re, s5_b_im, s5_c_re, s5_c_im, s5_d, s5_glu_w, s5_glu_b, rg_conv_w, rg_conv_b, rg_wa, rg_ba, rg_wx, rg_bx, rg_lambda, ln1_g, ln1_b, xa_wq, xa_wk, xa_wv, xa_wo, ln2_g, ln2_b, mlp_w1, mlp_w2, ln3_g, ln3_b, loss_target, m_w_in, m_w_out, m_ssd_conv_w, m_ssd_conv_b, m_ssd_dt_bias, m_ssd_a_log, m_ssd_d, m_ssd_norm_w, m_s5_lam_re, m_s5_lam_im, m_s5_log_step, m_s5_b_re, m_s5_b_im, m_s5_c_re, m_s5_c_im, m_s5_d, m_s5_glu_w, m_s5_glu_b, m_rg_conv_w, m_rg_conv_b, m_rg_wa, m_rg_ba, m_rg_wx, m_rg_bx, m_rg_lambda, m_ln1_g, m_ln1_b, m_xa_wq, m_xa_wk, m_xa_wv, m_xa_wo, m_ln2_g, m_ln2_b, m_mlp_w1, m_mlp_w2, m_ln3_g, m_ln3_b, v_w_in, v_w_out, v_ssd_conv_w, v_ssd_conv_b, v_ssd_dt_bias, v_ssd_a_log, v_ssd_d, v_ssd_norm_w, v_s5_lam_re, v_s5_lam_im, v_s5_log_step, v_s5_b_re, v_s5_b_im, v_s5_c_re, v_s5_c_im, v_s5_d, v_s5_glu_w, v_s5_glu_b, v_rg_conv_w, v_rg_conv_b, v_rg_wa, v_rg_ba, v_rg_wx, v_rg_bx, v_rg_lambda, v_ln1_g, v_ln1_b, v_xa_wq, v_xa_wk, v_xa_wv, v_xa_wo, v_ln2_g, v_ln2_b, v_mlp_w1, v_mlp_w2, v_ln3_g, v_ln3_b):
    args = dict(locals())
    weights = {n: args[n] for n in WEIGHT_ORDER}
    mom_m = {n: args["m_" + n] for n in WEIGHT_ORDER}
    mom_v = {n: args["v_" + n] for n in WEIGHT_ORDER}

    shards = []
    for l in range(DEPTH):
        for n, shp, ax in LAYER_GATHERED:
            w = weights[n][l]
            if w.shape[1] != shp[1]:
                w = jnp.pad(w, ((0, 0), (0, shp[1] - w.shape[1])))
            if n not in ("ssd_conv_w", "rg_conv_w"):
                w = w.astype(MXU_DTYPE)
            shards.append(w)
    handle = _gather_start(shards)

    def unpad(arr, padded, width):
        return jnp.concatenate([arr[:, padded * k:padded * k + width] for k in range(4)], axis=1)

    def fetch(l, grp, after):
        ts = [l * N_GATHERED + j for j in WAIT_GROUPS[grp]]
        _, landed = _gather_wait(handle, ts, after, name="weights_gather_wait_%d_%d" % (l, grp))
        out = {}
        for t, arr in zip(ts, landed):
            n = LAYER_GATHERED[t % N_GATHERED][0]
            if n == "w_in":
                arr = _pack_cols(unpad(arr, W_IN_PAD, W_IN_SHARD))
            elif n == "rg_conv_w":
                arr = unpad(arr, LANES, RG_CONV_SHARD)
            out[{"ssd_conv_w": "ssd_cw", "rg_conv_w": "rg_cw"}.get(n, n)] = arr
        return out

    rep = {n: weights[n] for n, _ in REPLICATED}

    loss11, dx, gsmall, gw_in0, gbuf, handles, c_arr = _local_step(x[0], mem[0], loss_target[0], rep, fetch)
    grad_x = dx[None]
    loss = lax.psum(loss11[0, 0], ("x", "y", "c"))

    small_q = _split_shards(gsmall, SMALL_SHARDED)
    rep_q = jnp.pad(_pack_shards(gsmall, REPLICATED), (0, 4 * REP_QROWS * FLAT - REP_ELEMS)).reshape(4, -1)
    misc = jnp.concatenate(
        [jnp.pad(small_q, ((0, 0), (0, MISC_REP_ROW * FLAT - SMALL_ELEMS))), rep_q,
         jnp.zeros((4, (MISC_ROWS - MISC_REP_ROW - REP_QROWS) * FLAT), F32)], axis=1).reshape(4, MISC_ROWS, FLAT)
    gbuf = lax.dynamic_update_slice(gbuf, _w_in_block(gw_in0, misc), (0, _grad_row("w_in", 0), 0))
    handles[2], token = _xy_start(_chip_sums(gbuf, c_arr, 2), name="grad_xy_start_2")
    fbuf = None
    for part in (0, 1):
        got = _xy_wait(handles[part], dx, name="grad_xy_wait_%d" % part)
        fbuf = _sum4_into_half(got[0], got[1] + token[0:1, 0:1], c_arr, part, fbuf)
    fbuf = _c_allgather_halves(fbuf, (0, 1))
    res = {n: _adamw(weights[n], mom_m[n], mom_v[n], fbuf, g_rows=[_grad_row(n, l) for l in range(DEPTH)])
           for n in ("mlp_w1", "mlp_w2")}
    got = _xy_wait(handles[2], res["mlp_w2"][1], name="grad_xy_wait_2")
    reduced = _c_allgather_halves(_sum4_into_half(got[0], got[1], c_arr, 2, fbuf), (2,))
    misc_red = reduced[ROW_MISC:]
    rep_all = _xy_allgather(misc_red[MISC_REP_ROW:MISC_REP_ROW + REP_QROWS], name="small_grads_allgather")
    g_red = {**_unpack(misc_red[:MISC_REP_ROW].reshape(-1), SMALL_SHARDED),
             **_unpack(rep_all.reshape(-1), REPLICATED)}
    g_red["w_in"] = jnp.stack([
        reduced[_grad_row("w_in", l):_grad_row("w_in", l) + W_IN_PAD].reshape(D_MODEL, W_IN_PAD)[:, :W_IN_SHARD]
        for l in range(DEPTH)])

    for n in WEIGHT_ORDER:
        if n in ("w_out", "xa_wq", "xa_wk", "xa_wv", "xa_wo"):
            res[n] = _adamw(weights[n], mom_m[n], mom_v[n], reduced, g_rows=[_grad_row(n, l) for l in range(DEPTH)])
        elif n not in res:
            res[n] = _adamw(weights[n], mom_m[n], mom_v[n], g_red[n])
    return (loss, grad_x, *[res[n][0] for n in WEIGHT_ORDER], *[res[n][1] for n in WEIGHT_ORDER],
            *[res[n][2] for n in WEIGHT_ORDER], *[res[n][3] for n in WEIGHT_ORDER])
```

```python
import functools
import math

import jax
import jax.numpy as jnp
from jax import lax
from jax.experimental import pallas as pl
from jax.experimental.pallas import tpu as pltpu

F32 = jnp.float32
MXU_DTYPE = jnp.bfloat16

D_MODEL = 1024
DEPTH = 2
MEM_LEN = 256
SSD_WIDTH = 512
SSD_HEADS = 8
SSD_STATE = 128
SSD_CHUNK = 128
SSD_XBC = 1024
S5_WIDTH = 256
S5_GROUPS = 16
S5_GROUP_CH = 16
S5_STATE = 64
S5_NSTATE = S5_GROUPS * S5_STATE
RG_WIDTH = 256
RG_BLOCKS = 4
RG_BLOCK_DIM = 64
RG_C = 8.0
XA_HEADS = 4
XA_HEAD_DIM = 256
D_FF = 4096
D_IN = 2312
ALPHA = (2.0 * DEPTH) ** 0.25
LN_EPS = 1e-5
ADAM_LR = 0.001
ADAM_B1 = 0.9
ADAM_B2 = 0.999
ADAM_EPS = 1e-08
ADAM_WD = 0.01
ADAM_STEP = 10

P_XBC, P_Z, P_U, P_XRG, P_GRG, P_DT = 0, 1024, 1536, 1792, 2048, 2304
D_PACK = 2432
O_Z, O_XBC, O_DT, O_U, O_XRG, O_GRG = 0, 512, 1536, 1544, 1800, 2056

LANES = 128
SUBLANES = 8
VMEM_LIMIT = 52 * 1024 * 1024
TM = 512
SSD_FWD_TM = 256
SSD_BWD_TM = 128
SCAN_TM = 512
FLAT = 1024

MESH = pl.DeviceIdType.MESH


def _cparams(sem):
    return pltpu.CompilerParams(dimension_semantics=sem, vmem_limit_bytes=VMEM_LIMIT)


def _dot(a, b):
    return jnp.dot(a.astype(MXU_DTYPE), b.astype(MXU_DTYPE), preferred_element_type=F32)


def _dot_nt(a, b):
    return lax.dot_general(a.astype(MXU_DTYPE), b.astype(MXU_DTYPE), (((1,), (1,)), ((), ())),
                           preferred_element_type=F32)


def _dot_tn(a, b):
    return lax.dot_general(a.astype(MXU_DTYPE), b.astype(MXU_DTYPE), (((0,), (0,)), ((), ())),
                           preferred_element_type=F32)


def _dot_f32(a, b):
    return jnp.dot(a, b, precision=lax.Precision.HIGHEST, preferred_element_type=F32)


def _dot_f32_tn(a, b):
    return lax.dot_general(a, b, (((0,), (0,)), ((), ())), precision=lax.Precision.HIGHEST,
                           preferred_element_type=F32)


def _sigmoid(x):
    return 1.0 / (1.0 + jnp.exp(-x))


def _softplus(x):
    return jnp.maximum(x, 0.0) + jnp.log(1.0 + jnp.exp(-jnp.abs(x)))


_GELU_K = math.sqrt(2.0 / math.pi)


def _gelu(x):
    return 0.5 * x * (1.0 + jnp.tanh(_GELU_K * (x + 0.044715 * x * x * x)))


def _gelu_grad(x):
    t = jnp.tanh(_GELU_K * (x + 0.044715 * x * x * x))
    return 0.5 * (1.0 + t) + 0.5 * x * (1.0 - t * t) * _GELU_K * (1.0 + 3.0 * 0.044715 * x * x)


def _expm1(x):
    small = x * (1.0 + x * (0.5 + x * (1.0 / 6.0 + x * (1.0 / 24.0))))
    return jnp.where(jnp.abs(x) < 0.05, small, jnp.exp(x) - 1.0)


def _sum0(x):
    return jnp.sum(x, axis=0, keepdims=True)


def _ln_fwd(r, g, b):
    mu = jnp.mean(r, axis=-1, keepdims=True)
    xc = r - mu
    var = jnp.mean(xc * xc, axis=-1, keepdims=True)
    rstd = lax.rsqrt(var + LN_EPS)
    xhat = xc * rstd
    return xhat * g + b, xhat, rstd


def _ln_bwd(dout, xhat, rstd, g):
    dxh = dout * g
    m1 = jnp.mean(dxh, axis=-1, keepdims=True)
    m2 = jnp.mean(dxh * xhat, axis=-1, keepdims=True)
    return rstd * (dxh - m1 - xhat * m2)


def _rows(tm, n, col=0):
    return pl.BlockSpec((tm, n), lambda i: (i, col))


def _const(shape):
    nd = len(shape)
    return pl.BlockSpec(shape, lambda i: (0,) * nd)


def _mm(a, w, *, name):
    t, k = a.shape
    n = w.shape[1]
    tm = min(TM, t)

    def body(a_ref, w_ref, o_ref):
        o_ref[...] = _dot(a_ref[...], w_ref[...])

    return pl.pallas_call(
        body, name=name, grid=(t // tm,), in_specs=[_rows(tm, k), _const(w.shape)], out_specs=_rows(tm, n),
        out_shape=jax.ShapeDtypeStruct((t, n), F32), compiler_params=_cparams(("arbitrary",)),
    )(a, w)


DPROJ_PIECES = ((P_XBC, 1024), (P_Z, 512), (P_U, 256), (P_XRG, 256), (P_GRG, 256), (P_DT, LANES))


def _in_proj_bwd(pieces, w, dres):
    t = dres.shape[0]
    npc = len(pieces)

    def body(*refs):
        w_ref, r_ref, o_ref = refs[npc:]
        acc = r_ref[...]
        for p_ref, (off, k) in zip(refs[:npc], DPROJ_PIECES):
            acc = acc + _dot_nt(p_ref[...], w_ref[:, off:off + k])
        o_ref[...] = acc

    return pl.pallas_call(
        body, name="in_proj_bwd", grid=(t // TM,),
        in_specs=[_rows(TM, k) for _, k in DPROJ_PIECES] + [_const(w.shape), _rows(TM, D_MODEL)],
        out_specs=_rows(TM, D_MODEL), out_shape=jax.ShapeDtypeStruct((t, D_MODEL), F32),
        compiler_params=_cparams(("arbitrary",)),
    )(*pieces, w, dres)


def _wgrad_in(h0, pieces):
    t = h0.shape[0]
    npc = len(pieces)

    def body(*refs):
        h_ref, o_ref = refs[npc], refs[npc + 1]
        @pl.when(pl.program_id(0) == 0)
        def _():
            o_ref[...] = jnp.zeros_like(o_ref)

        hb = h_ref[...].astype(MXU_DTYPE)
        for p_ref, (off, k) in zip(refs[:npc], DPROJ_PIECES):
            o_ref[:, off:off + k] += _dot_tn(hb, p_ref[...])

    return pl.pallas_call(
        body, name="wgrad_in", grid=(t // TM,),
        in_specs=[_rows(TM, k) for _, k in DPROJ_PIECES] + [_rows(TM, D_MODEL)],
        out_specs=_const((D_MODEL, D_PACK)), out_shape=jax.ShapeDtypeStruct((D_MODEL, D_PACK), F32),
        compiler_params=_cparams(("arbitrary",)),
    )(*pieces, h0)


G_ROWS = 8192
G_PARTS = ((0, 4096), (4096, 2048), (6144, 2048))
W_IN_SHARD = 578
W_IN_PAD = 640
MISC_ROWS = 128
MISC_REP_ROW = 40
ROW_MISC = G_ROWS - MISC_ROWS
W_IN_BLOCK_ROWS = W_IN_PAD + MISC_ROWS


def _grad_row(name, l):
    base = 0 if l == 1 else 4096
    mid = base + 2048 if l == 1 else 6144
    return {"mlp_w1": base, "mlp_w2": base + 1024, "w_out": mid, "xa_wq": mid + 256, "xa_wk": mid + 512,
            "xa_wv": mid + 768, "xa_wo": mid + 1024, "w_in": mid + 1280}[name]


def _wgrad_flat(a, g, buf, *, mode, row_off, name):
    pieces = list(a) if isinstance(a, (list, tuple)) else [a]
    t = g.shape[0]
    tt = min(1024, t)
    ns = t // tt
    blk = D_MODEL

    def accumulate(o_ref, parts, s):
        @pl.when(s == 0)
        def _():
            o_ref[...] = jnp.zeros_like(o_ref)

        for q, v in parts:
            o_ref[q] += v

    if mode == "rows4":
        grid = (ns,)
        in_specs = [pl.BlockSpec((tt, p.shape[1]), lambda s: (s, 0)) for p in pieces]
        in_specs.append(pl.BlockSpec((tt, blk), lambda s: (s, 0)))
        out_spec = pl.BlockSpec((4, 256, FLAT), lambda s: (0, row_off // 256, 0))
        sem = ("arbitrary",)
        npc = len(pieces)

        def body(*refs):
            g_v = refs[npc][...]
            parts, q0 = [], 0
            for p_ref in refs[:npc]:
                full = _dot_tn(p_ref[...], g_v)
                nq = full.shape[0] // 256
                parts += [(q0 + q, full[q * 256:(q + 1) * 256]) for q in range(nq)]
                q0 += nq
            accumulate(refs[-1], parts, pl.program_id(0))
    else:
        grid = (2, ns)
        if mode == "rowblk":
            in_specs = [pl.BlockSpec((tt, 2 * blk), lambda q, s: (s, q)), pl.BlockSpec((tt, blk), lambda q, s: (s, 0))]
        else:
            in_specs = [pl.BlockSpec((tt, blk), lambda q, s: (s, 0)), pl.BlockSpec((tt, 2 * blk), lambda q, s: (s, q))]
        out_spec = pl.BlockSpec((2, blk, FLAT), lambda q, s: (q, row_off // blk, 0))
        sem = ("arbitrary", "arbitrary")

        def body(a_ref, g_ref, *rest):
            full = _dot_tn(a_ref[...], g_ref[...])
            if mode == "rowblk":
                parts = [(0, full[:blk]), (1, full[blk:])]
            else:
                parts = [(0, full[:, :blk]), (1, full[:, blk:])]
            accumulate(rest[-1], parts, pl.program_id(1))

    args = pieces + [g]
    aliases = {}
    if buf is not None:
        in_specs.append(pl.BlockSpec(memory_space=pl.ANY))
        args.append(buf)
        aliases = {len(args) - 1: 0}
    return pl.pallas_call(
        body, name=name, grid=grid, in_specs=in_specs, out_specs=out_spec,
        out_shape=jax.ShapeDtypeStruct((4, G_ROWS, FLAT), F32), input_output_aliases=aliases,
        compiler_params=_cparams(sem),
    )(*args)


def _outproj_ln_fwd(ys, h, w, g, b):
    t = h.shape[0]
    npc = len(ys)

    def body(*refs):
        h_ref, w_ref, g_ref, b_ref, hn_ref, xh_ref, rs_ref = refs[npc:]
        r = ALPHA * h_ref[...]
        off = 0
        for y_ref in refs[:npc]:
            k = y_ref.shape[1]
            r = r + _dot(y_ref[...], w_ref[off:off + k, :])
            off += k
        out, xhat, rstd = _ln_fwd(r, g_ref[...], b_ref[...])
        hn_ref[...] = out
        xh_ref[...] = xhat
        rs_ref[...] = rstd

    return pl.pallas_call(
        body, name="outproj_ln_fwd", grid=(t // TM,),
        in_specs=[_rows(TM, y.shape[1]) for y in ys] + [_rows(TM, D_MODEL), _const((D_MODEL, D_MODEL)),
                                                        _const((1, D_MODEL)), _const((1, D_MODEL))],
        out_specs=[_rows(TM, D_MODEL), _rows(TM, D_MODEL), _rows(TM, 1)],
        out_shape=[jax.ShapeDtypeStruct((t, D_MODEL), F32), jax.ShapeDtypeStruct((t, D_MODEL), F32),
                   jax.ShapeDtypeStruct((t, 1), F32)],
        compiler_params=_cparams(("arbitrary",)),
    )(*ys, h, w, g, b)


def _attn_probs(q, kb, hh):
    sl = slice(hh * XA_HEAD_DIM, (hh + 1) * XA_HEAD_DIM)
    s = _dot_nt(q[:, sl], kb[:, sl]) * (1.0 / math.sqrt(XA_HEAD_DIM))
    m = jnp.max(s, axis=-1, keepdims=True)
    e = jnp.exp(s - m)
    return e / jnp.sum(e, axis=-1, keepdims=True)


def _attn_ln_fwd(h1, wq, wo, kb, vb, g, b):
    t = h1.shape[0]

    def body(h_ref, wq_ref, wo_ref, k_ref, v_ref, g_ref, b_ref, hn_ref, xh_ref, rs_ref, o_ref):
        h = h_ref[...]
        q = _dot(h, wq_ref[...])
        kb_ = k_ref[...]
        vb_ = v_ref[...]
        for hh in range(XA_HEADS):
            sl = slice(hh * XA_HEAD_DIM, (hh + 1) * XA_HEAD_DIM)
            p = _attn_probs(q, kb_, hh)
            o_ref[:, sl] = _dot(p, vb_[:, sl]).astype(o_ref.dtype)
        r = ALPHA * h + _dot(o_ref[...], wo_ref[...])
        out, xhat, rstd = _ln_fwd(r, g_ref[...], b_ref[...])
        hn_ref[...] = out
        xh_ref[...] = xhat
        rs_ref[...] = rstd

    return pl.pallas_call(
        body, name="attn_ln_fwd", grid=(t // TM,),
        in_specs=[_rows(TM, D_MODEL), _const((D_MODEL, D_MODEL)), _const((D_MODEL, D_MODEL)),
                  _const((MEM_LEN, D_MODEL)), _const((MEM_LEN, D_MODEL)), _const((1, D_MODEL)), _const((1, D_MODEL))],
        out_specs=[_rows(TM, D_MODEL), _rows(TM, D_MODEL), _rows(TM, 1), _rows(TM, D_MODEL)],
        out_shape=[jax.ShapeDtypeStruct((t, D_MODEL), F32), jax.ShapeDtypeStruct((t, D_MODEL), F32),
                   jax.ShapeDtypeStruct((t, 1), F32), jax.ShapeDtypeStruct((t, D_MODEL), MXU_DTYPE)],
        compiler_params=_cparams(("arbitrary",)),
    )(h1, wq, wo, kb, vb, g, b)


def _attn_ln_bwd(dh2, xhat, rstd, g, h1, wq, wo, kb, vb):
    t = h1.shape[0]

    def body(dh_ref, xh_ref, rs_ref, g_ref, h_ref, wq_ref, wo_ref, k_ref, v_ref,
             dr_ref, dq_ref, dh1_ref, dk_ref, dv_ref, dg_ref, db_ref):
        i = pl.program_id(0)

        @pl.when(i == 0)
        def _():
            dk_ref[...] = jnp.zeros_like(dk_ref)
            dv_ref[...] = jnp.zeros_like(dv_ref)
            dg_ref[...] = jnp.zeros_like(dg_ref)
            db_ref[...] = jnp.zeros_like(db_ref)

        dout = dh_ref[...]
        xh = xh_ref[...]
        dg_ref[...] += _sum0(dout * xh)
        db_ref[...] += _sum0(dout)
        dr = _ln_bwd(dout, xh, rs_ref[...], g_ref[...])
        dr_ref[...] = dr.astype(dr_ref.dtype)
        do = _dot_nt(dr, wo_ref[...])
        h = h_ref[...]
        q = _dot(h, wq_ref[...])
        kb_ = k_ref[...]
        vb_ = v_ref[...]
        scale = 1.0 / math.sqrt(XA_HEAD_DIM)
        for hh in range(XA_HEADS):
            sl = slice(hh * XA_HEAD_DIM, (hh + 1) * XA_HEAD_DIM)
            p = _attn_probs(q, kb_, hh)
            do_h = do[:, sl]
            dp = _dot_nt(do_h, vb_[:, sl])
            ds = p * (dp - jnp.sum(dp * p, axis=-1, keepdims=True)) * scale
            dq_ref[:, sl] = _dot(ds, kb_[:, sl]).astype(dq_ref.dtype)
            dk_ref[:, sl] += _dot_tn(ds, q[:, sl])
            dv_ref[:, sl] += _dot_tn(p, do_h)
        dh1_ref[...] = ALPHA * dr + _dot_nt(dq_ref[...], wq_ref[...])

    return pl.pallas_call(
        body, name="attn_ln_bwd", grid=(t // TM,),
        in_specs=[_rows(TM, D_MODEL), _rows(TM, D_MODEL), _rows(TM, 1), _const((1, D_MODEL)), _rows(TM, D_MODEL),
                  _const((D_MODEL, D_MODEL)), _const((D_MODEL, D_MODEL)), _const((MEM_LEN, D_MODEL)),
                  _const((MEM_LEN, D_MODEL))],
        out_specs=[_rows(TM, D_MODEL), _rows(TM, D_MODEL), _rows(TM, D_MODEL), _const((MEM_LEN, D_MODEL)),
                   _const((MEM_LEN, D_MODEL)), _const((1, D_MODEL)), _const((1, D_MODEL))],
        out_shape=[jax.ShapeDtypeStruct((t, D_MODEL), MXU_DTYPE), jax.ShapeDtypeStruct((t, D_MODEL), MXU_DTYPE),
                   jax.ShapeDtypeStruct((t, D_MODEL), F32), jax.ShapeDtypeStruct((MEM_LEN, D_MODEL), F32),
                   jax.ShapeDtypeStruct((MEM_LEN, D_MODEL), F32), jax.ShapeDtypeStruct((1, D_MODEL), F32),
                   jax.ShapeDtypeStruct((1, D_MODEL), F32)],
        compiler_params=_cparams(("arbitrary",)),
    )(dh2, xhat, rstd, g, h1, wq, wo, kb, vb)


FF_CHUNK = 1024
N_FF = D_FF // FF_CHUNK


def _load_resident(pairs, sems):
    copies = [pltpu.make_async_copy(src, dst, sems.at[k]) for k, (src, dst) in enumerate(pairs)]
    for cp in copies:
        cp.start()
    for cp in copies:
        cp.wait()


def _mlp_ln_fwd(h2, w1, w2, g, b):
    t = h2.shape[0]

    def body(h_ref, w1_hbm, w2_hbm, g_ref, b_ref, hn_ref, xh_ref, rs_ref, hd_ref, w1_v, w2_v, acc_ref, sems):
        @pl.when(pl.program_id(0) == 0)
        def _():
            _load_resident([(w1_hbm, w1_v), (w2_hbm, w2_v)], sems)

        h = h_ref[...]
        hb = h.astype(MXU_DTYPE)
        acc_ref[...] = ALPHA * h
        for j in range(N_FF):
            sl = slice(j * FF_CHUNK, (j + 1) * FF_CHUNK)
            u = _dot(hb, w1_v[:, sl])
            hd = jnp.square(jnp.maximum(u, 0.0)).astype(MXU_DTYPE)
            hd_ref[:, sl] = hd
            acc_ref[...] += _dot(hd, w2_v[sl, :])
        out, xhat, rstd = _ln_fwd(acc_ref[...], g_ref[...], b_ref[...])
        hn_ref[...] = out
        xh_ref[...] = xhat
        rs_ref[...] = rstd

    return pl.pallas_call(
        body, name="mlp_ln_fwd", grid=(t // TM,),
        in_specs=[_rows(TM, D_MODEL), _hbm(), _hbm(), _const((1, D_MODEL)), _const((1, D_MODEL))],
        out_specs=[_rows(TM, D_MODEL), _rows(TM, D_MODEL), _rows(TM, 1), _rows(TM, D_FF)],
        out_shape=[jax.ShapeDtypeStruct((t, D_MODEL), F32), jax.ShapeDtypeStruct((t, D_MODEL), F32),
                   jax.ShapeDtypeStruct((t, 1), F32), jax.ShapeDtypeStruct((t, D_FF), MXU_DTYPE)],
        scratch_shapes=[pltpu.VMEM((D_MODEL, D_FF), MXU_DTYPE), pltpu.VMEM((D_FF, D_MODEL), MXU_DTYPE),
                        pltpu.VMEM((TM, D_MODEL), F32), pltpu.SemaphoreType.DMA((2,))],
        compiler_params=_cparams(("arbitrary",)),
    )(h2, w1, w2, g, b)


def _mlp_ln_bwd(dh3, xhat, rstd, g, hdn, w1, w2):
    t = dh3.shape[0]

    def body(dh_ref, xh_ref, rs_ref, g_ref, hd_ref, w1_hbm, w2_hbm,
             dr_ref, du_ref, dh2_ref, dg_ref, db_ref, w1_v, w2_v, acc_ref, sems):
        @pl.when(pl.program_id(0) == 0)
        def _():
            _load_resident([(w1_hbm, w1_v), (w2_hbm, w2_v)], sems)
            dg_ref[...] = jnp.zeros_like(dg_ref)
            db_ref[...] = jnp.zeros_like(db_ref)

        dout = dh_ref[...]
        xh = xh_ref[...]
        dg_ref[...] += _sum0(dout * xh)
        db_ref[...] += _sum0(dout)
        dr = _ln_bwd(dout, xh, rs_ref[...], g_ref[...])
        drb = dr.astype(MXU_DTYPE)
        dr_ref[...] = drb
        acc_ref[...] = ALPHA * dr
        for j in range(N_FF):
            sl = slice(j * FF_CHUNK, (j + 1) * FF_CHUNK)
            dhd = _dot_nt(drb, w2_v[sl, :])
            du = (dhd * (2.0 * jnp.sqrt(hd_ref[:, sl].astype(F32)))).astype(MXU_DTYPE)
            du_ref[:, sl] = du
            acc_ref[...] += _dot_nt(du, w1_v[:, sl])
        dh2_ref[...] = acc_ref[...]

    tm = TM // 2
    return pl.pallas_call(
        body, name="mlp_ln_bwd", grid=(t // tm,),
        in_specs=[_rows(tm, D_MODEL), _rows(tm, D_MODEL), _rows(tm, 1), _const((1, D_MODEL)), _rows(tm, D_FF),
                  _hbm(), _hbm()],
        out_specs=[_rows(tm, D_MODEL), _rows(tm, D_FF), _rows(tm, D_MODEL), _const((1, D_MODEL)),
                   _const((1, D_MODEL))],
        out_shape=[jax.ShapeDtypeStruct((t, D_MODEL), MXU_DTYPE), jax.ShapeDtypeStruct((t, D_FF), MXU_DTYPE),
                   jax.ShapeDtypeStruct((t, D_MODEL), F32), jax.ShapeDtypeStruct((1, D_MODEL), F32),
                   jax.ShapeDtypeStruct((1, D_MODEL), F32)],
        scratch_shapes=[pltpu.VMEM((D_MODEL, D_FF), MXU_DTYPE), pltpu.VMEM((D_FF, D_MODEL), MXU_DTYPE),
                        pltpu.VMEM((tm, D_MODEL), F32), pltpu.SemaphoreType.DMA((2,))],
        compiler_params=_cparams(("arbitrary",)),
    )(dh3, xhat, rstd, g, hdn, w1, w2)


def _outproj_ln_bwd(dh1, xhat, rstd, g, w):
    t = dh1.shape[0]

    def body(dh_ref, xh_ref, rs_ref, g_ref, w_ref, dr_ref, res_ref, dy_ref, dg_ref, db_ref):
        i = pl.program_id(0)

        @pl.when(i == 0)
        def _():
            dg_ref[...] = jnp.zeros_like(dg_ref)
            db_ref[...] = jnp.zeros_like(db_ref)

        dout = dh_ref[...]
        xh = xh_ref[...]
        dg_ref[...] += _sum0(dout * xh)
        db_ref[...] += _sum0(dout)
        dr = _ln_bwd(dout, xh, rs_ref[...], g_ref[...])
        dr_ref[...] = dr.astype(dr_ref.dtype)
        res_ref[...] = ALPHA * dr
        dy_ref[...] = _dot_nt(dr, w_ref[...])

    return pl.pallas_call(
        body, name="outproj_ln_bwd", grid=(t // TM,),
        in_specs=[_rows(TM, D_MODEL), _rows(TM, D_MODEL), _rows(TM, 1), _const((1, D_MODEL)),
                  _const((D_MODEL, D_MODEL))],
        out_specs=[_rows(TM, D_MODEL), _rows(TM, D_MODEL), _rows(TM, D_MODEL), _const((1, D_MODEL)),
                   _const((1, D_MODEL))],
        out_shape=[jax.ShapeDtypeStruct((t, D_MODEL), MXU_DTYPE), jax.ShapeDtypeStruct((t, D_MODEL), F32),
                   jax.ShapeDtypeStruct((t, D_MODEL), F32), jax.ShapeDtypeStruct((1, D_MODEL), F32),
                   jax.ShapeDtypeStruct((1, D_MODEL), F32)],
        compiler_params=_cparams(("arbitrary",)),
    )(dh1, xhat, rstd, g, w)


def _loss_fwd_bwd(h, target):
    t = h.shape[0]

    def body(h_ref, t_ref, l_ref, dh_ref):
        i = pl.program_id(0)

        @pl.when(i == 0)
        def _():
            l_ref[...] = jnp.zeros_like(l_ref)

        e = h_ref[...] - t_ref[...]
        dh_ref[...] = e * (1.0 / D_MODEL)
        per_tok = jnp.mean(e * e, axis=-1, keepdims=True)
        l_ref[...] += 0.5 * jnp.sum(per_tok, axis=0, keepdims=True)

    return pl.pallas_call(
        body, name="loss_fwd_bwd", grid=(t // TM,),
        in_specs=[_rows(TM, D_MODEL), _rows(TM, D_MODEL)],
        out_specs=[_const((1, 1)), _rows(TM, D_MODEL)],
        out_shape=[jax.ShapeDtypeStruct((1, 1), F32), jax.ShapeDtypeStruct((t, D_MODEL), F32)],
        compiler_params=_cparams(("arbitrary",)),
    )(h, target)


def _pick_col(x, idx):
    lane = lax.broadcasted_iota(jnp.int32, x.shape, 1)
    return jnp.sum(jnp.where(lane == idx, x, 0.0), axis=1, keepdims=True)


def _pick_row(x, idx):
    sub = lax.broadcasted_iota(jnp.int32, x.shape, 0)
    return jnp.sum(jnp.where(sub == idx, x, 0.0), axis=0, keepdims=True)


def _conv_taps(pad_ref, w, tm, base):
    acc = w[0:1, :] * pad_ref[base:base + tm, :]
    for k in range(1, 4):
        acc = acc + w[k:k + 1, :] * pad_ref[base + k:base + k + tm, :]
    return acc


def _ssd_chunk_common(adt_c, tri):
    cs = _dot_f32(tri, adt_c)
    return cs, cs.T, jnp.exp(cs)


def _ssd_head_terms(cs, cst, ecs, dt_c, h, tri):
    cs_col = _pick_col(cs, h)
    cs_row = _pick_row(cst, h)
    dt_col = _pick_col(dt_c, h)
    cs_last = cs_col[SSD_CHUNK - 1:SSD_CHUNK, :]
    lmat = jnp.exp(jnp.where(tri > 0.0, cs_col - cs_row, -1e30))
    ecs_col = _pick_col(ecs, h)
    decay_col = jnp.exp(cs_last - cs_col)
    return cs_col, dt_col, cs_last, lmat, ecs_col, decay_col


def _ssd_fwd(proj, cw, cb, dtb, a_neg, d_lanes, nw):
    t = proj.shape[0]
    tm = SSD_FWD_TM
    nt = t // tm
    ncq = tm // SSD_CHUNK
    hb = tm // SUBLANES

    def body(xbc_ref, halo_ref, z_ref, dt_ref, cw_ref, cb_ref, dtb_ref, a_ref, d_ref, nw_ref,
             y_ref, yy_ref, st_ref, xpad, xact, state):
        i = pl.program_id(0)

        @pl.when(i == 0)
        def _():
            state[...] = jnp.zeros_like(state)

        xpad[0:SUBLANES, :] = jnp.where(i > 0, halo_ref[...], 0.0)
        xpad[SUBLANES:SUBLANES + tm, :] = xbc_ref[...]
        acc = cb_ref[...] + _conv_taps(xpad, cw_ref[...], tm, SUBLANES - 3)
        xact[...] = acc * _sigmoid(acc)
        dt = _softplus(dt_ref[...] + dtb_ref[...])
        adt = dt * a_ref[...]
        r_i = lax.broadcasted_iota(jnp.int32, (SSD_CHUNK, SSD_CHUNK), 0)
        c_i = lax.broadcasted_iota(jnp.int32, (SSD_CHUNK, SSD_CHUNK), 1)
        tri = (r_i >= c_i).astype(F32)
        lane1 = lax.broadcasted_iota(jnp.int32, (1, LANES), 1)
        for c in range(ncq):
            sl = slice(c * SSD_CHUNK, (c + 1) * SSD_CHUNK)
            dt_c = dt[sl]
            cs, cst, ecs = _ssd_chunk_common(adt[sl], tri)
            for g in range(2):
                bg = xact[sl, 512 + g * 128:512 + (g + 1) * 128]
                cg = xact[sl, 768 + g * 128:768 + (g + 1) * 128]
                cbm = _dot_nt(cg, bg)
                for pr in range(2):
                    pi = g * 2 + pr
                    psl = slice(pi * 128, (pi + 1) * 128)
                    xp = xact[sl, psl]
                    prev = state[pi]
                    st_ref[c, pi] = prev
                    yp = xp * d_ref[:, psl]
                    new_s = jnp.zeros((SSD_STATE, LANES), F32)
                    dec_lane = jnp.zeros((1, LANES), F32)
                    for hh in range(2):
                        h = g * 4 + pr * 2 + hh
                        lm = (lane1 >= 64) if hh else (lane1 < 64)
                        _, dt_col, cs_last, lmat, ecs_col, decay_col = _ssd_head_terms(cs, cst, ecs, dt_c, h, tri)
                        xdt = jnp.where(lm, xp, 0.0) * dt_col
                        yp = yp + _dot(cbm * lmat, xdt)
                        yp = yp + _dot(cg * ecs_col, jnp.where(lm, prev, 0.0))
                        new_s = new_s + _dot_tn(bg * decay_col, xdt)
                        dec_lane = dec_lane + jnp.where(lm, jnp.exp(cs_last), 0.0)
                    state[pi] = prev * dec_lane + new_s
                    yy_ref[sl, psl] = yp
        yy = yy_ref[...]
        z = z_ref[...]
        yg = yy * (z * _sigmoid(z))
        ms = jnp.mean(yg * yg, axis=-1, keepdims=True)
        y_ref[...] = (yg * lax.rsqrt(ms + LN_EPS) * nw_ref[...]).astype(y_ref.dtype)

    halo_map = lambda i: (jnp.maximum(i * hb - 1, 0), 0)
    return pl.pallas_call(
        body, name="ssd_fwd", grid=(nt,),
        in_specs=[pl.BlockSpec((tm, SSD_XBC), lambda i: (i, 0)), pl.BlockSpec((SUBLANES, SSD_XBC), halo_map),
                  pl.BlockSpec((tm, SSD_WIDTH), lambda i: (i, P_Z // SSD_WIDTH)),
                  pl.BlockSpec((tm, LANES), lambda i: (i, P_DT // LANES)),
                  _const((4, SSD_XBC)), _const((1, SSD_XBC)), _const((1, LANES)), _const((1, LANES)),
                  _const((1, SSD_WIDTH)), _const((1, SSD_WIDTH))],
        out_specs=[_rows(tm, SSD_WIDTH), _rows(tm, SSD_WIDTH),
                   pl.BlockSpec((ncq, 4, SSD_STATE, LANES), lambda i: (i, 0, 0, 0))],
        out_shape=[jax.ShapeDtypeStruct((t, SSD_WIDTH), MXU_DTYPE), jax.ShapeDtypeStruct((t, SSD_WIDTH), F32),
                   jax.ShapeDtypeStruct((t // SSD_CHUNK, 4, SSD_STATE, LANES), F32)],
        scratch_shapes=[pltpu.VMEM((tm + SUBLANES, SSD_XBC), F32), pltpu.VMEM((tm, SSD_XBC), F32),
                        pltpu.VMEM((4, SSD_STATE, LANES), F32)],
        compiler_params=_cparams(("arbitrary",)),
    )(proj, proj, proj, proj, cw, cb, dtb, a_neg, d_lanes, nw)


def _ssd_bwd(dycat, proj, yy, states, cw, cb, dtb, a_neg, d_lanes, nw):
    t = proj.shape[0]
    tm = SSD_BWD_TM
    nt = t // tm
    ncq = tm // SSD_CHUNK
    hb = tm // SUBLANES

    def body(dy_ref, xbc_ref, halo_ref, z_ref, dt_ref, yy_ref, st_ref, cw_ref, cb_ref, dtb_ref, a_ref, d_ref, nw_ref,
             dxbc_ref, dz_ref, ddt_ref, dcw_ref, dcb_ref, ddtb_ref, da_ref, dd_ref, dnw_ref,
             xpad, xact, dxact, dpad, dstate, dnext):
        i = pl.program_id(0)

        @pl.when(i == 0)
        def _():
            for r in (dcw_ref, dcb_ref, ddtb_ref, da_ref, dd_ref, dnw_ref, dstate, dnext):
                r[...] = jnp.zeros_like(r)

        xpad[0:SUBLANES, :] = jnp.where(i < nt - 1, halo_ref[...], 0.0)
        xpad[SUBLANES:SUBLANES + tm, :] = xbc_ref[...]
        cw_v = cw_ref[...]
        acc = cb_ref[...] + _conv_taps(xpad, cw_v, tm, SUBLANES - 3)
        sig = _sigmoid(acc)
        xact[...] = acc * sig
        dt_raw = dt_ref[...] + dtb_ref[...]
        dt = _softplus(dt_raw)
        a_v = a_ref[...]
        adt = dt * a_v
        yy = yy_ref[...]
        z = z_ref[...]
        sz = _sigmoid(z)
        siluz = z * sz
        yg = yy * siluz
        ms = jnp.mean(yg * yg, axis=-1, keepdims=True)
        rinv = lax.rsqrt(ms + LN_EPS)
        dout = dy_ref[...]
        dnw_ref[...] += _sum0(dout * yg * rinv)
        dyn = dout * nw_ref[...]
        dyg = rinv * dyn - yg * (rinv * rinv * rinv) * jnp.mean(dyn * yg, axis=-1, keepdims=True)
        dyy = dyg * siluz
        dz_ref[...] = (dyg * yy * (sz * (1.0 + z * (1.0 - sz)))).astype(dz_ref.dtype)
        dd_ref[...] += _sum0(dyy * xact[:, 0:SSD_WIDTH])

        r_i = lax.broadcasted_iota(jnp.int32, (SSD_CHUNK, SSD_CHUNK), 0)
        c_i = lax.broadcasted_iota(jnp.int32, (SSD_CHUNK, SSD_CHUNK), 1)
        tri = (r_i >= c_i).astype(F32)
        lane1 = lax.broadcasted_iota(jnp.int32, (1, LANES), 1)
        for c in reversed(range(ncq)):
            sl = slice(c * SSD_CHUNK, (c + 1) * SSD_CHUNK)
            dt_c = dt[sl]
            cs, cst, ecs = _ssd_chunk_common(adt[sl], tri)
            cacc = jnp.zeros((SSD_CHUNK, LANES), F32)
            racc = jnp.zeros((SSD_CHUNK, LANES), F32)
            ddtx = jnp.zeros((SSD_CHUNK, LANES), F32)
            for g in range(2):
                bg = xact[sl, 512 + g * 128:512 + (g + 1) * 128]
                cg = xact[sl, 768 + g * 128:768 + (g + 1) * 128]
                cbm = _dot_nt(cg, bg)
                dcb_m = jnp.zeros((SSD_CHUNK, SSD_CHUNK), F32)
                dbg = jnp.zeros((SSD_CHUNK, SSD_STATE), F32)
                dcg = jnp.zeros((SSD_CHUNK, SSD_STATE), F32)
                for pr in range(2):
                    pi = g * 2 + pr
                    psl = slice(pi * 128, (pi + 1) * 128)
                    xp = xact[sl, psl]
                    dyp = dyy[sl, psl]
                    prev = st_ref[c, pi]
                    ds_all = dstate[pi]
                    dxdt_p = jnp.zeros((SSD_CHUNK, LANES), F32)
                    dprev_new = jnp.zeros((SSD_STATE, LANES), F32)
                    dec_lane = jnp.zeros((1, LANES), F32)
                    dt_lanes = jnp.zeros((SSD_CHUNK, LANES), F32)
                    for hh in range(2):
                        h = g * 4 + pr * 2 + hh
                        lm = (lane1 >= 64) if hh else (lane1 < 64)
                        oh_l = (c_i == h).astype(F32)
                        oh_s = (r_i == h).astype(F32)
                        _, dt_col, cs_last, lmat, ecs_col, decay_col = _ssd_head_terms(cs, cst, ecs, dt_c, h, tri)
                        gm = cbm * lmat
                        xm = jnp.where(lm, xp, 0.0)
                        xdt = xm * dt_col
                        dym = jnp.where(lm, dyp, 0.0)
                        prevm = jnp.where(lm, prev, 0.0)
                        dsm = jnp.where(lm, ds_all, 0.0)
                        bdec = bg * decay_col
                        dxdt = _dot_tn(gm, dym) + _dot(bdec, dsm)
                        dxdt_p = dxdt_p + dxdt
                        ddtx = ddtx + oh_l * jnp.sum(dxdt * xm, axis=1, keepdims=True)
                        dt_lanes = dt_lanes + jnp.where(lm, dt_col, 0.0)
                        dgm = _dot_nt(dym, xdt)
                        dcb_m = dcb_m + dgm * lmat
                        w = dgm * gm
                        cacc = cacc + oh_l * jnp.sum(w, axis=1, keepdims=True)
                        racc = racc - oh_s * jnp.sum(w, axis=0, keepdims=True)
                        dce = _dot_nt(dym, prevm)
                        dcg = dcg + dce * ecs_col
                        cacc = cacc + oh_l * (jnp.sum(dce * cg, axis=1, keepdims=True) * ecs_col)
                        dprev_new = dprev_new + _dot_tn(cg * ecs_col, dym)
                        dbdec = _dot_nt(xdt, dsm)
                        dbg = dbg + dbdec * decay_col
                        dd = jnp.sum(dbdec * bg, axis=1, keepdims=True) * decay_col
                        cacc = cacc - oh_l * dd
                        cd = jnp.exp(cs_last)
                        dlast = jnp.sum(dd, axis=0, keepdims=True) + jnp.sum(
                            jnp.sum(dsm * prevm, axis=1, keepdims=True), axis=0, keepdims=True) * cd
                        cacc = cacc + jnp.where((r_i == SSD_CHUNK - 1) & (c_i == h), dlast, 0.0)
                        dec_lane = dec_lane + jnp.where(lm, cd, 0.0)
                    dstate[pi] = ds_all * dec_lane + dprev_new
                    dxact[sl, psl] = dxdt_p * dt_lanes + dyp * d_ref[:, psl]
                dcg = dcg + _dot(dcb_m, bg)
                dbg = dbg + _dot_tn(dcb_m, cg)
                dxact[sl, 512 + g * 128:512 + (g + 1) * 128] = dbg
                dxact[sl, 768 + g * 128:768 + (g + 1) * 128] = dcg
            dcs = cacc + racc.T
            dadt = _dot_f32((r_i <= c_i).astype(F32), dcs)
            ddt = dadt * a_v + ddtx
            da_ref[...] += _sum0(dadt * dt_c)
            ddt_raw = ddt * _sigmoid(dt_raw[sl])
            ddt_ref[sl, :] = ddt_raw.astype(ddt_ref.dtype)
            ddtb_ref[...] += _sum0(ddt_raw)
        dacc = dxact[...] * (sig * (1.0 + acc * (1.0 - sig)))
        dcb_ref[...] += _sum0(dacc)
        for k in range(4):
            dcw_ref[k:k + 1, :] += _sum0(dacc * xpad[SUBLANES - 3 + k:SUBLANES - 3 + k + tm, :])
        dpad[0:tm, :] = dacc
        dpad[tm:tm + SUBLANES, :] = dnext[...]
        dx = cw_v[0:1, :] * dpad[3:3 + tm, :]
        for k in range(1, 4):
            dx = dx + cw_v[k:k + 1, :] * dpad[3 - k:3 - k + tm, :]
        dxbc_ref[...] = dx.astype(dxbc_ref.dtype)
        dnext[...] = dacc[0:SUBLANES, :]

    rev = lambda i: nt - 1 - i
    halo_map = lambda i: (jnp.maximum(rev(i) * hb - 1, 0), 0)
    rrow = lambda n, col=0: pl.BlockSpec((tm, n), lambda i: (rev(i), col))
    return pl.pallas_call(
        body, name="ssd_bwd", grid=(nt,),
        in_specs=[rrow(SSD_WIDTH), rrow(SSD_XBC), pl.BlockSpec((SUBLANES, SSD_XBC), halo_map),
                  rrow(SSD_WIDTH, P_Z // SSD_WIDTH), rrow(LANES, P_DT // LANES), rrow(SSD_WIDTH),
                  pl.BlockSpec((ncq, 4, SSD_STATE, LANES), lambda i: (rev(i), 0, 0, 0)),
                  _const((4, SSD_XBC)), _const((1, SSD_XBC)), _const((1, LANES)), _const((1, LANES)),
                  _const((1, SSD_WIDTH)), _const((1, SSD_WIDTH))],
        out_specs=[rrow(SSD_XBC), rrow(SSD_WIDTH), rrow(LANES), _const((SUBLANES, SSD_XBC)), _const((1, SSD_XBC)),
                   _const((1, LANES)), _const((1, LANES)), _const((1, SSD_WIDTH)), _const((1, SSD_WIDTH))],
        out_shape=[jax.ShapeDtypeStruct((t, SSD_XBC), MXU_DTYPE), jax.ShapeDtypeStruct((t, SSD_WIDTH), MXU_DTYPE),
                   jax.ShapeDtypeStruct((t, LANES), MXU_DTYPE), jax.ShapeDtypeStruct((SUBLANES, SSD_XBC), F32),
                   jax.ShapeDtypeStruct((1, SSD_XBC), F32), jax.ShapeDtypeStruct((1, LANES), F32),
                   jax.ShapeDtypeStruct((1, LANES), F32), jax.ShapeDtypeStruct((1, SSD_WIDTH), F32),
                   jax.ShapeDtypeStruct((1, SSD_WIDTH), F32)],
        scratch_shapes=[pltpu.VMEM((tm + SUBLANES, SSD_XBC), F32), pltpu.VMEM((tm, SSD_XBC), F32),
                        pltpu.VMEM((tm, SSD_XBC), F32), pltpu.VMEM((tm + SUBLANES, SSD_XBC), F32),
                        pltpu.VMEM((4, SSD_STATE, LANES), F32), pltpu.VMEM((SUBLANES, SSD_XBC), F32)],
        compiler_params=_cparams(("arbitrary",)),
    )(dycat, proj, proj, proj, proj, yy, states, cw, cb, dtb, a_neg, d_lanes, nw)


def _cmul_add(ar, ai, br, bi, cr, ci):
    return ar + br * cr - bi * ci, ai + br * ci + bi * cr


def _s5_fwd(proj, bre, bim, cre, cim, d_skip, glu_w, glu_b, coef):
    t = proj.shape[0]
    tm = SCAN_TM
    ng = tm // SUBLANES

    def body(u_ref, bre_ref, bim_ref, cre_ref, cim_ref, d_ref, w_ref, b_ref, coef_ref,
             y_ref, y2_ref, hre_ref, him_ref, carry):
        i = pl.program_id(0)

        @pl.when(i == 0)
        def _():
            carry[...] = jnp.zeros_like(carry)

        u = u_ref[...]
        hre_ref[...] = _dot(u, bre_ref[...])
        him_ref[...] = _dot(u, bim_ref[...])

        def step(gi, car):
            cr_, ci_ = car
            rows = pl.ds(pl.multiple_of(gi * SUBLANES, SUBLANES), SUBLANES)
            r = hre_ref[rows, :]
            m = him_ref[rows, :]
            for k, sh in enumerate((1, 2, 4)):
                r, m = _cmul_add(r, m, coef_ref[k, 0], coef_ref[k, 1], pltpu.roll(r, sh, 0), pltpu.roll(m, sh, 0))
            r, m = _cmul_add(r, m, coef_ref[3, 0], coef_ref[3, 1], cr_, ci_)
            hre_ref[rows, :] = r
            him_ref[rows, :] = m
            return (jnp.broadcast_to(r[SUBLANES - 1:SUBLANES, :], r.shape),
                    jnp.broadcast_to(m[SUBLANES - 1:SUBLANES, :], m.shape))

        cr_, ci_ = lax.fori_loop(0, ng, step, (carry[0], carry[1]))
        carry[0] = cr_
        carry[1] = ci_
        y2 = _dot(hre_ref[...], cre_ref[...]) - _dot(him_ref[...], cim_ref[...]) + d_ref[...] * u
        y2_ref[...] = y2
        ya = _gelu(y2)
        y_ref[...] = (ya * _sigmoid(_dot(ya, w_ref[...]) + b_ref[...])).astype(y_ref.dtype)

    return pl.pallas_call(
        body, name="s5_fwd", grid=(t // tm,),
        in_specs=[pl.BlockSpec((tm, S5_WIDTH), lambda i: (i, P_U // S5_WIDTH)),
                  _const((S5_WIDTH, S5_NSTATE)), _const((S5_WIDTH, S5_NSTATE)), _const((S5_NSTATE, S5_WIDTH)),
                  _const((S5_NSTATE, S5_WIDTH)), _const((1, S5_WIDTH)), _const((S5_WIDTH, S5_WIDTH)),
                  _const((1, S5_WIDTH)), _const((5, 2, SUBLANES, S5_NSTATE))],
        out_specs=[_rows(tm, S5_WIDTH), _rows(tm, S5_WIDTH), _rows(tm, S5_NSTATE), _rows(tm, S5_NSTATE)],
        out_shape=[jax.ShapeDtypeStruct((t, S5_WIDTH), MXU_DTYPE), jax.ShapeDtypeStruct((t, S5_WIDTH), F32),
                   jax.ShapeDtypeStruct((t, S5_NSTATE), F32), jax.ShapeDtypeStruct((t, S5_NSTATE), F32)],
        scratch_shapes=[pltpu.VMEM((2, SUBLANES, S5_NSTATE), F32)],
        compiler_params=_cparams(("arbitrary",)),
    )(proj, bre, bim, cre, cim, d_skip, glu_w, glu_b, coef)


def _s5_bwd(dycat, proj, y2, hre, him, bre, bim, cre, cim, d_skip, glu_w, glu_b, rcoef):
    t = proj.shape[0]
    tm = SCAN_TM
    nt = t // tm
    ng = tm // SUBLANES
    hb = tm // SUBLANES

    def body(dy_ref, u_ref, y2_ref, hre_ref, him_ref, hre_halo, him_halo, bre_ref, bim_ref, cre_ref, cim_ref, d_ref,
             w_ref, b_ref, coef_ref,
             du_ref, dbre_ref, dbim_ref, dcre_ref, dcim_ref, dlam_ref, dd_ref, dw_ref, dgb_ref,
             gre, gim, hpre, hpim, carry):
        i = pl.program_id(0)

        @pl.when(i == 0)
        def _():
            for r in (dbre_ref, dbim_ref, dcre_ref, dcim_ref, dlam_ref, dd_ref, dw_ref, dgb_ref, carry):
                r[...] = jnp.zeros_like(r)

        u = u_ref[...]
        y2 = y2_ref[...]
        dout = dy_ref[...]
        ya = _gelu(y2)
        sg = _sigmoid(_dot(ya, w_ref[...]) + b_ref[...])
        dv = dout * ya * sg * (1.0 - sg)
        dya = dout * sg + _dot_nt(dv, w_ref[...])
        dw_ref[...] += _dot_tn(ya, dv)
        dgb_ref[...] += _sum0(dv)
        dy2 = dya * _gelu_grad(y2)
        dd_ref[...] += _sum0(dy2 * u)
        hre_v = hre_ref[...]
        him_v = him_ref[...]
        dcre_ref[...] += _dot_tn(hre_v, dy2)
        dcim_ref[...] -= _dot_tn(him_v, dy2)
        gre[...] = _dot_nt(dy2, cre_ref[...])
        gim[...] = -_dot_nt(dy2, cim_ref[...])
        first = i == nt - 1
        hpre[0:SUBLANES, :] = jnp.where(first, 0.0, hre_halo[...])
        hpim[0:SUBLANES, :] = jnp.where(first, 0.0, him_halo[...])
        hpre[SUBLANES:SUBLANES + tm, :] = hre_v
        hpim[SUBLANES:SUBLANES + tm, :] = him_v
        row0 = lax.broadcasted_iota(jnp.int32, (SUBLANES, S5_NSTATE), 0) == 0

        def step(k, car):
            cr_, ci_, dlr, dli = car
            gi = ng - 1 - k
            rows = pl.ds(pl.multiple_of(gi * SUBLANES, SUBLANES), SUBLANES)
            nrows = pl.ds(pl.multiple_of(gi * SUBLANES + SUBLANES, SUBLANES), SUBLANES)
            r = gre[rows, :]
            m = gim[rows, :]
            for kk, sh in enumerate((1, 2, 4)):
                r, m = _cmul_add(r, m, coef_ref[kk, 0], coef_ref[kk, 1], pltpu.roll(r, SUBLANES - sh, 0),
                                 pltpu.roll(m, SUBLANES - sh, 0))
            r, m = _cmul_add(r, m, coef_ref[3, 0], coef_ref[3, 1], cr_, ci_)
            gre[rows, :] = r
            gim[rows, :] = m
            pr_ = hpre[rows, :]
            pm_ = hpim[rows, :]
            hr_ = jnp.where(row0, jnp.broadcast_to(pr_[SUBLANES - 1:SUBLANES, :], pr_.shape),
                            pltpu.roll(hpre[nrows, :], 1, 0))
            hm_ = jnp.where(row0, jnp.broadcast_to(pm_[SUBLANES - 1:SUBLANES, :], pm_.shape),
                            pltpu.roll(hpim[nrows, :], 1, 0))
            dlr = dlr + hr_ * r + hm_ * m
            dli = dli + hr_ * m - hm_ * r
            return (jnp.broadcast_to(r[0:1, :], r.shape), jnp.broadcast_to(m[0:1, :], m.shape), dlr, dli)

        z8 = jnp.zeros((SUBLANES, S5_NSTATE), F32)
        cr_, ci_, dlr, dli = lax.fori_loop(0, ng, step, (carry[0], carry[1], z8, z8))
        carry[0] = cr_
        carry[1] = ci_
        dlam_ref[0] += dlr
        dlam_ref[1] += dli
        g_re = gre[...]
        g_im = gim[...]
        du_ref[...] = (dy2 * d_ref[...] + _dot_nt(g_re, bre_ref[...]) + _dot_nt(g_im, bim_ref[...])
                       ).astype(du_ref.dtype)
        dbre_ref[...] += _dot_tn(u, g_re)
        dbim_ref[...] += _dot_tn(u, g_im)

    rev = lambda i: nt - 1 - i
    rrow = lambda n, col=0: pl.BlockSpec((tm, n), lambda i: (rev(i), col))
    halo = pl.BlockSpec((SUBLANES, S5_NSTATE), lambda i: (jnp.maximum(rev(i) * hb - 1, 0), 0))
    return pl.pallas_call(
        body, name="s5_bwd", grid=(nt,),
        in_specs=[rrow(S5_WIDTH, 512 // S5_WIDTH), rrow(S5_WIDTH, P_U // S5_WIDTH), rrow(S5_WIDTH),
                  rrow(S5_NSTATE), rrow(S5_NSTATE), halo, halo,
                  _const((S5_WIDTH, S5_NSTATE)), _const((S5_WIDTH, S5_NSTATE)), _const((S5_NSTATE, S5_WIDTH)),
                  _const((S5_NSTATE, S5_WIDTH)), _const((1, S5_WIDTH)), _const((S5_WIDTH, S5_WIDTH)),
                  _const((1, S5_WIDTH)), _const((5, 2, SUBLANES, S5_NSTATE))],
        out_specs=[rrow(S5_WIDTH), _const((S5_WIDTH, S5_NSTATE)), _const((S5_WIDTH, S5_NSTATE)),
                   _const((S5_NSTATE, S5_WIDTH)), _const((S5_NSTATE, S5_WIDTH)), _const((2, SUBLANES, S5_NSTATE)),
                   _const((1, S5_WIDTH)), _const((S5_WIDTH, S5_WIDTH)), _const((1, S5_WIDTH))],
        out_shape=[jax.ShapeDtypeStruct((t, S5_WIDTH), MXU_DTYPE), jax.ShapeDtypeStruct((S5_WIDTH, S5_NSTATE), F32),
                   jax.ShapeDtypeStruct((S5_WIDTH, S5_NSTATE), F32), jax.ShapeDtypeStruct((S5_NSTATE, S5_WIDTH), F32),
                   jax.ShapeDtypeStruct((S5_NSTATE, S5_WIDTH), F32),
                   jax.ShapeDtypeStruct((2, SUBLANES, S5_NSTATE), F32), jax.ShapeDtypeStruct((1, S5_WIDTH), F32),
                   jax.ShapeDtypeStruct((S5_WIDTH, S5_WIDTH), F32), jax.ShapeDtypeStruct((1, S5_WIDTH), F32)],
        scratch_shapes=[pltpu.VMEM((tm, S5_NSTATE), F32), pltpu.VMEM((tm, S5_NSTATE), F32),
                        pltpu.VMEM((tm + SUBLANES, S5_NSTATE), F32), pltpu.VMEM((tm + SUBLANES, S5_NSTATE), F32),
                        pltpu.VMEM((2, SUBLANES, S5_NSTATE), F32)],
        compiler_params=_cparams(("arbitrary",)),
    )(dycat, proj, y2, hre, him, hre, him, bre, bim, cre, cim, d_skip, glu_w, glu_b, rcoef)


def _rg_gates(xc, wa, ba, wx, bx, nsp):
    r = _sigmoid(_dot(xc, wa) + ba)
    ig = _sigmoid(_dot(xc, wx) + bx)
    log_a = nsp * r
    a = jnp.exp(log_a)
    mult = jnp.sqrt(-_expm1(2.0 * log_a))
    return r, ig, a, mult


def _rg_fwd(proj, cw, cb, wa, ba, wx, bx, nsp):
    t = proj.shape[0]
    tm = SCAN_TM
    ng = tm // SUBLANES
    hb = tm // SUBLANES

    def body(x_ref, halo_ref, gt_ref, cw_ref, cb_ref, wa_ref, ba_ref, wx_ref, bx_ref, nsp_ref,
             y_ref, h_ref, xpad, abuf, carry):
        i = pl.program_id(0)

        @pl.when(i == 0)
        def _():
            carry[...] = jnp.zeros_like(carry)

        xpad[0:SUBLANES, :] = jnp.where(i > 0, halo_ref[...], 0.0)
        xpad[SUBLANES:SUBLANES + tm, :] = x_ref[...]
        xc = cb_ref[...] + _conv_taps(xpad, cw_ref[...], tm, SUBLANES - 3)
        _, ig, a, mult = _rg_gates(xc, wa_ref[...], ba_ref[...], wx_ref[...], bx_ref[...], nsp_ref[...])
        abuf[...] = a
        h_ref[...] = mult * (ig * xc)
        sub = lax.broadcasted_iota(jnp.int32, (SUBLANES, RG_WIDTH), 0)

        def step(gi, car):
            rows = pl.ds(pl.multiple_of(gi * SUBLANES, SUBLANES), SUBLANES)
            av = abuf[rows, :]
            bv = h_ref[rows, :]
            for sh in (1, 2, 4):
                m = sub >= sh
                bv = jnp.where(m, av * pltpu.roll(bv, sh, 0) + bv, bv)
                av = jnp.where(m, av * pltpu.roll(av, sh, 0), av)
            hv = bv + av * car
            h_ref[rows, :] = hv
            return jnp.broadcast_to(hv[SUBLANES - 1:SUBLANES, :], hv.shape)

        carry[...] = lax.fori_loop(0, ng, step, carry[...])
        y_ref[...] = (h_ref[...] * _gelu(gt_ref[...])).astype(y_ref.dtype)

    return pl.pallas_call(
        body, name="rg_fwd", grid=(t // tm,),
        in_specs=[pl.BlockSpec((tm, RG_WIDTH), lambda i: (i, P_XRG // RG_WIDTH)),
                  pl.BlockSpec((SUBLANES, RG_WIDTH), lambda i: (jnp.maximum(i * hb - 1, 0), P_XRG // RG_WIDTH)),
                  pl.BlockSpec((tm, RG_WIDTH), lambda i: (i, P_GRG // RG_WIDTH)),
                  _const((4, RG_WIDTH)), _const((1, RG_WIDTH)), _const((RG_WIDTH, RG_WIDTH)), _const((1, RG_WIDTH)),
                  _const((RG_WIDTH, RG_WIDTH)), _const((1, RG_WIDTH)), _const((1, RG_WIDTH))],
        out_specs=[_rows(tm, RG_WIDTH), _rows(tm, RG_WIDTH)],
        out_shape=[jax.ShapeDtypeStruct((t, RG_WIDTH), MXU_DTYPE), jax.ShapeDtypeStruct((t, RG_WIDTH), F32)],
        scratch_shapes=[pltpu.VMEM((tm + SUBLANES, RG_WIDTH), F32), pltpu.VMEM((tm, RG_WIDTH), F32),
                        pltpu.VMEM((SUBLANES, RG_WIDTH), F32)],
        compiler_params=_cparams(("arbitrary",)),
    )(proj, proj, proj, cw, cb, wa, ba, wx, bx, nsp)


def _rg_bwd(dycat, proj, hs, cw, cb, wa, ba, wx, bx, nsp):
    t = proj.shape[0]
    tm = SCAN_TM
    nt = t // tm
    ng = tm // SUBLANES
    hb = tm // SUBLANES

    def body(dy_ref, x_ref, halo_ref, gt_ref, h_ref, h_halo, cw_ref, cb_ref, wa_ref, ba_ref, wx_ref, bx_ref, nsp_ref,
             dx_ref, dgt_ref, dcw_ref, dcb_ref, dwa_ref, dba_ref, dwx_ref, dbx_ref, dnsp_ref,
             xpad, abuf, gbuf, hpad, dabuf, dpad, carry, dnext):
        i = pl.program_id(0)

        @pl.when(i == 0)
        def _():
            for r in (dcw_ref, dcb_ref, dwa_ref, dba_ref, dwx_ref, dbx_ref, dnsp_ref, carry, dnext):
                r[...] = jnp.zeros_like(r)

        first = i == nt - 1
        xpad[0:SUBLANES, :] = jnp.where(first, 0.0, halo_ref[...])
        xpad[SUBLANES:SUBLANES + tm, :] = x_ref[...]
        cw_v = cw_ref[...]
        xc = cb_ref[...] + _conv_taps(xpad, cw_v, tm, SUBLANES - 3)
        nsp_v = nsp_ref[...]
        r, ig, a, mult = _rg_gates(xc, wa_ref[...], ba_ref[...], wx_ref[...], bx_ref[...], nsp_v)
        abuf[...] = a
        hv = h_ref[...]
        hpad[0:SUBLANES, :] = jnp.where(first, 0.0, h_halo[...])
        hpad[SUBLANES:SUBLANES + tm, :] = hv
        gt = gt_ref[...]
        dout = dy_ref[...]
        dgt_ref[...] = (dout * hv * _gelu_grad(gt)).astype(dgt_ref.dtype)
        gbuf[...] = dout * _gelu(gt)
        sub = lax.broadcasted_iota(jnp.int32, (SUBLANES, RG_WIDTH), 0)
        last_row = sub == SUBLANES - 1
        row0 = sub == 0

        def step(k, car):
            gi = ng - 1 - k
            rows = pl.ds(pl.multiple_of(gi * SUBLANES, SUBLANES), SUBLANES)
            nrows = pl.ds(pl.multiple_of(gi * SUBLANES + SUBLANES, SUBLANES), SUBLANES)
            av = abuf[rows, :]
            bv = gbuf[rows, :] + jnp.where(last_row, car, 0.0)
            ev = jnp.where(last_row, 0.0, pltpu.roll(av, SUBLANES - 1, 0))
            for sh in (1, 2, 4):
                m = sub < SUBLANES - sh
                bv = jnp.where(m, bv + ev * pltpu.roll(bv, SUBLANES - sh, 0), bv)
                ev = jnp.where(m, ev * pltpu.roll(ev, SUBLANES - sh, 0), 0.0)
            gbuf[rows, :] = bv
            pv = hpad[rows, :]
            hprev = jnp.where(row0, jnp.broadcast_to(pv[SUBLANES - 1:SUBLANES, :], pv.shape),
                              pltpu.roll(hpad[nrows, :], 1, 0))
            dabuf[rows, :] = bv * hprev
            return jnp.broadcast_to((av * bv)[0:1, :], bv.shape)

        carry[...] = lax.fori_loop(0, ng, step, carry[...])
        gv = gbuf[...]
        da = dabuf[...]
        ix = ig * xc
        dmult = gv * ix
        dig = gv * mult * xc
        dxc = gv * mult * ig
        dlog_a = da * a - dmult * (a * a) / mult
        dnsp_ref[...] += _sum0(dlog_a * r)
        dpr = dlog_a * nsp_v * r * (1.0 - r)
        dpi = dig * ig * (1.0 - ig)
        dxc = dxc + _dot_nt(dpr, wa_ref[...]) + _dot_nt(dpi, wx_ref[...])
        dwa_ref[...] += _dot_tn(xc, dpr)
        dwx_ref[...] += _dot_tn(xc, dpi)
        dba_ref[...] += _sum0(dpr)
        dbx_ref[...] += _sum0(dpi)
        dcb_ref[...] += _sum0(dxc)
        for k in range(4):
            dcw_ref[k:k + 1, :] += _sum0(dxc * xpad[SUBLANES - 3 + k:SUBLANES - 3 + k + tm, :])
        dpad[0:tm, :] = dxc
        dpad[tm:tm + SUBLANES, :] = dnext[...]
        dx = cw_v[0:1, :] * dpad[3:3 + tm, :]
        for k in range(1, 4):
            dx = dx + cw_v[k:k + 1, :] * dpad[3 - k:3 - k + tm, :]
        dx_ref[...] = dx.astype(dx_ref.dtype)
        dnext[...] = dxc[0:SUBLANES, :]

    rev = lambda i: nt - 1 - i
    rrow = lambda n, col=0: pl.BlockSpec((tm, n), lambda i: (rev(i), col))
    sq = _const((RG_WIDTH, RG_WIDTH))
    vec = _const((1, RG_WIDTH))
    return pl.pallas_call(
        body, name="rg_bwd", grid=(nt,),
        in_specs=[rrow(RG_WIDTH, 768 // RG_WIDTH), rrow(RG_WIDTH, P_XRG // RG_WIDTH),
                  pl.BlockSpec((SUBLANES, RG_WIDTH), lambda i: (jnp.maximum(rev(i) * hb - 1, 0), P_XRG // RG_WIDTH)),
                  rrow(RG_WIDTH, P_GRG // RG_WIDTH), rrow(RG_WIDTH),
                  pl.BlockSpec((SUBLANES, RG_WIDTH), lambda i: (jnp.maximum(rev(i) * hb - 1, 0), 0)),
                  _const((4, RG_WIDTH)), vec, sq, vec, sq, vec, vec],
        out_specs=[rrow(RG_WIDTH), rrow(RG_WIDTH), _const((SUBLANES, RG_WIDTH)), vec, sq, vec, sq, vec, vec],
        out_shape=[jax.ShapeDtypeStruct((t, RG_WIDTH), MXU_DTYPE), jax.ShapeDtypeStruct((t, RG_WIDTH), MXU_DTYPE),
                   jax.ShapeDtypeStruct((SUBLANES, RG_WIDTH), F32), jax.ShapeDtypeStruct((1, RG_WIDTH), F32),
                   jax.ShapeDtypeStruct((RG_WIDTH, RG_WIDTH), F32), jax.ShapeDtypeStruct((1, RG_WIDTH), F32),
                   jax.ShapeDtypeStruct((RG_WIDTH, RG_WIDTH), F32), jax.ShapeDtypeStruct((1, RG_WIDTH), F32),
                   jax.ShapeDtypeStruct((1, RG_WIDTH), F32)],
        scratch_shapes=[pltpu.VMEM((tm + SUBLANES, RG_WIDTH), F32), pltpu.VMEM((tm, RG_WIDTH), F32),
                        pltpu.VMEM((tm, RG_WIDTH), F32), pltpu.VMEM((tm + SUBLANES, RG_WIDTH), F32),
                        pltpu.VMEM((tm, RG_WIDTH), F32), pltpu.VMEM((tm + SUBLANES, RG_WIDTH), F32),
                        pltpu.VMEM((SUBLANES, RG_WIDTH), F32), pltpu.VMEM((SUBLANES, RG_WIDTH), F32)],
        compiler_params=_cparams(("arbitrary",)),
    )(dycat, proj, proj, proj, hs, hs, cw, cb, wa, ba, wx, bx, nsp)


def _block_diag(blocks):
    g, a, b = blocks.shape
    eye = jnp.eye(g, dtype=blocks.dtype)
    return (eye[:, None, :, None] * blocks[:, :, None, :]).reshape(g * a, g * b)


def _block_diag_extract(m, g):
    a, b = m.shape[0] // g, m.shape[1] // g
    m4 = m.reshape(g, a, g, b)
    idx = jnp.arange(g)
    return m4[idx, :, idx, :]


def _s5_prepare(lam_re, lam_im, log_step, b_re, b_im, c_re, c_im):
    step = jnp.exp(log_step)[:, None]
    mag = jnp.exp(lam_re * step)
    lbr = mag * jnp.cos(lam_im * step)
    lbi = mag * jnp.sin(lam_im * step)
    nr, ni = lbr - 1.0, lbi
    den = lam_re * lam_re + lam_im * lam_im
    cr = (nr * lam_re + ni * lam_im) / den
    ci = (ni * lam_re - nr * lam_im) / den
    bbr = cr[..., None] * b_re - ci[..., None] * b_im
    bbi = cr[..., None] * b_im + ci[..., None] * b_re
    bre = _block_diag(jnp.swapaxes(bbr, 1, 2))
    bim = _block_diag(jnp.swapaxes(bbi, 1, 2))
    cre = _block_diag(jnp.swapaxes(c_re, 1, 2))
    cim = _block_diag(jnp.swapaxes(c_im, 1, 2))
    return lbr.reshape(-1), lbi.reshape(-1), bre, bim, cre, cim


def _s5_scan_coef(lbr, lbi, reverse):
    if reverse:
        lbi = -lbi
    pr, pi = [lbr], [lbi]
    for _ in range(7):
        pr, pi = pr + [pr[-1] * lbr - pi[-1] * lbi], pi + [pr[-1] * lbi + pi[-1] * lbr]
    row = jnp.arange(SUBLANES)[:, None]
    tabs = []
    for sh in (1, 2, 4):
        keep = (row < SUBLANES - sh) if reverse else (row >= sh)
        tabs.append(jnp.stack([jnp.where(keep, pr[sh - 1][None, :], 0.0), jnp.where(keep, pi[sh - 1][None, :], 0.0)]))
    powr = jnp.stack(pr)
    powi = jnp.stack(pi)
    if reverse:
        powr, powi = powr[::-1], powi[::-1]
    tabs.append(jnp.stack([powr, powi]))
    tabs.append(jnp.zeros_like(tabs[-1]))
    return jnp.stack(tabs).astype(F32)


def _xy_peers():
    x, y, c = lax.axis_index("x"), lax.axis_index("y"), lax.axis_index("c")
    return x, y, c, [(1 - x, y), (x, 1 - y), (1 - x, 1 - y)]


def _hbm():
    return pl.BlockSpec(memory_space=pl.ANY)


def _xy_allgather(buf, *, name):
    n, w = buf.shape

    def body(x_ref, out_ref, send_sems, recv_sems, local_sem):
        x, y, c, peers = _xy_peers()
        me = 2 * x + y
        own = pltpu.make_async_copy(x_ref, out_ref.at[me], local_sem)
        own.start()
        sends = []
        for k, (px, py) in enumerate(peers):
            cp = pltpu.make_async_remote_copy(src_ref=x_ref, dst_ref=out_ref.at[me], send_sem=send_sems.at[k],
                                              recv_sem=recv_sems.at[k], device_id=(px, py, c), device_id_type=MESH)
            cp.start()
            sends.append(cp)
        for k, (px, py) in enumerate(peers):
            pltpu.make_async_remote_copy(src_ref=x_ref, dst_ref=out_ref.at[2 * px + py], send_sem=send_sems.at[k],
                                         recv_sem=recv_sems.at[k], device_id=(px, py, c),
                                         device_id_type=MESH).wait_recv()
        for cp in sends:
            cp.wait_send()
        own.wait()

    return pl.pallas_call(
        body, name=name, in_specs=[_hbm()], out_specs=_hbm(),
        out_shape=jax.ShapeDtypeStruct((4, n, w), buf.dtype),
        scratch_shapes=[pltpu.SemaphoreType.DMA((3,)), pltpu.SemaphoreType.DMA((3,)), pltpu.SemaphoreType.DMA],
    )(buf)


def _remote(src, dst, send_sem, recv_sem, dev):
    return pltpu.make_async_remote_copy(src_ref=src, dst_ref=dst, send_sem=send_sem, recv_sem=recv_sem,
                                        device_id=dev, device_id_type=MESH)


LAYER_GATHERED = (
    ("ssd_conv_w", (4, 256), 1), ("rg_conv_w", (4, LANES), 1),
    ("w_in", (1024, W_IN_PAD), 1), ("s5_glu_w", (64, 256), 0), ("w_out", (256, 1024), 0), ("xa_wq", (256, 1024), 0),
    ("xa_wk", (256, 1024), 0), ("xa_wv", (256, 1024), 0), ("xa_wo", (256, 1024), 0), ("mlp_w1", (1024, 1024), 1),
    ("mlp_w2", (1024, 1024), 0),
)
N_GATHERED = len(LAYER_GATHERED)
WAIT_GROUPS = ((0, 1, 2, 3), (4,), (5, 6, 7, 8), (9, 10))
RG_CONV_SHARD = RG_WIDTH // 4
N_GATHER_COPIES = 3 * N_GATHERED * DEPTH


def _gather_part(ref, t, pos):
    _, shp, ax = LAYER_GATHERED[t % N_GATHERED]
    idx = tuple(pl.ds(pos * shp[ax], shp[ax]) if d == ax else slice(None) for d in range(len(shp)))
    return ref.at[idx]


def _gather_start(shards):
    n = len(shards)
    lands = []
    for t, s in enumerate(shards):
        _, shp, ax = LAYER_GATHERED[t % N_GATHERED]
        full = shp[:ax] + (4 * shp[ax],) + shp[ax + 1:]
        lands.append(pltpu.with_memory_space_constraint(lax.empty(full, s.dtype), pltpu.HBM))

    def body(*refs):
        srcs, lnds = refs[:n], refs[n:2 * n]
        send_sems, recv_sems, local_sems = refs[2 * n:2 * n + 3]
        token = refs[-1]
        x, y, c, peers = _xy_peers()
        me = 2 * x + y
        for t in range(n):
            for k, (px, py) in enumerate(peers):
                _remote(srcs[t], _gather_part(lnds[t], t, me), send_sems.at[k * n + t], recv_sems.at[k * n + t],
                        (px, py, c)).start()
            pltpu.make_async_copy(srcs[t], _gather_part(lnds[t], t, me), local_sems.at[t]).start()
        token[...] = jnp.zeros_like(token)

    hbm = pl.BlockSpec(memory_space=pltpu.HBM)
    sem = pl.BlockSpec(memory_space=pltpu.SEMAPHORE)
    outs = pl.pallas_call(
        body, name="weights_gather_start", in_specs=[hbm] * (2 * n),
        out_shape=(pltpu.SemaphoreType.DMA((3 * n,)), pltpu.SemaphoreType.DMA((3 * n,)),
                   pltpu.SemaphoreType.DMA((n,)),
                   *[pltpu.HBM(s.shape, s.dtype) for s in shards], *[pltpu.HBM(a.shape, a.dtype) for a in lands],
                   jax.ShapeDtypeStruct((SUBLANES, LANES), F32)),
        out_specs=(sem, sem, sem, *[hbm] * (2 * n), pl.BlockSpec(memory_space=pltpu.VMEM)),
        input_output_aliases={i: 3 + i for i in range(2 * n)},
        compiler_params=pltpu.CompilerParams(has_side_effects=pltpu.SideEffectType.DATAFLOW_SIDE_EFFECTING),
    )(*[pltpu.with_memory_space_constraint(s, pltpu.HBM) for s in shards], *lands)
    return outs[0], outs[1], outs[2], outs[3:3 + n], outs[3 + n:3 + 2 * n], outs[-1]


def _gather_wait(handle, ts, after, *, name):
    send_sems, recv_sems, local_sems, src_thru, land_thru, _ = handle
    n = len(src_thru)
    m = len(ts)

    def body(*refs):
        srcs, lnds = refs[:m], refs[m:2 * m]
        ssem, rsem, lsem = refs[2 * m:2 * m + 3]
        x, y, c, peers = _xy_peers()
        me = 2 * x + y
        for i, t in enumerate(ts):
            for k, (px, py) in enumerate(peers):
                cp = _remote(srcs[i], _gather_part(lnds[i], t, 2 * px + py), ssem.at[k * n + t], rsem.at[k * n + t],
                             (px, py, c))
                cp.wait_send()
                cp.wait_recv()
            pltpu.make_async_copy(srcs[i], _gather_part(lnds[i], t, me), lsem.at[t]).wait()

    hbm = pl.BlockSpec(memory_space=pltpu.HBM)
    sem = pl.BlockSpec(memory_space=pltpu.SEMAPHORE)
    args = [src_thru[t] for t in ts] + [land_thru[t] for t in ts]
    outs = pl.pallas_call(
        body, name=name, in_specs=[hbm] * (2 * m) + [sem, sem, sem, pl.BlockSpec(memory_space=pl.ANY)],
        out_shape=[pltpu.HBM(a.shape, a.dtype) for a in args], out_specs=[hbm] * (2 * m),
        input_output_aliases={i: i for i in range(2 * m)},
        compiler_params=pltpu.CompilerParams(has_side_effects=pltpu.SideEffectType.DATAFLOW_SIDE_EFFECTING),
    )(*args, send_sems, recv_sems, local_sems, after)
    return outs[:m], outs[m:]


C_CHUNKS = 8
XY_CHUNKS = 8
EW_ROWS = 512


def _c_exchange(g, part):
    w = g.shape[2]
    row0, nrows = G_PARTS[part]
    half = nrows // 2
    rq = half // C_CHUNKS

    def body(g_ref, got_ref, send_sems, recv_sems):
        x, y, c = lax.axis_index("x"), lax.axis_index("y"), lax.axis_index("c")
        cps = []
        for s in range(4):
            for q in range(C_CHUNKS):
                k = s * C_CHUNKS + q
                cp = _remote(g_ref.at[s, pl.ds(row0 + (1 - c) * half + q * rq, rq), :],
                             got_ref.at[s, pl.ds(q * rq, rq), :], send_sems.at[k], recv_sems.at[k], (x, y, 1 - c))
                cp.start()
                cps.append(cp)
        for cp in cps:
            cp.wait_recv()
        for cp in cps:
            cp.wait_send()

    return pl.pallas_call(
        body, name="grad_c_exchange_%d" % part, in_specs=[_hbm()], out_specs=_hbm(),
        out_shape=jax.ShapeDtypeStruct((4, half, w), g.dtype),
        scratch_shapes=[pltpu.SemaphoreType.DMA((4 * C_CHUNKS,)), pltpu.SemaphoreType.DMA((4 * C_CHUNKS,))],
    )(g)


XFER_DTYPE = jnp.bfloat16


def _add_own_half(g, got, c_arr, part):
    w = g.shape[2]
    row0, nrows = G_PARTS[part]
    half = nrows // 2
    nb = half // EW_ROWS
    b0 = row0 // EW_ROWS

    def body(c_ref, a_ref, b_ref, o_ref, t_ref):
        sm = a_ref[...] + b_ref[...]
        o_ref[...] = sm.astype(o_ref.dtype)

        @pl.when(pl.program_id(1) == nb - 1)
        def _():
            t_ref[...] = sm[:, EW_ROWS - MISC_ROWS:, :]

    grid_spec = pltpu.PrefetchScalarGridSpec(
        num_scalar_prefetch=1, grid=(4, nb),
        in_specs=[pl.BlockSpec((1, EW_ROWS, w), lambda s, i, c: (s, b0 + c[0] * nb + i, 0)),
                  pl.BlockSpec((1, EW_ROWS, w), lambda s, i, c: (s, i, 0))],
        out_specs=[pl.BlockSpec((1, EW_ROWS, w), lambda s, i, c: (s, i, 0)),
                   pl.BlockSpec((1, MISC_ROWS, w), lambda s, i, c: (s, 0, 0))])
    return pl.pallas_call(
        body, name="grad_add_halves", grid_spec=grid_spec,
        out_shape=[jax.ShapeDtypeStruct((4, half, w), XFER_DTYPE), jax.ShapeDtypeStruct((4, MISC_ROWS, w), g.dtype)],
        compiler_params=_cparams(("arbitrary", "arbitrary")),
    )(c_arr, g, got)


def _xy_pieces(arrs):
    pieces = []
    for a, arr in enumerate(arrs):
        nch = XY_CHUNKS if a == 0 else 1
        rq = arr.shape[1] // nch
        pieces += [(a, pl.ds(q * rq, rq)) for q in range(nch)]
    return pieces


def _xy_start(arrs, *, name):
    na = len(arrs)
    pieces = _xy_pieces(arrs)
    npc = len(pieces)
    lands = [pltpu.with_memory_space_constraint(lax.empty(a.shape, a.dtype), pltpu.HBM) for a in arrs]

    def body(*refs):
        ins, outs = refs[:na], refs[na:2 * na]
        send_sems, recv_sems, local_sems = refs[2 * na:2 * na + 3]
        token = refs[-1]
        x, y, c, peers = _xy_peers()
        me = 2 * x + y
        for k, (px, py) in enumerate(peers):
            for j, (a, rows) in enumerate(pieces):
                _remote(ins[a].at[2 * px + py, rows, :], outs[a].at[me, rows, :], send_sems.at[k * npc + j],
                        recv_sems.at[k * npc + j], (px, py, c)).start()
        for j, (a, rows) in enumerate(pieces):
            pltpu.make_async_copy(ins[a].at[me, rows, :], outs[a].at[me, rows, :], local_sems.at[j]).start()
        token[...] = jnp.zeros_like(token)

    hbm = pl.BlockSpec(memory_space=pltpu.HBM)
    sem = pl.BlockSpec(memory_space=pltpu.SEMAPHORE)
    outs = pl.pallas_call(
        body, name=name, in_specs=[hbm] * (2 * na),
        out_shape=(pltpu.SemaphoreType.DMA((3 * npc,)), pltpu.SemaphoreType.DMA((3 * npc,)),
                   pltpu.SemaphoreType.DMA((npc,)),
                   *[pltpu.HBM(a.shape, a.dtype) for a in arrs], *[pltpu.HBM(a.shape, a.dtype) for a in arrs],
                   jax.ShapeDtypeStruct((SUBLANES, LANES), F32)),
        out_specs=(sem, sem, sem, *[hbm] * (2 * na), pl.BlockSpec(memory_space=pltpu.VMEM)),
        input_output_aliases={i: 3 + i for i in range(2 * na)},
        compiler_params=pltpu.CompilerParams(has_side_effects=pltpu.SideEffectType.DATAFLOW_SIDE_EFFECTING),
    )(*[pltpu.with_memory_space_constraint(a, pltpu.HBM) for a in arrs], *lands)
    return (outs[0], outs[1], outs[2], outs[3:3 + na], outs[3 + na:3 + 2 * na]), outs[-1]


def _xy_wait(handle, after, *, name):
    send_sems, recv_sems, local_sems, src_thru, land_thru = handle
    na = len(src_thru)
    pieces = _xy_pieces(src_thru)
    npc = len(pieces)

    def body(*refs):
        ins, outs = refs[:na], refs[na:2 * na]
        ssem, rsem, lsem = refs[2 * na:2 * na + 3]
        x, y, c, peers = _xy_peers()
        me = 2 * x + y
        for k, (px, py) in enumerate(peers):
            for j, (a, rows) in enumerate(pieces):
                cp = _remote(ins[a].at[me, rows, :], outs[a].at[2 * px + py, rows, :], ssem.at[k * npc + j],
                             rsem.at[k * npc + j], (px, py, c))
                cp.wait_send()
                cp.wait_recv()
        for j, (a, rows) in enumerate(pieces):
            pltpu.make_async_copy(ins[a].at[me, rows, :], outs[a].at[me, rows, :], lsem.at[j]).wait()

    hbm = pl.BlockSpec(memory_space=pltpu.HBM)
    sem = pl.BlockSpec(memory_space=pltpu.SEMAPHORE)
    args = list(src_thru) + list(land_thru)
    outs = pl.pallas_call(
        body, name=name, in_specs=[hbm] * (2 * na) + [sem, sem, sem, pl.BlockSpec(memory_space=pl.ANY)],
        out_shape=[pltpu.HBM(a.shape, a.dtype) for a in args], out_specs=[hbm] * (2 * na),
        input_output_aliases={i: i for i in range(2 * na)},
        compiler_params=pltpu.CompilerParams(has_side_effects=pltpu.SideEffectType.DATAFLOW_SIDE_EFFECTING),
    )(*args, send_sems, recv_sems, local_sems, after)
    return outs[na:]


def _sum4_into_half(r, rt, c_arr, part, fbuf):
    _, half, w = r.shape
    nb = half // EW_ROWS
    b0 = G_PARTS[part][0] // EW_ROWS

    def body(c_ref, r_ref, t_ref, *rest):
        o_ref = rest[-1]
        o_ref[...] = ((r_ref[0].astype(F32) + r_ref[1].astype(F32)) + r_ref[2].astype(F32)) + r_ref[3].astype(F32)

        @pl.when(pl.program_id(0) == nb - 1)
        def _():
            o_ref[EW_ROWS - MISC_ROWS:, :] = ((t_ref[0] + t_ref[1]) + t_ref[2]) + t_ref[3]

    in_specs = [pl.BlockSpec((4, EW_ROWS, w), lambda i, c: (0, i, 0)),
                pl.BlockSpec((4, MISC_ROWS, w), lambda i, c: (0, 0, 0))]
    args = [c_arr, r, rt]
    aliases = {}
    if fbuf is not None:
        in_specs.append(pl.BlockSpec(memory_space=pl.ANY))
        args.append(fbuf)
        aliases = {3: 0}
    grid_spec = pltpu.PrefetchScalarGridSpec(
        num_scalar_prefetch=1, grid=(nb,), in_specs=in_specs,
        out_specs=pl.BlockSpec((EW_ROWS, w), lambda i, c: (b0 + c[0] * nb + i, 0)))
    return pl.pallas_call(
        body, name="grad_sum4", grid_spec=grid_spec, out_shape=jax.ShapeDtypeStruct((G_ROWS, w), F32),
        input_output_aliases=aliases, compiler_params=_cparams(("arbitrary",)),
    )(*args)


C_GATHER_ROWS = 512


def _c_allgather_halves(f, parts):
    w = f.shape[1]
    chunks = []
    for part in parts:
        chunks += [(part, r) for r in range(0, G_PARTS[part][1] // 2, C_GATHER_ROWS)]
    nch = len(chunks)

    def body(f_ref, out_ref, send_sems, recv_sems):
        x, y, c = lax.axis_index("x"), lax.axis_index("y"), lax.axis_index("c")

        def rows(q, owner):
            part, r = chunks[q]
            row0, nrows = G_PARTS[part]
            return pl.ds(row0 + owner * (nrows // 2) + r, C_GATHER_ROWS)

        sends = []
        for q in range(nch):
            cp = _remote(f_ref.at[rows(q, c), :], out_ref.at[rows(q, c), :], send_sems.at[q], recv_sems.at[q],
                         (x, y, 1 - c))
            cp.start()
            sends.append(cp)
        for q in range(nch):
            _remote(f_ref.at[rows(q, 1 - c), :], out_ref.at[rows(q, 1 - c), :], send_sems.at[q], recv_sems.at[q],
                    (x, y, 1 - c)).wait_recv()
        for cp in sends:
            cp.wait_send()

    return pl.pallas_call(
        body, name="grad_c_allgather_" + "".join(str(p) for p in parts), in_specs=[_hbm()], out_specs=_hbm(),
        input_output_aliases={0: 0},
        out_shape=jax.ShapeDtypeStruct((G_ROWS, w), f.dtype),
        scratch_shapes=[pltpu.SemaphoreType.DMA((nch,)), pltpu.SemaphoreType.DMA((nch,))],
    )(f)


def _adamw(w, m, v, g, g_rows=None):
    shape = w.shape
    cols = shape[-1]
    rows = int(math.prod(shape)) // cols
    tr = 256 if rows % 256 == 0 else rows
    from_flat = g_rows is not None
    c1 = 1.0 / (1.0 - ADAM_B1 ** ADAM_STEP)
    c2 = 1.0 / (1.0 - ADAM_B2 ** ADAM_STEP)

    def body(w_ref, m_ref, v_ref, g_ref, *outs):
        gg = g_ref[...]
        nm = ADAM_B1 * m_ref[...] + (1.0 - ADAM_B1) * gg
        nv = ADAM_B2 * v_ref[...] + (1.0 - ADAM_B2) * (gg * gg)
        if from_flat:
            outs[0][...] = gg
        d_ref, nm_ref, nv_ref = outs[-3:]
        nm_ref[...] = nm
        nv_ref[...] = nv
        d_ref[...] = -ADAM_LR * ((nm * c1) / (jnp.sqrt(nv * c2) + ADAM_EPS) + ADAM_WD * w_ref[...])

    spec = pl.BlockSpec((tr, cols), lambda i: (i, 0))
    if from_flat:
        nbl = rows // DEPTH // tr
        assert cols == FLAT and all(r % tr == 0 for r in g_rows) and len(g_rows) == DEPTH == 2
        b0, b1 = g_rows[0] // tr, g_rows[1] // tr
        g_spec = pl.BlockSpec((tr, cols), lambda i: (jnp.where(i < nbl, b0 + i, b1 + i - nbl), 0))
        g_arg = g
    else:
        g_spec = spec
        g_arg = g.reshape(rows, cols)
    n_out = 4 if from_flat else 3
    sds = jax.ShapeDtypeStruct((rows, cols), F32)
    outs = pl.pallas_call(
        body, name="adamw", grid=(rows // tr,), in_specs=[spec, spec, spec, g_spec], out_specs=[spec] * n_out,
        out_shape=[sds] * n_out, compiler_params=_cparams(("arbitrary",)),
    )(w.reshape(rows, cols), m.reshape(rows, cols), v.reshape(rows, cols), g_arg)
    outs = [o.reshape(shape) for o in outs]
    return outs if from_flat else [g] + outs


SMALL_SHARDED = (("s5_glu_w", (2, 64, 256), 1), ("ssd_conv_w", (2, 4, 256), 2), ("rg_conv_w", (2, 4, 64), 2))
REPLICATED = (
    ("ssd_conv_b", (2, 1024)), ("ssd_dt_bias", (2, 8)), ("ssd_a_log", (2, 8)), ("ssd_d", (2, 8)),
    ("ssd_norm_w", (2, 512)), ("s5_lam_re", (2, 16, 64)), ("s5_lam_im", (2, 16, 64)), ("s5_log_step", (2, 16)),
    ("s5_b_re", (2, 16, 64, 16)), ("s5_b_im", (2, 16, 64, 16)), ("s5_c_re", (2, 16, 16, 64)),
    ("s5_c_im", (2, 16, 16, 64)), ("s5_d", (2, 256)), ("s5_glu_b", (2, 256)), ("rg_conv_b", (2, 256)),
    ("rg_wa", (2, 4, 64, 64)), ("rg_ba", (2, 4, 64)), ("rg_wx", (2, 4, 64, 64)), ("rg_bx", (2, 4, 64)),
    ("rg_lambda", (2, 256)), ("ln1_g", (2, 1024)), ("ln1_b", (2, 1024)), ("ln2_g", (2, 1024)), ("ln2_b", (2, 1024)),
    ("ln3_g", (2, 1024)), ("ln3_b", (2, 1024)),
)
WEIGHT_ORDER = (
    "w_in", "w_out", "ssd_conv_w", "ssd_conv_b", "ssd_dt_bias", "ssd_a_log", "ssd_d", "ssd_norm_w", "s5_lam_re",
    "s5_lam_im", "s5_log_step", "s5_b_re", "s5_b_im", "s5_c_re", "s5_c_im", "s5_d", "s5_glu_w", "s5_glu_b",
    "rg_conv_w", "rg_conv_b", "rg_wa", "rg_ba", "rg_wx", "rg_bx", "rg_lambda", "ln1_g", "ln1_b", "xa_wq", "xa_wk",
    "xa_wv", "xa_wo", "ln2_g", "ln2_b", "mlp_w1", "mlp_w2", "ln3_g", "ln3_b",
)


def _size(shape):
    return int(math.prod(shape))


def _round_up(a, b):
    return (a + b - 1) // b * b


SMALL_ELEMS = sum(_size(s) for _, s, _ in SMALL_SHARDED)
REP_ELEMS = sum(_size(s) for _, s in REPLICATED)
REP_QROWS = _round_up(-(-REP_ELEMS // (4 * FLAT)), 8)
assert SMALL_ELEMS <= MISC_REP_ROW * FLAT and MISC_REP_ROW + REP_QROWS <= MISC_ROWS


def _pack_shards(tensors, names_shapes):
    return jnp.concatenate([tensors[n].reshape(-1) for n, *_ in names_shapes])


def _unpack(flat, names_shapes):
    out, off = {}, 0
    for n, s, *_ in names_shapes:
        out[n] = flat[off:off + _size(s)].reshape(s)
        off += _size(s)
    return out


def _split_shards(full, names_shapes):
    rows = []
    for k in range(4):
        parts = []
        for n, s, ax in names_shapes:
            w = s[ax]
            parts.append(lax.slice_in_dim(full[n], k * w, (k + 1) * w, axis=ax).reshape(-1))
        rows.append(jnp.concatenate(parts))
    return jnp.stack(rows)


def _pack_cols(w):
    pad = jnp.zeros((w.shape[0], LANES - SSD_HEADS), w.dtype)
    return jnp.concatenate([w[:, O_XBC:O_XBC + 1024], w[:, O_Z:O_Z + 512], w[:, O_U:O_U + 256],
                            w[:, O_XRG:O_XRG + 256], w[:, O_GRG:O_GRG + 256], w[:, O_DT:O_DT + 8], pad], axis=1)


def _unpack_cols(w):
    return jnp.concatenate([w[:, P_Z:P_Z + 512], w[:, P_XBC:P_XBC + 1024], w[:, P_DT:P_DT + 8],
                            w[:, P_U:P_U + 256], w[:, P_XRG:P_XRG + 256], w[:, P_GRG:P_GRG + 256]], axis=1)


def _lanes(v, width):
    return jnp.pad(v, (0, width - v.shape[0])).reshape(1, width)


def _layer_params(rep, l):
    p = {}
    p["ssd_cb"] = rep["ssd_conv_b"][l].reshape(1, -1)
    p["ssd_dtb"] = _lanes(rep["ssd_dt_bias"][l], LANES)
    p["ssd_a"] = _lanes(-jnp.exp(rep["ssd_a_log"][l]), LANES)
    p["ssd_d"] = jnp.repeat(rep["ssd_d"][l], 64).reshape(1, -1)
    p["ssd_nw"] = rep["ssd_norm_w"][l].reshape(1, -1)
    s5_args = tuple(rep[n][l] for n in ("s5_lam_re", "s5_lam_im", "s5_log_step", "s5_b_re", "s5_b_im", "s5_c_re",
                                        "s5_c_im"))
    (lbr, lbi, bre, bim, cre, cim), p["s5_vjp"] = jax.vjp(_s5_prepare, *s5_args)
    p.update(s5_bre=bre, s5_bim=bim, s5_cre=cre, s5_cim=cim)
    p["s5_coef"] = _s5_scan_coef(lbr, lbi, False)
    p["s5_rcoef"] = _s5_scan_coef(lbr, lbi, True)
    p["s5_d"] = rep["s5_d"][l].reshape(1, -1)
    p["s5_gb"] = rep["s5_glu_b"][l].reshape(1, -1)
    p["rg_cb"] = rep["rg_conv_b"][l].reshape(1, -1)
    p["rg_wa"] = _block_diag(rep["rg_wa"][l])
    p["rg_wx"] = _block_diag(rep["rg_wx"][l])
    p["rg_ba"] = rep["rg_ba"][l].reshape(1, -1)
    p["rg_bx"] = rep["rg_bx"][l].reshape(1, -1)
    p["rg_nsp"] = (-RG_C * jax.nn.softplus(-rep["rg_lambda"][l])).reshape(1, -1)
    p["rg_dnsp"] = RG_C * jax.nn.sigmoid(-rep["rg_lambda"][l])
    for n in ("ln1_g", "ln1_b", "ln2_g", "ln2_b", "ln3_g", "ln3_b"):
        p[n] = rep[n][l].reshape(1, -1)
    return p


def _layer_fwd(h, mem, p, fetch):
    s = {"h0": h}
    p.update(fetch(0, h))
    proj = _mm(h, p["w_in"], name="in_proj")
    s["proj"] = proj
    y_ssd, s["ssd_yy"], s["ssd_states"] = _ssd_fwd(proj, p["ssd_cw"], p["ssd_cb"], p["ssd_dtb"], p["ssd_a"],
                                                     p["ssd_d"], p["ssd_nw"])
    y_s5, s["s5_y2"], s["s5_hre"], s["s5_him"] = _s5_fwd(proj, p["s5_bre"], p["s5_bim"], p["s5_cre"], p["s5_cim"],
                                                         p["s5_d"], p["s5_glu_w"], p["s5_gb"], p["s5_coef"])
    y_rg, s["rg_h"] = _rg_fwd(proj, p["rg_cw"], p["rg_cb"], p["rg_wa"], p["rg_ba"], p["rg_wx"], p["rg_bx"],
                              p["rg_nsp"])
    s["ys"] = [y_ssd, y_s5, y_rg]
    p.update(fetch(1, y_rg))
    h1, s["xh1"], s["rs1"] = _outproj_ln_fwd(s["ys"], h, p["w_out"], p["ln1_g"], p["ln1_b"])
    s["h1"] = h1
    p.update(fetch(2, h1))
    kb = _mm(mem, p["xa_wk"], name="mem_proj")
    vb = _mm(mem, p["xa_wv"], name="mem_proj")
    s["kb"], s["vb"] = kb, vb
    h2, s["xh2"], s["rs2"], s["attn_o"] = _attn_ln_fwd(h1, p["xa_wq"], p["xa_wo"], kb, vb, p["ln2_g"], p["ln2_b"])
    s["h2"] = h2
    p.update(fetch(3, h2))
    h3, s["xh3"], s["rs3"], s["mlp_hdn"] = _mlp_ln_fwd(h2, p["mlp_w1"], p["mlp_w2"], p["ln3_g"], p["ln3_b"])
    return h3, s


def _layer_bwd(dh3, mem, p, s, l, gbuf, after_mlp=None):
    g = {}
    dr3, du, dh2, g["ln3_g"], g["ln3_b"] = _mlp_ln_bwd(dh3, s["xh3"], s["rs3"], p["ln3_g"], s["mlp_hdn"],
                                                        p["mlp_w1"], p["mlp_w2"])
    gbuf = _wgrad_flat(s["h2"], du, gbuf, mode="colblk", row_off=_grad_row("mlp_w1", l), name="wgrad_mlp_w1")
    gbuf = _wgrad_flat(s["mlp_hdn"], dr3, gbuf, mode="rowblk", row_off=_grad_row("mlp_w2", l), name="wgrad_mlp_w2")
    ln2_g = p["ln2_g"] if after_mlp is None else p["ln2_g"] + after_mlp(gbuf)[0:1, 0:1]
    dr2, dq, dh1, dkb, dvb, g["ln2_g"], g["ln2_b"] = _attn_ln_bwd(dh2, s["xh2"], s["rs2"], ln2_g, s["h1"],
                                                                   p["xa_wq"], p["xa_wo"], s["kb"], s["vb"])
    for n, a_op, g_op in (("xa_wo", s["attn_o"], dr2), ("xa_wq", s["h1"], dq), ("xa_wk", mem, dkb),
                          ("xa_wv", mem, dvb)):
        gbuf = _wgrad_flat(a_op, g_op, gbuf, mode="rows4", row_off=_grad_row(n, l), name="wgrad_" + n)
    dr1, dres, dycat, g["ln1_g"], g["ln1_b"] = _outproj_ln_bwd(dh1, s["xh1"], s["rs1"], p["ln1_g"], p["w_out"])
    gbuf = _wgrad_flat(s["ys"], dr1, gbuf, mode="rows4", row_off=_grad_row("w_out", l), name="wgrad_w_out")
    proj = s["proj"]
    (dxbc, dz, ddt, dcw, dcb, ddtb, da_neg, dd_l, dnw) = _ssd_bwd(
        dycat, proj, s["ssd_yy"], s["ssd_states"], p["ssd_cw"], p["ssd_cb"], p["ssd_dtb"], p["ssd_a"], p["ssd_d"],
        p["ssd_nw"])
    g["ssd_conv_w"] = dcw[0:4]
    g["ssd_conv_b"] = dcb[0]
    g["ssd_dt_bias"] = ddtb[0, :SSD_HEADS]
    g["ssd_a_log"] = da_neg[0, :SSD_HEADS] * p["ssd_a"][0, :SSD_HEADS]
    g["ssd_d"] = dd_l.reshape(SSD_HEADS, 64).sum(axis=1)
    g["ssd_norm_w"] = dnw[0]
    (du_s5, dbre, dbim, dcre, dcim, dlam, dd5, dgw, dgb) = _s5_bwd(
        dycat, proj, s["s5_y2"], s["s5_hre"], s["s5_him"], p["s5_bre"], p["s5_bim"], p["s5_cre"], p["s5_cim"],
        p["s5_d"], p["s5_glu_w"], p["s5_gb"], p["s5_rcoef"])
    dl = dlam.sum(axis=1)
    s5g = p["s5_vjp"]((dl[0], dl[1], dbre, dbim, dcre, dcim))
    for n, v in zip(("s5_lam_re", "s5_lam_im", "s5_log_step", "s5_b_re", "s5_b_im", "s5_c_re", "s5_c_im"), s5g):
        g[n] = v
    g["s5_d"] = dd5[0]
    g["s5_glu_w"] = dgw
    g["s5_glu_b"] = dgb[0]
    (dxrg, dgrg, drcw, drcb, dwa, dba, dwx, dbx, dnsp) = _rg_bwd(
        dycat, proj, s["rg_h"], p["rg_cw"], p["rg_cb"], p["rg_wa"], p["rg_ba"], p["rg_wx"], p["rg_bx"], p["rg_nsp"])
    g["rg_conv_w"] = drcw[0:4]
    g["rg_conv_b"] = drcb[0]
    g["rg_wa"] = _block_diag_extract(dwa, RG_BLOCKS)
    g["rg_wx"] = _block_diag_extract(dwx, RG_BLOCKS)
    g["rg_ba"] = dba.reshape(RG_BLOCKS, RG_BLOCK_DIM)
    g["rg_bx"] = dbx.reshape(RG_BLOCKS, RG_BLOCK_DIM)
    g["rg_lambda"] = dnsp[0] * p["rg_dnsp"]
    dproj = [dxbc, dz, du_s5, dxrg, dgrg, ddt]
    g["w_in"] = _unpack_cols(_wgrad_in(s["h0"], dproj))
    dh0 = _in_proj_bwd(dproj, p["w_in"], dres)
    for n in ("ln1_g", "ln1_b", "ln2_g", "ln2_b", "ln3_g", "ln3_b"):
        g[n] = g[n][0]
    return dh0, g, gbuf


def _local_step(h, memf, target, rep, fetch):
    params, saved = [], []
    for l in range(DEPTH):
        p = _layer_params(rep, l)
        params.append(p)
        h, s = _layer_fwd(h, memf, p, functools.partial(fetch, l))
        saved.append(s)
    loss11, dh = _loss_fwd_bwd(h, target)
    grads = [None] * DEPTH
    gbuf = None
    c_arr = lax.axis_index("c").astype(jnp.int32).reshape(1)
    handles = {}

    def start_part(buf, part):
        handles[part], token = _xy_start(_chip_sums(buf, c_arr, part), name="grad_xy_start_%d" % part)
        return token

    for l in reversed(range(DEPTH)):
        hook = functools.partial(start_part, part=1) if l == 0 else None
        dh, grads[l], gbuf = _layer_bwd(dh, memf, params[l], saved[l], l, gbuf, hook)
        if l == DEPTH - 1:
            gbuf = lax.dynamic_update_slice(
                gbuf, _w_in_block(grads[l]["w_in"], jnp.zeros((4, MISC_ROWS, FLAT), F32)),
                (0, _grad_row("w_in", l), 0))
            params[0]["ln3_g"] = params[0]["ln3_g"] + start_part(gbuf, 0)[0:1, 0:1]
    gsmall = {n: jnp.stack([grads[l][n] for l in range(DEPTH)]) for n in grads[0] if n != "w_in"}
    return loss11, dh, gsmall, grads[0]["w_in"], gbuf, handles, c_arr


def _w_in_block(gw, tail):
    gw = jnp.pad(gw.reshape(D_MODEL, 4, W_IN_SHARD), ((0, 0), (0, 0), (0, W_IN_PAD - W_IN_SHARD)))
    return jnp.concatenate([jnp.transpose(gw, (1, 0, 2)).reshape(4, W_IN_PAD, FLAT), tail], axis=1)


def _chip_sums(gbuf, c_arr, part):
    return list(_add_own_half(gbuf, _c_exchange(gbuf, part), c_arr, part))


def kernel(x, mem, w_in, w_out, ssd_conv_w, ssd_conv_b, ssd_dt_bias, ssd_a_log, ssd_d, ssd_norm_w, s5_lam_re, s5_lam_im, s5_log_step, s5_b_re, s5_b_im, s5_c_re, s5_c_im, s5_d, s5_glu_w, s5_glu_b, rg_conv_w, rg_conv_b, rg_wa, rg_ba, rg_wx, rg_bx, rg_lambda, ln1_g, ln1_b, xa_wq, xa_wk, xa_wv, xa_wo, ln2_g, ln2_b, mlp_w1, mlp_w2, ln3_g, ln3_b, loss_target, m_w_in, m_w_out, m_ssd_conv_w, m_ssd_conv_b, m_ssd_dt_bias, m_ssd_a_log, m_ssd_d, m_ssd_norm_w, m_s5_lam_re, m_s5_lam_im, m_s5_log_step, m_s5_b_re, m_s5_b_im, m_s5_c_re, m_s5_c_im, m_s5_d, m_s5_glu_w, m_s5_glu_b, m_rg_conv_w, m_rg_conv_b, m_rg_wa, m_rg_ba, m_rg_wx, m_rg_bx, m_rg_lambda, m_ln1_g, m_ln1_b, m_xa_wq, m_xa_wk, m_xa_wv, m_xa_wo, m_ln2_g, m_ln2_b, m_mlp_w1, m_mlp_w2, m_ln3_g, m_ln3_b, v_w_in, v_w_out, v_ssd_conv_w, v_ssd_conv_b, v_ssd_dt_bias, v_ssd_a_log, v_ssd_d, v_ssd_norm_w, v_s5_lam_re, v_s5_lam_im, v_s5_log_step, v_s5_b_re, v_s5_b_im, v_s5_c_re, v_s5_c_im, v_s5_d, v_s5_glu_w, v_s5_glu_b, v_rg_conv_w, v_rg_conv_b, v_rg_wa, v_rg_ba, v_rg_wx, v_rg_bx, v_rg_lambda, v_ln1_g, v_ln1_b, v_xa_wq, v_xa_wk, v_xa_wv, v_xa_wo, v_ln2_g, v_ln2_b, v_mlp_w1, v_mlp_w2, v_ln3_g, v_ln3_b):
    args = dict(locals())
    weights = {n: args[n] for n in WEIGHT_ORDER}
    mom_m = {n: args["m_" + n] for n in WEIGHT_ORDER}
    mom_v = {n: args["v_" + n] for n in WEIGHT_ORDER}

    shards = []
    for l in range(DEPTH):
        for n, shp, ax in LAYER_GATHERED:
            w = weights[n][l]
            if w.shape[1] != shp[1]:
                w = jnp.pad(w, ((0, 0), (0, shp[1] - w.shape[1])))
            if n not in ("ssd_conv_w", "rg_conv_w"):
                w = w.astype(MXU_DTYPE)
            shards.append(w)
    handle = _gather_start(shards)

    def unpad(arr, padded, width):
        return jnp.concatenate([arr[:, padded * k:padded * k + width] for k in range(4)], axis=1)

    def fetch(l, grp, after):
        ts = [l * N_GATHERED + j for j in WAIT_GROUPS[grp]]
        _, landed = _gather_wait(handle, ts, after, name="weights_gather_wait_%d_%d" % (l, grp))
        out = {}
        for t, arr in zip(ts, landed):
            n = LAYER_GATHERED[t % N_GATHERED][0]
            if n == "w_in":
                arr = _pack_cols(unpad(arr, W_IN_PAD, W_IN_SHARD))
            elif n == "rg_conv_w":
                arr = unpad(arr, LANES, RG_CONV_SHARD)
            out[{"ssd_conv_w": "ssd_cw", "rg_conv_w": "rg_cw"}.get(n, n)] = arr
        return out

    rep = {n: weights[n] for n, _ in REPLICATED}

    loss11, dx, gsmall, gw_in0, gbuf, handles, c_arr = _local_step(x[0], mem[0], loss_target[0], rep, fetch)
    grad_x = dx[None]
    loss = lax.psum(loss11[0, 0], ("x", "y", "c"))

    small_q = _split_shards(gsmall, SMALL_SHARDED)
    rep_q = jnp.pad(_pack_shards(gsmall, REPLICATED), (0, 4 * REP_QROWS * FLAT - REP_ELEMS)).reshape(4, -1)
    misc = jnp.concatenate(
        [jnp.pad(small_q, ((0, 0), (0, MISC_REP_ROW * FLAT - SMALL_ELEMS))), rep_q,
         jnp.zeros((4, (MISC_ROWS - MISC_REP_ROW - REP_QROWS) * FLAT), F32)], axis=1).reshape(4, MISC_ROWS, FLAT)
    gbuf = lax.dynamic_update_slice(gbuf, _w_in_block(gw_in0, misc), (0, _grad_row("w_in", 0), 0))
    handles[2], token = _xy_start(_chip_sums(gbuf, c_arr, 2), name="grad_xy_start_2")
    fbuf = None
    for part in (0, 1):
        got = _xy_wait(handles[part], dx, name="grad_xy_wait_%d" % part)
        fbuf = _sum4_into_half(got[0], got[1] + token[0:1, 0:1], c_arr, part, fbuf)
    fbuf = _c_allgather_halves(fbuf, (0, 1))
    res = {n: _adamw(weights[n], mom_m[n], mom_v[n], fbuf, g_rows=[_grad_row(n, l) for l in range(DEPTH)])
           for n in ("mlp_w1", "mlp_w2")}
    got = _xy_wait(handles[2], res["mlp_w2"][1], name="grad_xy_wait_2")
    reduced = _c_allgather_halves(_sum4_into_half(got[0], got[1], c_arr, 2, fbuf), (2,))
    misc_red = reduced[ROW_MISC:]
    rep_all = _xy_allgather(misc_red[MISC_REP_ROW:MISC_REP_ROW + REP_QROWS], name="small_grads_allgather")
    g_red = {**_unpack(misc_red[:MISC_REP_ROW].reshape(-1), SMALL_SHARDED),
             **_unpack(rep_all.reshape(-1), REPLICATED)}
    g_red["w_in"] = jnp.stack([
        reduced[_grad_row("w_in", l):_grad_row("w_in", l) + W_IN_PAD].reshape(D_MODEL, W_IN_PAD)[:, :W_IN_SHARD]
        for l in range(DEPTH)])

    for n in WEIGHT_ORDER:
        if n in ("w_out", "xa_wq", "xa_wk", "xa_wv", "xa_wo"):
            res[n] = _adamw(weights[n], mom_m[n], mom_v[n], reduced, g_rows=[_grad_row(n, l) for l in range(DEPTH)])
        elif n not in res:
            res[n] = _adamw(weights[n], mom_m[n], mom_v[n], g_red[n])
    return (loss, grad_x, *[res[n][0] for n in WEIGHT_ORDER], *[res[n][1] for n in WEIGHT_ORDER],
            *[res[n][2] for n in WEIGHT_ORDER], *[res[n][3] for n in WEIGHT_ORDER])
```

```python
import functools
import math

import jax
import jax.numpy as jnp
from jax import lax
from jax.experimental import pallas as pl
from jax.experimental.pallas import tpu as pltpu

F32 = jnp.float32
MXU_DTYPE = jnp.bfloat16

D_MODEL = 1024
DEPTH = 2
MEM_LEN = 256
SSD_WIDTH = 512
SSD_HEADS = 8
SSD_STATE = 128
SSD_CHUNK = 128
SSD_XBC = 1024
S5_WIDTH = 256
S5_GROUPS = 16
S5_GROUP_CH = 16
S5_STATE = 64
S5_NSTATE = S5_GROUPS * S5_STATE
RG_WIDTH = 256
RG_BLOCKS = 4
RG_BLOCK_DIM = 64
RG_C = 8.0
XA_HEADS = 4
XA_HEAD_DIM = 256
D_FF = 4096
D_IN = 2312
ALPHA = (2.0 * DEPTH) ** 0.25
LN_EPS = 1e-5
ADAM_LR = 0.001
ADAM_B1 = 0.9
ADAM_B2 = 0.999
ADAM_EPS = 1e-08
ADAM_WD = 0.01
ADAM_STEP = 10

P_XBC, P_Z, P_U, P_XRG, P_GRG, P_DT = 0, 1024, 1536, 1792, 2048, 2304
D_PACK = 2432
O_Z, O_XBC, O_DT, O_U, O_XRG, O_GRG = 0, 512, 1536, 1544, 1800, 2056

LANES = 128
SUBLANES = 8
VMEM_LIMIT = 52 * 1024 * 1024
TM = 512
SSD_FWD_TM = 256
SSD_BWD_TM = 128
SCAN_TM = 512
FLAT = 1024

MESH = pl.DeviceIdType.MESH


def _cparams(sem):
    return pltpu.CompilerParams(dimension_semantics=sem, vmem_limit_bytes=VMEM_LIMIT)


def _dot(a, b):
    return jnp.dot(a.astype(MXU_DTYPE), b.astype(MXU_DTYPE), preferred_element_type=F32)


def _dot_nt(a, b):
    return lax.dot_general(a.astype(MXU_DTYPE), b.astype(MXU_DTYPE), (((1,), (1,)), ((), ())),
                           preferred_element_type=F32)


def _dot_tn(a, b):
    return lax.dot_general(a.astype(MXU_DTYPE), b.astype(MXU_DTYPE), (((0,), (0,)), ((), ())),
                           preferred_element_type=F32)


def _dot_f32(a, b):
    return jnp.dot(a, b, precision=lax.Precision.HIGHEST, preferred_element_type=F32)


def _dot_f32_tn(a, b):
    return lax.dot_general(a, b, (((0,), (0,)), ((), ())), precision=lax.Precision.HIGHEST,
                           preferred_element_type=F32)


def _sigmoid(x):
    return 1.0 / (1.0 + jnp.exp(-x))


def _softplus(x):
    return jnp.maximum(x, 0.0) + jnp.log(1.0 + jnp.exp(-jnp.abs(x)))


_GELU_K = math.sqrt(2.0 / math.pi)


def _gelu(x):
    return 0.5 * x * (1.0 + jnp.tanh(_GELU_K * (x + 0.044715 * x * x * x)))


def _gelu_grad(x):
    t = jnp.tanh(_GELU_K * (x + 0.044715 * x * x * x))
    return 0.5 * (1.0 + t) + 0.5 * x * (1.0 - t * t) * _GELU_K * (1.0 + 3.0 * 0.044715 * x * x)


def _expm1(x):
    small = x * (1.0 + x * (0.5 + x * (1.0 / 6.0 + x * (1.0 / 24.0))))
    return jnp.where(jnp.abs(x) < 0.05, small, jnp.exp(x) - 1.0)


def _sum0(x):
    return jnp.sum(x, axis=0, keepdims=True)


def _ln_fwd(r, g, b):
    mu = jnp.mean(r, axis=-1, keepdims=True)
    xc = r - mu
    var = jnp.mean(xc * xc, axis=-1, keepdims=True)
    rstd = lax.rsqrt(var + LN_EPS)
    xhat = xc * rstd
    return xhat * g + b, xhat, rstd


def _ln_bwd(dout, xhat, rstd, g):
    dxh = dout * g
    m1 = jnp.mean(dxh, axis=-1, keepdims=True)
    m2 = jnp.mean(dxh * xhat, axis=-1, keepdims=True)
    return rstd * (dxh - m1 - xhat * m2)


def _rows(tm, n, col=0):
    return pl.BlockSpec((tm, n), lambda i: (i, col))


def _const(shape):
    nd = len(shape)
    return pl.BlockSpec(shape, lambda i: (0,) * nd)


def _mm(a, w, *, name):
    t, k = a.shape
    n = w.shape[1]
    tm = min(TM, t)

    def body(a_ref, w_ref, o_ref):
        o_ref[...] = _dot(a_ref[...], w_ref[...])

    return pl.pallas_call(
        body, name=name, grid=(t // tm,), in_specs=[_rows(tm, k), _const(w.shape)], out_specs=_rows(tm, n),
        out_shape=jax.ShapeDtypeStruct((t, n), F32), compiler_params=_cparams(("arbitrary",)),
    )(a, w)


DPROJ_PIECES = ((P_XBC, 1024), (P_Z, 512), (P_U, 256), (P_XRG, 256), (P_GRG, 256), (P_DT, LANES))


def _in_proj_bwd(pieces, w, dres):
    t = dres.shape[0]
    npc = len(pieces)

    def body(*refs):
        w_ref, r_ref, o_ref = refs[npc:]
        acc = r_ref[...]
        for p_ref, (off, k) in zip(refs[:npc], DPROJ_PIECES):
            acc = acc + _dot_nt(p_ref[...], w_ref[:, off:off + k])
        o_ref[...] = acc

    return pl.pallas_call(
        body, name="in_proj_bwd", grid=(t // TM,),
        in_specs=[_rows(TM, k) for _, k in DPROJ_PIECES] + [_const(w.shape), _rows(TM, D_MODEL)],
        out_specs=_rows(TM, D_MODEL), out_shape=jax.ShapeDtypeStruct((t, D_MODEL), F32),
        compiler_params=_cparams(("arbitrary",)),
    )(*pieces, w, dres)


def _wgrad_in(h0, pieces):
    t = h0.shape[0]
    npc = len(pieces)

    def body(*refs):
        h_ref, o_ref = refs[npc], refs[npc + 1]
        @pl.when(pl.program_id(0) == 0)
        def _():
            o_ref[...] = jnp.zeros_like(o_ref)

        hb = h_ref[...].astype(MXU_DTYPE)
        for p_ref, (off, k) in zip(refs[:npc], DPROJ_PIECES):
            o_ref[:, off:off + k] += _dot_tn(hb, p_ref[...])

    return pl.pallas_call(
        body, name="wgrad_in", grid=(t // TM,),
        in_specs=[_rows(TM, k) for _, k in DPROJ_PIECES] + [_rows(TM, D_MODEL)],
        out_specs=_const((D_MODEL, D_PACK)), out_shape=jax.ShapeDtypeStruct((D_MODEL, D_PACK), F32),
        compiler_params=_cparams(("arbitrary",)),
    )(*pieces, h0)


G_ROWS = 8192
G_PARTS = ((0, 4096), (4096, 2048), (6144, 2048))
W_IN_SHARD = 578
W_IN_PAD = 640
MISC_ROWS = 128
MISC_REP_ROW = 40
ROW_MISC = G_ROWS - MISC_ROWS
W_IN_BLOCK_ROWS = W_IN_PAD + MISC_ROWS


def _grad_row(name, l):
    base = 0 if l == 1 else 4096
    mid = base + 2048 if l == 1 else 6144
    return {"mlp_w1": base, "mlp_w2": base + 1024, "w_out": mid, "xa_wq": mid + 256, "xa_wk": mid + 512,
            "xa_wv": mid + 768, "xa_wo": mid + 1024, "w_in": mid + 1280}[name]


def _wgrad_flat(a, g, buf, *, mode, row_off, name):
    pieces = list(a) if isinstance(a, (list, tuple)) else [a]
    t = g.shape[0]
    tt = min(1024, t)
    ns = t // tt
    blk = D_MODEL

    def accumulate(o_ref, parts, s):
        @pl.when(s == 0)
        def _():
            o_ref[...] = jnp.zeros_like(o_ref)

        for q, v in parts:
            o_ref[q] += v

    if mode == "rows4":
        grid = (ns,)
        in_specs = [pl.BlockSpec((tt, p.shape[1]), lambda s: (s, 0)) for p in pieces]
        in_specs.append(pl.BlockSpec((tt, blk), lambda s: (s, 0)))
        out_spec = pl.BlockSpec((4, 256, FLAT), lambda s: (0, row_off // 256, 0))
        sem = ("arbitrary",)
        npc = len(pieces)

        def body(*refs):
            g_v = refs[npc][...]
            parts, q0 = [], 0
            for p_ref in refs[:npc]:
                full = _dot_tn(p_ref[...], g_v)
                nq = full.shape[0] // 256
                parts += [(q0 + q, full[q * 256:(q + 1) * 256]) for q in range(nq)]
                q0 += nq
            accumulate(refs[-1], parts, pl.program_id(0))
    else:
        grid = (2, ns)
        if mode == "rowblk":
            in_specs = [pl.BlockSpec((tt, 2 * blk), lambda q, s: (s, q)), pl.BlockSpec((tt, blk), lambda q, s: (s, 0))]
        else:
            in_specs = [pl.BlockSpec((tt, blk), lambda q, s: (s, 0)), pl.BlockSpec((tt, 2 * blk), lambda q, s: (s, q))]
        out_spec = pl.BlockSpec((2, blk, FLAT), lambda q, s: (q, row_off // blk, 0))
        sem = ("arbitrary", "arbitrary")

        def body(a_ref, g_ref, *rest):
            full = _dot_tn(a_ref[...], g_ref[...])
            if mode == "rowblk":
                parts = [(0, full[:blk]), (1, full[blk:])]
            else:
                parts = [(0, full[:, :blk]), (1, full[:, blk:])]
            accumulate(rest[-1], parts, pl.program_id(1))

    args = pieces + [g]
    aliases = {}
    if buf is not None:
        in_specs.append(pl.BlockSpec(memory_space=pl.ANY))
        args.append(buf)
        aliases = {len(args) - 1: 0}
    return pl.pallas_call(
        body, name=name, grid=grid, in_specs=in_specs, out_specs=out_spec,
        out_shape=jax.ShapeDtypeStruct((4, G_ROWS, FLAT), F32), input_output_aliases=aliases,
        compiler_params=_cparams(sem),
    )(*args)


def _outproj_ln_fwd(ys, h, w, g, b):
    t = h.shape[0]
    npc = len(ys)

    def body(*refs):
        h_ref, w_ref, g_ref, b_ref, hn_ref, xh_ref, rs_ref = refs[npc:]
        r = ALPHA * h_ref[...]
        off = 0
        for y_ref in refs[:npc]:
            k = y_ref.shape[1]
            r = r + _dot(y_ref[...], w_ref[off:off + k, :])
            off += k
        out, xhat, rstd = _ln_fwd(r, g_ref[...], b_ref[...])
        hn_ref[...] = out
        xh_ref[...] = xhat
        rs_ref[...] = rstd

    return pl.pallas_call(
        body, name="outproj_ln_fwd", grid=(t // TM,),
        in_specs=[_rows(TM, y.shape[1]) for y in ys] + [_rows(TM, D_MODEL), _const((D_MODEL, D_MODEL)),
                                                        _const((1, D_MODEL)), _const((1, D_MODEL))],
        out_specs=[_rows(TM, D_MODEL), _rows(TM, D_MODEL), _rows(TM, 1)],
        out_shape=[jax.ShapeDtypeStruct((t, D_MODEL), F32), jax.ShapeDtypeStruct((t, D_MODEL), F32),
                   jax.ShapeDtypeStruct((t, 1), F32)],
        compiler_params=_cparams(("arbitrary",)),
    )(*ys, h, w, g, b)


def _attn_probs(q, kb, hh):
    sl = slice(hh * XA_HEAD_DIM, (hh + 1) * XA_HEAD_DIM)
    s = _dot_nt(q[:, sl], kb[:, sl]) * (1.0 / math.sqrt(XA_HEAD_DIM))
    m = jnp.max(s, axis=-1, keepdims=True)
    e = jnp.exp(s - m)
    return e / jnp.sum(e, axis=-1, keepdims=True)


def _attn_ln_fwd(h1, wq, wo, kb, vb, g, b):
    t = h1.shape[0]

    def body(h_ref, wq_ref, wo_ref, k_ref, v_ref, g_ref, b_ref, hn_ref, xh_ref, rs_ref, o_ref):
        h = h_ref[...]
        q = _dot(h, wq_ref[...])
        kb_ = k_ref[...]
        vb_ = v_ref[...]
        for hh in range(XA_HEADS):
            sl = slice(hh * XA_HEAD_DIM, (hh + 1) * XA_HEAD_DIM)
            p = _attn_probs(q, kb_, hh)
            o_ref[:, sl] = _dot(p, vb_[:, sl]).astype(o_ref.dtype)
        r = ALPHA * h + _dot(o_ref[...], wo_ref[...])
        out, xhat, rstd = _ln_fwd(r, g_ref[...], b_ref[...])
        hn_ref[...] = out
        xh_ref[...] = xhat
        rs_ref[...] = rstd

    return pl.pallas_call(
        body, name="attn_ln_fwd", grid=(t // TM,),
        in_specs=[_rows(TM, D_MODEL), _const((D_MODEL, D_MODEL)), _const((D_MODEL, D_MODEL)),
                  _const((MEM_LEN, D_MODEL)), _const((MEM_LEN, D_MODEL)), _const((1, D_MODEL)), _const((1, D_MODEL))],
        out_specs=[_rows(TM, D_MODEL), _rows(TM, D_MODEL), _rows(TM, 1), _rows(TM, D_MODEL)],
        out_shape=[jax.ShapeDtypeStruct((t, D_MODEL), F32), jax.ShapeDtypeStruct((t, D_MODEL), F32),
                   jax.ShapeDtypeStruct((t, 1), F32), jax.ShapeDtypeStruct((t, D_MODEL), MXU_DTYPE)],
        compiler_params=_cparams(("arbitrary",)),
    )(h1, wq, wo, kb, vb, g, b)


def _attn_ln_bwd(dh2, xhat, rstd, g, h1, wq, wo, kb, vb):
    t = h1.shape[0]

    def body(dh_ref, xh_ref, rs_ref, g_ref, h_ref, wq_ref, wo_ref, k_ref, v_ref,
             dr_ref, dq_ref, dh1_ref, dk_ref, dv_ref, dg_ref, db_ref):
        i = pl.program_id(0)

        @pl.when(i == 0)
        def _():
            dk_ref[...] = jnp.zeros_like(dk_ref)
            dv_ref[...] = jnp.zeros_like(dv_ref)
            dg_ref[...] = jnp.zeros_like(dg_ref)
            db_ref[...] = jnp.zeros_like(db_ref)

        dout = dh_ref[...]
        xh = xh_ref[...]
        dg_ref[...] += _sum0(dout * xh)
        db_ref[...] += _sum0(dout)
        dr = _ln_bwd(dout, xh, rs_ref[...], g_ref[...])
        dr_ref[...] = dr.astype(dr_ref.dtype)
        do = _dot_nt(dr, wo_ref[...])
        h = h_ref[...]
        q = _dot(h, wq_ref[...])
        kb_ = k_ref[...]
        vb_ = v_ref[...]
        scale = 1.0 / math.sqrt(XA_HEAD_DIM)
        for hh in range(XA_HEADS):
            sl = slice(hh * XA_HEAD_DIM, (hh + 1) * XA_HEAD_DIM)
            p = _attn_probs(q, kb_, hh)
            do_h = do[:, sl]
            dp = _dot_nt(do_h, vb_[:, sl])
            ds = p * (dp - jnp.sum(dp * p, axis=-1, keepdims=True)) * scale
            dq_ref[:, sl] = _dot(ds, kb_[:, sl]).astype(dq_ref.dtype)
            dk_ref[:, sl] += _dot_tn(ds, q[:, sl])
            dv_ref[:, sl] += _dot_tn(p, do_h)
        dh1_ref[...] = ALPHA * dr + _dot_nt(dq_ref[...], wq_ref[...])

    tm = TM // 2
    return pl.pallas_call(
        body, name="attn_ln_bwd", grid=(t // tm,),
        in_specs=[_rows(tm, D_MODEL), _rows(tm, D_MODEL), _rows(tm, 1), _const((1, D_MODEL)), _rows(tm, D_MODEL),
                  _const((D_MODEL, D_MODEL)), _const((D_MODEL, D_MODEL)), _const((MEM_LEN, D_MODEL)),
                  _const((MEM_LEN, D_MODEL))],
        out_specs=[_rows(tm, D_MODEL), _rows(tm, D_MODEL), _rows(tm, D_MODEL), _const((MEM_LEN, D_MODEL)),
                   _const((MEM_LEN, D_MODEL)), _const((1, D_MODEL)), _const((1, D_MODEL))],
        out_shape=[jax.ShapeDtypeStruct((t, D_MODEL), MXU_DTYPE), jax.ShapeDtypeStruct((t, D_MODEL), MXU_DTYPE),
                   jax.ShapeDtypeStruct((t, D_MODEL), F32), jax.ShapeDtypeStruct((MEM_LEN, D_MODEL), F32),
                   jax.ShapeDtypeStruct((MEM_LEN, D_MODEL), F32), jax.ShapeDtypeStruct((1, D_MODEL), F32),
                   jax.ShapeDtypeStruct((1, D_MODEL), F32)],
        compiler_params=_cparams(("arbitrary",)),
    )(dh2, xhat, rstd, g, h1, wq, wo, kb, vb)


FF_CHUNK = 1024
N_FF = D_FF // FF_CHUNK


def _load_resident(pairs, sems):
    copies = [pltpu.make_async_copy(src, dst, sems.at[k]) for k, (src, dst) in enumerate(pairs)]
    for cp in copies:
        cp.start()
    for cp in copies:
        cp.wait()


def _mlp_ln_fwd(h2, w1, w2, g, b):
    t = h2.shape[0]

    def body(h_ref, w1_hbm, w2_hbm, g_ref, b_ref, hn_ref, xh_ref, rs_ref, hd_ref, w1_v, w2_v, acc_ref, sems):
        @pl.when(pl.program_id(0) == 0)
        def _():
            _load_resident([(w1_hbm, w1_v), (w2_hbm, w2_v)], sems)

        h = h_ref[...]
        hb = h.astype(MXU_DTYPE)
        acc_ref[...] = ALPHA * h
        for j in range(N_FF):
            sl = slice(j * FF_CHUNK, (j + 1) * FF_CHUNK)
            u = _dot(hb, w1_v[:, sl])
            hd = jnp.square(jnp.maximum(u, 0.0)).astype(MXU_DTYPE)
            hd_ref[:, sl] = hd
            acc_ref[...] += _dot(hd, w2_v[sl, :])
        out, xhat, rstd = _ln_fwd(acc_ref[...], g_ref[...], b_ref[...])
        hn_ref[...] = out
        xh_ref[...] = xhat
        rs_ref[...] = rstd

    return pl.pallas_call(
        body, name="mlp_ln_fwd", grid=(t // TM,),
        in_specs=[_rows(TM, D_MODEL), _hbm(), _hbm(), _const((1, D_MODEL)), _const((1, D_MODEL))],
        out_specs=[_rows(TM, D_MODEL), _rows(TM, D_MODEL), _rows(TM, 1), _rows(TM, D_FF)],
        out_shape=[jax.ShapeDtypeStruct((t, D_MODEL), F32), jax.ShapeDtypeStruct((t, D_MODEL), F32),
                   jax.ShapeDtypeStruct((t, 1), F32), jax.ShapeDtypeStruct((t, D_FF), MXU_DTYPE)],
        scratch_shapes=[pltpu.VMEM((D_MODEL, D_FF), MXU_DTYPE), pltpu.VMEM((D_FF, D_MODEL), MXU_DTYPE),
                        pltpu.VMEM((TM, D_MODEL), F32), pltpu.SemaphoreType.DMA((2,))],
        compiler_params=_cparams(("arbitrary",)),
    )(h2, w1, w2, g, b)


def _mlp_ln_bwd(dh3, xhat, rstd, g, hdn, w1, w2):
    t = dh3.shape[0]

    def body(dh_ref, xh_ref, rs_ref, g_ref, hd_ref, w1_hbm, w2_hbm,
             dr_ref, du_ref, dh2_ref, dg_ref, db_ref, w1_v, w2_v, acc_ref, sems):
        @pl.when(pl.program_id(0) == 0)
        def _():
            _load_resident([(w1_hbm, w1_v), (w2_hbm, w2_v)], sems)
            dg_ref[...] = jnp.zeros_like(dg_ref)
            db_ref[...] = jnp.zeros_like(db_ref)

        dout = dh_ref[...]
        xh = xh_ref[...]
        dg_ref[...] += _sum0(dout * xh)
        db_ref[...] += _sum0(dout)
        dr = _ln_bwd(dout, xh, rs_ref[...], g_ref[...])
        drb = dr.astype(MXU_DTYPE)
        dr_ref[...] = drb
        acc_ref[...] = ALPHA * dr
        for j in range(N_FF):
            sl = slice(j * FF_CHUNK, (j + 1) * FF_CHUNK)
            dhd = _dot_nt(drb, w2_v[sl, :])
            du = (dhd * (2.0 * jnp.sqrt(hd_ref[:, sl].astype(F32)))).astype(MXU_DTYPE)
            du_ref[:, sl] = du
            acc_ref[...] += _dot_nt(du, w1_v[:, sl])
        dh2_ref[...] = acc_ref[...]

    tm = TM // 2
    return pl.pallas_call(
        body, name="mlp_ln_bwd", grid=(t // tm,),
        in_specs=[_rows(tm, D_MODEL), _rows(tm, D_MODEL), _rows(tm, 1), _const((1, D_MODEL)), _rows(tm, D_FF),
                  _hbm(), _hbm()],
        out_specs=[_rows(tm, D_MODEL), _rows(tm, D_FF), _rows(tm, D_MODEL), _const((1, D_MODEL)),
                   _const((1, D_MODEL))],
        out_shape=[jax.ShapeDtypeStruct((t, D_MODEL), MXU_DTYPE), jax.ShapeDtypeStruct((t, D_FF), MXU_DTYPE),
                   jax.ShapeDtypeStruct((t, D_MODEL), F32), jax.ShapeDtypeStruct((1, D_MODEL), F32),
                   jax.ShapeDtypeStruct((1, D_MODEL), F32)],
        scratch_shapes=[pltpu.VMEM((D_MODEL, D_FF), MXU_DTYPE), pltpu.VMEM((D_FF, D_MODEL), MXU_DTYPE),
                        pltpu.VMEM((tm, D_MODEL), F32), pltpu.SemaphoreType.DMA((2,))],
        compiler_params=_cparams(("arbitrary",)),
    )(dh3, xhat, rstd, g, hdn, w1, w2)


def _outproj_ln_bwd(dh1, xhat, rstd, g, w):
    t = dh1.shape[0]

    def body(dh_ref, xh_ref, rs_ref, g_ref, w_ref, dr_ref, res_ref, dy_ref, dg_ref, db_ref):
        i = pl.program_id(0)

        @pl.when(i == 0)
        def _():
            dg_ref[...] = jnp.zeros_like(dg_ref)
            db_ref[...] = jnp.zeros_like(db_ref)

        dout = dh_ref[...]
        xh = xh_ref[...]
        dg_ref[...] += _sum0(dout * xh)
        db_ref[...] += _sum0(dout)
        dr = _ln_bwd(dout, xh, rs_ref[...], g_ref[...])
        dr_ref[...] = dr.astype(dr_ref.dtype)
        res_ref[...] = ALPHA * dr
        dy_ref[...] = _dot_nt(dr, w_ref[...])

    return pl.pallas_call(
        body, name="outproj_ln_bwd", grid=(t // TM,),
        in_specs=[_rows(TM, D_MODEL), _rows(TM, D_MODEL), _rows(TM, 1), _const((1, D_MODEL)),
                  _const((D_MODEL, D_MODEL))],
        out_specs=[_rows(TM, D_MODEL), _rows(TM, D_MODEL), _rows(TM, D_MODEL), _const((1, D_MODEL)),
                   _const((1, D_MODEL))],
        out_shape=[jax.ShapeDtypeStruct((t, D_MODEL), MXU_DTYPE), jax.ShapeDtypeStruct((t, D_MODEL), F32),
                   jax.ShapeDtypeStruct((t, D_MODEL), F32), jax.ShapeDtypeStruct((1, D_MODEL), F32),
                   jax.ShapeDtypeStruct((1, D_MODEL), F32)],
        compiler_params=_cparams(("arbitrary",)),
    )(dh1, xhat, rstd, g, w)


def _loss_fwd_bwd(h, target):
    t = h.shape[0]

    def body(h_ref, t_ref, l_ref, dh_ref):
        i = pl.program_id(0)

        @pl.when(i == 0)
        def _():
            l_ref[...] = jnp.zeros_like(l_ref)

        e = h_ref[...] - t_ref[...]
        dh_ref[...] = e * (1.0 / D_MODEL)
        per_tok = jnp.mean(e * e, axis=-1, keepdims=True)
        l_ref[...] += 0.5 * jnp.sum(per_tok, axis=0, keepdims=True)

    return pl.pallas_call(
        body, name="loss_fwd_bwd", grid=(t // TM,),
        in_specs=[_rows(TM, D_MODEL), _rows(TM, D_MODEL)],
        out_specs=[_const((1, 1)), _rows(TM, D_MODEL)],
        out_shape=[jax.ShapeDtypeStruct((1, 1), F32), jax.ShapeDtypeStruct((t, D_MODEL), F32)],
        compiler_params=_cparams(("arbitrary",)),
    )(h, target)


def _pick_col(x, idx):
    lane = lax.broadcasted_iota(jnp.int32, x.shape, 1)
    return jnp.sum(jnp.where(lane == idx, x, 0.0), axis=1, keepdims=True)


def _pick_row(x, idx):
    sub = lax.broadcasted_iota(jnp.int32, x.shape, 0)
    return jnp.sum(jnp.where(sub == idx, x, 0.0), axis=0, keepdims=True)


def _conv_taps(pad_ref, w, tm, base):
    acc = w[0:1, :] * pad_ref[base:base + tm, :]
    for k in range(1, 4):
        acc = acc + w[k:k + 1, :] * pad_ref[base + k:base + k + tm, :]
    return acc


def _ssd_chunk_common(adt_c, tri):
    cs = _dot_f32(tri, adt_c)
    return cs, cs.T, jnp.exp(cs)


def _ssd_head_terms(cs, cst, ecs, dt_c, h, tri):
    cs_col = _pick_col(cs, h)
    cs_row = _pick_row(cst, h)
    dt_col = _pick_col(dt_c, h)
    cs_last = cs_col[SSD_CHUNK - 1:SSD_CHUNK, :]
    lmat = jnp.exp(jnp.where(tri > 0.0, cs_col - cs_row, -1e30))
    ecs_col = _pick_col(ecs, h)
    decay_col = jnp.exp(cs_last - cs_col)
    return cs_col, dt_col, cs_last, lmat, ecs_col, decay_col


def _ssd_fwd(proj, cw, cb, dtb, a_neg, d_lanes, nw):
    t = proj.shape[0]
    tm = SSD_FWD_TM
    nt = t // tm
    ncq = tm // SSD_CHUNK
    hb = tm // SUBLANES

    def body(xbc_ref, halo_ref, z_ref, dt_ref, cw_ref, cb_ref, dtb_ref, a_ref, d_ref, nw_ref,
             y_ref, yy_ref, st_ref, xpad, xact, state):
        i = pl.program_id(0)

        @pl.when(i == 0)
        def _():
            state[...] = jnp.zeros_like(state)

        xpad[0:SUBLANES, :] = jnp.where(i > 0, halo_ref[...], 0.0)
        xpad[SUBLANES:SUBLANES + tm, :] = xbc_ref[...]
        acc = cb_ref[...] + _conv_taps(xpad, cw_ref[...], tm, SUBLANES - 3)
        xact[...] = acc * _sigmoid(acc)
        dt = _softplus(dt_ref[...] + dtb_ref[...])
        adt = dt * a_ref[...]
        r_i = lax.broadcasted_iota(jnp.int32, (SSD_CHUNK, SSD_CHUNK), 0)
        c_i = lax.broadcasted_iota(jnp.int32, (SSD_CHUNK, SSD_CHUNK), 1)
        tri = (r_i >= c_i).astype(F32)
        lane1 = lax.broadcasted_iota(jnp.int32, (1, LANES), 1)
        for c in range(ncq):
            sl = slice(c * SSD_CHUNK, (c + 1) * SSD_CHUNK)
            dt_c = dt[sl]
            cs, cst, ecs = _ssd_chunk_common(adt[sl], tri)
            for g in range(2):
                bg = xact[sl, 512 + g * 128:512 + (g + 1) * 128]
                cg = xact[sl, 768 + g * 128:768 + (g + 1) * 128]
                cbm = _dot_nt(cg, bg)
                for pr in range(2):
                    pi = g * 2 + pr
                    psl = slice(pi * 128, (pi + 1) * 128)
                    xp = xact[sl, psl]
                    prev = state[pi]
                    st_ref[c, pi] = prev
                    yp = xp * d_ref[:, psl]
                    new_s = jnp.zeros((SSD_STATE, LANES), F32)
                    dec_lane = jnp.zeros((1, LANES), F32)
                    for hh in range(2):
                        h = g * 4 + pr * 2 + hh
                        lm = (lane1 >= 64) if hh else (lane1 < 64)
                        _, dt_col, cs_last, lmat, ecs_col, decay_col = _ssd_head_terms(cs, cst, ecs, dt_c, h, tri)
                        xdt = jnp.where(lm, xp, 0.0) * dt_col
                        yp = yp + _dot(cbm * lmat, xdt)
                        yp = yp + _dot(cg * ecs_col, jnp.where(lm, prev, 0.0))
                        new_s = new_s + _dot_tn(bg * decay_col, xdt)
                        dec_lane = dec_lane + jnp.where(lm, jnp.exp(cs_last), 0.0)
                    state[pi] = prev * dec_lane + new_s
                    yy_ref[sl, psl] = yp
        yy = yy_ref[...]
        z = z_ref[...]
        yg = yy * (z * _sigmoid(z))
        ms = jnp.mean(yg * yg, axis=-1, keepdims=True)
        y_ref[...] = (yg * lax.rsqrt(ms + LN_EPS) * nw_ref[...]).astype(y_ref.dtype)

    halo_map = lambda i: (jnp.maximum(i * hb - 1, 0), 0)
    return pl.pallas_call(
        body, name="ssd_fwd", grid=(nt,),
        in_specs=[pl.BlockSpec((tm, SSD_XBC), lambda i: (i, 0)), pl.BlockSpec((SUBLANES, SSD_XBC), halo_map),
                  pl.BlockSpec((tm, SSD_WIDTH), lambda i: (i, P_Z // SSD_WIDTH)),
                  pl.BlockSpec((tm, LANES), lambda i: (i, P_DT // LANES)),
                  _const((4, SSD_XBC)), _const((1, SSD_XBC)), _const((1, LANES)), _const((1, LANES)),
                  _const((1, SSD_WIDTH)), _const((1, SSD_WIDTH))],
        out_specs=[_rows(tm, SSD_WIDTH), _rows(tm, SSD_WIDTH),
                   pl.BlockSpec((ncq, 4, SSD_STATE, LANES), lambda i: (i, 0, 0, 0))],
        out_shape=[jax.ShapeDtypeStruct((t, SSD_WIDTH), MXU_DTYPE), jax.ShapeDtypeStruct((t, SSD_WIDTH), F32),
                   jax.ShapeDtypeStruct((t // SSD_CHUNK, 4, SSD_STATE, LANES), F32)],
        scratch_shapes=[pltpu.VMEM((tm + SUBLANES, SSD_XBC), F32), pltpu.VMEM((tm, SSD_XBC), F32),
                        pltpu.VMEM((4, SSD_STATE, LANES), F32)],
        compiler_params=_cparams(("arbitrary",)),
    )(proj, proj, proj, proj, cw, cb, dtb, a_neg, d_lanes, nw)


def _ssd_bwd(dycat, proj, yy, states, cw, cb, dtb, a_neg, d_lanes, nw):
    t = proj.shape[0]
    tm = SSD_BWD_TM
    nt = t // tm
    ncq = tm // SSD_CHUNK
    hb = tm // SUBLANES

    def body(dy_ref, xbc_ref, halo_ref, z_ref, dt_ref, yy_ref, st_ref, cw_ref, cb_ref, dtb_ref, a_ref, d_ref, nw_ref,
             dxbc_ref, dz_ref, ddt_ref, dcw_ref, dcb_ref, ddtb_ref, da_ref, dd_ref, dnw_ref,
             xpad, xact, dxact, dpad, dstate, dnext):
        i = pl.program_id(0)

        @pl.when(i == 0)
        def _():
            for r in (dcw_ref, dcb_ref, ddtb_ref, da_ref, dd_ref, dnw_ref, dstate, dnext):
                r[...] = jnp.zeros_like(r)

        xpad[0:SUBLANES, :] = jnp.where(i < nt - 1, halo_ref[...], 0.0)
        xpad[SUBLANES:SUBLANES + tm, :] = xbc_ref[...]
        cw_v = cw_ref[...]
        acc = cb_ref[...] + _conv_taps(xpad, cw_v, tm, SUBLANES - 3)
        sig = _sigmoid(acc)
        xact[...] = acc * sig
        dt_raw = dt_ref[...] + dtb_ref[...]
        dt = _softplus(dt_raw)
        a_v = a_ref[...]
        adt = dt * a_v
        yy = yy_ref[...]
        z = z_ref[...]
        sz = _sigmoid(z)
        siluz = z * sz
        yg = yy * siluz
        ms = jnp.mean(yg * yg, axis=-1, keepdims=True)
        rinv = lax.rsqrt(ms + LN_EPS)
        dout = dy_ref[...]
        dnw_ref[...] += _sum0(dout * yg * rinv)
        dyn = dout * nw_ref[...]
        dyg = rinv * dyn - yg * (rinv * rinv * rinv) * jnp.mean(dyn * yg, axis=-1, keepdims=True)
        dyy = dyg * siluz
        dz_ref[...] = (dyg * yy * (sz * (1.0 + z * (1.0 - sz)))).astype(dz_ref.dtype)
        dd_ref[...] += _sum0(dyy * xact[:, 0:SSD_WIDTH])

        r_i = lax.broadcasted_iota(jnp.int32, (SSD_CHUNK, SSD_CHUNK), 0)
        c_i = lax.broadcasted_iota(jnp.int32, (SSD_CHUNK, SSD_CHUNK), 1)
        tri = (r_i >= c_i).astype(F32)
        lane1 = lax.broadcasted_iota(jnp.int32, (1, LANES), 1)
        for c in reversed(range(ncq)):
            sl = slice(c * SSD_CHUNK, (c + 1) * SSD_CHUNK)
            dt_c = dt[sl]
            cs, cst, ecs = _ssd_chunk_common(adt[sl], tri)
            cacc = jnp.zeros((SSD_CHUNK, LANES), F32)
            racc = jnp.zeros((SSD_CHUNK, LANES), F32)
            ddtx = jnp.zeros((SSD_CHUNK, LANES), F32)
            for g in range(2):
                bg = xact[sl, 512 + g * 128:512 + (g + 1) * 128]
                cg = xact[sl, 768 + g * 128:768 + (g + 1) * 128]
                cbm = _dot_nt(cg, bg)
                dcb_m = jnp.zeros((SSD_CHUNK, SSD_CHUNK), F32)
                dbg = jnp.zeros((SSD_CHUNK, SSD_STATE), F32)
                dcg = jnp.zeros((SSD_CHUNK, SSD_STATE), F32)
                for pr in range(2):
                    pi = g * 2 + pr
                    psl = slice(pi * 128, (pi + 1) * 128)
                    xp = xact[sl, psl]
                    dyp = dyy[sl, psl]
                    prev = st_ref[c, pi]
                    ds_all = dstate[pi]
                    dxdt_p = jnp.zeros((SSD_CHUNK, LANES), F32)
                    dprev_new = jnp.zeros((SSD_STATE, LANES), F32)
                    dec_lane = jnp.zeros((1, LANES), F32)
                    dt_lanes = jnp.zeros((SSD_CHUNK, LANES), F32)
                    for hh in range(2):
                        h = g * 4 + pr * 2 + hh
                        lm = (lane1 >= 64) if hh else (lane1 < 64)
                        oh_l = (c_i == h).astype(F32)
                        oh_s = (r_i == h).astype(F32)
                        _, dt_col, cs_last, lmat, ecs_col, decay_col = _ssd_head_terms(cs, cst, ecs, dt_c, h, tri)
                        gm = cbm * lmat
                        xm = jnp.where(lm, xp, 0.0)
                        xdt = xm * dt_col
                        dym = jnp.where(lm, dyp, 0.0)
                        prevm = jnp.where(lm, prev, 0.0)
                        dsm = jnp.where(lm, ds_all, 0.0)
                        bdec = bg * decay_col
                        dxdt = _dot_tn(gm, dym) + _dot(bdec, dsm)
                        dxdt_p = dxdt_p + dxdt
                        ddtx = ddtx + oh_l * jnp.sum(dxdt * xm, axis=1, keepdims=True)
                        dt_lanes = dt_lanes + jnp.where(lm, dt_col, 0.0)
                        dgm = _dot_nt(dym, xdt)
                        dcb_m = dcb_m + dgm * lmat
                        w = dgm * gm
                        cacc = cacc + oh_l * jnp.sum(w, axis=1, keepdims=True)
                        racc = racc - oh_s * jnp.sum(w, axis=0, keepdims=True)
                        dce = _dot_nt(dym, prevm)
                        dcg = dcg + dce * ecs_col
                        cacc = cacc + oh_l * (jnp.sum(dce * cg, axis=1, keepdims=True) * ecs_col)
                        dprev_new = dprev_new + _dot_tn(cg * ecs_col, dym)
                        dbdec = _dot_nt(xdt, dsm)
                        dbg = dbg + dbdec * decay_col
                        dd = jnp.sum(dbdec * bg, axis=1, keepdims=True) * decay_col
                        cacc = cacc - oh_l * dd
                        cd = jnp.exp(cs_last)
                        dlast = jnp.sum(dd, axis=0, keepdims=True) + jnp.sum(
                            jnp.sum(dsm * prevm, axis=1, keepdims=True), axis=0, keepdims=True) * cd
                        cacc = cacc + jnp.where((r_i == SSD_CHUNK - 1) & (c_i == h), dlast, 0.0)
                        dec_lane = dec_lane + jnp.where(lm, cd, 0.0)
                    dstate[pi] = ds_all * dec_lane + dprev_new
                    dxact[sl, psl] = dxdt_p * dt_lanes + dyp * d_ref[:, psl]
                dcg = dcg + _dot(dcb_m, bg)
                dbg = dbg + _dot_tn(dcb_m, cg)
                dxact[sl, 512 + g * 128:512 + (g + 1) * 128] = dbg
                dxact[sl, 768 + g * 128:768 + (g + 1) * 128] = dcg
            dcs = cacc + racc.T
            dadt = _dot_f32((r_i <= c_i).astype(F32), dcs)
            ddt = dadt * a_v + ddtx
            da_ref[...] += _sum0(dadt * dt_c)
            ddt_raw = ddt * _sigmoid(dt_raw[sl])
            ddt_ref[sl, :] = ddt_raw.astype(ddt_ref.dtype)
            ddtb_ref[...] += _sum0(ddt_raw)
        dacc = dxact[...] * (sig * (1.0 + acc * (1.0 - sig)))
        dcb_ref[...] += _sum0(dacc)
        for k in range(4):
            dcw_ref[k:k + 1, :] += _sum0(dacc * xpad[SUBLANES - 3 + k:SUBLANES - 3 + k + tm, :])
        dpad[0:tm, :] = dacc
        dpad[tm:tm + SUBLANES, :] = dnext[...]
        dx = cw_v[0:1, :] * dpad[3:3 + tm, :]
        for k in range(1, 4):
            dx = dx + cw_v[k:k + 1, :] * dpad[3 - k:3 - k + tm, :]
        dxbc_ref[...] = dx.astype(dxbc_ref.dtype)
        dnext[...] = dacc[0:SUBLANES, :]

    rev = lambda i: nt - 1 - i
    halo_map = lambda i: (jnp.maximum(rev(i) * hb - 1, 0), 0)
    rrow = lambda n, col=0: pl.BlockSpec((tm, n), lambda i: (rev(i), col))
    return pl.pallas_call(
        body, name="ssd_bwd", grid=(nt,),
        in_specs=[rrow(SSD_WIDTH), rrow(SSD_XBC), pl.BlockSpec((SUBLANES, SSD_XBC), halo_map),
                  rrow(SSD_WIDTH, P_Z // SSD_WIDTH), rrow(LANES, P_DT // LANES), rrow(SSD_WIDTH),
                  pl.BlockSpec((ncq, 4, SSD_STATE, LANES), lambda i: (rev(i), 0, 0, 0)),
                  _const((4, SSD_XBC)), _const((1, SSD_XBC)), _const((1, LANES)), _const((1, LANES)),
                  _const((1, SSD_WIDTH)), _const((1, SSD_WIDTH))],
        out_specs=[rrow(SSD_XBC), rrow(SSD_WIDTH), rrow(LANES), _const((SUBLANES, SSD_XBC)), _const((1, SSD_XBC)),
                   _const((1, LANES)), _const((1, LANES)), _const((1, SSD_WIDTH)), _const((1, SSD_WIDTH))],
        out_shape=[jax.ShapeDtypeStruct((t, SSD_XBC), MXU_DTYPE), jax.ShapeDtypeStruct((t, SSD_WIDTH), MXU_DTYPE),
                   jax.ShapeDtypeStruct((t, LANES), MXU_DTYPE), jax.ShapeDtypeStruct((SUBLANES, SSD_XBC), F32),
                   jax.ShapeDtypeStruct((1, SSD_XBC), F32), jax.ShapeDtypeStruct((1, LANES), F32),
                   jax.ShapeDtypeStruct((1, LANES), F32), jax.ShapeDtypeStruct((1, SSD_WIDTH), F32),
                   jax.ShapeDtypeStruct((1, SSD_WIDTH), F32)],
        scratch_shapes=[pltpu.VMEM((tm + SUBLANES, SSD_XBC), F32), pltpu.VMEM((tm, SSD_XBC), F32),
                        pltpu.VMEM((tm, SSD_XBC), F32), pltpu.VMEM((tm + SUBLANES, SSD_XBC), F32),
                        pltpu.VMEM((4, SSD_STATE, LANES), F32), pltpu.VMEM((SUBLANES, SSD_XBC), F32)],
        compiler_params=_cparams(("arbitrary",)),
    )(dycat, proj, proj, proj, proj, yy, states, cw, cb, dtb, a_neg, d_lanes, nw)


def _cmul_add(ar, ai, br, bi, cr, ci):
    return ar + br * cr - bi * ci, ai + br * ci + bi * cr


def _s5_fwd(proj, bre, bim, cre, cim, d_skip, glu_w, glu_b, coef):
    t = proj.shape[0]
    tm = SCAN_TM
    ng = tm // SUBLANES

    def body(u_ref, bre_ref, bim_ref, cre_ref, cim_ref, d_ref, w_ref, b_ref, coef_ref,
             y_ref, y2_ref, hre_ref, him_ref, carry):
        i = pl.program_id(0)

        @pl.when(i == 0)
        def _():
            carry[...] = jnp.zeros_like(carry)

        u = u_ref[...]
        hre_ref[...] = _dot(u, bre_ref[...])
        him_ref[...] = _dot(u, bim_ref[...])

        def step(gi, car):
            cr_, ci_ = car
            rows = pl.ds(pl.multiple_of(gi * SUBLANES, SUBLANES), SUBLANES)
            r = hre_ref[rows, :]
            m = him_ref[rows, :]
            for k, sh in enumerate((1, 2, 4)):
                r, m = _cmul_add(r, m, coef_ref[k, 0], coef_ref[k, 1], pltpu.roll(r, sh, 0), pltpu.roll(m, sh, 0))
            r, m = _cmul_add(r, m, coef_ref[3, 0], coef_ref[3, 1], cr_, ci_)
            hre_ref[rows, :] = r
            him_ref[rows, :] = m
            return (jnp.broadcast_to(r[SUBLANES - 1:SUBLANES, :], r.shape),
                    jnp.broadcast_to(m[SUBLANES - 1:SUBLANES, :], m.shape))

        cr_, ci_ = lax.fori_loop(0, ng, step, (carry[0], carry[1]))
        carry[0] = cr_
        carry[1] = ci_
        y2 = _dot(hre_ref[...], cre_ref[...]) - _dot(him_ref[...], cim_ref[...]) + d_ref[...] * u
        y2_ref[...] = y2
        ya = _gelu(y2)
        y_ref[...] = (ya * _sigmoid(_dot(ya, w_ref[...]) + b_ref[...])).astype(y_ref.dtype)

    return pl.pallas_call(
        body, name="s5_fwd", grid=(t // tm,),
        in_specs=[pl.BlockSpec((tm, S5_WIDTH), lambda i: (i, P_U // S5_WIDTH)),
                  _const((S5_WIDTH, S5_NSTATE)), _const((S5_WIDTH, S5_NSTATE)), _const((S5_NSTATE, S5_WIDTH)),
                  _const((S5_NSTATE, S5_WIDTH)), _const((1, S5_WIDTH)), _const((S5_WIDTH, S5_WIDTH)),
                  _const((1, S5_WIDTH)), _const((5, 2, SUBLANES, S5_NSTATE))],
        out_specs=[_rows(tm, S5_WIDTH), _rows(tm, S5_WIDTH), _rows(tm, S5_NSTATE), _rows(tm, S5_NSTATE)],
        out_shape=[jax.ShapeDtypeStruct((t, S5_WIDTH), MXU_DTYPE), jax.ShapeDtypeStruct((t, S5_WIDTH), F32),
                   jax.ShapeDtypeStruct((t, S5_NSTATE), F32), jax.ShapeDtypeStruct((t, S5_NSTATE), F32)],
        scratch_shapes=[pltpu.VMEM((2, SUBLANES, S5_NSTATE), F32)],
        compiler_params=_cparams(("arbitrary",)),
    )(proj, bre, bim, cre, cim, d_skip, glu_w, glu_b, coef)


def _s5_bwd(dycat, proj, y2, hre, him, bre, bim, cre, cim, d_skip, glu_w, glu_b, rcoef):
    t = proj.shape[0]
    tm = SCAN_TM
    nt = t // tm
    ng = tm // SUBLANES
    hb = tm // SUBLANES

    def body(dy_ref, u_ref, y2_ref, hre_ref, him_ref, hre_halo, him_halo, bre_ref, bim_ref, cre_ref, cim_ref, d_ref,
             w_ref, b_ref, coef_ref,
             du_ref, dbre_ref, dbim_ref, dcre_ref, dcim_ref, dlam_ref, dd_ref, dw_ref, dgb_ref,
             gre, gim, hpre, hpim, carry):
        i = pl.program_id(0)

        @pl.when(i == 0)
        def _():
            for r in (dbre_ref, dbim_ref, dcre_ref, dcim_ref, dlam_ref, dd_ref, dw_ref, dgb_ref, carry):
                r[...] = jnp.zeros_like(r)

        u = u_ref[...]
        y2 = y2_ref[...]
        dout = dy_ref[...]
        ya = _gelu(y2)
        sg = _sigmoid(_dot(ya, w_ref[...]) + b_ref[...])
        dv = dout * ya * sg * (1.0 - sg)
        dya = dout * sg + _dot_nt(dv, w_ref[...])
        dw_ref[...] += _dot_tn(ya, dv)
        dgb_ref[...] += _sum0(dv)
        dy2 = dya * _gelu_grad(y2)
        dd_ref[...] += _sum0(dy2 * u)
        hre_v = hre_ref[...]
        him_v = him_ref[...]
        dcre_ref[...] += _dot_tn(hre_v, dy2)
        dcim_ref[...] -= _dot_tn(him_v, dy2)
        gre[...] = _dot_nt(dy2, cre_ref[...])
        gim[...] = -_dot_nt(dy2, cim_ref[...])
        first = i == nt - 1
        hpre[0:SUBLANES, :] = jnp.where(first, 0.0, hre_halo[...])
        hpim[0:SUBLANES, :] = jnp.where(first, 0.0, him_halo[...])
        hpre[SUBLANES:SUBLANES + tm, :] = hre_v
        hpim[SUBLANES:SUBLANES + tm, :] = him_v
        row0 = lax.broadcasted_iota(jnp.int32, (SUBLANES, S5_NSTATE), 0) == 0

        def step(k, car):
            cr_, ci_, dlr, dli = car
            gi = ng - 1 - k
            rows = pl.ds(pl.multiple_of(gi * SUBLANES, SUBLANES), SUBLANES)
            nrows = pl.ds(pl.multiple_of(gi * SUBLANES + SUBLANES, SUBLANES), SUBLANES)
            r = gre[rows, :]
            m = gim[rows, :]
            for kk, sh in enumerate((1, 2, 4)):
                r, m = _cmul_add(r, m, coef_ref[kk, 0], coef_ref[kk, 1], pltpu.roll(r, SUBLANES - sh, 0),
                                 pltpu.roll(m, SUBLANES - sh, 0))
            r, m = _cmul_add(r, m, coef_ref[3, 0], coef_ref[3, 1], cr_, ci_)
            gre[rows, :] = r
            gim[rows, :] = m
            pr_ = hpre[rows, :]
            pm_ = hpim[rows, :]
            hr_ = jnp.where(row0, jnp.broadcast_to(pr_[SUBLANES - 1:SUBLANES, :], pr_.shape),
                            pltpu.roll(hpre[nrows, :], 1, 0))
            hm_ = jnp.where(row0, jnp.broadcast_to(pm_[SUBLANES - 1:SUBLANES, :], pm_.shape),
                            pltpu.roll(hpim[nrows, :], 1, 0))
            dlr = dlr + hr_ * r + hm_ * m
            dli = dli + hr_ * m - hm_ * r
            return (jnp.broadcast_to(r[0:1, :], r.shape), jnp.broadcast_to(m[0:1, :], m.shape), dlr, dli)

        z8 = jnp.zeros((SUBLANES, S5_NSTATE), F32)
        cr_, ci_, dlr, dli = lax.fori_loop(0, ng, step, (carry[0], carry[1], z8, z8))
        carry[0] = cr_
        carry[1] = ci_
        dlam_ref[0] += dlr
        dlam_ref[1] += dli
        g_re = gre[...]
        g_im = gim[...]
        du_ref[...] = (dy2 * d_ref[...] + _dot_nt(g_re, bre_ref[...]) + _dot_nt(g_im, bim_ref[...])
                       ).astype(du_ref.dtype)
        dbre_ref[...] += _dot_tn(u, g_re)
        dbim_ref[...] += _dot_tn(u, g_im)

    rev = lambda i: nt - 1 - i
    rrow = lambda n, col=0: pl.BlockSpec((tm, n), lambda i: (rev(i), col))
    halo = pl.BlockSpec((SUBLANES, S5_NSTATE), lambda i: (jnp.maximum(rev(i) * hb - 1, 0), 0))
    return pl.pallas_call(
        body, name="s5_bwd", grid=(nt,),
        in_specs=[rrow(S5_WIDTH, 512 // S5_WIDTH), rrow(S5_WIDTH, P_U // S5_WIDTH), rrow(S5_WIDTH),
                  rrow(S5_NSTATE), rrow(S5_NSTATE), halo, halo,
                  _const((S5_WIDTH, S5_NSTATE)), _const((S5_WIDTH, S5_NSTATE)), _const((S5_NSTATE, S5_WIDTH)),
                  _const((S5_NSTATE, S5_WIDTH)), _const((1, S5_WIDTH)), _const((S5_WIDTH, S5_WIDTH)),
                  _const((1, S5_WIDTH)), _const((5, 2, SUBLANES, S5_NSTATE))],
        out_specs=[rrow(S5_WIDTH), _const((S5_WIDTH, S5_NSTATE)), _const((S5_WIDTH, S5_NSTATE)),
                   _const((S5_NSTATE, S5_WIDTH)), _const((S5_NSTATE, S5_WIDTH)), _const((2, SUBLANES, S5_NSTATE)),
                   _const((1, S5_WIDTH)), _const((S5_WIDTH, S5_WIDTH)), _const((1, S5_WIDTH))],
        out_shape=[jax.ShapeDtypeStruct((t, S5_WIDTH), MXU_DTYPE), jax.ShapeDtypeStruct((S5_WIDTH, S5_NSTATE), F32),
                   jax.ShapeDtypeStruct((S5_WIDTH, S5_NSTATE), F32), jax.ShapeDtypeStruct((S5_NSTATE, S5_WIDTH), F32),
                   jax.ShapeDtypeStruct((S5_NSTATE, S5_WIDTH), F32),
                   jax.ShapeDtypeStruct((2, SUBLANES, S5_NSTATE), F32), jax.ShapeDtypeStruct((1, S5_WIDTH), F32),
                   jax.ShapeDtypeStruct((S5_WIDTH, S5_WIDTH), F32), jax.ShapeDtypeStruct((1, S5_WIDTH), F32)],
        scratch_shapes=[pltpu.VMEM((tm, S5_NSTATE), F32), pltpu.VMEM((tm, S5_NSTATE), F32),
                        pltpu.VMEM((tm + SUBLANES, S5_NSTATE), F32), pltpu.VMEM((tm + SUBLANES, S5_NSTATE), F32),
                        pltpu.VMEM((2, SUBLANES, S5_NSTATE), F32)],
        compiler_params=_cparams(("arbitrary",)),
    )(dycat, proj, y2, hre, him, hre, him, bre, bim, cre, cim, d_skip, glu_w, glu_b, rcoef)


def _rg_gates(xc, wa, ba, wx, bx, nsp):
    r = _sigmoid(_dot(xc, wa) + ba)
    ig = _sigmoid(_dot(xc, wx) + bx)
    log_a = nsp * r
    a = jnp.exp(log_a)
    mult = jnp.sqrt(-_expm1(2.0 * log_a))
    return r, ig, a, mult


def _rg_fwd(proj, cw, cb, wa, ba, wx, bx, nsp):
    t = proj.shape[0]
    tm = SCAN_TM
    ng = tm // SUBLANES
    hb = tm // SUBLANES

    def body(x_ref, halo_ref, gt_ref, cw_ref, cb_ref, wa_ref, ba_ref, wx_ref, bx_ref, nsp_ref,
             y_ref, h_ref, xpad, abuf, carry):
        i = pl.program_id(0)

        @pl.when(i == 0)
        def _():
            carry[...] = jnp.zeros_like(carry)

        xpad[0:SUBLANES, :] = jnp.where(i > 0, halo_ref[...], 0.0)
        xpad[SUBLANES:SUBLANES + tm, :] = x_ref[...]
        xc = cb_ref[...] + _conv_taps(xpad, cw_ref[...], tm, SUBLANES - 3)
        _, ig, a, mult = _rg_gates(xc, wa_ref[...], ba_ref[...], wx_ref[...], bx_ref[...], nsp_ref[...])
        abuf[...] = a
        h_ref[...] = mult * (ig * xc)
        sub = lax.broadcasted_iota(jnp.int32, (SUBLANES, RG_WIDTH), 0)

        def step(gi, car):
            rows = pl.ds(pl.multiple_of(gi * SUBLANES, SUBLANES), SUBLANES)
            av = abuf[rows, :]
            bv = h_ref[rows, :]
            for sh in (1, 2, 4):
                m = sub >= sh
                bv = jnp.where(m, av * pltpu.roll(bv, sh, 0) + bv, bv)
                av = jnp.where(m, av * pltpu.roll(av, sh, 0), av)
            hv = bv + av * car
            h_ref[rows, :] = hv
            return jnp.broadcast_to(hv[SUBLANES - 1:SUBLANES, :], hv.shape)

        carry[...] = lax.fori_loop(0, ng, step, carry[...])
        y_ref[...] = (h_ref[...] * _gelu(gt_ref[...])).astype(y_ref.dtype)

    return pl.pallas_call(
        body, name="rg_fwd", grid=(t // tm,),
        in_specs=[pl.BlockSpec((tm, RG_WIDTH), lambda i: (i, P_XRG // RG_WIDTH)),
                  pl.BlockSpec((SUBLANES, RG_WIDTH), lambda i: (jnp.maximum(i * hb - 1, 0), P_XRG // RG_WIDTH)),
                  pl.BlockSpec((tm, RG_WIDTH), lambda i: (i, P_GRG // RG_WIDTH)),
                  _const((4, RG_WIDTH)), _const((1, RG_WIDTH)), _const((RG_WIDTH, RG_WIDTH)), _const((1, RG_WIDTH)),
                  _const((RG_WIDTH, RG_WIDTH)), _const((1, RG_WIDTH)), _const((1, RG_WIDTH))],
        out_specs=[_rows(tm, RG_WIDTH), _rows(tm, RG_WIDTH)],
        out_shape=[jax.ShapeDtypeStruct((t, RG_WIDTH), MXU_DTYPE), jax.ShapeDtypeStruct((t, RG_WIDTH), F32)],
        scratch_shapes=[pltpu.VMEM((tm + SUBLANES, RG_WIDTH), F32), pltpu.VMEM((tm, RG_WIDTH), F32),
                        pltpu.VMEM((SUBLANES, RG_WIDTH), F32)],
        compiler_params=_cparams(("arbitrary",)),
    )(proj, proj, proj, cw, cb, wa, ba, wx, bx, nsp)


def _rg_bwd(dycat, proj, hs, cw, cb, wa, ba, wx, bx, nsp):
    t = proj.shape[0]
    tm = SCAN_TM
    nt = t // tm
    ng = tm // SUBLANES
    hb = tm // SUBLANES

    def body(dy_ref, x_ref, halo_ref, gt_ref, h_ref, h_halo, cw_ref, cb_ref, wa_ref, ba_ref, wx_ref, bx_ref, nsp_ref,
             dx_ref, dgt_ref, dcw_ref, dcb_ref, dwa_ref, dba_ref, dwx_ref, dbx_ref, dnsp_ref,
             xpad, abuf, gbuf, hpad, dabuf, dpad, carry, dnext):
        i = pl.program_id(0)

        @pl.when(i == 0)
        def _():
            for r in (dcw_ref, dcb_ref, dwa_ref, dba_ref, dwx_ref, dbx_ref, dnsp_ref, carry, dnext):
                r[...] = jnp.zeros_like(r)

        first = i == nt - 1
        xpad[0:SUBLANES, :] = jnp.where(first, 0.0, halo_ref[...])
        xpad[SUBLANES:SUBLANES + tm, :] = x_ref[...]
        cw_v = cw_ref[...]
        xc = cb_ref[...] + _conv_taps(xpad, cw_v, tm, SUBLANES - 3)
        nsp_v = nsp_ref[...]
        r, ig, a, mult = _rg_gates(xc, wa_ref[...], ba_ref[...], wx_ref[...], bx_ref[...], nsp_v)
        abuf[...] = a
        hv = h_ref[...]
        hpad[0:SUBLANES, :] = jnp.where(first, 0.0, h_halo[...])
        hpad[SUBLANES:SUBLANES + tm, :] = hv
        gt = gt_ref[...]
        dout = dy_ref[...]
        dgt_ref[...] = (dout * hv * _gelu_grad(gt)).astype(dgt_ref.dtype)
        gbuf[...] = dout * _gelu(gt)
        sub = lax.broadcasted_iota(jnp.int32, (SUBLANES, RG_WIDTH), 0)
        last_row = sub == SUBLANES - 1
        row0 = sub == 0

        def step(k, car):
            gi = ng - 1 - k
            rows = pl.ds(pl.multiple_of(gi * SUBLANES, SUBLANES), SUBLANES)
            nrows = pl.ds(pl.multiple_of(gi * SUBLANES + SUBLANES, SUBLANES), SUBLANES)
            av = abuf[rows, :]
            bv = gbuf[rows, :] + jnp.where(last_row, car, 0.0)
            ev = jnp.where(last_row, 0.0, pltpu.roll(av, SUBLANES - 1, 0))
            for sh in (1, 2, 4):
                m = sub < SUBLANES - sh
                bv = jnp.where(m, bv + ev * pltpu.roll(bv, SUBLANES - sh, 0), bv)
                ev = jnp.where(m, ev * pltpu.roll(ev, SUBLANES - sh, 0), 0.0)
            gbuf[rows, :] = bv
            pv = hpad[rows, :]
            hprev = jnp.where(row0, jnp.broadcast_to(pv[SUBLANES - 1:SUBLANES, :], pv.shape),
                              pltpu.roll(hpad[nrows, :], 1, 0))
            dabuf[rows, :] = bv * hprev
            return jnp.broadcast_to((av * bv)[0:1, :], bv.shape)

        carry[...] = lax.fori_loop(0, ng, step, carry[...])
        gv = gbuf[...]
        da = dabuf[...]
        ix = ig * xc
        dmult = gv * ix
        dig = gv * mult * xc
        dxc = gv * mult * ig
        dlog_a = da * a - dmult * (a * a) / mult
        dnsp_ref[...] += _sum0(dlog_a * r)
        dpr = dlog_a * nsp_v * r * (1.0 - r)
        dpi = dig * ig * (1.0 - ig)
        dxc = dxc + _dot_nt(dpr, wa_ref[...]) + _dot_nt(dpi, wx_ref[...])
        dwa_ref[...] += _dot_tn(xc, dpr)
        dwx_ref[...] += _dot_tn(xc, dpi)
        dba_ref[...] += _sum0(dpr)
        dbx_ref[...] += _sum0(dpi)
        dcb_ref[...] += _sum0(dxc)
        for k in range(4):
            dcw_ref[k:k + 1, :] += _sum0(dxc * xpad[SUBLANES - 3 + k:SUBLANES - 3 + k + tm, :])
        dpad[0:tm, :] = dxc
        dpad[tm:tm + SUBLANES, :] = dnext[...]
        dx = cw_v[0:1, :] * dpad[3:3 + tm, :]
        for k in range(1, 4):
            dx = dx + cw_v[k:k + 1, :] * dpad[3 - k:3 - k + tm, :]
        dx_ref[...] = dx.astype(dx_ref.dtype)
        dnext[...] = dxc[0:SUBLANES, :]

    rev = lambda i: nt - 1 - i
    rrow = lambda n, col=0: pl.BlockSpec((tm, n), lambda i: (rev(i), col))
    sq = _const((RG_WIDTH, RG_WIDTH))
    vec = _const((1, RG_WIDTH))
    return pl.pallas_call(
        body, name="rg_bwd", grid=(nt,),
        in_specs=[rrow(RG_WIDTH, 768 // RG_WIDTH), rrow(RG_WIDTH, P_XRG // RG_WIDTH),
                  pl.BlockSpec((SUBLANES, RG_WIDTH), lambda i: (jnp.maximum(rev(i) * hb - 1, 0), P_XRG // RG_WIDTH)),
                  rrow(RG_WIDTH, P_GRG // RG_WIDTH), rrow(RG_WIDTH),
                  pl.BlockSpec((SUBLANES, RG_WIDTH), lambda i: (jnp.maximum(rev(i) * hb - 1, 0), 0)),
                  _const((4, RG_WIDTH)), vec, sq, vec, sq, vec, vec],
        out_specs=[rrow(RG_WIDTH), rrow(RG_WIDTH), _const((SUBLANES, RG_WIDTH)), vec, sq, vec, sq, vec, vec],
        out_shape=[jax.ShapeDtypeStruct((t, RG_WIDTH), MXU_DTYPE), jax.ShapeDtypeStruct((t, RG_WIDTH), MXU_DTYPE),
                   jax.ShapeDtypeStruct((SUBLANES, RG_WIDTH), F32), jax.ShapeDtypeStruct((1, RG_WIDTH), F32),
                   jax.ShapeDtypeStruct((RG_WIDTH, RG_WIDTH), F32), jax.ShapeDtypeStruct((1, RG_WIDTH), F32),
                   jax.ShapeDtypeStruct((RG_WIDTH, RG_WIDTH), F32), jax.ShapeDtypeStruct((1, RG_WIDTH), F32),
                   jax.ShapeDtypeStruct((1, RG_WIDTH), F32)],
        scratch_shapes=[pltpu.VMEM((tm + SUBLANES, RG_WIDTH), F32), pltpu.VMEM((tm, RG_WIDTH), F32),
                        pltpu.VMEM((tm, RG_WIDTH), F32), pltpu.VMEM((tm + SUBLANES, RG_WIDTH), F32),
                        pltpu.VMEM((tm, RG_WIDTH), F32), pltpu.VMEM((tm + SUBLANES, RG_WIDTH), F32),
                        pltpu.VMEM((SUBLANES, RG_WIDTH), F32), pltpu.VMEM((SUBLANES, RG_WIDTH), F32)],
        compiler_params=_cparams(("arbitrary",)),
    )(dycat, proj, proj, proj, hs, hs, cw, cb, wa, ba, wx, bx, nsp)


def _block_diag(blocks):
    g, a, b = blocks.shape
    eye = jnp.eye(g, dtype=blocks.dtype)
    return (eye[:, None, :, None] * blocks[:, :, None, :]).reshape(g * a, g * b)


def _block_diag_extract(m, g):
    a, b = m.shape[0] // g, m.shape[1] // g
    m4 = m.reshape(g, a, g, b)
    idx = jnp.arange(g)
    return m4[idx, :, idx, :]


def _s5_prepare(lam_re, lam_im, log_step, b_re, b_im, c_re, c_im):
    step = jnp.exp(log_step)[:, None]
    mag = jnp.exp(lam_re * step)
    lbr = mag * jnp.cos(lam_im * step)
    lbi = mag * jnp.sin(lam_im * step)
    nr, ni = lbr - 1.0, lbi
    den = lam_re * lam_re + lam_im * lam_im
    cr = (nr * lam_re + ni * lam_im) / den
    ci = (ni * lam_re - nr * lam_im) / den
    bbr = cr[..., None] * b_re - ci[..., None] * b_im
    bbi = cr[..., None] * b_im + ci[..., None] * b_re
    bre = _block_diag(jnp.swapaxes(bbr, 1, 2))
    bim = _block_diag(jnp.swapaxes(bbi, 1, 2))
    cre = _block_diag(jnp.swapaxes(c_re, 1, 2))
    cim = _block_diag(jnp.swapaxes(c_im, 1, 2))
    return lbr.reshape(-1), lbi.reshape(-1), bre, bim, cre, cim


def _s5_scan_coef(lbr, lbi, reverse):
    if reverse:
        lbi = -lbi
    pr, pi = [lbr], [lbi]
    for _ in range(7):
        pr, pi = pr + [pr[-1] * lbr - pi[-1] * lbi], pi + [pr[-1] * lbi + pi[-1] * lbr]
    row = jnp.arange(SUBLANES)[:, None]
    tabs = []
    for sh in (1, 2, 4):
        keep = (row < SUBLANES - sh) if reverse else (row >= sh)
        tabs.append(jnp.stack([jnp.where(keep, pr[sh - 1][None, :], 0.0), jnp.where(keep, pi[sh - 1][None, :], 0.0)]))
    powr = jnp.stack(pr)
    powi = jnp.stack(pi)
    if reverse:
        powr, powi = powr[::-1], powi[::-1]
    tabs.append(jnp.stack([powr, powi]))
    tabs.append(jnp.zeros_like(tabs[-1]))
    return jnp.stack(tabs).astype(F32)


def _xy_peers():
    x, y, c = lax.axis_index("x"), lax.axis_index("y"), lax.axis_index("c")
    return x, y, c, [(1 - x, y), (x, 1 - y), (1 - x, 1 - y)]


def _hbm():
    return pl.BlockSpec(memory_space=pl.ANY)


def _xy_allgather(buf, *, name):
    n, w = buf.shape

    def body(x_ref, out_ref, send_sems, recv_sems, local_sem):
        x, y, c, peers = _xy_peers()
        me = 2 * x + y
        own = pltpu.make_async_copy(x_ref, out_ref.at[me], local_sem)
        own.start()
        sends = []
        for k, (px, py) in enumerate(peers):
            cp = pltpu.make_async_remote_copy(src_ref=x_ref, dst_ref=out_ref.at[me], send_sem=send_sems.at[k],
                                              recv_sem=recv_sems.at[k], device_id=(px, py, c), device_id_type=MESH)
            cp.start()
            sends.append(cp)
        for k, (px, py) in enumerate(peers):
            pltpu.make_async_remote_copy(src_ref=x_ref, dst_ref=out_ref.at[2 * px + py], send_sem=send_sems.at[k],
                                         recv_sem=recv_sems.at[k], device_id=(px, py, c),
                                         device_id_type=MESH).wait_recv()
        for cp in sends:
            cp.wait_send()
        own.wait()

    return pl.pallas_call(
        body, name=name, in_specs=[_hbm()], out_specs=_hbm(),
        out_shape=jax.ShapeDtypeStruct((4, n, w), buf.dtype),
        scratch_shapes=[pltpu.SemaphoreType.DMA((3,)), pltpu.SemaphoreType.DMA((3,)), pltpu.SemaphoreType.DMA],
    )(buf)


def _remote(src, dst, send_sem, recv_sem, dev):
    return pltpu.make_async_remote_copy(src_ref=src, dst_ref=dst, send_sem=send_sem, recv_sem=recv_sem,
                                        device_id=dev, device_id_type=MESH)


LAYER_GATHERED = (
    ("ssd_conv_w", (4, 256), 1), ("rg_conv_w", (4, LANES), 1),
    ("w_in", (1024, W_IN_PAD), 1), ("s5_glu_w", (64, 256), 0), ("w_out", (256, 1024), 0), ("xa_wq", (256, 1024), 0),
    ("xa_wk", (256, 1024), 0), ("xa_wv", (256, 1024), 0), ("xa_wo", (256, 1024), 0), ("mlp_w1", (1024, 1024), 1),
    ("mlp_w2", (1024, 1024), 0),
)
N_GATHERED = len(LAYER_GATHERED)
WAIT_GROUPS = ((0, 1, 2, 3), (4,), (5, 6, 7, 8), (9, 10))
RG_CONV_SHARD = RG_WIDTH // 4
N_GATHER_COPIES = 3 * N_GATHERED * DEPTH


def _gather_part(ref, t, pos):
    _, shp, ax = LAYER_GATHERED[t % N_GATHERED]
    idx = tuple(pl.ds(pos * shp[ax], shp[ax]) if d == ax else slice(None) for d in range(len(shp)))
    return ref.at[idx]


def _gather_start(shards):
    n = len(shards)
    lands = []
    for t, s in enumerate(shards):
        _, shp, ax = LAYER_GATHERED[t % N_GATHERED]
        full = shp[:ax] + (4 * shp[ax],) + shp[ax + 1:]
        lands.append(pltpu.with_memory_space_constraint(lax.empty(full, s.dtype), pltpu.HBM))

    def body(*refs):
        srcs, lnds = refs[:n], refs[n:2 * n]
        send_sems, recv_sems, local_sems = refs[2 * n:2 * n + 3]
        token = refs[-1]
        x, y, c, peers = _xy_peers()
        me = 2 * x + y
        for t in range(n):
            for k, (px, py) in enumerate(peers):
                _remote(srcs[t], _gather_part(lnds[t], t, me), send_sems.at[k * n + t], recv_sems.at[k * n + t],
                        (px, py, c)).start()
            pltpu.make_async_copy(srcs[t], _gather_part(lnds[t], t, me), local_sems.at[t]).start()
        token[...] = jnp.zeros_like(token)

    hbm = pl.BlockSpec(memory_space=pltpu.HBM)
    sem = pl.BlockSpec(memory_space=pltpu.SEMAPHORE)
    outs = pl.pallas_call(
        body, name="weights_gather_start", in_specs=[hbm] * (2 * n),
        out_shape=(pltpu.SemaphoreType.DMA((3 * n,)), pltpu.SemaphoreType.DMA((3 * n,)),
                   pltpu.SemaphoreType.DMA((n,)),
                   *[pltpu.HBM(s.shape, s.dtype) for s in shards], *[pltpu.HBM(a.shape, a.dtype) for a in lands],
                   jax.ShapeDtypeStruct((SUBLANES, LANES), F32)),
        out_specs=(sem, sem, sem, *[hbm] * (2 * n), pl.BlockSpec(memory_space=pltpu.VMEM)),
        input_output_aliases={i: 3 + i for i in range(2 * n)},
        compiler_params=pltpu.CompilerParams(has_side_effects=pltpu.SideEffectType.DATAFLOW_SIDE_EFFECTING),
    )(*[pltpu.with_memory_space_constraint(s, pltpu.HBM) for s in shards], *lands)
    return outs[0], outs[1], outs[2], outs[3:3 + n], outs[3 + n:3 + 2 * n], outs[-1]


def _gather_wait(handle, ts, after, *, name):
    send_sems, recv_sems, local_sems, src_thru, land_thru, _ = handle
    n = len(src_thru)
    m = len(ts)

    def body(*refs):
        srcs, lnds = refs[:m], refs[m:2 * m]
        ssem, rsem, lsem = refs[2 * m:2 * m + 3]
        x, y, c, peers = _xy_peers()
        me = 2 * x + y
        for i, t in enumerate(ts):
            for k, (px, py) in enumerate(peers):
                cp = _remote(srcs[i], _gather_part(lnds[i], t, 2 * px + py), ssem.at[k * n + t], rsem.at[k * n + t],
                             (px, py, c))
                cp.wait_send()
                cp.wait_recv()
            pltpu.make_async_copy(srcs[i], _gather_part(lnds[i], t, me), lsem.at[t]).wait()

    hbm = pl.BlockSpec(memory_space=pltpu.HBM)
    sem = pl.BlockSpec(memory_space=pltpu.SEMAPHORE)
    args = [src_thru[t] for t in ts] + [land_thru[t] for t in ts]
    outs = pl.pallas_call(
        body, name=name, in_specs=[hbm] * (2 * m) + [sem, sem, sem, pl.BlockSpec(memory_space=pl.ANY)],
        out_shape=[pltpu.HBM(a.shape, a.dtype) for a in args], out_specs=[hbm] * (2 * m),
        input_output_aliases={i: i for i in range(2 * m)},
        compiler_params=pltpu.CompilerParams(has_side_effects=pltpu.SideEffectType.DATAFLOW_SIDE_EFFECTING),
    )(*args, send_sems, recv_sems, local_sems, after)
    return outs[:m], outs[m:]


C_CHUNKS = 8
XY_CHUNKS = 8
EW_ROWS = 512


def _c_exchange(g, part):
    w = g.shape[2]
    row0, nrows = G_PARTS[part]
    half = nrows // 2
    rq = half // C_CHUNKS

    def body(g_ref, got_ref, send_sems, recv_sems):
        x, y, c = lax.axis_index("x"), lax.axis_index("y"), lax.axis_index("c")
        cps = []
        for s in range(4):
            for q in range(C_CHUNKS):
                k = s * C_CHUNKS + q
                cp = _remote(g_ref.at[s, pl.ds(row0 + (1 - c) * half + q * rq, rq), :],
                             got_ref.at[s, pl.ds(q * rq, rq), :], send_sems.at[k], recv_sems.at[k], (x, y, 1 - c))
                cp.start()
                cps.append(cp)
        for cp in cps:
            cp.wait_recv()
        for cp in cps:
            cp.wait_send()

    return pl.pallas_call(
        body, name="grad_c_exchange_%d" % part, in_specs=[_hbm()], out_specs=_hbm(),
        out_shape=jax.ShapeDtypeStruct((4, half, w), g.dtype),
        scratch_shapes=[pltpu.SemaphoreType.DMA((4 * C_CHUNKS,)), pltpu.SemaphoreType.DMA((4 * C_CHUNKS,))],
    )(g)


XFER_DTYPE = jnp.bfloat16


def _add_own_half(g, got, c_arr, part):
    w = g.shape[2]
    row0, nrows = G_PARTS[part]
    half = nrows // 2
    nb = half // EW_ROWS
    b0 = row0 // EW_ROWS

    def body(c_ref, a_ref, b_ref, o_ref, t_ref):
        sm = a_ref[...] + b_ref[...]
        o_ref[...] = sm.astype(o_ref.dtype)

        @pl.when(pl.program_id(1) == nb - 1)
        def _():
            t_ref[...] = sm[:, EW_ROWS - MISC_ROWS:, :]

    grid_spec = pltpu.PrefetchScalarGridSpec(
        num_scalar_prefetch=1, grid=(4, nb),
        in_specs=[pl.BlockSpec((1, EW_ROWS, w), lambda s, i, c: (s, b0 + c[0] * nb + i, 0)),
                  pl.BlockSpec((1, EW_ROWS, w), lambda s, i, c: (s, i, 0))],
        out_specs=[pl.BlockSpec((1, EW_ROWS, w), lambda s, i, c: (s, i, 0)),
                   pl.BlockSpec((1, MISC_ROWS, w), lambda s, i, c: (s, 0, 0))])
    return pl.pallas_call(
        body, name="grad_add_halves", grid_spec=grid_spec,
        out_shape=[jax.ShapeDtypeStruct((4, half, w), XFER_DTYPE), jax.ShapeDtypeStruct((4, MISC_ROWS, w), g.dtype)],
        compiler_params=_cparams(("arbitrary", "arbitrary")),
    )(c_arr, g, got)


def _xy_pieces(arrs):
    pieces = []
    for a, arr in enumerate(arrs):
        nch = XY_CHUNKS if a == 0 else 1
        rq = arr.shape[1] // nch
        pieces += [(a, pl.ds(q * rq, rq)) for q in range(nch)]
    return pieces


def _xy_start(arrs, *, name):
    na = len(arrs)
    pieces = _xy_pieces(arrs)
    npc = len(pieces)
    lands = [pltpu.with_memory_space_constraint(lax.empty(a.shape, a.dtype), pltpu.HBM) for a in arrs]

    def body(*refs):
        ins, outs = refs[:na], refs[na:2 * na]
        send_sems, recv_sems, local_sems = refs[2 * na:2 * na + 3]
        token = refs[-1]
        x, y, c, peers = _xy_peers()
        me = 2 * x + y
        for k, (px, py) in enumerate(peers):
            for j, (a, rows) in enumerate(pieces):
                _remote(ins[a].at[2 * px + py, rows, :], outs[a].at[me, rows, :], send_sems.at[k * npc + j],
                        recv_sems.at[k * npc + j], (px, py, c)).start()
        for j, (a, rows) in enumerate(pieces):
            pltpu.make_async_copy(ins[a].at[me, rows, :], outs[a].at[me, rows, :], local_sems.at[j]).start()
        token[...] = jnp.zeros_like(token)

    hbm = pl.BlockSpec(memory_space=pltpu.HBM)
    sem = pl.BlockSpec(memory_space=pltpu.SEMAPHORE)
    outs = pl.pallas_call(
        body, name=name, in_specs=[hbm] * (2 * na),
        out_shape=(pltpu.SemaphoreType.DMA((3 * npc,)), pltpu.SemaphoreType.DMA((3 * npc,)),
                   pltpu.SemaphoreType.DMA((npc,)),
                   *[pltpu.HBM(a.shape, a.dtype) for a in arrs], *[pltpu.HBM(a.shape, a.dtype) for a in arrs],
                   jax.ShapeDtypeStruct((SUBLANES, LANES), F32)),
        out_specs=(sem, sem, sem, *[hbm] * (2 * na), pl.BlockSpec(memory_space=pltpu.VMEM)),
        input_output_aliases={i: 3 + i for i in range(2 * na)},
        compiler_params=pltpu.CompilerParams(has_side_effects=pltpu.SideEffectType.DATAFLOW_SIDE_EFFECTING),
    )(*[pltpu.with_memory_space_constraint(a, pltpu.HBM) for a in arrs], *lands)
    return (outs[0], outs[1], outs[2], outs[3:3 + na], outs[3 + na:3 + 2 * na]), outs[-1]


def _xy_wait(handle, after, *, name):
    send_sems, recv_sems, local_sems, src_thru, land_thru = handle
    na = len(src_thru)
    pieces = _xy_pieces(src_thru)
    npc = len(pieces)

    def body(*refs):
        ins, outs = refs[:na], refs[na:2 * na]
        ssem, rsem, lsem = refs[2 * na:2 * na + 3]
        x, y, c, peers = _xy_peers()
        me = 2 * x + y
        for k, (px, py) in enumerate(peers):
            for j, (a, rows) in enumerate(pieces):
                cp = _remote(ins[a].at[me, rows, :], outs[a].at[2 * px + py, rows, :], ssem.at[k * npc + j],
                             rsem.at[k * npc + j], (px, py, c))
                cp.wait_send()
                cp.wait_recv()
        for j, (a, rows) in enumerate(pieces):
            pltpu.make_async_copy(ins[a].at[me, rows, :], outs[a].at[me, rows, :], lsem.at[j]).wait()

    hbm = pl.BlockSpec(memory_space=pltpu.HBM)
    sem = pl.BlockSpec(memory_space=pltpu.SEMAPHORE)
    args = list(src_thru) + list(land_thru)
    outs = pl.pallas_call(
        body, name=name, in_specs=[hbm] * (2 * na) + [sem, sem, sem, pl.BlockSpec(memory_space=pl.ANY)],
        out_shape=[pltpu.HBM(a.shape, a.dtype) for a in args], out_specs=[hbm] * (2 * na),
        input_output_aliases={i: i for i in range(2 * na)},
        compiler_params=pltpu.CompilerParams(has_side_effects=pltpu.SideEffectType.DATAFLOW_SIDE_EFFECTING),
    )(*args, send_sems, recv_sems, local_sems, after)
    return outs[na:]


def _sum4_into_half(r, rt, c_arr, part, fbuf):
    _, half, w = r.shape
    nb = half // EW_ROWS
    b0 = G_PARTS[part][0] // EW_ROWS

    def body(c_ref, r_ref, t_ref, *rest):
        o_ref = rest[-1]
        o_ref[...] = ((r_ref[0].astype(F32) + r_ref[1].astype(F32)) + r_ref[2].astype(F32)) + r_ref[3].astype(F32)

        @pl.when(pl.program_id(0) == nb - 1)
        def _():
            o_ref[EW_ROWS - MISC_ROWS:, :] = ((t_ref[0] + t_ref[1]) + t_ref[2]) + t_ref[3]

    in_specs = [pl.BlockSpec((4, EW_ROWS, w), lambda i, c: (0, i, 0)),
                pl.BlockSpec((4, MISC_ROWS, w), lambda i, c: (0, 0, 0))]
    args = [c_arr, r, rt]
    aliases = {}
    if fbuf is not None:
        in_specs.append(pl.BlockSpec(memory_space=pl.ANY))
        args.append(fbuf)
        aliases = {3: 0}
    grid_spec = pltpu.PrefetchScalarGridSpec(
        num_scalar_prefetch=1, grid=(nb,), in_specs=in_specs,
        out_specs=pl.BlockSpec((EW_ROWS, w), lambda i, c: (b0 + c[0] * nb + i, 0)))
    return pl.pallas_call(
        body, name="grad_sum4", grid_spec=grid_spec, out_shape=jax.ShapeDtypeStruct((G_ROWS, w), F32),
        input_output_aliases=aliases, compiler_params=_cparams(("arbitrary",)),
    )(*args)


C_GATHER_ROWS = 512


def _c_allgather_halves(f, parts):
    w = f.shape[1]
    chunks = []
    for part in parts:
        chunks += [(part, r) for r in range(0, G_PARTS[part][1] // 2, C_GATHER_ROWS)]
    nch = len(chunks)

    def body(f_ref, out_ref, send_sems, recv_sems):
        x, y, c = lax.axis_index("x"), lax.axis_index("y"), lax.axis_index("c")

        def rows(q, owner):
            part, r = chunks[q]
            row0, nrows = G_PARTS[part]
            return pl.ds(row0 + owner * (nrows // 2) + r, C_GATHER_ROWS)

        sends = []
        for q in range(nch):
            cp = _remote(f_ref.at[rows(q, c), :], out_ref.at[rows(q, c), :], send_sems.at[q], recv_sems.at[q],
                         (x, y, 1 - c))
            cp.start()
            sends.append(cp)
        for q in range(nch):
            _remote(f_ref.at[rows(q, 1 - c), :], out_ref.at[rows(q, 1 - c), :], send_sems.at[q], recv_sems.at[q],
                    (x, y, 1 - c)).wait_recv()
        for cp in sends:
            cp.wait_send()

    return pl.pallas_call(
        body, name="grad_c_allgather_" + "".join(str(p) for p in parts), in_specs=[_hbm()], out_specs=_hbm(),
        input_output_aliases={0: 0},
        out_shape=jax.ShapeDtypeStruct((G_ROWS, w), f.dtype),
        scratch_shapes=[pltpu.SemaphoreType.DMA((nch,)), pltpu.SemaphoreType.DMA((nch,))],
    )(f)


def _adamw(w, m, v, g, g_rows=None):
    shape = w.shape
    cols = shape[-1]
    rows = int(math.prod(shape)) // cols
    tr = 256 if rows % 256 == 0 else rows
    from_flat = g_rows is not None
    c1 = 1.0 / (1.0 - ADAM_B1 ** ADAM_STEP)
    c2 = 1.0 / (1.0 - ADAM_B2 ** ADAM_STEP)

    def body(w_ref, m_ref, v_ref, g_ref, *outs):
        gg = g_ref[...]
        nm = ADAM_B1 * m_ref[...] + (1.0 - ADAM_B1) * gg
        nv = ADAM_B2 * v_ref[...] + (1.0 - ADAM_B2) * (gg * gg)
        if from_flat:
            outs[0][...] = gg
        d_ref, nm_ref, nv_ref = outs[-3:]
        nm_ref[...] = nm
        nv_ref[...] = nv
        d_ref[...] = -ADAM_LR * ((nm * c1) / (jnp.sqrt(nv * c2) + ADAM_EPS) + ADAM_WD * w_ref[...])

    spec = pl.BlockSpec((tr, cols), lambda i: (i, 0))
    if from_flat:
        nbl = rows // DEPTH // tr
        assert cols == FLAT and all(r % tr == 0 for r in g_rows) and len(g_rows) == DEPTH == 2
        b0, b1 = g_rows[0] // tr, g_rows[1] // tr
        g_spec = pl.BlockSpec((tr, cols), lambda i: (jnp.where(i < nbl, b0 + i, b1 + i - nbl), 0))
        g_arg = g
    else:
        g_spec = spec
        g_arg = g.reshape(rows, cols)
    n_out = 4 if from_flat else 3
    sds = jax.ShapeDtypeStruct((rows, cols), F32)
    outs = pl.pallas_call(
        body, name="adamw", grid=(rows // tr,), in_specs=[spec, spec, spec, g_spec], out_specs=[spec] * n_out,
        out_shape=[sds] * n_out, compiler_params=_cparams(("arbitrary",)),
    )(w.reshape(rows, cols), m.reshape(rows, cols), v.reshape(rows, cols), g_arg)
    outs = [o.reshape(shape) for o in outs]
    return outs if from_flat else [g] + outs


SMALL_SHARDED = (("s5_glu_w", (2, 64, 256), 1), ("ssd_conv_w", (2, 4, 256), 2), ("rg_conv_w", (2, 4, 64), 2))
REPLICATED = (
    ("ssd_conv_b", (2, 1024)), ("ssd_dt_bias", (2, 8)), ("ssd_a_log", (2, 8)), ("ssd_d", (2, 8)),
    ("ssd_norm_w", (2, 512)), ("s5_lam_re", (2, 16, 64)), ("s5_lam_im", (2, 16, 64)), ("s5_log_step", (2, 16)),
    ("s5_b_re", (2, 16, 64, 16)), ("s5_b_im", (2, 16, 64, 16)), ("s5_c_re", (2, 16, 16, 64)),
    ("s5_c_im", (2, 16, 16, 64)), ("s5_d", (2, 256)), ("s5_glu_b", (2, 256)), ("rg_conv_b", (2, 256)),
    ("rg_wa", (2, 4, 64, 64)), ("rg_ba", (2, 4, 64)), ("rg_wx", (2, 4, 64, 64)), ("rg_bx", (2, 4, 64)),
    ("rg_lambda", (2, 256)), ("ln1_g", (2, 1024)), ("ln1_b", (2, 1024)), ("ln2_g", (2, 1024)), ("ln2_b", (2, 1024)),
    ("ln3_g", (2, 1024)), ("ln3_b", (2, 1024)),
)
WEIGHT_ORDER = (
    "w_in", "w_out", "ssd_conv_w", "ssd_conv_b", "ssd_dt_bias", "ssd_a_log", "ssd_d", "ssd_norm_w", "s5_lam_re",
    "s5_lam_im", "s5_log_step", "s5_b_re", "s5_b_im", "s5_c_re", "s5_c_im", "s5_d", "s5_glu_w", "s5_glu_b",
    "rg_conv_w", "rg_conv_b", "rg_wa", "rg_ba", "rg_wx", "rg_bx", "rg_lambda", "ln1_g", "ln1_b", "xa_wq", "xa_wk",
    "xa_wv", "xa_wo", "ln2_g", "ln2_b", "mlp_w1", "mlp_w2", "ln3_g", "ln3_b",
)


def _size(shape):
    return int(math.prod(shape))


def _round_up(a, b):
    return (a + b - 1) // b * b


SMALL_ELEMS = sum(_size(s) for _, s, _ in SMALL_SHARDED)
REP_ELEMS = sum(_size(s) for _, s in REPLICATED)
REP_QROWS = _round_up(-(-REP_ELEMS // (4 * FLAT)), 8)
assert SMALL_ELEMS <= MISC_REP_ROW * FLAT and MISC_REP_ROW + REP_QROWS <= MISC_ROWS


def _pack_shards(tensors, names_shapes):
    return jnp.concatenate([tensors[n].reshape(-1) for n, *_ in names_shapes])


def _unpack(flat, names_shapes):
    out, off = {}, 0
    for n, s, *_ in names_shapes:
        out[n] = flat[off:off + _size(s)].reshape(s)
        off += _size(s)
    return out


def _split_shards(full, names_shapes):
    rows = []
    for k in range(4):
        parts = []
        for n, s, ax in names_shapes:
            w = s[ax]
            parts.append(lax.slice_in_dim(full[n], k * w, (k + 1) * w, axis=ax).reshape(-1))
        rows.append(jnp.concatenate(parts))
    return jnp.stack(rows)


def _pack_cols(w):
    pad = jnp.zeros((w.shape[0], LANES - SSD_HEADS), w.dtype)
    return jnp.concatenate([w[:, O_XBC:O_XBC + 1024], w[:, O_Z:O_Z + 512], w[:, O_U:O_U + 256],
                            w[:, O_XRG:O_XRG + 256], w[:, O_GRG:O_GRG + 256], w[:, O_DT:O_DT + 8], pad], axis=1)


def _unpack_cols(w):
    return jnp.concatenate([w[:, P_Z:P_Z + 512], w[:, P_XBC:P_XBC + 1024], w[:, P_DT:P_DT + 8],
                            w[:, P_U:P_U + 256], w[:, P_XRG:P_XRG + 256], w[:, P_GRG:P_GRG + 256]], axis=1)


def _lanes(v, width):
    return jnp.pad(v, (0, width - v.shape[0])).reshape(1, width)


def _layer_params(rep, l):
    p = {}
    p["ssd_cb"] = rep["ssd_conv_b"][l].reshape(1, -1)
    p["ssd_dtb"] = _lanes(rep["ssd_dt_bias"][l], LANES)
    p["ssd_a"] = _lanes(-jnp.exp(rep["ssd_a_log"][l]), LANES)
    p["ssd_d"] = jnp.repeat(rep["ssd_d"][l], 64).reshape(1, -1)
    p["ssd_nw"] = rep["ssd_norm_w"][l].reshape(1, -1)
    s5_args = tuple(rep[n][l] for n in ("s5_lam_re", "s5_lam_im", "s5_log_step", "s5_b_re", "s5_b_im", "s5_c_re",
                                        "s5_c_im"))
    (lbr, lbi, bre, bim, cre, cim), p["s5_vjp"] = jax.vjp(_s5_prepare, *s5_args)
    p.update(s5_bre=bre, s5_bim=bim, s5_cre=cre, s5_cim=cim)
    p["s5_coef"] = _s5_scan_coef(lbr, lbi, False)
    p["s5_rcoef"] = _s5_scan_coef(lbr, lbi, True)
    p["s5_d"] = rep["s5_d"][l].reshape(1, -1)
    p["s5_gb"] = rep["s5_glu_b"][l].reshape(1, -1)
    p["rg_cb"] = rep["rg_conv_b"][l].reshape(1, -1)
    p["rg_wa"] = _block_diag(rep["rg_wa"][l])
    p["rg_wx"] = _block_diag(rep["rg_wx"][l])
    p["rg_ba"] = rep["rg_ba"][l].reshape(1, -1)
    p["rg_bx"] = rep["rg_bx"][l].reshape(1, -1)
    p["rg_nsp"] = (-RG_C * jax.nn.softplus(-rep["rg_lambda"][l])).reshape(1, -1)
    p["rg_dnsp"] = RG_C * jax.nn.sigmoid(-rep["rg_lambda"][l])
    for n in ("ln1_g", "ln1_b", "ln2_g", "ln2_b", "ln3_g", "ln3_b"):
        p[n] = rep[n][l].reshape(1, -1)
    return p


def _layer_fwd(h, mem, p, fetch):
    s = {"h0": h}
    p.update(fetch(0, h))
    proj = _mm(h, p["w_in"], name="in_proj")
    s["proj"] = proj
    y_ssd, s["ssd_yy"], s["ssd_states"] = _ssd_fwd(proj, p["ssd_cw"], p["ssd_cb"], p["ssd_dtb"], p["ssd_a"],
                                                     p["ssd_d"], p["ssd_nw"])
    y_s5, s["s5_y2"], s["s5_hre"], s["s5_him"] = _s5_fwd(proj, p["s5_bre"], p["s5_bim"], p["s5_cre"], p["s5_cim"],
                                                         p["s5_d"], p["s5_glu_w"], p["s5_gb"], p["s5_coef"])
    y_rg, s["rg_h"] = _rg_fwd(proj, p["rg_cw"], p["rg_cb"], p["rg_wa"], p["rg_ba"], p["rg_wx"], p["rg_bx"],
                              p["rg_nsp"])
    s["ys"] = [y_ssd, y_s5, y_rg]
    p.update(fetch(1, y_rg))
    h1, s["xh1"], s["rs1"] = _outproj_ln_fwd(s["ys"], h, p["w_out"], p["ln1_g"], p["ln1_b"])
    s["h1"] = h1
    p.update(fetch(2, h1))
    kb = _mm(mem, p["xa_wk"], name="mem_proj")
    vb = _mm(mem, p["xa_wv"], name="mem_proj")
    s["kb"], s["vb"] = kb, vb
    h2, s["xh2"], s["rs2"], s["attn_o"] = _attn_ln_fwd(h1, p["xa_wq"], p["xa_wo"], kb, vb, p["ln2_g"], p["ln2_b"])
    s["h2"] = h2
    p.update(fetch(3, h2))
    h3, s["xh3"], s["rs3"], s["mlp_hdn"] = _mlp_ln_fwd(h2, p["mlp_w1"], p["mlp_w2"], p["ln3_g"], p["ln3_b"])
    return h3, s


def _layer_bwd(dh3, mem, p, s, l, gbuf, after_mlp=None):
    g = {}
    dr3, du, dh2, g["ln3_g"], g["ln3_b"] = _mlp_ln_bwd(dh3, s["xh3"], s["rs3"], p["ln3_g"], s["mlp_hdn"],
                                                        p["mlp_w1"], p["mlp_w2"])
    gbuf = _wgrad_flat(s["h2"], du, gbuf, mode="colblk", row_off=_grad_row("mlp_w1", l), name="wgrad_mlp_w1")
    gbuf = _wgrad_flat(s["mlp_hdn"], dr3, gbuf, mode="rowblk", row_off=_grad_row("mlp_w2", l), name="wgrad_mlp_w2")
    ln2_g = p["ln2_g"] if after_mlp is None else p["ln2_g"] + after_mlp(gbuf)[0:1, 0:1]
    dr2, dq, dh1, dkb, dvb, g["ln2_g"], g["ln2_b"] = _attn_ln_bwd(dh2, s["xh2"], s["rs2"], ln2_g, s["h1"],
                                                                   p["xa_wq"], p["xa_wo"], s["kb"], s["vb"])
    for n, a_op, g_op in (("xa_wo", s["attn_o"], dr2), ("xa_wq", s["h1"], dq), ("xa_wk", mem, dkb),
                          ("xa_wv", mem, dvb)):
        gbuf = _wgrad_flat(a_op, g_op, gbuf, mode="rows4", row_off=_grad_row(n, l), name="wgrad_" + n)
    dr1, dres, dycat, g["ln1_g"], g["ln1_b"] = _outproj_ln_bwd(dh1, s["xh1"], s["rs1"], p["ln1_g"], p["w_out"])
    gbuf = _wgrad_flat(s["ys"], dr1, gbuf, mode="rows4", row_off=_grad_row("w_out", l), name="wgrad_w_out")
    proj = s["proj"]
    (dxbc, dz, ddt, dcw, dcb, ddtb, da_neg, dd_l, dnw) = _ssd_bwd(
        dycat, proj, s["ssd_yy"], s["ssd_states"], p["ssd_cw"], p["ssd_cb"], p["ssd_dtb"], p["ssd_a"], p["ssd_d"],
        p["ssd_nw"])
    g["ssd_conv_w"] = dcw[0:4]
    g["ssd_conv_b"] = dcb[0]
    g["ssd_dt_bias"] = ddtb[0, :SSD_HEADS]
    g["ssd_a_log"] = da_neg[0, :SSD_HEADS] * p["ssd_a"][0, :SSD_HEADS]
    g["ssd_d"] = dd_l.reshape(SSD_HEADS, 64).sum(axis=1)
    g["ssd_norm_w"] = dnw[0]
    (du_s5, dbre, dbim, dcre, dcim, dlam, dd5, dgw, dgb) = _s5_bwd(
        dycat, proj, s["s5_y2"], s["s5_hre"], s["s5_him"], p["s5_bre"], p["s5_bim"], p["s5_cre"], p["s5_cim"],
        p["s5_d"], p["s5_glu_w"], p["s5_gb"], p["s5_rcoef"])
    dl = dlam.sum(axis=1)
    s5g = p["s5_vjp"]((dl[0], dl[1], dbre, dbim, dcre, dcim))
    for n, v in zip(("s5_lam_re", "s5_lam_im", "s5_log_step", "s5_b_re", "s5_b_im", "s5_c_re", "s5_c_im"), s5g):
        g[n] = v
    g["s5_d"] = dd5[0]
    g["s5_glu_w"] = dgw
    g["s5_glu_b"] = dgb[0]
    (dxrg, dgrg, drcw, drcb, dwa, dba, dwx, dbx, dnsp) = _rg_bwd(
        dycat, proj, s["rg_h"], p["rg_cw"], p["rg_cb"], p["rg_wa"], p["rg_ba"], p["rg_wx"], p["rg_bx"], p["rg_nsp"])
    g["rg_conv_w"] = drcw[0:4]
    g["rg_conv_b"] = drcb[0]
    g["rg_wa"] = _block_diag_extract(dwa, RG_BLOCKS)
    g["rg_wx"] = _block_diag_extract(dwx, RG_BLOCKS)
    g["rg_ba"] = dba.reshape(RG_BLOCKS, RG_BLOCK_DIM)
    g["rg_bx"] = dbx.reshape(RG_BLOCKS, RG_BLOCK_DIM)
    g["rg_lambda"] = dnsp[0] * p["rg_dnsp"]
    dproj = [dxbc, dz, du_s5, dxrg, dgrg, ddt]
    g["w_in"] = _unpack_cols(_wgrad_in(s["h0"], dproj))
    dh0 = _in_proj_bwd(dproj, p["w_in"], dres)
    for n in ("ln1_g", "ln1_b", "ln2_g", "ln2_b", "ln3_g", "ln3_b"):
        g[n] = g[n][0]
    return dh0, g, gbuf


def _local_step(h, memf, target, rep, fetch):
    params, saved = [], []
    for l in range(DEPTH):
        p = _layer_params(rep, l)
        params.append(p)
        h, s = _layer_fwd(h, memf, p, functools.partial(fetch, l))
        saved.append(s)
    loss11, dh = _loss_fwd_bwd(h, target)
    grads = [None] * DEPTH
    gbuf = None
    c_arr = lax.axis_index("c").astype(jnp.int32).reshape(1)
    handles = {}

    def start_part(buf, part):
        handles[part], token = _xy_start(_chip_sums(buf, c_arr, part), name="grad_xy_start_%d" % part)
        return token

    for l in reversed(range(DEPTH)):
        hook = functools.partial(start_part, part=1) if l == 0 else None
        dh, grads[l], gbuf = _layer_bwd(dh, memf, params[l], saved[l], l, gbuf, hook)
        if l == DEPTH - 1:
            gbuf = lax.dynamic_update_slice(
                gbuf, _w_in_block(grads[l]["w_in"], jnp.zeros((4, MISC_ROWS, FLAT), F32)),
                (0, _grad_row("w_in", l), 0))
            params[0]["ln3_g"] = params[0]["ln3_g"] + start_part(gbuf, 0)[0:1, 0:1]
    gsmall = {n: jnp.stack([grads[l][n] for l in range(DEPTH)]) for n in grads[0] if n != "w_in"}
    return loss11, dh, gsmall, grads[0]["w_in"], gbuf, handles, c_arr


def _w_in_block(gw, tail):
    gw = jnp.pad(gw.reshape(D_MODEL, 4, W_IN_SHARD), ((0, 0), (0, 0), (0, W_IN_PAD - W_IN_SHARD)))
    return jnp.concatenate([jnp.transpose(gw, (1, 0, 2)).reshape(4, W_IN_PAD, FLAT), tail], axis=1)


def _chip_sums(gbuf, c_arr, part):
    return list(_add_own_half(gbuf, _c_exchange(gbuf, part), c_arr, part))


def kernel(x, mem, w_in, w_out, ssd_conv_w, ssd_conv_b, ssd_dt_bias, ssd_a_log, ssd_d, ssd_norm_w, s5_lam_re, s5_lam_im, s5_log_step, s5_b_re, s5_b_im, s5_c_re, s5_c_im, s5_d, s5_glu_w, s5_glu_b, rg_conv_w, rg_conv_b, rg_wa, rg_ba, rg_wx, rg_bx, rg_lambda, ln1_g, ln1_b, xa_wq, xa_wk, xa_wv, xa_wo, ln2_g, ln2_b, mlp_w1, mlp_w2, ln3_g, ln3_b, loss_target, m_w_in, m_w_out, m_ssd_conv_w, m_ssd_conv_b, m_ssd_dt_bias, m_ssd_a_log, m_ssd_d, m_ssd_norm_w, m_s5_lam_re, m_s5_lam_im, m_s5_log_step, m_s5_b_re, m_s5_b_im, m_s5_c_re, m_s5_c_im, m_s5_d, m_s5_glu_w, m_s5_glu_b, m_rg_conv_w, m_rg_conv_b, m_rg_wa, m_rg_ba, m_rg_wx, m_rg_bx, m_rg_lambda, m_ln1_g, m_ln1_b, m_xa_wq, m_xa_wk, m_xa_wv, m_xa_wo, m_ln2_g, m_ln2_b, m_mlp_w1, m_mlp_w2, m_ln3_g, m_ln3_b, v_w_in, v_w_out, v_ssd_conv_w, v_ssd_conv_b, v_ssd_dt_bias, v_ssd_a_log, v_ssd_d, v_ssd_norm_w, v_s5_lam_re, v_s5_lam_im, v_s5_log_step, v_s5_b_re, v_s5_b_im, v_s5_c_re, v_s5_c_im, v_s5_d, v_s5_glu_w, v_s5_glu_b, v_rg_conv_w, v_rg_conv_b, v_rg_wa, v_rg_ba, v_rg_wx, v_rg_bx, v_rg_lambda, v_ln1_g, v_ln1_b, v_xa_wq, v_xa_wk, v_xa_wv, v_xa_wo, v_ln2_g, v_ln2_b, v_mlp_w1, v_mlp_w2, v_ln3_g, v_ln3_b):
    args = dict(locals())
    weights = {n: args[n] for n in WEIGHT_ORDER}
    mom_m = {n: args["m_" + n] for n in WEIGHT_ORDER}
    mom_v = {n: args["v_" + n] for n in WEIGHT_ORDER}

    shards = []
    for l in range(DEPTH):
        for n, shp, ax in LAYER_GATHERED:
            w = weights[n][l]
            if w.shape[1] != shp[1]:
                w = jnp.pad(w, ((0, 0), (0, shp[1] - w.shape[1])))
            if n not in ("ssd_conv_w", "rg_conv_w"):
                w = w.astype(MXU_DTYPE)
            shards.append(w)
    handle = _gather_start(shards)

    def unpad(arr, padded, width):
        return jnp.concatenate([arr[:, padded * k:padded * k + width] for k in range(4)], axis=1)

    def fetch(l, grp, after):
        ts = [l * N_GATHERED + j for j in WAIT_GROUPS[grp]]
        _, landed = _gather_wait(handle, ts, after, name="weights_gather_wait_%d_%d" % (l, grp))
        out = {}
        for t, arr in zip(ts, landed):
            n = LAYER_GATHERED[t % N_GATHERED][0]
            if n == "w_in":
                arr = _pack_cols(unpad(arr, W_IN_PAD, W_IN_SHARD))
            elif n == "rg_conv_w":
                arr = unpad(arr, LANES, RG_CONV_SHARD)
            out[{"ssd_conv_w": "ssd_cw", "rg_conv_w": "rg_cw"}.get(n, n)] = arr
        return out

    rep = {n: weights[n] for n, _ in REPLICATED}

    loss11, dx, gsmall, gw_in0, gbuf, handles, c_arr = _local_step(x[0], mem[0], loss_target[0], rep, fetch)
    grad_x = dx[None]
    loss = lax.psum(loss11[0, 0], ("x", "y", "c"))

    small_q = _split_shards(gsmall, SMALL_SHARDED)
    rep_q = jnp.pad(_pack_shards(gsmall, REPLICATED), (0, 4 * REP_QROWS * FLAT - REP_ELEMS)).reshape(4, -1)
    misc = jnp.concatenate(
        [jnp.pad(small_q, ((0, 0), (0, MISC_REP_ROW * FLAT - SMALL_ELEMS))), rep_q,
         jnp.zeros((4, (MISC_ROWS - MISC_REP_ROW - REP_QROWS) * FLAT), F32)], axis=1).reshape(4, MISC_ROWS, FLAT)
    gbuf = lax.dynamic_update_slice(gbuf, _w_in_block(gw_in0, misc), (0, _grad_row("w_in", 0), 0))
    handles[2], token = _xy_start(_chip_sums(gbuf, c_arr, 2), name="grad_xy_start_2")
    fbuf = None
    for part in (0, 1):
        got = _xy_wait(handles[part], dx, name="grad_xy_wait_%d" % part)
        fbuf = _sum4_into_half(got[0], got[1] + token[0:1, 0:1], c_arr, part, fbuf)
    fbuf = _c_allgather_halves(fbuf, (0, 1))
    res = {n: _adamw(weights[n], mom_m[n], mom_v[n], fbuf, g_rows=[_grad_row(n, l) for l in range(DEPTH)])
           for n in ("mlp_w1", "mlp_w2")}
    got = _xy_wait(handles[2], res["mlp_w2"][1], name="grad_xy_wait_2")
    reduced = _c_allgather_halves(_sum4_into_half(got[0], got[1], c_arr, 2, fbuf), (2,))
    misc_red = reduced[ROW_MISC:]
    rep_all = _xy_allgather(misc_red[MISC_REP_ROW:MISC_REP_ROW + REP_QROWS], name="small_grads_allgather")
    g_red = {**_unpack(misc_red[:MISC_REP_ROW].reshape(-1), SMALL_SHARDED),
             **_unpack(rep_all.reshape(-1), REPLICATED)}
    g_red["w_in"] = jnp.stack([
        reduced[_grad_row("w_in", l):_grad_row("w_in", l) + W_IN_PAD].reshape(D_MODEL, W_IN_PAD)[:, :W_IN_SHARD]
        for l in range(DEPTH)])

    for n in WEIGHT_ORDER:
        if n in ("w_out", "xa_wq", "xa_wk", "xa_wv", "xa_wo"):
            res[n] = _adamw(weights[n], mom_m[n], mom_v[n], reduced, g_rows=[_grad_row(n, l) for l in range(DEPTH)])
        elif n not in res:
            res[n] = _adamw(weights[n], mom_m[n], mom_v[n], g_red[n])
    return (loss, grad_x, *[res[n][0] for n in WEIGHT_ORDER], *[res[n][1] for n in WEIGHT_ORDER],
            *[res[n][2] for n in WEIGHT_ORDER], *[res[n][3] for n in WEIGHT_ORDER])
```

```python
import functools
import math

import jax
import jax.numpy as jnp
from jax import lax
from jax.experimental import pallas as pl
from jax.experimental.pallas import tpu as pltpu

F32 = jnp.float32
MXU_DTYPE = jnp.bfloat16

D_MODEL = 1024
DEPTH = 2
MEM_LEN = 256
SSD_WIDTH = 512
SSD_HEADS = 8
SSD_STATE = 128
SSD_CHUNK = 128
SSD_XBC = 1024
S5_WIDTH = 256
S5_GROUPS = 16
S5_GROUP_CH = 16
S5_STATE = 64
S5_NSTATE = S5_GROUPS * S5_STATE
RG_WIDTH = 256
RG_BLOCKS = 4
RG_BLOCK_DIM = 64
RG_C = 8.0
XA_HEADS = 4
XA_HEAD_DIM = 256
D_FF = 4096
D_IN = 2312
ALPHA = (2.0 * DEPTH) ** 0.25
LN_EPS = 1e-5
ADAM_LR = 0.001
ADAM_B1 = 0.9
ADAM_B2 = 0.999
ADAM_EPS = 1e-08
ADAM_WD = 0.01
ADAM_STEP = 10

P_XBC, P_Z, P_U, P_XRG, P_GRG, P_DT = 0, 1024, 1536, 1792, 2048, 2304
D_PACK = 2432
O_Z, O_XBC, O_DT, O_U, O_XRG, O_GRG = 0, 512, 1536, 1544, 1800, 2056

LANES = 128
SUBLANES = 8
VMEM_LIMIT = 52 * 1024 * 1024
TM = 512
SSD_FWD_TM = 256
SSD_BWD_TM = 128
SCAN_TM = 512
SCAN_BWD_TM = 256
FLAT = 1024

MESH = pl.DeviceIdType.MESH


def _cparams(sem):
    return pltpu.CompilerParams(dimension_semantics=sem, vmem_limit_bytes=VMEM_LIMIT)


def _dot(a, b):
    return jnp.dot(a.astype(MXU_DTYPE), b.astype(MXU_DTYPE), preferred_element_type=F32)


def _dot_nt(a, b):
    return lax.dot_general(a.astype(MXU_DTYPE), b.astype(MXU_DTYPE), (((1,), (1,)), ((), ())),
                           preferred_element_type=F32)


def _dot_tn(a, b):
    return lax.dot_general(a.astype(MXU_DTYPE), b.astype(MXU_DTYPE), (((0,), (0,)), ((), ())),
                           preferred_element_type=F32)


def _dot_f32(a, b):
    return jnp.dot(a, b, precision=lax.Precision.HIGHEST, preferred_element_type=F32)


def _dot_f32_tn(a, b):
    return lax.dot_general(a, b, (((0,), (0,)), ((), ())), precision=lax.Precision.HIGHEST,
                           preferred_element_type=F32)


def _sigmoid(x):
    return 1.0 / (1.0 + jnp.exp(-x))


def _softplus(x):
    return jnp.maximum(x, 0.0) + jnp.log(1.0 + jnp.exp(-jnp.abs(x)))


_GELU_K = math.sqrt(2.0 / math.pi)


def _gelu(x):
    return 0.5 * x * (1.0 + jnp.tanh(_GELU_K * (x + 0.044715 * x * x * x)))


def _gelu_grad(x):
    t = jnp.tanh(_GELU_K * (x + 0.044715 * x * x * x))
    return 0.5 * (1.0 + t) + 0.5 * x * (1.0 - t * t) * _GELU_K * (1.0 + 3.0 * 0.044715 * x * x)


def _expm1(x):
    small = x * (1.0 + x * (0.5 + x * (1.0 / 6.0 + x * (1.0 / 24.0))))
    return jnp.where(jnp.abs(x) < 0.05, small, jnp.exp(x) - 1.0)


def _sum0(x):
    return jnp.sum(x, axis=0, keepdims=True)


def _ln_fwd(r, g, b):
    mu = jnp.mean(r, axis=-1, keepdims=True)
    xc = r - mu
    var = jnp.mean(xc * xc, axis=-1, keepdims=True)
    rstd = lax.rsqrt(var + LN_EPS)
    xhat = xc * rstd
    return xhat * g + b, xhat, rstd


def _ln_bwd(dout, xhat, rstd, g):
    dxh = dout * g
    m1 = jnp.mean(dxh, axis=-1, keepdims=True)
    m2 = jnp.mean(dxh * xhat, axis=-1, keepdims=True)
    return rstd * (dxh - m1 - xhat * m2)


def _rows(tm, n, col=0):
    return pl.BlockSpec((tm, n), lambda i: (i, col))


def _const(shape):
    nd = len(shape)
    return pl.BlockSpec(shape, lambda i: (0,) * nd)


def _mm(a, w, *, name):
    t, k = a.shape
    n = w.shape[1]
    tm = min(TM, t)

    def body(a_ref, w_ref, o_ref):
        o_ref[...] = _dot(a_ref[...], w_ref[...])

    return pl.pallas_call(
        body, name=name, grid=(t // tm,), in_specs=[_rows(tm, k), _const(w.shape)], out_specs=_rows(tm, n),
        out_shape=jax.ShapeDtypeStruct((t, n), F32), compiler_params=_cparams(("arbitrary",)),
    )(a, w)


DPROJ_PIECES = ((P_XBC, 1024), (P_Z, 512), (P_U, 256), (P_XRG, 256), (P_GRG, 256), (P_DT, LANES))


def _in_proj_bwd(pieces, w, dres):
    t = dres.shape[0]
    npc = len(pieces)

    def body(*refs):
        w_ref, r_ref, o_ref = refs[npc:]
        acc = r_ref[...]
        for p_ref, (off, k) in zip(refs[:npc], DPROJ_PIECES):
            acc = acc + _dot_nt(p_ref[...], w_ref[:, off:off + k])
        o_ref[...] = acc

    return pl.pallas_call(
        body, name="in_proj_bwd", grid=(t // TM,),
        in_specs=[_rows(TM, k) for _, k in DPROJ_PIECES] + [_const(w.shape), _rows(TM, D_MODEL)],
        out_specs=_rows(TM, D_MODEL), out_shape=jax.ShapeDtypeStruct((t, D_MODEL), F32),
        compiler_params=_cparams(("arbitrary",)),
    )(*pieces, w, dres)


def _wgrad_in(h0, pieces):
    t = h0.shape[0]
    npc = len(pieces)

    def body(*refs):
        h_ref, o_ref = refs[npc], refs[npc + 1]
        @pl.when(pl.program_id(0) == 0)
        def _():
            o_ref[...] = jnp.zeros_like(o_ref)

        hb = h_ref[...].astype(MXU_DTYPE)
        for p_ref, (off, k) in zip(refs[:npc], DPROJ_PIECES):
            o_ref[:, off:off + k] += _dot_tn(hb, p_ref[...])

    return pl.pallas_call(
        body, name="wgrad_in", grid=(t // TM,),
        in_specs=[_rows(TM, k) for _, k in DPROJ_PIECES] + [_rows(TM, D_MODEL)],
        out_specs=_const((D_MODEL, D_PACK)), out_shape=jax.ShapeDtypeStruct((D_MODEL, D_PACK), F32),
        compiler_params=_cparams(("arbitrary",)),
    )(*pieces, h0)


G_ROWS = 8192
G_PARTS = ((0, 4096), (4096, 2048), (6144, 2048))
W_IN_SHARD = 578
W_IN_PAD = 640
MISC_ROWS = 128
MISC_REP_ROW = 40
ROW_MISC = G_ROWS - MISC_ROWS
W_IN_BLOCK_ROWS = W_IN_PAD + MISC_ROWS


def _grad_row(name, l):
    base = 0 if l == 1 else 4096
    mid = base + 2048 if l == 1 else 6144
    return {"mlp_w1": base, "mlp_w2": base + 1024, "w_out": mid, "xa_wq": mid + 256, "xa_wk": mid + 512,
            "xa_wv": mid + 768, "xa_wo": mid + 1024, "w_in": mid + 1280}[name]


def _wgrad_flat(a, g, buf, *, mode, row_off, name):
    pieces = list(a) if isinstance(a, (list, tuple)) else [a]
    t = g.shape[0]
    tt = min(1024, t)
    ns = t // tt
    blk = D_MODEL

    def accumulate(o_ref, parts, s):
        @pl.when(s == 0)
        def _():
            o_ref[...] = jnp.zeros_like(o_ref)

        for q, v in parts:
            o_ref[q] += v

    if mode == "rows4":
        grid = (ns,)
        in_specs = [pl.BlockSpec((tt, p.shape[1]), lambda s: (s, 0)) for p in pieces]
        in_specs.append(pl.BlockSpec((tt, blk), lambda s: (s, 0)))
        out_spec = pl.BlockSpec((4, 256, FLAT), lambda s: (0, row_off // 256, 0))
        sem = ("arbitrary",)
        npc = len(pieces)

        def body(*refs):
            g_v = refs[npc][...]
            parts, q0 = [], 0
            for p_ref in refs[:npc]:
                full = _dot_tn(p_ref[...], g_v)
                nq = full.shape[0] // 256
                parts += [(q0 + q, full[q * 256:(q + 1) * 256]) for q in range(nq)]
                q0 += nq
            accumulate(refs[-1], parts, pl.program_id(0))
    else:
        grid = (2, ns)
        if mode == "rowblk":
            in_specs = [pl.BlockSpec((tt, 2 * blk), lambda q, s: (s, q)), pl.BlockSpec((tt, blk), lambda q, s: (s, 0))]
        else:
            in_specs = [pl.BlockSpec((tt, blk), lambda q, s: (s, 0)), pl.BlockSpec((tt, 2 * blk), lambda q, s: (s, q))]
        out_spec = pl.BlockSpec((2, blk, FLAT), lambda q, s: (q, row_off // blk, 0))
        sem = ("arbitrary", "arbitrary")

        def body(a_ref, g_ref, *rest):
            full = _dot_tn(a_ref[...], g_ref[...])
            if mode == "rowblk":
                parts = [(0, full[:blk]), (1, full[blk:])]
            else:
                parts = [(0, full[:, :blk]), (1, full[:, blk:])]
            accumulate(rest[-1], parts, pl.program_id(1))

    args = pieces + [g]
    aliases = {}
    if buf is not None:
        in_specs.append(pl.BlockSpec(memory_space=pl.ANY))
        args.append(buf)
        aliases = {len(args) - 1: 0}
    return pl.pallas_call(
        body, name=name, grid=grid, in_specs=in_specs, out_specs=out_spec,
        out_shape=jax.ShapeDtypeStruct((4, G_ROWS, FLAT), F32), input_output_aliases=aliases,
        compiler_params=_cparams(sem),
    )(*args)


def _outproj_ln_fwd(ys, h, w, g, b):
    t = h.shape[0]
    npc = len(ys)

    def body(*refs):
        h_ref, w_ref, g_ref, b_ref, hn_ref, xh_ref, rs_ref = refs[npc:]
        r = ALPHA * h_ref[...]
        off = 0
        for y_ref in refs[:npc]:
            k = y_ref.shape[1]
            r = r + _dot(y_ref[...], w_ref[off:off + k, :])
            off += k
        out, xhat, rstd = _ln_fwd(r, g_ref[...], b_ref[...])
        hn_ref[...] = out
        xh_ref[...] = xhat
        rs_ref[...] = rstd

    return pl.pallas_call(
        body, name="outproj_ln_fwd", grid=(t // TM,),
        in_specs=[_rows(TM, y.shape[1]) for y in ys] + [_rows(TM, D_MODEL), _const((D_MODEL, D_MODEL)),
                                                        _const((1, D_MODEL)), _const((1, D_MODEL))],
        out_specs=[_rows(TM, D_MODEL), _rows(TM, D_MODEL), _rows(TM, 1)],
        out_shape=[jax.ShapeDtypeStruct((t, D_MODEL), F32), jax.ShapeDtypeStruct((t, D_MODEL), F32),
                   jax.ShapeDtypeStruct((t, 1), F32)],
        compiler_params=_cparams(("arbitrary",)),
    )(*ys, h, w, g, b)


def _attn_probs(q, kb, hh):
    sl = slice(hh * XA_HEAD_DIM, (hh + 1) * XA_HEAD_DIM)
    s = _dot_nt(q[:, sl], kb[:, sl]) * (1.0 / math.sqrt(XA_HEAD_DIM))
    m = jnp.max(s, axis=-1, keepdims=True)
    e = jnp.exp(s - m)
    return e / jnp.sum(e, axis=-1, keepdims=True)


def _attn_ln_fwd(h1, wq, wo, kb, vb, g, b):
    t = h1.shape[0]

    def body(h_ref, wq_ref, wo_ref, k_ref, v_ref, g_ref, b_ref, hn_ref, xh_ref, rs_ref, o_ref):
        h = h_ref[...]
        q = _dot(h, wq_ref[...])
        kb_ = k_ref[...]
        vb_ = v_ref[...]
        for hh in range(XA_HEADS):
            sl = slice(hh * XA_HEAD_DIM, (hh + 1) * XA_HEAD_DIM)
            p = _attn_probs(q, kb_, hh)
            o_ref[:, sl] = _dot(p, vb_[:, sl]).astype(o_ref.dtype)
        r = ALPHA * h + _dot(o_ref[...], wo_ref[...])
        out, xhat, rstd = _ln_fwd(r, g_ref[...], b_ref[...])
        hn_ref[...] = out
        xh_ref[...] = xhat
        rs_ref[...] = rstd

    return pl.pallas_call(
        body, name="attn_ln_fwd", grid=(t // TM,),
        in_specs=[_rows(TM, D_MODEL), _const((D_MODEL, D_MODEL)), _const((D_MODEL, D_MODEL)),
                  _const((MEM_LEN, D_MODEL)), _const((MEM_LEN, D_MODEL)), _const((1, D_MODEL)), _const((1, D_MODEL))],
        out_specs=[_rows(TM, D_MODEL), _rows(TM, D_MODEL), _rows(TM, 1), _rows(TM, D_MODEL)],
        out_shape=[jax.ShapeDtypeStruct((t, D_MODEL), F32), jax.ShapeDtypeStruct((t, D_MODEL), F32),
                   jax.ShapeDtypeStruct((t, 1), F32), jax.ShapeDtypeStruct((t, D_MODEL), MXU_DTYPE)],
        compiler_params=_cparams(("arbitrary",)),
    )(h1, wq, wo, kb, vb, g, b)


def _attn_ln_bwd(dh2, xhat, rstd, g, h1, wq, wo, kb, vb):
    t = h1.shape[0]

    def body(dh_ref, xh_ref, rs_ref, g_ref, h_ref, wq_ref, wo_ref, k_ref, v_ref,
             dr_ref, dq_ref, dh1_ref, dk_ref, dv_ref, dg_ref, db_ref):
        i = pl.program_id(0)

        @pl.when(i == 0)
        def _():
            dk_ref[...] = jnp.zeros_like(dk_ref)
            dv_ref[...] = jnp.zeros_like(dv_ref)
            dg_ref[...] = jnp.zeros_like(dg_ref)
            db_ref[...] = jnp.zeros_like(db_ref)

        dout = dh_ref[...]
        xh = xh_ref[...]
        dg_ref[...] += _sum0(dout * xh)
        db_ref[...] += _sum0(dout)
        dr = _ln_bwd(dout, xh, rs_ref[...], g_ref[...])
        dr_ref[...] = dr.astype(dr_ref.dtype)
        do = _dot_nt(dr, wo_ref[...])
        h = h_ref[...]
        q = _dot(h, wq_ref[...])
        kb_ = k_ref[...]
        vb_ = v_ref[...]
        scale = 1.0 / math.sqrt(XA_HEAD_DIM)
        for hh in range(XA_HEADS):
            sl = slice(hh * XA_HEAD_DIM, (hh + 1) * XA_HEAD_DIM)
            p = _attn_probs(q, kb_, hh)
            do_h = do[:, sl]
            dp = _dot_nt(do_h, vb_[:, sl])
            ds = p * (dp - jnp.sum(dp * p, axis=-1, keepdims=True)) * scale
            dq_ref[:, sl] = _dot(ds, kb_[:, sl]).astype(dq_ref.dtype)
            dk_ref[:, sl] += _dot_tn(ds, q[:, sl])
            dv_ref[:, sl] += _dot_tn(p, do_h)
        dh1_ref[...] = ALPHA * dr + _dot_nt(dq_ref[...], wq_ref[...])

    return pl.pallas_call(
        body, name="attn_ln_bwd", grid=(t // TM,),
        in_specs=[_rows(TM, D_MODEL), _rows(TM, D_MODEL), _rows(TM, 1), _const((1, D_MODEL)), _rows(TM, D_MODEL),
                  _const((D_MODEL, D_MODEL)), _const((D_MODEL, D_MODEL)), _const((MEM_LEN, D_MODEL)),
                  _const((MEM_LEN, D_MODEL))],
        out_specs=[_rows(TM, D_MODEL), _rows(TM, D_MODEL), _rows(TM, D_MODEL), _const((MEM_LEN, D_MODEL)),
                   _const((MEM_LEN, D_MODEL)), _const((1, D_MODEL)), _const((1, D_MODEL))],
        out_shape=[jax.ShapeDtypeStruct((t, D_MODEL), MXU_DTYPE), jax.ShapeDtypeStruct((t, D_MODEL), MXU_DTYPE),
                   jax.ShapeDtypeStruct((t, D_MODEL), F32), jax.ShapeDtypeStruct((MEM_LEN, D_MODEL), F32),
                   jax.ShapeDtypeStruct((MEM_LEN, D_MODEL), F32), jax.ShapeDtypeStruct((1, D_MODEL), F32),
                   jax.ShapeDtypeStruct((1, D_MODEL), F32)],
        compiler_params=_cparams(("arbitrary",)),
    )(dh2, xhat, rstd, g, h1, wq, wo, kb, vb)


FF_CHUNK = 1024
N_FF = D_FF // FF_CHUNK


def _load_resident(pairs, sems):
    copies = [pltpu.make_async_copy(src, dst, sems.at[k]) for k, (src, dst) in enumerate(pairs)]
    for cp in copies:
        cp.start()
    for cp in copies:
        cp.wait()


def _mlp_ln_fwd(h2, w1, w2, g, b):
    t = h2.shape[0]

    def body(h_ref, w1_hbm, w2_hbm, g_ref, b_ref, hn_ref, xh_ref, rs_ref, hd_ref, w1_v, w2_v, acc_ref, sems):
        @pl.when(pl.program_id(0) == 0)
        def _():
            _load_resident([(w1_hbm, w1_v), (w2_hbm, w2_v)], sems)

        h = h_ref[...]
        hb = h.astype(MXU_DTYPE)
        acc_ref[...] = ALPHA * h
        for j in range(N_FF):
            sl = slice(j * FF_CHUNK, (j + 1) * FF_CHUNK)
            u = _dot(hb, w1_v[:, sl])
            hd = jnp.square(jnp.maximum(u, 0.0)).astype(MXU_DTYPE)
            hd_ref[:, sl] = hd
            acc_ref[...] += _dot(hd, w2_v[sl, :])
        out, xhat, rstd = _ln_fwd(acc_ref[...], g_ref[...], b_ref[...])
        hn_ref[...] = out
        xh_ref[...] = xhat
        rs_ref[...] = rstd

    return pl.pallas_call(
        body, name="mlp_ln_fwd", grid=(t // TM,),
        in_specs=[_rows(TM, D_MODEL), _hbm(), _hbm(), _const((1, D_MODEL)), _const((1, D_MODEL))],
        out_specs=[_rows(TM, D_MODEL), _rows(TM, D_MODEL), _rows(TM, 1), _rows(TM, D_FF)],
        out_shape=[jax.ShapeDtypeStruct((t, D_MODEL), F32), jax.ShapeDtypeStruct((t, D_MODEL), F32),
                   jax.ShapeDtypeStruct((t, 1), F32), jax.ShapeDtypeStruct((t, D_FF), MXU_DTYPE)],
        scratch_shapes=[pltpu.VMEM((D_MODEL, D_FF), MXU_DTYPE), pltpu.VMEM((D_FF, D_MODEL), MXU_DTYPE),
                        pltpu.VMEM((TM, D_MODEL), F32), pltpu.SemaphoreType.DMA((2,))],
        compiler_params=_cparams(("arbitrary",)),
    )(h2, w1, w2, g, b)


def _mlp_ln_bwd(dh3, xhat, rstd, g, hdn, w1, w2):
    t = dh3.shape[0]

    def body(dh_ref, xh_ref, rs_ref, g_ref, hd_ref, w1_hbm, w2_hbm,
             dr_ref, du_ref, dh2_ref, dg_ref, db_ref, w1_v, w2_v, acc_ref, sems):
        @pl.when(pl.program_id(0) == 0)
        def _():
            _load_resident([(w1_hbm, w1_v), (w2_hbm, w2_v)], sems)
            dg_ref[...] = jnp.zeros_like(dg_ref)
            db_ref[...] = jnp.zeros_like(db_ref)

        dout = dh_ref[...]
        xh = xh_ref[...]
        dg_ref[...] += _sum0(dout * xh)
        db_ref[...] += _sum0(dout)
        dr = _ln_bwd(dout, xh, rs_ref[...], g_ref[...])
        drb = dr.astype(MXU_DTYPE)
        dr_ref[...] = drb
        acc_ref[...] = ALPHA * dr
        for j in range(N_FF):
            sl = slice(j * FF_CHUNK, (j + 1) * FF_CHUNK)
            dhd = _dot_nt(drb, w2_v[sl, :])
            du = (dhd * (2.0 * jnp.sqrt(hd_ref[:, sl].astype(F32)))).astype(MXU_DTYPE)
            du_ref[:, sl] = du
            acc_ref[...] += _dot_nt(du, w1_v[:, sl])
        dh2_ref[...] = acc_ref[...]

    tm = TM // 2
    return pl.pallas_call(
        body, name="mlp_ln_bwd", grid=(t // tm,),
        in_specs=[_rows(tm, D_MODEL), _rows(tm, D_MODEL), _rows(tm, 1), _const((1, D_MODEL)), _rows(tm, D_FF),
                  _hbm(), _hbm()],
        out_specs=[_rows(tm, D_MODEL), _rows(tm, D_FF), _rows(tm, D_MODEL), _const((1, D_MODEL)),
                   _const((1, D_MODEL))],
        out_shape=[jax.ShapeDtypeStruct((t, D_MODEL), MXU_DTYPE), jax.ShapeDtypeStruct((t, D_FF), MXU_DTYPE),
                   jax.ShapeDtypeStruct((t, D_MODEL), F32), jax.ShapeDtypeStruct((1, D_MODEL), F32),
                   jax.ShapeDtypeStruct((1, D_MODEL), F32)],
        scratch_shapes=[pltpu.VMEM((D_MODEL, D_FF), MXU_DTYPE), pltpu.VMEM((D_FF, D_MODEL), MXU_DTYPE),
                        pltpu.VMEM((tm, D_MODEL), F32), pltpu.SemaphoreType.DMA((2,))],
        compiler_params=_cparams(("arbitrary",)),
    )(dh3, xhat, rstd, g, hdn, w1, w2)


def _outproj_ln_bwd(dh1, xhat, rstd, g, w):
    t = dh1.shape[0]

    def body(dh_ref, xh_ref, rs_ref, g_ref, w_ref, dr_ref, res_ref, dy_ref, dg_ref, db_ref):
        i = pl.program_id(0)

        @pl.when(i == 0)
        def _():
            dg_ref[...] = jnp.zeros_like(dg_ref)
            db_ref[...] = jnp.zeros_like(db_ref)

        dout = dh_ref[...]
        xh = xh_ref[...]
        dg_ref[...] += _sum0(dout * xh)
        db_ref[...] += _sum0(dout)
        dr = _ln_bwd(dout, xh, rs_ref[...], g_ref[...])
        dr_ref[...] = dr.astype(dr_ref.dtype)
        res_ref[...] = ALPHA * dr
        dy_ref[...] = _dot_nt(dr, w_ref[...])

    return pl.pallas_call(
        body, name="outproj_ln_bwd", grid=(t // TM,),
        in_specs=[_rows(TM, D_MODEL), _rows(TM, D_MODEL), _rows(TM, 1), _const((1, D_MODEL)),
                  _const((D_MODEL, D_MODEL))],
        out_specs=[_rows(TM, D_MODEL), _rows(TM, D_MODEL), _rows(TM, D_MODEL), _const((1, D_MODEL)),
                   _const((1, D_MODEL))],
        out_shape=[jax.ShapeDtypeStruct((t, D_MODEL), MXU_DTYPE), jax.ShapeDtypeStruct((t, D_MODEL), F32),
                   jax.ShapeDtypeStruct((t, D_MODEL), F32), jax.ShapeDtypeStruct((1, D_MODEL), F32),
                   jax.ShapeDtypeStruct((1, D_MODEL), F32)],
        compiler_params=_cparams(("arbitrary",)),
    )(dh1, xhat, rstd, g, w)


def _loss_fwd_bwd(h, target):
    t = h.shape[0]

    def body(h_ref, t_ref, l_ref, dh_ref):
        i = pl.program_id(0)

        @pl.when(i == 0)
        def _():
            l_ref[...] = jnp.zeros_like(l_ref)

        e = h_ref[...] - t_ref[...]
        dh_ref[...] = e * (1.0 / D_MODEL)
        per_tok = jnp.mean(e * e, axis=-1, keepdims=True)
        l_ref[...] += 0.5 * jnp.sum(per_tok, axis=0, keepdims=True)

    return pl.pallas_call(
        body, name="loss_fwd_bwd", grid=(t // TM,),
        in_specs=[_rows(TM, D_MODEL), _rows(TM, D_MODEL)],
        out_specs=[_const((1, 1)), _rows(TM, D_MODEL)],
        out_shape=[jax.ShapeDtypeStruct((1, 1), F32), jax.ShapeDtypeStruct((t, D_MODEL), F32)],
        compiler_params=_cparams(("arbitrary",)),
    )(h, target)


def _pick_col(x, idx):
    lane = lax.broadcasted_iota(jnp.int32, x.shape, 1)
    return jnp.sum(jnp.where(lane == idx, x, 0.0), axis=1, keepdims=True)


def _pick_row(x, idx):
    sub = lax.broadcasted_iota(jnp.int32, x.shape, 0)
    return jnp.sum(jnp.where(sub == idx, x, 0.0), axis=0, keepdims=True)


def _conv_taps(pad_ref, w, tm, base):
    acc = w[0:1, :] * pad_ref[base:base + tm, :]
    for k in range(1, 4):
        acc = acc + w[k:k + 1, :] * pad_ref[base + k:base + k + tm, :]
    return acc


def _ssd_chunk_common(adt_c, tri):
    cs = _dot_f32(tri, adt_c)
    return cs, cs.T, jnp.exp(cs)


def _ssd_head_terms(cs, cst, ecs, dt_c, h, tri):
    cs_col = _pick_col(cs, h)
    cs_row = _pick_row(cst, h)
    dt_col = _pick_col(dt_c, h)
    cs_last = cs_col[SSD_CHUNK - 1:SSD_CHUNK, :]
    lmat = jnp.exp(jnp.where(tri > 0.0, cs_col - cs_row, -1e30))
    ecs_col = _pick_col(ecs, h)
    decay_col = jnp.exp(cs_last - cs_col)
    return cs_col, dt_col, cs_last, lmat, ecs_col, decay_col


def _ssd_fwd(proj, cw, cb, dtb, a_neg, d_lanes, nw):
    t = proj.shape[0]
    tm = SSD_FWD_TM
    nt = t // tm
    ncq = tm // SSD_CHUNK
    hb = tm // SUBLANES

    def body(xbc_ref, halo_ref, z_ref, dt_ref, cw_ref, cb_ref, dtb_ref, a_ref, d_ref, nw_ref,
             y_ref, yy_ref, st_ref, xpad, xact, state):
        i = pl.program_id(0)

        @pl.when(i == 0)
        def _():
            state[...] = jnp.zeros_like(state)

        xpad[0:SUBLANES, :] = jnp.where(i > 0, halo_ref[...], 0.0)
        xpad[SUBLANES:SUBLANES + tm, :] = xbc_ref[...]
        acc = cb_ref[...] + _conv_taps(xpad, cw_ref[...], tm, SUBLANES - 3)
        xact[...] = acc * _sigmoid(acc)
        dt = _softplus(dt_ref[...] + dtb_ref[...])
        adt = dt * a_ref[...]
        r_i = lax.broadcasted_iota(jnp.int32, (SSD_CHUNK, SSD_CHUNK), 0)
        c_i = lax.broadcasted_iota(jnp.int32, (SSD_CHUNK, SSD_CHUNK), 1)
        tri = (r_i >= c_i).astype(F32)
        lane1 = lax.broadcasted_iota(jnp.int32, (1, LANES), 1)
        for c in range(ncq):
            sl = slice(c * SSD_CHUNK, (c + 1) * SSD_CHUNK)
            dt_c = dt[sl]
            cs, cst, ecs = _ssd_chunk_common(adt[sl], tri)
            for g in range(2):
                bg = xact[sl, 512 + g * 128:512 + (g + 1) * 128]
                cg = xact[sl, 768 + g * 128:768 + (g + 1) * 128]
                cbm = _dot_nt(cg, bg)
                for pr in range(2):
                    pi = g * 2 + pr
                    psl = slice(pi * 128, (pi + 1) * 128)
                    xp = xact[sl, psl]
                    prev = state[pi]
                    st_ref[c, pi] = prev
                    yp = xp * d_ref[:, psl]
                    new_s = jnp.zeros((SSD_STATE, LANES), F32)
                    dec_lane = jnp.zeros((1, LANES), F32)
                    for hh in range(2):
                        h = g * 4 + pr * 2 + hh
                        lm = (lane1 >= 64) if hh else (lane1 < 64)
                        _, dt_col, cs_last, lmat, ecs_col, decay_col = _ssd_head_terms(cs, cst, ecs, dt_c, h, tri)
                        xdt = jnp.where(lm, xp, 0.0) * dt_col
                        yp = yp + _dot(cbm * lmat, xdt)
                        yp = yp + _dot(cg * ecs_col, jnp.where(lm, prev, 0.0))
                        new_s = new_s + _dot_tn(bg * decay_col, xdt)
                        dec_lane = dec_lane + jnp.where(lm, jnp.exp(cs_last), 0.0)
                    state[pi] = prev * dec_lane + new_s
                    yy_ref[sl, psl] = yp
        yy = yy_ref[...]
        z = z_ref[...]
        yg = yy * (z * _sigmoid(z))
        ms = jnp.mean(yg * yg, axis=-1, keepdims=True)
        y_ref[...] = (yg * lax.rsqrt(ms + LN_EPS) * nw_ref[...]).astype(y_ref.dtype)

    halo_map = lambda i: (jnp.maximum(i * hb - 1, 0), 0)
    return pl.pallas_call(
        body, name="ssd_fwd", grid=(nt,),
        in_specs=[pl.BlockSpec((tm, SSD_XBC), lambda i: (i, 0)), pl.BlockSpec((SUBLANES, SSD_XBC), halo_map),
                  pl.BlockSpec((tm, SSD_WIDTH), lambda i: (i, P_Z // SSD_WIDTH)),
                  pl.BlockSpec((tm, LANES), lambda i: (i, P_DT // LANES)),
                  _const((4, SSD_XBC)), _const((1, SSD_XBC)), _const((1, LANES)), _const((1, LANES)),
                  _const((1, SSD_WIDTH)), _const((1, SSD_WIDTH))],
        out_specs=[_rows(tm, SSD_WIDTH), _rows(tm, SSD_WIDTH),
                   pl.BlockSpec((ncq, 4, SSD_STATE, LANES), lambda i: (i, 0, 0, 0))],
        out_shape=[jax.ShapeDtypeStruct((t, SSD_WIDTH), MXU_DTYPE), jax.ShapeDtypeStruct((t, SSD_WIDTH), F32),
                   jax.ShapeDtypeStruct((t // SSD_CHUNK, 4, SSD_STATE, LANES), F32)],
        scratch_shapes=[pltpu.VMEM((tm + SUBLANES, SSD_XBC), F32), pltpu.VMEM((tm, SSD_XBC), F32),
                        pltpu.VMEM((4, SSD_STATE, LANES), F32)],
        compiler_params=_cparams(("arbitrary",)),
    )(proj, proj, proj, proj, cw, cb, dtb, a_neg, d_lanes, nw)


def _ssd_bwd(dycat, proj, yy, states, cw, cb, dtb, a_neg, d_lanes, nw):
    t = proj.shape[0]
    tm = SSD_BWD_TM
    nt = t // tm
    ncq = tm // SSD_CHUNK
    hb = tm // SUBLANES

    def body(dy_ref, xbc_ref, halo_ref, z_ref, dt_ref, yy_ref, st_ref, cw_ref, cb_ref, dtb_ref, a_ref, d_ref, nw_ref,
             dxbc_ref, dz_ref, ddt_ref, dcw_ref, dcb_ref, ddtb_ref, da_ref, dd_ref, dnw_ref,
             xpad, xact, dxact, dpad, dstate, dnext):
        i = pl.program_id(0)

        @pl.when(i == 0)
        def _():
            for r in (dcw_ref, dcb_ref, ddtb_ref, da_ref, dd_ref, dnw_ref, dstate, dnext):
                r[...] = jnp.zeros_like(r)

        xpad[0:SUBLANES, :] = jnp.where(i < nt - 1, halo_ref[...], 0.0)
        xpad[SUBLANES:SUBLANES + tm, :] = xbc_ref[...]
        cw_v = cw_ref[...]
        acc = cb_ref[...] + _conv_taps(xpad, cw_v, tm, SUBLANES - 3)
        sig = _sigmoid(acc)
        xact[...] = acc * sig
        dt_raw = dt_ref[...] + dtb_ref[...]
        dt = _softplus(dt_raw)
        a_v = a_ref[...]
        adt = dt * a_v
        yy = yy_ref[...]
        z = z_ref[...]
        sz = _sigmoid(z)
        siluz = z * sz
        yg = yy * siluz
        ms = jnp.mean(yg * yg, axis=-1, keepdims=True)
        rinv = lax.rsqrt(ms + LN_EPS)
        dout = dy_ref[...]
        dnw_ref[...] += _sum0(dout * yg * rinv)
        dyn = dout * nw_ref[...]
        dyg = rinv * dyn - yg * (rinv * rinv * rinv) * jnp.mean(dyn * yg, axis=-1, keepdims=True)
        dyy = dyg * siluz
        dz_ref[...] = (dyg * yy * (sz * (1.0 + z * (1.0 - sz)))).astype(dz_ref.dtype)
        dd_ref[...] += _sum0(dyy * xact[:, 0:SSD_WIDTH])

        r_i = lax.broadcasted_iota(jnp.int32, (SSD_CHUNK, SSD_CHUNK), 0)
        c_i = lax.broadcasted_iota(jnp.int32, (SSD_CHUNK, SSD_CHUNK), 1)
        tri = (r_i >= c_i).astype(F32)
        lane1 = lax.broadcasted_iota(jnp.int32, (1, LANES), 1)
        for c in reversed(range(ncq)):
            sl = slice(c * SSD_CHUNK, (c + 1) * SSD_CHUNK)
            dt_c = dt[sl]
            cs, cst, ecs = _ssd_chunk_common(adt[sl], tri)
            cacc = jnp.zeros((SSD_CHUNK, LANES), F32)
            racc = jnp.zeros((SSD_CHUNK, LANES), F32)
            ddtx = jnp.zeros((SSD_CHUNK, LANES), F32)
            for g in range(2):
                bg = xact[sl, 512 + g * 128:512 + (g + 1) * 128]
                cg = xact[sl, 768 + g * 128:768 + (g + 1) * 128]
                cbm = _dot_nt(cg, bg)
                dcb_m = jnp.zeros((SSD_CHUNK, SSD_CHUNK), F32)
                dbg = jnp.zeros((SSD_CHUNK, SSD_STATE), F32)
                dcg = jnp.zeros((SSD_CHUNK, SSD_STATE), F32)
                for pr in range(2):
                    pi = g * 2 + pr
                    psl = slice(pi * 128, (pi + 1) * 128)
                    xp = xact[sl, psl]
                    dyp = dyy[sl, psl]
                    prev = st_ref[c, pi]
                    ds_all = dstate[pi]
                    dxdt_p = jnp.zeros((SSD_CHUNK, LANES), F32)
                    dprev_new = jnp.zeros((SSD_STATE, LANES), F32)
                    dec_lane = jnp.zeros((1, LANES), F32)
                    dt_lanes = jnp.zeros((SSD_CHUNK, LANES), F32)
                    for hh in range(2):
                        h = g * 4 + pr * 2 + hh
                        lm = (lane1 >= 64) if hh else (lane1 < 64)
                        oh_l = (c_i == h).astype(F32)
                        oh_s = (r_i == h).astype(F32)
                        _, dt_col, cs_last, lmat, ecs_col, decay_col = _ssd_head_terms(cs, cst, ecs, dt_c, h, tri)
                        gm = cbm * lmat
                        xm = jnp.where(lm, xp, 0.0)
                        xdt = xm * dt_col
                        dym = jnp.where(lm, dyp, 0.0)
                        prevm = jnp.where(lm, prev, 0.0)
                        dsm = jnp.where(lm, ds_all, 0.0)
                        bdec = bg * decay_col
                        dxdt = _dot_tn(gm, dym) + _dot(bdec, dsm)
                        dxdt_p = dxdt_p + dxdt
                        ddtx = ddtx + oh_l * jnp.sum(dxdt * xm, axis=1, keepdims=True)
                        dt_lanes = dt_lanes + jnp.where(lm, dt_col, 0.0)
                        dgm = _dot_nt(dym, xdt)
                        dcb_m = dcb_m + dgm * lmat
                        w = dgm * gm
                        cacc = cacc + oh_l * jnp.sum(w, axis=1, keepdims=True)
                        racc = racc - oh_s * jnp.sum(w, axis=0, keepdims=True)
                        dce = _dot_nt(dym, prevm)
                        dcg = dcg + dce * ecs_col
                        cacc = cacc + oh_l * (jnp.sum(dce * cg, axis=1, keepdims=True) * ecs_col)
                        dprev_new = dprev_new + _dot_tn(cg * ecs_col, dym)
                        dbdec = _dot_nt(xdt, dsm)
                        dbg = dbg + dbdec * decay_col
                        dd = jnp.sum(dbdec * bg, axis=1, keepdims=True) * decay_col
                        cacc = cacc - oh_l * dd
                        cd = jnp.exp(cs_last)
                        dlast = jnp.sum(dd, axis=0, keepdims=True) + jnp.sum(
                            jnp.sum(dsm * prevm, axis=1, keepdims=True), axis=0, keepdims=True) * cd
                        cacc = cacc + jnp.where((r_i == SSD_CHUNK - 1) & (c_i == h), dlast, 0.0)
                        dec_lane = dec_lane + jnp.where(lm, cd, 0.0)
                    dstate[pi] = ds_all * dec_lane + dprev_new
                    dxact[sl, psl] = dxdt_p * dt_lanes + dyp * d_ref[:, psl]
                dcg = dcg + _dot(dcb_m, bg)
                dbg = dbg + _dot_tn(dcb_m, cg)
                dxact[sl, 512 + g * 128:512 + (g + 1) * 128] = dbg
                dxact[sl, 768 + g * 128:768 + (g + 1) * 128] = dcg
            dcs = cacc + racc.T
            dadt = _dot_f32((r_i <= c_i).astype(F32), dcs)
            ddt = dadt * a_v + ddtx
            da_ref[...] += _sum0(dadt * dt_c)
            ddt_raw = ddt * _sigmoid(dt_raw[sl])
            ddt_ref[sl, :] = ddt_raw.astype(ddt_ref.dtype)
            ddtb_ref[...] += _sum0(ddt_raw)
        dacc = dxact[...] * (sig * (1.0 + acc * (1.0 - sig)))
        dcb_ref[...] += _sum0(dacc)
        for k in range(4):
            dcw_ref[k:k + 1, :] += _sum0(dacc * xpad[SUBLANES - 3 + k:SUBLANES - 3 + k + tm, :])
        dpad[0:tm, :] = dacc
        dpad[tm:tm + SUBLANES, :] = dnext[...]
        dx = cw_v[0:1, :] * dpad[3:3 + tm, :]
        for k in range(1, 4):
            dx = dx + cw_v[k:k + 1, :] * dpad[3 - k:3 - k + tm, :]
        dxbc_ref[...] = dx.astype(dxbc_ref.dtype)
        dnext[...] = dacc[0:SUBLANES, :]

    rev = lambda i: nt - 1 - i
    halo_map = lambda i: (jnp.maximum(rev(i) * hb - 1, 0), 0)
    rrow = lambda n, col=0: pl.BlockSpec((tm, n), lambda i: (rev(i), col))
    return pl.pallas_call(
        body, name="ssd_bwd", grid=(nt,),
        in_specs=[rrow(SSD_WIDTH), rrow(SSD_XBC), pl.BlockSpec((SUBLANES, SSD_XBC), halo_map),
                  rrow(SSD_WIDTH, P_Z // SSD_WIDTH), rrow(LANES, P_DT // LANES), rrow(SSD_WIDTH),
                  pl.BlockSpec((ncq, 4, SSD_STATE, LANES), lambda i: (rev(i), 0, 0, 0)),
                  _const((4, SSD_XBC)), _const((1, SSD_XBC)), _const((1, LANES)), _const((1, LANES)),
                  _const((1, SSD_WIDTH)), _const((1, SSD_WIDTH))],
        out_specs=[rrow(SSD_XBC), rrow(SSD_WIDTH), rrow(LANES), _const((SUBLANES, SSD_XBC)), _const((1, SSD_XBC)),
                   _const((1, LANES)), _const((1, LANES)), _const((1, SSD_WIDTH)), _const((1, SSD_WIDTH))],
        out_shape=[jax.ShapeDtypeStruct((t, SSD_XBC), MXU_DTYPE), jax.ShapeDtypeStruct((t, SSD_WIDTH), MXU_DTYPE),
                   jax.ShapeDtypeStruct((t, LANES), MXU_DTYPE), jax.ShapeDtypeStruct((SUBLANES, SSD_XBC), F32),
                   jax.ShapeDtypeStruct((1, SSD_XBC), F32), jax.ShapeDtypeStruct((1, LANES), F32),
                   jax.ShapeDtypeStruct((1, LANES), F32), jax.ShapeDtypeStruct((1, SSD_WIDTH), F32),
                   jax.ShapeDtypeStruct((1, SSD_WIDTH), F32)],
        scratch_shapes=[pltpu.VMEM((tm + SUBLANES, SSD_XBC), F32), pltpu.VMEM((tm, SSD_XBC), F32),
                        pltpu.VMEM((tm, SSD_XBC), F32), pltpu.VMEM((tm + SUBLANES, SSD_XBC), F32),
                        pltpu.VMEM((4, SSD_STATE, LANES), F32), pltpu.VMEM((SUBLANES, SSD_XBC), F32)],
        compiler_params=_cparams(("arbitrary",)),
    )(dycat, proj, proj, proj, proj, yy, states, cw, cb, dtb, a_neg, d_lanes, nw)


def _cmul_add(ar, ai, br, bi, cr, ci):
    return ar + br * cr - bi * ci, ai + br * ci + bi * cr


def _s5_fwd(proj, bre, bim, cre, cim, d_skip, glu_w, glu_b, coef):
    t = proj.shape[0]
    tm = SCAN_TM
    ng = tm // SUBLANES

    def body(u_ref, bre_ref, bim_ref, cre_ref, cim_ref, d_ref, w_ref, b_ref, coef_ref,
             y_ref, y2_ref, hre_ref, him_ref, carry):
        i = pl.program_id(0)

        @pl.when(i == 0)
        def _():
            carry[...] = jnp.zeros_like(carry)

        u = u_ref[...]
        hre_ref[...] = _dot(u, bre_ref[...])
        him_ref[...] = _dot(u, bim_ref[...])

        def step(gi, car):
            cr_, ci_ = car
            rows = pl.ds(pl.multiple_of(gi * SUBLANES, SUBLANES), SUBLANES)
            r = hre_ref[rows, :]
            m = him_ref[rows, :]
            for k, sh in enumerate((1, 2, 4)):
                r, m = _cmul_add(r, m, coef_ref[k, 0], coef_ref[k, 1], pltpu.roll(r, sh, 0), pltpu.roll(m, sh, 0))
            r, m = _cmul_add(r, m, coef_ref[3, 0], coef_ref[3, 1], cr_, ci_)
            hre_ref[rows, :] = r
            him_ref[rows, :] = m
            return (jnp.broadcast_to(r[SUBLANES - 1:SUBLANES, :], r.shape),
                    jnp.broadcast_to(m[SUBLANES - 1:SUBLANES, :], m.shape))

        cr_, ci_ = lax.fori_loop(0, ng, step, (carry[0], carry[1]))
        carry[0] = cr_
        carry[1] = ci_
        y2 = _dot(hre_ref[...], cre_ref[...]) - _dot(him_ref[...], cim_ref[...]) + d_ref[...] * u
        y2_ref[...] = y2
        ya = _gelu(y2)
        y_ref[...] = (ya * _sigmoid(_dot(ya, w_ref[...]) + b_ref[...])).astype(y_ref.dtype)

    return pl.pallas_call(
        body, name="s5_fwd", grid=(t // tm,),
        in_specs=[pl.BlockSpec((tm, S5_WIDTH), lambda i: (i, P_U // S5_WIDTH)),
                  _const((S5_WIDTH, S5_NSTATE)), _const((S5_WIDTH, S5_NSTATE)), _const((S5_NSTATE, S5_WIDTH)),
                  _const((S5_NSTATE, S5_WIDTH)), _const((1, S5_WIDTH)), _const((S5_WIDTH, S5_WIDTH)),
                  _const((1, S5_WIDTH)), _const((5, 2, SUBLANES, S5_NSTATE))],
        out_specs=[_rows(tm, S5_WIDTH), _rows(tm, S5_WIDTH), _rows(tm, S5_NSTATE), _rows(tm, S5_NSTATE)],
        out_shape=[jax.ShapeDtypeStruct((t, S5_WIDTH), MXU_DTYPE), jax.ShapeDtypeStruct((t, S5_WIDTH), F32),
                   jax.ShapeDtypeStruct((t, S5_NSTATE), F32), jax.ShapeDtypeStruct((t, S5_NSTATE), F32)],
        scratch_shapes=[pltpu.VMEM((2, SUBLANES, S5_NSTATE), F32)],
        compiler_params=_cparams(("arbitrary",)),
    )(proj, bre, bim, cre, cim, d_skip, glu_w, glu_b, coef)


def _s5_bwd(dycat, proj, y2, hre, him, bre, bim, cre, cim, d_skip, glu_w, glu_b, rcoef):
    t = proj.shape[0]
    tm = SCAN_BWD_TM
    nt = t // tm
    ng = tm // SUBLANES
    hb = tm // SUBLANES

    def body(dy_ref, u_ref, y2_ref, hre_ref, him_ref, hre_halo, him_halo, bre_ref, bim_ref, cre_ref, cim_ref, d_ref,
             w_ref, b_ref, coef_ref,
             du_ref, dbre_ref, dbim_ref, dcre_ref, dcim_ref, dlam_ref, dd_ref, dw_ref, dgb_ref,
             gre, gim, hpre, hpim, carry):
        i = pl.program_id(0)

        @pl.when(i == 0)
        def _():
            for r in (dbre_ref, dbim_ref, dcre_ref, dcim_ref, dlam_ref, dd_ref, dw_ref, dgb_ref, carry):
                r[...] = jnp.zeros_like(r)

        u = u_ref[...]
        y2 = y2_ref[...]
        dout = dy_ref[...]
        ya = _gelu(y2)
        sg = _sigmoid(_dot(ya, w_ref[...]) + b_ref[...])
        dv = dout * ya * sg * (1.0 - sg)
        dya = dout * sg + _dot_nt(dv, w_ref[...])
        dw_ref[...] += _dot_tn(ya, dv)
        dgb_ref[...] += _sum0(dv)
        dy2 = dya * _gelu_grad(y2)
        dd_ref[...] += _sum0(dy2 * u)
        hre_v = hre_ref[...]
        him_v = him_ref[...]
        dcre_ref[...] += _dot_tn(hre_v, dy2)
        dcim_ref[...] -= _dot_tn(him_v, dy2)
        gre[...] = _dot_nt(dy2, cre_ref[...])
        gim[...] = -_dot_nt(dy2, cim_ref[...])
        first = i == nt - 1
        hpre[0:SUBLANES, :] = jnp.where(first, 0.0, hre_halo[...])
        hpim[0:SUBLANES, :] = jnp.where(first, 0.0, him_halo[...])
        hpre[SUBLANES:SUBLANES + tm, :] = hre_v
        hpim[SUBLANES:SUBLANES + tm, :] = him_v
        row0 = lax.broadcasted_iota(jnp.int32, (SUBLANES, S5_NSTATE), 0) == 0

        def step(k, car):
            cr_, ci_, dlr, dli = car
            gi = ng - 1 - k
            rows = pl.ds(pl.multiple_of(gi * SUBLANES, SUBLANES), SUBLANES)
            nrows = pl.ds(pl.multiple_of(gi * SUBLANES + SUBLANES, SUBLANES), SUBLANES)
            r = gre[rows, :]
            m = gim[rows, :]
            for kk, sh in enumerate((1, 2, 4)):
                r, m = _cmul_add(r, m, coef_ref[kk, 0], coef_ref[kk, 1], pltpu.roll(r, SUBLANES - sh, 0),
                                 pltpu.roll(m, SUBLANES - sh, 0))
            r, m = _cmul_add(r, m, coef_ref[3, 0], coef_ref[3, 1], cr_, ci_)
            gre[rows, :] = r
            gim[rows, :] = m
            pr_ = hpre[rows, :]
            pm_ = hpim[rows, :]
            hr_ = jnp.where(row0, jnp.broadcast_to(pr_[SUBLANES - 1:SUBLANES, :], pr_.shape),
                            pltpu.roll(hpre[nrows, :], 1, 0))
            hm_ = jnp.where(row0, jnp.broadcast_to(pm_[SUBLANES - 1:SUBLANES, :], pm_.shape),
                            pltpu.roll(hpim[nrows, :], 1, 0))
            dlr = dlr + hr_ * r + hm_ * m
            dli = dli + hr_ * m - hm_ * r
            return (jnp.broadcast_to(r[0:1, :], r.shape), jnp.broadcast_to(m[0:1, :], m.shape), dlr, dli)

        z8 = jnp.zeros((SUBLANES, S5_NSTATE), F32)
        cr_, ci_, dlr, dli = lax.fori_loop(0, ng, step, (carry[0], carry[1], z8, z8))
        carry[0] = cr_
        carry[1] = ci_
        dlam_ref[0] += dlr
        dlam_ref[1] += dli
        g_re = gre[...]
        g_im = gim[...]
        du_ref[...] = (dy2 * d_ref[...] + _dot_nt(g_re, bre_ref[...]) + _dot_nt(g_im, bim_ref[...])
                       ).astype(du_ref.dtype)
        dbre_ref[...] += _dot_tn(u, g_re)
        dbim_ref[...] += _dot_tn(u, g_im)

    rev = lambda i: nt - 1 - i
    rrow = lambda n, col=0: pl.BlockSpec((tm, n), lambda i: (rev(i), col))
    halo = pl.BlockSpec((SUBLANES, S5_NSTATE), lambda i: (jnp.maximum(rev(i) * hb - 1, 0), 0))
    return pl.pallas_call(
        body, name="s5_bwd", grid=(nt,),
        in_specs=[rrow(S5_WIDTH, 512 // S5_WIDTH), rrow(S5_WIDTH, P_U // S5_WIDTH), rrow(S5_WIDTH),
                  rrow(S5_NSTATE), rrow(S5_NSTATE), halo, halo,
                  _const((S5_WIDTH, S5_NSTATE)), _const((S5_WIDTH, S5_NSTATE)), _const((S5_NSTATE, S5_WIDTH)),
                  _const((S5_NSTATE, S5_WIDTH)), _const((1, S5_WIDTH)), _const((S5_WIDTH, S5_WIDTH)),
                  _const((1, S5_WIDTH)), _const((5, 2, SUBLANES, S5_NSTATE))],
        out_specs=[rrow(S5_WIDTH), _const((S5_WIDTH, S5_NSTATE)), _const((S5_WIDTH, S5_NSTATE)),
                   _const((S5_NSTATE, S5_WIDTH)), _const((S5_NSTATE, S5_WIDTH)), _const((2, SUBLANES, S5_NSTATE)),
                   _const((1, S5_WIDTH)), _const((S5_WIDTH, S5_WIDTH)), _const((1, S5_WIDTH))],
        out_shape=[jax.ShapeDtypeStruct((t, S5_WIDTH), MXU_DTYPE), jax.ShapeDtypeStruct((S5_WIDTH, S5_NSTATE), F32),
                   jax.ShapeDtypeStruct((S5_WIDTH, S5_NSTATE), F32), jax.ShapeDtypeStruct((S5_NSTATE, S5_WIDTH), F32),
                   jax.ShapeDtypeStruct((S5_NSTATE, S5_WIDTH), F32),
                   jax.ShapeDtypeStruct((2, SUBLANES, S5_NSTATE), F32), jax.ShapeDtypeStruct((1, S5_WIDTH), F32),
                   jax.ShapeDtypeStruct((S5_WIDTH, S5_WIDTH), F32), jax.ShapeDtypeStruct((1, S5_WIDTH), F32)],
        scratch_shapes=[pltpu.VMEM((tm, S5_NSTATE), F32), pltpu.VMEM((tm, S5_NSTATE), F32),
                        pltpu.VMEM((tm + SUBLANES, S5_NSTATE), F32), pltpu.VMEM((tm + SUBLANES, S5_NSTATE), F32),
                        pltpu.VMEM((2, SUBLANES, S5_NSTATE), F32)],
        compiler_params=_cparams(("arbitrary",)),
    )(dycat, proj, y2, hre, him, hre, him, bre, bim, cre, cim, d_skip, glu_w, glu_b, rcoef)


def _rg_gates(xc, wa, ba, wx, bx, nsp):
    r = _sigmoid(_dot(xc, wa) + ba)
    ig = _sigmoid(_dot(xc, wx) + bx)
    log_a = nsp * r
    a = jnp.exp(log_a)
    mult = jnp.sqrt(-_expm1(2.0 * log_a))
    return r, ig, a, mult


def _rg_fwd(proj, cw, cb, wa, ba, wx, bx, nsp):
    t = proj.shape[0]
    tm = SCAN_TM
    ng = tm // SUBLANES
    hb = tm // SUBLANES

    def body(x_ref, halo_ref, gt_ref, cw_ref, cb_ref, wa_ref, ba_ref, wx_ref, bx_ref, nsp_ref,
             y_ref, h_ref, xpad, abuf, carry):
        i = pl.program_id(0)

        @pl.when(i == 0)
        def _():
            carry[...] = jnp.zeros_like(carry)

        xpad[0:SUBLANES, :] = jnp.where(i > 0, halo_ref[...], 0.0)
        xpad[SUBLANES:SUBLANES + tm, :] = x_ref[...]
        xc = cb_ref[...] + _conv_taps(xpad, cw_ref[...], tm, SUBLANES - 3)
        _, ig, a, mult = _rg_gates(xc, wa_ref[...], ba_ref[...], wx_ref[...], bx_ref[...], nsp_ref[...])
        abuf[...] = a
        h_ref[...] = mult * (ig * xc)
        sub = lax.broadcasted_iota(jnp.int32, (SUBLANES, RG_WIDTH), 0)

        def step(gi, car):
            rows = pl.ds(pl.multiple_of(gi * SUBLANES, SUBLANES), SUBLANES)
            av = abuf[rows, :]
            bv = h_ref[rows, :]
            for sh in (1, 2, 4):
                m = sub >= sh
                bv = jnp.where(m, av * pltpu.roll(bv, sh, 0) + bv, bv)
                av = jnp.where(m, av * pltpu.roll(av, sh, 0), av)
            hv = bv + av * car
            h_ref[rows, :] = hv
            return jnp.broadcast_to(hv[SUBLANES - 1:SUBLANES, :], hv.shape)

        carry[...] = lax.fori_loop(0, ng, step, carry[...])
        y_ref[...] = (h_ref[...] * _gelu(gt_ref[...])).astype(y_ref.dtype)

    return pl.pallas_call(
        body, name="rg_fwd", grid=(t // tm,),
        in_specs=[pl.BlockSpec((tm, RG_WIDTH), lambda i: (i, P_XRG // RG_WIDTH)),
                  pl.BlockSpec((SUBLANES, RG_WIDTH), lambda i: (jnp.maximum(i * hb - 1, 0), P_XRG // RG_WIDTH)),
                  pl.BlockSpec((tm, RG_WIDTH), lambda i: (i, P_GRG // RG_WIDTH)),
                  _const((4, RG_WIDTH)), _const((1, RG_WIDTH)), _const((RG_WIDTH, RG_WIDTH)), _const((1, RG_WIDTH)),
                  _const((RG_WIDTH, RG_WIDTH)), _const((1, RG_WIDTH)), _const((1, RG_WIDTH))],
        out_specs=[_rows(tm, RG_WIDTH), _rows(tm, RG_WIDTH)],
        out_shape=[jax.ShapeDtypeStruct((t, RG_WIDTH), MXU_DTYPE), jax.ShapeDtypeStruct((t, RG_WIDTH), F32)],
        scratch_shapes=[pltpu.VMEM((tm + SUBLANES, RG_WIDTH), F32), pltpu.VMEM((tm, RG_WIDTH), F32),
                        pltpu.VMEM((SUBLANES, RG_WIDTH), F32)],
        compiler_params=_cparams(("arbitrary",)),
    )(proj, proj, proj, cw, cb, wa, ba, wx, bx, nsp)


def _rg_bwd(dycat, proj, hs, cw, cb, wa, ba, wx, bx, nsp):
    t = proj.shape[0]
    tm = SCAN_TM
    nt = t // tm
    ng = tm // SUBLANES
    hb = tm // SUBLANES

    def body(dy_ref, x_ref, halo_ref, gt_ref, h_ref, h_halo, cw_ref, cb_ref, wa_ref, ba_ref, wx_ref, bx_ref, nsp_ref,
             dx_ref, dgt_ref, dcw_ref, dcb_ref, dwa_ref, dba_ref, dwx_ref, dbx_ref, dnsp_ref,
             xpad, abuf, gbuf, hpad, dabuf, dpad, carry, dnext):
        i = pl.program_id(0)

        @pl.when(i == 0)
        def _():
            for r in (dcw_ref, dcb_ref, dwa_ref, dba_ref, dwx_ref, dbx_ref, dnsp_ref, carry, dnext):
                r[...] = jnp.zeros_like(r)

        first = i == nt - 1
        xpad[0:SUBLANES, :] = jnp.where(first, 0.0, halo_ref[...])
        xpad[SUBLANES:SUBLANES + tm, :] = x_ref[...]
        cw_v = cw_ref[...]
        xc = cb_ref[...] + _conv_taps(xpad, cw_v, tm, SUBLANES - 3)
        nsp_v = nsp_ref[...]
        r, ig, a, mult = _rg_gates(xc, wa_ref[...], ba_ref[...], wx_ref[...], bx_ref[...], nsp_v)
        abuf[...] = a
        hv = h_ref[...]
        hpad[0:SUBLANES, :] = jnp.where(first, 0.0, h_halo[...])
        hpad[SUBLANES:SUBLANES + tm, :] = hv
        gt = gt_ref[...]
        dout = dy_ref[...]
        dgt_ref[...] = (dout * hv * _gelu_grad(gt)).astype(dgt_ref.dtype)
        gbuf[...] = dout * _gelu(gt)
        sub = lax.broadcasted_iota(jnp.int32, (SUBLANES, RG_WIDTH), 0)
        last_row = sub == SUBLANES - 1
        row0 = sub == 0

        def step(k, car):
            gi = ng - 1 - k
            rows = pl.ds(pl.multiple_of(gi * SUBLANES, SUBLANES), SUBLANES)
            nrows = pl.ds(pl.multiple_of(gi * SUBLANES + SUBLANES, SUBLANES), SUBLANES)
            av = abuf[rows, :]
            bv = gbuf[rows, :] + jnp.where(last_row, car, 0.0)
            ev = jnp.where(last_row, 0.0, pltpu.roll(av, SUBLANES - 1, 0))
            for sh in (1, 2, 4):
                m = sub < SUBLANES - sh
                bv = jnp.where(m, bv + ev * pltpu.roll(bv, SUBLANES - sh, 0), bv)
                ev = jnp.where(m, ev * pltpu.roll(ev, SUBLANES - sh, 0), 0.0)
            gbuf[rows, :] = bv
            pv = hpad[rows, :]
            hprev = jnp.where(row0, jnp.broadcast_to(pv[SUBLANES - 1:SUBLANES, :], pv.shape),
                              pltpu.roll(hpad[nrows, :], 1, 0))
            dabuf[rows, :] = bv * hprev
            return jnp.broadcast_to((av * bv)[0:1, :], bv.shape)

        carry[...] = lax.fori_loop(0, ng, step, carry[...])
        gv = gbuf[...]
        da = dabuf[...]
        ix = ig * xc
        dmult = gv * ix
        dig = gv * mult * xc
        dxc = gv * mult * ig
        dlog_a = da * a - dmult * (a * a) / mult
        dnsp_ref[...] += _sum0(dlog_a * r)
        dpr = dlog_a * nsp_v * r * (1.0 - r)
        dpi = dig * ig * (1.0 - ig)
        dxc = dxc + _dot_nt(dpr, wa_ref[...]) + _dot_nt(dpi, wx_ref[...])
        dwa_ref[...] += _dot_tn(xc, dpr)
        dwx_ref[...] += _dot_tn(xc, dpi)
        dba_ref[...] += _sum0(dpr)
        dbx_ref[...] += _sum0(dpi)
        dcb_ref[...] += _sum0(dxc)
        for k in range(4):
            dcw_ref[k:k + 1, :] += _sum0(dxc * xpad[SUBLANES - 3 + k:SUBLANES - 3 + k + tm, :])
        dpad[0:tm, :] = dxc
        dpad[tm:tm + SUBLANES, :] = dnext[...]
        dx = cw_v[0:1, :] * dpad[3:3 + tm, :]
        for k in range(1, 4):
            dx = dx + cw_v[k:k + 1, :] * dpad[3 - k:3 - k + tm, :]
        dx_ref[...] = dx.astype(dx_ref.dtype)
        dnext[...] = dxc[0:SUBLANES, :]

    rev = lambda i: nt - 1 - i
    rrow = lambda n, col=0: pl.BlockSpec((tm, n), lambda i: (rev(i), col))
    sq = _const((RG_WIDTH, RG_WIDTH))
    vec = _const((1, RG_WIDTH))
    return pl.pallas_call(
        body, name="rg_bwd", grid=(nt,),
        in_specs=[rrow(RG_WIDTH, 768 // RG_WIDTH), rrow(RG_WIDTH, P_XRG // RG_WIDTH),
                  pl.BlockSpec((SUBLANES, RG_WIDTH), lambda i: (jnp.maximum(rev(i) * hb - 1, 0), P_XRG // RG_WIDTH)),
                  rrow(RG_WIDTH, P_GRG // RG_WIDTH), rrow(RG_WIDTH),
                  pl.BlockSpec((SUBLANES, RG_WIDTH), lambda i: (jnp.maximum(rev(i) * hb - 1, 0), 0)),
                  _const((4, RG_WIDTH)), vec, sq, vec, sq, vec, vec],
        out_specs=[rrow(RG_WIDTH), rrow(RG_WIDTH), _const((SUBLANES, RG_WIDTH)), vec, sq, vec, sq, vec, vec],
        out_shape=[jax.ShapeDtypeStruct((t, RG_WIDTH), MXU_DTYPE), jax.ShapeDtypeStruct((t, RG_WIDTH), MXU_DTYPE),
                   jax.ShapeDtypeStruct((SUBLANES, RG_WIDTH), F32), jax.ShapeDtypeStruct((1, RG_WIDTH), F32),
                   jax.ShapeDtypeStruct((RG_WIDTH, RG_WIDTH), F32), jax.ShapeDtypeStruct((1, RG_WIDTH), F32),
                   jax.ShapeDtypeStruct((RG_WIDTH, RG_WIDTH), F32), jax.ShapeDtypeStruct((1, RG_WIDTH), F32),
                   jax.ShapeDtypeStruct((1, RG_WIDTH), F32)],
        scratch_shapes=[pltpu.VMEM((tm + SUBLANES, RG_WIDTH), F32), pltpu.VMEM((tm, RG_WIDTH), F32),
                        pltpu.VMEM((tm, RG_WIDTH), F32), pltpu.VMEM((tm + SUBLANES, RG_WIDTH), F32),
                        pltpu.VMEM((tm, RG_WIDTH), F32), pltpu.VMEM((tm + SUBLANES, RG_WIDTH), F32),
                        pltpu.VMEM((SUBLANES, RG_WIDTH), F32), pltpu.VMEM((SUBLANES, RG_WIDTH), F32)],
        compiler_params=_cparams(("arbitrary",)),
    )(dycat, proj, proj, proj, hs, hs, cw, cb, wa, ba, wx, bx, nsp)


def _block_diag(blocks):
    g, a, b = blocks.shape
    eye = jnp.eye(g, dtype=blocks.dtype)
    return (eye[:, None, :, None] * blocks[:, :, None, :]).reshape(g * a, g * b)


def _block_diag_extract(m, g):
    a, b = m.shape[0] // g, m.shape[1] // g
    m4 = m.reshape(g, a, g, b)
    idx = jnp.arange(g)
    return m4[idx, :, idx, :]


def _s5_prepare(lam_re, lam_im, log_step, b_re, b_im, c_re, c_im):
    step = jnp.exp(log_step)[:, None]
    mag = jnp.exp(lam_re * step)
    lbr = mag * jnp.cos(lam_im * step)
    lbi = mag * jnp.sin(lam_im * step)
    nr, ni = lbr - 1.0, lbi
    den = lam_re * lam_re + lam_im * lam_im
    cr = (nr * lam_re + ni * lam_im) / den
    ci = (ni * lam_re - nr * lam_im) / den
    bbr = cr[..., None] * b_re - ci[..., None] * b_im
    bbi = cr[..., None] * b_im + ci[..., None] * b_re
    bre = _block_diag(jnp.swapaxes(bbr, 1, 2))
    bim = _block_diag(jnp.swapaxes(bbi, 1, 2))
    cre = _block_diag(jnp.swapaxes(c_re, 1, 2))
    cim = _block_diag(jnp.swapaxes(c_im, 1, 2))
    return lbr.reshape(-1), lbi.reshape(-1), bre, bim, cre, cim


def _s5_scan_coef(lbr, lbi, reverse):
    if reverse:
        lbi = -lbi
    pr, pi = [lbr], [lbi]
    for _ in range(7):
        pr, pi = pr + [pr[-1] * lbr - pi[-1] * lbi], pi + [pr[-1] * lbi + pi[-1] * lbr]
    row = jnp.arange(SUBLANES)[:, None]
    tabs = []
    for sh in (1, 2, 4):
        keep = (row < SUBLANES - sh) if reverse else (row >= sh)
        tabs.append(jnp.stack([jnp.where(keep, pr[sh - 1][None, :], 0.0), jnp.where(keep, pi[sh - 1][None, :], 0.0)]))
    powr = jnp.stack(pr)
    powi = jnp.stack(pi)
    if reverse:
        powr, powi = powr[::-1], powi[::-1]
    tabs.append(jnp.stack([powr, powi]))
    tabs.append(jnp.zeros_like(tabs[-1]))
    return jnp.stack(tabs).astype(F32)


def _xy_peers():
    x, y, c = lax.axis_index("x"), lax.axis_index("y"), lax.axis_index("c")
    return x, y, c, [(1 - x, y), (x, 1 - y), (1 - x, 1 - y)]


def _hbm():
    return pl.BlockSpec(memory_space=pl.ANY)


def _xy_allgather(buf, *, name):
    n, w = buf.shape

    def body(x_ref, out_ref, send_sems, recv_sems, local_sem):
        x, y, c, peers = _xy_peers()
        me = 2 * x + y
        own = pltpu.make_async_copy(x_ref, out_ref.at[me], local_sem)
        own.start()
        sends = []
        for k, (px, py) in enumerate(peers):
            cp = pltpu.make_async_remote_copy(src_ref=x_ref, dst_ref=out_ref.at[me], send_sem=send_sems.at[k],
                                              recv_sem=recv_sems.at[k], device_id=(px, py, c), device_id_type=MESH)
            cp.start()
            sends.append(cp)
        for k, (px, py) in enumerate(peers):
            pltpu.make_async_remote_copy(src_ref=x_ref, dst_ref=out_ref.at[2 * px + py], send_sem=send_sems.at[k],
                                         recv_sem=recv_sems.at[k], device_id=(px, py, c),
                                         device_id_type=MESH).wait_recv()
        for cp in sends:
            cp.wait_send()
        own.wait()

    return pl.pallas_call(
        body, name=name, in_specs=[_hbm()], out_specs=_hbm(),
        out_shape=jax.ShapeDtypeStruct((4, n, w), buf.dtype),
        scratch_shapes=[pltpu.SemaphoreType.DMA((3,)), pltpu.SemaphoreType.DMA((3,)), pltpu.SemaphoreType.DMA],
    )(buf)


def _remote(src, dst, send_sem, recv_sem, dev):
    return pltpu.make_async_remote_copy(src_ref=src, dst_ref=dst, send_sem=send_sem, recv_sem=recv_sem,
                                        device_id=dev, device_id_type=MESH)


LAYER_GATHERED = (
    ("ssd_conv_w", (4, 256), 1), ("rg_conv_w", (4, LANES), 1),
    ("w_in", (1024, W_IN_PAD), 1), ("s5_glu_w", (64, 256), 0), ("w_out", (256, 1024), 0), ("xa_wq", (256, 1024), 0),
    ("xa_wk", (256, 1024), 0), ("xa_wv", (256, 1024), 0), ("xa_wo", (256, 1024), 0), ("mlp_w1", (1024, 1024), 1),
    ("mlp_w2", (1024, 1024), 0),
)
N_GATHERED = len(LAYER_GATHERED)
WAIT_GROUPS = ((0, 1, 2, 3), (4,), (5, 6, 7, 8), (9, 10))
RG_CONV_SHARD = RG_WIDTH // 4
N_GATHER_COPIES = 3 * N_GATHERED * DEPTH


def _gather_part(ref, t, pos):
    _, shp, ax = LAYER_GATHERED[t % N_GATHERED]
    idx = tuple(pl.ds(pos * shp[ax], shp[ax]) if d == ax else slice(None) for d in range(len(shp)))
    return ref.at[idx]


def _gather_start(shards):
    n = len(shards)
    lands = []
    for t, s in enumerate(shards):
        _, shp, ax = LAYER_GATHERED[t % N_GATHERED]
        full = shp[:ax] + (4 * shp[ax],) + shp[ax + 1:]
        lands.append(pltpu.with_memory_space_constraint(lax.empty(full, s.dtype), pltpu.HBM))

    def body(*refs):
        srcs, lnds = refs[:n], refs[n:2 * n]
        send_sems, recv_sems, local_sems = refs[2 * n:2 * n + 3]
        token = refs[-1]
        x, y, c, peers = _xy_peers()
        me = 2 * x + y
        for t in range(n):
            for k, (px, py) in enumerate(peers):
                _remote(srcs[t], _gather_part(lnds[t], t, me), send_sems.at[k * n + t], recv_sems.at[k * n + t],
                        (px, py, c)).start()
            pltpu.make_async_copy(srcs[t], _gather_part(lnds[t], t, me), local_sems.at[t]).start()
        token[...] = jnp.zeros_like(token)

    hbm = pl.BlockSpec(memory_space=pltpu.HBM)
    sem = pl.BlockSpec(memory_space=pltpu.SEMAPHORE)
    outs = pl.pallas_call(
        body, name="weights_gather_start", in_specs=[hbm] * (2 * n),
        out_shape=(pltpu.SemaphoreType.DMA((3 * n,)), pltpu.SemaphoreType.DMA((3 * n,)),
                   pltpu.SemaphoreType.DMA((n,)),
                   *[pltpu.HBM(s.shape, s.dtype) for s in shards], *[pltpu.HBM(a.shape, a.dtype) for a in lands],
                   jax.ShapeDtypeStruct((SUBLANES, LANES), F32)),
        out_specs=(sem, sem, sem, *[hbm] * (2 * n), pl.BlockSpec(memory_space=pltpu.VMEM)),
        input_output_aliases={i: 3 + i for i in range(2 * n)},
        compiler_params=pltpu.CompilerParams(has_side_effects=pltpu.SideEffectType.DATAFLOW_SIDE_EFFECTING),
    )(*[pltpu.with_memory_space_constraint(s, pltpu.HBM) for s in shards], *lands)
    return outs[0], outs[1], outs[2], outs[3:3 + n], outs[3 + n:3 + 2 * n], outs[-1]


def _gather_wait(handle, ts, after, *, name):
    send_sems, recv_sems, local_sems, src_thru, land_thru, _ = handle
    n = len(src_thru)
    m = len(ts)

    def body(*refs):
        srcs, lnds = refs[:m], refs[m:2 * m]
        ssem, rsem, lsem = refs[2 * m:2 * m + 3]
        x, y, c, peers = _xy_peers()
        me = 2 * x + y
        for i, t in enumerate(ts):
            for k, (px, py) in enumerate(peers):
                cp = _remote(srcs[i], _gather_part(lnds[i], t, 2 * px + py), ssem.at[k * n + t], rsem.at[k * n + t],
                             (px, py, c))
                cp.wait_send()
                cp.wait_recv()
            pltpu.make_async_copy(srcs[i], _gather_part(lnds[i], t, me), lsem.at[t]).wait()

    hbm = pl.BlockSpec(memory_space=pltpu.HBM)
    sem = pl.BlockSpec(memory_space=pltpu.SEMAPHORE)
    args = [src_thru[t] for t in ts] + [land_thru[t] for t in ts]
    outs = pl.pallas_call(
        body, name=name, in_specs=[hbm] * (2 * m) + [sem, sem, sem, pl.BlockSpec(memory_space=pl.ANY)],
        out_shape=[pltpu.HBM(a.shape, a.dtype) for a in args], out_specs=[hbm] * (2 * m),
        input_output_aliases={i: i for i in range(2 * m)},
        compiler_params=pltpu.CompilerParams(has_side_effects=pltpu.SideEffectType.DATAFLOW_SIDE_EFFECTING),
    )(*args, send_sems, recv_sems, local_sems, after)
    return outs[:m], outs[m:]


C_CHUNKS = 8
XY_CHUNKS = 8
EW_ROWS = 512


def _c_exchange(g, part):
    w = g.shape[2]
    row0, nrows = G_PARTS[part]
    half = nrows // 2
    rq = half // C_CHUNKS

    def body(g_ref, got_ref, send_sems, recv_sems):
        x, y, c = lax.axis_index("x"), lax.axis_index("y"), lax.axis_index("c")
        cps = []
        for s in range(4):
            for q in range(C_CHUNKS):
                k = s * C_CHUNKS + q
                cp = _remote(g_ref.at[s, pl.ds(row0 + (1 - c) * half + q * rq, rq), :],
                             got_ref.at[s, pl.ds(q * rq, rq), :], send_sems.at[k], recv_sems.at[k], (x, y, 1 - c))
                cp.start()
                cps.append(cp)
        for cp in cps:
            cp.wait_recv()
        for cp in cps:
            cp.wait_send()

    return pl.pallas_call(
        body, name="grad_c_exchange_%d" % part, in_specs=[_hbm()], out_specs=_hbm(),
        out_shape=jax.ShapeDtypeStruct((4, half, w), g.dtype),
        scratch_shapes=[pltpu.SemaphoreType.DMA((4 * C_CHUNKS,)), pltpu.SemaphoreType.DMA((4 * C_CHUNKS,))],
    )(g)


XFER_DTYPE = jnp.bfloat16


def _add_own_half(g, got, c_arr, part):
    w = g.shape[2]
    row0, nrows = G_PARTS[part]
    half = nrows // 2
    nb = half // EW_ROWS
    b0 = row0 // EW_ROWS

    def body(c_ref, a_ref, b_ref, o_ref, t_ref):
        sm = a_ref[...] + b_ref[...]
        o_ref[...] = sm.astype(o_ref.dtype)

        @pl.when(pl.program_id(1) == nb - 1)
        def _():
            t_ref[...] = sm[:, EW_ROWS - MISC_ROWS:, :]

    grid_spec = pltpu.PrefetchScalarGridSpec(
        num_scalar_prefetch=1, grid=(4, nb),
        in_specs=[pl.BlockSpec((1, EW_ROWS, w), lambda s, i, c: (s, b0 + c[0] * nb + i, 0)),
                  pl.BlockSpec((1, EW_ROWS, w), lambda s, i, c: (s, i, 0))],
        out_specs=[pl.BlockSpec((1, EW_ROWS, w), lambda s, i, c: (s, i, 0)),
                   pl.BlockSpec((1, MISC_ROWS, w), lambda s, i, c: (s, 0, 0))])
    return pl.pallas_call(
        body, name="grad_add_halves", grid_spec=grid_spec,
        out_shape=[jax.ShapeDtypeStruct((4, half, w), XFER_DTYPE), jax.ShapeDtypeStruct((4, MISC_ROWS, w), g.dtype)],
        compiler_params=_cparams(("arbitrary", "arbitrary")),
    )(c_arr, g, got)


def _xy_pieces(arrs):
    pieces = []
    for a, arr in enumerate(arrs):
        nch = XY_CHUNKS if a == 0 else 1
        rq = arr.shape[1] // nch
        pieces += [(a, pl.ds(q * rq, rq)) for q in range(nch)]
    return pieces


def _xy_start(arrs, *, name):
    na = len(arrs)
    pieces = _xy_pieces(arrs)
    npc = len(pieces)
    lands = [pltpu.with_memory_space_constraint(lax.empty(a.shape, a.dtype), pltpu.HBM) for a in arrs]

    def body(*refs):
        ins, outs = refs[:na], refs[na:2 * na]
        send_sems, recv_sems, local_sems = refs[2 * na:2 * na + 3]
        token = refs[-1]
        x, y, c, peers = _xy_peers()
        me = 2 * x + y
        for k, (px, py) in enumerate(peers):
            for j, (a, rows) in enumerate(pieces):
                _remote(ins[a].at[2 * px + py, rows, :], outs[a].at[me, rows, :], send_sems.at[k * npc + j],
                        recv_sems.at[k * npc + j], (px, py, c)).start()
        for j, (a, rows) in enumerate(pieces):
            pltpu.make_async_copy(ins[a].at[me, rows, :], outs[a].at[me, rows, :], local_sems.at[j]).start()
        token[...] = jnp.zeros_like(token)

    hbm = pl.BlockSpec(memory_space=pltpu.HBM)
    sem = pl.BlockSpec(memory_space=pltpu.SEMAPHORE)
    outs = pl.pallas_call(
        body, name=name, in_specs=[hbm] * (2 * na),
        out_shape=(pltpu.SemaphoreType.DMA((3 * npc,)), pltpu.SemaphoreType.DMA((3 * npc,)),
                   pltpu.SemaphoreType.DMA((npc,)),
                   *[pltpu.HBM(a.shape, a.dtype) for a in arrs], *[pltpu.HBM(a.shape, a.dtype) for a in arrs],
                   jax.ShapeDtypeStruct((SUBLANES, LANES), F32)),
        out_specs=(sem, sem, sem, *[hbm] * (2 * na), pl.BlockSpec(memory_space=pltpu.VMEM)),
        input_output_aliases={i: 3 + i for i in range(2 * na)},
        compiler_params=pltpu.CompilerParams(has_side_effects=pltpu.SideEffectType.DATAFLOW_SIDE_EFFECTING),
    )(*[pltpu.with_memory_space_constraint(a, pltpu.HBM) for a in arrs], *lands)
    return (outs[0], outs[1], outs[2], outs[3:3 + na], outs[3 + na:3 + 2 * na]), outs[-1]


def _xy_wait(handle, after, *, name):
    send_sems, recv_sems, local_sems, src_thru, land_thru = handle
    na = len(src_thru)
    pieces = _xy_pieces(src_thru)
    npc = len(pieces)

    def body(*refs):
        ins, outs = refs[:na], refs[na:2 * na]
        ssem, rsem, lsem = refs[2 * na:2 * na + 3]
        x, y, c, peers = _xy_peers()
        me = 2 * x + y
        for k, (px, py) in enumerate(peers):
            for j, (a, rows) in enumerate(pieces):
                cp = _remote(ins[a].at[me, rows, :], outs[a].at[2 * px + py, rows, :], ssem.at[k * npc + j],
                             rsem.at[k * npc + j], (px, py, c))
                cp.wait_send()
                cp.wait_recv()
        for j, (a, rows) in enumerate(pieces):
            pltpu.make_async_copy(ins[a].at[me, rows, :], outs[a].at[me, rows, :], lsem.at[j]).wait()

    hbm = pl.BlockSpec(memory_space=pltpu.HBM)
    sem = pl.BlockSpec(memory_space=pltpu.SEMAPHORE)
    args = list(src_thru) + list(land_thru)
    outs = pl.pallas_call(
        body, name=name, in_specs=[hbm] * (2 * na) + [sem, sem, sem, pl.BlockSpec(memory_space=pl.ANY)],
        out_shape=[pltpu.HBM(a.shape, a.dtype) for a in args], out_specs=[hbm] * (2 * na),
        input_output_aliases={i: i for i in range(2 * na)},
        compiler_params=pltpu.CompilerParams(has_side_effects=pltpu.SideEffectType.DATAFLOW_SIDE_EFFECTING),
    )(*args, send_sems, recv_sems, local_sems, after)
    return outs[na:]


def _sum4_into_half(r, rt, c_arr, part, fbuf):
    _, half, w = r.shape
    nb = half // EW_ROWS
    b0 = G_PARTS[part][0] // EW_ROWS

    def body(c_ref, r_ref, t_ref, *rest):
        o_ref = rest[-1]
        o_ref[...] = ((r_ref[0].astype(F32) + r_ref[1].astype(F32)) + r_ref[2].astype(F32)) + r_ref[3].astype(F32)

        @pl.when(pl.program_id(0) == nb - 1)
        def _():
            o_ref[EW_ROWS - MISC_ROWS:, :] = ((t_ref[0] + t_ref[1]) + t_ref[2]) + t_ref[3]

    in_specs = [pl.BlockSpec((4, EW_ROWS, w), lambda i, c: (0, i, 0)),
                pl.BlockSpec((4, MISC_ROWS, w), lambda i, c: (0, 0, 0))]
    args = [c_arr, r, rt]
    aliases = {}
    if fbuf is not None:
        in_specs.append(pl.BlockSpec(memory_space=pl.ANY))
        args.append(fbuf)
        aliases = {3: 0}
    grid_spec = pltpu.PrefetchScalarGridSpec(
        num_scalar_prefetch=1, grid=(nb,), in_specs=in_specs,
        out_specs=pl.BlockSpec((EW_ROWS, w), lambda i, c: (b0 + c[0] * nb + i, 0)))
    return pl.pallas_call(
        body, name="grad_sum4", grid_spec=grid_spec, out_shape=jax.ShapeDtypeStruct((G_ROWS, w), F32),
        input_output_aliases=aliases, compiler_params=_cparams(("arbitrary",)),
    )(*args)


C_GATHER_ROWS = 512


def _c_allgather_halves(f, parts):
    w = f.shape[1]
    chunks = []
    for part in parts:
        chunks += [(part, r) for r in range(0, G_PARTS[part][1] // 2, C_GATHER_ROWS)]
    nch = len(chunks)

    def body(f_ref, out_ref, send_sems, recv_sems):
        x, y, c = lax.axis_index("x"), lax.axis_index("y"), lax.axis_index("c")

        def rows(q, owner):
            part, r = chunks[q]
            row0, nrows = G_PARTS[part]
            return pl.ds(row0 + owner * (nrows // 2) + r, C_GATHER_ROWS)

        sends = []
        for q in range(nch):
            cp = _remote(f_ref.at[rows(q, c), :], out_ref.at[rows(q, c), :], send_sems.at[q], recv_sems.at[q],
                         (x, y, 1 - c))
            cp.start()
            sends.append(cp)
        for q in range(nch):
            _remote(f_ref.at[rows(q, 1 - c), :], out_ref.at[rows(q, 1 - c), :], send_sems.at[q], recv_sems.at[q],
                    (x, y, 1 - c)).wait_recv()
        for cp in sends:
            cp.wait_send()

    return pl.pallas_call(
        body, name="grad_c_allgather_" + "".join(str(p) for p in parts), in_specs=[_hbm()], out_specs=_hbm(),
        input_output_aliases={0: 0},
        out_shape=jax.ShapeDtypeStruct((G_ROWS, w), f.dtype),
        scratch_shapes=[pltpu.SemaphoreType.DMA((nch,)), pltpu.SemaphoreType.DMA((nch,))],
    )(f)


def _adamw(w, m, v, g, g_rows=None):
    shape = w.shape
    cols = shape[-1]
    rows = int(math.prod(shape)) // cols
    tr = 256 if rows % 256 == 0 else rows
    from_flat = g_rows is not None
    c1 = 1.0 / (1.0 - ADAM_B1 ** ADAM_STEP)
    c2 = 1.0 / (1.0 - ADAM_B2 ** ADAM_STEP)

    def body(w_ref, m_ref, v_ref, g_ref, *outs):
        gg = g_ref[...]
        nm = ADAM_B1 * m_ref[...] + (1.0 - ADAM_B1) * gg
        nv = ADAM_B2 * v_ref[...] + (1.0 - ADAM_B2) * (gg * gg)
        if from_flat:
            outs[0][...] = gg
        d_ref, nm_ref, nv_ref = outs[-3:]
        nm_ref[...] = nm
        nv_ref[...] = nv
        d_ref[...] = -ADAM_LR * ((nm * c1) / (jnp.sqrt(nv * c2) + ADAM_EPS) + ADAM_WD * w_ref[...])

    spec = pl.BlockSpec((tr, cols), lambda i: (i, 0))
    if from_flat:
        nbl = rows // DEPTH // tr
        assert cols == FLAT and all(r % tr == 0 for r in g_rows) and len(g_rows) == DEPTH == 2
        b0, b1 = g_rows[0] // tr, g_rows[1] // tr
        g_spec = pl.BlockSpec((tr, cols), lambda i: (jnp.where(i < nbl, b0 + i, b1 + i - nbl), 0))
        g_arg = g
    else:
        g_spec = spec
        g_arg = g.reshape(rows, cols)
    n_out = 4 if from_flat else 3
    sds = jax.ShapeDtypeStruct((rows, cols), F32)
    outs = pl.pallas_call(
        body, name="adamw", grid=(rows // tr,), in_specs=[spec, spec, spec, g_spec], out_specs=[spec] * n_out,
        out_shape=[sds] * n_out, compiler_params=_cparams(("arbitrary",)),
    )(w.reshape(rows, cols), m.reshape(rows, cols), v.reshape(rows, cols), g_arg)
    outs = [o.reshape(shape) for o in outs]
    return outs if from_flat else [g] + outs


SMALL_SHARDED = (("s5_glu_w", (2, 64, 256), 1), ("ssd_conv_w", (2, 4, 256), 2), ("rg_conv_w", (2, 4, 64), 2))
REPLICATED = (
    ("ssd_conv_b", (2, 1024)), ("ssd_dt_bias", (2, 8)), ("ssd_a_log", (2, 8)), ("ssd_d", (2, 8)),
    ("ssd_norm_w", (2, 512)), ("s5_lam_re", (2, 16, 64)), ("s5_lam_im", (2, 16, 64)), ("s5_log_step", (2, 16)),
    ("s5_b_re", (2, 16, 64, 16)), ("s5_b_im", (2, 16, 64, 16)), ("s5_c_re", (2, 16, 16, 64)),
    ("s5_c_im", (2, 16, 16, 64)), ("s5_d", (2, 256)), ("s5_glu_b", (2, 256)), ("rg_conv_b", (2, 256)),
    ("rg_wa", (2, 4, 64, 64)), ("rg_ba", (2, 4, 64)), ("rg_wx", (2, 4, 64, 64)), ("rg_bx", (2, 4, 64)),
    ("rg_lambda", (2, 256)), ("ln1_g", (2, 1024)), ("ln1_b", (2, 1024)), ("ln2_g", (2, 1024)), ("ln2_b", (2, 1024)),
    ("ln3_g", (2, 1024)), ("ln3_b", (2, 1024)),
)
WEIGHT_ORDER = (
    "w_in", "w_out", "ssd_conv_w", "ssd_conv_b", "ssd_dt_bias", "ssd_a_log", "ssd_d", "ssd_norm_w", "s5_lam_re",
    "s5_lam_im", "s5_log_step", "s5_b_re", "s5_b_im", "s5_c_re", "s5_c_im", "s5_d", "s5_glu_w", "s5_glu_b",
    "rg_conv_w", "rg_conv_b", "rg_wa", "rg_ba", "rg_wx", "rg_bx", "rg_lambda", "ln1_g", "ln1_b", "xa_wq", "xa_wk",
    "xa_wv", "xa_wo", "ln2_g", "ln2_b", "mlp_w1", "mlp_w2", "ln3_g", "ln3_b",
)


def _size(shape):
    return int(math.prod(shape))


def _round_up(a, b):
    return (a + b - 1) // b * b


SMALL_ELEMS = sum(_size(s) for _, s, _ in SMALL_SHARDED)
REP_ELEMS = sum(_size(s) for _, s in REPLICATED)
REP_QROWS = _round_up(-(-REP_ELEMS // (4 * FLAT)), 8)
assert SMALL_ELEMS <= MISC_REP_ROW * FLAT and MISC_REP_ROW + REP_QROWS <= MISC_ROWS


def _pack_shards(tensors, names_shapes):
    return jnp.concatenate([tensors[n].reshape(-1) for n, *_ in names_shapes])


def _unpack(flat, names_shapes):
    out, off = {}, 0
    for n, s, *_ in names_shapes:
        out[n] = flat[off:off + _size(s)].reshape(s)
        off += _size(s)
    return out


def _split_shards(full, names_shapes):
    rows = []
    for k in range(4):
        parts = []
        for n, s, ax in names_shapes:
            w = s[ax]
            parts.append(lax.slice_in_dim(full[n], k * w, (k + 1) * w, axis=ax).reshape(-1))
        rows.append(jnp.concatenate(parts))
    return jnp.stack(rows)


def _pack_cols(w):
    pad = jnp.zeros((w.shape[0], LANES - SSD_HEADS), w.dtype)
    return jnp.concatenate([w[:, O_XBC:O_XBC + 1024], w[:, O_Z:O_Z + 512], w[:, O_U:O_U + 256],
                            w[:, O_XRG:O_XRG + 256], w[:, O_GRG:O_GRG + 256], w[:, O_DT:O_DT + 8], pad], axis=1)


def _unpack_cols(w):
    return jnp.concatenate([w[:, P_Z:P_Z + 512], w[:, P_XBC:P_XBC + 1024], w[:, P_DT:P_DT + 8],
                            w[:, P_U:P_U + 256], w[:, P_XRG:P_XRG + 256], w[:, P_GRG:P_GRG + 256]], axis=1)


def _lanes(v, width):
    return jnp.pad(v, (0, width - v.shape[0])).reshape(1, width)


def _layer_params(rep, l):
    p = {}
    p["ssd_cb"] = rep["ssd_conv_b"][l].reshape(1, -1)
    p["ssd_dtb"] = _lanes(rep["ssd_dt_bias"][l], LANES)
    p["ssd_a"] = _lanes(-jnp.exp(rep["ssd_a_log"][l]), LANES)
    p["ssd_d"] = jnp.repeat(rep["ssd_d"][l], 64).reshape(1, -1)
    p["ssd_nw"] = rep["ssd_norm_w"][l].reshape(1, -1)
    s5_args = tuple(rep[n][l] for n in ("s5_lam_re", "s5_lam_im", "s5_log_step", "s5_b_re", "s5_b_im", "s5_c_re",
                                        "s5_c_im"))
    (lbr, lbi, bre, bim, cre, cim), p["s5_vjp"] = jax.vjp(_s5_prepare, *s5_args)
    p.update(s5_bre=bre, s5_bim=bim, s5_cre=cre, s5_cim=cim)
    p["s5_coef"] = _s5_scan_coef(lbr, lbi, False)
    p["s5_rcoef"] = _s5_scan_coef(lbr, lbi, True)
    p["s5_d"] = rep["s5_d"][l].reshape(1, -1)
    p["s5_gb"] = rep["s5_glu_b"][l].reshape(1, -1)
    p["rg_cb"] = rep["rg_conv_b"][l].reshape(1, -1)
    p["rg_wa"] = _block_diag(rep["rg_wa"][l])
    p["rg_wx"] = _block_diag(rep["rg_wx"][l])
    p["rg_ba"] = rep["rg_ba"][l].reshape(1, -1)
    p["rg_bx"] = rep["rg_bx"][l].reshape(1, -1)
    p["rg_nsp"] = (-RG_C * jax.nn.softplus(-rep["rg_lambda"][l])).reshape(1, -1)
    p["rg_dnsp"] = RG_C * jax.nn.sigmoid(-rep["rg_lambda"][l])
    for n in ("ln1_g", "ln1_b", "ln2_g", "ln2_b", "ln3_g", "ln3_b"):
        p[n] = rep[n][l].reshape(1, -1)
    return p


def _layer_fwd(h, mem, p, fetch):
    s = {"h0": h}
    p.update(fetch(0, h))
    proj = _mm(h, p["w_in"], name="in_proj")
    s["proj"] = proj
    y_ssd, s["ssd_yy"], s["ssd_states"] = _ssd_fwd(proj, p["ssd_cw"], p["ssd_cb"], p["ssd_dtb"], p["ssd_a"],
                                                     p["ssd_d"], p["ssd_nw"])
    y_s5, s["s5_y2"], s["s5_hre"], s["s5_him"] = _s5_fwd(proj, p["s5_bre"], p["s5_bim"], p["s5_cre"], p["s5_cim"],
                                                         p["s5_d"], p["s5_glu_w"], p["s5_gb"], p["s5_coef"])
    y_rg, s["rg_h"] = _rg_fwd(proj, p["rg_cw"], p["rg_cb"], p["rg_wa"], p["rg_ba"], p["rg_wx"], p["rg_bx"],
                              p["rg_nsp"])
    s["ys"] = [y_ssd, y_s5, y_rg]
    p.update(fetch(1, y_rg))
    h1, s["xh1"], s["rs1"] = _outproj_ln_fwd(s["ys"], h, p["w_out"], p["ln1_g"], p["ln1_b"])
    s["h1"] = h1
    p.update(fetch(2, h1))
    kb = _mm(mem, p["xa_wk"], name="mem_proj")
    vb = _mm(mem, p["xa_wv"], name="mem_proj")
    s["kb"], s["vb"] = kb, vb
    h2, s["xh2"], s["rs2"], s["attn_o"] = _attn_ln_fwd(h1, p["xa_wq"], p["xa_wo"], kb, vb, p["ln2_g"], p["ln2_b"])
    s["h2"] = h2
    p.update(fetch(3, h2))
    h3, s["xh3"], s["rs3"], s["mlp_hdn"] = _mlp_ln_fwd(h2, p["mlp_w1"], p["mlp_w2"], p["ln3_g"], p["ln3_b"])
    return h3, s


def _layer_bwd(dh3, mem, p, s, l, gbuf, after_mlp=None):
    g = {}
    dr3, du, dh2, g["ln3_g"], g["ln3_b"] = _mlp_ln_bwd(dh3, s["xh3"], s["rs3"], p["ln3_g"], s["mlp_hdn"],
                                                        p["mlp_w1"], p["mlp_w2"])
    gbuf = _wgrad_flat(s["h2"], du, gbuf, mode="colblk", row_off=_grad_row("mlp_w1", l), name="wgrad_mlp_w1")
    gbuf = _wgrad_flat(s["mlp_hdn"], dr3, gbuf, mode="rowblk", row_off=_grad_row("mlp_w2", l), name="wgrad_mlp_w2")
    ln2_g = p["ln2_g"] if after_mlp is None else p["ln2_g"] + after_mlp(gbuf)[0:1, 0:1]
    dr2, dq, dh1, dkb, dvb, g["ln2_g"], g["ln2_b"] = _attn_ln_bwd(dh2, s["xh2"], s["rs2"], ln2_g, s["h1"],
                                                                   p["xa_wq"], p["xa_wo"], s["kb"], s["vb"])
    for n, a_op, g_op in (("xa_wo", s["attn_o"], dr2), ("xa_wq", s["h1"], dq), ("xa_wk", mem, dkb),
                          ("xa_wv", mem, dvb)):
        gbuf = _wgrad_flat(a_op, g_op, gbuf, mode="rows4", row_off=_grad_row(n, l), name="wgrad_" + n)
    dr1, dres, dycat, g["ln1_g"], g["ln1_b"] = _outproj_ln_bwd(dh1, s["xh1"], s["rs1"], p["ln1_g"], p["w_out"])
    gbuf = _wgrad_flat(s["ys"], dr1, gbuf, mode="rows4", row_off=_grad_row("w_out", l), name="wgrad_w_out")
    proj = s["proj"]
    (dxbc, dz, ddt, dcw, dcb, ddtb, da_neg, dd_l, dnw) = _ssd_bwd(
        dycat, proj, s["ssd_yy"], s["ssd_states"], p["ssd_cw"], p["ssd_cb"], p["ssd_dtb"], p["ssd_a"], p["ssd_d"],
        p["ssd_nw"])
    g["ssd_conv_w"] = dcw[0:4]
    g["ssd_conv_b"] = dcb[0]
    g["ssd_dt_bias"] = ddtb[0, :SSD_HEADS]
    g["ssd_a_log"] = da_neg[0, :SSD_HEADS] * p["ssd_a"][0, :SSD_HEADS]
    g["ssd_d"] = dd_l.reshape(SSD_HEADS, 64).sum(axis=1)
    g["ssd_norm_w"] = dnw[0]
    (du_s5, dbre, dbim, dcre, dcim, dlam, dd5, dgw, dgb) = _s5_bwd(
        dycat, proj, s["s5_y2"], s["s5_hre"], s["s5_him"], p["s5_bre"], p["s5_bim"], p["s5_cre"], p["s5_cim"],
        p["s5_d"], p["s5_glu_w"], p["s5_gb"], p["s5_rcoef"])
    dl = dlam.sum(axis=1)
    s5g = p["s5_vjp"]((dl[0], dl[1], dbre, dbim, dcre, dcim))
    for n, v in zip(("s5_lam_re", "s5_lam_im", "s5_log_step", "s5_b_re", "s5_b_im", "s5_c_re", "s5_c_im"), s5g):
        g[n] = v
    g["s5_d"] = dd5[0]
    g["s5_glu_w"] = dgw
    g["s5_glu_b"] = dgb[0]
    (dxrg, dgrg, drcw, drcb, dwa, dba, dwx, dbx, dnsp) = _rg_bwd(
        dycat, proj, s["rg_h"], p["rg_cw"], p["rg_cb"], p["rg_wa"], p["rg_ba"], p["rg_wx"], p["rg_bx"], p["rg_nsp"])
    g["rg_conv_w"] = drcw[0:4]
    g["rg_conv_b"] = drcb[0]
    g["rg_wa"] = _block_diag_extract(dwa, RG_BLOCKS)
    g["rg_wx"] = _block_diag_extract(dwx, RG_BLOCKS)
    g["rg_ba"] = dba.reshape(RG_BLOCKS, RG_BLOCK_DIM)
    g["rg_bx"] = dbx.reshape(RG_BLOCKS, RG_BLOCK_DIM)
    g["rg_lambda"] = dnsp[0] * p["rg_dnsp"]
    dproj = [dxbc, dz, du_s5, dxrg, dgrg, ddt]
    g["w_in"] = _unpack_cols(_wgrad_in(s["h0"], dproj))
    dh0 = _in_proj_bwd(dproj, p["w_in"], dres)
    for n in ("ln1_g", "ln1_b", "ln2_g", "ln2_b", "ln3_g", "ln3_b"):
        g[n] = g[n][0]
    return dh0, g, gbuf


def _local_step(h, memf, target, rep, fetch):
    params, saved = [], []
    for l in range(DEPTH):
        p = _layer_params(rep, l)
        params.append(p)
        h, s = _layer_fwd(h, memf, p, functools.partial(fetch, l))
        saved.append(s)
    loss11, dh = _loss_fwd_bwd(h, target)
    grads = [None] * DEPTH
    gbuf = None
    c_arr = lax.axis_index("c").astype(jnp.int32).reshape(1)
    handles = {}

    def start_part(buf, part):
        handles[part], token = _xy_start(_chip_sums(buf, c_arr, part), name="grad_xy_start_%d" % part)
        return token

    for l in reversed(range(DEPTH)):
        hook = functools.partial(start_part, part=1) if l == 0 else None
        dh, grads[l], gbuf = _layer_bwd(dh, memf, params[l], saved[l], l, gbuf, hook)
        if l == DEPTH - 1:
            gbuf = lax.dynamic_update_slice(
                gbuf, _w_in_block(grads[l]["w_in"], jnp.zeros((4, MISC_ROWS, FLAT), F32)),
                (0, _grad_row("w_in", l), 0))
            params[0]["ln3_g"] = params[0]["ln3_g"] + start_part(gbuf, 0)[0:1, 0:1]
    gsmall = {n: jnp.stack([grads[l][n] for l in range(DEPTH)]) for n in grads[0] if n != "w_in"}
    return loss11, dh, gsmall, grads[0]["w_in"], gbuf, handles, c_arr


def _w_in_block(gw, tail):
    gw = jnp.pad(gw.reshape(D_MODEL, 4, W_IN_SHARD), ((0, 0), (0, 0), (0, W_IN_PAD - W_IN_SHARD)))
    return jnp.concatenate([jnp.transpose(gw, (1, 0, 2)).reshape(4, W_IN_PAD, FLAT), tail], axis=1)


def _chip_sums(gbuf, c_arr, part):
    return list(_add_own_half(gbuf, _c_exchange(gbuf, part), c_arr, part))


def kernel(x, mem, w_in, w_out, ssd_conv_w, ssd_conv_b, ssd_dt_bias, ssd_a_log, ssd_d, ssd_norm_w, s5_lam_re, s5_lam_im, s5_log_step, s5_b_re, s5_b_im, s5_c_re, s5_c_im, s5_d, s5_glu_w, s5_glu_b, rg_conv_w, rg_conv_b, rg_wa, rg_ba, rg_wx, rg_bx, rg_lambda, ln1_g, ln1_b, xa_wq, xa_wk, xa_wv, xa_wo, ln2_g, ln2_b, mlp_w1, mlp_w2, ln3_g, ln3_b, loss_target, m_w_in, m_w_out, m_ssd_conv_w, m_ssd_conv_b, m_ssd_dt_bias, m_ssd_a_log, m_ssd_d, m_ssd_norm_w, m_s5_lam_re, m_s5_lam_im, m_s5_log_step, m_s5_b_re, m_s5_b_im, m_s5_c_re, m_s5_c_im, m_s5_d, m_s5_glu_w, m_s5_glu_b, m_rg_conv_w, m_rg_conv_b, m_rg_wa, m_rg_ba, m_rg_wx, m_rg_bx, m_rg_lambda, m_ln1_g, m_ln1_b, m_xa_wq, m_xa_wk, m_xa_wv, m_xa_wo, m_ln2_g, m_ln2_b, m_mlp_w1, m_mlp_w2, m_ln3_g, m_ln3_b, v_w_in, v_w_out, v_ssd_conv_w, v_ssd_conv_b, v_ssd_dt_bias, v_ssd_a_log, v_ssd_d, v_ssd_norm_w, v_s5_lam_re, v_s5_lam_im, v_s5_log_step, v_s5_b_re, v_s5_b_im, v_s5_c_re, v_s5_c_im, v_s5_d, v_s5_glu_w, v_s5_glu_b, v_rg_conv_w, v_rg_conv_b, v_rg_wa, v_rg_ba, v_rg_wx, v_rg_bx, v_rg_lambda, v_ln1_g, v_ln1_b, v_xa_wq, v_xa_wk, v_xa_wv, v_xa_wo, v_ln2_g, v_ln2_b, v_mlp_w1, v_mlp_w2, v_ln3_g, v_ln3_b):
    args = dict(locals())
    weights = {n: args[n] for n in WEIGHT_ORDER}
    mom_m = {n: args["m_" + n] for n in WEIGHT_ORDER}
    mom_v = {n: args["v_" + n] for n in WEIGHT_ORDER}

    shards = []
    for l in range(DEPTH):
        for n, shp, ax in LAYER_GATHERED:
            w = weights[n][l]
            if w.shape[1] != shp[1]:
                w = jnp.pad(w, ((0, 0), (0, shp[1] - w.shape[1])))
            if n not in ("ssd_conv_w", "rg_conv_w"):
                w = w.astype(MXU_DTYPE)
            shards.append(w)
    handle = _gather_start(shards)

    def unpad(arr, padded, width):
        return jnp.concatenate([arr[:, padded * k:padded * k + width] for k in range(4)], axis=1)

    def fetch(l, grp, after):
        ts = [l * N_GATHERED + j for j in WAIT_GROUPS[grp]]
        _, landed = _gather_wait(handle, ts, after, name="weights_gather_wait_%d_%d" % (l, grp))
        out = {}
        for t, arr in zip(ts, landed):
            n = LAYER_GATHERED[t % N_GATHERED][0]
            if n == "w_in":
                arr = _pack_cols(unpad(arr, W_IN_PAD, W_IN_SHARD))
            elif n == "rg_conv_w":
                arr = unpad(arr, LANES, RG_CONV_SHARD)
            out[{"ssd_conv_w": "ssd_cw", "rg_conv_w": "rg_cw"}.get(n, n)] = arr
        return out

    rep = {n: weights[n] for n, _ in REPLICATED}

    loss11, dx, gsmall, gw_in0, gbuf, handles, c_arr = _local_step(x[0], mem[0], loss_target[0], rep, fetch)
    grad_x = dx[None]
    loss = lax.psum(loss11[0, 0], ("x", "y", "c"))

    small_q = _split_shards(gsmall, SMALL_SHARDED)
    rep_q = jnp.pad(_pack_shards(gsmall, REPLICATED), (0, 4 * REP_QROWS * FLAT - REP_ELEMS)).reshape(4, -1)
    misc = jnp.concatenate(
        [jnp.pad(small_q, ((0, 0), (0, MISC_REP_ROW * FLAT - SMALL_ELEMS))), rep_q,
         jnp.zeros((4, (MISC_ROWS - MISC_REP_ROW - REP_QROWS) * FLAT), F32)], axis=1).reshape(4, MISC_ROWS, FLAT)
    gbuf = lax.dynamic_update_slice(gbuf, _w_in_block(gw_in0, misc), (0, _grad_row("w_in", 0), 0))
    handles[2], token = _xy_start(_chip_sums(gbuf, c_arr, 2), name="grad_xy_start_2")
    fbuf = None
    for part in (0, 1):
        got = _xy_wait(handles[part], dx, name="grad_xy_wait_%d" % part)
        fbuf = _sum4_into_half(got[0], got[1] + token[0:1, 0:1], c_arr, part, fbuf)
    fbuf = _c_allgather_halves(fbuf, (0, 1))
    res = {n: _adamw(weights[n], mom_m[n], mom_v[n], fbuf, g_rows=[_grad_row(n, l) for l in range(DEPTH)])
           for n in ("mlp_w1", "mlp_w2")}
    got = _xy_wait(handles[2], res["mlp_w2"][1], name="grad_xy_wait_2")
    reduced = _c_allgather_halves(_sum4_into_half(got[0], got[1], c_arr, 2, fbuf), (2,))
    misc_red = reduced[ROW_MISC:]
    rep_all = _xy_allgather(misc_red[MISC_REP_ROW:MISC_REP_ROW + REP_QROWS], name="small_grads_allgather")
    g_red = {**_unpack(misc_red[:MISC_REP_ROW].reshape(-1), SMALL_SHARDED),
             **_unpack(rep_all.reshape(-1), REPLICATED)}
    g_red["w_in"] = jnp.stack([
        reduced[_grad_row("w_in", l):_grad_row("w_in", l) + W_IN_PAD].reshape(D_MODEL, W_IN_PAD)[:, :W_IN_SHARD]
        for l in range(DEPTH)])

    for n in WEIGHT_ORDER:
        if n in ("w_out", "xa_wq", "xa_wk", "xa_wv", "xa_wo"):
            res[n] = _adamw(weights[n], mom_m[n], mom_v[n], reduced, g_rows=[_grad_row(n, l) for l in range(DEPTH)])
        elif n not in res:
            res[n] = _adamw(weights[n], mom_m[n], mom_v[n], g_red[n])
    return (loss, grad_x, *[res[n][0] for n in WEIGHT_ORDER], *[res[n][1] for n in WEIGHT_ORDER],
            *[res[n][2] for n in WEIGHT_ORDER], *[res[n][3] for n in WEIGHT_ORDER])
```

```python
import functools
import math

import jax
import jax.numpy as jnp
from jax import lax
from jax.experimental import pallas as pl
from jax.experimental.pallas import tpu as pltpu

F32 = jnp.float32
MXU_DTYPE = jnp.bfloat16

D_MODEL = 1024
DEPTH = 2
MEM_LEN = 256
SSD_WIDTH = 512
SSD_HEADS = 8
SSD_STATE = 128
SSD_CHUNK = 128
SSD_XBC = 1024
S5_WIDTH = 256
S5_GROUPS = 16
S5_GROUP_CH = 16
S5_STATE = 64
S5_NSTATE = S5_GROUPS * S5_STATE
RG_WIDTH = 256
RG_BLOCKS = 4
RG_BLOCK_DIM = 64
RG_C = 8.0
XA_HEADS = 4
XA_HEAD_DIM = 256
D_FF = 4096
D_IN = 2312
ALPHA = (2.0 * DEPTH) ** 0.25
LN_EPS = 1e-5
ADAM_LR = 0.001
ADAM_B1 = 0.9
ADAM_B2 = 0.999
ADAM_EPS = 1e-08
ADAM_WD = 0.01
ADAM_STEP = 10

P_XBC, P_Z, P_U, P_XRG, P_GRG, P_DT = 0, 1024, 1536, 1792, 2048, 2304
D_PACK = 2432
O_Z, O_XBC, O_DT, O_U, O_XRG, O_GRG = 0, 512, 1536, 1544, 1800, 2056

LANES = 128
SUBLANES = 8
VMEM_LIMIT = 52 * 1024 * 1024
TM = 512
SSD_FWD_TM = 256
SSD_BWD_TM = 128
SCAN_TM = 512
FLAT = 1024

MESH = pl.DeviceIdType.MESH


def _cparams(sem):
    return pltpu.CompilerParams(dimension_semantics=sem, vmem_limit_bytes=VMEM_LIMIT)


def _dot(a, b):
    return jnp.dot(a.astype(MXU_DTYPE), b.astype(MXU_DTYPE), preferred_element_type=F32)


def _dot_nt(a, b):
    return lax.dot_general(a.astype(MXU_DTYPE), b.astype(MXU_DTYPE), (((1,), (1,)), ((), ())),
                           preferred_element_type=F32)


def _dot_tn(a, b):
    return lax.dot_general(a.astype(MXU_DTYPE), b.astype(MXU_DTYPE), (((0,), (0,)), ((), ())),
                           preferred_element_type=F32)


def _dot_f32(a, b):
    return jnp.dot(a, b, precision=lax.Precision.HIGHEST, preferred_element_type=F32)


def _dot_f32_tn(a, b):
    return lax.dot_general(a, b, (((0,), (0,)), ((), ())), precision=lax.Precision.HIGHEST,
                           preferred_element_type=F32)


def _sigmoid(x):
    return 1.0 / (1.0 + jnp.exp(-x))


def _softplus(x):
    return jnp.maximum(x, 0.0) + jnp.log(1.0 + jnp.exp(-jnp.abs(x)))


_GELU_K = math.sqrt(2.0 / math.pi)


def _gelu(x):
    return 0.5 * x * (1.0 + jnp.tanh(_GELU_K * (x + 0.044715 * x * x * x)))


def _gelu_grad(x):
    t = jnp.tanh(_GELU_K * (x + 0.044715 * x * x * x))
    return 0.5 * (1.0 + t) + 0.5 * x * (1.0 - t * t) * _GELU_K * (1.0 + 3.0 * 0.044715 * x * x)


def _expm1(x):
    small = x * (1.0 + x * (0.5 + x * (1.0 / 6.0 + x * (1.0 / 24.0))))
    return jnp.where(jnp.abs(x) < 0.05, small, jnp.exp(x) - 1.0)


def _sum0(x):
    return jnp.sum(x, axis=0, keepdims=True)


def _ln_fwd(r, g, b):
    mu = jnp.mean(r, axis=-1, keepdims=True)
    xc = r - mu
    var = jnp.mean(xc * xc, axis=-1, keepdims=True)
    rstd = lax.rsqrt(var + LN_EPS)
    xhat = xc * rstd
    return xhat * g + b, xhat, rstd


def _ln_bwd(dout, xhat, rstd, g):
    dxh = dout * g
    m1 = jnp.mean(dxh, axis=-1, keepdims=True)
    m2 = jnp.mean(dxh * xhat, axis=-1, keepdims=True)
    return rstd * (dxh - m1 - xhat * m2)


def _rows(tm, n, col=0):
    return pl.BlockSpec((tm, n), lambda i: (i, col))


def _const(shape):
    nd = len(shape)
    return pl.BlockSpec(shape, lambda i: (0,) * nd)


def _mm(a, w, *, name):
    t, k = a.shape
    n = w.shape[1]
    tm = min(TM, t)

    def body(a_ref, w_ref, o_ref):
        o_ref[...] = _dot(a_ref[...], w_ref[...])

    return pl.pallas_call(
        body, name=name, grid=(t // tm,), in_specs=[_rows(tm, k), _const(w.shape)], out_specs=_rows(tm, n),
        out_shape=jax.ShapeDtypeStruct((t, n), F32), compiler_params=_cparams(("arbitrary",)),
    )(a, w)


DPROJ_PIECES = ((P_XBC, 1024), (P_Z, 512), (P_U, 256), (P_XRG, 256), (P_GRG, 256), (P_DT, LANES))


def _in_proj_bwd(pieces, w, dres):
    t = dres.shape[0]
    npc = len(pieces)

    def body(*refs):
        w_ref, r_ref, o_ref = refs[npc:]
        acc = r_ref[...]
        for p_ref, (off, k) in zip(refs[:npc], DPROJ_PIECES):
            acc = acc + _dot_nt(p_ref[...], w_ref[:, off:off + k])
        o_ref[...] = acc

    return pl.pallas_call(
        body, name="in_proj_bwd", grid=(t // TM,),
        in_specs=[_rows(TM, k) for _, k in DPROJ_PIECES] + [_const(w.shape), _rows(TM, D_MODEL)],
        out_specs=_rows(TM, D_MODEL), out_shape=jax.ShapeDtypeStruct((t, D_MODEL), F32),
        compiler_params=_cparams(("arbitrary",)),
    )(*pieces, w, dres)


def _wgrad_in(h0, pieces):
    t = h0.shape[0]
    npc = len(pieces)

    def body(*refs):
        h_ref, o_ref = refs[npc], refs[npc + 1]
        @pl.when(pl.program_id(0) == 0)
        def _():
            o_ref[...] = jnp.zeros_like(o_ref)

        hb = h_ref[...].astype(MXU_DTYPE)
        for p_ref, (off, k) in zip(refs[:npc], DPROJ_PIECES):
            o_ref[:, off:off + k] += _dot_tn(hb, p_ref[...])

    return pl.pallas_call(
        body, name="wgrad_in", grid=(t // TM,),
        in_specs=[_rows(TM, k) for _, k in DPROJ_PIECES] + [_rows(TM, D_MODEL)],
        out_specs=_const((D_MODEL, D_PACK)), out_shape=jax.ShapeDtypeStruct((D_MODEL, D_PACK), F32),
        compiler_params=_cparams(("arbitrary",)),
    )(*pieces, h0)


G_ROWS = 8192
G_PARTS = ((0, 4096), (4096, 2048), (6144, 2048))
W_IN_SHARD = 578
W_IN_PAD = 640
MISC_ROWS = 128
MISC_REP_ROW = 40
ROW_MISC = G_ROWS - MISC_ROWS
W_IN_BLOCK_ROWS = W_IN_PAD + MISC_ROWS


def _grad_row(name, l):
    base = 0 if l == 1 else 4096
    mid = base + 2048 if l == 1 else 6144
    return {"mlp_w1": base, "mlp_w2": base + 1024, "w_out": mid, "xa_wq": mid + 256, "xa_wk": mid + 512,
            "xa_wv": mid + 768, "xa_wo": mid + 1024, "w_in": mid + 1280}[name]


def _wgrad_flat(a, g, buf, *, mode, row_off, name):
    pieces = list(a) if isinstance(a, (list, tuple)) else [a]
    t = g.shape[0]
    tt = min(1024, t)
    ns = t // tt
    blk = D_MODEL

    def accumulate(o_ref, parts, s):
        @pl.when(s == 0)
        def _():
            o_ref[...] = jnp.zeros_like(o_ref)

        for q, v in parts:
            o_ref[q] += v

    if mode == "rows4":
        grid = (ns,)
        in_specs = [pl.BlockSpec((tt, p.shape[1]), lambda s: (s, 0)) for p in pieces]
        in_specs.append(pl.BlockSpec((tt, blk), lambda s: (s, 0)))
        out_spec = pl.BlockSpec((4, 256, FLAT), lambda s: (0, row_off // 256, 0))
        sem = ("arbitrary",)
        npc = len(pieces)

        def body(*refs):
            g_v = refs[npc][...]
            parts, q0 = [], 0
            for p_ref in refs[:npc]:
                full = _dot_tn(p_ref[...], g_v)
                nq = full.shape[0] // 256
                parts += [(q0 + q, full[q * 256:(q + 1) * 256]) for q in range(nq)]
                q0 += nq
            accumulate(refs[-1], parts, pl.program_id(0))
    else:
        grid = (2, ns)
        if mode == "rowblk":
            in_specs = [pl.BlockSpec((tt, 2 * blk), lambda q, s: (s, q)), pl.BlockSpec((tt, blk), lambda q, s: (s, 0))]
        else:
            in_specs = [pl.BlockSpec((tt, blk), lambda q, s: (s, 0)), pl.BlockSpec((tt, 2 * blk), lambda q, s: (s, q))]
        out_spec = pl.BlockSpec((2, blk, FLAT), lambda q, s: (q, row_off // blk, 0))
        sem = ("arbitrary", "arbitrary")

        def body(a_ref, g_ref, *rest):
            full = _dot_tn(a_ref[...], g_ref[...])
            if mode == "rowblk":
                parts = [(0, full[:blk]), (1, full[blk:])]
            else:
                parts = [(0, full[:, :blk]), (1, full[:, blk:])]
            accumulate(rest[-1], parts, pl.program_id(1))

    args = pieces + [g]
    aliases = {}
    if buf is not None:
        in_specs.append(pl.BlockSpec(memory_space=pl.ANY))
        args.append(buf)
        aliases = {len(args) - 1: 0}
    return pl.pallas_call(
        body, name=name, grid=grid, in_specs=in_specs, out_specs=out_spec,
        out_shape=jax.ShapeDtypeStruct((4, G_ROWS, FLAT), F32), input_output_aliases=aliases,
        compiler_params=_cparams(sem),
    )(*args)


def _outproj_ln_fwd(ys, h, w, g, b):
    t = h.shape[0]
    npc = len(ys)

    def body(*refs):
        h_ref, w_ref, g_ref, b_ref, hn_ref, xh_ref, rs_ref = refs[npc:]
        r = ALPHA * h_ref[...]
        off = 0
        for y_ref in refs[:npc]:
            k = y_ref.shape[1]
            r = r + _dot(y_ref[...], w_ref[off:off + k, :])
            off += k
        out, xhat, rstd = _ln_fwd(r, g_ref[...], b_ref[...])
        hn_ref[...] = out
        xh_ref[...] = xhat
        rs_ref[...] = rstd

    return pl.pallas_call(
        body, name="outproj_ln_fwd", grid=(t // TM,),
        in_specs=[_rows(TM, y.shape[1]) for y in ys] + [_rows(TM, D_MODEL), _const((D_MODEL, D_MODEL)),
                                                        _const((1, D_MODEL)), _const((1, D_MODEL))],
        out_specs=[_rows(TM, D_MODEL), _rows(TM, D_MODEL), _rows(TM, 1)],
        out_shape=[jax.ShapeDtypeStruct((t, D_MODEL), F32), jax.ShapeDtypeStruct((t, D_MODEL), F32),
                   jax.ShapeDtypeStruct((t, 1), F32)],
        compiler_params=_cparams(("arbitrary",)),
    )(*ys, h, w, g, b)


def _attn_probs(q_h, k_h):
    s = _dot_nt(q_h, k_h) * (1.0 / math.sqrt(XA_HEAD_DIM))
    m = jnp.max(s, axis=-1, keepdims=True)
    e = jnp.exp(s - m)
    return e / jnp.sum(e, axis=-1, keepdims=True)


def _attn_ln_fwd(h1, wq, wo, kb, vb, g, b):
    t = h1.shape[0]

    def body(h_ref, wq_ref, wo_ref, k_ref, v_ref, g_ref, b_ref, hn_ref, xh_ref, rs_ref, o_ref):
        h = h_ref[...]
        hb = h.astype(MXU_DTYPE)
        for hh in range(XA_HEADS):
            sl = slice(hh * XA_HEAD_DIM, (hh + 1) * XA_HEAD_DIM)
            p = _attn_probs(_dot(hb, wq_ref[:, sl]), k_ref[:, sl])
            o_ref[:, sl] = _dot(p, v_ref[:, sl]).astype(o_ref.dtype)
        r = ALPHA * h + _dot(o_ref[...], wo_ref[...])
        out, xhat, rstd = _ln_fwd(r, g_ref[...], b_ref[...])
        hn_ref[...] = out
        xh_ref[...] = xhat
        rs_ref[...] = rstd

    return pl.pallas_call(
        body, name="attn_ln_fwd", grid=(t // TM,),
        in_specs=[_rows(TM, D_MODEL), _const((D_MODEL, D_MODEL)), _const((D_MODEL, D_MODEL)),
                  _const((MEM_LEN, D_MODEL)), _const((MEM_LEN, D_MODEL)), _const((1, D_MODEL)), _const((1, D_MODEL))],
        out_specs=[_rows(TM, D_MODEL), _rows(TM, D_MODEL), _rows(TM, 1), _rows(TM, D_MODEL)],
        out_shape=[jax.ShapeDtypeStruct((t, D_MODEL), F32), jax.ShapeDtypeStruct((t, D_MODEL), F32),
                   jax.ShapeDtypeStruct((t, 1), F32), jax.ShapeDtypeStruct((t, D_MODEL), MXU_DTYPE)],
        compiler_params=_cparams(("arbitrary",)),
    )(h1, wq, wo, kb, vb, g, b)


def _attn_ln_bwd(dh2, xhat, rstd, g, h1, wq, wo, kb, vb):
    t = h1.shape[0]

    def body(dh_ref, xh_ref, rs_ref, g_ref, h_ref, wq_ref, wo_ref, k_ref, v_ref,
             dr_ref, dq_ref, dh1_ref, dk_ref, dv_ref, dg_ref, db_ref):
        i = pl.program_id(0)

        @pl.when(i == 0)
        def _():
            dk_ref[...] = jnp.zeros_like(dk_ref)
            dv_ref[...] = jnp.zeros_like(dv_ref)
            dg_ref[...] = jnp.zeros_like(dg_ref)
            db_ref[...] = jnp.zeros_like(db_ref)

        dout = dh_ref[...]
        xh = xh_ref[...]
        dg_ref[...] += _sum0(dout * xh)
        db_ref[...] += _sum0(dout)
        dr = _ln_bwd(dout, xh, rs_ref[...], g_ref[...])
        drb = dr.astype(MXU_DTYPE)
        dr_ref[...] = drb
        hb = h_ref[...].astype(MXU_DTYPE)
        scale = 1.0 / math.sqrt(XA_HEAD_DIM)
        for hh in range(XA_HEADS):
            sl = slice(hh * XA_HEAD_DIM, (hh + 1) * XA_HEAD_DIM)
            q_h = _dot(hb, wq_ref[:, sl])
            do_h = _dot_nt(drb, wo_ref[sl, :])
            p = _attn_probs(q_h, k_ref[:, sl])
            dp = _dot_nt(do_h, v_ref[:, sl])
            ds = p * (dp - jnp.sum(dp * p, axis=-1, keepdims=True)) * scale
            dq_ref[:, sl] = _dot(ds, k_ref[:, sl]).astype(dq_ref.dtype)
            dk_ref[:, sl] += _dot_tn(ds, q_h)
            dv_ref[:, sl] += _dot_tn(p, do_h)
        dh1_ref[...] = ALPHA * dr + _dot_nt(dq_ref[...], wq_ref[...])

    return pl.pallas_call(
        body, name="attn_ln_bwd", grid=(t // TM,),
        in_specs=[_rows(TM, D_MODEL), _rows(TM, D_MODEL), _rows(TM, 1), _const((1, D_MODEL)), _rows(TM, D_MODEL),
                  _const((D_MODEL, D_MODEL)), _const((D_MODEL, D_MODEL)), _const((MEM_LEN, D_MODEL)),
                  _const((MEM_LEN, D_MODEL))],
        out_specs=[_rows(TM, D_MODEL), _rows(TM, D_MODEL), _rows(TM, D_MODEL), _const((MEM_LEN, D_MODEL)),
                   _const((MEM_LEN, D_MODEL)), _const((1, D_MODEL)), _const((1, D_MODEL))],
        out_shape=[jax.ShapeDtypeStruct((t, D_MODEL), MXU_DTYPE), jax.ShapeDtypeStruct((t, D_MODEL), MXU_DTYPE),
                   jax.ShapeDtypeStruct((t, D_MODEL), F32), jax.ShapeDtypeStruct((MEM_LEN, D_MODEL), F32),
                   jax.ShapeDtypeStruct((MEM_LEN, D_MODEL), F32), jax.ShapeDtypeStruct((1, D_MODEL), F32),
                   jax.ShapeDtypeStruct((1, D_MODEL), F32)],
        compiler_params=_cparams(("arbitrary",)),
    )(dh2, xhat, rstd, g, h1, wq, wo, kb, vb)


FF_CHUNK = 1024
N_FF = D_FF // FF_CHUNK


def _load_resident(pairs, sems):
    copies = [pltpu.make_async_copy(src, dst, sems.at[k]) for k, (src, dst) in enumerate(pairs)]
    for cp in copies:
        cp.start()
    for cp in copies:
        cp.wait()


def _mlp_ln_fwd(h2, w1, w2, g, b):
    t = h2.shape[0]

    def body(h_ref, w1_hbm, w2_hbm, g_ref, b_ref, hn_ref, xh_ref, rs_ref, hd_ref, w1_v, w2_v, acc_ref, sems):
        @pl.when(pl.program_id(0) == 0)
        def _():
            _load_resident([(w1_hbm, w1_v), (w2_hbm, w2_v)], sems)

        h = h_ref[...]
        hb = h.astype(MXU_DTYPE)
        acc_ref[...] = ALPHA * h
        for j in range(N_FF):
            sl = slice(j * FF_CHUNK, (j + 1) * FF_CHUNK)
            u = _dot(hb, w1_v[:, sl])
            hd = jnp.square(jnp.maximum(u, 0.0)).astype(MXU_DTYPE)
            hd_ref[:, sl] = hd
            acc_ref[...] += _dot(hd, w2_v[sl, :])
        out, xhat, rstd = _ln_fwd(acc_ref[...], g_ref[...], b_ref[...])
        hn_ref[...] = out
        xh_ref[...] = xhat
        rs_ref[...] = rstd

    return pl.pallas_call(
        body, name="mlp_ln_fwd", grid=(t // TM,),
        in_specs=[_rows(TM, D_MODEL), _hbm(), _hbm(), _const((1, D_MODEL)), _const((1, D_MODEL))],
        out_specs=[_rows(TM, D_MODEL), _rows(TM, D_MODEL), _rows(TM, 1), _rows(TM, D_FF)],
        out_shape=[jax.ShapeDtypeStruct((t, D_MODEL), F32), jax.ShapeDtypeStruct((t, D_MODEL), F32),
                   jax.ShapeDtypeStruct((t, 1), F32), jax.ShapeDtypeStruct((t, D_FF), MXU_DTYPE)],
        scratch_shapes=[pltpu.VMEM((D_MODEL, D_FF), MXU_DTYPE), pltpu.VMEM((D_FF, D_MODEL), MXU_DTYPE),
                        pltpu.VMEM((TM, D_MODEL), F32), pltpu.SemaphoreType.DMA((2,))],
        compiler_params=_cparams(("arbitrary",)),
    )(h2, w1, w2, g, b)


def _mlp_ln_bwd(dh3, xhat, rstd, g, hdn, w1, w2):
    t = dh3.shape[0]

    def body(dh_ref, xh_ref, rs_ref, g_ref, hd_ref, w1_hbm, w2_hbm,
             dr_ref, du_ref, dh2_ref, dg_ref, db_ref, w1_v, w2_v, acc_ref, sems):
        @pl.when(pl.program_id(0) == 0)
        def _():
            _load_resident([(w1_hbm, w1_v), (w2_hbm, w2_v)], sems)
            dg_ref[...] = jnp.zeros_like(dg_ref)
            db_ref[...] = jnp.zeros_like(db_ref)

        dout = dh_ref[...]
        xh = xh_ref[...]
        dg_ref[...] += _sum0(dout * xh)
        db_ref[...] += _sum0(dout)
        dr = _ln_bwd(dout, xh, rs_ref[...], g_ref[...])
        drb = dr.astype(MXU_DTYPE)
        dr_ref[...] = drb
        acc_ref[...] = ALPHA * dr
        for j in range(N_FF):
            sl = slice(j * FF_CHUNK, (j + 1) * FF_CHUNK)
            dhd = _dot_nt(drb, w2_v[sl, :])
            du = (dhd * (2.0 * jnp.sqrt(hd_ref[:, sl].astype(F32)))).astype(MXU_DTYPE)
            du_ref[:, sl] = du
            acc_ref[...] += _dot_nt(du, w1_v[:, sl])
        dh2_ref[...] = acc_ref[...]

    tm = TM // 2
    return pl.pallas_call(
        body, name="mlp_ln_bwd", grid=(t // tm,),
        in_specs=[_rows(tm, D_MODEL), _rows(tm, D_MODEL), _rows(tm, 1), _const((1, D_MODEL)), _rows(tm, D_FF),
                  _hbm(), _hbm()],
        out_specs=[_rows(tm, D_MODEL), _rows(tm, D_FF), _rows(tm, D_MODEL), _const((1, D_MODEL)),
                   _const((1, D_MODEL))],
        out_shape=[jax.ShapeDtypeStruct((t, D_MODEL), MXU_DTYPE), jax.ShapeDtypeStruct((t, D_FF), MXU_DTYPE),
                   jax.ShapeDtypeStruct((t, D_MODEL), F32), jax.ShapeDtypeStruct((1, D_MODEL), F32),
                   jax.ShapeDtypeStruct((1, D_MODEL), F32)],
        scratch_shapes=[pltpu.VMEM((D_MODEL, D_FF), MXU_DTYPE), pltpu.VMEM((D_FF, D_MODEL), MXU_DTYPE),
                        pltpu.VMEM((tm, D_MODEL), F32), pltpu.SemaphoreType.DMA((2,))],
        compiler_params=_cparams(("arbitrary",)),
    )(dh3, xhat, rstd, g, hdn, w1, w2)


def _outproj_ln_bwd(dh1, xhat, rstd, g, w):
    t = dh1.shape[0]

    def body(dh_ref, xh_ref, rs_ref, g_ref, w_ref, dr_ref, res_ref, dy_ref, dg_ref, db_ref):
        i = pl.program_id(0)

        @pl.when(i == 0)
        def _():
            dg_ref[...] = jnp.zeros_like(dg_ref)
            db_ref[...] = jnp.zeros_like(db_ref)

        dout = dh_ref[...]
        xh = xh_ref[...]
        dg_ref[...] += _sum0(dout * xh)
        db_ref[...] += _sum0(dout)
        dr = _ln_bwd(dout, xh, rs_ref[...], g_ref[...])
        dr_ref[...] = dr.astype(dr_ref.dtype)
        res_ref[...] = ALPHA * dr
        dy_ref[...] = _dot_nt(dr, w_ref[...])

    return pl.pallas_call(
        body, name="outproj_ln_bwd", grid=(t // TM,),
        in_specs=[_rows(TM, D_MODEL), _rows(TM, D_MODEL), _rows(TM, 1), _const((1, D_MODEL)),
                  _const((D_MODEL, D_MODEL))],
        out_specs=[_rows(TM, D_MODEL), _rows(TM, D_MODEL), _rows(TM, D_MODEL), _const((1, D_MODEL)),
                   _const((1, D_MODEL))],
        out_shape=[jax.ShapeDtypeStruct((t, D_MODEL), MXU_DTYPE), jax.ShapeDtypeStruct((t, D_MODEL), F32),
                   jax.ShapeDtypeStruct((t, D_MODEL), F32), jax.ShapeDtypeStruct((1, D_MODEL), F32),
                   jax.ShapeDtypeStruct((1, D_MODEL), F32)],
        compiler_params=_cparams(("arbitrary",)),
    )(dh1, xhat, rstd, g, w)


def _loss_fwd_bwd(h, target):
    t = h.shape[0]

    def body(h_ref, t_ref, l_ref, dh_ref):
        i = pl.program_id(0)

        @pl.when(i == 0)
        def _():
            l_ref[...] = jnp.zeros_like(l_ref)

        e = h_ref[...] - t_ref[...]
        dh_ref[...] = e * (1.0 / D_MODEL)
        per_tok = jnp.mean(e * e, axis=-1, keepdims=True)
        l_ref[...] += 0.5 * jnp.sum(per_tok, axis=0, keepdims=True)

    return pl.pallas_call(
        body, name="loss_fwd_bwd", grid=(t // TM,),
        in_specs=[_rows(TM, D_MODEL), _rows(TM, D_MODEL)],
        out_specs=[_const((1, 1)), _rows(TM, D_MODEL)],
        out_shape=[jax.ShapeDtypeStruct((1, 1), F32), jax.ShapeDtypeStruct((t, D_MODEL), F32)],
        compiler_params=_cparams(("arbitrary",)),
    )(h, target)


def _pick_col(x, idx):
    lane = lax.broadcasted_iota(jnp.int32, x.shape, 1)
    return jnp.sum(jnp.where(lane == idx, x, 0.0), axis=1, keepdims=True)


def _pick_row(x, idx):
    sub = lax.broadcasted_iota(jnp.int32, x.shape, 0)
    return jnp.sum(jnp.where(sub == idx, x, 0.0), axis=0, keepdims=True)


def _conv_taps(pad_ref, w, tm, base):
    acc = w[0:1, :] * pad_ref[base:base + tm, :]
    for k in range(1, 4):
        acc = acc + w[k:k + 1, :] * pad_ref[base + k:base + k + tm, :]
    return acc


def _ssd_chunk_common(adt_c, tri):
    cs = _dot_f32(tri, adt_c)
    return cs, cs.T, jnp.exp(cs)


def _ssd_head_terms(cs, cst, ecs, dt_c, h, tri):
    cs_col = _pick_col(cs, h)
    cs_row = _pick_row(cst, h)
    dt_col = _pick_col(dt_c, h)
    cs_last = cs_col[SSD_CHUNK - 1:SSD_CHUNK, :]
    lmat = jnp.exp(jnp.where(tri > 0.0, cs_col - cs_row, -1e30))
    ecs_col = _pick_col(ecs, h)
    decay_col = jnp.exp(cs_last - cs_col)
    return cs_col, dt_col, cs_last, lmat, ecs_col, decay_col


def _ssd_fwd(proj, cw, cb, dtb, a_neg, d_lanes, nw):
    t = proj.shape[0]
    tm = SSD_FWD_TM
    nt = t // tm
    ncq = tm // SSD_CHUNK
    hb = tm // SUBLANES

    def body(xbc_ref, halo_ref, z_ref, dt_ref, cw_ref, cb_ref, dtb_ref, a_ref, d_ref, nw_ref,
             y_ref, yy_ref, st_ref, xpad, xact, state):
        i = pl.program_id(0)

        @pl.when(i == 0)
        def _():
            state[...] = jnp.zeros_like(state)

        xpad[0:SUBLANES, :] = jnp.where(i > 0, halo_ref[...], 0.0)
        xpad[SUBLANES:SUBLANES + tm, :] = xbc_ref[...]
        acc = cb_ref[...] + _conv_taps(xpad, cw_ref[...], tm, SUBLANES - 3)
        xact[...] = acc * _sigmoid(acc)
        dt = _softplus(dt_ref[...] + dtb_ref[...])
        adt = dt * a_ref[...]
        r_i = lax.broadcasted_iota(jnp.int32, (SSD_CHUNK, SSD_CHUNK), 0)
        c_i = lax.broadcasted_iota(jnp.int32, (SSD_CHUNK, SSD_CHUNK), 1)
        tri = (r_i >= c_i).astype(F32)
        lane1 = lax.broadcasted_iota(jnp.int32, (1, LANES), 1)
        for c in range(ncq):
            sl = slice(c * SSD_CHUNK, (c + 1) * SSD_CHUNK)
            dt_c = dt[sl]
            cs, cst, ecs = _ssd_chunk_common(adt[sl], tri)
            for g in range(2):
                bg = xact[sl, 512 + g * 128:512 + (g + 1) * 128]
                cg = xact[sl, 768 + g * 128:768 + (g + 1) * 128]
                cbm = _dot_nt(cg, bg)
                for pr in range(2):
                    pi = g * 2 + pr
                    psl = slice(pi * 128, (pi + 1) * 128)
                    xp = xact[sl, psl]
                    prev = state[pi]
                    st_ref[c, pi] = prev
                    yp = xp * d_ref[:, psl]
                    new_s = jnp.zeros((SSD_STATE, LANES), F32)
                    dec_lane = jnp.zeros((1, LANES), F32)
                    for hh in range(2):
                        h = g * 4 + pr * 2 + hh
                        lm = (lane1 >= 64) if hh else (lane1 < 64)
                        _, dt_col, cs_last, lmat, ecs_col, decay_col = _ssd_head_terms(cs, cst, ecs, dt_c, h, tri)
                        xdt = jnp.where(lm, xp, 0.0) * dt_col
                        yp = yp + _dot(cbm * lmat, xdt)
                        yp = yp + _dot(cg * ecs_col, jnp.where(lm, prev, 0.0))
                        new_s = new_s + _dot_tn(bg * decay_col, xdt)
                        dec_lane = dec_lane + jnp.where(lm, jnp.exp(cs_last), 0.0)
                    state[pi] = prev * dec_lane + new_s
                    yy_ref[sl, psl] = yp
        yy = yy_ref[...]
        z = z_ref[...]
        yg = yy * (z * _sigmoid(z))
        ms = jnp.mean(yg * yg, axis=-1, keepdims=True)
        y_ref[...] = (yg * lax.rsqrt(ms + LN_EPS) * nw_ref[...]).astype(y_ref.dtype)

    halo_map = lambda i: (jnp.maximum(i * hb - 1, 0), 0)
    return pl.pallas_call(
        body, name="ssd_fwd", grid=(nt,),
        in_specs=[pl.BlockSpec((tm, SSD_XBC), lambda i: (i, 0)), pl.BlockSpec((SUBLANES, SSD_XBC), halo_map),
                  pl.BlockSpec((tm, SSD_WIDTH), lambda i: (i, P_Z // SSD_WIDTH)),
                  pl.BlockSpec((tm, LANES), lambda i: (i, P_DT // LANES)),
                  _const((4, SSD_XBC)), _const((1, SSD_XBC)), _const((1, LANES)), _const((1, LANES)),
                  _const((1, SSD_WIDTH)), _const((1, SSD_WIDTH))],
        out_specs=[_rows(tm, SSD_WIDTH), _rows(tm, SSD_WIDTH),
                   pl.BlockSpec((ncq, 4, SSD_STATE, LANES), lambda i: (i, 0, 0, 0))],
        out_shape=[jax.ShapeDtypeStruct((t, SSD_WIDTH), MXU_DTYPE), jax.ShapeDtypeStruct((t, SSD_WIDTH), F32),
                   jax.ShapeDtypeStruct((t // SSD_CHUNK, 4, SSD_STATE, LANES), F32)],
        scratch_shapes=[pltpu.VMEM((tm + SUBLANES, SSD_XBC), F32), pltpu.VMEM((tm, SSD_XBC), F32),
                        pltpu.VMEM((4, SSD_STATE, LANES), F32)],
        compiler_params=_cparams(("arbitrary",)),
    )(proj, proj, proj, proj, cw, cb, dtb, a_neg, d_lanes, nw)


def _ssd_bwd(dycat, proj, yy, states, cw, cb, dtb, a_neg, d_lanes, nw):
    t = proj.shape[0]
    tm = SSD_BWD_TM
    nt = t // tm
    ncq = tm // SSD_CHUNK
    hb = tm // SUBLANES

    def body(dy_ref, xbc_ref, halo_ref, z_ref, dt_ref, yy_ref, st_ref, cw_ref, cb_ref, dtb_ref, a_ref, d_ref, nw_ref,
             dxbc_ref, dz_ref, ddt_ref, dcw_ref, dcb_ref, ddtb_ref, da_ref, dd_ref, dnw_ref,
             xpad, xact, dxact, dpad, dstate, dnext):
        i = pl.program_id(0)

        @pl.when(i == 0)
        def _():
            for r in (dcw_ref, dcb_ref, ddtb_ref, da_ref, dd_ref, dnw_ref, dstate, dnext):
                r[...] = jnp.zeros_like(r)

        xpad[0:SUBLANES, :] = jnp.where(i < nt - 1, halo_ref[...], 0.0)
        xpad[SUBLANES:SUBLANES + tm, :] = xbc_ref[...]
        cw_v = cw_ref[...]
        acc = cb_ref[...] + _conv_taps(xpad, cw_v, tm, SUBLANES - 3)
        sig = _sigmoid(acc)
        xact[...] = acc * sig
        dt_raw = dt_ref[...] + dtb_ref[...]
        dt = _softplus(dt_raw)
        a_v = a_ref[...]
        adt = dt * a_v
        yy = yy_ref[...]
        z = z_ref[...]
        sz = _sigmoid(z)
        siluz = z * sz
        yg = yy * siluz
        ms = jnp.mean(yg * yg, axis=-1, keepdims=True)
        rinv = lax.rsqrt(ms + LN_EPS)
        dout = dy_ref[...]
        dnw_ref[...] += _sum0(dout * yg * rinv)
        dyn = dout * nw_ref[...]
        dyg = rinv * dyn - yg * (rinv * rinv * rinv) * jnp.mean(dyn * yg, axis=-1, keepdims=True)
        dyy = dyg * siluz
        dz_ref[...] = (dyg * yy * (sz * (1.0 + z * (1.0 - sz)))).astype(dz_ref.dtype)
        dd_ref[...] += _sum0(dyy * xact[:, 0:SSD_WIDTH])

        r_i = lax.broadcasted_iota(jnp.int32, (SSD_CHUNK, SSD_CHUNK), 0)
        c_i = lax.broadcasted_iota(jnp.int32, (SSD_CHUNK, SSD_CHUNK), 1)
        tri = (r_i >= c_i).astype(F32)
        lane1 = lax.broadcasted_iota(jnp.int32, (1, LANES), 1)
        for c in reversed(range(ncq)):
            sl = slice(c * SSD_CHUNK, (c + 1) * SSD_CHUNK)
            dt_c = dt[sl]
            cs, cst, ecs = _ssd_chunk_common(adt[sl], tri)
            cacc = jnp.zeros((SSD_CHUNK, LANES), F32)
            racc = jnp.zeros((SSD_CHUNK, LANES), F32)
            ddtx = jnp.zeros((SSD_CHUNK, LANES), F32)
            for g in range(2):
                bg = xact[sl, 512 + g * 128:512 + (g + 1) * 128]
                cg = xact[sl, 768 + g * 128:768 + (g + 1) * 128]
                cbm = _dot_nt(cg, bg)
                dcb_m = jnp.zeros((SSD_CHUNK, SSD_CHUNK), F32)
                dbg = jnp.zeros((SSD_CHUNK, SSD_STATE), F32)
                dcg = jnp.zeros((SSD_CHUNK, SSD_STATE), F32)
                for pr in range(2):
                    pi = g * 2 + pr
                    psl = slice(pi * 128, (pi + 1) * 128)
                    xp = xact[sl, psl]
                    dyp = dyy[sl, psl]
                    prev = st_ref[c, pi]
                    ds_all = dstate[pi]
                    dxdt_p = jnp.zeros((SSD_CHUNK, LANES), F32)
                    dprev_new = jnp.zeros((SSD_STATE, LANES), F32)
                    dec_lane = jnp.zeros((1, LANES), F32)
                    dt_lanes = jnp.zeros((SSD_CHUNK, LANES), F32)
                    for hh in range(2):
                        h = g * 4 + pr * 2 + hh
                        lm = (lane1 >= 64) if hh else (lane1 < 64)
                        oh_l = (c_i == h).astype(F32)
                        oh_s = (r_i == h).astype(F32)
                        _, dt_col, cs_last, lmat, ecs_col, decay_col = _ssd_head_terms(cs, cst, ecs, dt_c, h, tri)
                        gm = cbm * lmat
                        xm = jnp.where(lm, xp, 0.0)
                        xdt = xm * dt_col
                        dym = jnp.where(lm, dyp, 0.0)
                        prevm = jnp.where(lm, prev, 0.0)
                        dsm = jnp.where(lm, ds_all, 0.0)
                        bdec = bg * decay_col
                        dxdt = _dot_tn(gm, dym) + _dot(bdec, dsm)
                        dxdt_p = dxdt_p + dxdt
                        ddtx = ddtx + oh_l * jnp.sum(dxdt * xm, axis=1, keepdims=True)
                        dt_lanes = dt_lanes + jnp.where(lm, dt_col, 0.0)
                        dgm = _dot_nt(dym, xdt)
                        dcb_m = dcb_m + dgm * lmat
                        w = dgm * gm
                        cacc = cacc + oh_l * jnp.sum(w, axis=1, keepdims=True)
                        racc = racc - oh_s * jnp.sum(w, axis=0, keepdims=True)
                        dce = _dot_nt(dym, prevm)
                        dcg = dcg + dce * ecs_col
                        cacc = cacc + oh_l * (jnp.sum(dce * cg, axis=1, keepdims=True) * ecs_col)
                        dprev_new = dprev_new + _dot_tn(cg * ecs_col, dym)
                        dbdec = _dot_nt(xdt, dsm)
                        dbg = dbg + dbdec * decay_col
                        dd = jnp.sum(dbdec * bg, axis=1, keepdims=True) * decay_col
                        cacc = cacc - oh_l * dd
                        cd = jnp.exp(cs_last)
                        dlast = jnp.sum(dd, axis=0, keepdims=True) + jnp.sum(
                            jnp.sum(dsm * prevm, axis=1, keepdims=True), axis=0, keepdims=True) * cd
                        cacc = cacc + jnp.where((r_i == SSD_CHUNK - 1) & (c_i == h), dlast, 0.0)
                        dec_lane = dec_lane + jnp.where(lm, cd, 0.0)
                    dstate[pi] = ds_all * dec_lane + dprev_new
                    dxact[sl, psl] = dxdt_p * dt_lanes + dyp * d_ref[:, psl]
                dcg = dcg + _dot(dcb_m, bg)
                dbg = dbg + _dot_tn(dcb_m, cg)
                dxact[sl, 512 + g * 128:512 + (g + 1) * 128] = dbg
                dxact[sl, 768 + g * 128:768 + (g + 1) * 128] = dcg
            dcs = cacc + racc.T
            dadt = _dot_f32((r_i <= c_i).astype(F32), dcs)
            ddt = dadt * a_v + ddtx
            da_ref[...] += _sum0(dadt * dt_c)
            ddt_raw = ddt * _sigmoid(dt_raw[sl])
            ddt_ref[sl, :] = ddt_raw.astype(ddt_ref.dtype)
            ddtb_ref[...] += _sum0(ddt_raw)
        dacc = dxact[...] * (sig * (1.0 + acc * (1.0 - sig)))
        dcb_ref[...] += _sum0(dacc)
        for k in range(4):
            dcw_ref[k:k + 1, :] += _sum0(dacc * xpad[SUBLANES - 3 + k:SUBLANES - 3 + k + tm, :])
        dpad[0:tm, :] = dacc
        dpad[tm:tm + SUBLANES, :] = dnext[...]
        dx = cw_v[0:1, :] * dpad[3:3 + tm, :]
        for k in range(1, 4):
            dx = dx + cw_v[k:k + 1, :] * dpad[3 - k:3 - k + tm, :]
        dxbc_ref[...] = dx.astype(dxbc_ref.dtype)
        dnext[...] = dacc[0:SUBLANES, :]

    rev = lambda i: nt - 1 - i
    halo_map = lambda i: (jnp.maximum(rev(i) * hb - 1, 0), 0)
    rrow = lambda n, col=0: pl.BlockSpec((tm, n), lambda i: (rev(i), col))
    return pl.pallas_call(
        body, name="ssd_bwd", grid=(nt,),
        in_specs=[rrow(SSD_WIDTH), rrow(SSD_XBC), pl.BlockSpec((SUBLANES, SSD_XBC), halo_map),
                  rrow(SSD_WIDTH, P_Z // SSD_WIDTH), rrow(LANES, P_DT // LANES), rrow(SSD_WIDTH),
                  pl.BlockSpec((ncq, 4, SSD_STATE, LANES), lambda i: (rev(i), 0, 0, 0)),
                  _const((4, SSD_XBC)), _const((1, SSD_XBC)), _const((1, LANES)), _const((1, LANES)),
                  _const((1, SSD_WIDTH)), _const((1, SSD_WIDTH))],
        out_specs=[rrow(SSD_XBC), rrow(SSD_WIDTH), rrow(LANES), _const((SUBLANES, SSD_XBC)), _const((1, SSD_XBC)),
                   _const((1, LANES)), _const((1, LANES)), _const((1, SSD_WIDTH)), _const((1, SSD_WIDTH))],
        out_shape=[jax.ShapeDtypeStruct((t, SSD_XBC), MXU_DTYPE), jax.ShapeDtypeStruct((t, SSD_WIDTH), MXU_DTYPE),
                   jax.ShapeDtypeStruct((t, LANES), MXU_DTYPE), jax.ShapeDtypeStruct((SUBLANES, SSD_XBC), F32),
                   jax.ShapeDtypeStruct((1, SSD_XBC), F32), jax.ShapeDtypeStruct((1, LANES), F32),
                   jax.ShapeDtypeStruct((1, LANES), F32), jax.ShapeDtypeStruct((1, SSD_WIDTH), F32),
                   jax.ShapeDtypeStruct((1, SSD_WIDTH), F32)],
        scratch_shapes=[pltpu.VMEM((tm + SUBLANES, SSD_XBC), F32), pltpu.VMEM((tm, SSD_XBC), F32),
                        pltpu.VMEM((tm, SSD_XBC), F32), pltpu.VMEM((tm + SUBLANES, SSD_XBC), F32),
                        pltpu.VMEM((4, SSD_STATE, LANES), F32), pltpu.VMEM((SUBLANES, SSD_XBC), F32)],
        compiler_params=_cparams(("arbitrary",)),
    )(dycat, proj, proj, proj, proj, yy, states, cw, cb, dtb, a_neg, d_lanes, nw)


def _cmul_add(ar, ai, br, bi, cr, ci):
    return ar + br * cr - bi * ci, ai + br * ci + bi * cr


def _s5_fwd(proj, bre, bim, cre, cim, d_skip, glu_w, glu_b, coef):
    t = proj.shape[0]
    tm = SCAN_TM
    ng = tm // SUBLANES

    def body(u_ref, bre_ref, bim_ref, cre_ref, cim_ref, d_ref, w_ref, b_ref, coef_ref,
             y_ref, y2_ref, hre_ref, him_ref, carry):
        i = pl.program_id(0)

        @pl.when(i == 0)
        def _():
            carry[...] = jnp.zeros_like(carry)

        u = u_ref[...]
        hre_ref[...] = _dot(u, bre_ref[...])
        him_ref[...] = _dot(u, bim_ref[...])

        def step(gi, car):
            cr_, ci_ = car
            rows = pl.ds(pl.multiple_of(gi * SUBLANES, SUBLANES), SUBLANES)
            r = hre_ref[rows, :]
            m = him_ref[rows, :]
            for k, sh in enumerate((1, 2, 4)):
                r, m = _cmul_add(r, m, coef_ref[k, 0], coef_ref[k, 1], pltpu.roll(r, sh, 0), pltpu.roll(m, sh, 0))
            r, m = _cmul_add(r, m, coef_ref[3, 0], coef_ref[3, 1], cr_, ci_)
            hre_ref[rows, :] = r
            him_ref[rows, :] = m
            return (jnp.broadcast_to(r[SUBLANES - 1:SUBLANES, :], r.shape),
                    jnp.broadcast_to(m[SUBLANES - 1:SUBLANES, :], m.shape))

        cr_, ci_ = lax.fori_loop(0, ng, step, (carry[0], carry[1]))
        carry[0] = cr_
        carry[1] = ci_
        y2 = _dot(hre_ref[...], cre_ref[...]) - _dot(him_ref[...], cim_ref[...]) + d_ref[...] * u
        y2_ref[...] = y2
        ya = _gelu(y2)
        y_ref[...] = (ya * _sigmoid(_dot(ya, w_ref[...]) + b_ref[...])).astype(y_ref.dtype)

    return pl.pallas_call(
        body, name="s5_fwd", grid=(t // tm,),
        in_specs=[pl.BlockSpec((tm, S5_WIDTH), lambda i: (i, P_U // S5_WIDTH)),
                  _const((S5_WIDTH, S5_NSTATE)), _const((S5_WIDTH, S5_NSTATE)), _const((S5_NSTATE, S5_WIDTH)),
                  _const((S5_NSTATE, S5_WIDTH)), _const((1, S5_WIDTH)), _const((S5_WIDTH, S5_WIDTH)),
                  _const((1, S5_WIDTH)), _const((5, 2, SUBLANES, S5_NSTATE))],
        out_specs=[_rows(tm, S5_WIDTH), _rows(tm, S5_WIDTH), _rows(tm, S5_NSTATE), _rows(tm, S5_NSTATE)],
        out_shape=[jax.ShapeDtypeStruct((t, S5_WIDTH), MXU_DTYPE), jax.ShapeDtypeStruct((t, S5_WIDTH), F32),
                   jax.ShapeDtypeStruct((t, S5_NSTATE), F32), jax.ShapeDtypeStruct((t, S5_NSTATE), F32)],
        scratch_shapes=[pltpu.VMEM((2, SUBLANES, S5_NSTATE), F32)],
        compiler_params=_cparams(("arbitrary",)),
    )(proj, bre, bim, cre, cim, d_skip, glu_w, glu_b, coef)


def _s5_bwd(dycat, proj, y2, hre, him, bre, bim, cre, cim, d_skip, glu_w, glu_b, rcoef):
    t = proj.shape[0]
    tm = SCAN_TM
    nt = t // tm
    ng = tm // SUBLANES
    hb = tm // SUBLANES

    def body(dy_ref, u_ref, y2_ref, hre_ref, him_ref, hre_halo, him_halo, bre_ref, bim_ref, cre_ref, cim_ref, d_ref,
             w_ref, b_ref, coef_ref,
             du_ref, dbre_ref, dbim_ref, dcre_ref, dcim_ref, dlam_ref, dd_ref, dw_ref, dgb_ref,
             gre, gim, hpre, hpim, carry):
        i = pl.program_id(0)

        @pl.when(i == 0)
        def _():
            for r in (dbre_ref, dbim_ref, dcre_ref, dcim_ref, dlam_ref, dd_ref, dw_ref, dgb_ref, carry):
                r[...] = jnp.zeros_like(r)

        u = u_ref[...]
        y2 = y2_ref[...]
        dout = dy_ref[...]
        ya = _gelu(y2)
        sg = _sigmoid(_dot(ya, w_ref[...]) + b_ref[...])
        dv = dout * ya * sg * (1.0 - sg)
        dya = dout * sg + _dot_nt(dv, w_ref[...])
        dw_ref[...] += _dot_tn(ya, dv)
        dgb_ref[...] += _sum0(dv)
        dy2 = dya * _gelu_grad(y2)
        dd_ref[...] += _sum0(dy2 * u)
        hre_v = hre_ref[...]
        him_v = him_ref[...]
        dcre_ref[...] += _dot_tn(hre_v, dy2)
        dcim_ref[...] -= _dot_tn(him_v, dy2)
        gre[...] = _dot_nt(dy2, cre_ref[...])
        gim[...] = -_dot_nt(dy2, cim_ref[...])
        first = i == nt - 1
        hpre[0:SUBLANES, :] = jnp.where(first, 0.0, hre_halo[...])
        hpim[0:SUBLANES, :] = jnp.where(first, 0.0, him_halo[...])
        hpre[SUBLANES:SUBLANES + tm, :] = hre_v
        hpim[SUBLANES:SUBLANES + tm, :] = him_v
        row0 = lax.broadcasted_iota(jnp.int32, (SUBLANES, S5_NSTATE), 0) == 0

        def step(k, car):
            cr_, ci_, dlr, dli = car
            gi = ng - 1 - k
            rows = pl.ds(pl.multiple_of(gi * SUBLANES, SUBLANES), SUBLANES)
            nrows = pl.ds(pl.multiple_of(gi * SUBLANES + SUBLANES, SUBLANES), SUBLANES)
            r = gre[rows, :]
            m = gim[rows, :]
            for kk, sh in enumerate((1, 2, 4)):
                r, m = _cmul_add(r, m, coef_ref[kk, 0], coef_ref[kk, 1], pltpu.roll(r, SUBLANES - sh, 0),
                                 pltpu.roll(m, SUBLANES - sh, 0))
            r, m = _cmul_add(r, m, coef_ref[3, 0], coef_ref[3, 1], cr_, ci_)
            gre[rows, :] = r
            gim[rows, :] = m
            pr_ = hpre[rows, :]
            pm_ = hpim[rows, :]
            hr_ = jnp.where(row0, jnp.broadcast_to(pr_[SUBLANES - 1:SUBLANES, :], pr_.shape),
                            pltpu.roll(hpre[nrows, :], 1, 0))
            hm_ = jnp.where(row0, jnp.broadcast_to(pm_[SUBLANES - 1:SUBLANES, :], pm_.shape),
                            pltpu.roll(hpim[nrows, :], 1, 0))
            dlr = dlr + hr_ * r + hm_ * m
            dli = dli + hr_ * m - hm_ * r
            return (jnp.broadcast_to(r[0:1, :], r.shape), jnp.broadcast_to(m[0:1, :], m.shape), dlr, dli)

        z8 = jnp.zeros((SUBLANES, S5_NSTATE), F32)
        cr_, ci_, dlr, dli = lax.fori_loop(0, ng, step, (carry[0], carry[1], z8, z8))
        carry[0] = cr_
        carry[1] = ci_
        dlam_ref[0] += dlr
        dlam_ref[1] += dli
        g_re = gre[...]
        g_im = gim[...]
        du_ref[...] = (dy2 * d_ref[...] + _dot_nt(g_re, bre_ref[...]) + _dot_nt(g_im, bim_ref[...])
                       ).astype(du_ref.dtype)
        dbre_ref[...] += _dot_tn(u, g_re)
        dbim_ref[...] += _dot_tn(u, g_im)

    rev = lambda i: nt - 1 - i
    rrow = lambda n, col=0: pl.BlockSpec((tm, n), lambda i: (rev(i), col))
    halo = pl.BlockSpec((SUBLANES, S5_NSTATE), lambda i: (jnp.maximum(rev(i) * hb - 1, 0), 0))
    return pl.pallas_call(
        body, name="s5_bwd", grid=(nt,),
        in_specs=[rrow(S5_WIDTH, 512 // S5_WIDTH), rrow(S5_WIDTH, P_U // S5_WIDTH), rrow(S5_WIDTH),
                  rrow(S5_NSTATE), rrow(S5_NSTATE), halo, halo,
                  _const((S5_WIDTH, S5_NSTATE)), _const((S5_WIDTH, S5_NSTATE)), _const((S5_NSTATE, S5_WIDTH)),
                  _const((S5_NSTATE, S5_WIDTH)), _const((1, S5_WIDTH)), _const((S5_WIDTH, S5_WIDTH)),
                  _const((1, S5_WIDTH)), _const((5, 2, SUBLANES, S5_NSTATE))],
        out_specs=[rrow(S5_WIDTH), _const((S5_WIDTH, S5_NSTATE)), _const((S5_WIDTH, S5_NSTATE)),
                   _const((S5_NSTATE, S5_WIDTH)), _const((S5_NSTATE, S5_WIDTH)), _const((2, SUBLANES, S5_NSTATE)),
                   _const((1, S5_WIDTH)), _const((S5_WIDTH, S5_WIDTH)), _const((1, S5_WIDTH))],
        out_shape=[jax.ShapeDtypeStruct((t, S5_WIDTH), MXU_DTYPE), jax.ShapeDtypeStruct((S5_WIDTH, S5_NSTATE), F32),
                   jax.ShapeDtypeStruct((S5_WIDTH, S5_NSTATE), F32), jax.ShapeDtypeStruct((S5_NSTATE, S5_WIDTH), F32),
                   jax.ShapeDtypeStruct((S5_NSTATE, S5_WIDTH), F32),
                   jax.ShapeDtypeStruct((2, SUBLANES, S5_NSTATE), F32), jax.ShapeDtypeStruct((1, S5_WIDTH), F32),
                   jax.ShapeDtypeStruct((S5_WIDTH, S5_WIDTH), F32), jax.ShapeDtypeStruct((1, S5_WIDTH), F32)],
        scratch_shapes=[pltpu.VMEM((tm, S5_NSTATE), F32), pltpu.VMEM((tm, S5_NSTATE), F32),
                        pltpu.VMEM((tm + SUBLANES, S5_NSTATE), F32), pltpu.VMEM((tm + SUBLANES, S5_NSTATE), F32),
                        pltpu.VMEM((2, SUBLANES, S5_NSTATE), F32)],
        compiler_params=_cparams(("arbitrary",)),
    )(dycat, proj, y2, hre, him, hre, him, bre, bim, cre, cim, d_skip, glu_w, glu_b, rcoef)


def _rg_gates(xc, wa, ba, wx, bx, nsp):
    r = _sigmoid(_dot(xc, wa) + ba)
    ig = _sigmoid(_dot(xc, wx) + bx)
    log_a = nsp * r
    a = jnp.exp(log_a)
    mult = jnp.sqrt(-_expm1(2.0 * log_a))
    return r, ig, a, mult


def _rg_fwd(proj, cw, cb, wa, ba, wx, bx, nsp):
    t = proj.shape[0]
    tm = SCAN_TM
    ng = tm // SUBLANES
    hb = tm // SUBLANES

    def body(x_ref, halo_ref, gt_ref, cw_ref, cb_ref, wa_ref, ba_ref, wx_ref, bx_ref, nsp_ref,
             y_ref, h_ref, xpad, abuf, carry):
        i = pl.program_id(0)

        @pl.when(i == 0)
        def _():
            carry[...] = jnp.zeros_like(carry)

        xpad[0:SUBLANES, :] = jnp.where(i > 0, halo_ref[...], 0.0)
        xpad[SUBLANES:SUBLANES + tm, :] = x_ref[...]
        xc = cb_ref[...] + _conv_taps(xpad, cw_ref[...], tm, SUBLANES - 3)
        _, ig, a, mult = _rg_gates(xc, wa_ref[...], ba_ref[...], wx_ref[...], bx_ref[...], nsp_ref[...])
        abuf[...] = a
        h_ref[...] = mult * (ig * xc)
        sub = lax.broadcasted_iota(jnp.int32, (SUBLANES, RG_WIDTH), 0)

        def step(gi, car):
            rows = pl.ds(pl.multiple_of(gi * SUBLANES, SUBLANES), SUBLANES)
            av = abuf[rows, :]
            bv = h_ref[rows, :]
            for sh in (1, 2, 4):
                m = sub >= sh
                bv = jnp.where(m, av * pltpu.roll(bv, sh, 0) + bv, bv)
                av = jnp.where(m, av * pltpu.roll(av, sh, 0), av)
            hv = bv + av * car
            h_ref[rows, :] = hv
            return jnp.broadcast_to(hv[SUBLANES - 1:SUBLANES, :], hv.shape)

        carry[...] = lax.fori_loop(0, ng, step, carry[...])
        y_ref[...] = (h_ref[...] * _gelu(gt_ref[...])).astype(y_ref.dtype)

    return pl.pallas_call(
        body, name="rg_fwd", grid=(t // tm,),
        in_specs=[pl.BlockSpec((tm, RG_WIDTH), lambda i: (i, P_XRG // RG_WIDTH)),
                  pl.BlockSpec((SUBLANES, RG_WIDTH), lambda i: (jnp.maximum(i * hb - 1, 0), P_XRG // RG_WIDTH)),
                  pl.BlockSpec((tm, RG_WIDTH), lambda i: (i, P_GRG // RG_WIDTH)),
                  _const((4, RG_WIDTH)), _const((1, RG_WIDTH)), _const((RG_WIDTH, RG_WIDTH)), _const((1, RG_WIDTH)),
                  _const((RG_WIDTH, RG_WIDTH)), _const((1, RG_WIDTH)), _const((1, RG_WIDTH))],
        out_specs=[_rows(tm, RG_WIDTH), _rows(tm, RG_WIDTH)],
        out_shape=[jax.ShapeDtypeStruct((t, RG_WIDTH), MXU_DTYPE), jax.ShapeDtypeStruct((t, RG_WIDTH), F32)],
        scratch_shapes=[pltpu.VMEM((tm + SUBLANES, RG_WIDTH), F32), pltpu.VMEM((tm, RG_WIDTH), F32),
                        pltpu.VMEM((SUBLANES, RG_WIDTH), F32)],
        compiler_params=_cparams(("arbitrary",)),
    )(proj, proj, proj, cw, cb, wa, ba, wx, bx, nsp)


def _rg_bwd(dycat, proj, hs, cw, cb, wa, ba, wx, bx, nsp):
    t = proj.shape[0]
    tm = SCAN_TM
    nt = t // tm
    ng = tm // SUBLANES
    hb = tm // SUBLANES

    def body(dy_ref, x_ref, halo_ref, gt_ref, h_ref, h_halo, cw_ref, cb_ref, wa_ref, ba_ref, wx_ref, bx_ref, nsp_ref,
             dx_ref, dgt_ref, dcw_ref, dcb_ref, dwa_ref, dba_ref, dwx_ref, dbx_ref, dnsp_ref,
             xpad, abuf, gbuf, hpad, dabuf, dpad, carry, dnext):
        i = pl.program_id(0)

        @pl.when(i == 0)
        def _():
            for r in (dcw_ref, dcb_ref, dwa_ref, dba_ref, dwx_ref, dbx_ref, dnsp_ref, carry, dnext):
                r[...] = jnp.zeros_like(r)

        first = i == nt - 1
        xpad[0:SUBLANES, :] = jnp.where(first, 0.0, halo_ref[...])
        xpad[SUBLANES:SUBLANES + tm, :] = x_ref[...]
        cw_v = cw_ref[...]
        xc = cb_ref[...] + _conv_taps(xpad, cw_v, tm, SUBLANES - 3)
        nsp_v = nsp_ref[...]
        r, ig, a, mult = _rg_gates(xc, wa_ref[...], ba_ref[...], wx_ref[...], bx_ref[...], nsp_v)
        abuf[...] = a
        hv = h_ref[...]
        hpad[0:SUBLANES, :] = jnp.where(first, 0.0, h_halo[...])
        hpad[SUBLANES:SUBLANES + tm, :] = hv
        gt = gt_ref[...]
        dout = dy_ref[...]
        dgt_ref[...] = (dout * hv * _gelu_grad(gt)).astype(dgt_ref.dtype)
        gbuf[...] = dout * _gelu(gt)
        sub = lax.broadcasted_iota(jnp.int32, (SUBLANES, RG_WIDTH), 0)
        last_row = sub == SUBLANES - 1
        row0 = sub == 0

        def step(k, car):
            gi = ng - 1 - k
            rows = pl.ds(pl.multiple_of(gi * SUBLANES, SUBLANES), SUBLANES)
            nrows = pl.ds(pl.multiple_of(gi * SUBLANES + SUBLANES, SUBLANES), SUBLANES)
            av = abuf[rows, :]
            bv = gbuf[rows, :] + jnp.where(last_row, car, 0.0)
            ev = jnp.where(last_row, 0.0, pltpu.roll(av, SUBLANES - 1, 0))
            for sh in (1, 2, 4):
                m = sub < SUBLANES - sh
                bv = jnp.where(m, bv + ev * pltpu.roll(bv, SUBLANES - sh, 0), bv)
                ev = jnp.where(m, ev * pltpu.roll(ev, SUBLANES - sh, 0), 0.0)
            gbuf[rows, :] = bv
            pv = hpad[rows, :]
            hprev = jnp.where(row0, jnp.broadcast_to(pv[SUBLANES - 1:SUBLANES, :], pv.shape),
                              pltpu.roll(hpad[nrows, :], 1, 0))
            dabuf[rows, :] = bv * hprev
            return jnp.broadcast_to((av * bv)[0:1, :], bv.shape)

        carry[...] = lax.fori_loop(0, ng, step, carry[...])
        gv = gbuf[...]
        da = dabuf[...]
        ix = ig * xc
        dmult = gv * ix
        dig = gv * mult * xc
        dxc = gv * mult * ig
        dlog_a = da * a - dmult * (a * a) / mult
        dnsp_ref[...] += _sum0(dlog_a * r)
        dpr = dlog_a * nsp_v * r * (1.0 - r)
        dpi = dig * ig * (1.0 - ig)
        dxc = dxc + _dot_nt(dpr, wa_ref[...]) + _dot_nt(dpi, wx_ref[...])
        dwa_ref[...] += _dot_tn(xc, dpr)
        dwx_ref[...] += _dot_tn(xc, dpi)
        dba_ref[...] += _sum0(dpr)
        dbx_ref[...] += _sum0(dpi)
        dcb_ref[...] += _sum0(dxc)
        for k in range(4):
            dcw_ref[k:k + 1, :] += _sum0(dxc * xpad[SUBLANES - 3 + k:SUBLANES - 3 + k + tm, :])
        dpad[0:tm, :] = dxc
        dpad[tm:tm + SUBLANES, :] = dnext[...]
        dx = cw_v[0:1, :] * dpad[3:3 + tm, :]
        for k in range(1, 4):
            dx = dx + cw_v[k:k + 1, :] * dpad[3 - k:3 - k + tm, :]
        dx_ref[...] = dx.astype(dx_ref.dtype)
        dnext[...] = dxc[0:SUBLANES, :]

    rev = lambda i: nt - 1 - i
    rrow = lambda n, col=0: pl.BlockSpec((tm, n), lambda i: (rev(i), col))
    sq = _const((RG_WIDTH, RG_WIDTH))
    vec = _const((1, RG_WIDTH))
    return pl.pallas_call(
        body, name="rg_bwd", grid=(nt,),
        in_specs=[rrow(RG_WIDTH, 768 // RG_WIDTH), rrow(RG_WIDTH, P_XRG // RG_WIDTH),
                  pl.BlockSpec((SUBLANES, RG_WIDTH), lambda i: (jnp.maximum(rev(i) * hb - 1, 0), P_XRG // RG_WIDTH)),
                  rrow(RG_WIDTH, P_GRG // RG_WIDTH), rrow(RG_WIDTH),
                  pl.BlockSpec((SUBLANES, RG_WIDTH), lambda i: (jnp.maximum(rev(i) * hb - 1, 0), 0)),
                  _const((4, RG_WIDTH)), vec, sq, vec, sq, vec, vec],
        out_specs=[rrow(RG_WIDTH), rrow(RG_WIDTH), _const((SUBLANES, RG_WIDTH)), vec, sq, vec, sq, vec, vec],
        out_shape=[jax.ShapeDtypeStruct((t, RG_WIDTH), MXU_DTYPE), jax.ShapeDtypeStruct((t, RG_WIDTH), MXU_DTYPE),
                   jax.ShapeDtypeStruct((SUBLANES, RG_WIDTH), F32), jax.ShapeDtypeStruct((1, RG_WIDTH), F32),
                   jax.ShapeDtypeStruct((RG_WIDTH, RG_WIDTH), F32), jax.ShapeDtypeStruct((1, RG_WIDTH), F32),
                   jax.ShapeDtypeStruct((RG_WIDTH, RG_WIDTH), F32), jax.ShapeDtypeStruct((1, RG_WIDTH), F32),
                   jax.ShapeDtypeStruct((1, RG_WIDTH), F32)],
        scratch_shapes=[pltpu.VMEM((tm + SUBLANES, RG_WIDTH), F32), pltpu.VMEM((tm, RG_WIDTH), F32),
                        pltpu.VMEM((tm, RG_WIDTH), F32), pltpu.VMEM((tm + SUBLANES, RG_WIDTH), F32),
                        pltpu.VMEM((tm, RG_WIDTH), F32), pltpu.VMEM((tm + SUBLANES, RG_WIDTH), F32),
                        pltpu.VMEM((SUBLANES, RG_WIDTH), F32), pltpu.VMEM((SUBLANES, RG_WIDTH), F32)],
        compiler_params=_cparams(("arbitrary",)),
    )(dycat, proj, proj, proj, hs, hs, cw, cb, wa, ba, wx, bx, nsp)


def _block_diag(blocks):
    g, a, b = blocks.shape
    eye = jnp.eye(g, dtype=blocks.dtype)
    return (eye[:, None, :, None] * blocks[:, :, None, :]).reshape(g * a, g * b)


def _block_diag_extract(m, g):
    a, b = m.shape[0] // g, m.shape[1] // g
    m4 = m.reshape(g, a, g, b)
    idx = jnp.arange(g)
    return m4[idx, :, idx, :]


def _s5_prepare(lam_re, lam_im, log_step, b_re, b_im, c_re, c_im):
    step = jnp.exp(log_step)[:, None]
    mag = jnp.exp(lam_re * step)
    lbr = mag * jnp.cos(lam_im * step)
    lbi = mag * jnp.sin(lam_im * step)
    nr, ni = lbr - 1.0, lbi
    den = lam_re * lam_re + lam_im * lam_im
    cr = (nr * lam_re + ni * lam_im) / den
    ci = (ni * lam_re - nr * lam_im) / den
    bbr = cr[..., None] * b_re - ci[..., None] * b_im
    bbi = cr[..., None] * b_im + ci[..., None] * b_re
    bre = _block_diag(jnp.swapaxes(bbr, 1, 2))
    bim = _block_diag(jnp.swapaxes(bbi, 1, 2))
    cre = _block_diag(jnp.swapaxes(c_re, 1, 2))
    cim = _block_diag(jnp.swapaxes(c_im, 1, 2))
    return lbr.reshape(-1), lbi.reshape(-1), bre, bim, cre, cim


def _s5_scan_coef(lbr, lbi, reverse):
    if reverse:
        lbi = -lbi
    pr, pi = [lbr], [lbi]
    for _ in range(7):
        pr, pi = pr + [pr[-1] * lbr - pi[-1] * lbi], pi + [pr[-1] * lbi + pi[-1] * lbr]
    row = jnp.arange(SUBLANES)[:, None]
    tabs = []
    for sh in (1, 2, 4):
        keep = (row < SUBLANES - sh) if reverse else (row >= sh)
        tabs.append(jnp.stack([jnp.where(keep, pr[sh - 1][None, :], 0.0), jnp.where(keep, pi[sh - 1][None, :], 0.0)]))
    powr = jnp.stack(pr)
    powi = jnp.stack(pi)
    if reverse:
        powr, powi = powr[::-1], powi[::-1]
    tabs.append(jnp.stack([powr, powi]))
    tabs.append(jnp.zeros_like(tabs[-1]))
    return jnp.stack(tabs).astype(F32)


def _xy_peers():
    x, y, c = lax.axis_index("x"), lax.axis_index("y"), lax.axis_index("c")
    return x, y, c, [(1 - x, y), (x, 1 - y), (1 - x, 1 - y)]


def _hbm():
    return pl.BlockSpec(memory_space=pl.ANY)


def _xy_allgather(buf, *, name):
    n, w = buf.shape

    def body(x_ref, out_ref, send_sems, recv_sems, local_sem):
        x, y, c, peers = _xy_peers()
        me = 2 * x + y
        own = pltpu.make_async_copy(x_ref, out_ref.at[me], local_sem)
        own.start()
        sends = []
        for k, (px, py) in enumerate(peers):
            cp = pltpu.make_async_remote_copy(src_ref=x_ref, dst_ref=out_ref.at[me], send_sem=send_sems.at[k],
                                              recv_sem=recv_sems.at[k], device_id=(px, py, c), device_id_type=MESH)
            cp.start()
            sends.append(cp)
        for k, (px, py) in enumerate(peers):
            pltpu.make_async_remote_copy(src_ref=x_ref, dst_ref=out_ref.at[2 * px + py], send_sem=send_sems.at[k],
                                         recv_sem=recv_sems.at[k], device_id=(px, py, c),
                                         device_id_type=MESH).wait_recv()
        for cp in sends:
            cp.wait_send()
        own.wait()

    return pl.pallas_call(
        body, name=name, in_specs=[_hbm()], out_specs=_hbm(),
        out_shape=jax.ShapeDtypeStruct((4, n, w), buf.dtype),
        scratch_shapes=[pltpu.SemaphoreType.DMA((3,)), pltpu.SemaphoreType.DMA((3,)), pltpu.SemaphoreType.DMA],
    )(buf)


def _remote(src, dst, send_sem, recv_sem, dev):
    return pltpu.make_async_remote_copy(src_ref=src, dst_ref=dst, send_sem=send_sem, recv_sem=recv_sem,
                                        device_id=dev, device_id_type=MESH)


LAYER_GATHERED = (
    ("ssd_conv_w", (4, 256), 1), ("rg_conv_w", (4, LANES), 1),
    ("w_in", (1024, W_IN_PAD), 1), ("s5_glu_w", (64, 256), 0), ("w_out", (256, 1024), 0), ("xa_wq", (256, 1024), 0),
    ("xa_wk", (256, 1024), 0), ("xa_wv", (256, 1024), 0), ("xa_wo", (256, 1024), 0), ("mlp_w1", (1024, 1024), 1),
    ("mlp_w2", (1024, 1024), 0),
)
N_GATHERED = len(LAYER_GATHERED)
WAIT_GROUPS = ((0, 1, 2, 3), (4,), (5, 6, 7, 8), (9, 10))
RG_CONV_SHARD = RG_WIDTH // 4
N_GATHER_COPIES = 3 * N_GATHERED * DEPTH


def _gather_part(ref, t, pos):
    _, shp, ax = LAYER_GATHERED[t % N_GATHERED]
    idx = tuple(pl.ds(pos * shp[ax], shp[ax]) if d == ax else slice(None) for d in range(len(shp)))
    return ref.at[idx]


def _gather_start(shards):
    n = len(shards)
    lands = []
    for t, s in enumerate(shards):
        _, shp, ax = LAYER_GATHERED[t % N_GATHERED]
        full = shp[:ax] + (4 * shp[ax],) + shp[ax + 1:]
        lands.append(pltpu.with_memory_space_constraint(lax.empty(full, s.dtype), pltpu.HBM))

    def body(*refs):
        srcs, lnds = refs[:n], refs[n:2 * n]
        send_sems, recv_sems, local_sems = refs[2 * n:2 * n + 3]
        token = refs[-1]
        x, y, c, peers = _xy_peers()
        me = 2 * x + y
        for t in range(n):
            for k, (px, py) in enumerate(peers):
                _remote(srcs[t], _gather_part(lnds[t], t, me), send_sems.at[k * n + t], recv_sems.at[k * n + t],
                        (px, py, c)).start()
            pltpu.make_async_copy(srcs[t], _gather_part(lnds[t], t, me), local_sems.at[t]).start()
        token[...] = jnp.zeros_like(token)

    hbm = pl.BlockSpec(memory_space=pltpu.HBM)
    sem = pl.BlockSpec(memory_space=pltpu.SEMAPHORE)
    outs = pl.pallas_call(
        body, name="weights_gather_start", in_specs=[hbm] * (2 * n),
        out_shape=(pltpu.SemaphoreType.DMA((3 * n,)), pltpu.SemaphoreType.DMA((3 * n,)),
                   pltpu.SemaphoreType.DMA((n,)),
                   *[pltpu.HBM(s.shape, s.dtype) for s in shards], *[pltpu.HBM(a.shape, a.dtype) for a in lands],
                   jax.ShapeDtypeStruct((SUBLANES, LANES), F32)),
        out_specs=(sem, sem, sem, *[hbm] * (2 * n), pl.BlockSpec(memory_space=pltpu.VMEM)),
        input_output_aliases={i: 3 + i for i in range(2 * n)},
        compiler_params=pltpu.CompilerParams(has_side_effects=pltpu.SideEffectType.DATAFLOW_SIDE_EFFECTING),
    )(*[pltpu.with_memory_space_constraint(s, pltpu.HBM) for s in shards], *lands)
    return outs[0], outs[1], outs[2], outs[3:3 + n], outs[3 + n:3 + 2 * n], outs[-1]


def _gather_wait(handle, ts, after, *, name):
    send_sems, recv_sems, local_sems, src_thru, land_thru, _ = handle
    n = len(src_thru)
    m = len(ts)

    def body(*refs):
        srcs, lnds = refs[:m], refs[m:2 * m]
        ssem, rsem, lsem = refs[2 * m:2 * m + 3]
        x, y, c, peers = _xy_peers()
        me = 2 * x + y
        for i, t in enumerate(ts):
            for k, (px, py) in enumerate(peers):
                cp = _remote(srcs[i], _gather_part(lnds[i], t, 2 * px + py), ssem.at[k * n + t], rsem.at[k * n + t],
                             (px, py, c))
                cp.wait_send()
                cp.wait_recv()
            pltpu.make_async_copy(srcs[i], _gather_part(lnds[i], t, me), lsem.at[t]).wait()

    hbm = pl.BlockSpec(memory_space=pltpu.HBM)
    sem = pl.BlockSpec(memory_space=pltpu.SEMAPHORE)
    args = [src_thru[t] for t in ts] + [land_thru[t] for t in ts]
    outs = pl.pallas_call(
        body, name=name, in_specs=[hbm] * (2 * m) + [sem, sem, sem, pl.BlockSpec(memory_space=pl.ANY)],
        out_shape=[pltpu.HBM(a.shape, a.dtype) for a in args], out_specs=[hbm] * (2 * m),
        input_output_aliases={i: i for i in range(2 * m)},
        compiler_params=pltpu.CompilerParams(has_side_effects=pltpu.SideEffectType.DATAFLOW_SIDE_EFFECTING),
    )(*args, send_sems, recv_sems, local_sems, after)
    return outs[:m], outs[m:]


C_CHUNKS = 8
XY_CHUNKS = 8
EW_ROWS = 512


def _c_exchange(g, part):
    w = g.shape[2]
    row0, nrows = G_PARTS[part]
    half = nrows // 2
    rq = half // C_CHUNKS

    def body(g_ref, got_ref, send_sems, recv_sems):
        x, y, c = lax.axis_index("x"), lax.axis_index("y"), lax.axis_index("c")
        cps = []
        for s in range(4):
            for q in range(C_CHUNKS):
                k = s * C_CHUNKS + q
                cp = _remote(g_ref.at[s, pl.ds(row0 + (1 - c) * half + q * rq, rq), :],
                             got_ref.at[s, pl.ds(q * rq, rq), :], send_sems.at[k], recv_sems.at[k], (x, y, 1 - c))
                cp.start()
                cps.append(cp)
        for cp in cps:
            cp.wait_recv()
        for cp in cps:
            cp.wait_send()

    return pl.pallas_call(
        body, name="grad_c_exchange_%d" % part, in_specs=[_hbm()], out_specs=_hbm(),
        out_shape=jax.ShapeDtypeStruct((4, half, w), g.dtype),
        scratch_shapes=[pltpu.SemaphoreType.DMA((4 * C_CHUNKS,)), pltpu.SemaphoreType.DMA((4 * C_CHUNKS,))],
    )(g)


XFER_DTYPE = jnp.bfloat16


def _add_own_half(g, got, c_arr, part):
    w = g.shape[2]
    row0, nrows = G_PARTS[part]
    half = nrows // 2
    nb = half // EW_ROWS
    b0 = row0 // EW_ROWS

    def body(c_ref, a_ref, b_ref, o_ref, t_ref):
        sm = a_ref[...] + b_ref[...]
        o_ref[...] = sm.astype(o_ref.dtype)

        @pl.when(pl.program_id(1) == nb - 1)
        def _():
            t_ref[...] = sm[:, EW_ROWS - MISC_ROWS:, :]

    grid_spec = pltpu.PrefetchScalarGridSpec(
        num_scalar_prefetch=1, grid=(4, nb),
        in_specs=[pl.BlockSpec((1, EW_ROWS, w), lambda s, i, c: (s, b0 + c[0] * nb + i, 0)),
                  pl.BlockSpec((1, EW_ROWS, w), lambda s, i, c: (s, i, 0))],
        out_specs=[pl.BlockSpec((1, EW_ROWS, w), lambda s, i, c: (s, i, 0)),
                   pl.BlockSpec((1, MISC_ROWS, w), lambda s, i, c: (s, 0, 0))])
    return pl.pallas_call(
        body, name="grad_add_halves", grid_spec=grid_spec,
        out_shape=[jax.ShapeDtypeStruct((4, half, w), XFER_DTYPE), jax.ShapeDtypeStruct((4, MISC_ROWS, w), g.dtype)],
        compiler_params=_cparams(("arbitrary", "arbitrary")),
    )(c_arr, g, got)


def _xy_pieces(arrs):
    pieces = []
    for a, arr in enumerate(arrs):
        nch = XY_CHUNKS if a == 0 else 1
        rq = arr.shape[1] // nch
        pieces += [(a, pl.ds(q * rq, rq)) for q in range(nch)]
    return pieces


def _xy_start(arrs, *, name):
    na = len(arrs)
    pieces = _xy_pieces(arrs)
    npc = len(pieces)
    lands = [pltpu.with_memory_space_constraint(lax.empty(a.shape, a.dtype), pltpu.HBM) for a in arrs]

    def body(*refs):
        ins, outs = refs[:na], refs[na:2 * na]
        send_sems, recv_sems, local_sems = refs[2 * na:2 * na + 3]
        token = refs[-1]
        x, y, c, peers = _xy_peers()
        me = 2 * x + y
        for k, (px, py) in enumerate(peers):
            for j, (a, rows) in enumerate(pieces):
                _remote(ins[a].at[2 * px + py, rows, :], outs[a].at[me, rows, :], send_sems.at[k * npc + j],
                        recv_sems.at[k * npc + j], (px, py, c)).start()
        for j, (a, rows) in enumerate(pieces):
            pltpu.make_async_copy(ins[a].at[me, rows, :], outs[a].at[me, rows, :], local_sems.at[j]).start()
        token[...] = jnp.zeros_like(token)

    hbm = pl.BlockSpec(memory_space=pltpu.HBM)
    sem = pl.BlockSpec(memory_space=pltpu.SEMAPHORE)
    outs = pl.pallas_call(
        body, name=name, in_specs=[hbm] * (2 * na),
        out_shape=(pltpu.SemaphoreType.DMA((3 * npc,)), pltpu.SemaphoreType.DMA((3 * npc,)),
                   pltpu.SemaphoreType.DMA((npc,)),
                   *[pltpu.HBM(a.shape, a.dtype) for a in arrs], *[pltpu.HBM(a.shape, a.dtype) for a in arrs],
                   jax.ShapeDtypeStruct((SUBLANES, LANES), F32)),
        out_specs=(sem, sem, sem, *[hbm] * (2 * na), pl.BlockSpec(memory_space=pltpu.VMEM)),
        input_output_aliases={i: 3 + i for i in range(2 * na)},
        compiler_params=pltpu.CompilerParams(has_side_effects=pltpu.SideEffectType.DATAFLOW_SIDE_EFFECTING),
    )(*[pltpu.with_memory_space_constraint(a, pltpu.HBM) for a in arrs], *lands)
    return (outs[0], outs[1], outs[2], outs[3:3 + na], outs[3 + na:3 + 2 * na]), outs[-1]


def _xy_wait(handle, after, *, name):
    send_sems, recv_sems, local_sems, src_thru, land_thru = handle
    na = len(src_thru)
    pieces = _xy_pieces(src_thru)
    npc = len(pieces)

    def body(*refs):
        ins, outs = refs[:na], refs[na:2 * na]
        ssem, rsem, lsem = refs[2 * na:2 * na + 3]
        x, y, c, peers = _xy_peers()
        me = 2 * x + y
        for k, (px, py) in enumerate(peers):
            for j, (a, rows) in enumerate(pieces):
                cp = _remote(ins[a].at[me, rows, :], outs[a].at[2 * px + py, rows, :], ssem.at[k * npc + j],
                             rsem.at[k * npc + j], (px, py, c))
                cp.wait_send()
                cp.wait_recv()
        for j, (a, rows) in enumerate(pieces):
            pltpu.make_async_copy(ins[a].at[me, rows, :], outs[a].at[me, rows, :], lsem.at[j]).wait()

    hbm = pl.BlockSpec(memory_space=pltpu.HBM)
    sem = pl.BlockSpec(memory_space=pltpu.SEMAPHORE)
    args = list(src_thru) + list(land_thru)
    outs = pl.pallas_call(
        body, name=name, in_specs=[hbm] * (2 * na) + [sem, sem, sem, pl.BlockSpec(memory_space=pl.ANY)],
        out_shape=[pltpu.HBM(a.shape, a.dtype) for a in args], out_specs=[hbm] * (2 * na),
        input_output_aliases={i: i for i in range(2 * na)},
        compiler_params=pltpu.CompilerParams(has_side_effects=pltpu.SideEffectType.DATAFLOW_SIDE_EFFECTING),
    )(*args, send_sems, recv_sems, local_sems, after)
    return outs[na:]


def _sum4_into_half(r, rt, c_arr, part, fbuf):
    _, half, w = r.shape
    nb = half // EW_ROWS
    b0 = G_PARTS[part][0] // EW_ROWS

    def body(c_ref, r_ref, t_ref, *rest):
        o_ref = rest[-1]
        o_ref[...] = ((r_ref[0].astype(F32) + r_ref[1].astype(F32)) + r_ref[2].astype(F32)) + r_ref[3].astype(F32)

        @pl.when(pl.program_id(0) == nb - 1)
        def _():
            o_ref[EW_ROWS - MISC_ROWS:, :] = ((t_ref[0] + t_ref[1]) + t_ref[2]) + t_ref[3]

    in_specs = [pl.BlockSpec((4, EW_ROWS, w), lambda i, c: (0, i, 0)),
                pl.BlockSpec((4, MISC_ROWS, w), lambda i, c: (0, 0, 0))]
    args = [c_arr, r, rt]
    aliases = {}
    if fbuf is not None:
        in_specs.append(pl.BlockSpec(memory_space=pl.ANY))
        args.append(fbuf)
        aliases = {3: 0}
    grid_spec = pltpu.PrefetchScalarGridSpec(
        num_scalar_prefetch=1, grid=(nb,), in_specs=in_specs,
        out_specs=pl.BlockSpec((EW_ROWS, w), lambda i, c: (b0 + c[0] * nb + i, 0)))
    return pl.pallas_call(
        body, name="grad_sum4", grid_spec=grid_spec, out_shape=jax.ShapeDtypeStruct((G_ROWS, w), F32),
        input_output_aliases=aliases, compiler_params=_cparams(("arbitrary",)),
    )(*args)


C_GATHER_ROWS = 512


def _c_allgather_halves(f, parts):
    w = f.shape[1]
    chunks = []
    for part in parts:
        chunks += [(part, r) for r in range(0, G_PARTS[part][1] // 2, C_GATHER_ROWS)]
    nch = len(chunks)

    def body(f_ref, out_ref, send_sems, recv_sems):
        x, y, c = lax.axis_index("x"), lax.axis_index("y"), lax.axis_index("c")

        def rows(q, owner):
            part, r = chunks[q]
            row0, nrows = G_PARTS[part]
            return pl.ds(row0 + owner * (nrows // 2) + r, C_GATHER_ROWS)

        sends = []
        for q in range(nch):
            cp = _remote(f_ref.at[rows(q, c), :], out_ref.at[rows(q, c), :], send_sems.at[q], recv_sems.at[q],
                         (x, y, 1 - c))
            cp.start()
            sends.append(cp)
        for q in range(nch):
            _remote(f_ref.at[rows(q, 1 - c), :], out_ref.at[rows(q, 1 - c), :], send_sems.at[q], recv_sems.at[q],
                    (x, y, 1 - c)).wait_recv()
        for cp in sends:
            cp.wait_send()

    return pl.pallas_call(
        body, name="grad_c_allgather_" + "".join(str(p) for p in parts), in_specs=[_hbm()], out_specs=_hbm(),
        input_output_aliases={0: 0},
        out_shape=jax.ShapeDtypeStruct((G_ROWS, w), f.dtype),
        scratch_shapes=[pltpu.SemaphoreType.DMA((nch,)), pltpu.SemaphoreType.DMA((nch,))],
    )(f)


def _adamw(w, m, v, g, g_rows=None):
    shape = w.shape
    cols = shape[-1]
    rows = int(math.prod(shape)) // cols
    tr = 256 if rows % 256 == 0 else rows
    from_flat = g_rows is not None
    c1 = 1.0 / (1.0 - ADAM_B1 ** ADAM_STEP)
    c2 = 1.0 / (1.0 - ADAM_B2 ** ADAM_STEP)

    def body(w_ref, m_ref, v_ref, g_ref, *outs):
        gg = g_ref[...]
        nm = ADAM_B1 * m_ref[...] + (1.0 - ADAM_B1) * gg
        nv = ADAM_B2 * v_ref[...] + (1.0 - ADAM_B2) * (gg * gg)
        if from_flat:
            outs[0][...] = gg
        d_ref, nm_ref, nv_ref = outs[-3:]
        nm_ref[...] = nm
        nv_ref[...] = nv
        d_ref[...] = -ADAM_LR * ((nm * c1) / (jnp.sqrt(nv * c2) + ADAM_EPS) + ADAM_WD * w_ref[...])

    spec = pl.BlockSpec((tr, cols), lambda i: (i, 0))
    if from_flat:
        nbl = rows // DEPTH // tr
        assert cols == FLAT and all(r % tr == 0 for r in g_rows) and len(g_rows) == DEPTH == 2
        b0, b1 = g_rows[0] // tr, g_rows[1] // tr
        g_spec = pl.BlockSpec((tr, cols), lambda i: (jnp.where(i < nbl, b0 + i, b1 + i - nbl), 0))
        g_arg = g
    else:
        g_spec = spec
        g_arg = g.reshape(rows, cols)
    n_out = 4 if from_flat else 3
    sds = jax.ShapeDtypeStruct((rows, cols), F32)
    outs = pl.pallas_call(
        body, name="adamw", grid=(rows // tr,), in_specs=[spec, spec, spec, g_spec], out_specs=[spec] * n_out,
        out_shape=[sds] * n_out, compiler_params=_cparams(("arbitrary",)),
    )(w.reshape(rows, cols), m.reshape(rows, cols), v.reshape(rows, cols), g_arg)
    outs = [o.reshape(shape) for o in outs]
    return outs if from_flat else [g] + outs


SMALL_SHARDED = (("s5_glu_w", (2, 64, 256), 1), ("ssd_conv_w", (2, 4, 256), 2), ("rg_conv_w", (2, 4, 64), 2))
REPLICATED = (
    ("ssd_conv_b", (2, 1024)), ("ssd_dt_bias", (2, 8)), ("ssd_a_log", (2, 8)), ("ssd_d", (2, 8)),
    ("ssd_norm_w", (2, 512)), ("s5_lam_re", (2, 16, 64)), ("s5_lam_im", (2, 16, 64)), ("s5_log_step", (2, 16)),
    ("s5_b_re", (2, 16, 64, 16)), ("s5_b_im", (2, 16, 64, 16)), ("s5_c_re", (2, 16, 16, 64)),
    ("s5_c_im", (2, 16, 16, 64)), ("s5_d", (2, 256)), ("s5_glu_b", (2, 256)), ("rg_conv_b", (2, 256)),
    ("rg_wa", (2, 4, 64, 64)), ("rg_ba", (2, 4, 64)), ("rg_wx", (2, 4, 64, 64)), ("rg_bx", (2, 4, 64)),
    ("rg_lambda", (2, 256)), ("ln1_g", (2, 1024)), ("ln1_b", (2, 1024)), ("ln2_g", (2, 1024)), ("ln2_b", (2, 1024)),
    ("ln3_g", (2, 1024)), ("ln3_b", (2, 1024)),
)
WEIGHT_ORDER = (
    "w_in", "w_out", "ssd_conv_w", "ssd_conv_b", "ssd_dt_bias", "ssd_a_log", "ssd_d", "ssd_norm_w", "s5_lam_re",
    "s5_lam_im", "s5_log_step", "s5_b_re", "s5_b_im", "s5_c_re", "s5_c_im", "s5_d", "s5_glu_w", "s5_glu_b",
    "rg_conv_w", "rg_conv_b", "rg_wa", "rg_ba", "rg_wx", "rg_bx", "rg_lambda", "ln1_g", "ln1_b", "xa_wq", "xa_wk",
    "xa_wv", "xa_wo", "ln2_g", "ln2_b", "mlp_w1", "mlp_w2", "ln3_g", "ln3_b",
)


def _size(shape):
    return int(math.prod(shape))


def _round_up(a, b):
    return (a + b - 1) // b * b


SMALL_ELEMS = sum(_size(s) for _, s, _ in SMALL_SHARDED)
REP_ELEMS = sum(_size(s) for _, s in REPLICATED)
REP_QROWS = _round_up(-(-REP_ELEMS // (4 * FLAT)), 8)
assert SMALL_ELEMS <= MISC_REP_ROW * FLAT and MISC_REP_ROW + REP_QROWS <= MISC_ROWS


def _pack_shards(tensors, names_shapes):
    return jnp.concatenate([tensors[n].reshape(-1) for n, *_ in names_shapes])


def _unpack(flat, names_shapes):
    out, off = {}, 0
    for n, s, *_ in names_shapes:
        out[n] = flat[off:off + _size(s)].reshape(s)
        off += _size(s)
    return out


def _split_shards(full, names_shapes):
    rows = []
    for k in range(4):
        parts = []
        for n, s, ax in names_shapes:
            w = s[ax]
            parts.append(lax.slice_in_dim(full[n], k * w, (k + 1) * w, axis=ax).reshape(-1))
        rows.append(jnp.concatenate(parts))
    return jnp.stack(rows)


def _pack_cols(w):
    pad = jnp.zeros((w.shape[0], LANES - SSD_HEADS), w.dtype)
    return jnp.concatenate([w[:, O_XBC:O_XBC + 1024], w[:, O_Z:O_Z + 512], w[:, O_U:O_U + 256],
                            w[:, O_XRG:O_XRG + 256], w[:, O_GRG:O_GRG + 256], w[:, O_DT:O_DT + 8], pad], axis=1)


def _unpack_cols(w):
    return jnp.concatenate([w[:, P_Z:P_Z + 512], w[:, P_XBC:P_XBC + 1024], w[:, P_DT:P_DT + 8],
                            w[:, P_U:P_U + 256], w[:, P_XRG:P_XRG + 256], w[:, P_GRG:P_GRG + 256]], axis=1)


def _lanes(v, width):
    return jnp.pad(v, (0, width - v.shape[0])).reshape(1, width)


def _layer_params(rep, l):
    p = {}
    p["ssd_cb"] = rep["ssd_conv_b"][l].reshape(1, -1)
    p["ssd_dtb"] = _lanes(rep["ssd_dt_bias"][l], LANES)
    p["ssd_a"] = _lanes(-jnp.exp(rep["ssd_a_log"][l]), LANES)
    p["ssd_d"] = jnp.repeat(rep["ssd_d"][l], 64).reshape(1, -1)
    p["ssd_nw"] = rep["ssd_norm_w"][l].reshape(1, -1)
    s5_args = tuple(rep[n][l] for n in ("s5_lam_re", "s5_lam_im", "s5_log_step", "s5_b_re", "s5_b_im", "s5_c_re",
                                        "s5_c_im"))
    (lbr, lbi, bre, bim, cre, cim), p["s5_vjp"] = jax.vjp(_s5_prepare, *s5_args)
    p.update(s5_bre=bre, s5_bim=bim, s5_cre=cre, s5_cim=cim)
    p["s5_coef"] = _s5_scan_coef(lbr, lbi, False)
    p["s5_rcoef"] = _s5_scan_coef(lbr, lbi, True)
    p["s5_d"] = rep["s5_d"][l].reshape(1, -1)
    p["s5_gb"] = rep["s5_glu_b"][l].reshape(1, -1)
    p["rg_cb"] = rep["rg_conv_b"][l].reshape(1, -1)
    p["rg_wa"] = _block_diag(rep["rg_wa"][l])
    p["rg_wx"] = _block_diag(rep["rg_wx"][l])
    p["rg_ba"] = rep["rg_ba"][l].reshape(1, -1)
    p["rg_bx"] = rep["rg_bx"][l].reshape(1, -1)
    p["rg_nsp"] = (-RG_C * jax.nn.softplus(-rep["rg_lambda"][l])).reshape(1, -1)
    p["rg_dnsp"] = RG_C * jax.nn.sigmoid(-rep["rg_lambda"][l])
    for n in ("ln1_g", "ln1_b", "ln2_g", "ln2_b", "ln3_g", "ln3_b"):
        p[n] = rep[n][l].reshape(1, -1)
    return p


def _layer_fwd(h, mem, p, fetch):
    s = {"h0": h}
    p.update(fetch(0, h))
    proj = _mm(h, p["w_in"], name="in_proj")
    s["proj"] = proj
    y_ssd, s["ssd_yy"], s["ssd_states"] = _ssd_fwd(proj, p["ssd_cw"], p["ssd_cb"], p["ssd_dtb"], p["ssd_a"],
                                                     p["ssd_d"], p["ssd_nw"])
    y_s5, s["s5_y2"], s["s5_hre"], s["s5_him"] = _s5_fwd(proj, p["s5_bre"], p["s5_bim"], p["s5_cre"], p["s5_cim"],
                                                         p["s5_d"], p["s5_glu_w"], p["s5_gb"], p["s5_coef"])
    y_rg, s["rg_h"] = _rg_fwd(proj, p["rg_cw"], p["rg_cb"], p["rg_wa"], p["rg_ba"], p["rg_wx"], p["rg_bx"],
                              p["rg_nsp"])
    s["ys"] = [y_ssd, y_s5, y_rg]
    p.update(fetch(1, y_rg))
    h1, s["xh1"], s["rs1"] = _outproj_ln_fwd(s["ys"], h, p["w_out"], p["ln1_g"], p["ln1_b"])
    s["h1"] = h1
    p.update(fetch(2, h1))
    kb = _mm(mem, p["xa_wk"], name="mem_proj")
    vb = _mm(mem, p["xa_wv"], name="mem_proj")
    s["kb"], s["vb"] = kb, vb
    h2, s["xh2"], s["rs2"], s["attn_o"] = _attn_ln_fwd(h1, p["xa_wq"], p["xa_wo"], kb, vb, p["ln2_g"], p["ln2_b"])
    s["h2"] = h2
    p.update(fetch(3, h2))
    h3, s["xh3"], s["rs3"], s["mlp_hdn"] = _mlp_ln_fwd(h2, p["mlp_w1"], p["mlp_w2"], p["ln3_g"], p["ln3_b"])
    return h3, s


def _layer_bwd(dh3, mem, p, s, l, gbuf, after_mlp=None):
    g = {}
    dr3, du, dh2, g["ln3_g"], g["ln3_b"] = _mlp_ln_bwd(dh3, s["xh3"], s["rs3"], p["ln3_g"], s["mlp_hdn"],
                                                        p["mlp_w1"], p["mlp_w2"])
    gbuf = _wgrad_flat(s["h2"], du, gbuf, mode="colblk", row_off=_grad_row("mlp_w1", l), name="wgrad_mlp_w1")
    gbuf = _wgrad_flat(s["mlp_hdn"], dr3, gbuf, mode="rowblk", row_off=_grad_row("mlp_w2", l), name="wgrad_mlp_w2")
    ln2_g = p["ln2_g"] if after_mlp is None else p["ln2_g"] + after_mlp(gbuf)[0:1, 0:1]
    dr2, dq, dh1, dkb, dvb, g["ln2_g"], g["ln2_b"] = _attn_ln_bwd(dh2, s["xh2"], s["rs2"], ln2_g, s["h1"],
                                                                   p["xa_wq"], p["xa_wo"], s["kb"], s["vb"])
    for n, a_op, g_op in (("xa_wo", s["attn_o"], dr2), ("xa_wq", s["h1"], dq), ("xa_wk", mem, dkb),
                          ("xa_wv", mem, dvb)):
        gbuf = _wgrad_flat(a_op, g_op, gbuf, mode="rows4", row_off=_grad_row(n, l), name="wgrad_" + n)
    dr1, dres, dycat, g["ln1_g"], g["ln1_b"] = _outproj_ln_bwd(dh1, s["xh1"], s["rs1"], p["ln1_g"], p["w_out"])
    gbuf = _wgrad_flat(s["ys"], dr1, gbuf, mode="rows4", row_off=_grad_row("w_out", l), name="wgrad_w_out")
    proj = s["proj"]
    (dxbc, dz, ddt, dcw, dcb, ddtb, da_neg, dd_l, dnw) = _ssd_bwd(
        dycat, proj, s["ssd_yy"], s["ssd_states"], p["ssd_cw"], p["ssd_cb"], p["ssd_dtb"], p["ssd_a"], p["ssd_d"],
        p["ssd_nw"])
    g["ssd_conv_w"] = dcw[0:4]
    g["ssd_conv_b"] = dcb[0]
    g["ssd_dt_bias"] = ddtb[0, :SSD_HEADS]
    g["ssd_a_log"] = da_neg[0, :SSD_HEADS] * p["ssd_a"][0, :SSD_HEADS]
    g["ssd_d"] = dd_l.reshape(SSD_HEADS, 64).sum(axis=1)
    g["ssd_norm_w"] = dnw[0]
    (du_s5, dbre, dbim, dcre, dcim, dlam, dd5, dgw, dgb) = _s5_bwd(
        dycat, proj, s["s5_y2"], s["s5_hre"], s["s5_him"], p["s5_bre"], p["s5_bim"], p["s5_cre"], p["s5_cim"],
        p["s5_d"], p["s5_glu_w"], p["s5_gb"], p["s5_rcoef"])
    dl = dlam.sum(axis=1)
    s5g = p["s5_vjp"]((dl[0], dl[1], dbre, dbim, dcre, dcim))
    for n, v in zip(("s5_lam_re", "s5_lam_im", "s5_log_step", "s5_b_re", "s5_b_im", "s5_c_re", "s5_c_im"), s5g):
        g[n] = v
    g["s5_d"] = dd5[0]
    g["s5_glu_w"] = dgw
    g["s5_glu_b"] = dgb[0]
    (dxrg, dgrg, drcw, drcb, dwa, dba, dwx, dbx, dnsp) = _rg_bwd(
        dycat, proj, s["rg_h"], p["rg_cw"], p["rg_cb"], p["rg_wa"], p["rg_ba"], p["rg_wx"], p["rg_bx"], p["rg_nsp"])
    g["rg_conv_w"] = drcw[0:4]
    g["rg_conv_b"] = drcb[0]
    g["rg_wa"] = _block_diag_extract(dwa, RG_BLOCKS)
    g["rg_wx"] = _block_diag_extract(dwx, RG_BLOCKS)
    g["rg_ba"] = dba.reshape(RG_BLOCKS, RG_BLOCK_DIM)
    g["rg_bx"] = dbx.reshape(RG_BLOCKS, RG_BLOCK_DIM)
    g["rg_lambda"] = dnsp[0] * p["rg_dnsp"]
    dproj = [dxbc, dz, du_s5, dxrg, dgrg, ddt]
    g["w_in"] = _unpack_cols(_wgrad_in(s["h0"], dproj))
    dh0 = _in_proj_bwd(dproj, p["w_in"], dres)
    for n in ("ln1_g", "ln1_b", "ln2_g", "ln2_b", "ln3_g", "ln3_b"):
        g[n] = g[n][0]
    return dh0, g, gbuf


def _local_step(h, memf, target, rep, fetch):
    params, saved = [], []
    for l in range(DEPTH):
        p = _layer_params(rep, l)
        params.append(p)
        h, s = _layer_fwd(h, memf, p, functools.partial(fetch, l))
        saved.append(s)
    loss11, dh = _loss_fwd_bwd(h, target)
    grads = [None] * DEPTH
    gbuf = None
    c_arr = lax.axis_index("c").astype(jnp.int32).reshape(1)
    handles = {}

    def start_part(buf, part):
        handles[part], token = _xy_start(_chip_sums(buf, c_arr, part), name="grad_xy_start_%d" % part)
        return token

    for l in reversed(range(DEPTH)):
        hook = functools.partial(start_part, part=1) if l == 0 else None
        dh, grads[l], gbuf = _layer_bwd(dh, memf, params[l], saved[l], l, gbuf, hook)
        if l == DEPTH - 1:
            gbuf = lax.dynamic_update_slice(
                gbuf, _w_in_block(grads[l]["w_in"], jnp.zeros((4, MISC_ROWS, FLAT), F32)),
                (0, _grad_row("w_in", l), 0))
            params[0]["ln3_g"] = params[0]["ln3_g"] + start_part(gbuf, 0)[0:1, 0:1]
    gsmall = {n: jnp.stack([grads[l][n] for l in range(DEPTH)]) for n in grads[0] if n != "w_in"}
    return loss11, dh, gsmall, grads[0]["w_in"], gbuf, handles, c_arr


def _w_in_block(gw, tail):
    gw = jnp.pad(gw.reshape(D_MODEL, 4, W_IN_SHARD), ((0, 0), (0, 0), (0, W_IN_PAD - W_IN_SHARD)))
    return jnp.concatenate([jnp.transpose(gw, (1, 0, 2)).reshape(4, W_IN_PAD, FLAT), tail], axis=1)


def _chip_sums(gbuf, c_arr, part):
    return list(_add_own_half(gbuf, _c_exchange(gbuf, part), c_arr, part))


def kernel(x, mem, w_in, w_out, ssd_conv_w, ssd_conv_b, ssd_dt_bias, ssd_a_log, ssd_d, ssd_norm_w, s5_lam_re, s5_lam_im, s5_log_step, s5_b_re, s5_b_im, s5_c_re, s5_c_im, s5_d, s5_glu_w, s5_glu_b, rg_conv_w, rg_conv_b, rg_wa, rg_ba, rg_wx, rg_bx, rg_lambda, ln1_g, ln1_b, xa_wq, xa_wk, xa_wv, xa_wo, ln2_g, ln2_b, mlp_w1, mlp_w2, ln3_g, ln3_b, loss_target, m_w_in, m_w_out, m_ssd_conv_w, m_ssd_conv_b, m_ssd_dt_bias, m_ssd_a_log, m_ssd_d, m_ssd_norm_w, m_s5_lam_re, m_s5_lam_im, m_s5_log_step, m_s5_b_re, m_s5_b_im, m_s5_c_re, m_s5_c_im, m_s5_d, m_s5_glu_w, m_s5_glu_b, m_rg_conv_w, m_rg_conv_b, m_rg_wa, m_rg_ba, m_rg_wx, m_rg_bx, m_rg_lambda, m_ln1_g, m_ln1_b, m_xa_wq, m_xa_wk, m_xa_wv, m_xa_wo, m_ln2_g, m_ln2_b, m_mlp_w1, m_mlp_w2, m_ln3_g, m_ln3_b, v_w_in, v_w_out, v_ssd_conv_w, v_ssd_conv_b, v_ssd_dt_bias, v_ssd_a_log, v_ssd_d, v_ssd_norm_w, v_s5_lam_re, v_s5_lam_im, v_s5_log_step, v_s5_b_re, v_s5_b_im, v_s5_c_re, v_s5_c_im, v_s5_d, v_s5_glu_w, v_s5_glu_b, v_rg_conv_w, v_rg_conv_b, v_rg_wa, v_rg_ba, v_rg_wx, v_rg_bx, v_rg_lambda, v_ln1_g, v_ln1_b, v_xa_wq, v_xa_wk, v_xa_wv, v_xa_wo, v_ln2_g, v_ln2_b, v_mlp_w1, v_mlp_w2, v_ln3_g, v_ln3_b):
    args = dict(locals())
    weights = {n: args[n] for n in WEIGHT_ORDER}
    mom_m = {n: args["m_" + n] for n in WEIGHT_ORDER}
    mom_v = {n: args["v_" + n] for n in WEIGHT_ORDER}

    shards = []
    for l in range(DEPTH):
        for n, shp, ax in LAYER_GATHERED:
            w = weights[n][l]
            if w.shape[1] != shp[1]:
                w = jnp.pad(w, ((0, 0), (0, shp[1] - w.shape[1])))
            if n not in ("ssd_conv_w", "rg_conv_w"):
                w = w.astype(MXU_DTYPE)
            shards.append(w)
    handle = _gather_start(shards)

    def unpad(arr, padded, width):
        return jnp.concatenate([arr[:, padded * k:padded * k + width] for k in range(4)], axis=1)

    def fetch(l, grp, after):
        ts = [l * N_GATHERED + j for j in WAIT_GROUPS[grp]]
        _, landed = _gather_wait(handle, ts, after, name="weights_gather_wait_%d_%d" % (l, grp))
        out = {}
        for t, arr in zip(ts, landed):
            n = LAYER_GATHERED[t % N_GATHERED][0]
            if n == "w_in":
                arr = _pack_cols(unpad(arr, W_IN_PAD, W_IN_SHARD))
            elif n == "rg_conv_w":
                arr = unpad(arr, LANES, RG_CONV_SHARD)
            out[{"ssd_conv_w": "ssd_cw", "rg_conv_w": "rg_cw"}.get(n, n)] = arr
        return out

    rep = {n: weights[n] for n, _ in REPLICATED}

    loss11, dx, gsmall, gw_in0, gbuf, handles, c_arr = _local_step(x[0], mem[0], loss_target[0], rep, fetch)
    grad_x = dx[None]
    loss = lax.psum(loss11[0, 0], ("x", "y", "c"))

    small_q = _split_shards(gsmall, SMALL_SHARDED)
    rep_q = jnp.pad(_pack_shards(gsmall, REPLICATED), (0, 4 * REP_QROWS * FLAT - REP_ELEMS)).reshape(4, -1)
    misc = jnp.concatenate(
        [jnp.pad(small_q, ((0, 0), (0, MISC_REP_ROW * FLAT - SMALL_ELEMS))), rep_q,
         jnp.zeros((4, (MISC_ROWS - MISC_REP_ROW - REP_QROWS) * FLAT), F32)], axis=1).reshape(4, MISC_ROWS, FLAT)
    gbuf = lax.dynamic_update_slice(gbuf, _w_in_block(gw_in0, misc), (0, _grad_row("w_in", 0), 0))
    handles[2], token = _xy_start(_chip_sums(gbuf, c_arr, 2), name="grad_xy_start_2")
    fbuf = None
    for part in (0, 1):
        got = _xy_wait(handles[part], dx, name="grad_xy_wait_%d" % part)
        fbuf = _sum4_into_half(got[0], got[1] + token[0:1, 0:1], c_arr, part, fbuf)
    fbuf = _c_allgather_halves(fbuf, (0, 1))
    res = {n: _adamw(weights[n], mom_m[n], mom_v[n], fbuf, g_rows=[_grad_row(n, l) for l in range(DEPTH)])
           for n in ("mlp_w1", "mlp_w2")}
    got = _xy_wait(handles[2], res["mlp_w2"][1], name="grad_xy_wait_2")
    reduced = _c_allgather_halves(_sum4_into_half(got[0], got[1], c_arr, 2, fbuf), (2,))
    misc_red = reduced[ROW_MISC:]
    rep_all = _xy_allgather(misc_red[MISC_REP_ROW:MISC_REP_ROW + REP_QROWS], name="small_grads_allgather")
    g_red = {**_unpack(misc_red[:MISC_REP_ROW].reshape(-1), SMALL_SHARDED),
             **_unpack(rep_all.reshape(-1), REPLICATED)}
    g_red["w_in"] = jnp.stack([
        reduced[_grad_row("w_in", l):_grad_row("w_in", l) + W_IN_PAD].reshape(D_MODEL, W_IN_PAD)[:, :W_IN_SHARD]
        for l in range(DEPTH)])

    for n in WEIGHT_ORDER:
        if n in ("w_out", "xa_wq", "xa_wk", "xa_wv", "xa_wo"):
            res[n] = _adamw(weights[n], mom_m[n], mom_v[n], reduced, g_rows=[_grad_row(n, l) for l in range(DEPTH)])
        elif n not in res:
            res[n] = _adamw(weights[n], mom_m[n], mom_v[n], g_red[n])
    return (loss, grad_x, *[res[n][0] for n in WEIGHT_ORDER], *[res[n][1] for n in WEIGHT_ORDER],
            *[res[n][2] for n in WEIGHT_ORDER], *[res[n][3] for n in WEIGHT_ORDER])
```
